```python
import math
import jax, jax.numpy as jnp
from jax import lax
import numpy as np

D_MODEL = 1024
BATCH = 8
SEQ = 4096
DEPTH = 1

HEAD_DIM = 64
N_HEADS_DIL = 8
N_HEADS_SB = 8
D_DIL = N_HEADS_DIL * HEAD_DIM
D_SB = N_HEADS_SB * HEAD_DIM
D_MIX = D_DIL + D_SB
D_IN = 3 * D_DIL + 3 * D_SB
D_FF = -(-(8 * D_MODEL) // (3 * 256)) * 256
DILATED_PAIRS = ((128, 1), (512, 4), (2048, 16))
BLOCK = 128
ROPE_THETA = 10000.0
EPS = 1e-6

kernel_name = "hymba_dilated_stickbreaking_block"


def _rmsnorm(x, w):
    xf = x.astype(jnp.float32)
    y = xf * lax.rsqrt(jnp.mean(xf * xf, axis=-1, keepdims=True) + EPS)
    wb = w.astype(jnp.float32).reshape((1,) * (x.ndim - 1) + (w.shape[-1],))
    return (y * wb).astype(x.dtype)


def _rope_tables(seq_len):
    pos = jnp.arange(seq_len, dtype=jnp.float32)
    inv_freq = ROPE_THETA ** (-jnp.arange(0, HEAD_DIM, 2, dtype=jnp.float32) / HEAD_DIM)
    ang = pos[:, None] * inv_freq[None, :]
    return jnp.cos(ang)[None, None], jnp.sin(ang)[None, None]


def _apply_rope(x, cos, sin):
    xf = x.astype(jnp.float32)
    half = HEAD_DIM // 2
    x1, x2 = xf[..., :half], xf[..., half:]
    out = jnp.concatenate([x1 * cos - x2 * sin, x2 * cos + x1 * sin], axis=-1)
    return out.astype(x.dtype)


def _dilated_branch(q, k, v, window, dilation):
    B, H, S, Dh = q.shape
    r = dilation
    n_back = window // dilation
    L = S // r
    nb = -(-L // BLOCK)
    Lp = nb * BLOCK

    def strided(t):
        t = t.reshape(B, H, L, r, Dh).transpose(0, 1, 3, 2, 4)
        t = jnp.pad(t, ((0, 0), (0, 0), (0, 0), (0, Lp - L), (0, 0)))
        return t.reshape(B, H, r, nb, BLOCK, Dh)

    qb, kb, vb = strided(q), strided(k), strided(v)

    def with_prev(t):
        prev = jnp.pad(t, ((0, 0), (0, 0), (0, 0), (1, 0), (0, 0), (0, 0)))[:, :, :, :-1]
        return jnp.concatenate([prev, t], axis=-2)

    kk, vv = with_prev(kb), with_prev(vb)
    s = jnp.einsum('bhrnqd,bhrnkd->bhrnqk', qb, kk,
                   preferred_element_type=jnp.float32) * (Dh ** -0.5)
    i = jnp.arange(BLOCK)[:, None]
    j = jnp.arange(2 * BLOCK)[None, :]
    dist = i + BLOCK - j
    key_idx = (jnp.arange(nb)[:, None, None] - 1) * BLOCK + j[None]
    valid = (dist >= 0)[None] & (dist <= n_back)[None] & (key_idx >= 0)
    s = jnp.where(valid[None, None, None], s, -jnp.inf)
    m = jnp.max(s, axis=-1, keepdims=True)
    p = jnp.exp(s - m)
    den = jnp.sum(p, axis=-1, keepdims=True)
    o = jnp.einsum('bhrnqk,bhrnkd->bhrnqd', p.astype(v.dtype), vv,
                   preferred_element_type=jnp.float32) / den
    lse = (m + jnp.log(den))[..., 0]

    o = o.reshape(B, H, r, Lp, Dh)[:, :, :, :L].transpose(0, 1, 3, 2, 4).reshape(B, H, S, Dh)
    lse = lse.reshape(B, H, r, Lp)[:, :, :, :L].transpose(0, 1, 3, 2).reshape(B, H, S)
    return o, lse


def _dilated_attention(q, k, v):
    outs, lses = [], []
    for window, dilation in DILATED_PAIRS:
        o, lse = _dilated_branch(q, k, v, window, dilation)
        outs.append(o)
        lses.append(lse)
    w = jax.nn.softmax(jnp.stack(lses, axis=0), axis=0)
    o = jnp.sum(w[..., None] * jnp.stack(outs, axis=0), axis=0)
    return o.astype(q.dtype)


def _stick_breaking(q, k, v):
    B, H, S, Dh = q.shape
    nb = S // BLOCK
    scale = Dh ** -0.5
    qblocks = q.reshape(B, H, nb, BLOCK, Dh).transpose(2, 0, 1, 3, 4)
    kpos = jnp.arange(S)

    def one_block(args):
        qblk, bidx = args
        z = jnp.einsum('bhqd,bhkd->bhqk', qblk, k,
                       preferred_element_type=jnp.float32) * scale
        qpos = bidx * BLOCK + jnp.arange(BLOCK)
        causal = (kpos[None, :] < qpos[:, None])[None, None]
        log_beta = jax.nn.log_sigmoid(z)
        log_keep = jnp.where(causal, jax.nn.log_sigmoid(-z), 0.0)
        suffix = lax.cumsum(log_keep, axis=3, reverse=True) - log_keep
        a = jnp.where(causal, jnp.exp(log_beta + suffix), 0.0)
        return jnp.einsum('bhqk,bhkd->bhqd', a.astype(v.dtype), v,
                          preferred_element_type=jnp.float32).astype(v.dtype)

    out = lax.map(one_block, (qblocks, jnp.arange(nb)))
    return out.transpose(1, 2, 0, 3, 4).reshape(B, H, S, Dh)


def _fwd_setup_inputs(seed: int = 0) -> dict:
    key = jax.random.key(seed)
    ks = jax.random.split(key, 12)
    f32 = jnp.float32

    def gain(k, n):
        return (1.0 + 0.02 * jax.random.normal(k, (DEPTH, n))).astype(f32)

    return {
        "x": jax.random.normal(ks[0], (BATCH, SEQ, D_MODEL), f32),
        "attn_norm_w": gain(ks[1], D_MODEL),
        "w_in": jax.random.normal(ks[2], (DEPTH, D_MODEL, D_IN), f32) * D_MODEL ** -0.5,
        "q_norm_w": gain(ks[3], HEAD_DIM),
        "k_norm_w": gain(ks[4], HEAD_DIM),
        "dil_out_norm_w": gain(ks[5], D_DIL),
        "sb_out_norm_w": gain(ks[6], D_SB),
        "w_out": jax.random.normal(ks[7], (DEPTH, D_MIX, D_MODEL), f32) * D_MIX ** -0.5,
        "ffn_norm_w": gain(ks[8], D_MODEL),
        "w_gate": jax.random.normal(ks[9], (DEPTH, D_MODEL, D_FF), f32) * D_MODEL ** -0.5,
        "w_up": jax.random.normal(ks[10], (DEPTH, D_MODEL, D_FF), f32) * D_MODEL ** -0.5,
        "w_down": jax.random.normal(ks[11], (DEPTH, D_FF, D_MODEL), f32) * D_FF ** -0.5,
    }


def _fwd_reference(x, attn_norm_w, w_in, q_norm_w, k_norm_w, dil_out_norm_w, sb_out_norm_w,
              w_out, ffn_norm_w, w_gate, w_up, w_down):
    B, S, _ = x.shape
    cos, sin = _rope_tables(S)

    def heads(t, n):
        return t.reshape(B, S, n, HEAD_DIM).transpose(0, 2, 1, 3)

    def merge(t):
        return t.transpose(0, 2, 1, 3).reshape(B, S, t.shape[1] * t.shape[3])

    for l in range(DEPTH):
        h = _rmsnorm(x, attn_norm_w[l])
        proj = jnp.einsum('bsd,de->bse', h, w_in[l])
        qa = proj[..., 0:D_DIL]
        ka = proj[..., D_DIL:2 * D_DIL]
        va = proj[..., 2 * D_DIL:3 * D_DIL]
        o0 = 3 * D_DIL
        qs = proj[..., o0:o0 + D_SB]
        ksb = proj[..., o0 + D_SB:o0 + 2 * D_SB]
        vs = proj[..., o0 + 2 * D_SB:o0 + 3 * D_SB]

        qa = _apply_rope(_rmsnorm(heads(qa, N_HEADS_DIL), q_norm_w[l]), cos, sin)
        ka = _apply_rope(_rmsnorm(heads(ka, N_HEADS_DIL), k_norm_w[l]), cos, sin)
        o_dil = _dilated_attention(qa, ka, heads(va, N_HEADS_DIL))

        o_sb = _stick_breaking(heads(qs, N_HEADS_SB), heads(ksb, N_HEADS_SB),
                               heads(vs, N_HEADS_SB))

        mixed = jnp.concatenate([_rmsnorm(merge(o_dil), dil_out_norm_w[l]),
                                 _rmsnorm(merge(o_sb), sb_out_norm_w[l])], axis=-1)
        x = x + jnp.einsum('bse,ed->bsd', mixed, w_out[l])

        h = _rmsnorm(x, ffn_norm_w[l])
        g = jnp.einsum('bsd,df->bsf', h, w_gate[l])
        u = jnp.einsum('bsd,df->bsf', h, w_up[l])
        x = x + jnp.einsum('bsf,fd->bsd', jax.nn.silu(g) * u, w_down[l])
    return x


import jax as _jax
import jax.numpy as _jnp

TWIN_FORMAT = 'train_step'
FWD_PARAMS = ['x', 'attn_norm_w', 'w_in', 'q_norm_w', 'k_norm_w', 'dil_out_norm_w', 'sb_out_norm_w', 'w_out', 'ffn_norm_w', 'w_gate', 'w_up', 'w_down']
TWIN_WEIGHTS = ['attn_norm_w', 'w_in', 'q_norm_w', 'k_norm_w', 'dil_out_norm_w', 'sb_out_norm_w', 'w_out', 'ffn_norm_w', 'w_gate', 'w_up', 'w_down']
TWIN_DIFF_INPUT = 'x'
TWIN_INPUTS = ['x', 'attn_norm_w', 'w_in', 'q_norm_w', 'k_norm_w', 'dil_out_norm_w', 'sb_out_norm_w', 'w_out', 'ffn_norm_w', 'w_gate', 'w_up', 'w_down', 'loss_target', 'm_attn_norm_w', 'm_w_in', 'm_q_norm_w', 'm_k_norm_w', 'm_dil_out_norm_w', 'm_sb_out_norm_w', 'm_w_out', 'm_ffn_norm_w', 'm_w_gate', 'm_w_up', 'm_w_down', 'v_attn_norm_w', 'v_w_in', 'v_q_norm_w', 'v_k_norm_w', 'v_dil_out_norm_w', 'v_sb_out_norm_w', 'v_w_out', 'v_ffn_norm_w', 'v_w_gate', 'v_w_up', 'v_w_down']
TWIN_OUTPUTS = ['loss', 'grad_x', 'grad_attn_norm_w', 'grad_w_in', 'grad_q_norm_w', 'grad_k_norm_w', 'grad_dil_out_norm_w', 'grad_sb_out_norm_w', 'grad_w_out', 'grad_ffn_norm_w', 'grad_w_gate', 'grad_w_up', 'grad_w_down', 'delta_attn_norm_w', 'delta_w_in', 'delta_q_norm_w', 'delta_k_norm_w', 'delta_dil_out_norm_w', 'delta_sb_out_norm_w', 'delta_w_out', 'delta_ffn_norm_w', 'delta_w_gate', 'delta_w_up', 'delta_w_down', 'new_m_attn_norm_w', 'new_m_w_in', 'new_m_q_norm_w', 'new_m_k_norm_w', 'new_m_dil_out_norm_w', 'new_m_sb_out_norm_w', 'new_m_w_out', 'new_m_ffn_norm_w', 'new_m_w_gate', 'new_m_w_up', 'new_m_w_down', 'new_v_attn_norm_w', 'new_v_w_in', 'new_v_q_norm_w', 'new_v_k_norm_w', 'new_v_dil_out_norm_w', 'new_v_sb_out_norm_w', 'new_v_w_out', 'new_v_ffn_norm_w', 'new_v_w_gate', 'new_v_w_up', 'new_v_w_down']
TWIN_LEAF_KINDS = {'loss': 'loss', 'grad_x': 'grad_x', 'grad_attn_norm_w': 'grad_w', 'grad_w_in': 'grad_w', 'grad_q_norm_w': 'grad_w', 'grad_k_norm_w': 'grad_w', 'grad_dil_out_norm_w': 'grad_w', 'grad_sb_out_norm_w': 'grad_w', 'grad_w_out': 'grad_w', 'grad_ffn_norm_w': 'grad_w', 'grad_w_gate': 'grad_w', 'grad_w_up': 'grad_w', 'grad_w_down': 'grad_w', 'delta_attn_norm_w': 'delta_w', 'delta_w_in': 'delta_w', 'delta_q_norm_w': 'delta_w', 'delta_k_norm_w': 'delta_w', 'delta_dil_out_norm_w': 'delta_w', 'delta_sb_out_norm_w': 'delta_w', 'delta_w_out': 'delta_w', 'delta_ffn_norm_w': 'delta_w', 'delta_w_gate': 'delta_w', 'delta_w_up': 'delta_w', 'delta_w_down': 'delta_w', 'new_m_attn_norm_w': 'new_m', 'new_m_w_in': 'new_m', 'new_m_q_norm_w': 'new_m', 'new_m_k_norm_w': 'new_m', 'new_m_dil_out_norm_w': 'new_m', 'new_m_sb_out_norm_w': 'new_m', 'new_m_w_out': 'new_m', 'new_m_ffn_norm_w': 'new_m', 'new_m_w_gate': 'new_m', 'new_m_w_up': 'new_m', 'new_m_w_down': 'new_m', 'new_v_attn_norm_w': 'new_v', 'new_v_w_in': 'new_v', 'new_v_q_norm_w': 'new_v', 'new_v_k_norm_w': 'new_v', 'new_v_dil_out_norm_w': 'new_v', 'new_v_sb_out_norm_w': 'new_v', 'new_v_w_out': 'new_v', 'new_v_ffn_norm_w': 'new_v', 'new_v_w_gate': 'new_v', 'new_v_w_up': 'new_v', 'new_v_w_down': 'new_v'}


def _forward(args):
    return _fwd_reference(*[args[k] for k in FWD_PARAMS])


def _output_shape():
    out = _jax.eval_shape(lambda: _forward(_fwd_setup_inputs(0)))
    return out.shape, out.dtype

N_MICROBATCH = 1
ADAM_LR = 0.001
ADAM_B1 = 0.9
ADAM_B2 = 0.999
ADAM_EPS = 1e-08
ADAM_WD = 0.01
ADAM_STEP = 10
PER_EXAMPLE_BATCH_AXIS = {'x': 0, 'loss_target': 0}
SHARED_INPUTS = []
_WEIGHT_DTYPES = {'attn_norm_w': _jnp.float32, 'w_in': _jnp.float32, 'q_norm_w': _jnp.float32, 'k_norm_w': _jnp.float32, 'dil_out_norm_w': _jnp.float32, 'sb_out_norm_w': _jnp.float32, 'w_out': _jnp.float32, 'ffn_norm_w': _jnp.float32, 'w_gate': _jnp.float32, 'w_up': _jnp.float32, 'w_down': _jnp.float32}
MOMENT_SCALE = {'attn_norm_w': 7.669080e-01, 'w_in': 4.352435e-01, 'q_norm_w': 1.280347e+00, 'k_norm_w': 1.294264e+00, 'dil_out_norm_w': 3.187637e+01, 'sb_out_norm_w': 3.190036e+01, 'w_out': 8.731832e-01, 'ffn_norm_w': 2.471549e+01, 'w_gate': 1.992660e-01, 'w_up': 2.026349e-01, 'w_down': 3.103842e-01}


def _to_microbatches(a, axis):
    t = _jnp.moveaxis(a, axis, 0)
    t = t.reshape((N_MICROBATCH, t.shape[0] // N_MICROBATCH) + t.shape[1:])
    return _jnp.moveaxis(t, 1, axis + 1)


def setup_inputs(seed: int = 0) -> dict:
    inp = _fwd_setup_inputs(seed)
    key = _jax.random.fold_in(_jax.random.key(seed), 7919)
    shape, _ = _output_shape()
    out = dict(inp)
    out["loss_target"] = _jax.random.normal(_jax.random.fold_in(key, 0), shape, _jnp.float32)
    for i, name in enumerate(TWIN_WEIGHTS):
        w = inp[name].astype(_jnp.float32)
        if MOMENT_SCALE is None:
            s = _jnp.sqrt(_jnp.mean(_jnp.square(w)) + 1e-30)
        else:
            s = MOMENT_SCALE[name]
        km, kv = _jax.random.split(_jax.random.fold_in(key, i + 1))
        out[name] = w
        out["m_" + name] = s * _jax.random.normal(km, w.shape, _jnp.float32)
        out["v_" + name] = (s * s) * _jax.random.uniform(kv, w.shape, _jnp.float32, 0.5, 1.5)
    if N_MICROBATCH > 1:
        for name, axis in PER_EXAMPLE_BATCH_AXIS.items():
            out[name] = _to_microbatches(out[name], axis)
    return {'x': out['x'], 'attn_norm_w': out['attn_norm_w'], 'w_in': out['w_in'], 'q_norm_w': out['q_norm_w'], 'k_norm_w': out['k_norm_w'], 'dil_out_norm_w': out['dil_out_norm_w'], 'sb_out_norm_w': out['sb_out_norm_w'], 'w_out': out['w_out'], 'ffn_norm_w': out['ffn_norm_w'], 'w_gate': out['w_gate'], 'w_up': out['w_up'], 'w_down': out['w_down'], 'loss_target': out['loss_target'], 'm_attn_norm_w': out['m_attn_norm_w'], 'm_w_in': out['m_w_in'], 'm_q_norm_w': out['m_q_norm_w'], 'm_k_norm_w': out['m_k_norm_w'], 'm_dil_out_norm_w': out['m_dil_out_norm_w'], 'm_sb_out_norm_w': out['m_sb_out_norm_w'], 'm_w_out': out['m_w_out'], 'm_ffn_norm_w': out['m_ffn_norm_w'], 'm_w_gate': out['m_w_gate'], 'm_w_up': out['m_w_up'], 'm_w_down': out['m_w_down'], 'v_attn_norm_w': out['v_attn_norm_w'], 'v_w_in': out['v_w_in'], 'v_q_norm_w': out['v_q_norm_w'], 'v_k_norm_w': out['v_k_norm_w'], 'v_dil_out_norm_w': out['v_dil_out_norm_w'], 'v_sb_out_norm_w': out['v_sb_out_norm_w'], 'v_w_out': out['v_w_out'], 'v_ffn_norm_w': out['v_ffn_norm_w'], 'v_w_gate': out['v_w_gate'], 'v_w_up': out['v_w_up'], 'v_w_down': out['v_w_down']}


def _loss(weights, diff, rest, loss_target):
    with _jax.named_scope("forward"):
        args = {**rest, TWIN_DIFF_INPUT: diff, **{k: w.astype(_WEIGHT_DTYPES[k]) for k, w in weights.items()}}
        y = _forward(args)
    with _jax.named_scope("loss_head"):
        err = _jnp.square(y.astype(_jnp.float32) - loss_target)
        return 0.5 * _jnp.sum(_jnp.mean(err, axis=-1)) if err.ndim else 0.5 * err


def _adamw(w, g, m, v):
    m = ADAM_B1 * m + (1.0 - ADAM_B1) * g
    v = ADAM_B2 * v + (1.0 - ADAM_B2) * _jnp.square(g)
    m_hat = m / (1.0 - ADAM_B1 ** ADAM_STEP)
    v_hat = v / (1.0 - ADAM_B2 ** ADAM_STEP)
    delta = -ADAM_LR * (m_hat / (_jnp.sqrt(v_hat) + ADAM_EPS) + ADAM_WD * w)
    return delta, m, v


def reference(x, attn_norm_w, w_in, q_norm_w, k_norm_w, dil_out_norm_w, sb_out_norm_w, w_out, ffn_norm_w, w_gate, w_up, w_down, loss_target, m_attn_norm_w, m_w_in, m_q_norm_w, m_k_norm_w, m_dil_out_norm_w, m_sb_out_norm_w, m_w_out, m_ffn_norm_w, m_w_gate, m_w_up, m_w_down, v_attn_norm_w, v_w_in, v_q_norm_w, v_k_norm_w, v_dil_out_norm_w, v_sb_out_norm_w, v_w_out, v_ffn_norm_w, v_w_gate, v_w_up, v_w_down):
    given = dict(x=x, attn_norm_w=attn_norm_w, w_in=w_in, q_norm_w=q_norm_w, k_norm_w=k_norm_w, dil_out_norm_w=dil_out_norm_w, sb_out_norm_w=sb_out_norm_w, w_out=w_out, ffn_norm_w=ffn_norm_w, w_gate=w_gate, w_up=w_up, w_down=w_down, loss_target=loss_target, m_attn_norm_w=m_attn_norm_w, m_w_in=m_w_in, m_q_norm_w=m_q_norm_w, m_k_norm_w=m_k_norm_w, m_dil_out_norm_w=m_dil_out_norm_w, m_sb_out_norm_w=m_sb_out_norm_w, m_w_out=m_w_out, m_ffn_norm_w=m_ffn_norm_w, m_w_gate=m_w_gate, m_w_up=m_w_up, m_w_down=m_w_down, v_attn_norm_w=v_attn_norm_w, v_w_in=v_w_in, v_q_norm_w=v_q_norm_w, v_k_norm_w=v_k_norm_w, v_dil_out_norm_w=v_dil_out_norm_w, v_sb_out_norm_w=v_sb_out_norm_w, v_w_out=v_w_out, v_ffn_norm_w=v_ffn_norm_w, v_w_gate=v_w_gate, v_w_up=v_w_up, v_w_down=v_w_down)
    weights = {n: given[n] for n in TWIN_WEIGHTS}
    shared = {n: given[n] for n in SHARED_INPUTS}
    per_example = {n: given[n] for n in ['x']}
    grad_fn = _jax.value_and_grad(_loss, argnums=(0, 1))

    def one_microbatch(ex, loss_target):
        ex = dict(ex)
        diff = ex.pop(TWIN_DIFF_INPUT)
        return grad_fn(weights, diff, {**shared, **ex}, loss_target)

    if N_MICROBATCH == 1:
        loss, (grad_w, grad_x) = one_microbatch(per_example, given["loss_target"])
    else:
        def body(carry, xs):
            loss_sum, grad_sum = carry
            l_k, (gw_k, gx_k) = one_microbatch(xs[0], xs[1])
            with _jax.named_scope("update"):
                return (loss_sum + l_k, _jax.tree.map(_jnp.add, grad_sum, gw_k)), gx_k

        init = (_jnp.zeros((), _jnp.float32), _jax.tree.map(_jnp.zeros_like, weights))
        (loss, grad_w), grad_x = _jax.lax.scan(body, init, (per_example, given["loss_target"]))
    with _jax.named_scope("update"):
        delta_w, new_m, new_v = {}, {}, {}
        for n in TWIN_WEIGHTS:
            delta_w[n], new_m[n], new_v[n] = _adamw(weights[n], grad_w[n], given["m_" + n], given["v_" + n])
    return (loss, grad_x, *[grad_w[n] for n in TWIN_WEIGHTS], *[delta_w[n] for n in TWIN_WEIGHTS],
            *[new_m[n] for n in TWIN_WEIGHTS], *[new_v[n] for n in TWIN_WEIGHTS])
```

```python
import functools

import jax
import jax.numpy as jnp
from jax import lax
from jax.experimental import pallas as pl
from jax.experimental.pallas import tpu as pltpu

F32 = jnp.float32
BF16 = jnp.bfloat16

N_DEV = 8
D_MODEL = 1024
HEAD_DIM = 64
D_GRP = 512
D_IN = 6 * D_GRP
IN_SHARD = D_IN // N_DEV
FF_SHARD = 352
FF_PAD = 384
OUT_SHARD = D_MODEL // N_DEV
BLOCK = 128
DILATIONS = (1, 4, 16)
ROPE_THETA = 10000.0
EPS = 1e-6
ATT_SCALE = HEAD_DIM ** -0.5
NEG = -1e30

ADAM_LR = 0.001
ADAM_B1 = 0.9
ADAM_B2 = 0.999
ADAM_EPS = 1e-08
ADAM_WD = 0.01
ADAM_STEP = 10

SB_TILE = 256
ROW_TILE = 512
VMEM_LIMIT = 56 * 1024 * 1024
MESH = pl.DeviceIdType.MESH


def _dot(a, b):
    return jnp.dot(a, b, preferred_element_type=F32)


def _dot_nt(a, b):
    return lax.dot_general(a, b, (((1,), (1,)), ((), ())), preferred_element_type=F32)


def _dot_tn(a, b):
    return lax.dot_general(a, b, (((0,), (0,)), ((), ())), preferred_element_type=F32)


def _mm_split(t, m):
    hi = t.astype(BF16)
    lo = (t - hi.astype(F32)).astype(BF16)
    return _dot(hi, m) + _dot(lo, m)


def _params(**kw):
    return pltpu.CompilerParams(vmem_limit_bytes=VMEM_LIMIT, **kw)


def _full(shape):
    nd = len(shape)
    return pl.BlockSpec(shape, lambda *_: (0,) * nd)


def _swap_halves(t):
    lane = lax.broadcasted_iota(jnp.int32, t.shape, 1)
    first = (lane & 32) == 0
    return jnp.where(first, pltpu.roll(t, 96, 1), pltpu.roll(t, 32, 1))


def _log_sigmoid(z):
    return jnp.minimum(z, 0.0) - jnp.log(1.0 + jnp.exp(-jnp.abs(z)))


def _mesh_pos():
    return lax.axis_index("x"), lax.axis_index("y"), lax.axis_index("c")


def _flat_index(p):
    return 4 * p[0] + 2 * p[1] + p[2]


def _gather_weights(a, b):
    def body(a_ref, b_ref, ag_ref, bg_ref, send_sems, recv_sems, local_sems):
        x, y, c = _mesh_pos()
        me, sibling = (x, y, c), (x, y, 1 - c)
        chips = [(1 - x, y), (x, 1 - y), (1 - x, 1 - y)]
        srcs, outs = (a_ref, b_ref), (ag_ref, bg_ref)

        def copy(arr, k, block, to, own=False):
            dst = outs[arr].at[_flat_index(block)]
            return pltpu.make_async_remote_copy(
                src_ref=srcs[arr] if own else dst, dst_ref=dst,
                send_sem=send_sems.at[arr, k], recv_sem=recv_sems.at[arr, k],
                device_id=to, device_id_type=MESH)

        for arr in range(2):
            mine = pltpu.make_async_copy(srcs[arr], outs[arr].at[_flat_index(me)], local_sems.at[arr])
            mine.start()
            first = [copy(arr, 0, me, sibling, own=True)]
            first += [copy(arr, 1 + j, me, (*chip, c), own=True) for j, chip in enumerate(chips)]
            for cp in first:
                cp.start()
        for arr in range(2):
            passed = [copy(arr, 4 + j, (*chip, c), sibling) for j, chip in enumerate(chips)]
            for j, chip in enumerate(chips):
                copy(arr, 1 + j, (*chip, c), me).wait_recv()
                passed[j].start()
        for arr in range(2):
            copy(arr, 0, sibling, me).wait_recv()
            for j, chip in enumerate(chips):
                copy(arr, 4 + j, (*chip, 1 - c), me).wait_recv()
            for k in range(7):
                copy(arr, k, me, me).wait_send()
            pltpu.make_async_copy(srcs[arr], outs[arr].at[_flat_index(me)], local_sems.at[arr]).wait()

    any_spec = pl.BlockSpec(memory_space=pl.ANY)
    return pl.pallas_call(
        body, name="gather_weights",
        out_shape=(jax.ShapeDtypeStruct((N_DEV,) + a.shape, a.dtype),
                   jax.ShapeDtypeStruct((N_DEV,) + b.shape, b.dtype)),
        in_specs=[any_spec, any_spec], out_specs=(any_spec, any_spec),
        scratch_shapes=[pltpu.SemaphoreType.DMA((2, 7)), pltpu.SemaphoreType.DMA((2, 7)),
                        pltpu.SemaphoreType.DMA((2,))],
        compiler_params=pltpu.CompilerParams(has_side_effects=True),
    )(a, b)


def _exchange_grads(parts, small):
    n_arr = len(parts)

    def body(*refs):
        ins, outs = refs[:n_arr + 1], refs[n_arr + 1:2 * (n_arr + 1)]
        send_sems, recv_sems, local_sems = refs[2 * (n_arr + 1):]
        x, y, c = _mesh_pos()
        me = (x, y, c)
        my_idx = _flat_index(me)
        peers = []
        for m in range(1, N_DEV):
            peers.append((1 - x if m & 4 else x, 1 - y if m & 2 else y, 1 - c if m & 1 else c))

        def src_block(arr, dev):
            return ins[arr] if arr == n_arr else ins[arr].at[_flat_index(dev)]

        def copy(arr, k):
            return pltpu.make_async_remote_copy(
                src_ref=src_block(arr, peers[k]), dst_ref=outs[arr].at[my_idx],
                send_sem=send_sems.at[arr, k], recv_sem=recv_sems.at[arr, k],
                device_id=peers[k], device_id_type=MESH)

        def local(arr):
            return pltpu.make_async_copy(src_block(arr, me), outs[arr].at[my_idx], local_sems.at[arr])

        for arr in range(n_arr + 1):
            local(arr).start()
            for k in range(N_DEV - 1):
                copy(arr, k).start()
        for arr in range(n_arr + 1):
            for k in range(N_DEV - 1):
                cp = copy(arr, k)
                cp.wait_send()
                cp.wait_recv()
            local(arr).wait()

    any_spec = pl.BlockSpec(memory_space=pl.ANY)
    out_shape = tuple(jax.ShapeDtypeStruct(p.shape, p.dtype) for p in parts)
    out_shape += (jax.ShapeDtypeStruct((N_DEV,) + small.shape, small.dtype),)
    return pl.pallas_call(
        body, name="exchange_grads",
        out_shape=out_shape,
        in_specs=[any_spec] * (n_arr + 1), out_specs=(any_spec,) * (n_arr + 1),
        scratch_shapes=[pltpu.SemaphoreType.DMA((n_arr + 1, N_DEV - 1)),
                        pltpu.SemaphoreType.DMA((n_arr + 1, N_DEV - 1)),
                        pltpu.SemaphoreType.DMA((n_arr + 1,))],
        compiler_params=pltpu.CompilerParams(has_side_effects=True),
    )(*parts, small)


def _head_norm(t, w128, bd):
    ms = _mm_split(t * t, bd) * (1.0 / HEAD_DIM)
    r = lax.rsqrt(ms + EPS)
    return (t * r) * w128, r


def _attn_in(x2, wn1, a_g, cos2, sin2, qnw, knw, bd):
    s_len = x2.shape[0]
    tm = ROW_TILE

    def body(x_ref, wn_ref, w_ref, cos_ref, sin_ref, qnw_ref, knw_ref, bd_ref,
             h1_ref, qraw_ref, kraw_ref, q_ref, k_ref, va_ref, qs_ref, ks_ref, vs_ref, proj):
        xx = x_ref[...]
        r = lax.rsqrt(jnp.mean(xx * xx, axis=-1, keepdims=True) + EPS)
        h = ((xx * r) * wn_ref[...]).astype(BF16)
        h1_ref[...] = h
        for d in range(N_DEV):
            proj[:, IN_SHARD * d:IN_SHARD * (d + 1)] = _dot(h, w_ref[d])
        cos_t, sin_t, bdm = cos_ref[...], sin_ref[...], bd_ref[...]
        for grp, (raw_ref, rope_ref, nw_ref) in enumerate(((qraw_ref, q_ref, qnw_ref),
                                                           (kraw_ref, k_ref, knw_ref))):
            for p in range(4):
                cols = slice(D_GRP * grp + 128 * p, D_GRP * grp + 128 * (p + 1))
                t = proj[:, cols]
                raw_ref[:, 128 * p:128 * (p + 1)] = t
                yn, _ = _head_norm(t, nw_ref[...], bdm)
                rope_ref[:, 128 * p:128 * (p + 1)] = (yn * cos_t + _swap_halves(yn) * sin_t).astype(BF16)
        for grp, ref in ((2, va_ref), (3, qs_ref), (4, ks_ref), (5, vs_ref)):
            ref[...] = proj[:, D_GRP * grp:D_GRP * (grp + 1)].astype(BF16)

    row = lambda w: pl.BlockSpec((tm, w), lambda i: (i, 0))
    grp_bf = jax.ShapeDtypeStruct((s_len, D_GRP), BF16)
    grp_f32 = jax.ShapeDtypeStruct((s_len, D_GRP), F32)
    return pl.pallas_call(
        body, name="attn_in", grid=(s_len // tm,),
        in_specs=[row(D_MODEL), _full((1, D_MODEL)),
                  pl.BlockSpec((N_DEV, D_MODEL, IN_SHARD), lambda i: (0, 0, 0)),
                  row(128), row(128), _full((1, 128)), _full((1, 128)), _full((128, 128))],
        out_specs=(row(D_MODEL),) + (row(D_GRP),) * 8,
        out_shape=(jax.ShapeDtypeStruct((s_len, D_MODEL), BF16), grp_f32, grp_f32) + (grp_bf,) * 6,
        scratch_shapes=[pltpu.VMEM((tm, D_IN), F32)],
        compiler_params=_params(),
    )(x2, wn1, a_g, cos2, sin2, qnw, knw, bd)


def _band_mask(n):
    i = lax.broadcasted_iota(jnp.int32, (BLOCK, 2 * BLOCK), 0)
    j = lax.broadcasted_iota(jnp.int32, (BLOCK, 2 * BLOCK), 1)
    dist = i + BLOCK - j
    return (dist >= 0) & (dist <= BLOCK) & ((n - 1) * BLOCK + j >= 0)


def _dil_fwd(qv, kv, vv, r):
    sub_len = qv.shape[0]
    nb = sub_len // BLOCK

    def body(q_ref, kp_ref, kc_ref, vp_ref, vc_ref, o_ref, lse_ref):
        n = pl.program_id(1)
        valid = _band_mask(n)
        lane = lax.broadcasted_iota(jnp.int32, (BLOCK, 128), 1)
        head0 = lane < HEAD_DIM
        for p in range(4):
            cols = slice(128 * p, 128 * (p + 1))
            q2 = q_ref[:, cols]
            kk = jnp.concatenate([kp_ref[:, cols], kc_ref[:, cols]], axis=0)
            vv2 = jnp.concatenate([vp_ref[:, cols], vc_ref[:, cols]], axis=0)
            res = []
            for h in range(2):
                qh = jnp.where(head0, q2, 0) if h == 0 else jnp.where(head0, 0, q2)
                s = jnp.where(valid, _dot_nt(qh, kk) * ATT_SCALE, NEG)
                m = jnp.max(s, axis=-1, keepdims=True)
                pr = jnp.exp(s - m)
                den = jnp.sum(pr, axis=-1, keepdims=True)
                o = _dot(pr.astype(BF16), vv2) / den
                res.append((o, m + jnp.log(den)))
            o_ref[:, cols] = jnp.where(head0, res[0][0], res[1][0])
            lse_ref[:, cols] = jnp.where(head0, res[0][1], res[1][1])

    cur = pl.BlockSpec((BLOCK, D_GRP), lambda c, n: (n, c))
    prev = pl.BlockSpec((BLOCK, D_GRP), lambda c, n: (jnp.maximum(n - 1, 0), c))
    out = jax.ShapeDtypeStruct(qv.shape, F32)
    return pl.pallas_call(
        body, name=f"dil_fwd_r{r}", grid=(r, nb),
        in_specs=[cur, prev, cur, prev, cur], out_specs=(cur, cur), out_shape=(out, out),
        compiler_params=_params(),
    )(qv, kv, kv, vv, vv)


def _dil_bwd(qv, kv, vv, dov, lsev, deltav, r):
    sub_len = qv.shape[0]
    nb = sub_len // BLOCK

    def body(q_ref, kp_ref, kc_ref, vp_ref, vc_ref, do_ref, lse_ref, dl_ref,
             dq_ref, dk_ref, dv_ref, dk_carry, dv_carry):
        n = pl.program_id(1)

        @pl.when(n == 0)
        def _():
            dk_carry[...] = jnp.zeros_like(dk_carry)
            dv_carry[...] = jnp.zeros_like(dv_carry)

        @pl.when(n < nb)
        def _():
            valid = _band_mask(n)
            lane = lax.broadcasted_iota(jnp.int32, (BLOCK, 128), 1)
            head0 = lane < HEAD_DIM
            for p in range(4):
                cols = slice(128 * p, 128 * (p + 1))
                q2, do2 = q_ref[:, cols], do_ref[:, cols]
                lse2, dl2 = lse_ref[:, cols], dl_ref[:, cols]
                kk = jnp.concatenate([kp_ref[:, cols], kc_ref[:, cols]], axis=0)
                vv2 = jnp.concatenate([vp_ref[:, cols], vc_ref[:, cols]], axis=0)
                dq_h, dkk, dvv = [], 0.0, 0.0
                for h in range(2):
                    sel = (lambda t: jnp.where(head0, t, 0)) if h == 0 else (lambda t: jnp.where(head0, 0, t))
                    qh, doh = sel(q2), sel(do2)
                    one_lane = lane == (0 if h == 0 else HEAD_DIM)
                    lse = jnp.sum(jnp.where(one_lane, lse2, 0.0), axis=-1, keepdims=True)
                    dl = jnp.sum(jnp.where(one_lane, dl2, 0.0), axis=-1, keepdims=True)
                    s = _dot_nt(qh, kk) * ATT_SCALE
                    pr = jnp.where(valid, jnp.exp(jnp.minimum(s - lse, 0.0)), 0.0)
                    dp = _dot_nt(doh, vv2)
                    ds = (pr * (dp - dl) * ATT_SCALE).astype(BF16)
                    dq_h.append(_dot(ds, kk))
                    dkk = dkk + _dot_tn(ds, qh)
                    dvv = dvv + _dot_tn(pr.astype(BF16), doh)
                dq_ref[:, cols] = jnp.where(head0, dq_h[0], dq_h[1])
                dk_ref[:, cols] = dk_carry[:, cols] + dkk[:BLOCK]
                dv_ref[:, cols] = dv_carry[:, cols] + dvv[:BLOCK]
                dk_carry[:, cols] = dkk[BLOCK:]
                dv_carry[:, cols] = dvv[BLOCK:]

        @pl.when(n == nb)
        def _():
            dk_ref[...] = dk_carry[...]
            dv_ref[...] = dv_carry[...]

    last = nb - 1
    cur = pl.BlockSpec((BLOCK, D_GRP), lambda c, n: (jnp.minimum(n, last), c))
    prev = pl.BlockSpec((BLOCK, D_GRP), lambda c, n: (jnp.clip(n - 1, 0, last), c))
    out = jax.ShapeDtypeStruct(qv.shape, F32)
    return pl.pallas_call(
        body, name=f"dil_bwd_r{r}", grid=(r, nb + 1),
        in_specs=[cur, prev, cur, prev, cur, cur, cur, cur],
        out_specs=(cur, prev, prev), out_shape=(out, out, out),
        scratch_shapes=[pltpu.VMEM((BLOCK, D_GRP), F32), pltpu.VMEM((BLOCK, D_GRP), F32)],
        compiler_params=_params(),
    )(qv, kv, kv, vv, vv, dov, lsev, deltav)


def _sb_fwd(qs, ks, vs, tri_suf):
    s_len = qs.shape[0]
    t = SB_TILE
    nq = s_len // t

    def body(q_ref, k_ref, v_ref, u_ref, o_ref, c_ref, acc, cf, csave):
        row = lax.broadcasted_iota(jnp.int32, (t, t), 0)
        col = lax.broadcasted_iota(jnp.int32, (t, t), 1)
        lane = lax.broadcasted_iota(jnp.int32, (t, 128), 1)
        head0 = lane < HEAD_DIM
        u = u_ref[...]

        def q_block(i, _):
            rows = pl.ds(pl.multiple_of(i * t, t), t)
            q2 = q_ref[rows, :]
            qh = (jnp.where(head0, q2, 0), jnp.where(head0, 0, q2))
            acc[...] = jnp.zeros_like(acc)
            cf[...] = jnp.zeros_like(cf)
            csave[...] = jnp.zeros_like(csave)

            def k_block(step, _):
                kb = i - step
                krows = pl.ds(pl.multiple_of(kb * t, t), t)
                k2, v2 = k_ref[krows, :], v_ref[krows, :]
                causal = col < row + (i - kb) * t
                for h in range(2):
                    z = _dot_nt(qh[h], k2) * ATT_SCALE
                    lb = _log_sigmoid(z)
                    lk = jnp.where(causal, lb - z, 0.0)
                    carry = cf[h]
                    suffix = _mm_split(lk, u) + carry
                    a = jnp.where(causal, jnp.exp(lb + suffix), 0.0)
                    pv = _dot(a.astype(BF16), v2)
                    keep = head0 if h == 0 else jnp.logical_not(head0)
                    acc[...] += jnp.where(keep, pv, 0.0)
                    csave[h] = jnp.where(lane == kb, carry, csave[h])
                    cf[h] = carry + jnp.sum(lk, axis=-1, keepdims=True)
                return 0

            lax.fori_loop(0, i + 1, k_block, 0)
            o_ref[rows, :] = acc[...]
            c_ref[0, rows, :] = csave[0]
            c_ref[1, rows, :] = csave[1]
            return 0

        lax.fori_loop(0, nq, q_block, 0)

    pair = pl.BlockSpec((s_len, 128), lambda p: (0, p))
    return pl.pallas_call(
        body, name="sb_fwd", grid=(4,),
        in_specs=[pair, pair, pair, _full((t, t))],
        out_specs=(pair, pl.BlockSpec((2, s_len, 128), lambda p: (p, 0, 0))),
        out_shape=(jax.ShapeDtypeStruct((s_len, D_GRP), F32),
                   jax.ShapeDtypeStruct((8, s_len, 128), F32)),
        scratch_shapes=[pltpu.VMEM((t, 128), F32), pltpu.VMEM((2, t, 1), F32),
                        pltpu.VMEM((2, t, 128), F32)],
        compiler_params=_params(),
    )(qs, ks, vs, tri_suf)


def _sb_bwd(qs, ks, vs, dos, csaved, tri_suf, tri_pre):
    s_len = qs.shape[0]
    t = SB_TILE
    nq = s_len // t

    def body(q_ref, k_ref, v_ref, do_ref, c_ref, u_ref, p_ref, dq_ref, dk_ref, dv_ref, dq_acc, cg):
        row = lax.broadcasted_iota(jnp.int32, (t, t), 0)
        col = lax.broadcasted_iota(jnp.int32, (t, t), 1)
        lane = lax.broadcasted_iota(jnp.int32, (t, 128), 1)
        head0 = lane < HEAD_DIM
        u, pm = u_ref[...], p_ref[...]
        dk_ref[...] = jnp.zeros_like(dk_ref)
        dv_ref[...] = jnp.zeros_like(dv_ref)

        def q_block(i, _):
            rows = pl.ds(pl.multiple_of(i * t, t), t)
            q2, do2 = q_ref[rows, :], do_ref[rows, :]
            qh = (jnp.where(head0, q2, 0), jnp.where(head0, 0, q2))
            doh = (jnp.where(head0, do2, 0), jnp.where(head0, 0, do2))
            dq_acc[...] = jnp.zeros_like(dq_acc)
            cg[...] = jnp.zeros_like(cg)

            def k_block(kb, _):
                krows = pl.ds(pl.multiple_of(kb * t, t), t)
                k2, v2 = k_ref[krows, :], v_ref[krows, :]
                causal = col < row + (i - kb) * t
                dk_new, dv_new = 0.0, 0.0
                for h in range(2):
                    z = _dot_nt(qh[h], k2) * ATT_SCALE
                    lb = _log_sigmoid(z)
                    beta = jnp.exp(lb)
                    lk = jnp.where(causal, lb - z, 0.0)
                    cfh = jnp.sum(jnp.where(lane == kb, c_ref[h, rows, :], 0.0), axis=-1, keepdims=True)
                    suffix = _mm_split(lk, u) + cfh
                    a = jnp.where(causal, jnp.exp(lb + suffix), 0.0)
                    g = a * _dot_nt(doh[h], v2)
                    carry = cg[h]
                    gpre = _mm_split(g, pm) + carry
                    cg[h] = carry + jnp.sum(g, axis=-1, keepdims=True)
                    dz = jnp.where(causal, (g * (1.0 - beta) - gpre * beta) * ATT_SCALE, 0.0).astype(BF16)
                    keep = head0 if h == 0 else jnp.logical_not(head0)
                    dq_acc[...] += jnp.where(keep, _dot(dz, k2), 0.0)
                    dk_new = dk_new + _dot_tn(dz, qh[h])
                    dv_new = dv_new + _dot_tn(a.astype(BF16), doh[h])
                dk_ref[krows, :] += dk_new
                dv_ref[krows, :] += dv_new
                return 0

            lax.fori_loop(0, i + 1, k_block, 0)
            dq_ref[rows, :] = dq_acc[...]
            return 0

        lax.fori_loop(0, nq, q_block, 0)

    pair = pl.BlockSpec((s_len, 128), lambda p: (0, p))
    out = jax.ShapeDtypeStruct((s_len, D_GRP), F32)
    return pl.pallas_call(
        body, name="sb_bwd", grid=(4,),
        in_specs=[pair, pair, pair, pair, pl.BlockSpec((2, s_len, 128), lambda p: (p, 0, 0)),
                  _full((t, t)), _full((t, t))],
        out_specs=(pair, pair, pair), out_shape=(out, out, out),
        scratch_shapes=[pltpu.VMEM((t, 128), F32), pltpu.VMEM((2, t, 1), F32)],
        compiler_params=_params(),
    )(qs, ks, vs, dos, csaved, tri_suf, tri_pre)


def _attn_out(o_b, lse_b, o_sb, x2, wdil, wsb, b_g):
    s_len = x2.shape[0]
    tm = ROW_TILE

    def body(o0, o1, o2, l0, l1, l2, osb_ref, x_ref, wdil_ref, wsb_ref, w_ref,
             odil_ref, lse_ref, mixed_ref, x1_ref):
        ls = (l0[...], l1[...], l2[...])
        mx = jnp.maximum(jnp.maximum(ls[0], ls[1]), ls[2])
        es = [jnp.exp(l - mx) for l in ls]
        den = es[0] + es[1] + es[2]
        o_dil = (es[0] * o0[...] + es[1] * o1[...] + es[2] * o2[...]) / den
        odil_ref[...] = o_dil
        lse_ref[...] = mx + jnp.log(den)
        halves = []
        for t, w_r in ((o_dil, wdil_ref), (osb_ref[...], wsb_ref)):
            r = lax.rsqrt(jnp.mean(t * t, axis=-1, keepdims=True) + EPS)
            halves.append(((t * r) * w_r[...]).astype(BF16))
        mixed = jnp.concatenate(halves, axis=1)
        mixed_ref[...] = mixed
        w = w_ref[...].reshape(D_MODEL, D_MODEL)
        x1_ref[...] = x_ref[...] + _dot(mixed, w)

    row = lambda w: pl.BlockSpec((tm, w), lambda i: (i, 0))
    return pl.pallas_call(
        body, name="attn_out", grid=(s_len // tm,),
        in_specs=[row(D_GRP)] * 7 + [row(D_MODEL), _full((1, D_GRP)), _full((1, D_GRP)),
                                     pl.BlockSpec((N_DEV, OUT_SHARD, D_MODEL), lambda i: (0, 3, 0))],
        out_specs=(row(D_GRP), row(D_GRP), row(D_MODEL), row(D_MODEL)),
        out_shape=(jax.ShapeDtypeStruct((s_len, D_GRP), F32), jax.ShapeDtypeStruct((s_len, D_GRP), F32),
                   jax.ShapeDtypeStruct((s_len, D_MODEL), BF16), jax.ShapeDtypeStruct((s_len, D_MODEL), F32)),
        compiler_params=_params(),
    )(*o_b, *lse_b, o_sb, x2, wdil, wsb, b_g)


def _ffn_fwd(x1, wn2, tgt, a_g, b_g):
    s_len = x1.shape[0]
    tm = ROW_TILE
    ni = s_len // tm

    def body(x_ref, wn_ref, t_ref, wg_ref, wu_ref, wd_ref, g_ref, u_ref, h2_ref, dy_ref, loss_ref, acc):
        j = pl.program_id(1)

        @pl.when(j == 0)
        def _():
            xx = x_ref[...]
            r = lax.rsqrt(jnp.mean(xx * xx, axis=-1, keepdims=True) + EPS)
            h2_ref[...] = ((xx * r) * wn_ref[...]).astype(BF16)
            acc[...] = jnp.zeros_like(acc)

        h = h2_ref[...]
        g = _dot(h, wg_ref[0])
        u = _dot(h, wu_ref[0])
        g_ref[...] = g
        u_ref[...] = u
        act = (g * (1.0 / (1.0 + jnp.exp(-g)))) * u
        acc[...] += _dot(act.astype(BF16), wd_ref[0])

        @pl.when(j == N_DEV - 1)
        def _():
            err = (x_ref[...] + acc[...]) - t_ref[...]
            dy_ref[...] = err * (1.0 / D_MODEL)
            part = 0.5 * jnp.sum(jnp.mean(err * err, axis=-1, keepdims=True))
            loss_ref[...] = jnp.full((8, 128), part, F32)

    row = pl.BlockSpec((tm, D_MODEL), lambda i, j: (i, 0))
    hid = pl.BlockSpec((tm, FF_PAD), lambda i, j: (i, j))
    return pl.pallas_call(
        body, name="ffn_fwd", grid=(ni, N_DEV),
        in_specs=[row, pl.BlockSpec((1, D_MODEL), lambda i, j: (0, 0)), row,
                  pl.BlockSpec((1, D_MODEL, FF_PAD), lambda i, j: (j, 1, 0)),
                  pl.BlockSpec((1, D_MODEL, FF_PAD), lambda i, j: (j, 2, 0)),
                  pl.BlockSpec((1, FF_PAD, D_MODEL), lambda i, j: (j, 0, 0))],
        out_specs=(hid, hid, row, row, pl.BlockSpec((8, 128), lambda i, j: (i, 0))),
        out_shape=(jax.ShapeDtypeStruct((s_len, N_DEV * FF_PAD), F32),
                   jax.ShapeDtypeStruct((s_len, N_DEV * FF_PAD), F32),
                   jax.ShapeDtypeStruct((s_len, D_MODEL), BF16),
                   jax.ShapeDtypeStruct((s_len, D_MODEL), F32),
                   jax.ShapeDtypeStruct((ni * 8, 128), F32)),
        scratch_shapes=[pltpu.VMEM((tm, D_MODEL), F32)],
        compiler_params=_params(),
    )(x1, wn2, tgt, a_g, a_g, b_g)


def _ffn_bwd_dx(dy, g, u, a_g, b_g):
    s_len = dy.shape[0]
    tm = ROW_TILE

    def body(dy_ref, g_ref, u_ref, wg_ref, wu_ref, wd_ref, dg_ref, du_ref, act_ref, dh_ref, acc):
        j = pl.program_id(1)

        @pl.when(j == 0)
        def _():
            acc[...] = jnp.zeros_like(acc)

        gg, uu = g_ref[...], u_ref[...]
        da = _dot_nt(dy_ref[...].astype(BF16), wd_ref[0])
        sig = 1.0 / (1.0 + jnp.exp(-gg))
        silu = gg * sig
        act_ref[...] = (silu * uu).astype(BF16)
        du = (da * silu).astype(BF16)
        dg = (da * uu * (sig * (1.0 + gg * (1.0 - sig)))).astype(BF16)
        du_ref[...] = du
        dg_ref[...] = dg
        acc[...] += _dot_nt(dg, wg_ref[0]) + _dot_nt(du, wu_ref[0])

        @pl.when(j == N_DEV - 1)
        def _():
            dh_ref[...] = acc[...]

    row = pl.BlockSpec((tm, D_MODEL), lambda i, j: (i, 0))
    hid = pl.BlockSpec((tm, FF_PAD), lambda i, j: (i, j))
    hid_bf = jax.ShapeDtypeStruct((s_len, N_DEV * FF_PAD), BF16)
    return pl.pallas_call(
        body, name="ffn_bwd_dx", grid=(s_len // tm, N_DEV),
        in_specs=[row, hid, hid,
                  pl.BlockSpec((1, D_MODEL, FF_PAD), lambda i, j: (j, 1, 0)),
                  pl.BlockSpec((1, D_MODEL, FF_PAD), lambda i, j: (j, 2, 0)),
                  pl.BlockSpec((1, FF_PAD, D_MODEL), lambda i, j: (j, 0, 0))],
        out_specs=(hid, hid, hid, row),
        out_shape=(hid_bf, hid_bf, hid_bf, jax.ShapeDtypeStruct((s_len, D_MODEL), F32)),
        scratch_shapes=[pltpu.VMEM((tm, D_MODEL), F32)],
        compiler_params=_params(),
    )(dy, g, u, a_g, a_g, b_g)


def _ffn_bwd_dw(h2, dy, dg, du, act):
    s_len = h2.shape[0]
    tm = ROW_TILE
    ni = s_len // tm

    def body(h_ref, dy_ref, dg_ref, du_ref, act_ref, dwg_ref, dwu_ref, dwd_ref, ag, au, ad):
        i = pl.program_id(1)

        @pl.when(i == 0)
        def _():
            ag[...] = jnp.zeros_like(ag)
            au[...] = jnp.zeros_like(au)
            ad[...] = jnp.zeros_like(ad)

        h = h_ref[...]
        ag[...] += _dot_tn(h, dg_ref[...])
        au[...] += _dot_tn(h, du_ref[...])
        ad[...] += _dot_tn(act_ref[...], dy_ref[...].astype(BF16))

        @pl.when(i == ni - 1)
        def _():
            dwg_ref[0] = ag[...].astype(BF16)
            dwu_ref[0] = au[...].astype(BF16)
            dwd_ref[0] = ad[...].astype(BF16)

    row = pl.BlockSpec((tm, D_MODEL), lambda j, i: (i, 0))
    hid = pl.BlockSpec((tm, FF_PAD), lambda j, i: (i, j))
    col_w = pl.BlockSpec((1, D_MODEL, FF_PAD), lambda j, i: (j, 0, 0))
    row_w = pl.BlockSpec((1, FF_PAD, D_MODEL), lambda j, i: (j, 0, 0))
    return pl.pallas_call(
        body, name="ffn_bwd_dw", grid=(N_DEV, ni),
        in_specs=[row, row, hid, hid, hid], out_specs=(col_w, col_w, row_w),
        out_shape=(jax.ShapeDtypeStruct((N_DEV, D_MODEL, FF_PAD), BF16),
                   jax.ShapeDtypeStruct((N_DEV, D_MODEL, FF_PAD), BF16),
                   jax.ShapeDtypeStruct((N_DEV, FF_PAD, D_MODEL), BF16)),
        scratch_shapes=[pltpu.VMEM((D_MODEL, FF_PAD), F32), pltpu.VMEM((D_MODEL, FF_PAD), F32),
                        pltpu.VMEM((FF_PAD, D_MODEL), F32)],
        compiler_params=_params(),
    )(h2, dy, dg, du, act)


def _rms_bwd(dy, t, w):
    r = lax.rsqrt(jnp.mean(t * t, axis=-1, keepdims=True) + EPS)
    gw = dy * w
    dt = r * (gw - t * ((r * r) * jnp.mean(gw * t, axis=-1, keepdims=True)))
    return dt, dy * t * r


def _attn_out_bwd(dy, dh2, x1, wn2, b_g, mixed, o_dil, o_sb, wdil, wsb, bd512):
    s_len = dy.shape[0]
    tm = ROW_TILE
    ni = s_len // tm

    def body(dy_ref, dh_ref, x1_ref, wn_ref, w_ref, mixed_ref, odil_ref, osb_ref, wdil_ref, wsb_ref, bd_ref,
             dx1_ref, dodil_ref, delta_ref, dosb_ref, dwout_ref, dwn_ref, dwdil_ref, dwsb_ref, wacc):
        i = pl.program_id(0)

        @pl.when(i == 0)
        def _():
            wacc[...] = jnp.zeros_like(wacc)
            dwn_ref[...] = jnp.zeros_like(dwn_ref)
            dwdil_ref[...] = jnp.zeros_like(dwdil_ref)
            dwsb_ref[...] = jnp.zeros_like(dwsb_ref)

        dnorm, dw_rows = _rms_bwd(dh_ref[...], x1_ref[...], wn_ref[...])
        dx1 = dy_ref[...] + dnorm
        dx1_ref[...] = dx1
        dwn_ref[...] += jnp.sum(dw_rows, axis=0, keepdims=True)
        dx1b = dx1.astype(BF16)
        w = w_ref[...].reshape(D_MODEL, D_MODEL)
        dmixed = _dot_nt(dx1b, w)
        wacc[...] += _dot_tn(mixed_ref[...], dx1b)
        o_dil = odil_ref[...]
        d_odil, dw_rows = _rms_bwd(dmixed[:, :D_GRP], o_dil, wdil_ref[...])
        dwdil_ref[...] += jnp.sum(dw_rows, axis=0, keepdims=True)
        dodil_ref[...] = d_odil.astype(BF16)
        delta_ref[...] = _mm_split(d_odil * o_dil, bd_ref[...])
        d_osb, dw_rows = _rms_bwd(dmixed[:, D_GRP:], osb_ref[...], wsb_ref[...])
        dwsb_ref[...] += jnp.sum(dw_rows, axis=0, keepdims=True)
        dosb_ref[...] = d_osb.astype(BF16)

        @pl.when(i == ni - 1)
        def _():
            dwout_ref[...] = wacc[...].astype(BF16).reshape(N_DEV, OUT_SHARD, D_MODEL)

    row = lambda w: pl.BlockSpec((tm, w), lambda i: (i, 0))
    return pl.pallas_call(
        body, name="attn_out_bwd", grid=(ni,),
        in_specs=[row(D_MODEL), row(D_MODEL), row(D_MODEL), _full((1, D_MODEL)),
                  pl.BlockSpec((N_DEV, OUT_SHARD, D_MODEL), lambda i: (0, 3, 0)),
                  row(D_MODEL), row(D_GRP), row(D_GRP), _full((1, D_GRP)), _full((1, D_GRP)),
                  _full((D_GRP, D_GRP))],
        out_specs=(row(D_MODEL), row(D_GRP), row(D_GRP), row(D_GRP),
                   _full((N_DEV, OUT_SHARD, D_MODEL)), _full((1, D_MODEL)), _full((1, D_GRP)), _full((1, D_GRP))),
        out_shape=(jax.ShapeDtypeStruct((s_len, D_MODEL), F32), jax.ShapeDtypeStruct((s_len, D_GRP), BF16),
                   jax.ShapeDtypeStruct((s_len, D_GRP), F32), jax.ShapeDtypeStruct((s_len, D_GRP), BF16),
                   jax.ShapeDtypeStruct((N_DEV, OUT_SHARD, D_MODEL), BF16),
                   jax.ShapeDtypeStruct((1, D_MODEL), F32), jax.ShapeDtypeStruct((1, D_GRP), F32),
                   jax.ShapeDtypeStruct((1, D_GRP), F32)),
        scratch_shapes=[pltpu.VMEM((D_MODEL, D_MODEL), F32)],
        compiler_params=_params(),
    )(dy, dh2, x1, wn2, b_g, mixed, o_dil, o_sb, wdil, wsb, bd512)


def _qkv_bwd(dq_b, dk_b, dv_b, dqs, dks, dvs, qraw, kraw, cos2, sin2, qnw, knw, bd):
    s_len = qraw.shape[0]
    tm = ROW_TILE
    ni = s_len // tm

    def body(dq0, dq1, dq2, dk0, dk1, dk2, dv0, dv1, dv2, dqs_ref, dks_ref, dvs_ref,
             qraw_ref, kraw_ref, cos_ref, sin_ref, qnw_ref, knw_ref, bd_ref,
             dproj_ref, dqn_ref, dkn_ref):
        i = pl.program_id(0)

        @pl.when(i == 0)
        def _():
            dqn_ref[...] = jnp.zeros_like(dqn_ref)
            dkn_ref[...] = jnp.zeros_like(dkn_ref)

        cos_t, sin_t, bdm = cos_ref[...], sin_ref[...], bd_ref[...]
        for grp, (parts, raw_ref, nw_ref, dn_ref) in enumerate((((dq0, dq1, dq2), qraw_ref, qnw_ref, dqn_ref),
                                                                ((dk0, dk1, dk2), kraw_ref, knw_ref, dkn_ref))):
            dn_acc = 0.0
            for p in range(4):
                cols = slice(128 * p, 128 * (p + 1))
                d_rope = parts[0][:, cols] + parts[1][:, cols] + parts[2][:, cols]
                d_norm = d_rope * cos_t + _swap_halves(d_rope * sin_t)
                t = raw_ref[:, cols]
                w = nw_ref[...]
                r = lax.rsqrt(_mm_split(t * t, bdm) * (1.0 / HEAD_DIM) + EPS)
                gw = d_norm * w
                corr = _mm_split(gw * t, bdm) * (1.0 / HEAD_DIM)
                dt = r * (gw - t * ((r * r) * corr))
                dn_acc = dn_acc + jnp.sum(d_norm * t * r, axis=0, keepdims=True)
                dproj_ref[:, D_GRP * grp + 128 * p:D_GRP * grp + 128 * (p + 1)] = dt.astype(BF16)
            dn_ref[...] += dn_acc
        dproj_ref[:, 2 * D_GRP:3 * D_GRP] = (dv0[...] + dv1[...] + dv2[...]).astype(BF16)
        dproj_ref[:, 3 * D_GRP:4 * D_GRP] = dqs_ref[...].astype(BF16)
        dproj_ref[:, 4 * D_GRP:5 * D_GRP] = dks_ref[...].astype(BF16)
        dproj_ref[:, 5 * D_GRP:6 * D_GRP] = dvs_ref[...].astype(BF16)

    row = lambda w: pl.BlockSpec((tm, w), lambda i: (i, 0))
    return pl.pallas_call(
        body, name="qkv_bwd", grid=(ni,),
        in_specs=[row(D_GRP)] * 14 + [row(128), row(128), _full((1, 128)), _full((1, 128)), _full((128, 128))],
        out_specs=(row(D_IN), _full((1, 128)), _full((1, 128))),
        out_shape=(jax.ShapeDtypeStruct((s_len, D_IN), BF16), jax.ShapeDtypeStruct((1, 128), F32),
                   jax.ShapeDtypeStruct((1, 128), F32)),
        compiler_params=_params(),
    )(*dq_b, *dk_b, *dv_b, dqs, dks, dvs, qraw, kraw, cos2, sin2, qnw, knw, bd)


def _in_bwd_dx(dproj, a_g, x2, dx1, wn1):
    s_len = x2.shape[0]
    tm = ROW_TILE
    ni = s_len // tm

    def body(dp_ref, w_ref, x_ref, dx1_ref, wn_ref, gx_ref, dwn_ref):
        i = pl.program_id(0)

        @pl.when(i == 0)
        def _():
            dwn_ref[...] = jnp.zeros_like(dwn_ref)

        dh = 0.0
        for d in range(N_DEV):
            dh = dh + _dot_nt(dp_ref[:, IN_SHARD * d:IN_SHARD * (d + 1)], w_ref[d])
        dnorm, dw_rows = _rms_bwd(dh, x_ref[...], wn_ref[...])
        gx_ref[...] = dx1_ref[...] + dnorm
        dwn_ref[...] += jnp.sum(dw_rows, axis=0, keepdims=True)

    row = lambda w: pl.BlockSpec((tm, w), lambda i: (i, 0))
    return pl.pallas_call(
        body, name="in_bwd_dx", grid=(ni,),
        in_specs=[row(D_IN), pl.BlockSpec((N_DEV, D_MODEL, IN_SHARD), lambda i: (0, 0, 0)),
                  row(D_MODEL), row(D_MODEL), _full((1, D_MODEL))],
        out_specs=(row(D_MODEL), _full((1, D_MODEL))),
        out_shape=(jax.ShapeDtypeStruct((s_len, D_MODEL), F32), jax.ShapeDtypeStruct((1, D_MODEL), F32)),
        compiler_params=_params(),
    )(dproj, a_g, x2, dx1, wn1)


def _in_bwd_dw(h1, dproj):
    s_len = h1.shape[0]
    tm = ROW_TILE
    ni = s_len // tm

    def body(h_ref, dp_ref, dw_ref, acc):
        i = pl.program_id(1)

        @pl.when(i == 0)
        def _():
            acc[...] = jnp.zeros_like(acc)

        acc[...] += _dot_tn(h_ref[...], dp_ref[...])

        @pl.when(i == ni - 1)
        def _():
            dw_ref[0] = acc[...].astype(BF16)

    return pl.pallas_call(
        body, name="in_bwd_dw", grid=(N_DEV, ni),
        in_specs=[pl.BlockSpec((tm, D_MODEL), lambda d, i: (i, 0)),
                  pl.BlockSpec((tm, IN_SHARD), lambda d, i: (i, d))],
        out_specs=pl.BlockSpec((1, D_MODEL, IN_SHARD), lambda d, i: (d, 0, 0)),
        out_shape=jax.ShapeDtypeStruct((N_DEV, D_MODEL, IN_SHARD), BF16),
        scratch_shapes=[pltpu.VMEM((D_MODEL, IN_SHARD), F32)],
        compiler_params=_params(),
    )(h1, dproj)


def _adamw(recv, w, m, v):
    rows, cols = w.shape
    tr = 128 if rows % 128 == 0 else rows

    def body(p_ref, w_ref, m_ref, v_ref, g_ref, d_ref, nm_ref, nv_ref):
        g = p_ref[0].astype(F32)
        for s in range(1, N_DEV):
            g = g + p_ref[s].astype(F32)
        m_new = ADAM_B1 * m_ref[...] + (1.0 - ADAM_B1) * g
        v_new = ADAM_B2 * v_ref[...] + (1.0 - ADAM_B2) * (g * g)
        m_hat = m_new / (1.0 - ADAM_B1 ** ADAM_STEP)
        v_hat = v_new / (1.0 - ADAM_B2 ** ADAM_STEP)
        g_ref[...] = g
        d_ref[...] = -ADAM_LR * (m_hat / (jnp.sqrt(v_hat) + ADAM_EPS) + ADAM_WD * w_ref[...])
        nm_ref[...] = m_new
        nv_ref[...] = v_new

    blk = pl.BlockSpec((tr, cols), lambda i: (i, 0))
    out = jax.ShapeDtypeStruct((rows, cols), F32)
    return pl.pallas_call(
        body, name=f"adamw_{rows}x{cols}", grid=(rows // tr,),
        in_specs=[pl.BlockSpec((N_DEV, tr, cols), lambda i: (0, i, 0)), blk, blk, blk],
        out_specs=(blk,) * 4, out_shape=(out,) * 4,
        compiler_params=_params(),
    )(recv, w, m, v)


def _rope_tables(s_len):
    pos = jnp.arange(s_len, dtype=F32)
    inv_freq = ROPE_THETA ** (-jnp.arange(0, HEAD_DIM, 2, dtype=F32) / HEAD_DIM)
    ang = pos[:, None] * inv_freq[None, :]
    cos, sin = jnp.cos(ang), jnp.sin(ang)
    cos2 = jnp.concatenate([cos, cos, cos, cos], axis=1)
    sin2 = jnp.concatenate([-sin, sin, -sin, sin], axis=1)
    return cos2, sin2


def _block_diag_ones(n):
    i = jnp.arange(n)
    return (i[:, None] // HEAD_DIM == i[None, :] // HEAD_DIM).astype(BF16)


def _pad_cols(t):
    return jnp.pad(t, ((0, 0), (0, FF_PAD - FF_SHARD)))


def _pad_rows(t):
    return jnp.pad(t, ((0, FF_PAD - FF_SHARD), (0, 0)))


def _pack_small(n1, n2, ndil, nsb, nq, nk):
    pad = lambda t: jnp.pad(t.reshape(1, HEAD_DIM), ((0, 0), (0, 128 - HEAD_DIM)))
    rows = [n1.reshape(8, 128), n2.reshape(8, 128), ndil.reshape(4, 128), nsb.reshape(4, 128),
            pad(nq), pad(nk), jnp.zeros((6, 128), F32)]
    return jnp.concatenate(rows, axis=0)


def _unpack_small(t):
    return (t[0:8].reshape(1, D_MODEL), t[8:16].reshape(1, D_MODEL), t[16:20].reshape(1, D_GRP),
            t[20:24].reshape(1, D_GRP), t[24:25, :HEAD_DIM], t[25:26, :HEAD_DIM])


def kernel(x, attn_norm_w, w_in, q_norm_w, k_norm_w, dil_out_norm_w, sb_out_norm_w, w_out, ffn_norm_w, w_gate, w_up, w_down, loss_target, m_attn_norm_w, m_w_in, m_q_norm_w, m_k_norm_w, m_dil_out_norm_w, m_sb_out_norm_w, m_w_out, m_ffn_norm_w, m_w_gate, m_w_up, m_w_down, v_attn_norm_w, v_w_in, v_q_norm_w, v_k_norm_w, v_dil_out_norm_w, v_sb_out_norm_w, v_w_out, v_ffn_norm_w, v_w_gate, v_w_up, v_w_down):
    s_len = x.shape[1]
    x2, tgt = x[0], loss_target[0]

    a_loc = jnp.concatenate([w_in[0], _pad_cols(w_gate[0]), _pad_cols(w_up[0])], axis=0).astype(BF16)
    b_loc = jnp.concatenate([_pad_rows(w_down[0]), w_out[0]], axis=0).astype(BF16)
    a_g, b_g = _gather_weights(a_loc, b_loc)

    cos2, sin2 = _rope_tables(s_len)
    bd128, bd512 = _block_diag_ones(128), _block_diag_ones(D_GRP)
    idx = jnp.arange(SB_TILE)
    tri_suf = (idx[:, None] > idx[None, :]).astype(BF16)
    tri_pre = (idx[:, None] < idx[None, :]).astype(BF16)
    qnw2 = jnp.concatenate([q_norm_w, q_norm_w], axis=1)
    knw2 = jnp.concatenate([k_norm_w, k_norm_w], axis=1)

    h1, qraw, kraw, q, k, va, qs, ks, vs = _attn_in(x2, attn_norm_w, a_g, cos2, sin2, qnw2, knw2, bd128)
    view = lambda t, r: t.reshape(s_len // r, r * D_GRP)
    unview = lambda t: t.reshape(s_len, D_GRP)
    o_b, lse_b = [], []
    for r in DILATIONS:
        o, lse = _dil_fwd(view(q, r), view(k, r), view(va, r), r)
        o_b.append(unview(o))
        lse_b.append(unview(lse))
    o_sb, c_sb = _sb_fwd(qs, ks, vs, tri_suf)
    o_dil, lse_tot, mixed, x1 = _attn_out(o_b, lse_b, o_sb, x2, dil_out_norm_w, sb_out_norm_w, b_g)
    g, u, h2, dy, loss_parts = _ffn_fwd(x1, ffn_norm_w, tgt, a_g, b_g)
    loss = lax.psum(jnp.sum(loss_parts[::8, 0]), ("x", "y", "c"))

    dg, du, act, dh2 = _ffn_bwd_dx(dy, g, u, a_g, b_g)
    dwg, dwu, dwd = _ffn_bwd_dw(h2, dy, dg, du, act)
    dx1, do_dil, delta, do_sb, dwout, dn2, dndil, dnsb = _attn_out_bwd(
        dy, dh2, x1, ffn_norm_w, b_g, mixed, o_dil, o_sb, dil_out_norm_w, sb_out_norm_w, bd512)
    dqs, dks, dvs = _sb_bwd(qs, ks, vs, do_sb, c_sb, tri_suf, tri_pre)
    dq_b, dk_b, dv_b = [], [], []
    for r in DILATIONS:
        dq, dk, dv = _dil_bwd(view(q, r), view(k, r), view(va, r), view(do_dil, r), view(lse_tot, r),
                              view(delta, r), r)
        dq_b.append(unview(dq))
        dk_b.append(unview(dk))
        dv_b.append(unview(dv))
    dproj, dqn2, dkn2 = _qkv_bwd(dq_b, dk_b, dv_b, dqs, dks, dvs, qraw, kraw, cos2, sin2, qnw2, knw2, bd128)
    grad_x, dn1 = _in_bwd_dx(dproj, a_g, x2, dx1, attn_norm_w)
    dwin = _in_bwd_dw(h1, dproj)
    dqn = dqn2[:, :HEAD_DIM] + dqn2[:, HEAD_DIM:]
    dkn = dkn2[:, :HEAD_DIM] + dkn2[:, HEAD_DIM:]

    small = _pack_small(dn1, dn2, dndil, dnsb, dqn, dkn)
    r_in, r_gate, r_up, r_down, r_out, r_small = _exchange_grads([dwin, dwg, dwu, dwd, dwout], small)
    big = {
        "w_in": _adamw(r_in, w_in[0], m_w_in[0], v_w_in[0]),
        "w_gate": tuple(t[:, :FF_SHARD] for t in _adamw(r_gate, _pad_cols(w_gate[0]), _pad_cols(m_w_gate[0]), _pad_cols(v_w_gate[0]))),
        "w_up": tuple(t[:, :FF_SHARD] for t in _adamw(r_up, _pad_cols(w_up[0]), _pad_cols(m_w_up[0]), _pad_cols(v_w_up[0]))),
        "w_down": tuple(t[:FF_SHARD] for t in _adamw(r_down, _pad_rows(w_down[0]), _pad_rows(m_w_down[0]), _pad_rows(v_w_down[0]))),
        "w_out": _adamw(r_out, w_out[0], m_w_out[0], v_w_out[0]),
    }
    packs = [_pack_small(*ts) for ts in (
        (attn_norm_w, ffn_norm_w, dil_out_norm_w, sb_out_norm_w, q_norm_w, k_norm_w),
        (m_attn_norm_w, m_ffn_norm_w, m_dil_out_norm_w, m_sb_out_norm_w, m_q_norm_w, m_k_norm_w),
        (v_attn_norm_w, v_ffn_norm_w, v_dil_out_norm_w, v_sb_out_norm_w, v_q_norm_w, v_k_norm_w))]
    small_out = [_unpack_small(t) for t in _adamw(r_small, *packs)]
    names = ["attn_norm_w", "w_in", "q_norm_w", "k_norm_w", "dil_out_norm_w", "sb_out_norm_w", "w_out",
             "ffn_norm_w", "w_gate", "w_up", "w_down"]
    small_pos = {"attn_norm_w": 0, "ffn_norm_w": 1, "dil_out_norm_w": 2, "sb_out_norm_w": 3,
                 "q_norm_w": 4, "k_norm_w": 5}
    outs = [loss, grad_x[None]]
    for kind in range(4):
        for name in names:
            if name in small_pos:
                outs.append(small_out[kind][small_pos[name]])
            else:
                outs.append(big[name][kind][None])
    return tuple(outs)
```

```python
import functools

import jax
import jax.numpy as jnp
from jax import lax
from jax.experimental import pallas as pl
from jax.experimental.pallas import tpu as pltpu

F32 = jnp.float32
BF16 = jnp.bfloat16

N_DEV = 8
D_MODEL = 1024
HEAD_DIM = 64
D_GRP = 512
D_IN = 6 * D_GRP
IN_SHARD = D_IN // N_DEV
FF_SHARD = 352
FF_PAD = 384
OUT_SHARD = D_MODEL // N_DEV
BLOCK = 128
DILATIONS = (1, 4, 16)
ROPE_THETA = 10000.0
EPS = 1e-6
ATT_SCALE = HEAD_DIM ** -0.5
NEG = -1e30
LOG2_E = 1.4426950408889634

ADAM_LR = 0.001
ADAM_B1 = 0.9
ADAM_B2 = 0.999
ADAM_EPS = 1e-08
ADAM_WD = 0.01
ADAM_STEP = 10

SB_TILE = 256
SB_PAIRS = 2
ROW_TILE = 512
VMEM_LIMIT = 56 * 1024 * 1024
MESH = pl.DeviceIdType.MESH


def _dot(a, b):
    return jnp.dot(a, b, preferred_element_type=F32)


def _dot_nt(a, b):
    return lax.dot_general(a, b, (((1,), (1,)), ((), ())), preferred_element_type=F32)


def _dot_tn(a, b):
    return lax.dot_general(a, b, (((0,), (0,)), ((), ())), preferred_element_type=F32)


def _mm_split(t, m):
    hi = t.astype(BF16)
    lo = (t - hi.astype(F32)).astype(BF16)
    return _dot(hi, m) + _dot(lo, m)


def _params(**kw):
    return pltpu.CompilerParams(vmem_limit_bytes=VMEM_LIMIT, **kw)


def _full(shape):
    nd = len(shape)
    return pl.BlockSpec(shape, lambda *_: (0,) * nd)


def _swap_halves(t):
    lane = lax.broadcasted_iota(jnp.int32, t.shape, 1)
    first = (lane & 32) == 0
    return jnp.where(first, pltpu.roll(t, 96, 1), pltpu.roll(t, 32, 1))


def _log_sigmoid(z):
    return jnp.minimum(z, 0.0) - jnp.log(1.0 + jnp.exp(-jnp.abs(z)))


def _log2_sigmoid_pair(z):
    neg_abs = lax.bitcast_convert_type(lax.bitcast_convert_type(z, jnp.uint32) | jnp.uint32(0x80000000), F32)
    lb = jnp.minimum(z, 0.0) - jnp.log2(1.0 + jnp.exp2(neg_abs))
    return lb, lb - z


def _cumsum_mm(t, tri2):
    hi = lax.bitcast_convert_type(lax.bitcast_convert_type(t, jnp.uint32) & jnp.uint32(0xFFFF0000), F32)
    lhs = jnp.concatenate([hi.astype(BF16), (t - hi).astype(BF16)], axis=1)
    return _dot(lhs, tri2)


def _mesh_pos():
    return lax.axis_index("x"), lax.axis_index("y"), lax.axis_index("c")


def _flat_index(p):
    return 4 * p[0] + 2 * p[1] + p[2]


def _gather_weights(a, b):
    def body(a_ref, b_ref, ag_ref, bg_ref, send_sems, recv_sems, local_sems):
        x, y, c = _mesh_pos()
        me, sibling = (x, y, c), (x, y, 1 - c)
        chips = [(1 - x, y), (x, 1 - y), (1 - x, 1 - y)]
        srcs, outs = (a_ref, b_ref), (ag_ref, bg_ref)

        def copy(arr, k, block, to, own=False):
            dst = outs[arr].at[_flat_index(block)]
            return pltpu.make_async_remote_copy(
                src_ref=srcs[arr] if own else dst, dst_ref=dst,
                send_sem=send_sems.at[arr, k], recv_sem=recv_sems.at[arr, k],
                device_id=to, device_id_type=MESH)

        for arr in range(2):
            mine = pltpu.make_async_copy(srcs[arr], outs[arr].at[_flat_index(me)], local_sems.at[arr])
            mine.start()
            first = [copy(arr, 0, me, sibling, own=True)]
            first += [copy(arr, 1 + j, me, (*chip, c), own=True) for j, chip in enumerate(chips)]
            for cp in first:
                cp.start()
        for arr in range(2):
            passed = [copy(arr, 4 + j, (*chip, c), sibling) for j, chip in enumerate(chips)]
            for j, chip in enumerate(chips):
                copy(arr, 1 + j, (*chip, c), me).wait_recv()
                passed[j].start()
        for arr in range(2):
            copy(arr, 0, sibling, me).wait_recv()
            for j, chip in enumerate(chips):
                copy(arr, 4 + j, (*chip, 1 - c), me).wait_recv()
            for k in range(7):
                copy(arr, k, me, me).wait_send()
            pltpu.make_async_copy(srcs[arr], outs[arr].at[_flat_index(me)], local_sems.at[arr]).wait()

    any_spec = pl.BlockSpec(memory_space=pl.ANY)
    return pl.pallas_call(
        body, name="gather_weights",
        out_shape=(jax.ShapeDtypeStruct((N_DEV,) + a.shape, a.dtype),
                   jax.ShapeDtypeStruct((N_DEV,) + b.shape, b.dtype)),
        in_specs=[any_spec, any_spec], out_specs=(any_spec, any_spec),
        scratch_shapes=[pltpu.SemaphoreType.DMA((2, 7)), pltpu.SemaphoreType.DMA((2, 7)),
                        pltpu.SemaphoreType.DMA((2,))],
        compiler_params=pltpu.CompilerParams(has_side_effects=True),
    )(a, b)


def _exchange_grads(parts, small):
    n_arr = len(parts)

    def body(*refs):
        ins, outs = refs[:n_arr + 1], refs[n_arr + 1:2 * (n_arr + 1)]
        send_sems, recv_sems, local_sems = refs[2 * (n_arr + 1):]
        x, y, c = _mesh_pos()
        me = (x, y, c)
        my_idx = _flat_index(me)
        peers = []
        for m in range(1, N_DEV):
            peers.append((1 - x if m & 4 else x, 1 - y if m & 2 else y, 1 - c if m & 1 else c))

        def src_block(arr, dev):
            return ins[arr] if arr == n_arr else ins[arr].at[_flat_index(dev)]

        def copy(arr, k):
            return pltpu.make_async_remote_copy(
                src_ref=src_block(arr, peers[k]), dst_ref=outs[arr].at[my_idx],
                send_sem=send_sems.at[arr, k], recv_sem=recv_sems.at[arr, k],
                device_id=peers[k], device_id_type=MESH)

        def local(arr):
            return pltpu.make_async_copy(src_block(arr, me), outs[arr].at[my_idx], local_sems.at[arr])

        for arr in range(n_arr + 1):
            local(arr).start()
            for k in range(N_DEV - 1):
                copy(arr, k).start()
        for arr in range(n_arr + 1):
            for k in range(N_DEV - 1):
                cp = copy(arr, k)
                cp.wait_send()
                cp.wait_recv()
            local(arr).wait()

    any_spec = pl.BlockSpec(memory_space=pl.ANY)
    out_shape = tuple(jax.ShapeDtypeStruct(p.shape, p.dtype) for p in parts)
    out_shape += (jax.ShapeDtypeStruct((N_DEV,) + small.shape, small.dtype),)
    return pl.pallas_call(
        body, name="exchange_grads",
        out_shape=out_shape,
        in_specs=[any_spec] * (n_arr + 1), out_specs=(any_spec,) * (n_arr + 1),
        scratch_shapes=[pltpu.SemaphoreType.DMA((n_arr + 1, N_DEV - 1)),
                        pltpu.SemaphoreType.DMA((n_arr + 1, N_DEV - 1)),
                        pltpu.SemaphoreType.DMA((n_arr + 1,))],
        compiler_params=pltpu.CompilerParams(has_side_effects=True),
    )(*parts, small)


def _head_norm(t, w128, bd):
    ms = _mm_split(t * t, bd) * (1.0 / HEAD_DIM)
    r = lax.rsqrt(ms + EPS)
    return (t * r) * w128, r


def _attn_in(x2, wn1, a_g, cos2, sin2, qnw, knw, bd):
    s_len = x2.shape[0]
    tm = ROW_TILE

    def body(x_ref, wn_ref, w_ref, cos_ref, sin_ref, qnw_ref, knw_ref, bd_ref,
             h1_ref, qraw_ref, kraw_ref, q_ref, k_ref, va_ref, qs_ref, ks_ref, vs_ref, proj):
        xx = x_ref[...]
        r = lax.rsqrt(jnp.mean(xx * xx, axis=-1, keepdims=True) + EPS)
        h = ((xx * r) * wn_ref[...]).astype(BF16)
        h1_ref[...] = h
        for d in range(N_DEV):
            proj[:, IN_SHARD * d:IN_SHARD * (d + 1)] = _dot(h, w_ref[d])
        cos_t, sin_t, bdm = cos_ref[...], sin_ref[...], bd_ref[...]
        for grp, (raw_ref, rope_ref, nw_ref) in enumerate(((qraw_ref, q_ref, qnw_ref),
                                                           (kraw_ref, k_ref, knw_ref))):
            for p in range(4):
                cols = slice(D_GRP * grp + 128 * p, D_GRP * grp + 128 * (p + 1))
                t = proj[:, cols]
                raw_ref[:, 128 * p:128 * (p + 1)] = t
                yn, _ = _head_norm(t, nw_ref[...], bdm)
                rope_ref[:, 128 * p:128 * (p + 1)] = (yn * cos_t + _swap_halves(yn) * sin_t).astype(BF16)
        for grp, ref in ((2, va_ref), (3, qs_ref), (4, ks_ref), (5, vs_ref)):
            ref[...] = proj[:, D_GRP * grp:D_GRP * (grp + 1)].astype(BF16)

    row = lambda w: pl.BlockSpec((tm, w), lambda i: (i, 0))
    grp_bf = jax.ShapeDtypeStruct((s_len, D_GRP), BF16)
    grp_f32 = jax.ShapeDtypeStruct((s_len, D_GRP), F32)
    return pl.pallas_call(
        body, name="attn_in", grid=(s_len // tm,),
        in_specs=[row(D_MODEL), _full((1, D_MODEL)),
                  pl.BlockSpec((N_DEV, D_MODEL, IN_SHARD), lambda i: (0, 0, 0)),
                  row(128), row(128), _full((1, 128)), _full((1, 128)), _full((128, 128))],
        out_specs=(row(D_MODEL),) + (row(D_GRP),) * 8,
        out_shape=(jax.ShapeDtypeStruct((s_len, D_MODEL), BF16), grp_f32, grp_f32) + (grp_bf,) * 6,
        scratch_shapes=[pltpu.VMEM((tm, D_IN), F32)],
        compiler_params=_params(),
    )(x2, wn1, a_g, cos2, sin2, qnw, knw, bd)


def _band_mask(n):
    i = lax.broadcasted_iota(jnp.int32, (BLOCK, 2 * BLOCK), 0)
    j = lax.broadcasted_iota(jnp.int32, (BLOCK, 2 * BLOCK), 1)
    dist = i + BLOCK - j
    return (dist >= 0) & (dist <= BLOCK) & ((n - 1) * BLOCK + j >= 0)


def _dil_fwd(qv, kv, vv, r):
    sub_len = qv.shape[0]
    nb = sub_len // BLOCK

    def body(q_ref, kp_ref, kc_ref, vp_ref, vc_ref, o_ref, lse_ref):
        n = pl.program_id(1)
        valid = _band_mask(n)
        lane = lax.broadcasted_iota(jnp.int32, (BLOCK, 128), 1)
        head0 = lane < HEAD_DIM
        for p in range(4):
            cols = slice(128 * p, 128 * (p + 1))
            q2 = q_ref[:, cols]
            kk = jnp.concatenate([kp_ref[:, cols], kc_ref[:, cols]], axis=0)
            vv2 = jnp.concatenate([vp_ref[:, cols], vc_ref[:, cols]], axis=0)
            res = []
            for h in range(2):
                qh = jnp.where(head0, q2, 0) if h == 0 else jnp.where(head0, 0, q2)
                s = jnp.where(valid, _dot_nt(qh, kk) * ATT_SCALE, NEG)
                m = jnp.max(s, axis=-1, keepdims=True)
                pr = jnp.exp(s - m)
                den = jnp.sum(pr, axis=-1, keepdims=True)
                o = _dot(pr.astype(BF16), vv2) / den
                res.append((o, m + jnp.log(den)))
            o_ref[:, cols] = jnp.where(head0, res[0][0], res[1][0])
            lse_ref[:, cols] = jnp.where(head0, res[0][1], res[1][1])

    cur = pl.BlockSpec((BLOCK, D_GRP), lambda c, n: (n, c))
    prev = pl.BlockSpec((BLOCK, D_GRP), lambda c, n: (jnp.maximum(n - 1, 0), c))
    out = jax.ShapeDtypeStruct(qv.shape, F32)
    return pl.pallas_call(
        body, name=f"dil_fwd_r{r}", grid=(r, nb),
        in_specs=[cur, prev, cur, prev, cur], out_specs=(cur, cur), out_shape=(out, out),
        compiler_params=_params(),
    )(qv, kv, kv, vv, vv)


def _dil_bwd(qv, kv, vv, dov, lsev, deltav, r):
    sub_len = qv.shape[0]
    nb = sub_len // BLOCK

    def body(q_ref, kp_ref, kc_ref, vp_ref, vc_ref, do_ref, lse_ref, dl_ref,
             dq_ref, dk_ref, dv_ref, dk_carry, dv_carry):
        n = pl.program_id(1)

        @pl.when(n == 0)
        def _():
            dk_carry[...] = jnp.zeros_like(dk_carry)
            dv_carry[...] = jnp.zeros_like(dv_carry)

        @pl.when(n < nb)
        def _():
            valid = _band_mask(n)
            lane = lax.broadcasted_iota(jnp.int32, (BLOCK, 128), 1)
            head0 = lane < HEAD_DIM
            for p in range(4):
                cols = slice(128 * p, 128 * (p + 1))
                q2, do2 = q_ref[:, cols], do_ref[:, cols]
                lse2, dl2 = lse_ref[:, cols], dl_ref[:, cols]
                kk = jnp.concatenate([kp_ref[:, cols], kc_ref[:, cols]], axis=0)
                vv2 = jnp.concatenate([vp_ref[:, cols], vc_ref[:, cols]], axis=0)
                dq_h, dkk, dvv = [], 0.0, 0.0
                for h in range(2):
                    sel = (lambda t: jnp.where(head0, t, 0)) if h == 0 else (lambda t: jnp.where(head0, 0, t))
                    qh, doh = sel(q2), sel(do2)
                    one_lane = lane == (0 if h == 0 else HEAD_DIM)
                    lse = jnp.sum(jnp.where(one_lane, lse2, 0.0), axis=-1, keepdims=True)
                    dl = jnp.sum(jnp.where(one_lane, dl2, 0.0), axis=-1, keepdims=True)
                    s = _dot_nt(qh, kk) * ATT_SCALE
                    pr = jnp.where(valid, jnp.exp(jnp.minimum(s - lse, 0.0)), 0.0)
                    dp = _dot_nt(doh, vv2)
                    ds = (pr * (dp - dl) * ATT_SCALE).astype(BF16)
                    dq_h.append(_dot(ds, kk))
                    dkk = dkk + _dot_tn(ds, qh)
                    dvv = dvv + _dot_tn(pr.astype(BF16), doh)
                dq_ref[:, cols] = jnp.where(head0, dq_h[0], dq_h[1])
                dk_ref[:, cols] = dk_carry[:, cols] + dkk[:BLOCK]
                dv_ref[:, cols] = dv_carry[:, cols] + dvv[:BLOCK]
                dk_carry[:, cols] = dkk[BLOCK:]
                dv_carry[:, cols] = dvv[BLOCK:]

        @pl.when(n == nb)
        def _():
            dk_ref[...] = dk_carry[...]
            dv_ref[...] = dv_carry[...]

    last = nb - 1
    cur = pl.BlockSpec((BLOCK, D_GRP), lambda c, n: (jnp.minimum(n, last), c))
    prev = pl.BlockSpec((BLOCK, D_GRP), lambda c, n: (jnp.clip(n - 1, 0, last), c))
    out = jax.ShapeDtypeStruct(qv.shape, F32)
    return pl.pallas_call(
        body, name=f"dil_bwd_r{r}", grid=(r, nb + 1),
        in_specs=[cur, prev, cur, prev, cur, cur, cur, cur],
        out_specs=(cur, prev, prev), out_shape=(out, out, out),
        scratch_shapes=[pltpu.VMEM((BLOCK, D_GRP), F32), pltpu.VMEM((BLOCK, D_GRP), F32)],
        compiler_params=_params(),
    )(qv, kv, kv, vv, vv, dov, lsev, deltav)


def _sb_fwd(qs, ks, vs, tri_suf):
    s_len = qs.shape[0]
    t = SB_TILE
    nq = s_len // t

    npair = SB_PAIRS

    def body(q_ref, k_ref, v_ref, u_ref, o_ref, c_ref, qq, acc, cf, csave):
        row = lax.broadcasted_iota(jnp.int32, (2 * t, t), 0) & (t - 1)
        col = lax.broadcasted_iota(jnp.int32, (2 * t, t), 1)
        diag_mask = col < row
        lane1 = lax.broadcasted_iota(jnp.int32, (t, 128), 1)
        head0 = lane1 < HEAD_DIM
        lane2 = lax.broadcasted_iota(jnp.int32, (2 * t, 128), 1)
        uu = u_ref[...]

        def q_block(i, _):
            rows = pl.ds(pl.multiple_of(i * t, t), t)
            for pp in range(npair):
                q2 = q_ref[rows, 128 * pp:128 * (pp + 1)] * ATT_SCALE
                qq[pp, 0:t, :] = jnp.where(head0, q2, 0)
                qq[pp, t:2 * t, :] = jnp.where(head0, 0, q2)
            acc[...] = jnp.zeros_like(acc)
            cf[...] = jnp.zeros_like(cf)
            csave[...] = jnp.zeros_like(csave)

            def tile(kb, diag):
                krows = pl.ds(pl.multiple_of(kb * t, t), t)
                for pp in range(npair):
                    cols = slice(128 * pp, 128 * (pp + 1))
                    z = _dot_nt(qq[pp], k_ref[krows, cols]) * LOG2_E
                    lb, lk = _log2_sigmoid_pair(z)
                    if diag:
                        lk = jnp.where(diag_mask, lk, 0.0)
                    carry = cf[pp]
                    a = jnp.exp2(lb + (_cumsum_mm(lk, uu) + carry))
                    if diag:
                        a = jnp.where(diag_mask, a, 0.0)
                    pv = _dot(a.astype(BF16), v_ref[krows, cols])
                    acc[pp] += jnp.where(head0, pv[0:t], pv[t:2 * t])
                    csave[pp] = jnp.where(lane2 == kb, carry, csave[pp])
                    cf[pp] = carry + jnp.sum(lk, axis=-1, keepdims=True)

            tile(i, True)

            def k_block(step, _):
                tile(i - 1 - step, False)
                return 0

            lax.fori_loop(0, i, k_block, 0)
            for pp in range(npair):
                o_ref[rows, 128 * pp:128 * (pp + 1)] = acc[pp]
                c_ref[2 * pp, rows, :] = csave[pp, 0:t, :]
                c_ref[2 * pp + 1, rows, :] = csave[pp, t:2 * t, :]
            return 0

        lax.fori_loop(0, nq, q_block, 0)

    pairs = pl.BlockSpec((s_len, 128 * npair), lambda p: (0, p))
    return pl.pallas_call(
        body, name="sb_fwd", grid=(4 // npair,),
        in_specs=[pairs, pairs, pairs, _full((2 * t, t))],
        out_specs=(pairs, pl.BlockSpec((2 * npair, s_len, 128), lambda p: (p, 0, 0))),
        out_shape=(jax.ShapeDtypeStruct((s_len, D_GRP), F32),
                   jax.ShapeDtypeStruct((8, s_len, 128), F32)),
        scratch_shapes=[pltpu.VMEM((npair, 2 * t, 128), BF16), pltpu.VMEM((npair, t, 128), F32),
                        pltpu.VMEM((npair, 2 * t, 1), F32), pltpu.VMEM((npair, 2 * t, 128), F32)],
        compiler_params=_params(),
    )(qs, ks, vs, tri_suf)


def _sb_bwd(qs, ks, vs, dos, csaved, tri_suf, tri_pre):
    s_len = qs.shape[0]
    t = SB_TILE
    nq = s_len // t

    def body(q_ref, k_ref, v_ref, do_ref, c_ref, u_ref, p_ref, dq_ref, dk_ref, dv_ref, qq, dd, cs, dq_acc, cg):
        row = lax.broadcasted_iota(jnp.int32, (2 * t, t), 0) & (t - 1)
        col = lax.broadcasted_iota(jnp.int32, (2 * t, t), 1)
        diag_mask = col < row
        lane1 = lax.broadcasted_iota(jnp.int32, (t, 128), 1)
        head0 = lane1 < HEAD_DIM
        lane2 = lax.broadcasted_iota(jnp.int32, (2 * t, 128), 1)
        uu, pp = u_ref[...], p_ref[...]
        dk_ref[...] = jnp.zeros_like(dk_ref)
        dv_ref[...] = jnp.zeros_like(dv_ref)

        def q_block(i, _):
            rows = pl.ds(pl.multiple_of(i * t, t), t)
            q2 = q_ref[rows, :] * ATT_SCALE
            do2 = do_ref[rows, :]
            qq[0:t, :] = jnp.where(head0, q2, 0)
            qq[t:2 * t, :] = jnp.where(head0, 0, q2)
            dd[0:t, :] = jnp.where(head0, do2, 0)
            dd[t:2 * t, :] = jnp.where(head0, 0, do2)
            cs[0:t, :] = c_ref[0, rows, :]
            cs[t:2 * t, :] = c_ref[1, rows, :]
            dq_acc[...] = jnp.zeros_like(dq_acc)
            cg[...] = jnp.zeros_like(cg)

            def tile(kb, diag):
                krows = pl.ds(pl.multiple_of(kb * t, t), t)
                k2, v2 = k_ref[krows, :], v_ref[krows, :]
                z = _dot_nt(qq[...], k2) * LOG2_E
                lb, lk = _log2_sigmoid_pair(z)
                beta = jnp.exp2(lb)
                if diag:
                    lk = jnp.where(diag_mask, lk, 0.0)
                cf = jnp.sum(jnp.where(lane2 == kb, cs[...], 0.0), axis=-1, keepdims=True)
                a = jnp.exp2(lb + (_cumsum_mm(lk, uu) + cf))
                if diag:
                    a = jnp.where(diag_mask, a, 0.0)
                g = a * _dot_nt(dd[...], v2)
                carry = cg[...]
                gpre = _cumsum_mm(g, pp) + carry
                cg[...] = carry + jnp.sum(g, axis=-1, keepdims=True)
                dz = g * (1.0 - beta) - gpre * beta
                if diag:
                    dz = jnp.where(diag_mask, dz, 0.0)
                dz = dz.astype(BF16)
                dq2 = _dot(dz, k2)
                dq_acc[...] += jnp.where(head0, dq2[0:t], dq2[t:2 * t])
                dk_ref[krows, :] += _dot_tn(dz, qq[...])
                dv_ref[krows, :] += _dot_tn(a.astype(BF16), dd[...])

            def k_block(kb, _):
                tile(kb, False)
                return 0

            lax.fori_loop(0, i, k_block, 0)
            tile(i, True)
            dq_ref[rows, :] = dq_acc[...] * ATT_SCALE
            return 0

        lax.fori_loop(0, nq, q_block, 0)

    pair = pl.BlockSpec((s_len, 128), lambda p: (0, p))
    out = jax.ShapeDtypeStruct((s_len, D_GRP), F32)
    return pl.pallas_call(
        body, name="sb_bwd", grid=(4,),
        in_specs=[pair, pair, pair, pair, pl.BlockSpec((2, s_len, 128), lambda p: (p, 0, 0)),
                  _full((2 * t, t)), _full((2 * t, t))],
        out_specs=(pair, pair, pair), out_shape=(out, out, out),
        scratch_shapes=[pltpu.VMEM((2 * t, 128), BF16), pltpu.VMEM((2 * t, 128), BF16),
                        pltpu.VMEM((2 * t, 128), F32), pltpu.VMEM((t, 128), F32),
                        pltpu.VMEM((2 * t, 1), F32)],
        compiler_params=_params(),
    )(qs, ks, vs, dos, csaved, tri_suf, tri_pre)


def _attn_out(o_b, lse_b, o_sb, x2, wdil, wsb, b_g):
    s_len = x2.shape[0]
    tm = ROW_TILE

    def body(o0, o1, o2, l0, l1, l2, osb_ref, x_ref, wdil_ref, wsb_ref, w_ref,
             odil_ref, lse_ref, mixed_ref, x1_ref):
        ls = (l0[...], l1[...], l2[...])
        mx = jnp.maximum(jnp.maximum(ls[0], ls[1]), ls[2])
        es = [jnp.exp(l - mx) for l in ls]
        den = es[0] + es[1] + es[2]
        o_dil = (es[0] * o0[...] + es[1] * o1[...] + es[2] * o2[...]) / den
        odil_ref[...] = o_dil
        lse_ref[...] = mx + jnp.log(den)
        halves = []
        for t, w_r in ((o_dil, wdil_ref), (osb_ref[...], wsb_ref)):
            r = lax.rsqrt(jnp.mean(t * t, axis=-1, keepdims=True) + EPS)
            halves.append(((t * r) * w_r[...]).astype(BF16))
        mixed = jnp.concatenate(halves, axis=1)
        mixed_ref[...] = mixed
        w = w_ref[...].reshape(D_MODEL, D_MODEL)
        x1_ref[...] = x_ref[...] + _dot(mixed, w)

    row = lambda w: pl.BlockSpec((tm, w), lambda i: (i, 0))
    return pl.pallas_call(
        body, name="attn_out", grid=(s_len // tm,),
        in_specs=[row(D_GRP)] * 7 + [row(D_MODEL), _full((1, D_GRP)), _full((1, D_GRP)),
                                     pl.BlockSpec((N_DEV, OUT_SHARD, D_MODEL), lambda i: (0, 3, 0))],
        out_specs=(row(D_GRP), row(D_GRP), row(D_MODEL), row(D_MODEL)),
        out_shape=(jax.ShapeDtypeStruct((s_len, D_GRP), F32), jax.ShapeDtypeStruct((s_len, D_GRP), F32),
                   jax.ShapeDtypeStruct((s_len, D_MODEL), BF16), jax.ShapeDtypeStruct((s_len, D_MODEL), F32)),
        compiler_params=_params(),
    )(*o_b, *lse_b, o_sb, x2, wdil, wsb, b_g)


def _ffn_fwd(x1, wn2, tgt, a_g, b_g):
    s_len = x1.shape[0]
    tm = ROW_TILE
    ni = s_len // tm

    def body(x_ref, wn_ref, t_ref, wg_ref, wu_ref, wd_ref, g_ref, u_ref, h2_ref, dy_ref, loss_ref, acc):
        j = pl.program_id(1)

        @pl.when(j == 0)
        def _():
            xx = x_ref[...]
            r = lax.rsqrt(jnp.mean(xx * xx, axis=-1, keepdims=True) + EPS)
            h2_ref[...] = ((xx * r) * wn_ref[...]).astype(BF16)
            acc[...] = jnp.zeros_like(acc)

        h = h2_ref[...]
        g = _dot(h, wg_ref[0])
        u = _dot(h, wu_ref[0])
        g_ref[...] = g
        u_ref[...] = u
        act = (g * (1.0 / (1.0 + jnp.exp(-g)))) * u
        acc[...] += _dot(act.astype(BF16), wd_ref[0])

        @pl.when(j == N_DEV - 1)
        def _():
            err = (x_ref[...] + acc[...]) - t_ref[...]
            dy_ref[...] = err * (1.0 / D_MODEL)
            part = 0.5 * jnp.sum(jnp.mean(err * err, axis=-1, keepdims=True))
            loss_ref[...] = jnp.full((8, 128), part, F32)

    row = pl.BlockSpec((tm, D_MODEL), lambda i, j: (i, 0))
    hid = pl.BlockSpec((tm, FF_PAD), lambda i, j: (i, j))
    return pl.pallas_call(
        body, name="ffn_fwd", grid=(ni, N_DEV),
        in_specs=[row, pl.BlockSpec((1, D_MODEL), lambda i, j: (0, 0)), row,
                  pl.BlockSpec((1, D_MODEL, FF_PAD), lambda i, j: (j, 1, 0)),
                  pl.BlockSpec((1, D_MODEL, FF_PAD), lambda i, j: (j, 2, 0)),
                  pl.BlockSpec((1, FF_PAD, D_MODEL), lambda i, j: (j, 0, 0))],
        out_specs=(hid, hid, row, row, pl.BlockSpec((8, 128), lambda i, j: (i, 0))),
        out_shape=(jax.ShapeDtypeStruct((s_len, N_DEV * FF_PAD), F32),
                   jax.ShapeDtypeStruct((s_len, N_DEV * FF_PAD), F32),
                   jax.ShapeDtypeStruct((s_len, D_MODEL), BF16),
                   jax.ShapeDtypeStruct((s_len, D_MODEL), F32),
                   jax.ShapeDtypeStruct((ni * 8, 128), F32)),
        scratch_shapes=[pltpu.VMEM((tm, D_MODEL), F32)],
        compiler_params=_params(),
    )(x1, wn2, tgt, a_g, a_g, b_g)


def _ffn_bwd_dx(dy, g, u, a_g, b_g):
    s_len = dy.shape[0]
    tm = ROW_TILE

    def body(dy_ref, g_ref, u_ref, wg_ref, wu_ref, wd_ref, dg_ref, du_ref, act_ref, dh_ref, acc):
        j = pl.program_id(1)

        @pl.when(j == 0)
        def _():
            acc[...] = jnp.zeros_like(acc)

        gg, uu = g_ref[...], u_ref[...]
        da = _dot_nt(dy_ref[...].astype(BF16), wd_ref[0])
        sig = 1.0 / (1.0 + jnp.exp(-gg))
        silu = gg * sig
        act_ref[...] = (silu * uu).astype(BF16)
        du = (da * silu).astype(BF16)
        dg = (da * uu * (sig * (1.0 + gg * (1.0 - sig)))).astype(BF16)
        du_ref[...] = du
        dg_ref[...] = dg
        acc[...] += _dot_nt(dg, wg_ref[0]) + _dot_nt(du, wu_ref[0])

        @pl.when(j == N_DEV - 1)
        def _():
            dh_ref[...] = acc[...]

    row = pl.BlockSpec((tm, D_MODEL), lambda i, j: (i, 0))
    hid = pl.BlockSpec((tm, FF_PAD), lambda i, j: (i, j))
    hid_bf = jax.ShapeDtypeStruct((s_len, N_DEV * FF_PAD), BF16)
    return pl.pallas_call(
        body, name="ffn_bwd_dx", grid=(s_len // tm, N_DEV),
        in_specs=[row, hid, hid,
                  pl.BlockSpec((1, D_MODEL, FF_PAD), lambda i, j: (j, 1, 0)),
                  pl.BlockSpec((1, D_MODEL, FF_PAD), lambda i, j: (j, 2, 0)),
                  pl.BlockSpec((1, FF_PAD, D_MODEL), lambda i, j: (j, 0, 0))],
        out_specs=(hid, hid, hid, row),
        out_shape=(hid_bf, hid_bf, hid_bf, jax.ShapeDtypeStruct((s_len, D_MODEL), F32)),
        scratch_shapes=[pltpu.VMEM((tm, D_MODEL), F32)],
        compiler_params=_params(),
    )(dy, g, u, a_g, a_g, b_g)


def _ffn_bwd_dw(h2, dy, dg, du, act):
    s_len = h2.shape[0]
    tm = ROW_TILE
    ni = s_len // tm

    def body(h_ref, dy_ref, dg_ref, du_ref, act_ref, dwg_ref, dwu_ref, dwd_ref, ag, au, ad):
        i = pl.program_id(1)

        @pl.when(i == 0)
        def _():
            ag[...] = jnp.zeros_like(ag)
            au[...] = jnp.zeros_like(au)
            ad[...] = jnp.zeros_like(ad)

        h = h_ref[...]
        ag[...] += _dot_tn(h, dg_ref[...])
        au[...] += _dot_tn(h, du_ref[...])
        ad[...] += _dot_tn(act_ref[...], dy_ref[...].astype(BF16))

        @pl.when(i == ni - 1)
        def _():
            dwg_ref[0] = ag[...].astype(BF16)
            dwu_ref[0] = au[...].astype(BF16)
            dwd_ref[0] = ad[...].astype(BF16)

    row = pl.BlockSpec((tm, D_MODEL), lambda j, i: (i, 0))
    hid = pl.BlockSpec((tm, FF_PAD), lambda j, i: (i, j))
    col_w = pl.BlockSpec((1, D_MODEL, FF_PAD), lambda j, i: (j, 0, 0))
    row_w = pl.BlockSpec((1, FF_PAD, D_MODEL), lambda j, i: (j, 0, 0))
    return pl.pallas_call(
        body, name="ffn_bwd_dw", grid=(N_DEV, ni),
        in_specs=[row, row, hid, hid, hid], out_specs=(col_w, col_w, row_w),
        out_shape=(jax.ShapeDtypeStruct((N_DEV, D_MODEL, FF_PAD), BF16),
                   jax.ShapeDtypeStruct((N_DEV, D_MODEL, FF_PAD), BF16),
                   jax.ShapeDtypeStruct((N_DEV, FF_PAD, D_MODEL), BF16)),
        scratch_shapes=[pltpu.VMEM((D_MODEL, FF_PAD), F32), pltpu.VMEM((D_MODEL, FF_PAD), F32),
                        pltpu.VMEM((FF_PAD, D_MODEL), F32)],
        compiler_params=_params(),
    )(h2, dy, dg, du, act)


def _rms_bwd(dy, t, w):
    r = lax.rsqrt(jnp.mean(t * t, axis=-1, keepdims=True) + EPS)
    gw = dy * w
    dt = r * (gw - t * ((r * r) * jnp.mean(gw * t, axis=-1, keepdims=True)))
    return dt, dy * t * r


def _attn_out_bwd(dy, dh2, x1, wn2, b_g, mixed, o_dil, o_sb, wdil, wsb, bd512):
    s_len = dy.shape[0]
    tm = ROW_TILE
    ni = s_len // tm

    def body(dy_ref, dh_ref, x1_ref, wn_ref, w_ref, mixed_ref, odil_ref, osb_ref, wdil_ref, wsb_ref, bd_ref,
             dx1_ref, dodil_ref, delta_ref, dosb_ref, dwout_ref, dwn_ref, dwdil_ref, dwsb_ref, wacc):
        i = pl.program_id(0)

        @pl.when(i == 0)
        def _():
            wacc[...] = jnp.zeros_like(wacc)
            dwn_ref[...] = jnp.zeros_like(dwn_ref)
            dwdil_ref[...] = jnp.zeros_like(dwdil_ref)
            dwsb_ref[...] = jnp.zeros_like(dwsb_ref)

        dnorm, dw_rows = _rms_bwd(dh_ref[...], x1_ref[...], wn_ref[...])
        dx1 = dy_ref[...] + dnorm
        dx1_ref[...] = dx1
        dwn_ref[...] += jnp.sum(dw_rows, axis=0, keepdims=True)
        dx1b = dx1.astype(BF16)
        w = w_ref[...].reshape(D_MODEL, D_MODEL)
        dmixed = _dot_nt(dx1b, w)
        wacc[...] += _dot_tn(mixed_ref[...], dx1b)
        o_dil = odil_ref[...]
        d_odil, dw_rows = _rms_bwd(dmixed[:, :D_GRP], o_dil, wdil_ref[...])
        dwdil_ref[...] += jnp.sum(dw_rows, axis=0, keepdims=True)
        dodil_ref[...] = d_odil.astype(BF16)
        delta_ref[...] = _mm_split(d_odil * o_dil, bd_ref[...])
        d_osb, dw_rows = _rms_bwd(dmixed[:, D_GRP:], osb_ref[...], wsb_ref[...])
        dwsb_ref[...] += jnp.sum(dw_rows, axis=0, keepdims=True)
        dosb_ref[...] = d_osb.astype(BF16)

        @pl.when(i == ni - 1)
        def _():
            dwout_ref[...] = wacc[...].astype(BF16).reshape(N_DEV, OUT_SHARD, D_MODEL)

    row = lambda w: pl.BlockSpec((tm, w), lambda i: (i, 0))
    return pl.pallas_call(
        body, name="attn_out_bwd", grid=(ni,),
        in_specs=[row(D_MODEL), row(D_MODEL), row(D_MODEL), _full((1, D_MODEL)),
                  pl.BlockSpec((N_DEV, OUT_SHARD, D_MODEL), lambda i: (0, 3, 0)),
                  row(D_MODEL), row(D_GRP), row(D_GRP), _full((1, D_GRP)), _full((1, D_GRP)),
                  _full((D_GRP, D_GRP))],
        out_specs=(row(D_MODEL), row(D_GRP), row(D_GRP), row(D_GRP),
                   _full((N_DEV, OUT_SHARD, D_MODEL)), _full((1, D_MODEL)), _full((1, D_GRP)), _full((1, D_GRP))),
        out_shape=(jax.ShapeDtypeStruct((s_len, D_MODEL), F32), jax.ShapeDtypeStruct((s_len, D_GRP), BF16),
                   jax.ShapeDtypeStruct((s_len, D_GRP), F32), jax.ShapeDtypeStruct((s_len, D_GRP), BF16),
                   jax.ShapeDtypeStruct((N_DEV, OUT_SHARD, D_MODEL), BF16),
                   jax.ShapeDtypeStruct((1, D_MODEL), F32), jax.ShapeDtypeStruct((1, D_GRP), F32),
                   jax.ShapeDtypeStruct((1, D_GRP), F32)),
        scratch_shapes=[pltpu.VMEM((D_MODEL, D_MODEL), F32)],
        compiler_params=_params(),
    )(dy, dh2, x1, wn2, b_g, mixed, o_dil, o_sb, wdil, wsb, bd512)


def _qkv_bwd(dq_b, dk_b, dv_b, dqs, dks, dvs, qraw, kraw, cos2, sin2, qnw, knw, bd):
    s_len = qraw.shape[0]
    tm = ROW_TILE
    ni = s_len // tm

    def body(dq0, dq1, dq2, dk0, dk1, dk2, dv0, dv1, dv2, dqs_ref, dks_ref, dvs_ref,
             qraw_ref, kraw_ref, cos_ref, sin_ref, qnw_ref, knw_ref, bd_ref,
             dproj_ref, dqn_ref, dkn_ref):
        i = pl.program_id(0)

        @pl.when(i == 0)
        def _():
            dqn_ref[...] = jnp.zeros_like(dqn_ref)
            dkn_ref[...] = jnp.zeros_like(dkn_ref)

        cos_t, sin_t, bdm = cos_ref[...], sin_ref[...], bd_ref[...]
        for grp, (parts, raw_ref, nw_ref, dn_ref) in enumerate((((dq0, dq1, dq2), qraw_ref, qnw_ref, dqn_ref),
                                                                ((dk0, dk1, dk2), kraw_ref, knw_ref, dkn_ref))):
            dn_acc = 0.0
            for p in range(4):
                cols = slice(128 * p, 128 * (p + 1))
                d_rope = parts[0][:, cols] + parts[1][:, cols] + parts[2][:, cols]
                d_norm = d_rope * cos_t + _swap_halves(d_rope * sin_t)
                t = raw_ref[:, cols]
                w = nw_ref[...]
                r = lax.rsqrt(_mm_split(t * t, bdm) * (1.0 / HEAD_DIM) + EPS)
                gw = d_norm * w
                corr = _mm_split(gw * t, bdm) * (1.0 / HEAD_DIM)
                dt = r * (gw - t * ((r * r) * corr))
                dn_acc = dn_acc + jnp.sum(d_norm * t * r, axis=0, keepdims=True)
                dproj_ref[:, D_GRP * grp + 128 * p:D_GRP * grp + 128 * (p + 1)] = dt.astype(BF16)
            dn_ref[...] += dn_acc
        dproj_ref[:, 2 * D_GRP:3 * D_GRP] = (dv0[...] + dv1[...] + dv2[...]).astype(BF16)
        dproj_ref[:, 3 * D_GRP:4 * D_GRP] = dqs_ref[...].astype(BF16)
        dproj_ref[:, 4 * D_GRP:5 * D_GRP] = dks_ref[...].astype(BF16)
        dproj_ref[:, 5 * D_GRP:6 * D_GRP] = dvs_ref[...].astype(BF16)

    row = lambda w: pl.BlockSpec((tm, w), lambda i: (i, 0))
    return pl.pallas_call(
        body, name="qkv_bwd", grid=(ni,),
        in_specs=[row(D_GRP)] * 14 + [row(128), row(128), _full((1, 128)), _full((1, 128)), _full((128, 128))],
        out_specs=(row(D_IN), _full((1, 128)), _full((1, 128))),
        out_shape=(jax.ShapeDtypeStruct((s_len, D_IN), BF16), jax.ShapeDtypeStruct((1, 128), F32),
                   jax.ShapeDtypeStruct((1, 128), F32)),
        compiler_params=_params(),
    )(*dq_b, *dk_b, *dv_b, dqs, dks, dvs, qraw, kraw, cos2, sin2, qnw, knw, bd)


def _in_bwd_dx(dproj, a_g, x2, dx1, wn1):
    s_len = x2.shape[0]
    tm = ROW_TILE
    ni = s_len // tm

    def body(dp_ref, w_ref, x_ref, dx1_ref, wn_ref, gx_ref, dwn_ref):
        i = pl.program_id(0)

        @pl.when(i == 0)
        def _():
            dwn_ref[...] = jnp.zeros_like(dwn_ref)

        dh = 0.0
        for d in range(N_DEV):
            dh = dh + _dot_nt(dp_ref[:, IN_SHARD * d:IN_SHARD * (d + 1)], w_ref[d])
        dnorm, dw_rows = _rms_bwd(dh, x_ref[...], wn_ref[...])
        gx_ref[...] = dx1_ref[...] + dnorm
        dwn_ref[...] += jnp.sum(dw_rows, axis=0, keepdims=True)

    row = lambda w: pl.BlockSpec((tm, w), lambda i: (i, 0))
    return pl.pallas_call(
        body, name="in_bwd_dx", grid=(ni,),
        in_specs=[row(D_IN), pl.BlockSpec((N_DEV, D_MODEL, IN_SHARD), lambda i: (0, 0, 0)),
                  row(D_MODEL), row(D_MODEL), _full((1, D_MODEL))],
        out_specs=(row(D_MODEL), _full((1, D_MODEL))),
        out_shape=(jax.ShapeDtypeStruct((s_len, D_MODEL), F32), jax.ShapeDtypeStruct((1, D_MODEL), F32)),
        compiler_params=_params(),
    )(dproj, a_g, x2, dx1, wn1)


def _in_bwd_dw(h1, dproj):
    s_len = h1.shape[0]
    tm = ROW_TILE
    ni = s_len // tm

    def body(h_ref, dp_ref, dw_ref, acc):
        i = pl.program_id(1)

        @pl.when(i == 0)
        def _():
            acc[...] = jnp.zeros_like(acc)

        acc[...] += _dot_tn(h_ref[...], dp_ref[...])

        @pl.when(i == ni - 1)
        def _():
            dw_ref[0] = acc[...].astype(BF16)

    return pl.pallas_call(
        body, name="in_bwd_dw", grid=(N_DEV, ni),
        in_specs=[pl.BlockSpec((tm, D_MODEL), lambda d, i: (i, 0)),
                  pl.BlockSpec((tm, IN_SHARD), lambda d, i: (i, d))],
        out_specs=pl.BlockSpec((1, D_MODEL, IN_SHARD), lambda d, i: (d, 0, 0)),
        out_shape=jax.ShapeDtypeStruct((N_DEV, D_MODEL, IN_SHARD), BF16),
        scratch_shapes=[pltpu.VMEM((D_MODEL, IN_SHARD), F32)],
        compiler_params=_params(),
    )(h1, dproj)


def _adamw(recv, w, m, v):
    rows, cols = w.shape
    tr = 128 if rows % 128 == 0 else rows

    def body(p_ref, w_ref, m_ref, v_ref, g_ref, d_ref, nm_ref, nv_ref):
        g = p_ref[0].astype(F32)
        for s in range(1, N_DEV):
            g = g + p_ref[s].astype(F32)
        m_new = ADAM_B1 * m_ref[...] + (1.0 - ADAM_B1) * g
        v_new = ADAM_B2 * v_ref[...] + (1.0 - ADAM_B2) * (g * g)
        m_hat = m_new / (1.0 - ADAM_B1 ** ADAM_STEP)
        v_hat = v_new / (1.0 - ADAM_B2 ** ADAM_STEP)
        g_ref[...] = g
        d_ref[...] = -ADAM_LR * (m_hat / (jnp.sqrt(v_hat) + ADAM_EPS) + ADAM_WD * w_ref[...])
        nm_ref[...] = m_new
        nv_ref[...] = v_new

    blk = pl.BlockSpec((tr, cols), lambda i: (i, 0))
    out = jax.ShapeDtypeStruct((rows, cols), F32)
    return pl.pallas_call(
        body, name=f"adamw_{rows}x{cols}", grid=(rows // tr,),
        in_specs=[pl.BlockSpec((N_DEV, tr, cols), lambda i: (0, i, 0)), blk, blk, blk],
        out_specs=(blk,) * 4, out_shape=(out,) * 4,
        compiler_params=_params(),
    )(recv, w, m, v)


def _rope_tables(s_len):
    pos = jnp.arange(s_len, dtype=F32)
    inv_freq = ROPE_THETA ** (-jnp.arange(0, HEAD_DIM, 2, dtype=F32) / HEAD_DIM)
    ang = pos[:, None] * inv_freq[None, :]
    cos, sin = jnp.cos(ang), jnp.sin(ang)
    cos2 = jnp.concatenate([cos, cos, cos, cos], axis=1)
    sin2 = jnp.concatenate([-sin, sin, -sin, sin], axis=1)
    return cos2, sin2


def _block_diag_ones(n):
    i = jnp.arange(n)
    return (i[:, None] // HEAD_DIM == i[None, :] // HEAD_DIM).astype(BF16)


def _pad_cols(t):
    return jnp.pad(t, ((0, 0), (0, FF_PAD - FF_SHARD)))


def _pad_rows(t):
    return jnp.pad(t, ((0, FF_PAD - FF_SHARD), (0, 0)))


def _pack_small(n1, n2, ndil, nsb, nq, nk):
    pad = lambda t: jnp.pad(t.reshape(1, HEAD_DIM), ((0, 0), (0, 128 - HEAD_DIM)))
    rows = [n1.reshape(8, 128), n2.reshape(8, 128), ndil.reshape(4, 128), nsb.reshape(4, 128),
            pad(nq), pad(nk), jnp.zeros((6, 128), F32)]
    return jnp.concatenate(rows, axis=0)


def _unpack_small(t):
    return (t[0:8].reshape(1, D_MODEL), t[8:16].reshape(1, D_MODEL), t[16:20].reshape(1, D_GRP),
            t[20:24].reshape(1, D_GRP), t[24:25, :HEAD_DIM], t[25:26, :HEAD_DIM])


def kernel(x, attn_norm_w, w_in, q_norm_w, k_norm_w, dil_out_norm_w, sb_out_norm_w, w_out, ffn_norm_w, w_gate, w_up, w_down, loss_target, m_attn_norm_w, m_w_in, m_q_norm_w, m_k_norm_w, m_dil_out_norm_w, m_sb_out_norm_w, m_w_out, m_ffn_norm_w, m_w_gate, m_w_up, m_w_down, v_attn_norm_w, v_w_in, v_q_norm_w, v_k_norm_w, v_dil_out_norm_w, v_sb_out_norm_w, v_w_out, v_ffn_norm_w, v_w_gate, v_w_up, v_w_down):
    s_len = x.shape[1]
    x2, tgt = x[0], loss_target[0]

    a_loc = jnp.concatenate([w_in[0], _pad_cols(w_gate[0]), _pad_cols(w_up[0])], axis=0).astype(BF16)
    b_loc = jnp.concatenate([_pad_rows(w_down[0]), w_out[0]], axis=0).astype(BF16)
    a_g, b_g = _gather_weights(a_loc, b_loc)

    cos2, sin2 = _rope_tables(s_len)
    bd128, bd512 = _block_diag_ones(128), _block_diag_ones(D_GRP)
    idx = jnp.arange(SB_TILE)
    tri_suf = (idx[:, None] > idx[None, :]).astype(BF16)
    tri_pre = (idx[:, None] < idx[None, :]).astype(BF16)
    tri_suf = jnp.concatenate([tri_suf, tri_suf], axis=0)
    tri_pre = jnp.concatenate([tri_pre, tri_pre], axis=0)
    qnw2 = jnp.concatenate([q_norm_w, q_norm_w], axis=1)
    knw2 = jnp.concatenate([k_norm_w, k_norm_w], axis=1)

    h1, qraw, kraw, q, k, va, qs, ks, vs = _attn_in(x2, attn_norm_w, a_g, cos2, sin2, qnw2, knw2, bd128)
    view = lambda t, r: t.reshape(s_len // r, r * D_GRP)
    unview = lambda t: t.reshape(s_len, D_GRP)
    o_b, lse_b = [], []
    for r in DILATIONS:
        o, lse = _dil_fwd(view(q, r), view(k, r), view(va, r), r)
        o_b.append(unview(o))
        lse_b.append(unview(lse))
    o_sb, c_sb = _sb_fwd(qs, ks, vs, tri_suf)
    o_dil, lse_tot, mixed, x1 = _attn_out(o_b, lse_b, o_sb, x2, dil_out_norm_w, sb_out_norm_w, b_g)
    g, u, h2, dy, loss_parts = _ffn_fwd(x1, ffn_norm_w, tgt, a_g, b_g)
    loss = lax.psum(jnp.sum(loss_parts[::8, 0]), ("x", "y", "c"))

    dg, du, act, dh2 = _ffn_bwd_dx(dy, g, u, a_g, b_g)
    dwg, dwu, dwd = _ffn_bwd_dw(h2, dy, dg, du, act)
    dx1, do_dil, delta, do_sb, dwout, dn2, dndil, dnsb = _attn_out_bwd(
        dy, dh2, x1, ffn_norm_w, b_g, mixed, o_dil, o_sb, dil_out_norm_w, sb_out_norm_w, bd512)
    dqs, dks, dvs = _sb_bwd(qs, ks, vs, do_sb, c_sb, tri_suf, tri_pre)
    dq_b, dk_b, dv_b = [], [], []
    for r in DILATIONS:
        dq, dk, dv = _dil_bwd(view(q, r), view(k, r), view(va, r), view(do_dil, r), view(lse_tot, r),
                              view(delta, r), r)
        dq_b.append(unview(dq))
        dk_b.append(unview(dk))
        dv_b.append(unview(dv))
    dproj, dqn2, dkn2 = _qkv_bwd(dq_b, dk_b, dv_b, dqs, dks, dvs, qraw, kraw, cos2, sin2, qnw2, knw2, bd128)
    grad_x, dn1 = _in_bwd_dx(dproj, a_g, x2, dx1, attn_norm_w)
    dwin = _in_bwd_dw(h1, dproj)
    dqn = dqn2[:, :HEAD_DIM] + dqn2[:, HEAD_DIM:]
    dkn = dkn2[:, :HEAD_DIM] + dkn2[:, HEAD_DIM:]

    small = _pack_small(dn1, dn2, dndil, dnsb, dqn, dkn)
    r_in, r_gate, r_up, r_down, r_out, r_small = _exchange_grads([dwin, dwg, dwu, dwd, dwout], small)
    big = {
        "w_in": _adamw(r_in, w_in[0], m_w_in[0], v_w_in[0]),
        "w_gate": tuple(t[:, :FF_SHARD] for t in _adamw(r_gate, _pad_cols(w_gate[0]), _pad_cols(m_w_gate[0]), _pad_cols(v_w_gate[0]))),
        "w_up": tuple(t[:, :FF_SHARD] for t in _adamw(r_up, _pad_cols(w_up[0]), _pad_cols(m_w_up[0]), _pad_cols(v_w_up[0]))),
        "w_down": tuple(t[:FF_SHARD] for t in _adamw(r_down, _pad_rows(w_down[0]), _pad_rows(m_w_down[0]), _pad_rows(v_w_down[0]))),
        "w_out": _adamw(r_out, w_out[0], m_w_out[0], v_w_out[0]),
    }
    packs = [_pack_small(*ts) for ts in (
        (attn_norm_w, ffn_norm_w, dil_out_norm_w, sb_out_norm_w, q_norm_w, k_norm_w),
        (m_attn_norm_w, m_ffn_norm_w, m_dil_out_norm_w, m_sb_out_norm_w, m_q_norm_w, m_k_norm_w),
        (v_attn_norm_w, v_ffn_norm_w, v_dil_out_norm_w, v_sb_out_norm_w, v_q_norm_w, v_k_norm_w))]
    small_out = [_unpack_small(t) for t in _adamw(r_small, *packs)]
    names = ["attn_norm_w", "w_in", "q_norm_w", "k_norm_w", "dil_out_norm_w", "sb_out_norm_w", "w_out",
             "ffn_norm_w", "w_gate", "w_up", "w_down"]
    small_pos = {"attn_norm_w": 0, "ffn_norm_w": 1, "dil_out_norm_w": 2, "sb_out_norm_w": 3,
                 "q_norm_w": 4, "k_norm_w": 5}
    outs = [loss, grad_x[None]]
    for kind in range(4):
        for name in names:
            if name in small_pos:
                outs.append(small_out[kind][small_pos[name]])
            else:
                outs.append(big[name][kind][None])
    return tuple(outs)
```

```python
import functools

import jax
import jax.numpy as jnp
from jax import lax
from jax.experimental import pallas as pl
from jax.experimental.pallas import tpu as pltpu

F32 = jnp.float32
BF16 = jnp.bfloat16

N_DEV = 8
D_MODEL = 1024
HEAD_DIM = 64
D_GRP = 512
D_IN = 6 * D_GRP
IN_SHARD = D_IN // N_DEV
FF_SHARD = 352
FF_PAD = 384
OUT_SHARD = D_MODEL // N_DEV
BLOCK = 128
DILATIONS = (1, 4, 16)
ROPE_THETA = 10000.0
EPS = 1e-6
ATT_SCALE = HEAD_DIM ** -0.5
NEG = -1e30

ADAM_LR = 0.001
ADAM_B1 = 0.9
ADAM_B2 = 0.999
ADAM_EPS = 1e-08
ADAM_WD = 0.01
ADAM_STEP = 10

SB_TILE = 256
SB_PAIRS = 4
SB_BWD_PAIRS = 2
ROW_TILE = 512
VMEM_LIMIT = 56 * 1024 * 1024
MESH = pl.DeviceIdType.MESH


def _dot(a, b):
    return jnp.dot(a, b, preferred_element_type=F32)


def _dot_nt(a, b):
    return lax.dot_general(a, b, (((1,), (1,)), ((), ())), preferred_element_type=F32)


def _dot_tn(a, b):
    return lax.dot_general(a, b, (((0,), (0,)), ((), ())), preferred_element_type=F32)


def _mm_split(t, m):
    hi = t.astype(BF16)
    lo = (t - hi.astype(F32)).astype(BF16)
    return _dot(hi, m) + _dot(lo, m)


def _params(**kw):
    return pltpu.CompilerParams(vmem_limit_bytes=VMEM_LIMIT, **kw)


def _full(shape):
    nd = len(shape)
    return pl.BlockSpec(shape, lambda *_: (0,) * nd)


def _swap_halves(t):
    lane = lax.broadcasted_iota(jnp.int32, t.shape, 1)
    first = (lane & 32) == 0
    return jnp.where(first, pltpu.roll(t, 96, 1), pltpu.roll(t, 32, 1))


def _log_sigmoid(z):
    return jnp.minimum(z, 0.0) - jnp.log(1.0 + jnp.exp(-jnp.abs(z)))


def _log_sigmoid_pair(z):
    neg_abs = lax.bitcast_convert_type(lax.bitcast_convert_type(z, jnp.uint32) | jnp.uint32(0x80000000), F32)
    lb = jnp.minimum(z, 0.0) - jnp.log(1.0 + jnp.exp(neg_abs))
    return lb, lb - z


def _cumsum_mm(t, tri2):
    hi = lax.bitcast_convert_type(lax.bitcast_convert_type(t, jnp.uint32) & jnp.uint32(0xFFFF0000), F32)
    lhs = jnp.concatenate([hi.astype(BF16), (t - hi).astype(BF16)], axis=1)
    return _dot(lhs, tri2)


def _mesh_pos():
    return lax.axis_index("x"), lax.axis_index("y"), lax.axis_index("c")


def _flat_index(p):
    return 4 * p[0] + 2 * p[1] + p[2]


def _gather_weights(a, b):
    def body(a_ref, b_ref, ag_ref, bg_ref, send_sems, recv_sems, local_sems):
        x, y, c = _mesh_pos()
        me, sibling = (x, y, c), (x, y, 1 - c)
        chips = [(1 - x, y), (x, 1 - y), (1 - x, 1 - y)]
        srcs, outs = (a_ref, b_ref), (ag_ref, bg_ref)

        def copy(arr, k, block, to, own=False):
            dst = outs[arr].at[_flat_index(block)]
            return pltpu.make_async_remote_copy(
                src_ref=srcs[arr] if own else dst, dst_ref=dst,
                send_sem=send_sems.at[arr, k], recv_sem=recv_sems.at[arr, k],
                device_id=to, device_id_type=MESH)

        for arr in range(2):
            mine = pltpu.make_async_copy(srcs[arr], outs[arr].at[_flat_index(me)], local_sems.at[arr])
            mine.start()
            first = [copy(arr, 0, me, sibling, own=True)]
            first += [copy(arr, 1 + j, me, (*chip, c), own=True) for j, chip in enumerate(chips)]
            for cp in first:
                cp.start()
        for arr in range(2):
            passed = [copy(arr, 4 + j, (*chip, c), sibling) for j, chip in enumerate(chips)]
            for j, chip in enumerate(chips):
                copy(arr, 1 + j, (*chip, c), me).wait_recv()
                passed[j].start()
        for arr in range(2):
            copy(arr, 0, sibling, me).wait_recv()
            for j, chip in enumerate(chips):
                copy(arr, 4 + j, (*chip, 1 - c), me).wait_recv()
            for k in range(7):
                copy(arr, k, me, me).wait_send()
            pltpu.make_async_copy(srcs[arr], outs[arr].at[_flat_index(me)], local_sems.at[arr]).wait()

    any_spec = pl.BlockSpec(memory_space=pl.ANY)
    return pl.pallas_call(
        body, name="gather_weights",
        out_shape=(jax.ShapeDtypeStruct((N_DEV,) + a.shape, a.dtype),
                   jax.ShapeDtypeStruct((N_DEV,) + b.shape, b.dtype)),
        in_specs=[any_spec, any_spec], out_specs=(any_spec, any_spec),
        scratch_shapes=[pltpu.SemaphoreType.DMA((2, 7)), pltpu.SemaphoreType.DMA((2, 7)),
                        pltpu.SemaphoreType.DMA((2,))],
        compiler_params=pltpu.CompilerParams(has_side_effects=True),
    )(a, b)


def _exchange_grads(parts, small):
    n_arr = len(parts)

    def body(*refs):
        ins, outs = refs[:n_arr + 1], refs[n_arr + 1:2 * (n_arr + 1)]
        send_sems, recv_sems, local_sems = refs[2 * (n_arr + 1):]
        x, y, c = _mesh_pos()
        me = (x, y, c)
        my_idx = _flat_index(me)
        peers = []
        for m in range(1, N_DEV):
            peers.append((1 - x if m & 4 else x, 1 - y if m & 2 else y, 1 - c if m & 1 else c))

        def src_block(arr, dev):
            return ins[arr] if arr == n_arr else ins[arr].at[_flat_index(dev)]

        def copy(arr, k):
            return pltpu.make_async_remote_copy(
                src_ref=src_block(arr, peers[k]), dst_ref=outs[arr].at[my_idx],
                send_sem=send_sems.at[arr, k], recv_sem=recv_sems.at[arr, k],
                device_id=peers[k], device_id_type=MESH)

        def local(arr):
            return pltpu.make_async_copy(src_block(arr, me), outs[arr].at[my_idx], local_sems.at[arr])

        for arr in range(n_arr + 1):
            local(arr).start()
            for k in range(N_DEV - 1):
                copy(arr, k).start()
        for arr in range(n_arr + 1):
            for k in range(N_DEV - 1):
                cp = copy(arr, k)
                cp.wait_send()
                cp.wait_recv()
            local(arr).wait()

    any_spec = pl.BlockSpec(memory_space=pl.ANY)
    out_shape = tuple(jax.ShapeDtypeStruct(p.shape, p.dtype) for p in parts)
    out_shape += (jax.ShapeDtypeStruct((N_DEV,) + small.shape, small.dtype),)
    return pl.pallas_call(
        body, name="exchange_grads",
        out_shape=out_shape,
        in_specs=[any_spec] * (n_arr + 1), out_specs=(any_spec,) * (n_arr + 1),
        scratch_shapes=[pltpu.SemaphoreType.DMA((n_arr + 1, N_DEV - 1)),
                        pltpu.SemaphoreType.DMA((n_arr + 1, N_DEV - 1)),
                        pltpu.SemaphoreType.DMA((n_arr + 1,))],
        compiler_params=pltpu.CompilerParams(has_side_effects=True),
    )(*parts, small)


def _head_norm(t, w128, bd):
    ms = _mm_split(t * t, bd) * (1.0 / HEAD_DIM)
    r = lax.rsqrt(ms + EPS)
    return (t * r) * w128, r


def _attn_in(x2, wn1, a_g, cos2, sin2, qnw, knw, bd):
    s_len = x2.shape[0]
    tm = ROW_TILE

    def body(x_ref, wn_ref, w_ref, cos_ref, sin_ref, qnw_ref, knw_ref, bd_ref,
             h1_ref, qraw_ref, kraw_ref, q_ref, k_ref, va_ref, qs_ref, ks_ref, vs_ref, proj):
        xx = x_ref[...]
        r = lax.rsqrt(jnp.mean(xx * xx, axis=-1, keepdims=True) + EPS)
        h = ((xx * r) * wn_ref[...]).astype(BF16)
        h1_ref[...] = h
        for d in range(N_DEV):
            proj[:, IN_SHARD * d:IN_SHARD * (d + 1)] = _dot(h, w_ref[d])
        cos_t, sin_t, bdm = cos_ref[...], sin_ref[...], bd_ref[...]
        for grp, (raw_ref, rope_ref, nw_ref) in enumerate(((qraw_ref, q_ref, qnw_ref),
                                                           (kraw_ref, k_ref, knw_ref))):
            for p in range(4):
                cols = slice(D_GRP * grp + 128 * p, D_GRP * grp + 128 * (p + 1))
                t = proj[:, cols]
                raw_ref[:, 128 * p:128 * (p + 1)] = t
                yn, _ = _head_norm(t, nw_ref[...], bdm)
                rope_ref[:, 128 * p:128 * (p + 1)] = (yn * cos_t + _swap_halves(yn) * sin_t).astype(BF16)
        for grp, ref in ((2, va_ref), (3, qs_ref), (4, ks_ref), (5, vs_ref)):
            ref[...] = proj[:, D_GRP * grp:D_GRP * (grp + 1)].astype(BF16)

    row = lambda w: pl.BlockSpec((tm, w), lambda i: (i, 0))
    grp_bf = jax.ShapeDtypeStruct((s_len, D_GRP), BF16)
    grp_f32 = jax.ShapeDtypeStruct((s_len, D_GRP), F32)
    return pl.pallas_call(
        body, name="attn_in", grid=(s_len // tm,),
        in_specs=[row(D_MODEL), _full((1, D_MODEL)),
                  pl.BlockSpec((N_DEV, D_MODEL, IN_SHARD), lambda i: (0, 0, 0)),
                  row(128), row(128), _full((1, 128)), _full((1, 128)), _full((128, 128))],
        out_specs=(row(D_MODEL),) + (row(D_GRP),) * 8,
        out_shape=(jax.ShapeDtypeStruct((s_len, D_MODEL), BF16), grp_f32, grp_f32) + (grp_bf,) * 6,
        scratch_shapes=[pltpu.VMEM((tm, D_IN), F32)],
        compiler_params=_params(),
    )(x2, wn1, a_g, cos2, sin2, qnw, knw, bd)


def _band_mask(n):
    i = lax.broadcasted_iota(jnp.int32, (BLOCK, 2 * BLOCK), 0)
    j = lax.broadcasted_iota(jnp.int32, (BLOCK, 2 * BLOCK), 1)
    dist = i + BLOCK - j
    return (dist >= 0) & (dist <= BLOCK) & ((n - 1) * BLOCK + j >= 0)


def _dil_fwd(qv, kv, vv, r):
    sub_len = qv.shape[0]
    nb = sub_len // BLOCK

    def body(q_ref, kp_ref, kc_ref, vp_ref, vc_ref, o_ref, lse_ref):
        n = pl.program_id(1)
        valid = _band_mask(n)
        lane = lax.broadcasted_iota(jnp.int32, (BLOCK, 128), 1)
        head0 = lane < HEAD_DIM
        for p in range(4):
            cols = slice(128 * p, 128 * (p + 1))
            q2 = q_ref[:, cols]
            kk = jnp.concatenate([kp_ref[:, cols], kc_ref[:, cols]], axis=0)
            vv2 = jnp.concatenate([vp_ref[:, cols], vc_ref[:, cols]], axis=0)
            res = []
            for h in range(2):
                qh = jnp.where(head0, q2, 0) if h == 0 else jnp.where(head0, 0, q2)
                s = jnp.where(valid, _dot_nt(qh, kk) * ATT_SCALE, NEG)
                m = jnp.max(s, axis=-1, keepdims=True)
                pr = jnp.exp(s - m)
                den = jnp.sum(pr, axis=-1, keepdims=True)
                o = _dot(pr.astype(BF16), vv2) / den
                res.append((o, m + jnp.log(den)))
            o_ref[:, cols] = jnp.where(head0, res[0][0], res[1][0])
            lse_ref[:, cols] = jnp.where(head0, res[0][1], res[1][1])

    cur = pl.BlockSpec((BLOCK, D_GRP), lambda c, n: (n, c))
    prev = pl.BlockSpec((BLOCK, D_GRP), lambda c, n: (jnp.maximum(n - 1, 0), c))
    out = jax.ShapeDtypeStruct(qv.shape, F32)
    return pl.pallas_call(
        body, name=f"dil_fwd_r{r}", grid=(r, nb),
        in_specs=[cur, prev, cur, prev, cur], out_specs=(cur, cur), out_shape=(out, out),
        compiler_params=_params(),
    )(qv, kv, kv, vv, vv)


def _dil_bwd(qv, kv, vv, dov, lsev, deltav, r):
    sub_len = qv.shape[0]
    nb = sub_len // BLOCK

    def body(q_ref, kp_ref, kc_ref, vp_ref, vc_ref, do_ref, lse_ref, dl_ref,
             dq_ref, dk_ref, dv_ref, dk_carry, dv_carry):
        n = pl.program_id(1)

        @pl.when(n == 0)
        def _():
            dk_carry[...] = jnp.zeros_like(dk_carry)
            dv_carry[...] = jnp.zeros_like(dv_carry)

        @pl.when(n < nb)
        def _():
            valid = _band_mask(n)
            lane = lax.broadcasted_iota(jnp.int32, (BLOCK, 128), 1)
            head0 = lane < HEAD_DIM
            for p in range(4):
                cols = slice(128 * p, 128 * (p + 1))
                q2, do2 = q_ref[:, cols], do_ref[:, cols]
                lse2, dl2 = lse_ref[:, cols], dl_ref[:, cols]
                kk = jnp.concatenate([kp_ref[:, cols], kc_ref[:, cols]], axis=0)
                vv2 = jnp.concatenate([vp_ref[:, cols], vc_ref[:, cols]], axis=0)
                dq_h, dkk, dvv = [], 0.0, 0.0
                for h in range(2):
                    sel = (lambda t: jnp.where(head0, t, 0)) if h == 0 else (lambda t: jnp.where(head0, 0, t))
                    qh, doh = sel(q2), sel(do2)
                    one_lane = lane == (0 if h == 0 else HEAD_DIM)
                    lse = jnp.sum(jnp.where(one_lane, lse2, 0.0), axis=-1, keepdims=True)
                    dl = jnp.sum(jnp.where(one_lane, dl2, 0.0), axis=-1, keepdims=True)
                    s = _dot_nt(qh, kk) * ATT_SCALE
                    pr = jnp.where(valid, jnp.exp(jnp.minimum(s - lse, 0.0)), 0.0)
                    dp = _dot_nt(doh, vv2)
                    ds = (pr * (dp - dl) * ATT_SCALE).astype(BF16)
                    dq_h.append(_dot(ds, kk))
                    dkk = dkk + _dot_tn(ds, qh)
                    dvv = dvv + _dot_tn(pr.astype(BF16), doh)
                dq_ref[:, cols] = jnp.where(head0, dq_h[0], dq_h[1])
                dk_ref[:, cols] = dk_carry[:, cols] + dkk[:BLOCK]
                dv_ref[:, cols] = dv_carry[:, cols] + dvv[:BLOCK]
                dk_carry[:, cols] = dkk[BLOCK:]
                dv_carry[:, cols] = dvv[BLOCK:]

        @pl.when(n == nb)
        def _():
            dk_ref[...] = dk_carry[...]
            dv_ref[...] = dv_carry[...]

    last = nb - 1
    cur = pl.BlockSpec((BLOCK, D_GRP), lambda c, n: (jnp.minimum(n, last), c))
    prev = pl.BlockSpec((BLOCK, D_GRP), lambda c, n: (jnp.clip(n - 1, 0, last), c))
    out = jax.ShapeDtypeStruct(qv.shape, F32)
    return pl.pallas_call(
        body, name=f"dil_bwd_r{r}", grid=(r, nb + 1),
        in_specs=[cur, prev, cur, prev, cur, cur, cur, cur],
        out_specs=(cur, prev, prev), out_shape=(out, out, out),
        scratch_shapes=[pltpu.VMEM((BLOCK, D_GRP), F32), pltpu.VMEM((BLOCK, D_GRP), F32)],
        compiler_params=_params(),
    )(qv, kv, kv, vv, vv, dov, lsev, deltav)


def _sb_fwd(qs, ks, vs, tri_suf):
    s_len = qs.shape[0]
    t = SB_TILE
    nq = s_len // t

    npair = SB_PAIRS

    def body(q_ref, k_ref, v_ref, u_ref, o_ref, c_ref, qq, vt, acc, cf, csave):
        row = lax.broadcasted_iota(jnp.int32, (2 * t, t), 0) & (t - 1)
        col = lax.broadcasted_iota(jnp.int32, (2 * t, t), 1)
        diag_mask = col < row
        lane1 = lax.broadcasted_iota(jnp.int32, (t, 128), 1)
        head0 = lane1 < HEAD_DIM
        lane2 = lax.broadcasted_iota(jnp.int32, (2 * t, 128), 1)
        uu = u_ref[...]
        pr = range(npair)
        cols = [slice(128 * pp, 128 * (pp + 1)) for pp in pr]

        i = pl.program_id(1)

        @pl.when(i == 0)
        def _():
            def transpose_v(j, _):
                rows = pl.ds(pl.multiple_of(j * t, t), t)
                for pp in pr:
                    vt[pp, j] = v_ref[rows, cols[pp]].astype(F32).T.astype(BF16)
                return 0

            lax.fori_loop(0, nq, transpose_v, 0)

        for pp in pr:
            q2 = q_ref[:, cols[pp]] * ATT_SCALE
            qq[pp, 0:t, :] = jnp.where(head0, q2, 0)
            qq[pp, t:2 * t, :] = jnp.where(head0, 0, q2)
        acc[...] = jnp.zeros_like(acc)
        cf[...] = jnp.zeros_like(cf)
        csave[...] = jnp.zeros_like(csave)

        def tile(kb, diag):
            krows = pl.ds(pl.multiple_of(kb * t, t), t)
            zs = [_dot_nt(qq[pp], k_ref[krows, cols[pp]]) for pp in pr]
            lbk = [_log_sigmoid_pair(z) for z in zs]
            lks = [jnp.where(diag_mask, lk, 0.0) if diag else lk for _, lk in lbk]
            sufs = [_cumsum_mm(lk, uu) for lk in lks]
            carries = [cf[pp] for pp in pr]
            avs = []
            for pp in pr:
                a = jnp.exp(lbk[pp][0] + (sufs[pp] + jnp.concatenate([carries[pp]] * (t // 128), axis=1)))
                avs.append((jnp.where(diag_mask, a, 0.0) if diag else a).astype(BF16))
            pvs = [_dot_nt(vt[pp, kb], avs[pp]) for pp in pr]
            for pp in pr:
                acc[pp] += pvs[pp]
                csave[pp] = jnp.where(lane2 == kb, carries[pp], csave[pp])
                cf[pp] = carries[pp] + jnp.broadcast_to(jnp.sum(lks[pp], axis=-1, keepdims=True), (2 * t, 128))

        tile(i, True)

        def k_block(step, _):
            tile(i - 1 - step, False)
            return 0

        lax.fori_loop(0, i, k_block, 0)
        for pp in pr:
            o_ref[:, cols[pp]] = jnp.where(head0, acc[pp, :, 0:t].T, acc[pp, :, t:2 * t].T)
            c_ref[2 * pp] = csave[pp, 0:t, :]
            c_ref[2 * pp + 1] = csave[pp, t:2 * t, :]

    width = 128 * npair
    kv = pl.BlockSpec((s_len, width), lambda p, i: (0, p))
    qo = pl.BlockSpec((t, width), lambda p, i: (i, p))
    return pl.pallas_call(
        body, name="sb_fwd", grid=(4 // npair, nq),
        in_specs=[qo, kv, kv, pl.BlockSpec((2 * t, t), lambda p, i: (0, 0))],
        out_specs=(qo, pl.BlockSpec((2 * npair, t, 128), lambda p, i: (p, i, 0))),
        out_shape=(jax.ShapeDtypeStruct((s_len, D_GRP), F32),
                   jax.ShapeDtypeStruct((8, s_len, 128), F32)),
        scratch_shapes=[pltpu.VMEM((npair, 2 * t, 128), BF16), pltpu.VMEM((npair, nq, 128, t), BF16),
                        pltpu.VMEM((npair, 128, 2 * t), F32),
                        pltpu.VMEM((npair, 2 * t, 128), F32), pltpu.VMEM((npair, 2 * t, 128), F32)],
        compiler_params=_params(),
    )(qs, ks, vs, tri_suf)


def _sb_bwd(qs, ks, vs, dos, csaved, tri_suf, tri_pre):
    s_len = qs.shape[0]
    t = SB_TILE
    nq = s_len // t

    npair = SB_BWD_PAIRS

    def body(q_ref, k_ref, v_ref, do_ref, c_ref, u_ref, p_ref, dq_ref, dk_ref, dv_ref,
             qq, dd, qqt, ddt, kt, dq_acc, dkt, dvt, cg):
        row = lax.broadcasted_iota(jnp.int32, (2 * t, t), 0) & (t - 1)
        col = lax.broadcasted_iota(jnp.int32, (2 * t, t), 1)
        diag_mask = col < row
        lane1 = lax.broadcasted_iota(jnp.int32, (t, 128), 1)
        head0 = lane1 < HEAD_DIM
        lane2 = lax.broadcasted_iota(jnp.int32, (2 * t, 128), 1)
        uu, pm = u_ref[...], p_ref[...]
        pr = range(npair)
        cols = [slice(128 * pp, 128 * (pp + 1)) for pp in pr]
        i = pl.program_id(1)

        @pl.when(i == 0)
        def _():
            dkt[...] = jnp.zeros_like(dkt)
            dvt[...] = jnp.zeros_like(dvt)

            def transpose_k(j, _):
                rows = pl.ds(pl.multiple_of(j * t, t), t)
                for pp in pr:
                    kt[pp, j] = k_ref[rows, cols[pp]].astype(F32).T.astype(BF16)
                return 0

            lax.fori_loop(0, nq, transpose_k, 0)

        for pp in pr:
            q2 = q_ref[:, cols[pp]].astype(F32) * ATT_SCALE
            do2 = do_ref[:, cols[pp]].astype(F32)
            for src, nat, tr in ((q2, qq, qqt), (do2, dd, ddt)):
                stacked = jnp.concatenate([jnp.where(head0, src, 0.0), jnp.where(head0, 0.0, src)], axis=0)
                nat[pp] = stacked.astype(BF16)
                tr[pp] = stacked.T.astype(BF16)
        dq_acc[...] = jnp.zeros_like(dq_acc)
        cg[...] = jnp.zeros_like(cg)

        def tile(kb, diag):
            krows = pl.ds(pl.multiple_of(kb * t, t), t)
            zs = [_dot_nt(qq[pp], k_ref[krows, cols[pp]]) for pp in pr]
            das = [_dot_nt(dd[pp], v_ref[krows, cols[pp]]) for pp in pr]
            lbk = [_log_sigmoid_pair(z) for z in zs]
            lks = [jnp.where(diag_mask, lk, 0.0) if diag else lk for _, lk in lbk]
            sufs = [_cumsum_mm(lk, uu) for lk in lks]
            avs, gs = [], []
            for pp in pr:
                cs = jnp.concatenate([c_ref[2 * pp], c_ref[2 * pp + 1]], axis=0)
                cf = jnp.sum(jnp.where(lane2 == kb, cs, 0.0), axis=-1, keepdims=True)
                a = jnp.exp(lbk[pp][0] + (sufs[pp] + cf))
                a = jnp.where(diag_mask, a, 0.0) if diag else a
                avs.append(a.astype(BF16))
                gs.append(a * das[pp])
            gpres = [_cumsum_mm(g, pm) for g in gs]
            dzs = []
            for pp in pr:
                carry = cg[pp]
                beta = jnp.exp(lbk[pp][0])
                dz = gs[pp] - beta * (gs[pp] + (gpres[pp] + jnp.concatenate([carry] * (t // 128), axis=1)))
                dzs.append((jnp.where(diag_mask, dz, 0.0) if diag else dz).astype(BF16))
                cg[pp] = carry + jnp.broadcast_to(jnp.sum(gs[pp], axis=-1, keepdims=True), (2 * t, 128))
            dqs = [_dot_nt(kt[pp, kb], dzs[pp]) for pp in pr]
            dks = [_dot(qqt[pp], dzs[pp]) for pp in pr]
            dvs = [_dot(ddt[pp], avs[pp]) for pp in pr]
            for pp in pr:
                dq_acc[pp] += dqs[pp]
                dkt[pp, kb] += dks[pp]
                dvt[pp, kb] += dvs[pp]

        def k_block(kb, _):
            tile(kb, False)
            return 0

        lax.fori_loop(0, i, k_block, 0)
        tile(i, True)
        for pp in pr:
            dq_ref[:, cols[pp]] = jnp.where(head0, dq_acc[pp, :, 0:t].T, dq_acc[pp, :, t:2 * t].T) * ATT_SCALE

        @pl.when(i == nq - 1)
        def _():
            def untranspose(j, _):
                rows = pl.ds(pl.multiple_of(j * t, t), t)
                for pp in pr:
                    dk_ref[rows, cols[pp]] = dkt[pp, j].T
                    dv_ref[rows, cols[pp]] = dvt[pp, j].T
                return 0

            lax.fori_loop(0, nq, untranspose, 0)

    width = 128 * npair
    kv = pl.BlockSpec((s_len, width), lambda p, i: (0, p))
    qo = pl.BlockSpec((t, width), lambda p, i: (i, p))
    tri = pl.BlockSpec((2 * t, t), lambda p, i: (0, 0))
    out = jax.ShapeDtypeStruct((s_len, D_GRP), F32)
    return pl.pallas_call(
        body, name="sb_bwd", grid=(4 // npair, nq),
        in_specs=[qo, kv, kv, qo, pl.BlockSpec((2 * npair, t, 128), lambda p, i: (p, i, 0)), tri, tri],
        out_specs=(qo, kv, kv), out_shape=(out, out, out),
        scratch_shapes=[pltpu.VMEM((npair, 2 * t, 128), BF16), pltpu.VMEM((npair, 2 * t, 128), BF16),
                        pltpu.VMEM((npair, 128, 2 * t), BF16), pltpu.VMEM((npair, 128, 2 * t), BF16),
                        pltpu.VMEM((npair, nq, 128, t), BF16),
                        pltpu.VMEM((npair, 128, 2 * t), F32),
                        pltpu.VMEM((npair, nq, 128, t), F32), pltpu.VMEM((npair, nq, 128, t), F32),
                        pltpu.VMEM((npair, 2 * t, 128), F32)],
        compiler_params=_params(),
    )(qs, ks, vs, dos, csaved, tri_suf, tri_pre)


def _attn_out(o_b, lse_b, o_sb, x2, wdil, wsb, b_g):
    s_len = x2.shape[0]
    tm = ROW_TILE

    def body(o0, o1, o2, l0, l1, l2, osb_ref, x_ref, wdil_ref, wsb_ref, w_ref,
             odil_ref, lse_ref, mixed_ref, x1_ref):
        ls = (l0[...], l1[...], l2[...])
        mx = jnp.maximum(jnp.maximum(ls[0], ls[1]), ls[2])
        es = [jnp.exp(l - mx) for l in ls]
        den = es[0] + es[1] + es[2]
        o_dil = (es[0] * o0[...] + es[1] * o1[...] + es[2] * o2[...]) / den
        odil_ref[...] = o_dil
        lse_ref[...] = mx + jnp.log(den)
        halves = []
        for t, w_r in ((o_dil, wdil_ref), (osb_ref[...], wsb_ref)):
            r = lax.rsqrt(jnp.mean(t * t, axis=-1, keepdims=True) + EPS)
            halves.append(((t * r) * w_r[...]).astype(BF16))
        mixed = jnp.concatenate(halves, axis=1)
        mixed_ref[...] = mixed
        w = w_ref[...].reshape(D_MODEL, D_MODEL)
        x1_ref[...] = x_ref[...] + _dot(mixed, w)

    row = lambda w: pl.BlockSpec((tm, w), lambda i: (i, 0))
    return pl.pallas_call(
        body, name="attn_out", grid=(s_len // tm,),
        in_specs=[row(D_GRP)] * 7 + [row(D_MODEL), _full((1, D_GRP)), _full((1, D_GRP)),
                                     pl.BlockSpec((N_DEV, OUT_SHARD, D_MODEL), lambda i: (0, 3, 0))],
        out_specs=(row(D_GRP), row(D_GRP), row(D_MODEL), row(D_MODEL)),
        out_shape=(jax.ShapeDtypeStruct((s_len, D_GRP), F32), jax.ShapeDtypeStruct((s_len, D_GRP), F32),
                   jax.ShapeDtypeStruct((s_len, D_MODEL), BF16), jax.ShapeDtypeStruct((s_len, D_MODEL), F32)),
        compiler_params=_params(),
    )(*o_b, *lse_b, o_sb, x2, wdil, wsb, b_g)


def _ffn_fwd(x1, wn2, tgt, a_g, b_g):
    s_len = x1.shape[0]
    tm = ROW_TILE
    ni = s_len // tm

    def body(x_ref, wn_ref, t_ref, wg_ref, wu_ref, wd_ref, g_ref, u_ref, h2_ref, dy_ref, loss_ref, acc):
        j = pl.program_id(1)

        @pl.when(j == 0)
        def _():
            xx = x_ref[...]
            r = lax.rsqrt(jnp.mean(xx * xx, axis=-1, keepdims=True) + EPS)
            h2_ref[...] = ((xx * r) * wn_ref[...]).astype(BF16)
            acc[...] = jnp.zeros_like(acc)

        h = h2_ref[...]
        g = _dot(h, wg_ref[0])
        u = _dot(h, wu_ref[0])
        g_ref[...] = g
        u_ref[...] = u
        act = (g * (1.0 / (1.0 + jnp.exp(-g)))) * u
        acc[...] += _dot(act.astype(BF16), wd_ref[0])

        @pl.when(j == N_DEV - 1)
        def _():
            err = (x_ref[...] + acc[...]) - t_ref[...]
            dy_ref[...] = err * (1.0 / D_MODEL)
            part = 0.5 * jnp.sum(jnp.mean(err * err, axis=-1, keepdims=True))
            loss_ref[...] = jnp.full((8, 128), part, F32)

    row = pl.BlockSpec((tm, D_MODEL), lambda i, j: (i, 0))
    hid = pl.BlockSpec((tm, FF_PAD), lambda i, j: (i, j))
    return pl.pallas_call(
        body, name="ffn_fwd", grid=(ni, N_DEV),
        in_specs=[row, pl.BlockSpec((1, D_MODEL), lambda i, j: (0, 0)), row,
                  pl.BlockSpec((1, D_MODEL, FF_PAD), lambda i, j: (j, 1, 0)),
                  pl.BlockSpec((1, D_MODEL, FF_PAD), lambda i, j: (j, 2, 0)),
                  pl.BlockSpec((1, FF_PAD, D_MODEL), lambda i, j: (j, 0, 0))],
        out_specs=(hid, hid, row, row, pl.BlockSpec((8, 128), lambda i, j: (i, 0))),
        out_shape=(jax.ShapeDtypeStruct((s_len, N_DEV * FF_PAD), F32),
                   jax.ShapeDtypeStruct((s_len, N_DEV * FF_PAD), F32),
                   jax.ShapeDtypeStruct((s_len, D_MODEL), BF16),
                   jax.ShapeDtypeStruct((s_len, D_MODEL), F32),
                   jax.ShapeDtypeStruct((ni * 8, 128), F32)),
        scratch_shapes=[pltpu.VMEM((tm, D_MODEL), F32)],
        compiler_params=_params(),
    )(x1, wn2, tgt, a_g, a_g, b_g)


def _ffn_bwd_dx(dy, g, u, a_g, b_g):
    s_len = dy.shape[0]
    tm = ROW_TILE

    def body(dy_ref, g_ref, u_ref, wg_ref, wu_ref, wd_ref, dg_ref, du_ref, act_ref, dh_ref, acc):
        j = pl.program_id(1)

        @pl.when(j == 0)
        def _():
            acc[...] = jnp.zeros_like(acc)

        gg, uu = g_ref[...], u_ref[...]
        da = _dot_nt(dy_ref[...].astype(BF16), wd_ref[0])
        sig = 1.0 / (1.0 + jnp.exp(-gg))
        silu = gg * sig
        act_ref[...] = (silu * uu).astype(BF16)
        du = (da * silu).astype(BF16)
        dg = (da * uu * (sig * (1.0 + gg * (1.0 - sig)))).astype(BF16)
        du_ref[...] = du
        dg_ref[...] = dg
        acc[...] += _dot_nt(dg, wg_ref[0]) + _dot_nt(du, wu_ref[0])

        @pl.when(j == N_DEV - 1)
        def _():
            dh_ref[...] = acc[...]

    row = pl.BlockSpec((tm, D_MODEL), lambda i, j: (i, 0))
    hid = pl.BlockSpec((tm, FF_PAD), lambda i, j: (i, j))
    hid_bf = jax.ShapeDtypeStruct((s_len, N_DEV * FF_PAD), BF16)
    return pl.pallas_call(
        body, name="ffn_bwd_dx", grid=(s_len // tm, N_DEV),
        in_specs=[row, hid, hid,
                  pl.BlockSpec((1, D_MODEL, FF_PAD), lambda i, j: (j, 1, 0)),
                  pl.BlockSpec((1, D_MODEL, FF_PAD), lambda i, j: (j, 2, 0)),
                  pl.BlockSpec((1, FF_PAD, D_MODEL), lambda i, j: (j, 0, 0))],
        out_specs=(hid, hid, hid, row),
        out_shape=(hid_bf, hid_bf, hid_bf, jax.ShapeDtypeStruct((s_len, D_MODEL), F32)),
        scratch_shapes=[pltpu.VMEM((tm, D_MODEL), F32)],
        compiler_params=_params(),
    )(dy, g, u, a_g, a_g, b_g)


def _ffn_bwd_dw(h2, dy, dg, du, act):
    s_len = h2.shape[0]
    tm = ROW_TILE
    ni = s_len // tm

    def body(h_ref, dy_ref, dg_ref, du_ref, act_ref, dwg_ref, dwu_ref, dwd_ref, ag, au, ad):
        i = pl.program_id(1)

        @pl.when(i == 0)
        def _():
            ag[...] = jnp.zeros_like(ag)
            au[...] = jnp.zeros_like(au)
            ad[...] = jnp.zeros_like(ad)

        h = h_ref[...]
        ag[...] += _dot_tn(h, dg_ref[...])
        au[...] += _dot_tn(h, du_ref[...])
        ad[...] += _dot_tn(act_ref[...], dy_ref[...].astype(BF16))

        @pl.when(i == ni - 1)
        def _():
            dwg_ref[0] = ag[...].astype(BF16)
            dwu_ref[0] = au[...].astype(BF16)
            dwd_ref[0] = ad[...].astype(BF16)

    row = pl.BlockSpec((tm, D_MODEL), lambda j, i: (i, 0))
    hid = pl.BlockSpec((tm, FF_PAD), lambda j, i: (i, j))
    col_w = pl.BlockSpec((1, D_MODEL, FF_PAD), lambda j, i: (j, 0, 0))
    row_w = pl.BlockSpec((1, FF_PAD, D_MODEL), lambda j, i: (j, 0, 0))
    return pl.pallas_call(
        body, name="ffn_bwd_dw", grid=(N_DEV, ni),
        in_specs=[row, row, hid, hid, hid], out_specs=(col_w, col_w, row_w),
        out_shape=(jax.ShapeDtypeStruct((N_DEV, D_MODEL, FF_PAD), BF16),
                   jax.ShapeDtypeStruct((N_DEV, D_MODEL, FF_PAD), BF16),
                   jax.ShapeDtypeStruct((N_DEV, FF_PAD, D_MODEL), BF16)),
        scratch_shapes=[pltpu.VMEM((D_MODEL, FF_PAD), F32), pltpu.VMEM((D_MODEL, FF_PAD), F32),
                        pltpu.VMEM((FF_PAD, D_MODEL), F32)],
        compiler_params=_params(),
    )(h2, dy, dg, du, act)


def _rms_bwd(dy, t, w):
    r = lax.rsqrt(jnp.mean(t * t, axis=-1, keepdims=True) + EPS)
    gw = dy * w
    dt = r * (gw - t * ((r * r) * jnp.mean(gw * t, axis=-1, keepdims=True)))
    return dt, dy * t * r


def _attn_out_bwd(dy, dh2, x1, wn2, b_g, mixed, o_dil, o_sb, wdil, wsb, bd512):
    s_len = dy.shape[0]
    tm = ROW_TILE
    ni = s_len // tm

    def body(dy_ref, dh_ref, x1_ref, wn_ref, w_ref, mixed_ref, odil_ref, osb_ref, wdil_ref, wsb_ref, bd_ref,
             dx1_ref, dodil_ref, delta_ref, dosb_ref, dwout_ref, dwn_ref, dwdil_ref, dwsb_ref, wacc):
        i = pl.program_id(0)

        @pl.when(i == 0)
        def _():
            wacc[...] = jnp.zeros_like(wacc)
            dwn_ref[...] = jnp.zeros_like(dwn_ref)
            dwdil_ref[...] = jnp.zeros_like(dwdil_ref)
            dwsb_ref[...] = jnp.zeros_like(dwsb_ref)

        dnorm, dw_rows = _rms_bwd(dh_ref[...], x1_ref[...], wn_ref[...])
        dx1 = dy_ref[...] + dnorm
        dx1_ref[...] = dx1
        dwn_ref[...] += jnp.sum(dw_rows, axis=0, keepdims=True)
        dx1b = dx1.astype(BF16)
        w = w_ref[...].reshape(D_MODEL, D_MODEL)
        dmixed = _dot_nt(dx1b, w)
        wacc[...] += _dot_tn(mixed_ref[...], dx1b)
        o_dil = odil_ref[...]
        d_odil, dw_rows = _rms_bwd(dmixed[:, :D_GRP], o_dil, wdil_ref[...])
        dwdil_ref[...] += jnp.sum(dw_rows, axis=0, keepdims=True)
        dodil_ref[...] = d_odil.astype(BF16)
        delta_ref[...] = _mm_split(d_odil * o_dil, bd_ref[...])
        d_osb, dw_rows = _rms_bwd(dmixed[:, D_GRP:], osb_ref[...], wsb_ref[...])
        dwsb_ref[...] += jnp.sum(dw_rows, axis=0, keepdims=True)
        dosb_ref[...] = d_osb.astype(BF16)

        @pl.when(i == ni - 1)
        def _():
            dwout_ref[...] = wacc[...].astype(BF16).reshape(N_DEV, OUT_SHARD, D_MODEL)

    row = lambda w: pl.BlockSpec((tm, w), lambda i: (i, 0))
    return pl.pallas_call(
        body, name="attn_out_bwd", grid=(ni,),
        in_specs=[row(D_MODEL), row(D_MODEL), row(D_MODEL), _full((1, D_MODEL)),
                  pl.BlockSpec((N_DEV, OUT_SHARD, D_MODEL), lambda i: (0, 3, 0)),
                  row(D_MODEL), row(D_GRP), row(D_GRP), _full((1, D_GRP)), _full((1, D_GRP)),
                  _full((D_GRP, D_GRP))],
        out_specs=(row(D_MODEL), row(D_GRP), row(D_GRP), row(D_GRP),
                   _full((N_DEV, OUT_SHARD, D_MODEL)), _full((1, D_MODEL)), _full((1, D_GRP)), _full((1, D_GRP))),
        out_shape=(jax.ShapeDtypeStruct((s_len, D_MODEL), F32), jax.ShapeDtypeStruct((s_len, D_GRP), BF16),
                   jax.ShapeDtypeStruct((s_len, D_GRP), F32), jax.ShapeDtypeStruct((s_len, D_GRP), BF16),
                   jax.ShapeDtypeStruct((N_DEV, OUT_SHARD, D_MODEL), BF16),
                   jax.ShapeDtypeStruct((1, D_MODEL), F32), jax.ShapeDtypeStruct((1, D_GRP), F32),
                   jax.ShapeDtypeStruct((1, D_GRP), F32)),
        scratch_shapes=[pltpu.VMEM((D_MODEL, D_MODEL), F32)],
        compiler_params=_params(),
    )(dy, dh2, x1, wn2, b_g, mixed, o_dil, o_sb, wdil, wsb, bd512)


def _qkv_bwd(dq_b, dk_b, dv_b, dqs, dks, dvs, qraw, kraw, cos2, sin2, qnw, knw, bd):
    s_len = qraw.shape[0]
    tm = ROW_TILE
    ni = s_len // tm

    def body(dq0, dq1, dq2, dk0, dk1, dk2, dv0, dv1, dv2, dqs_ref, dks_ref, dvs_ref,
             qraw_ref, kraw_ref, cos_ref, sin_ref, qnw_ref, knw_ref, bd_ref,
             dproj_ref, dqn_ref, dkn_ref):
        i = pl.program_id(0)

        @pl.when(i == 0)
        def _():
            dqn_ref[...] = jnp.zeros_like(dqn_ref)
            dkn_ref[...] = jnp.zeros_like(dkn_ref)

        cos_t, sin_t, bdm = cos_ref[...], sin_ref[...], bd_ref[...]
        for grp, (parts, raw_ref, nw_ref, dn_ref) in enumerate((((dq0, dq1, dq2), qraw_ref, qnw_ref, dqn_ref),
                                                                ((dk0, dk1, dk2), kraw_ref, knw_ref, dkn_ref))):
            dn_acc = 0.0
            for p in range(4):
                cols = slice(128 * p, 128 * (p + 1))
                d_rope = parts[0][:, cols] + parts[1][:, cols] + parts[2][:, cols]
                d_norm = d_rope * cos_t + _swap_halves(d_rope * sin_t)
                t = raw_ref[:, cols]
                w = nw_ref[...]
                r = lax.rsqrt(_mm_split(t * t, bdm) * (1.0 / HEAD_DIM) + EPS)
                gw = d_norm * w
                corr = _mm_split(gw * t, bdm) * (1.0 / HEAD_DIM)
                dt = r * (gw - t * ((r * r) * corr))
                dn_acc = dn_acc + jnp.sum(d_norm * t * r, axis=0, keepdims=True)
                dproj_ref[:, D_GRP * grp + 128 * p:D_GRP * grp + 128 * (p + 1)] = dt.astype(BF16)
            dn_ref[...] += dn_acc
        dproj_ref[:, 2 * D_GRP:3 * D_GRP] = (dv0[...] + dv1[...] + dv2[...]).astype(BF16)
        dproj_ref[:, 3 * D_GRP:4 * D_GRP] = dqs_ref[...].astype(BF16)
        dproj_ref[:, 4 * D_GRP:5 * D_GRP] = dks_ref[...].astype(BF16)
        dproj_ref[:, 5 * D_GRP:6 * D_GRP] = dvs_ref[...].astype(BF16)

    row = lambda w: pl.BlockSpec((tm, w), lambda i: (i, 0))
    return pl.pallas_call(
        body, name="qkv_bwd", grid=(ni,),
        in_specs=[row(D_GRP)] * 14 + [row(128), row(128), _full((1, 128)), _full((1, 128)), _full((128, 128))],
        out_specs=(row(D_IN), _full((1, 128)), _full((1, 128))),
        out_shape=(jax.ShapeDtypeStruct((s_len, D_IN), BF16), jax.ShapeDtypeStruct((1, 128), F32),
                   jax.ShapeDtypeStruct((1, 128), F32)),
        compiler_params=_params(),
    )(*dq_b, *dk_b, *dv_b, dqs, dks, dvs, qraw, kraw, cos2, sin2, qnw, knw, bd)


def _in_bwd_dx(dproj, a_g, x2, dx1, wn1):
    s_len = x2.shape[0]
    tm = ROW_TILE
    ni = s_len // tm

    def body(dp_ref, w_ref, x_ref, dx1_ref, wn_ref, gx_ref, dwn_ref):
        i = pl.program_id(0)

        @pl.when(i == 0)
        def _():
            dwn_ref[...] = jnp.zeros_like(dwn_ref)

        dh = 0.0
        for d in range(N_DEV):
            dh = dh + _dot_nt(dp_ref[:, IN_SHARD * d:IN_SHARD * (d + 1)], w_ref[d])
        dnorm, dw_rows = _rms_bwd(dh, x_ref[...], wn_ref[...])
        gx_ref[...] = dx1_ref[...] + dnorm
        dwn_ref[...] += jnp.sum(dw_rows, axis=0, keepdims=True)

    row = lambda w: pl.BlockSpec((tm, w), lambda i: (i, 0))
    return pl.pallas_call(
        body, name="in_bwd_dx", grid=(ni,),
        in_specs=[row(D_IN), pl.BlockSpec((N_DEV, D_MODEL, IN_SHARD), lambda i: (0, 0, 0)),
                  row(D_MODEL), row(D_MODEL), _full((1, D_MODEL))],
        out_specs=(row(D_MODEL), _full((1, D_MODEL))),
        out_shape=(jax.ShapeDtypeStruct((s_len, D_MODEL), F32), jax.ShapeDtypeStruct((1, D_MODEL), F32)),
        compiler_params=_params(),
    )(dproj, a_g, x2, dx1, wn1)


def _in_bwd_dw(h1, dproj):
    s_len = h1.shape[0]
    tm = ROW_TILE
    ni = s_len // tm

    def body(h_ref, dp_ref, dw_ref, acc):
        i = pl.program_id(1)

        @pl.when(i == 0)
        def _():
            acc[...] = jnp.zeros_like(acc)

        acc[...] += _dot_tn(h_ref[...], dp_ref[...])

        @pl.when(i == ni - 1)
        def _():
            dw_ref[0] = acc[...].astype(BF16)

    return pl.pallas_call(
        body, name="in_bwd_dw", grid=(N_DEV, ni),
        in_specs=[pl.BlockSpec((tm, D_MODEL), lambda d, i: (i, 0)),
                  pl.BlockSpec((tm, IN_SHARD), lambda d, i: (i, d))],
        out_specs=pl.BlockSpec((1, D_MODEL, IN_SHARD), lambda d, i: (d, 0, 0)),
        out_shape=jax.ShapeDtypeStruct((N_DEV, D_MODEL, IN_SHARD), BF16),
        scratch_shapes=[pltpu.VMEM((D_MODEL, IN_SHARD), F32)],
        compiler_params=_params(),
    )(h1, dproj)


def _adamw(recv, w, m, v):
    rows, cols = w.shape
    tr = 128 if rows % 128 == 0 else rows

    def body(p_ref, w_ref, m_ref, v_ref, g_ref, d_ref, nm_ref, nv_ref):
        g = p_ref[0].astype(F32)
        for s in range(1, N_DEV):
            g = g + p_ref[s].astype(F32)
        m_new = ADAM_B1 * m_ref[...] + (1.0 - ADAM_B1) * g
        v_new = ADAM_B2 * v_ref[...] + (1.0 - ADAM_B2) * (g * g)
        m_hat = m_new / (1.0 - ADAM_B1 ** ADAM_STEP)
        v_hat = v_new / (1.0 - ADAM_B2 ** ADAM_STEP)
        g_ref[...] = g
        d_ref[...] = -ADAM_LR * (m_hat / (jnp.sqrt(v_hat) + ADAM_EPS) + ADAM_WD * w_ref[...])
        nm_ref[...] = m_new
        nv_ref[...] = v_new

    blk = pl.BlockSpec((tr, cols), lambda i: (i, 0))
    out = jax.ShapeDtypeStruct((rows, cols), F32)
    return pl.pallas_call(
        body, name=f"adamw_{rows}x{cols}", grid=(rows // tr,),
        in_specs=[pl.BlockSpec((N_DEV, tr, cols), lambda i: (0, i, 0)), blk, blk, blk],
        out_specs=(blk,) * 4, out_shape=(out,) * 4,
        compiler_params=_params(),
    )(recv, w, m, v)


def _rope_tables(s_len):
    pos = jnp.arange(s_len, dtype=F32)
    inv_freq = ROPE_THETA ** (-jnp.arange(0, HEAD_DIM, 2, dtype=F32) / HEAD_DIM)
    ang = pos[:, None] * inv_freq[None, :]
    cos, sin = jnp.cos(ang), jnp.sin(ang)
    cos2 = jnp.concatenate([cos, cos, cos, cos], axis=1)
    sin2 = jnp.concatenate([-sin, sin, -sin, sin], axis=1)
    return cos2, sin2


def _block_diag_ones(n):
    i = jnp.arange(n)
    return (i[:, None] // HEAD_DIM == i[None, :] // HEAD_DIM).astype(BF16)


def _pad_cols(t):
    return jnp.pad(t, ((0, 0), (0, FF_PAD - FF_SHARD)))


def _pad_rows(t):
    return jnp.pad(t, ((0, FF_PAD - FF_SHARD), (0, 0)))


def _pack_small(n1, n2, ndil, nsb, nq, nk):
    pad = lambda t: jnp.pad(t.reshape(1, HEAD_DIM), ((0, 0), (0, 128 - HEAD_DIM)))
    rows = [n1.reshape(8, 128), n2.reshape(8, 128), ndil.reshape(4, 128), nsb.reshape(4, 128),
            pad(nq), pad(nk), jnp.zeros((6, 128), F32)]
    return jnp.concatenate(rows, axis=0)


def _unpack_small(t):
    return (t[0:8].reshape(1, D_MODEL), t[8:16].reshape(1, D_MODEL), t[16:20].reshape(1, D_GRP),
            t[20:24].reshape(1, D_GRP), t[24:25, :HEAD_DIM], t[25:26, :HEAD_DIM])


def kernel(x, attn_norm_w, w_in, q_norm_w, k_norm_w, dil_out_norm_w, sb_out_norm_w, w_out, ffn_norm_w, w_gate, w_up, w_down, loss_target, m_attn_norm_w, m_w_in, m_q_norm_w, m_k_norm_w, m_dil_out_norm_w, m_sb_out_norm_w, m_w_out, m_ffn_norm_w, m_w_gate, m_w_up, m_w_down, v_attn_norm_w, v_w_in, v_q_norm_w, v_k_norm_w, v_dil_out_norm_w, v_sb_out_norm_w, v_w_out, v_ffn_norm_w, v_w_gate, v_w_up, v_w_down):
    s_len = x.shape[1]
    x2, tgt = x[0], loss_target[0]

    a_loc = jnp.concatenate([w_in[0], _pad_cols(w_gate[0]), _pad_cols(w_up[0])], axis=0).astype(BF16)
    b_loc = jnp.concatenate([_pad_rows(w_down[0]), w_out[0]], axis=0).astype(BF16)
    a_g, b_g = _gather_weights(a_loc, b_loc)

    cos2, sin2 = _rope_tables(s_len)
    bd128, bd512 = _block_diag_ones(128), _block_diag_ones(D_GRP)
    idx = jnp.arange(SB_TILE)
    tri_suf = (idx[:, None] > idx[None, :]).astype(BF16)
    tri_pre = (idx[:, None] < idx[None, :]).astype(BF16)
    tri_suf = jnp.concatenate([tri_suf, tri_suf], axis=0)
    tri_pre = jnp.concatenate([tri_pre, tri_pre], axis=0)
    qnw2 = jnp.concatenate([q_norm_w, q_norm_w], axis=1)
    knw2 = jnp.concatenate([k_norm_w, k_norm_w], axis=1)

    h1, qraw, kraw, q, k, va, qs, ks, vs = _attn_in(x2, attn_norm_w, a_g, cos2, sin2, qnw2, knw2, bd128)
    view = lambda t, r: t.reshape(s_len // r, r * D_GRP)
    unview = lambda t: t.reshape(s_len, D_GRP)
    o_b, lse_b = [], []
    for r in DILATIONS:
        o, lse = _dil_fwd(view(q, r), view(k, r), view(va, r), r)
        o_b.append(unview(o))
        lse_b.append(unview(lse))
    o_sb, c_sb = _sb_fwd(qs, ks, vs, tri_suf)
    o_dil, lse_tot, mixed, x1 = _attn_out(o_b, lse_b, o_sb, x2, dil_out_norm_w, sb_out_norm_w, b_g)
    g, u, h2, dy, loss_parts = _ffn_fwd(x1, ffn_norm_w, tgt, a_g, b_g)
    loss = lax.psum(jnp.sum(loss_parts[::8, 0]), ("x", "y", "c"))

    dg, du, act, dh2 = _ffn_bwd_dx(dy, g, u, a_g, b_g)
    dwg, dwu, dwd = _ffn_bwd_dw(h2, dy, dg, du, act)
    dx1, do_dil, delta, do_sb, dwout, dn2, dndil, dnsb = _attn_out_bwd(
        dy, dh2, x1, ffn_norm_w, b_g, mixed, o_dil, o_sb, dil_out_norm_w, sb_out_norm_w, bd512)
    dqs, dks, dvs = _sb_bwd(qs, ks, vs, do_sb, c_sb, tri_suf, tri_pre)
    dq_b, dk_b, dv_b = [], [], []
    for r in DILATIONS:
        dq, dk, dv = _dil_bwd(view(q, r), view(k, r), view(va, r), view(do_dil, r), view(lse_tot, r),
                              view(delta, r), r)
        dq_b.append(unview(dq))
        dk_b.append(unview(dk))
        dv_b.append(unview(dv))
    dproj, dqn2, dkn2 = _qkv_bwd(dq_b, dk_b, dv_b, dqs, dks, dvs, qraw, kraw, cos2, sin2, qnw2, knw2, bd128)
    grad_x, dn1 = _in_bwd_dx(dproj, a_g, x2, dx1, attn_norm_w)
    dwin = _in_bwd_dw(h1, dproj)
    dqn = dqn2[:, :HEAD_DIM] + dqn2[:, HEAD_DIM:]
    dkn = dkn2[:, :HEAD_DIM] + dkn2[:, HEAD_DIM:]

    small = _pack_small(dn1, dn2, dndil, dnsb, dqn, dkn)
    r_in, r_gate, r_up, r_down, r_out, r_small = _exchange_grads([dwin, dwg, dwu, dwd, dwout], small)
    big = {
        "w_in": _adamw(r_in, w_in[0], m_w_in[0], v_w_in[0]),
        "w_gate": tuple(t[:, :FF_SHARD] for t in _adamw(r_gate, _pad_cols(w_gate[0]), _pad_cols(m_w_gate[0]), _pad_cols(v_w_gate[0]))),
        "w_up": tuple(t[:, :FF_SHARD] for t in _adamw(r_up, _pad_cols(w_up[0]), _pad_cols(m_w_up[0]), _pad_cols(v_w_up[0]))),
        "w_down": tuple(t[:FF_SHARD] for t in _adamw(r_down, _pad_rows(w_down[0]), _pad_rows(m_w_down[0]), _pad_rows(v_w_down[0]))),
        "w_out": _adamw(r_out, w_out[0], m_w_out[0], v_w_out[0]),
    }
    packs = [_pack_small(*ts) for ts in (
        (attn_norm_w, ffn_norm_w, dil_out_norm_w, sb_out_norm_w, q_norm_w, k_norm_w),
        (m_attn_norm_w, m_ffn_norm_w, m_dil_out_norm_w, m_sb_out_norm_w, m_q_norm_w, m_k_norm_w),
        (v_attn_norm_w, v_ffn_norm_w, v_dil_out_norm_w, v_sb_out_norm_w, v_q_norm_w, v_k_norm_w))]
    small_out = [_unpack_small(t) for t in _adamw(r_small, *packs)]
    names = ["attn_norm_w", "w_in", "q_norm_w", "k_norm_w", "dil_out_norm_w", "sb_out_norm_w", "w_out",
             "ffn_norm_w", "w_gate", "w_up", "w_down"]
    small_pos = {"attn_norm_w": 0, "ffn_norm_w": 1, "dil_out_norm_w": 2, "sb_out_norm_w": 3,
                 "q_norm_w": 4, "k_norm_w": 5}
    outs = [loss, grad_x[None]]
    for kind in range(4):
        for name in names:
            if name in small_pos:
                outs.append(small_out[kind][small_pos[name]])
            else:
                outs.append(big[name][kind][None])
    return tuple(outs)
```

```python
import functools

import jax
import jax.numpy as jnp
from jax import lax
from jax.experimental import pallas as pl
from jax.experimental.pallas import tpu as pltpu

F32 = jnp.float32
BF16 = jnp.bfloat16

N_DEV = 8
D_MODEL = 1024
HEAD_DIM = 64
D_GRP = 512
D_IN = 6 * D_GRP
IN_SHARD = D_IN // N_DEV
FF_SHARD = 352
FF_PAD = 384
OUT_SHARD = D_MODEL // N_DEV
BLOCK = 128
DILATIONS = (1, 4, 16)
ROPE_THETA = 10000.0
EPS = 1e-6
ATT_SCALE = HEAD_DIM ** -0.5
NEG = -1e30

ADAM_LR = 0.001
ADAM_B1 = 0.9
ADAM_B2 = 0.999
ADAM_EPS = 1e-08
ADAM_WD = 0.01
ADAM_STEP = 10

SB_TILE = 256
SB_PAIRS = 4
SB_BWD_PAIRS = 2
ROW_TILE = 512
VMEM_LIMIT = 56 * 1024 * 1024
MESH = pl.DeviceIdType.MESH


def _dot(a, b):
    return jnp.dot(a, b, preferred_element_type=F32)


def _dot_nt(a, b):
    return lax.dot_general(a, b, (((1,), (1,)), ((), ())), preferred_element_type=F32)


def _dot_tn(a, b):
    return lax.dot_general(a, b, (((0,), (0,)), ((), ())), preferred_element_type=F32)


def _mm_split(t, m):
    hi = t.astype(BF16)
    lo = (t - hi.astype(F32)).astype(BF16)
    return _dot(hi, m) + _dot(lo, m)


def _params(**kw):
    return pltpu.CompilerParams(vmem_limit_bytes=VMEM_LIMIT, **kw)


def _full(shape):
    nd = len(shape)
    return pl.BlockSpec(shape, lambda *_: (0,) * nd)


def _view_shape(s_len, r, dtype):
    return jax.ShapeDtypeStruct((s_len // r, r * D_GRP), dtype)


def _view_spec(tm, r):
    return pl.BlockSpec((tm // r, r * D_GRP), lambda i: (i, 0))


def _swap_halves(t):
    lane = lax.broadcasted_iota(jnp.int32, t.shape, 1)
    first = (lane & 32) == 0
    return jnp.where(first, pltpu.roll(t, 96, 1), pltpu.roll(t, 32, 1))


def _log_sigmoid(z):
    return jnp.minimum(z, 0.0) - jnp.log(1.0 + jnp.exp(-jnp.abs(z)))


def _log_sigmoid_pair(z):
    neg_abs = lax.bitcast_convert_type(lax.bitcast_convert_type(z, jnp.uint32) | jnp.uint32(0x80000000), F32)
    lb = jnp.minimum(z, 0.0) - jnp.log(1.0 + jnp.exp(neg_abs))
    return lb, lb - z


def _cumsum_mm(t, tri2):
    hi = lax.bitcast_convert_type(lax.bitcast_convert_type(t, jnp.uint32) & jnp.uint32(0xFFFF0000), F32)
    lhs = jnp.concatenate([hi.astype(BF16), (t - hi).astype(BF16)], axis=1)
    return _dot(lhs, tri2)


def _split_views(src_ref, stage_ref, views4, views16):
    slabs, n, _ = src_ref.shape
    n4, n16 = n // 4, n // 16
    for j in range(slabs):
        g, lanes = j // 4, 128 * (j % 4)
        src, stage = src_ref.at[j], stage_ref.at[j]
        for c4 in range(4):
            blk = src[pl.ds(c4, n4, stride=4), :]
            stage[n4 * c4:n4 * (c4 + 1), :] = blk
            col = D_GRP * c4 + lanes
            views4[g][:, col:col + 128] = blk.astype(views4[g].dtype)
        for c4 in range(4):
            for c1 in range(4):
                blk = stage[pl.ds(n4 * c4 + c1, n16, stride=4), :]
                col = D_GRP * (4 * c1 + c4) + lanes
                views16[g][:, col:col + 128] = blk.astype(views16[g].dtype)


def _merge_views(views4, views16, stage_ref, dst4_ref, dst16_ref):
    slabs, n, _ = dst4_ref.shape
    n4, n16 = n // 4, n // 16
    for j in range(slabs):
        g, lanes = j // 4, 128 * (j % 4)
        dst4, dst16, stage = dst4_ref.at[j], dst16_ref.at[j], stage_ref.at[j]
        for c4 in range(4):
            col = D_GRP * c4 + lanes
            dst4[pl.ds(c4, n4, stride=4), :] = views4[g][:, col:col + 128].astype(F32)
            for c1 in range(4):
                col = D_GRP * (4 * c1 + c4) + lanes
                stage[pl.ds(n4 * c4 + c1, n16, stride=4), :] = views16[g][:, col:col + 128].astype(F32)
        for c4 in range(4):
            dst16[pl.ds(c4, n4, stride=4), :] = stage[n4 * c4:n4 * (c4 + 1), :]


def _slab_group(ref, g):
    return jnp.concatenate([ref[4 * g + p] for p in range(4)], axis=1)


def _mesh_pos():
    return lax.axis_index("x"), lax.axis_index("y"), lax.axis_index("c")


def _flat_index(p):
    return 4 * p[0] + 2 * p[1] + p[2]


def _gather_weights(a, b):
    def body(a_ref, b_ref, ag_ref, bg_ref, send_sems, recv_sems, local_sems):
        x, y, c = _mesh_pos()
        me, sibling = (x, y, c), (x, y, 1 - c)
        chips = [(1 - x, y), (x, 1 - y), (1 - x, 1 - y)]
        srcs, outs = (a_ref, b_ref), (ag_ref, bg_ref)

        def copy(arr, k, block, to, own=False):
            dst = outs[arr].at[_flat_index(block)]
            return pltpu.make_async_remote_copy(
                src_ref=srcs[arr] if own else dst, dst_ref=dst,
                send_sem=send_sems.at[arr, k], recv_sem=recv_sems.at[arr, k],
                device_id=to, device_id_type=MESH)

        for arr in range(2):
            mine = pltpu.make_async_copy(srcs[arr], outs[arr].at[_flat_index(me)], local_sems.at[arr])
            mine.start()
            first = [copy(arr, 0, me, sibling, own=True)]
            first += [copy(arr, 1 + j, me, (*chip, c), own=True) for j, chip in enumerate(chips)]
            for cp in first:
                cp.start()
        for arr in range(2):
            passed = [copy(arr, 4 + j, (*chip, c), sibling) for j, chip in enumerate(chips)]
            for j, chip in enumerate(chips):
                copy(arr, 1 + j, (*chip, c), me).wait_recv()
                passed[j].start()
        for arr in range(2):
            copy(arr, 0, sibling, me).wait_recv()
            for j, chip in enumerate(chips):
                copy(arr, 4 + j, (*chip, 1 - c), me).wait_recv()
            for k in range(7):
                copy(arr, k, me, me).wait_send()
            pltpu.make_async_copy(srcs[arr], outs[arr].at[_flat_index(me)], local_sems.at[arr]).wait()

    any_spec = pl.BlockSpec(memory_space=pl.ANY)
    return pl.pallas_call(
        body, name="gather_weights",
        out_shape=(jax.ShapeDtypeStruct((N_DEV,) + a.shape, a.dtype),
                   jax.ShapeDtypeStruct((N_DEV,) + b.shape, b.dtype)),
        in_specs=[any_spec, any_spec], out_specs=(any_spec, any_spec),
        scratch_shapes=[pltpu.SemaphoreType.DMA((2, 7)), pltpu.SemaphoreType.DMA((2, 7)),
                        pltpu.SemaphoreType.DMA((2,))],
        compiler_params=pltpu.CompilerParams(has_side_effects=True),
    )(a, b)


def _exchange_grads(parts, small):
    n_arr = len(parts)

    def body(*refs):
        ins, outs = refs[:n_arr + 1], refs[n_arr + 1:2 * (n_arr + 1)]
        send_sems, recv_sems, local_sems = refs[2 * (n_arr + 1):]
        x, y, c = _mesh_pos()
        me = (x, y, c)
        my_idx = _flat_index(me)
        peers = []
        for m in range(1, N_DEV):
            peers.append((1 - x if m & 4 else x, 1 - y if m & 2 else y, 1 - c if m & 1 else c))

        def src_block(arr, dev):
            return ins[arr] if arr == n_arr else ins[arr].at[_flat_index(dev)]

        def copy(arr, k):
            return pltpu.make_async_remote_copy(
                src_ref=src_block(arr, peers[k]), dst_ref=outs[arr].at[my_idx],
                send_sem=send_sems.at[arr, k], recv_sem=recv_sems.at[arr, k],
                device_id=peers[k], device_id_type=MESH)

        def local(arr):
            return pltpu.make_async_copy(src_block(arr, me), outs[arr].at[my_idx], local_sems.at[arr])

        for arr in range(n_arr + 1):
            local(arr).start()
            for k in range(N_DEV - 1):
                copy(arr, k).start()
        for arr in range(n_arr + 1):
            for k in range(N_DEV - 1):
                cp = copy(arr, k)
                cp.wait_send()
                cp.wait_recv()
            local(arr).wait()

    any_spec = pl.BlockSpec(memory_space=pl.ANY)
    out_shape = tuple(jax.ShapeDtypeStruct(p.shape, p.dtype) for p in parts)
    out_shape += (jax.ShapeDtypeStruct((N_DEV,) + small.shape, small.dtype),)
    return pl.pallas_call(
        body, name="exchange_grads",
        out_shape=out_shape,
        in_specs=[any_spec] * (n_arr + 1), out_specs=(any_spec,) * (n_arr + 1),
        scratch_shapes=[pltpu.SemaphoreType.DMA((n_arr + 1, N_DEV - 1)),
                        pltpu.SemaphoreType.DMA((n_arr + 1, N_DEV - 1)),
                        pltpu.SemaphoreType.DMA((n_arr + 1,))],
        compiler_params=pltpu.CompilerParams(has_side_effects=True),
    )(*parts, small)


def _head_norm(t, w128, bd):
    ms = _mm_split(t * t, bd) * (1.0 / HEAD_DIM)
    r = lax.rsqrt(ms + EPS)
    return (t * r) * w128, r


def _attn_in(x2, wn1, a_g, cos2, sin2, qnw, knw, bd):
    s_len = x2.shape[0]
    tm = ROW_TILE

    def body(x_ref, wn_ref, w_ref, cos_ref, sin_ref, qnw_ref, knw_ref, bd_ref,
             h1_ref, qraw_ref, kraw_ref, q_ref, k_ref, va_ref, qs_ref, ks_ref, vs_ref,
             q4_ref, k4_ref, v4_ref, q16_ref, k16_ref, v16_ref, proj, slabs, stage):
        xx = x_ref[...]
        r = lax.rsqrt(jnp.mean(xx * xx, axis=-1, keepdims=True) + EPS)
        h = ((xx * r) * wn_ref[...]).astype(BF16)
        h1_ref[...] = h
        for d in range(N_DEV):
            proj[:, IN_SHARD * d:IN_SHARD * (d + 1)] = _dot(h, w_ref[d])
        cos_t, sin_t, bdm = cos_ref[...], sin_ref[...], bd_ref[...]
        for grp, (raw_ref, rope_ref, nw_ref) in enumerate(((qraw_ref, q_ref, qnw_ref),
                                                           (kraw_ref, k_ref, knw_ref))):
            for p in range(4):
                cols = slice(D_GRP * grp + 128 * p, D_GRP * grp + 128 * (p + 1))
                t = proj[:, cols]
                raw_ref[:, 128 * p:128 * (p + 1)] = t
                yn, _ = _head_norm(t, nw_ref[...], bdm)
                roped = yn * cos_t + _swap_halves(yn) * sin_t
                slabs[4 * grp + p] = roped
                rope_ref[:, 128 * p:128 * (p + 1)] = roped.astype(BF16)
        for p in range(4):
            slabs[8 + p] = proj[:, 2 * D_GRP + 128 * p:2 * D_GRP + 128 * (p + 1)]
        for grp, ref in ((2, va_ref), (3, qs_ref), (4, ks_ref), (5, vs_ref)):
            ref[...] = proj[:, D_GRP * grp:D_GRP * (grp + 1)].astype(BF16)
        _split_views(slabs, stage, (q4_ref, k4_ref, v4_ref), (q16_ref, k16_ref, v16_ref))

    row = lambda w: pl.BlockSpec((tm, w), lambda i: (i, 0))
    grp_bf = jax.ShapeDtypeStruct((s_len, D_GRP), BF16)
    grp_f32 = jax.ShapeDtypeStruct((s_len, D_GRP), F32)
    return pl.pallas_call(
        body, name="attn_in", grid=(s_len // tm,),
        in_specs=[row(D_MODEL), _full((1, D_MODEL)),
                  pl.BlockSpec((N_DEV, D_MODEL, IN_SHARD), lambda i: (0, 0, 0)),
                  row(128), row(128), _full((1, 128)), _full((1, 128)), _full((128, 128))],
        out_specs=(row(D_MODEL),) + (row(D_GRP),) * 8 + (_view_spec(tm, 4),) * 3 + (_view_spec(tm, 16),) * 3,
        out_shape=(jax.ShapeDtypeStruct((s_len, D_MODEL), BF16), grp_f32, grp_f32) + (grp_bf,) * 6
        + (_view_shape(s_len, 4, BF16),) * 3 + (_view_shape(s_len, 16, BF16),) * 3,
        scratch_shapes=[pltpu.VMEM((tm, D_IN), F32), pltpu.VMEM((12, tm, 128), F32), pltpu.VMEM((12, tm, 128), F32)],
        compiler_params=_params(),
    )(x2, wn1, a_g, cos2, sin2, qnw, knw, bd)


def _band_mask(n):
    i = lax.broadcasted_iota(jnp.int32, (BLOCK, 2 * BLOCK), 0)
    j = lax.broadcasted_iota(jnp.int32, (BLOCK, 2 * BLOCK), 1)
    dist = i + BLOCK - j
    return (dist >= 0) & (dist <= BLOCK) & ((n - 1) * BLOCK + j >= 0)


def _dil_fwd(qv, kv, vv, r):
    sub_len = qv.shape[0]
    nb = sub_len // BLOCK

    def body(q_ref, kp_ref, kc_ref, vp_ref, vc_ref, o_ref, lse_ref):
        n = pl.program_id(1)
        valid = _band_mask(n)
        lane = lax.broadcasted_iota(jnp.int32, (BLOCK, 128), 1)
        head0 = lane < HEAD_DIM
        for p in range(4):
            cols = slice(128 * p, 128 * (p + 1))
            q2 = q_ref[:, cols]
            kk = jnp.concatenate([kp_ref[:, cols], kc_ref[:, cols]], axis=0)
            vv2 = jnp.concatenate([vp_ref[:, cols], vc_ref[:, cols]], axis=0)
            res = []
            for h in range(2):
                qh = jnp.where(head0, q2, 0) if h == 0 else jnp.where(head0, 0, q2)
                s = jnp.where(valid, _dot_nt(qh, kk) * ATT_SCALE, NEG)
                m = jnp.max(s, axis=-1, keepdims=True)
                pr = jnp.exp(s - m)
                den = jnp.sum(pr, axis=-1, keepdims=True)
                o = _dot(pr.astype(BF16), vv2) / den
                res.append((o, m + jnp.log(den)))
            o_ref[:, cols] = jnp.where(head0, res[0][0], res[1][0])
            lse_ref[:, cols] = jnp.where(head0, res[0][1], res[1][1])

    cur = pl.BlockSpec((BLOCK, D_GRP), lambda c, n: (n, c))
    prev = pl.BlockSpec((BLOCK, D_GRP), lambda c, n: (jnp.maximum(n - 1, 0), c))
    out = jax.ShapeDtypeStruct(qv.shape, F32)
    return pl.pallas_call(
        body, name=f"dil_fwd_r{r}", grid=(r, nb),
        in_specs=[cur, prev, cur, prev, cur], out_specs=(cur, cur), out_shape=(out, out),
        compiler_params=_params(),
    )(qv, kv, kv, vv, vv)


def _dil_bwd(qv, kv, vv, dov, lsev, deltav, r):
    sub_len = qv.shape[0]
    nb = sub_len // BLOCK

    def body(q_ref, kp_ref, kc_ref, vp_ref, vc_ref, do_ref, lse_ref, dl_ref,
             dq_ref, dk_ref, dv_ref, dk_carry, dv_carry):
        n = pl.program_id(1)

        @pl.when(n == 0)
        def _():
            dk_carry[...] = jnp.zeros_like(dk_carry)
            dv_carry[...] = jnp.zeros_like(dv_carry)

        @pl.when(n < nb)
        def _():
            valid = _band_mask(n)
            lane = lax.broadcasted_iota(jnp.int32, (BLOCK, 128), 1)
            head0 = lane < HEAD_DIM
            for p in range(4):
                cols = slice(128 * p, 128 * (p + 1))
                q2, do2 = q_ref[:, cols], do_ref[:, cols]
                lse2, dl2 = lse_ref[:, cols], dl_ref[:, cols]
                kk = jnp.concatenate([kp_ref[:, cols], kc_ref[:, cols]], axis=0)
                vv2 = jnp.concatenate([vp_ref[:, cols], vc_ref[:, cols]], axis=0)
                dq_h, dkk, dvv = [], 0.0, 0.0
                for h in range(2):
                    sel = (lambda t: jnp.where(head0, t, 0)) if h == 0 else (lambda t: jnp.where(head0, 0, t))
                    qh, doh = sel(q2), sel(do2)
                    one_lane = lane == (0 if h == 0 else HEAD_DIM)
                    lse = jnp.sum(jnp.where(one_lane, lse2, 0.0), axis=-1, keepdims=True)
                    dl = jnp.sum(jnp.where(one_lane, dl2, 0.0), axis=-1, keepdims=True)
                    s = _dot_nt(qh, kk) * ATT_SCALE
                    pr = jnp.where(valid, jnp.exp(jnp.minimum(s - lse, 0.0)), 0.0)
                    dp = _dot_nt(doh, vv2)
                    ds = (pr * (dp - dl) * ATT_SCALE).astype(BF16)
                    dq_h.append(_dot(ds, kk))
                    dkk = dkk + _dot_tn(ds, qh)
                    dvv = dvv + _dot_tn(pr.astype(BF16), doh)
                dq_ref[:, cols] = jnp.where(head0, dq_h[0], dq_h[1])
                dk_ref[:, cols] = dk_carry[:, cols] + dkk[:BLOCK]
                dv_ref[:, cols] = dv_carry[:, cols] + dvv[:BLOCK]
                dk_carry[:, cols] = dkk[BLOCK:]
                dv_carry[:, cols] = dvv[BLOCK:]

        @pl.when(n == nb)
        def _():
            dk_ref[...] = dk_carry[...]
            dv_ref[...] = dv_carry[...]

    last = nb - 1
    cur = pl.BlockSpec((BLOCK, D_GRP), lambda c, n: (jnp.minimum(n, last), c))
    prev = pl.BlockSpec((BLOCK, D_GRP), lambda c, n: (jnp.clip(n - 1, 0, last), c))
    out = jax.ShapeDtypeStruct(qv.shape, F32)
    return pl.pallas_call(
        body, name=f"dil_bwd_r{r}", grid=(r, nb + 1),
        in_specs=[cur, prev, cur, prev, cur, cur, cur, cur],
        out_specs=(cur, prev, prev), out_shape=(out, out, out),
        scratch_shapes=[pltpu.VMEM((BLOCK, D_GRP), F32), pltpu.VMEM((BLOCK, D_GRP), F32)],
        compiler_params=_params(),
    )(qv, kv, kv, vv, vv, dov, lsev, deltav)


def _sb_fwd(qs, ks, vs, tri_suf):
    s_len = qs.shape[0]
    t = SB_TILE
    nq = s_len // t

    npair = SB_PAIRS

    def body(q_ref, k_ref, v_ref, u_ref, o_ref, c_ref, qq, vt, acc, cf, csave):
        row = lax.broadcasted_iota(jnp.int32, (2 * t, t), 0) & (t - 1)
        col = lax.broadcasted_iota(jnp.int32, (2 * t, t), 1)
        diag_mask = col < row
        lane1 = lax.broadcasted_iota(jnp.int32, (t, 128), 1)
        head0 = lane1 < HEAD_DIM
        lane2 = lax.broadcasted_iota(jnp.int32, (2 * t, 128), 1)
        uu = u_ref[...]
        pr = range(npair)
        cols = [slice(128 * pp, 128 * (pp + 1)) for pp in pr]

        i = pl.program_id(1)

        @pl.when(i == 0)
        def _():
            def transpose_v(j, _):
                rows = pl.ds(pl.multiple_of(j * t, t), t)
                for pp in pr:
                    vt[pp, j] = v_ref[rows, cols[pp]].astype(F32).T.astype(BF16)
                return 0

            lax.fori_loop(0, nq, transpose_v, 0)

        for pp in pr:
            q2 = q_ref[:, cols[pp]] * ATT_SCALE
            qq[pp, 0:t, :] = jnp.where(head0, q2, 0)
            qq[pp, t:2 * t, :] = jnp.where(head0, 0, q2)
        acc[...] = jnp.zeros_like(acc)
        cf[...] = jnp.zeros_like(cf)
        csave[...] = jnp.zeros_like(csave)

        def tile(kb, diag):
            krows = pl.ds(pl.multiple_of(kb * t, t), t)
            zs = [_dot_nt(qq[pp], k_ref[krows, cols[pp]]) for pp in pr]
            lbk = [_log_sigmoid_pair(z) for z in zs]
            lks = [jnp.where(diag_mask, lk, 0.0) if diag else lk for _, lk in lbk]
            sufs = [_cumsum_mm(lk, uu) for lk in lks]
            carries = [cf[pp] for pp in pr]
            avs = []
            for pp in pr:
                a = jnp.exp(lbk[pp][0] + (sufs[pp] + jnp.concatenate([carries[pp]] * (t // 128), axis=1)))
                avs.append((jnp.where(diag_mask, a, 0.0) if diag else a).astype(BF16))
            pvs = [_dot_nt(vt[pp, kb], avs[pp]) for pp in pr]
            for pp in pr:
                acc[pp] += pvs[pp]
                csave[pp] = jnp.where(lane2 == kb, carries[pp], csave[pp])
                cf[pp] = carries[pp] + jnp.broadcast_to(jnp.sum(lks[pp], axis=-1, keepdims=True), (2 * t, 128))

        tile(i, True)

        def k_block(step, _):
            tile(i - 1 - step, False)
            return 0

        lax.fori_loop(0, i, k_block, 0)
        for pp in pr:
            o_ref[:, cols[pp]] = jnp.where(head0, acc[pp, :, 0:t].T, acc[pp, :, t:2 * t].T)
            c_ref[2 * pp] = csave[pp, 0:t, :]
            c_ref[2 * pp + 1] = csave[pp, t:2 * t, :]

    width = 128 * npair
    kv = pl.BlockSpec((s_len, width), lambda p, i: (0, p))
    qo = pl.BlockSpec((t, width), lambda p, i: (i, p))
    return pl.pallas_call(
        body, name="sb_fwd", grid=(4 // npair, nq),
        in_specs=[qo, kv, kv, pl.BlockSpec((2 * t, t), lambda p, i: (0, 0))],
        out_specs=(qo, pl.BlockSpec((2 * npair, t, 128), lambda p, i: (p, i, 0))),
        out_shape=(jax.ShapeDtypeStruct((s_len, D_GRP), F32),
                   jax.ShapeDtypeStruct((8, s_len, 128), F32)),
        scratch_shapes=[pltpu.VMEM((npair, 2 * t, 128), BF16), pltpu.VMEM((npair, nq, 128, t), BF16),
                        pltpu.VMEM((npair, 128, 2 * t), F32),
                        pltpu.VMEM((npair, 2 * t, 128), F32), pltpu.VMEM((npair, 2 * t, 128), F32)],
        compiler_params=_params(),
    )(qs, ks, vs, tri_suf)


def _sb_bwd(qs, ks, vs, dos, csaved, tri_suf, tri_pre):
    s_len = qs.shape[0]
    t = SB_TILE
    nq = s_len // t

    npair = SB_BWD_PAIRS

    def body(q_ref, k_ref, v_ref, do_ref, c_ref, u_ref, p_ref, dq_ref, dk_ref, dv_ref,
             qq, dd, qqt, ddt, kt, dq_acc, dkt, dvt, cg):
        row = lax.broadcasted_iota(jnp.int32, (2 * t, t), 0) & (t - 1)
        col = lax.broadcasted_iota(jnp.int32, (2 * t, t), 1)
        diag_mask = col < row
        lane1 = lax.broadcasted_iota(jnp.int32, (t, 128), 1)
        head0 = lane1 < HEAD_DIM
        lane2 = lax.broadcasted_iota(jnp.int32, (2 * t, 128), 1)
        uu, pm = u_ref[...], p_ref[...]
        pr = range(npair)
        cols = [slice(128 * pp, 128 * (pp + 1)) for pp in pr]
        i = pl.program_id(1)

        @pl.when(i == 0)
        def _():
            dkt[...] = jnp.zeros_like(dkt)
            dvt[...] = jnp.zeros_like(dvt)

            def transpose_k(j, _):
                rows = pl.ds(pl.multiple_of(j * t, t), t)
                for pp in pr:
                    kt[pp, j] = k_ref[rows, cols[pp]].astype(F32).T.astype(BF16)
                return 0

            lax.fori_loop(0, nq, transpose_k, 0)

        for pp in pr:
            q2 = q_ref[:, cols[pp]].astype(F32) * ATT_SCALE
            do2 = do_ref[:, cols[pp]].astype(F32)
            for src, nat, tr in ((q2, qq, qqt), (do2, dd, ddt)):
                stacked = jnp.concatenate([jnp.where(head0, src, 0.0), jnp.where(head0, 0.0, src)], axis=0)
                nat[pp] = stacked.astype(BF16)
                tr[pp] = stacked.T.astype(BF16)
        dq_acc[...] = jnp.zeros_like(dq_acc)
        cg[...] = jnp.zeros_like(cg)

        def tile(kb, diag):
            krows = pl.ds(pl.multiple_of(kb * t, t), t)
            zs = [_dot_nt(qq[pp], k_ref[krows, cols[pp]]) for pp in pr]
            das = [_dot_nt(dd[pp], v_ref[krows, cols[pp]]) for pp in pr]
            lbk = [_log_sigmoid_pair(z) for z in zs]
            lks = [jnp.where(diag_mask, lk, 0.0) if diag else lk for _, lk in lbk]
            sufs = [_cumsum_mm(lk, uu) for lk in lks]
            avs, gs = [], []
            for pp in pr:
                cs = jnp.concatenate([c_ref[2 * pp], c_ref[2 * pp + 1]], axis=0)
                cf = jnp.sum(jnp.where(lane2 == kb, cs, 0.0), axis=-1, keepdims=True)
                a = jnp.exp(lbk[pp][0] + (sufs[pp] + cf))
                a = jnp.where(diag_mask, a, 0.0) if diag else a
                avs.append(a.astype(BF16))
                gs.append(a * das[pp])
            gpres = [_cumsum_mm(g, pm) for g in gs]
            dzs = []
            for pp in pr:
                carry = cg[pp]
                beta = jnp.exp(lbk[pp][0])
                dz = gs[pp] - beta * (gs[pp] + (gpres[pp] + jnp.concatenate([carry] * (t // 128), axis=1)))
                dzs.append((jnp.where(diag_mask, dz, 0.0) if diag else dz).astype(BF16))
                cg[pp] = carry + jnp.broadcast_to(jnp.sum(gs[pp], axis=-1, keepdims=True), (2 * t, 128))
            dqs = [_dot_nt(kt[pp, kb], dzs[pp]) for pp in pr]
            dks = [_dot(qqt[pp], dzs[pp]) for pp in pr]
            dvs = [_dot(ddt[pp], avs[pp]) for pp in pr]
            for pp in pr:
                dq_acc[pp] += dqs[pp]
                dkt[pp, kb] += dks[pp]
                dvt[pp, kb] += dvs[pp]

        def k_block(kb, _):
            tile(kb, False)
            return 0

        lax.fori_loop(0, i, k_block, 0)
        tile(i, True)
        for pp in pr:
            dq_ref[:, cols[pp]] = jnp.where(head0, dq_acc[pp, :, 0:t].T, dq_acc[pp, :, t:2 * t].T) * ATT_SCALE

        @pl.when(i == nq - 1)
        def _():
            def untranspose(j, _):
                rows = pl.ds(pl.multiple_of(j * t, t), t)
                for pp in pr:
                    dk_ref[rows, cols[pp]] = dkt[pp, j].T
                    dv_ref[rows, cols[pp]] = dvt[pp, j].T
                return 0

            lax.fori_loop(0, nq, untranspose, 0)

    width = 128 * npair
    kv = pl.BlockSpec((s_len, width), lambda p, i: (0, p))
    qo = pl.BlockSpec((t, width), lambda p, i: (i, p))
    tri = pl.BlockSpec((2 * t, t), lambda p, i: (0, 0))
    out = jax.ShapeDtypeStruct((s_len, D_GRP), F32)
    return pl.pallas_call(
        body, name="sb_bwd", grid=(4 // npair, nq),
        in_specs=[qo, kv, kv, qo, pl.BlockSpec((2 * npair, t, 128), lambda p, i: (p, i, 0)), tri, tri],
        out_specs=(qo, kv, kv), out_shape=(out, out, out),
        scratch_shapes=[pltpu.VMEM((npair, 2 * t, 128), BF16), pltpu.VMEM((npair, 2 * t, 128), BF16),
                        pltpu.VMEM((npair, 128, 2 * t), BF16), pltpu.VMEM((npair, 128, 2 * t), BF16),
                        pltpu.VMEM((npair, nq, 128, t), BF16),
                        pltpu.VMEM((npair, 128, 2 * t), F32),
                        pltpu.VMEM((npair, nq, 128, t), F32), pltpu.VMEM((npair, nq, 128, t), F32),
                        pltpu.VMEM((npair, 2 * t, 128), F32)],
        compiler_params=_params(),
    )(qs, ks, vs, dos, csaved, tri_suf, tri_pre)


def _attn_out(o_b, lse_b, o_sb, x2, wdil, wsb, b_g):
    s_len = x2.shape[0]
    tm = ROW_TILE

    def body(o1_ref, l1_ref, o4_ref, l4_ref, o16_ref, l16_ref, osb_ref, x_ref, wdil_ref, wsb_ref, w_ref,
             odil_ref, lse_ref, lse4_ref, lse16_ref, mixed_ref, x1_ref, stage, nat4, nat16):
        _merge_views((o4_ref, l4_ref), (o16_ref, l16_ref), stage, nat4, nat16)
        os_ = (o1_ref[...], _slab_group(nat4, 0), _slab_group(nat16, 0))
        ls = (l1_ref[...], _slab_group(nat4, 1), _slab_group(nat16, 1))
        mx = jnp.maximum(jnp.maximum(ls[0], ls[1]), ls[2])
        es = [jnp.exp(l - mx) for l in ls]
        den = es[0] + es[1] + es[2]
        o_dil = (es[0] * os_[0] + es[1] * os_[1] + es[2] * os_[2]) / den
        odil_ref[...] = o_dil
        lse = mx + jnp.log(den)
        lse_ref[...] = lse
        for p in range(4):
            nat4[p] = lse[:, 128 * p:128 * (p + 1)]
        _split_views(nat4.at[0:4], stage.at[0:4], (lse4_ref,), (lse16_ref,))
        halves = []
        for t, w_r in ((o_dil, wdil_ref), (osb_ref[...], wsb_ref)):
            r = lax.rsqrt(jnp.mean(t * t, axis=-1, keepdims=True) + EPS)
            halves.append(((t * r) * w_r[...]).astype(BF16))
        mixed = jnp.concatenate(halves, axis=1)
        mixed_ref[...] = mixed
        w = w_ref[...].reshape(D_MODEL, D_MODEL)
        x1_ref[...] = x_ref[...] + _dot(mixed, w)

    row = lambda w: pl.BlockSpec((tm, w), lambda i: (i, 0))
    return pl.pallas_call(
        body, name="attn_out", grid=(s_len // tm,),
        in_specs=[row(D_GRP)] * 2 + [_view_spec(tm, 4)] * 2 + [_view_spec(tm, 16)] * 2
        + [row(D_GRP), row(D_MODEL), _full((1, D_GRP)), _full((1, D_GRP)),
           pl.BlockSpec((N_DEV, OUT_SHARD, D_MODEL), lambda i: (0, 3, 0))],
        out_specs=(row(D_GRP), row(D_GRP), _view_spec(tm, 4), _view_spec(tm, 16), row(D_MODEL), row(D_MODEL)),
        out_shape=(jax.ShapeDtypeStruct((s_len, D_GRP), F32), jax.ShapeDtypeStruct((s_len, D_GRP), F32),
                   _view_shape(s_len, 4, F32), _view_shape(s_len, 16, F32),
                   jax.ShapeDtypeStruct((s_len, D_MODEL), BF16), jax.ShapeDtypeStruct((s_len, D_MODEL), F32)),
        scratch_shapes=[pltpu.VMEM((8, tm, 128), F32)] * 3,
        compiler_params=_params(),
    )(o_b[0], lse_b[0], o_b[1], lse_b[1], o_b[2], lse_b[2], o_sb, x2, wdil, wsb, b_g)


def _ffn_fwd(x1, wn2, tgt, a_g, b_g):
    s_len = x1.shape[0]
    tm = ROW_TILE
    ni = s_len // tm

    def body(x_ref, wn_ref, t_ref, wg_ref, wu_ref, wd_ref, g_ref, u_ref, h2_ref, dy_ref, loss_ref, acc):
        j = pl.program_id(1)

        @pl.when(j == 0)
        def _():
            xx = x_ref[...]
            r = lax.rsqrt(jnp.mean(xx * xx, axis=-1, keepdims=True) + EPS)
            h2_ref[...] = ((xx * r) * wn_ref[...]).astype(BF16)
            acc[...] = jnp.zeros_like(acc)

        h = h2_ref[...]
        g = _dot(h, wg_ref[0])
        u = _dot(h, wu_ref[0])
        g_ref[...] = g
        u_ref[...] = u
        act = (g * (1.0 / (1.0 + jnp.exp(-g)))) * u
        acc[...] += _dot(act.astype(BF16), wd_ref[0])

        @pl.when(j == N_DEV - 1)
        def _():
            err = (x_ref[...] + acc[...]) - t_ref[...]
            dy_ref[...] = err * (1.0 / D_MODEL)
            part = 0.5 * jnp.sum(jnp.mean(err * err, axis=-1, keepdims=True))
            loss_ref[...] = jnp.full((8, 128), part, F32)

    row = pl.BlockSpec((tm, D_MODEL), lambda i, j: (i, 0))
    hid = pl.BlockSpec((tm, FF_PAD), lambda i, j: (i, j))
    return pl.pallas_call(
        body, name="ffn_fwd", grid=(ni, N_DEV),
        in_specs=[row, pl.BlockSpec((1, D_MODEL), lambda i, j: (0, 0)), row,
                  pl.BlockSpec((1, D_MODEL, FF_PAD), lambda i, j: (j, 1, 0)),
                  pl.BlockSpec((1, D_MODEL, FF_PAD), lambda i, j: (j, 2, 0)),
                  pl.BlockSpec((1, FF_PAD, D_MODEL), lambda i, j: (j, 0, 0))],
        out_specs=(hid, hid, row, row, pl.BlockSpec((8, 128), lambda i, j: (i, 0))),
        out_shape=(jax.ShapeDtypeStruct((s_len, N_DEV * FF_PAD), F32),
                   jax.ShapeDtypeStruct((s_len, N_DEV * FF_PAD), F32),
                   jax.ShapeDtypeStruct((s_len, D_MODEL), BF16),
                   jax.ShapeDtypeStruct((s_len, D_MODEL), F32),
                   jax.ShapeDtypeStruct((ni * 8, 128), F32)),
        scratch_shapes=[pltpu.VMEM((tm, D_MODEL), F32)],
        compiler_params=_params(),
    )(x1, wn2, tgt, a_g, a_g, b_g)


def _ffn_bwd_dx(dy, g, u, a_g, b_g):
    s_len = dy.shape[0]
    tm = ROW_TILE

    def body(dy_ref, g_ref, u_ref, wg_ref, wu_ref, wd_ref, dg_ref, du_ref, act_ref, dh_ref, acc):
        j = pl.program_id(1)

        @pl.when(j == 0)
        def _():
            acc[...] = jnp.zeros_like(acc)

        gg, uu = g_ref[...], u_ref[...]
        da = _dot_nt(dy_ref[...].astype(BF16), wd_ref[0])
        sig = 1.0 / (1.0 + jnp.exp(-gg))
        silu = gg * sig
        act_ref[...] = (silu * uu).astype(BF16)
        du = (da * silu).astype(BF16)
        dg = (da * uu * (sig * (1.0 + gg * (1.0 - sig)))).astype(BF16)
        du_ref[...] = du
        dg_ref[...] = dg
        acc[...] += _dot_nt(dg, wg_ref[0]) + _dot_nt(du, wu_ref[0])

        @pl.when(j == N_DEV - 1)
        def _():
            dh_ref[...] = acc[...]

    row = pl.BlockSpec((tm, D_MODEL), lambda i, j: (i, 0))
    hid = pl.BlockSpec((tm, FF_PAD), lambda i, j: (i, j))
    hid_bf = jax.ShapeDtypeStruct((s_len, N_DEV * FF_PAD), BF16)
    return pl.pallas_call(
        body, name="ffn_bwd_dx", grid=(s_len // tm, N_DEV),
        in_specs=[row, hid, hid,
                  pl.BlockSpec((1, D_MODEL, FF_PAD), lambda i, j: (j, 1, 0)),
                  pl.BlockSpec((1, D_MODEL, FF_PAD), lambda i, j: (j, 2, 0)),
                  pl.BlockSpec((1, FF_PAD, D_MODEL), lambda i, j: (j, 0, 0))],
        out_specs=(hid, hid, hid, row),
        out_shape=(hid_bf, hid_bf, hid_bf, jax.ShapeDtypeStruct((s_len, D_MODEL), F32)),
        scratch_shapes=[pltpu.VMEM((tm, D_MODEL), F32)],
        compiler_params=_params(),
    )(dy, g, u, a_g, a_g, b_g)


def _ffn_bwd_dw(h2, dy, dg, du, act):
    s_len = h2.shape[0]
    tm = ROW_TILE
    ni = s_len // tm

    def body(h_ref, dy_ref, dg_ref, du_ref, act_ref, dwg_ref, dwu_ref, dwd_ref, ag, au, ad):
        i = pl.program_id(1)

        @pl.when(i == 0)
        def _():
            ag[...] = jnp.zeros_like(ag)
            au[...] = jnp.zeros_like(au)
            ad[...] = jnp.zeros_like(ad)

        h = h_ref[...]
        ag[...] += _dot_tn(h, dg_ref[...])
        au[...] += _dot_tn(h, du_ref[...])
        ad[...] += _dot_tn(act_ref[...], dy_ref[...].astype(BF16))

        @pl.when(i == ni - 1)
        def _():
            dwg_ref[0] = ag[...].astype(BF16)
            dwu_ref[0] = au[...].astype(BF16)
            dwd_ref[0] = ad[...].astype(BF16)

    row = pl.BlockSpec((tm, D_MODEL), lambda j, i: (i, 0))
    hid = pl.BlockSpec((tm, FF_PAD), lambda j, i: (i, j))
    col_w = pl.BlockSpec((1, D_MODEL, FF_PAD), lambda j, i: (j, 0, 0))
    row_w = pl.BlockSpec((1, FF_PAD, D_MODEL), lambda j, i: (j, 0, 0))
    return pl.pallas_call(
        body, name="ffn_bwd_dw", grid=(N_DEV, ni),
        in_specs=[row, row, hid, hid, hid], out_specs=(col_w, col_w, row_w),
        out_shape=(jax.ShapeDtypeStruct((N_DEV, D_MODEL, FF_PAD), BF16),
                   jax.ShapeDtypeStruct((N_DEV, D_MODEL, FF_PAD), BF16),
                   jax.ShapeDtypeStruct((N_DEV, FF_PAD, D_MODEL), BF16)),
        scratch_shapes=[pltpu.VMEM((D_MODEL, FF_PAD), F32), pltpu.VMEM((D_MODEL, FF_PAD), F32),
                        pltpu.VMEM((FF_PAD, D_MODEL), F32)],
        compiler_params=_params(),
    )(h2, dy, dg, du, act)


def _rms_bwd(dy, t, w):
    r = lax.rsqrt(jnp.mean(t * t, axis=-1, keepdims=True) + EPS)
    gw = dy * w
    dt = r * (gw - t * ((r * r) * jnp.mean(gw * t, axis=-1, keepdims=True)))
    return dt, dy * t * r


def _attn_out_bwd(dy, dh2, x1, wn2, b_g, mixed, o_dil, o_sb, wdil, wsb, bd512):
    s_len = dy.shape[0]
    tm = ROW_TILE
    ni = s_len // tm

    def body(dy_ref, dh_ref, x1_ref, wn_ref, w_ref, mixed_ref, odil_ref, osb_ref, wdil_ref, wsb_ref, bd_ref,
             dx1_ref, dodil_ref, delta_ref, dosb_ref, dwout_ref, dwn_ref, dwdil_ref, dwsb_ref,
             do4_ref, dl4_ref, do16_ref, dl16_ref, wacc, both, stage):
        i = pl.program_id(0)

        @pl.when(i == 0)
        def _():
            wacc[...] = jnp.zeros_like(wacc)
            dwn_ref[...] = jnp.zeros_like(dwn_ref)
            dwdil_ref[...] = jnp.zeros_like(dwdil_ref)
            dwsb_ref[...] = jnp.zeros_like(dwsb_ref)

        dnorm, dw_rows = _rms_bwd(dh_ref[...], x1_ref[...], wn_ref[...])
        dx1 = dy_ref[...] + dnorm
        dx1_ref[...] = dx1
        dwn_ref[...] += jnp.sum(dw_rows, axis=0, keepdims=True)
        dx1b = dx1.astype(BF16)
        w = w_ref[...].reshape(D_MODEL, D_MODEL)
        dmixed = _dot_nt(dx1b, w)
        wacc[...] += _dot_tn(mixed_ref[...], dx1b)
        o_dil = odil_ref[...]
        d_odil, dw_rows = _rms_bwd(dmixed[:, :D_GRP], o_dil, wdil_ref[...])
        dwdil_ref[...] += jnp.sum(dw_rows, axis=0, keepdims=True)
        dodil_ref[...] = d_odil.astype(BF16)
        delta = _mm_split(d_odil * o_dil, bd_ref[...])
        delta_ref[...] = delta
        for p in range(4):
            both[p] = d_odil[:, 128 * p:128 * (p + 1)]
            both[4 + p] = delta[:, 128 * p:128 * (p + 1)]
        _split_views(both, stage, (do4_ref, dl4_ref), (do16_ref, dl16_ref))
        d_osb, dw_rows = _rms_bwd(dmixed[:, D_GRP:], osb_ref[...], wsb_ref[...])
        dwsb_ref[...] += jnp.sum(dw_rows, axis=0, keepdims=True)
        dosb_ref[...] = d_osb.astype(BF16)

        @pl.when(i == ni - 1)
        def _():
            dwout_ref[...] = wacc[...].astype(BF16).reshape(N_DEV, OUT_SHARD, D_MODEL)

    row = lambda w: pl.BlockSpec((tm, w), lambda i: (i, 0))
    return pl.pallas_call(
        body, name="attn_out_bwd", grid=(ni,),
        in_specs=[row(D_MODEL), row(D_MODEL), row(D_MODEL), _full((1, D_MODEL)),
                  pl.BlockSpec((N_DEV, OUT_SHARD, D_MODEL), lambda i: (0, 3, 0)),
                  row(D_MODEL), row(D_GRP), row(D_GRP), _full((1, D_GRP)), _full((1, D_GRP)),
                  _full((D_GRP, D_GRP))],
        out_specs=(row(D_MODEL), row(D_GRP), row(D_GRP), row(D_GRP),
                   _full((N_DEV, OUT_SHARD, D_MODEL)), _full((1, D_MODEL)), _full((1, D_GRP)), _full((1, D_GRP)),
                   _view_spec(tm, 4), _view_spec(tm, 4), _view_spec(tm, 16), _view_spec(tm, 16)),
        out_shape=(jax.ShapeDtypeStruct((s_len, D_MODEL), F32), jax.ShapeDtypeStruct((s_len, D_GRP), BF16),
                   jax.ShapeDtypeStruct((s_len, D_GRP), F32), jax.ShapeDtypeStruct((s_len, D_GRP), BF16),
                   jax.ShapeDtypeStruct((N_DEV, OUT_SHARD, D_MODEL), BF16),
                   jax.ShapeDtypeStruct((1, D_MODEL), F32), jax.ShapeDtypeStruct((1, D_GRP), F32),
                   jax.ShapeDtypeStruct((1, D_GRP), F32),
                   _view_shape(s_len, 4, BF16), _view_shape(s_len, 4, F32),
                   _view_shape(s_len, 16, BF16), _view_shape(s_len, 16, F32)),
        scratch_shapes=[pltpu.VMEM((D_MODEL, D_MODEL), F32), pltpu.VMEM((8, tm, 128), F32),
                        pltpu.VMEM((8, tm, 128), F32)],
        compiler_params=_params(),
    )(dy, dh2, x1, wn2, b_g, mixed, o_dil, o_sb, wdil, wsb, bd512)


def _qkv_bwd(dq_b, dk_b, dv_b, dqs, dks, dvs, qraw, kraw, cos2, sin2, qnw, knw, bd):
    s_len = qraw.shape[0]
    tm = ROW_TILE
    ni = s_len // tm

    def body(dq1, dk1, dv1, dq4, dk4, dv4, dq16, dk16, dv16, dqs_ref, dks_ref, dvs_ref,
             qraw_ref, kraw_ref, cos_ref, sin_ref, qnw_ref, knw_ref, bd_ref,
             dproj_ref, dqn_ref, dkn_ref, stage, nat4, nat16):
        i = pl.program_id(0)

        @pl.when(i == 0)
        def _():
            dqn_ref[...] = jnp.zeros_like(dqn_ref)
            dkn_ref[...] = jnp.zeros_like(dkn_ref)

        _merge_views((dq4, dk4, dv4), (dq16, dk16, dv16), stage, nat4, nat16)
        cos_t, sin_t, bdm = cos_ref[...], sin_ref[...], bd_ref[...]
        for grp, (part1, raw_ref, nw_ref, dn_ref) in enumerate(((dq1, qraw_ref, qnw_ref, dqn_ref),
                                                                (dk1, kraw_ref, knw_ref, dkn_ref))):
            dn_acc = 0.0
            for p in range(4):
                cols = slice(128 * p, 128 * (p + 1))
                d_rope = part1[:, cols] + nat4[4 * grp + p] + nat16[4 * grp + p]
                d_norm = d_rope * cos_t + _swap_halves(d_rope * sin_t)
                t = raw_ref[:, cols]
                w = nw_ref[...]
                r = lax.rsqrt(_mm_split(t * t, bdm) * (1.0 / HEAD_DIM) + EPS)
                gw = d_norm * w
                corr = _mm_split(gw * t, bdm) * (1.0 / HEAD_DIM)
                dt = r * (gw - t * ((r * r) * corr))
                dn_acc = dn_acc + jnp.sum(d_norm * t * r, axis=0, keepdims=True)
                dproj_ref[:, D_GRP * grp + 128 * p:D_GRP * grp + 128 * (p + 1)] = dt.astype(BF16)
            dn_ref[...] += dn_acc
        dproj_ref[:, 2 * D_GRP:3 * D_GRP] = (dv1[...] + _slab_group(nat4, 2) + _slab_group(nat16, 2)).astype(BF16)
        dproj_ref[:, 3 * D_GRP:4 * D_GRP] = dqs_ref[...].astype(BF16)
        dproj_ref[:, 4 * D_GRP:5 * D_GRP] = dks_ref[...].astype(BF16)
        dproj_ref[:, 5 * D_GRP:6 * D_GRP] = dvs_ref[...].astype(BF16)

    row = lambda w: pl.BlockSpec((tm, w), lambda i: (i, 0))
    return pl.pallas_call(
        body, name="qkv_bwd", grid=(ni,),
        in_specs=[row(D_GRP)] * 3 + [_view_spec(tm, 4)] * 3 + [_view_spec(tm, 16)] * 3 + [row(D_GRP)] * 5
        + [row(128), row(128), _full((1, 128)), _full((1, 128)), _full((128, 128))],
        out_specs=(row(D_IN), _full((1, 128)), _full((1, 128))),
        out_shape=(jax.ShapeDtypeStruct((s_len, D_IN), BF16), jax.ShapeDtypeStruct((1, 128), F32),
                   jax.ShapeDtypeStruct((1, 128), F32)),
        scratch_shapes=[pltpu.VMEM((12, tm, 128), F32)] * 3,
        compiler_params=_params(),
    )(dq_b[0], dk_b[0], dv_b[0], dq_b[1], dk_b[1], dv_b[1], dq_b[2], dk_b[2], dv_b[2],
      dqs, dks, dvs, qraw, kraw, cos2, sin2, qnw, knw, bd)


def _in_bwd_dx(dproj, a_g, x2, dx1, wn1):
    s_len = x2.shape[0]
    tm = ROW_TILE
    ni = s_len // tm

    def body(dp_ref, w_ref, x_ref, dx1_ref, wn_ref, gx_ref, dwn_ref):
        i = pl.program_id(0)

        @pl.when(i == 0)
        def _():
            dwn_ref[...] = jnp.zeros_like(dwn_ref)

        dh = 0.0
        for d in range(N_DEV):
            dh = dh + _dot_nt(dp_ref[:, IN_SHARD * d:IN_SHARD * (d + 1)], w_ref[d])
        dnorm, dw_rows = _rms_bwd(dh, x_ref[...], wn_ref[...])
        gx_ref[...] = dx1_ref[...] + dnorm
        dwn_ref[...] += jnp.sum(dw_rows, axis=0, keepdims=True)

    row = lambda w: pl.BlockSpec((tm, w), lambda i: (i, 0))
    return pl.pallas_call(
        body, name="in_bwd_dx", grid=(ni,),
        in_specs=[row(D_IN), pl.BlockSpec((N_DEV, D_MODEL, IN_SHARD), lambda i: (0, 0, 0)),
                  row(D_MODEL), row(D_MODEL), _full((1, D_MODEL))],
        out_specs=(row(D_MODEL), _full((1, D_MODEL))),
        out_shape=(jax.ShapeDtypeStruct((s_len, D_MODEL), F32), jax.ShapeDtypeStruct((1, D_MODEL), F32)),
        compiler_params=_params(),
    )(dproj, a_g, x2, dx1, wn1)


def _in_bwd_dw(h1, dproj):
    s_len = h1.shape[0]
    tm = ROW_TILE
    ni = s_len // tm

    def body(h_ref, dp_ref, dw_ref, acc):
        i = pl.program_id(1)

        @pl.when(i == 0)
        def _():
            acc[...] = jnp.zeros_like(acc)

        acc[...] += _dot_tn(h_ref[...], dp_ref[...])

        @pl.when(i == ni - 1)
        def _():
            dw_ref[0] = acc[...].astype(BF16)

    return pl.pallas_call(
        body, name="in_bwd_dw", grid=(N_DEV, ni),
        in_specs=[pl.BlockSpec((tm, D_MODEL), lambda d, i: (i, 0)),
                  pl.BlockSpec((tm, IN_SHARD), lambda d, i: (i, d))],
        out_specs=pl.BlockSpec((1, D_MODEL, IN_SHARD), lambda d, i: (d, 0, 0)),
        out_shape=jax.ShapeDtypeStruct((N_DEV, D_MODEL, IN_SHARD), BF16),
        scratch_shapes=[pltpu.VMEM((D_MODEL, IN_SHARD), F32)],
        compiler_params=_params(),
    )(h1, dproj)


def _adamw(recv, w, m, v):
    rows, cols = w.shape
    tr = 128 if rows % 128 == 0 else rows

    def body(p_ref, w_ref, m_ref, v_ref, g_ref, d_ref, nm_ref, nv_ref):
        g = p_ref[0].astype(F32)
        for s in range(1, N_DEV):
            g = g + p_ref[s].astype(F32)
        m_new = ADAM_B1 * m_ref[...] + (1.0 - ADAM_B1) * g
        v_new = ADAM_B2 * v_ref[...] + (1.0 - ADAM_B2) * (g * g)
        m_hat = m_new / (1.0 - ADAM_B1 ** ADAM_STEP)
        v_hat = v_new / (1.0 - ADAM_B2 ** ADAM_STEP)
        g_ref[...] = g
        d_ref[...] = -ADAM_LR * (m_hat / (jnp.sqrt(v_hat) + ADAM_EPS) + ADAM_WD * w_ref[...])
        nm_ref[...] = m_new
        nv_ref[...] = v_new

    blk = pl.BlockSpec((tr, cols), lambda i: (i, 0))
    out = jax.ShapeDtypeStruct((rows, cols), F32)
    return pl.pallas_call(
        body, name=f"adamw_{rows}x{cols}", grid=(rows // tr,),
        in_specs=[pl.BlockSpec((N_DEV, tr, cols), lambda i: (0, i, 0)), blk, blk, blk],
        out_specs=(blk,) * 4, out_shape=(out,) * 4,
        compiler_params=_params(),
    )(recv, w, m, v)


def _rope_tables(s_len):
    pos = jnp.arange(s_len, dtype=F32)
    inv_freq = ROPE_THETA ** (-jnp.arange(0, HEAD_DIM, 2, dtype=F32) / HEAD_DIM)
    ang = pos[:, None] * inv_freq[None, :]
    cos, sin = jnp.cos(ang), jnp.sin(ang)
    cos2 = jnp.concatenate([cos, cos, cos, cos], axis=1)
    sin2 = jnp.concatenate([-sin, sin, -sin, sin], axis=1)
    return cos2, sin2


def _block_diag_ones(n):
    i = jnp.arange(n)
    return (i[:, None] // HEAD_DIM == i[None, :] // HEAD_DIM).astype(BF16)


def _pad_cols(t):
    return jnp.pad(t, ((0, 0), (0, FF_PAD - FF_SHARD)))


def _pad_rows(t):
    return jnp.pad(t, ((0, FF_PAD - FF_SHARD), (0, 0)))


def _pack_small(n1, n2, ndil, nsb, nq, nk):
    pad = lambda t: jnp.pad(t.reshape(1, HEAD_DIM), ((0, 0), (0, 128 - HEAD_DIM)))
    rows = [n1.reshape(8, 128), n2.reshape(8, 128), ndil.reshape(4, 128), nsb.reshape(4, 128),
            pad(nq), pad(nk), jnp.zeros((6, 128), F32)]
    return jnp.concatenate(rows, axis=0)


def _unpack_small(t):
    return (t[0:8].reshape(1, D_MODEL), t[8:16].reshape(1, D_MODEL), t[16:20].reshape(1, D_GRP),
            t[20:24].reshape(1, D_GRP), t[24:25, :HEAD_DIM], t[25:26, :HEAD_DIM])


def kernel(x, attn_norm_w, w_in, q_norm_w, k_norm_w, dil_out_norm_w, sb_out_norm_w, w_out, ffn_norm_w, w_gate, w_up, w_down, loss_target, m_attn_norm_w, m_w_in, m_q_norm_w, m_k_norm_w, m_dil_out_norm_w, m_sb_out_norm_w, m_w_out, m_ffn_norm_w, m_w_gate, m_w_up, m_w_down, v_attn_norm_w, v_w_in, v_q_norm_w, v_k_norm_w, v_dil_out_norm_w, v_sb_out_norm_w, v_w_out, v_ffn_norm_w, v_w_gate, v_w_up, v_w_down):
    s_len = x.shape[1]
    x2, tgt = x[0], loss_target[0]

    a_loc = jnp.concatenate([w_in[0], _pad_cols(w_gate[0]), _pad_cols(w_up[0])], axis=0).astype(BF16)
    b_loc = jnp.concatenate([_pad_rows(w_down[0]), w_out[0]], axis=0).astype(BF16)
    a_g, b_g = _gather_weights(a_loc, b_loc)

    cos2, sin2 = _rope_tables(s_len)
    bd128, bd512 = _block_diag_ones(128), _block_diag_ones(D_GRP)
    idx = jnp.arange(SB_TILE)
    tri_suf = (idx[:, None] > idx[None, :]).astype(BF16)
    tri_pre = (idx[:, None] < idx[None, :]).astype(BF16)
    tri_suf = jnp.concatenate([tri_suf, tri_suf], axis=0)
    tri_pre = jnp.concatenate([tri_pre, tri_pre], axis=0)
    qnw2 = jnp.concatenate([q_norm_w, q_norm_w], axis=1)
    knw2 = jnp.concatenate([k_norm_w, k_norm_w], axis=1)

    (h1, qraw, kraw, q, k, va, qs, ks, vs,
     q4, k4, v4, q16, k16, v16) = _attn_in(x2, attn_norm_w, a_g, cos2, sin2, qnw2, knw2, bd128)
    qkv_views = {1: (q, k, va), 4: (q4, k4, v4), 16: (q16, k16, v16)}
    o_b, lse_b = [], []
    for r in DILATIONS:
        o, lse = _dil_fwd(*qkv_views[r], r)
        o_b.append(o)
        lse_b.append(lse)
    o_sb, c_sb = _sb_fwd(qs, ks, vs, tri_suf)
    o_dil, lse_tot, lse4, lse16, mixed, x1 = _attn_out(o_b, lse_b, o_sb, x2, dil_out_norm_w, sb_out_norm_w, b_g)
    g, u, h2, dy, loss_parts = _ffn_fwd(x1, ffn_norm_w, tgt, a_g, b_g)
    loss = lax.psum(jnp.sum(loss_parts[::8, 0]), ("x", "y", "c"))

    dg, du, act, dh2 = _ffn_bwd_dx(dy, g, u, a_g, b_g)
    dwg, dwu, dwd = _ffn_bwd_dw(h2, dy, dg, du, act)
    (dx1, do_dil, delta, do_sb, dwout, dn2, dndil, dnsb, do4, dl4, do16, dl16) = _attn_out_bwd(
        dy, dh2, x1, ffn_norm_w, b_g, mixed, o_dil, o_sb, dil_out_norm_w, sb_out_norm_w, bd512)
    dqs, dks, dvs = _sb_bwd(qs, ks, vs, do_sb, c_sb, tri_suf, tri_pre)
    cot_views = {1: (do_dil, lse_tot, delta), 4: (do4, lse4, dl4), 16: (do16, lse16, dl16)}
    dq_b, dk_b, dv_b = [], [], []
    for r in DILATIONS:
        dq, dk, dv = _dil_bwd(*qkv_views[r], *cot_views[r], r)
        dq_b.append(dq)
        dk_b.append(dk)
        dv_b.append(dv)
    dproj, dqn2, dkn2 = _qkv_bwd(dq_b, dk_b, dv_b, dqs, dks, dvs, qraw, kraw, cos2, sin2, qnw2, knw2, bd128)
    grad_x, dn1 = _in_bwd_dx(dproj, a_g, x2, dx1, attn_norm_w)
    dwin = _in_bwd_dw(h1, dproj)
    dqn = dqn2[:, :HEAD_DIM] + dqn2[:, HEAD_DIM:]
    dkn = dkn2[:, :HEAD_DIM] + dkn2[:, HEAD_DIM:]

    small = _pack_small(dn1, dn2, dndil, dnsb, dqn, dkn)
    r_in, r_gate, r_up, r_down, r_out, r_small = _exchange_grads([dwin, dwg, dwu, dwd, dwout], small)
    big = {
        "w_in": _adamw(r_in, w_in[0], m_w_in[0], v_w_in[0]),
        "w_gate": tuple(t[:, :FF_SHARD] for t in _adamw(r_gate, _pad_cols(w_gate[0]), _pad_cols(m_w_gate[0]), _pad_cols(v_w_gate[0]))),
        "w_up": tuple(t[:, :FF_SHARD] for t in _adamw(r_up, _pad_cols(w_up[0]), _pad_cols(m_w_up[0]), _pad_cols(v_w_up[0]))),
        "w_down": tuple(t[:FF_SHARD] for t in _adamw(r_down, _pad_rows(w_down[0]), _pad_rows(m_w_down[0]), _pad_rows(v_w_down[0]))),
        "w_out": _adamw(r_out, w_out[0], m_w_out[0], v_w_out[0]),
    }
    packs = [_pack_small(*ts) for ts in (
        (attn_norm_w, ffn_norm_w, dil_out_norm_w, sb_out_norm_w, q_norm_w, k_norm_w),
        (m_attn_norm_w, m_ffn_norm_w, m_dil_out_norm_w, m_sb_out_norm_w, m_q_norm_w, m_k_norm_w),
        (v_attn_norm_w, v_ffn_norm_w, v_dil_out_norm_w, v_sb_out_norm_w, v_q_norm_w, v_k_norm_w))]
    small_out = [_unpack_small(t) for t in _adamw(r_small, *packs)]
    names = ["attn_norm_w", "w_in", "q_norm_w", "k_norm_w", "dil_out_norm_w", "sb_out_norm_w", "w_out",
             "ffn_norm_w", "w_gate", "w_up", "w_down"]
    small_pos = {"attn_norm_w": 0, "ffn_norm_w": 1, "dil_out_norm_w": 2, "sb_out_norm_w": 3,
                 "q_norm_w": 4, "k_norm_w": 5}
    outs = [loss, grad_x[None]]
    for kind in range(4):
        for name in names:
            if name in small_pos:
                outs.append(small_out[kind][small_pos[name]])
            else:
                outs.append(big[name][kind][None])
    return tuple(outs)
```

```python
import functools

import jax
import jax.numpy as jnp
from jax import lax
from jax.experimental import pallas as pl
from jax.experimental.pallas import tpu as pltpu

F32 = jnp.float32
BF16 = jnp.bfloat16

N_DEV = 8
D_MODEL = 1024
HEAD_DIM = 64
D_GRP = 512
D_IN = 6 * D_GRP
IN_SHARD = D_IN // N_DEV
FF_SHARD = 352
FF_PAD = 384
OUT_SHARD = D_MODEL // N_DEV
BLOCK = 128
DILATIONS = (1, 4, 16)
ROPE_THETA = 10000.0
EPS = 1e-6
ATT_SCALE = HEAD_DIM ** -0.5
NEG = -1e30

ADAM_LR = 0.001
ADAM_B1 = 0.9
ADAM_B2 = 0.999
ADAM_EPS = 1e-08
ADAM_WD = 0.01
ADAM_STEP = 10

SB_TILE = 256
SB_PAIRS = 4
SB_BWD_PAIRS = 2
ROW_TILE = 512
VMEM_LIMIT = 56 * 1024 * 1024
MESH = pl.DeviceIdType.MESH


def _dot(a, b):
    return jnp.dot(a, b, preferred_element_type=F32)


def _dot_nt(a, b):
    return lax.dot_general(a, b, (((1,), (1,)), ((), ())), preferred_element_type=F32)


def _dot_tn(a, b):
    return lax.dot_general(a, b, (((0,), (0,)), ((), ())), preferred_element_type=F32)


def _mm_split(t, m):
    hi = t.astype(BF16)
    lo = (t - hi.astype(F32)).astype(BF16)
    return _dot(hi, m) + _dot(lo, m)


def _params(**kw):
    return pltpu.CompilerParams(vmem_limit_bytes=VMEM_LIMIT, **kw)


def _full(shape):
    nd = len(shape)
    return pl.BlockSpec(shape, lambda *_: (0,) * nd)


def _view_shape(s_len, r, dtype):
    return jax.ShapeDtypeStruct((s_len // r, r * D_GRP), dtype)


def _view_spec(tm, r):
    return pl.BlockSpec((tm // r, r * D_GRP), lambda i: (i, 0))


def _swap_halves(t):
    lane = lax.broadcasted_iota(jnp.int32, t.shape, 1)
    first = (lane & 32) == 0
    return jnp.where(first, pltpu.roll(t, 96, 1), pltpu.roll(t, 32, 1))


def _log_sigmoid(z):
    return jnp.minimum(z, 0.0) - jnp.log(1.0 + jnp.exp(-jnp.abs(z)))


def _log_sigmoid_pair(z):
    neg_abs = lax.bitcast_convert_type(lax.bitcast_convert_type(z, jnp.uint32) | jnp.uint32(0x80000000), F32)
    lb = jnp.minimum(z, 0.0) - jnp.log(1.0 + jnp.exp(neg_abs))
    return lb, lb - z


def _cumsum_mm(t, tri2):
    hi = lax.bitcast_convert_type(lax.bitcast_convert_type(t, jnp.uint32) & jnp.uint32(0xFFFF0000), F32)
    lhs = jnp.concatenate([hi.astype(BF16), (t - hi).astype(BF16)], axis=1)
    return _dot(lhs, tri2)


def _split_views(src_ref, stage_ref, views4, views16):
    slabs, n, _ = src_ref.shape
    n4, n16 = n // 4, n // 16
    for j in range(slabs):
        g, lanes = j // 4, 128 * (j % 4)
        src, stage = src_ref.at[j], stage_ref.at[j]
        for c4 in range(4):
            blk = src[pl.ds(c4, n4, stride=4), :]
            stage[n4 * c4:n4 * (c4 + 1), :] = blk
            col = D_GRP * c4 + lanes
            views4[g][:, col:col + 128] = blk.astype(views4[g].dtype)
        for c4 in range(4):
            for c1 in range(4):
                blk = stage[pl.ds(n4 * c4 + c1, n16, stride=4), :]
                col = D_GRP * (4 * c1 + c4) + lanes
                views16[g][:, col:col + 128] = blk.astype(views16[g].dtype)


def _merge_views(views4, views16, stage_ref, dst4_ref, dst16_ref):
    slabs, n, _ = dst4_ref.shape
    n4, n16 = n // 4, n // 16
    for j in range(slabs):
        g, lanes = j // 4, 128 * (j % 4)
        dst4, dst16, stage = dst4_ref.at[j], dst16_ref.at[j], stage_ref.at[j]
        for c4 in range(4):
            col = D_GRP * c4 + lanes
            dst4[pl.ds(c4, n4, stride=4), :] = views4[g][:, col:col + 128].astype(F32)
            for c1 in range(4):
                col = D_GRP * (4 * c1 + c4) + lanes
                stage[pl.ds(n4 * c4 + c1, n16, stride=4), :] = views16[g][:, col:col + 128].astype(F32)
        for c4 in range(4):
            dst16[pl.ds(c4, n4, stride=4), :] = stage[n4 * c4:n4 * (c4 + 1), :]


def _slab_group(ref, g):
    return jnp.concatenate([ref[4 * g + p] for p in range(4)], axis=1)


def _mesh_pos():
    return lax.axis_index("x"), lax.axis_index("y"), lax.axis_index("c")


def _flat_index(p):
    return 4 * p[0] + 2 * p[1] + p[2]


def _gather_weights(shards):
    n_arr = len(shards)

    def body(*refs):
        srcs, outs = refs[:n_arr], refs[n_arr:2 * n_arr]
        send_sems, recv_sems, local_sems = refs[2 * n_arr:]
        x, y, c = _mesh_pos()
        me, sibling = (x, y, c), (x, y, 1 - c)
        chips = [(1 - x, y), (x, 1 - y), (1 - x, 1 - y)]

        def copy(arr, k, block, to, own=False):
            dst = outs[arr].at[_flat_index(block)]
            return pltpu.make_async_remote_copy(
                src_ref=srcs[arr] if own else dst, dst_ref=dst,
                send_sem=send_sems.at[arr, k], recv_sem=recv_sems.at[arr, k],
                device_id=to, device_id_type=MESH)

        for arr in range(n_arr):
            mine = pltpu.make_async_copy(srcs[arr], outs[arr].at[_flat_index(me)], local_sems.at[arr])
            mine.start()
            first = [copy(arr, 0, me, sibling, own=True)]
            first += [copy(arr, 1 + j, me, (*chip, c), own=True) for j, chip in enumerate(chips)]
            for cp in first:
                cp.start()
        for arr in range(n_arr):
            passed = [copy(arr, 4 + j, (*chip, c), sibling) for j, chip in enumerate(chips)]
            for j, chip in enumerate(chips):
                copy(arr, 1 + j, (*chip, c), me).wait_recv()
                passed[j].start()
        for arr in range(n_arr):
            copy(arr, 0, sibling, me).wait_recv()
            for j, chip in enumerate(chips):
                copy(arr, 4 + j, (*chip, 1 - c), me).wait_recv()
            for k in range(7):
                copy(arr, k, me, me).wait_send()
            pltpu.make_async_copy(srcs[arr], outs[arr].at[_flat_index(me)], local_sems.at[arr]).wait()

    any_spec = pl.BlockSpec(memory_space=pl.ANY)
    return pl.pallas_call(
        body, name="gather_weights",
        out_shape=tuple(jax.ShapeDtypeStruct((N_DEV,) + s.shape, s.dtype) for s in shards),
        in_specs=[any_spec] * n_arr, out_specs=(any_spec,) * n_arr,
        scratch_shapes=[pltpu.SemaphoreType.DMA((n_arr, 7)), pltpu.SemaphoreType.DMA((n_arr, 7)),
                        pltpu.SemaphoreType.DMA((n_arr,))],
        compiler_params=pltpu.CompilerParams(has_side_effects=True),
    )(*shards)


_HBM_SPEC = pl.BlockSpec(memory_space=pltpu.HBM)
_SEM_SPEC = pl.BlockSpec(memory_space=pltpu.SEMAPHORE)
_DATAFLOW = pltpu.SideEffectType.DATAFLOW_SIDE_EFFECTING


def _peer_list(x, y, c):
    return [(1 - x if m & 4 else x, 1 - y if m & 2 else y, 1 - c if m & 1 else c) for m in range(1, N_DEV)]


def _spread_copies(src_refs, land_refs, send_sems, recv_sems, blockwise):
    x, y, c = _mesh_pos()
    my_idx = _flat_index((x, y, c))
    copies = []
    for a, (src, land) in enumerate(zip(src_refs, land_refs)):
        for k, peer in enumerate(_peer_list(x, y, c)):
            copies.append(pltpu.make_async_remote_copy(
                src_ref=src.at[_flat_index(peer)] if blockwise else src, dst_ref=land.at[my_idx],
                send_sem=send_sems.at[(N_DEV - 1) * a + k], recv_sem=recv_sems.at[(N_DEV - 1) * a + k],
                device_id=peer, device_id_type=MESH))
    return copies


def _spread_start(srcs, lands, blockwise, name):
    n = len(srcs)

    def body(*refs):
        for cp in _spread_copies(refs[:n], refs[n:2 * n], refs[2 * n], refs[2 * n + 1], blockwise):
            cp.start()
        token = refs[-1]
        token[...] = jnp.zeros_like(token)

    hbm = lambda t: pltpu.HBM(t.shape, t.dtype)
    sems = pltpu.SemaphoreType.DMA((n * (N_DEV - 1),))
    outs = pl.pallas_call(
        body, name=name,
        out_shape=(sems, sems) + tuple(hbm(t) for t in srcs) + tuple(hbm(t) for t in lands)
        + (jax.ShapeDtypeStruct((8, 128), F32),),
        in_specs=[_HBM_SPEC] * (2 * n),
        out_specs=(_SEM_SPEC, _SEM_SPEC) + (_HBM_SPEC,) * (2 * n) + (pl.BlockSpec(memory_space=pltpu.VMEM),),
        input_output_aliases={i: 2 + i for i in range(2 * n)},
        compiler_params=pltpu.CompilerParams(has_side_effects=_DATAFLOW),
    )(*[pltpu.with_memory_space_constraint(t, pltpu.HBM) for t in list(srcs) + list(lands)])
    return outs[0], outs[1], outs[2:2 + n], outs[2 + n:2 + 2 * n], outs[-1]


def _spread_wait(send_sems, recv_sems, srcs, lands, after, blockwise, name):
    n = len(srcs)

    def body(*refs):
        for cp in _spread_copies(refs[:n], refs[n:2 * n], refs[2 * n], refs[2 * n + 1], blockwise):
            cp.wait_send()
            cp.wait_recv()

    hbm = lambda t: pltpu.HBM(t.shape, t.dtype)
    outs = pl.pallas_call(
        body, name=name,
        out_shape=tuple(hbm(t) for t in srcs) + tuple(hbm(t) for t in lands),
        in_specs=[_HBM_SPEC] * (2 * n) + [_SEM_SPEC, _SEM_SPEC, pl.BlockSpec(memory_space=pl.ANY)],
        out_specs=(_HBM_SPEC,) * (2 * n),
        input_output_aliases={i: i for i in range(2 * n)},
        compiler_params=pltpu.CompilerParams(has_side_effects=_DATAFLOW),
    )(*srcs, *lands, send_sems, recv_sems, after)
    return outs[n:]


def _exchange_grads(parts, small):
    n_arr = len(parts)

    def body(*refs):
        ins, outs = refs[:n_arr + 1], refs[n_arr + 1:2 * (n_arr + 1)]
        send_sems, recv_sems, local_sems = refs[2 * (n_arr + 1):]
        x, y, c = _mesh_pos()
        me = (x, y, c)
        my_idx = _flat_index(me)
        peers = []
        for m in range(1, N_DEV):
            peers.append((1 - x if m & 4 else x, 1 - y if m & 2 else y, 1 - c if m & 1 else c))

        def src_block(arr, dev):
            return ins[arr] if arr == n_arr else ins[arr].at[_flat_index(dev)]

        def copy(arr, k):
            return pltpu.make_async_remote_copy(
                src_ref=src_block(arr, peers[k]), dst_ref=outs[arr].at[my_idx],
                send_sem=send_sems.at[arr, k], recv_sem=recv_sems.at[arr, k],
                device_id=peers[k], device_id_type=MESH)

        def local(arr):
            return pltpu.make_async_copy(src_block(arr, me), outs[arr].at[my_idx], local_sems.at[arr])

        for arr in range(n_arr + 1):
            local(arr).start()
            for k in range(N_DEV - 1):
                copy(arr, k).start()
        for arr in range(n_arr + 1):
            for k in range(N_DEV - 1):
                cp = copy(arr, k)
                cp.wait_send()
                cp.wait_recv()
            local(arr).wait()

    any_spec = pl.BlockSpec(memory_space=pl.ANY)
    out_shape = tuple(jax.ShapeDtypeStruct(p.shape, p.dtype) for p in parts)
    out_shape += (jax.ShapeDtypeStruct((N_DEV,) + small.shape, small.dtype),)
    return pl.pallas_call(
        body, name="exchange_grads",
        out_shape=out_shape,
        in_specs=[any_spec] * (n_arr + 1), out_specs=(any_spec,) * (n_arr + 1),
        scratch_shapes=[pltpu.SemaphoreType.DMA((n_arr + 1, N_DEV - 1)),
                        pltpu.SemaphoreType.DMA((n_arr + 1, N_DEV - 1)),
                        pltpu.SemaphoreType.DMA((n_arr + 1,))],
        compiler_params=pltpu.CompilerParams(has_side_effects=True),
    )(*parts, small)


def _head_norm(t, w128, bd):
    ms = _mm_split(t * t, bd) * (1.0 / HEAD_DIM)
    r = lax.rsqrt(ms + EPS)
    return (t * r) * w128, r


def _attn_in(x2, wn1, a_g, cos2, sin2, qnw, knw, bd):
    s_len = x2.shape[0]
    tm = ROW_TILE

    def body(x_ref, wn_ref, w_ref, cos_ref, sin_ref, qnw_ref, knw_ref, bd_ref,
             h1_ref, qraw_ref, kraw_ref, q_ref, k_ref, va_ref, qs_ref, ks_ref, vs_ref,
             q4_ref, k4_ref, v4_ref, q16_ref, k16_ref, v16_ref, proj, slabs, stage):
        xx = x_ref[...]
        r = lax.rsqrt(jnp.mean(xx * xx, axis=-1, keepdims=True) + EPS)
        h = ((xx * r) * wn_ref[...]).astype(BF16)
        h1_ref[...] = h
        for d in range(N_DEV):
            proj[:, IN_SHARD * d:IN_SHARD * (d + 1)] = _dot(h, w_ref[d])
        cos_t, sin_t, bdm = cos_ref[...], sin_ref[...], bd_ref[...]
        for grp, (raw_ref, rope_ref, nw_ref) in enumerate(((qraw_ref, q_ref, qnw_ref),
                                                           (kraw_ref, k_ref, knw_ref))):
            for p in range(4):
                cols = slice(D_GRP * grp + 128 * p, D_GRP * grp + 128 * (p + 1))
                t = proj[:, cols]
                raw_ref[:, 128 * p:128 * (p + 1)] = t
                yn, _ = _head_norm(t, nw_ref[...], bdm)
                roped = yn * cos_t + _swap_halves(yn) * sin_t
                slabs[4 * grp + p] = roped
                rope_ref[:, 128 * p:128 * (p + 1)] = roped.astype(BF16)
        for p in range(4):
            slabs[8 + p] = proj[:, 2 * D_GRP + 128 * p:2 * D_GRP + 128 * (p + 1)]
        for grp, ref in ((2, va_ref), (3, qs_ref), (4, ks_ref), (5, vs_ref)):
            ref[...] = proj[:, D_GRP * grp:D_GRP * (grp + 1)].astype(BF16)
        _split_views(slabs, stage, (q4_ref, k4_ref, v4_ref), (q16_ref, k16_ref, v16_ref))

    row = lambda w: pl.BlockSpec((tm, w), lambda i: (i, 0))
    grp_bf = jax.ShapeDtypeStruct((s_len, D_GRP), BF16)
    grp_f32 = jax.ShapeDtypeStruct((s_len, D_GRP), F32)
    return pl.pallas_call(
        body, name="attn_in", grid=(s_len // tm,),
        in_specs=[row(D_MODEL), _full((1, D_MODEL)),
                  pl.BlockSpec((N_DEV, D_MODEL, IN_SHARD), lambda i: (0, 0, 0)),
                  row(128), row(128), _full((1, 128)), _full((1, 128)), _full((128, 128))],
        out_specs=(row(D_MODEL),) + (row(D_GRP),) * 8 + (_view_spec(tm, 4),) * 3 + (_view_spec(tm, 16),) * 3,
        out_shape=(jax.ShapeDtypeStruct((s_len, D_MODEL), BF16), grp_f32, grp_f32) + (grp_bf,) * 6
        + (_view_shape(s_len, 4, BF16),) * 3 + (_view_shape(s_len, 16, BF16),) * 3,
        scratch_shapes=[pltpu.VMEM((tm, D_IN), F32), pltpu.VMEM((12, tm, 128), F32), pltpu.VMEM((12, tm, 128), F32)],
        compiler_params=_params(),
    )(x2, wn1, a_g, cos2, sin2, qnw, knw, bd)


def _band_mask(n):
    i = lax.broadcasted_iota(jnp.int32, (BLOCK, 2 * BLOCK), 0)
    j = lax.broadcasted_iota(jnp.int32, (BLOCK, 2 * BLOCK), 1)
    dist = i + BLOCK - j
    return (dist >= 0) & (dist <= BLOCK) & ((n - 1) * BLOCK + j >= 0)


def _dil_fwd(qv, kv, vv, r):
    sub_len = qv.shape[0]
    nb = sub_len // BLOCK

    def body(q_ref, kp_ref, kc_ref, vp_ref, vc_ref, o_ref, lse_ref):
        n = pl.program_id(1)
        valid = _band_mask(n)
        lane = lax.broadcasted_iota(jnp.int32, (BLOCK, 128), 1)
        head0 = lane < HEAD_DIM
        for p in range(4):
            cols = slice(128 * p, 128 * (p + 1))
            q2 = q_ref[:, cols]
            kk = jnp.concatenate([kp_ref[:, cols], kc_ref[:, cols]], axis=0)
            vv2 = jnp.concatenate([vp_ref[:, cols], vc_ref[:, cols]], axis=0)
            res = []
            for h in range(2):
                qh = jnp.where(head0, q2, 0) if h == 0 else jnp.where(head0, 0, q2)
                s = jnp.where(valid, _dot_nt(qh, kk) * ATT_SCALE, NEG)
                m = jnp.max(s, axis=-1, keepdims=True)
                pr = jnp.exp(s - m)
                den = jnp.sum(pr, axis=-1, keepdims=True)
                o = _dot(pr.astype(BF16), vv2) / den
                res.append((o, m + jnp.log(den)))
            o_ref[:, cols] = jnp.where(head0, res[0][0], res[1][0])
            lse_ref[:, cols] = jnp.where(head0, res[0][1], res[1][1])

    cur = pl.BlockSpec((BLOCK, D_GRP), lambda c, n: (n, c))
    prev = pl.BlockSpec((BLOCK, D_GRP), lambda c, n: (jnp.maximum(n - 1, 0), c))
    out = jax.ShapeDtypeStruct(qv.shape, F32)
    return pl.pallas_call(
        body, name=f"dil_fwd_r{r}", grid=(r, nb),
        in_specs=[cur, prev, cur, prev, cur], out_specs=(cur, cur), out_shape=(out, out),
        compiler_params=_params(),
    )(qv, kv, kv, vv, vv)


def _dil_bwd(qv, kv, vv, dov, lsev, deltav, r):
    sub_len = qv.shape[0]
    nb = sub_len // BLOCK

    def body(q_ref, kp_ref, kc_ref, vp_ref, vc_ref, do_ref, lse_ref, dl_ref,
             dq_ref, dk_ref, dv_ref, dk_carry, dv_carry):
        n = pl.program_id(1)

        @pl.when(n == 0)
        def _():
            dk_carry[...] = jnp.zeros_like(dk_carry)
            dv_carry[...] = jnp.zeros_like(dv_carry)

        @pl.when(n < nb)
        def _():
            valid = _band_mask(n)
            lane = lax.broadcasted_iota(jnp.int32, (BLOCK, 128), 1)
            head0 = lane < HEAD_DIM
            for p in range(4):
                cols = slice(128 * p, 128 * (p + 1))
                q2, do2 = q_ref[:, cols], do_ref[:, cols]
                lse2, dl2 = lse_ref[:, cols], dl_ref[:, cols]
                kk = jnp.concatenate([kp_ref[:, cols], kc_ref[:, cols]], axis=0)
                vv2 = jnp.concatenate([vp_ref[:, cols], vc_ref[:, cols]], axis=0)
                dq_h, dkk, dvv = [], 0.0, 0.0
                for h in range(2):
                    sel = (lambda t: jnp.where(head0, t, 0)) if h == 0 else (lambda t: jnp.where(head0, 0, t))
                    qh, doh = sel(q2), sel(do2)
                    one_lane = lane == (0 if h == 0 else HEAD_DIM)
                    lse = jnp.sum(jnp.where(one_lane, lse2, 0.0), axis=-1, keepdims=True)
                    dl = jnp.sum(jnp.where(one_lane, dl2, 0.0), axis=-1, keepdims=True)
                    s = _dot_nt(qh, kk) * ATT_SCALE
                    pr = jnp.where(valid, jnp.exp(jnp.minimum(s - lse, 0.0)), 0.0)
                    dp = _dot_nt(doh, vv2)
                    ds = (pr * (dp - dl) * ATT_SCALE).astype(BF16)
                    dq_h.append(_dot(ds, kk))
                    dkk = dkk + _dot_tn(ds, qh)
                    dvv = dvv + _dot_tn(pr.astype(BF16), doh)
                dq_ref[:, cols] = jnp.where(head0, dq_h[0], dq_h[1])
                dk_ref[:, cols] = dk_carry[:, cols] + dkk[:BLOCK]
                dv_ref[:, cols] = dv_carry[:, cols] + dvv[:BLOCK]
                dk_carry[:, cols] = dkk[BLOCK:]
                dv_carry[:, cols] = dvv[BLOCK:]

        @pl.when(n == nb)
        def _():
            dk_ref[...] = dk_carry[...]
            dv_ref[...] = dv_carry[...]

    last = nb - 1
    cur = pl.BlockSpec((BLOCK, D_GRP), lambda c, n: (jnp.minimum(n, last), c))
    prev = pl.BlockSpec((BLOCK, D_GRP), lambda c, n: (jnp.clip(n - 1, 0, last), c))
    out = jax.ShapeDtypeStruct(qv.shape, F32)
    return pl.pallas_call(
        body, name=f"dil_bwd_r{r}", grid=(r, nb + 1),
        in_specs=[cur, prev, cur, prev, cur, cur, cur, cur],
        out_specs=(cur, prev, prev), out_shape=(out, out, out),
        scratch_shapes=[pltpu.VMEM((BLOCK, D_GRP), F32), pltpu.VMEM((BLOCK, D_GRP), F32)],
        compiler_params=_params(),
    )(qv, kv, kv, vv, vv, dov, lsev, deltav)


def _sb_fwd(qs, ks, vs, tri_suf):
    s_len = qs.shape[0]
    t = SB_TILE
    nq = s_len // t

    npair = SB_PAIRS

    def body(q_ref, k_ref, v_ref, u_ref, o_ref, c_ref, qq, vt, acc, cf, csave):
        row = lax.broadcasted_iota(jnp.int32, (2 * t, t), 0) & (t - 1)
        col = lax.broadcasted_iota(jnp.int32, (2 * t, t), 1)
        diag_mask = col < row
        lane1 = lax.broadcasted_iota(jnp.int32, (t, 128), 1)
        head0 = lane1 < HEAD_DIM
        lane2 = lax.broadcasted_iota(jnp.int32, (2 * t, 128), 1)
        uu = u_ref[...]
        pr = range(npair)
        cols = [slice(128 * pp, 128 * (pp + 1)) for pp in pr]

        i = pl.program_id(1)

        @pl.when(i == 0)
        def _():
            def transpose_v(j, _):
                rows = pl.ds(pl.multiple_of(j * t, t), t)
                for pp in pr:
                    vt[pp, j] = v_ref[rows, cols[pp]].astype(F32).T.astype(BF16)
                return 0

            lax.fori_loop(0, nq, transpose_v, 0)

        for pp in pr:
            q2 = q_ref[:, cols[pp]] * ATT_SCALE
            qq[pp, 0:t, :] = jnp.where(head0, q2, 0)
            qq[pp, t:2 * t, :] = jnp.where(head0, 0, q2)
        acc[...] = jnp.zeros_like(acc)
        cf[...] = jnp.zeros_like(cf)
        csave[...] = jnp.zeros_like(csave)

        def tile(kb, diag):
            krows = pl.ds(pl.multiple_of(kb * t, t), t)
            zs = [_dot_nt(qq[pp], k_ref[krows, cols[pp]]) for pp in pr]
            lbk = [_log_sigmoid_pair(z) for z in zs]
            lks = [jnp.where(diag_mask, lk, 0.0) if diag else lk for _, lk in lbk]
            sufs = [_cumsum_mm(lk, uu) for lk in lks]
            carries = [cf[pp] for pp in pr]
            avs = []
            for pp in pr:
                a = jnp.exp(lbk[pp][0] + (sufs[pp] + jnp.concatenate([carries[pp]] * (t // 128), axis=1)))
                avs.append((jnp.where(diag_mask, a, 0.0) if diag else a).astype(BF16))
            pvs = [_dot_nt(vt[pp, kb], avs[pp]) for pp in pr]
            for pp in pr:
                acc[pp] += pvs[pp]
                csave[pp] = jnp.where(lane2 == kb, carries[pp], csave[pp])
                cf[pp] = carries[pp] + jnp.broadcast_to(jnp.sum(lks[pp], axis=-1, keepdims=True), (2 * t, 128))

        tile(i, True)

        def k_block(step, _):
            tile(i - 1 - step, False)
            return 0

        lax.fori_loop(0, i, k_block, 0)
        for pp in pr:
            o_ref[:, cols[pp]] = jnp.where(head0, acc[pp, :, 0:t].T, acc[pp, :, t:2 * t].T)
            c_ref[2 * pp] = csave[pp, 0:t, :]
            c_ref[2 * pp + 1] = csave[pp, t:2 * t, :]

    width = 128 * npair
    kv = pl.BlockSpec((s_len, width), lambda p, i: (0, p))
    qo = pl.BlockSpec((t, width), lambda p, i: (i, p))
    return pl.pallas_call(
        body, name="sb_fwd", grid=(4 // npair, nq),
        in_specs=[qo, kv, kv, pl.BlockSpec((2 * t, t), lambda p, i: (0, 0))],
        out_specs=(qo, pl.BlockSpec((2 * npair, t, 128), lambda p, i: (p, i, 0))),
        out_shape=(jax.ShapeDtypeStruct((s_len, D_GRP), F32),
                   jax.ShapeDtypeStruct((8, s_len, 128), F32)),
        scratch_shapes=[pltpu.VMEM((npair, 2 * t, 128), BF16), pltpu.VMEM((npair, nq, 128, t), BF16),
                        pltpu.VMEM((npair, 128, 2 * t), F32),
                        pltpu.VMEM((npair, 2 * t, 128), F32), pltpu.VMEM((npair, 2 * t, 128), F32)],
        compiler_params=_params(),
    )(qs, ks, vs, tri_suf)


def _sb_bwd(qs, ks, vs, dos, csaved, tri_suf, tri_pre):
    s_len = qs.shape[0]
    t = SB_TILE
    nq = s_len // t

    npair = SB_BWD_PAIRS

    def body(q_ref, k_ref, v_ref, do_ref, c_ref, u_ref, p_ref, dq_ref, dk_ref, dv_ref,
             qq, dd, qqt, ddt, kt, dq_acc, dkt, dvt, cg):
        row = lax.broadcasted_iota(jnp.int32, (2 * t, t), 0) & (t - 1)
        col = lax.broadcasted_iota(jnp.int32, (2 * t, t), 1)
        diag_mask = col < row
        lane1 = lax.broadcasted_iota(jnp.int32, (t, 128), 1)
        head0 = lane1 < HEAD_DIM
        lane2 = lax.broadcasted_iota(jnp.int32, (2 * t, 128), 1)
        uu, pm = u_ref[...], p_ref[...]
        pr = range(npair)
        cols = [slice(128 * pp, 128 * (pp + 1)) for pp in pr]
        i = pl.program_id(1)

        @pl.when(i == 0)
        def _():
            dkt[...] = jnp.zeros_like(dkt)
            dvt[...] = jnp.zeros_like(dvt)

            def transpose_k(j, _):
                rows = pl.ds(pl.multiple_of(j * t, t), t)
                for pp in pr:
                    kt[pp, j] = k_ref[rows, cols[pp]].astype(F32).T.astype(BF16)
                return 0

            lax.fori_loop(0, nq, transpose_k, 0)

        for pp in pr:
            q2 = q_ref[:, cols[pp]].astype(F32) * ATT_SCALE
            do2 = do_ref[:, cols[pp]].astype(F32)
            for src, nat, tr in ((q2, qq, qqt), (do2, dd, ddt)):
                stacked = jnp.concatenate([jnp.where(head0, src, 0.0), jnp.where(head0, 0.0, src)], axis=0)
                nat[pp] = stacked.astype(BF16)
                tr[pp] = stacked.T.astype(BF16)
        dq_acc[...] = jnp.zeros_like(dq_acc)
        cg[...] = jnp.zeros_like(cg)

        def tile(kb, diag):
            krows = pl.ds(pl.multiple_of(kb * t, t), t)
            zs = [_dot_nt(qq[pp], k_ref[krows, cols[pp]]) for pp in pr]
            das = [_dot_nt(dd[pp], v_ref[krows, cols[pp]]) for pp in pr]
            lbk = [_log_sigmoid_pair(z) for z in zs]
            lks = [jnp.where(diag_mask, lk, 0.0) if diag else lk for _, lk in lbk]
            sufs = [_cumsum_mm(lk, uu) for lk in lks]
            avs, gs = [], []
            for pp in pr:
                cs = jnp.concatenate([c_ref[2 * pp], c_ref[2 * pp + 1]], axis=0)
                cf = jnp.sum(jnp.where(lane2 == kb, cs, 0.0), axis=-1, keepdims=True)
                a = jnp.exp(lbk[pp][0] + (sufs[pp] + cf))
                a = jnp.where(diag_mask, a, 0.0) if diag else a
                avs.append(a.astype(BF16))
                gs.append(a * das[pp])
            gpres = [_cumsum_mm(g, pm) for g in gs]
            dzs = []
            for pp in pr:
                carry = cg[pp]
                beta = jnp.exp(lbk[pp][0])
                dz = gs[pp] - beta * (gs[pp] + (gpres[pp] + jnp.concatenate([carry] * (t // 128), axis=1)))
                dzs.append((jnp.where(diag_mask, dz, 0.0) if diag else dz).astype(BF16))
                cg[pp] = carry + jnp.broadcast_to(jnp.sum(gs[pp], axis=-1, keepdims=True), (2 * t, 128))
            dqs = [_dot_nt(kt[pp, kb], dzs[pp]) for pp in pr]
            dks = [_dot(qqt[pp], dzs[pp]) for pp in pr]
            dvs = [_dot(ddt[pp], avs[pp]) for pp in pr]
            for pp in pr:
                dq_acc[pp] += dqs[pp]
                dkt[pp, kb] += dks[pp]
                dvt[pp, kb] += dvs[pp]

        def k_block(kb, _):
            tile(kb, False)
            return 0

        lax.fori_loop(0, i, k_block, 0)
        tile(i, True)
        for pp in pr:
            dq_ref[:, cols[pp]] = jnp.where(head0, dq_acc[pp, :, 0:t].T, dq_acc[pp, :, t:2 * t].T) * ATT_SCALE

        @pl.when(i == nq - 1)
        def _():
            def untranspose(j, _):
                rows = pl.ds(pl.multiple_of(j * t, t), t)
                for pp in pr:
                    dk_ref[rows, cols[pp]] = dkt[pp, j].T
                    dv_ref[rows, cols[pp]] = dvt[pp, j].T
                return 0

            lax.fori_loop(0, nq, untranspose, 0)

    width = 128 * npair
    kv = pl.BlockSpec((s_len, width), lambda p, i: (0, p))
    qo = pl.BlockSpec((t, width), lambda p, i: (i, p))
    tri = pl.BlockSpec((2 * t, t), lambda p, i: (0, 0))
    out = jax.ShapeDtypeStruct((s_len, D_GRP), F32)
    return pl.pallas_call(
        body, name="sb_bwd", grid=(4 // npair, nq),
        in_specs=[qo, kv, kv, qo, pl.BlockSpec((2 * npair, t, 128), lambda p, i: (p, i, 0)), tri, tri],
        out_specs=(qo, kv, kv), out_shape=(out, out, out),
        scratch_shapes=[pltpu.VMEM((npair, 2 * t, 128), BF16), pltpu.VMEM((npair, 2 * t, 128), BF16),
                        pltpu.VMEM((npair, 128, 2 * t), BF16), pltpu.VMEM((npair, 128, 2 * t), BF16),
                        pltpu.VMEM((npair, nq, 128, t), BF16),
                        pltpu.VMEM((npair, 128, 2 * t), F32),
                        pltpu.VMEM((npair, nq, 128, t), F32), pltpu.VMEM((npair, nq, 128, t), F32),
                        pltpu.VMEM((npair, 2 * t, 128), F32)],
        compiler_params=_params(),
    )(qs, ks, vs, dos, csaved, tri_suf, tri_pre)


def _attn_out(o_b, lse_b, o_sb, x2, wdil, wsb, b_g):
    s_len = x2.shape[0]
    tm = ROW_TILE

    def body(o1_ref, l1_ref, o4_ref, l4_ref, o16_ref, l16_ref, osb_ref, x_ref, wdil_ref, wsb_ref, w_ref,
             odil_ref, lse_ref, lse4_ref, lse16_ref, mixed_ref, x1_ref, stage, nat4, nat16):
        _merge_views((o4_ref, l4_ref), (o16_ref, l16_ref), stage, nat4, nat16)
        os_ = (o1_ref[...], _slab_group(nat4, 0), _slab_group(nat16, 0))
        ls = (l1_ref[...], _slab_group(nat4, 1), _slab_group(nat16, 1))
        mx = jnp.maximum(jnp.maximum(ls[0], ls[1]), ls[2])
        es = [jnp.exp(l - mx) for l in ls]
        den = es[0] + es[1] + es[2]
        o_dil = (es[0] * os_[0] + es[1] * os_[1] + es[2] * os_[2]) / den
        odil_ref[...] = o_dil
        lse = mx + jnp.log(den)
        lse_ref[...] = lse
        for p in range(4):
            nat4[p] = lse[:, 128 * p:128 * (p + 1)]
        _split_views(nat4.at[0:4], stage.at[0:4], (lse4_ref,), (lse16_ref,))
        halves = []
        for t, w_r in ((o_dil, wdil_ref), (osb_ref[...], wsb_ref)):
            r = lax.rsqrt(jnp.mean(t * t, axis=-1, keepdims=True) + EPS)
            halves.append(((t * r) * w_r[...]).astype(BF16))
        mixed = jnp.concatenate(halves, axis=1)
        mixed_ref[...] = mixed
        w = w_ref[...].reshape(D_MODEL, D_MODEL)
        x1_ref[...] = x_ref[...] + _dot(mixed, w)

    row = lambda w: pl.BlockSpec((tm, w), lambda i: (i, 0))
    return pl.pallas_call(
        body, name="attn_out", grid=(s_len // tm,),
        in_specs=[row(D_GRP)] * 2 + [_view_spec(tm, 4)] * 2 + [_view_spec(tm, 16)] * 2
        + [row(D_GRP), row(D_MODEL), _full((1, D_GRP)), _full((1, D_GRP)),
           pl.BlockSpec((N_DEV, OUT_SHARD, D_MODEL), lambda i: (0, 3, 0))],
        out_specs=(row(D_GRP), row(D_GRP), _view_spec(tm, 4), _view_spec(tm, 16), row(D_MODEL), row(D_MODEL)),
        out_shape=(jax.ShapeDtypeStruct((s_len, D_GRP), F32), jax.ShapeDtypeStruct((s_len, D_GRP), F32),
                   _view_shape(s_len, 4, F32), _view_shape(s_len, 16, F32),
                   jax.ShapeDtypeStruct((s_len, D_MODEL), BF16), jax.ShapeDtypeStruct((s_len, D_MODEL), F32)),
        scratch_shapes=[pltpu.VMEM((8, tm, 128), F32)] * 3,
        compiler_params=_params(),
    )(o_b[0], lse_b[0], o_b[1], lse_b[1], o_b[2], lse_b[2], o_sb, x2, wdil, wsb, b_g)


def _ffn_fwd(x1, wn2, tgt, gu_g, b_g):
    s_len = x1.shape[0]
    tm = ROW_TILE
    ni = s_len // tm

    def body(x_ref, wn_ref, t_ref, wg_ref, wu_ref, wd_ref, g_ref, u_ref, h2_ref, dy_ref, loss_ref, acc):
        j = pl.program_id(1)

        @pl.when(j == 0)
        def _():
            xx = x_ref[...]
            r = lax.rsqrt(jnp.mean(xx * xx, axis=-1, keepdims=True) + EPS)
            h2_ref[...] = ((xx * r) * wn_ref[...]).astype(BF16)
            acc[...] = jnp.zeros_like(acc)

        h = h2_ref[...]
        g = _dot(h, wg_ref[0])
        u = _dot(h, wu_ref[0])
        g_ref[...] = g
        u_ref[...] = u
        act = (g * (1.0 / (1.0 + jnp.exp(-g)))) * u
        acc[...] += _dot(act.astype(BF16), wd_ref[0])

        @pl.when(j == N_DEV - 1)
        def _():
            err = (x_ref[...] + acc[...]) - t_ref[...]
            dy_ref[...] = err * (1.0 / D_MODEL)
            part = 0.5 * jnp.sum(jnp.mean(err * err, axis=-1, keepdims=True))
            loss_ref[...] = jnp.full((8, 128), part, F32)

    row = pl.BlockSpec((tm, D_MODEL), lambda i, j: (i, 0))
    hid = pl.BlockSpec((tm, FF_PAD), lambda i, j: (i, j))
    return pl.pallas_call(
        body, name="ffn_fwd", grid=(ni, N_DEV),
        in_specs=[row, pl.BlockSpec((1, D_MODEL), lambda i, j: (0, 0)), row,
                  pl.BlockSpec((1, D_MODEL, FF_PAD), lambda i, j: (j, 0, 0)),
                  pl.BlockSpec((1, D_MODEL, FF_PAD), lambda i, j: (j, 1, 0)),
                  pl.BlockSpec((1, FF_PAD, D_MODEL), lambda i, j: (j, 0, 0))],
        out_specs=(hid, hid, row, row, pl.BlockSpec((8, 128), lambda i, j: (i, 0))),
        out_shape=(jax.ShapeDtypeStruct((s_len, N_DEV * FF_PAD), F32),
                   jax.ShapeDtypeStruct((s_len, N_DEV * FF_PAD), F32),
                   jax.ShapeDtypeStruct((s_len, D_MODEL), BF16),
                   jax.ShapeDtypeStruct((s_len, D_MODEL), F32),
                   jax.ShapeDtypeStruct((ni * 8, 128), F32)),
        scratch_shapes=[pltpu.VMEM((tm, D_MODEL), F32)],
        compiler_params=_params(),
    )(x1, wn2, tgt, gu_g, gu_g, b_g)


def _ffn_bwd_dx(dy, g, u, gu_g, b_g):
    s_len = dy.shape[0]
    tm = ROW_TILE

    def body(dy_ref, g_ref, u_ref, wg_ref, wu_ref, wd_ref, dg_ref, du_ref, act_ref, dh_ref, acc):
        j = pl.program_id(1)

        @pl.when(j == 0)
        def _():
            acc[...] = jnp.zeros_like(acc)

        gg, uu = g_ref[...], u_ref[...]
        da = _dot_nt(dy_ref[...].astype(BF16), wd_ref[0])
        sig = 1.0 / (1.0 + jnp.exp(-gg))
        silu = gg * sig
        act_ref[...] = (silu * uu).astype(BF16)
        du = (da * silu).astype(BF16)
        dg = (da * uu * (sig * (1.0 + gg * (1.0 - sig)))).astype(BF16)
        du_ref[...] = du
        dg_ref[...] = dg
        acc[...] += _dot_nt(dg, wg_ref[0]) + _dot_nt(du, wu_ref[0])

        @pl.when(j == N_DEV - 1)
        def _():
            dh_ref[...] = acc[...]

    row = pl.BlockSpec((tm, D_MODEL), lambda i, j: (i, 0))
    hid = pl.BlockSpec((tm, FF_PAD), lambda i, j: (i, j))
    hid_bf = jax.ShapeDtypeStruct((s_len, N_DEV * FF_PAD), BF16)
    return pl.pallas_call(
        body, name="ffn_bwd_dx", grid=(s_len // tm, N_DEV),
        in_specs=[row, hid, hid,
                  pl.BlockSpec((1, D_MODEL, FF_PAD), lambda i, j: (j, 0, 0)),
                  pl.BlockSpec((1, D_MODEL, FF_PAD), lambda i, j: (j, 1, 0)),
                  pl.BlockSpec((1, FF_PAD, D_MODEL), lambda i, j: (j, 0, 0))],
        out_specs=(hid, hid, hid, row),
        out_shape=(hid_bf, hid_bf, hid_bf, jax.ShapeDtypeStruct((s_len, D_MODEL), F32)),
        scratch_shapes=[pltpu.VMEM((tm, D_MODEL), F32)],
        compiler_params=_params(),
    )(dy, g, u, gu_g, gu_g, b_g)


def _ffn_bwd_dw(h2, dy, dg, du, act):
    s_len = h2.shape[0]
    tm = ROW_TILE
    ni = s_len // tm

    def body(h_ref, dy_ref, dg_ref, du_ref, act_ref, dwg_ref, dwu_ref, dwd_ref, ag, au, ad):
        i = pl.program_id(1)

        @pl.when(i == 0)
        def _():
            ag[...] = jnp.zeros_like(ag)
            au[...] = jnp.zeros_like(au)
            ad[...] = jnp.zeros_like(ad)

        h = h_ref[...]
        ag[...] += _dot_tn(h, dg_ref[...])
        au[...] += _dot_tn(h, du_ref[...])
        ad[...] += _dot_tn(act_ref[...], dy_ref[...].astype(BF16))

        @pl.when(i == ni - 1)
        def _():
            dwg_ref[0] = ag[...].astype(BF16)
            dwu_ref[0] = au[...].astype(BF16)
            dwd_ref[0] = ad[...].astype(BF16)

    row = pl.BlockSpec((tm, D_MODEL), lambda j, i: (i, 0))
    hid = pl.BlockSpec((tm, FF_PAD), lambda j, i: (i, j))
    col_w = pl.BlockSpec((1, D_MODEL, FF_PAD), lambda j, i: (j, 0, 0))
    row_w = pl.BlockSpec((1, FF_PAD, D_MODEL), lambda j, i: (j, 0, 0))
    return pl.pallas_call(
        body, name="ffn_bwd_dw", grid=(N_DEV, ni),
        in_specs=[row, row, hid, hid, hid], out_specs=(col_w, col_w, row_w),
        out_shape=(jax.ShapeDtypeStruct((N_DEV, D_MODEL, FF_PAD), BF16),
                   jax.ShapeDtypeStruct((N_DEV, D_MODEL, FF_PAD), BF16),
                   jax.ShapeDtypeStruct((N_DEV, FF_PAD, D_MODEL), BF16)),
        scratch_shapes=[pltpu.VMEM((D_MODEL, FF_PAD), F32), pltpu.VMEM((D_MODEL, FF_PAD), F32),
                        pltpu.VMEM((FF_PAD, D_MODEL), F32)],
        compiler_params=_params(),
    )(h2, dy, dg, du, act)


def _rms_bwd(dy, t, w):
    r = lax.rsqrt(jnp.mean(t * t, axis=-1, keepdims=True) + EPS)
    gw = dy * w
    dt = r * (gw - t * ((r * r) * jnp.mean(gw * t, axis=-1, keepdims=True)))
    return dt, dy * t * r


def _attn_out_bwd(dy, dh2, x1, wn2, b_g, mixed, o_dil, o_sb, wdil, wsb, bd512):
    s_len = dy.shape[0]
    tm = ROW_TILE
    ni = s_len // tm

    def body(dy_ref, dh_ref, x1_ref, wn_ref, w_ref, mixed_ref, odil_ref, osb_ref, wdil_ref, wsb_ref, bd_ref,
             dx1_ref, dodil_ref, delta_ref, dosb_ref, dwout_ref, dwn_ref, dwdil_ref, dwsb_ref,
             do4_ref, dl4_ref, do16_ref, dl16_ref, wacc, both, stage):
        i = pl.program_id(0)

        @pl.when(i == 0)
        def _():
            wacc[...] = jnp.zeros_like(wacc)
            dwn_ref[...] = jnp.zeros_like(dwn_ref)
            dwdil_ref[...] = jnp.zeros_like(dwdil_ref)
            dwsb_ref[...] = jnp.zeros_like(dwsb_ref)

        dnorm, dw_rows = _rms_bwd(dh_ref[...], x1_ref[...], wn_ref[...])
        dx1 = dy_ref[...] + dnorm
        dx1_ref[...] = dx1
        dwn_ref[...] += jnp.sum(dw_rows, axis=0, keepdims=True)
        dx1b = dx1.astype(BF16)
        w = w_ref[...].reshape(D_MODEL, D_MODEL)
        dmixed = _dot_nt(dx1b, w)
        wacc[...] += _dot_tn(mixed_ref[...], dx1b)
        o_dil = odil_ref[...]
        d_odil, dw_rows = _rms_bwd(dmixed[:, :D_GRP], o_dil, wdil_ref[...])
        dwdil_ref[...] += jnp.sum(dw_rows, axis=0, keepdims=True)
        dodil_ref[...] = d_odil.astype(BF16)
        delta = _mm_split(d_odil * o_dil, bd_ref[...])
        delta_ref[...] = delta
        for p in range(4):
            both[p] = d_odil[:, 128 * p:128 * (p + 1)]
            both[4 + p] = delta[:, 128 * p:128 * (p + 1)]
        _split_views(both, stage, (do4_ref, dl4_ref), (do16_ref, dl16_ref))
        d_osb, dw_rows = _rms_bwd(dmixed[:, D_GRP:], osb_ref[...], wsb_ref[...])
        dwsb_ref[...] += jnp.sum(dw_rows, axis=0, keepdims=True)
        dosb_ref[...] = d_osb.astype(BF16)

        @pl.when(i == ni - 1)
        def _():
            dwout_ref[...] = wacc[...].astype(BF16).reshape(N_DEV, OUT_SHARD, D_MODEL)

    row = lambda w: pl.BlockSpec((tm, w), lambda i: (i, 0))
    return pl.pallas_call(
        body, name="attn_out_bwd", grid=(ni,),
        in_specs=[row(D_MODEL), row(D_MODEL), row(D_MODEL), _full((1, D_MODEL)),
                  pl.BlockSpec((N_DEV, OUT_SHARD, D_MODEL), lambda i: (0, 3, 0)),
                  row(D_MODEL), row(D_GRP), row(D_GRP), _full((1, D_GRP)), _full((1, D_GRP)),
                  _full((D_GRP, D_GRP))],
        out_specs=(row(D_MODEL), row(D_GRP), row(D_GRP), row(D_GRP),
                   _full((N_DEV, OUT_SHARD, D_MODEL)), _full((1, D_MODEL)), _full((1, D_GRP)), _full((1, D_GRP)),
                   _view_spec(tm, 4), _view_spec(tm, 4), _view_spec(tm, 16), _view_spec(tm, 16)),
        out_shape=(jax.ShapeDtypeStruct((s_len, D_MODEL), F32), jax.ShapeDtypeStruct((s_len, D_GRP), BF16),
                   jax.ShapeDtypeStruct((s_len, D_GRP), F32), jax.ShapeDtypeStruct((s_len, D_GRP), BF16),
                   jax.ShapeDtypeStruct((N_DEV, OUT_SHARD, D_MODEL), BF16),
                   jax.ShapeDtypeStruct((1, D_MODEL), F32), jax.ShapeDtypeStruct((1, D_GRP), F32),
                   jax.ShapeDtypeStruct((1, D_GRP), F32),
                   _view_shape(s_len, 4, BF16), _view_shape(s_len, 4, F32),
                   _view_shape(s_len, 16, BF16), _view_shape(s_len, 16, F32)),
        scratch_shapes=[pltpu.VMEM((D_MODEL, D_MODEL), F32), pltpu.VMEM((8, tm, 128), F32),
                        pltpu.VMEM((8, tm, 128), F32)],
        compiler_params=_params(),
    )(dy, dh2, x1, wn2, b_g, mixed, o_dil, o_sb, wdil, wsb, bd512)


def _qkv_bwd(dq_b, dk_b, dv_b, dqs, dks, dvs, qraw, kraw, cos2, sin2, qnw, knw, bd):
    s_len = qraw.shape[0]
    tm = ROW_TILE
    ni = s_len // tm

    def body(dq1, dk1, dv1, dq4, dk4, dv4, dq16, dk16, dv16, dqs_ref, dks_ref, dvs_ref,
             qraw_ref, kraw_ref, cos_ref, sin_ref, qnw_ref, knw_ref, bd_ref,
             dproj_ref, dqn_ref, dkn_ref, stage, nat4, nat16):
        i = pl.program_id(0)

        @pl.when(i == 0)
        def _():
            dqn_ref[...] = jnp.zeros_like(dqn_ref)
            dkn_ref[...] = jnp.zeros_like(dkn_ref)

        _merge_views((dq4, dk4, dv4), (dq16, dk16, dv16), stage, nat4, nat16)
        cos_t, sin_t, bdm = cos_ref[...], sin_ref[...], bd_ref[...]
        for grp, (part1, raw_ref, nw_ref, dn_ref) in enumerate(((dq1, qraw_ref, qnw_ref, dqn_ref),
                                                                (dk1, kraw_ref, knw_ref, dkn_ref))):
            dn_acc = 0.0
            for p in range(4):
                cols = slice(128 * p, 128 * (p + 1))
                d_rope = part1[:, cols] + nat4[4 * grp + p] + nat16[4 * grp + p]
                d_norm = d_rope * cos_t + _swap_halves(d_rope * sin_t)
                t = raw_ref[:, cols]
                w = nw_ref[...]
                r = lax.rsqrt(_mm_split(t * t, bdm) * (1.0 / HEAD_DIM) + EPS)
                gw = d_norm * w
                corr = _mm_split(gw * t, bdm) * (1.0 / HEAD_DIM)
                dt = r * (gw - t * ((r * r) * corr))
                dn_acc = dn_acc + jnp.sum(d_norm * t * r, axis=0, keepdims=True)
                dproj_ref[:, D_GRP * grp + 128 * p:D_GRP * grp + 128 * (p + 1)] = dt.astype(BF16)
            dn_ref[...] += dn_acc
        dproj_ref[:, 2 * D_GRP:3 * D_GRP] = (dv1[...] + _slab_group(nat4, 2) + _slab_group(nat16, 2)).astype(BF16)
        dproj_ref[:, 3 * D_GRP:4 * D_GRP] = dqs_ref[...].astype(BF16)
        dproj_ref[:, 4 * D_GRP:5 * D_GRP] = dks_ref[...].astype(BF16)
        dproj_ref[:, 5 * D_GRP:6 * D_GRP] = dvs_ref[...].astype(BF16)

    row = lambda w: pl.BlockSpec((tm, w), lambda i: (i, 0))
    return pl.pallas_call(
        body, name="qkv_bwd", grid=(ni,),
        in_specs=[row(D_GRP)] * 3 + [_view_spec(tm, 4)] * 3 + [_view_spec(tm, 16)] * 3 + [row(D_GRP)] * 5
        + [row(128), row(128), _full((1, 128)), _full((1, 128)), _full((128, 128))],
        out_specs=(row(D_IN), _full((1, 128)), _full((1, 128))),
        out_shape=(jax.ShapeDtypeStruct((s_len, D_IN), BF16), jax.ShapeDtypeStruct((1, 128), F32),
                   jax.ShapeDtypeStruct((1, 128), F32)),
        scratch_shapes=[pltpu.VMEM((12, tm, 128), F32)] * 3,
        compiler_params=_params(),
    )(dq_b[0], dk_b[0], dv_b[0], dq_b[1], dk_b[1], dv_b[1], dq_b[2], dk_b[2], dv_b[2],
      dqs, dks, dvs, qraw, kraw, cos2, sin2, qnw, knw, bd)


def _in_bwd_dx(dproj, a_g, x2, dx1, wn1):
    s_len = x2.shape[0]
    tm = ROW_TILE
    ni = s_len // tm

    def body(dp_ref, w_ref, x_ref, dx1_ref, wn_ref, gx_ref, dwn_ref):
        i = pl.program_id(0)

        @pl.when(i == 0)
        def _():
            dwn_ref[...] = jnp.zeros_like(dwn_ref)

        dh = 0.0
        for d in range(N_DEV):
            dh = dh + _dot_nt(dp_ref[:, IN_SHARD * d:IN_SHARD * (d + 1)], w_ref[d])
        dnorm, dw_rows = _rms_bwd(dh, x_ref[...], wn_ref[...])
        gx_ref[...] = dx1_ref[...] + dnorm
        dwn_ref[...] += jnp.sum(dw_rows, axis=0, keepdims=True)

    row = lambda w: pl.BlockSpec((tm, w), lambda i: (i, 0))
    return pl.pallas_call(
        body, name="in_bwd_dx", grid=(ni,),
        in_specs=[row(D_IN), pl.BlockSpec((N_DEV, D_MODEL, IN_SHARD), lambda i: (0, 0, 0)),
                  row(D_MODEL), row(D_MODEL), _full((1, D_MODEL))],
        out_specs=(row(D_MODEL), _full((1, D_MODEL))),
        out_shape=(jax.ShapeDtypeStruct((s_len, D_MODEL), F32), jax.ShapeDtypeStruct((1, D_MODEL), F32)),
        compiler_params=_params(),
    )(dproj, a_g, x2, dx1, wn1)


def _in_bwd_dw(h1, dproj):
    s_len = h1.shape[0]
    tm = ROW_TILE
    ni = s_len // tm

    def body(h_ref, dp_ref, dw_ref, acc):
        i = pl.program_id(1)

        @pl.when(i == 0)
        def _():
            acc[...] = jnp.zeros_like(acc)

        acc[...] += _dot_tn(h_ref[...], dp_ref[...])

        @pl.when(i == ni - 1)
        def _():
            dw_ref[0] = acc[...].astype(BF16)

    return pl.pallas_call(
        body, name="in_bwd_dw", grid=(N_DEV, ni),
        in_specs=[pl.BlockSpec((tm, D_MODEL), lambda d, i: (i, 0)),
                  pl.BlockSpec((tm, IN_SHARD), lambda d, i: (i, d))],
        out_specs=pl.BlockSpec((1, D_MODEL, IN_SHARD), lambda d, i: (d, 0, 0)),
        out_shape=jax.ShapeDtypeStruct((N_DEV, D_MODEL, IN_SHARD), BF16),
        scratch_shapes=[pltpu.VMEM((D_MODEL, IN_SHARD), F32)],
        compiler_params=_params(),
    )(h1, dproj)


def _adamw(recv, w, m, v):
    rows, cols = w.shape
    tr = 128 if rows % 128 == 0 else rows

    def body(p_ref, w_ref, m_ref, v_ref, g_ref, d_ref, nm_ref, nv_ref):
        g = p_ref[0].astype(F32)
        for s in range(1, N_DEV):
            g = g + p_ref[s].astype(F32)
        m_new = ADAM_B1 * m_ref[...] + (1.0 - ADAM_B1) * g
        v_new = ADAM_B2 * v_ref[...] + (1.0 - ADAM_B2) * (g * g)
        m_hat = m_new / (1.0 - ADAM_B1 ** ADAM_STEP)
        v_hat = v_new / (1.0 - ADAM_B2 ** ADAM_STEP)
        g_ref[...] = g
        d_ref[...] = -ADAM_LR * (m_hat / (jnp.sqrt(v_hat) + ADAM_EPS) + ADAM_WD * w_ref[...])
        nm_ref[...] = m_new
        nv_ref[...] = v_new

    blk = pl.BlockSpec((tr, cols), lambda i: (i, 0))
    out = jax.ShapeDtypeStruct((rows, cols), F32)
    return pl.pallas_call(
        body, name=f"adamw_{rows}x{cols}", grid=(rows // tr,),
        in_specs=[pl.BlockSpec((N_DEV, tr, cols), lambda i: (0, i, 0)), blk, blk, blk],
        out_specs=(blk,) * 4, out_shape=(out,) * 4,
        compiler_params=_params(),
    )(recv, w, m, v)


def _rope_tables(s_len):
    pos = jnp.arange(s_len, dtype=F32)
    inv_freq = ROPE_THETA ** (-jnp.arange(0, HEAD_DIM, 2, dtype=F32) / HEAD_DIM)
    ang = pos[:, None] * inv_freq[None, :]
    cos, sin = jnp.cos(ang), jnp.sin(ang)
    cos2 = jnp.concatenate([cos, cos, cos, cos], axis=1)
    sin2 = jnp.concatenate([-sin, sin, -sin, sin], axis=1)
    return cos2, sin2


def _block_diag_ones(n):
    i = jnp.arange(n)
    return (i[:, None] // HEAD_DIM == i[None, :] // HEAD_DIM).astype(BF16)


def _pad_cols(t):
    return jnp.pad(t, ((0, 0), (0, FF_PAD - FF_SHARD)))


def _pad_rows(t):
    return jnp.pad(t, ((0, FF_PAD - FF_SHARD), (0, 0)))


def _pack_small(n1, n2, ndil, nsb, nq, nk):
    pad = lambda t: jnp.pad(t.reshape(1, HEAD_DIM), ((0, 0), (0, 128 - HEAD_DIM)))
    rows = [n1.reshape(8, 128), n2.reshape(8, 128), ndil.reshape(4, 128), nsb.reshape(4, 128),
            pad(nq), pad(nk), jnp.zeros((6, 128), F32)]
    return jnp.concatenate(rows, axis=0)


def _unpack_small(t):
    return (t[0:8].reshape(1, D_MODEL), t[8:16].reshape(1, D_MODEL), t[16:20].reshape(1, D_GRP),
            t[20:24].reshape(1, D_GRP), t[24:25, :HEAD_DIM], t[25:26, :HEAD_DIM])


def kernel(x, attn_norm_w, w_in, q_norm_w, k_norm_w, dil_out_norm_w, sb_out_norm_w, w_out, ffn_norm_w, w_gate, w_up, w_down, loss_target, m_attn_norm_w, m_w_in, m_q_norm_w, m_k_norm_w, m_dil_out_norm_w, m_sb_out_norm_w, m_w_out, m_ffn_norm_w, m_w_gate, m_w_up, m_w_down, v_attn_norm_w, v_w_in, v_q_norm_w, v_k_norm_w, v_dil_out_norm_w, v_sb_out_norm_w, v_w_out, v_ffn_norm_w, v_w_gate, v_w_up, v_w_down):
    s_len = x.shape[1]
    x2, tgt = x[0], loss_target[0]

    my_idx = _flat_index(_mesh_pos())
    slot_is_mine = (jnp.arange(N_DEV) == my_idx)[:, None, None]

    (a_g,) = _gather_weights([w_in[0].astype(BF16)])
    gu_loc = jnp.concatenate([_pad_cols(w_gate[0]), _pad_cols(w_up[0])], axis=0).astype(BF16)
    b_loc = jnp.concatenate([_pad_rows(w_down[0]), w_out[0]], axis=0).astype(BF16)
    own_in_place = lambda t: jnp.where(slot_is_mine, t[None], jnp.zeros((), t.dtype))
    w_send, w_recv, w_srcs, w_lands, w_token = _spread_start(
        [gu_loc, b_loc], [own_in_place(gu_loc), own_in_place(b_loc)], blockwise=False, name="weights_start")

    cos2, sin2 = _rope_tables(s_len)
    bd128, bd512 = _block_diag_ones(128), _block_diag_ones(D_GRP)
    idx = jnp.arange(SB_TILE)
    tri_suf = (idx[:, None] > idx[None, :]).astype(BF16)
    tri_pre = (idx[:, None] < idx[None, :]).astype(BF16)
    tri_suf = jnp.concatenate([tri_suf, tri_suf], axis=0)
    tri_pre = jnp.concatenate([tri_pre, tri_pre], axis=0)
    qnw2 = jnp.concatenate([q_norm_w, q_norm_w], axis=1) + w_token[0:1]
    knw2 = jnp.concatenate([k_norm_w, k_norm_w], axis=1)

    (h1, qraw, kraw, q, k, va, qs, ks, vs,
     q4, k4, v4, q16, k16, v16) = _attn_in(x2, attn_norm_w, a_g, cos2, sin2, qnw2, knw2, bd128)
    qkv_views = {1: (q, k, va), 4: (q4, k4, v4), 16: (q16, k16, v16)}
    o_b, lse_b = [], []
    for r in DILATIONS:
        o, lse = _dil_fwd(*qkv_views[r], r)
        o_b.append(o)
        lse_b.append(lse)
    o_sb, c_sb = _sb_fwd(qs, ks, vs, tri_suf)
    gu_g, b_g = _spread_wait(w_send, w_recv, w_srcs, w_lands, c_sb, blockwise=False, name="weights_wait")
    o_dil, lse_tot, lse4, lse16, mixed, x1 = _attn_out(o_b, lse_b, o_sb, x2, dil_out_norm_w, sb_out_norm_w, b_g)
    g, u, h2, dy, loss_parts = _ffn_fwd(x1, ffn_norm_w, tgt, gu_g, b_g)
    loss = lax.psum(jnp.sum(loss_parts[::8, 0]), ("x", "y", "c"))

    dg, du, act, dh2 = _ffn_bwd_dx(dy, g, u, gu_g, b_g)
    (dx1, do_dil, delta, do_sb, dwout, dn2, dndil, dnsb, do4, dl4, do16, dl16) = _attn_out_bwd(
        dy, dh2, x1, ffn_norm_w, b_g, mixed, o_dil, o_sb, dil_out_norm_w, sb_out_norm_w, bd512)
    dwg, dwu, dwd = _ffn_bwd_dw(h2, dy, dg, du, act)
    early = [dwg, dwu, dwd, dwout]
    own_slot_only = lambda t: jnp.where(slot_is_mine, t, jnp.zeros((), t.dtype))
    g_send, g_recv, g_srcs, g_lands, g_token = _spread_start(
        early, [own_slot_only(t) for t in early], blockwise=True, name="grads_start")
    tri_pre = tri_pre + g_token[0, 0].astype(BF16)
    dqs, dks, dvs = _sb_bwd(qs, ks, vs, do_sb, c_sb, tri_suf, tri_pre)
    cot_views = {1: (do_dil, lse_tot, delta), 4: (do4, lse4, dl4), 16: (do16, lse16, dl16)}
    dq_b, dk_b, dv_b = [], [], []
    for r in DILATIONS:
        dq, dk, dv = _dil_bwd(*qkv_views[r], *cot_views[r], r)
        dq_b.append(dq)
        dk_b.append(dk)
        dv_b.append(dv)
    dproj, dqn2, dkn2 = _qkv_bwd(dq_b, dk_b, dv_b, dqs, dks, dvs, qraw, kraw, cos2, sin2, qnw2, knw2, bd128)
    grad_x, dn1 = _in_bwd_dx(dproj, a_g, x2, dx1, attn_norm_w)
    dwin = _in_bwd_dw(h1, dproj)
    dqn = dqn2[:, :HEAD_DIM] + dqn2[:, HEAD_DIM:]
    dkn = dkn2[:, :HEAD_DIM] + dkn2[:, HEAD_DIM:]

    small = _pack_small(dn1, dn2, dndil, dnsb, dqn, dkn)
    r_in, r_small = _exchange_grads([dwin], small)
    r_gate, r_up, r_down, r_out = _spread_wait(g_send, g_recv, g_srcs, g_lands, r_small, blockwise=True,
                                               name="grads_wait")
    big = {
        "w_in": _adamw(r_in, w_in[0], m_w_in[0], v_w_in[0]),
        "w_gate": tuple(t[:, :FF_SHARD] for t in _adamw(r_gate, _pad_cols(w_gate[0]), _pad_cols(m_w_gate[0]), _pad_cols(v_w_gate[0]))),
        "w_up": tuple(t[:, :FF_SHARD] for t in _adamw(r_up, _pad_cols(w_up[0]), _pad_cols(m_w_up[0]), _pad_cols(v_w_up[0]))),
        "w_down": tuple(t[:FF_SHARD] for t in _adamw(r_down, _pad_rows(w_down[0]), _pad_rows(m_w_down[0]), _pad_rows(v_w_down[0]))),
        "w_out": _adamw(r_out, w_out[0], m_w_out[0], v_w_out[0]),
    }
    packs = [_pack_small(*ts) for ts in (
        (attn_norm_w, ffn_norm_w, dil_out_norm_w, sb_out_norm_w, q_norm_w, k_norm_w),
        (m_attn_norm_w, m_ffn_norm_w, m_dil_out_norm_w, m_sb_out_norm_w, m_q_norm_w, m_k_norm_w),
        (v_attn_norm_w, v_ffn_norm_w, v_dil_out_norm_w, v_sb_out_norm_w, v_q_norm_w, v_k_norm_w))]
    small_out = [_unpack_small(t) for t in _adamw(r_small, *packs)]
    names = ["attn_norm_w", "w_in", "q_norm_w", "k_norm_w", "dil_out_norm_w", "sb_out_norm_w", "w_out",
             "ffn_norm_w", "w_gate", "w_up", "w_down"]
    small_pos = {"attn_norm_w": 0, "ffn_norm_w": 1, "dil_out_norm_w": 2, "sb_out_norm_w": 3,
                 "q_norm_w": 4, "k_norm_w": 5}
    outs = [loss, grad_x[None]]
    for kind in range(4):
        for name in names:
            if name in small_pos:
                outs.append(small_out[kind][small_pos[name]])
            else:
                outs.append(big[name][kind][None])
    return tuple(outs)
```

```python
import functools

import jax
import jax.numpy as jnp
from jax import lax
from jax.experimental import pallas as pl
from jax.experimental.pallas import tpu as pltpu

F32 = jnp.float32
BF16 = jnp.bfloat16

N_DEV = 8
D_MODEL = 1024
HEAD_DIM = 64
D_GRP = 512
D_IN = 6 * D_GRP
IN_SHARD = D_IN // N_DEV
FF_SHARD = 352
FF_PAD = 384
OUT_SHARD = D_MODEL // N_DEV
BLOCK = 128
DILATIONS = (1, 4, 16)
ROPE_THETA = 10000.0
EPS = 1e-6
ATT_SCALE = HEAD_DIM ** -0.5
NEG = -1e30

ADAM_LR = 0.001
ADAM_B1 = 0.9
ADAM_B2 = 0.999
ADAM_EPS = 1e-08
ADAM_WD = 0.01
ADAM_STEP = 10

SB_TILE = 256
SB_PAIRS = 4
SB_BWD_PAIRS = 2
ROW_TILE = 512
VMEM_LIMIT = 56 * 1024 * 1024
MESH = pl.DeviceIdType.MESH


def _dot(a, b):
    return jnp.dot(a, b, preferred_element_type=F32)


def _dot_nt(a, b):
    return lax.dot_general(a, b, (((1,), (1,)), ((), ())), preferred_element_type=F32)


def _dot_tn(a, b):
    return lax.dot_general(a, b, (((0,), (0,)), ((), ())), preferred_element_type=F32)


def _mm_split(t, m):
    hi = t.astype(BF16)
    lo = (t - hi.astype(F32)).astype(BF16)
    return _dot(hi, m) + _dot(lo, m)


def _params(**kw):
    return pltpu.CompilerParams(vmem_limit_bytes=VMEM_LIMIT, **kw)


def _full(shape):
    nd = len(shape)
    return pl.BlockSpec(shape, lambda *_: (0,) * nd)


def _view_shape(s_len, r, dtype):
    return jax.ShapeDtypeStruct((s_len // r, r * D_GRP), dtype)


def _view_spec(tm, r):
    return pl.BlockSpec((tm // r, r * D_GRP), lambda i: (i, 0))


def _swap_halves(t):
    lane = lax.broadcasted_iota(jnp.int32, t.shape, 1)
    first = (lane & 32) == 0
    return jnp.where(first, pltpu.roll(t, 96, 1), pltpu.roll(t, 32, 1))


def _log_sigmoid(z):
    return jnp.minimum(z, 0.0) - jnp.log(1.0 + jnp.exp(-jnp.abs(z)))


def _log_sigmoid_pair(z):
    neg_abs = lax.bitcast_convert_type(lax.bitcast_convert_type(z, jnp.uint32) | jnp.uint32(0x80000000), F32)
    lb = jnp.minimum(z, 0.0) - jnp.log(1.0 + jnp.exp(neg_abs))
    return lb, lb - z


def _cumsum_mm(t, tri):
    return _dot(t.astype(BF16), tri)


def _split_views(src_ref, stage_ref, views4, views16):
    slabs, n, _ = src_ref.shape
    n4, n16 = n // 4, n // 16
    for j in range(slabs):
        g, lanes = j // 4, 128 * (j % 4)
        src, stage = src_ref.at[j], stage_ref.at[j]
        for c4 in range(4):
            blk = src[pl.ds(c4, n4, stride=4), :]
            stage[n4 * c4:n4 * (c4 + 1), :] = blk
            col = D_GRP * c4 + lanes
            views4[g][:, col:col + 128] = blk.astype(views4[g].dtype)
        for c4 in range(4):
            for c1 in range(4):
                blk = stage[pl.ds(n4 * c4 + c1, n16, stride=4), :]
                col = D_GRP * (4 * c1 + c4) + lanes
                views16[g][:, col:col + 128] = blk.astype(views16[g].dtype)


def _merge_views(views4, views16, stage_ref, dst4_ref, dst16_ref):
    slabs, n, _ = dst4_ref.shape
    n4, n16 = n // 4, n // 16
    for j in range(slabs):
        g, lanes = j // 4, 128 * (j % 4)
        dst4, dst16, stage = dst4_ref.at[j], dst16_ref.at[j], stage_ref.at[j]
        for c4 in range(4):
            col = D_GRP * c4 + lanes
            dst4[pl.ds(c4, n4, stride=4), :] = views4[g][:, col:col + 128].astype(F32)
            for c1 in range(4):
                col = D_GRP * (4 * c1 + c4) + lanes
                stage[pl.ds(n4 * c4 + c1, n16, stride=4), :] = views16[g][:, col:col + 128].astype(F32)
        for c4 in range(4):
            dst16[pl.ds(c4, n4, stride=4), :] = stage[n4 * c4:n4 * (c4 + 1), :]


def _slab_group(ref, g):
    return jnp.concatenate([ref[4 * g + p] for p in range(4)], axis=1)


def _mesh_pos():
    return lax.axis_index("x"), lax.axis_index("y"), lax.axis_index("c")


def _flat_index(p):
    return 4 * p[0] + 2 * p[1] + p[2]


def _gather_weights(shards):
    n_arr = len(shards)

    def body(*refs):
        srcs, outs = refs[:n_arr], refs[n_arr:2 * n_arr]
        send_sems, recv_sems, local_sems = refs[2 * n_arr:]
        x, y, c = _mesh_pos()
        me, sibling = (x, y, c), (x, y, 1 - c)
        chips = [(1 - x, y), (x, 1 - y), (1 - x, 1 - y)]

        def copy(arr, k, block, to, own=False):
            dst = outs[arr].at[_flat_index(block)]
            return pltpu.make_async_remote_copy(
                src_ref=srcs[arr] if own else dst, dst_ref=dst,
                send_sem=send_sems.at[arr, k], recv_sem=recv_sems.at[arr, k],
                device_id=to, device_id_type=MESH)

        for arr in range(n_arr):
            mine = pltpu.make_async_copy(srcs[arr], outs[arr].at[_flat_index(me)], local_sems.at[arr])
            mine.start()
            first = [copy(arr, 0, me, sibling, own=True)]
            first += [copy(arr, 1 + j, me, (*chip, c), own=True) for j, chip in enumerate(chips)]
            for cp in first:
                cp.start()
        for arr in range(n_arr):
            passed = [copy(arr, 4 + j, (*chip, c), sibling) for j, chip in enumerate(chips)]
            for j, chip in enumerate(chips):
                copy(arr, 1 + j, (*chip, c), me).wait_recv()
                passed[j].start()
        for arr in range(n_arr):
            copy(arr, 0, sibling, me).wait_recv()
            for j, chip in enumerate(chips):
                copy(arr, 4 + j, (*chip, 1 - c), me).wait_recv()
            for k in range(7):
                copy(arr, k, me, me).wait_send()
            pltpu.make_async_copy(srcs[arr], outs[arr].at[_flat_index(me)], local_sems.at[arr]).wait()

    any_spec = pl.BlockSpec(memory_space=pl.ANY)
    return pl.pallas_call(
        body, name="gather_weights",
        out_shape=tuple(jax.ShapeDtypeStruct((N_DEV,) + s.shape, s.dtype) for s in shards),
        in_specs=[any_spec] * n_arr, out_specs=(any_spec,) * n_arr,
        scratch_shapes=[pltpu.SemaphoreType.DMA((n_arr, 7)), pltpu.SemaphoreType.DMA((n_arr, 7)),
                        pltpu.SemaphoreType.DMA((n_arr,))],
        compiler_params=pltpu.CompilerParams(has_side_effects=True),
    )(*shards)


_HBM_SPEC = pl.BlockSpec(memory_space=pltpu.HBM)
_SEM_SPEC = pl.BlockSpec(memory_space=pltpu.SEMAPHORE)
_DATAFLOW = pltpu.SideEffectType.DATAFLOW_SIDE_EFFECTING


def _peer_list(x, y, c):
    return [(1 - x if m & 4 else x, 1 - y if m & 2 else y, 1 - c if m & 1 else c) for m in range(1, N_DEV)]


def _spread_copies(src_refs, land_refs, send_sems, recv_sems, blockwise):
    x, y, c = _mesh_pos()
    my_idx = _flat_index((x, y, c))
    copies = []
    for a, (src, land) in enumerate(zip(src_refs, land_refs)):
        for k, peer in enumerate(_peer_list(x, y, c)):
            copies.append(pltpu.make_async_remote_copy(
                src_ref=src.at[_flat_index(peer)] if blockwise else src, dst_ref=land.at[my_idx],
                send_sem=send_sems.at[(N_DEV - 1) * a + k], recv_sem=recv_sems.at[(N_DEV - 1) * a + k],
                device_id=peer, device_id_type=MESH))
    return copies


def _spread_start(srcs, lands, blockwise, name):
    n = len(srcs)

    def body(*refs):
        for cp in _spread_copies(refs[:n], refs[n:2 * n], refs[2 * n], refs[2 * n + 1], blockwise):
            cp.start()
        token = refs[-1]
        token[...] = jnp.zeros_like(token)

    hbm = lambda t: pltpu.HBM(t.shape, t.dtype)
    sems = pltpu.SemaphoreType.DMA((n * (N_DEV - 1),))
    outs = pl.pallas_call(
        body, name=name,
        out_shape=(sems, sems) + tuple(hbm(t) for t in srcs) + tuple(hbm(t) for t in lands)
        + (jax.ShapeDtypeStruct((8, 128), F32),),
        in_specs=[_HBM_SPEC] * (2 * n),
        out_specs=(_SEM_SPEC, _SEM_SPEC) + (_HBM_SPEC,) * (2 * n) + (pl.BlockSpec(memory_space=pltpu.VMEM),),
        input_output_aliases={i: 2 + i for i in range(2 * n)},
        compiler_params=pltpu.CompilerParams(has_side_effects=_DATAFLOW),
    )(*[pltpu.with_memory_space_constraint(t, pltpu.HBM) for t in list(srcs) + list(lands)])
    return outs[0], outs[1], outs[2:2 + n], outs[2 + n:2 + 2 * n], outs[-1]


def _spread_wait(send_sems, recv_sems, srcs, lands, after, blockwise, name):
    n = len(srcs)

    def body(*refs):
        for cp in _spread_copies(refs[:n], refs[n:2 * n], refs[2 * n], refs[2 * n + 1], blockwise):
            cp.wait_send()
            cp.wait_recv()

    hbm = lambda t: pltpu.HBM(t.shape, t.dtype)
    outs = pl.pallas_call(
        body, name=name,
        out_shape=tuple(hbm(t) for t in srcs) + tuple(hbm(t) for t in lands),
        in_specs=[_HBM_SPEC] * (2 * n) + [_SEM_SPEC, _SEM_SPEC, pl.BlockSpec(memory_space=pl.ANY)],
        out_specs=(_HBM_SPEC,) * (2 * n),
        input_output_aliases={i: i for i in range(2 * n)},
        compiler_params=pltpu.CompilerParams(has_side_effects=_DATAFLOW),
    )(*srcs, *lands, send_sems, recv_sems, after)
    return outs[n:]


def _exchange_grads(parts, small):
    n_arr = len(parts)

    def body(*refs):
        ins, outs = refs[:n_arr + 1], refs[n_arr + 1:2 * (n_arr + 1)]
        send_sems, recv_sems, local_sems = refs[2 * (n_arr + 1):]
        x, y, c = _mesh_pos()
        me = (x, y, c)
        my_idx = _flat_index(me)
        peers = []
        for m in range(1, N_DEV):
            peers.append((1 - x if m & 4 else x, 1 - y if m & 2 else y, 1 - c if m & 1 else c))

        def src_block(arr, dev):
            return ins[arr] if arr == n_arr else ins[arr].at[_flat_index(dev)]

        def copy(arr, k):
            return pltpu.make_async_remote_copy(
                src_ref=src_block(arr, peers[k]), dst_ref=outs[arr].at[my_idx],
                send_sem=send_sems.at[arr, k], recv_sem=recv_sems.at[arr, k],
                device_id=peers[k], device_id_type=MESH)

        def local(arr):
            return pltpu.make_async_copy(src_block(arr, me), outs[arr].at[my_idx], local_sems.at[arr])

        for arr in range(n_arr + 1):
            local(arr).start()
            for k in range(N_DEV - 1):
                copy(arr, k).start()
        for arr in range(n_arr + 1):
            for k in range(N_DEV - 1):
                cp = copy(arr, k)
                cp.wait_send()
                cp.wait_recv()
            local(arr).wait()

    any_spec = pl.BlockSpec(memory_space=pl.ANY)
    out_shape = tuple(jax.ShapeDtypeStruct(p.shape, p.dtype) for p in parts)
    out_shape += (jax.ShapeDtypeStruct((N_DEV,) + small.shape, small.dtype),)
    return pl.pallas_call(
        body, name="exchange_grads",
        out_shape=out_shape,
        in_specs=[any_spec] * (n_arr + 1), out_specs=(any_spec,) * (n_arr + 1),
        scratch_shapes=[pltpu.SemaphoreType.DMA((n_arr + 1, N_DEV - 1)),
                        pltpu.SemaphoreType.DMA((n_arr + 1, N_DEV - 1)),
                        pltpu.SemaphoreType.DMA((n_arr + 1,))],
        compiler_params=pltpu.CompilerParams(has_side_effects=True),
    )(*parts, small)


def _head_norm(t, w128, bd):
    ms = _mm_split(t * t, bd) * (1.0 / HEAD_DIM)
    r = lax.rsqrt(ms + EPS)
    return (t * r) * w128, r


def _attn_in(x2, wn1, a_g, cos2, sin2, qnw, knw, bd):
    s_len = x2.shape[0]
    tm = ROW_TILE

    def body(x_ref, wn_ref, w_ref, cos_ref, sin_ref, qnw_ref, knw_ref, bd_ref,
             h1_ref, qraw_ref, kraw_ref, q_ref, k_ref, va_ref, qs_ref, ks_ref, vs_ref,
             q4_ref, k4_ref, v4_ref, q16_ref, k16_ref, v16_ref, proj, slabs, stage):
        xx = x_ref[...]
        r = lax.rsqrt(jnp.mean(xx * xx, axis=-1, keepdims=True) + EPS)
        h = ((xx * r) * wn_ref[...]).astype(BF16)
        h1_ref[...] = h
        for d in range(N_DEV):
            proj[:, IN_SHARD * d:IN_SHARD * (d + 1)] = _dot(h, w_ref[d])
        cos_t, sin_t, bdm = cos_ref[...], sin_ref[...], bd_ref[...]
        for grp, (raw_ref, rope_ref, nw_ref) in enumerate(((qraw_ref, q_ref, qnw_ref),
                                                           (kraw_ref, k_ref, knw_ref))):
            for p in range(4):
                cols = slice(D_GRP * grp + 128 * p, D_GRP * grp + 128 * (p + 1))
                t = proj[:, cols]
                raw_ref[:, 128 * p:128 * (p + 1)] = t
                yn, _ = _head_norm(t, nw_ref[...], bdm)
                roped = yn * cos_t + _swap_halves(yn) * sin_t
                slabs[4 * grp + p] = roped
                rope_ref[:, 128 * p:128 * (p + 1)] = roped.astype(BF16)
        for p in range(4):
            slabs[8 + p] = proj[:, 2 * D_GRP + 128 * p:2 * D_GRP + 128 * (p + 1)]
        for grp, ref in ((2, va_ref), (3, qs_ref), (4, ks_ref), (5, vs_ref)):
            ref[...] = proj[:, D_GRP * grp:D_GRP * (grp + 1)].astype(BF16)
        _split_views(slabs, stage, (q4_ref, k4_ref, v4_ref), (q16_ref, k16_ref, v16_ref))

    row = lambda w: pl.BlockSpec((tm, w), lambda i: (i, 0))
    grp_bf = jax.ShapeDtypeStruct((s_len, D_GRP), BF16)
    grp_f32 = jax.ShapeDtypeStruct((s_len, D_GRP), F32)
    return pl.pallas_call(
        body, name="attn_in", grid=(s_len // tm,),
        in_specs=[row(D_MODEL), _full((1, D_MODEL)),
                  pl.BlockSpec((N_DEV, D_MODEL, IN_SHARD), lambda i: (0, 0, 0)),
                  row(128), row(128), _full((1, 128)), _full((1, 128)), _full((128, 128))],
        out_specs=(row(D_MODEL),) + (row(D_GRP),) * 8 + (_view_spec(tm, 4),) * 3 + (_view_spec(tm, 16),) * 3,
        out_shape=(jax.ShapeDtypeStruct((s_len, D_MODEL), BF16), grp_f32, grp_f32) + (grp_bf,) * 6
        + (_view_shape(s_len, 4, BF16),) * 3 + (_view_shape(s_len, 16, BF16),) * 3,
        scratch_shapes=[pltpu.VMEM((tm, D_IN), F32), pltpu.VMEM((12, tm, 128), F32), pltpu.VMEM((12, tm, 128), F32)],
        compiler_params=_params(),
    )(x2, wn1, a_g, cos2, sin2, qnw, knw, bd)


def _band_mask(n):
    i = lax.broadcasted_iota(jnp.int32, (BLOCK, 2 * BLOCK), 0)
    j = lax.broadcasted_iota(jnp.int32, (BLOCK, 2 * BLOCK), 1)
    dist = i + BLOCK - j
    return (dist >= 0) & (dist <= BLOCK) & ((n - 1) * BLOCK + j >= 0)


def _dil_fwd(qv, kv, vv, r):
    sub_len = qv.shape[0]
    nb = sub_len // BLOCK

    def body(q_ref, kp_ref, kc_ref, vp_ref, vc_ref, o_ref, lse_ref):
        n = pl.program_id(1)
        valid = _band_mask(n)
        lane = lax.broadcasted_iota(jnp.int32, (BLOCK, 128), 1)
        head0 = lane < HEAD_DIM
        for p in range(4):
            cols = slice(128 * p, 128 * (p + 1))
            q2 = q_ref[:, cols]
            kk = jnp.concatenate([kp_ref[:, cols], kc_ref[:, cols]], axis=0)
            vv2 = jnp.concatenate([vp_ref[:, cols], vc_ref[:, cols]], axis=0)
            res = []
            for h in range(2):
                qh = jnp.where(head0, q2, 0) if h == 0 else jnp.where(head0, 0, q2)
                s = jnp.where(valid, _dot_nt(qh, kk) * ATT_SCALE, NEG)
                m = jnp.max(s, axis=-1, keepdims=True)
                pr = jnp.exp(s - m)
                den = jnp.sum(pr, axis=-1, keepdims=True)
                o = _dot(pr.astype(BF16), vv2) / den
                res.append((o, m + jnp.log(den)))
            o_ref[:, cols] = jnp.where(head0, res[0][0], res[1][0])
            lse_ref[:, cols] = jnp.where(head0, res[0][1], res[1][1])

    cur = pl.BlockSpec((BLOCK, D_GRP), lambda c, n: (n, c))
    prev = pl.BlockSpec((BLOCK, D_GRP), lambda c, n: (jnp.maximum(n - 1, 0), c))
    out = jax.ShapeDtypeStruct(qv.shape, F32)
    return pl.pallas_call(
        body, name=f"dil_fwd_r{r}", grid=(r, nb),
        in_specs=[cur, prev, cur, prev, cur], out_specs=(cur, cur), out_shape=(out, out),
        compiler_params=_params(),
    )(qv, kv, kv, vv, vv)


def _dil_bwd(qv, kv, vv, dov, lsev, deltav, r):
    sub_len = qv.shape[0]
    nb = sub_len // BLOCK

    def body(q_ref, kp_ref, kc_ref, vp_ref, vc_ref, do_ref, lse_ref, dl_ref,
             dq_ref, dk_ref, dv_ref, dk_carry, dv_carry):
        n = pl.program_id(1)

        @pl.when(n == 0)
        def _():
            dk_carry[...] = jnp.zeros_like(dk_carry)
            dv_carry[...] = jnp.zeros_like(dv_carry)

        @pl.when(n < nb)
        def _():
            valid = _band_mask(n)
            lane = lax.broadcasted_iota(jnp.int32, (BLOCK, 128), 1)
            head0 = lane < HEAD_DIM
            for p in range(4):
                cols = slice(128 * p, 128 * (p + 1))
                q2, do2 = q_ref[:, cols], do_ref[:, cols]
                lse2, dl2 = lse_ref[:, cols], dl_ref[:, cols]
                kk = jnp.concatenate([kp_ref[:, cols], kc_ref[:, cols]], axis=0)
                vv2 = jnp.concatenate([vp_ref[:, cols], vc_ref[:, cols]], axis=0)
                dq_h, dkk, dvv = [], 0.0, 0.0
                for h in range(2):
                    sel = (lambda t: jnp.where(head0, t, 0)) if h == 0 else (lambda t: jnp.where(head0, 0, t))
                    qh, doh = sel(q2), sel(do2)
                    one_lane = lane == (0 if h == 0 else HEAD_DIM)
                    lse = jnp.sum(jnp.where(one_lane, lse2, 0.0), axis=-1, keepdims=True)
                    dl = jnp.sum(jnp.where(one_lane, dl2, 0.0), axis=-1, keepdims=True)
                    s = _dot_nt(qh, kk) * ATT_SCALE
                    pr = jnp.where(valid, jnp.exp(jnp.minimum(s - lse, 0.0)), 0.0)
                    dp = _dot_nt(doh, vv2)
                    ds = (pr * (dp - dl) * ATT_SCALE).astype(BF16)
                    dq_h.append(_dot(ds, kk))
                    dkk = dkk + _dot_tn(ds, qh)
                    dvv = dvv + _dot_tn(pr.astype(BF16), doh)
                dq_ref[:, cols] = jnp.where(head0, dq_h[0], dq_h[1])
                dk_ref[:, cols] = dk_carry[:, cols] + dkk[:BLOCK]
                dv_ref[:, cols] = dv_carry[:, cols] + dvv[:BLOCK]
                dk_carry[:, cols] = dkk[BLOCK:]
                dv_carry[:, cols] = dvv[BLOCK:]

        @pl.when(n == nb)
        def _():
            dk_ref[...] = dk_carry[...]
            dv_ref[...] = dv_carry[...]

    last = nb - 1
    cur = pl.BlockSpec((BLOCK, D_GRP), lambda c, n: (jnp.minimum(n, last), c))
    prev = pl.BlockSpec((BLOCK, D_GRP), lambda c, n: (jnp.clip(n - 1, 0, last), c))
    out = jax.ShapeDtypeStruct(qv.shape, F32)
    return pl.pallas_call(
        body, name=f"dil_bwd_r{r}", grid=(r, nb + 1),
        in_specs=[cur, prev, cur, prev, cur, cur, cur, cur],
        out_specs=(cur, prev, prev), out_shape=(out, out, out),
        scratch_shapes=[pltpu.VMEM((BLOCK, D_GRP), F32), pltpu.VMEM((BLOCK, D_GRP), F32)],
        compiler_params=_params(),
    )(qv, kv, kv, vv, vv, dov, lsev, deltav)


def _sb_fwd(qs, ks, vs, tri_suf):
    s_len = qs.shape[0]
    t = SB_TILE
    nq = s_len // t

    npair = SB_PAIRS

    def body(q_ref, k_ref, v_ref, u_ref, o_ref, c_ref, qq, vt, acc, cf, csave):
        row = lax.broadcasted_iota(jnp.int32, (2 * t, t), 0) & (t - 1)
        col = lax.broadcasted_iota(jnp.int32, (2 * t, t), 1)
        diag_mask = col < row
        lane1 = lax.broadcasted_iota(jnp.int32, (t, 128), 1)
        head0 = lane1 < HEAD_DIM
        lane2 = lax.broadcasted_iota(jnp.int32, (2 * t, 128), 1)
        uu = u_ref[...]
        pr = range(npair)
        cols = [slice(128 * pp, 128 * (pp + 1)) for pp in pr]

        i = pl.program_id(1)

        @pl.when(i == 0)
        def _():
            def transpose_v(j, _):
                rows = pl.ds(pl.multiple_of(j * t, t), t)
                for pp in pr:
                    vt[pp, j] = v_ref[rows, cols[pp]].astype(F32).T.astype(BF16)
                return 0

            lax.fori_loop(0, nq, transpose_v, 0)

        for pp in pr:
            q2 = q_ref[:, cols[pp]] * ATT_SCALE
            qq[pp, 0:t, :] = jnp.where(head0, q2, 0)
            qq[pp, t:2 * t, :] = jnp.where(head0, 0, q2)
        acc[...] = jnp.zeros_like(acc)
        cf[...] = jnp.zeros_like(cf)
        csave[...] = jnp.zeros_like(csave)

        def tile(kb, diag):
            krows = pl.ds(pl.multiple_of(kb * t, t), t)
            zs = [_dot_nt(qq[pp], k_ref[krows, cols[pp]]) for pp in pr]
            lbk = [_log_sigmoid_pair(z) for z in zs]
            lks = [jnp.where(diag_mask, lk, 0.0) if diag else lk for _, lk in lbk]
            sufs = [_cumsum_mm(lk, uu) for lk in lks]
            carries = [cf[pp] for pp in pr]
            avs = []
            for pp in pr:
                a = jnp.exp(lbk[pp][0] + (sufs[pp] + jnp.concatenate([carries[pp]] * (t // 128), axis=1)))
                avs.append((jnp.where(diag_mask, a, 0.0) if diag else a).astype(BF16))
            pvs = [_dot_nt(vt[pp, kb], avs[pp]) for pp in pr]
            for pp in pr:
                acc[pp] += pvs[pp]
                csave[pp] = jnp.where(lane2 == kb, carries[pp], csave[pp])
                cf[pp] = carries[pp] + jnp.broadcast_to(jnp.sum(lks[pp], axis=-1, keepdims=True), (2 * t, 128))

        tile(i, True)

        def k_block(step, _):
            tile(i - 1 - step, False)
            return 0

        lax.fori_loop(0, i, k_block, 0)
        for pp in pr:
            o_ref[:, cols[pp]] = jnp.where(head0, acc[pp, :, 0:t].T, acc[pp, :, t:2 * t].T)
            c_ref[2 * pp] = csave[pp, 0:t, :]
            c_ref[2 * pp + 1] = csave[pp, t:2 * t, :]

    width = 128 * npair
    kv = pl.BlockSpec((s_len, width), lambda p, i: (0, p))
    qo = pl.BlockSpec((t, width), lambda p, i: (i, p))
    return pl.pallas_call(
        body, name="sb_fwd", grid=(4 // npair, nq),
        in_specs=[qo, kv, kv, pl.BlockSpec((t, t), lambda p, i: (0, 0))],
        out_specs=(qo, pl.BlockSpec((2 * npair, t, 128), lambda p, i: (p, i, 0))),
        out_shape=(jax.ShapeDtypeStruct((s_len, D_GRP), F32),
                   jax.ShapeDtypeStruct((8, s_len, 128), F32)),
        scratch_shapes=[pltpu.VMEM((npair, 2 * t, 128), BF16), pltpu.VMEM((npair, nq, 128, t), BF16),
                        pltpu.VMEM((npair, 128, 2 * t), F32),
                        pltpu.VMEM((npair, 2 * t, 128), F32), pltpu.VMEM((npair, 2 * t, 128), F32)],
        compiler_params=_params(),
    )(qs, ks, vs, tri_suf)


def _sb_bwd(qs, ks, vs, dos, csaved, tri_suf, tri_pre):
    s_len = qs.shape[0]
    t = SB_TILE
    nq = s_len // t

    npair = SB_BWD_PAIRS

    def body(q_ref, k_ref, v_ref, do_ref, c_ref, u_ref, p_ref, dq_ref, dk_ref, dv_ref,
             qq, dd, qqt, ddt, kt, dq_acc, dkt, dvt, cg):
        row = lax.broadcasted_iota(jnp.int32, (2 * t, t), 0) & (t - 1)
        col = lax.broadcasted_iota(jnp.int32, (2 * t, t), 1)
        diag_mask = col < row
        lane1 = lax.broadcasted_iota(jnp.int32, (t, 128), 1)
        head0 = lane1 < HEAD_DIM
        lane2 = lax.broadcasted_iota(jnp.int32, (2 * t, 128), 1)
        uu, pm = u_ref[...], p_ref[...]
        pr = range(npair)
        cols = [slice(128 * pp, 128 * (pp + 1)) for pp in pr]
        i = pl.program_id(1)

        @pl.when(i == 0)
        def _():
            dkt[...] = jnp.zeros_like(dkt)
            dvt[...] = jnp.zeros_like(dvt)

            def transpose_k(j, _):
                rows = pl.ds(pl.multiple_of(j * t, t), t)
                for pp in pr:
                    kt[pp, j] = k_ref[rows, cols[pp]].astype(F32).T.astype(BF16)
                return 0

            lax.fori_loop(0, nq, transpose_k, 0)

        for pp in pr:
            q2 = q_ref[:, cols[pp]].astype(F32) * ATT_SCALE
            do2 = do_ref[:, cols[pp]].astype(F32)
            for src, nat, tr in ((q2, qq, qqt), (do2, dd, ddt)):
                stacked = jnp.concatenate([jnp.where(head0, src, 0.0), jnp.where(head0, 0.0, src)], axis=0)
                nat[pp] = stacked.astype(BF16)
                tr[pp] = stacked.T.astype(BF16)
        dq_acc[...] = jnp.zeros_like(dq_acc)
        cg[...] = jnp.zeros_like(cg)

        def tile(kb, diag):
            krows = pl.ds(pl.multiple_of(kb * t, t), t)
            zs = [_dot_nt(qq[pp], k_ref[krows, cols[pp]]) for pp in pr]
            das = [_dot_nt(dd[pp], v_ref[krows, cols[pp]]) for pp in pr]
            lbk = [_log_sigmoid_pair(z) for z in zs]
            lks = [jnp.where(diag_mask, lk, 0.0) if diag else lk for _, lk in lbk]
            sufs = [_cumsum_mm(lk, uu) for lk in lks]
            avs, gs = [], []
            for pp in pr:
                cs = jnp.concatenate([c_ref[2 * pp], c_ref[2 * pp + 1]], axis=0)
                cf = jnp.sum(jnp.where(lane2 == kb, cs, 0.0), axis=-1, keepdims=True)
                a = jnp.exp(lbk[pp][0] + (sufs[pp] + cf))
                a = jnp.where(diag_mask, a, 0.0) if diag else a
                avs.append(a.astype(BF16))
                gs.append(a * das[pp])
            gpres = [_cumsum_mm(g, pm) for g in gs]
            dzs = []
            for pp in pr:
                carry = cg[pp]
                beta = jnp.exp(lbk[pp][0])
                dz = gs[pp] - beta * (gs[pp] + (gpres[pp] + jnp.concatenate([carry] * (t // 128), axis=1)))
                dzs.append((jnp.where(diag_mask, dz, 0.0) if diag else dz).astype(BF16))
                cg[pp] = carry + jnp.broadcast_to(jnp.sum(gs[pp], axis=-1, keepdims=True), (2 * t, 128))
            dqs = [_dot_nt(kt[pp, kb], dzs[pp]) for pp in pr]
            dks = [_dot(qqt[pp], dzs[pp]) for pp in pr]
            dvs = [_dot(ddt[pp], avs[pp]) for pp in pr]
            for pp in pr:
                dq_acc[pp] += dqs[pp]
                dkt[pp, kb] += dks[pp]
                dvt[pp, kb] += dvs[pp]

        def k_block(kb, _):
            tile(kb, False)
            return 0

        lax.fori_loop(0, i, k_block, 0)
        tile(i, True)
        for pp in pr:
            dq_ref[:, cols[pp]] = jnp.where(head0, dq_acc[pp, :, 0:t].T, dq_acc[pp, :, t:2 * t].T) * ATT_SCALE

        @pl.when(i == nq - 1)
        def _():
            def untranspose(j, _):
                rows = pl.ds(pl.multiple_of(j * t, t), t)
                for pp in pr:
                    dk_ref[rows, cols[pp]] = dkt[pp, j].T
                    dv_ref[rows, cols[pp]] = dvt[pp, j].T
                return 0

            lax.fori_loop(0, nq, untranspose, 0)

    width = 128 * npair
    kv = pl.BlockSpec((s_len, width), lambda p, i: (0, p))
    qo = pl.BlockSpec((t, width), lambda p, i: (i, p))
    tri = pl.BlockSpec((t, t), lambda p, i: (0, 0))
    out = jax.ShapeDtypeStruct((s_len, D_GRP), F32)
    return pl.pallas_call(
        body, name="sb_bwd", grid=(4 // npair, nq),
        in_specs=[qo, kv, kv, qo, pl.BlockSpec((2 * npair, t, 128), lambda p, i: (p, i, 0)), tri, tri],
        out_specs=(qo, kv, kv), out_shape=(out, out, out),
        scratch_shapes=[pltpu.VMEM((npair, 2 * t, 128), BF16), pltpu.VMEM((npair, 2 * t, 128), BF16),
                        pltpu.VMEM((npair, 128, 2 * t), BF16), pltpu.VMEM((npair, 128, 2 * t), BF16),
                        pltpu.VMEM((npair, nq, 128, t), BF16),
                        pltpu.VMEM((npair, 128, 2 * t), F32),
                        pltpu.VMEM((npair, nq, 128, t), F32), pltpu.VMEM((npair, nq, 128, t), F32),
                        pltpu.VMEM((npair, 2 * t, 128), F32)],
        compiler_params=_params(),
    )(qs, ks, vs, dos, csaved, tri_suf, tri_pre)


def _attn_out(o_b, lse_b, o_sb, x2, wdil, wsb, b_g):
    s_len = x2.shape[0]
    tm = ROW_TILE

    def body(o1_ref, l1_ref, o4_ref, l4_ref, o16_ref, l16_ref, osb_ref, x_ref, wdil_ref, wsb_ref, w_ref,
             odil_ref, lse_ref, lse4_ref, lse16_ref, mixed_ref, x1_ref, stage, nat4, nat16):
        _merge_views((o4_ref, l4_ref), (o16_ref, l16_ref), stage, nat4, nat16)
        os_ = (o1_ref[...], _slab_group(nat4, 0), _slab_group(nat16, 0))
        ls = (l1_ref[...], _slab_group(nat4, 1), _slab_group(nat16, 1))
        mx = jnp.maximum(jnp.maximum(ls[0], ls[1]), ls[2])
        es = [jnp.exp(l - mx) for l in ls]
        den = es[0] + es[1] + es[2]
        o_dil = (es[0] * os_[0] + es[1] * os_[1] + es[2] * os_[2]) / den
        odil_ref[...] = o_dil
        lse = mx + jnp.log(den)
        lse_ref[...] = lse
        for p in range(4):
            nat4[p] = lse[:, 128 * p:128 * (p + 1)]
        _split_views(nat4.at[0:4], stage.at[0:4], (lse4_ref,), (lse16_ref,))
        halves = []
        for t, w_r in ((o_dil, wdil_ref), (osb_ref[...], wsb_ref)):
            r = lax.rsqrt(jnp.mean(t * t, axis=-1, keepdims=True) + EPS)
            halves.append(((t * r) * w_r[...]).astype(BF16))
        mixed = jnp.concatenate(halves, axis=1)
        mixed_ref[...] = mixed
        w = w_ref[...].reshape(D_MODEL, D_MODEL)
        x1_ref[...] = x_ref[...] + _dot(mixed, w)

    row = lambda w: pl.BlockSpec((tm, w), lambda i: (i, 0))
    return pl.pallas_call(
        body, name="attn_out", grid=(s_len // tm,),
        in_specs=[row(D_GRP)] * 2 + [_view_spec(tm, 4)] * 2 + [_view_spec(tm, 16)] * 2
        + [row(D_GRP), row(D_MODEL), _full((1, D_GRP)), _full((1, D_GRP)),
           pl.BlockSpec((N_DEV, OUT_SHARD, D_MODEL), lambda i: (0, 3, 0))],
        out_specs=(row(D_GRP), row(D_GRP), _view_spec(tm, 4), _view_spec(tm, 16), row(D_MODEL), row(D_MODEL)),
        out_shape=(jax.ShapeDtypeStruct((s_len, D_GRP), F32), jax.ShapeDtypeStruct((s_len, D_GRP), F32),
                   _view_shape(s_len, 4, F32), _view_shape(s_len, 16, F32),
                   jax.ShapeDtypeStruct((s_len, D_MODEL), BF16), jax.ShapeDtypeStruct((s_len, D_MODEL), F32)),
        scratch_shapes=[pltpu.VMEM((8, tm, 128), F32)] * 3,
        compiler_params=_params(),
    )(o_b[0], lse_b[0], o_b[1], lse_b[1], o_b[2], lse_b[2], o_sb, x2, wdil, wsb, b_g)


def _ffn_fwd(x1, wn2, tgt, gu_g, b_g):
    s_len = x1.shape[0]
    tm = ROW_TILE
    ni = s_len // tm

    def body(x_ref, wn_ref, t_ref, wg_ref, wu_ref, wd_ref, g_ref, u_ref, h2_ref, dy_ref, loss_ref, acc):
        j = pl.program_id(1)

        @pl.when(j == 0)
        def _():
            xx = x_ref[...]
            r = lax.rsqrt(jnp.mean(xx * xx, axis=-1, keepdims=True) + EPS)
            h2_ref[...] = ((xx * r) * wn_ref[...]).astype(BF16)
            acc[...] = jnp.zeros_like(acc)

        h = h2_ref[...]
        g = _dot(h, wg_ref[0])
        u = _dot(h, wu_ref[0])
        g_ref[...] = g
        u_ref[...] = u
        act = (g * (1.0 / (1.0 + jnp.exp(-g)))) * u
        acc[...] += _dot(act.astype(BF16), wd_ref[0])

        @pl.when(j == N_DEV - 1)
        def _():
            err = (x_ref[...] + acc[...]) - t_ref[...]
            dy_ref[...] = err * (1.0 / D_MODEL)
            part = 0.5 * jnp.sum(jnp.mean(err * err, axis=-1, keepdims=True))
            loss_ref[...] = jnp.full((8, 128), part, F32)

    row = pl.BlockSpec((tm, D_MODEL), lambda i, j: (i, 0))
    hid = pl.BlockSpec((tm, FF_PAD), lambda i, j: (i, j))
    return pl.pallas_call(
        body, name="ffn_fwd", grid=(ni, N_DEV),
        in_specs=[row, pl.BlockSpec((1, D_MODEL), lambda i, j: (0, 0)), row,
                  pl.BlockSpec((1, D_MODEL, FF_PAD), lambda i, j: (j, 0, 0)),
                  pl.BlockSpec((1, D_MODEL, FF_PAD), lambda i, j: (j, 1, 0)),
                  pl.BlockSpec((1, FF_PAD, D_MODEL), lambda i, j: (j, 0, 0))],
        out_specs=(hid, hid, row, row, pl.BlockSpec((8, 128), lambda i, j: (i, 0))),
        out_shape=(jax.ShapeDtypeStruct((s_len, N_DEV * FF_PAD), F32),
                   jax.ShapeDtypeStruct((s_len, N_DEV * FF_PAD), F32),
                   jax.ShapeDtypeStruct((s_len, D_MODEL), BF16),
                   jax.ShapeDtypeStruct((s_len, D_MODEL), F32),
                   jax.ShapeDtypeStruct((ni * 8, 128), F32)),
        scratch_shapes=[pltpu.VMEM((tm, D_MODEL), F32)],
        compiler_params=_params(),
    )(x1, wn2, tgt, gu_g, gu_g, b_g)


def _ffn_bwd_dx(dy, g, u, gu_g, b_g):
    s_len = dy.shape[0]
    tm = ROW_TILE

    def body(dy_ref, g_ref, u_ref, wg_ref, wu_ref, wd_ref, dg_ref, du_ref, act_ref, dh_ref, acc):
        j = pl.program_id(1)

        @pl.when(j == 0)
        def _():
            acc[...] = jnp.zeros_like(acc)

        gg, uu = g_ref[...], u_ref[...]
        da = _dot_nt(dy_ref[...].astype(BF16), wd_ref[0])
        sig = 1.0 / (1.0 + jnp.exp(-gg))
        silu = gg * sig
        act_ref[...] = (silu * uu).astype(BF16)
        du = (da * silu).astype(BF16)
        dg = (da * uu * (sig * (1.0 + gg * (1.0 - sig)))).astype(BF16)
        du_ref[...] = du
        dg_ref[...] = dg
        acc[...] += _dot_nt(dg, wg_ref[0]) + _dot_nt(du, wu_ref[0])

        @pl.when(j == N_DEV - 1)
        def _():
            dh_ref[...] = acc[...]

    row = pl.BlockSpec((tm, D_MODEL), lambda i, j: (i, 0))
    hid = pl.BlockSpec((tm, FF_PAD), lambda i, j: (i, j))
    hid_bf = jax.ShapeDtypeStruct((s_len, N_DEV * FF_PAD), BF16)
    return pl.pallas_call(
        body, name="ffn_bwd_dx", grid=(s_len // tm, N_DEV),
        in_specs=[row, hid, hid,
                  pl.BlockSpec((1, D_MODEL, FF_PAD), lambda i, j: (j, 0, 0)),
                  pl.BlockSpec((1, D_MODEL, FF_PAD), lambda i, j: (j, 1, 0)),
                  pl.BlockSpec((1, FF_PAD, D_MODEL), lambda i, j: (j, 0, 0))],
        out_specs=(hid, hid, hid, row),
        out_shape=(hid_bf, hid_bf, hid_bf, jax.ShapeDtypeStruct((s_len, D_MODEL), F32)),
        scratch_shapes=[pltpu.VMEM((tm, D_MODEL), F32)],
        compiler_params=_params(),
    )(dy, g, u, gu_g, gu_g, b_g)


def _ffn_bwd_dw(h2, dy, dg, du, act):
    s_len = h2.shape[0]
    tm = ROW_TILE
    ni = s_len // tm

    def body(h_ref, dy_ref, dg_ref, du_ref, act_ref, dwg_ref, dwu_ref, dwd_ref, ag, au, ad):
        i = pl.program_id(1)

        @pl.when(i == 0)
        def _():
            ag[...] = jnp.zeros_like(ag)
            au[...] = jnp.zeros_like(au)
            ad[...] = jnp.zeros_like(ad)

        h = h_ref[...]
        ag[...] += _dot_tn(h, dg_ref[...])
        au[...] += _dot_tn(h, du_ref[...])
        ad[...] += _dot_tn(act_ref[...], dy_ref[...].astype(BF16))

        @pl.when(i == ni - 1)
        def _():
            dwg_ref[0] = ag[...].astype(BF16)
            dwu_ref[0] = au[...].astype(BF16)
            dwd_ref[0] = ad[...].astype(BF16)

    row = pl.BlockSpec((tm, D_MODEL), lambda j, i: (i, 0))
    hid = pl.BlockSpec((tm, FF_PAD), lambda j, i: (i, j))
    col_w = pl.BlockSpec((1, D_MODEL, FF_PAD), lambda j, i: (j, 0, 0))
    row_w = pl.BlockSpec((1, FF_PAD, D_MODEL), lambda j, i: (j, 0, 0))
    return pl.pallas_call(
        body, name="ffn_bwd_dw", grid=(N_DEV, ni),
        in_specs=[row, row, hid, hid, hid], out_specs=(col_w, col_w, row_w),
        out_shape=(jax.ShapeDtypeStruct((N_DEV, D_MODEL, FF_PAD), BF16),
                   jax.ShapeDtypeStruct((N_DEV, D_MODEL, FF_PAD), BF16),
                   jax.ShapeDtypeStruct((N_DEV, FF_PAD, D_MODEL), BF16)),
        scratch_shapes=[pltpu.VMEM((D_MODEL, FF_PAD), F32), pltpu.VMEM((D_MODEL, FF_PAD), F32),
                        pltpu.VMEM((FF_PAD, D_MODEL), F32)],
        compiler_params=_params(),
    )(h2, dy, dg, du, act)


def _rms_bwd(dy, t, w):
    r = lax.rsqrt(jnp.mean(t * t, axis=-1, keepdims=True) + EPS)
    gw = dy * w
    dt = r * (gw - t * ((r * r) * jnp.mean(gw * t, axis=-1, keepdims=True)))
    return dt, dy * t * r


def _attn_out_bwd(dy, dh2, x1, wn2, b_g, mixed, o_dil, o_sb, wdil, wsb, bd512):
    s_len = dy.shape[0]
    tm = ROW_TILE
    ni = s_len // tm

    def body(dy_ref, dh_ref, x1_ref, wn_ref, w_ref, mixed_ref, odil_ref, osb_ref, wdil_ref, wsb_ref, bd_ref,
             dx1_ref, dodil_ref, delta_ref, dosb_ref, dwout_ref, dwn_ref, dwdil_ref, dwsb_ref,
             do4_ref, dl4_ref, do16_ref, dl16_ref, wacc, both, stage):
        i = pl.program_id(0)

        @pl.when(i == 0)
        def _():
            wacc[...] = jnp.zeros_like(wacc)
            dwn_ref[...] = jnp.zeros_like(dwn_ref)
            dwdil_ref[...] = jnp.zeros_like(dwdil_ref)
            dwsb_ref[...] = jnp.zeros_like(dwsb_ref)

        dnorm, dw_rows = _rms_bwd(dh_ref[...], x1_ref[...], wn_ref[...])
        dx1 = dy_ref[...] + dnorm
        dx1_ref[...] = dx1
        dwn_ref[...] += jnp.sum(dw_rows, axis=0, keepdims=True)
        dx1b = dx1.astype(BF16)
        w = w_ref[...].reshape(D_MODEL, D_MODEL)
        dmixed = _dot_nt(dx1b, w)
        wacc[...] += _dot_tn(mixed_ref[...], dx1b)
        o_dil = odil_ref[...]
        d_odil, dw_rows = _rms_bwd(dmixed[:, :D_GRP], o_dil, wdil_ref[...])
        dwdil_ref[...] += jnp.sum(dw_rows, axis=0, keepdims=True)
        dodil_ref[...] = d_odil.astype(BF16)
        delta = _mm_split(d_odil * o_dil, bd_ref[...])
        delta_ref[...] = delta
        for p in range(4):
            both[p] = d_odil[:, 128 * p:128 * (p + 1)]
            both[4 + p] = delta[:, 128 * p:128 * (p + 1)]
        _split_views(both, stage, (do4_ref, dl4_ref), (do16_ref, dl16_ref))
        d_osb, dw_rows = _rms_bwd(dmixed[:, D_GRP:], osb_ref[...], wsb_ref[...])
        dwsb_ref[...] += jnp.sum(dw_rows, axis=0, keepdims=True)
        dosb_ref[...] = d_osb.astype(BF16)

        @pl.when(i == ni - 1)
        def _():
            dwout_ref[...] = wacc[...].astype(BF16).reshape(N_DEV, OUT_SHARD, D_MODEL)

    row = lambda w: pl.BlockSpec((tm, w), lambda i: (i, 0))
    return pl.pallas_call(
        body, name="attn_out_bwd", grid=(ni,),
        in_specs=[row(D_MODEL), row(D_MODEL), row(D_MODEL), _full((1, D_MODEL)),
                  pl.BlockSpec((N_DEV, OUT_SHARD, D_MODEL), lambda i: (0, 3, 0)),
                  row(D_MODEL), row(D_GRP), row(D_GRP), _full((1, D_GRP)), _full((1, D_GRP)),
                  _full((D_GRP, D_GRP))],
        out_specs=(row(D_MODEL), row(D_GRP), row(D_GRP), row(D_GRP),
                   _full((N_DEV, OUT_SHARD, D_MODEL)), _full((1, D_MODEL)), _full((1, D_GRP)), _full((1, D_GRP)),
                   _view_spec(tm, 4), _view_spec(tm, 4), _view_spec(tm, 16), _view_spec(tm, 16)),
        out_shape=(jax.ShapeDtypeStruct((s_len, D_MODEL), F32), jax.ShapeDtypeStruct((s_len, D_GRP), BF16),
                   jax.ShapeDtypeStruct((s_len, D_GRP), F32), jax.ShapeDtypeStruct((s_len, D_GRP), BF16),
                   jax.ShapeDtypeStruct((N_DEV, OUT_SHARD, D_MODEL), BF16),
                   jax.ShapeDtypeStruct((1, D_MODEL), F32), jax.ShapeDtypeStruct((1, D_GRP), F32),
                   jax.ShapeDtypeStruct((1, D_GRP), F32),
                   _view_shape(s_len, 4, BF16), _view_shape(s_len, 4, F32),
                   _view_shape(s_len, 16, BF16), _view_shape(s_len, 16, F32)),
        scratch_shapes=[pltpu.VMEM((D_MODEL, D_MODEL), F32), pltpu.VMEM((8, tm, 128), F32),
                        pltpu.VMEM((8, tm, 128), F32)],
        compiler_params=_params(),
    )(dy, dh2, x1, wn2, b_g, mixed, o_dil, o_sb, wdil, wsb, bd512)


def _qkv_bwd(dq_b, dk_b, dv_b, dqs, dks, dvs, qraw, kraw, cos2, sin2, qnw, knw, bd):
    s_len = qraw.shape[0]
    tm = ROW_TILE
    ni = s_len // tm

    def body(dq1, dk1, dv1, dq4, dk4, dv4, dq16, dk16, dv16, dqs_ref, dks_ref, dvs_ref,
             qraw_ref, kraw_ref, cos_ref, sin_ref, qnw_ref, knw_ref, bd_ref,
             dproj_ref, dqn_ref, dkn_ref, stage, nat4, nat16):
        i = pl.program_id(0)

        @pl.when(i == 0)
        def _():
            dqn_ref[...] = jnp.zeros_like(dqn_ref)
            dkn_ref[...] = jnp.zeros_like(dkn_ref)

        _merge_views((dq4, dk4, dv4), (dq16, dk16, dv16), stage, nat4, nat16)
        cos_t, sin_t, bdm = cos_ref[...], sin_ref[...], bd_ref[...]
        for grp, (part1, raw_ref, nw_ref, dn_ref) in enumerate(((dq1, qraw_ref, qnw_ref, dqn_ref),
                                                                (dk1, kraw_ref, knw_ref, dkn_ref))):
            dn_acc = 0.0
            for p in range(4):
                cols = slice(128 * p, 128 * (p + 1))
                d_rope = part1[:, cols] + nat4[4 * grp + p] + nat16[4 * grp + p]
                d_norm = d_rope * cos_t + _swap_halves(d_rope * sin_t)
                t = raw_ref[:, cols]
                w = nw_ref[...]
                r = lax.rsqrt(_mm_split(t * t, bdm) * (1.0 / HEAD_DIM) + EPS)
                gw = d_norm * w
                corr = _mm_split(gw * t, bdm) * (1.0 / HEAD_DIM)
                dt = r * (gw - t * ((r * r) * corr))
                dn_acc = dn_acc + jnp.sum(d_norm * t * r, axis=0, keepdims=True)
                dproj_ref[:, D_GRP * grp + 128 * p:D_GRP * grp + 128 * (p + 1)] = dt.astype(BF16)
            dn_ref[...] += dn_acc
        dproj_ref[:, 2 * D_GRP:3 * D_GRP] = (dv1[...] + _slab_group(nat4, 2) + _slab_group(nat16, 2)).astype(BF16)
        dproj_ref[:, 3 * D_GRP:4 * D_GRP] = dqs_ref[...].astype(BF16)
        dproj_ref[:, 4 * D_GRP:5 * D_GRP] = dks_ref[...].astype(BF16)
        dproj_ref[:, 5 * D_GRP:6 * D_GRP] = dvs_ref[...].astype(BF16)

    row = lambda w: pl.BlockSpec((tm, w), lambda i: (i, 0))
    return pl.pallas_call(
        body, name="qkv_bwd", grid=(ni,),
        in_specs=[row(D_GRP)] * 3 + [_view_spec(tm, 4)] * 3 + [_view_spec(tm, 16)] * 3 + [row(D_GRP)] * 5
        + [row(128), row(128), _full((1, 128)), _full((1, 128)), _full((128, 128))],
        out_specs=(row(D_IN), _full((1, 128)), _full((1, 128))),
        out_shape=(jax.ShapeDtypeStruct((s_len, D_IN), BF16), jax.ShapeDtypeStruct((1, 128), F32),
                   jax.ShapeDtypeStruct((1, 128), F32)),
        scratch_shapes=[pltpu.VMEM((12, tm, 128), F32)] * 3,
        compiler_params=_params(),
    )(dq_b[0], dk_b[0], dv_b[0], dq_b[1], dk_b[1], dv_b[1], dq_b[2], dk_b[2], dv_b[2],
      dqs, dks, dvs, qraw, kraw, cos2, sin2, qnw, knw, bd)


def _in_bwd_dx(dproj, a_g, x2, dx1, wn1):
    s_len = x2.shape[0]
    tm = ROW_TILE
    ni = s_len // tm

    def body(dp_ref, w_ref, x_ref, dx1_ref, wn_ref, gx_ref, dwn_ref):
        i = pl.program_id(0)

        @pl.when(i == 0)
        def _():
            dwn_ref[...] = jnp.zeros_like(dwn_ref)

        dh = 0.0
        for d in range(N_DEV):
            dh = dh + _dot_nt(dp_ref[:, IN_SHARD * d:IN_SHARD * (d + 1)], w_ref[d])
        dnorm, dw_rows = _rms_bwd(dh, x_ref[...], wn_ref[...])
        gx_ref[...] = dx1_ref[...] + dnorm
        dwn_ref[...] += jnp.sum(dw_rows, axis=0, keepdims=True)

    row = lambda w: pl.BlockSpec((tm, w), lambda i: (i, 0))
    return pl.pallas_call(
        body, name="in_bwd_dx", grid=(ni,),
        in_specs=[row(D_IN), pl.BlockSpec((N_DEV, D_MODEL, IN_SHARD), lambda i: (0, 0, 0)),
                  row(D_MODEL), row(D_MODEL), _full((1, D_MODEL))],
        out_specs=(row(D_MODEL), _full((1, D_MODEL))),
        out_shape=(jax.ShapeDtypeStruct((s_len, D_MODEL), F32), jax.ShapeDtypeStruct((1, D_MODEL), F32)),
        compiler_params=_params(),
    )(dproj, a_g, x2, dx1, wn1)


def _in_bwd_dw(h1, dproj):
    s_len = h1.shape[0]
    tm = ROW_TILE
    ni = s_len // tm

    def body(h_ref, dp_ref, dw_ref, acc):
        i = pl.program_id(1)

        @pl.when(i == 0)
        def _():
            acc[...] = jnp.zeros_like(acc)

        acc[...] += _dot_tn(h_ref[...], dp_ref[...])

        @pl.when(i == ni - 1)
        def _():
            dw_ref[0] = acc[...].astype(BF16)

    return pl.pallas_call(
        body, name="in_bwd_dw", grid=(N_DEV, ni),
        in_specs=[pl.BlockSpec((tm, D_MODEL), lambda d, i: (i, 0)),
                  pl.BlockSpec((tm, IN_SHARD), lambda d, i: (i, d))],
        out_specs=pl.BlockSpec((1, D_MODEL, IN_SHARD), lambda d, i: (d, 0, 0)),
        out_shape=jax.ShapeDtypeStruct((N_DEV, D_MODEL, IN_SHARD), BF16),
        scratch_shapes=[pltpu.VMEM((D_MODEL, IN_SHARD), F32)],
        compiler_params=_params(),
    )(h1, dproj)


def _adamw(recv, w, m, v):
    rows, cols = w.shape
    tr = 128 if rows % 128 == 0 else rows

    def body(p_ref, w_ref, m_ref, v_ref, g_ref, d_ref, nm_ref, nv_ref):
        g = p_ref[0].astype(F32)
        for s in range(1, N_DEV):
            g = g + p_ref[s].astype(F32)
        m_new = ADAM_B1 * m_ref[...] + (1.0 - ADAM_B1) * g
        v_new = ADAM_B2 * v_ref[...] + (1.0 - ADAM_B2) * (g * g)
        m_hat = m_new / (1.0 - ADAM_B1 ** ADAM_STEP)
        v_hat = v_new / (1.0 - ADAM_B2 ** ADAM_STEP)
        g_ref[...] = g
        d_ref[...] = -ADAM_LR * (m_hat / (jnp.sqrt(v_hat) + ADAM_EPS) + ADAM_WD * w_ref[...])
        nm_ref[...] = m_new
        nv_ref[...] = v_new

    blk = pl.BlockSpec((tr, cols), lambda i: (i, 0))
    out = jax.ShapeDtypeStruct((rows, cols), F32)
    return pl.pallas_call(
        body, name=f"adamw_{rows}x{cols}", grid=(rows // tr,),
        in_specs=[pl.BlockSpec((N_DEV, tr, cols), lambda i: (0, i, 0)), blk, blk, blk],
        out_specs=(blk,) * 4, out_shape=(out,) * 4,
        compiler_params=_params(),
    )(recv, w, m, v)


def _rope_tables(s_len):
    pos = jnp.arange(s_len, dtype=F32)
    inv_freq = ROPE_THETA ** (-jnp.arange(0, HEAD_DIM, 2, dtype=F32) / HEAD_DIM)
    ang = pos[:, None] * inv_freq[None, :]
    cos, sin = jnp.cos(ang), jnp.sin(ang)
    cos2 = jnp.concatenate([cos, cos, cos, cos], axis=1)
    sin2 = jnp.concatenate([-sin, sin, -sin, sin], axis=1)
    return cos2, sin2


def _block_diag_ones(n):
    i = jnp.arange(n)
    return (i[:, None] // HEAD_DIM == i[None, :] // HEAD_DIM).astype(BF16)


def _pad_cols(t):
    return jnp.pad(t, ((0, 0), (0, FF_PAD - FF_SHARD)))


def _pad_rows(t):
    return jnp.pad(t, ((0, FF_PAD - FF_SHARD), (0, 0)))


def _pack_small(n1, n2, ndil, nsb, nq, nk):
    pad = lambda t: jnp.pad(t.reshape(1, HEAD_DIM), ((0, 0), (0, 128 - HEAD_DIM)))
    rows = [n1.reshape(8, 128), n2.reshape(8, 128), ndil.reshape(4, 128), nsb.reshape(4, 128),
            pad(nq), pad(nk), jnp.zeros((6, 128), F32)]
    return jnp.concatenate(rows, axis=0)


def _unpack_small(t):
    return (t[0:8].reshape(1, D_MODEL), t[8:16].reshape(1, D_MODEL), t[16:20].reshape(1, D_GRP),
            t[20:24].reshape(1, D_GRP), t[24:25, :HEAD_DIM], t[25:26, :HEAD_DIM])


def kernel(x, attn_norm_w, w_in, q_norm_w, k_norm_w, dil_out_norm_w, sb_out_norm_w, w_out, ffn_norm_w, w_gate, w_up, w_down, loss_target, m_attn_norm_w, m_w_in, m_q_norm_w, m_k_norm_w, m_dil_out_norm_w, m_sb_out_norm_w, m_w_out, m_ffn_norm_w, m_w_gate, m_w_up, m_w_down, v_attn_norm_w, v_w_in, v_q_norm_w, v_k_norm_w, v_dil_out_norm_w, v_sb_out_norm_w, v_w_out, v_ffn_norm_w, v_w_gate, v_w_up, v_w_down):
    s_len = x.shape[1]
    x2, tgt = x[0], loss_target[0]

    my_idx = _flat_index(_mesh_pos())
    slot_is_mine = (jnp.arange(N_DEV) == my_idx)[:, None, None]

    (a_g,) = _gather_weights([w_in[0].astype(BF16)])
    gu_loc = jnp.concatenate([_pad_cols(w_gate[0]), _pad_cols(w_up[0])], axis=0).astype(BF16)
    b_loc = jnp.concatenate([_pad_rows(w_down[0]), w_out[0]], axis=0).astype(BF16)
    own_in_place = lambda t: jnp.where(slot_is_mine, t[None], jnp.zeros((), t.dtype))
    w_send, w_recv, w_srcs, w_lands, w_token = _spread_start(
        [gu_loc, b_loc], [own_in_place(gu_loc), own_in_place(b_loc)], blockwise=False, name="weights_start")

    cos2, sin2 = _rope_tables(s_len)
    bd128, bd512 = _block_diag_ones(128), _block_diag_ones(D_GRP)
    idx = jnp.arange(SB_TILE)
    tri_suf = (idx[:, None] > idx[None, :]).astype(BF16)
    tri_pre = (idx[:, None] < idx[None, :]).astype(BF16)
    qnw2 = jnp.concatenate([q_norm_w, q_norm_w], axis=1) + w_token[0:1]
    knw2 = jnp.concatenate([k_norm_w, k_norm_w], axis=1)

    (h1, qraw, kraw, q, k, va, qs, ks, vs,
     q4, k4, v4, q16, k16, v16) = _attn_in(x2, attn_norm_w, a_g, cos2, sin2, qnw2, knw2, bd128)
    qkv_views = {1: (q, k, va), 4: (q4, k4, v4), 16: (q16, k16, v16)}
    o_b, lse_b = [], []
    for r in DILATIONS:
        o, lse = _dil_fwd(*qkv_views[r], r)
        o_b.append(o)
        lse_b.append(lse)
    o_sb, c_sb = _sb_fwd(qs, ks, vs, tri_suf)
    gu_g, b_g = _spread_wait(w_send, w_recv, w_srcs, w_lands, c_sb, blockwise=False, name="weights_wait")
    o_dil, lse_tot, lse4, lse16, mixed, x1 = _attn_out(o_b, lse_b, o_sb, x2, dil_out_norm_w, sb_out_norm_w, b_g)
    g, u, h2, dy, loss_parts = _ffn_fwd(x1, ffn_norm_w, tgt, gu_g, b_g)
    loss = lax.psum(jnp.sum(loss_parts[::8, 0]), ("x", "y", "c"))

    dg, du, act, dh2 = _ffn_bwd_dx(dy, g, u, gu_g, b_g)
    (dx1, do_dil, delta, do_sb, dwout, dn2, dndil, dnsb, do4, dl4, do16, dl16) = _attn_out_bwd(
        dy, dh2, x1, ffn_norm_w, b_g, mixed, o_dil, o_sb, dil_out_norm_w, sb_out_norm_w, bd512)
    dwg, dwu, dwd = _ffn_bwd_dw(h2, dy, dg, du, act)
    early = [dwg, dwu, dwd, dwout]
    own_slot_only = lambda t: jnp.where(slot_is_mine, t, jnp.zeros((), t.dtype))
    g_send, g_recv, g_srcs, g_lands, g_token = _spread_start(
        early, [own_slot_only(t) for t in early], blockwise=True, name="grads_start")
    tri_pre = tri_pre + g_token[0, 0].astype(BF16)
    dqs, dks, dvs = _sb_bwd(qs, ks, vs, do_sb, c_sb, tri_suf, tri_pre)
    cot_views = {1: (do_dil, lse_tot, delta), 4: (do4, lse4, dl4), 16: (do16, lse16, dl16)}
    dq_b, dk_b, dv_b = [], [], []
    for r in DILATIONS:
        dq, dk, dv = _dil_bwd(*qkv_views[r], *cot_views[r], r)
        dq_b.append(dq)
        dk_b.append(dk)
        dv_b.append(dv)
    dproj, dqn2, dkn2 = _qkv_bwd(dq_b, dk_b, dv_b, dqs, dks, dvs, qraw, kraw, cos2, sin2, qnw2, knw2, bd128)
    grad_x, dn1 = _in_bwd_dx(dproj, a_g, x2, dx1, attn_norm_w)
    dwin = _in_bwd_dw(h1, dproj)
    dqn = dqn2[:, :HEAD_DIM] + dqn2[:, HEAD_DIM:]
    dkn = dkn2[:, :HEAD_DIM] + dkn2[:, HEAD_DIM:]

    small = _pack_small(dn1, dn2, dndil, dnsb, dqn, dkn)
    r_in, r_small = _exchange_grads([dwin], small)
    r_gate, r_up, r_down, r_out = _spread_wait(g_send, g_recv, g_srcs, g_lands, r_small, blockwise=True,
                                               name="grads_wait")
    big = {
        "w_in": _adamw(r_in, w_in[0], m_w_in[0], v_w_in[0]),
        "w_gate": tuple(t[:, :FF_SHARD] for t in _adamw(r_gate, _pad_cols(w_gate[0]), _pad_cols(m_w_gate[0]), _pad_cols(v_w_gate[0]))),
        "w_up": tuple(t[:, :FF_SHARD] for t in _adamw(r_up, _pad_cols(w_up[0]), _pad_cols(m_w_up[0]), _pad_cols(v_w_up[0]))),
        "w_down": tuple(t[:FF_SHARD] for t in _adamw(r_down, _pad_rows(w_down[0]), _pad_rows(m_w_down[0]), _pad_rows(v_w_down[0]))),
        "w_out": _adamw(r_out, w_out[0], m_w_out[0], v_w_out[0]),
    }
    packs = [_pack_small(*ts) for ts in (
        (attn_norm_w, ffn_norm_w, dil_out_norm_w, sb_out_norm_w, q_norm_w, k_norm_w),
        (m_attn_norm_w, m_ffn_norm_w, m_dil_out_norm_w, m_sb_out_norm_w, m_q_norm_w, m_k_norm_w),
        (v_attn_norm_w, v_ffn_norm_w, v_dil_out_norm_w, v_sb_out_norm_w, v_q_norm_w, v_k_norm_w))]
    small_out = [_unpack_small(t) for t in _adamw(r_small, *packs)]
    names = ["attn_norm_w", "w_in", "q_norm_w", "k_norm_w", "dil_out_norm_w", "sb_out_norm_w", "w_out",
             "ffn_norm_w", "w_gate", "w_up", "w_down"]
    small_pos = {"attn_norm_w": 0, "ffn_norm_w": 1, "dil_out_norm_w": 2, "sb_out_norm_w": 3,
                 "q_norm_w": 4, "k_norm_w": 5}
    outs = [loss, grad_x[None]]
    for kind in range(4):
        for name in names:
            if name in small_pos:
                outs.append(small_out[kind][small_pos[name]])
            else:
                outs.append(big[name][kind][None])
    return tuple(outs)
```

```python
import functools

import jax
import jax.numpy as jnp
from jax import lax
from jax.experimental import pallas as pl
from jax.experimental.pallas import tpu as pltpu

F32 = jnp.float32
BF16 = jnp.bfloat16

N_DEV = 8
D_MODEL = 1024
HEAD_DIM = 64
D_GRP = 512
D_IN = 6 * D_GRP
IN_SHARD = D_IN // N_DEV
FF_SHARD = 352
FF_PAD = 384
FF_BLOCK = 2 * FF_PAD
FF_STEPS = N_DEV // 2
OUT_SHARD = D_MODEL // N_DEV
BLOCK = 128
DILATIONS = (1, 4, 16)
ROPE_THETA = 10000.0
EPS = 1e-6
ATT_SCALE = HEAD_DIM ** -0.5
NEG = -1e30

ADAM_LR = 0.001
ADAM_B1 = 0.9
ADAM_B2 = 0.999
ADAM_EPS = 1e-08
ADAM_WD = 0.01
ADAM_STEP = 10

SB_TILE = 256
SB_PAIRS = 4
SB_BWD_PAIRS = 2
ROW_TILE = 512
VMEM_LIMIT = 56 * 1024 * 1024
MESH = pl.DeviceIdType.MESH


def _dot(a, b):
    return jnp.dot(a, b, preferred_element_type=F32)


def _dot_nt(a, b):
    return lax.dot_general(a, b, (((1,), (1,)), ((), ())), preferred_element_type=F32)


def _dot_tn(a, b):
    return lax.dot_general(a, b, (((0,), (0,)), ((), ())), preferred_element_type=F32)


def _mm_split(t, m):
    hi = t.astype(BF16)
    lo = (t - hi.astype(F32)).astype(BF16)
    return _dot(hi, m) + _dot(lo, m)


def _params(**kw):
    return pltpu.CompilerParams(vmem_limit_bytes=VMEM_LIMIT, **kw)


def _full(shape):
    nd = len(shape)
    return pl.BlockSpec(shape, lambda *_: (0,) * nd)


def _view_shape(s_len, r, dtype):
    return jax.ShapeDtypeStruct((s_len // r, r * D_GRP), dtype)


def _view_spec(tm, r):
    return pl.BlockSpec((tm // r, r * D_GRP), lambda i: (i, 0))


def _swap_halves(t):
    lane = lax.broadcasted_iota(jnp.int32, t.shape, 1)
    first = (lane & 32) == 0
    return jnp.where(first, pltpu.roll(t, 96, 1), pltpu.roll(t, 32, 1))


def _log_sigmoid(z):
    return jnp.minimum(z, 0.0) - jnp.log(1.0 + jnp.exp(-jnp.abs(z)))


def _log_sigmoid_pair(z):
    neg_abs = lax.bitcast_convert_type(lax.bitcast_convert_type(z, jnp.uint32) | jnp.uint32(0x80000000), F32)
    lb = jnp.minimum(z, 0.0) - jnp.log(1.0 + jnp.exp(neg_abs))
    return lb, lb - z


def _cumsum_mm(t, tri):
    return _dot(t.astype(BF16), tri)


def _split_views(src_ref, stage_ref, views4, views16):
    slabs, n, _ = src_ref.shape
    n4, n16 = n // 4, n // 16
    for j in range(slabs):
        g, lanes = j // 4, 128 * (j % 4)
        src, stage = src_ref.at[j], stage_ref.at[j]
        for c4 in range(4):
            blk = src[pl.ds(c4, n4, stride=4), :]
            stage[n4 * c4:n4 * (c4 + 1), :] = blk
            col = D_GRP * c4 + lanes
            views4[g][:, col:col + 128] = blk.astype(views4[g].dtype)
        for c4 in range(4):
            for c1 in range(4):
                blk = stage[pl.ds(n4 * c4 + c1, n16, stride=4), :]
                col = D_GRP * (4 * c1 + c4) + lanes
                views16[g][:, col:col + 128] = blk.astype(views16[g].dtype)


def _merge_views(views4, views16, stage_ref, dst4_ref, dst16_ref):
    slabs, n, _ = dst4_ref.shape
    n4, n16 = n // 4, n // 16
    for j in range(slabs):
        g, lanes = j // 4, 128 * (j % 4)
        dst4, dst16, stage = dst4_ref.at[j], dst16_ref.at[j], stage_ref.at[j]
        for c4 in range(4):
            col = D_GRP * c4 + lanes
            dst4[pl.ds(c4, n4, stride=4), :] = views4[g][:, col:col + 128].astype(F32)
            for c1 in range(4):
                col = D_GRP * (4 * c1 + c4) + lanes
                stage[pl.ds(n4 * c4 + c1, n16, stride=4), :] = views16[g][:, col:col + 128].astype(F32)
        for c4 in range(4):
            dst16[pl.ds(c4, n4, stride=4), :] = stage[n4 * c4:n4 * (c4 + 1), :]


def _slab_group(ref, g):
    return jnp.concatenate([ref[4 * g + p] for p in range(4)], axis=1)


def _mesh_pos():
    return lax.axis_index("x"), lax.axis_index("y"), lax.axis_index("c")


def _flat_index(p):
    return 4 * p[0] + 2 * p[1] + p[2]


def _gather_weights(shards):
    n_arr = len(shards)

    def body(*refs):
        srcs, outs = refs[:n_arr], refs[n_arr:2 * n_arr]
        send_sems, recv_sems, local_sems = refs[2 * n_arr:]
        x, y, c = _mesh_pos()
        me, sibling = (x, y, c), (x, y, 1 - c)
        chips = [(1 - x, y), (x, 1 - y), (1 - x, 1 - y)]

        def copy(arr, k, block, to, own=False):
            dst = outs[arr].at[_flat_index(block)]
            return pltpu.make_async_remote_copy(
                src_ref=srcs[arr] if own else dst, dst_ref=dst,
                send_sem=send_sems.at[arr, k], recv_sem=recv_sems.at[arr, k],
                device_id=to, device_id_type=MESH)

        for arr in range(n_arr):
            mine = pltpu.make_async_copy(srcs[arr], outs[arr].at[_flat_index(me)], local_sems.at[arr])
            mine.start()
            first = [copy(arr, 0, me, sibling, own=True)]
            first += [copy(arr, 1 + j, me, (*chip, c), own=True) for j, chip in enumerate(chips)]
            for cp in first:
                cp.start()
        for arr in range(n_arr):
            passed = [copy(arr, 4 + j, (*chip, c), sibling) for j, chip in enumerate(chips)]
            for j, chip in enumerate(chips):
                copy(arr, 1 + j, (*chip, c), me).wait_recv()
                passed[j].start()
        for arr in range(n_arr):
            copy(arr, 0, sibling, me).wait_recv()
            for j, chip in enumerate(chips):
                copy(arr, 4 + j, (*chip, 1 - c), me).wait_recv()
            for k in range(7):
                copy(arr, k, me, me).wait_send()
            pltpu.make_async_copy(srcs[arr], outs[arr].at[_flat_index(me)], local_sems.at[arr]).wait()

    any_spec = pl.BlockSpec(memory_space=pl.ANY)
    return pl.pallas_call(
        body, name="gather_weights",
        out_shape=tuple(jax.ShapeDtypeStruct((N_DEV,) + s.shape, s.dtype) for s in shards),
        in_specs=[any_spec] * n_arr, out_specs=(any_spec,) * n_arr,
        scratch_shapes=[pltpu.SemaphoreType.DMA((n_arr, 7)), pltpu.SemaphoreType.DMA((n_arr, 7)),
                        pltpu.SemaphoreType.DMA((n_arr,))],
        compiler_params=pltpu.CompilerParams(has_side_effects=True),
    )(*shards)


_HBM_SPEC = pl.BlockSpec(memory_space=pltpu.HBM)
_SEM_SPEC = pl.BlockSpec(memory_space=pltpu.SEMAPHORE)
_DATAFLOW = pltpu.SideEffectType.DATAFLOW_SIDE_EFFECTING


def _peer_list(x, y, c):
    return [(1 - x if m & 4 else x, 1 - y if m & 2 else y, 1 - c if m & 1 else c) for m in range(1, N_DEV)]


def _spread_copies(src_refs, land_refs, send_sems, recv_sems, blockwise):
    x, y, c = _mesh_pos()
    my_idx = _flat_index((x, y, c))
    copies = []
    for a, (src, land) in enumerate(zip(src_refs, land_refs)):
        for k, peer in enumerate(_peer_list(x, y, c)):
            copies.append(pltpu.make_async_remote_copy(
                src_ref=src.at[_flat_index(peer)] if blockwise else src, dst_ref=land.at[my_idx],
                send_sem=send_sems.at[(N_DEV - 1) * a + k], recv_sem=recv_sems.at[(N_DEV - 1) * a + k],
                device_id=peer, device_id_type=MESH))
    return copies


def _spread_start(srcs, lands, blockwise, name):
    n = len(srcs)

    def body(*refs):
        for cp in _spread_copies(refs[:n], refs[n:2 * n], refs[2 * n], refs[2 * n + 1], blockwise):
            cp.start()
        token = refs[-1]
        token[...] = jnp.zeros_like(token)

    hbm = lambda t: pltpu.HBM(t.shape, t.dtype)
    sems = pltpu.SemaphoreType.DMA((n * (N_DEV - 1),))
    outs = pl.pallas_call(
        body, name=name,
        out_shape=(sems, sems) + tuple(hbm(t) for t in srcs) + tuple(hbm(t) for t in lands)
        + (jax.ShapeDtypeStruct((8, 128), F32),),
        in_specs=[_HBM_SPEC] * (2 * n),
        out_specs=(_SEM_SPEC, _SEM_SPEC) + (_HBM_SPEC,) * (2 * n) + (pl.BlockSpec(memory_space=pltpu.VMEM),),
        input_output_aliases={i: 2 + i for i in range(2 * n)},
        compiler_params=pltpu.CompilerParams(has_side_effects=_DATAFLOW),
    )(*[pltpu.with_memory_space_constraint(t, pltpu.HBM) for t in list(srcs) + list(lands)])
    return outs[0], outs[1], outs[2:2 + n], outs[2 + n:2 + 2 * n], outs[-1]


def _spread_wait(send_sems, recv_sems, srcs, lands, after, blockwise, name):
    n = len(srcs)

    def body(*refs):
        for cp in _spread_copies(refs[:n], refs[n:2 * n], refs[2 * n], refs[2 * n + 1], blockwise):
            cp.wait_send()
            cp.wait_recv()

    hbm = lambda t: pltpu.HBM(t.shape, t.dtype)
    outs = pl.pallas_call(
        body, name=name,
        out_shape=tuple(hbm(t) for t in srcs) + tuple(hbm(t) for t in lands),
        in_specs=[_HBM_SPEC] * (2 * n) + [_SEM_SPEC, _SEM_SPEC, pl.BlockSpec(memory_space=pl.ANY)],
        out_specs=(_HBM_SPEC,) * (2 * n),
        input_output_aliases={i: i for i in range(2 * n)},
        compiler_params=pltpu.CompilerParams(has_side_effects=_DATAFLOW),
    )(*srcs, *lands, send_sems, recv_sems, after)
    return outs[n:]


def _exchange_grads(parts, small):
    n_arr = len(parts)

    def body(*refs):
        ins, outs = refs[:n_arr + 1], refs[n_arr + 1:2 * (n_arr + 1)]
        send_sems, recv_sems, local_sems = refs[2 * (n_arr + 1):]
        x, y, c = _mesh_pos()
        me = (x, y, c)
        my_idx = _flat_index(me)
        peers = []
        for m in range(1, N_DEV):
            peers.append((1 - x if m & 4 else x, 1 - y if m & 2 else y, 1 - c if m & 1 else c))

        def src_block(arr, dev):
            return ins[arr] if arr == n_arr else ins[arr].at[_flat_index(dev)]

        def copy(arr, k):
            return pltpu.make_async_remote_copy(
                src_ref=src_block(arr, peers[k]), dst_ref=outs[arr].at[my_idx],
                send_sem=send_sems.at[arr, k], recv_sem=recv_sems.at[arr, k],
                device_id=peers[k], device_id_type=MESH)

        def local(arr):
            return pltpu.make_async_copy(src_block(arr, me), outs[arr].at[my_idx], local_sems.at[arr])

        for arr in range(n_arr + 1):
            local(arr).start()
            for k in range(N_DEV - 1):
                copy(arr, k).start()
        for arr in range(n_arr + 1):
            for k in range(N_DEV - 1):
                cp = copy(arr, k)
                cp.wait_send()
                cp.wait_recv()
            local(arr).wait()

    any_spec = pl.BlockSpec(memory_space=pl.ANY)
    out_shape = tuple(jax.ShapeDtypeStruct(p.shape, p.dtype) for p in parts)
    out_shape += (jax.ShapeDtypeStruct((N_DEV,) + small.shape, small.dtype),)
    return pl.pallas_call(
        body, name="exchange_grads",
        out_shape=out_shape,
        in_specs=[any_spec] * (n_arr + 1), out_specs=(any_spec,) * (n_arr + 1),
        scratch_shapes=[pltpu.SemaphoreType.DMA((n_arr + 1, N_DEV - 1)),
                        pltpu.SemaphoreType.DMA((n_arr + 1, N_DEV - 1)),
                        pltpu.SemaphoreType.DMA((n_arr + 1,))],
        compiler_params=pltpu.CompilerParams(has_side_effects=True),
    )(*parts, small)


def _head_norm(t, w128, bd):
    ms = _mm_split(t * t, bd) * (1.0 / HEAD_DIM)
    r = lax.rsqrt(ms + EPS)
    return (t * r) * w128, r


def _attn_in(x2, wn1, a_g, cos2, sin2, qnw, knw, bd):
    s_len = x2.shape[0]
    tm = ROW_TILE

    def body(x_ref, wn_ref, w_ref, cos_ref, sin_ref, qnw_ref, knw_ref, bd_ref,
             h1_ref, qraw_ref, kraw_ref, q_ref, k_ref, va_ref, qs_ref, ks_ref, vs_ref,
             q4_ref, k4_ref, v4_ref, q16_ref, k16_ref, v16_ref, proj, slabs, stage):
        xx = x_ref[...]
        r = lax.rsqrt(jnp.mean(xx * xx, axis=-1, keepdims=True) + EPS)
        h = ((xx * r) * wn_ref[...]).astype(BF16)
        h1_ref[...] = h
        for d in range(N_DEV):
            proj[:, IN_SHARD * d:IN_SHARD * (d + 1)] = _dot(h, w_ref[d])
        cos_t, sin_t, bdm = cos_ref[...], sin_ref[...], bd_ref[...]
        for grp, (raw_ref, rope_ref, nw_ref) in enumerate(((qraw_ref, q_ref, qnw_ref),
                                                           (kraw_ref, k_ref, knw_ref))):
            for p in range(4):
                cols = slice(D_GRP * grp + 128 * p, D_GRP * grp + 128 * (p + 1))
                t = proj[:, cols]
                raw_ref[:, 128 * p:128 * (p + 1)] = t
                yn, _ = _head_norm(t, nw_ref[...], bdm)
                roped = yn * cos_t + _swap_halves(yn) * sin_t
                slabs[4 * grp + p] = roped
                rope_ref[:, 128 * p:128 * (p + 1)] = roped.astype(BF16)
        for p in range(4):
            slabs[8 + p] = proj[:, 2 * D_GRP + 128 * p:2 * D_GRP + 128 * (p + 1)]
        for grp, ref in ((2, va_ref), (3, qs_ref), (4, ks_ref), (5, vs_ref)):
            ref[...] = proj[:, D_GRP * grp:D_GRP * (grp + 1)].astype(BF16)
        _split_views(slabs, stage, (q4_ref, k4_ref, v4_ref), (q16_ref, k16_ref, v16_ref))

    row = lambda w: pl.BlockSpec((tm, w), lambda i: (i, 0))
    grp_bf = jax.ShapeDtypeStruct((s_len, D_GRP), BF16)
    grp_f32 = jax.ShapeDtypeStruct((s_len, D_GRP), F32)
    return pl.pallas_call(
        body, name="attn_in", grid=(s_len // tm,),
        in_specs=[row(D_MODEL), _full((1, D_MODEL)),
                  pl.BlockSpec((N_DEV, D_MODEL, IN_SHARD), lambda i: (0, 0, 0)),
                  row(128), row(128), _full((1, 128)), _full((1, 128)), _full((128, 128))],
        out_specs=(row(D_MODEL),) + (row(D_GRP),) * 8 + (_view_spec(tm, 4),) * 3 + (_view_spec(tm, 16),) * 3,
        out_shape=(jax.ShapeDtypeStruct((s_len, D_MODEL), BF16), grp_f32, grp_f32) + (grp_bf,) * 6
        + (_view_shape(s_len, 4, BF16),) * 3 + (_view_shape(s_len, 16, BF16),) * 3,
        scratch_shapes=[pltpu.VMEM((tm, D_IN), F32), pltpu.VMEM((12, tm, 128), F32), pltpu.VMEM((12, tm, 128), F32)],
        compiler_params=_params(),
    )(x2, wn1, a_g, cos2, sin2, qnw, knw, bd)


def _band_mask(n):
    i = lax.broadcasted_iota(jnp.int32, (BLOCK, 2 * BLOCK), 0)
    j = lax.broadcasted_iota(jnp.int32, (BLOCK, 2 * BLOCK), 1)
    dist = i + BLOCK - j
    return (dist >= 0) & (dist <= BLOCK) & ((n - 1) * BLOCK + j >= 0)


def _dil_fwd(qv, kv, vv, r):
    sub_len = qv.shape[0]
    nb = sub_len // BLOCK

    def body(q_ref, kp_ref, kc_ref, vp_ref, vc_ref, o_ref, lse_ref):
        n = pl.program_id(1)
        valid = _band_mask(n)
        lane = lax.broadcasted_iota(jnp.int32, (BLOCK, 128), 1)
        head0 = lane < HEAD_DIM
        for p in range(4):
            cols = slice(128 * p, 128 * (p + 1))
            q2 = q_ref[:, cols]
            kk = jnp.concatenate([kp_ref[:, cols], kc_ref[:, cols]], axis=0)
            vv2 = jnp.concatenate([vp_ref[:, cols], vc_ref[:, cols]], axis=0)
            res = []
            for h in range(2):
                qh = jnp.where(head0, q2, 0) if h == 0 else jnp.where(head0, 0, q2)
                s = jnp.where(valid, _dot_nt(qh, kk) * ATT_SCALE, NEG)
                m = jnp.max(s, axis=-1, keepdims=True)
                pr = jnp.exp(s - m)
                den = jnp.sum(pr, axis=-1, keepdims=True)
                o = _dot(pr.astype(BF16), vv2) / den
                res.append((o, m + jnp.log(den)))
            o_ref[:, cols] = jnp.where(head0, res[0][0], res[1][0])
            lse_ref[:, cols] = jnp.where(head0, res[0][1], res[1][1])

    cur = pl.BlockSpec((BLOCK, D_GRP), lambda c, n: (n, c))
    prev = pl.BlockSpec((BLOCK, D_GRP), lambda c, n: (jnp.maximum(n - 1, 0), c))
    out = jax.ShapeDtypeStruct(qv.shape, F32)
    return pl.pallas_call(
        body, name=f"dil_fwd_r{r}", grid=(r, nb),
        in_specs=[cur, prev, cur, prev, cur], out_specs=(cur, cur), out_shape=(out, out),
        compiler_params=_params(),
    )(qv, kv, kv, vv, vv)


def _dil_bwd(qv, kv, vv, dov, lsev, deltav, r):
    sub_len = qv.shape[0]
    nb = sub_len // BLOCK

    def body(q_ref, kp_ref, kc_ref, vp_ref, vc_ref, do_ref, lse_ref, dl_ref,
             dq_ref, dk_ref, dv_ref, dk_carry, dv_carry):
        n = pl.program_id(1)

        @pl.when(n == 0)
        def _():
            dk_carry[...] = jnp.zeros_like(dk_carry)
            dv_carry[...] = jnp.zeros_like(dv_carry)

        @pl.when(n < nb)
        def _():
            valid = _band_mask(n)
            lane = lax.broadcasted_iota(jnp.int32, (BLOCK, 128), 1)
            head0 = lane < HEAD_DIM
            for p in range(4):
                cols = slice(128 * p, 128 * (p + 1))
                q2, do2 = q_ref[:, cols], do_ref[:, cols]
                lse2, dl2 = lse_ref[:, cols], dl_ref[:, cols]
                kk = jnp.concatenate([kp_ref[:, cols], kc_ref[:, cols]], axis=0)
                vv2 = jnp.concatenate([vp_ref[:, cols], vc_ref[:, cols]], axis=0)
                dq_h, dkk, dvv = [], 0.0, 0.0
                for h in range(2):
                    sel = (lambda t: jnp.where(head0, t, 0)) if h == 0 else (lambda t: jnp.where(head0, 0, t))
                    qh, doh = sel(q2), sel(do2)
                    one_lane = lane == (0 if h == 0 else HEAD_DIM)
                    lse = jnp.sum(jnp.where(one_lane, lse2, 0.0), axis=-1, keepdims=True)
                    dl = jnp.sum(jnp.where(one_lane, dl2, 0.0), axis=-1, keepdims=True)
                    s = _dot_nt(qh, kk) * ATT_SCALE
                    pr = jnp.where(valid, jnp.exp(jnp.minimum(s - lse, 0.0)), 0.0)
                    dp = _dot_nt(doh, vv2)
                    ds = (pr * (dp - dl) * ATT_SCALE).astype(BF16)
                    dq_h.append(_dot(ds, kk))
                    dkk = dkk + _dot_tn(ds, qh)
                    dvv = dvv + _dot_tn(pr.astype(BF16), doh)
                dq_ref[:, cols] = jnp.where(head0, dq_h[0], dq_h[1])
                dk_ref[:, cols] = dk_carry[:, cols] + dkk[:BLOCK]
                dv_ref[:, cols] = dv_carry[:, cols] + dvv[:BLOCK]
                dk_carry[:, cols] = dkk[BLOCK:]
                dv_carry[:, cols] = dvv[BLOCK:]

        @pl.when(n == nb)
        def _():
            dk_ref[...] = dk_carry[...]
            dv_ref[...] = dv_carry[...]

    last = nb - 1
    cur = pl.BlockSpec((BLOCK, D_GRP), lambda c, n: (jnp.minimum(n, last), c))
    prev = pl.BlockSpec((BLOCK, D_GRP), lambda c, n: (jnp.clip(n - 1, 0, last), c))
    out = jax.ShapeDtypeStruct(qv.shape, F32)
    return pl.pallas_call(
        body, name=f"dil_bwd_r{r}", grid=(r, nb + 1),
        in_specs=[cur, prev, cur, prev, cur, cur, cur, cur],
        out_specs=(cur, prev, prev), out_shape=(out, out, out),
        scratch_shapes=[pltpu.VMEM((BLOCK, D_GRP), F32), pltpu.VMEM((BLOCK, D_GRP), F32)],
        compiler_params=_params(),
    )(qv, kv, kv, vv, vv, dov, lsev, deltav)


def _sb_fwd(qs, ks, vs, tri_suf):
    s_len = qs.shape[0]
    t = SB_TILE
    nq = s_len // t

    npair = SB_PAIRS

    def body(q_ref, k_ref, v_ref, u_ref, o_ref, c_ref, qq, vt, acc, cf, csave):
        row = lax.broadcasted_iota(jnp.int32, (2 * t, t), 0) & (t - 1)
        col = lax.broadcasted_iota(jnp.int32, (2 * t, t), 1)
        diag_mask = col < row
        lane1 = lax.broadcasted_iota(jnp.int32, (t, 128), 1)
        head0 = lane1 < HEAD_DIM
        lane2 = lax.broadcasted_iota(jnp.int32, (2 * t, 128), 1)
        uu = u_ref[...]
        pr = range(npair)
        cols = [slice(128 * pp, 128 * (pp + 1)) for pp in pr]

        i = pl.program_id(1)

        @pl.when(i == 0)
        def _():
            def transpose_v(j, _):
                rows = pl.ds(pl.multiple_of(j * t, t), t)
                for pp in pr:
                    vt[pp, j] = v_ref[rows, cols[pp]].astype(F32).T.astype(BF16)
                return 0

            lax.fori_loop(0, nq, transpose_v, 0)

        for pp in pr:
            q2 = q_ref[:, cols[pp]] * ATT_SCALE
            qq[pp, 0:t, :] = jnp.where(head0, q2, 0)
            qq[pp, t:2 * t, :] = jnp.where(head0, 0, q2)
        acc[...] = jnp.zeros_like(acc)
        cf[...] = jnp.zeros_like(cf)
        csave[...] = jnp.zeros_like(csave)

        def tile(kb, diag):
            krows = pl.ds(pl.multiple_of(kb * t, t), t)
            zs = [_dot_nt(qq[pp], k_ref[krows, cols[pp]]) for pp in pr]
            lbk = [_log_sigmoid_pair(z) for z in zs]
            lks = [jnp.where(diag_mask, lk, 0.0) if diag else lk for _, lk in lbk]
            sufs = [_cumsum_mm(lk, uu) for lk in lks]
            carries = [cf[pp] for pp in pr]
            avs = []
            for pp in pr:
                a = jnp.exp(lbk[pp][0] + (sufs[pp] + jnp.concatenate([carries[pp]] * (t // 128), axis=1)))
                avs.append((jnp.where(diag_mask, a, 0.0) if diag else a).astype(BF16))
            pvs = [_dot_nt(vt[pp, kb], avs[pp]) for pp in pr]
            for pp in pr:
                acc[pp] += pvs[pp]
                csave[pp] = jnp.where(lane2 == kb, carries[pp], csave[pp])
                cf[pp] = carries[pp] + jnp.broadcast_to(jnp.sum(lks[pp], axis=-1, keepdims=True), (2 * t, 128))

        tile(i, True)

        def k_block(step, _):
            tile(i - 1 - step, False)
            return 0

        lax.fori_loop(0, i, k_block, 0)
        for pp in pr:
            o_ref[:, cols[pp]] = jnp.where(head0, acc[pp, :, 0:t].T, acc[pp, :, t:2 * t].T)
            c_ref[2 * pp] = csave[pp, 0:t, :]
            c_ref[2 * pp + 1] = csave[pp, t:2 * t, :]

    width = 128 * npair
    kv = pl.BlockSpec((s_len, width), lambda p, i: (0, p))
    qo = pl.BlockSpec((t, width), lambda p, i: (i, p))
    return pl.pallas_call(
        body, name="sb_fwd", grid=(4 // npair, nq),
        in_specs=[qo, kv, kv, pl.BlockSpec((t, t), lambda p, i: (0, 0))],
        out_specs=(qo, pl.BlockSpec((2 * npair, t, 128), lambda p, i: (p, i, 0))),
        out_shape=(jax.ShapeDtypeStruct((s_len, D_GRP), F32),
                   jax.ShapeDtypeStruct((8, s_len, 128), F32)),
        scratch_shapes=[pltpu.VMEM((npair, 2 * t, 128), BF16), pltpu.VMEM((npair, nq, 128, t), BF16),
                        pltpu.VMEM((npair, 128, 2 * t), F32),
                        pltpu.VMEM((npair, 2 * t, 128), F32), pltpu.VMEM((npair, 2 * t, 128), F32)],
        compiler_params=_params(),
    )(qs, ks, vs, tri_suf)


def _sb_bwd(qs, ks, vs, dos, csaved, tri_suf, tri_pre):
    s_len = qs.shape[0]
    t = SB_TILE
    nq = s_len // t

    npair = SB_BWD_PAIRS

    def body(q_ref, k_ref, v_ref, do_ref, c_ref, u_ref, p_ref, dq_ref, dk_ref, dv_ref,
             qq, dd, qqt, ddt, kt, dq_acc, dkt, dvt, cg):
        row = lax.broadcasted_iota(jnp.int32, (2 * t, t), 0) & (t - 1)
        col = lax.broadcasted_iota(jnp.int32, (2 * t, t), 1)
        diag_mask = col < row
        lane1 = lax.broadcasted_iota(jnp.int32, (t, 128), 1)
        head0 = lane1 < HEAD_DIM
        lane2 = lax.broadcasted_iota(jnp.int32, (2 * t, 128), 1)
        uu, pm = u_ref[...], p_ref[...]
        pr = range(npair)
        cols = [slice(128 * pp, 128 * (pp + 1)) for pp in pr]
        i = pl.program_id(1)

        @pl.when(i == 0)
        def _():
            dkt[...] = jnp.zeros_like(dkt)
            dvt[...] = jnp.zeros_like(dvt)

            def transpose_k(j, _):
                rows = pl.ds(pl.multiple_of(j * t, t), t)
                for pp in pr:
                    kt[pp, j] = k_ref[rows, cols[pp]].astype(F32).T.astype(BF16)
                return 0

            lax.fori_loop(0, nq, transpose_k, 0)

        for pp in pr:
            q2 = q_ref[:, cols[pp]].astype(F32) * ATT_SCALE
            do2 = do_ref[:, cols[pp]].astype(F32)
            for src, nat, tr in ((q2, qq, qqt), (do2, dd, ddt)):
                stacked = jnp.concatenate([jnp.where(head0, src, 0.0), jnp.where(head0, 0.0, src)], axis=0)
                nat[pp] = stacked.astype(BF16)
                tr[pp] = stacked.T.astype(BF16)
        dq_acc[...] = jnp.zeros_like(dq_acc)
        cg[...] = jnp.zeros_like(cg)

        def tile(kb, diag):
            krows = pl.ds(pl.multiple_of(kb * t, t), t)
            zs = [_dot_nt(qq[pp], k_ref[krows, cols[pp]]) for pp in pr]
            das = [_dot_nt(dd[pp], v_ref[krows, cols[pp]]) for pp in pr]
            lbk = [_log_sigmoid_pair(z) for z in zs]
            lks = [jnp.where(diag_mask, lk, 0.0) if diag else lk for _, lk in lbk]
            sufs = [_cumsum_mm(lk, uu) for lk in lks]
            avs, gs = [], []
            for pp in pr:
                cs = jnp.concatenate([c_ref[2 * pp], c_ref[2 * pp + 1]], axis=0)
                cf = jnp.sum(jnp.where(lane2 == kb, cs, 0.0), axis=-1, keepdims=True)
                a = jnp.exp(lbk[pp][0] + (sufs[pp] + cf))
                a = jnp.where(diag_mask, a, 0.0) if diag else a
                avs.append(a.astype(BF16))
                gs.append(a * das[pp])
            gpres = [_cumsum_mm(g, pm) for g in gs]
            dzs = []
            for pp in pr:
                carry = cg[pp]
                beta = jnp.exp(lbk[pp][0])
                dz = gs[pp] - beta * (gs[pp] + (gpres[pp] + jnp.concatenate([carry] * (t // 128), axis=1)))
                dzs.append((jnp.where(diag_mask, dz, 0.0) if diag else dz).astype(BF16))
                cg[pp] = carry + jnp.broadcast_to(jnp.sum(gs[pp], axis=-1, keepdims=True), (2 * t, 128))
            dqs = [_dot_nt(kt[pp, kb], dzs[pp]) for pp in pr]
            dks = [_dot(qqt[pp], dzs[pp]) for pp in pr]
            dvs = [_dot(ddt[pp], avs[pp]) for pp in pr]
            for pp in pr:
                dq_acc[pp] += dqs[pp]
                dkt[pp, kb] += dks[pp]
                dvt[pp, kb] += dvs[pp]

        def k_block(kb, _):
            tile(kb, False)
            return 0

        lax.fori_loop(0, i, k_block, 0)
        tile(i, True)
        for pp in pr:
            dq_ref[:, cols[pp]] = jnp.where(head0, dq_acc[pp, :, 0:t].T, dq_acc[pp, :, t:2 * t].T) * ATT_SCALE

        @pl.when(i == nq - 1)
        def _():
            def untranspose(j, _):
                rows = pl.ds(pl.multiple_of(j * t, t), t)
                for pp in pr:
                    dk_ref[rows, cols[pp]] = dkt[pp, j].T
                    dv_ref[rows, cols[pp]] = dvt[pp, j].T
                return 0

            lax.fori_loop(0, nq, untranspose, 0)

    width = 128 * npair
    kv = pl.BlockSpec((s_len, width), lambda p, i: (0, p))
    qo = pl.BlockSpec((t, width), lambda p, i: (i, p))
    tri = pl.BlockSpec((t, t), lambda p, i: (0, 0))
    out = jax.ShapeDtypeStruct((s_len, D_GRP), F32)
    return pl.pallas_call(
        body, name="sb_bwd", grid=(4 // npair, nq),
        in_specs=[qo, kv, kv, qo, pl.BlockSpec((2 * npair, t, 128), lambda p, i: (p, i, 0)), tri, tri],
        out_specs=(qo, kv, kv), out_shape=(out, out, out),
        scratch_shapes=[pltpu.VMEM((npair, 2 * t, 128), BF16), pltpu.VMEM((npair, 2 * t, 128), BF16),
                        pltpu.VMEM((npair, 128, 2 * t), BF16), pltpu.VMEM((npair, 128, 2 * t), BF16),
                        pltpu.VMEM((npair, nq, 128, t), BF16),
                        pltpu.VMEM((npair, 128, 2 * t), F32),
                        pltpu.VMEM((npair, nq, 128, t), F32), pltpu.VMEM((npair, nq, 128, t), F32),
                        pltpu.VMEM((npair, 2 * t, 128), F32)],
        compiler_params=_params(),
    )(qs, ks, vs, dos, csaved, tri_suf, tri_pre)


def _attn_out(o_b, lse_b, o_sb, x2, wdil, wsb, b_g):
    s_len = x2.shape[0]
    tm = ROW_TILE

    def body(o1_ref, l1_ref, o4_ref, l4_ref, o16_ref, l16_ref, osb_ref, x_ref, wdil_ref, wsb_ref, w_ref,
             odil_ref, lse_ref, lse4_ref, lse16_ref, mixed_ref, x1_ref, stage, nat4, nat16):
        _merge_views((o4_ref, l4_ref), (o16_ref, l16_ref), stage, nat4, nat16)
        os_ = (o1_ref[...], _slab_group(nat4, 0), _slab_group(nat16, 0))
        ls = (l1_ref[...], _slab_group(nat4, 1), _slab_group(nat16, 1))
        mx = jnp.maximum(jnp.maximum(ls[0], ls[1]), ls[2])
        es = [jnp.exp(l - mx) for l in ls]
        den = es[0] + es[1] + es[2]
        o_dil = (es[0] * os_[0] + es[1] * os_[1] + es[2] * os_[2]) / den
        odil_ref[...] = o_dil
        lse = mx + jnp.log(den)
        lse_ref[...] = lse
        for p in range(4):
            nat4[p] = lse[:, 128 * p:128 * (p + 1)]
        _split_views(nat4.at[0:4], stage.at[0:4], (lse4_ref,), (lse16_ref,))
        halves = []
        for t, w_r in ((o_dil, wdil_ref), (osb_ref[...], wsb_ref)):
            r = lax.rsqrt(jnp.mean(t * t, axis=-1, keepdims=True) + EPS)
            halves.append(((t * r) * w_r[...]).astype(BF16))
        mixed = jnp.concatenate(halves, axis=1)
        mixed_ref[...] = mixed
        w = w_ref[...].reshape(D_MODEL, D_MODEL)
        x1_ref[...] = x_ref[...] + _dot(mixed, w)

    row = lambda w: pl.BlockSpec((tm, w), lambda i: (i, 0))
    return pl.pallas_call(
        body, name="attn_out", grid=(s_len // tm,),
        in_specs=[row(D_GRP)] * 2 + [_view_spec(tm, 4)] * 2 + [_view_spec(tm, 16)] * 2
        + [row(D_GRP), row(D_MODEL), _full((1, D_GRP)), _full((1, D_GRP)),
           pl.BlockSpec((N_DEV, OUT_SHARD, D_MODEL), lambda i: (0, 3, 0))],
        out_specs=(row(D_GRP), row(D_GRP), _view_spec(tm, 4), _view_spec(tm, 16), row(D_MODEL), row(D_MODEL)),
        out_shape=(jax.ShapeDtypeStruct((s_len, D_GRP), F32), jax.ShapeDtypeStruct((s_len, D_GRP), F32),
                   _view_shape(s_len, 4, F32), _view_shape(s_len, 16, F32),
                   jax.ShapeDtypeStruct((s_len, D_MODEL), BF16), jax.ShapeDtypeStruct((s_len, D_MODEL), F32)),
        scratch_shapes=[pltpu.VMEM((8, tm, 128), F32)] * 3,
        compiler_params=_params(),
    )(o_b[0], lse_b[0], o_b[1], lse_b[1], o_b[2], lse_b[2], o_sb, x2, wdil, wsb, b_g)


def _side_by_side(w_ref):
    return jnp.concatenate([w_ref[0], w_ref[1]], axis=1)


def _ffn_fwd(x1, wn2, tgt, gu_g, b_g):
    s_len = x1.shape[0]
    tm = ROW_TILE
    ni = s_len // tm

    def body(x_ref, wn_ref, t_ref, wg_ref, wu_ref, wd_ref, g_ref, u_ref, h2_ref, dy_ref, loss_ref, acc):
        j = pl.program_id(1)

        @pl.when(j == 0)
        def _():
            xx = x_ref[...]
            r = lax.rsqrt(jnp.mean(xx * xx, axis=-1, keepdims=True) + EPS)
            h2_ref[...] = ((xx * r) * wn_ref[...]).astype(BF16)
            acc[...] = jnp.zeros_like(acc)

        h = h2_ref[...]
        g = _dot(h, _side_by_side(wg_ref))
        u = _dot(h, _side_by_side(wu_ref))
        g_ref[...] = g
        u_ref[...] = u
        act = (g * (1.0 / (1.0 + jnp.exp(-g)))) * u
        acc[...] += _dot(act.astype(BF16), wd_ref[...].reshape(FF_BLOCK, D_MODEL))

        @pl.when(j == FF_STEPS - 1)
        def _():
            err = (x_ref[...] + acc[...]) - t_ref[...]
            dy_ref[...] = err * (1.0 / D_MODEL)
            part = 0.5 * jnp.sum(jnp.mean(err * err, axis=-1, keepdims=True))
            loss_ref[...] = jnp.full((8, 128), part, F32)

    row = pl.BlockSpec((tm, D_MODEL), lambda i, j: (i, 0))
    hid = pl.BlockSpec((tm, FF_BLOCK), lambda i, j: (i, j))
    return pl.pallas_call(
        body, name="ffn_fwd", grid=(ni, FF_STEPS),
        in_specs=[row, pl.BlockSpec((1, D_MODEL), lambda i, j: (0, 0)), row,
                  pl.BlockSpec((2, D_MODEL, FF_PAD), lambda i, j: (j, 0, 0)),
                  pl.BlockSpec((2, D_MODEL, FF_PAD), lambda i, j: (j, 1, 0)),
                  pl.BlockSpec((2, FF_PAD, D_MODEL), lambda i, j: (j, 0, 0))],
        out_specs=(hid, hid, row, row, pl.BlockSpec((8, 128), lambda i, j: (i, 0))),
        out_shape=(jax.ShapeDtypeStruct((s_len, N_DEV * FF_PAD), F32),
                   jax.ShapeDtypeStruct((s_len, N_DEV * FF_PAD), F32),
                   jax.ShapeDtypeStruct((s_len, D_MODEL), BF16),
                   jax.ShapeDtypeStruct((s_len, D_MODEL), F32),
                   jax.ShapeDtypeStruct((ni * 8, 128), F32)),
        scratch_shapes=[pltpu.VMEM((tm, D_MODEL), F32)],
        compiler_params=_params(),
    )(x1, wn2, tgt, gu_g, gu_g, b_g)


def _ffn_bwd_dx(dy, g, u, gu_g, b_g):
    s_len = dy.shape[0]
    tm = ROW_TILE

    def body(dy_ref, g_ref, u_ref, wg_ref, wu_ref, wd_ref, dg_ref, du_ref, act_ref, dh_ref, acc):
        j = pl.program_id(1)

        @pl.when(j == 0)
        def _():
            acc[...] = jnp.zeros_like(acc)

        gg, uu = g_ref[...], u_ref[...]
        da = _dot_nt(dy_ref[...].astype(BF16), wd_ref[...].reshape(FF_BLOCK, D_MODEL))
        sig = 1.0 / (1.0 + jnp.exp(-gg))
        silu = gg * sig
        act_ref[...] = (silu * uu).astype(BF16)
        du = (da * silu).astype(BF16)
        dg = (da * uu * (sig * (1.0 + gg * (1.0 - sig)))).astype(BF16)
        du_ref[...] = du
        dg_ref[...] = dg
        acc[...] += _dot_nt(dg, _side_by_side(wg_ref)) + _dot_nt(du, _side_by_side(wu_ref))

        @pl.when(j == FF_STEPS - 1)
        def _():
            dh_ref[...] = acc[...]

    row = pl.BlockSpec((tm, D_MODEL), lambda i, j: (i, 0))
    hid = pl.BlockSpec((tm, FF_BLOCK), lambda i, j: (i, j))
    hid_bf = jax.ShapeDtypeStruct((s_len, N_DEV * FF_PAD), BF16)
    return pl.pallas_call(
        body, name="ffn_bwd_dx", grid=(s_len // tm, FF_STEPS),
        in_specs=[row, hid, hid,
                  pl.BlockSpec((2, D_MODEL, FF_PAD), lambda i, j: (j, 0, 0)),
                  pl.BlockSpec((2, D_MODEL, FF_PAD), lambda i, j: (j, 1, 0)),
                  pl.BlockSpec((2, FF_PAD, D_MODEL), lambda i, j: (j, 0, 0))],
        out_specs=(hid, hid, hid, row),
        out_shape=(hid_bf, hid_bf, hid_bf, jax.ShapeDtypeStruct((s_len, D_MODEL), F32)),
        scratch_shapes=[pltpu.VMEM((tm, D_MODEL), F32)],
        compiler_params=_params(),
    )(dy, g, u, gu_g, gu_g, b_g)


def _ffn_bwd_dw(h2, dy, dg, du, act):
    s_len = h2.shape[0]
    tm = ROW_TILE
    ni = s_len // tm

    def body(h_ref, dy_ref, dg_ref, du_ref, act_ref, dwg_ref, dwu_ref, dwd_ref, ag, au, ad):
        i = pl.program_id(1)

        @pl.when(i == 0)
        def _():
            ag[...] = jnp.zeros_like(ag)
            au[...] = jnp.zeros_like(au)
            ad[...] = jnp.zeros_like(ad)

        h = h_ref[...]
        ag[...] += _dot_tn(h, dg_ref[...])
        au[...] += _dot_tn(h, du_ref[...])
        ad[...] += _dot_tn(act_ref[...], dy_ref[...].astype(BF16))

        @pl.when(i == ni - 1)
        def _():
            for half in range(2):
                cols = slice(FF_PAD * half, FF_PAD * (half + 1))
                dwg_ref[half] = ag[:, cols].astype(BF16)
                dwu_ref[half] = au[:, cols].astype(BF16)
            dwd_ref[...] = ad[...].astype(BF16).reshape(2, FF_PAD, D_MODEL)

    row = pl.BlockSpec((tm, D_MODEL), lambda j, i: (i, 0))
    hid = pl.BlockSpec((tm, FF_BLOCK), lambda j, i: (i, j))
    col_w = pl.BlockSpec((2, D_MODEL, FF_PAD), lambda j, i: (j, 0, 0))
    row_w = pl.BlockSpec((2, FF_PAD, D_MODEL), lambda j, i: (j, 0, 0))
    return pl.pallas_call(
        body, name="ffn_bwd_dw", grid=(FF_STEPS, ni),
        in_specs=[row, row, hid, hid, hid], out_specs=(col_w, col_w, row_w),
        out_shape=(jax.ShapeDtypeStruct((N_DEV, D_MODEL, FF_PAD), BF16),
                   jax.ShapeDtypeStruct((N_DEV, D_MODEL, FF_PAD), BF16),
                   jax.ShapeDtypeStruct((N_DEV, FF_PAD, D_MODEL), BF16)),
        scratch_shapes=[pltpu.VMEM((D_MODEL, FF_BLOCK), F32), pltpu.VMEM((D_MODEL, FF_BLOCK), F32),
                        pltpu.VMEM((FF_BLOCK, D_MODEL), F32)],
        compiler_params=_params(),
    )(h2, dy, dg, du, act)


def _rms_bwd(dy, t, w):
    r = lax.rsqrt(jnp.mean(t * t, axis=-1, keepdims=True) + EPS)
    gw = dy * w
    dt = r * (gw - t * ((r * r) * jnp.mean(gw * t, axis=-1, keepdims=True)))
    return dt, dy * t * r


def _attn_out_bwd(dy, dh2, x1, wn2, b_g, mixed, o_dil, o_sb, wdil, wsb, bd512):
    s_len = dy.shape[0]
    tm = ROW_TILE
    ni = s_len // tm

    def body(dy_ref, dh_ref, x1_ref, wn_ref, w_ref, mixed_ref, odil_ref, osb_ref, wdil_ref, wsb_ref, bd_ref,
             dx1_ref, dodil_ref, delta_ref, dosb_ref, dwout_ref, dwn_ref, dwdil_ref, dwsb_ref,
             do4_ref, dl4_ref, do16_ref, dl16_ref, wacc, both, stage):
        i = pl.program_id(0)

        @pl.when(i == 0)
        def _():
            wacc[...] = jnp.zeros_like(wacc)
            dwn_ref[...] = jnp.zeros_like(dwn_ref)
            dwdil_ref[...] = jnp.zeros_like(dwdil_ref)
            dwsb_ref[...] = jnp.zeros_like(dwsb_ref)

        dnorm, dw_rows = _rms_bwd(dh_ref[...], x1_ref[...], wn_ref[...])
        dx1 = dy_ref[...] + dnorm
        dx1_ref[...] = dx1
        dwn_ref[...] += jnp.sum(dw_rows, axis=0, keepdims=True)
        dx1b = dx1.astype(BF16)
        w = w_ref[...].reshape(D_MODEL, D_MODEL)
        dmixed = _dot_nt(dx1b, w)
        wacc[...] += _dot_tn(mixed_ref[...], dx1b)
        o_dil = odil_ref[...]
        d_odil, dw_rows = _rms_bwd(dmixed[:, :D_GRP], o_dil, wdil_ref[...])
        dwdil_ref[...] += jnp.sum(dw_rows, axis=0, keepdims=True)
        dodil_ref[...] = d_odil.astype(BF16)
        delta = _mm_split(d_odil * o_dil, bd_ref[...])
        delta_ref[...] = delta
        for p in range(4):
            both[p] = d_odil[:, 128 * p:128 * (p + 1)]
            both[4 + p] = delta[:, 128 * p:128 * (p + 1)]
        _split_views(both, stage, (do4_ref, dl4_ref), (do16_ref, dl16_ref))
        d_osb, dw_rows = _rms_bwd(dmixed[:, D_GRP:], osb_ref[...], wsb_ref[...])
        dwsb_ref[...] += jnp.sum(dw_rows, axis=0, keepdims=True)
        dosb_ref[...] = d_osb.astype(BF16)

        @pl.when(i == ni - 1)
        def _():
            dwout_ref[...] = wacc[...].astype(BF16).reshape(N_DEV, OUT_SHARD, D_MODEL)

    row = lambda w: pl.BlockSpec((tm, w), lambda i: (i, 0))
    return pl.pallas_call(
        body, name="attn_out_bwd", grid=(ni,),
        in_specs=[row(D_MODEL), row(D_MODEL), row(D_MODEL), _full((1, D_MODEL)),
                  pl.BlockSpec((N_DEV, OUT_SHARD, D_MODEL), lambda i: (0, 3, 0)),
                  row(D_MODEL), row(D_GRP), row(D_GRP), _full((1, D_GRP)), _full((1, D_GRP)),
                  _full((D_GRP, D_GRP))],
        out_specs=(row(D_MODEL), row(D_GRP), row(D_GRP), row(D_GRP),
                   _full((N_DEV, OUT_SHARD, D_MODEL)), _full((1, D_MODEL)), _full((1, D_GRP)), _full((1, D_GRP)),
                   _view_spec(tm, 4), _view_spec(tm, 4), _view_spec(tm, 16), _view_spec(tm, 16)),
        out_shape=(jax.ShapeDtypeStruct((s_len, D_MODEL), F32), jax.ShapeDtypeStruct((s_len, D_GRP), BF16),
                   jax.ShapeDtypeStruct((s_len, D_GRP), F32), jax.ShapeDtypeStruct((s_len, D_GRP), BF16),
                   jax.ShapeDtypeStruct((N_DEV, OUT_SHARD, D_MODEL), BF16),
                   jax.ShapeDtypeStruct((1, D_MODEL), F32), jax.ShapeDtypeStruct((1, D_GRP), F32),
                   jax.ShapeDtypeStruct((1, D_GRP), F32),
                   _view_shape(s_len, 4, BF16), _view_shape(s_len, 4, F32),
                   _view_shape(s_len, 16, BF16), _view_shape(s_len, 16, F32)),
        scratch_shapes=[pltpu.VMEM((D_MODEL, D_MODEL), F32), pltpu.VMEM((8, tm, 128), F32),
                        pltpu.VMEM((8, tm, 128), F32)],
        compiler_params=_params(),
    )(dy, dh2, x1, wn2, b_g, mixed, o_dil, o_sb, wdil, wsb, bd512)


def _qkv_bwd(dq_b, dk_b, dv_b, dqs, dks, dvs, qraw, kraw, cos2, sin2, qnw, knw, bd):
    s_len = qraw.shape[0]
    tm = ROW_TILE
    ni = s_len // tm

    def body(dq1, dk1, dv1, dq4, dk4, dv4, dq16, dk16, dv16, dqs_ref, dks_ref, dvs_ref,
             qraw_ref, kraw_ref, cos_ref, sin_ref, qnw_ref, knw_ref, bd_ref,
             dproj_ref, dqn_ref, dkn_ref, stage, nat4, nat16):
        i = pl.program_id(0)

        @pl.when(i == 0)
        def _():
            dqn_ref[...] = jnp.zeros_like(dqn_ref)
            dkn_ref[...] = jnp.zeros_like(dkn_ref)

        _merge_views((dq4, dk4, dv4), (dq16, dk16, dv16), stage, nat4, nat16)
        cos_t, sin_t, bdm = cos_ref[...], sin_ref[...], bd_ref[...]
        for grp, (part1, raw_ref, nw_ref, dn_ref) in enumerate(((dq1, qraw_ref, qnw_ref, dqn_ref),
                                                                (dk1, kraw_ref, knw_ref, dkn_ref))):
            dn_acc = 0.0
            for p in range(4):
                cols = slice(128 * p, 128 * (p + 1))
                d_rope = part1[:, cols] + nat4[4 * grp + p] + nat16[4 * grp + p]
                d_norm = d_rope * cos_t + _swap_halves(d_rope * sin_t)
                t = raw_ref[:, cols]
                w = nw_ref[...]
                r = lax.rsqrt(_mm_split(t * t, bdm) * (1.0 / HEAD_DIM) + EPS)
                gw = d_norm * w
                corr = _mm_split(gw * t, bdm) * (1.0 / HEAD_DIM)
                dt = r * (gw - t * ((r * r) * corr))
                dn_acc = dn_acc + jnp.sum(d_norm * t * r, axis=0, keepdims=True)
                dproj_ref[:, D_GRP * grp + 128 * p:D_GRP * grp + 128 * (p + 1)] = dt.astype(BF16)
            dn_ref[...] += dn_acc
        dproj_ref[:, 2 * D_GRP:3 * D_GRP] = (dv1[...] + _slab_group(nat4, 2) + _slab_group(nat16, 2)).astype(BF16)
        dproj_ref[:, 3 * D_GRP:4 * D_GRP] = dqs_ref[...].astype(BF16)
        dproj_ref[:, 4 * D_GRP:5 * D_GRP] = dks_ref[...].astype(BF16)
        dproj_ref[:, 5 * D_GRP:6 * D_GRP] = dvs_ref[...].astype(BF16)

    row = lambda w: pl.BlockSpec((tm, w), lambda i: (i, 0))
    return pl.pallas_call(
        body, name="qkv_bwd", grid=(ni,),
        in_specs=[row(D_GRP)] * 3 + [_view_spec(tm, 4)] * 3 + [_view_spec(tm, 16)] * 3 + [row(D_GRP)] * 5
        + [row(128), row(128), _full((1, 128)), _full((1, 128)), _full((128, 128))],
        out_specs=(row(D_IN), _full((1, 128)), _full((1, 128))),
        out_shape=(jax.ShapeDtypeStruct((s_len, D_IN), BF16), jax.ShapeDtypeStruct((1, 128), F32),
                   jax.ShapeDtypeStruct((1, 128), F32)),
        scratch_shapes=[pltpu.VMEM((12, tm, 128), F32)] * 3,
        compiler_params=_params(),
    )(dq_b[0], dk_b[0], dv_b[0], dq_b[1], dk_b[1], dv_b[1], dq_b[2], dk_b[2], dv_b[2],
      dqs, dks, dvs, qraw, kraw, cos2, sin2, qnw, knw, bd)


def _in_bwd_dx(dproj, a_g, x2, dx1, wn1):
    s_len = x2.shape[0]
    tm = ROW_TILE
    ni = s_len // tm

    def body(dp_ref, w_ref, x_ref, dx1_ref, wn_ref, gx_ref, dwn_ref):
        i = pl.program_id(0)

        @pl.when(i == 0)
        def _():
            dwn_ref[...] = jnp.zeros_like(dwn_ref)

        dh = 0.0
        for d in range(N_DEV):
            dh = dh + _dot_nt(dp_ref[:, IN_SHARD * d:IN_SHARD * (d + 1)], w_ref[d])
        dnorm, dw_rows = _rms_bwd(dh, x_ref[...], wn_ref[...])
        gx_ref[...] = dx1_ref[...] + dnorm
        dwn_ref[...] += jnp.sum(dw_rows, axis=0, keepdims=True)

    row = lambda w: pl.BlockSpec((tm, w), lambda i: (i, 0))
    return pl.pallas_call(
        body, name="in_bwd_dx", grid=(ni,),
        in_specs=[row(D_IN), pl.BlockSpec((N_DEV, D_MODEL, IN_SHARD), lambda i: (0, 0, 0)),
                  row(D_MODEL), row(D_MODEL), _full((1, D_MODEL))],
        out_specs=(row(D_MODEL), _full((1, D_MODEL))),
        out_shape=(jax.ShapeDtypeStruct((s_len, D_MODEL), F32), jax.ShapeDtypeStruct((1, D_MODEL), F32)),
        compiler_params=_params(),
    )(dproj, a_g, x2, dx1, wn1)


def _in_bwd_dw(h1, dproj):
    s_len = h1.shape[0]
    tm = ROW_TILE
    ni = s_len // tm

    def body(h_ref, dp_ref, dw_ref, acc):
        i = pl.program_id(1)

        @pl.when(i == 0)
        def _():
            acc[...] = jnp.zeros_like(acc)

        acc[...] += _dot_tn(h_ref[...], dp_ref[...])

        @pl.when(i == ni - 1)
        def _():
            dw_ref[0] = acc[...].astype(BF16)

    return pl.pallas_call(
        body, name="in_bwd_dw", grid=(N_DEV, ni),
        in_specs=[pl.BlockSpec((tm, D_MODEL), lambda d, i: (i, 0)),
                  pl.BlockSpec((tm, IN_SHARD), lambda d, i: (i, d))],
        out_specs=pl.BlockSpec((1, D_MODEL, IN_SHARD), lambda d, i: (d, 0, 0)),
        out_shape=jax.ShapeDtypeStruct((N_DEV, D_MODEL, IN_SHARD), BF16),
        scratch_shapes=[pltpu.VMEM((D_MODEL, IN_SHARD), F32)],
        compiler_params=_params(),
    )(h1, dproj)


def _adamw(recv, w, m, v):
    rows, cols = w.shape
    tr = 128 if rows % 128 == 0 else rows

    def body(p_ref, w_ref, m_ref, v_ref, g_ref, d_ref, nm_ref, nv_ref):
        g = p_ref[0].astype(F32)
        for s in range(1, N_DEV):
            g = g + p_ref[s].astype(F32)
        m_new = ADAM_B1 * m_ref[...] + (1.0 - ADAM_B1) * g
        v_new = ADAM_B2 * v_ref[...] + (1.0 - ADAM_B2) * (g * g)
        m_hat = m_new / (1.0 - ADAM_B1 ** ADAM_STEP)
        v_hat = v_new / (1.0 - ADAM_B2 ** ADAM_STEP)
        g_ref[...] = g
        d_ref[...] = -ADAM_LR * (m_hat / (jnp.sqrt(v_hat) + ADAM_EPS) + ADAM_WD * w_ref[...])
        nm_ref[...] = m_new
        nv_ref[...] = v_new

    blk = pl.BlockSpec((tr, cols), lambda i: (i, 0))
    out = jax.ShapeDtypeStruct((rows, cols), F32)
    return pl.pallas_call(
        body, name=f"adamw_{rows}x{cols}", grid=(rows // tr,),
        in_specs=[pl.BlockSpec((N_DEV, tr, cols), lambda i: (0, i, 0)), blk, blk, blk],
        out_specs=(blk,) * 4, out_shape=(out,) * 4,
        compiler_params=_params(),
    )(recv, w, m, v)


def _rope_tables(s_len):
    pos = jnp.arange(s_len, dtype=F32)
    inv_freq = ROPE_THETA ** (-jnp.arange(0, HEAD_DIM, 2, dtype=F32) / HEAD_DIM)
    ang = pos[:, None] * inv_freq[None, :]
    cos, sin = jnp.cos(ang), jnp.sin(ang)
    cos2 = jnp.concatenate([cos, cos, cos, cos], axis=1)
    sin2 = jnp.concatenate([-sin, sin, -sin, sin], axis=1)
    return cos2, sin2


def _block_diag_ones(n):
    i = jnp.arange(n)
    return (i[:, None] // HEAD_DIM == i[None, :] // HEAD_DIM).astype(BF16)


def _pad_cols(t):
    return jnp.pad(t, ((0, 0), (0, FF_PAD - FF_SHARD)))


def _pad_rows(t):
    return jnp.pad(t, ((0, FF_PAD - FF_SHARD), (0, 0)))


def _pack_small(n1, n2, ndil, nsb, nq, nk):
    pad = lambda t: jnp.pad(t.reshape(1, HEAD_DIM), ((0, 0), (0, 128 - HEAD_DIM)))
    rows = [n1.reshape(8, 128), n2.reshape(8, 128), ndil.reshape(4, 128), nsb.reshape(4, 128),
            pad(nq), pad(nk), jnp.zeros((6, 128), F32)]
    return jnp.concatenate(rows, axis=0)


def _unpack_small(t):
    return (t[0:8].reshape(1, D_MODEL), t[8:16].reshape(1, D_MODEL), t[16:20].reshape(1, D_GRP),
            t[20:24].reshape(1, D_GRP), t[24:25, :HEAD_DIM], t[25:26, :HEAD_DIM])


def kernel(x, attn_norm_w, w_in, q_norm_w, k_norm_w, dil_out_norm_w, sb_out_norm_w, w_out, ffn_norm_w, w_gate, w_up, w_down, loss_target, m_attn_norm_w, m_w_in, m_q_norm_w, m_k_norm_w, m_dil_out_norm_w, m_sb_out_norm_w, m_w_out, m_ffn_norm_w, m_w_gate, m_w_up, m_w_down, v_attn_norm_w, v_w_in, v_q_norm_w, v_k_norm_w, v_dil_out_norm_w, v_sb_out_norm_w, v_w_out, v_ffn_norm_w, v_w_gate, v_w_up, v_w_down):
    s_len = x.shape[1]
    x2, tgt = x[0], loss_target[0]

    my_idx = _flat_index(_mesh_pos())
    slot_is_mine = (jnp.arange(N_DEV) == my_idx)[:, None, None]

    (a_g,) = _gather_weights([w_in[0].astype(BF16)])
    gu_loc = jnp.concatenate([_pad_cols(w_gate[0]), _pad_cols(w_up[0])], axis=0).astype(BF16)
    b_loc = jnp.concatenate([_pad_rows(w_down[0]), w_out[0]], axis=0).astype(BF16)
    own_in_place = lambda t: jnp.where(slot_is_mine, t[None], jnp.zeros((), t.dtype))
    w_send, w_recv, w_srcs, w_lands, w_token = _spread_start(
        [gu_loc, b_loc], [own_in_place(gu_loc), own_in_place(b_loc)], blockwise=False, name="weights_start")

    cos2, sin2 = _rope_tables(s_len)
    bd128, bd512 = _block_diag_ones(128), _block_diag_ones(D_GRP)
    idx = jnp.arange(SB_TILE)
    tri_suf = (idx[:, None] > idx[None, :]).astype(BF16)
    tri_pre = (idx[:, None] < idx[None, :]).astype(BF16)
    qnw2 = jnp.concatenate([q_norm_w, q_norm_w], axis=1) + w_token[0:1]
    knw2 = jnp.concatenate([k_norm_w, k_norm_w], axis=1)

    (h1, qraw, kraw, q, k, va, qs, ks, vs,
     q4, k4, v4, q16, k16, v16) = _attn_in(x2, attn_norm_w, a_g, cos2, sin2, qnw2, knw2, bd128)
    qkv_views = {1: (q, k, va), 4: (q4, k4, v4), 16: (q16, k16, v16)}
    o_b, lse_b = [], []
    for r in DILATIONS:
        o, lse = _dil_fwd(*qkv_views[r], r)
        o_b.append(o)
        lse_b.append(lse)
    o_sb, c_sb = _sb_fwd(qs, ks, vs, tri_suf)
    gu_g, b_g = _spread_wait(w_send, w_recv, w_srcs, w_lands, c_sb, blockwise=False, name="weights_wait")
    o_dil, lse_tot, lse4, lse16, mixed, x1 = _attn_out(o_b, lse_b, o_sb, x2, dil_out_norm_w, sb_out_norm_w, b_g)
    g, u, h2, dy, loss_parts = _ffn_fwd(x1, ffn_norm_w, tgt, gu_g, b_g)
    loss = lax.psum(jnp.sum(loss_parts[::8, 0]), ("x", "y", "c"))

    dg, du, act, dh2 = _ffn_bwd_dx(dy, g, u, gu_g, b_g)
    (dx1, do_dil, delta, do_sb, dwout, dn2, dndil, dnsb, do4, dl4, do16, dl16) = _attn_out_bwd(
        dy, dh2, x1, ffn_norm_w, b_g, mixed, o_dil, o_sb, dil_out_norm_w, sb_out_norm_w, bd512)
    dwg, dwu, dwd = _ffn_bwd_dw(h2, dy, dg, du, act)
    early = [dwg, dwu, dwd, dwout]
    own_slot_only = lambda t: jnp.where(slot_is_mine, t, jnp.zeros((), t.dtype))
    g_send, g_recv, g_srcs, g_lands, g_token = _spread_start(
        early, [own_slot_only(t) for t in early], blockwise=True, name="grads_start")
    tri_pre = tri_pre + g_token[0, 0].astype(BF16)
    dqs, dks, dvs = _sb_bwd(qs, ks, vs, do_sb, c_sb, tri_suf, tri_pre)
    cot_views = {1: (do_dil, lse_tot, delta), 4: (do4, lse4, dl4), 16: (do16, lse16, dl16)}
    dq_b, dk_b, dv_b = [], [], []
    for r in DILATIONS:
        dq, dk, dv = _dil_bwd(*qkv_views[r], *cot_views[r], r)
        dq_b.append(dq)
        dk_b.append(dk)
        dv_b.append(dv)
    dproj, dqn2, dkn2 = _qkv_bwd(dq_b, dk_b, dv_b, dqs, dks, dvs, qraw, kraw, cos2, sin2, qnw2, knw2, bd128)
    grad_x, dn1 = _in_bwd_dx(dproj, a_g, x2, dx1, attn_norm_w)
    dwin = _in_bwd_dw(h1, dproj)
    dqn = dqn2[:, :HEAD_DIM] + dqn2[:, HEAD_DIM:]
    dkn = dkn2[:, :HEAD_DIM] + dkn2[:, HEAD_DIM:]

    small = _pack_small(dn1, dn2, dndil, dnsb, dqn, dkn)
    r_in, r_small = _exchange_grads([dwin], small)
    r_gate, r_up, r_down, r_out = _spread_wait(g_send, g_recv, g_srcs, g_lands, r_small, blockwise=True,
                                               name="grads_wait")
    big = {
        "w_in": _adamw(r_in, w_in[0], m_w_in[0], v_w_in[0]),
        "w_gate": tuple(t[:, :FF_SHARD] for t in _adamw(r_gate, _pad_cols(w_gate[0]), _pad_cols(m_w_gate[0]), _pad_cols(v_w_gate[0]))),
        "w_up": tuple(t[:, :FF_SHARD] for t in _adamw(r_up, _pad_cols(w_up[0]), _pad_cols(m_w_up[0]), _pad_cols(v_w_up[0]))),
        "w_down": tuple(t[:FF_SHARD] for t in _adamw(r_down, _pad_rows(w_down[0]), _pad_rows(m_w_down[0]), _pad_rows(v_w_down[0]))),
        "w_out": _adamw(r_out, w_out[0], m_w_out[0], v_w_out[0]),
    }
    packs = [_pack_small(*ts) for ts in (
        (attn_norm_w, ffn_norm_w, dil_out_norm_w, sb_out_norm_w, q_norm_w, k_norm_w),
        (m_attn_norm_w, m_ffn_norm_w, m_dil_out_norm_w, m_sb_out_norm_w, m_q_norm_w, m_k_norm_w),
        (v_attn_norm_w, v_ffn_norm_w, v_dil_out_norm_w, v_sb_out_norm_w, v_q_norm_w, v_k_norm_w))]
    small_out = [_unpack_small(t) for t in _adamw(r_small, *packs)]
    names = ["attn_norm_w", "w_in", "q_norm_w", "k_norm_w", "dil_out_norm_w", "sb_out_norm_w", "w_out",
             "ffn_norm_w", "w_gate", "w_up", "w_down"]
    small_pos = {"attn_norm_w": 0, "ffn_norm_w": 1, "dil_out_norm_w": 2, "sb_out_norm_w": 3,
                 "q_norm_w": 4, "k_norm_w": 5}
    outs = [loss, grad_x[None]]
    for kind in range(4):
        for name in names:
            if name in small_pos:
                outs.append(small_out[kind][small_pos[name]])
            else:
                outs.append(big[name][kind][None])
    return tuple(outs)
```

```python
import functools

import jax
import jax.numpy as jnp
from jax import lax
from jax.experimental import pallas as pl
from jax.experimental.pallas import tpu as pltpu

F32 = jnp.float32
BF16 = jnp.bfloat16

N_DEV = 8
D_MODEL = 1024
HEAD_DIM = 64
D_GRP = 512
D_IN = 6 * D_GRP
IN_SHARD = D_IN // N_DEV
FF_SHARD = 352
FF_PAD = 384
FF_BLOCK = 2 * FF_PAD
FF_STEPS = N_DEV // 2
W_PACK_ROWS = 3 * FF_PAD + 128
W_OUT_BLOCK = 3 * FF_PAD // 128
OUT_SHARD = D_MODEL // N_DEV
BLOCK = 128
DILATIONS = (1, 4, 16)
ROPE_THETA = 10000.0
EPS = 1e-6
ATT_SCALE = HEAD_DIM ** -0.5
NEG = -1e30

ADAM_LR = 0.001
ADAM_B1 = 0.9
ADAM_B2 = 0.999
ADAM_EPS = 1e-08
ADAM_WD = 0.01
ADAM_STEP = 10

SB_TILE = 256
SB_PAIRS = 4
SB_BWD_PAIRS = 2
ROW_TILE = 512
VMEM_LIMIT = 56 * 1024 * 1024
MESH = pl.DeviceIdType.MESH


def _dot(a, b):
    return jnp.dot(a, b, preferred_element_type=F32)


def _dot_nt(a, b):
    return lax.dot_general(a, b, (((1,), (1,)), ((), ())), preferred_element_type=F32)


def _dot_tn(a, b):
    return lax.dot_general(a, b, (((0,), (0,)), ((), ())), preferred_element_type=F32)


def _mm_split(t, m):
    hi = t.astype(BF16)
    lo = (t - hi.astype(F32)).astype(BF16)
    return _dot(hi, m) + _dot(lo, m)


def _params(**kw):
    return pltpu.CompilerParams(vmem_limit_bytes=VMEM_LIMIT, **kw)


def _full(shape):
    nd = len(shape)
    return pl.BlockSpec(shape, lambda *_: (0,) * nd)


def _view_shape(s_len, r, dtype):
    return jax.ShapeDtypeStruct((s_len // r, r * D_GRP), dtype)


def _view_spec(tm, r):
    return pl.BlockSpec((tm // r, r * D_GRP), lambda i: (i, 0))


def _swap_halves(t):
    lane = lax.broadcasted_iota(jnp.int32, t.shape, 1)
    first = (lane & 32) == 0
    return jnp.where(first, pltpu.roll(t, 96, 1), pltpu.roll(t, 32, 1))


def _log_sigmoid(z):
    return jnp.minimum(z, 0.0) - jnp.log(1.0 + jnp.exp(-jnp.abs(z)))


def _log_sigmoid_pair(z):
    neg_abs = lax.bitcast_convert_type(lax.bitcast_convert_type(z, jnp.uint32) | jnp.uint32(0x80000000), F32)
    lb = jnp.minimum(z, 0.0) - jnp.log(1.0 + jnp.exp(neg_abs))
    return lb, lb - z


def _cumsum_mm(t, tri):
    return _dot(t.astype(BF16), tri)


def _split_views(src_ref, stage_ref, views4, views16):
    slabs, n, _ = src_ref.shape
    n4, n16 = n // 4, n // 16
    for j in range(slabs):
        g, lanes = j // 4, 128 * (j % 4)
        src, stage = src_ref.at[j], stage_ref.at[j]
        for c4 in range(4):
            blk = src[pl.ds(c4, n4, stride=4), :]
            stage[n4 * c4:n4 * (c4 + 1), :] = blk
            col = D_GRP * c4 + lanes
            views4[g][:, col:col + 128] = blk.astype(views4[g].dtype)
        for c4 in range(4):
            for c1 in range(4):
                blk = stage[pl.ds(n4 * c4 + c1, n16, stride=4), :]
                col = D_GRP * (4 * c1 + c4) + lanes
                views16[g][:, col:col + 128] = blk.astype(views16[g].dtype)


def _merge_views(views4, views16, stage_ref, dst4_ref, dst16_ref):
    slabs, n, _ = dst4_ref.shape
    n4, n16 = n // 4, n // 16
    for j in range(slabs):
        g, lanes = j // 4, 128 * (j % 4)
        dst4, dst16, stage = dst4_ref.at[j], dst16_ref.at[j], stage_ref.at[j]
        for c4 in range(4):
            col = D_GRP * c4 + lanes
            dst4[pl.ds(c4, n4, stride=4), :] = views4[g][:, col:col + 128].astype(F32)
            for c1 in range(4):
                col = D_GRP * (4 * c1 + c4) + lanes
                stage[pl.ds(n4 * c4 + c1, n16, stride=4), :] = views16[g][:, col:col + 128].astype(F32)
        for c4 in range(4):
            dst16[pl.ds(c4, n4, stride=4), :] = stage[n4 * c4:n4 * (c4 + 1), :]


def _slab_group(ref, g):
    return jnp.concatenate([ref[4 * g + p] for p in range(4)], axis=1)


def _mesh_pos():
    return lax.axis_index("x"), lax.axis_index("y"), lax.axis_index("c")


def _flat_index(p):
    return 4 * p[0] + 2 * p[1] + p[2]


def _gather_weights(shards):
    n_arr = len(shards)

    def body(*refs):
        srcs, outs = refs[:n_arr], refs[n_arr:2 * n_arr]
        send_sems, recv_sems, local_sems = refs[2 * n_arr:]
        x, y, c = _mesh_pos()
        me, sibling = (x, y, c), (x, y, 1 - c)
        chips = [(1 - x, y), (x, 1 - y), (1 - x, 1 - y)]

        def copy(arr, k, block, to, own=False):
            dst = outs[arr].at[_flat_index(block)]
            return pltpu.make_async_remote_copy(
                src_ref=srcs[arr] if own else dst, dst_ref=dst,
                send_sem=send_sems.at[arr, k], recv_sem=recv_sems.at[arr, k],
                device_id=to, device_id_type=MESH)

        for arr in range(n_arr):
            mine = pltpu.make_async_copy(srcs[arr], outs[arr].at[_flat_index(me)], local_sems.at[arr])
            mine.start()
            first = [copy(arr, 0, me, sibling, own=True)]
            first += [copy(arr, 1 + j, me, (*chip, c), own=True) for j, chip in enumerate(chips)]
            for cp in first:
                cp.start()
        for arr in range(n_arr):
            passed = [copy(arr, 4 + j, (*chip, c), sibling) for j, chip in enumerate(chips)]
            for j, chip in enumerate(chips):
                copy(arr, 1 + j, (*chip, c), me).wait_recv()
                passed[j].start()
        for arr in range(n_arr):
            copy(arr, 0, sibling, me).wait_recv()
            for j, chip in enumerate(chips):
                copy(arr, 4 + j, (*chip, 1 - c), me).wait_recv()
            for k in range(7):
                copy(arr, k, me, me).wait_send()
            pltpu.make_async_copy(srcs[arr], outs[arr].at[_flat_index(me)], local_sems.at[arr]).wait()

    any_spec = pl.BlockSpec(memory_space=pl.ANY)
    return pl.pallas_call(
        body, name="gather_weights",
        out_shape=tuple(jax.ShapeDtypeStruct((N_DEV,) + s.shape, s.dtype) for s in shards),
        in_specs=[any_spec] * n_arr, out_specs=(any_spec,) * n_arr,
        scratch_shapes=[pltpu.SemaphoreType.DMA((n_arr, 7)), pltpu.SemaphoreType.DMA((n_arr, 7)),
                        pltpu.SemaphoreType.DMA((n_arr,))],
        compiler_params=pltpu.CompilerParams(has_side_effects=True),
    )(*shards)


_HBM_SPEC = pl.BlockSpec(memory_space=pltpu.HBM)
_SEM_SPEC = pl.BlockSpec(memory_space=pltpu.SEMAPHORE)
_DATAFLOW = pltpu.SideEffectType.DATAFLOW_SIDE_EFFECTING


def _peer_list(x, y, c):
    return [(1 - x if m & 4 else x, 1 - y if m & 2 else y, 1 - c if m & 1 else c) for m in range(1, N_DEV)]


def _spread_copies(src_refs, land_refs, send_sems, recv_sems, blockwise):
    x, y, c = _mesh_pos()
    my_idx = _flat_index((x, y, c))
    copies = []
    for a, (src, land) in enumerate(zip(src_refs, land_refs)):
        for k, peer in enumerate(_peer_list(x, y, c)):
            copies.append(pltpu.make_async_remote_copy(
                src_ref=src.at[_flat_index(peer)] if blockwise else src, dst_ref=land.at[my_idx],
                send_sem=send_sems.at[(N_DEV - 1) * a + k], recv_sem=recv_sems.at[(N_DEV - 1) * a + k],
                device_id=peer, device_id_type=MESH))
    return copies


def _spread_start(srcs, lands, blockwise, name):
    n = len(srcs)

    def body(*refs):
        for cp in _spread_copies(refs[:n], refs[n:2 * n], refs[2 * n], refs[2 * n + 1], blockwise):
            cp.start()
        token = refs[-1]
        token[...] = jnp.zeros_like(token)

    hbm = lambda t: pltpu.HBM(t.shape, t.dtype)
    sems = pltpu.SemaphoreType.DMA((n * (N_DEV - 1),))
    outs = pl.pallas_call(
        body, name=name,
        out_shape=(sems, sems) + tuple(hbm(t) for t in srcs) + tuple(hbm(t) for t in lands)
        + (jax.ShapeDtypeStruct((8, 128), F32),),
        in_specs=[_HBM_SPEC] * (2 * n),
        out_specs=(_SEM_SPEC, _SEM_SPEC) + (_HBM_SPEC,) * (2 * n) + (pl.BlockSpec(memory_space=pltpu.VMEM),),
        input_output_aliases={i: 2 + i for i in range(2 * n)},
        compiler_params=pltpu.CompilerParams(has_side_effects=_DATAFLOW),
    )(*[pltpu.with_memory_space_constraint(t, pltpu.HBM) for t in list(srcs) + list(lands)])
    return outs[0], outs[1], outs[2:2 + n], outs[2 + n:2 + 2 * n], outs[-1]


def _spread_wait(send_sems, recv_sems, srcs, lands, after, blockwise, name):
    n = len(srcs)

    def body(*refs):
        for cp in _spread_copies(refs[:n], refs[n:2 * n], refs[2 * n], refs[2 * n + 1], blockwise):
            cp.wait_send()
            cp.wait_recv()

    hbm = lambda t: pltpu.HBM(t.shape, t.dtype)
    outs = pl.pallas_call(
        body, name=name,
        out_shape=tuple(hbm(t) for t in srcs) + tuple(hbm(t) for t in lands),
        in_specs=[_HBM_SPEC] * (2 * n) + [_SEM_SPEC, _SEM_SPEC, pl.BlockSpec(memory_space=pl.ANY)],
        out_specs=(_HBM_SPEC,) * (2 * n),
        input_output_aliases={i: i for i in range(2 * n)},
        compiler_params=pltpu.CompilerParams(has_side_effects=_DATAFLOW),
    )(*srcs, *lands, send_sems, recv_sems, after)
    return outs[n:]


def _exchange_grads(parts, small):
    n_arr = len(parts)

    def body(*refs):
        ins, outs = refs[:n_arr + 1], refs[n_arr + 1:2 * (n_arr + 1)]
        send_sems, recv_sems, local_sems = refs[2 * (n_arr + 1):]
        x, y, c = _mesh_pos()
        me = (x, y, c)
        my_idx = _flat_index(me)
        peers = []
        for m in range(1, N_DEV):
            peers.append((1 - x if m & 4 else x, 1 - y if m & 2 else y, 1 - c if m & 1 else c))

        def src_block(arr, dev):
            return ins[arr] if arr == n_arr else ins[arr].at[_flat_index(dev)]

        def copy(arr, k):
            return pltpu.make_async_remote_copy(
                src_ref=src_block(arr, peers[k]), dst_ref=outs[arr].at[my_idx],
                send_sem=send_sems.at[arr, k], recv_sem=recv_sems.at[arr, k],
                device_id=peers[k], device_id_type=MESH)

        def local(arr):
            return pltpu.make_async_copy(src_block(arr, me), outs[arr].at[my_idx], local_sems.at[arr])

        for arr in range(n_arr + 1):
            local(arr).start()
            for k in range(N_DEV - 1):
                copy(arr, k).start()
        for arr in range(n_arr + 1):
            for k in range(N_DEV - 1):
                cp = copy(arr, k)
                cp.wait_send()
                cp.wait_recv()
            local(arr).wait()

    any_spec = pl.BlockSpec(memory_space=pl.ANY)
    out_shape = tuple(jax.ShapeDtypeStruct(p.shape, p.dtype) for p in parts)
    out_shape += (jax.ShapeDtypeStruct((N_DEV,) + small.shape, small.dtype),)
    return pl.pallas_call(
        body, name="exchange_grads",
        out_shape=out_shape,
        in_specs=[any_spec] * (n_arr + 1), out_specs=(any_spec,) * (n_arr + 1),
        scratch_shapes=[pltpu.SemaphoreType.DMA((n_arr + 1, N_DEV - 1)),
                        pltpu.SemaphoreType.DMA((n_arr + 1, N_DEV - 1)),
                        pltpu.SemaphoreType.DMA((n_arr + 1,))],
        compiler_params=pltpu.CompilerParams(has_side_effects=True),
    )(*parts, small)


def _head_norm(t, w128, bd):
    ms = _mm_split(t * t, bd) * (1.0 / HEAD_DIM)
    r = lax.rsqrt(ms + EPS)
    return (t * r) * w128, r


def _attn_in(x2, wn1, a_g, cos2, sin2, qnw, knw, bd):
    s_len = x2.shape[0]
    tm = ROW_TILE

    def body(x_ref, wn_ref, w_ref, cos_ref, sin_ref, qnw_ref, knw_ref, bd_ref,
             h1_ref, qraw_ref, kraw_ref, q_ref, k_ref, va_ref, qs_ref, ks_ref, vs_ref,
             q4_ref, k4_ref, v4_ref, q16_ref, k16_ref, v16_ref, proj, slabs, stage):
        xx = x_ref[...]
        r = lax.rsqrt(jnp.mean(xx * xx, axis=-1, keepdims=True) + EPS)
        h = ((xx * r) * wn_ref[...]).astype(BF16)
        h1_ref[...] = h
        for d in range(N_DEV):
            proj[:, IN_SHARD * d:IN_SHARD * (d + 1)] = _dot(h, w_ref[d])
        cos_t, sin_t, bdm = cos_ref[...], sin_ref[...], bd_ref[...]
        for grp, (raw_ref, rope_ref, nw_ref) in enumerate(((qraw_ref, q_ref, qnw_ref),
                                                           (kraw_ref, k_ref, knw_ref))):
            for p in range(4):
                cols = slice(D_GRP * grp + 128 * p, D_GRP * grp + 128 * (p + 1))
                t = proj[:, cols]
                raw_ref[:, 128 * p:128 * (p + 1)] = t
                yn, _ = _head_norm(t, nw_ref[...], bdm)
                roped = yn * cos_t + _swap_halves(yn) * sin_t
                slabs[4 * grp + p] = roped
                rope_ref[:, 128 * p:128 * (p + 1)] = roped.astype(BF16)
        for p in range(4):
            slabs[8 + p] = proj[:, 2 * D_GRP + 128 * p:2 * D_GRP + 128 * (p + 1)]
        for grp, ref in ((2, va_ref), (3, qs_ref), (4, ks_ref), (5, vs_ref)):
            ref[...] = proj[:, D_GRP * grp:D_GRP * (grp + 1)].astype(BF16)
        _split_views(slabs, stage, (q4_ref, k4_ref, v4_ref), (q16_ref, k16_ref, v16_ref))

    row = lambda w: pl.BlockSpec((tm, w), lambda i: (i, 0))
    grp_bf = jax.ShapeDtypeStruct((s_len, D_GRP), BF16)
    grp_f32 = jax.ShapeDtypeStruct((s_len, D_GRP), F32)
    return pl.pallas_call(
        body, name="attn_in", grid=(s_len // tm,),
        in_specs=[row(D_MODEL), _full((1, D_MODEL)),
                  pl.BlockSpec((N_DEV, D_MODEL, IN_SHARD), lambda i: (0, 0, 0)),
                  row(128), row(128), _full((1, 128)), _full((1, 128)), _full((128, 128))],
        out_specs=(row(D_MODEL),) + (row(D_GRP),) * 8 + (_view_spec(tm, 4),) * 3 + (_view_spec(tm, 16),) * 3,
        out_shape=(jax.ShapeDtypeStruct((s_len, D_MODEL), BF16), grp_f32, grp_f32) + (grp_bf,) * 6
        + (_view_shape(s_len, 4, BF16),) * 3 + (_view_shape(s_len, 16, BF16),) * 3,
        scratch_shapes=[pltpu.VMEM((tm, D_IN), F32), pltpu.VMEM((12, tm, 128), F32), pltpu.VMEM((12, tm, 128), F32)],
        compiler_params=_params(),
    )(x2, wn1, a_g, cos2, sin2, qnw, knw, bd)


def _band_mask(n):
    i = lax.broadcasted_iota(jnp.int32, (2 * BLOCK, 2 * BLOCK), 0) & (BLOCK - 1)
    j = lax.broadcasted_iota(jnp.int32, (2 * BLOCK, 2 * BLOCK), 1)
    dist = i + BLOCK - j
    return (dist >= 0) & (dist <= BLOCK) & ((n - 1) * BLOCK + j >= 0)


def _stack_heads(t2, head0):
    return jnp.concatenate([jnp.where(head0, t2, 0), jnp.where(head0, 0, t2)], axis=0)


def _unstack_heads(t, head0):
    return jnp.where(head0, t[0:BLOCK], t[BLOCK:2 * BLOCK])


def _dil_fwd(qv, kv, vv, r):
    sub_len = qv.shape[0]
    nb = sub_len // BLOCK

    def body(q_ref, kp_ref, kc_ref, vp_ref, vc_ref, o_ref, lse_ref):
        n = pl.program_id(1)
        valid = _band_mask(n)
        lane = lax.broadcasted_iota(jnp.int32, (BLOCK, 128), 1)
        head0 = lane < HEAD_DIM
        pairs = [slice(128 * p, 128 * (p + 1)) for p in range(4)]
        qqs = [_stack_heads(q_ref[:, c] * ATT_SCALE, head0) for c in pairs]
        kks = [jnp.concatenate([kp_ref[:, c], kc_ref[:, c]], axis=0) for c in pairs]
        vvs = [jnp.concatenate([vp_ref[:, c], vc_ref[:, c]], axis=0) for c in pairs]
        ss = [_dot_nt(qq, kk) for qq, kk in zip(qqs, kks)]
        prs, dens, lses = [], [], []
        for s in ss:
            s = jnp.where(valid, s, NEG)
            m = jnp.max(s, axis=-1, keepdims=True)
            pr = jnp.exp(s - m)
            den = jnp.sum(pr, axis=-1, keepdims=True)
            prs.append(pr.astype(BF16))
            dens.append(den)
            lses.append(m + jnp.log(den))
        pvs = [_dot(pr, vv2) for pr, vv2 in zip(prs, vvs)]
        for c, pv, den, lse in zip(pairs, pvs, dens, lses):
            o_ref[:, c] = _unstack_heads(pv / den, head0)
            lse_ref[:, c] = _unstack_heads(jnp.broadcast_to(lse, (2 * BLOCK, 128)), head0)

    cur = pl.BlockSpec((BLOCK, D_GRP), lambda c, n: (n, c))
    prev = pl.BlockSpec((BLOCK, D_GRP), lambda c, n: (jnp.maximum(n - 1, 0), c))
    out = jax.ShapeDtypeStruct(qv.shape, F32)
    return pl.pallas_call(
        body, name=f"dil_fwd_r{r}", grid=(r, nb),
        in_specs=[cur, prev, cur, prev, cur], out_specs=(cur, cur), out_shape=(out, out),
        compiler_params=_params(),
    )(qv, kv, kv, vv, vv)


def _dil_bwd(qv, kv, vv, dov, lsev, deltav, r):
    sub_len = qv.shape[0]
    nb = sub_len // BLOCK

    def body(q_ref, kp_ref, kc_ref, vp_ref, vc_ref, do_ref, lse_ref, dl_ref,
             dq_ref, dk_ref, dv_ref, dk_carry, dv_carry):
        n = pl.program_id(1)

        @pl.when(n == 0)
        def _():
            dk_carry[...] = jnp.zeros_like(dk_carry)
            dv_carry[...] = jnp.zeros_like(dv_carry)

        @pl.when(n < nb)
        def _():
            valid = _band_mask(n)
            lane = lax.broadcasted_iota(jnp.int32, (BLOCK, 128), 1)
            head0 = lane < HEAD_DIM
            pairs = [slice(128 * p, 128 * (p + 1)) for p in range(4)]
            qqs = [_stack_heads(q_ref[:, c] * ATT_SCALE, head0) for c in pairs]
            dos = [_stack_heads(do_ref[:, c], head0) for c in pairs]
            kks = [jnp.concatenate([kp_ref[:, c], kc_ref[:, c]], axis=0) for c in pairs]
            vvs = [jnp.concatenate([vp_ref[:, c], vc_ref[:, c]], axis=0) for c in pairs]
            ss = [_dot_nt(qq, kk) for qq, kk in zip(qqs, kks)]
            dps = [_dot_nt(do, vv2) for do, vv2 in zip(dos, vvs)]
            prs, dss = [], []
            for c, s, dp in zip(pairs, ss, dps):
                stats = []
                for ref in (lse_ref, dl_ref):
                    t2 = ref[:, c]
                    stats.append(jnp.concatenate(
                        [jnp.sum(jnp.where(lane == 0, t2, 0.0), axis=-1, keepdims=True),
                         jnp.sum(jnp.where(lane == HEAD_DIM, t2, 0.0), axis=-1, keepdims=True)], axis=0))
                pr = jnp.where(valid, jnp.exp(jnp.minimum(s - stats[0], 0.0)), 0.0)
                prs.append(pr.astype(BF16))
                dss.append((pr * (dp - stats[1])).astype(BF16))
            dqs = [_dot(ds, kk) for ds, kk in zip(dss, kks)]
            dkks = [_dot_tn(ds, qq) for ds, qq in zip(dss, qqs)]
            dvvs = [_dot_tn(pr, do) for pr, do in zip(prs, dos)]
            for c, dq, dkk, dvv in zip(pairs, dqs, dkks, dvvs):
                dq_ref[:, c] = _unstack_heads(dq, head0) * ATT_SCALE
                dk_ref[:, c] = dk_carry[:, c] + dkk[:BLOCK]
                dv_ref[:, c] = dv_carry[:, c] + dvv[:BLOCK]
                dk_carry[:, c] = dkk[BLOCK:]
                dv_carry[:, c] = dvv[BLOCK:]

        @pl.when(n == nb)
        def _():
            dk_ref[...] = dk_carry[...]
            dv_ref[...] = dv_carry[...]

    last = nb - 1
    cur = pl.BlockSpec((BLOCK, D_GRP), lambda c, n: (jnp.minimum(n, last), c))
    prev = pl.BlockSpec((BLOCK, D_GRP), lambda c, n: (jnp.clip(n - 1, 0, last), c))
    out = jax.ShapeDtypeStruct(qv.shape, F32)
    return pl.pallas_call(
        body, name=f"dil_bwd_r{r}", grid=(r, nb + 1),
        in_specs=[cur, prev, cur, prev, cur, cur, cur, cur],
        out_specs=(cur, prev, prev), out_shape=(out, out, out),
        scratch_shapes=[pltpu.VMEM((BLOCK, D_GRP), F32), pltpu.VMEM((BLOCK, D_GRP), F32)],
        compiler_params=_params(),
    )(qv, kv, kv, vv, vv, dov, lsev, deltav)


def _sb_fwd(qs, ks, vs, tri_suf):
    s_len = qs.shape[0]
    t = SB_TILE
    nq = s_len // t

    npair = SB_PAIRS

    def body(q_ref, k_ref, v_ref, u_ref, o_ref, c_ref, qq, vt, acc, cf, csave):
        row = lax.broadcasted_iota(jnp.int32, (2 * t, t), 0) & (t - 1)
        col = lax.broadcasted_iota(jnp.int32, (2 * t, t), 1)
        diag_mask = col < row
        lane1 = lax.broadcasted_iota(jnp.int32, (t, 128), 1)
        head0 = lane1 < HEAD_DIM
        lane2 = lax.broadcasted_iota(jnp.int32, (2 * t, 128), 1)
        uu = u_ref[...]
        pr = range(npair)
        cols = [slice(128 * pp, 128 * (pp + 1)) for pp in pr]

        i = pl.program_id(1)

        @pl.when(i == 0)
        def _():
            def transpose_v(j, _):
                rows = pl.ds(pl.multiple_of(j * t, t), t)
                for pp in pr:
                    vt[pp, j] = v_ref[rows, cols[pp]].astype(F32).T.astype(BF16)
                return 0

            lax.fori_loop(0, nq, transpose_v, 0)

        for pp in pr:
            q2 = q_ref[:, cols[pp]] * ATT_SCALE
            qq[pp, 0:t, :] = jnp.where(head0, q2, 0)
            qq[pp, t:2 * t, :] = jnp.where(head0, 0, q2)
        acc[...] = jnp.zeros_like(acc)
        cf[...] = jnp.zeros_like(cf)
        csave[...] = jnp.zeros_like(csave)

        def tile(kb, diag):
            krows = pl.ds(pl.multiple_of(kb * t, t), t)
            zs = [_dot_nt(qq[pp], k_ref[krows, cols[pp]]) for pp in pr]
            lbk = [_log_sigmoid_pair(z) for z in zs]
            lks = [jnp.where(diag_mask, lk, 0.0) if diag else lk for _, lk in lbk]
            sufs = [_cumsum_mm(lk, uu) for lk in lks]
            carries = [cf[pp] for pp in pr]
            avs = []
            for pp in pr:
                a = jnp.exp(lbk[pp][0] + (sufs[pp] + jnp.concatenate([carries[pp]] * (t // 128), axis=1)))
                avs.append((jnp.where(diag_mask, a, 0.0) if diag else a).astype(BF16))
            pvs = [_dot_nt(vt[pp, kb], avs[pp]) for pp in pr]
            for pp in pr:
                acc[pp] += pvs[pp]
                csave[pp] = jnp.where(lane2 == kb, carries[pp], csave[pp])
                cf[pp] = carries[pp] + jnp.broadcast_to(jnp.sum(lks[pp], axis=-1, keepdims=True), (2 * t, 128))

        tile(i, True)

        def k_block(step, _):
            tile(i - 1 - step, False)
            return 0

        lax.fori_loop(0, i, k_block, 0)
        for pp in pr:
            o_ref[:, cols[pp]] = jnp.where(head0, acc[pp, :, 0:t].T, acc[pp, :, t:2 * t].T)
            c_ref[2 * pp] = csave[pp, 0:t, :]
            c_ref[2 * pp + 1] = csave[pp, t:2 * t, :]

    width = 128 * npair
    kv = pl.BlockSpec((s_len, width), lambda p, i: (0, p))
    qo = pl.BlockSpec((t, width), lambda p, i: (i, p))
    return pl.pallas_call(
        body, name="sb_fwd", grid=(4 // npair, nq),
        in_specs=[qo, kv, kv, pl.BlockSpec((t, t), lambda p, i: (0, 0))],
        out_specs=(qo, pl.BlockSpec((2 * npair, t, 128), lambda p, i: (p, i, 0))),
        out_shape=(jax.ShapeDtypeStruct((s_len, D_GRP), F32),
                   jax.ShapeDtypeStruct((8, s_len, 128), F32)),
        scratch_shapes=[pltpu.VMEM((npair, 2 * t, 128), BF16), pltpu.VMEM((npair, nq, 128, t), BF16),
                        pltpu.VMEM((npair, 128, 2 * t), F32),
                        pltpu.VMEM((npair, 2 * t, 128), F32), pltpu.VMEM((npair, 2 * t, 128), F32)],
        compiler_params=_params(),
    )(qs, ks, vs, tri_suf)


def _sb_bwd(qs, ks, vs, dos, csaved, tri_suf, tri_pre):
    s_len = qs.shape[0]
    t = SB_TILE
    nq = s_len // t

    npair = SB_BWD_PAIRS

    def body(q_ref, k_ref, v_ref, do_ref, c_ref, u_ref, p_ref, dq_ref, dk_ref, dv_ref,
             qq, dd, qqt, ddt, kt, dq_acc, dkt, dvt, cg):
        row = lax.broadcasted_iota(jnp.int32, (2 * t, t), 0) & (t - 1)
        col = lax.broadcasted_iota(jnp.int32, (2 * t, t), 1)
        diag_mask = col < row
        lane1 = lax.broadcasted_iota(jnp.int32, (t, 128), 1)
        head0 = lane1 < HEAD_DIM
        lane2 = lax.broadcasted_iota(jnp.int32, (2 * t, 128), 1)
        uu, pm = u_ref[...], p_ref[...]
        pr = range(npair)
        cols = [slice(128 * pp, 128 * (pp + 1)) for pp in pr]
        i = pl.program_id(1)

        @pl.when(i == 0)
        def _():
            dkt[...] = jnp.zeros_like(dkt)
            dvt[...] = jnp.zeros_like(dvt)

            def transpose_k(j, _):
                rows = pl.ds(pl.multiple_of(j * t, t), t)
                for pp in pr:
                    kt[pp, j] = k_ref[rows, cols[pp]].astype(F32).T.astype(BF16)
                return 0

            lax.fori_loop(0, nq, transpose_k, 0)

        for pp in pr:
            q2 = q_ref[:, cols[pp]].astype(F32) * ATT_SCALE
            do2 = do_ref[:, cols[pp]].astype(F32)
            for src, nat, tr in ((q2, qq, qqt), (do2, dd, ddt)):
                stacked = jnp.concatenate([jnp.where(head0, src, 0.0), jnp.where(head0, 0.0, src)], axis=0)
                nat[pp] = stacked.astype(BF16)
                tr[pp] = stacked.T.astype(BF16)
        dq_acc[...] = jnp.zeros_like(dq_acc)
        cg[...] = jnp.zeros_like(cg)

        def tile(kb, diag):
            krows = pl.ds(pl.multiple_of(kb * t, t), t)
            zs = [_dot_nt(qq[pp], k_ref[krows, cols[pp]]) for pp in pr]
            das = [_dot_nt(dd[pp], v_ref[krows, cols[pp]]) for pp in pr]
            lbk = [_log_sigmoid_pair(z) for z in zs]
            lks = [jnp.where(diag_mask, lk, 0.0) if diag else lk for _, lk in lbk]
            sufs = [_cumsum_mm(lk, uu) for lk in lks]
            avs, gs = [], []
            for pp in pr:
                cs = jnp.concatenate([c_ref[2 * pp], c_ref[2 * pp + 1]], axis=0)
                cf = jnp.sum(jnp.where(lane2 == kb, cs, 0.0), axis=-1, keepdims=True)
                a = jnp.exp(lbk[pp][0] + (sufs[pp] + cf))
                a = jnp.where(diag_mask, a, 0.0) if diag else a
                avs.append(a.astype(BF16))
                gs.append(a * das[pp])
            gpres = [_cumsum_mm(g, pm) for g in gs]
            dzs = []
            for pp in pr:
                carry = cg[pp]
                beta = jnp.exp(lbk[pp][0])
                dz = gs[pp] - beta * (gs[pp] + (gpres[pp] + jnp.concatenate([carry] * (t // 128), axis=1)))
                dzs.append((jnp.where(diag_mask, dz, 0.0) if diag else dz).astype(BF16))
                cg[pp] = carry + jnp.broadcast_to(jnp.sum(gs[pp], axis=-1, keepdims=True), (2 * t, 128))
            dqs = [_dot_nt(kt[pp, kb], dzs[pp]) for pp in pr]
            dks = [_dot(qqt[pp], dzs[pp]) for pp in pr]
            dvs = [_dot(ddt[pp], avs[pp]) for pp in pr]
            for pp in pr:
                dq_acc[pp] += dqs[pp]
                dkt[pp, kb] += dks[pp]
                dvt[pp, kb] += dvs[pp]

        def k_block(kb, _):
            tile(kb, False)
            return 0

        lax.fori_loop(0, i, k_block, 0)
        tile(i, True)
        for pp in pr:
            dq_ref[:, cols[pp]] = jnp.where(head0, dq_acc[pp, :, 0:t].T, dq_acc[pp, :, t:2 * t].T) * ATT_SCALE

        @pl.when(i == nq - 1)
        def _():
            def untranspose(j, _):
                rows = pl.ds(pl.multiple_of(j * t, t), t)
                for pp in pr:
                    dk_ref[rows, cols[pp]] = dkt[pp, j].T
                    dv_ref[rows, cols[pp]] = dvt[pp, j].T
                return 0

            lax.fori_loop(0, nq, untranspose, 0)

    width = 128 * npair
    kv = pl.BlockSpec((s_len, width), lambda p, i: (0, p))
    qo = pl.BlockSpec((t, width), lambda p, i: (i, p))
    tri = pl.BlockSpec((t, t), lambda p, i: (0, 0))
    out = jax.ShapeDtypeStruct((s_len, D_GRP), F32)
    return pl.pallas_call(
        body, name="sb_bwd", grid=(4 // npair, nq),
        in_specs=[qo, kv, kv, qo, pl.BlockSpec((2 * npair, t, 128), lambda p, i: (p, i, 0)), tri, tri],
        out_specs=(qo, kv, kv), out_shape=(out, out, out),
        scratch_shapes=[pltpu.VMEM((npair, 2 * t, 128), BF16), pltpu.VMEM((npair, 2 * t, 128), BF16),
                        pltpu.VMEM((npair, 128, 2 * t), BF16), pltpu.VMEM((npair, 128, 2 * t), BF16),
                        pltpu.VMEM((npair, nq, 128, t), BF16),
                        pltpu.VMEM((npair, 128, 2 * t), F32),
                        pltpu.VMEM((npair, nq, 128, t), F32), pltpu.VMEM((npair, nq, 128, t), F32),
                        pltpu.VMEM((npair, 2 * t, 128), F32)],
        compiler_params=_params(),
    )(qs, ks, vs, dos, csaved, tri_suf, tri_pre)


def _attn_out(o_b, lse_b, o_sb, x2, wdil, wsb, b_g):
    s_len = x2.shape[0]
    tm = ROW_TILE

    def body(o1_ref, l1_ref, o4_ref, l4_ref, o16_ref, l16_ref, osb_ref, x_ref, wdil_ref, wsb_ref, w_ref,
             odil_ref, lse_ref, lse4_ref, lse16_ref, mixed_ref, x1_ref, stage, nat4, nat16):
        _merge_views((o4_ref, l4_ref), (o16_ref, l16_ref), stage, nat4, nat16)
        os_ = (o1_ref[...], _slab_group(nat4, 0), _slab_group(nat16, 0))
        ls = (l1_ref[...], _slab_group(nat4, 1), _slab_group(nat16, 1))
        mx = jnp.maximum(jnp.maximum(ls[0], ls[1]), ls[2])
        es = [jnp.exp(l - mx) for l in ls]
        den = es[0] + es[1] + es[2]
        o_dil = (es[0] * os_[0] + es[1] * os_[1] + es[2] * os_[2]) / den
        odil_ref[...] = o_dil
        lse = mx + jnp.log(den)
        lse_ref[...] = lse
        for p in range(4):
            nat4[p] = lse[:, 128 * p:128 * (p + 1)]
        _split_views(nat4.at[0:4], stage.at[0:4], (lse4_ref,), (lse16_ref,))
        halves = []
        for t, w_r in ((o_dil, wdil_ref), (osb_ref[...], wsb_ref)):
            r = lax.rsqrt(jnp.mean(t * t, axis=-1, keepdims=True) + EPS)
            halves.append(((t * r) * w_r[...]).astype(BF16))
        mixed = jnp.concatenate(halves, axis=1)
        mixed_ref[...] = mixed
        w = w_ref[...].reshape(D_MODEL, D_MODEL)
        x1_ref[...] = x_ref[...] + _dot(mixed, w)

    row = lambda w: pl.BlockSpec((tm, w), lambda i: (i, 0))
    return pl.pallas_call(
        body, name="attn_out", grid=(s_len // tm,),
        in_specs=[row(D_GRP)] * 2 + [_view_spec(tm, 4)] * 2 + [_view_spec(tm, 16)] * 2
        + [row(D_GRP), row(D_MODEL), _full((1, D_GRP)), _full((1, D_GRP)),
           pl.BlockSpec((N_DEV, OUT_SHARD, D_MODEL), lambda i: (0, W_OUT_BLOCK, 0))],
        out_specs=(row(D_GRP), row(D_GRP), _view_spec(tm, 4), _view_spec(tm, 16), row(D_MODEL), row(D_MODEL)),
        out_shape=(jax.ShapeDtypeStruct((s_len, D_GRP), F32), jax.ShapeDtypeStruct((s_len, D_GRP), F32),
                   _view_shape(s_len, 4, F32), _view_shape(s_len, 16, F32),
                   jax.ShapeDtypeStruct((s_len, D_MODEL), BF16), jax.ShapeDtypeStruct((s_len, D_MODEL), F32)),
        scratch_shapes=[pltpu.VMEM((8, tm, 128), F32)] * 3,
        compiler_params=_params(),
    )(o_b[0], lse_b[0], o_b[1], lse_b[1], o_b[2], lse_b[2], o_sb, x2, wdil, wsb, b_g)


def _two_shards(w_ref):
    return w_ref[...].reshape(FF_BLOCK, D_MODEL)


def _ffn_fwd(x1, wn2, tgt, w_g):
    s_len = x1.shape[0]
    tm = ROW_TILE
    ni = s_len // tm

    def body(x_ref, wn_ref, t_ref, wg_ref, wu_ref, wd_ref, g_ref, u_ref, h2_ref, dy_ref, loss_ref, acc):
        j = pl.program_id(1)

        @pl.when(j == 0)
        def _():
            xx = x_ref[...]
            r = lax.rsqrt(jnp.mean(xx * xx, axis=-1, keepdims=True) + EPS)
            h2_ref[...] = ((xx * r) * wn_ref[...]).astype(BF16)
            acc[...] = jnp.zeros_like(acc)

        h = h2_ref[...]
        g = _dot_nt(h, _two_shards(wg_ref))
        u = _dot_nt(h, _two_shards(wu_ref))
        g_ref[...] = g
        u_ref[...] = u
        act = (g * (1.0 / (1.0 + jnp.exp(-g)))) * u
        acc[...] += _dot(act.astype(BF16), _two_shards(wd_ref))

        @pl.when(j == FF_STEPS - 1)
        def _():
            err = (x_ref[...] + acc[...]) - t_ref[...]
            dy_ref[...] = err * (1.0 / D_MODEL)
            part = 0.5 * jnp.sum(jnp.mean(err * err, axis=-1, keepdims=True))
            loss_ref[...] = jnp.full((8, 128), part, F32)

    row = pl.BlockSpec((tm, D_MODEL), lambda i, j: (i, 0))
    hid = pl.BlockSpec((tm, FF_BLOCK), lambda i, j: (i, j))
    return pl.pallas_call(
        body, name="ffn_fwd", grid=(ni, FF_STEPS),
        in_specs=[row, pl.BlockSpec((1, D_MODEL), lambda i, j: (0, 0)), row,
                  pl.BlockSpec((2, FF_PAD, D_MODEL), lambda i, j: (j, 0, 0)),
                  pl.BlockSpec((2, FF_PAD, D_MODEL), lambda i, j: (j, 1, 0)),
                  pl.BlockSpec((2, FF_PAD, D_MODEL), lambda i, j: (j, 2, 0))],
        out_specs=(hid, hid, row, row, pl.BlockSpec((8, 128), lambda i, j: (i, 0))),
        out_shape=(jax.ShapeDtypeStruct((s_len, N_DEV * FF_PAD), F32),
                   jax.ShapeDtypeStruct((s_len, N_DEV * FF_PAD), F32),
                   jax.ShapeDtypeStruct((s_len, D_MODEL), BF16),
                   jax.ShapeDtypeStruct((s_len, D_MODEL), F32),
                   jax.ShapeDtypeStruct((ni * 8, 128), F32)),
        scratch_shapes=[pltpu.VMEM((tm, D_MODEL), F32)],
        compiler_params=_params(),
    )(x1, wn2, tgt, w_g, w_g, w_g)


def _ffn_bwd_dx(dy, g, u, w_g):
    s_len = dy.shape[0]
    tm = ROW_TILE

    def body(dy_ref, g_ref, u_ref, wg_ref, wu_ref, wd_ref, dg_ref, du_ref, act_ref, dh_ref, acc):
        j = pl.program_id(1)

        @pl.when(j == 0)
        def _():
            acc[...] = jnp.zeros_like(acc)

        gg, uu = g_ref[...], u_ref[...]
        da = _dot_nt(dy_ref[...].astype(BF16), _two_shards(wd_ref))
        sig = 1.0 / (1.0 + jnp.exp(-gg))
        silu = gg * sig
        act_ref[...] = (silu * uu).astype(BF16)
        du = (da * silu).astype(BF16)
        dg = (da * uu * (sig * (1.0 + gg * (1.0 - sig)))).astype(BF16)
        du_ref[...] = du
        dg_ref[...] = dg
        acc[...] += _dot(dg, _two_shards(wg_ref)) + _dot(du, _two_shards(wu_ref))

        @pl.when(j == FF_STEPS - 1)
        def _():
            dh_ref[...] = acc[...]

    row = pl.BlockSpec((tm, D_MODEL), lambda i, j: (i, 0))
    hid = pl.BlockSpec((tm, FF_BLOCK), lambda i, j: (i, j))
    hid_bf = jax.ShapeDtypeStruct((s_len, N_DEV * FF_PAD), BF16)
    return pl.pallas_call(
        body, name="ffn_bwd_dx", grid=(s_len // tm, FF_STEPS),
        in_specs=[row, hid, hid,
                  pl.BlockSpec((2, FF_PAD, D_MODEL), lambda i, j: (j, 0, 0)),
                  pl.BlockSpec((2, FF_PAD, D_MODEL), lambda i, j: (j, 1, 0)),
                  pl.BlockSpec((2, FF_PAD, D_MODEL), lambda i, j: (j, 2, 0))],
        out_specs=(hid, hid, hid, row),
        out_shape=(hid_bf, hid_bf, hid_bf, jax.ShapeDtypeStruct((s_len, D_MODEL), F32)),
        scratch_shapes=[pltpu.VMEM((tm, D_MODEL), F32)],
        compiler_params=_params(),
    )(dy, g, u, w_g, w_g, w_g)


def _ffn_bwd_dw(h2, dy, dg, du, act):
    s_len = h2.shape[0]
    tm = ROW_TILE
    ni = s_len // tm

    def body(h_ref, dy_ref, dg_ref, du_ref, act_ref, dwg_ref, dwu_ref, dwd_ref, ag, au, ad):
        i = pl.program_id(1)

        @pl.when(i == 0)
        def _():
            ag[...] = jnp.zeros_like(ag)
            au[...] = jnp.zeros_like(au)
            ad[...] = jnp.zeros_like(ad)

        h = h_ref[...]
        ag[...] += _dot_tn(h, dg_ref[...])
        au[...] += _dot_tn(h, du_ref[...])
        ad[...] += _dot_tn(act_ref[...], dy_ref[...].astype(BF16))

        @pl.when(i == ni - 1)
        def _():
            for half in range(2):
                cols = slice(FF_PAD * half, FF_PAD * (half + 1))
                dwg_ref[half] = ag[:, cols].astype(BF16)
                dwu_ref[half] = au[:, cols].astype(BF16)
            dwd_ref[...] = ad[...].astype(BF16).reshape(2, FF_PAD, D_MODEL)

    row = pl.BlockSpec((tm, D_MODEL), lambda j, i: (i, 0))
    hid = pl.BlockSpec((tm, FF_BLOCK), lambda j, i: (i, j))
    col_w = pl.BlockSpec((2, D_MODEL, FF_PAD), lambda j, i: (j, 0, 0))
    row_w = pl.BlockSpec((2, FF_PAD, D_MODEL), lambda j, i: (j, 0, 0))
    return pl.pallas_call(
        body, name="ffn_bwd_dw", grid=(FF_STEPS, ni),
        in_specs=[row, row, hid, hid, hid], out_specs=(col_w, col_w, row_w),
        out_shape=(jax.ShapeDtypeStruct((N_DEV, D_MODEL, FF_PAD), BF16),
                   jax.ShapeDtypeStruct((N_DEV, D_MODEL, FF_PAD), BF16),
                   jax.ShapeDtypeStruct((N_DEV, FF_PAD, D_MODEL), BF16)),
        scratch_shapes=[pltpu.VMEM((D_MODEL, FF_BLOCK), F32), pltpu.VMEM((D_MODEL, FF_BLOCK), F32),
                        pltpu.VMEM((FF_BLOCK, D_MODEL), F32)],
        compiler_params=_params(),
    )(h2, dy, dg, du, act)


def _rms_bwd(dy, t, w):
    r = lax.rsqrt(jnp.mean(t * t, axis=-1, keepdims=True) + EPS)
    gw = dy * w
    dt = r * (gw - t * ((r * r) * jnp.mean(gw * t, axis=-1, keepdims=True)))
    return dt, dy * t * r


def _attn_out_bwd(dy, dh2, x1, wn2, b_g, mixed, o_dil, o_sb, wdil, wsb, bd512):
    s_len = dy.shape[0]
    tm = ROW_TILE
    ni = s_len // tm

    def body(dy_ref, dh_ref, x1_ref, wn_ref, w_ref, mixed_ref, odil_ref, osb_ref, wdil_ref, wsb_ref, bd_ref,
             dx1_ref, dodil_ref, delta_ref, dosb_ref, dwout_ref, dwn_ref, dwdil_ref, dwsb_ref,
             do4_ref, dl4_ref, do16_ref, dl16_ref, wacc, both, stage):
        i = pl.program_id(0)

        @pl.when(i == 0)
        def _():
            wacc[...] = jnp.zeros_like(wacc)
            dwn_ref[...] = jnp.zeros_like(dwn_ref)
            dwdil_ref[...] = jnp.zeros_like(dwdil_ref)
            dwsb_ref[...] = jnp.zeros_like(dwsb_ref)

        dnorm, dw_rows = _rms_bwd(dh_ref[...], x1_ref[...], wn_ref[...])
        dx1 = dy_ref[...] + dnorm
        dx1_ref[...] = dx1
        dwn_ref[...] += jnp.sum(dw_rows, axis=0, keepdims=True)
        dx1b = dx1.astype(BF16)
        w = w_ref[...].reshape(D_MODEL, D_MODEL)
        dmixed = _dot_nt(dx1b, w)
        wacc[...] += _dot_tn(mixed_ref[...], dx1b)
        o_dil = odil_ref[...]
        d_odil, dw_rows = _rms_bwd(dmixed[:, :D_GRP], o_dil, wdil_ref[...])
        dwdil_ref[...] += jnp.sum(dw_rows, axis=0, keepdims=True)
        dodil_ref[...] = d_odil.astype(BF16)
        delta = _mm_split(d_odil * o_dil, bd_ref[...])
        delta_ref[...] = delta
        for p in range(4):
            both[p] = d_odil[:, 128 * p:128 * (p + 1)]
            both[4 + p] = delta[:, 128 * p:128 * (p + 1)]
        _split_views(both, stage, (do4_ref, dl4_ref), (do16_ref, dl16_ref))
        d_osb, dw_rows = _rms_bwd(dmixed[:, D_GRP:], osb_ref[...], wsb_ref[...])
        dwsb_ref[...] += jnp.sum(dw_rows, axis=0, keepdims=True)
        dosb_ref[...] = d_osb.astype(BF16)

        @pl.when(i == ni - 1)
        def _():
            dwout_ref[...] = wacc[...].astype(BF16).reshape(N_DEV, OUT_SHARD, D_MODEL)

    row = lambda w: pl.BlockSpec((tm, w), lambda i: (i, 0))
    return pl.pallas_call(
        body, name="attn_out_bwd", grid=(ni,),
        in_specs=[row(D_MODEL), row(D_MODEL), row(D_MODEL), _full((1, D_MODEL)),
                  pl.BlockSpec((N_DEV, OUT_SHARD, D_MODEL), lambda i: (0, W_OUT_BLOCK, 0)),
                  row(D_MODEL), row(D_GRP), row(D_GRP), _full((1, D_GRP)), _full((1, D_GRP)),
                  _full((D_GRP, D_GRP))],
        out_specs=(row(D_MODEL), row(D_GRP), row(D_GRP), row(D_GRP),
                   _full((N_DEV, OUT_SHARD, D_MODEL)), _full((1, D_MODEL)), _full((1, D_GRP)), _full((1, D_GRP)),
                   _view_spec(tm, 4), _view_spec(tm, 4), _view_spec(tm, 16), _view_spec(tm, 16)),
        out_shape=(jax.ShapeDtypeStruct((s_len, D_MODEL), F32), jax.ShapeDtypeStruct((s_len, D_GRP), BF16),
                   jax.ShapeDtypeStruct((s_len, D_GRP), F32), jax.ShapeDtypeStruct((s_len, D_GRP), BF16),
                   jax.ShapeDtypeStruct((N_DEV, OUT_SHARD, D_MODEL), BF16),
                   jax.ShapeDtypeStruct((1, D_MODEL), F32), jax.ShapeDtypeStruct((1, D_GRP), F32),
                   jax.ShapeDtypeStruct((1, D_GRP), F32),
                   _view_shape(s_len, 4, BF16), _view_shape(s_len, 4, F32),
                   _view_shape(s_len, 16, BF16), _view_shape(s_len, 16, F32)),
        scratch_shapes=[pltpu.VMEM((D_MODEL, D_MODEL), F32), pltpu.VMEM((8, tm, 128), F32),
                        pltpu.VMEM((8, tm, 128), F32)],
        compiler_params=_params(),
    )(dy, dh2, x1, wn2, b_g, mixed, o_dil, o_sb, wdil, wsb, bd512)


def _qkv_bwd(dq_b, dk_b, dv_b, dqs, dks, dvs, qraw, kraw, cos2, sin2, qnw, knw, bd):
    s_len = qraw.shape[0]
    tm = ROW_TILE
    ni = s_len // tm

    def body(dq1, dk1, dv1, dq4, dk4, dv4, dq16, dk16, dv16, dqs_ref, dks_ref, dvs_ref,
             qraw_ref, kraw_ref, cos_ref, sin_ref, qnw_ref, knw_ref, bd_ref,
             dproj_ref, dqn_ref, dkn_ref, stage, nat4, nat16):
        i = pl.program_id(0)

        @pl.when(i == 0)
        def _():
            dqn_ref[...] = jnp.zeros_like(dqn_ref)
            dkn_ref[...] = jnp.zeros_like(dkn_ref)

        _merge_views((dq4, dk4, dv4), (dq16, dk16, dv16), stage, nat4, nat16)
        cos_t, sin_t, bdm = cos_ref[...], sin_ref[...], bd_ref[...]
        for grp, (part1, raw_ref, nw_ref, dn_ref) in enumerate(((dq1, qraw_ref, qnw_ref, dqn_ref),
                                                                (dk1, kraw_ref, knw_ref, dkn_ref))):
            dn_acc = 0.0
            for p in range(4):
                cols = slice(128 * p, 128 * (p + 1))
                d_rope = part1[:, cols] + nat4[4 * grp + p] + nat16[4 * grp + p]
                d_norm = d_rope * cos_t + _swap_halves(d_rope * sin_t)
                t = raw_ref[:, cols]
                w = nw_ref[...]
                r = lax.rsqrt(_mm_split(t * t, bdm) * (1.0 / HEAD_DIM) + EPS)
                gw = d_norm * w
                corr = _mm_split(gw * t, bdm) * (1.0 / HEAD_DIM)
                dt = r * (gw - t * ((r * r) * corr))
                dn_acc = dn_acc + jnp.sum(d_norm * t * r, axis=0, keepdims=True)
                dproj_ref[:, D_GRP * grp + 128 * p:D_GRP * grp + 128 * (p + 1)] = dt.astype(BF16)
            dn_ref[...] += dn_acc
        dproj_ref[:, 2 * D_GRP:3 * D_GRP] = (dv1[...] + _slab_group(nat4, 2) + _slab_group(nat16, 2)).astype(BF16)
        dproj_ref[:, 3 * D_GRP:4 * D_GRP] = dqs_ref[...].astype(BF16)
        dproj_ref[:, 4 * D_GRP:5 * D_GRP] = dks_ref[...].astype(BF16)
        dproj_ref[:, 5 * D_GRP:6 * D_GRP] = dvs_ref[...].astype(BF16)

    row = lambda w: pl.BlockSpec((tm, w), lambda i: (i, 0))
    return pl.pallas_call(
        body, name="qkv_bwd", grid=(ni,),
        in_specs=[row(D_GRP)] * 3 + [_view_spec(tm, 4)] * 3 + [_view_spec(tm, 16)] * 3 + [row(D_GRP)] * 5
        + [row(128), row(128), _full((1, 128)), _full((1, 128)), _full((128, 128))],
        out_specs=(row(D_IN), _full((1, 128)), _full((1, 128))),
        out_shape=(jax.ShapeDtypeStruct((s_len, D_IN), BF16), jax.ShapeDtypeStruct((1, 128), F32),
                   jax.ShapeDtypeStruct((1, 128), F32)),
        scratch_shapes=[pltpu.VMEM((12, tm, 128), F32)] * 3,
        compiler_params=_params(),
    )(dq_b[0], dk_b[0], dv_b[0], dq_b[1], dk_b[1], dv_b[1], dq_b[2], dk_b[2], dv_b[2],
      dqs, dks, dvs, qraw, kraw, cos2, sin2, qnw, knw, bd)


def _in_bwd_dx(dproj, a_g, x2, dx1, wn1):
    s_len = x2.shape[0]
    tm = ROW_TILE
    ni = s_len // tm

    def body(dp_ref, w_ref, x_ref, dx1_ref, wn_ref, gx_ref, dwn_ref):
        i = pl.program_id(0)

        @pl.when(i == 0)
        def _():
            dwn_ref[...] = jnp.zeros_like(dwn_ref)

        dh = 0.0
        for d in range(N_DEV):
            dh = dh + _dot_nt(dp_ref[:, IN_SHARD * d:IN_SHARD * (d + 1)], w_ref[d])
        dnorm, dw_rows = _rms_bwd(dh, x_ref[...], wn_ref[...])
        gx_ref[...] = dx1_ref[...] + dnorm
        dwn_ref[...] += jnp.sum(dw_rows, axis=0, keepdims=True)

    row = lambda w: pl.BlockSpec((tm, w), lambda i: (i, 0))
    return pl.pallas_call(
        body, name="in_bwd_dx", grid=(ni,),
        in_specs=[row(D_IN), pl.BlockSpec((N_DEV, D_MODEL, IN_SHARD), lambda i: (0, 0, 0)),
                  row(D_MODEL), row(D_MODEL), _full((1, D_MODEL))],
        out_specs=(row(D_MODEL), _full((1, D_MODEL))),
        out_shape=(jax.ShapeDtypeStruct((s_len, D_MODEL), F32), jax.ShapeDtypeStruct((1, D_MODEL), F32)),
        compiler_params=_params(),
    )(dproj, a_g, x2, dx1, wn1)


def _in_bwd_dw(h1, dproj):
    s_len = h1.shape[0]
    tm = ROW_TILE
    ni = s_len // tm

    def body(h_ref, dp_ref, dw_ref, acc):
        i = pl.program_id(1)

        @pl.when(i == 0)
        def _():
            acc[...] = jnp.zeros_like(acc)

        acc[...] += _dot_tn(h_ref[...], dp_ref[...])

        @pl.when(i == ni - 1)
        def _():
            dw_ref[0] = acc[...].astype(BF16)

    return pl.pallas_call(
        body, name="in_bwd_dw", grid=(N_DEV, ni),
        in_specs=[pl.BlockSpec((tm, D_MODEL), lambda d, i: (i, 0)),
                  pl.BlockSpec((tm, IN_SHARD), lambda d, i: (i, d))],
        out_specs=pl.BlockSpec((1, D_MODEL, IN_SHARD), lambda d, i: (d, 0, 0)),
        out_shape=jax.ShapeDtypeStruct((N_DEV, D_MODEL, IN_SHARD), BF16),
        scratch_shapes=[pltpu.VMEM((D_MODEL, IN_SHARD), F32)],
        compiler_params=_params(),
    )(h1, dproj)


def _adamw(recv, w, m, v):
    rows, cols = w.shape
    tr = 128 if rows % 128 == 0 else rows

    def body(p_ref, w_ref, m_ref, v_ref, g_ref, d_ref, nm_ref, nv_ref):
        g = p_ref[0].astype(F32)
        for s in range(1, N_DEV):
            g = g + p_ref[s].astype(F32)
        m_new = ADAM_B1 * m_ref[...] + (1.0 - ADAM_B1) * g
        v_new = ADAM_B2 * v_ref[...] + (1.0 - ADAM_B2) * (g * g)
        m_hat = m_new / (1.0 - ADAM_B1 ** ADAM_STEP)
        v_hat = v_new / (1.0 - ADAM_B2 ** ADAM_STEP)
        g_ref[...] = g
        d_ref[...] = -ADAM_LR * (m_hat / (jnp.sqrt(v_hat) + ADAM_EPS) + ADAM_WD * w_ref[...])
        nm_ref[...] = m_new
        nv_ref[...] = v_new

    blk = pl.BlockSpec((tr, cols), lambda i: (i, 0))
    out = jax.ShapeDtypeStruct((rows, cols), F32)
    return pl.pallas_call(
        body, name=f"adamw_{rows}x{cols}", grid=(rows // tr,),
        in_specs=[pl.BlockSpec((N_DEV, tr, cols), lambda i: (0, i, 0)), blk, blk, blk],
        out_specs=(blk,) * 4, out_shape=(out,) * 4,
        compiler_params=_params(),
    )(recv, w, m, v)


def _rope_tables(s_len):
    pos = jnp.arange(s_len, dtype=F32)
    inv_freq = ROPE_THETA ** (-jnp.arange(0, HEAD_DIM, 2, dtype=F32) / HEAD_DIM)
    ang = pos[:, None] * inv_freq[None, :]
    cos, sin = jnp.cos(ang), jnp.sin(ang)
    cos2 = jnp.concatenate([cos, cos, cos, cos], axis=1)
    sin2 = jnp.concatenate([-sin, sin, -sin, sin], axis=1)
    return cos2, sin2


def _block_diag_ones(n):
    i = jnp.arange(n)
    return (i[:, None] // HEAD_DIM == i[None, :] // HEAD_DIM).astype(BF16)


def _pad_cols(t):
    return jnp.pad(t, ((0, 0), (0, FF_PAD - FF_SHARD)))


def _pad_rows(t):
    return jnp.pad(t, ((0, FF_PAD - FF_SHARD), (0, 0)))


def _pack_small(n1, n2, ndil, nsb, nq, nk):
    pad = lambda t: jnp.pad(t.reshape(1, HEAD_DIM), ((0, 0), (0, 128 - HEAD_DIM)))
    rows = [n1.reshape(8, 128), n2.reshape(8, 128), ndil.reshape(4, 128), nsb.reshape(4, 128),
            pad(nq), pad(nk), jnp.zeros((6, 128), F32)]
    return jnp.concatenate(rows, axis=0)


def _unpack_small(t):
    return (t[0:8].reshape(1, D_MODEL), t[8:16].reshape(1, D_MODEL), t[16:20].reshape(1, D_GRP),
            t[20:24].reshape(1, D_GRP), t[24:25, :HEAD_DIM], t[25:26, :HEAD_DIM])


def kernel(x, attn_norm_w, w_in, q_norm_w, k_norm_w, dil_out_norm_w, sb_out_norm_w, w_out, ffn_norm_w, w_gate, w_up, w_down, loss_target, m_attn_norm_w, m_w_in, m_q_norm_w, m_k_norm_w, m_dil_out_norm_w, m_sb_out_norm_w, m_w_out, m_ffn_norm_w, m_w_gate, m_w_up, m_w_down, v_attn_norm_w, v_w_in, v_q_norm_w, v_k_norm_w, v_dil_out_norm_w, v_sb_out_norm_w, v_w_out, v_ffn_norm_w, v_w_gate, v_w_up, v_w_down):
    s_len = x.shape[1]
    x2, tgt = x[0], loss_target[0]

    my_idx = _flat_index(_mesh_pos())
    slot_is_mine = (jnp.arange(N_DEV) == my_idx)[:, None, None]

    (a_g,) = _gather_weights([w_in[0].astype(BF16)])
    w_loc = jnp.concatenate([_pad_cols(w_gate[0]).T, _pad_cols(w_up[0]).T, _pad_rows(w_down[0]), w_out[0]],
                            axis=0).astype(BF16)
    own_in_place = lambda t: jnp.where(slot_is_mine, t[None], jnp.zeros((), t.dtype))
    w_send, w_recv, w_srcs, w_lands, w_token = _spread_start(
        [w_loc], [own_in_place(w_loc)], blockwise=False, name="weights_start")

    cos2, sin2 = _rope_tables(s_len)
    bd128, bd512 = _block_diag_ones(128), _block_diag_ones(D_GRP)
    idx = jnp.arange(SB_TILE)
    tri_suf = (idx[:, None] > idx[None, :]).astype(BF16)
    tri_pre = (idx[:, None] < idx[None, :]).astype(BF16)
    qnw2 = jnp.concatenate([q_norm_w, q_norm_w], axis=1) + w_token[0:1]
    knw2 = jnp.concatenate([k_norm_w, k_norm_w], axis=1)

    (h1, qraw, kraw, q, k, va, qs, ks, vs,
     q4, k4, v4, q16, k16, v16) = _attn_in(x2, attn_norm_w, a_g, cos2, sin2, qnw2, knw2, bd128)
    qkv_views = {1: (q, k, va), 4: (q4, k4, v4), 16: (q16, k16, v16)}
    o_b, lse_b = [], []
    for r in DILATIONS:
        o, lse = _dil_fwd(*qkv_views[r], r)
        o_b.append(o)
        lse_b.append(lse)
    o_sb, c_sb = _sb_fwd(qs, ks, vs, tri_suf)
    (w_g,) = _spread_wait(w_send, w_recv, w_srcs, w_lands, c_sb, blockwise=False, name="weights_wait")
    o_dil, lse_tot, lse4, lse16, mixed, x1 = _attn_out(o_b, lse_b, o_sb, x2, dil_out_norm_w, sb_out_norm_w, w_g)
    g, u, h2, dy, loss_parts = _ffn_fwd(x1, ffn_norm_w, tgt, w_g)
    loss = lax.psum(jnp.sum(loss_parts[::8, 0]), ("x", "y", "c"))

    dg, du, act, dh2 = _ffn_bwd_dx(dy, g, u, w_g)
    (dx1, do_dil, delta, do_sb, dwout, dn2, dndil, dnsb, do4, dl4, do16, dl16) = _attn_out_bwd(
        dy, dh2, x1, ffn_norm_w, w_g, mixed, o_dil, o_sb, dil_out_norm_w, sb_out_norm_w, bd512)
    dwg, dwu, dwd = _ffn_bwd_dw(h2, dy, dg, du, act)
    early = [dwg, dwu, dwd, dwout]
    own_slot_only = lambda t: jnp.where(slot_is_mine, t, jnp.zeros((), t.dtype))
    g_send, g_recv, g_srcs, g_lands, g_token = _spread_start(
        early, [own_slot_only(t) for t in early], blockwise=True, name="grads_start")
    tri_pre = tri_pre + g_token[0, 0].astype(BF16)
    dqs, dks, dvs = _sb_bwd(qs, ks, vs, do_sb, c_sb, tri_suf, tri_pre)
    cot_views = {1: (do_dil, lse_tot, delta), 4: (do4, lse4, dl4), 16: (do16, lse16, dl16)}
    dq_b, dk_b, dv_b = [], [], []
    for r in DILATIONS:
        dq, dk, dv = _dil_bwd(*qkv_views[r], *cot_views[r], r)
        dq_b.append(dq)
        dk_b.append(dk)
        dv_b.append(dv)
    dproj, dqn2, dkn2 = _qkv_bwd(dq_b, dk_b, dv_b, dqs, dks, dvs, qraw, kraw, cos2, sin2, qnw2, knw2, bd128)
    grad_x, dn1 = _in_bwd_dx(dproj, a_g, x2, dx1, attn_norm_w)
    dwin = _in_bwd_dw(h1, dproj)
    dqn = dqn2[:, :HEAD_DIM] + dqn2[:, HEAD_DIM:]
    dkn = dkn2[:, :HEAD_DIM] + dkn2[:, HEAD_DIM:]

    small = _pack_small(dn1, dn2, dndil, dnsb, dqn, dkn)
    r_in, r_small = _exchange_grads([dwin], small)
    r_gate, r_up, r_down, r_out = _spread_wait(g_send, g_recv, g_srcs, g_lands, r_small, blockwise=True,
                                               name="grads_wait")
    big = {
        "w_in": _adamw(r_in, w_in[0], m_w_in[0], v_w_in[0]),
        "w_gate": tuple(t[:, :FF_SHARD] for t in _adamw(r_gate, _pad_cols(w_gate[0]), _pad_cols(m_w_gate[0]), _pad_cols(v_w_gate[0]))),
        "w_up": tuple(t[:, :FF_SHARD] for t in _adamw(r_up, _pad_cols(w_up[0]), _pad_cols(m_w_up[0]), _pad_cols(v_w_up[0]))),
        "w_down": tuple(t[:FF_SHARD] for t in _adamw(r_down, _pad_rows(w_down[0]), _pad_rows(m_w_down[0]), _pad_rows(v_w_down[0]))),
        "w_out": _adamw(r_out, w_out[0], m_w_out[0], v_w_out[0]),
    }
    packs = [_pack_small(*ts) for ts in (
        (attn_norm_w, ffn_norm_w, dil_out_norm_w, sb_out_norm_w, q_norm_w, k_norm_w),
        (m_attn_norm_w, m_ffn_norm_w, m_dil_out_norm_w, m_sb_out_norm_w, m_q_norm_w, m_k_norm_w),
        (v_attn_norm_w, v_ffn_norm_w, v_dil_out_norm_w, v_sb_out_norm_w, v_q_norm_w, v_k_norm_w))]
    small_out = [_unpack_small(t) for t in _adamw(r_small, *packs)]
    names = ["attn_norm_w", "w_in", "q_norm_w", "k_norm_w", "dil_out_norm_w", "sb_out_norm_w", "w_out",
             "ffn_norm_w", "w_gate", "w_up", "w_down"]
    small_pos = {"attn_norm_w": 0, "ffn_norm_w": 1, "dil_out_norm_w": 2, "sb_out_norm_w": 3,
                 "q_norm_w": 4, "k_norm_w": 5}
    outs = [loss, grad_x[None]]
    for kind in range(4):
        for name in names:
            if name in small_pos:
                outs.append(small_out[kind][small_pos[name]])
            else:
                outs.append(big[name][kind][None])
    return tuple(outs)
```

```python
import functools

import jax
import jax.numpy as jnp
from jax import lax
from jax.experimental import pallas as pl
from jax.experimental.pallas import tpu as pltpu

F32 = jnp.float32
BF16 = jnp.bfloat16

N_DEV = 8
D_MODEL = 1024
HEAD_DIM = 64
D_GRP = 512
D_IN = 6 * D_GRP
IN_SHARD = D_IN // N_DEV
FF_SHARD = 352
FF_PAD = 384
FF_BLOCK = 2 * FF_PAD
FF_STEPS = N_DEV // 2
W_PACK_ROWS = 3 * FF_PAD + 128
W_OUT_BLOCK = 3 * FF_PAD // 128
OUT_SHARD = D_MODEL // N_DEV
BLOCK = 128
DILATIONS = (1, 4, 16)
ROPE_THETA = 10000.0
EPS = 1e-6
ATT_SCALE = HEAD_DIM ** -0.5
NEG = -1e30

ADAM_LR = 0.001
ADAM_B1 = 0.9
ADAM_B2 = 0.999
ADAM_EPS = 1e-08
ADAM_WD = 0.01
ADAM_STEP = 10

SB_TILE = 256
SB_PAIRS = 4
SB_BWD_PAIRS = 2
ROW_TILE = 512
VMEM_LIMIT = 56 * 1024 * 1024
MESH = pl.DeviceIdType.MESH


def _dot(a, b):
    return jnp.dot(a, b, preferred_element_type=F32)


def _dot_nt(a, b):
    return lax.dot_general(a, b, (((1,), (1,)), ((), ())), preferred_element_type=F32)


def _dot_tn(a, b):
    return lax.dot_general(a, b, (((0,), (0,)), ((), ())), preferred_element_type=F32)


def _mm_split(t, m):
    hi = t.astype(BF16)
    lo = (t - hi.astype(F32)).astype(BF16)
    return _dot(hi, m) + _dot(lo, m)


def _params(**kw):
    return pltpu.CompilerParams(vmem_limit_bytes=VMEM_LIMIT, **kw)


def _full(shape):
    nd = len(shape)
    return pl.BlockSpec(shape, lambda *_: (0,) * nd)


def _view_shape(s_len, r, dtype):
    return jax.ShapeDtypeStruct((s_len // r, r * D_GRP), dtype)


def _view_spec(tm, r):
    return pl.BlockSpec((tm // r, r * D_GRP), lambda i: (i, 0))


def _swap_halves(t):
    lane = lax.broadcasted_iota(jnp.int32, t.shape, 1)
    first = (lane & 32) == 0
    return jnp.where(first, pltpu.roll(t, 96, 1), pltpu.roll(t, 32, 1))


def _log_sigmoid(z):
    return jnp.minimum(z, 0.0) - jnp.log(1.0 + jnp.exp(-jnp.abs(z)))


def _log_sigmoid_pair(z):
    neg_abs = lax.bitcast_convert_type(lax.bitcast_convert_type(z, jnp.uint32) | jnp.uint32(0x80000000), F32)
    lb = jnp.minimum(z, 0.0) - jnp.log(1.0 + jnp.exp(neg_abs))
    return lb, lb - z


def _cumsum_mm(t, tri):
    return _dot(t.astype(BF16), tri)


def _split_views(src_ref, stage_ref, views4, views16):
    slabs, n, _ = src_ref.shape
    n4, n16 = n // 4, n // 16
    for j in range(slabs):
        g, lanes = j // 4, 128 * (j % 4)
        src, stage = src_ref.at[j], stage_ref.at[j]
        for c4 in range(4):
            blk = src[pl.ds(c4, n4, stride=4), :]
            stage[n4 * c4:n4 * (c4 + 1), :] = blk
            col = D_GRP * c4 + lanes
            views4[g][:, col:col + 128] = blk.astype(views4[g].dtype)
        for c4 in range(4):
            for c1 in range(4):
                blk = stage[pl.ds(n4 * c4 + c1, n16, stride=4), :]
                col = D_GRP * (4 * c1 + c4) + lanes
                views16[g][:, col:col + 128] = blk.astype(views16[g].dtype)


def _merge_views(views4, views16, stage_ref, dst4_ref, dst16_ref):
    slabs, n, _ = dst4_ref.shape
    n4, n16 = n // 4, n // 16
    for j in range(slabs):
        g, lanes = j // 4, 128 * (j % 4)
        dst4, dst16, stage = dst4_ref.at[j], dst16_ref.at[j], stage_ref.at[j]
        for c4 in range(4):
            col = D_GRP * c4 + lanes
            dst4[pl.ds(c4, n4, stride=4), :] = views4[g][:, col:col + 128].astype(F32)
            for c1 in range(4):
                col = D_GRP * (4 * c1 + c4) + lanes
                stage[pl.ds(n4 * c4 + c1, n16, stride=4), :] = views16[g][:, col:col + 128].astype(F32)
        for c4 in range(4):
            dst16[pl.ds(c4, n4, stride=4), :] = stage[n4 * c4:n4 * (c4 + 1), :]


def _slab_group(ref, g):
    return jnp.concatenate([ref[4 * g + p] for p in range(4)], axis=1)


def _mesh_pos():
    return lax.axis_index("x"), lax.axis_index("y"), lax.axis_index("c")


def _flat_index(p):
    return 4 * p[0] + 2 * p[1] + p[2]


def _gather_weights(shards):
    n_arr = len(shards)

    def body(*refs):
        srcs, outs = refs[:n_arr], refs[n_arr:2 * n_arr]
        send_sems, recv_sems, local_sems = refs[2 * n_arr:]
        x, y, c = _mesh_pos()
        me, sibling = (x, y, c), (x, y, 1 - c)
        chips = [(1 - x, y), (x, 1 - y), (1 - x, 1 - y)]

        def copy(arr, k, block, to, own=False):
            dst = outs[arr].at[_flat_index(block)]
            return pltpu.make_async_remote_copy(
                src_ref=srcs[arr] if own else dst, dst_ref=dst,
                send_sem=send_sems.at[arr, k], recv_sem=recv_sems.at[arr, k],
                device_id=to, device_id_type=MESH)

        for arr in range(n_arr):
            mine = pltpu.make_async_copy(srcs[arr], outs[arr].at[_flat_index(me)], local_sems.at[arr])
            mine.start()
            first = [copy(arr, 0, me, sibling, own=True)]
            first += [copy(arr, 1 + j, me, (*chip, c), own=True) for j, chip in enumerate(chips)]
            for cp in first:
                cp.start()
        for arr in range(n_arr):
            passed = [copy(arr, 4 + j, (*chip, c), sibling) for j, chip in enumerate(chips)]
            for j, chip in enumerate(chips):
                copy(arr, 1 + j, (*chip, c), me).wait_recv()
                passed[j].start()
        for arr in range(n_arr):
            copy(arr, 0, sibling, me).wait_recv()
            for j, chip in enumerate(chips):
                copy(arr, 4 + j, (*chip, 1 - c), me).wait_recv()
            for k in range(7):
                copy(arr, k, me, me).wait_send()
            pltpu.make_async_copy(srcs[arr], outs[arr].at[_flat_index(me)], local_sems.at[arr]).wait()

    any_spec = pl.BlockSpec(memory_space=pl.ANY)
    return pl.pallas_call(
        body, name="gather_weights",
        out_shape=tuple(jax.ShapeDtypeStruct((N_DEV,) + s.shape, s.dtype) for s in shards),
        in_specs=[any_spec] * n_arr, out_specs=(any_spec,) * n_arr,
        scratch_shapes=[pltpu.SemaphoreType.DMA((n_arr, 7)), pltpu.SemaphoreType.DMA((n_arr, 7)),
                        pltpu.SemaphoreType.DMA((n_arr,))],
        compiler_params=pltpu.CompilerParams(has_side_effects=True),
    )(*shards)


_HBM_SPEC = pl.BlockSpec(memory_space=pltpu.HBM)
_SEM_SPEC = pl.BlockSpec(memory_space=pltpu.SEMAPHORE)
_DATAFLOW = pltpu.SideEffectType.DATAFLOW_SIDE_EFFECTING


def _peer_list(x, y, c):
    return [(1 - x if m & 4 else x, 1 - y if m & 2 else y, 1 - c if m & 1 else c) for m in range(1, N_DEV)]


def _spread_copies(src_refs, land_refs, send_sems, recv_sems, blockwise):
    x, y, c = _mesh_pos()
    my_idx = _flat_index((x, y, c))
    copies = []
    for a, (src, land) in enumerate(zip(src_refs, land_refs)):
        for k, peer in enumerate(_peer_list(x, y, c)):
            copies.append(pltpu.make_async_remote_copy(
                src_ref=src.at[_flat_index(peer)] if blockwise else src, dst_ref=land.at[my_idx],
                send_sem=send_sems.at[(N_DEV - 1) * a + k], recv_sem=recv_sems.at[(N_DEV - 1) * a + k],
                device_id=peer, device_id_type=MESH))
    return copies


def _spread_start(srcs, lands, blockwise, name):
    n = len(srcs)

    def body(*refs):
        for cp in _spread_copies(refs[:n], refs[n:2 * n], refs[2 * n], refs[2 * n + 1], blockwise):
            cp.start()
        token = refs[-1]
        token[...] = jnp.zeros_like(token)

    hbm = lambda t: pltpu.HBM(t.shape, t.dtype)
    sems = pltpu.SemaphoreType.DMA((n * (N_DEV - 1),))
    outs = pl.pallas_call(
        body, name=name,
        out_shape=(sems, sems) + tuple(hbm(t) for t in srcs) + tuple(hbm(t) for t in lands)
        + (jax.ShapeDtypeStruct((8, 128), F32),),
        in_specs=[_HBM_SPEC] * (2 * n),
        out_specs=(_SEM_SPEC, _SEM_SPEC) + (_HBM_SPEC,) * (2 * n) + (pl.BlockSpec(memory_space=pltpu.VMEM),),
        input_output_aliases={i: 2 + i for i in range(2 * n)},
        compiler_params=pltpu.CompilerParams(has_side_effects=_DATAFLOW),
    )(*[pltpu.with_memory_space_constraint(t, pltpu.HBM) for t in list(srcs) + list(lands)])
    return outs[0], outs[1], outs[2:2 + n], outs[2 + n:2 + 2 * n], outs[-1]


def _spread_wait(send_sems, recv_sems, srcs, lands, after, blockwise, name):
    n = len(srcs)

    def body(*refs):
        for cp in _spread_copies(refs[:n], refs[n:2 * n], refs[2 * n], refs[2 * n + 1], blockwise):
            cp.wait_send()
            cp.wait_recv()

    hbm = lambda t: pltpu.HBM(t.shape, t.dtype)
    outs = pl.pallas_call(
        body, name=name,
        out_shape=tuple(hbm(t) for t in srcs) + tuple(hbm(t) for t in lands),
        in_specs=[_HBM_SPEC] * (2 * n) + [_SEM_SPEC, _SEM_SPEC, pl.BlockSpec(memory_space=pl.ANY)],
        out_specs=(_HBM_SPEC,) * (2 * n),
        input_output_aliases={i: i for i in range(2 * n)},
        compiler_params=pltpu.CompilerParams(has_side_effects=_DATAFLOW),
    )(*srcs, *lands, send_sems, recv_sems, after)
    return outs[n:]


def _exchange_grads(parts, small):
    n_arr = len(parts)

    def body(*refs):
        ins, outs = refs[:n_arr + 1], refs[n_arr + 1:2 * (n_arr + 1)]
        send_sems, recv_sems, local_sems = refs[2 * (n_arr + 1):]
        x, y, c = _mesh_pos()
        me = (x, y, c)
        my_idx = _flat_index(me)
        peers = []
        for m in range(1, N_DEV):
            peers.append((1 - x if m & 4 else x, 1 - y if m & 2 else y, 1 - c if m & 1 else c))

        def src_block(arr, dev):
            return ins[arr] if arr == n_arr else ins[arr].at[_flat_index(dev)]

        def copy(arr, k):
            return pltpu.make_async_remote_copy(
                src_ref=src_block(arr, peers[k]), dst_ref=outs[arr].at[my_idx],
                send_sem=send_sems.at[arr, k], recv_sem=recv_sems.at[arr, k],
                device_id=peers[k], device_id_type=MESH)

        def local(arr):
            return pltpu.make_async_copy(src_block(arr, me), outs[arr].at[my_idx], local_sems.at[arr])

        for arr in range(n_arr + 1):
            local(arr).start()
            for k in range(N_DEV - 1):
                copy(arr, k).start()
        for arr in range(n_arr + 1):
            for k in range(N_DEV - 1):
                cp = copy(arr, k)
                cp.wait_send()
                cp.wait_recv()
            local(arr).wait()

    any_spec = pl.BlockSpec(memory_space=pl.ANY)
    out_shape = tuple(jax.ShapeDtypeStruct(p.shape, p.dtype) for p in parts)
    out_shape += (jax.ShapeDtypeStruct((N_DEV,) + small.shape, small.dtype),)
    return pl.pallas_call(
        body, name="exchange_grads",
        out_shape=out_shape,
        in_specs=[any_spec] * (n_arr + 1), out_specs=(any_spec,) * (n_arr + 1),
        scratch_shapes=[pltpu.SemaphoreType.DMA((n_arr + 1, N_DEV - 1)),
                        pltpu.SemaphoreType.DMA((n_arr + 1, N_DEV - 1)),
                        pltpu.SemaphoreType.DMA((n_arr + 1,))],
        compiler_params=pltpu.CompilerParams(has_side_effects=True),
    )(*parts, small)


def _head_norm(t, w128, bd):
    ms = _mm_split(t * t, bd) * (1.0 / HEAD_DIM)
    r = lax.rsqrt(ms + EPS)
    return (t * r) * w128, r


def _attn_in(x2, wn1, a_g, cos2, sin2, qnw, knw, bd):
    s_len = x2.shape[0]
    tm = ROW_TILE

    def body(x_ref, wn_ref, w_ref, cos_ref, sin_ref, qnw_ref, knw_ref, bd_ref,
             h1_ref, qraw_ref, kraw_ref, q_ref, k_ref, va_ref, qs_ref, ks_ref, vs_ref,
             q4_ref, k4_ref, v4_ref, q16_ref, k16_ref, v16_ref, proj, slabs, stage, w_full):
        @pl.when(pl.program_id(0) == 0)
        def _():
            for d in range(N_DEV):
                w_full[:, IN_SHARD * d:IN_SHARD * (d + 1)] = w_ref[d]

        xx = x_ref[...]
        r = lax.rsqrt(jnp.mean(xx * xx, axis=-1, keepdims=True) + EPS)
        h = ((xx * r) * wn_ref[...]).astype(BF16)
        h1_ref[...] = h
        proj[...] = _dot(h, w_full[...])
        cos_t, sin_t, bdm = cos_ref[...], sin_ref[...], bd_ref[...]
        for grp, (raw_ref, rope_ref, nw_ref) in enumerate(((qraw_ref, q_ref, qnw_ref),
                                                           (kraw_ref, k_ref, knw_ref))):
            for p in range(4):
                cols = slice(D_GRP * grp + 128 * p, D_GRP * grp + 128 * (p + 1))
                t = proj[:, cols]
                raw_ref[:, 128 * p:128 * (p + 1)] = t
                yn, _ = _head_norm(t, nw_ref[...], bdm)
                roped = yn * cos_t + _swap_halves(yn) * sin_t
                slabs[4 * grp + p] = roped
                rope_ref[:, 128 * p:128 * (p + 1)] = roped.astype(BF16)
        for p in range(4):
            slabs[8 + p] = proj[:, 2 * D_GRP + 128 * p:2 * D_GRP + 128 * (p + 1)]
        for grp, ref in ((2, va_ref), (3, qs_ref), (4, ks_ref), (5, vs_ref)):
            ref[...] = proj[:, D_GRP * grp:D_GRP * (grp + 1)].astype(BF16)
        _split_views(slabs, stage, (q4_ref, k4_ref, v4_ref), (q16_ref, k16_ref, v16_ref))

    row = lambda w: pl.BlockSpec((tm, w), lambda i: (i, 0))
    grp_bf = jax.ShapeDtypeStruct((s_len, D_GRP), BF16)
    grp_f32 = jax.ShapeDtypeStruct((s_len, D_GRP), F32)
    return pl.pallas_call(
        body, name="attn_in", grid=(s_len // tm,),
        in_specs=[row(D_MODEL), _full((1, D_MODEL)),
                  pl.BlockSpec((N_DEV, D_MODEL, IN_SHARD), lambda i: (0, 0, 0)),
                  row(128), row(128), _full((1, 128)), _full((1, 128)), _full((128, 128))],
        out_specs=(row(D_MODEL),) + (row(D_GRP),) * 8 + (_view_spec(tm, 4),) * 3 + (_view_spec(tm, 16),) * 3,
        out_shape=(jax.ShapeDtypeStruct((s_len, D_MODEL), BF16), grp_f32, grp_f32) + (grp_bf,) * 6
        + (_view_shape(s_len, 4, BF16),) * 3 + (_view_shape(s_len, 16, BF16),) * 3,
        scratch_shapes=[pltpu.VMEM((tm, D_IN), F32), pltpu.VMEM((12, tm, 128), F32), pltpu.VMEM((12, tm, 128), F32),
                        pltpu.VMEM((D_MODEL, D_IN), BF16)],
        compiler_params=_params(),
    )(x2, wn1, a_g, cos2, sin2, qnw, knw, bd)


def _band_mask(n):
    i = lax.broadcasted_iota(jnp.int32, (2 * BLOCK, 2 * BLOCK), 0) & (BLOCK - 1)
    j = lax.broadcasted_iota(jnp.int32, (2 * BLOCK, 2 * BLOCK), 1)
    dist = i + BLOCK - j
    return (dist >= 0) & (dist <= BLOCK) & ((n - 1) * BLOCK + j >= 0)


def _stack_heads(t2, head0):
    return jnp.concatenate([jnp.where(head0, t2, 0), jnp.where(head0, 0, t2)], axis=0)


def _unstack_heads(t, head0):
    return jnp.where(head0, t[0:BLOCK], t[BLOCK:2 * BLOCK])


def _dil_fwd(qv, kv, vv, r):
    sub_len = qv.shape[0]
    nb = sub_len // BLOCK

    qb = 2 if nb % 2 == 0 else 1

    def body(q_ref, kp_ref, kc_ref, vp_ref, vc_ref, o_ref, lse_ref):
        n = pl.program_id(1)
        lane = lax.broadcasted_iota(jnp.int32, (BLOCK, 128), 1)
        head0 = lane < HEAD_DIM
        units = [(b, slice(128 * p, 128 * (p + 1))) for b in range(qb) for p in range(4)]
        valid = [_band_mask(qb * n + b) for b in range(qb)]
        rows = [slice(BLOCK * b, BLOCK * (b + 1)) for b in range(qb)]

        def keys(prev_ref, cur_ref, b, c):
            before = prev_ref[:, c] if b == 0 else cur_ref[rows[b - 1], c]
            return jnp.concatenate([before, cur_ref[rows[b], c]], axis=0)

        qqs = [_stack_heads(q_ref[rows[b], c] * ATT_SCALE, head0) for b, c in units]
        kks = [keys(kp_ref, kc_ref, b, c) for b, c in units]
        vvs = [keys(vp_ref, vc_ref, b, c) for b, c in units]
        ss = [_dot_nt(qq, kk) for qq, kk in zip(qqs, kks)]
        prs, dens, lses = [], [], []
        for (b, _), s in zip(units, ss):
            s = jnp.where(valid[b], s, NEG)
            m = jnp.max(s, axis=-1, keepdims=True)
            pr = jnp.exp(s - m)
            den = jnp.sum(pr, axis=-1, keepdims=True)
            prs.append(pr.astype(BF16))
            dens.append(den)
            lses.append(m + jnp.log(den))
        pvs = [_dot(pr, vv2) for pr, vv2 in zip(prs, vvs)]
        for (b, c), pv, den, lse in zip(units, pvs, dens, lses):
            o_ref[rows[b], c] = _unstack_heads(pv / den, head0)
            lse_ref[rows[b], c] = _unstack_heads(jnp.broadcast_to(lse, (2 * BLOCK, 128)), head0)

    cur = pl.BlockSpec((qb * BLOCK, D_GRP), lambda c, n: (n, c))
    prev = pl.BlockSpec((BLOCK, D_GRP), lambda c, n: (jnp.maximum(qb * n - 1, 0), c))
    out = jax.ShapeDtypeStruct(qv.shape, F32)
    return pl.pallas_call(
        body, name=f"dil_fwd_r{r}", grid=(r, nb // qb),
        in_specs=[cur, prev, cur, prev, cur], out_specs=(cur, cur), out_shape=(out, out),
        compiler_params=_params(),
    )(qv, kv, kv, vv, vv)


def _dil_bwd(qv, kv, vv, dov, lsev, deltav, r):
    sub_len = qv.shape[0]
    nb = sub_len // BLOCK

    def body(q_ref, kp_ref, kc_ref, vp_ref, vc_ref, do_ref, lse_ref, dl_ref,
             dq_ref, dk_ref, dv_ref, dk_carry, dv_carry):
        n = pl.program_id(1)

        @pl.when(n == 0)
        def _():
            dk_carry[...] = jnp.zeros_like(dk_carry)
            dv_carry[...] = jnp.zeros_like(dv_carry)

        @pl.when(n < nb)
        def _():
            valid = _band_mask(n)
            lane = lax.broadcasted_iota(jnp.int32, (BLOCK, 128), 1)
            head0 = lane < HEAD_DIM
            pairs = [slice(128 * p, 128 * (p + 1)) for p in range(4)]
            qqs = [_stack_heads(q_ref[:, c] * ATT_SCALE, head0) for c in pairs]
            dos = [_stack_heads(do_ref[:, c], head0) for c in pairs]
            kks = [jnp.concatenate([kp_ref[:, c], kc_ref[:, c]], axis=0) for c in pairs]
            vvs = [jnp.concatenate([vp_ref[:, c], vc_ref[:, c]], axis=0) for c in pairs]
            ss = [_dot_nt(qq, kk) for qq, kk in zip(qqs, kks)]
            dps = [_dot_nt(do, vv2) for do, vv2 in zip(dos, vvs)]
            prs, dss = [], []
            for c, s, dp in zip(pairs, ss, dps):
                stats = []
                for ref in (lse_ref, dl_ref):
                    t2 = ref[:, c]
                    stats.append(jnp.concatenate(
                        [jnp.sum(jnp.where(lane == 0, t2, 0.0), axis=-1, keepdims=True),
                         jnp.sum(jnp.where(lane == HEAD_DIM, t2, 0.0), axis=-1, keepdims=True)], axis=0))
                pr = jnp.where(valid, jnp.exp(jnp.minimum(s - stats[0], 0.0)), 0.0)
                prs.append(pr.astype(BF16))
                dss.append((pr * (dp - stats[1])).astype(BF16))
            dqs = [_dot(ds, kk) for ds, kk in zip(dss, kks)]
            dkks = [_dot_tn(ds, qq) for ds, qq in zip(dss, qqs)]
            dvvs = [_dot_tn(pr, do) for pr, do in zip(prs, dos)]
            for c, dq, dkk, dvv in zip(pairs, dqs, dkks, dvvs):
                dq_ref[:, c] = _unstack_heads(dq, head0) * ATT_SCALE
                dk_ref[:, c] = dk_carry[:, c] + dkk[:BLOCK]
                dv_ref[:, c] = dv_carry[:, c] + dvv[:BLOCK]
                dk_carry[:, c] = dkk[BLOCK:]
                dv_carry[:, c] = dvv[BLOCK:]

        @pl.when(n == nb)
        def _():
            dk_ref[...] = dk_carry[...]
            dv_ref[...] = dv_carry[...]

    last = nb - 1
    cur = pl.BlockSpec((BLOCK, D_GRP), lambda c, n: (jnp.minimum(n, last), c))
    prev = pl.BlockSpec((BLOCK, D_GRP), lambda c, n: (jnp.clip(n - 1, 0, last), c))
    out = jax.ShapeDtypeStruct(qv.shape, F32)
    return pl.pallas_call(
        body, name=f"dil_bwd_r{r}", grid=(r, nb + 1),
        in_specs=[cur, prev, cur, prev, cur, cur, cur, cur],
        out_specs=(cur, prev, prev), out_shape=(out, out, out),
        scratch_shapes=[pltpu.VMEM((BLOCK, D_GRP), F32), pltpu.VMEM((BLOCK, D_GRP), F32)],
        compiler_params=_params(),
    )(qv, kv, kv, vv, vv, dov, lsev, deltav)


def _sb_fwd(qs, ks, vs, tri_suf):
    s_len = qs.shape[0]
    t = SB_TILE
    nq = s_len // t

    npair = SB_PAIRS

    def body(q_ref, k_ref, v_ref, u_ref, o_ref, c_ref, qq, vt, acc, cf, csave):
        row = lax.broadcasted_iota(jnp.int32, (2 * t, t), 0) & (t - 1)
        col = lax.broadcasted_iota(jnp.int32, (2 * t, t), 1)
        diag_mask = col < row
        lane1 = lax.broadcasted_iota(jnp.int32, (t, 128), 1)
        head0 = lane1 < HEAD_DIM
        lane2 = lax.broadcasted_iota(jnp.int32, (2 * t, 128), 1)
        uu = u_ref[...]
        pr = range(npair)
        cols = [slice(128 * pp, 128 * (pp + 1)) for pp in pr]

        i = pl.program_id(1)

        @pl.when(i == 0)
        def _():
            def transpose_v(j, _):
                rows = pl.ds(pl.multiple_of(j * t, t), t)
                for pp in pr:
                    vt[pp, j] = v_ref[rows, cols[pp]].astype(F32).T.astype(BF16)
                return 0

            lax.fori_loop(0, nq, transpose_v, 0)

        for pp in pr:
            q2 = q_ref[:, cols[pp]] * ATT_SCALE
            qq[pp, 0:t, :] = jnp.where(head0, q2, 0)
            qq[pp, t:2 * t, :] = jnp.where(head0, 0, q2)
        acc[...] = jnp.zeros_like(acc)
        cf[...] = jnp.zeros_like(cf)
        csave[...] = jnp.zeros_like(csave)

        def tile(kb, diag):
            krows = pl.ds(pl.multiple_of(kb * t, t), t)
            zs = [_dot_nt(qq[pp], k_ref[krows, cols[pp]]) for pp in pr]
            lbk = [_log_sigmoid_pair(z) for z in zs]
            lks = [jnp.where(diag_mask, lk, 0.0) if diag else lk for _, lk in lbk]
            sufs = [_cumsum_mm(lk, uu) for lk in lks]
            carries = [cf[pp] for pp in pr]
            avs = []
            for pp in pr:
                a = jnp.exp(lbk[pp][0] + (sufs[pp] + jnp.concatenate([carries[pp]] * (t // 128), axis=1)))
                avs.append((jnp.where(diag_mask, a, 0.0) if diag else a).astype(BF16))
            pvs = [_dot_nt(vt[pp, kb], avs[pp]) for pp in pr]
            for pp in pr:
                acc[pp] += pvs[pp]
                csave[pp] = jnp.where(lane2 == kb, carries[pp], csave[pp])
                cf[pp] = carries[pp] + jnp.broadcast_to(jnp.sum(lks[pp], axis=-1, keepdims=True), (2 * t, 128))

        tile(i, True)

        def k_block(step, _):
            tile(i - 1 - step, False)
            return 0

        lax.fori_loop(0, i, k_block, 0)
        for pp in pr:
            o_ref[:, cols[pp]] = jnp.where(head0, acc[pp, :, 0:t].T, acc[pp, :, t:2 * t].T)
            c_ref[2 * pp] = csave[pp, 0:t, :]
            c_ref[2 * pp + 1] = csave[pp, t:2 * t, :]

    width = 128 * npair
    kv = pl.BlockSpec((s_len, width), lambda p, i: (0, p))
    qo = pl.BlockSpec((t, width), lambda p, i: (i, p))
    return pl.pallas_call(
        body, name="sb_fwd", grid=(4 // npair, nq),
        in_specs=[qo, kv, kv, pl.BlockSpec((t, t), lambda p, i: (0, 0))],
        out_specs=(qo, pl.BlockSpec((2 * npair, t, 128), lambda p, i: (p, i, 0))),
        out_shape=(jax.ShapeDtypeStruct((s_len, D_GRP), F32),
                   jax.ShapeDtypeStruct((8, s_len, 128), F32)),
        scratch_shapes=[pltpu.VMEM((npair, 2 * t, 128), BF16), pltpu.VMEM((npair, nq, 128, t), BF16),
                        pltpu.VMEM((npair, 128, 2 * t), F32),
                        pltpu.VMEM((npair, 2 * t, 128), F32), pltpu.VMEM((npair, 2 * t, 128), F32)],
        compiler_params=_params(),
    )(qs, ks, vs, tri_suf)


def _sb_bwd(qs, ks, vs, dos, csaved, tri_suf, tri_pre):
    s_len = qs.shape[0]
    t = SB_TILE
    nq = s_len // t

    npair = SB_BWD_PAIRS

    def body(q_ref, k_ref, v_ref, do_ref, c_ref, u_ref, p_ref, dq_ref, dk_ref, dv_ref,
             qq, dd, qqt, ddt, kt, dq_acc, dkt, dvt, cg):
        row = lax.broadcasted_iota(jnp.int32, (2 * t, t), 0) & (t - 1)
        col = lax.broadcasted_iota(jnp.int32, (2 * t, t), 1)
        diag_mask = col < row
        lane1 = lax.broadcasted_iota(jnp.int32, (t, 128), 1)
        head0 = lane1 < HEAD_DIM
        lane2 = lax.broadcasted_iota(jnp.int32, (2 * t, 128), 1)
        uu, pm = u_ref[...], p_ref[...]
        pr = range(npair)
        cols = [slice(128 * pp, 128 * (pp + 1)) for pp in pr]
        i = pl.program_id(1)

        @pl.when(i == 0)
        def _():
            dkt[...] = jnp.zeros_like(dkt)
            dvt[...] = jnp.zeros_like(dvt)

            def transpose_k(j, _):
                rows = pl.ds(pl.multiple_of(j * t, t), t)
                for pp in pr:
                    kt[pp, j] = k_ref[rows, cols[pp]].astype(F32).T.astype(BF16)
                return 0

            lax.fori_loop(0, nq, transpose_k, 0)

        for pp in pr:
            q2 = q_ref[:, cols[pp]].astype(F32) * ATT_SCALE
            do2 = do_ref[:, cols[pp]].astype(F32)
            for src, nat, tr in ((q2, qq, qqt), (do2, dd, ddt)):
                stacked = jnp.concatenate([jnp.where(head0, src, 0.0), jnp.where(head0, 0.0, src)], axis=0)
                nat[pp] = stacked.astype(BF16)
                tr[pp] = stacked.T.astype(BF16)
        dq_acc[...] = jnp.zeros_like(dq_acc)
        cg[...] = jnp.zeros_like(cg)

        def tile(kb, diag):
            krows = pl.ds(pl.multiple_of(kb * t, t), t)
            zs = [_dot_nt(qq[pp], k_ref[krows, cols[pp]]) for pp in pr]
            das = [_dot_nt(dd[pp], v_ref[krows, cols[pp]]) for pp in pr]
            lbk = [_log_sigmoid_pair(z) for z in zs]
            lks = [jnp.where(diag_mask, lk, 0.0) if diag else lk for _, lk in lbk]
            sufs = [_cumsum_mm(lk, uu) for lk in lks]
            avs, gs = [], []
            for pp in pr:
                cs = jnp.concatenate([c_ref[2 * pp], c_ref[2 * pp + 1]], axis=0)
                cf = jnp.sum(jnp.where(lane2 == kb, cs, 0.0), axis=-1, keepdims=True)
                a = jnp.exp(lbk[pp][0] + (sufs[pp] + cf))
                a = jnp.where(diag_mask, a, 0.0) if diag else a
                avs.append(a.astype(BF16))
                gs.append(a * das[pp])
            gpres = [_cumsum_mm(g, pm) for g in gs]
            dzs = []
            for pp in pr:
                carry = cg[pp]
                beta = jnp.exp(lbk[pp][0])
                dz = gs[pp] - beta * (gs[pp] + (gpres[pp] + jnp.concatenate([carry] * (t // 128), axis=1)))
                dzs.append((jnp.where(diag_mask, dz, 0.0) if diag else dz).astype(BF16))
                cg[pp] = carry + jnp.broadcast_to(jnp.sum(gs[pp], axis=-1, keepdims=True), (2 * t, 128))
            dqs = [_dot_nt(kt[pp, kb], dzs[pp]) for pp in pr]
            dks = [_dot(qqt[pp], dzs[pp]) for pp in pr]
            dvs = [_dot(ddt[pp], avs[pp]) for pp in pr]
            for pp in pr:
                dq_acc[pp] += dqs[pp]
                dkt[pp, kb] += dks[pp]
                dvt[pp, kb] += dvs[pp]

        def k_block(kb, _):
            tile(kb, False)
            return 0

        lax.fori_loop(0, i, k_block, 0)
        tile(i, True)
        for pp in pr:
            dq_ref[:, cols[pp]] = jnp.where(head0, dq_acc[pp, :, 0:t].T, dq_acc[pp, :, t:2 * t].T) * ATT_SCALE

        @pl.when(i == nq - 1)
        def _():
            def untranspose(j, _):
                rows = pl.ds(pl.multiple_of(j * t, t), t)
                for pp in pr:
                    dk_ref[rows, cols[pp]] = dkt[pp, j].T
                    dv_ref[rows, cols[pp]] = dvt[pp, j].T
                return 0

            lax.fori_loop(0, nq, untranspose, 0)

    width = 128 * npair
    kv = pl.BlockSpec((s_len, width), lambda p, i: (0, p))
    qo = pl.BlockSpec((t, width), lambda p, i: (i, p))
    tri = pl.BlockSpec((t, t), lambda p, i: (0, 0))
    out = jax.ShapeDtypeStruct((s_len, D_GRP), F32)
    return pl.pallas_call(
        body, name="sb_bwd", grid=(4 // npair, nq),
        in_specs=[qo, kv, kv, qo, pl.BlockSpec((2 * npair, t, 128), lambda p, i: (p, i, 0)), tri, tri],
        out_specs=(qo, kv, kv), out_shape=(out, out, out),
        scratch_shapes=[pltpu.VMEM((npair, 2 * t, 128), BF16), pltpu.VMEM((npair, 2 * t, 128), BF16),
                        pltpu.VMEM((npair, 128, 2 * t), BF16), pltpu.VMEM((npair, 128, 2 * t), BF16),
                        pltpu.VMEM((npair, nq, 128, t), BF16),
                        pltpu.VMEM((npair, 128, 2 * t), F32),
                        pltpu.VMEM((npair, nq, 128, t), F32), pltpu.VMEM((npair, nq, 128, t), F32),
                        pltpu.VMEM((npair, 2 * t, 128), F32)],
        compiler_params=_params(),
    )(qs, ks, vs, dos, csaved, tri_suf, tri_pre)


def _attn_out(o_b, lse_b, o_sb, x2, wdil, wsb, b_g):
    s_len = x2.shape[0]
    tm = ROW_TILE

    def body(o1_ref, l1_ref, o4_ref, l4_ref, o16_ref, l16_ref, osb_ref, x_ref, wdil_ref, wsb_ref, w_ref,
             odil_ref, lse_ref, lse4_ref, lse16_ref, mixed_ref, x1_ref, stage, nat4, nat16):
        _merge_views((o4_ref, l4_ref), (o16_ref, l16_ref), stage, nat4, nat16)
        os_ = (o1_ref[...], _slab_group(nat4, 0), _slab_group(nat16, 0))
        ls = (l1_ref[...], _slab_group(nat4, 1), _slab_group(nat16, 1))
        mx = jnp.maximum(jnp.maximum(ls[0], ls[1]), ls[2])
        es = [jnp.exp(l - mx) for l in ls]
        den = es[0] + es[1] + es[2]
        o_dil = (es[0] * os_[0] + es[1] * os_[1] + es[2] * os_[2]) / den
        odil_ref[...] = o_dil
        lse = mx + jnp.log(den)
        lse_ref[...] = lse
        for p in range(4):
            nat4[p] = lse[:, 128 * p:128 * (p + 1)]
        _split_views(nat4.at[0:4], stage.at[0:4], (lse4_ref,), (lse16_ref,))
        halves = []
        for t, w_r in ((o_dil, wdil_ref), (osb_ref[...], wsb_ref)):
            r = lax.rsqrt(jnp.mean(t * t, axis=-1, keepdims=True) + EPS)
            halves.append(((t * r) * w_r[...]).astype(BF16))
        mixed = jnp.concatenate(halves, axis=1)
        mixed_ref[...] = mixed
        w = w_ref[...].reshape(D_MODEL, D_MODEL)
        x1_ref[...] = x_ref[...] + _dot(mixed, w)

    row = lambda w: pl.BlockSpec((tm, w), lambda i: (i, 0))
    return pl.pallas_call(
        body, name="attn_out", grid=(s_len // tm,),
        in_specs=[row(D_GRP)] * 2 + [_view_spec(tm, 4)] * 2 + [_view_spec(tm, 16)] * 2
        + [row(D_GRP), row(D_MODEL), _full((1, D_GRP)), _full((1, D_GRP)),
           pl.BlockSpec((N_DEV, OUT_SHARD, D_MODEL), lambda i: (0, W_OUT_BLOCK, 0))],
        out_specs=(row(D_GRP), row(D_GRP), _view_spec(tm, 4), _view_spec(tm, 16), row(D_MODEL), row(D_MODEL)),
        out_shape=(jax.ShapeDtypeStruct((s_len, D_GRP), F32), jax.ShapeDtypeStruct((s_len, D_GRP), F32),
                   _view_shape(s_len, 4, F32), _view_shape(s_len, 16, F32),
                   jax.ShapeDtypeStruct((s_len, D_MODEL), BF16), jax.ShapeDtypeStruct((s_len, D_MODEL), F32)),
        scratch_shapes=[pltpu.VMEM((8, tm, 128), F32)] * 3,
        compiler_params=_params(),
    )(o_b[0], lse_b[0], o_b[1], lse_b[1], o_b[2], lse_b[2], o_sb, x2, wdil, wsb, b_g)


def _two_shards(w_ref):
    return w_ref[...].reshape(FF_BLOCK, D_MODEL)


def _ffn_fwd(x1, wn2, tgt, w_g):
    s_len = x1.shape[0]
    tm = ROW_TILE
    ni = s_len // tm

    def body(x_ref, wn_ref, t_ref, wg_ref, wu_ref, wd_ref, g_ref, u_ref, h2_ref, dy_ref, loss_ref, acc):
        j = pl.program_id(1)

        @pl.when(j == 0)
        def _():
            xx = x_ref[...]
            r = lax.rsqrt(jnp.mean(xx * xx, axis=-1, keepdims=True) + EPS)
            h2_ref[...] = ((xx * r) * wn_ref[...]).astype(BF16)
            acc[...] = jnp.zeros_like(acc)

        h = h2_ref[...]
        g = _dot_nt(h, _two_shards(wg_ref))
        u = _dot_nt(h, _two_shards(wu_ref))
        g_ref[...] = g
        u_ref[...] = u
        act = (g * (1.0 / (1.0 + jnp.exp(-g)))) * u
        acc[...] += _dot(act.astype(BF16), _two_shards(wd_ref))

        @pl.when(j == FF_STEPS - 1)
        def _():
            err = (x_ref[...] + acc[...]) - t_ref[...]
            dy_ref[...] = err * (1.0 / D_MODEL)
            part = 0.5 * jnp.sum(jnp.mean(err * err, axis=-1, keepdims=True))
            loss_ref[...] = jnp.full((8, 128), part, F32)

    row = pl.BlockSpec((tm, D_MODEL), lambda i, j: (i, 0))
    hid = pl.BlockSpec((tm, FF_BLOCK), lambda i, j: (i, j))
    return pl.pallas_call(
        body, name="ffn_fwd", grid=(ni, FF_STEPS),
        in_specs=[row, pl.BlockSpec((1, D_MODEL), lambda i, j: (0, 0)), row,
                  pl.BlockSpec((2, FF_PAD, D_MODEL), lambda i, j: (j, 0, 0)),
                  pl.BlockSpec((2, FF_PAD, D_MODEL), lambda i, j: (j, 1, 0)),
                  pl.BlockSpec((2, FF_PAD, D_MODEL), lambda i, j: (j, 2, 0))],
        out_specs=(hid, hid, row, row, pl.BlockSpec((8, 128), lambda i, j: (i, 0))),
        out_shape=(jax.ShapeDtypeStruct((s_len, N_DEV * FF_PAD), F32),
                   jax.ShapeDtypeStruct((s_len, N_DEV * FF_PAD), F32),
                   jax.ShapeDtypeStruct((s_len, D_MODEL), BF16),
                   jax.ShapeDtypeStruct((s_len, D_MODEL), F32),
                   jax.ShapeDtypeStruct((ni * 8, 128), F32)),
        scratch_shapes=[pltpu.VMEM((tm, D_MODEL), F32)],
        compiler_params=_params(),
    )(x1, wn2, tgt, w_g, w_g, w_g)


def _ffn_bwd_dx(dy, g, u, w_g):
    s_len = dy.shape[0]
    tm = ROW_TILE

    def body(dy_ref, g_ref, u_ref, wg_ref, wu_ref, wd_ref, dg_ref, du_ref, act_ref, dh_ref, acc):
        j = pl.program_id(1)

        @pl.when(j == 0)
        def _():
            acc[...] = jnp.zeros_like(acc)

        gg, uu = g_ref[...], u_ref[...]
        da = _dot_nt(dy_ref[...].astype(BF16), _two_shards(wd_ref))
        sig = 1.0 / (1.0 + jnp.exp(-gg))
        silu = gg * sig
        act_ref[...] = (silu * uu).astype(BF16)
        du = (da * silu).astype(BF16)
        dg = (da * uu * (sig * (1.0 + gg * (1.0 - sig)))).astype(BF16)
        du_ref[...] = du
        dg_ref[...] = dg
        acc[...] += _dot(dg, _two_shards(wg_ref)) + _dot(du, _two_shards(wu_ref))

        @pl.when(j == FF_STEPS - 1)
        def _():
            dh_ref[...] = acc[...]

    row = pl.BlockSpec((tm, D_MODEL), lambda i, j: (i, 0))
    hid = pl.BlockSpec((tm, FF_BLOCK), lambda i, j: (i, j))
    hid_bf = jax.ShapeDtypeStruct((s_len, N_DEV * FF_PAD), BF16)
    return pl.pallas_call(
        body, name="ffn_bwd_dx", grid=(s_len // tm, FF_STEPS),
        in_specs=[row, hid, hid,
                  pl.BlockSpec((2, FF_PAD, D_MODEL), lambda i, j: (j, 0, 0)),
                  pl.BlockSpec((2, FF_PAD, D_MODEL), lambda i, j: (j, 1, 0)),
                  pl.BlockSpec((2, FF_PAD, D_MODEL), lambda i, j: (j, 2, 0))],
        out_specs=(hid, hid, hid, row),
        out_shape=(hid_bf, hid_bf, hid_bf, jax.ShapeDtypeStruct((s_len, D_MODEL), F32)),
        scratch_shapes=[pltpu.VMEM((tm, D_MODEL), F32)],
        compiler_params=_params(),
    )(dy, g, u, w_g, w_g, w_g)


def _ffn_bwd_dw(h2, dy, dg, du, act):
    s_len = h2.shape[0]
    tm = ROW_TILE
    ni = s_len // tm

    def body(h_ref, dy_ref, dg_ref, du_ref, act_ref, dwg_ref, dwu_ref, dwd_ref, ag, au, ad):
        i = pl.program_id(1)

        @pl.when(i == 0)
        def _():
            ag[...] = jnp.zeros_like(ag)
            au[...] = jnp.zeros_like(au)
            ad[...] = jnp.zeros_like(ad)

        h = h_ref[...]
        ag[...] += _dot_tn(h, dg_ref[...])
        au[...] += _dot_tn(h, du_ref[...])
        ad[...] += _dot_tn(act_ref[...], dy_ref[...].astype(BF16))

        @pl.when(i == ni - 1)
        def _():
            for half in range(2):
                cols = slice(FF_PAD * half, FF_PAD * (half + 1))
                dwg_ref[half] = ag[:, cols].astype(BF16)
                dwu_ref[half] = au[:, cols].astype(BF16)
            dwd_ref[...] = ad[...].astype(BF16).reshape(2, FF_PAD, D_MODEL)

    row = pl.BlockSpec((tm, D_MODEL), lambda j, i: (i, 0))
    hid = pl.BlockSpec((tm, FF_BLOCK), lambda j, i: (i, j))
    col_w = pl.BlockSpec((2, D_MODEL, FF_PAD), lambda j, i: (j, 0, 0))
    row_w = pl.BlockSpec((2, FF_PAD, D_MODEL), lambda j, i: (j, 0, 0))
    return pl.pallas_call(
        body, name="ffn_bwd_dw", grid=(FF_STEPS, ni),
        in_specs=[row, row, hid, hid, hid], out_specs=(col_w, col_w, row_w),
        out_shape=(jax.ShapeDtypeStruct((N_DEV, D_MODEL, FF_PAD), BF16),
                   jax.ShapeDtypeStruct((N_DEV, D_MODEL, FF_PAD), BF16),
                   jax.ShapeDtypeStruct((N_DEV, FF_PAD, D_MODEL), BF16)),
        scratch_shapes=[pltpu.VMEM((D_MODEL, FF_BLOCK), F32), pltpu.VMEM((D_MODEL, FF_BLOCK), F32),
                        pltpu.VMEM((FF_BLOCK, D_MODEL), F32)],
        compiler_params=_params(),
    )(h2, dy, dg, du, act)


def _rms_bwd(dy, t, w):
    r = lax.rsqrt(jnp.mean(t * t, axis=-1, keepdims=True) + EPS)
    gw = dy * w
    dt = r * (gw - t * ((r * r) * jnp.mean(gw * t, axis=-1, keepdims=True)))
    return dt, dy * t * r


def _attn_out_bwd(dy, dh2, x1, wn2, b_g, mixed, o_dil, o_sb, wdil, wsb, bd512):
    s_len = dy.shape[0]
    tm = ROW_TILE
    ni = s_len // tm

    def body(dy_ref, dh_ref, x1_ref, wn_ref, w_ref, mixed_ref, odil_ref, osb_ref, wdil_ref, wsb_ref, bd_ref,
             dx1_ref, dodil_ref, delta_ref, dosb_ref, dwout_ref, dwn_ref, dwdil_ref, dwsb_ref,
             do4_ref, dl4_ref, do16_ref, dl16_ref, wacc, both, stage):
        i = pl.program_id(0)

        @pl.when(i == 0)
        def _():
            wacc[...] = jnp.zeros_like(wacc)
            dwn_ref[...] = jnp.zeros_like(dwn_ref)
            dwdil_ref[...] = jnp.zeros_like(dwdil_ref)
            dwsb_ref[...] = jnp.zeros_like(dwsb_ref)

        dnorm, dw_rows = _rms_bwd(dh_ref[...], x1_ref[...], wn_ref[...])
        dx1 = dy_ref[...] + dnorm
        dx1_ref[...] = dx1
        dwn_ref[...] += jnp.sum(dw_rows, axis=0, keepdims=True)
        dx1b = dx1.astype(BF16)
        w = w_ref[...].reshape(D_MODEL, D_MODEL)
        dmixed = _dot_nt(dx1b, w)
        wacc[...] += _dot_tn(mixed_ref[...], dx1b)
        o_dil = odil_ref[...]
        d_odil, dw_rows = _rms_bwd(dmixed[:, :D_GRP], o_dil, wdil_ref[...])
        dwdil_ref[...] += jnp.sum(dw_rows, axis=0, keepdims=True)
        dodil_ref[...] = d_odil.astype(BF16)
        delta = _mm_split(d_odil * o_dil, bd_ref[...])
        delta_ref[...] = delta
        for p in range(4):
            both[p] = d_odil[:, 128 * p:128 * (p + 1)]
            both[4 + p] = delta[:, 128 * p:128 * (p + 1)]
        _split_views(both, stage, (do4_ref, dl4_ref), (do16_ref, dl16_ref))
        d_osb, dw_rows = _rms_bwd(dmixed[:, D_GRP:], osb_ref[...], wsb_ref[...])
        dwsb_ref[...] += jnp.sum(dw_rows, axis=0, keepdims=True)
        dosb_ref[...] = d_osb.astype(BF16)

        @pl.when(i == ni - 1)
        def _():
            dwout_ref[...] = wacc[...].astype(BF16).reshape(N_DEV, OUT_SHARD, D_MODEL)

    row = lambda w: pl.BlockSpec((tm, w), lambda i: (i, 0))
    return pl.pallas_call(
        body, name="attn_out_bwd", grid=(ni,),
        in_specs=[row(D_MODEL), row(D_MODEL), row(D_MODEL), _full((1, D_MODEL)),
                  pl.BlockSpec((N_DEV, OUT_SHARD, D_MODEL), lambda i: (0, W_OUT_BLOCK, 0)),
                  row(D_MODEL), row(D_GRP), row(D_GRP), _full((1, D_GRP)), _full((1, D_GRP)),
                  _full((D_GRP, D_GRP))],
        out_specs=(row(D_MODEL), row(D_GRP), row(D_GRP), row(D_GRP),
                   _full((N_DEV, OUT_SHARD, D_MODEL)), _full((1, D_MODEL)), _full((1, D_GRP)), _full((1, D_GRP)),
                   _view_spec(tm, 4), _view_spec(tm, 4), _view_spec(tm, 16), _view_spec(tm, 16)),
        out_shape=(jax.ShapeDtypeStruct((s_len, D_MODEL), F32), jax.ShapeDtypeStruct((s_len, D_GRP), BF16),
                   jax.ShapeDtypeStruct((s_len, D_GRP), F32), jax.ShapeDtypeStruct((s_len, D_GRP), BF16),
                   jax.ShapeDtypeStruct((N_DEV, OUT_SHARD, D_MODEL), BF16),
                   jax.ShapeDtypeStruct((1, D_MODEL), F32), jax.ShapeDtypeStruct((1, D_GRP), F32),
                   jax.ShapeDtypeStruct((1, D_GRP), F32),
                   _view_shape(s_len, 4, BF16), _view_shape(s_len, 4, F32),
                   _view_shape(s_len, 16, BF16), _view_shape(s_len, 16, F32)),
        scratch_shapes=[pltpu.VMEM((D_MODEL, D_MODEL), F32), pltpu.VMEM((8, tm, 128), F32),
                        pltpu.VMEM((8, tm, 128), F32)],
        compiler_params=_params(),
    )(dy, dh2, x1, wn2, b_g, mixed, o_dil, o_sb, wdil, wsb, bd512)


def _qkv_bwd(dq_b, dk_b, dv_b, dqs, dks, dvs, qraw, kraw, cos2, sin2, qnw, knw, bd):
    s_len = qraw.shape[0]
    tm = ROW_TILE
    ni = s_len // tm

    def body(dq1, dk1, dv1, dq4, dk4, dv4, dq16, dk16, dv16, dqs_ref, dks_ref, dvs_ref,
             qraw_ref, kraw_ref, cos_ref, sin_ref, qnw_ref, knw_ref, bd_ref,
             dproj_ref, dqn_ref, dkn_ref, stage, nat4, nat16):
        i = pl.program_id(0)

        @pl.when(i == 0)
        def _():
            dqn_ref[...] = jnp.zeros_like(dqn_ref)
            dkn_ref[...] = jnp.zeros_like(dkn_ref)

        _merge_views((dq4, dk4, dv4), (dq16, dk16, dv16), stage, nat4, nat16)
        cos_t, sin_t, bdm = cos_ref[...], sin_ref[...], bd_ref[...]
        for grp, (part1, raw_ref, nw_ref, dn_ref) in enumerate(((dq1, qraw_ref, qnw_ref, dqn_ref),
                                                                (dk1, kraw_ref, knw_ref, dkn_ref))):
            dn_acc = 0.0
            for p in range(4):
                cols = slice(128 * p, 128 * (p + 1))
                d_rope = part1[:, cols] + nat4[4 * grp + p] + nat16[4 * grp + p]
                d_norm = d_rope * cos_t + _swap_halves(d_rope * sin_t)
                t = raw_ref[:, cols]
                w = nw_ref[...]
                r = lax.rsqrt(_mm_split(t * t, bdm) * (1.0 / HEAD_DIM) + EPS)
                gw = d_norm * w
                corr = _mm_split(gw * t, bdm) * (1.0 / HEAD_DIM)
                dt = r * (gw - t * ((r * r) * corr))
                dn_acc = dn_acc + jnp.sum(d_norm * t * r, axis=0, keepdims=True)
                dproj_ref[:, D_GRP * grp + 128 * p:D_GRP * grp + 128 * (p + 1)] = dt.astype(BF16)
            dn_ref[...] += dn_acc
        dproj_ref[:, 2 * D_GRP:3 * D_GRP] = (dv1[...] + _slab_group(nat4, 2) + _slab_group(nat16, 2)).astype(BF16)
        dproj_ref[:, 3 * D_GRP:4 * D_GRP] = dqs_ref[...].astype(BF16)
        dproj_ref[:, 4 * D_GRP:5 * D_GRP] = dks_ref[...].astype(BF16)
        dproj_ref[:, 5 * D_GRP:6 * D_GRP] = dvs_ref[...].astype(BF16)

    row = lambda w: pl.BlockSpec((tm, w), lambda i: (i, 0))
    return pl.pallas_call(
        body, name="qkv_bwd", grid=(ni,),
        in_specs=[row(D_GRP)] * 3 + [_view_spec(tm, 4)] * 3 + [_view_spec(tm, 16)] * 3 + [row(D_GRP)] * 5
        + [row(128), row(128), _full((1, 128)), _full((1, 128)), _full((128, 128))],
        out_specs=(row(D_IN), _full((1, 128)), _full((1, 128))),
        out_shape=(jax.ShapeDtypeStruct((s_len, D_IN), BF16), jax.ShapeDtypeStruct((1, 128), F32),
                   jax.ShapeDtypeStruct((1, 128), F32)),
        scratch_shapes=[pltpu.VMEM((12, tm, 128), F32)] * 3,
        compiler_params=_params(),
    )(dq_b[0], dk_b[0], dv_b[0], dq_b[1], dk_b[1], dv_b[1], dq_b[2], dk_b[2], dv_b[2],
      dqs, dks, dvs, qraw, kraw, cos2, sin2, qnw, knw, bd)


def _in_bwd_dx(dproj, a_g, x2, dx1, wn1):
    s_len = x2.shape[0]
    tm = ROW_TILE
    ni = s_len // tm

    def body(dp_ref, w_ref, x_ref, dx1_ref, wn_ref, gx_ref, dwn_ref, w_full):
        i = pl.program_id(0)

        @pl.when(i == 0)
        def _():
            dwn_ref[...] = jnp.zeros_like(dwn_ref)
            for d in range(N_DEV):
                w_full[:, IN_SHARD * d:IN_SHARD * (d + 1)] = w_ref[d]

        dh = _dot_nt(dp_ref[...], w_full[...])
        dnorm, dw_rows = _rms_bwd(dh, x_ref[...], wn_ref[...])
        gx_ref[...] = dx1_ref[...] + dnorm
        dwn_ref[...] += jnp.sum(dw_rows, axis=0, keepdims=True)

    row = lambda w: pl.BlockSpec((tm, w), lambda i: (i, 0))
    return pl.pallas_call(
        body, name="in_bwd_dx", grid=(ni,),
        in_specs=[row(D_IN), pl.BlockSpec((N_DEV, D_MODEL, IN_SHARD), lambda i: (0, 0, 0)),
                  row(D_MODEL), row(D_MODEL), _full((1, D_MODEL))],
        out_specs=(row(D_MODEL), _full((1, D_MODEL))),
        out_shape=(jax.ShapeDtypeStruct((s_len, D_MODEL), F32), jax.ShapeDtypeStruct((1, D_MODEL), F32)),
        scratch_shapes=[pltpu.VMEM((D_MODEL, D_IN), BF16)],
        compiler_params=_params(),
    )(dproj, a_g, x2, dx1, wn1)


def _in_bwd_dw(h1, dproj):
    s_len = h1.shape[0]
    tm = ROW_TILE
    ni = s_len // tm

    def body(h_ref, dp_ref, dw_ref, acc):
        i = pl.program_id(1)

        @pl.when(i == 0)
        def _():
            acc[...] = jnp.zeros_like(acc)

        acc[...] += _dot_tn(h_ref[...], dp_ref[...])

        @pl.when(i == ni - 1)
        def _():
            for half in range(2):
                dw_ref[half] = acc[:, IN_SHARD * half:IN_SHARD * (half + 1)].astype(BF16)

    return pl.pallas_call(
        body, name="in_bwd_dw", grid=(N_DEV // 2, ni),
        in_specs=[pl.BlockSpec((tm, D_MODEL), lambda d, i: (i, 0)),
                  pl.BlockSpec((tm, 2 * IN_SHARD), lambda d, i: (i, d))],
        out_specs=pl.BlockSpec((2, D_MODEL, IN_SHARD), lambda d, i: (d, 0, 0)),
        out_shape=jax.ShapeDtypeStruct((N_DEV, D_MODEL, IN_SHARD), BF16),
        scratch_shapes=[pltpu.VMEM((D_MODEL, 2 * IN_SHARD), F32)],
        compiler_params=_params(),
    )(h1, dproj)


def _adamw(recv, w, m, v):
    rows, cols = w.shape
    tr = 128 if rows % 128 == 0 else rows

    def body(p_ref, w_ref, m_ref, v_ref, g_ref, d_ref, nm_ref, nv_ref):
        g = p_ref[0].astype(F32)
        for s in range(1, N_DEV):
            g = g + p_ref[s].astype(F32)
        m_new = ADAM_B1 * m_ref[...] + (1.0 - ADAM_B1) * g
        v_new = ADAM_B2 * v_ref[...] + (1.0 - ADAM_B2) * (g * g)
        m_hat = m_new / (1.0 - ADAM_B1 ** ADAM_STEP)
        v_hat = v_new / (1.0 - ADAM_B2 ** ADAM_STEP)
        g_ref[...] = g
        d_ref[...] = -ADAM_LR * (m_hat / (jnp.sqrt(v_hat) + ADAM_EPS) + ADAM_WD * w_ref[...])
        nm_ref[...] = m_new
        nv_ref[...] = v_new

    blk = pl.BlockSpec((tr, cols), lambda i: (i, 0))
    out = jax.ShapeDtypeStruct((rows, cols), F32)
    return pl.pallas_call(
        body, name=f"adamw_{rows}x{cols}", grid=(rows // tr,),
        in_specs=[pl.BlockSpec((N_DEV, tr, cols), lambda i: (0, i, 0)), blk, blk, blk],
        out_specs=(blk,) * 4, out_shape=(out,) * 4,
        compiler_params=_params(),
    )(recv, w, m, v)


def _rope_tables(s_len):
    pos = jnp.arange(s_len, dtype=F32)
    inv_freq = ROPE_THETA ** (-jnp.arange(0, HEAD_DIM, 2, dtype=F32) / HEAD_DIM)
    ang = pos[:, None] * inv_freq[None, :]
    cos, sin = jnp.cos(ang), jnp.sin(ang)
    cos2 = jnp.concatenate([cos, cos, cos, cos], axis=1)
    sin2 = jnp.concatenate([-sin, sin, -sin, sin], axis=1)
    return cos2, sin2


def _block_diag_ones(n):
    i = jnp.arange(n)
    return (i[:, None] // HEAD_DIM == i[None, :] // HEAD_DIM).astype(BF16)


def _pad_cols(t):
    return jnp.pad(t, ((0, 0), (0, FF_PAD - FF_SHARD)))


def _pad_rows(t):
    return jnp.pad(t, ((0, FF_PAD - FF_SHARD), (0, 0)))


def _pack_small(n1, n2, ndil, nsb, nq, nk):
    pad = lambda t: jnp.pad(t.reshape(1, HEAD_DIM), ((0, 0), (0, 128 - HEAD_DIM)))
    rows = [n1.reshape(8, 128), n2.reshape(8, 128), ndil.reshape(4, 128), nsb.reshape(4, 128),
            pad(nq), pad(nk), jnp.zeros((6, 128), F32)]
    return jnp.concatenate(rows, axis=0)


def _unpack_small(t):
    return (t[0:8].reshape(1, D_MODEL), t[8:16].reshape(1, D_MODEL), t[16:20].reshape(1, D_GRP),
            t[20:24].reshape(1, D_GRP), t[24:25, :HEAD_DIM], t[25:26, :HEAD_DIM])


def kernel(x, attn_norm_w, w_in, q_norm_w, k_norm_w, dil_out_norm_w, sb_out_norm_w, w_out, ffn_norm_w, w_gate, w_up, w_down, loss_target, m_attn_norm_w, m_w_in, m_q_norm_w, m_k_norm_w, m_dil_out_norm_w, m_sb_out_norm_w, m_w_out, m_ffn_norm_w, m_w_gate, m_w_up, m_w_down, v_attn_norm_w, v_w_in, v_q_norm_w, v_k_norm_w, v_dil_out_norm_w, v_sb_out_norm_w, v_w_out, v_ffn_norm_w, v_w_gate, v_w_up, v_w_down):
    s_len = x.shape[1]
    x2, tgt = x[0], loss_target[0]

    my_idx = _flat_index(_mesh_pos())
    slot_is_mine = (jnp.arange(N_DEV) == my_idx)[:, None, None]

    (a_g,) = _gather_weights([w_in[0].astype(BF16)])
    w_loc = jnp.concatenate([_pad_cols(w_gate[0]).T, _pad_cols(w_up[0]).T, _pad_rows(w_down[0]), w_out[0]],
                            axis=0).astype(BF16)
    own_in_place = lambda t: jnp.where(slot_is_mine, t[None], jnp.zeros((), t.dtype))
    w_send, w_recv, w_srcs, w_lands, w_token = _spread_start(
        [w_loc], [own_in_place(w_loc)], blockwise=False, name="weights_start")

    cos2, sin2 = _rope_tables(s_len)
    bd128, bd512 = _block_diag_ones(128), _block_diag_ones(D_GRP)
    idx = jnp.arange(SB_TILE)
    tri_suf = (idx[:, None] > idx[None, :]).astype(BF16)
    tri_pre = (idx[:, None] < idx[None, :]).astype(BF16)
    qnw2 = jnp.concatenate([q_norm_w, q_norm_w], axis=1) + w_token[0:1]
    knw2 = jnp.concatenate([k_norm_w, k_norm_w], axis=1)

    (h1, qraw, kraw, q, k, va, qs, ks, vs,
     q4, k4, v4, q16, k16, v16) = _attn_in(x2, attn_norm_w, a_g, cos2, sin2, qnw2, knw2, bd128)
    qkv_views = {1: (q, k, va), 4: (q4, k4, v4), 16: (q16, k16, v16)}
    o_b, lse_b = [], []
    for r in DILATIONS:
        o, lse = _dil_fwd(*qkv_views[r], r)
        o_b.append(o)
        lse_b.append(lse)
    o_sb, c_sb = _sb_fwd(qs, ks, vs, tri_suf)
    (w_g,) = _spread_wait(w_send, w_recv, w_srcs, w_lands, c_sb, blockwise=False, name="weights_wait")
    o_dil, lse_tot, lse4, lse16, mixed, x1 = _attn_out(o_b, lse_b, o_sb, x2, dil_out_norm_w, sb_out_norm_w, w_g)
    g, u, h2, dy, loss_parts = _ffn_fwd(x1, ffn_norm_w, tgt, w_g)
    loss = lax.psum(jnp.sum(loss_parts[::8, 0]), ("x", "y", "c"))

    dg, du, act, dh2 = _ffn_bwd_dx(dy, g, u, w_g)
    (dx1, do_dil, delta, do_sb, dwout, dn2, dndil, dnsb, do4, dl4, do16, dl16) = _attn_out_bwd(
        dy, dh2, x1, ffn_norm_w, w_g, mixed, o_dil, o_sb, dil_out_norm_w, sb_out_norm_w, bd512)
    dwg, dwu, dwd = _ffn_bwd_dw(h2, dy, dg, du, act)
    early = [dwg, dwu, dwd, dwout]
    own_slot_only = lambda t: jnp.where(slot_is_mine, t, jnp.zeros((), t.dtype))
    g_send, g_recv, g_srcs, g_lands, g_token = _spread_start(
        early, [own_slot_only(t) for t in early], blockwise=True, name="grads_start")
    tri_pre = tri_pre + g_token[0, 0].astype(BF16)
    dqs, dks, dvs = _sb_bwd(qs, ks, vs, do_sb, c_sb, tri_suf, tri_pre)
    cot_views = {1: (do_dil, lse_tot, delta), 4: (do4, lse4, dl4), 16: (do16, lse16, dl16)}
    dq_b, dk_b, dv_b = [], [], []
    for r in DILATIONS:
        dq, dk, dv = _dil_bwd(*qkv_views[r], *cot_views[r], r)
        dq_b.append(dq)
        dk_b.append(dk)
        dv_b.append(dv)
    dproj, dqn2, dkn2 = _qkv_bwd(dq_b, dk_b, dv_b, dqs, dks, dvs, qraw, kraw, cos2, sin2, qnw2, knw2, bd128)
    grad_x, dn1 = _in_bwd_dx(dproj, a_g, x2, dx1, attn_norm_w)
    dwin = _in_bwd_dw(h1, dproj)
    dqn = dqn2[:, :HEAD_DIM] + dqn2[:, HEAD_DIM:]
    dkn = dkn2[:, :HEAD_DIM] + dkn2[:, HEAD_DIM:]

    small = _pack_small(dn1, dn2, dndil, dnsb, dqn, dkn)
    i_send, i_recv, i_srcs, i_lands, i_token = _spread_start(
        [dwin], [own_slot_only(dwin)], blockwise=True, name="grads_in_start")
    (r_small,) = _exchange_grads([], small + i_token[0, 0])
    r_gate, r_up, r_down, r_out = _spread_wait(g_send, g_recv, g_srcs, g_lands, r_small, blockwise=True,
                                               name="grads_wait")
    big = {
        "w_gate": tuple(t[:, :FF_SHARD] for t in _adamw(r_gate, _pad_cols(w_gate[0]), _pad_cols(m_w_gate[0]), _pad_cols(v_w_gate[0]))),
        "w_up": tuple(t[:, :FF_SHARD] for t in _adamw(r_up, _pad_cols(w_up[0]), _pad_cols(m_w_up[0]), _pad_cols(v_w_up[0]))),
        "w_down": tuple(t[:FF_SHARD] for t in _adamw(r_down, _pad_rows(w_down[0]), _pad_rows(m_w_down[0]), _pad_rows(v_w_down[0]))),
        "w_out": _adamw(r_out, w_out[0], m_w_out[0], v_w_out[0]),
    }
    packs = [_pack_small(*ts) for ts in (
        (attn_norm_w, ffn_norm_w, dil_out_norm_w, sb_out_norm_w, q_norm_w, k_norm_w),
        (m_attn_norm_w, m_ffn_norm_w, m_dil_out_norm_w, m_sb_out_norm_w, m_q_norm_w, m_k_norm_w),
        (v_attn_norm_w, v_ffn_norm_w, v_dil_out_norm_w, v_sb_out_norm_w, v_q_norm_w, v_k_norm_w))]
    small_out = [_unpack_small(t) for t in _adamw(r_small, *packs)]
    (r_in,) = _spread_wait(i_send, i_recv, i_srcs, i_lands, small_out[3][0], blockwise=True,
                           name="grads_in_wait")
    big["w_in"] = _adamw(r_in, w_in[0], m_w_in[0], v_w_in[0])
    names = ["attn_norm_w", "w_in", "q_norm_w", "k_norm_w", "dil_out_norm_w", "sb_out_norm_w", "w_out",
             "ffn_norm_w", "w_gate", "w_up", "w_down"]
    small_pos = {"attn_norm_w": 0, "ffn_norm_w": 1, "dil_out_norm_w": 2, "sb_out_norm_w": 3,
                 "q_norm_w": 4, "k_norm_w": 5}
    outs = [loss, grad_x[None]]
    for kind in range(4):
        for name in names:
            if name in small_pos:
                outs.append(small_out[kind][small_pos[name]])
            else:
                outs.append(big[name][kind][None])
    return tuple(outs)
```

```python
import functools

import jax
import jax.numpy as jnp
from jax import lax
from jax.experimental import pallas as pl
from jax.experimental.pallas import tpu as pltpu

F32 = jnp.float32
BF16 = jnp.bfloat16

N_DEV = 8
D_MODEL = 1024
HEAD_DIM = 64
D_GRP = 512
D_IN = 6 * D_GRP
IN_SHARD = D_IN // N_DEV
FF_SHARD = 352
FF_PAD = 384
FF_BLOCK = 2 * FF_PAD
FF_STEPS = N_DEV // 2
W_PACK_ROWS = 3 * FF_PAD + 128
W_OUT_BLOCK = 3 * FF_PAD // 128
OUT_SHARD = D_MODEL // N_DEV
BLOCK = 128
DILATIONS = (1, 4, 16)
ROPE_THETA = 10000.0
EPS = 1e-6
ATT_SCALE = HEAD_DIM ** -0.5
NEG = -1e30

ADAM_LR = 0.001
ADAM_B1 = 0.9
ADAM_B2 = 0.999
ADAM_EPS = 1e-08
ADAM_WD = 0.01
ADAM_STEP = 10

SB_TILE = 256
SB_PAIRS = 4
SB_BWD_PAIRS = 2
ROW_TILE = 512
VMEM_LIMIT = 56 * 1024 * 1024
MESH = pl.DeviceIdType.MESH


def _dot(a, b):
    return jnp.dot(a, b, preferred_element_type=F32)


def _dot_nt(a, b):
    return lax.dot_general(a, b, (((1,), (1,)), ((), ())), preferred_element_type=F32)


def _dot_tn(a, b):
    return lax.dot_general(a, b, (((0,), (0,)), ((), ())), preferred_element_type=F32)


def _mm_split(t, m):
    hi = t.astype(BF16)
    lo = (t - hi.astype(F32)).astype(BF16)
    return _dot(hi, m) + _dot(lo, m)


def _params(**kw):
    return pltpu.CompilerParams(vmem_limit_bytes=VMEM_LIMIT, **kw)


def _full(shape):
    nd = len(shape)
    return pl.BlockSpec(shape, lambda *_: (0,) * nd)


def _view_shape(s_len, r, dtype):
    return jax.ShapeDtypeStruct((s_len // r, r * D_GRP), dtype)


def _view_spec(tm, r):
    return pl.BlockSpec((tm // r, r * D_GRP), lambda i: (i, 0))


def _swap_halves(t):
    lane = lax.broadcasted_iota(jnp.int32, t.shape, 1)
    first = (lane & 32) == 0
    return jnp.where(first, pltpu.roll(t, 96, 1), pltpu.roll(t, 32, 1))


def _log_sigmoid(z):
    return jnp.minimum(z, 0.0) - jnp.log(1.0 + jnp.exp(-jnp.abs(z)))


def _log_sigmoid_pair(z):
    neg_abs = lax.bitcast_convert_type(lax.bitcast_convert_type(z, jnp.uint32) | jnp.uint32(0x80000000), F32)
    lb = jnp.minimum(z, 0.0) - jnp.log(1.0 + jnp.exp(neg_abs))
    return lb, lb - z


def _cumsum_mm(t, tri):
    return _dot(t.astype(BF16), tri)


def _split_views(src_ref, stage_ref, views4, views16):
    slabs, n, _ = src_ref.shape
    n4, n16 = n // 4, n // 16
    for j in range(slabs):
        g, lanes = j // 4, 128 * (j % 4)
        src, stage = src_ref.at[j], stage_ref.at[j]
        for c4 in range(4):
            blk = src[pl.ds(c4, n4, stride=4), :]
            stage[n4 * c4:n4 * (c4 + 1), :] = blk
            col = D_GRP * c4 + lanes
            views4[g][:, col:col + 128] = blk.astype(views4[g].dtype)
        for c4 in range(4):
            for c1 in range(4):
                blk = stage[pl.ds(n4 * c4 + c1, n16, stride=4), :]
                col = D_GRP * (4 * c1 + c4) + lanes
                views16[g][:, col:col + 128] = blk.astype(views16[g].dtype)


def _merge_views(views4, views16, stage_ref, dst4_ref, dst16_ref):
    slabs, n, _ = dst4_ref.shape
    n4, n16 = n // 4, n // 16
    for j in range(slabs):
        g, lanes = j // 4, 128 * (j % 4)
        dst4, dst16, stage = dst4_ref.at[j], dst16_ref.at[j], stage_ref.at[j]
        for c4 in range(4):
            col = D_GRP * c4 + lanes
            dst4[pl.ds(c4, n4, stride=4), :] = views4[g][:, col:col + 128].astype(F32)
            for c1 in range(4):
                col = D_GRP * (4 * c1 + c4) + lanes
                stage[pl.ds(n4 * c4 + c1, n16, stride=4), :] = views16[g][:, col:col + 128].astype(F32)
        for c4 in range(4):
            dst16[pl.ds(c4, n4, stride=4), :] = stage[n4 * c4:n4 * (c4 + 1), :]


def _slab_group(ref, g):
    return jnp.concatenate([ref[4 * g + p] for p in range(4)], axis=1)


def _mesh_pos():
    return lax.axis_index("x"), lax.axis_index("y"), lax.axis_index("c")


def _flat_index(p):
    return 4 * p[0] + 2 * p[1] + p[2]


def _gather_weights(shards):
    n_arr = len(shards)

    def body(*refs):
        srcs, outs = refs[:n_arr], refs[n_arr:2 * n_arr]
        send_sems, recv_sems, local_sems = refs[2 * n_arr:]
        x, y, c = _mesh_pos()
        me, sibling = (x, y, c), (x, y, 1 - c)
        chips = [(1 - x, y), (x, 1 - y), (1 - x, 1 - y)]

        def copy(arr, k, block, to, own=False):
            dst = outs[arr].at[_flat_index(block)]
            return pltpu.make_async_remote_copy(
                src_ref=srcs[arr] if own else dst, dst_ref=dst,
                send_sem=send_sems.at[arr, k], recv_sem=recv_sems.at[arr, k],
                device_id=to, device_id_type=MESH)

        for arr in range(n_arr):
            mine = pltpu.make_async_copy(srcs[arr], outs[arr].at[_flat_index(me)], local_sems.at[arr])
            mine.start()
            first = [copy(arr, 0, me, sibling, own=True)]
            first += [copy(arr, 1 + j, me, (*chip, c), own=True) for j, chip in enumerate(chips)]
            for cp in first:
                cp.start()
        for arr in range(n_arr):
            passed = [copy(arr, 4 + j, (*chip, c), sibling) for j, chip in enumerate(chips)]
            for j, chip in enumerate(chips):
                copy(arr, 1 + j, (*chip, c), me).wait_recv()
                passed[j].start()
        for arr in range(n_arr):
            copy(arr, 0, sibling, me).wait_recv()
            for j, chip in enumerate(chips):
                copy(arr, 4 + j, (*chip, 1 - c), me).wait_recv()
            for k in range(7):
                copy(arr, k, me, me).wait_send()
            pltpu.make_async_copy(srcs[arr], outs[arr].at[_flat_index(me)], local_sems.at[arr]).wait()

    any_spec = pl.BlockSpec(memory_space=pl.ANY)
    return pl.pallas_call(
        body, name="gather_weights",
        out_shape=tuple(jax.ShapeDtypeStruct((N_DEV,) + s.shape, s.dtype) for s in shards),
        in_specs=[any_spec] * n_arr, out_specs=(any_spec,) * n_arr,
        scratch_shapes=[pltpu.SemaphoreType.DMA((n_arr, 7)), pltpu.SemaphoreType.DMA((n_arr, 7)),
                        pltpu.SemaphoreType.DMA((n_arr,))],
        compiler_params=pltpu.CompilerParams(has_side_effects=True),
    )(*shards)


_HBM_SPEC = pl.BlockSpec(memory_space=pltpu.HBM)
_SEM_SPEC = pl.BlockSpec(memory_space=pltpu.SEMAPHORE)
_DATAFLOW = pltpu.SideEffectType.DATAFLOW_SIDE_EFFECTING


def _peer_list(x, y, c):
    return [(1 - x if m & 4 else x, 1 - y if m & 2 else y, 1 - c if m & 1 else c) for m in range(1, N_DEV)]


def _spread_copies(src_refs, land_refs, send_sems, recv_sems, blockwise):
    x, y, c = _mesh_pos()
    my_idx = _flat_index((x, y, c))
    copies = []
    for a, (src, land) in enumerate(zip(src_refs, land_refs)):
        for k, peer in enumerate(_peer_list(x, y, c)):
            copies.append(pltpu.make_async_remote_copy(
                src_ref=src.at[_flat_index(peer)] if blockwise else src, dst_ref=land.at[my_idx],
                send_sem=send_sems.at[(N_DEV - 1) * a + k], recv_sem=recv_sems.at[(N_DEV - 1) * a + k],
                device_id=peer, device_id_type=MESH))
    return copies


def _spread_start(srcs, lands, blockwise, name, after=None):
    n = len(srcs)
    extra = [] if after is None else [after]

    def body(*refs):
        ins, outs = refs[:2 * n], refs[2 * n + len(extra):]
        for cp in _spread_copies(ins[:n], ins[n:], outs[0], outs[1], blockwise):
            cp.start()
        token = refs[-1]
        token[...] = jnp.zeros_like(token)

    hbm = lambda t: pltpu.HBM(t.shape, t.dtype)
    sems = pltpu.SemaphoreType.DMA((n * (N_DEV - 1),))
    outs = pl.pallas_call(
        body, name=name,
        out_shape=(sems, sems) + tuple(hbm(t) for t in srcs) + tuple(hbm(t) for t in lands)
        + (jax.ShapeDtypeStruct((8, 128), F32),),
        in_specs=[_HBM_SPEC] * (2 * n) + [pl.BlockSpec(memory_space=pl.ANY)] * len(extra),
        out_specs=(_SEM_SPEC, _SEM_SPEC) + (_HBM_SPEC,) * (2 * n) + (pl.BlockSpec(memory_space=pltpu.VMEM),),
        input_output_aliases={i: 2 + i for i in range(2 * n)},
        compiler_params=pltpu.CompilerParams(has_side_effects=_DATAFLOW),
    )(*[pltpu.with_memory_space_constraint(t, pltpu.HBM) for t in list(srcs) + list(lands)], *extra)
    return outs[0], outs[1], outs[2:2 + n], outs[2 + n:2 + 2 * n], outs[-1]


def _spread_wait(send_sems, recv_sems, srcs, lands, after, blockwise, name):
    n = len(srcs)

    def body(*refs):
        for cp in _spread_copies(refs[:n], refs[n:2 * n], refs[2 * n], refs[2 * n + 1], blockwise):
            cp.wait_send()
            cp.wait_recv()

    hbm = lambda t: pltpu.HBM(t.shape, t.dtype)
    outs = pl.pallas_call(
        body, name=name,
        out_shape=tuple(hbm(t) for t in srcs) + tuple(hbm(t) for t in lands),
        in_specs=[_HBM_SPEC] * (2 * n) + [_SEM_SPEC, _SEM_SPEC, pl.BlockSpec(memory_space=pl.ANY)],
        out_specs=(_HBM_SPEC,) * (2 * n),
        input_output_aliases={i: i for i in range(2 * n)},
        compiler_params=pltpu.CompilerParams(has_side_effects=_DATAFLOW),
    )(*srcs, *lands, send_sems, recv_sems, after)
    return outs[n:]


def _exchange_grads(parts, small):
    n_arr = len(parts)

    def body(*refs):
        ins, outs = refs[:n_arr + 1], refs[n_arr + 1:2 * (n_arr + 1)]
        send_sems, recv_sems, local_sems = refs[2 * (n_arr + 1):]
        x, y, c = _mesh_pos()
        me = (x, y, c)
        my_idx = _flat_index(me)
        peers = []
        for m in range(1, N_DEV):
            peers.append((1 - x if m & 4 else x, 1 - y if m & 2 else y, 1 - c if m & 1 else c))

        def src_block(arr, dev):
            return ins[arr] if arr == n_arr else ins[arr].at[_flat_index(dev)]

        def copy(arr, k):
            return pltpu.make_async_remote_copy(
                src_ref=src_block(arr, peers[k]), dst_ref=outs[arr].at[my_idx],
                send_sem=send_sems.at[arr, k], recv_sem=recv_sems.at[arr, k],
                device_id=peers[k], device_id_type=MESH)

        def local(arr):
            return pltpu.make_async_copy(src_block(arr, me), outs[arr].at[my_idx], local_sems.at[arr])

        for arr in range(n_arr + 1):
            local(arr).start()
            for k in range(N_DEV - 1):
                copy(arr, k).start()
        for arr in range(n_arr + 1):
            for k in range(N_DEV - 1):
                cp = copy(arr, k)
                cp.wait_send()
                cp.wait_recv()
            local(arr).wait()

    any_spec = pl.BlockSpec(memory_space=pl.ANY)
    out_shape = tuple(jax.ShapeDtypeStruct(p.shape, p.dtype) for p in parts)
    out_shape += (jax.ShapeDtypeStruct((N_DEV,) + small.shape, small.dtype),)
    return pl.pallas_call(
        body, name="exchange_grads",
        out_shape=out_shape,
        in_specs=[any_spec] * (n_arr + 1), out_specs=(any_spec,) * (n_arr + 1),
        scratch_shapes=[pltpu.SemaphoreType.DMA((n_arr + 1, N_DEV - 1)),
                        pltpu.SemaphoreType.DMA((n_arr + 1, N_DEV - 1)),
                        pltpu.SemaphoreType.DMA((n_arr + 1,))],
        compiler_params=pltpu.CompilerParams(has_side_effects=True),
    )(*parts, small)


def _head_norm(t, w128, bd):
    ms = _mm_split(t * t, bd) * (1.0 / HEAD_DIM)
    r = lax.rsqrt(ms + EPS)
    return (t * r) * w128, r


def _attn_in(x2, wn1, a_g, cos2, sin2, qnw, knw, bd):
    s_len = x2.shape[0]
    tm = ROW_TILE

    def body(x_ref, wn_ref, w_ref, cos_ref, sin_ref, qnw_ref, knw_ref, bd_ref,
             h1_ref, qraw_ref, kraw_ref, q_ref, k_ref, va_ref, qs_ref, ks_ref, vs_ref,
             q4_ref, k4_ref, v4_ref, q16_ref, k16_ref, v16_ref, proj, slabs, stage, w_full):
        @pl.when(pl.program_id(0) == 0)
        def _():
            for d in range(N_DEV):
                w_full[:, IN_SHARD * d:IN_SHARD * (d + 1)] = w_ref[d]

        xx = x_ref[...]
        r = lax.rsqrt(jnp.mean(xx * xx, axis=-1, keepdims=True) + EPS)
        h = ((xx * r) * wn_ref[...]).astype(BF16)
        h1_ref[...] = h
        proj[...] = _dot(h, w_full[...])
        cos_t, sin_t, bdm = cos_ref[...], sin_ref[...], bd_ref[...]
        for grp, (raw_ref, rope_ref, nw_ref) in enumerate(((qraw_ref, q_ref, qnw_ref),
                                                           (kraw_ref, k_ref, knw_ref))):
            for p in range(4):
                cols = slice(D_GRP * grp + 128 * p, D_GRP * grp + 128 * (p + 1))
                t = proj[:, cols]
                raw_ref[:, 128 * p:128 * (p + 1)] = t
                yn, _ = _head_norm(t, nw_ref[...], bdm)
                roped = yn * cos_t + _swap_halves(yn) * sin_t
                slabs[4 * grp + p] = roped
                rope_ref[:, 128 * p:128 * (p + 1)] = roped.astype(BF16)
        for p in range(4):
            slabs[8 + p] = proj[:, 2 * D_GRP + 128 * p:2 * D_GRP + 128 * (p + 1)]
        for grp, ref in ((2, va_ref), (3, qs_ref), (4, ks_ref), (5, vs_ref)):
            ref[...] = proj[:, D_GRP * grp:D_GRP * (grp + 1)].astype(BF16)
        _split_views(slabs, stage, (q4_ref, k4_ref, v4_ref), (q16_ref, k16_ref, v16_ref))

    row = lambda w: pl.BlockSpec((tm, w), lambda i: (i, 0))
    grp_bf = jax.ShapeDtypeStruct((s_len, D_GRP), BF16)
    grp_f32 = jax.ShapeDtypeStruct((s_len, D_GRP), F32)
    return pl.pallas_call(
        body, name="attn_in", grid=(s_len // tm,),
        in_specs=[row(D_MODEL), _full((1, D_MODEL)),
                  pl.BlockSpec((N_DEV, D_MODEL, IN_SHARD), lambda i: (0, 0, 0)),
                  row(128), row(128), _full((1, 128)), _full((1, 128)), _full((128, 128))],
        out_specs=(row(D_MODEL),) + (row(D_GRP),) * 8 + (_view_spec(tm, 4),) * 3 + (_view_spec(tm, 16),) * 3,
        out_shape=(jax.ShapeDtypeStruct((s_len, D_MODEL), BF16), grp_f32, grp_f32) + (grp_bf,) * 6
        + (_view_shape(s_len, 4, BF16),) * 3 + (_view_shape(s_len, 16, BF16),) * 3,
        scratch_shapes=[pltpu.VMEM((tm, D_IN), F32), pltpu.VMEM((12, tm, 128), F32), pltpu.VMEM((12, tm, 128), F32),
                        pltpu.VMEM((D_MODEL, D_IN), BF16)],
        compiler_params=_params(),
    )(x2, wn1, a_g, cos2, sin2, qnw, knw, bd)


def _band_mask(n):
    i = lax.broadcasted_iota(jnp.int32, (2 * BLOCK, 2 * BLOCK), 0) & (BLOCK - 1)
    j = lax.broadcasted_iota(jnp.int32, (2 * BLOCK, 2 * BLOCK), 1)
    dist = i + BLOCK - j
    return (dist >= 0) & (dist <= BLOCK) & ((n - 1) * BLOCK + j >= 0)


def _stack_heads(t2, head0):
    return jnp.concatenate([jnp.where(head0, t2, 0), jnp.where(head0, 0, t2)], axis=0)


def _unstack_heads(t, head0):
    return jnp.where(head0, t[0:BLOCK], t[BLOCK:2 * BLOCK])


def _dil_fwd(qv, kv, vv, r):
    sub_len = qv.shape[0]
    nb = sub_len // BLOCK

    qb = 2 if nb % 2 == 0 else 1

    def body(q_ref, kp_ref, kc_ref, vp_ref, vc_ref, o_ref, lse_ref):
        n = pl.program_id(1)
        lane = lax.broadcasted_iota(jnp.int32, (BLOCK, 128), 1)
        head0 = lane < HEAD_DIM
        units = [(b, slice(128 * p, 128 * (p + 1))) for b in range(qb) for p in range(4)]
        valid = [_band_mask(qb * n + b) for b in range(qb)]
        rows = [slice(BLOCK * b, BLOCK * (b + 1)) for b in range(qb)]

        def keys(prev_ref, cur_ref, b, c):
            before = prev_ref[:, c] if b == 0 else cur_ref[rows[b - 1], c]
            return jnp.concatenate([before, cur_ref[rows[b], c]], axis=0)

        qqs = [_stack_heads(q_ref[rows[b], c] * ATT_SCALE, head0) for b, c in units]
        kks = [keys(kp_ref, kc_ref, b, c) for b, c in units]
        vvs = [keys(vp_ref, vc_ref, b, c) for b, c in units]
        ss = [_dot_nt(qq, kk) for qq, kk in zip(qqs, kks)]
        prs, dens, lses = [], [], []
        for (b, _), s in zip(units, ss):
            s = jnp.where(valid[b], s, NEG)
            m = jnp.max(s, axis=-1, keepdims=True)
            pr = jnp.exp(s - m)
            den = jnp.sum(pr, axis=-1, keepdims=True)
            prs.append(pr.astype(BF16))
            dens.append(den)
            lses.append(m + jnp.log(den))
        pvs = [_dot(pr, vv2) for pr, vv2 in zip(prs, vvs)]
        for (b, c), pv, den, lse in zip(units, pvs, dens, lses):
            o_ref[rows[b], c] = _unstack_heads(pv / den, head0)
            lse_ref[rows[b], c] = _unstack_heads(jnp.broadcast_to(lse, (2 * BLOCK, 128)), head0)

    cur = pl.BlockSpec((qb * BLOCK, D_GRP), lambda c, n: (n, c))
    prev = pl.BlockSpec((BLOCK, D_GRP), lambda c, n: (jnp.maximum(qb * n - 1, 0), c))
    out = jax.ShapeDtypeStruct(qv.shape, F32)
    return pl.pallas_call(
        body, name=f"dil_fwd_r{r}", grid=(r, nb // qb),
        in_specs=[cur, prev, cur, prev, cur], out_specs=(cur, cur), out_shape=(out, out),
        compiler_params=_params(),
    )(qv, kv, kv, vv, vv)


def _dil_bwd(qv, kv, vv, dov, lsev, deltav, r):
    sub_len = qv.shape[0]
    nb = sub_len // BLOCK

    def body(q_ref, kp_ref, kc_ref, vp_ref, vc_ref, do_ref, lse_ref, dl_ref,
             dq_ref, dk_ref, dv_ref, dk_carry, dv_carry):
        n = pl.program_id(1)

        @pl.when(n == 0)
        def _():
            dk_carry[...] = jnp.zeros_like(dk_carry)
            dv_carry[...] = jnp.zeros_like(dv_carry)

        @pl.when(n < nb)
        def _():
            valid = _band_mask(n)
            lane = lax.broadcasted_iota(jnp.int32, (BLOCK, 128), 1)
            head0 = lane < HEAD_DIM
            pairs = [slice(128 * p, 128 * (p + 1)) for p in range(4)]
            qqs = [_stack_heads(q_ref[:, c] * ATT_SCALE, head0) for c in pairs]
            dos = [_stack_heads(do_ref[:, c], head0) for c in pairs]
            kks = [jnp.concatenate([kp_ref[:, c], kc_ref[:, c]], axis=0) for c in pairs]
            vvs = [jnp.concatenate([vp_ref[:, c], vc_ref[:, c]], axis=0) for c in pairs]
            ss = [_dot_nt(qq, kk) for qq, kk in zip(qqs, kks)]
            dps = [_dot_nt(do, vv2) for do, vv2 in zip(dos, vvs)]
            prs, dss = [], []
            for c, s, dp in zip(pairs, ss, dps):
                stats = []
                for ref in (lse_ref, dl_ref):
                    t2 = ref[:, c]
                    stats.append(jnp.concatenate(
                        [jnp.sum(jnp.where(lane == 0, t2, 0.0), axis=-1, keepdims=True),
                         jnp.sum(jnp.where(lane == HEAD_DIM, t2, 0.0), axis=-1, keepdims=True)], axis=0))
                pr = jnp.where(valid, jnp.exp(jnp.minimum(s - stats[0], 0.0)), 0.0)
                prs.append(pr.astype(BF16))
                dss.append((pr * (dp - stats[1])).astype(BF16))
            dqs = [_dot(ds, kk) for ds, kk in zip(dss, kks)]
            dkks = [_dot_tn(ds, qq) for ds, qq in zip(dss, qqs)]
            dvvs = [_dot_tn(pr, do) for pr, do in zip(prs, dos)]
            for c, dq, dkk, dvv in zip(pairs, dqs, dkks, dvvs):
                dq_ref[:, c] = _unstack_heads(dq, head0) * ATT_SCALE
                dk_ref[:, c] = dk_carry[:, c] + dkk[:BLOCK]
                dv_ref[:, c] = dv_carry[:, c] + dvv[:BLOCK]
                dk_carry[:, c] = dkk[BLOCK:]
                dv_carry[:, c] = dvv[BLOCK:]

        @pl.when(n == nb)
        def _():
            dk_ref[...] = dk_carry[...]
            dv_ref[...] = dv_carry[...]

    last = nb - 1
    cur = pl.BlockSpec((BLOCK, D_GRP), lambda c, n: (jnp.minimum(n, last), c))
    prev = pl.BlockSpec((BLOCK, D_GRP), lambda c, n: (jnp.clip(n - 1, 0, last), c))
    out = jax.ShapeDtypeStruct(qv.shape, F32)
    return pl.pallas_call(
        body, name=f"dil_bwd_r{r}", grid=(r, nb + 1),
        in_specs=[cur, prev, cur, prev, cur, cur, cur, cur],
        out_specs=(cur, prev, prev), out_shape=(out, out, out),
        scratch_shapes=[pltpu.VMEM((BLOCK, D_GRP), F32), pltpu.VMEM((BLOCK, D_GRP), F32)],
        compiler_params=_params(),
    )(qv, kv, kv, vv, vv, dov, lsev, deltav)


def _sb_fwd(qs, ks, vs, tri_suf):
    s_len = qs.shape[0]
    t = SB_TILE
    nq = s_len // t

    npair = SB_PAIRS

    def body(q_ref, k_ref, v_ref, u_ref, o_ref, c_ref, qq, vt, acc, cf, csave):
        row = lax.broadcasted_iota(jnp.int32, (2 * t, t), 0) & (t - 1)
        col = lax.broadcasted_iota(jnp.int32, (2 * t, t), 1)
        diag_mask = col < row
        lane1 = lax.broadcasted_iota(jnp.int32, (t, 128), 1)
        head0 = lane1 < HEAD_DIM
        lane2 = lax.broadcasted_iota(jnp.int32, (2 * t, 128), 1)
        uu = u_ref[...]
        pr = range(npair)
        cols = [slice(128 * pp, 128 * (pp + 1)) for pp in pr]

        i = pl.program_id(1)

        @pl.when(i == 0)
        def _():
            def transpose_v(j, _):
                rows = pl.ds(pl.multiple_of(j * t, t), t)
                for pp in pr:
                    vt[pp, j] = v_ref[rows, cols[pp]].astype(F32).T.astype(BF16)
                return 0

            lax.fori_loop(0, nq, transpose_v, 0)

        for pp in pr:
            q2 = q_ref[:, cols[pp]] * ATT_SCALE
            qq[pp, 0:t, :] = jnp.where(head0, q2, 0)
            qq[pp, t:2 * t, :] = jnp.where(head0, 0, q2)
        acc[...] = jnp.zeros_like(acc)
        cf[...] = jnp.zeros_like(cf)
        csave[...] = jnp.zeros_like(csave)

        def tile(kb, diag):
            krows = pl.ds(pl.multiple_of(kb * t, t), t)
            zs = [_dot_nt(qq[pp], k_ref[krows, cols[pp]]) for pp in pr]
            lbk = [_log_sigmoid_pair(z) for z in zs]
            lks = [jnp.where(diag_mask, lk, 0.0) if diag else lk for _, lk in lbk]
            sufs = [_cumsum_mm(lk, uu) for lk in lks]
            carries = [cf[pp] for pp in pr]
            avs = []
            for pp in pr:
                a = jnp.exp(lbk[pp][0] + (sufs[pp] + jnp.concatenate([carries[pp]] * (t // 128), axis=1)))
                avs.append((jnp.where(diag_mask, a, 0.0) if diag else a).astype(BF16))
            pvs = [_dot_nt(vt[pp, kb], avs[pp]) for pp in pr]
            for pp in pr:
                acc[pp] += pvs[pp]
                csave[pp] = jnp.where(lane2 == kb, carries[pp], csave[pp])
                cf[pp] = carries[pp] + jnp.broadcast_to(jnp.sum(lks[pp], axis=-1, keepdims=True), (2 * t, 128))

        tile(i, True)

        def k_block(step, _):
            tile(i - 1 - step, False)
            return 0

        lax.fori_loop(0, i, k_block, 0)
        for pp in pr:
            o_ref[:, cols[pp]] = jnp.where(head0, acc[pp, :, 0:t].T, acc[pp, :, t:2 * t].T)
            c_ref[2 * pp] = csave[pp, 0:t, :]
            c_ref[2 * pp + 1] = csave[pp, t:2 * t, :]

    width = 128 * npair
    kv = pl.BlockSpec((s_len, width), lambda p, i: (0, p))
    qo = pl.BlockSpec((t, width), lambda p, i: (i, p))
    return pl.pallas_call(
        body, name="sb_fwd", grid=(4 // npair, nq),
        in_specs=[qo, kv, kv, pl.BlockSpec((t, t), lambda p, i: (0, 0))],
        out_specs=(qo, pl.BlockSpec((2 * npair, t, 128), lambda p, i: (p, i, 0))),
        out_shape=(jax.ShapeDtypeStruct((s_len, D_GRP), F32),
                   jax.ShapeDtypeStruct((8, s_len, 128), F32)),
        scratch_shapes=[pltpu.VMEM((npair, 2 * t, 128), BF16), pltpu.VMEM((npair, nq, 128, t), BF16),
                        pltpu.VMEM((npair, 128, 2 * t), F32),
                        pltpu.VMEM((npair, 2 * t, 128), F32), pltpu.VMEM((npair, 2 * t, 128), F32)],
        compiler_params=_params(),
    )(qs, ks, vs, tri_suf)


def _sb_bwd(qs, ks, vs, dos, csaved, tri_suf, tri_pre):
    s_len = qs.shape[0]
    t = SB_TILE
    nq = s_len // t

    npair = SB_BWD_PAIRS

    def body(q_ref, k_ref, v_ref, do_ref, c_ref, u_ref, p_ref, dq_ref, dk_ref, dv_ref,
             qq, dd, qqt, ddt, kt, dq_acc, dkt, dvt, cg):
        row = lax.broadcasted_iota(jnp.int32, (2 * t, t), 0) & (t - 1)
        col = lax.broadcasted_iota(jnp.int32, (2 * t, t), 1)
        diag_mask = col < row
        lane1 = lax.broadcasted_iota(jnp.int32, (t, 128), 1)
        head0 = lane1 < HEAD_DIM
        lane2 = lax.broadcasted_iota(jnp.int32, (2 * t, 128), 1)
        uu, pm = u_ref[...], p_ref[...]
        pr = range(npair)
        cols = [slice(128 * pp, 128 * (pp + 1)) for pp in pr]
        i = pl.program_id(1)

        @pl.when(i == 0)
        def _():
            dkt[...] = jnp.zeros_like(dkt)
            dvt[...] = jnp.zeros_like(dvt)

            def transpose_k(j, _):
                rows = pl.ds(pl.multiple_of(j * t, t), t)
                for pp in pr:
                    kt[pp, j] = k_ref[rows, cols[pp]].astype(F32).T.astype(BF16)
                return 0

            lax.fori_loop(0, nq, transpose_k, 0)

        for pp in pr:
            q2 = q_ref[:, cols[pp]].astype(F32) * ATT_SCALE
            do2 = do_ref[:, cols[pp]].astype(F32)
            for src, nat, tr in ((q2, qq, qqt), (do2, dd, ddt)):
                stacked = jnp.concatenate([jnp.where(head0, src, 0.0), jnp.where(head0, 0.0, src)], axis=0)
                nat[pp] = stacked.astype(BF16)
                tr[pp] = stacked.T.astype(BF16)
        dq_acc[...] = jnp.zeros_like(dq_acc)
        cg[...] = jnp.zeros_like(cg)

        def tile(kb, diag):
            krows = pl.ds(pl.multiple_of(kb * t, t), t)
            zs = [_dot_nt(qq[pp], k_ref[krows, cols[pp]]) for pp in pr]
            das = [_dot_nt(dd[pp], v_ref[krows, cols[pp]]) for pp in pr]
            lbk = [_log_sigmoid_pair(z) for z in zs]
            lks = [jnp.where(diag_mask, lk, 0.0) if diag else lk for _, lk in lbk]
            sufs = [_cumsum_mm(lk, uu) for lk in lks]
            avs, gs = [], []
            for pp in pr:
                cs = jnp.concatenate([c_ref[2 * pp], c_ref[2 * pp + 1]], axis=0)
                cf = jnp.sum(jnp.where(lane2 == kb, cs, 0.0), axis=-1, keepdims=True)
                a = jnp.exp(lbk[pp][0] + (sufs[pp] + cf))
                a = jnp.where(diag_mask, a, 0.0) if diag else a
                avs.append(a.astype(BF16))
                gs.append(a * das[pp])
            gpres = [_cumsum_mm(g, pm) for g in gs]
            dzs = []
            for pp in pr:
                carry = cg[pp]
                beta = jnp.exp(lbk[pp][0])
                dz = gs[pp] - beta * (gs[pp] + (gpres[pp] + jnp.concatenate([carry] * (t // 128), axis=1)))
                dzs.append((jnp.where(diag_mask, dz, 0.0) if diag else dz).astype(BF16))
                cg[pp] = carry + jnp.broadcast_to(jnp.sum(gs[pp], axis=-1, keepdims=True), (2 * t, 128))
            dqs = [_dot_nt(kt[pp, kb], dzs[pp]) for pp in pr]
            dks = [_dot(qqt[pp], dzs[pp]) for pp in pr]
            dvs = [_dot(ddt[pp], avs[pp]) for pp in pr]
            for pp in pr:
                dq_acc[pp] += dqs[pp]
                dkt[pp, kb] += dks[pp]
                dvt[pp, kb] += dvs[pp]

        def k_block(kb, _):
            tile(kb, False)
            return 0

        lax.fori_loop(0, i, k_block, 0)
        tile(i, True)
        for pp in pr:
            dq_ref[:, cols[pp]] = jnp.where(head0, dq_acc[pp, :, 0:t].T, dq_acc[pp, :, t:2 * t].T) * ATT_SCALE

        @pl.when(i == nq - 1)
        def _():
            def untranspose(j, _):
                rows = pl.ds(pl.multiple_of(j * t, t), t)
                for pp in pr:
                    dk_ref[rows, cols[pp]] = dkt[pp, j].T
                    dv_ref[rows, cols[pp]] = dvt[pp, j].T
                return 0

            lax.fori_loop(0, nq, untranspose, 0)

    width = 128 * npair
    kv = pl.BlockSpec((s_len, width), lambda p, i: (0, p))
    qo = pl.BlockSpec((t, width), lambda p, i: (i, p))
    tri = pl.BlockSpec((t, t), lambda p, i: (0, 0))
    out = jax.ShapeDtypeStruct((s_len, D_GRP), F32)
    return pl.pallas_call(
        body, name="sb_bwd", grid=(4 // npair, nq),
        in_specs=[qo, kv, kv, qo, pl.BlockSpec((2 * npair, t, 128), lambda p, i: (p, i, 0)), tri, tri],
        out_specs=(qo, kv, kv), out_shape=(out, out, out),
        scratch_shapes=[pltpu.VMEM((npair, 2 * t, 128), BF16), pltpu.VMEM((npair, 2 * t, 128), BF16),
                        pltpu.VMEM((npair, 128, 2 * t), BF16), pltpu.VMEM((npair, 128, 2 * t), BF16),
                        pltpu.VMEM((npair, nq, 128, t), BF16),
                        pltpu.VMEM((npair, 128, 2 * t), F32),
                        pltpu.VMEM((npair, nq, 128, t), F32), pltpu.VMEM((npair, nq, 128, t), F32),
                        pltpu.VMEM((npair, 2 * t, 128), F32)],
        compiler_params=_params(),
    )(qs, ks, vs, dos, csaved, tri_suf, tri_pre)


def _attn_out(o_b, lse_b, o_sb, x2, wdil, wsb, b_g):
    s_len = x2.shape[0]
    tm = ROW_TILE

    def body(o1_ref, l1_ref, o4_ref, l4_ref, o16_ref, l16_ref, osb_ref, x_ref, wdil_ref, wsb_ref, w_ref,
             odil_ref, lse_ref, lse4_ref, lse16_ref, mixed_ref, x1_ref, stage, nat4, nat16):
        _merge_views((o4_ref, l4_ref), (o16_ref, l16_ref), stage, nat4, nat16)
        os_ = (o1_ref[...], _slab_group(nat4, 0), _slab_group(nat16, 0))
        ls = (l1_ref[...], _slab_group(nat4, 1), _slab_group(nat16, 1))
        mx = jnp.maximum(jnp.maximum(ls[0], ls[1]), ls[2])
        es = [jnp.exp(l - mx) for l in ls]
        den = es[0] + es[1] + es[2]
        o_dil = (es[0] * os_[0] + es[1] * os_[1] + es[2] * os_[2]) / den
        odil_ref[...] = o_dil
        lse = mx + jnp.log(den)
        lse_ref[...] = lse
        for p in range(4):
            nat4[p] = lse[:, 128 * p:128 * (p + 1)]
        _split_views(nat4.at[0:4], stage.at[0:4], (lse4_ref,), (lse16_ref,))
        halves = []
        for t, w_r in ((o_dil, wdil_ref), (osb_ref[...], wsb_ref)):
            r = lax.rsqrt(jnp.mean(t * t, axis=-1, keepdims=True) + EPS)
            halves.append(((t * r) * w_r[...]).astype(BF16))
        mixed = jnp.concatenate(halves, axis=1)
        mixed_ref[...] = mixed
        w = w_ref[...].reshape(D_MODEL, D_MODEL)
        x1_ref[...] = x_ref[...] + _dot(mixed, w)

    row = lambda w: pl.BlockSpec((tm, w), lambda i: (i, 0))
    return pl.pallas_call(
        body, name="attn_out", grid=(s_len // tm,),
        in_specs=[row(D_GRP)] * 2 + [_view_spec(tm, 4)] * 2 + [_view_spec(tm, 16)] * 2
        + [row(D_GRP), row(D_MODEL), _full((1, D_GRP)), _full((1, D_GRP)),
           pl.BlockSpec((N_DEV, OUT_SHARD, D_MODEL), lambda i: (0, W_OUT_BLOCK, 0))],
        out_specs=(row(D_GRP), row(D_GRP), _view_spec(tm, 4), _view_spec(tm, 16), row(D_MODEL), row(D_MODEL)),
        out_shape=(jax.ShapeDtypeStruct((s_len, D_GRP), F32), jax.ShapeDtypeStruct((s_len, D_GRP), F32),
                   _view_shape(s_len, 4, F32), _view_shape(s_len, 16, F32),
                   jax.ShapeDtypeStruct((s_len, D_MODEL), BF16), jax.ShapeDtypeStruct((s_len, D_MODEL), F32)),
        scratch_shapes=[pltpu.VMEM((8, tm, 128), F32)] * 3,
        compiler_params=_params(),
    )(o_b[0], lse_b[0], o_b[1], lse_b[1], o_b[2], lse_b[2], o_sb, x2, wdil, wsb, b_g)


def _two_shards(w_ref):
    return w_ref[...].reshape(FF_BLOCK, D_MODEL)


def _ffn_fwd(x1, wn2, tgt, w_g):
    s_len = x1.shape[0]
    tm = ROW_TILE
    ni = s_len // tm

    def body(x_ref, wn_ref, t_ref, wg_ref, wu_ref, wd_ref, g_ref, u_ref, h2_ref, dy_ref, loss_ref, acc):
        j = pl.program_id(1)

        @pl.when(j == 0)
        def _():
            xx = x_ref[...]
            r = lax.rsqrt(jnp.mean(xx * xx, axis=-1, keepdims=True) + EPS)
            h2_ref[...] = ((xx * r) * wn_ref[...]).astype(BF16)
            acc[...] = jnp.zeros_like(acc)

        h = h2_ref[...]
        g = _dot_nt(h, _two_shards(wg_ref))
        u = _dot_nt(h, _two_shards(wu_ref))
        g_ref[...] = g
        u_ref[...] = u
        act = (g * (1.0 / (1.0 + jnp.exp(-g)))) * u
        acc[...] += _dot(act.astype(BF16), _two_shards(wd_ref))

        @pl.when(j == FF_STEPS - 1)
        def _():
            err = (x_ref[...] + acc[...]) - t_ref[...]
            dy_ref[...] = err * (1.0 / D_MODEL)
            part = 0.5 * jnp.sum(jnp.mean(err * err, axis=-1, keepdims=True))
            loss_ref[...] = jnp.full((8, 128), part, F32)

    row = pl.BlockSpec((tm, D_MODEL), lambda i, j: (i, 0))
    hid = pl.BlockSpec((tm, FF_BLOCK), lambda i, j: (i, j))
    return pl.pallas_call(
        body, name="ffn_fwd", grid=(ni, FF_STEPS),
        in_specs=[row, pl.BlockSpec((1, D_MODEL), lambda i, j: (0, 0)), row,
                  pl.BlockSpec((2, FF_PAD, D_MODEL), lambda i, j: (j, 0, 0)),
                  pl.BlockSpec((2, FF_PAD, D_MODEL), lambda i, j: (j, 1, 0)),
                  pl.BlockSpec((2, FF_PAD, D_MODEL), lambda i, j: (j, 2, 0))],
        out_specs=(hid, hid, row, row, pl.BlockSpec((8, 128), lambda i, j: (i, 0))),
        out_shape=(jax.ShapeDtypeStruct((s_len, N_DEV * FF_PAD), F32),
                   jax.ShapeDtypeStruct((s_len, N_DEV * FF_PAD), F32),
                   jax.ShapeDtypeStruct((s_len, D_MODEL), BF16),
                   jax.ShapeDtypeStruct((s_len, D_MODEL), F32),
                   jax.ShapeDtypeStruct((ni * 8, 128), F32)),
        scratch_shapes=[pltpu.VMEM((tm, D_MODEL), F32)],
        compiler_params=_params(),
    )(x1, wn2, tgt, w_g, w_g, w_g)


def _ffn_bwd_dx(dy, g, u, w_g):
    s_len = dy.shape[0]
    tm = ROW_TILE

    def body(dy_ref, g_ref, u_ref, wg_ref, wu_ref, wd_ref, dg_ref, du_ref, act_ref, dh_ref, acc):
        j = pl.program_id(1)

        @pl.when(j == 0)
        def _():
            acc[...] = jnp.zeros_like(acc)

        gg, uu = g_ref[...], u_ref[...]
        da = _dot_nt(dy_ref[...].astype(BF16), _two_shards(wd_ref))
        sig = 1.0 / (1.0 + jnp.exp(-gg))
        silu = gg * sig
        act_ref[...] = (silu * uu).astype(BF16)
        du = (da * silu).astype(BF16)
        dg = (da * uu * (sig * (1.0 + gg * (1.0 - sig)))).astype(BF16)
        du_ref[...] = du
        dg_ref[...] = dg
        acc[...] += _dot(dg, _two_shards(wg_ref)) + _dot(du, _two_shards(wu_ref))

        @pl.when(j == FF_STEPS - 1)
        def _():
            dh_ref[...] = acc[...]

    row = pl.BlockSpec((tm, D_MODEL), lambda i, j: (i, 0))
    hid = pl.BlockSpec((tm, FF_BLOCK), lambda i, j: (i, j))
    hid_bf = jax.ShapeDtypeStruct((s_len, N_DEV * FF_PAD), BF16)
    return pl.pallas_call(
        body, name="ffn_bwd_dx", grid=(s_len // tm, FF_STEPS),
        in_specs=[row, hid, hid,
                  pl.BlockSpec((2, FF_PAD, D_MODEL), lambda i, j: (j, 0, 0)),
                  pl.BlockSpec((2, FF_PAD, D_MODEL), lambda i, j: (j, 1, 0)),
                  pl.BlockSpec((2, FF_PAD, D_MODEL), lambda i, j: (j, 2, 0))],
        out_specs=(hid, hid, hid, row),
        out_shape=(hid_bf, hid_bf, hid_bf, jax.ShapeDtypeStruct((s_len, D_MODEL), F32)),
        scratch_shapes=[pltpu.VMEM((tm, D_MODEL), F32)],
        compiler_params=_params(),
    )(dy, g, u, w_g, w_g, w_g)


def _ffn_bwd_dw(h2, dy, dg, du, act):
    s_len = h2.shape[0]
    tm = ROW_TILE
    ni = s_len // tm

    def body(h_ref, dy_ref, dg_ref, du_ref, act_ref, dwg_ref, dwu_ref, dwd_ref, ag, au, ad):
        i = pl.program_id(1)

        @pl.when(i == 0)
        def _():
            ag[...] = jnp.zeros_like(ag)
            au[...] = jnp.zeros_like(au)
            ad[...] = jnp.zeros_like(ad)

        h = h_ref[...]
        ag[...] += _dot_tn(h, dg_ref[...])
        au[...] += _dot_tn(h, du_ref[...])
        ad[...] += _dot_tn(act_ref[...], dy_ref[...].astype(BF16))

        @pl.when(i == ni - 1)
        def _():
            for half in range(2):
                cols = slice(FF_PAD * half, FF_PAD * (half + 1))
                dwg_ref[half] = ag[:, cols].astype(BF16)
                dwu_ref[half] = au[:, cols].astype(BF16)
            dwd_ref[...] = ad[...].astype(BF16).reshape(2, FF_PAD, D_MODEL)

    row = pl.BlockSpec((tm, D_MODEL), lambda j, i: (i, 0))
    hid = pl.BlockSpec((tm, FF_BLOCK), lambda j, i: (i, j))
    col_w = pl.BlockSpec((2, D_MODEL, FF_PAD), lambda j, i: (j, 0, 0))
    row_w = pl.BlockSpec((2, FF_PAD, D_MODEL), lambda j, i: (j, 0, 0))
    return pl.pallas_call(
        body, name="ffn_bwd_dw", grid=(FF_STEPS, ni),
        in_specs=[row, row, hid, hid, hid], out_specs=(col_w, col_w, row_w),
        out_shape=(jax.ShapeDtypeStruct((N_DEV, D_MODEL, FF_PAD), BF16),
                   jax.ShapeDtypeStruct((N_DEV, D_MODEL, FF_PAD), BF16),
                   jax.ShapeDtypeStruct((N_DEV, FF_PAD, D_MODEL), BF16)),
        scratch_shapes=[pltpu.VMEM((D_MODEL, FF_BLOCK), F32), pltpu.VMEM((D_MODEL, FF_BLOCK), F32),
                        pltpu.VMEM((FF_BLOCK, D_MODEL), F32)],
        compiler_params=_params(),
    )(h2, dy, dg, du, act)


def _rms_bwd(dy, t, w):
    r = lax.rsqrt(jnp.mean(t * t, axis=-1, keepdims=True) + EPS)
    gw = dy * w
    dt = r * (gw - t * ((r * r) * jnp.mean(gw * t, axis=-1, keepdims=True)))
    return dt, dy * t * r


def _attn_out_bwd(dy, dh2, x1, wn2, b_g, mixed, o_dil, o_sb, wdil, wsb, bd512):
    s_len = dy.shape[0]
    tm = ROW_TILE
    ni = s_len // tm

    def body(dy_ref, dh_ref, x1_ref, wn_ref, w_ref, mixed_ref, odil_ref, osb_ref, wdil_ref, wsb_ref, bd_ref,
             dx1_ref, dodil_ref, delta_ref, dosb_ref, dwout_ref, dwn_ref, dwdil_ref, dwsb_ref,
             do4_ref, dl4_ref, do16_ref, dl16_ref, wacc, both, stage):
        i = pl.program_id(0)

        @pl.when(i == 0)
        def _():
            wacc[...] = jnp.zeros_like(wacc)
            dwn_ref[...] = jnp.zeros_like(dwn_ref)
            dwdil_ref[...] = jnp.zeros_like(dwdil_ref)
            dwsb_ref[...] = jnp.zeros_like(dwsb_ref)

        dnorm, dw_rows = _rms_bwd(dh_ref[...], x1_ref[...], wn_ref[...])
        dx1 = dy_ref[...] + dnorm
        dx1_ref[...] = dx1
        dwn_ref[...] += jnp.sum(dw_rows, axis=0, keepdims=True)
        dx1b = dx1.astype(BF16)
        w = w_ref[...].reshape(D_MODEL, D_MODEL)
        dmixed = _dot_nt(dx1b, w)
        wacc[...] += _dot_tn(mixed_ref[...], dx1b)
        o_dil = odil_ref[...]
        d_odil, dw_rows = _rms_bwd(dmixed[:, :D_GRP], o_dil, wdil_ref[...])
        dwdil_ref[...] += jnp.sum(dw_rows, axis=0, keepdims=True)
        dodil_ref[...] = d_odil.astype(BF16)
        delta = _mm_split(d_odil * o_dil, bd_ref[...])
        delta_ref[...] = delta
        for p in range(4):
            both[p] = d_odil[:, 128 * p:128 * (p + 1)]
            both[4 + p] = delta[:, 128 * p:128 * (p + 1)]
        _split_views(both, stage, (do4_ref, dl4_ref), (do16_ref, dl16_ref))
        d_osb, dw_rows = _rms_bwd(dmixed[:, D_GRP:], osb_ref[...], wsb_ref[...])
        dwsb_ref[...] += jnp.sum(dw_rows, axis=0, keepdims=True)
        dosb_ref[...] = d_osb.astype(BF16)

        @pl.when(i == ni - 1)
        def _():
            dwout_ref[...] = wacc[...].astype(BF16).reshape(N_DEV, OUT_SHARD, D_MODEL)

    row = lambda w: pl.BlockSpec((tm, w), lambda i: (i, 0))
    return pl.pallas_call(
        body, name="attn_out_bwd", grid=(ni,),
        in_specs=[row(D_MODEL), row(D_MODEL), row(D_MODEL), _full((1, D_MODEL)),
                  pl.BlockSpec((N_DEV, OUT_SHARD, D_MODEL), lambda i: (0, W_OUT_BLOCK, 0)),
                  row(D_MODEL), row(D_GRP), row(D_GRP), _full((1, D_GRP)), _full((1, D_GRP)),
                  _full((D_GRP, D_GRP))],
        out_specs=(row(D_MODEL), row(D_GRP), row(D_GRP), row(D_GRP),
                   _full((N_DEV, OUT_SHARD, D_MODEL)), _full((1, D_MODEL)), _full((1, D_GRP)), _full((1, D_GRP)),
                   _view_spec(tm, 4), _view_spec(tm, 4), _view_spec(tm, 16), _view_spec(tm, 16)),
        out_shape=(jax.ShapeDtypeStruct((s_len, D_MODEL), F32), jax.ShapeDtypeStruct((s_len, D_GRP), BF16),
                   jax.ShapeDtypeStruct((s_len, D_GRP), F32), jax.ShapeDtypeStruct((s_len, D_GRP), BF16),
                   jax.ShapeDtypeStruct((N_DEV, OUT_SHARD, D_MODEL), BF16),
                   jax.ShapeDtypeStruct((1, D_MODEL), F32), jax.ShapeDtypeStruct((1, D_GRP), F32),
                   jax.ShapeDtypeStruct((1, D_GRP), F32),
                   _view_shape(s_len, 4, BF16), _view_shape(s_len, 4, F32),
                   _view_shape(s_len, 16, BF16), _view_shape(s_len, 16, F32)),
        scratch_shapes=[pltpu.VMEM((D_MODEL, D_MODEL), F32), pltpu.VMEM((8, tm, 128), F32),
                        pltpu.VMEM((8, tm, 128), F32)],
        compiler_params=_params(),
    )(dy, dh2, x1, wn2, b_g, mixed, o_dil, o_sb, wdil, wsb, bd512)


def _qkv_bwd(dq_b, dk_b, dv_b, dqs, dks, dvs, qraw, kraw, cos2, sin2, qnw, knw, bd):
    s_len = qraw.shape[0]
    tm = ROW_TILE
    ni = s_len // tm

    def body(dq1, dk1, dv1, dq4, dk4, dv4, dq16, dk16, dv16, dqs_ref, dks_ref, dvs_ref,
             qraw_ref, kraw_ref, cos_ref, sin_ref, qnw_ref, knw_ref, bd_ref,
             dproj_ref, dqn_ref, dkn_ref, stage, nat4, nat16):
        i = pl.program_id(0)

        @pl.when(i == 0)
        def _():
            dqn_ref[...] = jnp.zeros_like(dqn_ref)
            dkn_ref[...] = jnp.zeros_like(dkn_ref)

        _merge_views((dq4, dk4, dv4), (dq16, dk16, dv16), stage, nat4, nat16)
        cos_t, sin_t, bdm = cos_ref[...], sin_ref[...], bd_ref[...]
        for grp, (part1, raw_ref, nw_ref, dn_ref) in enumerate(((dq1, qraw_ref, qnw_ref, dqn_ref),
                                                                (dk1, kraw_ref, knw_ref, dkn_ref))):
            dn_acc = 0.0
            for p in range(4):
                cols = slice(128 * p, 128 * (p + 1))
                d_rope = part1[:, cols] + nat4[4 * grp + p] + nat16[4 * grp + p]
                d_norm = d_rope * cos_t + _swap_halves(d_rope * sin_t)
                t = raw_ref[:, cols]
                w = nw_ref[...]
                r = lax.rsqrt(_mm_split(t * t, bdm) * (1.0 / HEAD_DIM) + EPS)
                gw = d_norm * w
                corr = _mm_split(gw * t, bdm) * (1.0 / HEAD_DIM)
                dt = r * (gw - t * ((r * r) * corr))
                dn_acc = dn_acc + jnp.sum(d_norm * t * r, axis=0, keepdims=True)
                dproj_ref[:, D_GRP * grp + 128 * p:D_GRP * grp + 128 * (p + 1)] = dt.astype(BF16)
            dn_ref[...] += dn_acc
        dproj_ref[:, 2 * D_GRP:3 * D_GRP] = (dv1[...] + _slab_group(nat4, 2) + _slab_group(nat16, 2)).astype(BF16)
        dproj_ref[:, 3 * D_GRP:4 * D_GRP] = dqs_ref[...].astype(BF16)
        dproj_ref[:, 4 * D_GRP:5 * D_GRP] = dks_ref[...].astype(BF16)
        dproj_ref[:, 5 * D_GRP:6 * D_GRP] = dvs_ref[...].astype(BF16)

    row = lambda w: pl.BlockSpec((tm, w), lambda i: (i, 0))
    return pl.pallas_call(
        body, name="qkv_bwd", grid=(ni,),
        in_specs=[row(D_GRP)] * 3 + [_view_spec(tm, 4)] * 3 + [_view_spec(tm, 16)] * 3 + [row(D_GRP)] * 5
        + [row(128), row(128), _full((1, 128)), _full((1, 128)), _full((128, 128))],
        out_specs=(row(D_IN), _full((1, 128)), _full((1, 128))),
        out_shape=(jax.ShapeDtypeStruct((s_len, D_IN), BF16), jax.ShapeDtypeStruct((1, 128), F32),
                   jax.ShapeDtypeStruct((1, 128), F32)),
        scratch_shapes=[pltpu.VMEM((12, tm, 128), F32)] * 3,
        compiler_params=_params(),
    )(dq_b[0], dk_b[0], dv_b[0], dq_b[1], dk_b[1], dv_b[1], dq_b[2], dk_b[2], dv_b[2],
      dqs, dks, dvs, qraw, kraw, cos2, sin2, qnw, knw, bd)


def _in_bwd_dx(dproj, a_g, x2, dx1, wn1):
    s_len = x2.shape[0]
    tm = ROW_TILE
    ni = s_len // tm

    def body(dp_ref, w_ref, x_ref, dx1_ref, wn_ref, gx_ref, dwn_ref, w_full):
        i = pl.program_id(0)

        @pl.when(i == 0)
        def _():
            dwn_ref[...] = jnp.zeros_like(dwn_ref)
            for d in range(N_DEV):
                w_full[:, IN_SHARD * d:IN_SHARD * (d + 1)] = w_ref[d]

        dh = _dot_nt(dp_ref[...], w_full[...])
        dnorm, dw_rows = _rms_bwd(dh, x_ref[...], wn_ref[...])
        gx_ref[...] = dx1_ref[...] + dnorm
        dwn_ref[...] += jnp.sum(dw_rows, axis=0, keepdims=True)

    row = lambda w: pl.BlockSpec((tm, w), lambda i: (i, 0))
    return pl.pallas_call(
        body, name="in_bwd_dx", grid=(ni,),
        in_specs=[row(D_IN), pl.BlockSpec((N_DEV, D_MODEL, IN_SHARD), lambda i: (0, 0, 0)),
                  row(D_MODEL), row(D_MODEL), _full((1, D_MODEL))],
        out_specs=(row(D_MODEL), _full((1, D_MODEL))),
        out_shape=(jax.ShapeDtypeStruct((s_len, D_MODEL), F32), jax.ShapeDtypeStruct((1, D_MODEL), F32)),
        scratch_shapes=[pltpu.VMEM((D_MODEL, D_IN), BF16)],
        compiler_params=_params(),
    )(dproj, a_g, x2, dx1, wn1)


def _in_bwd_dw(h1, dproj):
    s_len = h1.shape[0]
    tm = ROW_TILE
    ni = s_len // tm

    def body(h_ref, dp_ref, dw_ref, acc):
        i = pl.program_id(1)

        @pl.when(i == 0)
        def _():
            acc[...] = jnp.zeros_like(acc)

        acc[...] += _dot_tn(h_ref[...], dp_ref[...])

        @pl.when(i == ni - 1)
        def _():
            for half in range(2):
                dw_ref[half] = acc[:, IN_SHARD * half:IN_SHARD * (half + 1)].astype(BF16)

    return pl.pallas_call(
        body, name="in_bwd_dw", grid=(N_DEV // 2, ni),
        in_specs=[pl.BlockSpec((tm, D_MODEL), lambda d, i: (i, 0)),
                  pl.BlockSpec((tm, 2 * IN_SHARD), lambda d, i: (i, d))],
        out_specs=pl.BlockSpec((2, D_MODEL, IN_SHARD), lambda d, i: (d, 0, 0)),
        out_shape=jax.ShapeDtypeStruct((N_DEV, D_MODEL, IN_SHARD), BF16),
        scratch_shapes=[pltpu.VMEM((D_MODEL, 2 * IN_SHARD), F32)],
        compiler_params=_params(),
    )(h1, dproj)


def _adamw(recv, w, m, v):
    rows, cols = w.shape
    tr = next((t for t in (128, 32) if rows % t == 0), rows)

    def body(p_ref, w_ref, m_ref, v_ref, g_ref, d_ref, nm_ref, nv_ref):
        g = p_ref[0].astype(F32)
        for s in range(1, N_DEV):
            g = g + p_ref[s].astype(F32)
        m_new = ADAM_B1 * m_ref[...] + (1.0 - ADAM_B1) * g
        v_new = ADAM_B2 * v_ref[...] + (1.0 - ADAM_B2) * (g * g)
        m_hat = m_new / (1.0 - ADAM_B1 ** ADAM_STEP)
        v_hat = v_new / (1.0 - ADAM_B2 ** ADAM_STEP)
        g_ref[...] = g
        d_ref[...] = -ADAM_LR * (m_hat / (jnp.sqrt(v_hat) + ADAM_EPS) + ADAM_WD * w_ref[...])
        nm_ref[...] = m_new
        nv_ref[...] = v_new

    blk = pl.BlockSpec((tr, cols), lambda i: (i, 0))
    out = jax.ShapeDtypeStruct((rows, cols), F32)
    return pl.pallas_call(
        body, name=f"adamw_{rows}x{cols}", grid=(rows // tr,),
        in_specs=[pl.BlockSpec((N_DEV, tr, cols), lambda i: (0, i, 0)), blk, blk, blk],
        out_specs=(blk,) * 4, out_shape=(out,) * 4,
        compiler_params=_params(),
    )(recv, w, m, v)


def _rope_tables(s_len):
    pos = jnp.arange(s_len, dtype=F32)
    inv_freq = ROPE_THETA ** (-jnp.arange(0, HEAD_DIM, 2, dtype=F32) / HEAD_DIM)
    ang = pos[:, None] * inv_freq[None, :]
    cos, sin = jnp.cos(ang), jnp.sin(ang)
    cos2 = jnp.concatenate([cos, cos, cos, cos], axis=1)
    sin2 = jnp.concatenate([-sin, sin, -sin, sin], axis=1)
    return cos2, sin2


def _block_diag_ones(n):
    i = jnp.arange(n)
    return (i[:, None] // HEAD_DIM == i[None, :] // HEAD_DIM).astype(BF16)


def _pad_cols(t):
    return jnp.pad(t, ((0, 0), (0, FF_PAD - FF_SHARD)))


def _pad_rows(t):
    return jnp.pad(t, ((0, FF_PAD - FF_SHARD), (0, 0)))


LOSS_ROW = 26


def _pack_small(n1, n2, ndil, nsb, nq, nk, scalar=None):
    pad = lambda t: jnp.pad(t.reshape(1, -1), ((0, 0), (0, 128 - t.size)))
    last = jnp.zeros((1, 128), F32) if scalar is None else pad(scalar)
    rows = [n1.reshape(8, 128), n2.reshape(8, 128), ndil.reshape(4, 128), nsb.reshape(4, 128),
            pad(nq), pad(nk), last, jnp.zeros((5, 128), F32)]
    return jnp.concatenate(rows, axis=0)


def _unpack_small(t):
    return (t[0:8].reshape(1, D_MODEL), t[8:16].reshape(1, D_MODEL), t[16:20].reshape(1, D_GRP),
            t[20:24].reshape(1, D_GRP), t[24:25, :HEAD_DIM], t[25:26, :HEAD_DIM])


def kernel(x, attn_norm_w, w_in, q_norm_w, k_norm_w, dil_out_norm_w, sb_out_norm_w, w_out, ffn_norm_w, w_gate, w_up, w_down, loss_target, m_attn_norm_w, m_w_in, m_q_norm_w, m_k_norm_w, m_dil_out_norm_w, m_sb_out_norm_w, m_w_out, m_ffn_norm_w, m_w_gate, m_w_up, m_w_down, v_attn_norm_w, v_w_in, v_q_norm_w, v_k_norm_w, v_dil_out_norm_w, v_sb_out_norm_w, v_w_out, v_ffn_norm_w, v_w_gate, v_w_up, v_w_down):
    s_len = x.shape[1]
    x2, tgt = x[0], loss_target[0]

    my_idx = _flat_index(_mesh_pos())
    slot_is_mine = (jnp.arange(N_DEV) == my_idx)[:, None, None]

    (a_g,) = _gather_weights([w_in[0].astype(BF16)])
    w_loc = jnp.concatenate([_pad_cols(w_gate[0]).T, _pad_cols(w_up[0]).T, _pad_rows(w_down[0]), w_out[0]],
                            axis=0).astype(BF16)
    own_in_place = lambda t: lax.dynamic_update_index_in_dim(lax.empty((N_DEV,) + t.shape, t.dtype), t, my_idx, 0)
    w_send, w_recv, w_srcs, w_lands, w_token = _spread_start(
        [w_loc], [own_in_place(w_loc)], blockwise=False, name="weights_start")

    cos2, sin2 = _rope_tables(s_len)
    bd128, bd512 = _block_diag_ones(128), _block_diag_ones(D_GRP)
    idx = jnp.arange(SB_TILE)
    tri_suf = (idx[:, None] > idx[None, :]).astype(BF16)
    tri_pre = (idx[:, None] < idx[None, :]).astype(BF16)
    qnw2 = jnp.concatenate([q_norm_w, q_norm_w], axis=1) + w_token[0:1]
    knw2 = jnp.concatenate([k_norm_w, k_norm_w], axis=1)

    (h1, qraw, kraw, q, k, va, qs, ks, vs,
     q4, k4, v4, q16, k16, v16) = _attn_in(x2, attn_norm_w, a_g, cos2, sin2, qnw2, knw2, bd128)
    qkv_views = {1: (q, k, va), 4: (q4, k4, v4), 16: (q16, k16, v16)}
    o_b, lse_b = [], []
    for r in DILATIONS:
        o, lse = _dil_fwd(*qkv_views[r], r)
        o_b.append(o)
        lse_b.append(lse)
    o_sb, c_sb = _sb_fwd(qs, ks, vs, tri_suf)
    (w_g,) = _spread_wait(w_send, w_recv, w_srcs, w_lands, c_sb, blockwise=False, name="weights_wait")
    o_dil, lse_tot, lse4, lse16, mixed, x1 = _attn_out(o_b, lse_b, o_sb, x2, dil_out_norm_w, sb_out_norm_w, w_g)
    g, u, h2, dy, loss_parts = _ffn_fwd(x1, ffn_norm_w, tgt, w_g)
    loss_local = jnp.sum(loss_parts[::8, 0])

    dg, du, act, dh2 = _ffn_bwd_dx(dy, g, u, w_g)
    (dx1, do_dil, delta, do_sb, dwout, dn2, dndil, dnsb, do4, dl4, do16, dl16) = _attn_out_bwd(
        dy, dh2, x1, ffn_norm_w, w_g, mixed, o_dil, o_sb, dil_out_norm_w, sb_out_norm_w, bd512)
    dwg, dwu, dwd = _ffn_bwd_dw(h2, dy, dg, du, act)
    early = [dwg, dwu, dwd, dwout]
    own_slot_only = lambda t: own_in_place(lax.dynamic_index_in_dim(t, my_idx, 0, keepdims=False))
    g_send, g_recv, g_srcs, g_lands, g_token = _spread_start(
        early, [own_slot_only(t) for t in early], blockwise=True, name="grads_start")
    tri_pre = tri_pre + g_token[0, 0].astype(BF16)
    dqs, dks, dvs = _sb_bwd(qs, ks, vs, do_sb, c_sb, tri_suf, tri_pre)
    cot_views = {1: (do_dil, lse_tot, delta), 4: (do4, lse4, dl4), 16: (do16, lse16, dl16)}
    dq_b, dk_b, dv_b = [], [], []
    for r in DILATIONS:
        dq, dk, dv = _dil_bwd(*qkv_views[r], *cot_views[r], r)
        dq_b.append(dq)
        dk_b.append(dk)
        dv_b.append(dv)
    dproj, dqn2, dkn2 = _qkv_bwd(dq_b, dk_b, dv_b, dqs, dks, dvs, qraw, kraw, cos2, sin2, qnw2, knw2, bd128)
    grad_x, dn1 = _in_bwd_dx(dproj, a_g, x2, dx1, attn_norm_w)
    dwin = _in_bwd_dw(h1, dproj)
    dqn = dqn2[:, :HEAD_DIM] + dqn2[:, HEAD_DIM:]
    dkn = dkn2[:, :HEAD_DIM] + dkn2[:, HEAD_DIM:]

    small = _pack_small(dn1, dn2, dndil, dnsb, dqn, dkn, loss_local)
    (r_small,) = _exchange_grads([], small)
    i_send, i_recv, i_srcs, i_lands, i_token = _spread_start(
        [dwin], [own_slot_only(dwin)], blockwise=True, name="grads_in_start", after=r_small)
    r_gate, r_up, r_down, r_out = _spread_wait(g_send, g_recv, g_srcs, g_lands, i_token, blockwise=True,
                                               name="grads_wait")
    big = {
        "w_gate": tuple(t[:, :FF_SHARD] for t in _adamw(r_gate, _pad_cols(w_gate[0]), _pad_cols(m_w_gate[0]), _pad_cols(v_w_gate[0]))),
        "w_up": tuple(t[:, :FF_SHARD] for t in _adamw(r_up, _pad_cols(w_up[0]), _pad_cols(m_w_up[0]), _pad_cols(v_w_up[0]))),
        "w_down": _adamw(r_down, w_down[0], m_w_down[0], v_w_down[0]),
        "w_out": _adamw(r_out, w_out[0], m_w_out[0], v_w_out[0]),
    }
    packs = [_pack_small(*ts) for ts in (
        (attn_norm_w, ffn_norm_w, dil_out_norm_w, sb_out_norm_w, q_norm_w, k_norm_w),
        (m_attn_norm_w, m_ffn_norm_w, m_dil_out_norm_w, m_sb_out_norm_w, m_q_norm_w, m_k_norm_w),
        (v_attn_norm_w, v_ffn_norm_w, v_dil_out_norm_w, v_sb_out_norm_w, v_q_norm_w, v_k_norm_w))]
    small_raw = _adamw(r_small, *packs)
    loss = small_raw[0][LOSS_ROW, 0]
    small_out = [_unpack_small(t) for t in small_raw]
    (r_in,) = _spread_wait(i_send, i_recv, i_srcs, i_lands, small_out[3][0], blockwise=True,
                           name="grads_in_wait")
    big["w_in"] = _adamw(r_in, w_in[0], m_w_in[0], v_w_in[0])
    names = ["attn_norm_w", "w_in", "q_norm_w", "k_norm_w", "dil_out_norm_w", "sb_out_norm_w", "w_out",
             "ffn_norm_w", "w_gate", "w_up", "w_down"]
    small_pos = {"attn_norm_w": 0, "ffn_norm_w": 1, "dil_out_norm_w": 2, "sb_out_norm_w": 3,
                 "q_norm_w": 4, "k_norm_w": 5}
    outs = [loss, grad_x[None]]
    for kind in range(4):
        for name in names:
            if name in small_pos:
                outs.append(small_out[kind][small_pos[name]])
            else:
                outs.append(big[name][kind][None])
    return tuple(outs)
```

```python
import functools

import jax
import jax.numpy as jnp
from jax import lax
from jax.experimental import pallas as pl
from jax.experimental.pallas import tpu as pltpu

F32 = jnp.float32
BF16 = jnp.bfloat16

N_DEV = 8
D_MODEL = 1024
HEAD_DIM = 64
D_GRP = 512
D_IN = 6 * D_GRP
IN_SHARD = D_IN // N_DEV
FF_SHARD = 352
FF_PAD = 384
FF_BLOCK = 2 * FF_PAD
FF_STEPS = N_DEV // 2
W_PACK_ROWS = 3 * FF_PAD + 128
W_OUT_BLOCK = 3 * FF_PAD // 128
OUT_SHARD = D_MODEL // N_DEV
BLOCK = 128
DILATIONS = (1, 4, 16)
ROPE_THETA = 10000.0
EPS = 1e-6
ATT_SCALE = HEAD_DIM ** -0.5
NEG = -1e30

ADAM_LR = 0.001
ADAM_B1 = 0.9
ADAM_B2 = 0.999
ADAM_EPS = 1e-08
ADAM_WD = 0.01
ADAM_STEP = 10

SB_TILE = 256
SB_PAIRS = 4
SB_BWD_PAIRS = 2
ROW_TILE = 512
VMEM_LIMIT = 56 * 1024 * 1024
MESH = pl.DeviceIdType.MESH


def _dot(a, b):
    return jnp.dot(a, b, preferred_element_type=F32)


def _dot_nt(a, b):
    return lax.dot_general(a, b, (((1,), (1,)), ((), ())), preferred_element_type=F32)


def _dot_tn(a, b):
    return lax.dot_general(a, b, (((0,), (0,)), ((), ())), preferred_element_type=F32)


def _mm_split(t, m):
    hi = t.astype(BF16)
    lo = (t - hi.astype(F32)).astype(BF16)
    return _dot(hi, m) + _dot(lo, m)


def _params(**kw):
    return pltpu.CompilerParams(vmem_limit_bytes=VMEM_LIMIT, **kw)


def _full(shape):
    nd = len(shape)
    return pl.BlockSpec(shape, lambda *_: (0,) * nd)


def _view_shape(s_len, r, dtype):
    return jax.ShapeDtypeStruct((s_len // r, r * D_GRP), dtype)


def _view_spec(tm, r):
    return pl.BlockSpec((tm // r, r * D_GRP), lambda i: (i, 0))


def _swap_halves(t):
    lane = lax.broadcasted_iota(jnp.int32, t.shape, 1)
    first = (lane & 32) == 0
    return jnp.where(first, pltpu.roll(t, 96, 1), pltpu.roll(t, 32, 1))


def _log_sigmoid(z):
    return jnp.minimum(z, 0.0) - jnp.log(1.0 + jnp.exp(-jnp.abs(z)))


def _log_sigmoid_pair(z):
    neg_abs = lax.bitcast_convert_type(lax.bitcast_convert_type(z, jnp.uint32) | jnp.uint32(0x80000000), F32)
    lb = jnp.minimum(z, 0.0) - jnp.log(1.0 + jnp.exp(neg_abs))
    return lb, lb - z


def _cumsum_mm(t, tri):
    return _dot(t.astype(BF16), tri)


def _split_views(src_ref, stage_ref, views4, views16):
    slabs, n, _ = src_ref.shape
    n4, n16 = n // 4, n // 16
    for j in range(slabs):
        g, lanes = j // 4, 128 * (j % 4)
        src, stage = src_ref.at[j], stage_ref.at[j]
        for c4 in range(4):
            blk = src[pl.ds(c4, n4, stride=4), :]
            stage[n4 * c4:n4 * (c4 + 1), :] = blk
            col = D_GRP * c4 + lanes
            views4[g][:, col:col + 128] = blk.astype(views4[g].dtype)
        for c4 in range(4):
            for c1 in range(4):
                blk = stage[pl.ds(n4 * c4 + c1, n16, stride=4), :]
                col = D_GRP * (4 * c1 + c4) + lanes
                views16[g][:, col:col + 128] = blk.astype(views16[g].dtype)


def _merge_views(views4, views16, stage_ref, dst4_ref, dst16_ref):
    slabs, n, _ = dst4_ref.shape
    n4, n16 = n // 4, n // 16
    for j in range(slabs):
        g, lanes = j // 4, 128 * (j % 4)
        dst4, dst16, stage = dst4_ref.at[j], dst16_ref.at[j], stage_ref.at[j]
        for c4 in range(4):
            col = D_GRP * c4 + lanes
            dst4[pl.ds(c4, n4, stride=4), :] = views4[g][:, col:col + 128].astype(F32)
            for c1 in range(4):
                col = D_GRP * (4 * c1 + c4) + lanes
                stage[pl.ds(n4 * c4 + c1, n16, stride=4), :] = views16[g][:, col:col + 128].astype(F32)
        for c4 in range(4):
            dst16[pl.ds(c4, n4, stride=4), :] = stage[n4 * c4:n4 * (c4 + 1), :]


def _slab_group(ref, g):
    return jnp.concatenate([ref[4 * g + p] for p in range(4)], axis=1)


def _mesh_pos():
    return lax.axis_index("x"), lax.axis_index("y"), lax.axis_index("c")


def _flat_index(p):
    return 4 * p[0] + 2 * p[1] + p[2]


def _gather_weights(shards):
    n_arr = len(shards)

    def body(*refs):
        srcs, outs = refs[:n_arr], refs[n_arr:2 * n_arr]
        send_sems, recv_sems, local_sems = refs[2 * n_arr:]
        x, y, c = _mesh_pos()
        me, sibling = (x, y, c), (x, y, 1 - c)
        chips = [(1 - x, y), (x, 1 - y), (1 - x, 1 - y)]

        def copy(arr, k, block, to, own=False):
            dst = outs[arr].at[_flat_index(block)]
            return pltpu.make_async_remote_copy(
                src_ref=srcs[arr] if own else dst, dst_ref=dst,
                send_sem=send_sems.at[arr, k], recv_sem=recv_sems.at[arr, k],
                device_id=to, device_id_type=MESH)

        for arr in range(n_arr):
            mine = pltpu.make_async_copy(srcs[arr], outs[arr].at[_flat_index(me)], local_sems.at[arr])
            mine.start()
            first = [copy(arr, 0, me, sibling, own=True)]
            first += [copy(arr, 1 + j, me, (*chip, c), own=True) for j, chip in enumerate(chips)]
            for cp in first:
                cp.start()
        for arr in range(n_arr):
            passed = [copy(arr, 4 + j, (*chip, c), sibling) for j, chip in enumerate(chips)]
            for j, chip in enumerate(chips):
                copy(arr, 1 + j, (*chip, c), me).wait_recv()
                passed[j].start()
        for arr in range(n_arr):
            copy(arr, 0, sibling, me).wait_recv()
            for j, chip in enumerate(chips):
                copy(arr, 4 + j, (*chip, 1 - c), me).wait_recv()
            for k in range(7):
                copy(arr, k, me, me).wait_send()
            pltpu.make_async_copy(srcs[arr], outs[arr].at[_flat_index(me)], local_sems.at[arr]).wait()

    any_spec = pl.BlockSpec(memory_space=pl.ANY)
    return pl.pallas_call(
        body, name="gather_weights",
        out_shape=tuple(jax.ShapeDtypeStruct((N_DEV,) + s.shape, s.dtype) for s in shards),
        in_specs=[any_spec] * n_arr, out_specs=(any_spec,) * n_arr,
        scratch_shapes=[pltpu.SemaphoreType.DMA((n_arr, 7)), pltpu.SemaphoreType.DMA((n_arr, 7)),
                        pltpu.SemaphoreType.DMA((n_arr,))],
        compiler_params=pltpu.CompilerParams(has_side_effects=True),
    )(*shards)


_HBM_SPEC = pl.BlockSpec(memory_space=pltpu.HBM)
_SEM_SPEC = pl.BlockSpec(memory_space=pltpu.SEMAPHORE)
_DATAFLOW = pltpu.SideEffectType.DATAFLOW_SIDE_EFFECTING


def _peer_list(x, y, c):
    return [(1 - x if m & 4 else x, 1 - y if m & 2 else y, 1 - c if m & 1 else c) for m in range(1, N_DEV)]


def _spread_copies(src_refs, land_refs, send_sems, recv_sems, blockwise):
    x, y, c = _mesh_pos()
    my_idx = _flat_index((x, y, c))
    copies = []
    for a, (src, land) in enumerate(zip(src_refs, land_refs)):
        for k, peer in enumerate(_peer_list(x, y, c)):
            copies.append(pltpu.make_async_remote_copy(
                src_ref=src.at[_flat_index(peer)] if blockwise else src, dst_ref=land.at[my_idx],
                send_sem=send_sems.at[(N_DEV - 1) * a + k], recv_sem=recv_sems.at[(N_DEV - 1) * a + k],
                device_id=peer, device_id_type=MESH))
    return copies


def _spread_start(srcs, lands, blockwise, name, after=None):
    n = len(srcs)
    extra = [] if after is None else [after]

    def body(*refs):
        ins, outs = refs[:2 * n], refs[2 * n + len(extra):]
        for cp in _spread_copies(ins[:n], ins[n:], outs[0], outs[1], blockwise):
            cp.start()
        token = refs[-1]
        token[...] = jnp.zeros_like(token)

    hbm = lambda t: pltpu.HBM(t.shape, t.dtype)
    sems = pltpu.SemaphoreType.DMA((n * (N_DEV - 1),))
    outs = pl.pallas_call(
        body, name=name,
        out_shape=(sems, sems) + tuple(hbm(t) for t in srcs) + tuple(hbm(t) for t in lands)
        + (jax.ShapeDtypeStruct((8, 128), F32),),
        in_specs=[_HBM_SPEC] * (2 * n) + [pl.BlockSpec(memory_space=pl.ANY)] * len(extra),
        out_specs=(_SEM_SPEC, _SEM_SPEC) + (_HBM_SPEC,) * (2 * n) + (pl.BlockSpec(memory_space=pltpu.VMEM),),
        input_output_aliases={i: 2 + i for i in range(2 * n)},
        compiler_params=pltpu.CompilerParams(has_side_effects=_DATAFLOW),
    )(*[pltpu.with_memory_space_constraint(t, pltpu.HBM) for t in list(srcs) + list(lands)], *extra)
    return outs[0], outs[1], outs[2:2 + n], outs[2 + n:2 + 2 * n], outs[-1]


def _spread_wait(send_sems, recv_sems, srcs, lands, after, blockwise, name):
    n = len(srcs)

    def body(*refs):
        for cp in _spread_copies(refs[:n], refs[n:2 * n], refs[2 * n], refs[2 * n + 1], blockwise):
            cp.wait_send()
            cp.wait_recv()

    hbm = lambda t: pltpu.HBM(t.shape, t.dtype)
    outs = pl.pallas_call(
        body, name=name,
        out_shape=tuple(hbm(t) for t in srcs) + tuple(hbm(t) for t in lands),
        in_specs=[_HBM_SPEC] * (2 * n) + [_SEM_SPEC, _SEM_SPEC, pl.BlockSpec(memory_space=pl.ANY)],
        out_specs=(_HBM_SPEC,) * (2 * n),
        input_output_aliases={i: i for i in range(2 * n)},
        compiler_params=pltpu.CompilerParams(has_side_effects=_DATAFLOW),
    )(*srcs, *lands, send_sems, recv_sems, after)
    return outs[n:]


def _exchange_grads(parts, small):
    n_arr = len(parts)

    def body(*refs):
        ins, outs = refs[:n_arr + 1], refs[n_arr + 1:2 * (n_arr + 1)]
        send_sems, recv_sems, local_sems = refs[2 * (n_arr + 1):]
        x, y, c = _mesh_pos()
        me = (x, y, c)
        my_idx = _flat_index(me)
        peers = []
        for m in range(1, N_DEV):
            peers.append((1 - x if m & 4 else x, 1 - y if m & 2 else y, 1 - c if m & 1 else c))

        def src_block(arr, dev):
            return ins[arr] if arr == n_arr else ins[arr].at[_flat_index(dev)]

        def copy(arr, k):
            return pltpu.make_async_remote_copy(
                src_ref=src_block(arr, peers[k]), dst_ref=outs[arr].at[my_idx],
                send_sem=send_sems.at[arr, k], recv_sem=recv_sems.at[arr, k],
                device_id=peers[k], device_id_type=MESH)

        def local(arr):
            return pltpu.make_async_copy(src_block(arr, me), outs[arr].at[my_idx], local_sems.at[arr])

        for arr in range(n_arr + 1):
            local(arr).start()
            for k in range(N_DEV - 1):
                copy(arr, k).start()
        for arr in range(n_arr + 1):
            for k in range(N_DEV - 1):
                cp = copy(arr, k)
                cp.wait_send()
                cp.wait_recv()
            local(arr).wait()

    any_spec = pl.BlockSpec(memory_space=pl.ANY)
    out_shape = tuple(jax.ShapeDtypeStruct(p.shape, p.dtype) for p in parts)
    out_shape += (jax.ShapeDtypeStruct((N_DEV,) + small.shape, small.dtype),)
    return pl.pallas_call(
        body, name="exchange_grads",
        out_shape=out_shape,
        in_specs=[any_spec] * (n_arr + 1), out_specs=(any_spec,) * (n_arr + 1),
        scratch_shapes=[pltpu.SemaphoreType.DMA((n_arr + 1, N_DEV - 1)),
                        pltpu.SemaphoreType.DMA((n_arr + 1, N_DEV - 1)),
                        pltpu.SemaphoreType.DMA((n_arr + 1,))],
        compiler_params=pltpu.CompilerParams(has_side_effects=True),
    )(*parts, small)


def _head_norm(t, w128, bd):
    ms = _mm_split(t * t, bd) * (1.0 / HEAD_DIM)
    r = lax.rsqrt(ms + EPS)
    return (t * r) * w128, r


def _attn_in(x2, wn1, a_g, cos2, sin2, qnw, knw, bd):
    s_len = x2.shape[0]
    tm = ROW_TILE

    def body(x_ref, wn_ref, w_ref, cos_ref, sin_ref, qnw_ref, knw_ref, bd_ref,
             h1_ref, qraw_ref, kraw_ref, q_ref, k_ref, va_ref, qs_ref, ks_ref, vs_ref,
             q4_ref, k4_ref, v4_ref, q16_ref, k16_ref, v16_ref, proj, slabs, stage, w_full):
        @pl.when(pl.program_id(0) == 0)
        def _():
            for d in range(N_DEV):
                w_full[:, IN_SHARD * d:IN_SHARD * (d + 1)] = w_ref[d]

        xx = x_ref[...]
        r = lax.rsqrt(jnp.mean(xx * xx, axis=-1, keepdims=True) + EPS)
        h = ((xx * r) * wn_ref[...]).astype(BF16)
        h1_ref[...] = h
        proj[...] = _dot(h, w_full[...])
        cos_t, sin_t, bdm = cos_ref[...], sin_ref[...], bd_ref[...]
        for grp, (raw_ref, rope_ref, nw_ref) in enumerate(((qraw_ref, q_ref, qnw_ref),
                                                           (kraw_ref, k_ref, knw_ref))):
            for p in range(4):
                cols = slice(D_GRP * grp + 128 * p, D_GRP * grp + 128 * (p + 1))
                t = proj[:, cols]
                raw_ref[:, 128 * p:128 * (p + 1)] = t
                yn, _ = _head_norm(t, nw_ref[...], bdm)
                roped = yn * cos_t + _swap_halves(yn) * sin_t
                slabs[4 * grp + p] = roped
                rope_ref[:, 128 * p:128 * (p + 1)] = roped.astype(BF16)
        for p in range(4):
            slabs[8 + p] = proj[:, 2 * D_GRP + 128 * p:2 * D_GRP + 128 * (p + 1)]
        for grp, ref in ((2, va_ref), (3, qs_ref), (4, ks_ref), (5, vs_ref)):
            ref[...] = proj[:, D_GRP * grp:D_GRP * (grp + 1)].astype(BF16)
        _split_views(slabs, stage, (q4_ref, k4_ref, v4_ref), (q16_ref, k16_ref, v16_ref))

    row = lambda w: pl.BlockSpec((tm, w), lambda i: (i, 0))
    grp_bf = jax.ShapeDtypeStruct((s_len, D_GRP), BF16)
    grp_f32 = jax.ShapeDtypeStruct((s_len, D_GRP), F32)
    return pl.pallas_call(
        body, name="attn_in", grid=(s_len // tm,),
        in_specs=[row(D_MODEL), _full((1, D_MODEL)),
                  pl.BlockSpec((N_DEV, D_MODEL, IN_SHARD), lambda i: (0, 0, 0)),
                  row(128), row(128), _full((1, 128)), _full((1, 128)), _full((128, 128))],
        out_specs=(row(D_MODEL),) + (row(D_GRP),) * 8 + (_view_spec(tm, 4),) * 3 + (_view_spec(tm, 16),) * 3,
        out_shape=(jax.ShapeDtypeStruct((s_len, D_MODEL), BF16), grp_f32, grp_f32) + (grp_bf,) * 6
        + (_view_shape(s_len, 4, BF16),) * 3 + (_view_shape(s_len, 16, BF16),) * 3,
        scratch_shapes=[pltpu.VMEM((tm, D_IN), F32), pltpu.VMEM((12, tm, 128), F32), pltpu.VMEM((12, tm, 128), F32),
                        pltpu.VMEM((D_MODEL, D_IN), BF16)],
        compiler_params=_params(),
    )(x2, wn1, a_g, cos2, sin2, qnw, knw, bd)


def _band_mask(n):
    i = lax.broadcasted_iota(jnp.int32, (2 * BLOCK, 2 * BLOCK), 0) & (BLOCK - 1)
    j = lax.broadcasted_iota(jnp.int32, (2 * BLOCK, 2 * BLOCK), 1)
    dist = i + BLOCK - j
    return (dist >= 0) & (dist <= BLOCK) & ((n - 1) * BLOCK + j >= 0)


def _stack_heads(t2, head0):
    return jnp.concatenate([jnp.where(head0, t2, 0), jnp.where(head0, 0, t2)], axis=0)


def _unstack_heads(t, head0):
    return jnp.where(head0, t[0:BLOCK], t[BLOCK:2 * BLOCK])


def _dil_fwd(qv, kv, vv, r):
    sub_len = qv.shape[0]
    nb = sub_len // BLOCK

    qb = 2 if nb % 2 == 0 else 1

    def body(q_ref, kp_ref, kc_ref, vp_ref, vc_ref, o_ref, lse_ref):
        n = pl.program_id(1)
        lane = lax.broadcasted_iota(jnp.int32, (BLOCK, 128), 1)
        head0 = lane < HEAD_DIM
        units = [(b, slice(128 * p, 128 * (p + 1))) for b in range(qb) for p in range(4)]
        valid = [_band_mask(qb * n + b) for b in range(qb)]
        rows = [slice(BLOCK * b, BLOCK * (b + 1)) for b in range(qb)]

        def keys(prev_ref, cur_ref, b, c):
            before = prev_ref[:, c] if b == 0 else cur_ref[rows[b - 1], c]
            return jnp.concatenate([before, cur_ref[rows[b], c]], axis=0)

        qqs = [_stack_heads(q_ref[rows[b], c] * ATT_SCALE, head0) for b, c in units]
        kks = [keys(kp_ref, kc_ref, b, c) for b, c in units]
        vvs = [keys(vp_ref, vc_ref, b, c) for b, c in units]
        ss = [_dot_nt(qq, kk) for qq, kk in zip(qqs, kks)]
        prs, dens, lses = [], [], []
        for (b, _), s in zip(units, ss):
            s = jnp.where(valid[b], s, NEG)
            m = jnp.max(s, axis=-1, keepdims=True)
            pr = jnp.exp(s - m)
            den = jnp.sum(pr, axis=-1, keepdims=True)
            prs.append(pr.astype(BF16))
            dens.append(den)
            lses.append(m + jnp.log(den))
        pvs = [_dot(pr, vv2) for pr, vv2 in zip(prs, vvs)]
        for (b, c), pv, den, lse in zip(units, pvs, dens, lses):
            o_ref[rows[b], c] = _unstack_heads(pv / den, head0)
            lse_ref[rows[b], c] = _unstack_heads(jnp.broadcast_to(lse, (2 * BLOCK, 128)), head0)

    cur = pl.BlockSpec((qb * BLOCK, D_GRP), lambda c, n: (n, c))
    prev = pl.BlockSpec((BLOCK, D_GRP), lambda c, n: (jnp.maximum(qb * n - 1, 0), c))
    out = jax.ShapeDtypeStruct(qv.shape, F32)
    return pl.pallas_call(
        body, name=f"dil_fwd_r{r}", grid=(r, nb // qb),
        in_specs=[cur, prev, cur, prev, cur], out_specs=(cur, cur), out_shape=(out, out),
        compiler_params=_params(),
    )(qv, kv, kv, vv, vv)


def _dil_bwd(qv, kv, vv, dov, lsev, deltav, r):
    sub_len = qv.shape[0]
    nb = sub_len // BLOCK

    def body(q_ref, kp_ref, kc_ref, vp_ref, vc_ref, do_ref, lse_ref, dl_ref,
             dq_ref, dk_ref, dv_ref, dk_carry, dv_carry):
        n = pl.program_id(1)

        @pl.when(n == 0)
        def _():
            dk_carry[...] = jnp.zeros_like(dk_carry)
            dv_carry[...] = jnp.zeros_like(dv_carry)

        @pl.when(n < nb)
        def _():
            valid = _band_mask(n)
            lane = lax.broadcasted_iota(jnp.int32, (BLOCK, 128), 1)
            head0 = lane < HEAD_DIM
            pairs = [slice(128 * p, 128 * (p + 1)) for p in range(4)]
            qqs = [_stack_heads(q_ref[:, c] * ATT_SCALE, head0) for c in pairs]
            dos = [_stack_heads(do_ref[:, c], head0) for c in pairs]
            kks = [jnp.concatenate([kp_ref[:, c], kc_ref[:, c]], axis=0) for c in pairs]
            vvs = [jnp.concatenate([vp_ref[:, c], vc_ref[:, c]], axis=0) for c in pairs]
            ss = [_dot_nt(qq, kk) for qq, kk in zip(qqs, kks)]
            dps = [_dot_nt(do, vv2) for do, vv2 in zip(dos, vvs)]
            prs, dss = [], []
            for c, s, dp in zip(pairs, ss, dps):
                stats = []
                for ref in (lse_ref, dl_ref):
                    t2 = ref[:, c]
                    stats.append(jnp.concatenate(
                        [jnp.sum(jnp.where(lane == 0, t2, 0.0), axis=-1, keepdims=True),
                         jnp.sum(jnp.where(lane == HEAD_DIM, t2, 0.0), axis=-1, keepdims=True)], axis=0))
                pr = jnp.where(valid, jnp.exp(jnp.minimum(s - stats[0], 0.0)), 0.0)
                prs.append(pr.astype(BF16))
                dss.append((pr * (dp - stats[1])).astype(BF16))
            dqs = [_dot(ds, kk) for ds, kk in zip(dss, kks)]
            dkks = [_dot_tn(ds, qq) for ds, qq in zip(dss, qqs)]
            dvvs = [_dot_tn(pr, do) for pr, do in zip(prs, dos)]
            for c, dq, dkk, dvv in zip(pairs, dqs, dkks, dvvs):
                dq_ref[:, c] = _unstack_heads(dq, head0) * ATT_SCALE
                dk_ref[:, c] = dk_carry[:, c] + dkk[:BLOCK]
                dv_ref[:, c] = dv_carry[:, c] + dvv[:BLOCK]
                dk_carry[:, c] = dkk[BLOCK:]
                dv_carry[:, c] = dvv[BLOCK:]

        @pl.when(n == nb)
        def _():
            dk_ref[...] = dk_carry[...]
            dv_ref[...] = dv_carry[...]

    last = nb - 1
    cur = pl.BlockSpec((BLOCK, D_GRP), lambda c, n: (jnp.minimum(n, last), c))
    prev = pl.BlockSpec((BLOCK, D_GRP), lambda c, n: (jnp.clip(n - 1, 0, last), c))
    out = jax.ShapeDtypeStruct(qv.shape, F32)
    return pl.pallas_call(
        body, name=f"dil_bwd_r{r}", grid=(r, nb + 1),
        in_specs=[cur, prev, cur, prev, cur, cur, cur, cur],
        out_specs=(cur, prev, prev), out_shape=(out, out, out),
        scratch_shapes=[pltpu.VMEM((BLOCK, D_GRP), F32), pltpu.VMEM((BLOCK, D_GRP), F32)],
        compiler_params=_params(),
    )(qv, kv, kv, vv, vv, dov, lsev, deltav)


def _sb_fwd(qs, ks, vs, tri_suf):
    s_len = qs.shape[0]
    t = SB_TILE
    nq = s_len // t

    npair = SB_PAIRS

    def body(q_ref, k_ref, v_ref, u_ref, o_ref, c_ref, qq, vt, acc, cf, csave):
        row = lax.broadcasted_iota(jnp.int32, (2 * t, t), 0) & (t - 1)
        col = lax.broadcasted_iota(jnp.int32, (2 * t, t), 1)
        diag_mask = col < row
        lane1 = lax.broadcasted_iota(jnp.int32, (t, 128), 1)
        head0 = lane1 < HEAD_DIM
        lane2 = lax.broadcasted_iota(jnp.int32, (2 * t, 128), 1)
        uu = u_ref[...]
        pr = range(npair)
        cols = [slice(128 * pp, 128 * (pp + 1)) for pp in pr]

        i = pl.program_id(1)

        @pl.when(i == 0)
        def _():
            def transpose_v(j, _):
                rows = pl.ds(pl.multiple_of(j * t, t), t)
                for pp in pr:
                    vt[pp, j] = v_ref[rows, cols[pp]].astype(F32).T.astype(BF16)
                return 0

            lax.fori_loop(0, nq, transpose_v, 0)

        for pp in pr:
            q2 = q_ref[:, cols[pp]] * ATT_SCALE
            qq[pp, 0:t, :] = jnp.where(head0, q2, 0)
            qq[pp, t:2 * t, :] = jnp.where(head0, 0, q2)
        acc[...] = jnp.zeros_like(acc)
        cf[...] = jnp.zeros_like(cf)
        csave[...] = jnp.zeros_like(csave)

        def tile(kb, diag):
            krows = pl.ds(pl.multiple_of(kb * t, t), t)
            zs = [_dot_nt(qq[pp], k_ref[krows, cols[pp]]) for pp in pr]
            lbk = [_log_sigmoid_pair(z) for z in zs]
            lks = [jnp.where(diag_mask, lk, 0.0) if diag else lk for _, lk in lbk]
            sufs = [_cumsum_mm(lk, uu) for lk in lks]
            carries = [cf[pp] for pp in pr]
            avs = []
            for pp in pr:
                a = jnp.exp(lbk[pp][0] + (sufs[pp] + jnp.concatenate([carries[pp]] * (t // 128), axis=1)))
                avs.append((jnp.where(diag_mask, a, 0.0) if diag else a).astype(BF16))
            pvs = [_dot_nt(vt[pp, kb], avs[pp]) for pp in pr]
            for pp in pr:
                acc[pp] += pvs[pp]
                csave[pp] = jnp.where(lane2 == kb, carries[pp], csave[pp])
                cf[pp] = carries[pp] + jnp.broadcast_to(jnp.sum(lks[pp], axis=-1, keepdims=True), (2 * t, 128))

        tile(i, True)

        def k_block(step, _):
            tile(i - 1 - step, False)
            return 0

        lax.fori_loop(0, i, k_block, 0)
        for pp in pr:
            o_ref[:, cols[pp]] = jnp.where(head0, acc[pp, :, 0:t].T, acc[pp, :, t:2 * t].T)
            c_ref[2 * pp] = csave[pp, 0:t, :]
            c_ref[2 * pp + 1] = csave[pp, t:2 * t, :]

    width = 128 * npair
    kv = pl.BlockSpec((s_len, width), lambda p, i: (0, p))
    qo = pl.BlockSpec((t, width), lambda p, i: (i, p))
    return pl.pallas_call(
        body, name="sb_fwd", grid=(4 // npair, nq),
        in_specs=[qo, kv, kv, pl.BlockSpec((t, t), lambda p, i: (0, 0))],
        out_specs=(qo, pl.BlockSpec((2 * npair, t, 128), lambda p, i: (p, i, 0))),
        out_shape=(jax.ShapeDtypeStruct((s_len, D_GRP), F32),
                   jax.ShapeDtypeStruct((8, s_len, 128), F32)),
        scratch_shapes=[pltpu.VMEM((npair, 2 * t, 128), BF16), pltpu.VMEM((npair, nq, 128, t), BF16),
                        pltpu.VMEM((npair, 128, 2 * t), F32),
                        pltpu.VMEM((npair, 2 * t, 128), F32), pltpu.VMEM((npair, 2 * t, 128), F32)],
        compiler_params=_params(),
    )(qs, ks, vs, tri_suf)


def _sb_bwd(qs, ks, vs, dos, csaved, tri_suf, tri_pre):
    s_len = qs.shape[0]
    t = SB_TILE
    nq = s_len // t

    npair = SB_BWD_PAIRS

    def body(q_ref, k_ref, v_ref, do_ref, c_ref, u_ref, p_ref, dq_ref, dk_ref, dv_ref,
             qq, dd, qqt, ddt, kt, dq_acc, dkt, dvt, cg):
        row = lax.broadcasted_iota(jnp.int32, (2 * t, t), 0) & (t - 1)
        col = lax.broadcasted_iota(jnp.int32, (2 * t, t), 1)
        diag_mask = col < row
        lane1 = lax.broadcasted_iota(jnp.int32, (t, 128), 1)
        head0 = lane1 < HEAD_DIM
        lane2 = lax.broadcasted_iota(jnp.int32, (2 * t, 128), 1)
        uu, pm = u_ref[...], p_ref[...]
        pr = range(npair)
        cols = [slice(128 * pp, 128 * (pp + 1)) for pp in pr]
        i = pl.program_id(1)

        @pl.when(i == 0)
        def _():
            dkt[...] = jnp.zeros_like(dkt)
            dvt[...] = jnp.zeros_like(dvt)

            def transpose_k(j, _):
                rows = pl.ds(pl.multiple_of(j * t, t), t)
                for pp in pr:
                    kt[pp, j] = k_ref[rows, cols[pp]].astype(F32).T.astype(BF16)
                return 0

            lax.fori_loop(0, nq, transpose_k, 0)

        for pp in pr:
            q2 = q_ref[:, cols[pp]].astype(F32) * ATT_SCALE
            do2 = do_ref[:, cols[pp]].astype(F32)
            for src, nat, tr in ((q2, qq, qqt), (do2, dd, ddt)):
                stacked = jnp.concatenate([jnp.where(head0, src, 0.0), jnp.where(head0, 0.0, src)], axis=0)
                nat[pp] = stacked.astype(BF16)
                tr[pp] = stacked.T.astype(BF16)
        dq_acc[...] = jnp.zeros_like(dq_acc)
        cg[...] = jnp.zeros_like(cg)

        def tile(kb, diag):
            krows = pl.ds(pl.multiple_of(kb * t, t), t)
            zs = [_dot_nt(qq[pp], k_ref[krows, cols[pp]]) for pp in pr]
            das = [_dot_nt(dd[pp], v_ref[krows, cols[pp]]) for pp in pr]
            lbk = [_log_sigmoid_pair(z) for z in zs]
            lks = [jnp.where(diag_mask, lk, 0.0) if diag else lk for _, lk in lbk]
            sufs = [_cumsum_mm(lk, uu) for lk in lks]
            avs, gs = [], []
            for pp in pr:
                cs = jnp.concatenate([c_ref[2 * pp], c_ref[2 * pp + 1]], axis=0)
                cf = jnp.sum(jnp.where(lane2 == kb, cs, 0.0), axis=-1, keepdims=True)
                a = jnp.exp(lbk[pp][0] + (sufs[pp] + cf))
                a = jnp.where(diag_mask, a, 0.0) if diag else a
                avs.append(a.astype(BF16))
                gs.append(a * das[pp])
            gpres = [_cumsum_mm(g, pm) for g in gs]
            dzs = []
            for pp in pr:
                carry = cg[pp]
                beta = jnp.exp(lbk[pp][0])
                dz = gs[pp] - beta * (gs[pp] + (gpres[pp] + jnp.concatenate([carry] * (t // 128), axis=1)))
                dzs.append((jnp.where(diag_mask, dz, 0.0) if diag else dz).astype(BF16))
                cg[pp] = carry + jnp.broadcast_to(jnp.sum(gs[pp], axis=-1, keepdims=True), (2 * t, 128))
            dqs = [_dot_nt(kt[pp, kb], dzs[pp]) for pp in pr]
            dks = [_dot(qqt[pp], dzs[pp]) for pp in pr]
            dvs = [_dot(ddt[pp], avs[pp]) for pp in pr]
            for pp in pr:
                dq_acc[pp] += dqs[pp]
                dkt[pp, kb] += dks[pp]
                dvt[pp, kb] += dvs[pp]

        def k_block(kb, _):
            tile(kb, False)
            return 0

        lax.fori_loop(0, i, k_block, 0)
        tile(i, True)
        for pp in pr:
            dq_ref[:, cols[pp]] = jnp.where(head0, dq_acc[pp, :, 0:t].T, dq_acc[pp, :, t:2 * t].T) * ATT_SCALE

        @pl.when(i == nq - 1)
        def _():
            def untranspose(j, _):
                rows = pl.ds(pl.multiple_of(j * t, t), t)
                for pp in pr:
                    dk_ref[rows, cols[pp]] = dkt[pp, j].T
                    dv_ref[rows, cols[pp]] = dvt[pp, j].T
                return 0

            lax.fori_loop(0, nq, untranspose, 0)

    width = 128 * npair
    kv = pl.BlockSpec((s_len, width), lambda p, i: (0, p))
    qo = pl.BlockSpec((t, width), lambda p, i: (i, p))
    tri = pl.BlockSpec((t, t), lambda p, i: (0, 0))
    out = jax.ShapeDtypeStruct((s_len, D_GRP), F32)
    return pl.pallas_call(
        body, name="sb_bwd", grid=(4 // npair, nq),
        in_specs=[qo, kv, kv, qo, pl.BlockSpec((2 * npair, t, 128), lambda p, i: (p, i, 0)), tri, tri],
        out_specs=(qo, kv, kv), out_shape=(out, out, out),
        scratch_shapes=[pltpu.VMEM((npair, 2 * t, 128), BF16), pltpu.VMEM((npair, 2 * t, 128), BF16),
                        pltpu.VMEM((npair, 128, 2 * t), BF16), pltpu.VMEM((npair, 128, 2 * t), BF16),
                        pltpu.VMEM((npair, nq, 128, t), BF16),
                        pltpu.VMEM((npair, 128, 2 * t), F32),
                        pltpu.VMEM((npair, nq, 128, t), F32), pltpu.VMEM((npair, nq, 128, t), F32),
                        pltpu.VMEM((npair, 2 * t, 128), F32)],
        compiler_params=_params(),
    )(qs, ks, vs, dos, csaved, tri_suf, tri_pre)


def _attn_out(o_b, lse_b, o_sb, x2, wdil, wsb, b_g):
    s_len = x2.shape[0]
    tm = ROW_TILE

    def body(o1_ref, l1_ref, o4_ref, l4_ref, o16_ref, l16_ref, osb_ref, x_ref, wdil_ref, wsb_ref, w_ref,
             odil_ref, lse_ref, lse4_ref, lse16_ref, mixed_ref, x1_ref, stage, nat4, nat16):
        _merge_views((o4_ref, l4_ref), (o16_ref, l16_ref), stage, nat4, nat16)
        os_ = (o1_ref[...], _slab_group(nat4, 0), _slab_group(nat16, 0))
        ls = (l1_ref[...], _slab_group(nat4, 1), _slab_group(nat16, 1))
        mx = jnp.maximum(jnp.maximum(ls[0], ls[1]), ls[2])
        es = [jnp.exp(l - mx) for l in ls]
        den = es[0] + es[1] + es[2]
        o_dil = (es[0] * os_[0] + es[1] * os_[1] + es[2] * os_[2]) / den
        odil_ref[...] = o_dil
        lse = mx + jnp.log(den)
        lse_ref[...] = lse
        for p in range(4):
            nat4[p] = lse[:, 128 * p:128 * (p + 1)]
        _split_views(nat4.at[0:4], stage.at[0:4], (lse4_ref,), (lse16_ref,))
        halves = []
        for t, w_r in ((o_dil, wdil_ref), (osb_ref[...], wsb_ref)):
            r = lax.rsqrt(jnp.mean(t * t, axis=-1, keepdims=True) + EPS)
            halves.append(((t * r) * w_r[...]).astype(BF16))
        mixed = jnp.concatenate(halves, axis=1)
        mixed_ref[...] = mixed
        w = w_ref[...].reshape(D_MODEL, D_MODEL)
        x1_ref[...] = x_ref[...] + _dot(mixed, w)

    row = lambda w: pl.BlockSpec((tm, w), lambda i: (i, 0))
    return pl.pallas_call(
        body, name="attn_out", grid=(s_len // tm,),
        in_specs=[row(D_GRP)] * 2 + [_view_spec(tm, 4)] * 2 + [_view_spec(tm, 16)] * 2
        + [row(D_GRP), row(D_MODEL), _full((1, D_GRP)), _full((1, D_GRP)),
           pl.BlockSpec((N_DEV, OUT_SHARD, D_MODEL), lambda i: (0, W_OUT_BLOCK, 0))],
        out_specs=(row(D_GRP), row(D_GRP), _view_spec(tm, 4), _view_spec(tm, 16), row(D_MODEL), row(D_MODEL)),
        out_shape=(jax.ShapeDtypeStruct((s_len, D_GRP), F32), jax.ShapeDtypeStruct((s_len, D_GRP), F32),
                   _view_shape(s_len, 4, F32), _view_shape(s_len, 16, F32),
                   jax.ShapeDtypeStruct((s_len, D_MODEL), BF16), jax.ShapeDtypeStruct((s_len, D_MODEL), F32)),
        scratch_shapes=[pltpu.VMEM((8, tm, 128), F32)] * 3,
        compiler_params=_params(),
    )(o_b[0], lse_b[0], o_b[1], lse_b[1], o_b[2], lse_b[2], o_sb, x2, wdil, wsb, b_g)


def _two_shards(w_ref):
    return w_ref[...].reshape(FF_BLOCK, D_MODEL)


def _ffn_fwd(x1, wn2, tgt, w_g):
    s_len = x1.shape[0]
    tm = ROW_TILE
    ni = s_len // tm

    def body(x_ref, wn_ref, t_ref, wg_ref, wu_ref, wd_ref, g_ref, u_ref, h2_ref, dy_ref, loss_ref, acc):
        j = pl.program_id(1)

        @pl.when(j == 0)
        def _():
            xx = x_ref[...]
            r = lax.rsqrt(jnp.mean(xx * xx, axis=-1, keepdims=True) + EPS)
            h2_ref[...] = ((xx * r) * wn_ref[...]).astype(BF16)
            acc[...] = jnp.zeros_like(acc)

        h = h2_ref[...]
        g = _dot_nt(h, _two_shards(wg_ref))
        u = _dot_nt(h, _two_shards(wu_ref))
        g_ref[...] = g
        u_ref[...] = u
        act = (g * (1.0 / (1.0 + jnp.exp(-g)))) * u
        acc[...] += _dot(act.astype(BF16), _two_shards(wd_ref))

        @pl.when(j == FF_STEPS - 1)
        def _():
            err = (x_ref[...] + acc[...]) - t_ref[...]
            dy_ref[...] = err * (1.0 / D_MODEL)
            part = 0.5 * jnp.sum(jnp.mean(err * err, axis=-1, keepdims=True))
            loss_ref[...] = jnp.full((8, 128), part, F32)

    row = pl.BlockSpec((tm, D_MODEL), lambda i, j: (i, 0))
    hid = pl.BlockSpec((tm, FF_BLOCK), lambda i, j: (i, j))
    return pl.pallas_call(
        body, name="ffn_fwd", grid=(ni, FF_STEPS),
        in_specs=[row, pl.BlockSpec((1, D_MODEL), lambda i, j: (0, 0)), row,
                  pl.BlockSpec((2, FF_PAD, D_MODEL), lambda i, j: (j, 0, 0)),
                  pl.BlockSpec((2, FF_PAD, D_MODEL), lambda i, j: (j, 1, 0)),
                  pl.BlockSpec((2, FF_PAD, D_MODEL), lambda i, j: (j, 2, 0))],
        out_specs=(hid, hid, row, row, pl.BlockSpec((8, 128), lambda i, j: (i, 0))),
        out_shape=(jax.ShapeDtypeStruct((s_len, N_DEV * FF_PAD), F32),
                   jax.ShapeDtypeStruct((s_len, N_DEV * FF_PAD), F32),
                   jax.ShapeDtypeStruct((s_len, D_MODEL), BF16),
                   jax.ShapeDtypeStruct((s_len, D_MODEL), F32),
                   jax.ShapeDtypeStruct((ni * 8, 128), F32)),
        scratch_shapes=[pltpu.VMEM((tm, D_MODEL), F32)],
        compiler_params=_params(),
    )(x1, wn2, tgt, w_g, w_g, w_g)


def _ffn_bwd_dx(dy, g, u, w_g):
    s_len = dy.shape[0]
    tm = ROW_TILE

    def body(dy_ref, g_ref, u_ref, wg_ref, wu_ref, wd_ref, dg_ref, du_ref, act_ref, dh_ref, acc):
        j = pl.program_id(1)

        @pl.when(j == 0)
        def _():
            acc[...] = jnp.zeros_like(acc)

        gg, uu = g_ref[...], u_ref[...]
        da = _dot_nt(dy_ref[...].astype(BF16), _two_shards(wd_ref))
        sig = 1.0 / (1.0 + jnp.exp(-gg))
        silu = gg * sig
        act_ref[...] = (silu * uu).astype(BF16)
        du = (da * silu).astype(BF16)
        dg = (da * uu * (sig * (1.0 + gg * (1.0 - sig)))).astype(BF16)
        du_ref[...] = du
        dg_ref[...] = dg
        acc[...] += _dot(dg, _two_shards(wg_ref)) + _dot(du, _two_shards(wu_ref))

        @pl.when(j == FF_STEPS - 1)
        def _():
            dh_ref[...] = acc[...]

    row = pl.BlockSpec((tm, D_MODEL), lambda i, j: (i, 0))
    hid = pl.BlockSpec((tm, FF_BLOCK), lambda i, j: (i, j))
    hid_bf = jax.ShapeDtypeStruct((s_len, N_DEV * FF_PAD), BF16)
    return pl.pallas_call(
        body, name="ffn_bwd_dx", grid=(s_len // tm, FF_STEPS),
        in_specs=[row, hid, hid,
                  pl.BlockSpec((2, FF_PAD, D_MODEL), lambda i, j: (j, 0, 0)),
                  pl.BlockSpec((2, FF_PAD, D_MODEL), lambda i, j: (j, 1, 0)),
                  pl.BlockSpec((2, FF_PAD, D_MODEL), lambda i, j: (j, 2, 0))],
        out_specs=(hid, hid, hid, row),
        out_shape=(hid_bf, hid_bf, hid_bf, jax.ShapeDtypeStruct((s_len, D_MODEL), F32)),
        scratch_shapes=[pltpu.VMEM((tm, D_MODEL), F32)],
        compiler_params=_params(),
    )(dy, g, u, w_g, w_g, w_g)


def _ffn_bwd_dw(h2, dy, dg, du, act):
    s_len = h2.shape[0]
    tm = ROW_TILE
    ni = s_len // tm

    def body(h_ref, dy_ref, dg_ref, du_ref, act_ref, dwg_ref, dwu_ref, dwd_ref, ag, au, ad):
        i = pl.program_id(1)

        @pl.when(i == 0)
        def _():
            ag[...] = jnp.zeros_like(ag)
            au[...] = jnp.zeros_like(au)
            ad[...] = jnp.zeros_like(ad)

        h = h_ref[...]
        ag[...] += _dot_tn(h, dg_ref[...])
        au[...] += _dot_tn(h, du_ref[...])
        ad[...] += _dot_tn(act_ref[...], dy_ref[...].astype(BF16))

        @pl.when(i == ni - 1)
        def _():
            for half in range(2):
                cols = slice(FF_PAD * half, FF_PAD * (half + 1))
                dwg_ref[half] = ag[:, cols].astype(BF16)
                dwu_ref[half] = au[:, cols].astype(BF16)
            dwd_ref[...] = ad[...].astype(BF16).reshape(2, FF_PAD, D_MODEL)

    row = pl.BlockSpec((tm, D_MODEL), lambda j, i: (i, 0))
    hid = pl.BlockSpec((tm, FF_BLOCK), lambda j, i: (i, j))
    col_w = pl.BlockSpec((2, D_MODEL, FF_PAD), lambda j, i: (j, 0, 0))
    row_w = pl.BlockSpec((2, FF_PAD, D_MODEL), lambda j, i: (j, 0, 0))
    return pl.pallas_call(
        body, name="ffn_bwd_dw", grid=(FF_STEPS, ni),
        in_specs=[row, row, hid, hid, hid], out_specs=(col_w, col_w, row_w),
        out_shape=(jax.ShapeDtypeStruct((N_DEV, D_MODEL, FF_PAD), BF16),
                   jax.ShapeDtypeStruct((N_DEV, D_MODEL, FF_PAD), BF16),
                   jax.ShapeDtypeStruct((N_DEV, FF_PAD, D_MODEL), BF16)),
        scratch_shapes=[pltpu.VMEM((D_MODEL, FF_BLOCK), F32), pltpu.VMEM((D_MODEL, FF_BLOCK), F32),
                        pltpu.VMEM((FF_BLOCK, D_MODEL), F32)],
        compiler_params=_params(),
    )(h2, dy, dg, du, act)


def _rms_bwd(dy, t, w):
    r = lax.rsqrt(jnp.mean(t * t, axis=-1, keepdims=True) + EPS)
    gw = dy * w
    dt = r * (gw - t * ((r * r) * jnp.mean(gw * t, axis=-1, keepdims=True)))
    return dt, dy * t * r


def _attn_out_bwd(dy, dh2, x1, wn2, b_g, mixed, o_dil, o_sb, wdil, wsb, bd512):
    s_len = dy.shape[0]
    tm = ROW_TILE
    ni = s_len // tm

    def body(dy_ref, dh_ref, x1_ref, wn_ref, w_ref, mixed_ref, odil_ref, osb_ref, wdil_ref, wsb_ref, bd_ref,
             dx1_ref, dodil_ref, delta_ref, dosb_ref, dwout_ref, dwn_ref, dwdil_ref, dwsb_ref,
             do4_ref, dl4_ref, do16_ref, dl16_ref, wacc, both, stage):
        i = pl.program_id(0)

        @pl.when(i == 0)
        def _():
            wacc[...] = jnp.zeros_like(wacc)
            dwn_ref[...] = jnp.zeros_like(dwn_ref)
            dwdil_ref[...] = jnp.zeros_like(dwdil_ref)
            dwsb_ref[...] = jnp.zeros_like(dwsb_ref)

        dnorm, dw_rows = _rms_bwd(dh_ref[...], x1_ref[...], wn_ref[...])
        dx1 = dy_ref[...] + dnorm
        dx1_ref[...] = dx1
        dwn_ref[...] += jnp.sum(dw_rows, axis=0, keepdims=True)
        dx1b = dx1.astype(BF16)
        w = w_ref[...].reshape(D_MODEL, D_MODEL)
        dmixed = _dot_nt(dx1b, w)
        wacc[...] += _dot_tn(mixed_ref[...], dx1b)
        o_dil = odil_ref[...]
        d_odil, dw_rows = _rms_bwd(dmixed[:, :D_GRP], o_dil, wdil_ref[...])
        dwdil_ref[...] += jnp.sum(dw_rows, axis=0, keepdims=True)
        dodil_ref[...] = d_odil.astype(BF16)
        delta = _mm_split(d_odil * o_dil, bd_ref[...])
        delta_ref[...] = delta
        for p in range(4):
            both[p] = d_odil[:, 128 * p:128 * (p + 1)]
            both[4 + p] = delta[:, 128 * p:128 * (p + 1)]
        _split_views(both, stage, (do4_ref, dl4_ref), (do16_ref, dl16_ref))
        d_osb, dw_rows = _rms_bwd(dmixed[:, D_GRP:], osb_ref[...], wsb_ref[...])
        dwsb_ref[...] += jnp.sum(dw_rows, axis=0, keepdims=True)
        dosb_ref[...] = d_osb.astype(BF16)

        @pl.when(i == ni - 1)
        def _():
            dwout_ref[...] = wacc[...].astype(BF16).reshape(N_DEV, OUT_SHARD, D_MODEL)

    row = lambda w: pl.BlockSpec((tm, w), lambda i: (i, 0))
    return pl.pallas_call(
        body, name="attn_out_bwd", grid=(ni,),
        in_specs=[row(D_MODEL), row(D_MODEL), row(D_MODEL), _full((1, D_MODEL)),
                  pl.BlockSpec((N_DEV, OUT_SHARD, D_MODEL), lambda i: (0, W_OUT_BLOCK, 0)),
                  row(D_MODEL), row(D_GRP), row(D_GRP), _full((1, D_GRP)), _full((1, D_GRP)),
                  _full((D_GRP, D_GRP))],
        out_specs=(row(D_MODEL), row(D_GRP), row(D_GRP), row(D_GRP),
                   _full((N_DEV, OUT_SHARD, D_MODEL)), _full((1, D_MODEL)), _full((1, D_GRP)), _full((1, D_GRP)),
                   _view_spec(tm, 4), _view_spec(tm, 4), _view_spec(tm, 16), _view_spec(tm, 16)),
        out_shape=(jax.ShapeDtypeStruct((s_len, D_MODEL), F32), jax.ShapeDtypeStruct((s_len, D_GRP), BF16),
                   jax.ShapeDtypeStruct((s_len, D_GRP), F32), jax.ShapeDtypeStruct((s_len, D_GRP), BF16),
                   jax.ShapeDtypeStruct((N_DEV, OUT_SHARD, D_MODEL), BF16),
                   jax.ShapeDtypeStruct((1, D_MODEL), F32), jax.ShapeDtypeStruct((1, D_GRP), F32),
                   jax.ShapeDtypeStruct((1, D_GRP), F32),
                   _view_shape(s_len, 4, BF16), _view_shape(s_len, 4, F32),
                   _view_shape(s_len, 16, BF16), _view_shape(s_len, 16, F32)),
        scratch_shapes=[pltpu.VMEM((D_MODEL, D_MODEL), F32), pltpu.VMEM((8, tm, 128), F32),
                        pltpu.VMEM((8, tm, 128), F32)],
        compiler_params=_params(),
    )(dy, dh2, x1, wn2, b_g, mixed, o_dil, o_sb, wdil, wsb, bd512)


def _qkv_bwd(dq_b, dk_b, dv_b, dqs, dks, dvs, qraw, kraw, cos2, sin2, qnw, knw, bd):
    s_len = qraw.shape[0]
    tm = ROW_TILE
    ni = s_len // tm

    def body(dq1, dk1, dv1, dq4, dk4, dv4, dq16, dk16, dv16, dqs_ref, dks_ref, dvs_ref,
             qraw_ref, kraw_ref, cos_ref, sin_ref, qnw_ref, knw_ref, bd_ref,
             dproj_ref, dqn_ref, dkn_ref, stage, nat4, nat16):
        i = pl.program_id(0)

        @pl.when(i == 0)
        def _():
            dqn_ref[...] = jnp.zeros_like(dqn_ref)
            dkn_ref[...] = jnp.zeros_like(dkn_ref)

        _merge_views((dq4, dk4, dv4), (dq16, dk16, dv16), stage, nat4, nat16)
        cos_t, sin_t, bdm = cos_ref[...], sin_ref[...], bd_ref[...]
        for grp, (part1, raw_ref, nw_ref, dn_ref) in enumerate(((dq1, qraw_ref, qnw_ref, dqn_ref),
                                                                (dk1, kraw_ref, knw_ref, dkn_ref))):
            dn_acc = 0.0
            for p in range(4):
                cols = slice(128 * p, 128 * (p + 1))
                d_rope = part1[:, cols] + nat4[4 * grp + p] + nat16[4 * grp + p]
                d_norm = d_rope * cos_t + _swap_halves(d_rope * sin_t)
                t = raw_ref[:, cols]
                w = nw_ref[...]
                r = lax.rsqrt(_mm_split(t * t, bdm) * (1.0 / HEAD_DIM) + EPS)
                gw = d_norm * w
                corr = _mm_split(gw * t, bdm) * (1.0 / HEAD_DIM)
                dt = r * (gw - t * ((r * r) * corr))
                dn_acc = dn_acc + jnp.sum(d_norm * t * r, axis=0, keepdims=True)
                dproj_ref[:, D_GRP * grp + 128 * p:D_GRP * grp + 128 * (p + 1)] = dt.astype(BF16)
            dn_ref[...] += dn_acc
        dproj_ref[:, 2 * D_GRP:3 * D_GRP] = (dv1[...] + _slab_group(nat4, 2) + _slab_group(nat16, 2)).astype(BF16)
        dproj_ref[:, 3 * D_GRP:4 * D_GRP] = dqs_ref[...].astype(BF16)
        dproj_ref[:, 4 * D_GRP:5 * D_GRP] = dks_ref[...].astype(BF16)
        dproj_ref[:, 5 * D_GRP:6 * D_GRP] = dvs_ref[...].astype(BF16)

    row = lambda w: pl.BlockSpec((tm, w), lambda i: (i, 0))
    return pl.pallas_call(
        body, name="qkv_bwd", grid=(ni,),
        in_specs=[row(D_GRP)] * 3 + [_view_spec(tm, 4)] * 3 + [_view_spec(tm, 16)] * 3 + [row(D_GRP)] * 5
        + [row(128), row(128), _full((1, 128)), _full((1, 128)), _full((128, 128))],
        out_specs=(row(D_IN), _full((1, 128)), _full((1, 128))),
        out_shape=(jax.ShapeDtypeStruct((s_len, D_IN), BF16), jax.ShapeDtypeStruct((1, 128), F32),
                   jax.ShapeDtypeStruct((1, 128), F32)),
        scratch_shapes=[pltpu.VMEM((12, tm, 128), F32)] * 3,
        compiler_params=_params(),
    )(dq_b[0], dk_b[0], dv_b[0], dq_b[1], dk_b[1], dv_b[1], dq_b[2], dk_b[2], dv_b[2],
      dqs, dks, dvs, qraw, kraw, cos2, sin2, qnw, knw, bd)


def _in_bwd_dx(dproj, a_g, x2, dx1, wn1):
    s_len = x2.shape[0]
    tm = ROW_TILE
    ni = s_len // tm

    def body(dp_ref, w_ref, x_ref, dx1_ref, wn_ref, gx_ref, dwn_ref, w_full):
        i = pl.program_id(0)

        @pl.when(i == 0)
        def _():
            dwn_ref[...] = jnp.zeros_like(dwn_ref)
            for d in range(N_DEV):
                w_full[:, IN_SHARD * d:IN_SHARD * (d + 1)] = w_ref[d]

        dh = _dot_nt(dp_ref[...], w_full[...])
        dnorm, dw_rows = _rms_bwd(dh, x_ref[...], wn_ref[...])
        gx_ref[...] = dx1_ref[...] + dnorm
        dwn_ref[...] += jnp.sum(dw_rows, axis=0, keepdims=True)

    row = lambda w: pl.BlockSpec((tm, w), lambda i: (i, 0))
    return pl.pallas_call(
        body, name="in_bwd_dx", grid=(ni,),
        in_specs=[row(D_IN), pl.BlockSpec((N_DEV, D_MODEL, IN_SHARD), lambda i: (0, 0, 0)),
                  row(D_MODEL), row(D_MODEL), _full((1, D_MODEL))],
        out_specs=(row(D_MODEL), _full((1, D_MODEL))),
        out_shape=(jax.ShapeDtypeStruct((s_len, D_MODEL), F32), jax.ShapeDtypeStruct((1, D_MODEL), F32)),
        scratch_shapes=[pltpu.VMEM((D_MODEL, D_IN), BF16)],
        compiler_params=_params(),
    )(dproj, a_g, x2, dx1, wn1)


def _in_bwd_dw(h1, dproj):
    s_len = h1.shape[0]
    tm = ROW_TILE
    ni = s_len // tm

    def body(h_ref, dp_ref, dw_ref, acc):
        i = pl.program_id(1)

        @pl.when(i == 0)
        def _():
            acc[...] = jnp.zeros_like(acc)

        acc[...] += _dot_tn(h_ref[...], dp_ref[...])

        @pl.when(i == ni - 1)
        def _():
            for half in range(2):
                dw_ref[half] = acc[:, IN_SHARD * half:IN_SHARD * (half + 1)].astype(BF16)

    return pl.pallas_call(
        body, name="in_bwd_dw", grid=(N_DEV // 2, ni),
        in_specs=[pl.BlockSpec((tm, D_MODEL), lambda d, i: (i, 0)),
                  pl.BlockSpec((tm, 2 * IN_SHARD), lambda d, i: (i, d))],
        out_specs=pl.BlockSpec((2, D_MODEL, IN_SHARD), lambda d, i: (d, 0, 0)),
        out_shape=jax.ShapeDtypeStruct((N_DEV, D_MODEL, IN_SHARD), BF16),
        scratch_shapes=[pltpu.VMEM((D_MODEL, 2 * IN_SHARD), F32)],
        compiler_params=_params(),
    )(h1, dproj)


def _adamw(recv, w, m, v):
    rows, cols = w.shape
    tr = next((t for t in (128, 32) if rows % t == 0), rows)

    def body(p_ref, w_ref, m_ref, v_ref, g_ref, d_ref, nm_ref, nv_ref):
        g = p_ref[0].astype(F32)
        for s in range(1, N_DEV):
            g = g + p_ref[s].astype(F32)
        m_new = ADAM_B1 * m_ref[...] + (1.0 - ADAM_B1) * g
        v_new = ADAM_B2 * v_ref[...] + (1.0 - ADAM_B2) * (g * g)
        m_hat = m_new / (1.0 - ADAM_B1 ** ADAM_STEP)
        v_hat = v_new / (1.0 - ADAM_B2 ** ADAM_STEP)
        g_ref[...] = g
        d_ref[...] = -ADAM_LR * (m_hat / (jnp.sqrt(v_hat) + ADAM_EPS) + ADAM_WD * w_ref[...])
        nm_ref[...] = m_new
        nv_ref[...] = v_new

    blk = pl.BlockSpec((tr, cols), lambda i: (i, 0))
    out = jax.ShapeDtypeStruct((rows, cols), F32)
    return pl.pallas_call(
        body, name=f"adamw_{rows}x{cols}", grid=(rows // tr,),
        in_specs=[pl.BlockSpec((N_DEV, tr, cols), lambda i: (0, i, 0)), blk, blk, blk],
        out_specs=(blk,) * 4, out_shape=(out,) * 4,
        compiler_params=_params(),
    )(recv, w, m, v)


def _rope_tables(s_len):
    pos = jnp.arange(s_len, dtype=F32)
    inv_freq = ROPE_THETA ** (-jnp.arange(0, HEAD_DIM, 2, dtype=F32) / HEAD_DIM)
    ang = pos[:, None] * inv_freq[None, :]
    cos, sin = jnp.cos(ang), jnp.sin(ang)
    cos2 = jnp.concatenate([cos, cos, cos, cos], axis=1)
    sin2 = jnp.concatenate([-sin, sin, -sin, sin], axis=1)
    return cos2, sin2


def _block_diag_ones(n):
    i = jnp.arange(n)
    return (i[:, None] // HEAD_DIM == i[None, :] // HEAD_DIM).astype(BF16)


def _pad_cols(t):
    return jnp.pad(t, ((0, 0), (0, FF_PAD - FF_SHARD)))


def _pad_rows(t):
    return jnp.pad(t, ((0, FF_PAD - FF_SHARD), (0, 0)))


LOSS_ROW = 26


def _pack_small(n1, n2, ndil, nsb, nq, nk, scalar=None):
    pad = lambda t: jnp.pad(t.reshape(1, -1), ((0, 0), (0, 128 - t.size)))
    last = jnp.zeros((1, 128), F32) if scalar is None else pad(scalar)
    rows = [n1.reshape(8, 128), n2.reshape(8, 128), ndil.reshape(4, 128), nsb.reshape(4, 128),
            pad(nq), pad(nk), last, jnp.zeros((5, 128), F32)]
    return jnp.concatenate(rows, axis=0)


def _unpack_small(t):
    return (t[0:8].reshape(1, D_MODEL), t[8:16].reshape(1, D_MODEL), t[16:20].reshape(1, D_GRP),
            t[20:24].reshape(1, D_GRP), t[24:25, :HEAD_DIM], t[25:26, :HEAD_DIM])


def kernel(x, attn_norm_w, w_in, q_norm_w, k_norm_w, dil_out_norm_w, sb_out_norm_w, w_out, ffn_norm_w, w_gate, w_up, w_down, loss_target, m_attn_norm_w, m_w_in, m_q_norm_w, m_k_norm_w, m_dil_out_norm_w, m_sb_out_norm_w, m_w_out, m_ffn_norm_w, m_w_gate, m_w_up, m_w_down, v_attn_norm_w, v_w_in, v_q_norm_w, v_k_norm_w, v_dil_out_norm_w, v_sb_out_norm_w, v_w_out, v_ffn_norm_w, v_w_gate, v_w_up, v_w_down):
    s_len = x.shape[1]
    x2, tgt = x[0], loss_target[0]

    my_idx = _flat_index(_mesh_pos())
    slot_is_mine = (jnp.arange(N_DEV) == my_idx)[:, None, None]

    (a_g,) = _gather_weights([w_in[0].astype(BF16)])
    w_loc = jnp.concatenate([_pad_cols(w_gate[0]).T, _pad_cols(w_up[0]).T, _pad_rows(w_down[0]), w_out[0]],
                            axis=0).astype(BF16)
    own_in_place = lambda t: lax.dynamic_update_index_in_dim(lax.empty((N_DEV,) + t.shape, t.dtype), t, my_idx, 0)
    w_send, w_recv, w_srcs, w_lands, w_token = _spread_start(
        [w_loc], [own_in_place(w_loc)], blockwise=False, name="weights_start", after=a_g)

    cos2, sin2 = _rope_tables(s_len)
    bd128, bd512 = _block_diag_ones(128), _block_diag_ones(D_GRP)
    idx = jnp.arange(SB_TILE)
    tri_suf = (idx[:, None] > idx[None, :]).astype(BF16)
    tri_pre = (idx[:, None] < idx[None, :]).astype(BF16)
    qnw2 = jnp.concatenate([q_norm_w, q_norm_w], axis=1) + w_token[0:1]
    knw2 = jnp.concatenate([k_norm_w, k_norm_w], axis=1)

    (h1, qraw, kraw, q, k, va, qs, ks, vs,
     q4, k4, v4, q16, k16, v16) = _attn_in(x2, attn_norm_w, a_g, cos2, sin2, qnw2, knw2, bd128)
    qkv_views = {1: (q, k, va), 4: (q4, k4, v4), 16: (q16, k16, v16)}
    o_b, lse_b = [], []
    for r in DILATIONS:
        o, lse = _dil_fwd(*qkv_views[r], r)
        o_b.append(o)
        lse_b.append(lse)
    o_sb, c_sb = _sb_fwd(qs, ks, vs, tri_suf)
    (w_g,) = _spread_wait(w_send, w_recv, w_srcs, w_lands, c_sb, blockwise=False, name="weights_wait")
    o_dil, lse_tot, lse4, lse16, mixed, x1 = _attn_out(o_b, lse_b, o_sb, x2, dil_out_norm_w, sb_out_norm_w, w_g)
    g, u, h2, dy, loss_parts = _ffn_fwd(x1, ffn_norm_w, tgt, w_g)
    loss_local = jnp.sum(loss_parts[::8, 0])

    dg, du, act, dh2 = _ffn_bwd_dx(dy, g, u, w_g)
    (dx1, do_dil, delta, do_sb, dwout, dn2, dndil, dnsb, do4, dl4, do16, dl16) = _attn_out_bwd(
        dy, dh2, x1, ffn_norm_w, w_g, mixed, o_dil, o_sb, dil_out_norm_w, sb_out_norm_w, bd512)
    dwg, dwu, dwd = _ffn_bwd_dw(h2, dy, dg, du, act)
    early = [dwg, dwu, dwd, dwout]
    own_slot_only = lambda t: own_in_place(lax.dynamic_index_in_dim(t, my_idx, 0, keepdims=False))
    g_send, g_recv, g_srcs, g_lands, g_token = _spread_start(
        early, [own_slot_only(t) for t in early], blockwise=True, name="grads_start")
    tri_pre = tri_pre + g_token[0, 0].astype(BF16)
    dqs, dks, dvs = _sb_bwd(qs, ks, vs, do_sb, c_sb, tri_suf, tri_pre)
    cot_views = {1: (do_dil, lse_tot, delta), 4: (do4, lse4, dl4), 16: (do16, lse16, dl16)}
    dq_b, dk_b, dv_b = [], [], []
    for r in DILATIONS:
        dq, dk, dv = _dil_bwd(*qkv_views[r], *cot_views[r], r)
        dq_b.append(dq)
        dk_b.append(dk)
        dv_b.append(dv)
    dproj, dqn2, dkn2 = _qkv_bwd(dq_b, dk_b, dv_b, dqs, dks, dvs, qraw, kraw, cos2, sin2, qnw2, knw2, bd128)
    grad_x, dn1 = _in_bwd_dx(dproj, a_g, x2, dx1, attn_norm_w)
    dwin = _in_bwd_dw(h1, dproj)
    dqn = dqn2[:, :HEAD_DIM] + dqn2[:, HEAD_DIM:]
    dkn = dkn2[:, :HEAD_DIM] + dkn2[:, HEAD_DIM:]

    small = _pack_small(dn1, dn2, dndil, dnsb, dqn, dkn, loss_local)
    (r_small,) = _exchange_grads([], small)
    i_send, i_recv, i_srcs, i_lands, i_token = _spread_start(
        [dwin], [own_slot_only(dwin)], blockwise=True, name="grads_in_start", after=r_small)
    r_gate, r_up, r_down, r_out = _spread_wait(g_send, g_recv, g_srcs, g_lands, i_token, blockwise=True,
                                               name="grads_wait")
    big = {
        "w_gate": tuple(t[:, :FF_SHARD] for t in _adamw(r_gate, _pad_cols(w_gate[0]), _pad_cols(m_w_gate[0]), _pad_cols(v_w_gate[0]))),
        "w_up": tuple(t[:, :FF_SHARD] for t in _adamw(r_up, _pad_cols(w_up[0]), _pad_cols(m_w_up[0]), _pad_cols(v_w_up[0]))),
        "w_down": _adamw(r_down, w_down[0], m_w_down[0], v_w_down[0]),
        "w_out": _adamw(r_out, w_out[0], m_w_out[0], v_w_out[0]),
    }
    packs = [_pack_small(*ts) for ts in (
        (attn_norm_w, ffn_norm_w, dil_out_norm_w, sb_out_norm_w, q_norm_w, k_norm_w),
        (m_attn_norm_w, m_ffn_norm_w, m_dil_out_norm_w, m_sb_out_norm_w, m_q_norm_w, m_k_norm_w),
        (v_attn_norm_w, v_ffn_norm_w, v_dil_out_norm_w, v_sb_out_norm_w, v_q_norm_w, v_k_norm_w))]
    small_raw = _adamw(r_small, *packs)
    loss = small_raw[0][LOSS_ROW, 0]
    small_out = [_unpack_small(t) for t in small_raw]
    (r_in,) = _spread_wait(i_send, i_recv, i_srcs, i_lands, small_out[3][0], blockwise=True,
                           name="grads_in_wait")
    big["w_in"] = _adamw(r_in, w_in[0], m_w_in[0], v_w_in[0])
    names = ["attn_norm_w", "w_in", "q_norm_w", "k_norm_w", "dil_out_norm_w", "sb_out_norm_w", "w_out",
             "ffn_norm_w", "w_gate", "w_up", "w_down"]
    small_pos = {"attn_norm_w": 0, "ffn_norm_w": 1, "dil_out_norm_w": 2, "sb_out_norm_w": 3,
                 "q_norm_w": 4, "k_norm_w": 5}
    outs = [loss, grad_x[None]]
    for kind in range(4):
        for name in names:
            if name in small_pos:
                outs.append(small_out[kind][small_pos[name]])
            else:
                outs.append(big[name][kind][None])
    return tuple(outs)
```

```python
import functools

import jax
import jax.numpy as jnp
from jax import lax
from jax.experimental import pallas as pl
from jax.experimental.pallas import tpu as pltpu

F32 = jnp.float32
BF16 = jnp.bfloat16

N_DEV = 8
D_MODEL = 1024
HEAD_DIM = 64
D_GRP = 512
D_IN = 6 * D_GRP
IN_SHARD = D_IN // N_DEV
FF_SHARD = 352
FF_PAD = 384
FF_BLOCK = 2 * FF_PAD
FF_STEPS = N_DEV // 2
W_PACK_ROWS = 3 * FF_PAD + 128
W_OUT_BLOCK = 3 * FF_PAD // 128
OUT_SHARD = D_MODEL // N_DEV
BLOCK = 128
DILATIONS = (1, 4, 16)
ROPE_THETA = 10000.0
EPS = 1e-6
ATT_SCALE = HEAD_DIM ** -0.5
NEG = -1e30

ADAM_LR = 0.001
ADAM_B1 = 0.9
ADAM_B2 = 0.999
ADAM_EPS = 1e-08
ADAM_WD = 0.01
ADAM_STEP = 10

SB_TILE = 256
SB_DEAD = -104.0
SB_PAIRS = 4
SB_BWD_PAIRS = 2
ROW_TILE = 512
VMEM_LIMIT = 56 * 1024 * 1024
MESH = pl.DeviceIdType.MESH


def _dot(a, b):
    return jnp.dot(a, b, preferred_element_type=F32)


def _dot_nt(a, b):
    return lax.dot_general(a, b, (((1,), (1,)), ((), ())), preferred_element_type=F32)


def _dot_tn(a, b):
    return lax.dot_general(a, b, (((0,), (0,)), ((), ())), preferred_element_type=F32)


def _mm_split(t, m):
    hi = t.astype(BF16)
    lo = (t - hi.astype(F32)).astype(BF16)
    return _dot(hi, m) + _dot(lo, m)


def _params(**kw):
    return pltpu.CompilerParams(vmem_limit_bytes=VMEM_LIMIT, **kw)


def _full(shape):
    nd = len(shape)
    return pl.BlockSpec(shape, lambda *_: (0,) * nd)


def _view_shape(s_len, r, dtype):
    return jax.ShapeDtypeStruct((s_len // r, r * D_GRP), dtype)


def _view_spec(tm, r):
    return pl.BlockSpec((tm // r, r * D_GRP), lambda i: (i, 0))


def _swap_halves(t):
    lane = lax.broadcasted_iota(jnp.int32, t.shape, 1)
    first = (lane & 32) == 0
    return jnp.where(first, pltpu.roll(t, 96, 1), pltpu.roll(t, 32, 1))


def _log_sigmoid(z):
    return jnp.minimum(z, 0.0) - jnp.log(1.0 + jnp.exp(-jnp.abs(z)))


def _log_sigmoid_pair(z):
    neg_abs = lax.bitcast_convert_type(lax.bitcast_convert_type(z, jnp.uint32) | jnp.uint32(0x80000000), F32)
    lb = jnp.minimum(z, 0.0) - jnp.log(1.0 + jnp.exp(neg_abs))
    return lb, lb - z


def _cumsum_mm(t, tri):
    return _dot(t.astype(BF16), tri)


def _split_views(src_ref, stage_ref, views4, views16):
    slabs, n, _ = src_ref.shape
    n4, n16 = n // 4, n // 16
    for j in range(slabs):
        g, lanes = j // 4, 128 * (j % 4)
        src, stage = src_ref.at[j], stage_ref.at[j]
        for c4 in range(4):
            blk = src[pl.ds(c4, n4, stride=4), :]
            stage[n4 * c4:n4 * (c4 + 1), :] = blk
            col = D_GRP * c4 + lanes
            views4[g][:, col:col + 128] = blk.astype(views4[g].dtype)
        for c4 in range(4):
            for c1 in range(4):
                blk = stage[pl.ds(n4 * c4 + c1, n16, stride=4), :]
                col = D_GRP * (4 * c1 + c4) + lanes
                views16[g][:, col:col + 128] = blk.astype(views16[g].dtype)


def _merge_views(views4, views16, stage_ref, dst4_ref, dst16_ref):
    slabs, n, _ = dst4_ref.shape
    n4, n16 = n // 4, n // 16
    for j in range(slabs):
        g, lanes = j // 4, 128 * (j % 4)
        dst4, dst16, stage = dst4_ref.at[j], dst16_ref.at[j], stage_ref.at[j]
        for c4 in range(4):
            col = D_GRP * c4 + lanes
            dst4[pl.ds(c4, n4, stride=4), :] = views4[g][:, col:col + 128].astype(F32)
            for c1 in range(4):
                col = D_GRP * (4 * c1 + c4) + lanes
                stage[pl.ds(n4 * c4 + c1, n16, stride=4), :] = views16[g][:, col:col + 128].astype(F32)
        for c4 in range(4):
            dst16[pl.ds(c4, n4, stride=4), :] = stage[n4 * c4:n4 * (c4 + 1), :]


def _slab_group(ref, g):
    return jnp.concatenate([ref[4 * g + p] for p in range(4)], axis=1)


def _mesh_pos():
    return lax.axis_index("x"), lax.axis_index("y"), lax.axis_index("c")


def _flat_index(p):
    return 4 * p[0] + 2 * p[1] + p[2]


def _gather_weights(shards):
    n_arr = len(shards)

    def body(*refs):
        srcs, outs = refs[:n_arr], refs[n_arr:2 * n_arr]
        send_sems, recv_sems, local_sems = refs[2 * n_arr:]
        x, y, c = _mesh_pos()
        me, sibling = (x, y, c), (x, y, 1 - c)
        chips = [(1 - x, y), (x, 1 - y), (1 - x, 1 - y)]

        def copy(arr, k, block, to, own=False):
            dst = outs[arr].at[_flat_index(block)]
            return pltpu.make_async_remote_copy(
                src_ref=srcs[arr] if own else dst, dst_ref=dst,
                send_sem=send_sems.at[arr, k], recv_sem=recv_sems.at[arr, k],
                device_id=to, device_id_type=MESH)

        for arr in range(n_arr):
            mine = pltpu.make_async_copy(srcs[arr], outs[arr].at[_flat_index(me)], local_sems.at[arr])
            mine.start()
            first = [copy(arr, 0, me, sibling, own=True)]
            first += [copy(arr, 1 + j, me, (*chip, c), own=True) for j, chip in enumerate(chips)]
            for cp in first:
                cp.start()
        for arr in range(n_arr):
            passed = [copy(arr, 4 + j, (*chip, c), sibling) for j, chip in enumerate(chips)]
            for j, chip in enumerate(chips):
                copy(arr, 1 + j, (*chip, c), me).wait_recv()
                passed[j].start()
        for arr in range(n_arr):
            copy(arr, 0, sibling, me).wait_recv()
            for j, chip in enumerate(chips):
                copy(arr, 4 + j, (*chip, 1 - c), me).wait_recv()
            for k in range(7):
                copy(arr, k, me, me).wait_send()
            pltpu.make_async_copy(srcs[arr], outs[arr].at[_flat_index(me)], local_sems.at[arr]).wait()

    any_spec = pl.BlockSpec(memory_space=pl.ANY)
    return pl.pallas_call(
        body, name="gather_weights",
        out_shape=tuple(jax.ShapeDtypeStruct((N_DEV,) + s.shape, s.dtype) for s in shards),
        in_specs=[any_spec] * n_arr, out_specs=(any_spec,) * n_arr,
        scratch_shapes=[pltpu.SemaphoreType.DMA((n_arr, 7)), pltpu.SemaphoreType.DMA((n_arr, 7)),
                        pltpu.SemaphoreType.DMA((n_arr,))],
        compiler_params=pltpu.CompilerParams(has_side_effects=True),
    )(*shards)


_HBM_SPEC = pl.BlockSpec(memory_space=pltpu.HBM)
_SEM_SPEC = pl.BlockSpec(memory_space=pltpu.SEMAPHORE)
_DATAFLOW = pltpu.SideEffectType.DATAFLOW_SIDE_EFFECTING


def _peer_list(x, y, c):
    return [(1 - x if m & 4 else x, 1 - y if m & 2 else y, 1 - c if m & 1 else c) for m in range(1, N_DEV)]


def _spread_copies(src_refs, land_refs, send_sems, recv_sems, blockwise):
    x, y, c = _mesh_pos()
    my_idx = _flat_index((x, y, c))
    copies = []
    for a, (src, land) in enumerate(zip(src_refs, land_refs)):
        for k, peer in enumerate(_peer_list(x, y, c)):
            copies.append(pltpu.make_async_remote_copy(
                src_ref=src.at[_flat_index(peer)] if blockwise else src, dst_ref=land.at[my_idx],
                send_sem=send_sems.at[(N_DEV - 1) * a + k], recv_sem=recv_sems.at[(N_DEV - 1) * a + k],
                device_id=peer, device_id_type=MESH))
    return copies


def _spread_start(srcs, lands, blockwise, name, after=None):
    n = len(srcs)
    extra = [] if after is None else [after]

    def body(*refs):
        ins, outs = refs[:2 * n], refs[2 * n + len(extra):]
        for cp in _spread_copies(ins[:n], ins[n:], outs[0], outs[1], blockwise):
            cp.start()
        token = refs[-1]
        token[...] = jnp.zeros_like(token)

    hbm = lambda t: pltpu.HBM(t.shape, t.dtype)
    sems = pltpu.SemaphoreType.DMA((n * (N_DEV - 1),))
    outs = pl.pallas_call(
        body, name=name,
        out_shape=(sems, sems) + tuple(hbm(t) for t in srcs) + tuple(hbm(t) for t in lands)
        + (jax.ShapeDtypeStruct((8, 128), F32),),
        in_specs=[_HBM_SPEC] * (2 * n) + [pl.BlockSpec(memory_space=pl.ANY)] * len(extra),
        out_specs=(_SEM_SPEC, _SEM_SPEC) + (_HBM_SPEC,) * (2 * n) + (pl.BlockSpec(memory_space=pltpu.VMEM),),
        input_output_aliases={i: 2 + i for i in range(2 * n)},
        compiler_params=pltpu.CompilerParams(has_side_effects=_DATAFLOW),
    )(*[pltpu.with_memory_space_constraint(t, pltpu.HBM) for t in list(srcs) + list(lands)], *extra)
    return outs[0], outs[1], outs[2:2 + n], outs[2 + n:2 + 2 * n], outs[-1]


def _spread_wait(send_sems, recv_sems, srcs, lands, after, blockwise, name):
    n = len(srcs)

    def body(*refs):
        for cp in _spread_copies(refs[:n], refs[n:2 * n], refs[2 * n], refs[2 * n + 1], blockwise):
            cp.wait_send()
            cp.wait_recv()

    hbm = lambda t: pltpu.HBM(t.shape, t.dtype)
    outs = pl.pallas_call(
        body, name=name,
        out_shape=tuple(hbm(t) for t in srcs) + tuple(hbm(t) for t in lands),
        in_specs=[_HBM_SPEC] * (2 * n) + [_SEM_SPEC, _SEM_SPEC, pl.BlockSpec(memory_space=pl.ANY)],
        out_specs=(_HBM_SPEC,) * (2 * n),
        input_output_aliases={i: i for i in range(2 * n)},
        compiler_params=pltpu.CompilerParams(has_side_effects=_DATAFLOW),
    )(*srcs, *lands, send_sems, recv_sems, after)
    return outs[n:]


def _exchange_grads(parts, small):
    n_arr = len(parts)

    def body(*refs):
        ins, outs = refs[:n_arr + 1], refs[n_arr + 1:2 * (n_arr + 1)]
        send_sems, recv_sems, local_sems = refs[2 * (n_arr + 1):]
        x, y, c = _mesh_pos()
        me = (x, y, c)
        my_idx = _flat_index(me)
        peers = []
        for m in range(1, N_DEV):
            peers.append((1 - x if m & 4 else x, 1 - y if m & 2 else y, 1 - c if m & 1 else c))

        def src_block(arr, dev):
            return ins[arr] if arr == n_arr else ins[arr].at[_flat_index(dev)]

        def copy(arr, k):
            return pltpu.make_async_remote_copy(
                src_ref=src_block(arr, peers[k]), dst_ref=outs[arr].at[my_idx],
                send_sem=send_sems.at[arr, k], recv_sem=recv_sems.at[arr, k],
                device_id=peers[k], device_id_type=MESH)

        def local(arr):
            return pltpu.make_async_copy(src_block(arr, me), outs[arr].at[my_idx], local_sems.at[arr])

        for arr in range(n_arr + 1):
            local(arr).start()
            for k in range(N_DEV - 1):
                copy(arr, k).start()
        for arr in range(n_arr + 1):
            for k in range(N_DEV - 1):
                cp = copy(arr, k)
                cp.wait_send()
                cp.wait_recv()
            local(arr).wait()

    any_spec = pl.BlockSpec(memory_space=pl.ANY)
    out_shape = tuple(jax.ShapeDtypeStruct(p.shape, p.dtype) for p in parts)
    out_shape += (jax.ShapeDtypeStruct((N_DEV,) + small.shape, small.dtype),)
    return pl.pallas_call(
        body, name="exchange_grads",
        out_shape=out_shape,
        in_specs=[any_spec] * (n_arr + 1), out_specs=(any_spec,) * (n_arr + 1),
        scratch_shapes=[pltpu.SemaphoreType.DMA((n_arr + 1, N_DEV - 1)),
                        pltpu.SemaphoreType.DMA((n_arr + 1, N_DEV - 1)),
                        pltpu.SemaphoreType.DMA((n_arr + 1,))],
        compiler_params=pltpu.CompilerParams(has_side_effects=True),
    )(*parts, small)


def _head_norm(t, w128, bd):
    ms = _mm_split(t * t, bd) * (1.0 / HEAD_DIM)
    r = lax.rsqrt(ms + EPS)
    return (t * r) * w128, r


def _attn_in(x2, wn1, a_g, cos2, sin2, qnw, knw, bd):
    s_len = x2.shape[0]
    tm = ROW_TILE

    def body(x_ref, wn_ref, w_ref, cos_ref, sin_ref, qnw_ref, knw_ref, bd_ref,
             h1_ref, qraw_ref, kraw_ref, q_ref, k_ref, va_ref, qs_ref, ks_ref, vs_ref,
             q4_ref, k4_ref, v4_ref, q16_ref, k16_ref, v16_ref, proj, slabs, stage, w_full):
        @pl.when(pl.program_id(0) == 0)
        def _():
            for d in range(N_DEV):
                w_full[:, IN_SHARD * d:IN_SHARD * (d + 1)] = w_ref[d]

        xx = x_ref[...]
        r = lax.rsqrt(jnp.mean(xx * xx, axis=-1, keepdims=True) + EPS)
        h = ((xx * r) * wn_ref[...]).astype(BF16)
        h1_ref[...] = h
        proj[...] = _dot(h, w_full[...])
        cos_t, sin_t, bdm = cos_ref[...], sin_ref[...], bd_ref[...]
        for grp, (raw_ref, rope_ref, nw_ref) in enumerate(((qraw_ref, q_ref, qnw_ref),
                                                           (kraw_ref, k_ref, knw_ref))):
            for p in range(4):
                cols = slice(D_GRP * grp + 128 * p, D_GRP * grp + 128 * (p + 1))
                t = proj[:, cols]
                raw_ref[:, 128 * p:128 * (p + 1)] = t
                yn, _ = _head_norm(t, nw_ref[...], bdm)
                roped = yn * cos_t + _swap_halves(yn) * sin_t
                slabs[4 * grp + p] = roped
                rope_ref[:, 128 * p:128 * (p + 1)] = roped.astype(BF16)
        for p in range(4):
            slabs[8 + p] = proj[:, 2 * D_GRP + 128 * p:2 * D_GRP + 128 * (p + 1)]
        for grp, ref in ((2, va_ref), (3, qs_ref), (4, ks_ref), (5, vs_ref)):
            ref[...] = proj[:, D_GRP * grp:D_GRP * (grp + 1)].astype(BF16)
        _split_views(slabs, stage, (q4_ref, k4_ref, v4_ref), (q16_ref, k16_ref, v16_ref))

    row = lambda w: pl.BlockSpec((tm, w), lambda i: (i, 0))
    grp_bf = jax.ShapeDtypeStruct((s_len, D_GRP), BF16)
    grp_f32 = jax.ShapeDtypeStruct((s_len, D_GRP), F32)
    return pl.pallas_call(
        body, name="attn_in", grid=(s_len // tm,),
        in_specs=[row(D_MODEL), _full((1, D_MODEL)),
                  pl.BlockSpec((N_DEV, D_MODEL, IN_SHARD), lambda i: (0, 0, 0)),
                  row(128), row(128), _full((1, 128)), _full((1, 128)), _full((128, 128))],
        out_specs=(row(D_MODEL),) + (row(D_GRP),) * 8 + (_view_spec(tm, 4),) * 3 + (_view_spec(tm, 16),) * 3,
        out_shape=(jax.ShapeDtypeStruct((s_len, D_MODEL), BF16), grp_f32, grp_f32) + (grp_bf,) * 6
        + (_view_shape(s_len, 4, BF16),) * 3 + (_view_shape(s_len, 16, BF16),) * 3,
        scratch_shapes=[pltpu.VMEM((tm, D_IN), F32), pltpu.VMEM((12, tm, 128), F32), pltpu.VMEM((12, tm, 128), F32),
                        pltpu.VMEM((D_MODEL, D_IN), BF16)],
        compiler_params=_params(),
    )(x2, wn1, a_g, cos2, sin2, qnw, knw, bd)


def _band_mask(n):
    i = lax.broadcasted_iota(jnp.int32, (2 * BLOCK, 2 * BLOCK), 0) & (BLOCK - 1)
    j = lax.broadcasted_iota(jnp.int32, (2 * BLOCK, 2 * BLOCK), 1)
    dist = i + BLOCK - j
    return (dist >= 0) & (dist <= BLOCK) & ((n - 1) * BLOCK + j >= 0)


def _stack_heads(t2, head0):
    return jnp.concatenate([jnp.where(head0, t2, 0), jnp.where(head0, 0, t2)], axis=0)


def _unstack_heads(t, head0):
    return jnp.where(head0, t[0:BLOCK], t[BLOCK:2 * BLOCK])


def _dil_fwd(qv, kv, vv, r):
    sub_len = qv.shape[0]
    nb = sub_len // BLOCK

    qb = 2 if nb % 2 == 0 else 1

    def body(q_ref, kp_ref, kc_ref, vp_ref, vc_ref, o_ref, lse_ref):
        n = pl.program_id(1)
        lane = lax.broadcasted_iota(jnp.int32, (BLOCK, 128), 1)
        head0 = lane < HEAD_DIM
        units = [(b, slice(128 * p, 128 * (p + 1))) for b in range(qb) for p in range(4)]
        valid = [_band_mask(qb * n + b) for b in range(qb)]
        rows = [slice(BLOCK * b, BLOCK * (b + 1)) for b in range(qb)]

        def keys(prev_ref, cur_ref, b, c):
            before = prev_ref[:, c] if b == 0 else cur_ref[rows[b - 1], c]
            return jnp.concatenate([before, cur_ref[rows[b], c]], axis=0)

        qqs = [_stack_heads(q_ref[rows[b], c] * ATT_SCALE, head0) for b, c in units]
        kks = [keys(kp_ref, kc_ref, b, c) for b, c in units]
        vvs = [keys(vp_ref, vc_ref, b, c) for b, c in units]
        ss = [_dot_nt(qq, kk) for qq, kk in zip(qqs, kks)]
        prs, dens, lses = [], [], []
        for (b, _), s in zip(units, ss):
            s = jnp.where(valid[b], s, NEG)
            m = jnp.max(s, axis=-1, keepdims=True)
            pr = jnp.exp(s - m)
            den = jnp.sum(pr, axis=-1, keepdims=True)
            prs.append(pr.astype(BF16))
            dens.append(den)
            lses.append(m + jnp.log(den))
        pvs = [_dot(pr, vv2) for pr, vv2 in zip(prs, vvs)]
        for (b, c), pv, den, lse in zip(units, pvs, dens, lses):
            o_ref[rows[b], c] = _unstack_heads(pv / den, head0)
            lse_ref[rows[b], c] = _unstack_heads(jnp.broadcast_to(lse, (2 * BLOCK, 128)), head0)

    cur = pl.BlockSpec((qb * BLOCK, D_GRP), lambda c, n: (n, c))
    prev = pl.BlockSpec((BLOCK, D_GRP), lambda c, n: (jnp.maximum(qb * n - 1, 0), c))
    out = jax.ShapeDtypeStruct(qv.shape, F32)
    return pl.pallas_call(
        body, name=f"dil_fwd_r{r}", grid=(r, nb // qb),
        in_specs=[cur, prev, cur, prev, cur], out_specs=(cur, cur), out_shape=(out, out),
        compiler_params=_params(),
    )(qv, kv, kv, vv, vv)


def _dil_bwd(qv, kv, vv, dov, lsev, deltav, r):
    sub_len = qv.shape[0]
    nb = sub_len // BLOCK

    def body(q_ref, kp_ref, kc_ref, vp_ref, vc_ref, do_ref, lse_ref, dl_ref,
             dq_ref, dk_ref, dv_ref, dk_carry, dv_carry):
        n = pl.program_id(1)

        @pl.when(n == 0)
        def _():
            dk_carry[...] = jnp.zeros_like(dk_carry)
            dv_carry[...] = jnp.zeros_like(dv_carry)

        @pl.when(n < nb)
        def _():
            valid = _band_mask(n)
            lane = lax.broadcasted_iota(jnp.int32, (BLOCK, 128), 1)
            head0 = lane < HEAD_DIM
            pairs = [slice(128 * p, 128 * (p + 1)) for p in range(4)]
            qqs = [_stack_heads(q_ref[:, c] * ATT_SCALE, head0) for c in pairs]
            dos = [_stack_heads(do_ref[:, c], head0) for c in pairs]
            kks = [jnp.concatenate([kp_ref[:, c], kc_ref[:, c]], axis=0) for c in pairs]
            vvs = [jnp.concatenate([vp_ref[:, c], vc_ref[:, c]], axis=0) for c in pairs]
            ss = [_dot_nt(qq, kk) for qq, kk in zip(qqs, kks)]
            dps = [_dot_nt(do, vv2) for do, vv2 in zip(dos, vvs)]
            prs, dss = [], []
            for c, s, dp in zip(pairs, ss, dps):
                stats = []
                for ref in (lse_ref, dl_ref):
                    t2 = ref[:, c]
                    stats.append(jnp.concatenate(
                        [jnp.sum(jnp.where(lane == 0, t2, 0.0), axis=-1, keepdims=True),
                         jnp.sum(jnp.where(lane == HEAD_DIM, t2, 0.0), axis=-1, keepdims=True)], axis=0))
                pr = jnp.where(valid, jnp.exp(jnp.minimum(s - stats[0], 0.0)), 0.0)
                prs.append(pr.astype(BF16))
                dss.append((pr * (dp - stats[1])).astype(BF16))
            dqs = [_dot(ds, kk) for ds, kk in zip(dss, kks)]
            dkks = [_dot_tn(ds, qq) for ds, qq in zip(dss, qqs)]
            dvvs = [_dot_tn(pr, do) for pr, do in zip(prs, dos)]
            for c, dq, dkk, dvv in zip(pairs, dqs, dkks, dvvs):
                dq_ref[:, c] = _unstack_heads(dq, head0) * ATT_SCALE
                dk_ref[:, c] = dk_carry[:, c] + dkk[:BLOCK]
                dv_ref[:, c] = dv_carry[:, c] + dvv[:BLOCK]
                dk_carry[:, c] = dkk[BLOCK:]
                dv_carry[:, c] = dvv[BLOCK:]

        @pl.when(n == nb)
        def _():
            dk_ref[...] = dk_carry[...]
            dv_ref[...] = dv_carry[...]

    last = nb - 1
    cur = pl.BlockSpec((BLOCK, D_GRP), lambda c, n: (jnp.minimum(n, last), c))
    prev = pl.BlockSpec((BLOCK, D_GRP), lambda c, n: (jnp.clip(n - 1, 0, last), c))
    out = jax.ShapeDtypeStruct(qv.shape, F32)
    return pl.pallas_call(
        body, name=f"dil_bwd_r{r}", grid=(r, nb + 1),
        in_specs=[cur, prev, cur, prev, cur, cur, cur, cur],
        out_specs=(cur, prev, prev), out_shape=(out, out, out),
        scratch_shapes=[pltpu.VMEM((BLOCK, D_GRP), F32), pltpu.VMEM((BLOCK, D_GRP), F32)],
        compiler_params=_params(),
    )(qv, kv, kv, vv, vv, dov, lsev, deltav)


def _sb_fwd(qs, ks, vs, tri_suf):
    s_len = qs.shape[0]
    t = SB_TILE
    nq = s_len // t

    npair = SB_PAIRS

    def body(q_ref, k_ref, v_ref, u_ref, o_ref, c_ref, qq, vt, acc, cf, csave):
        row = lax.broadcasted_iota(jnp.int32, (2 * t, t), 0) & (t - 1)
        col = lax.broadcasted_iota(jnp.int32, (2 * t, t), 1)
        diag_mask = col < row
        lane1 = lax.broadcasted_iota(jnp.int32, (t, 128), 1)
        head0 = lane1 < HEAD_DIM
        lane2 = lax.broadcasted_iota(jnp.int32, (2 * t, 128), 1)
        uu = u_ref[...]
        pr = range(npair)
        cols = [slice(128 * pp, 128 * (pp + 1)) for pp in pr]

        i = pl.program_id(1)

        @pl.when(i == 0)
        def _():
            def transpose_v(j, _):
                rows = pl.ds(pl.multiple_of(j * t, t), t)
                for pp in pr:
                    vt[pp, j] = v_ref[rows, cols[pp]].astype(F32).T.astype(BF16)
                return 0

            lax.fori_loop(0, nq, transpose_v, 0)

        for pp in pr:
            q2 = q_ref[:, cols[pp]] * ATT_SCALE
            qq[pp, 0:t, :] = jnp.where(head0, q2, 0)
            qq[pp, t:2 * t, :] = jnp.where(head0, 0, q2)
        acc[...] = jnp.zeros_like(acc)
        cf[...] = jnp.zeros_like(cf)
        csave[...] = jnp.full(csave.shape, 2.0 * SB_DEAD, F32)

        def tile(kb, diag):
            krows = pl.ds(pl.multiple_of(kb * t, t), t)
            zs = [_dot_nt(qq[pp], k_ref[krows, cols[pp]]) for pp in pr]
            lbk = [_log_sigmoid_pair(z) for z in zs]
            lks = [jnp.where(diag_mask, lk, 0.0) if diag else lk for _, lk in lbk]
            sufs = [_cumsum_mm(lk, uu) for lk in lks]
            carries = [cf[pp] for pp in pr]
            avs = []
            for pp in pr:
                a = jnp.exp(lbk[pp][0] + (sufs[pp] + jnp.concatenate([carries[pp]] * (t // 128), axis=1)))
                avs.append((jnp.where(diag_mask, a, 0.0) if diag else a).astype(BF16))
            pvs = [_dot_nt(vt[pp, kb], avs[pp]) for pp in pr]
            for pp in pr:
                acc[pp] += pvs[pp]
                csave[pp] = jnp.where(lane2 == kb, carries[pp], csave[pp])
                cf[pp] = carries[pp] + jnp.broadcast_to(jnp.sum(lks[pp], axis=-1, keepdims=True), (2 * t, 128))

        tile(i, True)

        def alive():
            return jnp.max(cf[...]) > SB_DEAD

        def k_block(state):
            kb, _ = state
            tile(kb, False)
            return kb - 1, alive()

        lax.while_loop(lambda state: jnp.logical_and(state[0] >= 0, state[1]), k_block, (i - 1, alive()))
        for pp in pr:
            o_ref[:, cols[pp]] = jnp.where(head0, acc[pp, :, 0:t].T, acc[pp, :, t:2 * t].T)
            c_ref[2 * pp] = csave[pp, 0:t, :]
            c_ref[2 * pp + 1] = csave[pp, t:2 * t, :]

    width = 128 * npair
    kv = pl.BlockSpec((s_len, width), lambda p, i: (0, p))
    qo = pl.BlockSpec((t, width), lambda p, i: (i, p))
    return pl.pallas_call(
        body, name="sb_fwd", grid=(4 // npair, nq),
        in_specs=[qo, kv, kv, pl.BlockSpec((t, t), lambda p, i: (0, 0))],
        out_specs=(qo, pl.BlockSpec((2 * npair, t, 128), lambda p, i: (p, i, 0))),
        out_shape=(jax.ShapeDtypeStruct((s_len, D_GRP), F32),
                   jax.ShapeDtypeStruct((8, s_len, 128), F32)),
        scratch_shapes=[pltpu.VMEM((npair, 2 * t, 128), BF16), pltpu.VMEM((npair, nq, 128, t), BF16),
                        pltpu.VMEM((npair, 128, 2 * t), F32),
                        pltpu.VMEM((npair, 2 * t, 128), F32), pltpu.VMEM((npair, 2 * t, 128), F32)],
        compiler_params=_params(),
    )(qs, ks, vs, tri_suf)


def _sb_bwd(qs, ks, vs, dos, csaved, tri_suf, tri_pre):
    s_len = qs.shape[0]
    t = SB_TILE
    nq = s_len // t

    npair = SB_BWD_PAIRS

    def body(q_ref, k_ref, v_ref, do_ref, c_ref, u_ref, p_ref, dq_ref, dk_ref, dv_ref,
             qq, dd, qqt, ddt, kt, dq_acc, dkt, dvt, cg):
        row = lax.broadcasted_iota(jnp.int32, (2 * t, t), 0) & (t - 1)
        col = lax.broadcasted_iota(jnp.int32, (2 * t, t), 1)
        diag_mask = col < row
        lane1 = lax.broadcasted_iota(jnp.int32, (t, 128), 1)
        head0 = lane1 < HEAD_DIM
        lane2 = lax.broadcasted_iota(jnp.int32, (2 * t, 128), 1)
        uu, pm = u_ref[...], p_ref[...]
        pr = range(npair)
        cols = [slice(128 * pp, 128 * (pp + 1)) for pp in pr]
        i = pl.program_id(1)

        @pl.when(i == 0)
        def _():
            dkt[...] = jnp.zeros_like(dkt)
            dvt[...] = jnp.zeros_like(dvt)

            def transpose_k(j, _):
                rows = pl.ds(pl.multiple_of(j * t, t), t)
                for pp in pr:
                    kt[pp, j] = k_ref[rows, cols[pp]].astype(F32).T.astype(BF16)
                return 0

            lax.fori_loop(0, nq, transpose_k, 0)

        for pp in pr:
            q2 = q_ref[:, cols[pp]].astype(F32) * ATT_SCALE
            do2 = do_ref[:, cols[pp]].astype(F32)
            for src, nat, tr in ((q2, qq, qqt), (do2, dd, ddt)):
                stacked = jnp.concatenate([jnp.where(head0, src, 0.0), jnp.where(head0, 0.0, src)], axis=0)
                nat[pp] = stacked.astype(BF16)
                tr[pp] = stacked.T.astype(BF16)
        dq_acc[...] = jnp.zeros_like(dq_acc)
        cg[...] = jnp.zeros_like(cg)

        def tile(kb, diag):
            krows = pl.ds(pl.multiple_of(kb * t, t), t)
            zs = [_dot_nt(qq[pp], k_ref[krows, cols[pp]]) for pp in pr]
            das = [_dot_nt(dd[pp], v_ref[krows, cols[pp]]) for pp in pr]
            lbk = [_log_sigmoid_pair(z) for z in zs]
            lks = [jnp.where(diag_mask, lk, 0.0) if diag else lk for _, lk in lbk]
            sufs = [_cumsum_mm(lk, uu) for lk in lks]
            avs, gs = [], []
            for pp in pr:
                cs = jnp.concatenate([c_ref[2 * pp], c_ref[2 * pp + 1]], axis=0)
                cf = jnp.sum(jnp.where(lane2 == kb, cs, 0.0), axis=-1, keepdims=True)
                a = jnp.exp(lbk[pp][0] + (sufs[pp] + cf))
                a = jnp.where(diag_mask, a, 0.0) if diag else a
                avs.append(a.astype(BF16))
                gs.append(a * das[pp])
            gpres = [_cumsum_mm(g, pm) for g in gs]
            dzs = []
            for pp in pr:
                carry = cg[pp]
                beta = jnp.exp(lbk[pp][0])
                dz = gs[pp] - beta * (gs[pp] + (gpres[pp] + jnp.concatenate([carry] * (t // 128), axis=1)))
                dzs.append((jnp.where(diag_mask, dz, 0.0) if diag else dz).astype(BF16))
                cg[pp] = carry + jnp.broadcast_to(jnp.sum(gs[pp], axis=-1, keepdims=True), (2 * t, 128))
            dqs = [_dot_nt(kt[pp, kb], dzs[pp]) for pp in pr]
            dks = [_dot(qqt[pp], dzs[pp]) for pp in pr]
            dvs = [_dot(ddt[pp], avs[pp]) for pp in pr]
            for pp in pr:
                dq_acc[pp] += dqs[pp]
                dkt[pp, kb] += dks[pp]
                dvt[pp, kb] += dvs[pp]

        def k_block(kb, _):
            tile(kb, False)
            return 0

        col_max = jnp.max(jnp.max(c_ref[...], axis=0), axis=0, keepdims=True)
        lane_row = lax.broadcasted_iota(jnp.int32, (1, 128), 1)
        n_live = jnp.sum(jnp.where(jnp.logical_and(col_max > SB_DEAD, lane_row < i), 1, 0))
        lax.fori_loop(i - n_live, i, k_block, 0)
        tile(i, True)
        for pp in pr:
            dq_ref[:, cols[pp]] = jnp.where(head0, dq_acc[pp, :, 0:t].T, dq_acc[pp, :, t:2 * t].T) * ATT_SCALE

        @pl.when(i == nq - 1)
        def _():
            def untranspose(j, _):
                rows = pl.ds(pl.multiple_of(j * t, t), t)
                for pp in pr:
                    dk_ref[rows, cols[pp]] = dkt[pp, j].T
                    dv_ref[rows, cols[pp]] = dvt[pp, j].T
                return 0

            lax.fori_loop(0, nq, untranspose, 0)

    width = 128 * npair
    kv = pl.BlockSpec((s_len, width), lambda p, i: (0, p))
    qo = pl.BlockSpec((t, width), lambda p, i: (i, p))
    tri = pl.BlockSpec((t, t), lambda p, i: (0, 0))
    out = jax.ShapeDtypeStruct((s_len, D_GRP), F32)
    return pl.pallas_call(
        body, name="sb_bwd", grid=(4 // npair, nq),
        in_specs=[qo, kv, kv, qo, pl.BlockSpec((2 * npair, t, 128), lambda p, i: (p, i, 0)), tri, tri],
        out_specs=(qo, kv, kv), out_shape=(out, out, out),
        scratch_shapes=[pltpu.VMEM((npair, 2 * t, 128), BF16), pltpu.VMEM((npair, 2 * t, 128), BF16),
                        pltpu.VMEM((npair, 128, 2 * t), BF16), pltpu.VMEM((npair, 128, 2 * t), BF16),
                        pltpu.VMEM((npair, nq, 128, t), BF16),
                        pltpu.VMEM((npair, 128, 2 * t), F32),
                        pltpu.VMEM((npair, nq, 128, t), F32), pltpu.VMEM((npair, nq, 128, t), F32),
                        pltpu.VMEM((npair, 2 * t, 128), F32)],
        compiler_params=_params(),
    )(qs, ks, vs, dos, csaved, tri_suf, tri_pre)


def _attn_out(o_b, lse_b, o_sb, x2, wdil, wsb, b_g):
    s_len = x2.shape[0]
    tm = ROW_TILE

    def body(o1_ref, l1_ref, o4_ref, l4_ref, o16_ref, l16_ref, osb_ref, x_ref, wdil_ref, wsb_ref, w_ref,
             odil_ref, lse_ref, lse4_ref, lse16_ref, mixed_ref, x1_ref, stage, nat4, nat16):
        _merge_views((o4_ref, l4_ref), (o16_ref, l16_ref), stage, nat4, nat16)
        os_ = (o1_ref[...], _slab_group(nat4, 0), _slab_group(nat16, 0))
        ls = (l1_ref[...], _slab_group(nat4, 1), _slab_group(nat16, 1))
        mx = jnp.maximum(jnp.maximum(ls[0], ls[1]), ls[2])
        es = [jnp.exp(l - mx) for l in ls]
        den = es[0] + es[1] + es[2]
        o_dil = (es[0] * os_[0] + es[1] * os_[1] + es[2] * os_[2]) / den
        odil_ref[...] = o_dil
        lse = mx + jnp.log(den)
        lse_ref[...] = lse
        for p in range(4):
            nat4[p] = lse[:, 128 * p:128 * (p + 1)]
        _split_views(nat4.at[0:4], stage.at[0:4], (lse4_ref,), (lse16_ref,))
        halves = []
        for t, w_r in ((o_dil, wdil_ref), (osb_ref[...], wsb_ref)):
            r = lax.rsqrt(jnp.mean(t * t, axis=-1, keepdims=True) + EPS)
            halves.append(((t * r) * w_r[...]).astype(BF16))
        mixed = jnp.concatenate(halves, axis=1)
        mixed_ref[...] = mixed
        w = w_ref[...].reshape(D_MODEL, D_MODEL)
        x1_ref[...] = x_ref[...] + _dot(mixed, w)

    row = lambda w: pl.BlockSpec((tm, w), lambda i: (i, 0))
    return pl.pallas_call(
        body, name="attn_out", grid=(s_len // tm,),
        in_specs=[row(D_GRP)] * 2 + [_view_spec(tm, 4)] * 2 + [_view_spec(tm, 16)] * 2
        + [row(D_GRP), row(D_MODEL), _full((1, D_GRP)), _full((1, D_GRP)),
           pl.BlockSpec((N_DEV, OUT_SHARD, D_MODEL), lambda i: (0, W_OUT_BLOCK, 0))],
        out_specs=(row(D_GRP), row(D_GRP), _view_spec(tm, 4), _view_spec(tm, 16), row(D_MODEL), row(D_MODEL)),
        out_shape=(jax.ShapeDtypeStruct((s_len, D_GRP), F32), jax.ShapeDtypeStruct((s_len, D_GRP), F32),
                   _view_shape(s_len, 4, F32), _view_shape(s_len, 16, F32),
                   jax.ShapeDtypeStruct((s_len, D_MODEL), BF16), jax.ShapeDtypeStruct((s_len, D_MODEL), F32)),
        scratch_shapes=[pltpu.VMEM((8, tm, 128), F32)] * 3,
        compiler_params=_params(),
    )(o_b[0], lse_b[0], o_b[1], lse_b[1], o_b[2], lse_b[2], o_sb, x2, wdil, wsb, b_g)


def _two_shards(w_ref):
    return w_ref[...].reshape(FF_BLOCK, D_MODEL)


def _ffn_fwd(x1, wn2, tgt, w_g):
    s_len = x1.shape[0]
    tm = ROW_TILE
    ni = s_len // tm

    def body(x_ref, wn_ref, t_ref, wg_ref, wu_ref, wd_ref, g_ref, u_ref, h2_ref, dy_ref, loss_ref, acc):
        j = pl.program_id(1)

        @pl.when(j == 0)
        def _():
            xx = x_ref[...]
            r = lax.rsqrt(jnp.mean(xx * xx, axis=-1, keepdims=True) + EPS)
            h2_ref[...] = ((xx * r) * wn_ref[...]).astype(BF16)
            acc[...] = jnp.zeros_like(acc)

        h = h2_ref[...]
        g = _dot_nt(h, _two_shards(wg_ref))
        u = _dot_nt(h, _two_shards(wu_ref))
        g_ref[...] = g
        u_ref[...] = u
        act = (g * (1.0 / (1.0 + jnp.exp(-g)))) * u
        acc[...] += _dot(act.astype(BF16), _two_shards(wd_ref))

        @pl.when(j == FF_STEPS - 1)
        def _():
            err = (x_ref[...] + acc[...]) - t_ref[...]
            dy_ref[...] = err * (1.0 / D_MODEL)
            part = 0.5 * jnp.sum(jnp.mean(err * err, axis=-1, keepdims=True))
            loss_ref[...] = jnp.full((8, 128), part, F32)

    row = pl.BlockSpec((tm, D_MODEL), lambda i, j: (i, 0))
    hid = pl.BlockSpec((tm, FF_BLOCK), lambda i, j: (i, j))
    return pl.pallas_call(
        body, name="ffn_fwd", grid=(ni, FF_STEPS),
        in_specs=[row, pl.BlockSpec((1, D_MODEL), lambda i, j: (0, 0)), row,
                  pl.BlockSpec((2, FF_PAD, D_MODEL), lambda i, j: (j, 0, 0)),
                  pl.BlockSpec((2, FF_PAD, D_MODEL), lambda i, j: (j, 1, 0)),
                  pl.BlockSpec((2, FF_PAD, D_MODEL), lambda i, j: (j, 2, 0))],
        out_specs=(hid, hid, row, row, pl.BlockSpec((8, 128), lambda i, j: (i, 0))),
        out_shape=(jax.ShapeDtypeStruct((s_len, N_DEV * FF_PAD), F32),
                   jax.ShapeDtypeStruct((s_len, N_DEV * FF_PAD), F32),
                   jax.ShapeDtypeStruct((s_len, D_MODEL), BF16),
                   jax.ShapeDtypeStruct((s_len, D_MODEL), F32),
                   jax.ShapeDtypeStruct((ni * 8, 128), F32)),
        scratch_shapes=[pltpu.VMEM((tm, D_MODEL), F32)],
        compiler_params=_params(),
    )(x1, wn2, tgt, w_g, w_g, w_g)


def _ffn_bwd_dx(dy, g, u, w_g):
    s_len = dy.shape[0]
    tm = ROW_TILE

    def body(dy_ref, g_ref, u_ref, wg_ref, wu_ref, wd_ref, dg_ref, du_ref, act_ref, dh_ref, acc):
        j = pl.program_id(1)

        @pl.when(j == 0)
        def _():
            acc[...] = jnp.zeros_like(acc)

        gg, uu = g_ref[...], u_ref[...]
        da = _dot_nt(dy_ref[...].astype(BF16), _two_shards(wd_ref))
        sig = 1.0 / (1.0 + jnp.exp(-gg))
        silu = gg * sig
        act_ref[...] = (silu * uu).astype(BF16)
        du = (da * silu).astype(BF16)
        dg = (da * uu * (sig * (1.0 + gg * (1.0 - sig)))).astype(BF16)
        du_ref[...] = du
        dg_ref[...] = dg
        acc[...] += _dot(dg, _two_shards(wg_ref)) + _dot(du, _two_shards(wu_ref))

        @pl.when(j == FF_STEPS - 1)
        def _():
            dh_ref[...] = acc[...]

    row = pl.BlockSpec((tm, D_MODEL), lambda i, j: (i, 0))
    hid = pl.BlockSpec((tm, FF_BLOCK), lambda i, j: (i, j))
    hid_bf = jax.ShapeDtypeStruct((s_len, N_DEV * FF_PAD), BF16)
    return pl.pallas_call(
        body, name="ffn_bwd_dx", grid=(s_len // tm, FF_STEPS),
        in_specs=[row, hid, hid,
                  pl.BlockSpec((2, FF_PAD, D_MODEL), lambda i, j: (j, 0, 0)),
                  pl.BlockSpec((2, FF_PAD, D_MODEL), lambda i, j: (j, 1, 0)),
                  pl.BlockSpec((2, FF_PAD, D_MODEL), lambda i, j: (j, 2, 0))],
        out_specs=(hid, hid, hid, row),
        out_shape=(hid_bf, hid_bf, hid_bf, jax.ShapeDtypeStruct((s_len, D_MODEL), F32)),
        scratch_shapes=[pltpu.VMEM((tm, D_MODEL), F32)],
        compiler_params=_params(),
    )(dy, g, u, w_g, w_g, w_g)


def _ffn_bwd_dw(h2, dy, dg, du, act):
    s_len = h2.shape[0]
    tm = ROW_TILE
    ni = s_len // tm

    def body(h_ref, dy_ref, dg_ref, du_ref, act_ref, dwg_ref, dwu_ref, dwd_ref, ag, au, ad):
        i = pl.program_id(1)

        @pl.when(i == 0)
        def _():
            ag[...] = jnp.zeros_like(ag)
            au[...] = jnp.zeros_like(au)
            ad[...] = jnp.zeros_like(ad)

        h = h_ref[...]
        ag[...] += _dot_tn(h, dg_ref[...])
        au[...] += _dot_tn(h, du_ref[...])
        ad[...] += _dot_tn(act_ref[...], dy_ref[...].astype(BF16))

        @pl.when(i == ni - 1)
        def _():
            for half in range(2):
                cols = slice(FF_PAD * half, FF_PAD * (half + 1))
                dwg_ref[half] = ag[:, cols].astype(BF16)
                dwu_ref[half] = au[:, cols].astype(BF16)
            dwd_ref[...] = ad[...].astype(BF16).reshape(2, FF_PAD, D_MODEL)

    row = pl.BlockSpec((tm, D_MODEL), lambda j, i: (i, 0))
    hid = pl.BlockSpec((tm, FF_BLOCK), lambda j, i: (i, j))
    col_w = pl.BlockSpec((2, D_MODEL, FF_PAD), lambda j, i: (j, 0, 0))
    row_w = pl.BlockSpec((2, FF_PAD, D_MODEL), lambda j, i: (j, 0, 0))
    return pl.pallas_call(
        body, name="ffn_bwd_dw", grid=(FF_STEPS, ni),
        in_specs=[row, row, hid, hid, hid], out_specs=(col_w, col_w, row_w),
        out_shape=(jax.ShapeDtypeStruct((N_DEV, D_MODEL, FF_PAD), BF16),
                   jax.ShapeDtypeStruct((N_DEV, D_MODEL, FF_PAD), BF16),
                   jax.ShapeDtypeStruct((N_DEV, FF_PAD, D_MODEL), BF16)),
        scratch_shapes=[pltpu.VMEM((D_MODEL, FF_BLOCK), F32), pltpu.VMEM((D_MODEL, FF_BLOCK), F32),
                        pltpu.VMEM((FF_BLOCK, D_MODEL), F32)],
        compiler_params=_params(),
    )(h2, dy, dg, du, act)


def _rms_bwd(dy, t, w):
    r = lax.rsqrt(jnp.mean(t * t, axis=-1, keepdims=True) + EPS)
    gw = dy * w
    dt = r * (gw - t * ((r * r) * jnp.mean(gw * t, axis=-1, keepdims=True)))
    return dt, dy * t * r


def _attn_out_bwd(dy, dh2, x1, wn2, b_g, mixed, o_dil, o_sb, wdil, wsb, bd512):
    s_len = dy.shape[0]
    tm = ROW_TILE
    ni = s_len // tm

    def body(dy_ref, dh_ref, x1_ref, wn_ref, w_ref, mixed_ref, odil_ref, osb_ref, wdil_ref, wsb_ref, bd_ref,
             dx1_ref, dodil_ref, delta_ref, dosb_ref, dwout_ref, dwn_ref, dwdil_ref, dwsb_ref,
             do4_ref, dl4_ref, do16_ref, dl16_ref, wacc, both, stage):
        i = pl.program_id(0)

        @pl.when(i == 0)
        def _():
            wacc[...] = jnp.zeros_like(wacc)
            dwn_ref[...] = jnp.zeros_like(dwn_ref)
            dwdil_ref[...] = jnp.zeros_like(dwdil_ref)
            dwsb_ref[...] = jnp.zeros_like(dwsb_ref)

        dnorm, dw_rows = _rms_bwd(dh_ref[...], x1_ref[...], wn_ref[...])
        dx1 = dy_ref[...] + dnorm
        dx1_ref[...] = dx1
        dwn_ref[...] += jnp.sum(dw_rows, axis=0, keepdims=True)
        dx1b = dx1.astype(BF16)
        w = w_ref[...].reshape(D_MODEL, D_MODEL)
        dmixed = _dot_nt(dx1b, w)
        wacc[...] += _dot_tn(mixed_ref[...], dx1b)
        o_dil = odil_ref[...]
        d_odil, dw_rows = _rms_bwd(dmixed[:, :D_GRP], o_dil, wdil_ref[...])
        dwdil_ref[...] += jnp.sum(dw_rows, axis=0, keepdims=True)
        dodil_ref[...] = d_odil.astype(BF16)
        delta = _mm_split(d_odil * o_dil, bd_ref[...])
        delta_ref[...] = delta
        for p in range(4):
            both[p] = d_odil[:, 128 * p:128 * (p + 1)]
            both[4 + p] = delta[:, 128 * p:128 * (p + 1)]
        _split_views(both, stage, (do4_ref, dl4_ref), (do16_ref, dl16_ref))
        d_osb, dw_rows = _rms_bwd(dmixed[:, D_GRP:], osb_ref[...], wsb_ref[...])
        dwsb_ref[...] += jnp.sum(dw_rows, axis=0, keepdims=True)
        dosb_ref[...] = d_osb.astype(BF16)

        @pl.when(i == ni - 1)
        def _():
            dwout_ref[...] = wacc[...].astype(BF16).reshape(N_DEV, OUT_SHARD, D_MODEL)

    row = lambda w: pl.BlockSpec((tm, w), lambda i: (i, 0))
    return pl.pallas_call(
        body, name="attn_out_bwd", grid=(ni,),
        in_specs=[row(D_MODEL), row(D_MODEL), row(D_MODEL), _full((1, D_MODEL)),
                  pl.BlockSpec((N_DEV, OUT_SHARD, D_MODEL), lambda i: (0, W_OUT_BLOCK, 0)),
                  row(D_MODEL), row(D_GRP), row(D_GRP), _full((1, D_GRP)), _full((1, D_GRP)),
                  _full((D_GRP, D_GRP))],
        out_specs=(row(D_MODEL), row(D_GRP), row(D_GRP), row(D_GRP),
                   _full((N_DEV, OUT_SHARD, D_MODEL)), _full((1, D_MODEL)), _full((1, D_GRP)), _full((1, D_GRP)),
                   _view_spec(tm, 4), _view_spec(tm, 4), _view_spec(tm, 16), _view_spec(tm, 16)),
        out_shape=(jax.ShapeDtypeStruct((s_len, D_MODEL), F32), jax.ShapeDtypeStruct((s_len, D_GRP), BF16),
                   jax.ShapeDtypeStruct((s_len, D_GRP), F32), jax.ShapeDtypeStruct((s_len, D_GRP), BF16),
                   jax.ShapeDtypeStruct((N_DEV, OUT_SHARD, D_MODEL), BF16),
                   jax.ShapeDtypeStruct((1, D_MODEL), F32), jax.ShapeDtypeStruct((1, D_GRP), F32),
                   jax.ShapeDtypeStruct((1, D_GRP), F32),
                   _view_shape(s_len, 4, BF16), _view_shape(s_len, 4, F32),
                   _view_shape(s_len, 16, BF16), _view_shape(s_len, 16, F32)),
        scratch_shapes=[pltpu.VMEM((D_MODEL, D_MODEL), F32), pltpu.VMEM((8, tm, 128), F32),
                        pltpu.VMEM((8, tm, 128), F32)],
        compiler_params=_params(),
    )(dy, dh2, x1, wn2, b_g, mixed, o_dil, o_sb, wdil, wsb, bd512)


def _qkv_bwd(dq_b, dk_b, dv_b, dqs, dks, dvs, qraw, kraw, cos2, sin2, qnw, knw, bd):
    s_len = qraw.shape[0]
    tm = ROW_TILE
    ni = s_len // tm

    def body(dq1, dk1, dv1, dq4, dk4, dv4, dq16, dk16, dv16, dqs_ref, dks_ref, dvs_ref,
             qraw_ref, kraw_ref, cos_ref, sin_ref, qnw_ref, knw_ref, bd_ref,
             dproj_ref, dqn_ref, dkn_ref, stage, nat4, nat16):
        i = pl.program_id(0)

        @pl.when(i == 0)
        def _():
            dqn_ref[...] = jnp.zeros_like(dqn_ref)
            dkn_ref[...] = jnp.zeros_like(dkn_ref)

        _merge_views((dq4, dk4, dv4), (dq16, dk16, dv16), stage, nat4, nat16)
        cos_t, sin_t, bdm = cos_ref[...], sin_ref[...], bd_ref[...]
        for grp, (part1, raw_ref, nw_ref, dn_ref) in enumerate(((dq1, qraw_ref, qnw_ref, dqn_ref),
                                                                (dk1, kraw_ref, knw_ref, dkn_ref))):
            dn_acc = 0.0
            for p in range(4):
                cols = slice(128 * p, 128 * (p + 1))
                d_rope = part1[:, cols] + nat4[4 * grp + p] + nat16[4 * grp + p]
                d_norm = d_rope * cos_t + _swap_halves(d_rope * sin_t)
                t = raw_ref[:, cols]
                w = nw_ref[...]
                r = lax.rsqrt(_mm_split(t * t, bdm) * (1.0 / HEAD_DIM) + EPS)
                gw = d_norm * w
                corr = _mm_split(gw * t, bdm) * (1.0 / HEAD_DIM)
                dt = r * (gw - t * ((r * r) * corr))
                dn_acc = dn_acc + jnp.sum(d_norm * t * r, axis=0, keepdims=True)
                dproj_ref[:, D_GRP * grp + 128 * p:D_GRP * grp + 128 * (p + 1)] = dt.astype(BF16)
            dn_ref[...] += dn_acc
        dproj_ref[:, 2 * D_GRP:3 * D_GRP] = (dv1[...] + _slab_group(nat4, 2) + _slab_group(nat16, 2)).astype(BF16)
        dproj_ref[:, 3 * D_GRP:4 * D_GRP] = dqs_ref[...].astype(BF16)
        dproj_ref[:, 4 * D_GRP:5 * D_GRP] = dks_ref[...].astype(BF16)
        dproj_ref[:, 5 * D_GRP:6 * D_GRP] = dvs_ref[...].astype(BF16)

    row = lambda w: pl.BlockSpec((tm, w), lambda i: (i, 0))
    return pl.pallas_call(
        body, name="qkv_bwd", grid=(ni,),
        in_specs=[row(D_GRP)] * 3 + [_view_spec(tm, 4)] * 3 + [_view_spec(tm, 16)] * 3 + [row(D_GRP)] * 5
        + [row(128), row(128), _full((1, 128)), _full((1, 128)), _full((128, 128))],
        out_specs=(row(D_IN), _full((1, 128)), _full((1, 128))),
        out_shape=(jax.ShapeDtypeStruct((s_len, D_IN), BF16), jax.ShapeDtypeStruct((1, 128), F32),
                   jax.ShapeDtypeStruct((1, 128), F32)),
        scratch_shapes=[pltpu.VMEM((12, tm, 128), F32)] * 3,
        compiler_params=_params(),
    )(dq_b[0], dk_b[0], dv_b[0], dq_b[1], dk_b[1], dv_b[1], dq_b[2], dk_b[2], dv_b[2],
      dqs, dks, dvs, qraw, kraw, cos2, sin2, qnw, knw, bd)


def _in_bwd_dx(dproj, a_g, x2, dx1, wn1):
    s_len = x2.shape[0]
    tm = ROW_TILE
    ni = s_len // tm

    def body(dp_ref, w_ref, x_ref, dx1_ref, wn_ref, gx_ref, dwn_ref, w_full):
        i = pl.program_id(0)

        @pl.when(i == 0)
        def _():
            dwn_ref[...] = jnp.zeros_like(dwn_ref)
            for d in range(N_DEV):
                w_full[:, IN_SHARD * d:IN_SHARD * (d + 1)] = w_ref[d]

        dh = _dot_nt(dp_ref[...], w_full[...])
        dnorm, dw_rows = _rms_bwd(dh, x_ref[...], wn_ref[...])
        gx_ref[...] = dx1_ref[...] + dnorm
        dwn_ref[...] += jnp.sum(dw_rows, axis=0, keepdims=True)

    row = lambda w: pl.BlockSpec((tm, w), lambda i: (i, 0))
    return pl.pallas_call(
        body, name="in_bwd_dx", grid=(ni,),
        in_specs=[row(D_IN), pl.BlockSpec((N_DEV, D_MODEL, IN_SHARD), lambda i: (0, 0, 0)),
                  row(D_MODEL), row(D_MODEL), _full((1, D_MODEL))],
        out_specs=(row(D_MODEL), _full((1, D_MODEL))),
        out_shape=(jax.ShapeDtypeStruct((s_len, D_MODEL), F32), jax.ShapeDtypeStruct((1, D_MODEL), F32)),
        scratch_shapes=[pltpu.VMEM((D_MODEL, D_IN), BF16)],
        compiler_params=_params(),
    )(dproj, a_g, x2, dx1, wn1)


def _in_bwd_dw(h1, dproj):
    s_len = h1.shape[0]
    tm = ROW_TILE
    ni = s_len // tm

    def body(h_ref, dp_ref, dw_ref, acc):
        i = pl.program_id(1)

        @pl.when(i == 0)
        def _():
            acc[...] = jnp.zeros_like(acc)

        acc[...] += _dot_tn(h_ref[...], dp_ref[...])

        @pl.when(i == ni - 1)
        def _():
            for half in range(2):
                dw_ref[half] = acc[:, IN_SHARD * half:IN_SHARD * (half + 1)].astype(BF16)

    return pl.pallas_call(
        body, name="in_bwd_dw", grid=(N_DEV // 2, ni),
        in_specs=[pl.BlockSpec((tm, D_MODEL), lambda d, i: (i, 0)),
                  pl.BlockSpec((tm, 2 * IN_SHARD), lambda d, i: (i, d))],
        out_specs=pl.BlockSpec((2, D_MODEL, IN_SHARD), lambda d, i: (d, 0, 0)),
        out_shape=jax.ShapeDtypeStruct((N_DEV, D_MODEL, IN_SHARD), BF16),
        scratch_shapes=[pltpu.VMEM((D_MODEL, 2 * IN_SHARD), F32)],
        compiler_params=_params(),
    )(h1, dproj)


def _adamw(recv, w, m, v):
    rows, cols = w.shape
    tr = next((t for t in (128, 32) if rows % t == 0), rows)

    def body(p_ref, w_ref, m_ref, v_ref, g_ref, d_ref, nm_ref, nv_ref):
        g = p_ref[0].astype(F32)
        for s in range(1, N_DEV):
            g = g + p_ref[s].astype(F32)
        m_new = ADAM_B1 * m_ref[...] + (1.0 - ADAM_B1) * g
        v_new = ADAM_B2 * v_ref[...] + (1.0 - ADAM_B2) * (g * g)
        m_hat = m_new / (1.0 - ADAM_B1 ** ADAM_STEP)
        v_hat = v_new / (1.0 - ADAM_B2 ** ADAM_STEP)
        g_ref[...] = g
        d_ref[...] = -ADAM_LR * (m_hat / (jnp.sqrt(v_hat) + ADAM_EPS) + ADAM_WD * w_ref[...])
        nm_ref[...] = m_new
        nv_ref[...] = v_new

    blk = pl.BlockSpec((tr, cols), lambda i: (i, 0))
    out = jax.ShapeDtypeStruct((rows, cols), F32)
    return pl.pallas_call(
        body, name=f"adamw_{rows}x{cols}", grid=(rows // tr,),
        in_specs=[pl.BlockSpec((N_DEV, tr, cols), lambda i: (0, i, 0)), blk, blk, blk],
        out_specs=(blk,) * 4, out_shape=(out,) * 4,
        compiler_params=_params(),
    )(recv, w, m, v)


def _rope_tables(s_len):
    pos = jnp.arange(s_len, dtype=F32)
    inv_freq = ROPE_THETA ** (-jnp.arange(0, HEAD_DIM, 2, dtype=F32) / HEAD_DIM)
    ang = pos[:, None] * inv_freq[None, :]
    cos, sin = jnp.cos(ang), jnp.sin(ang)
    cos2 = jnp.concatenate([cos, cos, cos, cos], axis=1)
    sin2 = jnp.concatenate([-sin, sin, -sin, sin], axis=1)
    return cos2, sin2


def _block_diag_ones(n):
    i = jnp.arange(n)
    return (i[:, None] // HEAD_DIM == i[None, :] // HEAD_DIM).astype(BF16)


def _pad_cols(t):
    return jnp.pad(t, ((0, 0), (0, FF_PAD - FF_SHARD)))


def _pad_rows(t):
    return jnp.pad(t, ((0, FF_PAD - FF_SHARD), (0, 0)))


LOSS_ROW = 26


def _pack_small(n1, n2, ndil, nsb, nq, nk, scalar=None):
    pad = lambda t: jnp.pad(t.reshape(1, -1), ((0, 0), (0, 128 - t.size)))
    last = jnp.zeros((1, 128), F32) if scalar is None else pad(scalar)
    rows = [n1.reshape(8, 128), n2.reshape(8, 128), ndil.reshape(4, 128), nsb.reshape(4, 128),
            pad(nq), pad(nk), last, jnp.zeros((5, 128), F32)]
    return jnp.concatenate(rows, axis=0)


def _unpack_small(t):
    return (t[0:8].reshape(1, D_MODEL), t[8:16].reshape(1, D_MODEL), t[16:20].reshape(1, D_GRP),
            t[20:24].reshape(1, D_GRP), t[24:25, :HEAD_DIM], t[25:26, :HEAD_DIM])


def kernel(x, attn_norm_w, w_in, q_norm_w, k_norm_w, dil_out_norm_w, sb_out_norm_w, w_out, ffn_norm_w, w_gate, w_up, w_down, loss_target, m_attn_norm_w, m_w_in, m_q_norm_w, m_k_norm_w, m_dil_out_norm_w, m_sb_out_norm_w, m_w_out, m_ffn_norm_w, m_w_gate, m_w_up, m_w_down, v_attn_norm_w, v_w_in, v_q_norm_w, v_k_norm_w, v_dil_out_norm_w, v_sb_out_norm_w, v_w_out, v_ffn_norm_w, v_w_gate, v_w_up, v_w_down):
    s_len = x.shape[1]
    x2, tgt = x[0], loss_target[0]

    w_loc = jnp.concatenate([_pad_cols(w_gate[0]).T, _pad_cols(w_up[0]).T, _pad_rows(w_down[0]), w_out[0]],
                            axis=0).astype(BF16)
    a_g, w_g = _gather_weights([w_in[0].astype(BF16), w_loc])

    cos2, sin2 = _rope_tables(s_len)
    bd128, bd512 = _block_diag_ones(128), _block_diag_ones(D_GRP)
    idx = jnp.arange(SB_TILE)
    tri_suf = (idx[:, None] > idx[None, :]).astype(BF16)
    tri_pre = (idx[:, None] < idx[None, :]).astype(BF16)
    qnw2 = jnp.concatenate([q_norm_w, q_norm_w], axis=1)
    knw2 = jnp.concatenate([k_norm_w, k_norm_w], axis=1)

    (h1, qraw, kraw, q, k, va, qs, ks, vs,
     q4, k4, v4, q16, k16, v16) = _attn_in(x2, attn_norm_w, a_g, cos2, sin2, qnw2, knw2, bd128)
    qkv_views = {1: (q, k, va), 4: (q4, k4, v4), 16: (q16, k16, v16)}
    o_b, lse_b = [], []
    for r in DILATIONS:
        o, lse = _dil_fwd(*qkv_views[r], r)
        o_b.append(o)
        lse_b.append(lse)
    o_sb, c_sb = _sb_fwd(qs, ks, vs, tri_suf)
    o_dil, lse_tot, lse4, lse16, mixed, x1 = _attn_out(o_b, lse_b, o_sb, x2, dil_out_norm_w, sb_out_norm_w, w_g)
    g, u, h2, dy, loss_parts = _ffn_fwd(x1, ffn_norm_w, tgt, w_g)
    loss_local = jnp.sum(loss_parts[::8, 0])

    dg, du, act, dh2 = _ffn_bwd_dx(dy, g, u, w_g)
    (dx1, do_dil, delta, do_sb, dwout, dn2, dndil, dnsb, do4, dl4, do16, dl16) = _attn_out_bwd(
        dy, dh2, x1, ffn_norm_w, w_g, mixed, o_dil, o_sb, dil_out_norm_w, sb_out_norm_w, bd512)
    dwg, dwu, dwd = _ffn_bwd_dw(h2, dy, dg, du, act)
    dqs, dks, dvs = _sb_bwd(qs, ks, vs, do_sb, c_sb, tri_suf, tri_pre)
    cot_views = {1: (do_dil, lse_tot, delta), 4: (do4, lse4, dl4), 16: (do16, lse16, dl16)}
    dq_b, dk_b, dv_b = [], [], []
    for r in DILATIONS:
        dq, dk, dv = _dil_bwd(*qkv_views[r], *cot_views[r], r)
        dq_b.append(dq)
        dk_b.append(dk)
        dv_b.append(dv)
    dproj, dqn2, dkn2 = _qkv_bwd(dq_b, dk_b, dv_b, dqs, dks, dvs, qraw, kraw, cos2, sin2, qnw2, knw2, bd128)
    grad_x, dn1 = _in_bwd_dx(dproj, a_g, x2, dx1, attn_norm_w)
    dwin = _in_bwd_dw(h1, dproj)
    dqn = dqn2[:, :HEAD_DIM] + dqn2[:, HEAD_DIM:]
    dkn = dkn2[:, :HEAD_DIM] + dkn2[:, HEAD_DIM:]

    small = _pack_small(dn1, dn2, dndil, dnsb, dqn, dkn, loss_local)
    r_in, r_gate, r_up, r_down, r_out, r_small = _exchange_grads([dwin, dwg, dwu, dwd, dwout], small)
    big = {
        "w_in": _adamw(r_in, w_in[0], m_w_in[0], v_w_in[0]),
        "w_gate": tuple(t[:, :FF_SHARD] for t in _adamw(r_gate, _pad_cols(w_gate[0]), _pad_cols(m_w_gate[0]), _pad_cols(v_w_gate[0]))),
        "w_up": tuple(t[:, :FF_SHARD] for t in _adamw(r_up, _pad_cols(w_up[0]), _pad_cols(m_w_up[0]), _pad_cols(v_w_up[0]))),
        "w_down": _adamw(r_down, w_down[0], m_w_down[0], v_w_down[0]),
        "w_out": _adamw(r_out, w_out[0], m_w_out[0], v_w_out[0]),
    }
    packs = [_pack_small(*ts) for ts in (
        (attn_norm_w, ffn_norm_w, dil_out_norm_w, sb_out_norm_w, q_norm_w, k_norm_w),
        (m_attn_norm_w, m_ffn_norm_w, m_dil_out_norm_w, m_sb_out_norm_w, m_q_norm_w, m_k_norm_w),
        (v_attn_norm_w, v_ffn_norm_w, v_dil_out_norm_w, v_sb_out_norm_w, v_q_norm_w, v_k_norm_w))]
    small_raw = _adamw(r_small, *packs)
    loss = small_raw[0][LOSS_ROW, 0]
    small_out = [_unpack_small(t) for t in small_raw]
    names = ["attn_norm_w", "w_in", "q_norm_w", "k_norm_w", "dil_out_norm_w", "sb_out_norm_w", "w_out",
             "ffn_norm_w", "w_gate", "w_up", "w_down"]
    small_pos = {"attn_norm_w": 0, "ffn_norm_w": 1, "dil_out_norm_w": 2, "sb_out_norm_w": 3,
                 "q_norm_w": 4, "k_norm_w": 5}
    outs = [loss, grad_x[None]]
    for kind in range(4):
        for name in names:
            if name in small_pos:
                outs.append(small_out[kind][small_pos[name]])
            else:
                outs.append(big[name][kind][None])
    return tuple(outs)
```

```python
import functools

import jax
import jax.numpy as jnp
from jax import lax
from jax.experimental import pallas as pl
from jax.experimental.pallas import tpu as pltpu

F32 = jnp.float32
BF16 = jnp.bfloat16

N_DEV = 8
D_MODEL = 1024
HEAD_DIM = 64
D_GRP = 512
D_IN = 6 * D_GRP
IN_SHARD = D_IN // N_DEV
FF_SHARD = 352
FF_PAD = 384
FF_BLOCK = 2 * FF_PAD
FF_STEPS = N_DEV // 2
W_PACK_ROWS = 3 * FF_PAD + 128
W_OUT_BLOCK = 3 * FF_PAD // 128
OUT_SHARD = D_MODEL // N_DEV
BLOCK = 128
DILATIONS = (1, 4, 16)
ROPE_THETA = 10000.0
EPS = 1e-6
ATT_SCALE = HEAD_DIM ** -0.5
NEG = -1e30

ADAM_LR = 0.001
ADAM_B1 = 0.9
ADAM_B2 = 0.999
ADAM_EPS = 1e-08
ADAM_WD = 0.01
ADAM_STEP = 10

SB_TILE = 256
SB_DEAD = -104.0
SB_PAIRS = 4
SB_BWD_PAIRS = 2
ROW_TILE = 512
VMEM_LIMIT = 56 * 1024 * 1024
MESH = pl.DeviceIdType.MESH


def _dot(a, b):
    return jnp.dot(a, b, preferred_element_type=F32)


def _dot_nt(a, b):
    return lax.dot_general(a, b, (((1,), (1,)), ((), ())), preferred_element_type=F32)


def _dot_tn(a, b):
    return lax.dot_general(a, b, (((0,), (0,)), ((), ())), preferred_element_type=F32)


def _mm_split(t, m):
    hi = t.astype(BF16)
    lo = (t - hi.astype(F32)).astype(BF16)
    return _dot(hi, m) + _dot(lo, m)


def _params(**kw):
    return pltpu.CompilerParams(vmem_limit_bytes=VMEM_LIMIT, **kw)


def _full(shape):
    nd = len(shape)
    return pl.BlockSpec(shape, lambda *_: (0,) * nd)


def _view_shape(s_len, r, dtype):
    return jax.ShapeDtypeStruct((s_len // r, r * D_GRP), dtype)


def _view_spec(tm, r):
    return pl.BlockSpec((tm // r, r * D_GRP), lambda i: (i, 0))


def _swap_halves(t):
    lane = lax.broadcasted_iota(jnp.int32, t.shape, 1)
    first = (lane & 32) == 0
    return jnp.where(first, pltpu.roll(t, 96, 1), pltpu.roll(t, 32, 1))


def _log_sigmoid(z):
    return jnp.minimum(z, 0.0) - jnp.log(1.0 + jnp.exp(-jnp.abs(z)))


def _log_sigmoid_pair(z):
    neg_abs = lax.bitcast_convert_type(lax.bitcast_convert_type(z, jnp.uint32) | jnp.uint32(0x80000000), F32)
    lb = jnp.minimum(z, 0.0) - jnp.log(1.0 + jnp.exp(neg_abs))
    return lb, lb - z


def _cumsum_mm(t, tri):
    return _dot(t.astype(BF16), tri)


def _split_views(src_ref, stage_ref, views4, views16):
    slabs, n, _ = src_ref.shape
    n4, n16 = n // 4, n // 16
    for j in range(slabs):
        g, lanes = j // 4, 128 * (j % 4)
        src, stage = src_ref.at[j], stage_ref.at[j]
        for c4 in range(4):
            blk = src[pl.ds(c4, n4, stride=4), :]
            stage[n4 * c4:n4 * (c4 + 1), :] = blk
            col = D_GRP * c4 + lanes
            views4[g][:, col:col + 128] = blk.astype(views4[g].dtype)
        for c4 in range(4):
            for c1 in range(4):
                blk = stage[pl.ds(n4 * c4 + c1, n16, stride=4), :]
                col = D_GRP * (4 * c1 + c4) + lanes
                views16[g][:, col:col + 128] = blk.astype(views16[g].dtype)


def _merge_views(views4, views16, stage_ref, dst4_ref, dst16_ref):
    slabs, n, _ = dst4_ref.shape
    n4, n16 = n // 4, n // 16
    for j in range(slabs):
        g, lanes = j // 4, 128 * (j % 4)
        dst4, dst16, stage = dst4_ref.at[j], dst16_ref.at[j], stage_ref.at[j]
        for c4 in range(4):
            col = D_GRP * c4 + lanes
            dst4[pl.ds(c4, n4, stride=4), :] = views4[g][:, col:col + 128].astype(F32)
            for c1 in range(4):
                col = D_GRP * (4 * c1 + c4) + lanes
                stage[pl.ds(n4 * c4 + c1, n16, stride=4), :] = views16[g][:, col:col + 128].astype(F32)
        for c4 in range(4):
            dst16[pl.ds(c4, n4, stride=4), :] = stage[n4 * c4:n4 * (c4 + 1), :]


def _slab_group(ref, g):
    return jnp.concatenate([ref[4 * g + p] for p in range(4)], axis=1)


def _mesh_pos():
    return lax.axis_index("x"), lax.axis_index("y"), lax.axis_index("c")


def _flat_index(p):
    return 4 * p[0] + 2 * p[1] + p[2]


def _gather_weights(shards):
    n_arr = len(shards)

    def body(*refs):
        srcs, outs = refs[:n_arr], refs[n_arr:2 * n_arr]
        send_sems, recv_sems, local_sems = refs[2 * n_arr:]
        x, y, c = _mesh_pos()
        me, sibling = (x, y, c), (x, y, 1 - c)
        chips = [(1 - x, y), (x, 1 - y), (1 - x, 1 - y)]

        def copy(arr, k, block, to, own=False):
            dst = outs[arr].at[_flat_index(block)]
            return pltpu.make_async_remote_copy(
                src_ref=srcs[arr] if own else dst, dst_ref=dst,
                send_sem=send_sems.at[arr, k], recv_sem=recv_sems.at[arr, k],
                device_id=to, device_id_type=MESH)

        for arr in range(n_arr):
            mine = pltpu.make_async_copy(srcs[arr], outs[arr].at[_flat_index(me)], local_sems.at[arr])
            mine.start()
            first = [copy(arr, 0, me, sibling, own=True)]
            first += [copy(arr, 1 + j, me, (*chip, c), own=True) for j, chip in enumerate(chips)]
            for cp in first:
                cp.start()
        for arr in range(n_arr):
            passed = [copy(arr, 4 + j, (*chip, c), sibling) for j, chip in enumerate(chips)]
            for j, chip in enumerate(chips):
                copy(arr, 1 + j, (*chip, c), me).wait_recv()
                passed[j].start()
        for arr in range(n_arr):
            copy(arr, 0, sibling, me).wait_recv()
            for j, chip in enumerate(chips):
                copy(arr, 4 + j, (*chip, 1 - c), me).wait_recv()
            for k in range(7):
                copy(arr, k, me, me).wait_send()
            pltpu.make_async_copy(srcs[arr], outs[arr].at[_flat_index(me)], local_sems.at[arr]).wait()

    any_spec = pl.BlockSpec(memory_space=pl.ANY)
    return pl.pallas_call(
        body, name="gather_weights",
        out_shape=tuple(jax.ShapeDtypeStruct((N_DEV,) + s.shape, s.dtype) for s in shards),
        in_specs=[any_spec] * n_arr, out_specs=(any_spec,) * n_arr,
        scratch_shapes=[pltpu.SemaphoreType.DMA((n_arr, 7)), pltpu.SemaphoreType.DMA((n_arr, 7)),
                        pltpu.SemaphoreType.DMA((n_arr,))],
        compiler_params=pltpu.CompilerParams(has_side_effects=True),
    )(*shards)


_HBM_SPEC = pl.BlockSpec(memory_space=pltpu.HBM)
_SEM_SPEC = pl.BlockSpec(memory_space=pltpu.SEMAPHORE)
_DATAFLOW = pltpu.SideEffectType.DATAFLOW_SIDE_EFFECTING


def _peer_list(x, y, c):
    return [(1 - x if m & 4 else x, 1 - y if m & 2 else y, 1 - c if m & 1 else c) for m in range(1, N_DEV)]


def _spread_copies(src_refs, land_refs, send_sems, recv_sems, blockwise):
    x, y, c = _mesh_pos()
    my_idx = _flat_index((x, y, c))
    copies = []
    for a, (src, land) in enumerate(zip(src_refs, land_refs)):
        for k, peer in enumerate(_peer_list(x, y, c)):
            copies.append(pltpu.make_async_remote_copy(
                src_ref=src.at[_flat_index(peer)] if blockwise else src, dst_ref=land.at[my_idx],
                send_sem=send_sems.at[(N_DEV - 1) * a + k], recv_sem=recv_sems.at[(N_DEV - 1) * a + k],
                device_id=peer, device_id_type=MESH))
    return copies


def _spread_start(srcs, lands, blockwise, name, after=None):
    n = len(srcs)
    extra = [] if after is None else [after]

    def body(*refs):
        ins, outs = refs[:2 * n], refs[2 * n + len(extra):]
        for cp in _spread_copies(ins[:n], ins[n:], outs[0], outs[1], blockwise):
            cp.start()
        token = refs[-1]
        token[...] = jnp.zeros_like(token)

    hbm = lambda t: pltpu.HBM(t.shape, t.dtype)
    sems = pltpu.SemaphoreType.DMA((n * (N_DEV - 1),))
    outs = pl.pallas_call(
        body, name=name,
        out_shape=(sems, sems) + tuple(hbm(t) for t in srcs) + tuple(hbm(t) for t in lands)
        + (jax.ShapeDtypeStruct((8, 128), F32),),
        in_specs=[_HBM_SPEC] * (2 * n) + [pl.BlockSpec(memory_space=pl.ANY)] * len(extra),
        out_specs=(_SEM_SPEC, _SEM_SPEC) + (_HBM_SPEC,) * (2 * n) + (pl.BlockSpec(memory_space=pltpu.VMEM),),
        input_output_aliases={i: 2 + i for i in range(2 * n)},
        compiler_params=pltpu.CompilerParams(has_side_effects=_DATAFLOW),
    )(*[pltpu.with_memory_space_constraint(t, pltpu.HBM) for t in list(srcs) + list(lands)], *extra)
    return outs[0], outs[1], outs[2:2 + n], outs[2 + n:2 + 2 * n], outs[-1]


def _spread_wait(send_sems, recv_sems, srcs, lands, after, blockwise, name):
    n = len(srcs)

    def body(*refs):
        for cp in _spread_copies(refs[:n], refs[n:2 * n], refs[2 * n], refs[2 * n + 1], blockwise):
            cp.wait_send()
            cp.wait_recv()

    hbm = lambda t: pltpu.HBM(t.shape, t.dtype)
    outs = pl.pallas_call(
        body, name=name,
        out_shape=tuple(hbm(t) for t in srcs) + tuple(hbm(t) for t in lands),
        in_specs=[_HBM_SPEC] * (2 * n) + [_SEM_SPEC, _SEM_SPEC, pl.BlockSpec(memory_space=pl.ANY)],
        out_specs=(_HBM_SPEC,) * (2 * n),
        input_output_aliases={i: i for i in range(2 * n)},
        compiler_params=pltpu.CompilerParams(has_side_effects=_DATAFLOW),
    )(*srcs, *lands, send_sems, recv_sems, after)
    return outs[n:]


def _exchange_grads(parts, small):
    n_arr = len(parts)

    def body(*refs):
        ins, outs = refs[:n_arr + 1], refs[n_arr + 1:2 * (n_arr + 1)]
        send_sems, recv_sems, local_sems = refs[2 * (n_arr + 1):]
        x, y, c = _mesh_pos()
        me = (x, y, c)
        my_idx = _flat_index(me)
        peers = []
        for m in range(1, N_DEV):
            peers.append((1 - x if m & 4 else x, 1 - y if m & 2 else y, 1 - c if m & 1 else c))

        def src_block(arr, dev):
            return ins[arr] if arr == n_arr else ins[arr].at[_flat_index(dev)]

        def copy(arr, k):
            return pltpu.make_async_remote_copy(
                src_ref=src_block(arr, peers[k]), dst_ref=outs[arr].at[my_idx],
                send_sem=send_sems.at[arr, k], recv_sem=recv_sems.at[arr, k],
                device_id=peers[k], device_id_type=MESH)

        def local(arr):
            return pltpu.make_async_copy(src_block(arr, me), outs[arr].at[my_idx], local_sems.at[arr])

        for arr in range(n_arr + 1):
            local(arr).start()
            for k in range(N_DEV - 1):
                copy(arr, k).start()
        for arr in range(n_arr + 1):
            for k in range(N_DEV - 1):
                cp = copy(arr, k)
                cp.wait_send()
                cp.wait_recv()
            local(arr).wait()

    any_spec = pl.BlockSpec(memory_space=pl.ANY)
    out_shape = tuple(jax.ShapeDtypeStruct(p.shape, p.dtype) for p in parts)
    out_shape += (jax.ShapeDtypeStruct((N_DEV,) + small.shape, small.dtype),)
    return pl.pallas_call(
        body, name="exchange_grads",
        out_shape=out_shape,
        in_specs=[any_spec] * (n_arr + 1), out_specs=(any_spec,) * (n_arr + 1),
        scratch_shapes=[pltpu.SemaphoreType.DMA((n_arr + 1, N_DEV - 1)),
                        pltpu.SemaphoreType.DMA((n_arr + 1, N_DEV - 1)),
                        pltpu.SemaphoreType.DMA((n_arr + 1,))],
        compiler_params=pltpu.CompilerParams(has_side_effects=True),
    )(*parts, small)


def _call_with_exchange(body, rides, first_step, last_step, *, name, grid, in_specs, out_specs, out_shape,
                        scratch_shapes=()):
    out_specs = tuple(out_specs) if isinstance(out_specs, (tuple, list)) else (out_specs,)
    out_shape = tuple(out_shape) if isinstance(out_shape, (tuple, list)) else (out_shape,)
    n_in, n_out, n_scr, n = len(in_specs), len(out_specs), len(scratch_shapes), len(rides)
    if n == 0:
        return pl.pallas_call(body, name=name, grid=grid, in_specs=list(in_specs), out_specs=out_specs,
                              out_shape=out_shape, scratch_shapes=list(scratch_shapes),
                              compiler_params=_params())

    def full_body(*refs):
        ins, srcs = refs[:n_in], refs[n_in:n_in + n]
        outs, lands = refs[n_in + n:n_in + n + n_out], refs[n_in + n + n_out:n_in + 2 * n + n_out]
        scratch = refs[n_in + 2 * n + n_out:n_in + 2 * n + n_out + n_scr]
        send_sems, recv_sems, local_sems = refs[-3:]
        x, y, c = _mesh_pos()
        my_idx = _flat_index((x, y, c))
        peers = _peer_list(x, y, c)

        def remote(a, k):
            return pltpu.make_async_remote_copy(
                src_ref=srcs[a].at[_flat_index(peers[k])], dst_ref=lands[a].at[my_idx],
                send_sem=send_sems.at[a, k], recv_sem=recv_sems.at[a, k],
                device_id=peers[k], device_id_type=MESH)

        def local(a):
            return pltpu.make_async_copy(srcs[a].at[my_idx], lands[a].at[my_idx], local_sems.at[a])

        @pl.when(first_step())
        def _():
            for a in range(n):
                local(a).start()
                for k in range(N_DEV - 1):
                    remote(a, k).start()

        body(*ins, *outs, *scratch)

        @pl.when(last_step())
        def _():
            for a in range(n):
                for k in range(N_DEV - 1):
                    cp = remote(a, k)
                    cp.wait_send()
                    cp.wait_recv()
                local(a).wait()

    any_spec = pl.BlockSpec(memory_space=pl.ANY)
    res = pl.pallas_call(
        full_body, name=name, grid=grid,
        in_specs=list(in_specs) + [any_spec] * n,
        out_specs=out_specs + (any_spec,) * n,
        out_shape=out_shape + tuple(jax.ShapeDtypeStruct(t.shape, t.dtype) for t in rides),
        scratch_shapes=list(scratch_shapes) + [pltpu.SemaphoreType.DMA((n, N_DEV - 1)),
                                               pltpu.SemaphoreType.DMA((n, N_DEV - 1)),
                                               pltpu.SemaphoreType.DMA((n,))],
        compiler_params=_params(has_side_effects=True),
    )
    return res


def _head_norm(t, w128, bd):
    ms = _mm_split(t * t, bd) * (1.0 / HEAD_DIM)
    r = lax.rsqrt(ms + EPS)
    return (t * r) * w128, r


def _attn_in(x2, wn1, a_g, cos2, sin2, qnw, knw, bd):
    s_len = x2.shape[0]
    tm = ROW_TILE

    def body(x_ref, wn_ref, w_ref, cos_ref, sin_ref, qnw_ref, knw_ref, bd_ref,
             h1_ref, qraw_ref, kraw_ref, q_ref, k_ref, va_ref, qs_ref, ks_ref, vs_ref,
             q4_ref, k4_ref, v4_ref, q16_ref, k16_ref, v16_ref, proj, slabs, stage, w_full):
        @pl.when(pl.program_id(0) == 0)
        def _():
            for d in range(N_DEV):
                w_full[:, IN_SHARD * d:IN_SHARD * (d + 1)] = w_ref[d]

        xx = x_ref[...]
        r = lax.rsqrt(jnp.mean(xx * xx, axis=-1, keepdims=True) + EPS)
        h = ((xx * r) * wn_ref[...]).astype(BF16)
        h1_ref[...] = h
        proj[...] = _dot(h, w_full[...])
        cos_t, sin_t, bdm = cos_ref[...], sin_ref[...], bd_ref[...]
        for grp, (raw_ref, rope_ref, nw_ref) in enumerate(((qraw_ref, q_ref, qnw_ref),
                                                           (kraw_ref, k_ref, knw_ref))):
            for p in range(4):
                cols = slice(D_GRP * grp + 128 * p, D_GRP * grp + 128 * (p + 1))
                t = proj[:, cols]
                raw_ref[:, 128 * p:128 * (p + 1)] = t
                yn, _ = _head_norm(t, nw_ref[...], bdm)
                roped = yn * cos_t + _swap_halves(yn) * sin_t
                slabs[4 * grp + p] = roped
                rope_ref[:, 128 * p:128 * (p + 1)] = roped.astype(BF16)
        for p in range(4):
            slabs[8 + p] = proj[:, 2 * D_GRP + 128 * p:2 * D_GRP + 128 * (p + 1)]
        for grp, ref in ((2, va_ref), (3, qs_ref), (4, ks_ref), (5, vs_ref)):
            ref[...] = proj[:, D_GRP * grp:D_GRP * (grp + 1)].astype(BF16)
        _split_views(slabs, stage, (q4_ref, k4_ref, v4_ref), (q16_ref, k16_ref, v16_ref))

    row = lambda w: pl.BlockSpec((tm, w), lambda i: (i, 0))
    grp_bf = jax.ShapeDtypeStruct((s_len, D_GRP), BF16)
    grp_f32 = jax.ShapeDtypeStruct((s_len, D_GRP), F32)
    return pl.pallas_call(
        body, name="attn_in", grid=(s_len // tm,),
        in_specs=[row(D_MODEL), _full((1, D_MODEL)),
                  pl.BlockSpec((N_DEV, D_MODEL, IN_SHARD), lambda i: (0, 0, 0)),
                  row(128), row(128), _full((1, 128)), _full((1, 128)), _full((128, 128))],
        out_specs=(row(D_MODEL),) + (row(D_GRP),) * 8 + (_view_spec(tm, 4),) * 3 + (_view_spec(tm, 16),) * 3,
        out_shape=(jax.ShapeDtypeStruct((s_len, D_MODEL), BF16), grp_f32, grp_f32) + (grp_bf,) * 6
        + (_view_shape(s_len, 4, BF16),) * 3 + (_view_shape(s_len, 16, BF16),) * 3,
        scratch_shapes=[pltpu.VMEM((tm, D_IN), F32), pltpu.VMEM((12, tm, 128), F32), pltpu.VMEM((12, tm, 128), F32),
                        pltpu.VMEM((D_MODEL, D_IN), BF16)],
        compiler_params=_params(),
    )(x2, wn1, a_g, cos2, sin2, qnw, knw, bd)


def _band_mask(n):
    i = lax.broadcasted_iota(jnp.int32, (2 * BLOCK, 2 * BLOCK), 0) & (BLOCK - 1)
    j = lax.broadcasted_iota(jnp.int32, (2 * BLOCK, 2 * BLOCK), 1)
    dist = i + BLOCK - j
    return (dist >= 0) & (dist <= BLOCK) & ((n - 1) * BLOCK + j >= 0)


def _stack_heads(t2, head0):
    return jnp.concatenate([jnp.where(head0, t2, 0), jnp.where(head0, 0, t2)], axis=0)


def _unstack_heads(t, head0):
    return jnp.where(head0, t[0:BLOCK], t[BLOCK:2 * BLOCK])


def _dil_fwd(qv, kv, vv, r):
    sub_len = qv.shape[0]
    nb = sub_len // BLOCK

    qb = 2 if nb % 2 == 0 else 1

    def body(q_ref, kp_ref, kc_ref, vp_ref, vc_ref, o_ref, lse_ref):
        n = pl.program_id(1)
        lane = lax.broadcasted_iota(jnp.int32, (BLOCK, 128), 1)
        head0 = lane < HEAD_DIM
        units = [(b, slice(128 * p, 128 * (p + 1))) for b in range(qb) for p in range(4)]
        valid = [_band_mask(qb * n + b) for b in range(qb)]
        rows = [slice(BLOCK * b, BLOCK * (b + 1)) for b in range(qb)]

        def keys(prev_ref, cur_ref, b, c):
            before = prev_ref[:, c] if b == 0 else cur_ref[rows[b - 1], c]
            return jnp.concatenate([before, cur_ref[rows[b], c]], axis=0)

        qqs = [_stack_heads(q_ref[rows[b], c] * ATT_SCALE, head0) for b, c in units]
        kks = [keys(kp_ref, kc_ref, b, c) for b, c in units]
        vvs = [keys(vp_ref, vc_ref, b, c) for b, c in units]
        ss = [_dot_nt(qq, kk) for qq, kk in zip(qqs, kks)]
        prs, dens, lses = [], [], []
        for (b, _), s in zip(units, ss):
            s = jnp.where(valid[b], s, NEG)
            m = jnp.max(s, axis=-1, keepdims=True)
            pr = jnp.exp(s - m)
            den = jnp.sum(pr, axis=-1, keepdims=True)
            prs.append(pr.astype(BF16))
            dens.append(den)
            lses.append(m + jnp.log(den))
        pvs = [_dot(pr, vv2) for pr, vv2 in zip(prs, vvs)]
        for (b, c), pv, den, lse in zip(units, pvs, dens, lses):
            o_ref[rows[b], c] = _unstack_heads(pv / den, head0)
            lse_ref[rows[b], c] = _unstack_heads(jnp.broadcast_to(lse, (2 * BLOCK, 128)), head0)

    cur = pl.BlockSpec((qb * BLOCK, D_GRP), lambda c, n: (n, c))
    prev = pl.BlockSpec((BLOCK, D_GRP), lambda c, n: (jnp.maximum(qb * n - 1, 0), c))
    out = jax.ShapeDtypeStruct(qv.shape, F32)
    return pl.pallas_call(
        body, name=f"dil_fwd_r{r}", grid=(r, nb // qb),
        in_specs=[cur, prev, cur, prev, cur], out_specs=(cur, cur), out_shape=(out, out),
        compiler_params=_params(),
    )(qv, kv, kv, vv, vv)


def _dil_bwd(qv, kv, vv, dov, lsev, deltav, r, rides):
    sub_len = qv.shape[0]
    nb = sub_len // BLOCK

    def body(q_ref, kp_ref, kc_ref, vp_ref, vc_ref, do_ref, lse_ref, dl_ref,
             dq_ref, dk_ref, dv_ref, dk_carry, dv_carry):
        n = pl.program_id(1)

        @pl.when(n == 0)
        def _():
            dk_carry[...] = jnp.zeros_like(dk_carry)
            dv_carry[...] = jnp.zeros_like(dv_carry)

        @pl.when(n < nb)
        def _():
            valid = _band_mask(n)
            lane = lax.broadcasted_iota(jnp.int32, (BLOCK, 128), 1)
            head0 = lane < HEAD_DIM
            pairs = [slice(128 * p, 128 * (p + 1)) for p in range(4)]
            qqs = [_stack_heads(q_ref[:, c] * ATT_SCALE, head0) for c in pairs]
            dos = [_stack_heads(do_ref[:, c], head0) for c in pairs]
            kks = [jnp.concatenate([kp_ref[:, c], kc_ref[:, c]], axis=0) for c in pairs]
            vvs = [jnp.concatenate([vp_ref[:, c], vc_ref[:, c]], axis=0) for c in pairs]
            ss = [_dot_nt(qq, kk) for qq, kk in zip(qqs, kks)]
            dps = [_dot_nt(do, vv2) for do, vv2 in zip(dos, vvs)]
            prs, dss = [], []
            for c, s, dp in zip(pairs, ss, dps):
                stats = []
                for ref in (lse_ref, dl_ref):
                    t2 = ref[:, c]
                    stats.append(jnp.concatenate(
                        [jnp.sum(jnp.where(lane == 0, t2, 0.0), axis=-1, keepdims=True),
                         jnp.sum(jnp.where(lane == HEAD_DIM, t2, 0.0), axis=-1, keepdims=True)], axis=0))
                pr = jnp.where(valid, jnp.exp(jnp.minimum(s - stats[0], 0.0)), 0.0)
                prs.append(pr.astype(BF16))
                dss.append((pr * (dp - stats[1])).astype(BF16))
            dqs = [_dot(ds, kk) for ds, kk in zip(dss, kks)]
            dkks = [_dot_tn(ds, qq) for ds, qq in zip(dss, qqs)]
            dvvs = [_dot_tn(pr, do) for pr, do in zip(prs, dos)]
            for c, dq, dkk, dvv in zip(pairs, dqs, dkks, dvvs):
                dq_ref[:, c] = _unstack_heads(dq, head0) * ATT_SCALE
                dk_ref[:, c] = dk_carry[:, c] + dkk[:BLOCK]
                dv_ref[:, c] = dv_carry[:, c] + dvv[:BLOCK]
                dk_carry[:, c] = dkk[BLOCK:]
                dv_carry[:, c] = dvv[BLOCK:]

        @pl.when(n == nb)
        def _():
            dk_ref[...] = dk_carry[...]
            dv_ref[...] = dv_carry[...]

    last = nb - 1
    cur = pl.BlockSpec((BLOCK, D_GRP), lambda c, n: (jnp.minimum(n, last), c))
    prev = pl.BlockSpec((BLOCK, D_GRP), lambda c, n: (jnp.clip(n - 1, 0, last), c))
    out = jax.ShapeDtypeStruct(qv.shape, F32)
    return _call_with_exchange(
        body, rides,
        lambda: jnp.logical_and(pl.program_id(0) == 0, pl.program_id(1) == 0),
        lambda: jnp.logical_and(pl.program_id(0) == r - 1, pl.program_id(1) == nb),
        name=f"dil_bwd_r{r}", grid=(r, nb + 1),
        in_specs=[cur, prev, cur, prev, cur, cur, cur, cur],
        out_specs=(cur, prev, prev), out_shape=(out, out, out),
        scratch_shapes=[pltpu.VMEM((BLOCK, D_GRP), F32), pltpu.VMEM((BLOCK, D_GRP), F32)],
    )(qv, kv, kv, vv, vv, dov, lsev, deltav, *rides)


def _sb_fwd(qs, ks, vs, tri_suf):
    s_len = qs.shape[0]
    t = SB_TILE
    nq = s_len // t

    npair = SB_PAIRS

    def body(q_ref, k_ref, v_ref, u_ref, o_ref, c_ref, qq, vt, acc, cf, csave):
        row = lax.broadcasted_iota(jnp.int32, (2 * t, t), 0) & (t - 1)
        col = lax.broadcasted_iota(jnp.int32, (2 * t, t), 1)
        diag_mask = col < row
        lane1 = lax.broadcasted_iota(jnp.int32, (t, 128), 1)
        head0 = lane1 < HEAD_DIM
        lane2 = lax.broadcasted_iota(jnp.int32, (2 * t, 128), 1)
        uu = u_ref[...]
        pr = range(npair)
        cols = [slice(128 * pp, 128 * (pp + 1)) for pp in pr]

        i = pl.program_id(1)

        @pl.when(i == 0)
        def _():
            def transpose_v(j, _):
                rows = pl.ds(pl.multiple_of(j * t, t), t)
                for pp in pr:
                    vt[pp, j] = v_ref[rows, cols[pp]].astype(F32).T.astype(BF16)
                return 0

            lax.fori_loop(0, nq, transpose_v, 0)

        for pp in pr:
            q2 = q_ref[:, cols[pp]] * ATT_SCALE
            qq[pp, 0:t, :] = jnp.where(head0, q2, 0)
            qq[pp, t:2 * t, :] = jnp.where(head0, 0, q2)
        acc[...] = jnp.zeros_like(acc)
        cf[...] = jnp.zeros_like(cf)
        csave[...] = jnp.full(csave.shape, 2.0 * SB_DEAD, F32)

        def tile(kb, diag):
            krows = pl.ds(pl.multiple_of(kb * t, t), t)
            zs = [_dot_nt(qq[pp], k_ref[krows, cols[pp]]) for pp in pr]
            lbk = [_log_sigmoid_pair(z) for z in zs]
            lks = [jnp.where(diag_mask, lk, 0.0) if diag else lk for _, lk in lbk]
            sufs = [_cumsum_mm(lk, uu) for lk in lks]
            carries = [cf[pp] for pp in pr]
            avs = []
            for pp in pr:
                a = jnp.exp(lbk[pp][0] + (sufs[pp] + jnp.concatenate([carries[pp]] * (t // 128), axis=1)))
                avs.append((jnp.where(diag_mask, a, 0.0) if diag else a).astype(BF16))
            pvs = [_dot_nt(vt[pp, kb], avs[pp]) for pp in pr]
            for pp in pr:
                acc[pp] += pvs[pp]
                csave[pp] = jnp.where(lane2 == kb, carries[pp], csave[pp])
                cf[pp] = carries[pp] + jnp.broadcast_to(jnp.sum(lks[pp], axis=-1, keepdims=True), (2 * t, 128))

        tile(i, True)

        def alive():
            return jnp.max(cf[...]) > SB_DEAD

        def k_block(state):
            kb, _ = state
            tile(kb, False)
            return kb - 1, alive()

        lax.while_loop(lambda state: jnp.logical_and(state[0] >= 0, state[1]), k_block, (i - 1, alive()))
        for pp in pr:
            o_ref[:, cols[pp]] = jnp.where(head0, acc[pp, :, 0:t].T, acc[pp, :, t:2 * t].T)
            c_ref[2 * pp] = csave[pp, 0:t, :]
            c_ref[2 * pp + 1] = csave[pp, t:2 * t, :]

    width = 128 * npair
    kv = pl.BlockSpec((s_len, width), lambda p, i: (0, p))
    qo = pl.BlockSpec((t, width), lambda p, i: (i, p))
    return pl.pallas_call(
        body, name="sb_fwd", grid=(4 // npair, nq),
        in_specs=[qo, kv, kv, pl.BlockSpec((t, t), lambda p, i: (0, 0))],
        out_specs=(qo, pl.BlockSpec((2 * npair, t, 128), lambda p, i: (p, i, 0))),
        out_shape=(jax.ShapeDtypeStruct((s_len, D_GRP), F32),
                   jax.ShapeDtypeStruct((8, s_len, 128), F32)),
        scratch_shapes=[pltpu.VMEM((npair, 2 * t, 128), BF16), pltpu.VMEM((npair, nq, 128, t), BF16),
                        pltpu.VMEM((npair, 128, 2 * t), F32),
                        pltpu.VMEM((npair, 2 * t, 128), F32), pltpu.VMEM((npair, 2 * t, 128), F32)],
        compiler_params=_params(),
    )(qs, ks, vs, tri_suf)


def _sb_bwd(qs, ks, vs, dos, csaved, tri_suf, tri_pre, rides):
    s_len = qs.shape[0]
    t = SB_TILE
    nq = s_len // t

    npair = SB_BWD_PAIRS

    def body(q_ref, k_ref, v_ref, do_ref, c_ref, u_ref, p_ref, dq_ref, dk_ref, dv_ref,
             qq, dd, qqt, ddt, kt, dq_acc, dkt, dvt, cg):
        row = lax.broadcasted_iota(jnp.int32, (2 * t, t), 0) & (t - 1)
        col = lax.broadcasted_iota(jnp.int32, (2 * t, t), 1)
        diag_mask = col < row
        lane1 = lax.broadcasted_iota(jnp.int32, (t, 128), 1)
        head0 = lane1 < HEAD_DIM
        lane2 = lax.broadcasted_iota(jnp.int32, (2 * t, 128), 1)
        uu, pm = u_ref[...], p_ref[...]
        pr = range(npair)
        cols = [slice(128 * pp, 128 * (pp + 1)) for pp in pr]
        i = pl.program_id(1)

        @pl.when(i == 0)
        def _():
            dkt[...] = jnp.zeros_like(dkt)
            dvt[...] = jnp.zeros_like(dvt)

            def transpose_k(j, _):
                rows = pl.ds(pl.multiple_of(j * t, t), t)
                for pp in pr:
                    kt[pp, j] = k_ref[rows, cols[pp]].astype(F32).T.astype(BF16)
                return 0

            lax.fori_loop(0, nq, transpose_k, 0)

        for pp in pr:
            q2 = q_ref[:, cols[pp]].astype(F32) * ATT_SCALE
            do2 = do_ref[:, cols[pp]].astype(F32)
            for src, nat, tr in ((q2, qq, qqt), (do2, dd, ddt)):
                stacked = jnp.concatenate([jnp.where(head0, src, 0.0), jnp.where(head0, 0.0, src)], axis=0)
                nat[pp] = stacked.astype(BF16)
                tr[pp] = stacked.T.astype(BF16)
        dq_acc[...] = jnp.zeros_like(dq_acc)
        cg[...] = jnp.zeros_like(cg)

        def tile(kb, diag):
            krows = pl.ds(pl.multiple_of(kb * t, t), t)
            zs = [_dot_nt(qq[pp], k_ref[krows, cols[pp]]) for pp in pr]
            das = [_dot_nt(dd[pp], v_ref[krows, cols[pp]]) for pp in pr]
            lbk = [_log_sigmoid_pair(z) for z in zs]
            lks = [jnp.where(diag_mask, lk, 0.0) if diag else lk for _, lk in lbk]
            sufs = [_cumsum_mm(lk, uu) for lk in lks]
            avs, gs = [], []
            for pp in pr:
                cs = jnp.concatenate([c_ref[2 * pp], c_ref[2 * pp + 1]], axis=0)
                cf = jnp.sum(jnp.where(lane2 == kb, cs, 0.0), axis=-1, keepdims=True)
                a = jnp.exp(lbk[pp][0] + (sufs[pp] + cf))
                a = jnp.where(diag_mask, a, 0.0) if diag else a
                avs.append(a.astype(BF16))
                gs.append(a * das[pp])
            gpres = [_cumsum_mm(g, pm) for g in gs]
            dzs = []
            for pp in pr:
                carry = cg[pp]
                beta = jnp.exp(lbk[pp][0])
                dz = gs[pp] - beta * (gs[pp] + (gpres[pp] + jnp.concatenate([carry] * (t // 128), axis=1)))
                dzs.append((jnp.where(diag_mask, dz, 0.0) if diag else dz).astype(BF16))
                cg[pp] = carry + jnp.broadcast_to(jnp.sum(gs[pp], axis=-1, keepdims=True), (2 * t, 128))
            dqs = [_dot_nt(kt[pp, kb], dzs[pp]) for pp in pr]
            dks = [_dot(qqt[pp], dzs[pp]) for pp in pr]
            dvs = [_dot(ddt[pp], avs[pp]) for pp in pr]
            for pp in pr:
                dq_acc[pp] += dqs[pp]
                dkt[pp, kb] += dks[pp]
                dvt[pp, kb] += dvs[pp]

        def k_block(kb, _):
            tile(kb, False)
            return 0

        col_max = jnp.max(jnp.max(c_ref[...], axis=0), axis=0, keepdims=True)
        lane_row = lax.broadcasted_iota(jnp.int32, (1, 128), 1)
        n_live = jnp.sum(jnp.where(jnp.logical_and(col_max > SB_DEAD, lane_row < i), 1, 0))
        lax.fori_loop(i - n_live, i, k_block, 0)
        tile(i, True)
        for pp in pr:
            dq_ref[:, cols[pp]] = jnp.where(head0, dq_acc[pp, :, 0:t].T, dq_acc[pp, :, t:2 * t].T) * ATT_SCALE

        @pl.when(i == nq - 1)
        def _():
            def untranspose(j, _):
                rows = pl.ds(pl.multiple_of(j * t, t), t)
                for pp in pr:
                    dk_ref[rows, cols[pp]] = dkt[pp, j].T
                    dv_ref[rows, cols[pp]] = dvt[pp, j].T
                return 0

            lax.fori_loop(0, nq, untranspose, 0)

    width = 128 * npair
    kv = pl.BlockSpec((s_len, width), lambda p, i: (0, p))
    qo = pl.BlockSpec((t, width), lambda p, i: (i, p))
    tri = pl.BlockSpec((t, t), lambda p, i: (0, 0))
    out = jax.ShapeDtypeStruct((s_len, D_GRP), F32)
    steps = 4 // npair
    return _call_with_exchange(
        body, rides,
        lambda: jnp.logical_and(pl.program_id(0) == 0, pl.program_id(1) == 0),
        lambda: jnp.logical_and(pl.program_id(0) == steps - 1, pl.program_id(1) == nq - 1),
        name="sb_bwd", grid=(steps, nq),
        in_specs=[qo, kv, kv, qo, pl.BlockSpec((2 * npair, t, 128), lambda p, i: (p, i, 0)), tri, tri],
        out_specs=(qo, kv, kv), out_shape=(out, out, out),
        scratch_shapes=[pltpu.VMEM((npair, 2 * t, 128), BF16), pltpu.VMEM((npair, 2 * t, 128), BF16),
                        pltpu.VMEM((npair, 128, 2 * t), BF16), pltpu.VMEM((npair, 128, 2 * t), BF16),
                        pltpu.VMEM((npair, nq, 128, t), BF16),
                        pltpu.VMEM((npair, 128, 2 * t), F32),
                        pltpu.VMEM((npair, nq, 128, t), F32), pltpu.VMEM((npair, nq, 128, t), F32),
                        pltpu.VMEM((npair, 2 * t, 128), F32)],
    )(qs, ks, vs, dos, csaved, tri_suf, tri_pre, *rides)


def _attn_out(o_b, lse_b, o_sb, x2, wdil, wsb, b_g):
    s_len = x2.shape[0]
    tm = ROW_TILE

    def body(o1_ref, l1_ref, o4_ref, l4_ref, o16_ref, l16_ref, osb_ref, x_ref, wdil_ref, wsb_ref, w_ref,
             odil_ref, lse_ref, lse4_ref, lse16_ref, mixed_ref, x1_ref, stage, nat4, nat16):
        _merge_views((o4_ref, l4_ref), (o16_ref, l16_ref), stage, nat4, nat16)
        os_ = (o1_ref[...], _slab_group(nat4, 0), _slab_group(nat16, 0))
        ls = (l1_ref[...], _slab_group(nat4, 1), _slab_group(nat16, 1))
        mx = jnp.maximum(jnp.maximum(ls[0], ls[1]), ls[2])
        es = [jnp.exp(l - mx) for l in ls]
        den = es[0] + es[1] + es[2]
        o_dil = (es[0] * os_[0] + es[1] * os_[1] + es[2] * os_[2]) / den
        odil_ref[...] = o_dil
        lse = mx + jnp.log(den)
        lse_ref[...] = lse
        for p in range(4):
            nat4[p] = lse[:, 128 * p:128 * (p + 1)]
        _split_views(nat4.at[0:4], stage.at[0:4], (lse4_ref,), (lse16_ref,))
        halves = []
        for t, w_r in ((o_dil, wdil_ref), (osb_ref[...], wsb_ref)):
            r = lax.rsqrt(jnp.mean(t * t, axis=-1, keepdims=True) + EPS)
            halves.append(((t * r) * w_r[...]).astype(BF16))
        mixed = jnp.concatenate(halves, axis=1)
        mixed_ref[...] = mixed
        w = w_ref[...].reshape(D_MODEL, D_MODEL)
        x1_ref[...] = x_ref[...] + _dot(mixed, w)

    row = lambda w: pl.BlockSpec((tm, w), lambda i: (i, 0))
    return pl.pallas_call(
        body, name="attn_out", grid=(s_len // tm,),
        in_specs=[row(D_GRP)] * 2 + [_view_spec(tm, 4)] * 2 + [_view_spec(tm, 16)] * 2
        + [row(D_GRP), row(D_MODEL), _full((1, D_GRP)), _full((1, D_GRP)),
           pl.BlockSpec((N_DEV, OUT_SHARD, D_MODEL), lambda i: (0, W_OUT_BLOCK, 0))],
        out_specs=(row(D_GRP), row(D_GRP), _view_spec(tm, 4), _view_spec(tm, 16), row(D_MODEL), row(D_MODEL)),
        out_shape=(jax.ShapeDtypeStruct((s_len, D_GRP), F32), jax.ShapeDtypeStruct((s_len, D_GRP), F32),
                   _view_shape(s_len, 4, F32), _view_shape(s_len, 16, F32),
                   jax.ShapeDtypeStruct((s_len, D_MODEL), BF16), jax.ShapeDtypeStruct((s_len, D_MODEL), F32)),
        scratch_shapes=[pltpu.VMEM((8, tm, 128), F32)] * 3,
        compiler_params=_params(),
    )(o_b[0], lse_b[0], o_b[1], lse_b[1], o_b[2], lse_b[2], o_sb, x2, wdil, wsb, b_g)


def _two_shards(w_ref):
    return w_ref[...].reshape(FF_BLOCK, D_MODEL)


def _ffn_fwd(x1, wn2, tgt, w_g):
    s_len = x1.shape[0]
    tm = ROW_TILE
    ni = s_len // tm

    def body(x_ref, wn_ref, t_ref, wg_ref, wu_ref, wd_ref, g_ref, u_ref, h2_ref, dy_ref, loss_ref, acc):
        j = pl.program_id(1)

        @pl.when(j == 0)
        def _():
            xx = x_ref[...]
            r = lax.rsqrt(jnp.mean(xx * xx, axis=-1, keepdims=True) + EPS)
            h2_ref[...] = ((xx * r) * wn_ref[...]).astype(BF16)
            acc[...] = jnp.zeros_like(acc)

        h = h2_ref[...]
        g = _dot_nt(h, _two_shards(wg_ref))
        u = _dot_nt(h, _two_shards(wu_ref))
        g_ref[...] = g
        u_ref[...] = u
        act = (g * (1.0 / (1.0 + jnp.exp(-g)))) * u
        acc[...] += _dot(act.astype(BF16), _two_shards(wd_ref))

        @pl.when(j == FF_STEPS - 1)
        def _():
            err = (x_ref[...] + acc[...]) - t_ref[...]
            dy_ref[...] = err * (1.0 / D_MODEL)
            part = 0.5 * jnp.sum(jnp.mean(err * err, axis=-1, keepdims=True))
            loss_ref[...] = jnp.full((8, 128), part, F32)

    row = pl.BlockSpec((tm, D_MODEL), lambda i, j: (i, 0))
    hid = pl.BlockSpec((tm, FF_BLOCK), lambda i, j: (i, j))
    return pl.pallas_call(
        body, name="ffn_fwd", grid=(ni, FF_STEPS),
        in_specs=[row, pl.BlockSpec((1, D_MODEL), lambda i, j: (0, 0)), row,
                  pl.BlockSpec((2, FF_PAD, D_MODEL), lambda i, j: (j, 0, 0)),
                  pl.BlockSpec((2, FF_PAD, D_MODEL), lambda i, j: (j, 1, 0)),
                  pl.BlockSpec((2, FF_PAD, D_MODEL), lambda i, j: (j, 2, 0))],
        out_specs=(hid, hid, row, row, pl.BlockSpec((8, 128), lambda i, j: (i, 0))),
        out_shape=(jax.ShapeDtypeStruct((s_len, N_DEV * FF_PAD), F32),
                   jax.ShapeDtypeStruct((s_len, N_DEV * FF_PAD), F32),
                   jax.ShapeDtypeStruct((s_len, D_MODEL), BF16),
                   jax.ShapeDtypeStruct((s_len, D_MODEL), F32),
                   jax.ShapeDtypeStruct((ni * 8, 128), F32)),
        scratch_shapes=[pltpu.VMEM((tm, D_MODEL), F32)],
        compiler_params=_params(),
    )(x1, wn2, tgt, w_g, w_g, w_g)


def _ffn_bwd_dx(dy, g, u, w_g):
    s_len = dy.shape[0]
    tm = ROW_TILE

    def body(dy_ref, g_ref, u_ref, wg_ref, wu_ref, wd_ref, dg_ref, du_ref, act_ref, dh_ref, acc):
        j = pl.program_id(1)

        @pl.when(j == 0)
        def _():
            acc[...] = jnp.zeros_like(acc)

        gg, uu = g_ref[...], u_ref[...]
        da = _dot_nt(dy_ref[...].astype(BF16), _two_shards(wd_ref))
        sig = 1.0 / (1.0 + jnp.exp(-gg))
        silu = gg * sig
        act_ref[...] = (silu * uu).astype(BF16)
        du = (da * silu).astype(BF16)
        dg = (da * uu * (sig * (1.0 + gg * (1.0 - sig)))).astype(BF16)
        du_ref[...] = du
        dg_ref[...] = dg
        acc[...] += _dot(dg, _two_shards(wg_ref)) + _dot(du, _two_shards(wu_ref))

        @pl.when(j == FF_STEPS - 1)
        def _():
            dh_ref[...] = acc[...]

    row = pl.BlockSpec((tm, D_MODEL), lambda i, j: (i, 0))
    hid = pl.BlockSpec((tm, FF_BLOCK), lambda i, j: (i, j))
    hid_bf = jax.ShapeDtypeStruct((s_len, N_DEV * FF_PAD), BF16)
    return pl.pallas_call(
        body, name="ffn_bwd_dx", grid=(s_len // tm, FF_STEPS),
        in_specs=[row, hid, hid,
                  pl.BlockSpec((2, FF_PAD, D_MODEL), lambda i, j: (j, 0, 0)),
                  pl.BlockSpec((2, FF_PAD, D_MODEL), lambda i, j: (j, 1, 0)),
                  pl.BlockSpec((2, FF_PAD, D_MODEL), lambda i, j: (j, 2, 0))],
        out_specs=(hid, hid, hid, row),
        out_shape=(hid_bf, hid_bf, hid_bf, jax.ShapeDtypeStruct((s_len, D_MODEL), F32)),
        scratch_shapes=[pltpu.VMEM((tm, D_MODEL), F32)],
        compiler_params=_params(),
    )(dy, g, u, w_g, w_g, w_g)


def _ffn_bwd_dw(h2, dy, dg, du, act):
    s_len = h2.shape[0]
    tm = ROW_TILE
    ni = s_len // tm

    def body(h_ref, dy_ref, dg_ref, du_ref, act_ref, dwg_ref, dwu_ref, dwd_ref, ag, au, ad):
        i = pl.program_id(1)

        @pl.when(i == 0)
        def _():
            ag[...] = jnp.zeros_like(ag)
            au[...] = jnp.zeros_like(au)
            ad[...] = jnp.zeros_like(ad)

        h = h_ref[...]
        ag[...] += _dot_tn(h, dg_ref[...])
        au[...] += _dot_tn(h, du_ref[...])
        ad[...] += _dot_tn(act_ref[...], dy_ref[...].astype(BF16))

        @pl.when(i == ni - 1)
        def _():
            for half in range(2):
                cols = slice(FF_PAD * half, FF_PAD * (half + 1))
                dwg_ref[half] = ag[:, cols].astype(BF16)
                dwu_ref[half] = au[:, cols].astype(BF16)
            dwd_ref[...] = ad[...].astype(BF16).reshape(2, FF_PAD, D_MODEL)

    row = pl.BlockSpec((tm, D_MODEL), lambda j, i: (i, 0))
    hid = pl.BlockSpec((tm, FF_BLOCK), lambda j, i: (i, j))
    col_w = pl.BlockSpec((2, D_MODEL, FF_PAD), lambda j, i: (j, 0, 0))
    row_w = pl.BlockSpec((2, FF_PAD, D_MODEL), lambda j, i: (j, 0, 0))
    return pl.pallas_call(
        body, name="ffn_bwd_dw", grid=(FF_STEPS, ni),
        in_specs=[row, row, hid, hid, hid], out_specs=(col_w, col_w, row_w),
        out_shape=(jax.ShapeDtypeStruct((N_DEV, D_MODEL, FF_PAD), BF16),
                   jax.ShapeDtypeStruct((N_DEV, D_MODEL, FF_PAD), BF16),
                   jax.ShapeDtypeStruct((N_DEV, FF_PAD, D_MODEL), BF16)),
        scratch_shapes=[pltpu.VMEM((D_MODEL, FF_BLOCK), F32), pltpu.VMEM((D_MODEL, FF_BLOCK), F32),
                        pltpu.VMEM((FF_BLOCK, D_MODEL), F32)],
        compiler_params=_params(),
    )(h2, dy, dg, du, act)


def _rms_bwd(dy, t, w):
    r = lax.rsqrt(jnp.mean(t * t, axis=-1, keepdims=True) + EPS)
    gw = dy * w
    dt = r * (gw - t * ((r * r) * jnp.mean(gw * t, axis=-1, keepdims=True)))
    return dt, dy * t * r


def _attn_out_bwd(dy, dh2, x1, wn2, b_g, mixed, o_dil, o_sb, wdil, wsb, bd512):
    s_len = dy.shape[0]
    tm = ROW_TILE
    ni = s_len // tm

    def body(dy_ref, dh_ref, x1_ref, wn_ref, w_ref, mixed_ref, odil_ref, osb_ref, wdil_ref, wsb_ref, bd_ref,
             dx1_ref, dodil_ref, delta_ref, dosb_ref, dwout_ref, dwn_ref, dwdil_ref, dwsb_ref,
             do4_ref, dl4_ref, do16_ref, dl16_ref, wacc, both, stage):
        i = pl.program_id(0)

        @pl.when(i == 0)
        def _():
            wacc[...] = jnp.zeros_like(wacc)
            dwn_ref[...] = jnp.zeros_like(dwn_ref)
            dwdil_ref[...] = jnp.zeros_like(dwdil_ref)
            dwsb_ref[...] = jnp.zeros_like(dwsb_ref)

        dnorm, dw_rows = _rms_bwd(dh_ref[...], x1_ref[...], wn_ref[...])
        dx1 = dy_ref[...] + dnorm
        dx1_ref[...] = dx1
        dwn_ref[...] += jnp.sum(dw_rows, axis=0, keepdims=True)
        dx1b = dx1.astype(BF16)
        w = w_ref[...].reshape(D_MODEL, D_MODEL)
        dmixed = _dot_nt(dx1b, w)
        wacc[...] += _dot_tn(mixed_ref[...], dx1b)
        o_dil = odil_ref[...]
        d_odil, dw_rows = _rms_bwd(dmixed[:, :D_GRP], o_dil, wdil_ref[...])
        dwdil_ref[...] += jnp.sum(dw_rows, axis=0, keepdims=True)
        dodil_ref[...] = d_odil.astype(BF16)
        delta = _mm_split(d_odil * o_dil, bd_ref[...])
        delta_ref[...] = delta
        for p in range(4):
            both[p] = d_odil[:, 128 * p:128 * (p + 1)]
            both[4 + p] = delta[:, 128 * p:128 * (p + 1)]
        _split_views(both, stage, (do4_ref, dl4_ref), (do16_ref, dl16_ref))
        d_osb, dw_rows = _rms_bwd(dmixed[:, D_GRP:], osb_ref[...], wsb_ref[...])
        dwsb_ref[...] += jnp.sum(dw_rows, axis=0, keepdims=True)
        dosb_ref[...] = d_osb.astype(BF16)

        @pl.when(i == ni - 1)
        def _():
            dwout_ref[...] = wacc[...].astype(BF16).reshape(N_DEV, OUT_SHARD, D_MODEL)

    row = lambda w: pl.BlockSpec((tm, w), lambda i: (i, 0))
    return pl.pallas_call(
        body, name="attn_out_bwd", grid=(ni,),
        in_specs=[row(D_MODEL), row(D_MODEL), row(D_MODEL), _full((1, D_MODEL)),
                  pl.BlockSpec((N_DEV, OUT_SHARD, D_MODEL), lambda i: (0, W_OUT_BLOCK, 0)),
                  row(D_MODEL), row(D_GRP), row(D_GRP), _full((1, D_GRP)), _full((1, D_GRP)),
                  _full((D_GRP, D_GRP))],
        out_specs=(row(D_MODEL), row(D_GRP), row(D_GRP), row(D_GRP),
                   _full((N_DEV, OUT_SHARD, D_MODEL)), _full((1, D_MODEL)), _full((1, D_GRP)), _full((1, D_GRP)),
                   _view_spec(tm, 4), _view_spec(tm, 4), _view_spec(tm, 16), _view_spec(tm, 16)),
        out_shape=(jax.ShapeDtypeStruct((s_len, D_MODEL), F32), jax.ShapeDtypeStruct((s_len, D_GRP), BF16),
                   jax.ShapeDtypeStruct((s_len, D_GRP), F32), jax.ShapeDtypeStruct((s_len, D_GRP), BF16),
                   jax.ShapeDtypeStruct((N_DEV, OUT_SHARD, D_MODEL), BF16),
                   jax.ShapeDtypeStruct((1, D_MODEL), F32), jax.ShapeDtypeStruct((1, D_GRP), F32),
                   jax.ShapeDtypeStruct((1, D_GRP), F32),
                   _view_shape(s_len, 4, BF16), _view_shape(s_len, 4, F32),
                   _view_shape(s_len, 16, BF16), _view_shape(s_len, 16, F32)),
        scratch_shapes=[pltpu.VMEM((D_MODEL, D_MODEL), F32), pltpu.VMEM((8, tm, 128), F32),
                        pltpu.VMEM((8, tm, 128), F32)],
        compiler_params=_params(),
    )(dy, dh2, x1, wn2, b_g, mixed, o_dil, o_sb, wdil, wsb, bd512)


def _qkv_bwd(dq_b, dk_b, dv_b, dqs, dks, dvs, qraw, kraw, cos2, sin2, qnw, knw, bd):
    s_len = qraw.shape[0]
    tm = ROW_TILE
    ni = s_len // tm

    def body(dq1, dk1, dv1, dq4, dk4, dv4, dq16, dk16, dv16, dqs_ref, dks_ref, dvs_ref,
             qraw_ref, kraw_ref, cos_ref, sin_ref, qnw_ref, knw_ref, bd_ref,
             dproj_ref, dqn_ref, dkn_ref, stage, nat4, nat16):
        i = pl.program_id(0)

        @pl.when(i == 0)
        def _():
            dqn_ref[...] = jnp.zeros_like(dqn_ref)
            dkn_ref[...] = jnp.zeros_like(dkn_ref)

        _merge_views((dq4, dk4, dv4), (dq16, dk16, dv16), stage, nat4, nat16)
        cos_t, sin_t, bdm = cos_ref[...], sin_ref[...], bd_ref[...]
        for grp, (part1, raw_ref, nw_ref, dn_ref) in enumerate(((dq1, qraw_ref, qnw_ref, dqn_ref),
                                                                (dk1, kraw_ref, knw_ref, dkn_ref))):
            dn_acc = 0.0
            for p in range(4):
                cols = slice(128 * p, 128 * (p + 1))
                d_rope = part1[:, cols] + nat4[4 * grp + p] + nat16[4 * grp + p]
                d_norm = d_rope * cos_t + _swap_halves(d_rope * sin_t)
                t = raw_ref[:, cols]
                w = nw_ref[...]
                r = lax.rsqrt(_mm_split(t * t, bdm) * (1.0 / HEAD_DIM) + EPS)
                gw = d_norm * w
                corr = _mm_split(gw * t, bdm) * (1.0 / HEAD_DIM)
                dt = r * (gw - t * ((r * r) * corr))
                dn_acc = dn_acc + jnp.sum(d_norm * t * r, axis=0, keepdims=True)
                dproj_ref[:, D_GRP * grp + 128 * p:D_GRP * grp + 128 * (p + 1)] = dt.astype(BF16)
            dn_ref[...] += dn_acc
        dproj_ref[:, 2 * D_GRP:3 * D_GRP] = (dv1[...] + _slab_group(nat4, 2) + _slab_group(nat16, 2)).astype(BF16)
        dproj_ref[:, 3 * D_GRP:4 * D_GRP] = dqs_ref[...].astype(BF16)
        dproj_ref[:, 4 * D_GRP:5 * D_GRP] = dks_ref[...].astype(BF16)
        dproj_ref[:, 5 * D_GRP:6 * D_GRP] = dvs_ref[...].astype(BF16)

    row = lambda w: pl.BlockSpec((tm, w), lambda i: (i, 0))
    return pl.pallas_call(
        body, name="qkv_bwd", grid=(ni,),
        in_specs=[row(D_GRP)] * 3 + [_view_spec(tm, 4)] * 3 + [_view_spec(tm, 16)] * 3 + [row(D_GRP)] * 5
        + [row(128), row(128), _full((1, 128)), _full((1, 128)), _full((128, 128))],
        out_specs=(row(D_IN), _full((1, 128)), _full((1, 128))),
        out_shape=(jax.ShapeDtypeStruct((s_len, D_IN), BF16), jax.ShapeDtypeStruct((1, 128), F32),
                   jax.ShapeDtypeStruct((1, 128), F32)),
        scratch_shapes=[pltpu.VMEM((12, tm, 128), F32)] * 3,
        compiler_params=_params(),
    )(dq_b[0], dk_b[0], dv_b[0], dq_b[1], dk_b[1], dv_b[1], dq_b[2], dk_b[2], dv_b[2],
      dqs, dks, dvs, qraw, kraw, cos2, sin2, qnw, knw, bd)


def _in_bwd_dx(dproj, a_g, x2, dx1, wn1, rides):
    s_len = x2.shape[0]
    tm = ROW_TILE
    ni = s_len // tm

    def body(dp_ref, w_ref, x_ref, dx1_ref, wn_ref, gx_ref, dwn_ref, w_full):
        i = pl.program_id(0)

        @pl.when(i == 0)
        def _():
            dwn_ref[...] = jnp.zeros_like(dwn_ref)
            for d in range(N_DEV):
                w_full[:, IN_SHARD * d:IN_SHARD * (d + 1)] = w_ref[d]

        dh = _dot_nt(dp_ref[...], w_full[...])
        dnorm, dw_rows = _rms_bwd(dh, x_ref[...], wn_ref[...])
        gx_ref[...] = dx1_ref[...] + dnorm
        dwn_ref[...] += jnp.sum(dw_rows, axis=0, keepdims=True)

    row = lambda w: pl.BlockSpec((tm, w), lambda i: (i, 0))
    return _call_with_exchange(
        body, rides, lambda: pl.program_id(0) == 0, lambda: pl.program_id(0) == ni - 1,
        name="in_bwd_dx", grid=(ni,),
        in_specs=[row(D_IN), pl.BlockSpec((N_DEV, D_MODEL, IN_SHARD), lambda i: (0, 0, 0)),
                  row(D_MODEL), row(D_MODEL), _full((1, D_MODEL))],
        out_specs=(row(D_MODEL), _full((1, D_MODEL))),
        out_shape=(jax.ShapeDtypeStruct((s_len, D_MODEL), F32), jax.ShapeDtypeStruct((1, D_MODEL), F32)),
        scratch_shapes=[pltpu.VMEM((D_MODEL, D_IN), BF16)],
    )(dproj, a_g, x2, dx1, wn1, *rides)


def _in_bwd_dw(h1, dproj):
    s_len = h1.shape[0]
    tm = ROW_TILE
    ni = s_len // tm

    def body(h_ref, dp_ref, dw_ref, acc):
        i = pl.program_id(1)

        @pl.when(i == 0)
        def _():
            acc[...] = jnp.zeros_like(acc)

        acc[...] += _dot_tn(h_ref[...], dp_ref[...])

        @pl.when(i == ni - 1)
        def _():
            for half in range(2):
                dw_ref[half] = acc[:, IN_SHARD * half:IN_SHARD * (half + 1)].astype(BF16)

    return pl.pallas_call(
        body, name="in_bwd_dw", grid=(N_DEV // 2, ni),
        in_specs=[pl.BlockSpec((tm, D_MODEL), lambda d, i: (i, 0)),
                  pl.BlockSpec((tm, 2 * IN_SHARD), lambda d, i: (i, d))],
        out_specs=pl.BlockSpec((2, D_MODEL, IN_SHARD), lambda d, i: (d, 0, 0)),
        out_shape=jax.ShapeDtypeStruct((N_DEV, D_MODEL, IN_SHARD), BF16),
        scratch_shapes=[pltpu.VMEM((D_MODEL, 2 * IN_SHARD), F32)],
        compiler_params=_params(),
    )(h1, dproj)


def _adamw(recv, w, m, v):
    rows, cols = w.shape
    tr = next((t for t in (128, 32) if rows % t == 0), rows)

    def body(p_ref, w_ref, m_ref, v_ref, g_ref, d_ref, nm_ref, nv_ref):
        g = p_ref[0].astype(F32)
        for s in range(1, N_DEV):
            g = g + p_ref[s].astype(F32)
        m_new = ADAM_B1 * m_ref[...] + (1.0 - ADAM_B1) * g
        v_new = ADAM_B2 * v_ref[...] + (1.0 - ADAM_B2) * (g * g)
        m_hat = m_new / (1.0 - ADAM_B1 ** ADAM_STEP)
        v_hat = v_new / (1.0 - ADAM_B2 ** ADAM_STEP)
        g_ref[...] = g
        d_ref[...] = -ADAM_LR * (m_hat / (jnp.sqrt(v_hat) + ADAM_EPS) + ADAM_WD * w_ref[...])
        nm_ref[...] = m_new
        nv_ref[...] = v_new

    blk = pl.BlockSpec((tr, cols), lambda i: (i, 0))
    out = jax.ShapeDtypeStruct((rows, cols), F32)
    return pl.pallas_call(
        body, name=f"adamw_{rows}x{cols}", grid=(rows // tr,),
        in_specs=[pl.BlockSpec((N_DEV, tr, cols), lambda i: (0, i, 0)), blk, blk, blk],
        out_specs=(blk,) * 4, out_shape=(out,) * 4,
        compiler_params=_params(),
    )(recv, w, m, v)


def _rope_tables(s_len):
    pos = jnp.arange(s_len, dtype=F32)
    inv_freq = ROPE_THETA ** (-jnp.arange(0, HEAD_DIM, 2, dtype=F32) / HEAD_DIM)
    ang = pos[:, None] * inv_freq[None, :]
    cos, sin = jnp.cos(ang), jnp.sin(ang)
    cos2 = jnp.concatenate([cos, cos, cos, cos], axis=1)
    sin2 = jnp.concatenate([-sin, sin, -sin, sin], axis=1)
    return cos2, sin2


def _block_diag_ones(n):
    i = jnp.arange(n)
    return (i[:, None] // HEAD_DIM == i[None, :] // HEAD_DIM).astype(BF16)


def _pad_cols(t):
    return jnp.pad(t, ((0, 0), (0, FF_PAD - FF_SHARD)))


def _pad_rows(t):
    return jnp.pad(t, ((0, FF_PAD - FF_SHARD), (0, 0)))


LOSS_ROW = 26


def _pack_small(n1, n2, ndil, nsb, nq, nk, scalar=None):
    pad = lambda t: jnp.pad(t.reshape(1, -1), ((0, 0), (0, 128 - t.size)))
    last = jnp.zeros((1, 128), F32) if scalar is None else pad(scalar)
    rows = [n1.reshape(8, 128), n2.reshape(8, 128), ndil.reshape(4, 128), nsb.reshape(4, 128),
            pad(nq), pad(nk), last, jnp.zeros((5, 128), F32)]
    return jnp.concatenate(rows, axis=0)


def _unpack_small(t):
    return (t[0:8].reshape(1, D_MODEL), t[8:16].reshape(1, D_MODEL), t[16:20].reshape(1, D_GRP),
            t[20:24].reshape(1, D_GRP), t[24:25, :HEAD_DIM], t[25:26, :HEAD_DIM])


def kernel(x, attn_norm_w, w_in, q_norm_w, k_norm_w, dil_out_norm_w, sb_out_norm_w, w_out, ffn_norm_w, w_gate, w_up, w_down, loss_target, m_attn_norm_w, m_w_in, m_q_norm_w, m_k_norm_w, m_dil_out_norm_w, m_sb_out_norm_w, m_w_out, m_ffn_norm_w, m_w_gate, m_w_up, m_w_down, v_attn_norm_w, v_w_in, v_q_norm_w, v_k_norm_w, v_dil_out_norm_w, v_sb_out_norm_w, v_w_out, v_ffn_norm_w, v_w_gate, v_w_up, v_w_down):
    s_len = x.shape[1]
    x2, tgt = x[0], loss_target[0]

    w_loc = jnp.concatenate([_pad_cols(w_gate[0]).T, _pad_cols(w_up[0]).T, _pad_rows(w_down[0]), w_out[0]],
                            axis=0).astype(BF16)
    a_g, w_g = _gather_weights([w_in[0].astype(BF16), w_loc])

    cos2, sin2 = _rope_tables(s_len)
    bd128, bd512 = _block_diag_ones(128), _block_diag_ones(D_GRP)
    idx = jnp.arange(SB_TILE)
    tri_suf = (idx[:, None] > idx[None, :]).astype(BF16)
    tri_pre = (idx[:, None] < idx[None, :]).astype(BF16)
    qnw2 = jnp.concatenate([q_norm_w, q_norm_w], axis=1)
    knw2 = jnp.concatenate([k_norm_w, k_norm_w], axis=1)

    (h1, qraw, kraw, q, k, va, qs, ks, vs,
     q4, k4, v4, q16, k16, v16) = _attn_in(x2, attn_norm_w, a_g, cos2, sin2, qnw2, knw2, bd128)
    qkv_views = {1: (q, k, va), 4: (q4, k4, v4), 16: (q16, k16, v16)}
    o_b, lse_b = [], []
    for r in DILATIONS:
        o, lse = _dil_fwd(*qkv_views[r], r)
        o_b.append(o)
        lse_b.append(lse)
    o_sb, c_sb = _sb_fwd(qs, ks, vs, tri_suf)
    o_dil, lse_tot, lse4, lse16, mixed, x1 = _attn_out(o_b, lse_b, o_sb, x2, dil_out_norm_w, sb_out_norm_w, w_g)
    g, u, h2, dy, loss_parts = _ffn_fwd(x1, ffn_norm_w, tgt, w_g)
    loss_local = jnp.sum(loss_parts[::8, 0])

    dg, du, act, dh2 = _ffn_bwd_dx(dy, g, u, w_g)
    (dx1, do_dil, delta, do_sb, dwout, dn2, dndil, dnsb, do4, dl4, do16, dl16) = _attn_out_bwd(
        dy, dh2, x1, ffn_norm_w, w_g, mixed, o_dil, o_sb, dil_out_norm_w, sb_out_norm_w, bd512)
    dwg, dwu, dwd = _ffn_bwd_dw(h2, dy, dg, du, act)
    dqs, dks, dvs, r_gate, r_out = _sb_bwd(qs, ks, vs, do_sb, c_sb, tri_suf, tri_pre, rides=[dwg, dwout])
    cot_views = {1: (do_dil, lse_tot, delta), 4: (do4, lse4, dl4), 16: (do16, lse16, dl16)}
    riders = {1: [], 4: [dwd], 16: [dwu]}
    dq_b, dk_b, dv_b, landed = [], [], [], {}
    for r in DILATIONS:
        dq, dk, dv, *landed[r] = _dil_bwd(*qkv_views[r], *cot_views[r], r, rides=riders[r])
        dq_b.append(dq)
        dk_b.append(dk)
        dv_b.append(dv)
    (r_down,), (r_up,) = landed[4], landed[16]
    dproj, dqn2, dkn2 = _qkv_bwd(dq_b, dk_b, dv_b, dqs, dks, dvs, qraw, kraw, cos2, sin2, qnw2, knw2, bd128)
    dwin = _in_bwd_dw(h1, dproj)
    grad_x, dn1, r_in = _in_bwd_dx(dproj, a_g, x2, dx1, attn_norm_w, rides=[dwin])
    dqn = dqn2[:, :HEAD_DIM] + dqn2[:, HEAD_DIM:]
    dkn = dkn2[:, :HEAD_DIM] + dkn2[:, HEAD_DIM:]

    small = _pack_small(dn1, dn2, dndil, dnsb, dqn, dkn, loss_local)
    (r_small,) = _exchange_grads([], small)
    big = {
        "w_in": _adamw(r_in, w_in[0], m_w_in[0], v_w_in[0]),
        "w_gate": tuple(t[:, :FF_SHARD] for t in _adamw(r_gate, _pad_cols(w_gate[0]), _pad_cols(m_w_gate[0]), _pad_cols(v_w_gate[0]))),
        "w_up": tuple(t[:, :FF_SHARD] for t in _adamw(r_up, _pad_cols(w_up[0]), _pad_cols(m_w_up[0]), _pad_cols(v_w_up[0]))),
        "w_down": _adamw(r_down, w_down[0], m_w_down[0], v_w_down[0]),
        "w_out": _adamw(r_out, w_out[0], m_w_out[0], v_w_out[0]),
    }
    packs = [_pack_small(*ts) for ts in (
        (attn_norm_w, ffn_norm_w, dil_out_norm_w, sb_out_norm_w, q_norm_w, k_norm_w),
        (m_attn_norm_w, m_ffn_norm_w, m_dil_out_norm_w, m_sb_out_norm_w, m_q_norm_w, m_k_norm_w),
        (v_attn_norm_w, v_ffn_norm_w, v_dil_out_norm_w, v_sb_out_norm_w, v_q_norm_w, v_k_norm_w))]
    small_raw = _adamw(r_small, *packs)
    loss = small_raw[0][LOSS_ROW, 0]
    small_out = [_unpack_small(t) for t in small_raw]
    names = ["attn_norm_w", "w_in", "q_norm_w", "k_norm_w", "dil_out_norm_w", "sb_out_norm_w", "w_out",
             "ffn_norm_w", "w_gate", "w_up", "w_down"]
    small_pos = {"attn_norm_w": 0, "ffn_norm_w": 1, "dil_out_norm_w": 2, "sb_out_norm_w": 3,
                 "q_norm_w": 4, "k_norm_w": 5}
    outs = [loss, grad_x[None]]
    for kind in range(4):
        for name in names:
            if name in small_pos:
                outs.append(small_out[kind][small_pos[name]])
            else:
                outs.append(big[name][kind][None])
    return tuple(outs)
```

```python
import functools

import jax
import jax.numpy as jnp
from jax import lax
from jax.experimental import pallas as pl
from jax.experimental.pallas import tpu as pltpu

F32 = jnp.float32
BF16 = jnp.bfloat16

N_DEV = 8
D_MODEL = 1024
HEAD_DIM = 64
D_GRP = 512
D_IN = 6 * D_GRP
IN_SHARD = D_IN // N_DEV
FF_SHARD = 352
FF_PAD = 384
FF_BLOCK = 2 * FF_PAD
FF_STEPS = N_DEV // 2
OUT_SHARD = D_MODEL // N_DEV
BLOCK = 128
DILATIONS = (1, 4, 16)
ROPE_THETA = 10000.0
EPS = 1e-6
ATT_SCALE = HEAD_DIM ** -0.5
NEG = -1e30

ADAM_LR = 0.001
ADAM_B1 = 0.9
ADAM_B2 = 0.999
ADAM_EPS = 1e-08
ADAM_WD = 0.01
ADAM_STEP = 10

SB_TILE = 256
SB_DEAD = -104.0
SB_PAIRS = 4
SB_BWD_PAIRS = 2
ROW_TILE = 512
VMEM_LIMIT = 56 * 1024 * 1024
MESH = pl.DeviceIdType.MESH


def _dot(a, b):
    return jnp.dot(a, b, preferred_element_type=F32)


def _dot_nt(a, b):
    return lax.dot_general(a, b, (((1,), (1,)), ((), ())), preferred_element_type=F32)


def _dot_tn(a, b):
    return lax.dot_general(a, b, (((0,), (0,)), ((), ())), preferred_element_type=F32)


def _mm_split(t, m):
    hi = t.astype(BF16)
    lo = (t - hi.astype(F32)).astype(BF16)
    return _dot(hi, m) + _dot(lo, m)


def _params(**kw):
    return pltpu.CompilerParams(vmem_limit_bytes=VMEM_LIMIT, **kw)


def _full(shape):
    nd = len(shape)
    return pl.BlockSpec(shape, lambda *_: (0,) * nd)


def _view_shape(s_len, r, dtype):
    return jax.ShapeDtypeStruct((s_len // r, r * D_GRP), dtype)


def _view_spec(tm, r):
    return pl.BlockSpec((tm // r, r * D_GRP), lambda i: (i, 0))


def _swap_halves(t):
    lane = lax.broadcasted_iota(jnp.int32, t.shape, 1)
    first = (lane & 32) == 0
    return jnp.where(first, pltpu.roll(t, 96, 1), pltpu.roll(t, 32, 1))


def _log_sigmoid(z):
    return jnp.minimum(z, 0.0) - jnp.log(1.0 + jnp.exp(-jnp.abs(z)))


def _log_sigmoid_pair(z):
    neg_abs = lax.bitcast_convert_type(lax.bitcast_convert_type(z, jnp.uint32) | jnp.uint32(0x80000000), F32)
    lb = jnp.minimum(z, 0.0) - jnp.log(1.0 + jnp.exp(neg_abs))
    return lb, lb - z


def _cumsum_mm(t, tri):
    return _dot(t.astype(BF16), tri)


def _split_views(src_ref, stage_ref, views4, views16):
    slabs, n, _ = src_ref.shape
    n4, n16 = n // 4, n // 16
    for j in range(slabs):
        g, lanes = j // 4, 128 * (j % 4)
        src, stage = src_ref.at[j], stage_ref.at[j]
        for c4 in range(4):
            blk = src[pl.ds(c4, n4, stride=4), :]
            stage[n4 * c4:n4 * (c4 + 1), :] = blk
            col = D_GRP * c4 + lanes
            views4[g][:, col:col + 128] = blk.astype(views4[g].dtype)
        for c4 in range(4):
            for c1 in range(4):
                blk = stage[pl.ds(n4 * c4 + c1, n16, stride=4), :]
                col = D_GRP * (4 * c1 + c4) + lanes
                views16[g][:, col:col + 128] = blk.astype(views16[g].dtype)


def _merge_views(views4, views16, stage_ref, dst4_ref, dst16_ref):
    slabs, n, _ = dst4_ref.shape
    n4, n16 = n // 4, n // 16
    for j in range(slabs):
        g, lanes = j // 4, 128 * (j % 4)
        dst4, dst16, stage = dst4_ref.at[j], dst16_ref.at[j], stage_ref.at[j]
        for c4 in range(4):
            col = D_GRP * c4 + lanes
            dst4[pl.ds(c4, n4, stride=4), :] = views4[g][:, col:col + 128].astype(F32)
            for c1 in range(4):
                col = D_GRP * (4 * c1 + c4) + lanes
                stage[pl.ds(n4 * c4 + c1, n16, stride=4), :] = views16[g][:, col:col + 128].astype(F32)
        for c4 in range(4):
            dst16[pl.ds(c4, n4, stride=4), :] = stage[n4 * c4:n4 * (c4 + 1), :]


def _slab_group(ref, g):
    return jnp.concatenate([ref[4 * g + p] for p in range(4)], axis=1)


def _mesh_pos():
    return lax.axis_index("x"), lax.axis_index("y"), lax.axis_index("c")


def _flat_index(p):
    return 4 * p[0] + 2 * p[1] + p[2]


def _gather_weights(shards):
    n_arr = len(shards)

    def body(*refs):
        srcs, outs = refs[:n_arr], refs[n_arr:2 * n_arr]
        send_sems, recv_sems, local_sems = refs[2 * n_arr:]
        x, y, c = _mesh_pos()
        me, sibling = (x, y, c), (x, y, 1 - c)
        chips = [(1 - x, y), (x, 1 - y), (1 - x, 1 - y)]

        def copy(arr, k, block, to, own=False):
            dst = outs[arr].at[_flat_index(block)]
            return pltpu.make_async_remote_copy(
                src_ref=srcs[arr] if own else dst, dst_ref=dst,
                send_sem=send_sems.at[arr, k], recv_sem=recv_sems.at[arr, k],
                device_id=to, device_id_type=MESH)

        for arr in range(n_arr):
            mine = pltpu.make_async_copy(srcs[arr], outs[arr].at[_flat_index(me)], local_sems.at[arr])
            mine.start()
            first = [copy(arr, 0, me, sibling, own=True)]
            first += [copy(arr, 1 + j, me, (*chip, c), own=True) for j, chip in enumerate(chips)]
            for cp in first:
                cp.start()
        for arr in range(n_arr):
            passed = [copy(arr, 4 + j, (*chip, c), sibling) for j, chip in enumerate(chips)]
            for j, chip in enumerate(chips):
                copy(arr, 1 + j, (*chip, c), me).wait_recv()
                passed[j].start()
        for arr in range(n_arr):
            copy(arr, 0, sibling, me).wait_recv()
            for j, chip in enumerate(chips):
                copy(arr, 4 + j, (*chip, 1 - c), me).wait_recv()
            for k in range(7):
                copy(arr, k, me, me).wait_send()
            pltpu.make_async_copy(srcs[arr], outs[arr].at[_flat_index(me)], local_sems.at[arr]).wait()

    any_spec = pl.BlockSpec(memory_space=pl.ANY)
    return pl.pallas_call(
        body, name="gather_weights",
        out_shape=tuple(jax.ShapeDtypeStruct((N_DEV,) + s.shape, s.dtype) for s in shards),
        in_specs=[any_spec] * n_arr, out_specs=(any_spec,) * n_arr,
        scratch_shapes=[pltpu.SemaphoreType.DMA((n_arr, 7)), pltpu.SemaphoreType.DMA((n_arr, 7)),
                        pltpu.SemaphoreType.DMA((n_arr,))],
        compiler_params=pltpu.CompilerParams(has_side_effects=True),
    )(*shards)


_HBM_SPEC = pl.BlockSpec(memory_space=pltpu.HBM)
_SEM_SPEC = pl.BlockSpec(memory_space=pltpu.SEMAPHORE)
_DATAFLOW = pltpu.SideEffectType.DATAFLOW_SIDE_EFFECTING


def _peer_list(x, y, c):
    return [(1 - x if m & 4 else x, 1 - y if m & 2 else y, 1 - c if m & 1 else c) for m in range(1, N_DEV)]


def _spread_copies(src_refs, land_refs, send_sems, recv_sems, blockwise):
    x, y, c = _mesh_pos()
    my_idx = _flat_index((x, y, c))
    copies = []
    for a, (src, land) in enumerate(zip(src_refs, land_refs)):
        for k, peer in enumerate(_peer_list(x, y, c)):
            copies.append(pltpu.make_async_remote_copy(
                src_ref=src.at[_flat_index(peer)] if blockwise else src, dst_ref=land.at[my_idx],
                send_sem=send_sems.at[(N_DEV - 1) * a + k], recv_sem=recv_sems.at[(N_DEV - 1) * a + k],
                device_id=peer, device_id_type=MESH))
    return copies


def _spread_start(srcs, lands, blockwise, name, after=None):
    n = len(srcs)
    extra = [] if after is None else [after]

    def body(*refs):
        ins, outs = refs[:2 * n], refs[2 * n + len(extra):]
        for cp in _spread_copies(ins[:n], ins[n:], outs[0], outs[1], blockwise):
            cp.start()
        token = refs[-1]
        token[...] = jnp.zeros_like(token)

    hbm = lambda t: pltpu.HBM(t.shape, t.dtype)
    sems = pltpu.SemaphoreType.DMA((n * (N_DEV - 1),))
    outs = pl.pallas_call(
        body, name=name,
        out_shape=(sems, sems) + tuple(hbm(t) for t in srcs) + tuple(hbm(t) for t in lands)
        + (jax.ShapeDtypeStruct((8, 128), F32),),
        in_specs=[_HBM_SPEC] * (2 * n) + [pl.BlockSpec(memory_space=pl.ANY)] * len(extra),
        out_specs=(_SEM_SPEC, _SEM_SPEC) + (_HBM_SPEC,) * (2 * n) + (pl.BlockSpec(memory_space=pltpu.VMEM),),
        input_output_aliases={i: 2 + i for i in range(2 * n)},
        compiler_params=pltpu.CompilerParams(has_side_effects=_DATAFLOW),
    )(*[pltpu.with_memory_space_constraint(t, pltpu.HBM) for t in list(srcs) + list(lands)], *extra)
    return outs[0], outs[1], outs[2:2 + n], outs[2 + n:2 + 2 * n], outs[-1]


def _spread_wait(send_sems, recv_sems, srcs, lands, after, blockwise, name):
    n = len(srcs)

    def body(*refs):
        for cp in _spread_copies(refs[:n], refs[n:2 * n], refs[2 * n], refs[2 * n + 1], blockwise):
            cp.wait_send()
            cp.wait_recv()

    hbm = lambda t: pltpu.HBM(t.shape, t.dtype)
    outs = pl.pallas_call(
        body, name=name,
        out_shape=tuple(hbm(t) for t in srcs) + tuple(hbm(t) for t in lands),
        in_specs=[_HBM_SPEC] * (2 * n) + [_SEM_SPEC, _SEM_SPEC, pl.BlockSpec(memory_space=pl.ANY)],
        out_specs=(_HBM_SPEC,) * (2 * n),
        input_output_aliases={i: i for i in range(2 * n)},
        compiler_params=pltpu.CompilerParams(has_side_effects=_DATAFLOW),
    )(*srcs, *lands, send_sems, recv_sems, after)
    return outs[n:]


def _exchange_grads(parts, small):
    n_arr = len(parts)

    def body(*refs):
        ins, outs = refs[:n_arr + 1], refs[n_arr + 1:2 * (n_arr + 1)]
        send_sems, recv_sems, local_sems = refs[2 * (n_arr + 1):]
        x, y, c = _mesh_pos()
        me = (x, y, c)
        my_idx = _flat_index(me)
        peers = []
        for m in range(1, N_DEV):
            peers.append((1 - x if m & 4 else x, 1 - y if m & 2 else y, 1 - c if m & 1 else c))

        def src_block(arr, dev):
            return ins[arr] if arr == n_arr else ins[arr].at[_flat_index(dev)]

        def copy(arr, k):
            return pltpu.make_async_remote_copy(
                src_ref=src_block(arr, peers[k]), dst_ref=outs[arr].at[my_idx],
                send_sem=send_sems.at[arr, k], recv_sem=recv_sems.at[arr, k],
                device_id=peers[k], device_id_type=MESH)

        def local(arr):
            return pltpu.make_async_copy(src_block(arr, me), outs[arr].at[my_idx], local_sems.at[arr])

        for arr in range(n_arr + 1):
            local(arr).start()
            for k in range(N_DEV - 1):
                copy(arr, k).start()
        for arr in range(n_arr + 1):
            for k in range(N_DEV - 1):
                cp = copy(arr, k)
                cp.wait_send()
                cp.wait_recv()
            local(arr).wait()

    any_spec = pl.BlockSpec(memory_space=pl.ANY)
    out_shape = tuple(jax.ShapeDtypeStruct(p.shape, p.dtype) for p in parts)
    out_shape += (jax.ShapeDtypeStruct((N_DEV,) + small.shape, small.dtype),)
    return pl.pallas_call(
        body, name="exchange_grads",
        out_shape=out_shape,
        in_specs=[any_spec] * (n_arr + 1), out_specs=(any_spec,) * (n_arr + 1),
        scratch_shapes=[pltpu.SemaphoreType.DMA((n_arr + 1, N_DEV - 1)),
                        pltpu.SemaphoreType.DMA((n_arr + 1, N_DEV - 1)),
                        pltpu.SemaphoreType.DMA((n_arr + 1,))],
        compiler_params=pltpu.CompilerParams(has_side_effects=True),
    )(*parts, small)


def _call_with_gather(body, shards, first_step, mid_step, last_step, *, name, grid, in_specs, out_specs,
                      out_shape, scratch_shapes=()):
    out_specs = tuple(out_specs) if isinstance(out_specs, (tuple, list)) else (out_specs,)
    out_shape = tuple(out_shape) if isinstance(out_shape, (tuple, list)) else (out_shape,)
    n_in, n_out, n_scr, n = len(in_specs), len(out_specs), len(scratch_shapes), len(shards)

    def full_body(*refs):
        ins, srcs = refs[:n_in], refs[n_in:n_in + n]
        outs, lands = refs[n_in + n:n_in + n + n_out], refs[n_in + n + n_out:n_in + 2 * n + n_out]
        scratch = refs[n_in + 2 * n + n_out:n_in + 2 * n + n_out + n_scr]
        send_sems, recv_sems, local_sems = refs[-3:]
        x, y, c = _mesh_pos()
        me, sibling = (x, y, c), (x, y, 1 - c)
        chips = [(1 - x, y), (x, 1 - y), (1 - x, 1 - y)]

        def copy(a, k, block, to, own=False):
            dst = lands[a].at[_flat_index(block)]
            return pltpu.make_async_remote_copy(
                src_ref=srcs[a] if own else dst, dst_ref=dst,
                send_sem=send_sems.at[a, k], recv_sem=recv_sems.at[a, k],
                device_id=to, device_id_type=MESH)

        def local(a):
            return pltpu.make_async_copy(srcs[a], lands[a].at[_flat_index(me)], local_sems.at[a])

        @pl.when(first_step())
        def _():
            for a in range(n):
                local(a).start()
                copy(a, 0, me, sibling, own=True).start()
                for j, chip in enumerate(chips):
                    copy(a, 1 + j, me, (*chip, c), own=True).start()

        @pl.when(mid_step())
        def _():
            for a in range(n):
                for j, chip in enumerate(chips):
                    copy(a, 1 + j, (*chip, c), me).wait_recv()
                    copy(a, 4 + j, (*chip, c), sibling).start()

        body(*ins, *outs, *scratch)

        @pl.when(last_step())
        def _():
            for a in range(n):
                copy(a, 0, sibling, me).wait_recv()
                for j, chip in enumerate(chips):
                    copy(a, 4 + j, (*chip, 1 - c), me).wait_recv()
                for k in range(N_DEV - 1):
                    copy(a, k, me, me).wait_send()
                local(a).wait()

    any_spec = pl.BlockSpec(memory_space=pl.ANY)
    return pl.pallas_call(
        full_body, name=name, grid=grid,
        in_specs=list(in_specs) + [any_spec] * n,
        out_specs=out_specs + (any_spec,) * n,
        out_shape=out_shape + tuple(jax.ShapeDtypeStruct((N_DEV,) + t.shape, t.dtype) for t in shards),
        scratch_shapes=list(scratch_shapes) + [pltpu.SemaphoreType.DMA((n, N_DEV - 1)),
                                               pltpu.SemaphoreType.DMA((n, N_DEV - 1)),
                                               pltpu.SemaphoreType.DMA((n,))],
        compiler_params=_params(has_side_effects=True),
    )


def _call_with_exchange(body, rides, first_step, last_step, *, name, grid, in_specs, out_specs, out_shape,
                        scratch_shapes=()):
    out_specs = tuple(out_specs) if isinstance(out_specs, (tuple, list)) else (out_specs,)
    out_shape = tuple(out_shape) if isinstance(out_shape, (tuple, list)) else (out_shape,)
    n_in, n_out, n_scr, n = len(in_specs), len(out_specs), len(scratch_shapes), len(rides)
    if n == 0:
        return pl.pallas_call(body, name=name, grid=grid, in_specs=list(in_specs), out_specs=out_specs,
                              out_shape=out_shape, scratch_shapes=list(scratch_shapes),
                              compiler_params=_params())

    def full_body(*refs):
        ins, srcs = refs[:n_in], refs[n_in:n_in + n]
        outs, lands = refs[n_in + n:n_in + n + n_out], refs[n_in + n + n_out:n_in + 2 * n + n_out]
        scratch = refs[n_in + 2 * n + n_out:n_in + 2 * n + n_out + n_scr]
        send_sems, recv_sems, local_sems = refs[-3:]
        x, y, c = _mesh_pos()
        my_idx = _flat_index((x, y, c))
        peers = _peer_list(x, y, c)

        def remote(a, k):
            return pltpu.make_async_remote_copy(
                src_ref=srcs[a].at[_flat_index(peers[k])], dst_ref=lands[a].at[my_idx],
                send_sem=send_sems.at[a, k], recv_sem=recv_sems.at[a, k],
                device_id=peers[k], device_id_type=MESH)

        def local(a):
            return pltpu.make_async_copy(srcs[a].at[my_idx], lands[a].at[my_idx], local_sems.at[a])

        @pl.when(first_step())
        def _():
            for a in range(n):
                local(a).start()
                for k in range(N_DEV - 1):
                    remote(a, k).start()

        body(*ins, *outs, *scratch)

        @pl.when(last_step())
        def _():
            for a in range(n):
                for k in range(N_DEV - 1):
                    cp = remote(a, k)
                    cp.wait_send()
                    cp.wait_recv()
                local(a).wait()

    any_spec = pl.BlockSpec(memory_space=pl.ANY)
    res = pl.pallas_call(
        full_body, name=name, grid=grid,
        in_specs=list(in_specs) + [any_spec] * n,
        out_specs=out_specs + (any_spec,) * n,
        out_shape=out_shape + tuple(jax.ShapeDtypeStruct(t.shape, t.dtype) for t in rides),
        scratch_shapes=list(scratch_shapes) + [pltpu.SemaphoreType.DMA((n, N_DEV - 1)),
                                               pltpu.SemaphoreType.DMA((n, N_DEV - 1)),
                                               pltpu.SemaphoreType.DMA((n,))],
        compiler_params=_params(has_side_effects=True),
    )
    return res


def _head_norm(t, w128, bd):
    ms = _mm_split(t * t, bd) * (1.0 / HEAD_DIM)
    r = lax.rsqrt(ms + EPS)
    return (t * r) * w128, r


def _attn_in(x2, wn1, a_g, cos2, sin2, qnw, knw, bd, shards):
    s_len = x2.shape[0]
    tm = ROW_TILE

    def body(x_ref, wn_ref, w_ref, cos_ref, sin_ref, qnw_ref, knw_ref, bd_ref,
             h1_ref, qraw_ref, kraw_ref, q_ref, k_ref, va_ref, qs_ref, ks_ref, vs_ref,
             q4_ref, k4_ref, v4_ref, q16_ref, k16_ref, v16_ref, proj, slabs, stage, w_full):
        @pl.when(pl.program_id(0) == 0)
        def _():
            for d in range(N_DEV):
                w_full[:, IN_SHARD * d:IN_SHARD * (d + 1)] = w_ref[d]

        xx = x_ref[...]
        r = lax.rsqrt(jnp.mean(xx * xx, axis=-1, keepdims=True) + EPS)
        h = ((xx * r) * wn_ref[...]).astype(BF16)
        h1_ref[...] = h
        proj[...] = _dot(h, w_full[...])
        cos_t, sin_t, bdm = cos_ref[...], sin_ref[...], bd_ref[...]
        for grp, (raw_ref, rope_ref, nw_ref) in enumerate(((qraw_ref, q_ref, qnw_ref),
                                                           (kraw_ref, k_ref, knw_ref))):
            for p in range(4):
                cols = slice(D_GRP * grp + 128 * p, D_GRP * grp + 128 * (p + 1))
                t = proj[:, cols]
                raw_ref[:, 128 * p:128 * (p + 1)] = t
                yn, _ = _head_norm(t, nw_ref[...], bdm)
                roped = yn * cos_t + _swap_halves(yn) * sin_t
                slabs[4 * grp + p] = roped
                rope_ref[:, 128 * p:128 * (p + 1)] = roped.astype(BF16)
        for p in range(4):
            slabs[8 + p] = proj[:, 2 * D_GRP + 128 * p:2 * D_GRP + 128 * (p + 1)]
        for grp, ref in ((2, va_ref), (3, qs_ref), (4, ks_ref), (5, vs_ref)):
            ref[...] = proj[:, D_GRP * grp:D_GRP * (grp + 1)].astype(BF16)
        _split_views(slabs, stage, (q4_ref, k4_ref, v4_ref), (q16_ref, k16_ref, v16_ref))

    row = lambda w: pl.BlockSpec((tm, w), lambda i: (i, 0))
    grp_bf = jax.ShapeDtypeStruct((s_len, D_GRP), BF16)
    grp_f32 = jax.ShapeDtypeStruct((s_len, D_GRP), F32)
    ni = s_len // tm
    return _call_with_gather(
        body, shards, lambda: pl.program_id(0) == 0, lambda: pl.program_id(0) == ni - 2,
        lambda: pl.program_id(0) == ni - 1,
        name="attn_in", grid=(ni,),
        in_specs=[row(D_MODEL), _full((1, D_MODEL)),
                  pl.BlockSpec((N_DEV, D_MODEL, IN_SHARD), lambda i: (0, 0, 0)),
                  row(128), row(128), _full((1, 128)), _full((1, 128)), _full((128, 128))],
        out_specs=(row(D_MODEL),) + (row(D_GRP),) * 8 + (_view_spec(tm, 4),) * 3 + (_view_spec(tm, 16),) * 3,
        out_shape=(jax.ShapeDtypeStruct((s_len, D_MODEL), BF16), grp_f32, grp_f32) + (grp_bf,) * 6
        + (_view_shape(s_len, 4, BF16),) * 3 + (_view_shape(s_len, 16, BF16),) * 3,
        scratch_shapes=[pltpu.VMEM((tm, D_IN), F32), pltpu.VMEM((12, tm, 128), F32), pltpu.VMEM((12, tm, 128), F32),
                        pltpu.VMEM((D_MODEL, D_IN), BF16)],
    )(x2, wn1, a_g, cos2, sin2, qnw, knw, bd, *shards)


def _band_mask(n):
    i = lax.broadcasted_iota(jnp.int32, (2 * BLOCK, 2 * BLOCK), 0) & (BLOCK - 1)
    j = lax.broadcasted_iota(jnp.int32, (2 * BLOCK, 2 * BLOCK), 1)
    dist = i + BLOCK - j
    return (dist >= 0) & (dist <= BLOCK) & ((n - 1) * BLOCK + j >= 0)


def _stack_heads(t2, head0):
    return jnp.concatenate([jnp.where(head0, t2, 0), jnp.where(head0, 0, t2)], axis=0)


def _unstack_heads(t, head0):
    return jnp.where(head0, t[0:BLOCK], t[BLOCK:2 * BLOCK])


def _dil_fwd(qv, kv, vv, r):
    sub_len = qv.shape[0]
    nb = sub_len // BLOCK

    qb = 2 if nb % 2 == 0 else 1

    def body(q_ref, kp_ref, kc_ref, vp_ref, vc_ref, o_ref, lse_ref):
        n = pl.program_id(1)
        lane = lax.broadcasted_iota(jnp.int32, (BLOCK, 128), 1)
        head0 = lane < HEAD_DIM
        units = [(b, slice(128 * p, 128 * (p + 1))) for b in range(qb) for p in range(4)]
        valid = [_band_mask(qb * n + b) for b in range(qb)]
        rows = [slice(BLOCK * b, BLOCK * (b + 1)) for b in range(qb)]

        def keys(prev_ref, cur_ref, b, c):
            before = prev_ref[:, c] if b == 0 else cur_ref[rows[b - 1], c]
            return jnp.concatenate([before, cur_ref[rows[b], c]], axis=0)

        qqs = [_stack_heads(q_ref[rows[b], c] * ATT_SCALE, head0) for b, c in units]
        kks = [keys(kp_ref, kc_ref, b, c) for b, c in units]
        vvs = [keys(vp_ref, vc_ref, b, c) for b, c in units]
        ss = [_dot_nt(qq, kk) for qq, kk in zip(qqs, kks)]
        prs, dens, lses = [], [], []
        for (b, _), s in zip(units, ss):
            s = jnp.where(valid[b], s, NEG)
            m = jnp.max(s, axis=-1, keepdims=True)
            pr = jnp.exp(s - m)
            den = jnp.sum(pr, axis=-1, keepdims=True)
            prs.append(pr.astype(BF16))
            dens.append(den)
            lses.append(m + jnp.log(den))
        pvs = [_dot(pr, vv2) for pr, vv2 in zip(prs, vvs)]
        for (b, c), pv, den, lse in zip(units, pvs, dens, lses):
            o_ref[rows[b], c] = _unstack_heads(pv / den, head0)
            lse_ref[rows[b], c] = _unstack_heads(jnp.broadcast_to(lse, (2 * BLOCK, 128)), head0)

    cur = pl.BlockSpec((qb * BLOCK, D_GRP), lambda c, n: (n, c))
    prev = pl.BlockSpec((BLOCK, D_GRP), lambda c, n: (jnp.maximum(qb * n - 1, 0), c))
    out = jax.ShapeDtypeStruct(qv.shape, F32)
    return pl.pallas_call(
        body, name=f"dil_fwd_r{r}", grid=(r, nb // qb),
        in_specs=[cur, prev, cur, prev, cur], out_specs=(cur, cur), out_shape=(out, out),
        compiler_params=_params(),
    )(qv, kv, kv, vv, vv)


def _dil_bwd(qv, kv, vv, dov, lsev, deltav, r, rides):
    sub_len = qv.shape[0]
    nb = sub_len // BLOCK

    def body(q_ref, kp_ref, kc_ref, vp_ref, vc_ref, do_ref, lse_ref, dl_ref,
             dq_ref, dk_ref, dv_ref, dk_carry, dv_carry):
        n = pl.program_id(1)

        @pl.when(n == 0)
        def _():
            dk_carry[...] = jnp.zeros_like(dk_carry)
            dv_carry[...] = jnp.zeros_like(dv_carry)

        @pl.when(n < nb)
        def _():
            valid = _band_mask(n)
            lane = lax.broadcasted_iota(jnp.int32, (BLOCK, 128), 1)
            head0 = lane < HEAD_DIM
            pairs = [slice(128 * p, 128 * (p + 1)) for p in range(4)]
            qqs = [_stack_heads(q_ref[:, c] * ATT_SCALE, head0) for c in pairs]
            dos = [_stack_heads(do_ref[:, c], head0) for c in pairs]
            kks = [jnp.concatenate([kp_ref[:, c], kc_ref[:, c]], axis=0) for c in pairs]
            vvs = [jnp.concatenate([vp_ref[:, c], vc_ref[:, c]], axis=0) for c in pairs]
            ss = [_dot_nt(qq, kk) for qq, kk in zip(qqs, kks)]
            dps = [_dot_nt(do, vv2) for do, vv2 in zip(dos, vvs)]
            prs, dss = [], []
            for c, s, dp in zip(pairs, ss, dps):
                stats = []
                for ref in (lse_ref, dl_ref):
                    t2 = ref[:, c]
                    stats.append(jnp.concatenate(
                        [jnp.sum(jnp.where(lane == 0, t2, 0.0), axis=-1, keepdims=True),
                         jnp.sum(jnp.where(lane == HEAD_DIM, t2, 0.0), axis=-1, keepdims=True)], axis=0))
                pr = jnp.where(valid, jnp.exp(jnp.minimum(s - stats[0], 0.0)), 0.0)
                prs.append(pr.astype(BF16))
                dss.append((pr * (dp - stats[1])).astype(BF16))
            dqs = [_dot(ds, kk) for ds, kk in zip(dss, kks)]
            dkks = [_dot_tn(ds, qq) for ds, qq in zip(dss, qqs)]
            dvvs = [_dot_tn(pr, do) for pr, do in zip(prs, dos)]
            for c, dq, dkk, dvv in zip(pairs, dqs, dkks, dvvs):
                dq_ref[:, c] = _unstack_heads(dq, head0) * ATT_SCALE
                dk_ref[:, c] = dk_carry[:, c] + dkk[:BLOCK]
                dv_ref[:, c] = dv_carry[:, c] + dvv[:BLOCK]
                dk_carry[:, c] = dkk[BLOCK:]
                dv_carry[:, c] = dvv[BLOCK:]

        @pl.when(n == nb)
        def _():
            dk_ref[...] = dk_carry[...]
            dv_ref[...] = dv_carry[...]

    last = nb - 1
    cur = pl.BlockSpec((BLOCK, D_GRP), lambda c, n: (jnp.minimum(n, last), c))
    prev = pl.BlockSpec((BLOCK, D_GRP), lambda c, n: (jnp.clip(n - 1, 0, last), c))
    out = jax.ShapeDtypeStruct(qv.shape, F32)
    return _call_with_exchange(
        body, rides,
        lambda: jnp.logical_and(pl.program_id(0) == 0, pl.program_id(1) == 0),
        lambda: jnp.logical_and(pl.program_id(0) == r - 1, pl.program_id(1) == nb),
        name=f"dil_bwd_r{r}", grid=(r, nb + 1),
        in_specs=[cur, prev, cur, prev, cur, cur, cur, cur],
        out_specs=(cur, prev, prev), out_shape=(out, out, out),
        scratch_shapes=[pltpu.VMEM((BLOCK, D_GRP), F32), pltpu.VMEM((BLOCK, D_GRP), F32)],
    )(qv, kv, kv, vv, vv, dov, lsev, deltav, *rides)


def _sb_fwd(qs, ks, vs, tri_suf, shards):
    s_len = qs.shape[0]
    t = SB_TILE
    nq = s_len // t

    npair = SB_PAIRS

    def body(q_ref, k_ref, v_ref, u_ref, o_ref, c_ref, qq, vt, acc, cf, csave):
        row = lax.broadcasted_iota(jnp.int32, (2 * t, t), 0) & (t - 1)
        col = lax.broadcasted_iota(jnp.int32, (2 * t, t), 1)
        diag_mask = col < row
        lane1 = lax.broadcasted_iota(jnp.int32, (t, 128), 1)
        head0 = lane1 < HEAD_DIM
        lane2 = lax.broadcasted_iota(jnp.int32, (2 * t, 128), 1)
        uu = u_ref[...]
        pr = range(npair)
        cols = [slice(128 * pp, 128 * (pp + 1)) for pp in pr]

        i = pl.program_id(1)

        @pl.when(i == 0)
        def _():
            def transpose_v(j, _):
                rows = pl.ds(pl.multiple_of(j * t, t), t)
                for pp in pr:
                    vt[pp, j] = v_ref[rows, cols[pp]].astype(F32).T.astype(BF16)
                return 0

            lax.fori_loop(0, nq, transpose_v, 0)

        for pp in pr:
            q2 = q_ref[:, cols[pp]] * ATT_SCALE
            qq[pp, 0:t, :] = jnp.where(head0, q2, 0)
            qq[pp, t:2 * t, :] = jnp.where(head0, 0, q2)
        acc[...] = jnp.zeros_like(acc)
        cf[...] = jnp.zeros_like(cf)
        csave[...] = jnp.full(csave.shape, 2.0 * SB_DEAD, F32)

        def tile(kb, diag):
            krows = pl.ds(pl.multiple_of(kb * t, t), t)
            zs = [_dot_nt(qq[pp], k_ref[krows, cols[pp]]) for pp in pr]
            lbk = [_log_sigmoid_pair(z) for z in zs]
            lks = [jnp.where(diag_mask, lk, 0.0) if diag else lk for _, lk in lbk]
            sufs = [_cumsum_mm(lk, uu) for lk in lks]
            carries = [cf[pp] for pp in pr]
            avs = []
            for pp in pr:
                a = jnp.exp(lbk[pp][0] + (sufs[pp] + jnp.concatenate([carries[pp]] * (t // 128), axis=1)))
                avs.append((jnp.where(diag_mask, a, 0.0) if diag else a).astype(BF16))
            pvs = [_dot_nt(vt[pp, kb], avs[pp]) for pp in pr]
            for pp in pr:
                acc[pp] += pvs[pp]
                csave[pp] = jnp.where(lane2 == kb, carries[pp], csave[pp])
                cf[pp] = carries[pp] + jnp.broadcast_to(jnp.sum(lks[pp], axis=-1, keepdims=True), (2 * t, 128))

        tile(i, True)

        def alive():
            return jnp.max(cf[...]) > SB_DEAD

        def k_block(state):
            kb, _ = state
            tile(kb, False)
            return kb - 1, alive()

        lax.while_loop(lambda state: jnp.logical_and(state[0] >= 0, state[1]), k_block, (i - 1, alive()))
        for pp in pr:
            o_ref[:, cols[pp]] = jnp.where(head0, acc[pp, :, 0:t].T, acc[pp, :, t:2 * t].T)
            c_ref[2 * pp] = csave[pp, 0:t, :]
            c_ref[2 * pp + 1] = csave[pp, t:2 * t, :]

    width = 128 * npair
    kv = pl.BlockSpec((s_len, width), lambda p, i: (0, p))
    qo = pl.BlockSpec((t, width), lambda p, i: (i, p))
    steps = 4 // npair

    def at(p, i):
        return lambda: jnp.logical_and(pl.program_id(0) == p, pl.program_id(1) == i)

    return _call_with_gather(
        body, shards, at(0, 0), at(steps - 1, (2 * nq) // 3), at(steps - 1, nq - 1),
        name="sb_fwd", grid=(steps, nq),
        in_specs=[qo, kv, kv, pl.BlockSpec((t, t), lambda p, i: (0, 0))],
        out_specs=(qo, pl.BlockSpec((2 * npair, t, 128), lambda p, i: (p, i, 0))),
        out_shape=(jax.ShapeDtypeStruct((s_len, D_GRP), F32),
                   jax.ShapeDtypeStruct((8, s_len, 128), F32)),
        scratch_shapes=[pltpu.VMEM((npair, 2 * t, 128), BF16), pltpu.VMEM((npair, nq, 128, t), BF16),
                        pltpu.VMEM((npair, 128, 2 * t), F32),
                        pltpu.VMEM((npair, 2 * t, 128), F32), pltpu.VMEM((npair, 2 * t, 128), F32)],
    )(qs, ks, vs, tri_suf, *shards)


def _sb_bwd(qs, ks, vs, dos, csaved, tri_suf, tri_pre, rides):
    s_len = qs.shape[0]
    t = SB_TILE
    nq = s_len // t

    npair = SB_BWD_PAIRS

    def body(q_ref, k_ref, v_ref, do_ref, c_ref, u_ref, p_ref, dq_ref, dk_ref, dv_ref,
             qq, dd, qqt, ddt, kt, dq_acc, dkt, dvt, cg):
        row = lax.broadcasted_iota(jnp.int32, (2 * t, t), 0) & (t - 1)
        col = lax.broadcasted_iota(jnp.int32, (2 * t, t), 1)
        diag_mask = col < row
        lane1 = lax.broadcasted_iota(jnp.int32, (t, 128), 1)
        head0 = lane1 < HEAD_DIM
        lane2 = lax.broadcasted_iota(jnp.int32, (2 * t, 128), 1)
        uu, pm = u_ref[...], p_ref[...]
        pr = range(npair)
        cols = [slice(128 * pp, 128 * (pp + 1)) for pp in pr]
        i = pl.program_id(1)

        @pl.when(i == 0)
        def _():
            dkt[...] = jnp.zeros_like(dkt)
            dvt[...] = jnp.zeros_like(dvt)

            def transpose_k(j, _):
                rows = pl.ds(pl.multiple_of(j * t, t), t)
                for pp in pr:
                    kt[pp, j] = k_ref[rows, cols[pp]].astype(F32).T.astype(BF16)
                return 0

            lax.fori_loop(0, nq, transpose_k, 0)

        for pp in pr:
            q2 = q_ref[:, cols[pp]].astype(F32) * ATT_SCALE
            do2 = do_ref[:, cols[pp]].astype(F32)
            for src, nat, tr in ((q2, qq, qqt), (do2, dd, ddt)):
                stacked = jnp.concatenate([jnp.where(head0, src, 0.0), jnp.where(head0, 0.0, src)], axis=0)
                nat[pp] = stacked.astype(BF16)
                tr[pp] = stacked.T.astype(BF16)
        dq_acc[...] = jnp.zeros_like(dq_acc)
        cg[...] = jnp.zeros_like(cg)

        def tile(kb, diag):
            krows = pl.ds(pl.multiple_of(kb * t, t), t)
            zs = [_dot_nt(qq[pp], k_ref[krows, cols[pp]]) for pp in pr]
            das = [_dot_nt(dd[pp], v_ref[krows, cols[pp]]) for pp in pr]
            lbk = [_log_sigmoid_pair(z) for z in zs]
            lks = [jnp.where(diag_mask, lk, 0.0) if diag else lk for _, lk in lbk]
            sufs = [_cumsum_mm(lk, uu) for lk in lks]
            avs, gs = [], []
            for pp in pr:
                cs = jnp.concatenate([c_ref[2 * pp], c_ref[2 * pp + 1]], axis=0)
                cf = jnp.sum(jnp.where(lane2 == kb, cs, 0.0), axis=-1, keepdims=True)
                a = jnp.exp(lbk[pp][0] + (sufs[pp] + cf))
                a = jnp.where(diag_mask, a, 0.0) if diag else a
                avs.append(a.astype(BF16))
                gs.append(a * das[pp])
            gpres = [_cumsum_mm(g, pm) for g in gs]
            dzs = []
            for pp in pr:
                carry = cg[pp]
                beta = jnp.exp(lbk[pp][0])
                dz = gs[pp] - beta * (gs[pp] + (gpres[pp] + jnp.concatenate([carry] * (t // 128), axis=1)))
                dzs.append((jnp.where(diag_mask, dz, 0.0) if diag else dz).astype(BF16))
                cg[pp] = carry + jnp.broadcast_to(jnp.sum(gs[pp], axis=-1, keepdims=True), (2 * t, 128))
            dqs = [_dot_nt(kt[pp, kb], dzs[pp]) for pp in pr]
            dks = [_dot(qqt[pp], dzs[pp]) for pp in pr]
            dvs = [_dot(ddt[pp], avs[pp]) for pp in pr]
            for pp in pr:
                dq_acc[pp] += dqs[pp]
                dkt[pp, kb] += dks[pp]
                dvt[pp, kb] += dvs[pp]

        def k_block(kb, _):
            tile(kb, False)
            return 0

        col_max = jnp.max(jnp.max(c_ref[...], axis=0), axis=0, keepdims=True)
        lane_row = lax.broadcasted_iota(jnp.int32, (1, 128), 1)
        n_live = jnp.sum(jnp.where(jnp.logical_and(col_max > SB_DEAD, lane_row < i), 1, 0))
        lax.fori_loop(i - n_live, i, k_block, 0)
        tile(i, True)
        for pp in pr:
            dq_ref[:, cols[pp]] = jnp.where(head0, dq_acc[pp, :, 0:t].T, dq_acc[pp, :, t:2 * t].T) * ATT_SCALE

        @pl.when(i == nq - 1)
        def _():
            def untranspose(j, _):
                rows = pl.ds(pl.multiple_of(j * t, t), t)
                for pp in pr:
                    dk_ref[rows, cols[pp]] = dkt[pp, j].T
                    dv_ref[rows, cols[pp]] = dvt[pp, j].T
                return 0

            lax.fori_loop(0, nq, untranspose, 0)

    width = 128 * npair
    kv = pl.BlockSpec((s_len, width), lambda p, i: (0, p))
    qo = pl.BlockSpec((t, width), lambda p, i: (i, p))
    tri = pl.BlockSpec((t, t), lambda p, i: (0, 0))
    out = jax.ShapeDtypeStruct((s_len, D_GRP), F32)
    steps = 4 // npair
    return _call_with_exchange(
        body, rides,
        lambda: jnp.logical_and(pl.program_id(0) == 0, pl.program_id(1) == 0),
        lambda: jnp.logical_and(pl.program_id(0) == steps - 1, pl.program_id(1) == nq - 1),
        name="sb_bwd", grid=(steps, nq),
        in_specs=[qo, kv, kv, qo, pl.BlockSpec((2 * npair, t, 128), lambda p, i: (p, i, 0)), tri, tri],
        out_specs=(qo, kv, kv), out_shape=(out, out, out),
        scratch_shapes=[pltpu.VMEM((npair, 2 * t, 128), BF16), pltpu.VMEM((npair, 2 * t, 128), BF16),
                        pltpu.VMEM((npair, 128, 2 * t), BF16), pltpu.VMEM((npair, 128, 2 * t), BF16),
                        pltpu.VMEM((npair, nq, 128, t), BF16),
                        pltpu.VMEM((npair, 128, 2 * t), F32),
                        pltpu.VMEM((npair, nq, 128, t), F32), pltpu.VMEM((npair, nq, 128, t), F32),
                        pltpu.VMEM((npair, 2 * t, 128), F32)],
    )(qs, ks, vs, dos, csaved, tri_suf, tri_pre, *rides)


def _attn_out(o_b, lse_b, o_sb, x2, wdil, wsb, out_g, shards):
    s_len = x2.shape[0]
    tm = ROW_TILE

    def body(o1_ref, l1_ref, o4_ref, l4_ref, o16_ref, l16_ref, osb_ref, x_ref, wdil_ref, wsb_ref, w_ref,
             odil_ref, lse_ref, lse4_ref, lse16_ref, mixed_ref, x1_ref, stage, nat4, nat16):
        _merge_views((o4_ref, l4_ref), (o16_ref, l16_ref), stage, nat4, nat16)
        os_ = (o1_ref[...], _slab_group(nat4, 0), _slab_group(nat16, 0))
        ls = (l1_ref[...], _slab_group(nat4, 1), _slab_group(nat16, 1))
        mx = jnp.maximum(jnp.maximum(ls[0], ls[1]), ls[2])
        es = [jnp.exp(l - mx) for l in ls]
        den = es[0] + es[1] + es[2]
        o_dil = (es[0] * os_[0] + es[1] * os_[1] + es[2] * os_[2]) / den
        odil_ref[...] = o_dil
        lse = mx + jnp.log(den)
        lse_ref[...] = lse
        for p in range(4):
            nat4[p] = lse[:, 128 * p:128 * (p + 1)]
        _split_views(nat4.at[0:4], stage.at[0:4], (lse4_ref,), (lse16_ref,))
        halves = []
        for t, w_r in ((o_dil, wdil_ref), (osb_ref[...], wsb_ref)):
            r = lax.rsqrt(jnp.mean(t * t, axis=-1, keepdims=True) + EPS)
            halves.append(((t * r) * w_r[...]).astype(BF16))
        mixed = jnp.concatenate(halves, axis=1)
        mixed_ref[...] = mixed
        w = w_ref[...].reshape(D_MODEL, D_MODEL)
        x1_ref[...] = x_ref[...] + _dot(mixed, w)

    row = lambda w: pl.BlockSpec((tm, w), lambda i: (i, 0))
    ni = s_len // tm
    return _call_with_gather(
        body, shards, lambda: pl.program_id(0) == 0, lambda: pl.program_id(0) == ni - 2,
        lambda: pl.program_id(0) == ni - 1,
        name="attn_out", grid=(ni,),
        in_specs=[row(D_GRP)] * 2 + [_view_spec(tm, 4)] * 2 + [_view_spec(tm, 16)] * 2
        + [row(D_GRP), row(D_MODEL), _full((1, D_GRP)), _full((1, D_GRP)), _full((N_DEV, OUT_SHARD, D_MODEL))],
        out_specs=(row(D_GRP), row(D_GRP), _view_spec(tm, 4), _view_spec(tm, 16), row(D_MODEL), row(D_MODEL)),
        out_shape=(jax.ShapeDtypeStruct((s_len, D_GRP), F32), jax.ShapeDtypeStruct((s_len, D_GRP), F32),
                   _view_shape(s_len, 4, F32), _view_shape(s_len, 16, F32),
                   jax.ShapeDtypeStruct((s_len, D_MODEL), BF16), jax.ShapeDtypeStruct((s_len, D_MODEL), F32)),
        scratch_shapes=[pltpu.VMEM((8, tm, 128), F32)] * 3,
    )(o_b[0], lse_b[0], o_b[1], lse_b[1], o_b[2], lse_b[2], o_sb, x2, wdil, wsb, out_g, *shards)


def _two_shards(w_ref):
    return w_ref[...].reshape(FF_BLOCK, D_MODEL)


def _ffn_fwd(x1, wn2, tgt, gate_g, up_g, down_g):
    s_len = x1.shape[0]
    tm = ROW_TILE
    ni = s_len // tm

    def body(x_ref, wn_ref, t_ref, wg_ref, wu_ref, wd_ref, g_ref, u_ref, h2_ref, dy_ref, loss_ref, acc):
        j = pl.program_id(1)

        @pl.when(j == 0)
        def _():
            xx = x_ref[...]
            r = lax.rsqrt(jnp.mean(xx * xx, axis=-1, keepdims=True) + EPS)
            h2_ref[...] = ((xx * r) * wn_ref[...]).astype(BF16)
            acc[...] = jnp.zeros_like(acc)

        h = h2_ref[...]
        g = _dot_nt(h, _two_shards(wg_ref))
        u = _dot_nt(h, _two_shards(wu_ref))
        g_ref[...] = g
        u_ref[...] = u
        act = (g * (1.0 / (1.0 + jnp.exp(-g)))) * u
        acc[...] += _dot(act.astype(BF16), _two_shards(wd_ref))

        @pl.when(j == FF_STEPS - 1)
        def _():
            err = (x_ref[...] + acc[...]) - t_ref[...]
            dy_ref[...] = err * (1.0 / D_MODEL)
            part = 0.5 * jnp.sum(jnp.mean(err * err, axis=-1, keepdims=True))
            loss_ref[...] = jnp.full((8, 128), part, F32)

    row = pl.BlockSpec((tm, D_MODEL), lambda i, j: (i, 0))
    hid = pl.BlockSpec((tm, FF_BLOCK), lambda i, j: (i, j))
    return pl.pallas_call(
        body, name="ffn_fwd", grid=(ni, FF_STEPS),
        in_specs=[row, pl.BlockSpec((1, D_MODEL), lambda i, j: (0, 0)), row,
                  pl.BlockSpec((2, FF_PAD, D_MODEL), lambda i, j: (j, 0, 0)),
                  pl.BlockSpec((2, FF_PAD, D_MODEL), lambda i, j: (j, 0, 0)),
                  pl.BlockSpec((2, FF_PAD, D_MODEL), lambda i, j: (j, 0, 0))],
        out_specs=(hid, hid, row, row, pl.BlockSpec((8, 128), lambda i, j: (i, 0))),
        out_shape=(jax.ShapeDtypeStruct((s_len, N_DEV * FF_PAD), F32),
                   jax.ShapeDtypeStruct((s_len, N_DEV * FF_PAD), F32),
                   jax.ShapeDtypeStruct((s_len, D_MODEL), BF16),
                   jax.ShapeDtypeStruct((s_len, D_MODEL), F32),
                   jax.ShapeDtypeStruct((ni * 8, 128), F32)),
        scratch_shapes=[pltpu.VMEM((tm, D_MODEL), F32)],
        compiler_params=_params(),
    )(x1, wn2, tgt, gate_g, up_g, down_g)


def _ffn_bwd_dx(dy, g, u, gate_g, up_g, down_g):
    s_len = dy.shape[0]
    tm = ROW_TILE

    def body(dy_ref, g_ref, u_ref, wg_ref, wu_ref, wd_ref, dg_ref, du_ref, act_ref, dh_ref, acc):
        j = pl.program_id(1)

        @pl.when(j == 0)
        def _():
            acc[...] = jnp.zeros_like(acc)

        gg, uu = g_ref[...], u_ref[...]
        da = _dot_nt(dy_ref[...].astype(BF16), _two_shards(wd_ref))
        sig = 1.0 / (1.0 + jnp.exp(-gg))
        silu = gg * sig
        act_ref[...] = (silu * uu).astype(BF16)
        du = (da * silu).astype(BF16)
        dg = (da * uu * (sig * (1.0 + gg * (1.0 - sig)))).astype(BF16)
        du_ref[...] = du
        dg_ref[...] = dg
        acc[...] += _dot(dg, _two_shards(wg_ref)) + _dot(du, _two_shards(wu_ref))

        @pl.when(j == FF_STEPS - 1)
        def _():
            dh_ref[...] = acc[...]

    row = pl.BlockSpec((tm, D_MODEL), lambda i, j: (i, 0))
    hid = pl.BlockSpec((tm, FF_BLOCK), lambda i, j: (i, j))
    hid_bf = jax.ShapeDtypeStruct((s_len, N_DEV * FF_PAD), BF16)
    return pl.pallas_call(
        body, name="ffn_bwd_dx", grid=(s_len // tm, FF_STEPS),
        in_specs=[row, hid, hid,
                  pl.BlockSpec((2, FF_PAD, D_MODEL), lambda i, j: (j, 0, 0)),
                  pl.BlockSpec((2, FF_PAD, D_MODEL), lambda i, j: (j, 0, 0)),
                  pl.BlockSpec((2, FF_PAD, D_MODEL), lambda i, j: (j, 0, 0))],
        out_specs=(hid, hid, hid, row),
        out_shape=(hid_bf, hid_bf, hid_bf, jax.ShapeDtypeStruct((s_len, D_MODEL), F32)),
        scratch_shapes=[pltpu.VMEM((tm, D_MODEL), F32)],
        compiler_params=_params(),
    )(dy, g, u, gate_g, up_g, down_g)


def _ffn_bwd_dw(h2, dy, dg, du, act):
    s_len = h2.shape[0]
    tm = ROW_TILE
    ni = s_len // tm

    def body(h_ref, dy_ref, dg_ref, du_ref, act_ref, dwg_ref, dwu_ref, dwd_ref, ag, au, ad):
        i = pl.program_id(1)

        @pl.when(i == 0)
        def _():
            ag[...] = jnp.zeros_like(ag)
            au[...] = jnp.zeros_like(au)
            ad[...] = jnp.zeros_like(ad)

        h = h_ref[...]
        ag[...] += _dot_tn(h, dg_ref[...])
        au[...] += _dot_tn(h, du_ref[...])
        ad[...] += _dot_tn(act_ref[...], dy_ref[...].astype(BF16))

        @pl.when(i == ni - 1)
        def _():
            for half in range(2):
                cols = slice(FF_PAD * half, FF_PAD * (half + 1))
                dwg_ref[half] = ag[:, cols].astype(BF16)
                dwu_ref[half] = au[:, cols].astype(BF16)
            dwd_ref[...] = ad[...].astype(BF16).reshape(2, FF_PAD, D_MODEL)

    row = pl.BlockSpec((tm, D_MODEL), lambda j, i: (i, 0))
    hid = pl.BlockSpec((tm, FF_BLOCK), lambda j, i: (i, j))
    col_w = pl.BlockSpec((2, D_MODEL, FF_PAD), lambda j, i: (j, 0, 0))
    row_w = pl.BlockSpec((2, FF_PAD, D_MODEL), lambda j, i: (j, 0, 0))
    return pl.pallas_call(
        body, name="ffn_bwd_dw", grid=(FF_STEPS, ni),
        in_specs=[row, row, hid, hid, hid], out_specs=(col_w, col_w, row_w),
        out_shape=(jax.ShapeDtypeStruct((N_DEV, D_MODEL, FF_PAD), BF16),
                   jax.ShapeDtypeStruct((N_DEV, D_MODEL, FF_PAD), BF16),
                   jax.ShapeDtypeStruct((N_DEV, FF_PAD, D_MODEL), BF16)),
        scratch_shapes=[pltpu.VMEM((D_MODEL, FF_BLOCK), F32), pltpu.VMEM((D_MODEL, FF_BLOCK), F32),
                        pltpu.VMEM((FF_BLOCK, D_MODEL), F32)],
        compiler_params=_params(),
    )(h2, dy, dg, du, act)


def _rms_bwd(dy, t, w):
    r = lax.rsqrt(jnp.mean(t * t, axis=-1, keepdims=True) + EPS)
    gw = dy * w
    dt = r * (gw - t * ((r * r) * jnp.mean(gw * t, axis=-1, keepdims=True)))
    return dt, dy * t * r


def _attn_out_bwd(dy, dh2, x1, wn2, b_g, mixed, o_dil, o_sb, wdil, wsb, bd512):
    s_len = dy.shape[0]
    tm = ROW_TILE
    ni = s_len // tm

    def body(dy_ref, dh_ref, x1_ref, wn_ref, w_ref, mixed_ref, odil_ref, osb_ref, wdil_ref, wsb_ref, bd_ref,
             dx1_ref, dodil_ref, delta_ref, dosb_ref, dwout_ref, dwn_ref, dwdil_ref, dwsb_ref,
             do4_ref, dl4_ref, do16_ref, dl16_ref, wacc, both, stage):
        i = pl.program_id(0)

        @pl.when(i == 0)
        def _():
            wacc[...] = jnp.zeros_like(wacc)
            dwn_ref[...] = jnp.zeros_like(dwn_ref)
            dwdil_ref[...] = jnp.zeros_like(dwdil_ref)
            dwsb_ref[...] = jnp.zeros_like(dwsb_ref)

        dnorm, dw_rows = _rms_bwd(dh_ref[...], x1_ref[...], wn_ref[...])
        dx1 = dy_ref[...] + dnorm
        dx1_ref[...] = dx1
        dwn_ref[...] += jnp.sum(dw_rows, axis=0, keepdims=True)
        dx1b = dx1.astype(BF16)
        w = w_ref[...].reshape(D_MODEL, D_MODEL)
        dmixed = _dot_nt(dx1b, w)
        wacc[...] += _dot_tn(mixed_ref[...], dx1b)
        o_dil = odil_ref[...]
        d_odil, dw_rows = _rms_bwd(dmixed[:, :D_GRP], o_dil, wdil_ref[...])
        dwdil_ref[...] += jnp.sum(dw_rows, axis=0, keepdims=True)
        dodil_ref[...] = d_odil.astype(BF16)
        delta = _mm_split(d_odil * o_dil, bd_ref[...])
        delta_ref[...] = delta
        for p in range(4):
            both[p] = d_odil[:, 128 * p:128 * (p + 1)]
            both[4 + p] = delta[:, 128 * p:128 * (p + 1)]
        _split_views(both, stage, (do4_ref, dl4_ref), (do16_ref, dl16_ref))
        d_osb, dw_rows = _rms_bwd(dmixed[:, D_GRP:], osb_ref[...], wsb_ref[...])
        dwsb_ref[...] += jnp.sum(dw_rows, axis=0, keepdims=True)
        dosb_ref[...] = d_osb.astype(BF16)

        @pl.when(i == ni - 1)
        def _():
            dwout_ref[...] = wacc[...].astype(BF16).reshape(N_DEV, OUT_SHARD, D_MODEL)

    row = lambda w: pl.BlockSpec((tm, w), lambda i: (i, 0))
    return pl.pallas_call(
        body, name="attn_out_bwd", grid=(ni,),
        in_specs=[row(D_MODEL), row(D_MODEL), row(D_MODEL), _full((1, D_MODEL)),
                  _full((N_DEV, OUT_SHARD, D_MODEL)),
                  row(D_MODEL), row(D_GRP), row(D_GRP), _full((1, D_GRP)), _full((1, D_GRP)),
                  _full((D_GRP, D_GRP))],
        out_specs=(row(D_MODEL), row(D_GRP), row(D_GRP), row(D_GRP),
                   _full((N_DEV, OUT_SHARD, D_MODEL)), _full((1, D_MODEL)), _full((1, D_GRP)), _full((1, D_GRP)),
                   _view_spec(tm, 4), _view_spec(tm, 4), _view_spec(tm, 16), _view_spec(tm, 16)),
        out_shape=(jax.ShapeDtypeStruct((s_len, D_MODEL), F32), jax.ShapeDtypeStruct((s_len, D_GRP), BF16),
                   jax.ShapeDtypeStruct((s_len, D_GRP), F32), jax.ShapeDtypeStruct((s_len, D_GRP), BF16),
                   jax.ShapeDtypeStruct((N_DEV, OUT_SHARD, D_MODEL), BF16),
                   jax.ShapeDtypeStruct((1, D_MODEL), F32), jax.ShapeDtypeStruct((1, D_GRP), F32),
                   jax.ShapeDtypeStruct((1, D_GRP), F32),
                   _view_shape(s_len, 4, BF16), _view_shape(s_len, 4, F32),
                   _view_shape(s_len, 16, BF16), _view_shape(s_len, 16, F32)),
        scratch_shapes=[pltpu.VMEM((D_MODEL, D_MODEL), F32), pltpu.VMEM((8, tm, 128), F32),
                        pltpu.VMEM((8, tm, 128), F32)],
        compiler_params=_params(),
    )(dy, dh2, x1, wn2, b_g, mixed, o_dil, o_sb, wdil, wsb, bd512)


def _qkv_bwd(dq_b, dk_b, dv_b, dqs, dks, dvs, qraw, kraw, cos2, sin2, qnw, knw, bd):
    s_len = qraw.shape[0]
    tm = ROW_TILE
    ni = s_len // tm

    def body(dq1, dk1, dv1, dq4, dk4, dv4, dq16, dk16, dv16, dqs_ref, dks_ref, dvs_ref,
             qraw_ref, kraw_ref, cos_ref, sin_ref, qnw_ref, knw_ref, bd_ref,
             dproj_ref, dqn_ref, dkn_ref, stage, nat4, nat16):
        i = pl.program_id(0)

        @pl.when(i == 0)
        def _():
            dqn_ref[...] = jnp.zeros_like(dqn_ref)
            dkn_ref[...] = jnp.zeros_like(dkn_ref)

        _merge_views((dq4, dk4, dv4), (dq16, dk16, dv16), stage, nat4, nat16)
        cos_t, sin_t, bdm = cos_ref[...], sin_ref[...], bd_ref[...]
        for grp, (part1, raw_ref, nw_ref, dn_ref) in enumerate(((dq1, qraw_ref, qnw_ref, dqn_ref),
                                                                (dk1, kraw_ref, knw_ref, dkn_ref))):
            dn_acc = 0.0
            for p in range(4):
                cols = slice(128 * p, 128 * (p + 1))
                d_rope = part1[:, cols] + nat4[4 * grp + p] + nat16[4 * grp + p]
                d_norm = d_rope * cos_t + _swap_halves(d_rope * sin_t)
                t = raw_ref[:, cols]
                w = nw_ref[...]
                r = lax.rsqrt(_mm_split(t * t, bdm) * (1.0 / HEAD_DIM) + EPS)
                gw = d_norm * w
                corr = _mm_split(gw * t, bdm) * (1.0 / HEAD_DIM)
                dt = r * (gw - t * ((r * r) * corr))
                dn_acc = dn_acc + jnp.sum(d_norm * t * r, axis=0, keepdims=True)
                dproj_ref[:, D_GRP * grp + 128 * p:D_GRP * grp + 128 * (p + 1)] = dt.astype(BF16)
            dn_ref[...] += dn_acc
        dproj_ref[:, 2 * D_GRP:3 * D_GRP] = (dv1[...] + _slab_group(nat4, 2) + _slab_group(nat16, 2)).astype(BF16)
        dproj_ref[:, 3 * D_GRP:4 * D_GRP] = dqs_ref[...].astype(BF16)
        dproj_ref[:, 4 * D_GRP:5 * D_GRP] = dks_ref[...].astype(BF16)
        dproj_ref[:, 5 * D_GRP:6 * D_GRP] = dvs_ref[...].astype(BF16)

    row = lambda w: pl.BlockSpec((tm, w), lambda i: (i, 0))
    return pl.pallas_call(
        body, name="qkv_bwd", grid=(ni,),
        in_specs=[row(D_GRP)] * 3 + [_view_spec(tm, 4)] * 3 + [_view_spec(tm, 16)] * 3 + [row(D_GRP)] * 5
        + [row(128), row(128), _full((1, 128)), _full((1, 128)), _full((128, 128))],
        out_specs=(row(D_IN), _full((1, 128)), _full((1, 128))),
        out_shape=(jax.ShapeDtypeStruct((s_len, D_IN), BF16), jax.ShapeDtypeStruct((1, 128), F32),
                   jax.ShapeDtypeStruct((1, 128), F32)),
        scratch_shapes=[pltpu.VMEM((12, tm, 128), F32)] * 3,
        compiler_params=_params(),
    )(dq_b[0], dk_b[0], dv_b[0], dq_b[1], dk_b[1], dv_b[1], dq_b[2], dk_b[2], dv_b[2],
      dqs, dks, dvs, qraw, kraw, cos2, sin2, qnw, knw, bd)


def _in_bwd_dx(dproj, a_g, x2, dx1, wn1, rides):
    s_len = x2.shape[0]
    tm = ROW_TILE
    ni = s_len // tm

    def body(dp_ref, w_ref, x_ref, dx1_ref, wn_ref, gx_ref, dwn_ref, w_full):
        i = pl.program_id(0)

        @pl.when(i == 0)
        def _():
            dwn_ref[...] = jnp.zeros_like(dwn_ref)
            for d in range(N_DEV):
                w_full[:, IN_SHARD * d:IN_SHARD * (d + 1)] = w_ref[d]

        dh = _dot_nt(dp_ref[...], w_full[...])
        dnorm, dw_rows = _rms_bwd(dh, x_ref[...], wn_ref[...])
        gx_ref[...] = dx1_ref[...] + dnorm
        dwn_ref[...] += jnp.sum(dw_rows, axis=0, keepdims=True)

    row = lambda w: pl.BlockSpec((tm, w), lambda i: (i, 0))
    return _call_with_exchange(
        body, rides, lambda: pl.program_id(0) == 0, lambda: pl.program_id(0) == ni - 1,
        name="in_bwd_dx", grid=(ni,),
        in_specs=[row(D_IN), pl.BlockSpec((N_DEV, D_MODEL, IN_SHARD), lambda i: (0, 0, 0)),
                  row(D_MODEL), row(D_MODEL), _full((1, D_MODEL))],
        out_specs=(row(D_MODEL), _full((1, D_MODEL))),
        out_shape=(jax.ShapeDtypeStruct((s_len, D_MODEL), F32), jax.ShapeDtypeStruct((1, D_MODEL), F32)),
        scratch_shapes=[pltpu.VMEM((D_MODEL, D_IN), BF16)],
    )(dproj, a_g, x2, dx1, wn1, *rides)


def _in_bwd_dw(h1, dproj):
    s_len = h1.shape[0]
    tm = ROW_TILE
    ni = s_len // tm

    def body(h_ref, dp_ref, dw_ref, acc):
        i = pl.program_id(1)

        @pl.when(i == 0)
        def _():
            acc[...] = jnp.zeros_like(acc)

        acc[...] += _dot_tn(h_ref[...], dp_ref[...])

        @pl.when(i == ni - 1)
        def _():
            for half in range(2):
                dw_ref[half] = acc[:, IN_SHARD * half:IN_SHARD * (half + 1)].astype(BF16)

    return pl.pallas_call(
        body, name="in_bwd_dw", grid=(N_DEV // 2, ni),
        in_specs=[pl.BlockSpec((tm, D_MODEL), lambda d, i: (i, 0)),
                  pl.BlockSpec((tm, 2 * IN_SHARD), lambda d, i: (i, d))],
        out_specs=pl.BlockSpec((2, D_MODEL, IN_SHARD), lambda d, i: (d, 0, 0)),
        out_shape=jax.ShapeDtypeStruct((N_DEV, D_MODEL, IN_SHARD), BF16),
        scratch_shapes=[pltpu.VMEM((D_MODEL, 2 * IN_SHARD), F32)],
        compiler_params=_params(),
    )(h1, dproj)


def _adamw(recv, w, m, v):
    rows, cols = w.shape
    tr = next((t for t in (128, 32) if rows % t == 0), rows)

    def body(p_ref, w_ref, m_ref, v_ref, g_ref, d_ref, nm_ref, nv_ref):
        g = p_ref[0].astype(F32)
        for s in range(1, N_DEV):
            g = g + p_ref[s].astype(F32)
        m_new = ADAM_B1 * m_ref[...] + (1.0 - ADAM_B1) * g
        v_new = ADAM_B2 * v_ref[...] + (1.0 - ADAM_B2) * (g * g)
        m_hat = m_new / (1.0 - ADAM_B1 ** ADAM_STEP)
        v_hat = v_new / (1.0 - ADAM_B2 ** ADAM_STEP)
        g_ref[...] = g
        d_ref[...] = -ADAM_LR * (m_hat / (jnp.sqrt(v_hat) + ADAM_EPS) + ADAM_WD * w_ref[...])
        nm_ref[...] = m_new
        nv_ref[...] = v_new

    blk = pl.BlockSpec((tr, cols), lambda i: (i, 0))
    out = jax.ShapeDtypeStruct((rows, cols), F32)
    return pl.pallas_call(
        body, name=f"adamw_{rows}x{cols}", grid=(rows // tr,),
        in_specs=[pl.BlockSpec((N_DEV, tr, cols), lambda i: (0, i, 0)), blk, blk, blk],
        out_specs=(blk,) * 4, out_shape=(out,) * 4,
        compiler_params=_params(),
    )(recv, w, m, v)


def _rope_tables(s_len):
    pos = jnp.arange(s_len, dtype=F32)
    inv_freq = ROPE_THETA ** (-jnp.arange(0, HEAD_DIM, 2, dtype=F32) / HEAD_DIM)
    ang = pos[:, None] * inv_freq[None, :]
    cos, sin = jnp.cos(ang), jnp.sin(ang)
    cos2 = jnp.concatenate([cos, cos, cos, cos], axis=1)
    sin2 = jnp.concatenate([-sin, sin, -sin, sin], axis=1)
    return cos2, sin2


def _block_diag_ones(n):
    i = jnp.arange(n)
    return (i[:, None] // HEAD_DIM == i[None, :] // HEAD_DIM).astype(BF16)


def _pad_cols(t):
    return jnp.pad(t, ((0, 0), (0, FF_PAD - FF_SHARD)))


def _pad_rows(t):
    return jnp.pad(t, ((0, FF_PAD - FF_SHARD), (0, 0)))


LOSS_ROW = 26


def _pack_small(n1, n2, ndil, nsb, nq, nk, scalar=None):
    pad = lambda t: jnp.pad(t.reshape(1, -1), ((0, 0), (0, 128 - t.size)))
    last = jnp.zeros((1, 128), F32) if scalar is None else pad(scalar)
    rows = [n1.reshape(8, 128), n2.reshape(8, 128), ndil.reshape(4, 128), nsb.reshape(4, 128),
            pad(nq), pad(nk), last, jnp.zeros((5, 128), F32)]
    return jnp.concatenate(rows, axis=0)


def _unpack_small(t):
    return (t[0:8].reshape(1, D_MODEL), t[8:16].reshape(1, D_MODEL), t[16:20].reshape(1, D_GRP),
            t[20:24].reshape(1, D_GRP), t[24:25, :HEAD_DIM], t[25:26, :HEAD_DIM])


def kernel(x, attn_norm_w, w_in, q_norm_w, k_norm_w, dil_out_norm_w, sb_out_norm_w, w_out, ffn_norm_w, w_gate, w_up, w_down, loss_target, m_attn_norm_w, m_w_in, m_q_norm_w, m_k_norm_w, m_dil_out_norm_w, m_sb_out_norm_w, m_w_out, m_ffn_norm_w, m_w_gate, m_w_up, m_w_down, v_attn_norm_w, v_w_in, v_q_norm_w, v_k_norm_w, v_dil_out_norm_w, v_sb_out_norm_w, v_w_out, v_ffn_norm_w, v_w_gate, v_w_up, v_w_down):
    s_len = x.shape[1]
    x2, tgt = x[0], loss_target[0]

    (a_g,) = _gather_weights([w_in[0].astype(BF16)])
    gate_loc = _pad_cols(w_gate[0]).T.astype(BF16)
    up_loc = _pad_cols(w_up[0]).T.astype(BF16)
    down_loc = _pad_rows(w_down[0]).astype(BF16)
    out_loc = w_out[0].astype(BF16)

    cos2, sin2 = _rope_tables(s_len)
    bd128, bd512 = _block_diag_ones(128), _block_diag_ones(D_GRP)
    idx = jnp.arange(SB_TILE)
    tri_suf = (idx[:, None] > idx[None, :]).astype(BF16)
    tri_pre = (idx[:, None] < idx[None, :]).astype(BF16)
    qnw2 = jnp.concatenate([q_norm_w, q_norm_w], axis=1)
    knw2 = jnp.concatenate([k_norm_w, k_norm_w], axis=1)

    (h1, qraw, kraw, q, k, va, qs, ks, vs, q4, k4, v4, q16, k16, v16,
     out_g, gate_g) = _attn_in(x2, attn_norm_w, a_g, cos2, sin2, qnw2, knw2, bd128, shards=[out_loc, gate_loc])
    qkv_views = {1: (q, k, va), 4: (q4, k4, v4), 16: (q16, k16, v16)}
    o_b, lse_b = [], []
    for r in DILATIONS:
        o, lse = _dil_fwd(*qkv_views[r], r)
        o_b.append(o)
        lse_b.append(lse)
    o_sb, c_sb, up_g = _sb_fwd(qs, ks, vs, tri_suf, shards=[up_loc])
    o_dil, lse_tot, lse4, lse16, mixed, x1, down_g = _attn_out(
        o_b, lse_b, o_sb, x2, dil_out_norm_w, sb_out_norm_w, out_g, shards=[down_loc])
    g, u, h2, dy, loss_parts = _ffn_fwd(x1, ffn_norm_w, tgt, gate_g, up_g, down_g)
    loss_local = jnp.sum(loss_parts[::8, 0])

    dg, du, act, dh2 = _ffn_bwd_dx(dy, g, u, gate_g, up_g, down_g)
    (dx1, do_dil, delta, do_sb, dwout, dn2, dndil, dnsb, do4, dl4, do16, dl16) = _attn_out_bwd(
        dy, dh2, x1, ffn_norm_w, out_g, mixed, o_dil, o_sb, dil_out_norm_w, sb_out_norm_w, bd512)
    dwg, dwu, dwd = _ffn_bwd_dw(h2, dy, dg, du, act)
    dqs, dks, dvs, r_gate, r_down = _sb_bwd(qs, ks, vs, do_sb, c_sb, tri_suf, tri_pre, rides=[dwg, dwd])
    cot_views = {1: (do_dil, lse_tot, delta), 4: (do4, lse4, dl4), 16: (do16, lse16, dl16)}
    riders = {1: [], 4: [dwout], 16: [dwu]}
    dq_b, dk_b, dv_b, landed = [], [], [], {}
    for r in DILATIONS:
        dq, dk, dv, *landed[r] = _dil_bwd(*qkv_views[r], *cot_views[r], r, rides=riders[r])
        dq_b.append(dq)
        dk_b.append(dk)
        dv_b.append(dv)
    (r_out,), (r_up,) = landed[4], landed[16]
    dproj, dqn2, dkn2 = _qkv_bwd(dq_b, dk_b, dv_b, dqs, dks, dvs, qraw, kraw, cos2, sin2, qnw2, knw2, bd128)
    dwin = _in_bwd_dw(h1, dproj)
    grad_x, dn1, r_in = _in_bwd_dx(dproj, a_g, x2, dx1, attn_norm_w, rides=[dwin])
    dqn = dqn2[:, :HEAD_DIM] + dqn2[:, HEAD_DIM:]
    dkn = dkn2[:, :HEAD_DIM] + dkn2[:, HEAD_DIM:]

    small = _pack_small(dn1, dn2, dndil, dnsb, dqn, dkn, loss_local)
    (r_small,) = _exchange_grads([], small)
    big = {
        "w_in": _adamw(r_in, w_in[0], m_w_in[0], v_w_in[0]),
        "w_gate": tuple(t[:, :FF_SHARD] for t in _adamw(r_gate, _pad_cols(w_gate[0]), _pad_cols(m_w_gate[0]), _pad_cols(v_w_gate[0]))),
        "w_up": tuple(t[:, :FF_SHARD] for t in _adamw(r_up, _pad_cols(w_up[0]), _pad_cols(m_w_up[0]), _pad_cols(v_w_up[0]))),
        "w_down": _adamw(r_down, w_down[0], m_w_down[0], v_w_down[0]),
        "w_out": _adamw(r_out, w_out[0], m_w_out[0], v_w_out[0]),
    }
    packs = [_pack_small(*ts) for ts in (
        (attn_norm_w, ffn_norm_w, dil_out_norm_w, sb_out_norm_w, q_norm_w, k_norm_w),
        (m_attn_norm_w, m_ffn_norm_w, m_dil_out_norm_w, m_sb_out_norm_w, m_q_norm_w, m_k_norm_w),
        (v_attn_norm_w, v_ffn_norm_w, v_dil_out_norm_w, v_sb_out_norm_w, v_q_norm_w, v_k_norm_w))]
    small_raw = _adamw(r_small, *packs)
    loss = small_raw[0][LOSS_ROW, 0]
    small_out = [_unpack_small(t) for t in small_raw]
    names = ["attn_norm_w", "w_in", "q_norm_w", "k_norm_w", "dil_out_norm_w", "sb_out_norm_w", "w_out",
             "ffn_norm_w", "w_gate", "w_up", "w_down"]
    small_pos = {"attn_norm_w": 0, "ffn_norm_w": 1, "dil_out_norm_w": 2, "sb_out_norm_w": 3,
                 "q_norm_w": 4, "k_norm_w": 5}
    outs = [loss, grad_x[None]]
    for kind in range(4):
        for name in names:
            if name in small_pos:
                outs.append(small_out[kind][small_pos[name]])
            else:
                outs.append(big[name][kind][None])
    return tuple(outs)
```

```python
import jax
import jax.numpy as jnp
from jax import lax
from jax.experimental import pallas as pl
from jax.experimental.pallas import tpu as pltpu

F32 = jnp.float32
BF16 = jnp.bfloat16

N_DEV = 8
D_MODEL = 1024
HEAD_DIM = 64
D_GRP = 512
D_IN = 6 * D_GRP
IN_SHARD = D_IN // N_DEV
FF_SHARD = 352
FF_PAD = 384
FF_BLOCK = 2 * FF_PAD
FF_STEPS = N_DEV // 2
OUT_SHARD = D_MODEL // N_DEV
BLOCK = 128
DILATIONS = (1, 4, 16)
ROPE_THETA = 10000.0
EPS = 1e-6
ATT_SCALE = HEAD_DIM ** -0.5
NEG = -1e30

ADAM_LR = 0.001
ADAM_B1 = 0.9
ADAM_B2 = 0.999
ADAM_EPS = 1e-08
ADAM_WD = 0.01
ADAM_STEP = 10

SB_TILE = 256
SB_DEAD = -104.0
SB_PAIRS = 4
SB_BWD_PAIRS = 2
ROW_TILE = 512
DW_ROW_TILE = 1024
VMEM_LIMIT = 56 * 1024 * 1024
MESH = pl.DeviceIdType.MESH


def _dot(a, b):
    return jnp.dot(a, b, preferred_element_type=F32)


def _dot_nt(a, b):
    return lax.dot_general(a, b, (((1,), (1,)), ((), ())), preferred_element_type=F32)


def _dot_tn(a, b):
    return lax.dot_general(a, b, (((0,), (0,)), ((), ())), preferred_element_type=F32)


def _mm_split(t, m):
    hi = t.astype(BF16)
    lo = (t - hi.astype(F32)).astype(BF16)
    return _dot(hi, m) + _dot(lo, m)


def _params(**kw):
    return pltpu.CompilerParams(vmem_limit_bytes=VMEM_LIMIT, **kw)


def _full(shape):
    nd = len(shape)
    return pl.BlockSpec(shape, lambda *_: (0,) * nd)


def _view_shape(s_len, r, dtype):
    return jax.ShapeDtypeStruct((s_len // r, r * D_GRP), dtype)


def _view_spec(tm, r):
    return pl.BlockSpec((tm // r, r * D_GRP), lambda i: (i, 0))


def _swap_halves(t):
    lane = lax.broadcasted_iota(jnp.int32, t.shape, 1)
    first = (lane & 32) == 0
    return jnp.where(first, pltpu.roll(t, 96, 1), pltpu.roll(t, 32, 1))


def _log_sigmoid_pair(z):
    neg_abs = lax.bitcast_convert_type(lax.bitcast_convert_type(z, jnp.uint32) | jnp.uint32(0x80000000), F32)
    lb = jnp.minimum(z, 0.0) - jnp.log(1.0 + jnp.exp(neg_abs))
    return lb, lb - z


def _cumsum_mm(t, tri):
    return _dot(t.astype(BF16), tri)


def _split_views(src_ref, stage_ref, views4, views16):
    slabs, n, _ = src_ref.shape
    n4, n16 = n // 4, n // 16
    for j in range(slabs):
        g, lanes = j // 4, 128 * (j % 4)
        src, stage = src_ref.at[j], stage_ref.at[j]
        for c4 in range(4):
            blk = src[pl.ds(c4, n4, stride=4), :]
            stage[n4 * c4:n4 * (c4 + 1), :] = blk
            col = D_GRP * c4 + lanes
            views4[g][:, col:col + 128] = blk.astype(views4[g].dtype)
        for c4 in range(4):
            for c1 in range(4):
                blk = stage[pl.ds(n4 * c4 + c1, n16, stride=4), :]
                col = D_GRP * (4 * c1 + c4) + lanes
                views16[g][:, col:col + 128] = blk.astype(views16[g].dtype)


def _merge_views(views4, views16, stage_ref, dst4_ref, dst16_ref):
    slabs, n, _ = dst4_ref.shape
    n4, n16 = n // 4, n // 16
    for j in range(slabs):
        g, lanes = j // 4, 128 * (j % 4)
        dst4, dst16, stage = dst4_ref.at[j], dst16_ref.at[j], stage_ref.at[j]
        for c4 in range(4):
            col = D_GRP * c4 + lanes
            dst4[pl.ds(c4, n4, stride=4), :] = views4[g][:, col:col + 128].astype(F32)
            for c1 in range(4):
                col = D_GRP * (4 * c1 + c4) + lanes
                stage[pl.ds(n4 * c4 + c1, n16, stride=4), :] = views16[g][:, col:col + 128].astype(F32)
        for c4 in range(4):
            dst16[pl.ds(c4, n4, stride=4), :] = stage[n4 * c4:n4 * (c4 + 1), :]


def _slab_group(ref, g):
    return jnp.concatenate([ref[4 * g + p] for p in range(4)], axis=1)


def _mesh_pos():
    return lax.axis_index("x"), lax.axis_index("y"), lax.axis_index("c")


def _flat_index(p):
    return 4 * p[0] + 2 * p[1] + p[2]


def _gather_weights(shards):
    n_arr = len(shards)

    def body(*refs):
        srcs, outs = refs[:n_arr], refs[n_arr:2 * n_arr]
        send_sems, recv_sems, local_sems = refs[2 * n_arr:]
        x, y, c = _mesh_pos()
        me, sibling = (x, y, c), (x, y, 1 - c)
        chips = [(1 - x, y), (x, 1 - y), (1 - x, 1 - y)]

        def copy(arr, k, block, to, own=False):
            dst = outs[arr].at[_flat_index(block)]
            return pltpu.make_async_remote_copy(
                src_ref=srcs[arr] if own else dst, dst_ref=dst,
                send_sem=send_sems.at[arr, k], recv_sem=recv_sems.at[arr, k],
                device_id=to, device_id_type=MESH)

        for arr in range(n_arr):
            mine = pltpu.make_async_copy(srcs[arr], outs[arr].at[_flat_index(me)], local_sems.at[arr])
            mine.start()
            first = [copy(arr, 0, me, sibling, own=True)]
            first += [copy(arr, 1 + j, me, (*chip, c), own=True) for j, chip in enumerate(chips)]
            for cp in first:
                cp.start()
        for arr in range(n_arr):
            passed = [copy(arr, 4 + j, (*chip, c), sibling) for j, chip in enumerate(chips)]
            for j, chip in enumerate(chips):
                copy(arr, 1 + j, (*chip, c), me).wait_recv()
                passed[j].start()
        for arr in range(n_arr):
            copy(arr, 0, sibling, me).wait_recv()
            for j, chip in enumerate(chips):
                copy(arr, 4 + j, (*chip, 1 - c), me).wait_recv()
            for k in range(7):
                copy(arr, k, me, me).wait_send()
            pltpu.make_async_copy(srcs[arr], outs[arr].at[_flat_index(me)], local_sems.at[arr]).wait()

    any_spec = pl.BlockSpec(memory_space=pl.ANY)
    return pl.pallas_call(
        body, name="gather_weights",
        out_shape=tuple(jax.ShapeDtypeStruct((N_DEV,) + s.shape, s.dtype) for s in shards),
        in_specs=[any_spec] * n_arr, out_specs=(any_spec,) * n_arr,
        scratch_shapes=[pltpu.SemaphoreType.DMA((n_arr, 7)), pltpu.SemaphoreType.DMA((n_arr, 7)),
                        pltpu.SemaphoreType.DMA((n_arr,))],
        compiler_params=pltpu.CompilerParams(has_side_effects=True),
    )(*shards)


def _peer_list(x, y, c):
    return [(1 - x if m & 4 else x, 1 - y if m & 2 else y, 1 - c if m & 1 else c) for m in range(1, N_DEV)]


def _exchange_grads(parts, small):
    n_arr = len(parts)

    def body(*refs):
        ins, outs = refs[:n_arr + 1], refs[n_arr + 1:2 * (n_arr + 1)]
        send_sems, recv_sems, local_sems = refs[2 * (n_arr + 1):]
        x, y, c = _mesh_pos()
        me = (x, y, c)
        my_idx = _flat_index(me)
        peers = []
        for m in range(1, N_DEV):
            peers.append((1 - x if m & 4 else x, 1 - y if m & 2 else y, 1 - c if m & 1 else c))

        def src_block(arr, dev):
            return ins[arr] if arr == n_arr else ins[arr].at[_flat_index(dev)]

        def copy(arr, k):
            return pltpu.make_async_remote_copy(
                src_ref=src_block(arr, peers[k]), dst_ref=outs[arr].at[my_idx],
                send_sem=send_sems.at[arr, k], recv_sem=recv_sems.at[arr, k],
                device_id=peers[k], device_id_type=MESH)

        def local(arr):
            return pltpu.make_async_copy(src_block(arr, me), outs[arr].at[my_idx], local_sems.at[arr])

        for arr in range(n_arr + 1):
            local(arr).start()
            for k in range(N_DEV - 1):
                copy(arr, k).start()
        for arr in range(n_arr + 1):
            for k in range(N_DEV - 1):
                cp = copy(arr, k)
                cp.wait_send()
                cp.wait_recv()
            local(arr).wait()

    any_spec = pl.BlockSpec(memory_space=pl.ANY)
    out_shape = tuple(jax.ShapeDtypeStruct(p.shape, p.dtype) for p in parts)
    out_shape += (jax.ShapeDtypeStruct((N_DEV,) + small.shape, small.dtype),)
    return pl.pallas_call(
        body, name="exchange_grads",
        out_shape=out_shape,
        in_specs=[any_spec] * (n_arr + 1), out_specs=(any_spec,) * (n_arr + 1),
        scratch_shapes=[pltpu.SemaphoreType.DMA((n_arr + 1, N_DEV - 1)),
                        pltpu.SemaphoreType.DMA((n_arr + 1, N_DEV - 1)),
                        pltpu.SemaphoreType.DMA((n_arr + 1,))],
        compiler_params=pltpu.CompilerParams(has_side_effects=True),
    )(*parts, small)


def _call_with_gather(body, shards, first_step, mid_step, last_step, *, name, grid, in_specs, out_specs,
                      out_shape, scratch_shapes=()):
    out_specs = tuple(out_specs) if isinstance(out_specs, (tuple, list)) else (out_specs,)
    out_shape = tuple(out_shape) if isinstance(out_shape, (tuple, list)) else (out_shape,)
    n_in, n_out, n_scr, n = len(in_specs), len(out_specs), len(scratch_shapes), len(shards)

    def full_body(*refs):
        ins, srcs = refs[:n_in], refs[n_in:n_in + n]
        outs, lands = refs[n_in + n:n_in + n + n_out], refs[n_in + n + n_out:n_in + 2 * n + n_out]
        scratch = refs[n_in + 2 * n + n_out:n_in + 2 * n + n_out + n_scr]
        send_sems, recv_sems, local_sems = refs[-3:]
        x, y, c = _mesh_pos()
        me, sibling = (x, y, c), (x, y, 1 - c)
        chips = [(1 - x, y), (x, 1 - y), (1 - x, 1 - y)]

        def copy(a, k, block, to, own=False):
            dst = lands[a].at[_flat_index(block)]
            return pltpu.make_async_remote_copy(
                src_ref=srcs[a] if own else dst, dst_ref=dst,
                send_sem=send_sems.at[a, k], recv_sem=recv_sems.at[a, k],
                device_id=to, device_id_type=MESH)

        def local(a):
            return pltpu.make_async_copy(srcs[a], lands[a].at[_flat_index(me)], local_sems.at[a])

        @pl.when(first_step())
        def _():
            for a in range(n):
                local(a).start()
                copy(a, 0, me, sibling, own=True).start()
                for j, chip in enumerate(chips):
                    copy(a, 1 + j, me, (*chip, c), own=True).start()

        @pl.when(mid_step())
        def _():
            for a in range(n):
                for j, chip in enumerate(chips):
                    copy(a, 1 + j, (*chip, c), me).wait_recv()
                    copy(a, 4 + j, (*chip, c), sibling).start()

        body(*ins, *outs, *scratch)

        @pl.when(last_step())
        def _():
            for a in range(n):
                copy(a, 0, sibling, me).wait_recv()
                for j, chip in enumerate(chips):
                    copy(a, 4 + j, (*chip, 1 - c), me).wait_recv()
                for k in range(N_DEV - 1):
                    copy(a, k, me, me).wait_send()
                local(a).wait()

    any_spec = pl.BlockSpec(memory_space=pl.ANY)
    return pl.pallas_call(
        full_body, name=name, grid=grid,
        in_specs=list(in_specs) + [any_spec] * n,
        out_specs=out_specs + (any_spec,) * n,
        out_shape=out_shape + tuple(jax.ShapeDtypeStruct((N_DEV,) + t.shape, t.dtype) for t in shards),
        scratch_shapes=list(scratch_shapes) + [pltpu.SemaphoreType.DMA((n, N_DEV - 1)),
                                               pltpu.SemaphoreType.DMA((n, N_DEV - 1)),
                                               pltpu.SemaphoreType.DMA((n,))],
        compiler_params=_params(has_side_effects=True),
    )


def _call_with_exchange(body, rides, first_step, last_step, *, name, grid, in_specs, out_specs, out_shape,
                        scratch_shapes=()):
    out_specs = tuple(out_specs) if isinstance(out_specs, (tuple, list)) else (out_specs,)
    out_shape = tuple(out_shape) if isinstance(out_shape, (tuple, list)) else (out_shape,)
    n_in, n_out, n_scr, n = len(in_specs), len(out_specs), len(scratch_shapes), len(rides)
    if n == 0:
        return pl.pallas_call(body, name=name, grid=grid, in_specs=list(in_specs), out_specs=out_specs,
                              out_shape=out_shape, scratch_shapes=list(scratch_shapes),
                              compiler_params=_params())

    def full_body(*refs):
        ins, srcs = refs[:n_in], refs[n_in:n_in + n]
        outs, lands = refs[n_in + n:n_in + n + n_out], refs[n_in + n + n_out:n_in + 2 * n + n_out]
        scratch = refs[n_in + 2 * n + n_out:n_in + 2 * n + n_out + n_scr]
        send_sems, recv_sems, local_sems = refs[-3:]
        x, y, c = _mesh_pos()
        my_idx = _flat_index((x, y, c))
        peers = _peer_list(x, y, c)

        def remote(a, k):
            return pltpu.make_async_remote_copy(
                src_ref=srcs[a].at[_flat_index(peers[k])], dst_ref=lands[a].at[my_idx],
                send_sem=send_sems.at[a, k], recv_sem=recv_sems.at[a, k],
                device_id=peers[k], device_id_type=MESH)

        def local(a):
            return pltpu.make_async_copy(srcs[a].at[my_idx], lands[a].at[my_idx], local_sems.at[a])

        @pl.when(first_step())
        def _():
            for a in range(n):
                local(a).start()
                for k in range(N_DEV - 1):
                    remote(a, k).start()

        body(*ins, *outs, *scratch)

        @pl.when(last_step())
        def _():
            for a in range(n):
                for k in range(N_DEV - 1):
                    cp = remote(a, k)
                    cp.wait_send()
                    cp.wait_recv()
                local(a).wait()

    any_spec = pl.BlockSpec(memory_space=pl.ANY)
    res = pl.pallas_call(
        full_body, name=name, grid=grid,
        in_specs=list(in_specs) + [any_spec] * n,
        out_specs=out_specs + (any_spec,) * n,
        out_shape=out_shape + tuple(jax.ShapeDtypeStruct(t.shape, t.dtype) for t in rides),
        scratch_shapes=list(scratch_shapes) + [pltpu.SemaphoreType.DMA((n, N_DEV - 1)),
                                               pltpu.SemaphoreType.DMA((n, N_DEV - 1)),
                                               pltpu.SemaphoreType.DMA((n,))],
        compiler_params=_params(has_side_effects=True),
    )
    return res


def _head_norm(t, w128, bd):
    ms = _mm_split(t * t, bd) * (1.0 / HEAD_DIM)
    r = lax.rsqrt(ms + EPS)
    return (t * r) * w128, r


def _attn_in(x2, wn1, a_g, cos2, sin2, qnw, knw, bd, shards):
    s_len = x2.shape[0]
    tm = ROW_TILE

    def body(x_ref, wn_ref, w_ref, cos_ref, sin_ref, qnw_ref, knw_ref, bd_ref,
             h1_ref, qraw_ref, kraw_ref, q_ref, k_ref, va_ref, qs_ref, ks_ref, vs_ref,
             q4_ref, k4_ref, v4_ref, q16_ref, k16_ref, v16_ref, proj, slabs, stage, w_full):
        @pl.when(pl.program_id(0) == 0)
        def _():
            for d in range(N_DEV):
                w_full[:, IN_SHARD * d:IN_SHARD * (d + 1)] = w_ref[d]

        xx = x_ref[...]
        r = lax.rsqrt(jnp.mean(xx * xx, axis=-1, keepdims=True) + EPS)
        h = ((xx * r) * wn_ref[...]).astype(BF16)
        h1_ref[...] = h
        proj[...] = _dot(h, w_full[...])
        cos_t, sin_t, bdm = cos_ref[...], sin_ref[...], bd_ref[...]
        for grp, (raw_ref, rope_ref, nw_ref) in enumerate(((qraw_ref, q_ref, qnw_ref),
                                                           (kraw_ref, k_ref, knw_ref))):
            for p in range(4):
                cols = slice(D_GRP * grp + 128 * p, D_GRP * grp + 128 * (p + 1))
                t = proj[:, cols]
                raw_ref[:, 128 * p:128 * (p + 1)] = t
                yn, _ = _head_norm(t, nw_ref[...], bdm)
                roped = yn * cos_t + _swap_halves(yn) * sin_t
                slabs[4 * grp + p] = roped
                rope_ref[:, 128 * p:128 * (p + 1)] = roped.astype(BF16)
        for p in range(4):
            slabs[8 + p] = proj[:, 2 * D_GRP + 128 * p:2 * D_GRP + 128 * (p + 1)]
        for grp, ref in ((2, va_ref), (3, qs_ref), (4, ks_ref), (5, vs_ref)):
            ref[...] = proj[:, D_GRP * grp:D_GRP * (grp + 1)].astype(BF16)
        _split_views(slabs, stage, (q4_ref, k4_ref, v4_ref), (q16_ref, k16_ref, v16_ref))

    row = lambda w: pl.BlockSpec((tm, w), lambda i: (i, 0))
    grp_bf = jax.ShapeDtypeStruct((s_len, D_GRP), BF16)
    grp_f32 = jax.ShapeDtypeStruct((s_len, D_GRP), F32)
    ni = s_len // tm
    return _call_with_gather(
        body, shards, lambda: pl.program_id(0) == 0, lambda: pl.program_id(0) == ni - 2,
        lambda: pl.program_id(0) == ni - 1,
        name="attn_in", grid=(ni,),
        in_specs=[row(D_MODEL), _full((1, D_MODEL)),
                  pl.BlockSpec((N_DEV, D_MODEL, IN_SHARD), lambda i: (0, 0, 0)),
                  row(128), row(128), _full((1, 128)), _full((1, 128)), _full((128, 128))],
        out_specs=(row(D_MODEL),) + (row(D_GRP),) * 8 + (_view_spec(tm, 4),) * 3 + (_view_spec(tm, 16),) * 3,
        out_shape=(jax.ShapeDtypeStruct((s_len, D_MODEL), BF16), grp_f32, grp_f32) + (grp_bf,) * 6
        + (_view_shape(s_len, 4, BF16),) * 3 + (_view_shape(s_len, 16, BF16),) * 3,
        scratch_shapes=[pltpu.VMEM((tm, D_IN), F32), pltpu.VMEM((12, tm, 128), F32), pltpu.VMEM((12, tm, 128), F32),
                        pltpu.VMEM((D_MODEL, D_IN), BF16)],
    )(x2, wn1, a_g, cos2, sin2, qnw, knw, bd, *shards)


def _band_mask(n):
    i = lax.broadcasted_iota(jnp.int32, (2 * BLOCK, 2 * BLOCK), 0) & (BLOCK - 1)
    j = lax.broadcasted_iota(jnp.int32, (2 * BLOCK, 2 * BLOCK), 1)
    dist = i + BLOCK - j
    return (dist >= 0) & (dist <= BLOCK) & ((n - 1) * BLOCK + j >= 0)


def _stack_heads(t2, head0):
    return jnp.concatenate([jnp.where(head0, t2, 0), jnp.where(head0, 0, t2)], axis=0)


def _unstack_heads(t, head0):
    return jnp.where(head0, t[0:BLOCK], t[BLOCK:2 * BLOCK])


def _dil_fwd(qv, kv, vv, r):
    sub_len = qv.shape[0]
    nb = sub_len // BLOCK

    qb = 2 if nb % 2 == 0 else 1

    def body(q_ref, kp_ref, kc_ref, vp_ref, vc_ref, o_ref, lse_ref):
        n = pl.program_id(1)
        lane = lax.broadcasted_iota(jnp.int32, (BLOCK, 128), 1)
        head0 = lane < HEAD_DIM
        units = [(b, slice(128 * p, 128 * (p + 1))) for b in range(qb) for p in range(4)]
        valid = [_band_mask(qb * n + b) for b in range(qb)]
        rows = [slice(BLOCK * b, BLOCK * (b + 1)) for b in range(qb)]

        def keys(prev_ref, cur_ref, b, c):
            before = prev_ref[:, c] if b == 0 else cur_ref[rows[b - 1], c]
            return jnp.concatenate([before, cur_ref[rows[b], c]], axis=0)

        qqs = [_stack_heads(q_ref[rows[b], c] * ATT_SCALE, head0) for b, c in units]
        kks = [keys(kp_ref, kc_ref, b, c) for b, c in units]
        vvs = [keys(vp_ref, vc_ref, b, c) for b, c in units]
        ss = [_dot_nt(qq, kk) for qq, kk in zip(qqs, kks)]
        prs, dens, lses = [], [], []
        for (b, _), s in zip(units, ss):
            s = jnp.where(valid[b], s, NEG)
            m = jnp.max(s, axis=-1, keepdims=True)
            pr = jnp.exp(s - m)
            den = jnp.sum(pr, axis=-1, keepdims=True)
            prs.append(pr.astype(BF16))
            dens.append(den)
            lses.append(m + jnp.log(den))
        pvs = [_dot(pr, vv2) for pr, vv2 in zip(prs, vvs)]
        for (b, c), pv, den, lse in zip(units, pvs, dens, lses):
            o_ref[rows[b], c] = _unstack_heads(pv / den, head0)
            lse_ref[rows[b], c] = _unstack_heads(jnp.broadcast_to(lse, (2 * BLOCK, 128)), head0)

    cur = pl.BlockSpec((qb * BLOCK, D_GRP), lambda c, n: (n, c))
    prev = pl.BlockSpec((BLOCK, D_GRP), lambda c, n: (jnp.maximum(qb * n - 1, 0), c))
    out = jax.ShapeDtypeStruct(qv.shape, F32)
    return pl.pallas_call(
        body, name=f"dil_fwd_r{r}", grid=(r, nb // qb),
        in_specs=[cur, prev, cur, prev, cur], out_specs=(cur, cur), out_shape=(out, out),
        compiler_params=_params(),
    )(qv, kv, kv, vv, vv)


def _dil_bwd(qv, kv, vv, dov, lsev, deltav, r, rides):
    sub_len = qv.shape[0]
    nb = sub_len // BLOCK

    def body(q_ref, kp_ref, kc_ref, vp_ref, vc_ref, do_ref, lse_ref, dl_ref,
             dq_ref, dk_ref, dv_ref, dk_carry, dv_carry):
        n = pl.program_id(1)

        @pl.when(n == 0)
        def _():
            dk_carry[...] = jnp.zeros_like(dk_carry)
            dv_carry[...] = jnp.zeros_like(dv_carry)

        @pl.when(n < nb)
        def _():
            valid = _band_mask(n)
            lane = lax.broadcasted_iota(jnp.int32, (BLOCK, 128), 1)
            head0 = lane < HEAD_DIM
            pairs = [slice(128 * p, 128 * (p + 1)) for p in range(4)]
            qqs = [_stack_heads(q_ref[:, c] * ATT_SCALE, head0) for c in pairs]
            dos = [_stack_heads(do_ref[:, c], head0) for c in pairs]
            kks = [jnp.concatenate([kp_ref[:, c], kc_ref[:, c]], axis=0) for c in pairs]
            vvs = [jnp.concatenate([vp_ref[:, c], vc_ref[:, c]], axis=0) for c in pairs]
            ss = [_dot_nt(qq, kk) for qq, kk in zip(qqs, kks)]
            dps = [_dot_nt(do, vv2) for do, vv2 in zip(dos, vvs)]
            prs, dss = [], []
            for c, s, dp in zip(pairs, ss, dps):
                stats = []
                for ref in (lse_ref, dl_ref):
                    t2 = ref[:, c]
                    stats.append(jnp.concatenate(
                        [jnp.sum(jnp.where(lane == 0, t2, 0.0), axis=-1, keepdims=True),
                         jnp.sum(jnp.where(lane == HEAD_DIM, t2, 0.0), axis=-1, keepdims=True)], axis=0))
                pr = jnp.where(valid, jnp.exp(jnp.minimum(s - stats[0], 0.0)), 0.0)
                prs.append(pr.astype(BF16))
                dss.append((pr * (dp - stats[1])).astype(BF16))
            dqs = [_dot(ds, kk) for ds, kk in zip(dss, kks)]
            dkks = [_dot_tn(ds, qq) for ds, qq in zip(dss, qqs)]
            dvvs = [_dot_tn(pr, do) for pr, do in zip(prs, dos)]
            for c, dq, dkk, dvv in zip(pairs, dqs, dkks, dvvs):
                dq_ref[:, c] = _unstack_heads(dq, head0) * ATT_SCALE
                dk_ref[:, c] = dk_carry[:, c] + dkk[:BLOCK]
                dv_ref[:, c] = dv_carry[:, c] + dvv[:BLOCK]
                dk_carry[:, c] = dkk[BLOCK:]
                dv_carry[:, c] = dvv[BLOCK:]

        @pl.when(n == nb)
        def _():
            dk_ref[...] = dk_carry[...]
            dv_ref[...] = dv_carry[...]

    last = nb - 1
    cur = pl.BlockSpec((BLOCK, D_GRP), lambda c, n: (jnp.minimum(n, last), c))
    prev = pl.BlockSpec((BLOCK, D_GRP), lambda c, n: (jnp.clip(n - 1, 0, last), c))
    out = jax.ShapeDtypeStruct(qv.shape, F32)
    return _call_with_exchange(
        body, rides,
        lambda: jnp.logical_and(pl.program_id(0) == 0, pl.program_id(1) == 0),
        lambda: jnp.logical_and(pl.program_id(0) == r - 1, pl.program_id(1) == nb),
        name=f"dil_bwd_r{r}", grid=(r, nb + 1),
        in_specs=[cur, prev, cur, prev, cur, cur, cur, cur],
        out_specs=(cur, prev, prev), out_shape=(out, out, out),
        scratch_shapes=[pltpu.VMEM((BLOCK, D_GRP), F32), pltpu.VMEM((BLOCK, D_GRP), F32)],
    )(qv, kv, kv, vv, vv, dov, lsev, deltav, *rides)


def _sb_fwd(qs, ks, vs, tri_suf, shards):
    s_len = qs.shape[0]
    t = SB_TILE
    nq = s_len // t

    npair = SB_PAIRS

    def body(q_ref, k_ref, v_ref, u_ref, o_ref, c_ref, qq, vt, acc, cf, csave):
        row = lax.broadcasted_iota(jnp.int32, (2 * t, t), 0) & (t - 1)
        col = lax.broadcasted_iota(jnp.int32, (2 * t, t), 1)
        diag_mask = col < row
        lane1 = lax.broadcasted_iota(jnp.int32, (t, 128), 1)
        head0 = lane1 < HEAD_DIM
        lane2 = lax.broadcasted_iota(jnp.int32, (2 * t, 128), 1)
        uu = u_ref[...]
        pr = range(npair)
        cols = [slice(128 * pp, 128 * (pp + 1)) for pp in pr]

        i = pl.program_id(1)

        @pl.when(i == 0)
        def _():
            def transpose_v(j, _):
                rows = pl.ds(pl.multiple_of(j * t, t), t)
                for pp in pr:
                    vt[pp, j] = v_ref[rows, cols[pp]].astype(F32).T.astype(BF16)
                return 0

            lax.fori_loop(0, nq, transpose_v, 0)

        for pp in pr:
            q2 = q_ref[:, cols[pp]] * ATT_SCALE
            qq[pp, 0:t, :] = jnp.where(head0, q2, 0)
            qq[pp, t:2 * t, :] = jnp.where(head0, 0, q2)
        acc[...] = jnp.zeros_like(acc)
        cf[...] = jnp.zeros_like(cf)
        csave[...] = jnp.full(csave.shape, 2.0 * SB_DEAD, F32)

        def tile(kb, diag):
            krows = pl.ds(pl.multiple_of(kb * t, t), t)
            zs = [_dot_nt(qq[pp], k_ref[krows, cols[pp]]) for pp in pr]
            lbk = [_log_sigmoid_pair(z) for z in zs]
            lks = [jnp.where(diag_mask, lk, 0.0) if diag else lk for _, lk in lbk]
            sufs = [_cumsum_mm(lk, uu) for lk in lks]
            carries = [cf[pp] for pp in pr]
            avs = []
            for pp in pr:
                a = jnp.exp(lbk[pp][0] + (sufs[pp] + jnp.concatenate([carries[pp]] * (t // 128), axis=1)))
                avs.append((jnp.where(diag_mask, a, 0.0) if diag else a).astype(BF16))
            pvs = [_dot_nt(vt[pp, kb], avs[pp]) for pp in pr]
            for pp in pr:
                acc[pp] += pvs[pp]
                csave[pp] = jnp.where(lane2 == kb, carries[pp], csave[pp])
                cf[pp] = carries[pp] + jnp.broadcast_to(jnp.sum(lks[pp], axis=-1, keepdims=True), (2 * t, 128))

        tile(i, True)

        def alive():
            return jnp.max(cf[...]) > SB_DEAD

        def k_block(state):
            kb, _ = state
            tile(kb, False)
            return kb - 1, alive()

        lax.while_loop(lambda state: jnp.logical_and(state[0] >= 0, state[1]), k_block, (i - 1, alive()))
        for pp in pr:
            o_ref[:, cols[pp]] = jnp.where(head0, acc[pp, :, 0:t].T, acc[pp, :, t:2 * t].T)
            c_ref[2 * pp] = csave[pp, 0:t, :]
            c_ref[2 * pp + 1] = csave[pp, t:2 * t, :]

    width = 128 * npair
    kv = pl.BlockSpec((s_len, width), lambda p, i: (0, p))
    qo = pl.BlockSpec((t, width), lambda p, i: (i, p))
    steps = 4 // npair

    def at(p, i):
        return lambda: jnp.logical_and(pl.program_id(0) == p, pl.program_id(1) == i)

    return _call_with_gather(
        body, shards, at(0, 0), at(steps - 1, (2 * nq) // 3), at(steps - 1, nq - 1),
        name="sb_fwd", grid=(steps, nq),
        in_specs=[qo, kv, kv, pl.BlockSpec((t, t), lambda p, i: (0, 0))],
        out_specs=(qo, pl.BlockSpec((2 * npair, t, 128), lambda p, i: (p, i, 0))),
        out_shape=(jax.ShapeDtypeStruct((s_len, D_GRP), F32),
                   jax.ShapeDtypeStruct((8, s_len, 128), F32)),
        scratch_shapes=[pltpu.VMEM((npair, 2 * t, 128), BF16), pltpu.VMEM((npair, nq, 128, t), BF16),
                        pltpu.VMEM((npair, 128, 2 * t), F32),
                        pltpu.VMEM((npair, 2 * t, 128), F32), pltpu.VMEM((npair, 2 * t, 128), F32)],
    )(qs, ks, vs, tri_suf, *shards)


def _sb_bwd(qs, ks, vs, dos, csaved, tri_suf, tri_pre, rides):
    s_len = qs.shape[0]
    t = SB_TILE
    nq = s_len // t

    npair = SB_BWD_PAIRS

    def body(q_ref, k_ref, v_ref, do_ref, c_ref, u_ref, p_ref, dq_ref, dk_ref, dv_ref,
             qq, dd, qqt, ddt, kt, dq_acc, dkt, dvt, cg):
        row = lax.broadcasted_iota(jnp.int32, (2 * t, t), 0) & (t - 1)
        col = lax.broadcasted_iota(jnp.int32, (2 * t, t), 1)
        diag_mask = col < row
        lane1 = lax.broadcasted_iota(jnp.int32, (t, 128), 1)
        head0 = lane1 < HEAD_DIM
        lane2 = lax.broadcasted_iota(jnp.int32, (2 * t, 128), 1)
        uu, pm = u_ref[...], p_ref[...]
        pr = range(npair)
        cols = [slice(128 * pp, 128 * (pp + 1)) for pp in pr]
        i = pl.program_id(1)

        @pl.when(i == 0)
        def _():
            dkt[...] = jnp.zeros_like(dkt)
            dvt[...] = jnp.zeros_like(dvt)

            def transpose_k(j, _):
                rows = pl.ds(pl.multiple_of(j * t, t), t)
                for pp in pr:
                    kt[pp, j] = k_ref[rows, cols[pp]].astype(F32).T.astype(BF16)
                return 0

            lax.fori_loop(0, nq, transpose_k, 0)

        for pp in pr:
            q2 = q_ref[:, cols[pp]].astype(F32) * ATT_SCALE
            do2 = do_ref[:, cols[pp]].astype(F32)
            for src, nat, tr in ((q2, qq, qqt), (do2, dd, ddt)):
                stacked = jnp.concatenate([jnp.where(head0, src, 0.0), jnp.where(head0, 0.0, src)], axis=0)
                nat[pp] = stacked.astype(BF16)
                tr[pp] = stacked.T.astype(BF16)
        dq_acc[...] = jnp.zeros_like(dq_acc)
        cg[...] = jnp.zeros_like(cg)

        def tile(kb, diag):
            krows = pl.ds(pl.multiple_of(kb * t, t), t)
            zs = [_dot_nt(qq[pp], k_ref[krows, cols[pp]]) for pp in pr]
            das = [_dot_nt(dd[pp], v_ref[krows, cols[pp]]) for pp in pr]
            lbk = [_log_sigmoid_pair(z) for z in zs]
            lks = [jnp.where(diag_mask, lk, 0.0) if diag else lk for _, lk in lbk]
            sufs = [_cumsum_mm(lk, uu) for lk in lks]
            avs, gs = [], []
            for pp in pr:
                cs = jnp.concatenate([c_ref[2 * pp], c_ref[2 * pp + 1]], axis=0)
                cf = jnp.sum(jnp.where(lane2 == kb, cs, 0.0), axis=-1, keepdims=True)
                a = jnp.exp(lbk[pp][0] + (sufs[pp] + cf))
                a = jnp.where(diag_mask, a, 0.0) if diag else a
                avs.append(a.astype(BF16))
                gs.append(a * das[pp])
            gpres = [_cumsum_mm(g, pm) for g in gs]
            dzs = []
            for pp in pr:
                carry = cg[pp]
                beta = jnp.exp(lbk[pp][0])
                dz = gs[pp] - beta * (gs[pp] + (gpres[pp] + jnp.concatenate([carry] * (t // 128), axis=1)))
                dzs.append((jnp.where(diag_mask, dz, 0.0) if diag else dz).astype(BF16))
                cg[pp] = carry + jnp.broadcast_to(jnp.sum(gs[pp], axis=-1, keepdims=True), (2 * t, 128))
            dqs = [_dot_nt(kt[pp, kb], dzs[pp]) for pp in pr]
            dks = [_dot(qqt[pp], dzs[pp]) for pp in pr]
            dvs = [_dot(ddt[pp], avs[pp]) for pp in pr]
            for pp in pr:
                dq_acc[pp] += dqs[pp]
                dkt[pp, kb] += dks[pp]
                dvt[pp, kb] += dvs[pp]

        def k_block(kb, _):
            tile(kb, False)
            return 0

        col_max = jnp.max(jnp.max(c_ref[...], axis=0), axis=0, keepdims=True)
        lane_row = lax.broadcasted_iota(jnp.int32, (1, 128), 1)
        n_live = jnp.sum(jnp.where(jnp.logical_and(col_max > SB_DEAD, lane_row < i), 1, 0))
        lax.fori_loop(i - n_live, i, k_block, 0)
        tile(i, True)
        for pp in pr:
            dq_ref[:, cols[pp]] = jnp.where(head0, dq_acc[pp, :, 0:t].T, dq_acc[pp, :, t:2 * t].T) * ATT_SCALE

        @pl.when(i == nq - 1)
        def _():
            def untranspose(j, _):
                rows = pl.ds(pl.multiple_of(j * t, t), t)
                for pp in pr:
                    dk_ref[rows, cols[pp]] = dkt[pp, j].T
                    dv_ref[rows, cols[pp]] = dvt[pp, j].T
                return 0

            lax.fori_loop(0, nq, untranspose, 0)

    width = 128 * npair
    kv = pl.BlockSpec((s_len, width), lambda p, i: (0, p))
    qo = pl.BlockSpec((t, width), lambda p, i: (i, p))
    tri = pl.BlockSpec((t, t), lambda p, i: (0, 0))
    out = jax.ShapeDtypeStruct((s_len, D_GRP), F32)
    steps = 4 // npair
    return _call_with_exchange(
        body, rides,
        lambda: jnp.logical_and(pl.program_id(0) == 0, pl.program_id(1) == 0),
        lambda: jnp.logical_and(pl.program_id(0) == steps - 1, pl.program_id(1) == nq - 1),
        name="sb_bwd", grid=(steps, nq),
        in_specs=[qo, kv, kv, qo, pl.BlockSpec((2 * npair, t, 128), lambda p, i: (p, i, 0)), tri, tri],
        out_specs=(qo, kv, kv), out_shape=(out, out, out),
        scratch_shapes=[pltpu.VMEM((npair, 2 * t, 128), BF16), pltpu.VMEM((npair, 2 * t, 128), BF16),
                        pltpu.VMEM((npair, 128, 2 * t), BF16), pltpu.VMEM((npair, 128, 2 * t), BF16),
                        pltpu.VMEM((npair, nq, 128, t), BF16),
                        pltpu.VMEM((npair, 128, 2 * t), F32),
                        pltpu.VMEM((npair, nq, 128, t), F32), pltpu.VMEM((npair, nq, 128, t), F32),
                        pltpu.VMEM((npair, 2 * t, 128), F32)],
    )(qs, ks, vs, dos, csaved, tri_suf, tri_pre, *rides)


def _attn_out(o_b, lse_b, o_sb, x2, wdil, wsb, out_g, shards):
    s_len = x2.shape[0]
    tm = ROW_TILE

    def body(o1_ref, l1_ref, o4_ref, l4_ref, o16_ref, l16_ref, osb_ref, x_ref, wdil_ref, wsb_ref, w_ref,
             odil_ref, lse_ref, lse4_ref, lse16_ref, mixed_ref, x1_ref, stage, nat4, nat16):
        _merge_views((o4_ref, l4_ref), (o16_ref, l16_ref), stage, nat4, nat16)
        os_ = (o1_ref[...], _slab_group(nat4, 0), _slab_group(nat16, 0))
        ls = (l1_ref[...], _slab_group(nat4, 1), _slab_group(nat16, 1))
        mx = jnp.maximum(jnp.maximum(ls[0], ls[1]), ls[2])
        es = [jnp.exp(l - mx) for l in ls]
        den = es[0] + es[1] + es[2]
        o_dil = (es[0] * os_[0] + es[1] * os_[1] + es[2] * os_[2]) / den
        odil_ref[...] = o_dil
        lse = mx + jnp.log(den)
        lse_ref[...] = lse
        for p in range(4):
            nat4[p] = lse[:, 128 * p:128 * (p + 1)]
        _split_views(nat4.at[0:4], stage.at[0:4], (lse4_ref,), (lse16_ref,))
        halves = []
        for t, w_r in ((o_dil, wdil_ref), (osb_ref[...], wsb_ref)):
            r = lax.rsqrt(jnp.mean(t * t, axis=-1, keepdims=True) + EPS)
            halves.append(((t * r) * w_r[...]).astype(BF16))
        mixed = jnp.concatenate(halves, axis=1)
        mixed_ref[...] = mixed
        w = w_ref[...].reshape(D_MODEL, D_MODEL)
        x1_ref[...] = x_ref[...] + _dot(mixed, w)

    row = lambda w: pl.BlockSpec((tm, w), lambda i: (i, 0))
    ni = s_len // tm
    return _call_with_gather(
        body, shards, lambda: pl.program_id(0) == 0, lambda: pl.program_id(0) == ni - 2,
        lambda: pl.program_id(0) == ni - 1,
        name="attn_out", grid=(ni,),
        in_specs=[row(D_GRP)] * 2 + [_view_spec(tm, 4)] * 2 + [_view_spec(tm, 16)] * 2
        + [row(D_GRP), row(D_MODEL), _full((1, D_GRP)), _full((1, D_GRP)), _full((N_DEV, OUT_SHARD, D_MODEL))],
        out_specs=(row(D_GRP), row(D_GRP), _view_spec(tm, 4), _view_spec(tm, 16), row(D_MODEL), row(D_MODEL)),
        out_shape=(jax.ShapeDtypeStruct((s_len, D_GRP), F32), jax.ShapeDtypeStruct((s_len, D_GRP), F32),
                   _view_shape(s_len, 4, F32), _view_shape(s_len, 16, F32),
                   jax.ShapeDtypeStruct((s_len, D_MODEL), BF16), jax.ShapeDtypeStruct((s_len, D_MODEL), F32)),
        scratch_shapes=[pltpu.VMEM((8, tm, 128), F32)] * 3,
    )(o_b[0], lse_b[0], o_b[1], lse_b[1], o_b[2], lse_b[2], o_sb, x2, wdil, wsb, out_g, *shards)


def _two_shards(w_ref):
    return w_ref[...].reshape(FF_BLOCK, D_MODEL)


def _ffn_fwd(x1, wn2, tgt, gate_g, up_g, down_g):
    s_len = x1.shape[0]
    tm = ROW_TILE
    ni = s_len // tm

    def body(x_ref, wn_ref, t_ref, wg_ref, wu_ref, wd_ref, g_ref, u_ref, h2_ref, dy_ref, loss_ref, acc):
        j = pl.program_id(1)

        @pl.when(j == 0)
        def _():
            xx = x_ref[...]
            r = lax.rsqrt(jnp.mean(xx * xx, axis=-1, keepdims=True) + EPS)
            h2_ref[...] = ((xx * r) * wn_ref[...]).astype(BF16)
            acc[...] = jnp.zeros_like(acc)

        h = h2_ref[...]
        g = _dot_nt(h, _two_shards(wg_ref))
        u = _dot_nt(h, _two_shards(wu_ref))
        g_ref[...] = g
        u_ref[...] = u
        act = (g * (1.0 / (1.0 + jnp.exp(-g)))) * u
        acc[...] += _dot(act.astype(BF16), _two_shards(wd_ref))

        @pl.when(j == FF_STEPS - 1)
        def _():
            err = (x_ref[...] + acc[...]) - t_ref[...]
            dy_ref[...] = err * (1.0 / D_MODEL)
            part = 0.5 * jnp.sum(jnp.mean(err * err, axis=-1, keepdims=True))
            loss_ref[...] = jnp.full((8, 128), part, F32)

    row = pl.BlockSpec((tm, D_MODEL), lambda i, j: (i, 0))
    hid = pl.BlockSpec((tm, FF_BLOCK), lambda i, j: (i, j))
    return pl.pallas_call(
        body, name="ffn_fwd", grid=(ni, FF_STEPS),
        in_specs=[row, pl.BlockSpec((1, D_MODEL), lambda i, j: (0, 0)), row,
                  pl.BlockSpec((2, FF_PAD, D_MODEL), lambda i, j: (j, 0, 0)),
                  pl.BlockSpec((2, FF_PAD, D_MODEL), lambda i, j: (j, 0, 0)),
                  pl.BlockSpec((2, FF_PAD, D_MODEL), lambda i, j: (j, 0, 0))],
        out_specs=(hid, hid, row, row, pl.BlockSpec((8, 128), lambda i, j: (i, 0))),
        out_shape=(jax.ShapeDtypeStruct((s_len, N_DEV * FF_PAD), F32),
                   jax.ShapeDtypeStruct((s_len, N_DEV * FF_PAD), F32),
                   jax.ShapeDtypeStruct((s_len, D_MODEL), BF16),
                   jax.ShapeDtypeStruct((s_len, D_MODEL), F32),
                   jax.ShapeDtypeStruct((ni * 8, 128), F32)),
        scratch_shapes=[pltpu.VMEM((tm, D_MODEL), F32)],
        compiler_params=_params(),
    )(x1, wn2, tgt, gate_g, up_g, down_g)


def _ffn_bwd_dx(dy, g, u, gate_g, up_g, down_g):
    s_len = dy.shape[0]
    tm = ROW_TILE

    def body(dy_ref, g_ref, u_ref, wg_ref, wu_ref, wd_ref, dg_ref, du_ref, act_ref, dh_ref, acc):
        j = pl.program_id(1)

        @pl.when(j == 0)
        def _():
            acc[...] = jnp.zeros_like(acc)

        gg, uu = g_ref[...], u_ref[...]
        da = _dot_nt(dy_ref[...].astype(BF16), _two_shards(wd_ref))
        sig = 1.0 / (1.0 + jnp.exp(-gg))
        silu = gg * sig
        act_ref[...] = (silu * uu).astype(BF16)
        du = (da * silu).astype(BF16)
        dg = (da * uu * (sig * (1.0 + gg * (1.0 - sig)))).astype(BF16)
        du_ref[...] = du
        dg_ref[...] = dg
        acc[...] += _dot(dg, _two_shards(wg_ref)) + _dot(du, _two_shards(wu_ref))

        @pl.when(j == FF_STEPS - 1)
        def _():
            dh_ref[...] = acc[...]

    row = pl.BlockSpec((tm, D_MODEL), lambda i, j: (i, 0))
    hid = pl.BlockSpec((tm, FF_BLOCK), lambda i, j: (i, j))
    hid_bf = jax.ShapeDtypeStruct((s_len, N_DEV * FF_PAD), BF16)
    return pl.pallas_call(
        body, name="ffn_bwd_dx", grid=(s_len // tm, FF_STEPS),
        in_specs=[row, hid, hid,
                  pl.BlockSpec((2, FF_PAD, D_MODEL), lambda i, j: (j, 0, 0)),
                  pl.BlockSpec((2, FF_PAD, D_MODEL), lambda i, j: (j, 0, 0)),
                  pl.BlockSpec((2, FF_PAD, D_MODEL), lambda i, j: (j, 0, 0))],
        out_specs=(hid, hid, hid, row),
        out_shape=(hid_bf, hid_bf, hid_bf, jax.ShapeDtypeStruct((s_len, D_MODEL), F32)),
        scratch_shapes=[pltpu.VMEM((tm, D_MODEL), F32)],
        compiler_params=_params(),
    )(dy, g, u, gate_g, up_g, down_g)


def _ffn_bwd_dw(h2, dy, dg, du, act):
    s_len = h2.shape[0]
    tm = DW_ROW_TILE
    ni = s_len // tm

    def body(h_ref, dy_ref, dg_ref, du_ref, act_ref, dwg_ref, dwu_ref, dwd_ref, ag, au, ad):
        i = pl.program_id(1)

        @pl.when(i == 0)
        def _():
            ag[...] = jnp.zeros_like(ag)
            au[...] = jnp.zeros_like(au)
            ad[...] = jnp.zeros_like(ad)

        h = h_ref[...]
        ag[...] += _dot_tn(h, dg_ref[...])
        au[...] += _dot_tn(h, du_ref[...])
        ad[...] += _dot_tn(act_ref[...], dy_ref[...].astype(BF16))

        @pl.when(i == ni - 1)
        def _():
            for half in range(2):
                cols = slice(FF_PAD * half, FF_PAD * (half + 1))
                dwg_ref[half] = ag[:, cols].astype(BF16)
                dwu_ref[half] = au[:, cols].astype(BF16)
            dwd_ref[...] = ad[...].astype(BF16).reshape(2, FF_PAD, D_MODEL)

    row = pl.BlockSpec((tm, D_MODEL), lambda j, i: (i, 0))
    hid = pl.BlockSpec((tm, FF_BLOCK), lambda j, i: (i, j))
    col_w = pl.BlockSpec((2, D_MODEL, FF_PAD), lambda j, i: (j, 0, 0))
    row_w = pl.BlockSpec((2, FF_PAD, D_MODEL), lambda j, i: (j, 0, 0))
    return pl.pallas_call(
        body, name="ffn_bwd_dw", grid=(FF_STEPS, ni),
        in_specs=[row, row, hid, hid, hid], out_specs=(col_w, col_w, row_w),
        out_shape=(jax.ShapeDtypeStruct((N_DEV, D_MODEL, FF_PAD), BF16),
                   jax.ShapeDtypeStruct((N_DEV, D_MODEL, FF_PAD), BF16),
                   jax.ShapeDtypeStruct((N_DEV, FF_PAD, D_MODEL), BF16)),
        scratch_shapes=[pltpu.VMEM((D_MODEL, FF_BLOCK), F32), pltpu.VMEM((D_MODEL, FF_BLOCK), F32),
                        pltpu.VMEM((FF_BLOCK, D_MODEL), F32)],
        compiler_params=_params(),
    )(h2, dy, dg, du, act)


def _rms_bwd(dy, t, w):
    r = lax.rsqrt(jnp.mean(t * t, axis=-1, keepdims=True) + EPS)
    gw = dy * w
    dt = r * (gw - t * ((r * r) * jnp.mean(gw * t, axis=-1, keepdims=True)))
    return dt, dy * t * r


def _attn_out_bwd(dy, dh2, x1, wn2, b_g, mixed, o_dil, o_sb, wdil, wsb, bd512):
    s_len = dy.shape[0]
    tm = ROW_TILE
    ni = s_len // tm

    def body(dy_ref, dh_ref, x1_ref, wn_ref, w_ref, mixed_ref, odil_ref, osb_ref, wdil_ref, wsb_ref, bd_ref,
             dx1_ref, dodil_ref, delta_ref, dosb_ref, dwout_ref, dwn_ref, dwdil_ref, dwsb_ref,
             do4_ref, dl4_ref, do16_ref, dl16_ref, wacc, both, stage):
        i = pl.program_id(0)

        @pl.when(i == 0)
        def _():
            wacc[...] = jnp.zeros_like(wacc)
            dwn_ref[...] = jnp.zeros_like(dwn_ref)
            dwdil_ref[...] = jnp.zeros_like(dwdil_ref)
            dwsb_ref[...] = jnp.zeros_like(dwsb_ref)

        dnorm, dw_rows = _rms_bwd(dh_ref[...], x1_ref[...], wn_ref[...])
        dx1 = dy_ref[...] + dnorm
        dx1_ref[...] = dx1
        dwn_ref[...] += jnp.sum(dw_rows, axis=0, keepdims=True)
        dx1b = dx1.astype(BF16)
        w = w_ref[...].reshape(D_MODEL, D_MODEL)
        dmixed = _dot_nt(dx1b, w)
        wacc[...] += _dot_tn(mixed_ref[...], dx1b)
        o_dil = odil_ref[...]
        d_odil, dw_rows = _rms_bwd(dmixed[:, :D_GRP], o_dil, wdil_ref[...])
        dwdil_ref[...] += jnp.sum(dw_rows, axis=0, keepdims=True)
        dodil_ref[...] = d_odil.astype(BF16)
        delta = _mm_split(d_odil * o_dil, bd_ref[...])
        delta_ref[...] = delta
        for p in range(4):
            both[p] = d_odil[:, 128 * p:128 * (p + 1)]
            both[4 + p] = delta[:, 128 * p:128 * (p + 1)]
        _split_views(both, stage, (do4_ref, dl4_ref), (do16_ref, dl16_ref))
        d_osb, dw_rows = _rms_bwd(dmixed[:, D_GRP:], osb_ref[...], wsb_ref[...])
        dwsb_ref[...] += jnp.sum(dw_rows, axis=0, keepdims=True)
        dosb_ref[...] = d_osb.astype(BF16)

        @pl.when(i == ni - 1)
        def _():
            dwout_ref[...] = wacc[...].astype(BF16).reshape(N_DEV, OUT_SHARD, D_MODEL)

    row = lambda w: pl.BlockSpec((tm, w), lambda i: (i, 0))
    return pl.pallas_call(
        body, name="attn_out_bwd", grid=(ni,),
        in_specs=[row(D_MODEL), row(D_MODEL), row(D_MODEL), _full((1, D_MODEL)),
                  _full((N_DEV, OUT_SHARD, D_MODEL)),
                  row(D_MODEL), row(D_GRP), row(D_GRP), _full((1, D_GRP)), _full((1, D_GRP)),
                  _full((D_GRP, D_GRP))],
        out_specs=(row(D_MODEL), row(D_GRP), row(D_GRP), row(D_GRP),
                   _full((N_DEV, OUT_SHARD, D_MODEL)), _full((1, D_MODEL)), _full((1, D_GRP)), _full((1, D_GRP)),
                   _view_spec(tm, 4), _view_spec(tm, 4), _view_spec(tm, 16), _view_spec(tm, 16)),
        out_shape=(jax.ShapeDtypeStruct((s_len, D_MODEL), F32), jax.ShapeDtypeStruct((s_len, D_GRP), BF16),
                   jax.ShapeDtypeStruct((s_len, D_GRP), F32), jax.ShapeDtypeStruct((s_len, D_GRP), BF16),
                   jax.ShapeDtypeStruct((N_DEV, OUT_SHARD, D_MODEL), BF16),
                   jax.ShapeDtypeStruct((1, D_MODEL), F32), jax.ShapeDtypeStruct((1, D_GRP), F32),
                   jax.ShapeDtypeStruct((1, D_GRP), F32),
                   _view_shape(s_len, 4, BF16), _view_shape(s_len, 4, F32),
                   _view_shape(s_len, 16, BF16), _view_shape(s_len, 16, F32)),
        scratch_shapes=[pltpu.VMEM((D_MODEL, D_MODEL), F32), pltpu.VMEM((8, tm, 128), F32),
                        pltpu.VMEM((8, tm, 128), F32)],
        compiler_params=_params(),
    )(dy, dh2, x1, wn2, b_g, mixed, o_dil, o_sb, wdil, wsb, bd512)


def _qkv_bwd(dq_b, dk_b, dv_b, dqs, dks, dvs, qraw, kraw, cos2, sin2, qnw, knw, bd):
    s_len = qraw.shape[0]
    tm = ROW_TILE
    ni = s_len // tm

    def body(dq1, dk1, dv1, dq4, dk4, dv4, dq16, dk16, dv16, dqs_ref, dks_ref, dvs_ref,
             qraw_ref, kraw_ref, cos_ref, sin_ref, qnw_ref, knw_ref, bd_ref,
             dproj_ref, dqn_ref, dkn_ref, stage, nat4, nat16):
        i = pl.program_id(0)

        @pl.when(i == 0)
        def _():
            dqn_ref[...] = jnp.zeros_like(dqn_ref)
            dkn_ref[...] = jnp.zeros_like(dkn_ref)

        _merge_views((dq4, dk4, dv4), (dq16, dk16, dv16), stage, nat4, nat16)
        cos_t, sin_t, bdm = cos_ref[...], sin_ref[...], bd_ref[...]
        for grp, (part1, raw_ref, nw_ref, dn_ref) in enumerate(((dq1, qraw_ref, qnw_ref, dqn_ref),
                                                                (dk1, kraw_ref, knw_ref, dkn_ref))):
            dn_acc = 0.0
            for p in range(4):
                cols = slice(128 * p, 128 * (p + 1))
                d_rope = part1[:, cols] + nat4[4 * grp + p] + nat16[4 * grp + p]
                d_norm = d_rope * cos_t + _swap_halves(d_rope * sin_t)
                t = raw_ref[:, cols]
                w = nw_ref[...]
                r = lax.rsqrt(_mm_split(t * t, bdm) * (1.0 / HEAD_DIM) + EPS)
                gw = d_norm * w
                corr = _mm_split(gw * t, bdm) * (1.0 / HEAD_DIM)
                dt = r * (gw - t * ((r * r) * corr))
                dn_acc = dn_acc + jnp.sum(d_norm * t * r, axis=0, keepdims=True)
                dproj_ref[:, D_GRP * grp + 128 * p:D_GRP * grp + 128 * (p + 1)] = dt.astype(BF16)
            dn_ref[...] += dn_acc
        dproj_ref[:, 2 * D_GRP:3 * D_GRP] = (dv1[...] + _slab_group(nat4, 2) + _slab_group(nat16, 2)).astype(BF16)
        dproj_ref[:, 3 * D_GRP:4 * D_GRP] = dqs_ref[...].astype(BF16)
        dproj_ref[:, 4 * D_GRP:5 * D_GRP] = dks_ref[...].astype(BF16)
        dproj_ref[:, 5 * D_GRP:6 * D_GRP] = dvs_ref[...].astype(BF16)

    row = lambda w: pl.BlockSpec((tm, w), lambda i: (i, 0))
    return pl.pallas_call(
        body, name="qkv_bwd", grid=(ni,),
        in_specs=[row(D_GRP)] * 3 + [_view_spec(tm, 4)] * 3 + [_view_spec(tm, 16)] * 3 + [row(D_GRP)] * 5
        + [row(128), row(128), _full((1, 128)), _full((1, 128)), _full((128, 128))],
        out_specs=(row(D_IN), _full((1, 128)), _full((1, 128))),
        out_shape=(jax.ShapeDtypeStruct((s_len, D_IN), BF16), jax.ShapeDtypeStruct((1, 128), F32),
                   jax.ShapeDtypeStruct((1, 128), F32)),
        scratch_shapes=[pltpu.VMEM((12, tm, 128), F32)] * 3,
        compiler_params=_params(),
    )(dq_b[0], dk_b[0], dv_b[0], dq_b[1], dk_b[1], dv_b[1], dq_b[2], dk_b[2], dv_b[2],
      dqs, dks, dvs, qraw, kraw, cos2, sin2, qnw, knw, bd)


def _in_bwd_dx(dproj, a_g, x2, dx1, wn1, rides):
    s_len = x2.shape[0]
    tm = ROW_TILE
    ni = s_len // tm

    def body(dp_ref, w_ref, x_ref, dx1_ref, wn_ref, gx_ref, dwn_ref, w_full):
        i = pl.program_id(0)

        @pl.when(i == 0)
        def _():
            dwn_ref[...] = jnp.zeros_like(dwn_ref)
            for d in range(N_DEV):
                w_full[:, IN_SHARD * d:IN_SHARD * (d + 1)] = w_ref[d]

        dh = _dot_nt(dp_ref[...], w_full[...])
        dnorm, dw_rows = _rms_bwd(dh, x_ref[...], wn_ref[...])
        gx_ref[...] = dx1_ref[...] + dnorm
        dwn_ref[...] += jnp.sum(dw_rows, axis=0, keepdims=True)

    row = lambda w: pl.BlockSpec((tm, w), lambda i: (i, 0))
    return _call_with_exchange(
        body, rides, lambda: pl.program_id(0) == 0, lambda: pl.program_id(0) == ni - 1,
        name="in_bwd_dx", grid=(ni,),
        in_specs=[row(D_IN), pl.BlockSpec((N_DEV, D_MODEL, IN_SHARD), lambda i: (0, 0, 0)),
                  row(D_MODEL), row(D_MODEL), _full((1, D_MODEL))],
        out_specs=(row(D_MODEL), _full((1, D_MODEL))),
        out_shape=(jax.ShapeDtypeStruct((s_len, D_MODEL), F32), jax.ShapeDtypeStruct((1, D_MODEL), F32)),
        scratch_shapes=[pltpu.VMEM((D_MODEL, D_IN), BF16)],
    )(dproj, a_g, x2, dx1, wn1, *rides)


def _in_bwd_dw(h1, dproj):
    s_len = h1.shape[0]
    tm = DW_ROW_TILE
    ni = s_len // tm

    def body(h_ref, dp_ref, dw_ref, acc):
        i = pl.program_id(1)

        @pl.when(i == 0)
        def _():
            acc[...] = jnp.zeros_like(acc)

        acc[...] += _dot_tn(h_ref[...], dp_ref[...])

        @pl.when(i == ni - 1)
        def _():
            for half in range(2):
                dw_ref[half] = acc[:, IN_SHARD * half:IN_SHARD * (half + 1)].astype(BF16)

    return pl.pallas_call(
        body, name="in_bwd_dw", grid=(N_DEV // 2, ni),
        in_specs=[pl.BlockSpec((tm, D_MODEL), lambda d, i: (i, 0)),
                  pl.BlockSpec((tm, 2 * IN_SHARD), lambda d, i: (i, d))],
        out_specs=pl.BlockSpec((2, D_MODEL, IN_SHARD), lambda d, i: (d, 0, 0)),
        out_shape=jax.ShapeDtypeStruct((N_DEV, D_MODEL, IN_SHARD), BF16),
        scratch_shapes=[pltpu.VMEM((D_MODEL, 2 * IN_SHARD), F32)],
        compiler_params=_params(),
    )(h1, dproj)


def _adamw(recv, w, m, v):
    rows, cols = w.shape
    tr = next((t for t in (128, 32) if rows % t == 0), rows)

    def body(p_ref, w_ref, m_ref, v_ref, g_ref, d_ref, nm_ref, nv_ref):
        g = p_ref[0].astype(F32)
        for s in range(1, N_DEV):
            g = g + p_ref[s].astype(F32)
        m_new = ADAM_B1 * m_ref[...] + (1.0 - ADAM_B1) * g
        v_new = ADAM_B2 * v_ref[...] + (1.0 - ADAM_B2) * (g * g)
        m_hat = m_new / (1.0 - ADAM_B1 ** ADAM_STEP)
        v_hat = v_new / (1.0 - ADAM_B2 ** ADAM_STEP)
        g_ref[...] = g
        d_ref[...] = -ADAM_LR * (m_hat / (jnp.sqrt(v_hat) + ADAM_EPS) + ADAM_WD * w_ref[...])
        nm_ref[...] = m_new
        nv_ref[...] = v_new

    blk = pl.BlockSpec((tr, cols), lambda i: (i, 0))
    out = jax.ShapeDtypeStruct((rows, cols), F32)
    return pl.pallas_call(
        body, name=f"adamw_{rows}x{cols}", grid=(rows // tr,),
        in_specs=[pl.BlockSpec((N_DEV, tr, cols), lambda i: (0, i, 0)), blk, blk, blk],
        out_specs=(blk,) * 4, out_shape=(out,) * 4,
        compiler_params=_params(),
    )(recv, w, m, v)


def _rope_tables(s_len):
    pos = jnp.arange(s_len, dtype=F32)
    inv_freq = ROPE_THETA ** (-jnp.arange(0, HEAD_DIM, 2, dtype=F32) / HEAD_DIM)
    ang = pos[:, None] * inv_freq[None, :]
    cos, sin = jnp.cos(ang), jnp.sin(ang)
    cos2 = jnp.concatenate([cos, cos, cos, cos], axis=1)
    sin2 = jnp.concatenate([-sin, sin, -sin, sin], axis=1)
    return cos2, sin2


def _block_diag_ones(n):
    i = jnp.arange(n)
    return (i[:, None] // HEAD_DIM == i[None, :] // HEAD_DIM).astype(BF16)


def _pad_cols(t):
    return jnp.pad(t, ((0, 0), (0, FF_PAD - FF_SHARD)))


def _pad_rows(t):
    return jnp.pad(t, ((0, FF_PAD - FF_SHARD), (0, 0)))


LOSS_ROW = 26


def _pack_small(n1, n2, ndil, nsb, nq, nk, scalar=None):
    pad = lambda t: jnp.pad(t.reshape(1, -1), ((0, 0), (0, 128 - t.size)))
    last = jnp.zeros((1, 128), F32) if scalar is None else pad(scalar)
    rows = [n1.reshape(8, 128), n2.reshape(8, 128), ndil.reshape(4, 128), nsb.reshape(4, 128),
            pad(nq), pad(nk), last, jnp.zeros((5, 128), F32)]
    return jnp.concatenate(rows, axis=0)


def _unpack_small(t):
    return (t[0:8].reshape(1, D_MODEL), t[8:16].reshape(1, D_MODEL), t[16:20].reshape(1, D_GRP),
            t[20:24].reshape(1, D_GRP), t[24:25, :HEAD_DIM], t[25:26, :HEAD_DIM])


def kernel(x, attn_norm_w, w_in, q_norm_w, k_norm_w, dil_out_norm_w, sb_out_norm_w, w_out, ffn_norm_w, w_gate, w_up, w_down, loss_target, m_attn_norm_w, m_w_in, m_q_norm_w, m_k_norm_w, m_dil_out_norm_w, m_sb_out_norm_w, m_w_out, m_ffn_norm_w, m_w_gate, m_w_up, m_w_down, v_attn_norm_w, v_w_in, v_q_norm_w, v_k_norm_w, v_dil_out_norm_w, v_sb_out_norm_w, v_w_out, v_ffn_norm_w, v_w_gate, v_w_up, v_w_down):
    s_len = x.shape[1]
    x2, tgt = x[0], loss_target[0]

    (a_g,) = _gather_weights([w_in[0].astype(BF16)])
    gate_loc = _pad_cols(w_gate[0]).T.astype(BF16)
    up_loc = _pad_cols(w_up[0]).T.astype(BF16)
    down_loc = _pad_rows(w_down[0]).astype(BF16)
    out_loc = w_out[0].astype(BF16)

    cos2, sin2 = _rope_tables(s_len)
    bd128, bd512 = _block_diag_ones(128), _block_diag_ones(D_GRP)
    idx = jnp.arange(SB_TILE)
    tri_suf = (idx[:, None] > idx[None, :]).astype(BF16)
    tri_pre = (idx[:, None] < idx[None, :]).astype(BF16)
    qnw2 = jnp.concatenate([q_norm_w, q_norm_w], axis=1)
    knw2 = jnp.concatenate([k_norm_w, k_norm_w], axis=1)

    (h1, qraw, kraw, q, k, va, qs, ks, vs, q4, k4, v4, q16, k16, v16,
     out_g, gate_g) = _attn_in(x2, attn_norm_w, a_g, cos2, sin2, qnw2, knw2, bd128, shards=[out_loc, gate_loc])
    qkv_views = {1: (q, k, va), 4: (q4, k4, v4), 16: (q16, k16, v16)}
    o_b, lse_b = [], []
    for r in DILATIONS:
        o, lse = _dil_fwd(*qkv_views[r], r)
        o_b.append(o)
        lse_b.append(lse)
    o_sb, c_sb, up_g = _sb_fwd(qs, ks, vs, tri_suf, shards=[up_loc])
    o_dil, lse_tot, lse4, lse16, mixed, x1, down_g = _attn_out(
        o_b, lse_b, o_sb, x2, dil_out_norm_w, sb_out_norm_w, out_g, shards=[down_loc])
    g, u, h2, dy, loss_parts = _ffn_fwd(x1, ffn_norm_w, tgt, gate_g, up_g, down_g)
    loss_local = jnp.sum(loss_parts[::8, 0])

    dg, du, act, dh2 = _ffn_bwd_dx(dy, g, u, gate_g, up_g, down_g)
    (dx1, do_dil, delta, do_sb, dwout, dn2, dndil, dnsb, do4, dl4, do16, dl16) = _attn_out_bwd(
        dy, dh2, x1, ffn_norm_w, out_g, mixed, o_dil, o_sb, dil_out_norm_w, sb_out_norm_w, bd512)
    dwg, dwu, dwd = _ffn_bwd_dw(h2, dy, dg, du, act)
    dqs, dks, dvs, r_gate, r_down = _sb_bwd(qs, ks, vs, do_sb, c_sb, tri_suf, tri_pre, rides=[dwg, dwd])
    cot_views = {1: (do_dil, lse_tot, delta), 4: (do4, lse4, dl4), 16: (do16, lse16, dl16)}
    riders = {1: [], 4: [dwout], 16: [dwu]}
    dq_b, dk_b, dv_b, landed = [], [], [], {}
    for r in DILATIONS:
        dq, dk, dv, *landed[r] = _dil_bwd(*qkv_views[r], *cot_views[r], r, rides=riders[r])
        dq_b.append(dq)
        dk_b.append(dk)
        dv_b.append(dv)
    (r_out,), (r_up,) = landed[4], landed[16]
    dproj, dqn2, dkn2 = _qkv_bwd(dq_b, dk_b, dv_b, dqs, dks, dvs, qraw, kraw, cos2, sin2, qnw2, knw2, bd128)
    dwin = _in_bwd_dw(h1, dproj)
    grad_x, dn1, r_in = _in_bwd_dx(dproj, a_g, x2, dx1, attn_norm_w, rides=[dwin])
    dqn = dqn2[:, :HEAD_DIM] + dqn2[:, HEAD_DIM:]
    dkn = dkn2[:, :HEAD_DIM] + dkn2[:, HEAD_DIM:]

    small = _pack_small(dn1, dn2, dndil, dnsb, dqn, dkn, loss_local)
    (r_small,) = _exchange_grads([], small)
    big = {
        "w_in": _adamw(r_in, w_in[0], m_w_in[0], v_w_in[0]),
        "w_gate": tuple(t[:, :FF_SHARD] for t in _adamw(r_gate, _pad_cols(w_gate[0]), _pad_cols(m_w_gate[0]), _pad_cols(v_w_gate[0]))),
        "w_up": tuple(t[:, :FF_SHARD] for t in _adamw(r_up, _pad_cols(w_up[0]), _pad_cols(m_w_up[0]), _pad_cols(v_w_up[0]))),
        "w_down": _adamw(r_down, w_down[0], m_w_down[0], v_w_down[0]),
        "w_out": _adamw(r_out, w_out[0], m_w_out[0], v_w_out[0]),
    }
    packs = [_pack_small(*ts) for ts in (
        (attn_norm_w, ffn_norm_w, dil_out_norm_w, sb_out_norm_w, q_norm_w, k_norm_w),
        (m_attn_norm_w, m_ffn_norm_w, m_dil_out_norm_w, m_sb_out_norm_w, m_q_norm_w, m_k_norm_w),
        (v_attn_norm_w, v_ffn_norm_w, v_dil_out_norm_w, v_sb_out_norm_w, v_q_norm_w, v_k_norm_w))]
    small_raw = _adamw(r_small, *packs)
    loss = small_raw[0][LOSS_ROW, 0]
    small_out = [_unpack_small(t) for t in small_raw]
    names = ["attn_norm_w", "w_in", "q_norm_w", "k_norm_w", "dil_out_norm_w", "sb_out_norm_w", "w_out",
             "ffn_norm_w", "w_gate", "w_up", "w_down"]
    small_pos = {"attn_norm_w": 0, "ffn_norm_w": 1, "dil_out_norm_w": 2, "sb_out_norm_w": 3,
                 "q_norm_w": 4, "k_norm_w": 5}
    outs = [loss, grad_x[None]]
    for kind in range(4):
        for name in names:
            if name in small_pos:
                outs.append(small_out[kind][small_pos[name]])
            else:
                outs.append(big[name][kind][None])
    return tuple(outs)
```

```python
import jax
import jax.numpy as jnp
from jax import lax
from jax.experimental import pallas as pl
from jax.experimental.pallas import tpu as pltpu

F32 = jnp.float32
BF16 = jnp.bfloat16

N_DEV = 8
D_MODEL = 1024
HEAD_DIM = 64
D_GRP = 512
D_IN = 6 * D_GRP
IN_SHARD = D_IN // N_DEV
FF_SHARD = 352
FF_PAD = 384
FF_BLOCK = 2 * FF_PAD
FF_STEPS = N_DEV // 2
OUT_SHARD = D_MODEL // N_DEV
BLOCK = 128
DILATIONS = (1, 4, 16)
ROPE_THETA = 10000.0
EPS = 1e-6
ATT_SCALE = HEAD_DIM ** -0.5
NEG = -1e30

ADAM_LR = 0.001
ADAM_B1 = 0.9
ADAM_B2 = 0.999
ADAM_EPS = 1e-08
ADAM_WD = 0.01
ADAM_STEP = 10

SB_TILE = 256
SB_DEAD = -104.0
SB_PAIRS = 4
SB_BWD_PAIRS = 2
ROW_TILE = 512
DW_ROW_TILE = 1024
VMEM_LIMIT = 56 * 1024 * 1024
MESH = pl.DeviceIdType.MESH


def _dot(a, b):
    return jnp.dot(a, b, preferred_element_type=F32)


def _dot_nt(a, b):
    return lax.dot_general(a, b, (((1,), (1,)), ((), ())), preferred_element_type=F32)


def _dot_tn(a, b):
    return lax.dot_general(a, b, (((0,), (0,)), ((), ())), preferred_element_type=F32)


def _mm_split(t, m):
    hi = t.astype(BF16)
    lo = (t - hi.astype(F32)).astype(BF16)
    return _dot(hi, m) + _dot(lo, m)


def _params(**kw):
    return pltpu.CompilerParams(vmem_limit_bytes=VMEM_LIMIT, **kw)


def _full(shape):
    nd = len(shape)
    return pl.BlockSpec(shape, lambda *_: (0,) * nd)


def _view_shape(s_len, r, dtype):
    return jax.ShapeDtypeStruct((s_len // r, r * D_GRP), dtype)


def _view_spec(tm, r):
    return pl.BlockSpec((tm // r, r * D_GRP), lambda i: (i, 0))


def _swap_halves(t):
    lane = lax.broadcasted_iota(jnp.int32, t.shape, 1)
    first = (lane & 32) == 0
    return jnp.where(first, pltpu.roll(t, 96, 1), pltpu.roll(t, 32, 1))


def _log_sigmoid_pair(z):
    neg_abs = lax.bitcast_convert_type(lax.bitcast_convert_type(z, jnp.uint32) | jnp.uint32(0x80000000), F32)
    lb = jnp.minimum(z, 0.0) - jnp.log(1.0 + jnp.exp(neg_abs))
    return lb, lb - z


def _cumsum_mm(t, tri):
    return _dot(t.astype(BF16), tri)


def _split_views(src_ref, stage_ref, views4, views16):
    slabs, n, _ = src_ref.shape
    n4, n16 = n // 4, n // 16
    for j in range(slabs):
        g, lanes = j // 4, 128 * (j % 4)
        src, stage = src_ref.at[j], stage_ref.at[j]
        for c4 in range(4):
            blk = src[pl.ds(c4, n4, stride=4), :]
            stage[n4 * c4:n4 * (c4 + 1), :] = blk
            col = D_GRP * c4 + lanes
            views4[g][:, col:col + 128] = blk.astype(views4[g].dtype)
        for c4 in range(4):
            for c1 in range(4):
                blk = stage[pl.ds(n4 * c4 + c1, n16, stride=4), :]
                col = D_GRP * (4 * c1 + c4) + lanes
                views16[g][:, col:col + 128] = blk.astype(views16[g].dtype)


def _merge_views(views4, views16, stage_ref, dst4_ref, dst16_ref):
    slabs, n, _ = dst4_ref.shape
    n4, n16 = n // 4, n // 16
    for j in range(slabs):
        g, lanes = j // 4, 128 * (j % 4)
        dst4, dst16, stage = dst4_ref.at[j], dst16_ref.at[j], stage_ref.at[j]
        for c4 in range(4):
            col = D_GRP * c4 + lanes
            dst4[pl.ds(c4, n4, stride=4), :] = views4[g][:, col:col + 128].astype(F32)
            for c1 in range(4):
                col = D_GRP * (4 * c1 + c4) + lanes
                stage[pl.ds(n4 * c4 + c1, n16, stride=4), :] = views16[g][:, col:col + 128].astype(F32)
        for c4 in range(4):
            dst16[pl.ds(c4, n4, stride=4), :] = stage[n4 * c4:n4 * (c4 + 1), :]


def _slab_group(ref, g):
    return jnp.concatenate([ref[4 * g + p] for p in range(4)], axis=1)


def _mesh_pos():
    return lax.axis_index("x"), lax.axis_index("y"), lax.axis_index("c")


def _flat_index(p):
    return 4 * p[0] + 2 * p[1] + p[2]


def _gather_weights(shards):
    n_arr = len(shards)

    def body(*refs):
        srcs, outs = refs[:n_arr], refs[n_arr:2 * n_arr]
        send_sems, recv_sems, local_sems = refs[2 * n_arr:]
        x, y, c = _mesh_pos()
        me, sibling = (x, y, c), (x, y, 1 - c)
        chips = [(1 - x, y), (x, 1 - y), (1 - x, 1 - y)]

        def copy(arr, k, block, to, own=False):
            dst = outs[arr].at[_flat_index(block)]
            return pltpu.make_async_remote_copy(
                src_ref=srcs[arr] if own else dst, dst_ref=dst,
                send_sem=send_sems.at[arr, k], recv_sem=recv_sems.at[arr, k],
                device_id=to, device_id_type=MESH)

        for arr in range(n_arr):
            mine = pltpu.make_async_copy(srcs[arr], outs[arr].at[_flat_index(me)], local_sems.at[arr])
            mine.start()
            first = [copy(arr, 0, me, sibling, own=True)]
            first += [copy(arr, 1 + j, me, (*chip, c), own=True) for j, chip in enumerate(chips)]
            for cp in first:
                cp.start()
        for arr in range(n_arr):
            passed = [copy(arr, 4 + j, (*chip, c), sibling) for j, chip in enumerate(chips)]
            for j, chip in enumerate(chips):
                copy(arr, 1 + j, (*chip, c), me).wait_recv()
                passed[j].start()
        for arr in range(n_arr):
            copy(arr, 0, sibling, me).wait_recv()
            for j, chip in enumerate(chips):
                copy(arr, 4 + j, (*chip, 1 - c), me).wait_recv()
            for k in range(7):
                copy(arr, k, me, me).wait_send()
            pltpu.make_async_copy(srcs[arr], outs[arr].at[_flat_index(me)], local_sems.at[arr]).wait()

    any_spec = pl.BlockSpec(memory_space=pl.ANY)
    return pl.pallas_call(
        body, name="gather_weights",
        out_shape=tuple(jax.ShapeDtypeStruct((N_DEV,) + s.shape, s.dtype) for s in shards),
        in_specs=[any_spec] * n_arr, out_specs=(any_spec,) * n_arr,
        scratch_shapes=[pltpu.SemaphoreType.DMA((n_arr, 7)), pltpu.SemaphoreType.DMA((n_arr, 7)),
                        pltpu.SemaphoreType.DMA((n_arr,))],
        compiler_params=pltpu.CompilerParams(has_side_effects=True),
    )(*shards)


def _peer_list(x, y, c):
    return [(1 - x if m & 4 else x, 1 - y if m & 2 else y, 1 - c if m & 1 else c) for m in range(1, N_DEV)]


def _exchange_grads(parts, small):
    n_arr = len(parts)

    def body(*refs):
        ins, outs = refs[:n_arr + 1], refs[n_arr + 1:2 * (n_arr + 1)]
        send_sems, recv_sems, local_sems = refs[2 * (n_arr + 1):]
        x, y, c = _mesh_pos()
        me = (x, y, c)
        my_idx = _flat_index(me)
        peers = []
        for m in range(1, N_DEV):
            peers.append((1 - x if m & 4 else x, 1 - y if m & 2 else y, 1 - c if m & 1 else c))

        def src_block(arr, dev):
            return ins[arr] if arr == n_arr else ins[arr].at[_flat_index(dev)]

        def copy(arr, k):
            return pltpu.make_async_remote_copy(
                src_ref=src_block(arr, peers[k]), dst_ref=outs[arr].at[my_idx],
                send_sem=send_sems.at[arr, k], recv_sem=recv_sems.at[arr, k],
                device_id=peers[k], device_id_type=MESH)

        def local(arr):
            return pltpu.make_async_copy(src_block(arr, me), outs[arr].at[my_idx], local_sems.at[arr])

        for arr in range(n_arr + 1):
            local(arr).start()
            for k in range(N_DEV - 1):
                copy(arr, k).start()
        for arr in range(n_arr + 1):
            for k in range(N_DEV - 1):
                cp = copy(arr, k)
                cp.wait_send()
                cp.wait_recv()
            local(arr).wait()

    any_spec = pl.BlockSpec(memory_space=pl.ANY)
    out_shape = tuple(jax.ShapeDtypeStruct(p.shape, p.dtype) for p in parts)
    out_shape += (jax.ShapeDtypeStruct((N_DEV,) + small.shape, small.dtype),)
    return pl.pallas_call(
        body, name="exchange_grads",
        out_shape=out_shape,
        in_specs=[any_spec] * (n_arr + 1), out_specs=(any_spec,) * (n_arr + 1),
        scratch_shapes=[pltpu.SemaphoreType.DMA((n_arr + 1, N_DEV - 1)),
                        pltpu.SemaphoreType.DMA((n_arr + 1, N_DEV - 1)),
                        pltpu.SemaphoreType.DMA((n_arr + 1,))],
        compiler_params=pltpu.CompilerParams(has_side_effects=True),
    )(*parts, small)


def _call_with_gather(body, shards, first_step, mid_step, last_step, *, name, grid, in_specs, out_specs,
                      out_shape, scratch_shapes=()):
    out_specs = tuple(out_specs) if isinstance(out_specs, (tuple, list)) else (out_specs,)
    out_shape = tuple(out_shape) if isinstance(out_shape, (tuple, list)) else (out_shape,)
    n_in, n_out, n_scr, n = len(in_specs), len(out_specs), len(scratch_shapes), len(shards)

    def full_body(*refs):
        ins, srcs = refs[:n_in], refs[n_in:n_in + n]
        outs, lands = refs[n_in + n:n_in + n + n_out], refs[n_in + n + n_out:n_in + 2 * n + n_out]
        scratch = refs[n_in + 2 * n + n_out:n_in + 2 * n + n_out + n_scr]
        send_sems, recv_sems, local_sems = refs[-3:]
        x, y, c = _mesh_pos()
        me, sibling = (x, y, c), (x, y, 1 - c)
        chips = [(1 - x, y), (x, 1 - y), (1 - x, 1 - y)]

        def copy(a, k, block, to, own=False):
            dst = lands[a].at[_flat_index(block)]
            return pltpu.make_async_remote_copy(
                src_ref=srcs[a] if own else dst, dst_ref=dst,
                send_sem=send_sems.at[a, k], recv_sem=recv_sems.at[a, k],
                device_id=to, device_id_type=MESH)

        def local(a):
            return pltpu.make_async_copy(srcs[a], lands[a].at[_flat_index(me)], local_sems.at[a])

        @pl.when(first_step())
        def _():
            for a in range(n):
                local(a).start()
                copy(a, 0, me, sibling, own=True).start()
                for j, chip in enumerate(chips):
                    copy(a, 1 + j, me, (*chip, c), own=True).start()

        @pl.when(mid_step())
        def _():
            for a in range(n):
                for j, chip in enumerate(chips):
                    copy(a, 1 + j, (*chip, c), me).wait_recv()
                    copy(a, 4 + j, (*chip, c), sibling).start()

        body(*ins, *outs, *scratch)

        @pl.when(last_step())
        def _():
            for a in range(n):
                copy(a, 0, sibling, me).wait_recv()
                for j, chip in enumerate(chips):
                    copy(a, 4 + j, (*chip, 1 - c), me).wait_recv()
                for k in range(N_DEV - 1):
                    copy(a, k, me, me).wait_send()
                local(a).wait()

    any_spec = pl.BlockSpec(memory_space=pl.ANY)
    return pl.pallas_call(
        full_body, name=name, grid=grid,
        in_specs=list(in_specs) + [any_spec] * n,
        out_specs=out_specs + (any_spec,) * n,
        out_shape=out_shape + tuple(jax.ShapeDtypeStruct((N_DEV,) + t.shape, t.dtype) for t in shards),
        scratch_shapes=list(scratch_shapes) + [pltpu.SemaphoreType.DMA((n, N_DEV - 1)),
                                               pltpu.SemaphoreType.DMA((n, N_DEV - 1)),
                                               pltpu.SemaphoreType.DMA((n,))],
        compiler_params=_params(has_side_effects=True),
    )


def _call_with_exchange(body, rides, first_step, last_step, *, name, grid, in_specs, out_specs, out_shape,
                        scratch_shapes=()):
    out_specs = tuple(out_specs) if isinstance(out_specs, (tuple, list)) else (out_specs,)
    out_shape = tuple(out_shape) if isinstance(out_shape, (tuple, list)) else (out_shape,)
    n_in, n_out, n_scr, n = len(in_specs), len(out_specs), len(scratch_shapes), len(rides)
    if n == 0:
        return pl.pallas_call(body, name=name, grid=grid, in_specs=list(in_specs), out_specs=out_specs,
                              out_shape=out_shape, scratch_shapes=list(scratch_shapes),
                              compiler_params=_params())

    def full_body(*refs):
        ins, srcs = refs[:n_in], refs[n_in:n_in + n]
        outs, lands = refs[n_in + n:n_in + n + n_out], refs[n_in + n + n_out:n_in + 2 * n + n_out]
        scratch = refs[n_in + 2 * n + n_out:n_in + 2 * n + n_out + n_scr]
        send_sems, recv_sems, local_sems = refs[-3:]
        x, y, c = _mesh_pos()
        my_idx = _flat_index((x, y, c))
        peers = _peer_list(x, y, c)

        def remote(a, k):
            return pltpu.make_async_remote_copy(
                src_ref=srcs[a].at[_flat_index(peers[k])], dst_ref=lands[a].at[my_idx],
                send_sem=send_sems.at[a, k], recv_sem=recv_sems.at[a, k],
                device_id=peers[k], device_id_type=MESH)

        def local(a):
            return pltpu.make_async_copy(srcs[a].at[my_idx], lands[a].at[my_idx], local_sems.at[a])

        @pl.when(first_step())
        def _():
            for a in range(n):
                local(a).start()
                for k in range(N_DEV - 1):
                    remote(a, k).start()

        body(*ins, *outs, *scratch)

        @pl.when(last_step())
        def _():
            for a in range(n):
                for k in range(N_DEV - 1):
                    cp = remote(a, k)
                    cp.wait_send()
                    cp.wait_recv()
                local(a).wait()

    any_spec = pl.BlockSpec(memory_space=pl.ANY)
    res = pl.pallas_call(
        full_body, name=name, grid=grid,
        in_specs=list(in_specs) + [any_spec] * n,
        out_specs=out_specs + (any_spec,) * n,
        out_shape=out_shape + tuple(jax.ShapeDtypeStruct(t.shape, t.dtype) for t in rides),
        scratch_shapes=list(scratch_shapes) + [pltpu.SemaphoreType.DMA((n, N_DEV - 1)),
                                               pltpu.SemaphoreType.DMA((n, N_DEV - 1)),
                                               pltpu.SemaphoreType.DMA((n,))],
        compiler_params=_params(has_side_effects=True),
    )
    return res


def _head_norm(t, w128, bd):
    ms = _mm_split(t * t, bd) * (1.0 / HEAD_DIM)
    r = lax.rsqrt(ms + EPS)
    return (t * r) * w128, r


def _attn_in(x2, wn1, a_g, cos2, sin2, qnw, knw, bd, shards):
    s_len = x2.shape[0]
    tm = ROW_TILE

    def body(x_ref, wn_ref, w_ref, cos_ref, sin_ref, qnw_ref, knw_ref, bd_ref,
             h1_ref, qraw_ref, kraw_ref, q_ref, k_ref, va_ref, qs_ref, ks_ref, vs_ref,
             q4_ref, k4_ref, v4_ref, q16_ref, k16_ref, v16_ref, proj, slabs, stage, w_full):
        @pl.when(pl.program_id(0) == 0)
        def _():
            for d in range(N_DEV):
                w_full[:, IN_SHARD * d:IN_SHARD * (d + 1)] = w_ref[d]

        xx = x_ref[...]
        r = lax.rsqrt(jnp.mean(xx * xx, axis=-1, keepdims=True) + EPS)
        h = ((xx * r) * wn_ref[...]).astype(BF16)
        h1_ref[...] = h
        proj[...] = _dot(h, w_full[...])
        cos_t, sin_t, bdm = cos_ref[...], sin_ref[...], bd_ref[...]
        for grp, (raw_ref, rope_ref, nw_ref) in enumerate(((qraw_ref, q_ref, qnw_ref),
                                                           (kraw_ref, k_ref, knw_ref))):
            for p in range(4):
                cols = slice(D_GRP * grp + 128 * p, D_GRP * grp + 128 * (p + 1))
                t = proj[:, cols]
                raw_ref[:, 128 * p:128 * (p + 1)] = t
                yn, _ = _head_norm(t, nw_ref[...], bdm)
                roped = yn * cos_t + _swap_halves(yn) * sin_t
                slabs[4 * grp + p] = roped
                rope_ref[:, 128 * p:128 * (p + 1)] = roped.astype(BF16)
        for p in range(4):
            slabs[8 + p] = proj[:, 2 * D_GRP + 128 * p:2 * D_GRP + 128 * (p + 1)]
        for grp, ref in ((2, va_ref), (3, qs_ref), (4, ks_ref), (5, vs_ref)):
            ref[...] = proj[:, D_GRP * grp:D_GRP * (grp + 1)].astype(BF16)
        _split_views(slabs, stage, (q4_ref, k4_ref, v4_ref), (q16_ref, k16_ref, v16_ref))

    row = lambda w: pl.BlockSpec((tm, w), lambda i: (i, 0))
    grp_bf = jax.ShapeDtypeStruct((s_len, D_GRP), BF16)
    grp_f32 = jax.ShapeDtypeStruct((s_len, D_GRP), F32)
    ni = s_len // tm
    return _call_with_gather(
        body, shards, lambda: pl.program_id(0) == 0, lambda: pl.program_id(0) == ni - 2,
        lambda: pl.program_id(0) == ni - 1,
        name="attn_in", grid=(ni,),
        in_specs=[row(D_MODEL), _full((1, D_MODEL)),
                  pl.BlockSpec((N_DEV, D_MODEL, IN_SHARD), lambda i: (0, 0, 0)),
                  row(128), row(128), _full((1, 128)), _full((1, 128)), _full((128, 128))],
        out_specs=(row(D_MODEL),) + (row(D_GRP),) * 8 + (_view_spec(tm, 4),) * 3 + (_view_spec(tm, 16),) * 3,
        out_shape=(jax.ShapeDtypeStruct((s_len, D_MODEL), BF16), grp_f32, grp_f32) + (grp_bf,) * 6
        + (_view_shape(s_len, 4, BF16),) * 3 + (_view_shape(s_len, 16, BF16),) * 3,
        scratch_shapes=[pltpu.VMEM((tm, D_IN), F32), pltpu.VMEM((12, tm, 128), F32), pltpu.VMEM((12, tm, 128), F32),
                        pltpu.VMEM((D_MODEL, D_IN), BF16)],
    )(x2, wn1, a_g, cos2, sin2, qnw, knw, bd, *shards)


def _band_mask(n):
    i = lax.broadcasted_iota(jnp.int32, (2 * BLOCK, 2 * BLOCK), 0) & (BLOCK - 1)
    j = lax.broadcasted_iota(jnp.int32, (2 * BLOCK, 2 * BLOCK), 1)
    dist = i + BLOCK - j
    return (dist >= 0) & (dist <= BLOCK) & ((n - 1) * BLOCK + j >= 0)


def _stack_heads(t2, head0):
    return jnp.concatenate([jnp.where(head0, t2, 0), jnp.where(head0, 0, t2)], axis=0)


def _unstack_heads(t, head0):
    return jnp.where(head0, t[0:BLOCK], t[BLOCK:2 * BLOCK])


def _dil_fwd(qv, kv, vv, r):
    sub_len = qv.shape[0]
    nb = sub_len // BLOCK

    qb = 2 if nb % 2 == 0 else 1

    def body(q_ref, kp_ref, kc_ref, vp_ref, vc_ref, o_ref, lse_ref):
        n = pl.program_id(1)
        lane = lax.broadcasted_iota(jnp.int32, (BLOCK, 128), 1)
        head0 = lane < HEAD_DIM
        units = [(b, slice(128 * p, 128 * (p + 1))) for b in range(qb) for p in range(4)]
        valid = [_band_mask(qb * n + b) for b in range(qb)]
        rows = [slice(BLOCK * b, BLOCK * (b + 1)) for b in range(qb)]

        def keys(prev_ref, cur_ref, b, c):
            before = prev_ref[:, c] if b == 0 else cur_ref[rows[b - 1], c]
            return jnp.concatenate([before, cur_ref[rows[b], c]], axis=0)

        qqs = [_stack_heads(q_ref[rows[b], c] * ATT_SCALE, head0) for b, c in units]
        kks = [keys(kp_ref, kc_ref, b, c) for b, c in units]
        vvs = [keys(vp_ref, vc_ref, b, c) for b, c in units]
        ss = [_dot_nt(qq, kk) for qq, kk in zip(qqs, kks)]
        prs, dens, lses = [], [], []
        for (b, _), s in zip(units, ss):
            s = jnp.where(valid[b], s, NEG)
            m = jnp.max(s, axis=-1, keepdims=True)
            pr = jnp.exp(s - m)
            den = jnp.sum(pr, axis=-1, keepdims=True)
            prs.append(pr.astype(BF16))
            dens.append(den)
            lses.append(m + jnp.log(den))
        pvs = [_dot(pr, vv2) for pr, vv2 in zip(prs, vvs)]
        for (b, c), pv, den, lse in zip(units, pvs, dens, lses):
            o_ref[rows[b], c] = _unstack_heads(pv / den, head0)
            lse_ref[rows[b], c] = _unstack_heads(jnp.broadcast_to(lse, (2 * BLOCK, 128)), head0)

    cur = pl.BlockSpec((qb * BLOCK, D_GRP), lambda c, n: (n, c))
    prev = pl.BlockSpec((BLOCK, D_GRP), lambda c, n: (jnp.maximum(qb * n - 1, 0), c))
    out = jax.ShapeDtypeStruct(qv.shape, F32)
    return pl.pallas_call(
        body, name=f"dil_fwd_r{r}", grid=(r, nb // qb),
        in_specs=[cur, prev, cur, prev, cur], out_specs=(cur, cur), out_shape=(out, out),
        compiler_params=_params(),
    )(qv, kv, kv, vv, vv)


def _dil_bwd(qv, kv, vv, dov, lsev, deltav, r, rides):
    sub_len = qv.shape[0]
    nb = sub_len // BLOCK

    def body(q_ref, kp_ref, kc_ref, vp_ref, vc_ref, do_ref, lse_ref, dl_ref,
             dq_ref, dk_ref, dv_ref, dk_carry, dv_carry):
        n = pl.program_id(1)

        @pl.when(n == 0)
        def _():
            dk_carry[...] = jnp.zeros_like(dk_carry)
            dv_carry[...] = jnp.zeros_like(dv_carry)

        @pl.when(n < nb)
        def _():
            valid = _band_mask(n)
            lane = lax.broadcasted_iota(jnp.int32, (BLOCK, 128), 1)
            head0 = lane < HEAD_DIM
            pairs = [slice(128 * p, 128 * (p + 1)) for p in range(4)]
            qqs = [_stack_heads(q_ref[:, c] * ATT_SCALE, head0) for c in pairs]
            dos = [_stack_heads(do_ref[:, c], head0) for c in pairs]
            kks = [jnp.concatenate([kp_ref[:, c], kc_ref[:, c]], axis=0) for c in pairs]
            vvs = [jnp.concatenate([vp_ref[:, c], vc_ref[:, c]], axis=0) for c in pairs]
            ss = [_dot_nt(qq, kk) for qq, kk in zip(qqs, kks)]
            dps = [_dot_nt(do, vv2) for do, vv2 in zip(dos, vvs)]
            def softmax_terms(p):
                stats = []
                for ref in (lse_ref, dl_ref):
                    t2 = ref[:, pairs[p]]
                    stats.append(jnp.concatenate(
                        [jnp.sum(jnp.where(lane == 0, t2, 0.0), axis=-1, keepdims=True),
                         jnp.sum(jnp.where(lane == HEAD_DIM, t2, 0.0), axis=-1, keepdims=True)], axis=0))
                pr = jnp.where(valid, jnp.exp(jnp.minimum(ss[p] - stats[0], 0.0)), 0.0)
                return pr.astype(BF16), (pr * (dps[p] - stats[1])).astype(BF16)

            def products(p, pr, ds):
                c = pairs[p]
                dq, dkk, dvv = _dot(ds, kks[p]), _dot_tn(ds, qqs[p]), _dot_tn(pr, dos[p])
                dq_ref[:, c] = _unstack_heads(dq, head0) * ATT_SCALE
                dk_ref[:, c] = dk_carry[:, c] + dkk[:BLOCK]
                dv_ref[:, c] = dv_carry[:, c] + dvv[:BLOCK]
                dk_carry[:, c] = dkk[BLOCK:]
                dv_carry[:, c] = dvv[BLOCK:]

            terms = [softmax_terms(p) for p in (0, 1)]
            for p in (0, 1):
                products(p, *terms[p])
            terms = [softmax_terms(p) for p in (2, 3)]
            for p in (2, 3):
                products(p, *terms[p - 2])

        @pl.when(n == nb)
        def _():
            dk_ref[...] = dk_carry[...]
            dv_ref[...] = dv_carry[...]

    last = nb - 1
    cur = pl.BlockSpec((BLOCK, D_GRP), lambda c, n: (jnp.minimum(n, last), c))
    prev = pl.BlockSpec((BLOCK, D_GRP), lambda c, n: (jnp.clip(n - 1, 0, last), c))
    out = jax.ShapeDtypeStruct(qv.shape, F32)
    return _call_with_exchange(
        body, rides,
        lambda: jnp.logical_and(pl.program_id(0) == 0, pl.program_id(1) == 0),
        lambda: jnp.logical_and(pl.program_id(0) == r - 1, pl.program_id(1) == nb),
        name=f"dil_bwd_r{r}", grid=(r, nb + 1),
        in_specs=[cur, prev, cur, prev, cur, cur, cur, cur],
        out_specs=(cur, prev, prev), out_shape=(out, out, out),
        scratch_shapes=[pltpu.VMEM((BLOCK, D_GRP), F32), pltpu.VMEM((BLOCK, D_GRP), F32)],
    )(qv, kv, kv, vv, vv, dov, lsev, deltav, *rides)


def _sb_fwd(qs, ks, vs, tri_suf, shards):
    s_len = qs.shape[0]
    t = SB_TILE
    nq = s_len // t

    npair = SB_PAIRS

    def body(q_ref, k_ref, v_ref, u_ref, o_ref, c_ref, qq, vt, acc, cf, csave):
        row = lax.broadcasted_iota(jnp.int32, (2 * t, t), 0) & (t - 1)
        col = lax.broadcasted_iota(jnp.int32, (2 * t, t), 1)
        diag_mask = col < row
        lane1 = lax.broadcasted_iota(jnp.int32, (t, 128), 1)
        head0 = lane1 < HEAD_DIM
        lane2 = lax.broadcasted_iota(jnp.int32, (2 * t, 128), 1)
        uu = u_ref[...]
        pr = range(npair)
        cols = [slice(128 * pp, 128 * (pp + 1)) for pp in pr]

        i = pl.program_id(1)

        @pl.when(i == 0)
        def _():
            def transpose_v(j, _):
                rows = pl.ds(pl.multiple_of(j * t, t), t)
                for pp in pr:
                    vt[pp, j] = v_ref[rows, cols[pp]].astype(F32).T.astype(BF16)
                return 0

            lax.fori_loop(0, nq, transpose_v, 0)

        for pp in pr:
            q2 = q_ref[:, cols[pp]] * ATT_SCALE
            qq[pp, 0:t, :] = jnp.where(head0, q2, 0)
            qq[pp, t:2 * t, :] = jnp.where(head0, 0, q2)
        acc[...] = jnp.zeros_like(acc)
        cf[...] = jnp.zeros_like(cf)
        csave[...] = jnp.full(csave.shape, 2.0 * SB_DEAD, F32)

        def tile(kb, diag):
            krows = pl.ds(pl.multiple_of(kb * t, t), t)
            zs = [_dot_nt(qq[pp], k_ref[krows, cols[pp]]) for pp in pr]
            lbk = [_log_sigmoid_pair(z) for z in zs]
            lks = [jnp.where(diag_mask, lk, 0.0) if diag else lk for _, lk in lbk]
            sufs = [_cumsum_mm(lk, uu) for lk in lks]
            carries = [cf[pp] for pp in pr]
            avs = []
            for pp in pr:
                a = jnp.exp(lbk[pp][0] + (sufs[pp] + jnp.concatenate([carries[pp]] * (t // 128), axis=1)))
                avs.append((jnp.where(diag_mask, a, 0.0) if diag else a).astype(BF16))
            pvs = [_dot_nt(vt[pp, kb], avs[pp]) for pp in pr]
            for pp in pr:
                acc[pp] += pvs[pp]
                csave[pp] = jnp.where(lane2 == kb, carries[pp], csave[pp])
                cf[pp] = carries[pp] + jnp.broadcast_to(jnp.sum(lks[pp], axis=-1, keepdims=True), (2 * t, 128))

        tile(i, True)

        def alive():
            return jnp.max(cf[...]) > SB_DEAD

        def k_block(state):
            kb, _ = state
            tile(kb, False)
            return kb - 1, alive()

        lax.while_loop(lambda state: jnp.logical_and(state[0] >= 0, state[1]), k_block, (i - 1, alive()))
        for pp in pr:
            o_ref[:, cols[pp]] = jnp.where(head0, acc[pp, :, 0:t].T, acc[pp, :, t:2 * t].T)
            c_ref[2 * pp] = csave[pp, 0:t, :]
            c_ref[2 * pp + 1] = csave[pp, t:2 * t, :]

    width = 128 * npair
    kv = pl.BlockSpec((s_len, width), lambda p, i: (0, p))
    qo = pl.BlockSpec((t, width), lambda p, i: (i, p))
    steps = 4 // npair

    def at(p, i):
        return lambda: jnp.logical_and(pl.program_id(0) == p, pl.program_id(1) == i)

    return _call_with_gather(
        body, shards, at(0, 0), at(steps - 1, (2 * nq) // 3), at(steps - 1, nq - 1),
        name="sb_fwd", grid=(steps, nq),
        in_specs=[qo, kv, kv, pl.BlockSpec((t, t), lambda p, i: (0, 0))],
        out_specs=(qo, pl.BlockSpec((2 * npair, t, 128), lambda p, i: (p, i, 0))),
        out_shape=(jax.ShapeDtypeStruct((s_len, D_GRP), F32),
                   jax.ShapeDtypeStruct((8, s_len, 128), F32)),
        scratch_shapes=[pltpu.VMEM((npair, 2 * t, 128), BF16), pltpu.VMEM((npair, nq, 128, t), BF16),
                        pltpu.VMEM((npair, 128, 2 * t), F32),
                        pltpu.VMEM((npair, 2 * t, 128), F32), pltpu.VMEM((npair, 2 * t, 128), F32)],
    )(qs, ks, vs, tri_suf, *shards)


def _sb_bwd(qs, ks, vs, dos, csaved, tri_suf, tri_pre, rides):
    s_len = qs.shape[0]
    t = SB_TILE
    nq = s_len // t

    npair = SB_BWD_PAIRS

    def body(q_ref, k_ref, v_ref, do_ref, c_ref, u_ref, p_ref, dq_ref, dk_ref, dv_ref,
             qq, dd, qqt, ddt, kt, dq_acc, dkt, dvt, cg):
        row = lax.broadcasted_iota(jnp.int32, (2 * t, t), 0) & (t - 1)
        col = lax.broadcasted_iota(jnp.int32, (2 * t, t), 1)
        diag_mask = col < row
        lane1 = lax.broadcasted_iota(jnp.int32, (t, 128), 1)
        head0 = lane1 < HEAD_DIM
        lane2 = lax.broadcasted_iota(jnp.int32, (2 * t, 128), 1)
        uu, pm = u_ref[...], p_ref[...]
        pr = range(npair)
        cols = [slice(128 * pp, 128 * (pp + 1)) for pp in pr]
        i = pl.program_id(1)

        @pl.when(i == 0)
        def _():
            dkt[...] = jnp.zeros_like(dkt)
            dvt[...] = jnp.zeros_like(dvt)

            def transpose_k(j, _):
                rows = pl.ds(pl.multiple_of(j * t, t), t)
                for pp in pr:
                    kt[pp, j] = k_ref[rows, cols[pp]].astype(F32).T.astype(BF16)
                return 0

            lax.fori_loop(0, nq, transpose_k, 0)

        for pp in pr:
            q2 = q_ref[:, cols[pp]].astype(F32) * ATT_SCALE
            do2 = do_ref[:, cols[pp]].astype(F32)
            for src, nat, tr in ((q2, qq, qqt), (do2, dd, ddt)):
                stacked = jnp.concatenate([jnp.where(head0, src, 0.0), jnp.where(head0, 0.0, src)], axis=0)
                nat[pp] = stacked.astype(BF16)
                tr[pp] = stacked.T.astype(BF16)
        dq_acc[...] = jnp.zeros_like(dq_acc)
        cg[...] = jnp.zeros_like(cg)

        def tile(kb, diag):
            krows = pl.ds(pl.multiple_of(kb * t, t), t)
            zs = [_dot_nt(qq[pp], k_ref[krows, cols[pp]]) for pp in pr]
            das = [_dot_nt(dd[pp], v_ref[krows, cols[pp]]) for pp in pr]
            lbk = [_log_sigmoid_pair(z) for z in zs]
            lks = [jnp.where(diag_mask, lk, 0.0) if diag else lk for _, lk in lbk]
            sufs = [_cumsum_mm(lk, uu) for lk in lks]
            avs, gs = [], []
            for pp in pr:
                cs = jnp.concatenate([c_ref[2 * pp], c_ref[2 * pp + 1]], axis=0)
                cf = jnp.sum(jnp.where(lane2 == kb, cs, 0.0), axis=-1, keepdims=True)
                a = jnp.exp(lbk[pp][0] + (sufs[pp] + cf))
                a = jnp.where(diag_mask, a, 0.0) if diag else a
                avs.append(a.astype(BF16))
                gs.append(a * das[pp])
            gpres = [_cumsum_mm(g, pm) for g in gs]
            dzs = []
            for pp in pr:
                carry = cg[pp]
                beta = jnp.exp(lbk[pp][0])
                dz = gs[pp] - beta * (gs[pp] + (gpres[pp] + jnp.concatenate([carry] * (t // 128), axis=1)))
                dzs.append((jnp.where(diag_mask, dz, 0.0) if diag else dz).astype(BF16))
                cg[pp] = carry + jnp.broadcast_to(jnp.sum(gs[pp], axis=-1, keepdims=True), (2 * t, 128))
            dqs = [_dot_nt(kt[pp, kb], dzs[pp]) for pp in pr]
            dks = [_dot(qqt[pp], dzs[pp]) for pp in pr]
            dvs = [_dot(ddt[pp], avs[pp]) for pp in pr]
            for pp in pr:
                dq_acc[pp] += dqs[pp]
                dkt[pp, kb] += dks[pp]
                dvt[pp, kb] += dvs[pp]

        def k_block(kb, _):
            tile(kb, False)
            return 0

        col_max = jnp.max(jnp.max(c_ref[...], axis=0), axis=0, keepdims=True)
        lane_row = lax.broadcasted_iota(jnp.int32, (1, 128), 1)
        n_live = jnp.sum(jnp.where(jnp.logical_and(col_max > SB_DEAD, lane_row < i), 1, 0))
        lax.fori_loop(i - n_live, i, k_block, 0)
        tile(i, True)
        for pp in pr:
            dq_ref[:, cols[pp]] = jnp.where(head0, dq_acc[pp, :, 0:t].T, dq_acc[pp, :, t:2 * t].T) * ATT_SCALE

        @pl.when(i == nq - 1)
        def _():
            def untranspose(j, _):
                rows = pl.ds(pl.multiple_of(j * t, t), t)
                for pp in pr:
                    dk_ref[rows, cols[pp]] = dkt[pp, j].T
                    dv_ref[rows, cols[pp]] = dvt[pp, j].T
                return 0

            lax.fori_loop(0, nq, untranspose, 0)

    width = 128 * npair
    kv = pl.BlockSpec((s_len, width), lambda p, i: (0, p))
    qo = pl.BlockSpec((t, width), lambda p, i: (i, p))
    tri = pl.BlockSpec((t, t), lambda p, i: (0, 0))
    out = jax.ShapeDtypeStruct((s_len, D_GRP), F32)
    steps = 4 // npair
    return _call_with_exchange(
        body, rides,
        lambda: jnp.logical_and(pl.program_id(0) == 0, pl.program_id(1) == 0),
        lambda: jnp.logical_and(pl.program_id(0) == steps - 1, pl.program_id(1) == nq - 1),
        name="sb_bwd", grid=(steps, nq),
        in_specs=[qo, kv, kv, qo, pl.BlockSpec((2 * npair, t, 128), lambda p, i: (p, i, 0)), tri, tri],
        out_specs=(qo, kv, kv), out_shape=(out, out, out),
        scratch_shapes=[pltpu.VMEM((npair, 2 * t, 128), BF16), pltpu.VMEM((npair, 2 * t, 128), BF16),
                        pltpu.VMEM((npair, 128, 2 * t), BF16), pltpu.VMEM((npair, 128, 2 * t), BF16),
                        pltpu.VMEM((npair, nq, 128, t), BF16),
                        pltpu.VMEM((npair, 128, 2 * t), F32),
                        pltpu.VMEM((npair, nq, 128, t), F32), pltpu.VMEM((npair, nq, 128, t), F32),
                        pltpu.VMEM((npair, 2 * t, 128), F32)],
    )(qs, ks, vs, dos, csaved, tri_suf, tri_pre, *rides)


def _attn_out(o_b, lse_b, o_sb, x2, wdil, wsb, out_g, shards):
    s_len = x2.shape[0]
    tm = ROW_TILE

    def body(o1_ref, l1_ref, o4_ref, l4_ref, o16_ref, l16_ref, osb_ref, x_ref, wdil_ref, wsb_ref, w_ref,
             odil_ref, lse_ref, lse4_ref, lse16_ref, mixed_ref, x1_ref, stage, nat4, nat16):
        _merge_views((o4_ref, l4_ref), (o16_ref, l16_ref), stage, nat4, nat16)
        os_ = (o1_ref[...], _slab_group(nat4, 0), _slab_group(nat16, 0))
        ls = (l1_ref[...], _slab_group(nat4, 1), _slab_group(nat16, 1))
        mx = jnp.maximum(jnp.maximum(ls[0], ls[1]), ls[2])
        es = [jnp.exp(l - mx) for l in ls]
        den = es[0] + es[1] + es[2]
        o_dil = (es[0] * os_[0] + es[1] * os_[1] + es[2] * os_[2]) / den
        odil_ref[...] = o_dil
        lse = mx + jnp.log(den)
        lse_ref[...] = lse
        for p in range(4):
            nat4[p] = lse[:, 128 * p:128 * (p + 1)]
        _split_views(nat4.at[0:4], stage.at[0:4], (lse4_ref,), (lse16_ref,))
        halves = []
        for t, w_r in ((o_dil, wdil_ref), (osb_ref[...], wsb_ref)):
            r = lax.rsqrt(jnp.mean(t * t, axis=-1, keepdims=True) + EPS)
            halves.append(((t * r) * w_r[...]).astype(BF16))
        mixed = jnp.concatenate(halves, axis=1)
        mixed_ref[...] = mixed
        w = w_ref[...].reshape(D_MODEL, D_MODEL)
        x1_ref[...] = x_ref[...] + _dot(mixed, w)

    row = lambda w: pl.BlockSpec((tm, w), lambda i: (i, 0))
    ni = s_len // tm
    return _call_with_gather(
        body, shards, lambda: pl.program_id(0) == 0, lambda: pl.program_id(0) == ni - 2,
        lambda: pl.program_id(0) == ni - 1,
        name="attn_out", grid=(ni,),
        in_specs=[row(D_GRP)] * 2 + [_view_spec(tm, 4)] * 2 + [_view_spec(tm, 16)] * 2
        + [row(D_GRP), row(D_MODEL), _full((1, D_GRP)), _full((1, D_GRP)), _full((N_DEV, OUT_SHARD, D_MODEL))],
        out_specs=(row(D_GRP), row(D_GRP), _view_spec(tm, 4), _view_spec(tm, 16), row(D_MODEL), row(D_MODEL)),
        out_shape=(jax.ShapeDtypeStruct((s_len, D_GRP), F32), jax.ShapeDtypeStruct((s_len, D_GRP), F32),
                   _view_shape(s_len, 4, F32), _view_shape(s_len, 16, F32),
                   jax.ShapeDtypeStruct((s_len, D_MODEL), BF16), jax.ShapeDtypeStruct((s_len, D_MODEL), F32)),
        scratch_shapes=[pltpu.VMEM((8, tm, 128), F32)] * 3,
    )(o_b[0], lse_b[0], o_b[1], lse_b[1], o_b[2], lse_b[2], o_sb, x2, wdil, wsb, out_g, *shards)


def _two_shards(w_ref):
    return w_ref[...].reshape(FF_BLOCK, D_MODEL)


def _ffn_fwd(x1, wn2, tgt, gate_g, up_g, down_g):
    s_len = x1.shape[0]
    tm = ROW_TILE
    ni = s_len // tm

    def body(x_ref, wn_ref, t_ref, wg_ref, wu_ref, wd_ref, g_ref, u_ref, h2_ref, dy_ref, loss_ref, acc):
        j = pl.program_id(1)

        @pl.when(j == 0)
        def _():
            xx = x_ref[...]
            r = lax.rsqrt(jnp.mean(xx * xx, axis=-1, keepdims=True) + EPS)
            h2_ref[...] = ((xx * r) * wn_ref[...]).astype(BF16)
            acc[...] = jnp.zeros_like(acc)

        h = h2_ref[...]
        g = _dot_nt(h, _two_shards(wg_ref))
        u = _dot_nt(h, _two_shards(wu_ref))
        g_ref[...] = g
        u_ref[...] = u
        act = (g * (1.0 / (1.0 + jnp.exp(-g)))) * u
        acc[...] += _dot(act.astype(BF16), _two_shards(wd_ref))

        @pl.when(j == FF_STEPS - 1)
        def _():
            err = (x_ref[...] + acc[...]) - t_ref[...]
            dy_ref[...] = err * (1.0 / D_MODEL)
            part = 0.5 * jnp.sum(jnp.mean(err * err, axis=-1, keepdims=True))
            loss_ref[...] = jnp.full((8, 128), part, F32)

    row = pl.BlockSpec((tm, D_MODEL), lambda i, j: (i, 0))
    hid = pl.BlockSpec((tm, FF_BLOCK), lambda i, j: (i, j))
    return pl.pallas_call(
        body, name="ffn_fwd", grid=(ni, FF_STEPS),
        in_specs=[row, pl.BlockSpec((1, D_MODEL), lambda i, j: (0, 0)), row,
                  pl.BlockSpec((2, FF_PAD, D_MODEL), lambda i, j: (j, 0, 0)),
                  pl.BlockSpec((2, FF_PAD, D_MODEL), lambda i, j: (j, 0, 0)),
                  pl.BlockSpec((2, FF_PAD, D_MODEL), lambda i, j: (j, 0, 0))],
        out_specs=(hid, hid, row, row, pl.BlockSpec((8, 128), lambda i, j: (i, 0))),
        out_shape=(jax.ShapeDtypeStruct((s_len, N_DEV * FF_PAD), F32),
                   jax.ShapeDtypeStruct((s_len, N_DEV * FF_PAD), F32),
                   jax.ShapeDtypeStruct((s_len, D_MODEL), BF16),
                   jax.ShapeDtypeStruct((s_len, D_MODEL), F32),
                   jax.ShapeDtypeStruct((ni * 8, 128), F32)),
        scratch_shapes=[pltpu.VMEM((tm, D_MODEL), F32)],
        compiler_params=_params(),
    )(x1, wn2, tgt, gate_g, up_g, down_g)


def _ffn_bwd_dx(dy, g, u, gate_g, up_g, down_g):
    s_len = dy.shape[0]
    tm = ROW_TILE

    def body(dy_ref, g_ref, u_ref, wg_ref, wu_ref, wd_ref, dg_ref, du_ref, act_ref, dh_ref, acc):
        j = pl.program_id(1)

        @pl.when(j == 0)
        def _():
            acc[...] = jnp.zeros_like(acc)

        halves = [slice(0, tm // 2), slice(tm // 2, tm)]
        wd, wg, wu = _two_shards(wd_ref), _two_shards(wg_ref), _two_shards(wu_ref)
        das = [_dot_nt(dy_ref[rows, :].astype(BF16), wd) for rows in halves]

        def elementwise(rows, da):
            gg, uu = g_ref[rows, :], u_ref[rows, :]
            sig = 1.0 / (1.0 + jnp.exp(-gg))
            silu = gg * sig
            act_ref[rows, :] = (silu * uu).astype(BF16)
            du = (da * silu).astype(BF16)
            dg = (da * uu * (sig * (1.0 + gg * (1.0 - sig)))).astype(BF16)
            du_ref[rows, :] = du
            dg_ref[rows, :] = dg
            return dg, du

        dg0, du0 = elementwise(halves[0], das[0])
        acc[halves[0], :] += _dot(dg0, wg) + _dot(du0, wu)
        dg1, du1 = elementwise(halves[1], das[1])
        acc[halves[1], :] += _dot(dg1, wg) + _dot(du1, wu)

        @pl.when(j == FF_STEPS - 1)
        def _():
            dh_ref[...] = acc[...]

    row = pl.BlockSpec((tm, D_MODEL), lambda i, j: (i, 0))
    hid = pl.BlockSpec((tm, FF_BLOCK), lambda i, j: (i, j))
    hid_bf = jax.ShapeDtypeStruct((s_len, N_DEV * FF_PAD), BF16)
    return pl.pallas_call(
        body, name="ffn_bwd_dx", grid=(s_len // tm, FF_STEPS),
        in_specs=[row, hid, hid,
                  pl.BlockSpec((2, FF_PAD, D_MODEL), lambda i, j: (j, 0, 0)),
                  pl.BlockSpec((2, FF_PAD, D_MODEL), lambda i, j: (j, 0, 0)),
                  pl.BlockSpec((2, FF_PAD, D_MODEL), lambda i, j: (j, 0, 0))],
        out_specs=(hid, hid, hid, row),
        out_shape=(hid_bf, hid_bf, hid_bf, jax.ShapeDtypeStruct((s_len, D_MODEL), F32)),
        scratch_shapes=[pltpu.VMEM((tm, D_MODEL), F32)],
        compiler_params=_params(),
    )(dy, g, u, gate_g, up_g, down_g)


def _ffn_bwd_dw(h2, dy, dg, du, act):
    s_len = h2.shape[0]
    tm = DW_ROW_TILE
    ni = s_len // tm

    def body(h_ref, dy_ref, dg_ref, du_ref, act_ref, dwg_ref, dwu_ref, dwd_ref, ag, au, ad):
        i = pl.program_id(1)

        @pl.when(i == 0)
        def _():
            ag[...] = jnp.zeros_like(ag)
            au[...] = jnp.zeros_like(au)
            ad[...] = jnp.zeros_like(ad)

        h = h_ref[...]
        ag[...] += _dot_tn(h, dg_ref[...])
        au[...] += _dot_tn(h, du_ref[...])
        ad[...] += _dot_tn(act_ref[...], dy_ref[...].astype(BF16))

        @pl.when(i == ni - 1)
        def _():
            for half in range(2):
                cols = slice(FF_PAD * half, FF_PAD * (half + 1))
                dwg_ref[half] = ag[:, cols].astype(BF16)
                dwu_ref[half] = au[:, cols].astype(BF16)
            dwd_ref[...] = ad[...].astype(BF16).reshape(2, FF_PAD, D_MODEL)

    row = pl.BlockSpec((tm, D_MODEL), lambda j, i: (i, 0))
    hid = pl.BlockSpec((tm, FF_BLOCK), lambda j, i: (i, j))
    col_w = pl.BlockSpec((2, D_MODEL, FF_PAD), lambda j, i: (j, 0, 0))
    row_w = pl.BlockSpec((2, FF_PAD, D_MODEL), lambda j, i: (j, 0, 0))
    return pl.pallas_call(
        body, name="ffn_bwd_dw", grid=(FF_STEPS, ni),
        in_specs=[row, row, hid, hid, hid], out_specs=(col_w, col_w, row_w),
        out_shape=(jax.ShapeDtypeStruct((N_DEV, D_MODEL, FF_PAD), BF16),
                   jax.ShapeDtypeStruct((N_DEV, D_MODEL, FF_PAD), BF16),
                   jax.ShapeDtypeStruct((N_DEV, FF_PAD, D_MODEL), BF16)),
        scratch_shapes=[pltpu.VMEM((D_MODEL, FF_BLOCK), F32), pltpu.VMEM((D_MODEL, FF_BLOCK), F32),
                        pltpu.VMEM((FF_BLOCK, D_MODEL), F32)],
        compiler_params=_params(),
    )(h2, dy, dg, du, act)


def _rms_bwd(dy, t, w):
    r = lax.rsqrt(jnp.mean(t * t, axis=-1, keepdims=True) + EPS)
    gw = dy * w
    dt = r * (gw - t * ((r * r) * jnp.mean(gw * t, axis=-1, keepdims=True)))
    return dt, dy * t * r


def _attn_out_bwd(dy, dh2, x1, wn2, b_g, mixed, o_dil, o_sb, wdil, wsb, bd512):
    s_len = dy.shape[0]
    tm = ROW_TILE
    ni = s_len // tm

    def body(dy_ref, dh_ref, x1_ref, wn_ref, w_ref, mixed_ref, odil_ref, osb_ref, wdil_ref, wsb_ref, bd_ref,
             dx1_ref, dodil_ref, delta_ref, dosb_ref, dwout_ref, dwn_ref, dwdil_ref, dwsb_ref,
             do4_ref, dl4_ref, do16_ref, dl16_ref, wacc, both, stage):
        i = pl.program_id(0)

        @pl.when(i == 0)
        def _():
            wacc[...] = jnp.zeros_like(wacc)
            dwn_ref[...] = jnp.zeros_like(dwn_ref)
            dwdil_ref[...] = jnp.zeros_like(dwdil_ref)
            dwsb_ref[...] = jnp.zeros_like(dwsb_ref)

        dnorm, dw_rows = _rms_bwd(dh_ref[...], x1_ref[...], wn_ref[...])
        dx1 = dy_ref[...] + dnorm
        dx1_ref[...] = dx1
        dwn_ref[...] += jnp.sum(dw_rows, axis=0, keepdims=True)
        dx1b = dx1.astype(BF16)
        w = w_ref[...].reshape(D_MODEL, D_MODEL)
        dmixed = _dot_nt(dx1b, w)
        wacc[...] += _dot_tn(mixed_ref[...], dx1b)
        o_dil = odil_ref[...]
        d_odil, dw_rows = _rms_bwd(dmixed[:, :D_GRP], o_dil, wdil_ref[...])
        dwdil_ref[...] += jnp.sum(dw_rows, axis=0, keepdims=True)
        dodil_ref[...] = d_odil.astype(BF16)
        delta = _mm_split(d_odil * o_dil, bd_ref[...])
        delta_ref[...] = delta
        for p in range(4):
            both[p] = d_odil[:, 128 * p:128 * (p + 1)]
            both[4 + p] = delta[:, 128 * p:128 * (p + 1)]
        _split_views(both, stage, (do4_ref, dl4_ref), (do16_ref, dl16_ref))
        d_osb, dw_rows = _rms_bwd(dmixed[:, D_GRP:], osb_ref[...], wsb_ref[...])
        dwsb_ref[...] += jnp.sum(dw_rows, axis=0, keepdims=True)
        dosb_ref[...] = d_osb.astype(BF16)

        @pl.when(i == ni - 1)
        def _():
            dwout_ref[...] = wacc[...].astype(BF16).reshape(N_DEV, OUT_SHARD, D_MODEL)

    row = lambda w: pl.BlockSpec((tm, w), lambda i: (i, 0))
    return pl.pallas_call(
        body, name="attn_out_bwd", grid=(ni,),
        in_specs=[row(D_MODEL), row(D_MODEL), row(D_MODEL), _full((1, D_MODEL)),
                  _full((N_DEV, OUT_SHARD, D_MODEL)),
                  row(D_MODEL), row(D_GRP), row(D_GRP), _full((1, D_GRP)), _full((1, D_GRP)),
                  _full((D_GRP, D_GRP))],
        out_specs=(row(D_MODEL), row(D_GRP), row(D_GRP), row(D_GRP),
                   _full((N_DEV, OUT_SHARD, D_MODEL)), _full((1, D_MODEL)), _full((1, D_GRP)), _full((1, D_GRP)),
                   _view_spec(tm, 4), _view_spec(tm, 4), _view_spec(tm, 16), _view_spec(tm, 16)),
        out_shape=(jax.ShapeDtypeStruct((s_len, D_MODEL), F32), jax.ShapeDtypeStruct((s_len, D_GRP), BF16),
                   jax.ShapeDtypeStruct((s_len, D_GRP), F32), jax.ShapeDtypeStruct((s_len, D_GRP), BF16),
                   jax.ShapeDtypeStruct((N_DEV, OUT_SHARD, D_MODEL), BF16),
                   jax.ShapeDtypeStruct((1, D_MODEL), F32), jax.ShapeDtypeStruct((1, D_GRP), F32),
                   jax.ShapeDtypeStruct((1, D_GRP), F32),
                   _view_shape(s_len, 4, BF16), _view_shape(s_len, 4, F32),
                   _view_shape(s_len, 16, BF16), _view_shape(s_len, 16, F32)),
        scratch_shapes=[pltpu.VMEM((D_MODEL, D_MODEL), F32), pltpu.VMEM((8, tm, 128), F32),
                        pltpu.VMEM((8, tm, 128), F32)],
        compiler_params=_params(),
    )(dy, dh2, x1, wn2, b_g, mixed, o_dil, o_sb, wdil, wsb, bd512)


def _qkv_bwd(dq_b, dk_b, dv_b, dqs, dks, dvs, qraw, kraw, cos2, sin2, qnw, knw, bd):
    s_len = qraw.shape[0]
    tm = ROW_TILE
    ni = s_len // tm

    def body(dq1, dk1, dv1, dq4, dk4, dv4, dq16, dk16, dv16, dqs_ref, dks_ref, dvs_ref,
             qraw_ref, kraw_ref, cos_ref, sin_ref, qnw_ref, knw_ref, bd_ref,
             dproj_ref, dqn_ref, dkn_ref, stage, nat4, nat16):
        i = pl.program_id(0)

        @pl.when(i == 0)
        def _():
            dqn_ref[...] = jnp.zeros_like(dqn_ref)
            dkn_ref[...] = jnp.zeros_like(dkn_ref)

        _merge_views((dq4, dk4, dv4), (dq16, dk16, dv16), stage, nat4, nat16)
        cos_t, sin_t, bdm = cos_ref[...], sin_ref[...], bd_ref[...]
        for grp, (part1, raw_ref, nw_ref, dn_ref) in enumerate(((dq1, qraw_ref, qnw_ref, dqn_ref),
                                                                (dk1, kraw_ref, knw_ref, dkn_ref))):
            dn_acc = 0.0
            for p in range(4):
                cols = slice(128 * p, 128 * (p + 1))
                d_rope = part1[:, cols] + nat4[4 * grp + p] + nat16[4 * grp + p]
                d_norm = d_rope * cos_t + _swap_halves(d_rope * sin_t)
                t = raw_ref[:, cols]
                w = nw_ref[...]
                r = lax.rsqrt(_mm_split(t * t, bdm) * (1.0 / HEAD_DIM) + EPS)
                gw = d_norm * w
                corr = _mm_split(gw * t, bdm) * (1.0 / HEAD_DIM)
                dt = r * (gw - t * ((r * r) * corr))
                dn_acc = dn_acc + jnp.sum(d_norm * t * r, axis=0, keepdims=True)
                dproj_ref[:, D_GRP * grp + 128 * p:D_GRP * grp + 128 * (p + 1)] = dt.astype(BF16)
            dn_ref[...] += dn_acc
        dproj_ref[:, 2 * D_GRP:3 * D_GRP] = (dv1[...] + _slab_group(nat4, 2) + _slab_group(nat16, 2)).astype(BF16)
        dproj_ref[:, 3 * D_GRP:4 * D_GRP] = dqs_ref[...].astype(BF16)
        dproj_ref[:, 4 * D_GRP:5 * D_GRP] = dks_ref[...].astype(BF16)
        dproj_ref[:, 5 * D_GRP:6 * D_GRP] = dvs_ref[...].astype(BF16)

    row = lambda w: pl.BlockSpec((tm, w), lambda i: (i, 0))
    return pl.pallas_call(
        body, name="qkv_bwd", grid=(ni,),
        in_specs=[row(D_GRP)] * 3 + [_view_spec(tm, 4)] * 3 + [_view_spec(tm, 16)] * 3 + [row(D_GRP)] * 5
        + [row(128), row(128), _full((1, 128)), _full((1, 128)), _full((128, 128))],
        out_specs=(row(D_IN), _full((1, 128)), _full((1, 128))),
        out_shape=(jax.ShapeDtypeStruct((s_len, D_IN), BF16), jax.ShapeDtypeStruct((1, 128), F32),
                   jax.ShapeDtypeStruct((1, 128), F32)),
        scratch_shapes=[pltpu.VMEM((12, tm, 128), F32)] * 3,
        compiler_params=_params(),
    )(dq_b[0], dk_b[0], dv_b[0], dq_b[1], dk_b[1], dv_b[1], dq_b[2], dk_b[2], dv_b[2],
      dqs, dks, dvs, qraw, kraw, cos2, sin2, qnw, knw, bd)


def _in_bwd_dx(dproj, a_g, x2, dx1, wn1, rides):
    s_len = x2.shape[0]
    tm = ROW_TILE
    ni = s_len // tm

    def body(dp_ref, w_ref, x_ref, dx1_ref, wn_ref, gx_ref, dwn_ref, w_full):
        i = pl.program_id(0)

        @pl.when(i == 0)
        def _():
            dwn_ref[...] = jnp.zeros_like(dwn_ref)
            for d in range(N_DEV):
                w_full[:, IN_SHARD * d:IN_SHARD * (d + 1)] = w_ref[d]

        dh = _dot_nt(dp_ref[...], w_full[...])
        dnorm, dw_rows = _rms_bwd(dh, x_ref[...], wn_ref[...])
        gx_ref[...] = dx1_ref[...] + dnorm
        dwn_ref[...] += jnp.sum(dw_rows, axis=0, keepdims=True)

    row = lambda w: pl.BlockSpec((tm, w), lambda i: (i, 0))
    return _call_with_exchange(
        body, rides, lambda: pl.program_id(0) == 0, lambda: pl.program_id(0) == ni - 1,
        name="in_bwd_dx", grid=(ni,),
        in_specs=[row(D_IN), pl.BlockSpec((N_DEV, D_MODEL, IN_SHARD), lambda i: (0, 0, 0)),
                  row(D_MODEL), row(D_MODEL), _full((1, D_MODEL))],
        out_specs=(row(D_MODEL), _full((1, D_MODEL))),
        out_shape=(jax.ShapeDtypeStruct((s_len, D_MODEL), F32), jax.ShapeDtypeStruct((1, D_MODEL), F32)),
        scratch_shapes=[pltpu.VMEM((D_MODEL, D_IN), BF16)],
    )(dproj, a_g, x2, dx1, wn1, *rides)


def _in_bwd_dw(h1, dproj):
    s_len = h1.shape[0]
    tm = DW_ROW_TILE
    ni = s_len // tm

    def body(h_ref, dp_ref, dw_ref, acc):
        i = pl.program_id(1)

        @pl.when(i == 0)
        def _():
            acc[...] = jnp.zeros_like(acc)

        acc[...] += _dot_tn(h_ref[...], dp_ref[...])

        @pl.when(i == ni - 1)
        def _():
            for half in range(2):
                dw_ref[half] = acc[:, IN_SHARD * half:IN_SHARD * (half + 1)].astype(BF16)

    return pl.pallas_call(
        body, name="in_bwd_dw", grid=(N_DEV // 2, ni),
        in_specs=[pl.BlockSpec((tm, D_MODEL), lambda d, i: (i, 0)),
                  pl.BlockSpec((tm, 2 * IN_SHARD), lambda d, i: (i, d))],
        out_specs=pl.BlockSpec((2, D_MODEL, IN_SHARD), lambda d, i: (d, 0, 0)),
        out_shape=jax.ShapeDtypeStruct((N_DEV, D_MODEL, IN_SHARD), BF16),
        scratch_shapes=[pltpu.VMEM((D_MODEL, 2 * IN_SHARD), F32)],
        compiler_params=_params(),
    )(h1, dproj)


def _adamw(recv, w, m, v):
    rows, cols = w.shape
    tr = next((t for t in (128, 32) if rows % t == 0), rows)

    def body(p_ref, w_ref, m_ref, v_ref, g_ref, d_ref, nm_ref, nv_ref):
        g = p_ref[0].astype(F32)
        for s in range(1, N_DEV):
            g = g + p_ref[s].astype(F32)
        m_new = ADAM_B1 * m_ref[...] + (1.0 - ADAM_B1) * g
        v_new = ADAM_B2 * v_ref[...] + (1.0 - ADAM_B2) * (g * g)
        m_hat = m_new / (1.0 - ADAM_B1 ** ADAM_STEP)
        v_hat = v_new / (1.0 - ADAM_B2 ** ADAM_STEP)
        g_ref[...] = g
        d_ref[...] = -ADAM_LR * (m_hat / (jnp.sqrt(v_hat) + ADAM_EPS) + ADAM_WD * w_ref[...])
        nm_ref[...] = m_new
        nv_ref[...] = v_new

    blk = pl.BlockSpec((tr, cols), lambda i: (i, 0))
    out = jax.ShapeDtypeStruct((rows, cols), F32)
    return pl.pallas_call(
        body, name=f"adamw_{rows}x{cols}", grid=(rows // tr,),
        in_specs=[pl.BlockSpec((N_DEV, tr, cols), lambda i: (0, i, 0)), blk, blk, blk],
        out_specs=(blk,) * 4, out_shape=(out,) * 4,
        compiler_params=_params(),
    )(recv, w, m, v)


def _rope_tables(s_len):
    pos = jnp.arange(s_len, dtype=F32)
    inv_freq = ROPE_THETA ** (-jnp.arange(0, HEAD_DIM, 2, dtype=F32) / HEAD_DIM)
    ang = pos[:, None] * inv_freq[None, :]
    cos, sin = jnp.cos(ang), jnp.sin(ang)
    cos2 = jnp.concatenate([cos, cos, cos, cos], axis=1)
    sin2 = jnp.concatenate([-sin, sin, -sin, sin], axis=1)
    return cos2, sin2


def _block_diag_ones(n):
    i = jnp.arange(n)
    return (i[:, None] // HEAD_DIM == i[None, :] // HEAD_DIM).astype(BF16)


def _pad_cols(t):
    return jnp.pad(t, ((0, 0), (0, FF_PAD - FF_SHARD)))


def _pad_rows(t):
    return jnp.pad(t, ((0, FF_PAD - FF_SHARD), (0, 0)))


LOSS_ROW = 26


def _pack_small(n1, n2, ndil, nsb, nq, nk, scalar=None):
    pad = lambda t: jnp.pad(t.reshape(1, -1), ((0, 0), (0, 128 - t.size)))
    last = jnp.zeros((1, 128), F32) if scalar is None else pad(scalar)
    rows = [n1.reshape(8, 128), n2.reshape(8, 128), ndil.reshape(4, 128), nsb.reshape(4, 128),
            pad(nq), pad(nk), last, jnp.zeros((5, 128), F32)]
    return jnp.concatenate(rows, axis=0)


def _unpack_small(t):
    return (t[0:8].reshape(1, D_MODEL), t[8:16].reshape(1, D_MODEL), t[16:20].reshape(1, D_GRP),
            t[20:24].reshape(1, D_GRP), t[24:25, :HEAD_DIM], t[25:26, :HEAD_DIM])


def kernel(x, attn_norm_w, w_in, q_norm_w, k_norm_w, dil_out_norm_w, sb_out_norm_w, w_out, ffn_norm_w, w_gate, w_up, w_down, loss_target, m_attn_norm_w, m_w_in, m_q_norm_w, m_k_norm_w, m_dil_out_norm_w, m_sb_out_norm_w, m_w_out, m_ffn_norm_w, m_w_gate, m_w_up, m_w_down, v_attn_norm_w, v_w_in, v_q_norm_w, v_k_norm_w, v_dil_out_norm_w, v_sb_out_norm_w, v_w_out, v_ffn_norm_w, v_w_gate, v_w_up, v_w_down):
    s_len = x.shape[1]
    x2, tgt = x[0], loss_target[0]

    (a_g,) = _gather_weights([w_in[0].astype(BF16)])
    gate_loc = _pad_cols(w_gate[0]).T.astype(BF16)
    up_loc = _pad_cols(w_up[0]).T.astype(BF16)
    down_loc = _pad_rows(w_down[0]).astype(BF16)
    out_loc = w_out[0].astype(BF16)

    cos2, sin2 = _rope_tables(s_len)
    bd128, bd512 = _block_diag_ones(128), _block_diag_ones(D_GRP)
    idx = jnp.arange(SB_TILE)
    tri_suf = (idx[:, None] > idx[None, :]).astype(BF16)
    tri_pre = (idx[:, None] < idx[None, :]).astype(BF16)
    qnw2 = jnp.concatenate([q_norm_w, q_norm_w], axis=1)
    knw2 = jnp.concatenate([k_norm_w, k_norm_w], axis=1)

    (h1, qraw, kraw, q, k, va, qs, ks, vs, q4, k4, v4, q16, k16, v16,
     out_g, gate_g) = _attn_in(x2, attn_norm_w, a_g, cos2, sin2, qnw2, knw2, bd128, shards=[out_loc, gate_loc])
    qkv_views = {1: (q, k, va), 4: (q4, k4, v4), 16: (q16, k16, v16)}
    o_b, lse_b = [], []
    for r in DILATIONS:
        o, lse = _dil_fwd(*qkv_views[r], r)
        o_b.append(o)
        lse_b.append(lse)
    o_sb, c_sb, up_g = _sb_fwd(qs, ks, vs, tri_suf, shards=[up_loc])
    o_dil, lse_tot, lse4, lse16, mixed, x1, down_g = _attn_out(
        o_b, lse_b, o_sb, x2, dil_out_norm_w, sb_out_norm_w, out_g, shards=[down_loc])
    g, u, h2, dy, loss_parts = _ffn_fwd(x1, ffn_norm_w, tgt, gate_g, up_g, down_g)
    loss_local = jnp.sum(loss_parts[::8, 0])

    dg, du, act, dh2 = _ffn_bwd_dx(dy, g, u, gate_g, up_g, down_g)
    (dx1, do_dil, delta, do_sb, dwout, dn2, dndil, dnsb, do4, dl4, do16, dl16) = _attn_out_bwd(
        dy, dh2, x1, ffn_norm_w, out_g, mixed, o_dil, o_sb, dil_out_norm_w, sb_out_norm_w, bd512)
    dwg, dwu, dwd = _ffn_bwd_dw(h2, dy, dg, du, act)
    dqs, dks, dvs, r_gate, r_down = _sb_bwd(qs, ks, vs, do_sb, c_sb, tri_suf, tri_pre, rides=[dwg, dwd])
    cot_views = {1: (do_dil, lse_tot, delta), 4: (do4, lse4, dl4), 16: (do16, lse16, dl16)}
    riders = {1: [], 4: [dwout], 16: [dwu]}
    dq_b, dk_b, dv_b, landed = [], [], [], {}
    for r in DILATIONS:
        dq, dk, dv, *landed[r] = _dil_bwd(*qkv_views[r], *cot_views[r], r, rides=riders[r])
        dq_b.append(dq)
        dk_b.append(dk)
        dv_b.append(dv)
    (r_out,), (r_up,) = landed[4], landed[16]
    dproj, dqn2, dkn2 = _qkv_bwd(dq_b, dk_b, dv_b, dqs, dks, dvs, qraw, kraw, cos2, sin2, qnw2, knw2, bd128)
    dwin = _in_bwd_dw(h1, dproj)
    grad_x, dn1, r_in = _in_bwd_dx(dproj, a_g, x2, dx1, attn_norm_w, rides=[dwin])
    dqn = dqn2[:, :HEAD_DIM] + dqn2[:, HEAD_DIM:]
    dkn = dkn2[:, :HEAD_DIM] + dkn2[:, HEAD_DIM:]

    small = _pack_small(dn1, dn2, dndil, dnsb, dqn, dkn, loss_local)
    (r_small,) = _exchange_grads([], small)
    big = {
        "w_in": _adamw(r_in, w_in[0], m_w_in[0], v_w_in[0]),
        "w_gate": tuple(t[:, :FF_SHARD] for t in _adamw(r_gate, _pad_cols(w_gate[0]), _pad_cols(m_w_gate[0]), _pad_cols(v_w_gate[0]))),
        "w_up": tuple(t[:, :FF_SHARD] for t in _adamw(r_up, _pad_cols(w_up[0]), _pad_cols(m_w_up[0]), _pad_cols(v_w_up[0]))),
        "w_down": _adamw(r_down, w_down[0], m_w_down[0], v_w_down[0]),
        "w_out": _adamw(r_out, w_out[0], m_w_out[0], v_w_out[0]),
    }
    packs = [_pack_small(*ts) for ts in (
        (attn_norm_w, ffn_norm_w, dil_out_norm_w, sb_out_norm_w, q_norm_w, k_norm_w),
        (m_attn_norm_w, m_ffn_norm_w, m_dil_out_norm_w, m_sb_out_norm_w, m_q_norm_w, m_k_norm_w),
        (v_attn_norm_w, v_ffn_norm_w, v_dil_out_norm_w, v_sb_out_norm_w, v_q_norm_w, v_k_norm_w))]
    small_raw = _adamw(r_small, *packs)
    loss = small_raw[0][LOSS_ROW, 0]
    small_out = [_unpack_small(t) for t in small_raw]
    names = ["attn_norm_w", "w_in", "q_norm_w", "k_norm_w", "dil_out_norm_w", "sb_out_norm_w", "w_out",
             "ffn_norm_w", "w_gate", "w_up", "w_down"]
    small_pos = {"attn_norm_w": 0, "ffn_norm_w": 1, "dil_out_norm_w": 2, "sb_out_norm_w": 3,
                 "q_norm_w": 4, "k_norm_w": 5}
    outs = [loss, grad_x[None]]
    for kind in range(4):
        for name in names:
            if name in small_pos:
                outs.append(small_out[kind][small_pos[name]])
            else:
                outs.append(big[name][kind][None])
    return tuple(outs)
```

```python
import jax
import jax.numpy as jnp
from jax import lax
from jax.experimental import pallas as pl
from jax.experimental.pallas import tpu as pltpu

F32 = jnp.float32
BF16 = jnp.bfloat16

N_DEV = 8
D_MODEL = 1024
HEAD_DIM = 64
D_GRP = 512
D_IN = 6 * D_GRP
IN_SHARD = D_IN // N_DEV
FF_SHARD = 352
FF_PAD = 384
FF_BLOCK = 2 * FF_PAD
FF_STEPS = N_DEV // 2
OUT_SHARD = D_MODEL // N_DEV
BLOCK = 128
DILATIONS = (1, 4, 16)
ROPE_THETA = 10000.0
EPS = 1e-6
ATT_SCALE = HEAD_DIM ** -0.5
NEG = -1e30

ADAM_LR = 0.001
ADAM_B1 = 0.9
ADAM_B2 = 0.999
ADAM_EPS = 1e-08
ADAM_WD = 0.01
ADAM_STEP = 10

SB_TILE = 256
SB_DEAD = -104.0
SB_PAIRS = 4
SB_BWD_PAIRS = 2
ROW_TILE = 512
DW_ROW_TILE = 1024
VMEM_LIMIT = 56 * 1024 * 1024
MESH = pl.DeviceIdType.MESH


def _dot(a, b):
    return jnp.dot(a, b, preferred_element_type=F32)


def _dot_nt(a, b):
    return lax.dot_general(a, b, (((1,), (1,)), ((), ())), preferred_element_type=F32)


def _dot_tn(a, b):
    return lax.dot_general(a, b, (((0,), (0,)), ((), ())), preferred_element_type=F32)


def _mm_split(t, m):
    hi = t.astype(BF16)
    lo = (t - hi.astype(F32)).astype(BF16)
    return _dot(hi, m) + _dot(lo, m)


def _params(**kw):
    return pltpu.CompilerParams(vmem_limit_bytes=VMEM_LIMIT, **kw)


def _full(shape):
    nd = len(shape)
    return pl.BlockSpec(shape, lambda *_: (0,) * nd)


def _view_shape(s_len, r, dtype):
    return jax.ShapeDtypeStruct((s_len // r, r * D_GRP), dtype)


def _view_spec(tm, r):
    return pl.BlockSpec((tm // r, r * D_GRP), lambda i: (i, 0))


def _swap_halves(t):
    lane = lax.broadcasted_iota(jnp.int32, t.shape, 1)
    first = (lane & 32) == 0
    return jnp.where(first, pltpu.roll(t, 96, 1), pltpu.roll(t, 32, 1))


def _log_sigmoid_pair(z):
    neg_abs = lax.bitcast_convert_type(lax.bitcast_convert_type(z, jnp.uint32) | jnp.uint32(0x80000000), F32)
    lb = jnp.minimum(z, 0.0) - jnp.log(1.0 + jnp.exp(neg_abs))
    return lb, lb - z


def _cumsum_mm(t, tri):
    return _dot(t.astype(BF16), tri)


def _split_views(src_ref, stage_ref, views4, views16):
    slabs, n, _ = src_ref.shape
    n4, n16 = n // 4, n // 16
    for j in range(slabs):
        g, lanes = j // 4, 128 * (j % 4)
        src, stage = src_ref.at[j], stage_ref.at[j]
        for c4 in range(4):
            blk = src[pl.ds(c4, n4, stride=4), :]
            stage[n4 * c4:n4 * (c4 + 1), :] = blk
            col = D_GRP * c4 + lanes
            views4[g][:, col:col + 128] = blk.astype(views4[g].dtype)
        for c4 in range(4):
            for c1 in range(4):
                blk = stage[pl.ds(n4 * c4 + c1, n16, stride=4), :]
                col = D_GRP * (4 * c1 + c4) + lanes
                views16[g][:, col:col + 128] = blk.astype(views16[g].dtype)


def _merge_views(views4, views16, stage_ref, dst4_ref, dst16_ref):
    slabs, n, _ = dst4_ref.shape
    n4, n16 = n // 4, n // 16
    for j in range(slabs):
        g, lanes = j // 4, 128 * (j % 4)
        dst4, dst16, stage = dst4_ref.at[j], dst16_ref.at[j], stage_ref.at[j]
        for c4 in range(4):
            col = D_GRP * c4 + lanes
            dst4[pl.ds(c4, n4, stride=4), :] = views4[g][:, col:col + 128].astype(F32)
            for c1 in range(4):
                col = D_GRP * (4 * c1 + c4) + lanes
                stage[pl.ds(n4 * c4 + c1, n16, stride=4), :] = views16[g][:, col:col + 128].astype(F32)
        for c4 in range(4):
            dst16[pl.ds(c4, n4, stride=4), :] = stage[n4 * c4:n4 * (c4 + 1), :]


def _slab_group(ref, g):
    return jnp.concatenate([ref[4 * g + p] for p in range(4)], axis=1)


def _mesh_pos():
    return lax.axis_index("x"), lax.axis_index("y"), lax.axis_index("c")


def _flat_index(p):
    return 4 * p[0] + 2 * p[1] + p[2]


def _gather_weights(shards):
    n_arr = len(shards)

    def body(*refs):
        srcs, outs = refs[:n_arr], refs[n_arr:2 * n_arr]
        send_sems, recv_sems, local_sems = refs[2 * n_arr:]
        x, y, c = _mesh_pos()
        me, sibling = (x, y, c), (x, y, 1 - c)
        chips = [(1 - x, y), (x, 1 - y), (1 - x, 1 - y)]

        def copy(arr, k, block, to, own=False):
            dst = outs[arr].at[_flat_index(block)]
            return pltpu.make_async_remote_copy(
                src_ref=srcs[arr] if own else dst, dst_ref=dst,
                send_sem=send_sems.at[arr, k], recv_sem=recv_sems.at[arr, k],
                device_id=to, device_id_type=MESH)

        for arr in range(n_arr):
            mine = pltpu.make_async_copy(srcs[arr], outs[arr].at[_flat_index(me)], local_sems.at[arr])
            mine.start()
            first = [copy(arr, 0, me, sibling, own=True)]
            first += [copy(arr, 1 + j, me, (*chip, c), own=True) for j, chip in enumerate(chips)]
            for cp in first:
                cp.start()
        for arr in range(n_arr):
            passed = [copy(arr, 4 + j, (*chip, c), sibling) for j, chip in enumerate(chips)]
            for j, chip in enumerate(chips):
                copy(arr, 1 + j, (*chip, c), me).wait_recv()
                passed[j].start()
        for arr in range(n_arr):
            copy(arr, 0, sibling, me).wait_recv()
            for j, chip in enumerate(chips):
                copy(arr, 4 + j, (*chip, 1 - c), me).wait_recv()
            for k in range(7):
                copy(arr, k, me, me).wait_send()
            pltpu.make_async_copy(srcs[arr], outs[arr].at[_flat_index(me)], local_sems.at[arr]).wait()

    any_spec = pl.BlockSpec(memory_space=pl.ANY)
    return pl.pallas_call(
        body, name="gather_weights",
        out_shape=tuple(jax.ShapeDtypeStruct((N_DEV,) + s.shape, s.dtype) for s in shards),
        in_specs=[any_spec] * n_arr, out_specs=(any_spec,) * n_arr,
        scratch_shapes=[pltpu.SemaphoreType.DMA((n_arr, 7)), pltpu.SemaphoreType.DMA((n_arr, 7)),
                        pltpu.SemaphoreType.DMA((n_arr,))],
        compiler_params=pltpu.CompilerParams(has_side_effects=True),
    )(*shards)


def _peer_list(x, y, c):
    return [(1 - x if m & 4 else x, 1 - y if m & 2 else y, 1 - c if m & 1 else c) for m in range(1, N_DEV)]


def _exchange_grads(parts, small):
    n_arr = len(parts)

    def body(*refs):
        ins, outs = refs[:n_arr + 1], refs[n_arr + 1:2 * (n_arr + 1)]
        send_sems, recv_sems, local_sems = refs[2 * (n_arr + 1):]
        x, y, c = _mesh_pos()
        me = (x, y, c)
        my_idx = _flat_index(me)
        peers = []
        for m in range(1, N_DEV):
            peers.append((1 - x if m & 4 else x, 1 - y if m & 2 else y, 1 - c if m & 1 else c))

        def src_block(arr, dev):
            return ins[arr] if arr == n_arr else ins[arr].at[_flat_index(dev)]

        def copy(arr, k):
            return pltpu.make_async_remote_copy(
                src_ref=src_block(arr, peers[k]), dst_ref=outs[arr].at[my_idx],
                send_sem=send_sems.at[arr, k], recv_sem=recv_sems.at[arr, k],
                device_id=peers[k], device_id_type=MESH)

        def local(arr):
            return pltpu.make_async_copy(src_block(arr, me), outs[arr].at[my_idx], local_sems.at[arr])

        for arr in range(n_arr + 1):
            local(arr).start()
            for k in range(N_DEV - 1):
                copy(arr, k).start()
        for arr in range(n_arr + 1):
            for k in range(N_DEV - 1):
                cp = copy(arr, k)
                cp.wait_send()
                cp.wait_recv()
            local(arr).wait()

    any_spec = pl.BlockSpec(memory_space=pl.ANY)
    out_shape = tuple(jax.ShapeDtypeStruct(p.shape, p.dtype) for p in parts)
    out_shape += (jax.ShapeDtypeStruct((N_DEV,) + small.shape, small.dtype),)
    return pl.pallas_call(
        body, name="exchange_grads",
        out_shape=out_shape,
        in_specs=[any_spec] * (n_arr + 1), out_specs=(any_spec,) * (n_arr + 1),
        scratch_shapes=[pltpu.SemaphoreType.DMA((n_arr + 1, N_DEV - 1)),
                        pltpu.SemaphoreType.DMA((n_arr + 1, N_DEV - 1)),
                        pltpu.SemaphoreType.DMA((n_arr + 1,))],
        compiler_params=pltpu.CompilerParams(has_side_effects=True),
    )(*parts, small)


def _call_with_gather(body, shards, first_step, mid_step, last_step, *, name, grid, in_specs, out_specs,
                      out_shape, scratch_shapes=()):
    out_specs = tuple(out_specs) if isinstance(out_specs, (tuple, list)) else (out_specs,)
    out_shape = tuple(out_shape) if isinstance(out_shape, (tuple, list)) else (out_shape,)
    n_in, n_out, n_scr, n = len(in_specs), len(out_specs), len(scratch_shapes), len(shards)

    def full_body(*refs):
        ins, srcs = refs[:n_in], refs[n_in:n_in + n]
        outs, lands = refs[n_in + n:n_in + n + n_out], refs[n_in + n + n_out:n_in + 2 * n + n_out]
        scratch = refs[n_in + 2 * n + n_out:n_in + 2 * n + n_out + n_scr]
        send_sems, recv_sems, local_sems = refs[-3:]
        x, y, c = _mesh_pos()
        me, sibling = (x, y, c), (x, y, 1 - c)
        chips = [(1 - x, y), (x, 1 - y), (1 - x, 1 - y)]

        def copy(a, k, block, to, own=False):
            dst = lands[a].at[_flat_index(block)]
            return pltpu.make_async_remote_copy(
                src_ref=srcs[a] if own else dst, dst_ref=dst,
                send_sem=send_sems.at[a, k], recv_sem=recv_sems.at[a, k],
                device_id=to, device_id_type=MESH)

        def local(a):
            return pltpu.make_async_copy(srcs[a], lands[a].at[_flat_index(me)], local_sems.at[a])

        @pl.when(first_step())
        def _():
            for a in range(n):
                local(a).start()
                copy(a, 0, me, sibling, own=True).start()
                for j, chip in enumerate(chips):
                    copy(a, 1 + j, me, (*chip, c), own=True).start()

        @pl.when(mid_step())
        def _():
            for a in range(n):
                for j, chip in enumerate(chips):
                    copy(a, 1 + j, (*chip, c), me).wait_recv()
                    copy(a, 4 + j, (*chip, c), sibling).start()

        body(*ins, *outs, *scratch)

        @pl.when(last_step())
        def _():
            for a in range(n):
                copy(a, 0, sibling, me).wait_recv()
                for j, chip in enumerate(chips):
                    copy(a, 4 + j, (*chip, 1 - c), me).wait_recv()
                for k in range(N_DEV - 1):
                    copy(a, k, me, me).wait_send()
                local(a).wait()

    any_spec = pl.BlockSpec(memory_space=pl.ANY)
    return pl.pallas_call(
        full_body, name=name, grid=grid,
        in_specs=list(in_specs) + [any_spec] * n,
        out_specs=out_specs + (any_spec,) * n,
        out_shape=out_shape + tuple(jax.ShapeDtypeStruct((N_DEV,) + t.shape, t.dtype) for t in shards),
        scratch_shapes=list(scratch_shapes) + [pltpu.SemaphoreType.DMA((n, N_DEV - 1)),
                                               pltpu.SemaphoreType.DMA((n, N_DEV - 1)),
                                               pltpu.SemaphoreType.DMA((n,))],
        compiler_params=_params(has_side_effects=True),
    )


def _call_with_exchange(body, rides, first_step, last_step, *, name, grid, in_specs, out_specs, out_shape,
                        scratch_shapes=()):
    out_specs = tuple(out_specs) if isinstance(out_specs, (tuple, list)) else (out_specs,)
    out_shape = tuple(out_shape) if isinstance(out_shape, (tuple, list)) else (out_shape,)
    n_in, n_out, n_scr, n = len(in_specs), len(out_specs), len(scratch_shapes), len(rides)
    if n == 0:
        return pl.pallas_call(body, name=name, grid=grid, in_specs=list(in_specs), out_specs=out_specs,
                              out_shape=out_shape, scratch_shapes=list(scratch_shapes),
                              compiler_params=_params())

    def full_body(*refs):
        ins, srcs = refs[:n_in], refs[n_in:n_in + n]
        outs, lands = refs[n_in + n:n_in + n + n_out], refs[n_in + n + n_out:n_in + 2 * n + n_out]
        scratch = refs[n_in + 2 * n + n_out:n_in + 2 * n + n_out + n_scr]
        send_sems, recv_sems, local_sems = refs[-3:]
        x, y, c = _mesh_pos()
        my_idx = _flat_index((x, y, c))
        peers = _peer_list(x, y, c)

        def remote(a, k):
            return pltpu.make_async_remote_copy(
                src_ref=srcs[a].at[_flat_index(peers[k])], dst_ref=lands[a].at[my_idx],
                send_sem=send_sems.at[a, k], recv_sem=recv_sems.at[a, k],
                device_id=peers[k], device_id_type=MESH)

        def local(a):
            return pltpu.make_async_copy(srcs[a].at[my_idx], lands[a].at[my_idx], local_sems.at[a])

        @pl.when(first_step())
        def _():
            for a in range(n):
                local(a).start()
                for k in range(N_DEV - 1):
                    remote(a, k).start()

        body(*ins, *outs, *scratch)

        @pl.when(last_step())
        def _():
            for a in range(n):
                for k in range(N_DEV - 1):
                    cp = remote(a, k)
                    cp.wait_send()
                    cp.wait_recv()
                local(a).wait()

    any_spec = pl.BlockSpec(memory_space=pl.ANY)
    res = pl.pallas_call(
        full_body, name=name, grid=grid,
        in_specs=list(in_specs) + [any_spec] * n,
        out_specs=out_specs + (any_spec,) * n,
        out_shape=out_shape + tuple(jax.ShapeDtypeStruct(t.shape, t.dtype) for t in rides),
        scratch_shapes=list(scratch_shapes) + [pltpu.SemaphoreType.DMA((n, N_DEV - 1)),
                                               pltpu.SemaphoreType.DMA((n, N_DEV - 1)),
                                               pltpu.SemaphoreType.DMA((n,))],
        compiler_params=_params(has_side_effects=True),
    )
    return res


def _head_norm(t, w128, bd):
    ms = _mm_split(t * t, bd) * (1.0 / HEAD_DIM)
    r = lax.rsqrt(ms + EPS)
    return (t * r) * w128, r


def _attn_in(x2, wn1, a_g, cos2, sin2, qnw, knw, bd, shards):
    s_len = x2.shape[0]
    tm = ROW_TILE

    def body(x_ref, wn_ref, w_ref, cos_ref, sin_ref, qnw_ref, knw_ref, bd_ref,
             h1_ref, qraw_ref, kraw_ref, q_ref, k_ref, va_ref, qs_ref, ks_ref, vs_ref,
             q4_ref, k4_ref, v4_ref, q16_ref, k16_ref, v16_ref, proj, slabs, stage, w_full):
        @pl.when(pl.program_id(0) == 0)
        def _():
            for d in range(N_DEV):
                w_full[:, IN_SHARD * d:IN_SHARD * (d + 1)] = w_ref[d]

        xx = x_ref[...]
        r = lax.rsqrt(jnp.mean(xx * xx, axis=-1, keepdims=True) + EPS)
        h = ((xx * r) * wn_ref[...]).astype(BF16)
        h1_ref[...] = h
        proj[...] = _dot(h, w_full[...])
        cos_t, sin_t, bdm = cos_ref[...], sin_ref[...], bd_ref[...]
        for grp, (raw_ref, rope_ref, nw_ref) in enumerate(((qraw_ref, q_ref, qnw_ref),
                                                           (kraw_ref, k_ref, knw_ref))):
            for p in range(4):
                cols = slice(D_GRP * grp + 128 * p, D_GRP * grp + 128 * (p + 1))
                t = proj[:, cols]
                raw_ref[:, 128 * p:128 * (p + 1)] = t
                yn, _ = _head_norm(t, nw_ref[...], bdm)
                roped = yn * cos_t + _swap_halves(yn) * sin_t
                slabs[4 * grp + p] = roped
                rope_ref[:, 128 * p:128 * (p + 1)] = roped.astype(BF16)
        for p in range(4):
            slabs[8 + p] = proj[:, 2 * D_GRP + 128 * p:2 * D_GRP + 128 * (p + 1)]
        for grp, ref in ((2, va_ref), (3, qs_ref), (4, ks_ref), (5, vs_ref)):
            ref[...] = proj[:, D_GRP * grp:D_GRP * (grp + 1)].astype(BF16)
        _split_views(slabs, stage, (q4_ref, k4_ref, v4_ref), (q16_ref, k16_ref, v16_ref))

    row = lambda w: pl.BlockSpec((tm, w), lambda i: (i, 0))
    grp_bf = jax.ShapeDtypeStruct((s_len, D_GRP), BF16)
    grp_f32 = jax.ShapeDtypeStruct((s_len, D_GRP), F32)
    ni = s_len // tm
    return _call_with_gather(
        body, shards, lambda: pl.program_id(0) == 0, lambda: pl.program_id(0) == ni - 2,
        lambda: pl.program_id(0) == ni - 1,
        name="attn_in", grid=(ni,),
        in_specs=[row(D_MODEL), _full((1, D_MODEL)),
                  pl.BlockSpec((N_DEV, D_MODEL, IN_SHARD), lambda i: (0, 0, 0)),
                  row(128), row(128), _full((1, 128)), _full((1, 128)), _full((128, 128))],
        out_specs=(row(D_MODEL),) + (row(D_GRP),) * 8 + (_view_spec(tm, 4),) * 3 + (_view_spec(tm, 16),) * 3,
        out_shape=(jax.ShapeDtypeStruct((s_len, D_MODEL), BF16), grp_f32, grp_f32) + (grp_bf,) * 6
        + (_view_shape(s_len, 4, BF16),) * 3 + (_view_shape(s_len, 16, BF16),) * 3,
        scratch_shapes=[pltpu.VMEM((tm, D_IN), F32), pltpu.VMEM((12, tm, 128), F32), pltpu.VMEM((12, tm, 128), F32),
                        pltpu.VMEM((D_MODEL, D_IN), BF16)],
    )(x2, wn1, a_g, cos2, sin2, qnw, knw, bd, *shards)


def _band_mask(n):
    i = lax.broadcasted_iota(jnp.int32, (2 * BLOCK, 2 * BLOCK), 0) & (BLOCK - 1)
    j = lax.broadcasted_iota(jnp.int32, (2 * BLOCK, 2 * BLOCK), 1)
    dist = i + BLOCK - j
    return (dist >= 0) & (dist <= BLOCK) & ((n - 1) * BLOCK + j >= 0)


def _stack_heads(t2, head0):
    return jnp.concatenate([jnp.where(head0, t2, 0), jnp.where(head0, 0, t2)], axis=0)


def _unstack_heads(t, head0):
    return jnp.where(head0, t[0:BLOCK], t[BLOCK:2 * BLOCK])


def _dil_fwd(qv, kv, vv, r):
    sub_len = qv.shape[0]
    nb = sub_len // BLOCK

    qb = 2 if nb % 2 == 0 else 1

    def body(q_ref, kp_ref, kc_ref, vp_ref, vc_ref, o_ref, lse_ref):
        n = pl.program_id(1)
        lane = lax.broadcasted_iota(jnp.int32, (BLOCK, 128), 1)
        head0 = lane < HEAD_DIM
        units = [(b, slice(128 * p, 128 * (p + 1))) for b in range(qb) for p in range(4)]
        valid = [_band_mask(qb * n + b) for b in range(qb)]
        rows = [slice(BLOCK * b, BLOCK * (b + 1)) for b in range(qb)]

        def keys(prev_ref, cur_ref, b, c):
            before = prev_ref[:, c] if b == 0 else cur_ref[rows[b - 1], c]
            return jnp.concatenate([before, cur_ref[rows[b], c]], axis=0)

        qqs = [_stack_heads(q_ref[rows[b], c] * ATT_SCALE, head0) for b, c in units]
        kks = [keys(kp_ref, kc_ref, b, c) for b, c in units]
        vvs = [keys(vp_ref, vc_ref, b, c) for b, c in units]
        ss = [_dot_nt(qq, kk) for qq, kk in zip(qqs, kks)]
        prs, dens, lses = [], [], []
        for (b, _), s in zip(units, ss):
            s = jnp.where(valid[b], s, NEG)
            m = jnp.max(s, axis=-1, keepdims=True)
            pr = jnp.exp(s - m)
            den = jnp.sum(pr, axis=-1, keepdims=True)
            prs.append(pr.astype(BF16))
            dens.append(den)
            lses.append(m + jnp.log(den))
        pvs = [_dot(pr, vv2) for pr, vv2 in zip(prs, vvs)]
        for (b, c), pv, den, lse in zip(units, pvs, dens, lses):
            o_ref[rows[b], c] = _unstack_heads(pv / den, head0)
            lse_ref[rows[b], c] = _unstack_heads(jnp.broadcast_to(lse, (2 * BLOCK, 128)), head0)

    cur = pl.BlockSpec((qb * BLOCK, D_GRP), lambda c, n: (n, c))
    prev = pl.BlockSpec((BLOCK, D_GRP), lambda c, n: (jnp.maximum(qb * n - 1, 0), c))
    out = jax.ShapeDtypeStruct(qv.shape, F32)
    return pl.pallas_call(
        body, name=f"dil_fwd_r{r}", grid=(r, nb // qb),
        in_specs=[cur, prev, cur, prev, cur], out_specs=(cur, cur), out_shape=(out, out),
        compiler_params=_params(),
    )(qv, kv, kv, vv, vv)


def _dil_bwd(qv, kv, vv, dov, lsev, deltav, r, rides):
    sub_len = qv.shape[0]
    nb = sub_len // BLOCK

    def body(q_ref, kp_ref, kc_ref, vp_ref, vc_ref, do_ref, lse_ref, dl_ref,
             dq_ref, dk_ref, dv_ref, dk_carry, dv_carry):
        n = pl.program_id(1)

        @pl.when(n == 0)
        def _():
            dk_carry[...] = jnp.zeros_like(dk_carry)
            dv_carry[...] = jnp.zeros_like(dv_carry)

        @pl.when(n < nb)
        def _():
            valid = _band_mask(n)
            lane = lax.broadcasted_iota(jnp.int32, (BLOCK, 128), 1)
            head0 = lane < HEAD_DIM
            pairs = [slice(128 * p, 128 * (p + 1)) for p in range(4)]
            qqs = [_stack_heads(q_ref[:, c] * ATT_SCALE, head0) for c in pairs]
            dos = [_stack_heads(do_ref[:, c], head0) for c in pairs]
            kks = [jnp.concatenate([kp_ref[:, c], kc_ref[:, c]], axis=0) for c in pairs]
            vvs = [jnp.concatenate([vp_ref[:, c], vc_ref[:, c]], axis=0) for c in pairs]
            ss = [_dot_nt(qq, kk) for qq, kk in zip(qqs, kks)]
            dps = [_dot_nt(do, vv2) for do, vv2 in zip(dos, vvs)]
            def softmax_terms(p):
                stats = []
                for ref in (lse_ref, dl_ref):
                    t2 = ref[:, pairs[p]]
                    stats.append(jnp.concatenate(
                        [jnp.sum(jnp.where(lane == 0, t2, 0.0), axis=-1, keepdims=True),
                         jnp.sum(jnp.where(lane == HEAD_DIM, t2, 0.0), axis=-1, keepdims=True)], axis=0))
                pr = jnp.where(valid, jnp.exp(jnp.minimum(ss[p] - stats[0], 0.0)), 0.0)
                return pr.astype(BF16), (pr * (dps[p] - stats[1])).astype(BF16)

            terms = [softmax_terms(p) for p in range(4)]
            dqs = [_dot(terms[p][1], kks[p]) for p in range(4)]
            dkks = [_dot_tn(terms[p][1], qqs[p]) for p in range(4)]
            dvvs = [_dot_tn(terms[p][0], dos[p]) for p in range(4)]
            for c, dq, dkk, dvv in zip(pairs, dqs, dkks, dvvs):
                dq_ref[:, c] = _unstack_heads(dq, head0) * ATT_SCALE
                dk_ref[:, c] = dk_carry[:, c] + dkk[:BLOCK]
                dv_ref[:, c] = dv_carry[:, c] + dvv[:BLOCK]
                dk_carry[:, c] = dkk[BLOCK:]
                dv_carry[:, c] = dvv[BLOCK:]

        @pl.when(n == nb)
        def _():
            dk_ref[...] = dk_carry[...]
            dv_ref[...] = dv_carry[...]

    last = nb - 1
    cur = pl.BlockSpec((BLOCK, D_GRP), lambda c, n: (jnp.minimum(n, last), c))
    prev = pl.BlockSpec((BLOCK, D_GRP), lambda c, n: (jnp.clip(n - 1, 0, last), c))
    out = jax.ShapeDtypeStruct(qv.shape, F32)
    return _call_with_exchange(
        body, rides,
        lambda: jnp.logical_and(pl.program_id(0) == 0, pl.program_id(1) == 0),
        lambda: jnp.logical_and(pl.program_id(0) == r - 1, pl.program_id(1) == nb),
        name=f"dil_bwd_r{r}", grid=(r, nb + 1),
        in_specs=[cur, prev, cur, prev, cur, cur, cur, cur],
        out_specs=(cur, prev, prev), out_shape=(out, out, out),
        scratch_shapes=[pltpu.VMEM((BLOCK, D_GRP), F32), pltpu.VMEM((BLOCK, D_GRP), F32)],
    )(qv, kv, kv, vv, vv, dov, lsev, deltav, *rides)


def _sb_fwd(qs, ks, vs, tri_suf, shards):
    s_len = qs.shape[0]
    t = SB_TILE
    nq = s_len // t

    npair = SB_PAIRS

    def body(q_ref, k_ref, v_ref, u_ref, o_ref, c_ref, qq, vt, acc, cf, csave):
        row = lax.broadcasted_iota(jnp.int32, (2 * t, t), 0) & (t - 1)
        col = lax.broadcasted_iota(jnp.int32, (2 * t, t), 1)
        diag_mask = col < row
        lane1 = lax.broadcasted_iota(jnp.int32, (t, 128), 1)
        head0 = lane1 < HEAD_DIM
        lane2 = lax.broadcasted_iota(jnp.int32, (2 * t, 128), 1)
        uu = u_ref[...]
        pr = range(npair)
        cols = [slice(128 * pp, 128 * (pp + 1)) for pp in pr]

        i = pl.program_id(1)

        @pl.when(i == 0)
        def _():
            def transpose_v(j, _):
                rows = pl.ds(pl.multiple_of(j * t, t), t)
                for pp in pr:
                    vt[pp, j] = v_ref[rows, cols[pp]].astype(F32).T.astype(BF16)
                return 0

            lax.fori_loop(0, nq, transpose_v, 0)

        for pp in pr:
            q2 = q_ref[:, cols[pp]] * ATT_SCALE
            qq[pp, 0:t, :] = jnp.where(head0, q2, 0)
            qq[pp, t:2 * t, :] = jnp.where(head0, 0, q2)
        acc[...] = jnp.zeros_like(acc)
        cf[...] = jnp.zeros_like(cf)
        csave[...] = jnp.full(csave.shape, 2.0 * SB_DEAD, F32)

        def tile(kb, diag):
            krows = pl.ds(pl.multiple_of(kb * t, t), t)
            zs = [_dot_nt(qq[pp], k_ref[krows, cols[pp]]) for pp in pr]
            lbk = [_log_sigmoid_pair(z) for z in zs]
            lks = [jnp.where(diag_mask, lk, 0.0) if diag else lk for _, lk in lbk]
            sufs = [_cumsum_mm(lk, uu) for lk in lks]
            carries = [cf[pp] for pp in pr]
            avs = []
            for pp in pr:
                a = jnp.exp(lbk[pp][0] + (sufs[pp] + jnp.concatenate([carries[pp]] * (t // 128), axis=1)))
                avs.append((jnp.where(diag_mask, a, 0.0) if diag else a).astype(BF16))
            pvs = [_dot_nt(vt[pp, kb], avs[pp]) for pp in pr]
            for pp in pr:
                acc[pp] += pvs[pp]
                csave[pp] = jnp.where(lane2 == kb, carries[pp], csave[pp])
                cf[pp] = carries[pp] + jnp.broadcast_to(jnp.sum(lks[pp], axis=-1, keepdims=True), (2 * t, 128))

        tile(i, True)

        def alive():
            return jnp.max(cf[...]) > SB_DEAD

        def k_block(state):
            kb, _ = state
            tile(kb, False)
            return kb - 1, alive()

        lax.while_loop(lambda state: jnp.logical_and(state[0] >= 0, state[1]), k_block, (i - 1, alive()))
        for pp in pr:
            o_ref[:, cols[pp]] = jnp.where(head0, acc[pp, :, 0:t].T, acc[pp, :, t:2 * t].T)
            c_ref[2 * pp] = csave[pp, 0:t, :]
            c_ref[2 * pp + 1] = csave[pp, t:2 * t, :]

    width = 128 * npair
    kv = pl.BlockSpec((s_len, width), lambda p, i: (0, p))
    qo = pl.BlockSpec((t, width), lambda p, i: (i, p))
    steps = 4 // npair

    def at(p, i):
        return lambda: jnp.logical_and(pl.program_id(0) == p, pl.program_id(1) == i)

    return _call_with_gather(
        body, shards, at(0, 0), at(steps - 1, (2 * nq) // 3), at(steps - 1, nq - 1),
        name="sb_fwd", grid=(steps, nq),
        in_specs=[qo, kv, kv, pl.BlockSpec((t, t), lambda p, i: (0, 0))],
        out_specs=(qo, pl.BlockSpec((2 * npair, t, 128), lambda p, i: (p, i, 0))),
        out_shape=(jax.ShapeDtypeStruct((s_len, D_GRP), F32),
                   jax.ShapeDtypeStruct((8, s_len, 128), F32)),
        scratch_shapes=[pltpu.VMEM((npair, 2 * t, 128), BF16), pltpu.VMEM((npair, nq, 128, t), BF16),
                        pltpu.VMEM((npair, 128, 2 * t), F32),
                        pltpu.VMEM((npair, 2 * t, 128), F32), pltpu.VMEM((npair, 2 * t, 128), F32)],
    )(qs, ks, vs, tri_suf, *shards)


def _sb_bwd(qs, ks, vs, dos, csaved, tri_suf, tri_pre, rides):
    s_len = qs.shape[0]
    t = SB_TILE
    nq = s_len // t

    npair = SB_BWD_PAIRS

    def body(q_ref, k_ref, v_ref, do_ref, c_ref, u_ref, p_ref, dq_ref, dk_ref, dv_ref,
             qq, dd, qqt, ddt, kt, dq_acc, dkt, dvt, cg):
        row = lax.broadcasted_iota(jnp.int32, (2 * t, t), 0) & (t - 1)
        col = lax.broadcasted_iota(jnp.int32, (2 * t, t), 1)
        diag_mask = col < row
        lane1 = lax.broadcasted_iota(jnp.int32, (t, 128), 1)
        head0 = lane1 < HEAD_DIM
        lane2 = lax.broadcasted_iota(jnp.int32, (2 * t, 128), 1)
        uu, pm = u_ref[...], p_ref[...]
        pr = range(npair)
        cols = [slice(128 * pp, 128 * (pp + 1)) for pp in pr]
        i = pl.program_id(1)

        @pl.when(i == 0)
        def _():
            dkt[...] = jnp.zeros_like(dkt)
            dvt[...] = jnp.zeros_like(dvt)

            def transpose_k(j, _):
                rows = pl.ds(pl.multiple_of(j * t, t), t)
                for pp in pr:
                    kt[pp, j] = k_ref[rows, cols[pp]].astype(F32).T.astype(BF16)
                return 0

            lax.fori_loop(0, nq, transpose_k, 0)

        for pp in pr:
            q2 = q_ref[:, cols[pp]].astype(F32) * ATT_SCALE
            do2 = do_ref[:, cols[pp]].astype(F32)
            for src, nat, tr in ((q2, qq, qqt), (do2, dd, ddt)):
                stacked = jnp.concatenate([jnp.where(head0, src, 0.0), jnp.where(head0, 0.0, src)], axis=0)
                nat[pp] = stacked.astype(BF16)
                tr[pp] = stacked.T.astype(BF16)
        dq_acc[...] = jnp.zeros_like(dq_acc)
        cg[...] = jnp.zeros_like(cg)

        def tile(kb, diag):
            krows = pl.ds(pl.multiple_of(kb * t, t), t)
            zs = [_dot_nt(qq[pp], k_ref[krows, cols[pp]]) for pp in pr]
            das = [_dot_nt(dd[pp], v_ref[krows, cols[pp]]) for pp in pr]
            lbk = [_log_sigmoid_pair(z) for z in zs]
            lks = [jnp.where(diag_mask, lk, 0.0) if diag else lk for _, lk in lbk]
            sufs = [_cumsum_mm(lk, uu) for lk in lks]
            avs, gs = [], []
            for pp in pr:
                cs = jnp.concatenate([c_ref[2 * pp], c_ref[2 * pp + 1]], axis=0)
                cf = jnp.sum(jnp.where(lane2 == kb, cs, 0.0), axis=-1, keepdims=True)
                a = jnp.exp(lbk[pp][0] + (sufs[pp] + cf))
                a = jnp.where(diag_mask, a, 0.0) if diag else a
                avs.append(a.astype(BF16))
                gs.append(a * das[pp])
            gpres = [_cumsum_mm(g, pm) for g in gs]
            dzs = []
            for pp in pr:
                carry = cg[pp]
                beta = jnp.exp(lbk[pp][0])
                dz = gs[pp] - beta * (gs[pp] + (gpres[pp] + jnp.concatenate([carry] * (t // 128), axis=1)))
                dzs.append((jnp.where(diag_mask, dz, 0.0) if diag else dz).astype(BF16))
                cg[pp] = carry + jnp.broadcast_to(jnp.sum(gs[pp], axis=-1, keepdims=True), (2 * t, 128))
            dqs = [_dot_nt(kt[pp, kb], dzs[pp]) for pp in pr]
            dks = [_dot(qqt[pp], dzs[pp]) for pp in pr]
            dvs = [_dot(ddt[pp], avs[pp]) for pp in pr]
            for pp in pr:
                dq_acc[pp] += dqs[pp]
                dkt[pp, kb] += dks[pp]
                dvt[pp, kb] += dvs[pp]

        def k_block(kb, _):
            tile(kb, False)
            return 0

        col_max = jnp.max(jnp.max(c_ref[...], axis=0), axis=0, keepdims=True)
        lane_row = lax.broadcasted_iota(jnp.int32, (1, 128), 1)
        n_live = jnp.sum(jnp.where(jnp.logical_and(col_max > SB_DEAD, lane_row < i), 1, 0))
        lax.fori_loop(i - n_live, i, k_block, 0)
        tile(i, True)
        for pp in pr:
            dq_ref[:, cols[pp]] = jnp.where(head0, dq_acc[pp, :, 0:t].T, dq_acc[pp, :, t:2 * t].T) * ATT_SCALE

        @pl.when(i == nq - 1)
        def _():
            def untranspose(j, _):
                rows = pl.ds(pl.multiple_of(j * t, t), t)
                for pp in pr:
                    dk_ref[rows, cols[pp]] = dkt[pp, j].T
                    dv_ref[rows, cols[pp]] = dvt[pp, j].T
                return 0

            lax.fori_loop(0, nq, untranspose, 0)

    width = 128 * npair
    kv = pl.BlockSpec((s_len, width), lambda p, i: (0, p))
    qo = pl.BlockSpec((t, width), lambda p, i: (i, p))
    tri = pl.BlockSpec((t, t), lambda p, i: (0, 0))
    out = jax.ShapeDtypeStruct((s_len, D_GRP), F32)
    steps = 4 // npair
    return _call_with_exchange(
        body, rides,
        lambda: jnp.logical_and(pl.program_id(0) == 0, pl.program_id(1) == 0),
        lambda: jnp.logical_and(pl.program_id(0) == steps - 1, pl.program_id(1) == nq - 1),
        name="sb_bwd", grid=(steps, nq),
        in_specs=[qo, kv, kv, qo, pl.BlockSpec((2 * npair, t, 128), lambda p, i: (p, i, 0)), tri, tri],
        out_specs=(qo, kv, kv), out_shape=(out, out, out),
        scratch_shapes=[pltpu.VMEM((npair, 2 * t, 128), BF16), pltpu.VMEM((npair, 2 * t, 128), BF16),
                        pltpu.VMEM((npair, 128, 2 * t), BF16), pltpu.VMEM((npair, 128, 2 * t), BF16),
                        pltpu.VMEM((npair, nq, 128, t), BF16),
                        pltpu.VMEM((npair, 128, 2 * t), F32),
                        pltpu.VMEM((npair, nq, 128, t), F32), pltpu.VMEM((npair, nq, 128, t), F32),
                        pltpu.VMEM((npair, 2 * t, 128), F32)],
    )(qs, ks, vs, dos, csaved, tri_suf, tri_pre, *rides)


def _attn_out(o_b, lse_b, o_sb, x2, wdil, wsb, out_g, shards):
    s_len = x2.shape[0]
    tm = ROW_TILE

    def body(o1_ref, l1_ref, o4_ref, l4_ref, o16_ref, l16_ref, osb_ref, x_ref, wdil_ref, wsb_ref, w_ref,
             odil_ref, lse_ref, lse4_ref, lse16_ref, mixed_ref, x1_ref, stage, nat4, nat16):
        _merge_views((o4_ref, l4_ref), (o16_ref, l16_ref), stage, nat4, nat16)
        os_ = (o1_ref[...], _slab_group(nat4, 0), _slab_group(nat16, 0))
        ls = (l1_ref[...], _slab_group(nat4, 1), _slab_group(nat16, 1))
        mx = jnp.maximum(jnp.maximum(ls[0], ls[1]), ls[2])
        es = [jnp.exp(l - mx) for l in ls]
        den = es[0] + es[1] + es[2]
        o_dil = (es[0] * os_[0] + es[1] * os_[1] + es[2] * os_[2]) / den
        odil_ref[...] = o_dil
        lse = mx + jnp.log(den)
        lse_ref[...] = lse
        for p in range(4):
            nat4[p] = lse[:, 128 * p:128 * (p + 1)]
        _split_views(nat4.at[0:4], stage.at[0:4], (lse4_ref,), (lse16_ref,))
        halves = []
        for t, w_r in ((o_dil, wdil_ref), (osb_ref[...], wsb_ref)):
            r = lax.rsqrt(jnp.mean(t * t, axis=-1, keepdims=True) + EPS)
            halves.append(((t * r) * w_r[...]).astype(BF16))
        mixed = jnp.concatenate(halves, axis=1)
        mixed_ref[...] = mixed
        w = w_ref[...].reshape(D_MODEL, D_MODEL)
        x1_ref[...] = x_ref[...] + _dot(mixed, w)

    row = lambda w: pl.BlockSpec((tm, w), lambda i: (i, 0))
    ni = s_len // tm
    return _call_with_gather(
        body, shards, lambda: pl.program_id(0) == 0, lambda: pl.program_id(0) == ni - 2,
        lambda: pl.program_id(0) == ni - 1,
        name="attn_out", grid=(ni,),
        in_specs=[row(D_GRP)] * 2 + [_view_spec(tm, 4)] * 2 + [_view_spec(tm, 16)] * 2
        + [row(D_GRP), row(D_MODEL), _full((1, D_GRP)), _full((1, D_GRP)), _full((N_DEV, OUT_SHARD, D_MODEL))],
        out_specs=(row(D_GRP), row(D_GRP), _view_spec(tm, 4), _view_spec(tm, 16), row(D_MODEL), row(D_MODEL)),
        out_shape=(jax.ShapeDtypeStruct((s_len, D_GRP), F32), jax.ShapeDtypeStruct((s_len, D_GRP), F32),
                   _view_shape(s_len, 4, F32), _view_shape(s_len, 16, F32),
                   jax.ShapeDtypeStruct((s_len, D_MODEL), BF16), jax.ShapeDtypeStruct((s_len, D_MODEL), F32)),
        scratch_shapes=[pltpu.VMEM((8, tm, 128), F32)] * 3,
    )(o_b[0], lse_b[0], o_b[1], lse_b[1], o_b[2], lse_b[2], o_sb, x2, wdil, wsb, out_g, *shards)


def _two_shards(w_ref):
    return w_ref[...].reshape(FF_BLOCK, D_MODEL)


def _ffn_fwd(x1, wn2, tgt, gate_g, up_g, down_g):
    s_len = x1.shape[0]
    tm = ROW_TILE
    ni = s_len // tm

    def body(x_ref, wn_ref, t_ref, wg_ref, wu_ref, wd_ref, g_ref, u_ref, h2_ref, dy_ref, loss_ref, acc):
        j = pl.program_id(1)

        @pl.when(j == 0)
        def _():
            xx = x_ref[...]
            r = lax.rsqrt(jnp.mean(xx * xx, axis=-1, keepdims=True) + EPS)
            h2_ref[...] = ((xx * r) * wn_ref[...]).astype(BF16)
            acc[...] = jnp.zeros_like(acc)

        h = h2_ref[...]
        g = _dot_nt(h, _two_shards(wg_ref))
        u = _dot_nt(h, _two_shards(wu_ref))
        g_ref[...] = g
        u_ref[...] = u
        act = (g * (1.0 / (1.0 + jnp.exp(-g)))) * u
        acc[...] += _dot(act.astype(BF16), _two_shards(wd_ref))

        @pl.when(j == FF_STEPS - 1)
        def _():
            err = (x_ref[...] + acc[...]) - t_ref[...]
            dy_ref[...] = err * (1.0 / D_MODEL)
            part = 0.5 * jnp.sum(jnp.mean(err * err, axis=-1, keepdims=True))
            loss_ref[...] = jnp.full((8, 128), part, F32)

    row = pl.BlockSpec((tm, D_MODEL), lambda i, j: (i, 0))
    hid = pl.BlockSpec((tm, FF_BLOCK), lambda i, j: (i, j))
    return pl.pallas_call(
        body, name="ffn_fwd", grid=(ni, FF_STEPS),
        in_specs=[row, pl.BlockSpec((1, D_MODEL), lambda i, j: (0, 0)), row,
                  pl.BlockSpec((2, FF_PAD, D_MODEL), lambda i, j: (j, 0, 0)),
                  pl.BlockSpec((2, FF_PAD, D_MODEL), lambda i, j: (j, 0, 0)),
                  pl.BlockSpec((2, FF_PAD, D_MODEL), lambda i, j: (j, 0, 0))],
        out_specs=(hid, hid, row, row, pl.BlockSpec((8, 128), lambda i, j: (i, 0))),
        out_shape=(jax.ShapeDtypeStruct((s_len, N_DEV * FF_PAD), F32),
                   jax.ShapeDtypeStruct((s_len, N_DEV * FF_PAD), F32),
                   jax.ShapeDtypeStruct((s_len, D_MODEL), BF16),
                   jax.ShapeDtypeStruct((s_len, D_MODEL), F32),
                   jax.ShapeDtypeStruct((ni * 8, 128), F32)),
        scratch_shapes=[pltpu.VMEM((tm, D_MODEL), F32)],
        compiler_params=_params(),
    )(x1, wn2, tgt, gate_g, up_g, down_g)


def _ffn_bwd_dx(dy, g, u, gate_g, up_g, down_g):
    s_len = dy.shape[0]
    tm = ROW_TILE

    def body(dy_ref, g_ref, u_ref, wg_ref, wu_ref, wd_ref, dg_ref, du_ref, act_ref, dh_ref, acc):
        j = pl.program_id(1)

        @pl.when(j == 0)
        def _():
            acc[...] = jnp.zeros_like(acc)

        halves = [slice(0, tm // 2), slice(tm // 2, tm)]
        wd, wg, wu = _two_shards(wd_ref), _two_shards(wg_ref), _two_shards(wu_ref)
        das = [_dot_nt(dy_ref[rows, :].astype(BF16), wd) for rows in halves]

        def elementwise(rows, da):
            gg, uu = g_ref[rows, :], u_ref[rows, :]
            sig = 1.0 / (1.0 + jnp.exp(-gg))
            silu = gg * sig
            act_ref[rows, :] = (silu * uu).astype(BF16)
            du = (da * silu).astype(BF16)
            dg = (da * uu * (sig * (1.0 + gg * (1.0 - sig)))).astype(BF16)
            du_ref[rows, :] = du
            dg_ref[rows, :] = dg
            return dg, du

        dg0, du0 = elementwise(halves[0], das[0])
        acc[halves[0], :] += _dot(dg0, wg) + _dot(du0, wu)
        dg1, du1 = elementwise(halves[1], das[1])
        acc[halves[1], :] += _dot(dg1, wg) + _dot(du1, wu)

        @pl.when(j == FF_STEPS - 1)
        def _():
            dh_ref[...] = acc[...]

    row = pl.BlockSpec((tm, D_MODEL), lambda i, j: (i, 0))
    hid = pl.BlockSpec((tm, FF_BLOCK), lambda i, j: (i, j))
    hid_bf = jax.ShapeDtypeStruct((s_len, N_DEV * FF_PAD), BF16)
    return pl.pallas_call(
        body, name="ffn_bwd_dx", grid=(s_len // tm, FF_STEPS),
        in_specs=[row, hid, hid,
                  pl.BlockSpec((2, FF_PAD, D_MODEL), lambda i, j: (j, 0, 0)),
                  pl.BlockSpec((2, FF_PAD, D_MODEL), lambda i, j: (j, 0, 0)),
                  pl.BlockSpec((2, FF_PAD, D_MODEL), lambda i, j: (j, 0, 0))],
        out_specs=(hid, hid, hid, row),
        out_shape=(hid_bf, hid_bf, hid_bf, jax.ShapeDtypeStruct((s_len, D_MODEL), F32)),
        scratch_shapes=[pltpu.VMEM((tm, D_MODEL), F32)],
        compiler_params=_params(),
    )(dy, g, u, gate_g, up_g, down_g)


def _ffn_bwd_dw(h2, dy, dg, du, act):
    s_len = h2.shape[0]
    tm = DW_ROW_TILE
    ni = s_len // tm

    def body(h_ref, dy_ref, dg_ref, du_ref, act_ref, dwg_ref, dwu_ref, dwd_ref, ag, au, ad):
        i = pl.program_id(1)

        @pl.when(i == 0)
        def _():
            ag[...] = jnp.zeros_like(ag)
            au[...] = jnp.zeros_like(au)
            ad[...] = jnp.zeros_like(ad)

        h = h_ref[...]
        ag[...] += _dot_tn(dg_ref[...], h)
        au[...] += _dot_tn(du_ref[...], h)
        ad[...] += _dot_tn(act_ref[...], dy_ref[...].astype(BF16))

        @pl.when(i == ni - 1)
        def _():
            for acc_ref, out_ref in ((ag, dwg_ref), (au, dwu_ref), (ad, dwd_ref)):
                out_ref[...] = acc_ref[...].astype(BF16).reshape(2, FF_PAD, D_MODEL)

    row = pl.BlockSpec((tm, D_MODEL), lambda j, i: (i, 0))
    hid = pl.BlockSpec((tm, FF_BLOCK), lambda j, i: (i, j))
    row_w = pl.BlockSpec((2, FF_PAD, D_MODEL), lambda j, i: (j, 0, 0))
    grad = jax.ShapeDtypeStruct((N_DEV, FF_PAD, D_MODEL), BF16)
    return pl.pallas_call(
        body, name="ffn_bwd_dw", grid=(FF_STEPS, ni),
        in_specs=[row, row, hid, hid, hid], out_specs=(row_w, row_w, row_w),
        out_shape=(grad, grad, grad),
        scratch_shapes=[pltpu.VMEM((FF_BLOCK, D_MODEL), F32)] * 3,
        compiler_params=_params(),
    )(h2, dy, dg, du, act)


def _rms_bwd(dy, t, w):
    r = lax.rsqrt(jnp.mean(t * t, axis=-1, keepdims=True) + EPS)
    gw = dy * w
    dt = r * (gw - t * ((r * r) * jnp.mean(gw * t, axis=-1, keepdims=True)))
    return dt, dy * t * r


def _attn_out_bwd(dy, dh2, x1, wn2, b_g, mixed, o_dil, o_sb, wdil, wsb, bd512):
    s_len = dy.shape[0]
    tm = ROW_TILE
    ni = s_len // tm

    def body(dy_ref, dh_ref, x1_ref, wn_ref, w_ref, mixed_ref, odil_ref, osb_ref, wdil_ref, wsb_ref, bd_ref,
             dx1_ref, dodil_ref, delta_ref, dosb_ref, dwout_ref, dwn_ref, dwdil_ref, dwsb_ref,
             do4_ref, dl4_ref, do16_ref, dl16_ref, wacc, both, stage):
        i = pl.program_id(0)

        @pl.when(i == 0)
        def _():
            wacc[...] = jnp.zeros_like(wacc)
            dwn_ref[...] = jnp.zeros_like(dwn_ref)
            dwdil_ref[...] = jnp.zeros_like(dwdil_ref)
            dwsb_ref[...] = jnp.zeros_like(dwsb_ref)

        dnorm, dw_rows = _rms_bwd(dh_ref[...], x1_ref[...], wn_ref[...])
        dx1 = dy_ref[...] + dnorm
        dx1_ref[...] = dx1
        dwn_ref[...] += jnp.sum(dw_rows, axis=0, keepdims=True)
        dx1b = dx1.astype(BF16)
        w = w_ref[...].reshape(D_MODEL, D_MODEL)
        dmixed = _dot_nt(dx1b, w)
        wacc[...] += _dot_tn(mixed_ref[...], dx1b)
        o_dil = odil_ref[...]
        d_odil, dw_rows = _rms_bwd(dmixed[:, :D_GRP], o_dil, wdil_ref[...])
        dwdil_ref[...] += jnp.sum(dw_rows, axis=0, keepdims=True)
        dodil_ref[...] = d_odil.astype(BF16)
        delta = _mm_split(d_odil * o_dil, bd_ref[...])
        delta_ref[...] = delta
        for p in range(4):
            both[p] = d_odil[:, 128 * p:128 * (p + 1)]
            both[4 + p] = delta[:, 128 * p:128 * (p + 1)]
        _split_views(both, stage, (do4_ref, dl4_ref), (do16_ref, dl16_ref))
        d_osb, dw_rows = _rms_bwd(dmixed[:, D_GRP:], osb_ref[...], wsb_ref[...])
        dwsb_ref[...] += jnp.sum(dw_rows, axis=0, keepdims=True)
        dosb_ref[...] = d_osb.astype(BF16)

        @pl.when(i == ni - 1)
        def _():
            dwout_ref[...] = wacc[...].astype(BF16).reshape(N_DEV, OUT_SHARD, D_MODEL)

    row = lambda w: pl.BlockSpec((tm, w), lambda i: (i, 0))
    return pl.pallas_call(
        body, name="attn_out_bwd", grid=(ni,),
        in_specs=[row(D_MODEL), row(D_MODEL), row(D_MODEL), _full((1, D_MODEL)),
                  _full((N_DEV, OUT_SHARD, D_MODEL)),
                  row(D_MODEL), row(D_GRP), row(D_GRP), _full((1, D_GRP)), _full((1, D_GRP)),
                  _full((D_GRP, D_GRP))],
        out_specs=(row(D_MODEL), row(D_GRP), row(D_GRP), row(D_GRP),
                   _full((N_DEV, OUT_SHARD, D_MODEL)), _full((1, D_MODEL)), _full((1, D_GRP)), _full((1, D_GRP)),
                   _view_spec(tm, 4), _view_spec(tm, 4), _view_spec(tm, 16), _view_spec(tm, 16)),
        out_shape=(jax.ShapeDtypeStruct((s_len, D_MODEL), F32), jax.ShapeDtypeStruct((s_len, D_GRP), BF16),
                   jax.ShapeDtypeStruct((s_len, D_GRP), F32), jax.ShapeDtypeStruct((s_len, D_GRP), BF16),
                   jax.ShapeDtypeStruct((N_DEV, OUT_SHARD, D_MODEL), BF16),
                   jax.ShapeDtypeStruct((1, D_MODEL), F32), jax.ShapeDtypeStruct((1, D_GRP), F32),
                   jax.ShapeDtypeStruct((1, D_GRP), F32),
                   _view_shape(s_len, 4, BF16), _view_shape(s_len, 4, F32),
                   _view_shape(s_len, 16, BF16), _view_shape(s_len, 16, F32)),
        scratch_shapes=[pltpu.VMEM((D_MODEL, D_MODEL), F32), pltpu.VMEM((8, tm, 128), F32),
                        pltpu.VMEM((8, tm, 128), F32)],
        compiler_params=_params(),
    )(dy, dh2, x1, wn2, b_g, mixed, o_dil, o_sb, wdil, wsb, bd512)


def _qkv_bwd(dq_b, dk_b, dv_b, dqs, dks, dvs, qraw, kraw, cos2, sin2, qnw, knw, bd):
    s_len = qraw.shape[0]
    tm = ROW_TILE
    ni = s_len // tm

    def body(dq1, dk1, dv1, dq4, dk4, dv4, dq16, dk16, dv16, dqs_ref, dks_ref, dvs_ref,
             qraw_ref, kraw_ref, cos_ref, sin_ref, qnw_ref, knw_ref, bd_ref,
             dproj_ref, dqn_ref, dkn_ref, stage, nat4, nat16):
        i = pl.program_id(0)

        @pl.when(i == 0)
        def _():
            dqn_ref[...] = jnp.zeros_like(dqn_ref)
            dkn_ref[...] = jnp.zeros_like(dkn_ref)

        _merge_views((dq4, dk4, dv4), (dq16, dk16, dv16), stage, nat4, nat16)
        cos_t, sin_t, bdm = cos_ref[...], sin_ref[...], bd_ref[...]
        for grp, (part1, raw_ref, nw_ref, dn_ref) in enumerate(((dq1, qraw_ref, qnw_ref, dqn_ref),
                                                                (dk1, kraw_ref, knw_ref, dkn_ref))):
            dn_acc = 0.0
            for p in range(4):
                cols = slice(128 * p, 128 * (p + 1))
                d_rope = part1[:, cols] + nat4[4 * grp + p] + nat16[4 * grp + p]
                d_norm = d_rope * cos_t + _swap_halves(d_rope * sin_t)
                t = raw_ref[:, cols]
                w = nw_ref[...]
                r = lax.rsqrt(_mm_split(t * t, bdm) * (1.0 / HEAD_DIM) + EPS)
                gw = d_norm * w
                corr = _mm_split(gw * t, bdm) * (1.0 / HEAD_DIM)
                dt = r * (gw - t * ((r * r) * corr))
                dn_acc = dn_acc + jnp.sum(d_norm * t * r, axis=0, keepdims=True)
                dproj_ref[:, D_GRP * grp + 128 * p:D_GRP * grp + 128 * (p + 1)] = dt.astype(BF16)
            dn_ref[...] += dn_acc
        dproj_ref[:, 2 * D_GRP:3 * D_GRP] = (dv1[...] + _slab_group(nat4, 2) + _slab_group(nat16, 2)).astype(BF16)
        dproj_ref[:, 3 * D_GRP:4 * D_GRP] = dqs_ref[...].astype(BF16)
        dproj_ref[:, 4 * D_GRP:5 * D_GRP] = dks_ref[...].astype(BF16)
        dproj_ref[:, 5 * D_GRP:6 * D_GRP] = dvs_ref[...].astype(BF16)

    row = lambda w: pl.BlockSpec((tm, w), lambda i: (i, 0))
    return pl.pallas_call(
        body, name="qkv_bwd", grid=(ni,),
        in_specs=[row(D_GRP)] * 3 + [_view_spec(tm, 4)] * 3 + [_view_spec(tm, 16)] * 3 + [row(D_GRP)] * 5
        + [row(128), row(128), _full((1, 128)), _full((1, 128)), _full((128, 128))],
        out_specs=(row(D_IN), _full((1, 128)), _full((1, 128))),
        out_shape=(jax.ShapeDtypeStruct((s_len, D_IN), BF16), jax.ShapeDtypeStruct((1, 128), F32),
                   jax.ShapeDtypeStruct((1, 128), F32)),
        scratch_shapes=[pltpu.VMEM((12, tm, 128), F32)] * 3,
        compiler_params=_params(),
    )(dq_b[0], dk_b[0], dv_b[0], dq_b[1], dk_b[1], dv_b[1], dq_b[2], dk_b[2], dv_b[2],
      dqs, dks, dvs, qraw, kraw, cos2, sin2, qnw, knw, bd)


def _in_bwd_dx(dproj, a_g, x2, dx1, wn1, rides):
    s_len = x2.shape[0]
    tm = ROW_TILE
    ni = s_len // tm

    def body(dp_ref, w_ref, x_ref, dx1_ref, wn_ref, gx_ref, dwn_ref, w_full):
        i = pl.program_id(0)

        @pl.when(i == 0)
        def _():
            dwn_ref[...] = jnp.zeros_like(dwn_ref)
            for d in range(N_DEV):
                w_full[:, IN_SHARD * d:IN_SHARD * (d + 1)] = w_ref[d]

        dh = _dot_nt(dp_ref[...], w_full[...])
        dnorm, dw_rows = _rms_bwd(dh, x_ref[...], wn_ref[...])
        gx_ref[...] = dx1_ref[...] + dnorm
        dwn_ref[...] += jnp.sum(dw_rows, axis=0, keepdims=True)

    row = lambda w: pl.BlockSpec((tm, w), lambda i: (i, 0))
    return _call_with_exchange(
        body, rides, lambda: pl.program_id(0) == 0, lambda: pl.program_id(0) == ni - 1,
        name="in_bwd_dx", grid=(ni,),
        in_specs=[row(D_IN), pl.BlockSpec((N_DEV, D_MODEL, IN_SHARD), lambda i: (0, 0, 0)),
                  row(D_MODEL), row(D_MODEL), _full((1, D_MODEL))],
        out_specs=(row(D_MODEL), _full((1, D_MODEL))),
        out_shape=(jax.ShapeDtypeStruct((s_len, D_MODEL), F32), jax.ShapeDtypeStruct((1, D_MODEL), F32)),
        scratch_shapes=[pltpu.VMEM((D_MODEL, D_IN), BF16)],
    )(dproj, a_g, x2, dx1, wn1, *rides)


def _in_bwd_dw(h1, dproj):
    s_len = h1.shape[0]
    tm = DW_ROW_TILE
    ni = s_len // tm

    def body(h_ref, dp_ref, dw_ref, acc):
        i = pl.program_id(1)

        @pl.when(i == 0)
        def _():
            acc[...] = jnp.zeros_like(acc)

        acc[...] += _dot_tn(h_ref[...], dp_ref[...])

        @pl.when(i == ni - 1)
        def _():
            for half in range(2):
                dw_ref[half] = acc[:, IN_SHARD * half:IN_SHARD * (half + 1)].astype(BF16)

    return pl.pallas_call(
        body, name="in_bwd_dw", grid=(N_DEV // 2, ni),
        in_specs=[pl.BlockSpec((tm, D_MODEL), lambda d, i: (i, 0)),
                  pl.BlockSpec((tm, 2 * IN_SHARD), lambda d, i: (i, d))],
        out_specs=pl.BlockSpec((2, D_MODEL, IN_SHARD), lambda d, i: (d, 0, 0)),
        out_shape=jax.ShapeDtypeStruct((N_DEV, D_MODEL, IN_SHARD), BF16),
        scratch_shapes=[pltpu.VMEM((D_MODEL, 2 * IN_SHARD), F32)],
        compiler_params=_params(),
    )(h1, dproj)


def _adamw(recv, w, m, v):
    rows, cols = w.shape
    tr = next((t for t in (128, 32) if rows % t == 0), rows)

    def body(p_ref, w_ref, m_ref, v_ref, g_ref, d_ref, nm_ref, nv_ref):
        g = p_ref[0].astype(F32)
        for s in range(1, N_DEV):
            g = g + p_ref[s].astype(F32)
        m_new = ADAM_B1 * m_ref[...] + (1.0 - ADAM_B1) * g
        v_new = ADAM_B2 * v_ref[...] + (1.0 - ADAM_B2) * (g * g)
        m_hat = m_new / (1.0 - ADAM_B1 ** ADAM_STEP)
        v_hat = v_new / (1.0 - ADAM_B2 ** ADAM_STEP)
        g_ref[...] = g
        d_ref[...] = -ADAM_LR * (m_hat / (jnp.sqrt(v_hat) + ADAM_EPS) + ADAM_WD * w_ref[...])
        nm_ref[...] = m_new
        nv_ref[...] = v_new

    blk = pl.BlockSpec((tr, cols), lambda i: (i, 0))
    out = jax.ShapeDtypeStruct((rows, cols), F32)
    return pl.pallas_call(
        body, name=f"adamw_{rows}x{cols}", grid=(rows // tr,),
        in_specs=[pl.BlockSpec((N_DEV, tr, cols), lambda i: (0, i, 0)), blk, blk, blk],
        out_specs=(blk,) * 4, out_shape=(out,) * 4,
        compiler_params=_params(),
    )(recv, w, m, v)


def _rope_tables(s_len):
    pos = jnp.arange(s_len, dtype=F32)
    inv_freq = ROPE_THETA ** (-jnp.arange(0, HEAD_DIM, 2, dtype=F32) / HEAD_DIM)
    ang = pos[:, None] * inv_freq[None, :]
    cos, sin = jnp.cos(ang), jnp.sin(ang)
    cos2 = jnp.concatenate([cos, cos, cos, cos], axis=1)
    sin2 = jnp.concatenate([-sin, sin, -sin, sin], axis=1)
    return cos2, sin2


def _block_diag_ones(n):
    i = jnp.arange(n)
    return (i[:, None] // HEAD_DIM == i[None, :] // HEAD_DIM).astype(BF16)


def _pad_cols(t):
    return jnp.pad(t, ((0, 0), (0, FF_PAD - FF_SHARD)))


def _pad_rows(t):
    return jnp.pad(t, ((0, FF_PAD - FF_SHARD), (0, 0)))


LOSS_ROW = 26


def _pack_small(n1, n2, ndil, nsb, nq, nk, scalar=None):
    pad = lambda t: jnp.pad(t.reshape(1, -1), ((0, 0), (0, 128 - t.size)))
    last = jnp.zeros((1, 128), F32) if scalar is None else pad(scalar)
    rows = [n1.reshape(8, 128), n2.reshape(8, 128), ndil.reshape(4, 128), nsb.reshape(4, 128),
            pad(nq), pad(nk), last, jnp.zeros((5, 128), F32)]
    return jnp.concatenate(rows, axis=0)


def _unpack_small(t):
    return (t[0:8].reshape(1, D_MODEL), t[8:16].reshape(1, D_MODEL), t[16:20].reshape(1, D_GRP),
            t[20:24].reshape(1, D_GRP), t[24:25, :HEAD_DIM], t[25:26, :HEAD_DIM])


def kernel(x, attn_norm_w, w_in, q_norm_w, k_norm_w, dil_out_norm_w, sb_out_norm_w, w_out, ffn_norm_w, w_gate, w_up, w_down, loss_target, m_attn_norm_w, m_w_in, m_q_norm_w, m_k_norm_w, m_dil_out_norm_w, m_sb_out_norm_w, m_w_out, m_ffn_norm_w, m_w_gate, m_w_up, m_w_down, v_attn_norm_w, v_w_in, v_q_norm_w, v_k_norm_w, v_dil_out_norm_w, v_sb_out_norm_w, v_w_out, v_ffn_norm_w, v_w_gate, v_w_up, v_w_down):
    s_len = x.shape[1]
    x2, tgt = x[0], loss_target[0]

    (a_g,) = _gather_weights([w_in[0].astype(BF16)])
    gate_loc = _pad_cols(w_gate[0]).T.astype(BF16)
    up_loc = _pad_cols(w_up[0]).T.astype(BF16)
    down_loc = _pad_rows(w_down[0]).astype(BF16)
    out_loc = w_out[0].astype(BF16)

    cos2, sin2 = _rope_tables(s_len)
    bd128, bd512 = _block_diag_ones(128), _block_diag_ones(D_GRP)
    idx = jnp.arange(SB_TILE)
    tri_suf = (idx[:, None] > idx[None, :]).astype(BF16)
    tri_pre = (idx[:, None] < idx[None, :]).astype(BF16)
    qnw2 = jnp.concatenate([q_norm_w, q_norm_w], axis=1)
    knw2 = jnp.concatenate([k_norm_w, k_norm_w], axis=1)

    (h1, qraw, kraw, q, k, va, qs, ks, vs, q4, k4, v4, q16, k16, v16,
     out_g, gate_g) = _attn_in(x2, attn_norm_w, a_g, cos2, sin2, qnw2, knw2, bd128, shards=[out_loc, gate_loc])
    qkv_views = {1: (q, k, va), 4: (q4, k4, v4), 16: (q16, k16, v16)}
    o_b, lse_b = [], []
    for r in DILATIONS:
        o, lse = _dil_fwd(*qkv_views[r], r)
        o_b.append(o)
        lse_b.append(lse)
    o_sb, c_sb, up_g = _sb_fwd(qs, ks, vs, tri_suf, shards=[up_loc])
    o_dil, lse_tot, lse4, lse16, mixed, x1, down_g = _attn_out(
        o_b, lse_b, o_sb, x2, dil_out_norm_w, sb_out_norm_w, out_g, shards=[down_loc])
    g, u, h2, dy, loss_parts = _ffn_fwd(x1, ffn_norm_w, tgt, gate_g, up_g, down_g)
    loss_local = jnp.sum(loss_parts[::8, 0])

    dg, du, act, dh2 = _ffn_bwd_dx(dy, g, u, gate_g, up_g, down_g)
    (dx1, do_dil, delta, do_sb, dwout, dn2, dndil, dnsb, do4, dl4, do16, dl16) = _attn_out_bwd(
        dy, dh2, x1, ffn_norm_w, out_g, mixed, o_dil, o_sb, dil_out_norm_w, sb_out_norm_w, bd512)
    dwg, dwu, dwd = _ffn_bwd_dw(h2, dy, dg, du, act)
    dqs, dks, dvs, r_gate, r_down = _sb_bwd(qs, ks, vs, do_sb, c_sb, tri_suf, tri_pre, rides=[dwg, dwd])
    cot_views = {1: (do_dil, lse_tot, delta), 4: (do4, lse4, dl4), 16: (do16, lse16, dl16)}
    riders = {1: [], 4: [dwout], 16: [dwu]}
    dq_b, dk_b, dv_b, landed = [], [], [], {}
    for r in DILATIONS:
        dq, dk, dv, *landed[r] = _dil_bwd(*qkv_views[r], *cot_views[r], r, rides=riders[r])
        dq_b.append(dq)
        dk_b.append(dk)
        dv_b.append(dv)
    (r_out,), (r_up,) = landed[4], landed[16]
    dproj, dqn2, dkn2 = _qkv_bwd(dq_b, dk_b, dv_b, dqs, dks, dvs, qraw, kraw, cos2, sin2, qnw2, knw2, bd128)
    dwin = _in_bwd_dw(h1, dproj)
    grad_x, dn1, r_in = _in_bwd_dx(dproj, a_g, x2, dx1, attn_norm_w, rides=[dwin])
    dqn = dqn2[:, :HEAD_DIM] + dqn2[:, HEAD_DIM:]
    dkn = dkn2[:, :HEAD_DIM] + dkn2[:, HEAD_DIM:]

    small = _pack_small(dn1, dn2, dndil, dnsb, dqn, dkn, loss_local)
    (r_small,) = _exchange_grads([], small)
    big = {
        "w_in": _adamw(r_in, w_in[0], m_w_in[0], v_w_in[0]),
        "w_gate": tuple(t.T for t in _adamw(r_gate, w_gate[0].T, m_w_gate[0].T, v_w_gate[0].T)),
        "w_up": tuple(t.T for t in _adamw(r_up, w_up[0].T, m_w_up[0].T, v_w_up[0].T)),
        "w_down": _adamw(r_down, w_down[0], m_w_down[0], v_w_down[0]),
        "w_out": _adamw(r_out, w_out[0], m_w_out[0], v_w_out[0]),
    }
    packs = [_pack_small(*ts) for ts in (
        (attn_norm_w, ffn_norm_w, dil_out_norm_w, sb_out_norm_w, q_norm_w, k_norm_w),
        (m_attn_norm_w, m_ffn_norm_w, m_dil_out_norm_w, m_sb_out_norm_w, m_q_norm_w, m_k_norm_w),
        (v_attn_norm_w, v_ffn_norm_w, v_dil_out_norm_w, v_sb_out_norm_w, v_q_norm_w, v_k_norm_w))]
    small_raw = _adamw(r_small, *packs)
    loss = small_raw[0][LOSS_ROW, 0]
    small_out = [_unpack_small(t) for t in small_raw]
    names = ["attn_norm_w", "w_in", "q_norm_w", "k_norm_w", "dil_out_norm_w", "sb_out_norm_w", "w_out",
             "ffn_norm_w", "w_gate", "w_up", "w_down"]
    small_pos = {"attn_norm_w": 0, "ffn_norm_w": 1, "dil_out_norm_w": 2, "sb_out_norm_w": 3,
                 "q_norm_w": 4, "k_norm_w": 5}
    outs = [loss, grad_x[None]]
    for kind in range(4):
        for name in names:
            if name in small_pos:
                outs.append(small_out[kind][small_pos[name]])
            else:
                outs.append(big[name][kind][None])
    return tuple(outs)
```

```python
import jax
import jax.numpy as jnp
from jax import lax
from jax.experimental import pallas as pl
from jax.experimental.pallas import tpu as pltpu

F32 = jnp.float32
BF16 = jnp.bfloat16

N_DEV = 8
D_MODEL = 1024
HEAD_DIM = 64
D_GRP = 512
D_IN = 6 * D_GRP
IN_SHARD = D_IN // N_DEV
FF_SHARD = 352
FF_PAD = 384
FF_BLOCK = 2 * FF_PAD
FF_STEPS = N_DEV // 2
OUT_SHARD = D_MODEL // N_DEV
BLOCK = 128
DILATIONS = (1, 4, 16)
ROPE_THETA = 10000.0
EPS = 1e-6
ATT_SCALE = HEAD_DIM ** -0.5
NEG = -1e30

ADAM_LR = 0.001
ADAM_B1 = 0.9
ADAM_B2 = 0.999
ADAM_EPS = 1e-08
ADAM_WD = 0.01
ADAM_STEP = 10

SB_TILE = 256
SB_DEAD = -104.0
SB_PAIRS = 4
SB_BWD_PAIRS = 2
ROW_TILE = 512
DW_ROW_TILE = 1024
VMEM_LIMIT = 56 * 1024 * 1024
MESH = pl.DeviceIdType.MESH


def _dot(a, b):
    return jnp.dot(a, b, preferred_element_type=F32)


def _dot_nt(a, b):
    return lax.dot_general(a, b, (((1,), (1,)), ((), ())), preferred_element_type=F32)


def _dot_tn(a, b):
    return lax.dot_general(a, b, (((0,), (0,)), ((), ())), preferred_element_type=F32)


def _mm_split(t, m):
    hi = t.astype(BF16)
    lo = (t - hi.astype(F32)).astype(BF16)
    return _dot(hi, m) + _dot(lo, m)


def _params(**kw):
    return pltpu.CompilerParams(vmem_limit_bytes=VMEM_LIMIT, **kw)


def _full(shape):
    nd = len(shape)
    return pl.BlockSpec(shape, lambda *_: (0,) * nd)


def _view_shape(s_len, r, dtype):
    return jax.ShapeDtypeStruct((s_len // r, r * D_GRP), dtype)


def _view_spec(tm, r):
    return pl.BlockSpec((tm // r, r * D_GRP), lambda i: (i, 0))


def _swap_halves(t):
    lane = lax.broadcasted_iota(jnp.int32, t.shape, 1)
    first = (lane & 32) == 0
    return jnp.where(first, pltpu.roll(t, 96, 1), pltpu.roll(t, 32, 1))


def _log_sigmoid_pair(z):
    neg_abs = lax.bitcast_convert_type(lax.bitcast_convert_type(z, jnp.uint32) | jnp.uint32(0x80000000), F32)
    lb = jnp.minimum(z, 0.0) - jnp.log(1.0 + jnp.exp(neg_abs))
    return lb, lb - z


def _cumsum_mm(t, tri):
    return _dot(t.astype(BF16), tri)


def _split_views(src_ref, stage_ref, views4, views16):
    slabs, n, _ = src_ref.shape
    n4, n16 = n // 4, n // 16
    for j in range(slabs):
        g, lanes = j // 4, 128 * (j % 4)
        src, stage = src_ref.at[j], stage_ref.at[j]
        for c4 in range(4):
            blk = src[pl.ds(c4, n4, stride=4), :]
            stage[n4 * c4:n4 * (c4 + 1), :] = blk
            col = D_GRP * c4 + lanes
            views4[g][:, col:col + 128] = blk.astype(views4[g].dtype)
        for c4 in range(4):
            for c1 in range(4):
                blk = stage[pl.ds(n4 * c4 + c1, n16, stride=4), :]
                col = D_GRP * (4 * c1 + c4) + lanes
                views16[g][:, col:col + 128] = blk.astype(views16[g].dtype)


def _merge_views(views4, views16, stage_ref, dst4_ref, dst16_ref):
    slabs, n, _ = dst4_ref.shape
    n4, n16 = n // 4, n // 16
    for j in range(slabs):
        g, lanes = j // 4, 128 * (j % 4)
        dst4, dst16, stage = dst4_ref.at[j], dst16_ref.at[j], stage_ref.at[j]
        for c4 in range(4):
            col = D_GRP * c4 + lanes
            dst4[pl.ds(c4, n4, stride=4), :] = views4[g][:, col:col + 128].astype(F32)
            for c1 in range(4):
                col = D_GRP * (4 * c1 + c4) + lanes
                stage[pl.ds(n4 * c4 + c1, n16, stride=4), :] = views16[g][:, col:col + 128].astype(F32)
        for c4 in range(4):
            dst16[pl.ds(c4, n4, stride=4), :] = stage[n4 * c4:n4 * (c4 + 1), :]


def _slab_group(ref, g):
    return jnp.concatenate([ref[4 * g + p] for p in range(4)], axis=1)


def _mesh_pos():
    return lax.axis_index("x"), lax.axis_index("y"), lax.axis_index("c")


def _flat_index(p):
    return 4 * p[0] + 2 * p[1] + p[2]


def _gather_weights(shards):
    n_arr = len(shards)

    def body(*refs):
        srcs, outs = refs[:n_arr], refs[n_arr:2 * n_arr]
        send_sems, recv_sems, local_sems = refs[2 * n_arr:]
        x, y, c = _mesh_pos()
        me, sibling = (x, y, c), (x, y, 1 - c)
        chips = [(1 - x, y), (x, 1 - y), (1 - x, 1 - y)]

        def copy(arr, k, block, to, own=False):
            dst = outs[arr].at[_flat_index(block)]
            return pltpu.make_async_remote_copy(
                src_ref=srcs[arr] if own else dst, dst_ref=dst,
                send_sem=send_sems.at[arr, k], recv_sem=recv_sems.at[arr, k],
                device_id=to, device_id_type=MESH)

        for arr in range(n_arr):
            mine = pltpu.make_async_copy(srcs[arr], outs[arr].at[_flat_index(me)], local_sems.at[arr])
            mine.start()
            first = [copy(arr, 0, me, sibling, own=True)]
            first += [copy(arr, 1 + j, me, (*chip, c), own=True) for j, chip in enumerate(chips)]
            for cp in first:
                cp.start()
        for arr in range(n_arr):
            passed = [copy(arr, 4 + j, (*chip, c), sibling) for j, chip in enumerate(chips)]
            for j, chip in enumerate(chips):
                copy(arr, 1 + j, (*chip, c), me).wait_recv()
                passed[j].start()
        for arr in range(n_arr):
            copy(arr, 0, sibling, me).wait_recv()
            for j, chip in enumerate(chips):
                copy(arr, 4 + j, (*chip, 1 - c), me).wait_recv()
            for k in range(7):
                copy(arr, k, me, me).wait_send()
            pltpu.make_async_copy(srcs[arr], outs[arr].at[_flat_index(me)], local_sems.at[arr]).wait()

    any_spec = pl.BlockSpec(memory_space=pl.ANY)
    return pl.pallas_call(
        body, name="gather_weights",
        out_shape=tuple(jax.ShapeDtypeStruct((N_DEV,) + s.shape, s.dtype) for s in shards),
        in_specs=[any_spec] * n_arr, out_specs=(any_spec,) * n_arr,
        scratch_shapes=[pltpu.SemaphoreType.DMA((n_arr, 7)), pltpu.SemaphoreType.DMA((n_arr, 7)),
                        pltpu.SemaphoreType.DMA((n_arr,))],
        compiler_params=pltpu.CompilerParams(has_side_effects=True),
    )(*shards)


def _peer_list(x, y, c):
    return [(1 - x if m & 4 else x, 1 - y if m & 2 else y, 1 - c if m & 1 else c) for m in range(1, N_DEV)]


def _exchange_grads(parts, small):
    n_arr = len(parts)

    def body(*refs):
        ins, outs = refs[:n_arr + 1], refs[n_arr + 1:2 * (n_arr + 1)]
        send_sems, recv_sems, local_sems = refs[2 * (n_arr + 1):]
        x, y, c = _mesh_pos()
        me = (x, y, c)
        my_idx = _flat_index(me)
        peers = []
        for m in range(1, N_DEV):
            peers.append((1 - x if m & 4 else x, 1 - y if m & 2 else y, 1 - c if m & 1 else c))

        def src_block(arr, dev):
            return ins[arr] if arr == n_arr else ins[arr].at[_flat_index(dev)]

        def copy(arr, k):
            return pltpu.make_async_remote_copy(
                src_ref=src_block(arr, peers[k]), dst_ref=outs[arr].at[my_idx],
                send_sem=send_sems.at[arr, k], recv_sem=recv_sems.at[arr, k],
                device_id=peers[k], device_id_type=MESH)

        def local(arr):
            return pltpu.make_async_copy(src_block(arr, me), outs[arr].at[my_idx], local_sems.at[arr])

        for arr in range(n_arr + 1):
            local(arr).start()
            for k in range(N_DEV - 1):
                copy(arr, k).start()
        for arr in range(n_arr + 1):
            for k in range(N_DEV - 1):
                cp = copy(arr, k)
                cp.wait_send()
                cp.wait_recv()
            local(arr).wait()

    any_spec = pl.BlockSpec(memory_space=pl.ANY)
    out_shape = tuple(jax.ShapeDtypeStruct(p.shape, p.dtype) for p in parts)
    out_shape += (jax.ShapeDtypeStruct((N_DEV,) + small.shape, small.dtype),)
    return pl.pallas_call(
        body, name="exchange_grads",
        out_shape=out_shape,
        in_specs=[any_spec] * (n_arr + 1), out_specs=(any_spec,) * (n_arr + 1),
        scratch_shapes=[pltpu.SemaphoreType.DMA((n_arr + 1, N_DEV - 1)),
                        pltpu.SemaphoreType.DMA((n_arr + 1, N_DEV - 1)),
                        pltpu.SemaphoreType.DMA((n_arr + 1,))],
        compiler_params=pltpu.CompilerParams(has_side_effects=True),
    )(*parts, small)


def _call_with_gather(body, shards, first_step, mid_step, last_step, *, name, grid, in_specs, out_specs,
                      out_shape, scratch_shapes=()):
    out_specs = tuple(out_specs) if isinstance(out_specs, (tuple, list)) else (out_specs,)
    out_shape = tuple(out_shape) if isinstance(out_shape, (tuple, list)) else (out_shape,)
    n_in, n_out, n_scr, n = len(in_specs), len(out_specs), len(scratch_shapes), len(shards)

    def full_body(*refs):
        ins, srcs = refs[:n_in], refs[n_in:n_in + n]
        outs, lands = refs[n_in + n:n_in + n + n_out], refs[n_in + n + n_out:n_in + 2 * n + n_out]
        scratch = refs[n_in + 2 * n + n_out:n_in + 2 * n + n_out + n_scr]
        send_sems, recv_sems, local_sems = refs[-3:]
        x, y, c = _mesh_pos()
        me, sibling = (x, y, c), (x, y, 1 - c)
        chips = [(1 - x, y), (x, 1 - y), (1 - x, 1 - y)]

        def copy(a, k, block, to, own=False):
            dst = lands[a].at[_flat_index(block)]
            return pltpu.make_async_remote_copy(
                src_ref=srcs[a] if own else dst, dst_ref=dst,
                send_sem=send_sems.at[a, k], recv_sem=recv_sems.at[a, k],
                device_id=to, device_id_type=MESH)

        def local(a):
            return pltpu.make_async_copy(srcs[a], lands[a].at[_flat_index(me)], local_sems.at[a])

        @pl.when(first_step())
        def _():
            for a in range(n):
                local(a).start()
                copy(a, 0, me, sibling, own=True).start()
                for j, chip in enumerate(chips):
                    copy(a, 1 + j, me, (*chip, c), own=True).start()

        @pl.when(mid_step())
        def _():
            for a in range(n):
                for j, chip in enumerate(chips):
                    copy(a, 1 + j, (*chip, c), me).wait_recv()
                    copy(a, 4 + j, (*chip, c), sibling).start()

        body(*ins, *outs, *scratch)

        @pl.when(last_step())
        def _():
            for a in range(n):
                copy(a, 0, sibling, me).wait_recv()
                for j, chip in enumerate(chips):
                    copy(a, 4 + j, (*chip, 1 - c), me).wait_recv()
                for k in range(N_DEV - 1):
                    copy(a, k, me, me).wait_send()
                local(a).wait()

    any_spec = pl.BlockSpec(memory_space=pl.ANY)
    return pl.pallas_call(
        full_body, name=name, grid=grid,
        in_specs=list(in_specs) + [any_spec] * n,
        out_specs=out_specs + (any_spec,) * n,
        out_shape=out_shape + tuple(jax.ShapeDtypeStruct((N_DEV,) + t.shape, t.dtype) for t in shards),
        scratch_shapes=list(scratch_shapes) + [pltpu.SemaphoreType.DMA((n, N_DEV - 1)),
                                               pltpu.SemaphoreType.DMA((n, N_DEV - 1)),
                                               pltpu.SemaphoreType.DMA((n,))],
        compiler_params=_params(has_side_effects=True),
    )


def _ride_arrays(rides):
    return [r[0] if isinstance(r, tuple) else r for r in rides]


def _call_with_exchange(body, rides, first_step, last_step, *, name, grid, in_specs, out_specs, out_shape,
                        scratch_shapes=()):
    out_specs = tuple(out_specs) if isinstance(out_specs, (tuple, list)) else (out_specs,)
    out_shape = tuple(out_shape) if isinstance(out_shape, (tuple, list)) else (out_shape,)
    n_in, n_out, n_scr, n = len(in_specs), len(out_specs), len(scratch_shapes), len(rides)
    if n == 0:
        return pl.pallas_call(body, name=name, grid=grid, in_specs=list(in_specs), out_specs=out_specs,
                              out_shape=out_shape, scratch_shapes=list(scratch_shapes),
                              compiler_params=_params())
    rides = [r if isinstance(r, tuple) else (r, 0) for r in rides]
    firsts = [first for _, first in rides]
    counts = [t.shape[0] for t, _ in rides]

    def full_body(*refs):
        ins, srcs = refs[:n_in], refs[n_in:n_in + n]
        outs, lands = refs[n_in + n:n_in + n + n_out], refs[n_in + n + n_out:n_in + 2 * n + n_out]
        scratch = refs[n_in + 2 * n + n_out:n_in + 2 * n + n_out + n_scr]
        send_sems, recv_sems, local_sems = refs[-3:]
        x, y, c = _mesh_pos()
        my_idx = _flat_index((x, y, c))
        peers = _peer_list(x, y, c)

        def when_dest(a, idx, action):
            if counts[a] == N_DEV:
                action()
            else:
                pl.when(jnp.logical_and(idx >= firsts[a], idx < firsts[a] + counts[a]))(action)

        def remote(a, k):
            return pltpu.make_async_remote_copy(
                src_ref=srcs[a].at[_flat_index(peers[k]) - firsts[a]], dst_ref=lands[a].at[my_idx],
                send_sem=send_sems.at[a, k], recv_sem=recv_sems.at[a, k],
                device_id=peers[k], device_id_type=MESH)

        def local(a):
            return pltpu.make_async_copy(srcs[a].at[my_idx - firsts[a]], lands[a].at[my_idx], local_sems.at[a])

        @pl.when(first_step())
        def _():
            for a in range(n):
                when_dest(a, my_idx, lambda a=a: local(a).start())
                for k in range(N_DEV - 1):
                    when_dest(a, _flat_index(peers[k]), lambda a=a, k=k: remote(a, k).start())

        body(*ins, *outs, *scratch)

        @pl.when(last_step())
        def _():
            for a in range(n):
                for k in range(N_DEV - 1):
                    when_dest(a, _flat_index(peers[k]), lambda a=a, k=k: remote(a, k).wait_send())
                    when_dest(a, my_idx, lambda a=a, k=k: remote(a, k).wait_recv())
                when_dest(a, my_idx, lambda a=a: local(a).wait())

    any_spec = pl.BlockSpec(memory_space=pl.ANY)
    res = pl.pallas_call(
        full_body, name=name, grid=grid,
        in_specs=list(in_specs) + [any_spec] * n,
        out_specs=out_specs + (any_spec,) * n,
        out_shape=out_shape + tuple(jax.ShapeDtypeStruct((N_DEV,) + t.shape[1:], t.dtype) for t, _ in rides),
        scratch_shapes=list(scratch_shapes) + [pltpu.SemaphoreType.DMA((n, N_DEV - 1)),
                                               pltpu.SemaphoreType.DMA((n, N_DEV - 1)),
                                               pltpu.SemaphoreType.DMA((n,))],
        compiler_params=_params(has_side_effects=True),
    )
    return res


def _head_norm(t, w128, bd):
    ms = _mm_split(t * t, bd) * (1.0 / HEAD_DIM)
    r = lax.rsqrt(ms + EPS)
    return (t * r) * w128, r


def _attn_in(x2, wn1, a_g, cos2, sin2, qnw, knw, bd, shards):
    s_len = x2.shape[0]
    tm = ROW_TILE

    def body(x_ref, wn_ref, w_ref, cos_ref, sin_ref, qnw_ref, knw_ref, bd_ref,
             h1_ref, qraw_ref, kraw_ref, q_ref, k_ref, va_ref, qs_ref, ks_ref, vs_ref,
             q4_ref, k4_ref, v4_ref, q16_ref, k16_ref, v16_ref, proj, slabs, stage, w_full):
        @pl.when(pl.program_id(0) == 0)
        def _():
            for d in range(N_DEV):
                w_full[:, IN_SHARD * d:IN_SHARD * (d + 1)] = w_ref[d]

        xx = x_ref[...]
        r = lax.rsqrt(jnp.mean(xx * xx, axis=-1, keepdims=True) + EPS)
        h = ((xx * r) * wn_ref[...]).astype(BF16)
        h1_ref[...] = h
        proj[...] = _dot(h, w_full[...])
        cos_t, sin_t, bdm = cos_ref[...], sin_ref[...], bd_ref[...]
        for grp, (raw_ref, rope_ref, nw_ref) in enumerate(((qraw_ref, q_ref, qnw_ref),
                                                           (kraw_ref, k_ref, knw_ref))):
            for p in range(4):
                cols = slice(D_GRP * grp + 128 * p, D_GRP * grp + 128 * (p + 1))
                t = proj[:, cols]
                raw_ref[:, 128 * p:128 * (p + 1)] = t
                yn, _ = _head_norm(t, nw_ref[...], bdm)
                roped = yn * cos_t + _swap_halves(yn) * sin_t
                slabs[4 * grp + p] = roped
                rope_ref[:, 128 * p:128 * (p + 1)] = roped.astype(BF16)
        for p in range(4):
            slabs[8 + p] = proj[:, 2 * D_GRP + 128 * p:2 * D_GRP + 128 * (p + 1)]
        for grp, ref in ((2, va_ref), (3, qs_ref), (4, ks_ref), (5, vs_ref)):
            ref[...] = proj[:, D_GRP * grp:D_GRP * (grp + 1)].astype(BF16)
        _split_views(slabs, stage, (q4_ref, k4_ref, v4_ref), (q16_ref, k16_ref, v16_ref))

    row = lambda w: pl.BlockSpec((tm, w), lambda i: (i, 0))
    grp_bf = jax.ShapeDtypeStruct((s_len, D_GRP), BF16)
    grp_f32 = jax.ShapeDtypeStruct((s_len, D_GRP), F32)
    ni = s_len // tm
    return _call_with_gather(
        body, shards, lambda: pl.program_id(0) == 0, lambda: pl.program_id(0) == ni - 2,
        lambda: pl.program_id(0) == ni - 1,
        name="attn_in", grid=(ni,),
        in_specs=[row(D_MODEL), _full((1, D_MODEL)),
                  pl.BlockSpec((N_DEV, D_MODEL, IN_SHARD), lambda i: (0, 0, 0)),
                  row(128), row(128), _full((1, 128)), _full((1, 128)), _full((128, 128))],
        out_specs=(row(D_MODEL),) + (row(D_GRP),) * 8 + (_view_spec(tm, 4),) * 3 + (_view_spec(tm, 16),) * 3,
        out_shape=(jax.ShapeDtypeStruct((s_len, D_MODEL), BF16), grp_f32, grp_f32) + (grp_bf,) * 6
        + (_view_shape(s_len, 4, BF16),) * 3 + (_view_shape(s_len, 16, BF16),) * 3,
        scratch_shapes=[pltpu.VMEM((tm, D_IN), F32), pltpu.VMEM((12, tm, 128), F32), pltpu.VMEM((12, tm, 128), F32),
                        pltpu.VMEM((D_MODEL, D_IN), BF16)],
    )(x2, wn1, a_g, cos2, sin2, qnw, knw, bd, *shards)


def _band_mask(n):
    i = lax.broadcasted_iota(jnp.int32, (2 * BLOCK, 2 * BLOCK), 0) & (BLOCK - 1)
    j = lax.broadcasted_iota(jnp.int32, (2 * BLOCK, 2 * BLOCK), 1)
    dist = i + BLOCK - j
    return (dist >= 0) & (dist <= BLOCK) & ((n - 1) * BLOCK + j >= 0)


def _stack_heads(t2, head0):
    return jnp.concatenate([jnp.where(head0, t2, 0), jnp.where(head0, 0, t2)], axis=0)


def _unstack_heads(t, head0):
    return jnp.where(head0, t[0:BLOCK], t[BLOCK:2 * BLOCK])


def _dil_fwd(qv, kv, vv, r):
    sub_len = qv.shape[0]
    nb = sub_len // BLOCK

    qb = 2 if nb % 2 == 0 else 1

    def body(q_ref, kp_ref, kc_ref, vp_ref, vc_ref, o_ref, lse_ref):
        n = pl.program_id(1)
        lane = lax.broadcasted_iota(jnp.int32, (BLOCK, 128), 1)
        head0 = lane < HEAD_DIM
        units = [(b, slice(128 * p, 128 * (p + 1))) for b in range(qb) for p in range(4)]
        valid = [_band_mask(qb * n + b) for b in range(qb)]
        rows = [slice(BLOCK * b, BLOCK * (b + 1)) for b in range(qb)]

        def keys(prev_ref, cur_ref, b, c):
            before = prev_ref[:, c] if b == 0 else cur_ref[rows[b - 1], c]
            return jnp.concatenate([before, cur_ref[rows[b], c]], axis=0)

        qqs = [_stack_heads(q_ref[rows[b], c] * ATT_SCALE, head0) for b, c in units]
        kks = [keys(kp_ref, kc_ref, b, c) for b, c in units]
        vvs = [keys(vp_ref, vc_ref, b, c) for b, c in units]
        ss = [_dot_nt(qq, kk) for qq, kk in zip(qqs, kks)]
        prs, dens, lses = [], [], []
        for (b, _), s in zip(units, ss):
            s = jnp.where(valid[b], s, NEG)
            m = jnp.max(s, axis=-1, keepdims=True)
            pr = jnp.exp(s - m)
            den = jnp.sum(pr, axis=-1, keepdims=True)
            prs.append(pr.astype(BF16))
            dens.append(den)
            lses.append(m + jnp.log(den))
        pvs = [_dot(pr, vv2) for pr, vv2 in zip(prs, vvs)]
        for (b, c), pv, den, lse in zip(units, pvs, dens, lses):
            o_ref[rows[b], c] = _unstack_heads(pv / den, head0)
            lse_ref[rows[b], c] = _unstack_heads(jnp.broadcast_to(lse, (2 * BLOCK, 128)), head0)

    cur = pl.BlockSpec((qb * BLOCK, D_GRP), lambda c, n: (n, c))
    prev = pl.BlockSpec((BLOCK, D_GRP), lambda c, n: (jnp.maximum(qb * n - 1, 0), c))
    out = jax.ShapeDtypeStruct(qv.shape, F32)
    return pl.pallas_call(
        body, name=f"dil_fwd_r{r}", grid=(r, nb // qb),
        in_specs=[cur, prev, cur, prev, cur], out_specs=(cur, cur), out_shape=(out, out),
        compiler_params=_params(),
    )(qv, kv, kv, vv, vv)


def _dil_bwd(qv, kv, vv, dov, lsev, deltav, r, rides):
    sub_len = qv.shape[0]
    nb = sub_len // BLOCK

    def body(q_ref, kp_ref, kc_ref, vp_ref, vc_ref, do_ref, lse_ref, dl_ref,
             dq_ref, dk_ref, dv_ref, dk_carry, dv_carry):
        n = pl.program_id(1)

        @pl.when(n == 0)
        def _():
            dk_carry[...] = jnp.zeros_like(dk_carry)
            dv_carry[...] = jnp.zeros_like(dv_carry)

        @pl.when(n < nb)
        def _():
            valid = _band_mask(n)
            lane = lax.broadcasted_iota(jnp.int32, (BLOCK, 128), 1)
            head0 = lane < HEAD_DIM
            pairs = [slice(128 * p, 128 * (p + 1)) for p in range(4)]
            qqs = [_stack_heads(q_ref[:, c] * ATT_SCALE, head0) for c in pairs]
            dos = [_stack_heads(do_ref[:, c], head0) for c in pairs]
            kks = [jnp.concatenate([kp_ref[:, c], kc_ref[:, c]], axis=0) for c in pairs]
            vvs = [jnp.concatenate([vp_ref[:, c], vc_ref[:, c]], axis=0) for c in pairs]
            ss = [_dot_nt(qq, kk) for qq, kk in zip(qqs, kks)]
            dps = [_dot_nt(do, vv2) for do, vv2 in zip(dos, vvs)]
            def softmax_terms(p):
                stats = []
                for ref in (lse_ref, dl_ref):
                    t2 = ref[:, pairs[p]]
                    stats.append(jnp.concatenate(
                        [jnp.sum(jnp.where(lane == 0, t2, 0.0), axis=-1, keepdims=True),
                         jnp.sum(jnp.where(lane == HEAD_DIM, t2, 0.0), axis=-1, keepdims=True)], axis=0))
                pr = jnp.where(valid, jnp.exp(jnp.minimum(ss[p] - stats[0], 0.0)), 0.0)
                return pr.astype(BF16), (pr * (dps[p] - stats[1])).astype(BF16)

            terms = [softmax_terms(p) for p in range(4)]
            dqs = [_dot(terms[p][1], kks[p]) for p in range(4)]
            dkks = [_dot_tn(terms[p][1], qqs[p]) for p in range(4)]
            dvvs = [_dot_tn(terms[p][0], dos[p]) for p in range(4)]
            for c, dq, dkk, dvv in zip(pairs, dqs, dkks, dvvs):
                dq_ref[:, c] = _unstack_heads(dq, head0) * ATT_SCALE
                dk_ref[:, c] = dk_carry[:, c] + dkk[:BLOCK]
                dv_ref[:, c] = dv_carry[:, c] + dvv[:BLOCK]
                dk_carry[:, c] = dkk[BLOCK:]
                dv_carry[:, c] = dvv[BLOCK:]

        @pl.when(n == nb)
        def _():
            dk_ref[...] = dk_carry[...]
            dv_ref[...] = dv_carry[...]

    last = nb - 1
    cur = pl.BlockSpec((BLOCK, D_GRP), lambda c, n: (jnp.minimum(n, last), c))
    prev = pl.BlockSpec((BLOCK, D_GRP), lambda c, n: (jnp.clip(n - 1, 0, last), c))
    out = jax.ShapeDtypeStruct(qv.shape, F32)
    return _call_with_exchange(
        body, rides,
        lambda: jnp.logical_and(pl.program_id(0) == 0, pl.program_id(1) == 0),
        lambda: jnp.logical_and(pl.program_id(0) == r - 1, pl.program_id(1) == nb),
        name=f"dil_bwd_r{r}", grid=(r, nb + 1),
        in_specs=[cur, prev, cur, prev, cur, cur, cur, cur],
        out_specs=(cur, prev, prev), out_shape=(out, out, out),
        scratch_shapes=[pltpu.VMEM((BLOCK, D_GRP), F32), pltpu.VMEM((BLOCK, D_GRP), F32)],
    )(qv, kv, kv, vv, vv, dov, lsev, deltav, *rides)


def _sb_fwd(qs, ks, vs, tri_suf, shards):
    s_len = qs.shape[0]
    t = SB_TILE
    nq = s_len // t

    npair = SB_PAIRS

    def body(q_ref, k_ref, v_ref, u_ref, o_ref, c_ref, qq, vt, acc, cf, csave):
        row = lax.broadcasted_iota(jnp.int32, (2 * t, t), 0) & (t - 1)
        col = lax.broadcasted_iota(jnp.int32, (2 * t, t), 1)
        diag_mask = col < row
        lane1 = lax.broadcasted_iota(jnp.int32, (t, 128), 1)
        head0 = lane1 < HEAD_DIM
        lane2 = lax.broadcasted_iota(jnp.int32, (2 * t, 128), 1)
        uu = u_ref[...]
        pr = range(npair)
        cols = [slice(128 * pp, 128 * (pp + 1)) for pp in pr]

        i = pl.program_id(1)

        @pl.when(i == 0)
        def _():
            def transpose_v(j, _):
                rows = pl.ds(pl.multiple_of(j * t, t), t)
                for pp in pr:
                    vt[pp, j] = v_ref[rows, cols[pp]].astype(F32).T.astype(BF16)
                return 0

            lax.fori_loop(0, nq, transpose_v, 0)

        for pp in pr:
            q2 = q_ref[:, cols[pp]] * ATT_SCALE
            qq[pp, 0:t, :] = jnp.where(head0, q2, 0)
            qq[pp, t:2 * t, :] = jnp.where(head0, 0, q2)
        acc[...] = jnp.zeros_like(acc)
        cf[...] = jnp.zeros_like(cf)
        csave[...] = jnp.full(csave.shape, 2.0 * SB_DEAD, F32)

        def tile(kb, diag):
            krows = pl.ds(pl.multiple_of(kb * t, t), t)
            zs = [_dot_nt(qq[pp], k_ref[krows, cols[pp]]) for pp in pr]
            lbk = [_log_sigmoid_pair(z) for z in zs]
            lks = [jnp.where(diag_mask, lk, 0.0) if diag else lk for _, lk in lbk]
            sufs = [_cumsum_mm(lk, uu) for lk in lks]
            carries = [cf[pp] for pp in pr]
            avs = []
            for pp in pr:
                a = jnp.exp(lbk[pp][0] + (sufs[pp] + jnp.concatenate([carries[pp]] * (t // 128), axis=1)))
                avs.append((jnp.where(diag_mask, a, 0.0) if diag else a).astype(BF16))
            pvs = [_dot_nt(vt[pp, kb], avs[pp]) for pp in pr]
            for pp in pr:
                acc[pp] += pvs[pp]
                csave[pp] = jnp.where(lane2 == kb, carries[pp], csave[pp])
                cf[pp] = carries[pp] + jnp.broadcast_to(jnp.sum(lks[pp], axis=-1, keepdims=True), (2 * t, 128))

        tile(i, True)

        def alive():
            return jnp.max(cf[...]) > SB_DEAD

        def k_block(state):
            kb, _ = state
            tile(kb, False)
            return kb - 1, alive()

        lax.while_loop(lambda state: jnp.logical_and(state[0] >= 0, state[1]), k_block, (i - 1, alive()))
        for pp in pr:
            o_ref[:, cols[pp]] = jnp.where(head0, acc[pp, :, 0:t].T, acc[pp, :, t:2 * t].T)
            c_ref[2 * pp] = csave[pp, 0:t, :]
            c_ref[2 * pp + 1] = csave[pp, t:2 * t, :]

    width = 128 * npair
    kv = pl.BlockSpec((s_len, width), lambda p, i: (0, p))
    qo = pl.BlockSpec((t, width), lambda p, i: (i, p))
    steps = 4 // npair

    def at(p, i):
        return lambda: jnp.logical_and(pl.program_id(0) == p, pl.program_id(1) == i)

    return _call_with_gather(
        body, shards, at(0, 0), at(steps - 1, (2 * nq) // 3), at(steps - 1, nq - 1),
        name="sb_fwd", grid=(steps, nq),
        in_specs=[qo, kv, kv, pl.BlockSpec((t, t), lambda p, i: (0, 0))],
        out_specs=(qo, pl.BlockSpec((2 * npair, t, 128), lambda p, i: (p, i, 0))),
        out_shape=(jax.ShapeDtypeStruct((s_len, D_GRP), F32),
                   jax.ShapeDtypeStruct((8, s_len, 128), F32)),
        scratch_shapes=[pltpu.VMEM((npair, 2 * t, 128), BF16), pltpu.VMEM((npair, nq, 128, t), BF16),
                        pltpu.VMEM((npair, 128, 2 * t), F32),
                        pltpu.VMEM((npair, 2 * t, 128), F32), pltpu.VMEM((npair, 2 * t, 128), F32)],
    )(qs, ks, vs, tri_suf, *shards)


def _sb_bwd(qs, ks, vs, dos, csaved, tri_suf, tri_pre, rides):
    s_len = qs.shape[0]
    t = SB_TILE
    nq = s_len // t

    npair = SB_BWD_PAIRS

    def body(q_ref, k_ref, v_ref, do_ref, c_ref, u_ref, p_ref, dq_ref, dk_ref, dv_ref,
             qq, dd, qqt, ddt, kt, dq_acc, dkt, dvt, cg):
        row = lax.broadcasted_iota(jnp.int32, (2 * t, t), 0) & (t - 1)
        col = lax.broadcasted_iota(jnp.int32, (2 * t, t), 1)
        diag_mask = col < row
        lane1 = lax.broadcasted_iota(jnp.int32, (t, 128), 1)
        head0 = lane1 < HEAD_DIM
        lane2 = lax.broadcasted_iota(jnp.int32, (2 * t, 128), 1)
        uu, pm = u_ref[...], p_ref[...]
        pr = range(npair)
        cols = [slice(128 * pp, 128 * (pp + 1)) for pp in pr]
        i = pl.program_id(1)

        @pl.when(i == 0)
        def _():
            dkt[...] = jnp.zeros_like(dkt)
            dvt[...] = jnp.zeros_like(dvt)

            def transpose_k(j, _):
                rows = pl.ds(pl.multiple_of(j * t, t), t)
                for pp in pr:
                    kt[pp, j] = k_ref[rows, cols[pp]].astype(F32).T.astype(BF16)
                return 0

            lax.fori_loop(0, nq, transpose_k, 0)

        for pp in pr:
            q2 = q_ref[:, cols[pp]].astype(F32) * ATT_SCALE
            do2 = do_ref[:, cols[pp]].astype(F32)
            for src, nat, tr in ((q2, qq, qqt), (do2, dd, ddt)):
                stacked = jnp.concatenate([jnp.where(head0, src, 0.0), jnp.where(head0, 0.0, src)], axis=0)
                nat[pp] = stacked.astype(BF16)
                tr[pp] = stacked.T.astype(BF16)
        dq_acc[...] = jnp.zeros_like(dq_acc)
        cg[...] = jnp.zeros_like(cg)

        def tile(kb, diag):
            krows = pl.ds(pl.multiple_of(kb * t, t), t)
            zs = [_dot_nt(qq[pp], k_ref[krows, cols[pp]]) for pp in pr]
            das = [_dot_nt(dd[pp], v_ref[krows, cols[pp]]) for pp in pr]
            lbk = [_log_sigmoid_pair(z) for z in zs]
            lks = [jnp.where(diag_mask, lk, 0.0) if diag else lk for _, lk in lbk]
            sufs = [_cumsum_mm(lk, uu) for lk in lks]
            avs, gs = [], []
            for pp in pr:
                cs = jnp.concatenate([c_ref[2 * pp], c_ref[2 * pp + 1]], axis=0)
                cf = jnp.sum(jnp.where(lane2 == kb, cs, 0.0), axis=-1, keepdims=True)
                a = jnp.exp(lbk[pp][0] + (sufs[pp] + cf))
                a = jnp.where(diag_mask, a, 0.0) if diag else a
                avs.append(a.astype(BF16))
                gs.append(a * das[pp])
            gpres = [_cumsum_mm(g, pm) for g in gs]
            dzs = []
            for pp in pr:
                carry = cg[pp]
                beta = jnp.exp(lbk[pp][0])
                dz = gs[pp] - beta * (gs[pp] + (gpres[pp] + jnp.concatenate([carry] * (t // 128), axis=1)))
                dzs.append((jnp.where(diag_mask, dz, 0.0) if diag else dz).astype(BF16))
                cg[pp] = carry + jnp.broadcast_to(jnp.sum(gs[pp], axis=-1, keepdims=True), (2 * t, 128))
            dqs = [_dot_nt(kt[pp, kb], dzs[pp]) for pp in pr]
            dks = [_dot(qqt[pp], dzs[pp]) for pp in pr]
            dvs = [_dot(ddt[pp], avs[pp]) for pp in pr]
            for pp in pr:
                dq_acc[pp] += dqs[pp]
                dkt[pp, kb] += dks[pp]
                dvt[pp, kb] += dvs[pp]

        def k_block(kb, _):
            tile(kb, False)
            return 0

        col_max = jnp.max(jnp.max(c_ref[...], axis=0), axis=0, keepdims=True)
        lane_row = lax.broadcasted_iota(jnp.int32, (1, 128), 1)
        n_live = jnp.sum(jnp.where(jnp.logical_and(col_max > SB_DEAD, lane_row < i), 1, 0))
        lax.fori_loop(i - n_live, i, k_block, 0)
        tile(i, True)
        for pp in pr:
            dq_ref[:, cols[pp]] = jnp.where(head0, dq_acc[pp, :, 0:t].T, dq_acc[pp, :, t:2 * t].T) * ATT_SCALE

        @pl.when(i == nq - 1)
        def _():
            def untranspose(j, _):
                rows = pl.ds(pl.multiple_of(j * t, t), t)
                for pp in pr:
                    dk_ref[rows, cols[pp]] = dkt[pp, j].T
                    dv_ref[rows, cols[pp]] = dvt[pp, j].T
                return 0

            lax.fori_loop(0, nq, untranspose, 0)

    width = 128 * npair
    kv = pl.BlockSpec((s_len, width), lambda p, i: (0, p))
    qo = pl.BlockSpec((t, width), lambda p, i: (i, p))
    tri = pl.BlockSpec((t, t), lambda p, i: (0, 0))
    out = jax.ShapeDtypeStruct((s_len, D_GRP), F32)
    steps = 4 // npair
    return _call_with_exchange(
        body, rides,
        lambda: jnp.logical_and(pl.program_id(0) == 0, pl.program_id(1) == 0),
        lambda: jnp.logical_and(pl.program_id(0) == steps - 1, pl.program_id(1) == nq - 1),
        name="sb_bwd", grid=(steps, nq),
        in_specs=[qo, kv, kv, qo, pl.BlockSpec((2 * npair, t, 128), lambda p, i: (p, i, 0)), tri, tri],
        out_specs=(qo, kv, kv), out_shape=(out, out, out),
        scratch_shapes=[pltpu.VMEM((npair, 2 * t, 128), BF16), pltpu.VMEM((npair, 2 * t, 128), BF16),
                        pltpu.VMEM((npair, 128, 2 * t), BF16), pltpu.VMEM((npair, 128, 2 * t), BF16),
                        pltpu.VMEM((npair, nq, 128, t), BF16),
                        pltpu.VMEM((npair, 128, 2 * t), F32),
                        pltpu.VMEM((npair, nq, 128, t), F32), pltpu.VMEM((npair, nq, 128, t), F32),
                        pltpu.VMEM((npair, 2 * t, 128), F32)],
    )(qs, ks, vs, dos, csaved, tri_suf, tri_pre, *rides)


def _attn_out(o_b, lse_b, o_sb, x2, wdil, wsb, out_g, shards):
    s_len = x2.shape[0]
    tm = ROW_TILE

    def body(o1_ref, l1_ref, o4_ref, l4_ref, o16_ref, l16_ref, osb_ref, x_ref, wdil_ref, wsb_ref, w_ref,
             odil_ref, lse_ref, lse4_ref, lse16_ref, mixed_ref, x1_ref, stage, nat4, nat16):
        _merge_views((o4_ref, l4_ref), (o16_ref, l16_ref), stage, nat4, nat16)
        os_ = (o1_ref[...], _slab_group(nat4, 0), _slab_group(nat16, 0))
        ls = (l1_ref[...], _slab_group(nat4, 1), _slab_group(nat16, 1))
        mx = jnp.maximum(jnp.maximum(ls[0], ls[1]), ls[2])
        es = [jnp.exp(l - mx) for l in ls]
        den = es[0] + es[1] + es[2]
        o_dil = (es[0] * os_[0] + es[1] * os_[1] + es[2] * os_[2]) / den
        odil_ref[...] = o_dil
        lse = mx + jnp.log(den)
        lse_ref[...] = lse
        for p in range(4):
            nat4[p] = lse[:, 128 * p:128 * (p + 1)]
        _split_views(nat4.at[0:4], stage.at[0:4], (lse4_ref,), (lse16_ref,))
        halves = []
        for t, w_r in ((o_dil, wdil_ref), (osb_ref[...], wsb_ref)):
            r = lax.rsqrt(jnp.mean(t * t, axis=-1, keepdims=True) + EPS)
            halves.append(((t * r) * w_r[...]).astype(BF16))
        mixed = jnp.concatenate(halves, axis=1)
        mixed_ref[...] = mixed
        w = w_ref[...].reshape(D_MODEL, D_MODEL)
        x1_ref[...] = x_ref[...] + _dot(mixed, w)

    row = lambda w: pl.BlockSpec((tm, w), lambda i: (i, 0))
    ni = s_len // tm
    return _call_with_gather(
        body, shards, lambda: pl.program_id(0) == 0, lambda: pl.program_id(0) == ni - 2,
        lambda: pl.program_id(0) == ni - 1,
        name="attn_out", grid=(ni,),
        in_specs=[row(D_GRP)] * 2 + [_view_spec(tm, 4)] * 2 + [_view_spec(tm, 16)] * 2
        + [row(D_GRP), row(D_MODEL), _full((1, D_GRP)), _full((1, D_GRP)), _full((N_DEV, OUT_SHARD, D_MODEL))],
        out_specs=(row(D_GRP), row(D_GRP), _view_spec(tm, 4), _view_spec(tm, 16), row(D_MODEL), row(D_MODEL)),
        out_shape=(jax.ShapeDtypeStruct((s_len, D_GRP), F32), jax.ShapeDtypeStruct((s_len, D_GRP), F32),
                   _view_shape(s_len, 4, F32), _view_shape(s_len, 16, F32),
                   jax.ShapeDtypeStruct((s_len, D_MODEL), BF16), jax.ShapeDtypeStruct((s_len, D_MODEL), F32)),
        scratch_shapes=[pltpu.VMEM((8, tm, 128), F32)] * 3,
    )(o_b[0], lse_b[0], o_b[1], lse_b[1], o_b[2], lse_b[2], o_sb, x2, wdil, wsb, out_g, *shards)


def _two_shards(w_ref):
    return w_ref[...].reshape(FF_BLOCK, D_MODEL)


def _ffn_fwd(x1, wn2, tgt, gate_g, up_g, down_g):
    s_len = x1.shape[0]
    tm = ROW_TILE
    ni = s_len // tm

    def body(x_ref, wn_ref, t_ref, wg_ref, wu_ref, wd_ref, g_ref, u_ref, h2_ref, dy_ref, loss_ref, acc):
        j = pl.program_id(1)

        @pl.when(j == 0)
        def _():
            xx = x_ref[...]
            r = lax.rsqrt(jnp.mean(xx * xx, axis=-1, keepdims=True) + EPS)
            h2_ref[...] = ((xx * r) * wn_ref[...]).astype(BF16)
            acc[...] = jnp.zeros_like(acc)

        h = h2_ref[...]
        g = _dot_nt(h, _two_shards(wg_ref))
        u = _dot_nt(h, _two_shards(wu_ref))
        g_ref[...] = g
        u_ref[...] = u
        act = (g * (1.0 / (1.0 + jnp.exp(-g)))) * u
        acc[...] += _dot(act.astype(BF16), _two_shards(wd_ref))

        @pl.when(j == FF_STEPS - 1)
        def _():
            err = (x_ref[...] + acc[...]) - t_ref[...]
            dy_ref[...] = err * (1.0 / D_MODEL)
            part = 0.5 * jnp.sum(jnp.mean(err * err, axis=-1, keepdims=True))
            loss_ref[...] = jnp.full((8, 128), part, F32)

    row = pl.BlockSpec((tm, D_MODEL), lambda i, j: (i, 0))
    hid = pl.BlockSpec((tm, FF_BLOCK), lambda i, j: (i, j))
    return pl.pallas_call(
        body, name="ffn_fwd", grid=(ni, FF_STEPS),
        in_specs=[row, pl.BlockSpec((1, D_MODEL), lambda i, j: (0, 0)), row,
                  pl.BlockSpec((2, FF_PAD, D_MODEL), lambda i, j: (j, 0, 0)),
                  pl.BlockSpec((2, FF_PAD, D_MODEL), lambda i, j: (j, 0, 0)),
                  pl.BlockSpec((2, FF_PAD, D_MODEL), lambda i, j: (j, 0, 0))],
        out_specs=(hid, hid, row, row, pl.BlockSpec((8, 128), lambda i, j: (i, 0))),
        out_shape=(jax.ShapeDtypeStruct((s_len, N_DEV * FF_PAD), F32),
                   jax.ShapeDtypeStruct((s_len, N_DEV * FF_PAD), F32),
                   jax.ShapeDtypeStruct((s_len, D_MODEL), BF16),
                   jax.ShapeDtypeStruct((s_len, D_MODEL), F32),
                   jax.ShapeDtypeStruct((ni * 8, 128), F32)),
        scratch_shapes=[pltpu.VMEM((tm, D_MODEL), F32)],
        compiler_params=_params(),
    )(x1, wn2, tgt, gate_g, up_g, down_g)


def _ffn_bwd_dx(dy, g, u, gate_g, up_g, down_g):
    s_len = dy.shape[0]
    tm = ROW_TILE

    def body(dy_ref, g_ref, u_ref, wg_ref, wu_ref, wd_ref, dg_ref, du_ref, act_ref, dh_ref, acc):
        j = pl.program_id(1)

        @pl.when(j == 0)
        def _():
            acc[...] = jnp.zeros_like(acc)

        halves = [slice(0, tm // 2), slice(tm // 2, tm)]
        wd, wg, wu = _two_shards(wd_ref), _two_shards(wg_ref), _two_shards(wu_ref)
        das = [_dot_nt(dy_ref[rows, :].astype(BF16), wd) for rows in halves]

        def elementwise(rows, da):
            gg, uu = g_ref[rows, :], u_ref[rows, :]
            sig = 1.0 / (1.0 + jnp.exp(-gg))
            silu = gg * sig
            act_ref[rows, :] = (silu * uu).astype(BF16)
            du = (da * silu).astype(BF16)
            dg = (da * uu * (sig * (1.0 + gg * (1.0 - sig)))).astype(BF16)
            du_ref[rows, :] = du
            dg_ref[rows, :] = dg
            return dg, du

        dg0, du0 = elementwise(halves[0], das[0])
        acc[halves[0], :] += _dot(dg0, wg) + _dot(du0, wu)
        dg1, du1 = elementwise(halves[1], das[1])
        acc[halves[1], :] += _dot(dg1, wg) + _dot(du1, wu)

        @pl.when(j == FF_STEPS - 1)
        def _():
            dh_ref[...] = acc[...]

    row = pl.BlockSpec((tm, D_MODEL), lambda i, j: (i, 0))
    hid = pl.BlockSpec((tm, FF_BLOCK), lambda i, j: (i, j))
    hid_bf = jax.ShapeDtypeStruct((s_len, N_DEV * FF_PAD), BF16)
    return pl.pallas_call(
        body, name="ffn_bwd_dx", grid=(s_len // tm, FF_STEPS),
        in_specs=[row, hid, hid,
                  pl.BlockSpec((2, FF_PAD, D_MODEL), lambda i, j: (j, 0, 0)),
                  pl.BlockSpec((2, FF_PAD, D_MODEL), lambda i, j: (j, 0, 0)),
                  pl.BlockSpec((2, FF_PAD, D_MODEL), lambda i, j: (j, 0, 0))],
        out_specs=(hid, hid, hid, row),
        out_shape=(hid_bf, hid_bf, hid_bf, jax.ShapeDtypeStruct((s_len, D_MODEL), F32)),
        scratch_shapes=[pltpu.VMEM((tm, D_MODEL), F32)],
        compiler_params=_params(),
    )(dy, g, u, gate_g, up_g, down_g)


def _ffn_bwd_dw(h2, dy, dg, du, act):
    s_len = h2.shape[0]
    tm = DW_ROW_TILE
    ni = s_len // tm

    def body(h_ref, dy_ref, dg_ref, du_ref, act_ref, dwg_ref, dwu_ref, dwd_ref, ag, au, ad):
        i = pl.program_id(1)

        @pl.when(i == 0)
        def _():
            ag[...] = jnp.zeros_like(ag)
            au[...] = jnp.zeros_like(au)
            ad[...] = jnp.zeros_like(ad)

        h = h_ref[...]
        ag[...] += _dot_tn(dg_ref[...], h)
        au[...] += _dot_tn(du_ref[...], h)
        ad[...] += _dot_tn(act_ref[...], dy_ref[...].astype(BF16))

        @pl.when(i == ni - 1)
        def _():
            for acc_ref, out_ref in ((ag, dwg_ref), (au, dwu_ref), (ad, dwd_ref)):
                out_ref[...] = acc_ref[...].astype(BF16).reshape(2, FF_PAD, D_MODEL)

    row = pl.BlockSpec((tm, D_MODEL), lambda j, i: (i, 0))
    hid = pl.BlockSpec((tm, FF_BLOCK), lambda j, i: (i, j))
    row_w = pl.BlockSpec((2, FF_PAD, D_MODEL), lambda j, i: (j, 0, 0))
    grad = jax.ShapeDtypeStruct((N_DEV, FF_PAD, D_MODEL), BF16)
    return pl.pallas_call(
        body, name="ffn_bwd_dw", grid=(FF_STEPS, ni),
        in_specs=[row, row, hid, hid, hid], out_specs=(row_w, row_w, row_w),
        out_shape=(grad, grad, grad),
        scratch_shapes=[pltpu.VMEM((FF_BLOCK, D_MODEL), F32)] * 3,
        compiler_params=_params(),
    )(h2, dy, dg, du, act)


def _rms_bwd(dy, t, w):
    r = lax.rsqrt(jnp.mean(t * t, axis=-1, keepdims=True) + EPS)
    gw = dy * w
    dt = r * (gw - t * ((r * r) * jnp.mean(gw * t, axis=-1, keepdims=True)))
    return dt, dy * t * r


def _attn_out_bwd(dy, dh2, x1, wn2, b_g, mixed, o_dil, o_sb, wdil, wsb, bd512):
    s_len = dy.shape[0]
    tm = ROW_TILE
    ni = s_len // tm

    def body(dy_ref, dh_ref, x1_ref, wn_ref, w_ref, mixed_ref, odil_ref, osb_ref, wdil_ref, wsb_ref, bd_ref,
             dx1_ref, dodil_ref, delta_ref, dosb_ref, dwout_ref, dwn_ref, dwdil_ref, dwsb_ref,
             do4_ref, dl4_ref, do16_ref, dl16_ref, wacc, both, stage):
        i = pl.program_id(0)

        @pl.when(i == 0)
        def _():
            wacc[...] = jnp.zeros_like(wacc)
            dwn_ref[...] = jnp.zeros_like(dwn_ref)
            dwdil_ref[...] = jnp.zeros_like(dwdil_ref)
            dwsb_ref[...] = jnp.zeros_like(dwsb_ref)

        dnorm, dw_rows = _rms_bwd(dh_ref[...], x1_ref[...], wn_ref[...])
        dx1 = dy_ref[...] + dnorm
        dx1_ref[...] = dx1
        dwn_ref[...] += jnp.sum(dw_rows, axis=0, keepdims=True)
        dx1b = dx1.astype(BF16)
        w = w_ref[...].reshape(D_MODEL, D_MODEL)
        dmixed = _dot_nt(dx1b, w)
        wacc[...] += _dot_tn(mixed_ref[...], dx1b)
        o_dil = odil_ref[...]
        d_odil, dw_rows = _rms_bwd(dmixed[:, :D_GRP], o_dil, wdil_ref[...])
        dwdil_ref[...] += jnp.sum(dw_rows, axis=0, keepdims=True)
        dodil_ref[...] = d_odil.astype(BF16)
        delta = _mm_split(d_odil * o_dil, bd_ref[...])
        delta_ref[...] = delta
        for p in range(4):
            both[p] = d_odil[:, 128 * p:128 * (p + 1)]
            both[4 + p] = delta[:, 128 * p:128 * (p + 1)]
        _split_views(both, stage, (do4_ref, dl4_ref), (do16_ref, dl16_ref))
        d_osb, dw_rows = _rms_bwd(dmixed[:, D_GRP:], osb_ref[...], wsb_ref[...])
        dwsb_ref[...] += jnp.sum(dw_rows, axis=0, keepdims=True)
        dosb_ref[...] = d_osb.astype(BF16)

        @pl.when(i == ni - 1)
        def _():
            dwout_ref[...] = wacc[...].astype(BF16).reshape(N_DEV, OUT_SHARD, D_MODEL)

    row = lambda w: pl.BlockSpec((tm, w), lambda i: (i, 0))
    return pl.pallas_call(
        body, name="attn_out_bwd", grid=(ni,),
        in_specs=[row(D_MODEL), row(D_MODEL), row(D_MODEL), _full((1, D_MODEL)),
                  _full((N_DEV, OUT_SHARD, D_MODEL)),
                  row(D_MODEL), row(D_GRP), row(D_GRP), _full((1, D_GRP)), _full((1, D_GRP)),
                  _full((D_GRP, D_GRP))],
        out_specs=(row(D_MODEL), row(D_GRP), row(D_GRP), row(D_GRP),
                   _full((N_DEV, OUT_SHARD, D_MODEL)), _full((1, D_MODEL)), _full((1, D_GRP)), _full((1, D_GRP)),
                   _view_spec(tm, 4), _view_spec(tm, 4), _view_spec(tm, 16), _view_spec(tm, 16)),
        out_shape=(jax.ShapeDtypeStruct((s_len, D_MODEL), F32), jax.ShapeDtypeStruct((s_len, D_GRP), BF16),
                   jax.ShapeDtypeStruct((s_len, D_GRP), F32), jax.ShapeDtypeStruct((s_len, D_GRP), BF16),
                   jax.ShapeDtypeStruct((N_DEV, OUT_SHARD, D_MODEL), BF16),
                   jax.ShapeDtypeStruct((1, D_MODEL), F32), jax.ShapeDtypeStruct((1, D_GRP), F32),
                   jax.ShapeDtypeStruct((1, D_GRP), F32),
                   _view_shape(s_len, 4, BF16), _view_shape(s_len, 4, F32),
                   _view_shape(s_len, 16, BF16), _view_shape(s_len, 16, F32)),
        scratch_shapes=[pltpu.VMEM((D_MODEL, D_MODEL), F32), pltpu.VMEM((8, tm, 128), F32),
                        pltpu.VMEM((8, tm, 128), F32)],
        compiler_params=_params(),
    )(dy, dh2, x1, wn2, b_g, mixed, o_dil, o_sb, wdil, wsb, bd512)


def _qkv_bwd(dq_b, dk_b, dv_b, dqs, dks, dvs, qraw, kraw, cos2, sin2, qnw, knw, bd):
    s_len = qraw.shape[0]
    tm = ROW_TILE
    ni = s_len // tm

    def body(dq1, dk1, dv1, dq4, dk4, dv4, dq16, dk16, dv16, dqs_ref, dks_ref, dvs_ref,
             qraw_ref, kraw_ref, cos_ref, sin_ref, qnw_ref, knw_ref, bd_ref,
             dproj_ref, dqn_ref, dkn_ref, stage, nat4, nat16):
        i = pl.program_id(0)

        @pl.when(i == 0)
        def _():
            dqn_ref[...] = jnp.zeros_like(dqn_ref)
            dkn_ref[...] = jnp.zeros_like(dkn_ref)

        _merge_views((dq4, dk4, dv4), (dq16, dk16, dv16), stage, nat4, nat16)
        cos_t, sin_t, bdm = cos_ref[...], sin_ref[...], bd_ref[...]
        for grp, (part1, raw_ref, nw_ref, dn_ref) in enumerate(((dq1, qraw_ref, qnw_ref, dqn_ref),
                                                                (dk1, kraw_ref, knw_ref, dkn_ref))):
            dn_acc = 0.0
            for p in range(4):
                cols = slice(128 * p, 128 * (p + 1))
                d_rope = part1[:, cols] + nat4[4 * grp + p] + nat16[4 * grp + p]
                d_norm = d_rope * cos_t + _swap_halves(d_rope * sin_t)
                t = raw_ref[:, cols]
                w = nw_ref[...]
                r = lax.rsqrt(_mm_split(t * t, bdm) * (1.0 / HEAD_DIM) + EPS)
                gw = d_norm * w
                corr = _mm_split(gw * t, bdm) * (1.0 / HEAD_DIM)
                dt = r * (gw - t * ((r * r) * corr))
                dn_acc = dn_acc + jnp.sum(d_norm * t * r, axis=0, keepdims=True)
                dproj_ref[:, D_GRP * grp + 128 * p:D_GRP * grp + 128 * (p + 1)] = dt.astype(BF16)
            dn_ref[...] += dn_acc
        dproj_ref[:, 2 * D_GRP:3 * D_GRP] = (dv1[...] + _slab_group(nat4, 2) + _slab_group(nat16, 2)).astype(BF16)
        dproj_ref[:, 3 * D_GRP:4 * D_GRP] = dqs_ref[...].astype(BF16)
        dproj_ref[:, 4 * D_GRP:5 * D_GRP] = dks_ref[...].astype(BF16)
        dproj_ref[:, 5 * D_GRP:6 * D_GRP] = dvs_ref[...].astype(BF16)

    row = lambda w: pl.BlockSpec((tm, w), lambda i: (i, 0))
    return pl.pallas_call(
        body, name="qkv_bwd", grid=(ni,),
        in_specs=[row(D_GRP)] * 3 + [_view_spec(tm, 4)] * 3 + [_view_spec(tm, 16)] * 3 + [row(D_GRP)] * 5
        + [row(128), row(128), _full((1, 128)), _full((1, 128)), _full((128, 128))],
        out_specs=(row(D_IN), _full((1, 128)), _full((1, 128))),
        out_shape=(jax.ShapeDtypeStruct((s_len, D_IN), BF16), jax.ShapeDtypeStruct((1, 128), F32),
                   jax.ShapeDtypeStruct((1, 128), F32)),
        scratch_shapes=[pltpu.VMEM((12, tm, 128), F32)] * 3,
        compiler_params=_params(),
    )(dq_b[0], dk_b[0], dv_b[0], dq_b[1], dk_b[1], dv_b[1], dq_b[2], dk_b[2], dv_b[2],
      dqs, dks, dvs, qraw, kraw, cos2, sin2, qnw, knw, bd)


def _in_bwd_dx(dproj, a_g, x2, dx1, wn1, rides):
    s_len = x2.shape[0]
    tm = ROW_TILE
    ni = s_len // tm

    def body(dp_ref, w_ref, x_ref, dx1_ref, wn_ref, gx_ref, dwn_ref, w_full):
        i = pl.program_id(0)

        @pl.when(i == 0)
        def _():
            dwn_ref[...] = jnp.zeros_like(dwn_ref)
            for d in range(N_DEV):
                w_full[:, IN_SHARD * d:IN_SHARD * (d + 1)] = w_ref[d]

        dh = _dot_nt(dp_ref[...], w_full[...])
        dnorm, dw_rows = _rms_bwd(dh, x_ref[...], wn_ref[...])
        gx_ref[...] = dx1_ref[...] + dnorm
        dwn_ref[...] += jnp.sum(dw_rows, axis=0, keepdims=True)

    row = lambda w: pl.BlockSpec((tm, w), lambda i: (i, 0))
    return _call_with_exchange(
        body, rides, lambda: pl.program_id(0) == 0, lambda: pl.program_id(0) == ni - 1,
        name="in_bwd_dx", grid=(ni,),
        in_specs=[row(D_IN), pl.BlockSpec((N_DEV, D_MODEL, IN_SHARD), lambda i: (0, 0, 0)),
                  row(D_MODEL), row(D_MODEL), _full((1, D_MODEL))],
        out_specs=(row(D_MODEL), _full((1, D_MODEL))),
        out_shape=(jax.ShapeDtypeStruct((s_len, D_MODEL), F32), jax.ShapeDtypeStruct((1, D_MODEL), F32)),
        scratch_shapes=[pltpu.VMEM((D_MODEL, D_IN), BF16)],
    )(dproj, a_g, x2, dx1, wn1, *_ride_arrays(rides))


def _in_bwd_dw(h1, dproj, part, rides):
    s_len = h1.shape[0]
    tm = DW_ROW_TILE
    ni = s_len // tm

    def body(h_ref, dp_ref, dw_ref, acc):
        i = pl.program_id(1)

        @pl.when(i == 0)
        def _():
            acc[...] = jnp.zeros_like(acc)

        acc[...] += _dot_tn(h_ref[...], dp_ref[...])

        @pl.when(i == ni - 1)
        def _():
            for half in range(2):
                dw_ref[half] = acc[:, IN_SHARD * half:IN_SHARD * (half + 1)].astype(BF16)

    return _call_with_exchange(
        body, rides,
        lambda: jnp.logical_and(pl.program_id(0) == 0, pl.program_id(1) == 0),
        lambda: jnp.logical_and(pl.program_id(0) == 1, pl.program_id(1) == ni - 1),
        name=f"in_bwd_dw_{part}", grid=(2, ni),
        in_specs=[pl.BlockSpec((tm, D_MODEL), lambda d, i: (i, 0)),
                  pl.BlockSpec((tm, 2 * IN_SHARD), lambda d, i: (i, 2 * part + d))],
        out_specs=pl.BlockSpec((2, D_MODEL, IN_SHARD), lambda d, i: (d, 0, 0)),
        out_shape=jax.ShapeDtypeStruct((N_DEV // 2, D_MODEL, IN_SHARD), BF16),
        scratch_shapes=[pltpu.VMEM((D_MODEL, 2 * IN_SHARD), F32)],
    )(h1, dproj, *_ride_arrays(rides))


def _adamw(recv, w, m, v, recv_hi=None):
    rows, cols = w.shape
    tr = next((t for t in (128, 32) if rows % t == 0), rows)
    recvs = [recv] if recv_hi is None else [recv, recv_hi]

    def body(*refs):
        p_refs = refs[:len(recvs)]
        w_ref, m_ref, v_ref, g_ref, d_ref, nm_ref, nv_ref = refs[len(recvs):]

        def slot(s):
            if len(p_refs) == 1:
                return p_refs[0][s].astype(F32)
            low_half = _flat_index(_mesh_pos()) < N_DEV // 2
            return jnp.where(low_half, p_refs[0][s], p_refs[1][s]).astype(F32)

        g = slot(0)
        for s in range(1, N_DEV):
            g = g + slot(s)
        m_new = ADAM_B1 * m_ref[...] + (1.0 - ADAM_B1) * g
        v_new = ADAM_B2 * v_ref[...] + (1.0 - ADAM_B2) * (g * g)
        m_hat = m_new / (1.0 - ADAM_B1 ** ADAM_STEP)
        v_hat = v_new / (1.0 - ADAM_B2 ** ADAM_STEP)
        g_ref[...] = g
        d_ref[...] = -ADAM_LR * (m_hat / (jnp.sqrt(v_hat) + ADAM_EPS) + ADAM_WD * w_ref[...])
        nm_ref[...] = m_new
        nv_ref[...] = v_new

    blk = pl.BlockSpec((tr, cols), lambda i: (i, 0))
    out = jax.ShapeDtypeStruct((rows, cols), F32)
    return pl.pallas_call(
        body, name=f"adamw_{rows}x{cols}", grid=(rows // tr,),
        in_specs=[pl.BlockSpec((N_DEV, tr, cols), lambda i: (0, i, 0))] * len(recvs) + [blk, blk, blk],
        out_specs=(blk,) * 4, out_shape=(out,) * 4,
        compiler_params=_params(),
    )(*recvs, w, m, v)


def _rope_tables(s_len):
    pos = jnp.arange(s_len, dtype=F32)
    inv_freq = ROPE_THETA ** (-jnp.arange(0, HEAD_DIM, 2, dtype=F32) / HEAD_DIM)
    ang = pos[:, None] * inv_freq[None, :]
    cos, sin = jnp.cos(ang), jnp.sin(ang)
    cos2 = jnp.concatenate([cos, cos, cos, cos], axis=1)
    sin2 = jnp.concatenate([-sin, sin, -sin, sin], axis=1)
    return cos2, sin2


def _block_diag_ones(n):
    i = jnp.arange(n)
    return (i[:, None] // HEAD_DIM == i[None, :] // HEAD_DIM).astype(BF16)


def _pad_cols(t):
    return jnp.pad(t, ((0, 0), (0, FF_PAD - FF_SHARD)))


def _pad_rows(t):
    return jnp.pad(t, ((0, FF_PAD - FF_SHARD), (0, 0)))


LOSS_ROW = 26


def _pack_small(n1, n2, ndil, nsb, nq, nk, scalar=None):
    pad = lambda t: jnp.pad(t.reshape(1, -1), ((0, 0), (0, 128 - t.size)))
    last = jnp.zeros((1, 128), F32) if scalar is None else pad(scalar)
    rows = [n1.reshape(8, 128), n2.reshape(8, 128), ndil.reshape(4, 128), nsb.reshape(4, 128),
            pad(nq), pad(nk), last, jnp.zeros((5, 128), F32)]
    return jnp.concatenate(rows, axis=0)


def _unpack_small(t):
    return (t[0:8].reshape(1, D_MODEL), t[8:16].reshape(1, D_MODEL), t[16:20].reshape(1, D_GRP),
            t[20:24].reshape(1, D_GRP), t[24:25, :HEAD_DIM], t[25:26, :HEAD_DIM])


def kernel(x, attn_norm_w, w_in, q_norm_w, k_norm_w, dil_out_norm_w, sb_out_norm_w, w_out, ffn_norm_w, w_gate, w_up, w_down, loss_target, m_attn_norm_w, m_w_in, m_q_norm_w, m_k_norm_w, m_dil_out_norm_w, m_sb_out_norm_w, m_w_out, m_ffn_norm_w, m_w_gate, m_w_up, m_w_down, v_attn_norm_w, v_w_in, v_q_norm_w, v_k_norm_w, v_dil_out_norm_w, v_sb_out_norm_w, v_w_out, v_ffn_norm_w, v_w_gate, v_w_up, v_w_down):
    s_len = x.shape[1]
    x2, tgt = x[0], loss_target[0]

    (a_g,) = _gather_weights([w_in[0].astype(BF16)])
    gate_loc = _pad_cols(w_gate[0]).T.astype(BF16)
    up_loc = _pad_cols(w_up[0]).T.astype(BF16)
    down_loc = _pad_rows(w_down[0]).astype(BF16)
    out_loc = w_out[0].astype(BF16)

    cos2, sin2 = _rope_tables(s_len)
    bd128, bd512 = _block_diag_ones(128), _block_diag_ones(D_GRP)
    idx = jnp.arange(SB_TILE)
    tri_suf = (idx[:, None] > idx[None, :]).astype(BF16)
    tri_pre = (idx[:, None] < idx[None, :]).astype(BF16)
    qnw2 = jnp.concatenate([q_norm_w, q_norm_w], axis=1)
    knw2 = jnp.concatenate([k_norm_w, k_norm_w], axis=1)

    (h1, qraw, kraw, q, k, va, qs, ks, vs, q4, k4, v4, q16, k16, v16,
     out_g, gate_g) = _attn_in(x2, attn_norm_w, a_g, cos2, sin2, qnw2, knw2, bd128, shards=[out_loc, gate_loc])
    qkv_views = {1: (q, k, va), 4: (q4, k4, v4), 16: (q16, k16, v16)}
    o_b, lse_b = [], []
    for r in DILATIONS:
        o, lse = _dil_fwd(*qkv_views[r], r)
        o_b.append(o)
        lse_b.append(lse)
    o_sb, c_sb, up_g = _sb_fwd(qs, ks, vs, tri_suf, shards=[up_loc])
    o_dil, lse_tot, lse4, lse16, mixed, x1, down_g = _attn_out(
        o_b, lse_b, o_sb, x2, dil_out_norm_w, sb_out_norm_w, out_g, shards=[down_loc])
    g, u, h2, dy, loss_parts = _ffn_fwd(x1, ffn_norm_w, tgt, gate_g, up_g, down_g)
    loss_local = jnp.sum(loss_parts[::8, 0])

    dg, du, act, dh2 = _ffn_bwd_dx(dy, g, u, gate_g, up_g, down_g)
    (dx1, do_dil, delta, do_sb, dwout, dn2, dndil, dnsb, do4, dl4, do16, dl16) = _attn_out_bwd(
        dy, dh2, x1, ffn_norm_w, out_g, mixed, o_dil, o_sb, dil_out_norm_w, sb_out_norm_w, bd512)
    dwg, dwu, dwd = _ffn_bwd_dw(h2, dy, dg, du, act)
    dqs, dks, dvs, r_gate, r_down = _sb_bwd(qs, ks, vs, do_sb, c_sb, tri_suf, tri_pre, rides=[dwg, dwd])
    cot_views = {1: (do_dil, lse_tot, delta), 4: (do4, lse4, dl4), 16: (do16, lse16, dl16)}
    riders = {1: [], 4: [dwout], 16: [dwu]}
    dq_b, dk_b, dv_b, landed = [], [], [], {}
    for r in DILATIONS:
        dq, dk, dv, *landed[r] = _dil_bwd(*qkv_views[r], *cot_views[r], r, rides=riders[r])
        dq_b.append(dq)
        dk_b.append(dk)
        dv_b.append(dv)
    (r_out,), (r_up,) = landed[4], landed[16]
    dproj, dqn2, dkn2 = _qkv_bwd(dq_b, dk_b, dv_b, dqs, dks, dvs, qraw, kraw, cos2, sin2, qnw2, knw2, bd128)
    (dwin_lo,) = _in_bwd_dw(h1, dproj, 0, rides=[])
    dwin_hi, r_in_lo = _in_bwd_dw(h1, dproj, 1, rides=[(dwin_lo, 0)])
    grad_x, dn1, r_in_hi = _in_bwd_dx(dproj, a_g, x2, dx1, attn_norm_w, rides=[(dwin_hi, N_DEV // 2)])
    dqn = dqn2[:, :HEAD_DIM] + dqn2[:, HEAD_DIM:]
    dkn = dkn2[:, :HEAD_DIM] + dkn2[:, HEAD_DIM:]

    small = _pack_small(dn1, dn2, dndil, dnsb, dqn, dkn, loss_local)
    (r_small,) = _exchange_grads([], small)
    big = {
        "w_in": _adamw(r_in_lo, w_in[0], m_w_in[0], v_w_in[0], recv_hi=r_in_hi),
        "w_gate": tuple(t.T for t in _adamw(r_gate, w_gate[0].T, m_w_gate[0].T, v_w_gate[0].T)),
        "w_up": tuple(t.T for t in _adamw(r_up, w_up[0].T, m_w_up[0].T, v_w_up[0].T)),
        "w_down": _adamw(r_down, w_down[0], m_w_down[0], v_w_down[0]),
        "w_out": _adamw(r_out, w_out[0], m_w_out[0], v_w_out[0]),
    }
    packs = [_pack_small(*ts) for ts in (
        (attn_norm_w, ffn_norm_w, dil_out_norm_w, sb_out_norm_w, q_norm_w, k_norm_w),
        (m_attn_norm_w, m_ffn_norm_w, m_dil_out_norm_w, m_sb_out_norm_w, m_q_norm_w, m_k_norm_w),
        (v_attn_norm_w, v_ffn_norm_w, v_dil_out_norm_w, v_sb_out_norm_w, v_q_norm_w, v_k_norm_w))]
    small_raw = _adamw(r_small, *packs)
    loss = small_raw[0][LOSS_ROW, 0]
    small_out = [_unpack_small(t) for t in small_raw]
    names = ["attn_norm_w", "w_in", "q_norm_w", "k_norm_w", "dil_out_norm_w", "sb_out_norm_w", "w_out",
             "ffn_norm_w", "w_gate", "w_up", "w_down"]
    small_pos = {"attn_norm_w": 0, "ffn_norm_w": 1, "dil_out_norm_w": 2, "sb_out_norm_w": 3,
                 "q_norm_w": 4, "k_norm_w": 5}
    outs = [loss, grad_x[None]]
    for kind in range(4):
        for name in names:
            if name in small_pos:
                outs.append(small_out[kind][small_pos[name]])
            else:
                outs.append(big[name][kind][None])
    return tuple(outs)
```

```python
import jax
import jax.numpy as jnp
from jax import lax
from jax.experimental import pallas as pl
from jax.experimental.pallas import tpu as pltpu

F32 = jnp.float32
BF16 = jnp.bfloat16

N_DEV = 8
D_MODEL = 1024
HEAD_DIM = 64
D_GRP = 512
D_IN = 6 * D_GRP
IN_SHARD = D_IN // N_DEV
FF_SHARD = 352
FF_PAD = 384
FF_BLOCK = 2 * FF_PAD
FF_STEPS = N_DEV // 2
OUT_SHARD = D_MODEL // N_DEV
BLOCK = 128
DILATIONS = (1, 4, 16)
ROPE_THETA = 10000.0
EPS = 1e-6
ATT_SCALE = HEAD_DIM ** -0.5
NEG = -1e30

ADAM_LR = 0.001
ADAM_B1 = 0.9
ADAM_B2 = 0.999
ADAM_EPS = 1e-08
ADAM_WD = 0.01
ADAM_STEP = 10

SB_TILE = 256
SB_DEAD = -104.0
SB_PAIRS = 4
SB_BWD_PAIRS = 2
ROW_TILE = 512
DW_ROW_TILE = 1024
VMEM_LIMIT = 56 * 1024 * 1024
MESH = pl.DeviceIdType.MESH


def _dot(a, b):
    return jnp.dot(a, b, preferred_element_type=F32)


def _dot_nt(a, b):
    return lax.dot_general(a, b, (((1,), (1,)), ((), ())), preferred_element_type=F32)


def _dot_tn(a, b):
    return lax.dot_general(a, b, (((0,), (0,)), ((), ())), preferred_element_type=F32)


def _mm_split(t, m):
    hi = t.astype(BF16)
    lo = (t - hi.astype(F32)).astype(BF16)
    return _dot(hi, m) + _dot(lo, m)


def _params(**kw):
    return pltpu.CompilerParams(vmem_limit_bytes=VMEM_LIMIT, **kw)


def _full(shape):
    nd = len(shape)
    return pl.BlockSpec(shape, lambda *_: (0,) * nd)


def _view_shape(s_len, r, dtype):
    return jax.ShapeDtypeStruct((s_len // r, r * D_GRP), dtype)


def _view_spec(tm, r):
    return pl.BlockSpec((tm // r, r * D_GRP), lambda i: (i, 0))


def _swap_halves(t):
    lane = lax.broadcasted_iota(jnp.int32, t.shape, 1)
    first = (lane & 32) == 0
    return jnp.where(first, pltpu.roll(t, 96, 1), pltpu.roll(t, 32, 1))


def _log_sigmoid_pair(z):
    neg_abs = lax.bitcast_convert_type(lax.bitcast_convert_type(z, jnp.uint32) | jnp.uint32(0x80000000), F32)
    lb = jnp.minimum(z, 0.0) - jnp.log(1.0 + jnp.exp(neg_abs))
    return lb, lb - z


def _cumsum_mm(t, tri):
    return _dot(t.astype(BF16), tri)


def _split_views(src_ref, stage_ref, views4, views16):
    slabs, n, _ = src_ref.shape
    n4, n16 = n // 4, n // 16
    for j in range(slabs):
        g, lanes = j // 4, 128 * (j % 4)
        src, stage = src_ref.at[j], stage_ref.at[j]
        for c4 in range(4):
            blk = src[pl.ds(c4, n4, stride=4), :]
            stage[n4 * c4:n4 * (c4 + 1), :] = blk
            col = D_GRP * c4 + lanes
            views4[g][:, col:col + 128] = blk.astype(views4[g].dtype)
        for c4 in range(4):
            for c1 in range(4):
                blk = stage[pl.ds(n4 * c4 + c1, n16, stride=4), :]
                col = D_GRP * (4 * c1 + c4) + lanes
                views16[g][:, col:col + 128] = blk.astype(views16[g].dtype)


def _merge_views(views4, views16, stage_ref, dst4_ref, dst16_ref):
    slabs, n, _ = dst4_ref.shape
    n4, n16 = n // 4, n // 16
    for j in range(slabs):
        g, lanes = j // 4, 128 * (j % 4)
        dst4, dst16, stage = dst4_ref.at[j], dst16_ref.at[j], stage_ref.at[j]
        for c4 in range(4):
            col = D_GRP * c4 + lanes
            dst4[pl.ds(c4, n4, stride=4), :] = views4[g][:, col:col + 128].astype(F32)
            for c1 in range(4):
                col = D_GRP * (4 * c1 + c4) + lanes
                stage[pl.ds(n4 * c4 + c1, n16, stride=4), :] = views16[g][:, col:col + 128].astype(F32)
        for c4 in range(4):
            dst16[pl.ds(c4, n4, stride=4), :] = stage[n4 * c4:n4 * (c4 + 1), :]


def _slab_group(ref, g):
    return jnp.concatenate([ref[4 * g + p] for p in range(4)], axis=1)


def _mesh_pos():
    return lax.axis_index("x"), lax.axis_index("y"), lax.axis_index("c")


def _flat_index(p):
    return 4 * p[0] + 2 * p[1] + p[2]


def _gather_weights(shards):
    n_arr = len(shards)

    def body(*refs):
        srcs, outs = refs[:n_arr], refs[n_arr:2 * n_arr]
        send_sems, recv_sems, local_sems = refs[2 * n_arr:]
        x, y, c = _mesh_pos()
        me, sibling = (x, y, c), (x, y, 1 - c)
        chips = [(1 - x, y), (x, 1 - y), (1 - x, 1 - y)]

        def copy(arr, k, block, to, own=False):
            dst = outs[arr].at[_flat_index(block)]
            return pltpu.make_async_remote_copy(
                src_ref=srcs[arr] if own else dst, dst_ref=dst,
                send_sem=send_sems.at[arr, k], recv_sem=recv_sems.at[arr, k],
                device_id=to, device_id_type=MESH)

        for arr in range(n_arr):
            mine = pltpu.make_async_copy(srcs[arr], outs[arr].at[_flat_index(me)], local_sems.at[arr])
            mine.start()
            first = [copy(arr, 0, me, sibling, own=True)]
            first += [copy(arr, 1 + j, me, (*chip, c), own=True) for j, chip in enumerate(chips)]
            for cp in first:
                cp.start()
        for arr in range(n_arr):
            passed = [copy(arr, 4 + j, (*chip, c), sibling) for j, chip in enumerate(chips)]
            for j, chip in enumerate(chips):
                copy(arr, 1 + j, (*chip, c), me).wait_recv()
                passed[j].start()
        for arr in range(n_arr):
            copy(arr, 0, sibling, me).wait_recv()
            for j, chip in enumerate(chips):
                copy(arr, 4 + j, (*chip, 1 - c), me).wait_recv()
            for k in range(7):
                copy(arr, k, me, me).wait_send()
            pltpu.make_async_copy(srcs[arr], outs[arr].at[_flat_index(me)], local_sems.at[arr]).wait()

    any_spec = pl.BlockSpec(memory_space=pl.ANY)
    return pl.pallas_call(
        body, name="gather_weights",
        out_shape=tuple(jax.ShapeDtypeStruct((N_DEV,) + s.shape, s.dtype) for s in shards),
        in_specs=[any_spec] * n_arr, out_specs=(any_spec,) * n_arr,
        scratch_shapes=[pltpu.SemaphoreType.DMA((n_arr, 7)), pltpu.SemaphoreType.DMA((n_arr, 7)),
                        pltpu.SemaphoreType.DMA((n_arr,))],
        compiler_params=pltpu.CompilerParams(has_side_effects=True),
    )(*shards)


def _peer_list(x, y, c):
    return [(1 - x if m & 4 else x, 1 - y if m & 2 else y, 1 - c if m & 1 else c) for m in range(1, N_DEV)]


def _exchange_grads(parts, small):
    n_arr = len(parts)

    def body(*refs):
        ins, outs = refs[:n_arr + 1], refs[n_arr + 1:2 * (n_arr + 1)]
        send_sems, recv_sems, local_sems = refs[2 * (n_arr + 1):]
        x, y, c = _mesh_pos()
        me = (x, y, c)
        my_idx = _flat_index(me)
        peers = []
        for m in range(1, N_DEV):
            peers.append((1 - x if m & 4 else x, 1 - y if m & 2 else y, 1 - c if m & 1 else c))

        def src_block(arr, dev):
            return ins[arr] if arr == n_arr else ins[arr].at[_flat_index(dev)]

        def copy(arr, k):
            return pltpu.make_async_remote_copy(
                src_ref=src_block(arr, peers[k]), dst_ref=outs[arr].at[my_idx],
                send_sem=send_sems.at[arr, k], recv_sem=recv_sems.at[arr, k],
                device_id=peers[k], device_id_type=MESH)

        def local(arr):
            return pltpu.make_async_copy(src_block(arr, me), outs[arr].at[my_idx], local_sems.at[arr])

        for arr in range(n_arr + 1):
            local(arr).start()
            for k in range(N_DEV - 1):
                copy(arr, k).start()
        for arr in range(n_arr + 1):
            for k in range(N_DEV - 1):
                cp = copy(arr, k)
                cp.wait_send()
                cp.wait_recv()
            local(arr).wait()

    any_spec = pl.BlockSpec(memory_space=pl.ANY)
    out_shape = tuple(jax.ShapeDtypeStruct(p.shape, p.dtype) for p in parts)
    out_shape += (jax.ShapeDtypeStruct((N_DEV,) + small.shape, small.dtype),)
    return pl.pallas_call(
        body, name="exchange_grads",
        out_shape=out_shape,
        in_specs=[any_spec] * (n_arr + 1), out_specs=(any_spec,) * (n_arr + 1),
        scratch_shapes=[pltpu.SemaphoreType.DMA((n_arr + 1, N_DEV - 1)),
                        pltpu.SemaphoreType.DMA((n_arr + 1, N_DEV - 1)),
                        pltpu.SemaphoreType.DMA((n_arr + 1,))],
        compiler_params=pltpu.CompilerParams(has_side_effects=True),
    )(*parts, small)


def _call_with_gather(body, shards, first_step, mid_step, last_step, *, name, grid, in_specs, out_specs,
                      out_shape, scratch_shapes=()):
    out_specs = tuple(out_specs) if isinstance(out_specs, (tuple, list)) else (out_specs,)
    out_shape = tuple(out_shape) if isinstance(out_shape, (tuple, list)) else (out_shape,)
    n_in, n_out, n_scr, n = len(in_specs), len(out_specs), len(scratch_shapes), len(shards)

    def full_body(*refs):
        ins, srcs = refs[:n_in], refs[n_in:n_in + n]
        outs, lands = refs[n_in + n:n_in + n + n_out], refs[n_in + n + n_out:n_in + 2 * n + n_out]
        scratch = refs[n_in + 2 * n + n_out:n_in + 2 * n + n_out + n_scr]
        send_sems, recv_sems, local_sems = refs[-3:]
        x, y, c = _mesh_pos()
        me, sibling = (x, y, c), (x, y, 1 - c)
        chips = [(1 - x, y), (x, 1 - y), (1 - x, 1 - y)]

        def copy(a, k, block, to, own=False):
            dst = lands[a].at[_flat_index(block)]
            return pltpu.make_async_remote_copy(
                src_ref=srcs[a] if own else dst, dst_ref=dst,
                send_sem=send_sems.at[a, k], recv_sem=recv_sems.at[a, k],
                device_id=to, device_id_type=MESH)

        def local(a):
            return pltpu.make_async_copy(srcs[a], lands[a].at[_flat_index(me)], local_sems.at[a])

        @pl.when(first_step())
        def _():
            for a in range(n):
                local(a).start()
                copy(a, 0, me, sibling, own=True).start()
                for j, chip in enumerate(chips):
                    copy(a, 1 + j, me, (*chip, c), own=True).start()

        @pl.when(mid_step())
        def _():
            for a in range(n):
                for j, chip in enumerate(chips):
                    copy(a, 1 + j, (*chip, c), me).wait_recv()
                    copy(a, 4 + j, (*chip, c), sibling).start()

        body(*ins, *outs, *scratch)

        @pl.when(last_step())
        def _():
            for a in range(n):
                copy(a, 0, sibling, me).wait_recv()
                for j, chip in enumerate(chips):
                    copy(a, 4 + j, (*chip, 1 - c), me).wait_recv()
                for k in range(N_DEV - 1):
                    copy(a, k, me, me).wait_send()
                local(a).wait()

    any_spec = pl.BlockSpec(memory_space=pl.ANY)
    return pl.pallas_call(
        full_body, name=name, grid=grid,
        in_specs=list(in_specs) + [any_spec] * n,
        out_specs=out_specs + (any_spec,) * n,
        out_shape=out_shape + tuple(jax.ShapeDtypeStruct((N_DEV,) + t.shape, t.dtype) for t in shards),
        scratch_shapes=list(scratch_shapes) + [pltpu.SemaphoreType.DMA((n, N_DEV - 1)),
                                               pltpu.SemaphoreType.DMA((n, N_DEV - 1)),
                                               pltpu.SemaphoreType.DMA((n,))],
        compiler_params=_params(has_side_effects=True),
    )


def _ride_arrays(rides):
    return [r[0] if isinstance(r, tuple) else r for r in rides]


def _call_with_exchange(body, rides, first_step, last_step, *, name, grid, in_specs, out_specs, out_shape,
                        scratch_shapes=()):
    out_specs = tuple(out_specs) if isinstance(out_specs, (tuple, list)) else (out_specs,)
    out_shape = tuple(out_shape) if isinstance(out_shape, (tuple, list)) else (out_shape,)
    n_in, n_out, n_scr, n = len(in_specs), len(out_specs), len(scratch_shapes), len(rides)
    if n == 0:
        return pl.pallas_call(body, name=name, grid=grid, in_specs=list(in_specs), out_specs=out_specs,
                              out_shape=out_shape, scratch_shapes=list(scratch_shapes),
                              compiler_params=_params())

    def full_body(*refs):
        ins, srcs = refs[:n_in], refs[n_in:n_in + n]
        outs, lands = refs[n_in + n:n_in + n + n_out], refs[n_in + n + n_out:n_in + 2 * n + n_out]
        scratch = refs[n_in + 2 * n + n_out:n_in + 2 * n + n_out + n_scr]
        send_sems, recv_sems, local_sems = refs[-3:]
        x, y, c = _mesh_pos()
        my_idx = _flat_index((x, y, c))
        peers = _peer_list(x, y, c)

        def remote(a, k):
            return pltpu.make_async_remote_copy(
                src_ref=srcs[a].at[_flat_index(peers[k])], dst_ref=lands[a].at[my_idx],
                send_sem=send_sems.at[a, k], recv_sem=recv_sems.at[a, k],
                device_id=peers[k], device_id_type=MESH)

        def local(a):
            return pltpu.make_async_copy(srcs[a].at[my_idx], lands[a].at[my_idx], local_sems.at[a])

        @pl.when(first_step())
        def _():
            for a in range(n):
                local(a).start()
                for k in range(N_DEV - 1):
                    remote(a, k).start()

        body(*ins, *outs, *scratch)

        @pl.when(last_step())
        def _():
            for a in range(n):
                for k in range(N_DEV - 1):
                    cp = remote(a, k)
                    cp.wait_send()
                    cp.wait_recv()
                local(a).wait()

    any_spec = pl.BlockSpec(memory_space=pl.ANY)
    res = pl.pallas_call(
        full_body, name=name, grid=grid,
        in_specs=list(in_specs) + [any_spec] * n,
        out_specs=out_specs + (any_spec,) * n,
        out_shape=out_shape + tuple(jax.ShapeDtypeStruct(t.shape, t.dtype) for t in rides),
        scratch_shapes=list(scratch_shapes) + [pltpu.SemaphoreType.DMA((n, N_DEV - 1)),
                                               pltpu.SemaphoreType.DMA((n, N_DEV - 1)),
                                               pltpu.SemaphoreType.DMA((n,))],
        compiler_params=_params(has_side_effects=True),
    )
    return res


def _head_norm(t, w128, bd):
    ms = _mm_split(t * t, bd) * (1.0 / HEAD_DIM)
    r = lax.rsqrt(ms + EPS)
    return (t * r) * w128, r


def _attn_in(x2, wn1, a_g, cos2, sin2, qnw, knw, bd, shards):
    s_len = x2.shape[0]
    tm = ROW_TILE

    def body(x_ref, wn_ref, w_ref, cos_ref, sin_ref, qnw_ref, knw_ref, bd_ref,
             h1_ref, qraw_ref, kraw_ref, q_ref, k_ref, va_ref, qs_ref, ks_ref, vs_ref,
             q4_ref, k4_ref, v4_ref, q16_ref, k16_ref, v16_ref, proj, slabs, stage, w_full):
        @pl.when(pl.program_id(0) == 0)
        def _():
            for d in range(N_DEV):
                w_full[:, IN_SHARD * d:IN_SHARD * (d + 1)] = w_ref[d]

        xx = x_ref[...]
        r = lax.rsqrt(jnp.mean(xx * xx, axis=-1, keepdims=True) + EPS)
        h = ((xx * r) * wn_ref[...]).astype(BF16)
        h1_ref[...] = h
        proj[...] = _dot(h, w_full[...])
        cos_t, sin_t, bdm = cos_ref[...], sin_ref[...], bd_ref[...]
        for grp, (raw_ref, rope_ref, nw_ref) in enumerate(((qraw_ref, q_ref, qnw_ref),
                                                           (kraw_ref, k_ref, knw_ref))):
            for p in range(4):
                cols = slice(D_GRP * grp + 128 * p, D_GRP * grp + 128 * (p + 1))
                t = proj[:, cols]
                raw_ref[:, 128 * p:128 * (p + 1)] = t
                yn, _ = _head_norm(t, nw_ref[...], bdm)
                roped = yn * cos_t + _swap_halves(yn) * sin_t
                slabs[4 * grp + p] = roped
                rope_ref[:, 128 * p:128 * (p + 1)] = roped.astype(BF16)
        for p in range(4):
            slabs[8 + p] = proj[:, 2 * D_GRP + 128 * p:2 * D_GRP + 128 * (p + 1)]
        for grp, ref in ((2, va_ref), (3, qs_ref), (4, ks_ref), (5, vs_ref)):
            ref[...] = proj[:, D_GRP * grp:D_GRP * (grp + 1)].astype(BF16)
        _split_views(slabs, stage, (q4_ref, k4_ref, v4_ref), (q16_ref, k16_ref, v16_ref))

    row = lambda w: pl.BlockSpec((tm, w), lambda i: (i, 0))
    grp_bf = jax.ShapeDtypeStruct((s_len, D_GRP), BF16)
    grp_f32 = jax.ShapeDtypeStruct((s_len, D_GRP), F32)
    ni = s_len // tm
    return _call_with_gather(
        body, shards, lambda: pl.program_id(0) == 0, lambda: pl.program_id(0) == ni - 2,
        lambda: pl.program_id(0) == ni - 1,
        name="attn_in", grid=(ni,),
        in_specs=[row(D_MODEL), _full((1, D_MODEL)),
                  pl.BlockSpec((N_DEV, D_MODEL, IN_SHARD), lambda i: (0, 0, 0)),
                  row(128), row(128), _full((1, 128)), _full((1, 128)), _full((128, 128))],
        out_specs=(row(D_MODEL),) + (row(D_GRP),) * 8 + (_view_spec(tm, 4),) * 3 + (_view_spec(tm, 16),) * 3,
        out_shape=(jax.ShapeDtypeStruct((s_len, D_MODEL), BF16), grp_f32, grp_f32) + (grp_bf,) * 6
        + (_view_shape(s_len, 4, BF16),) * 3 + (_view_shape(s_len, 16, BF16),) * 3,
        scratch_shapes=[pltpu.VMEM((tm, D_IN), F32), pltpu.VMEM((12, tm, 128), F32), pltpu.VMEM((12, tm, 128), F32),
                        pltpu.VMEM((D_MODEL, D_IN), BF16)],
    )(x2, wn1, a_g, cos2, sin2, qnw, knw, bd, *shards)


def _band_mask(n):
    i = lax.broadcasted_iota(jnp.int32, (2 * BLOCK, 2 * BLOCK), 0) & (BLOCK - 1)
    j = lax.broadcasted_iota(jnp.int32, (2 * BLOCK, 2 * BLOCK), 1)
    dist = i + BLOCK - j
    return (dist >= 0) & (dist <= BLOCK) & ((n - 1) * BLOCK + j >= 0)


def _stack_heads(t2, head0):
    return jnp.concatenate([jnp.where(head0, t2, 0), jnp.where(head0, 0, t2)], axis=0)


def _unstack_heads(t, head0):
    return jnp.where(head0, t[0:BLOCK], t[BLOCK:2 * BLOCK])


def _dil_fwd(qv, kv, vv, r):
    sub_len = qv.shape[0]
    nb = sub_len // BLOCK

    qb = 2 if nb % 2 == 0 else 1

    def body(q_ref, kp_ref, kc_ref, vp_ref, vc_ref, o_ref, lse_ref):
        n = pl.program_id(1)
        lane = lax.broadcasted_iota(jnp.int32, (BLOCK, 128), 1)
        head0 = lane < HEAD_DIM
        units = [(b, slice(128 * p, 128 * (p + 1))) for b in range(qb) for p in range(4)]
        valid = [_band_mask(qb * n + b) for b in range(qb)]
        rows = [slice(BLOCK * b, BLOCK * (b + 1)) for b in range(qb)]

        def keys(prev_ref, cur_ref, b, c):
            before = prev_ref[:, c] if b == 0 else cur_ref[rows[b - 1], c]
            return jnp.concatenate([before, cur_ref[rows[b], c]], axis=0)

        qqs = [_stack_heads(q_ref[rows[b], c] * ATT_SCALE, head0) for b, c in units]
        kks = [keys(kp_ref, kc_ref, b, c) for b, c in units]
        vvs = [keys(vp_ref, vc_ref, b, c) for b, c in units]
        ss = [_dot_nt(qq, kk) for qq, kk in zip(qqs, kks)]
        prs, dens, lses = [], [], []
        for (b, _), s in zip(units, ss):
            s = jnp.where(valid[b], s, NEG)
            m = jnp.max(s, axis=-1, keepdims=True)
            pr = jnp.exp(s - m)
            den = jnp.sum(pr, axis=-1, keepdims=True)
            prs.append(pr.astype(BF16))
            dens.append(den)
            lses.append(m + jnp.log(den))
        pvs = [_dot(pr, vv2) for pr, vv2 in zip(prs, vvs)]
        for (b, c), pv, den, lse in zip(units, pvs, dens, lses):
            o_ref[rows[b], c] = _unstack_heads(pv / den, head0)
            lse_ref[rows[b], c] = _unstack_heads(jnp.broadcast_to(lse, (2 * BLOCK, 128)), head0)

    cur = pl.BlockSpec((qb * BLOCK, D_GRP), lambda c, n: (n, c))
    prev = pl.BlockSpec((BLOCK, D_GRP), lambda c, n: (jnp.maximum(qb * n - 1, 0), c))
    out = jax.ShapeDtypeStruct(qv.shape, F32)
    return pl.pallas_call(
        body, name=f"dil_fwd_r{r}", grid=(r, nb // qb),
        in_specs=[cur, prev, cur, prev, cur], out_specs=(cur, cur), out_shape=(out, out),
        compiler_params=_params(),
    )(qv, kv, kv, vv, vv)


def _dil_bwd(qv, kv, vv, dov, lsev, deltav, r, rides):
    sub_len = qv.shape[0]
    nb = sub_len // BLOCK

    def body(q_ref, kp_ref, kc_ref, vp_ref, vc_ref, do_ref, lse_ref, dl_ref,
             dq_ref, dk_ref, dv_ref, dk_carry, dv_carry):
        n = pl.program_id(1)

        @pl.when(n == 0)
        def _():
            dk_carry[...] = jnp.zeros_like(dk_carry)
            dv_carry[...] = jnp.zeros_like(dv_carry)

        @pl.when(n < nb)
        def _():
            valid = _band_mask(n)
            lane = lax.broadcasted_iota(jnp.int32, (BLOCK, 128), 1)
            head0 = lane < HEAD_DIM
            pairs = [slice(128 * p, 128 * (p + 1)) for p in range(4)]
            qqs = [_stack_heads(q_ref[:, c] * ATT_SCALE, head0) for c in pairs]
            dos = [_stack_heads(do_ref[:, c], head0) for c in pairs]
            kks = [jnp.concatenate([kp_ref[:, c], kc_ref[:, c]], axis=0) for c in pairs]
            vvs = [jnp.concatenate([vp_ref[:, c], vc_ref[:, c]], axis=0) for c in pairs]
            ss = [_dot_nt(qq, kk) for qq, kk in zip(qqs, kks)]
            dps = [_dot_nt(do, vv2) for do, vv2 in zip(dos, vvs)]
            def softmax_terms(p):
                stats = []
                for ref in (lse_ref, dl_ref):
                    t2 = ref[:, pairs[p]]
                    stats.append(jnp.concatenate(
                        [jnp.sum(jnp.where(lane == 0, t2, 0.0), axis=-1, keepdims=True),
                         jnp.sum(jnp.where(lane == HEAD_DIM, t2, 0.0), axis=-1, keepdims=True)], axis=0))
                pr = jnp.where(valid, jnp.exp(jnp.minimum(ss[p] - stats[0], 0.0)), 0.0)
                return pr.astype(BF16), (pr * (dps[p] - stats[1])).astype(BF16)

            terms = [softmax_terms(p) for p in range(4)]
            dqs = [_dot(terms[p][1], kks[p]) for p in range(4)]
            dkks = [_dot_tn(terms[p][1], qqs[p]) for p in range(4)]
            dvvs = [_dot_tn(terms[p][0], dos[p]) for p in range(4)]
            for c, dq, dkk, dvv in zip(pairs, dqs, dkks, dvvs):
                dq_ref[:, c] = _unstack_heads(dq, head0) * ATT_SCALE
                dk_ref[:, c] = dk_carry[:, c] + dkk[:BLOCK]
                dv_ref[:, c] = dv_carry[:, c] + dvv[:BLOCK]
                dk_carry[:, c] = dkk[BLOCK:]
                dv_carry[:, c] = dvv[BLOCK:]

        @pl.when(n == nb)
        def _():
            dk_ref[...] = dk_carry[...]
            dv_ref[...] = dv_carry[...]

    last = nb - 1
    cur = pl.BlockSpec((BLOCK, D_GRP), lambda c, n: (jnp.minimum(n, last), c))
    prev = pl.BlockSpec((BLOCK, D_GRP), lambda c, n: (jnp.clip(n - 1, 0, last), c))
    out = jax.ShapeDtypeStruct(qv.shape, F32)
    return _call_with_exchange(
        body, rides,
        lambda: jnp.logical_and(pl.program_id(0) == 0, pl.program_id(1) == 0),
        lambda: jnp.logical_and(pl.program_id(0) == r - 1, pl.program_id(1) == nb),
        name=f"dil_bwd_r{r}", grid=(r, nb + 1),
        in_specs=[cur, prev, cur, prev, cur, cur, cur, cur],
        out_specs=(cur, prev, prev), out_shape=(out, out, out),
        scratch_shapes=[pltpu.VMEM((BLOCK, D_GRP), F32), pltpu.VMEM((BLOCK, D_GRP), F32)],
    )(qv, kv, kv, vv, vv, dov, lsev, deltav, *rides)


def _sb_fwd(qs, ks, vs, tri_suf, shards):
    s_len = qs.shape[0]
    t = SB_TILE
    nq = s_len // t

    npair = SB_PAIRS

    def body(q_ref, k_ref, v_ref, u_ref, o_ref, c_ref, qq, vt, acc, cf, csave):
        row = lax.broadcasted_iota(jnp.int32, (2 * t, t), 0) & (t - 1)
        col = lax.broadcasted_iota(jnp.int32, (2 * t, t), 1)
        diag_mask = col < row
        lane1 = lax.broadcasted_iota(jnp.int32, (t, 128), 1)
        head0 = lane1 < HEAD_DIM
        lane2 = lax.broadcasted_iota(jnp.int32, (2 * t, 128), 1)
        uu = u_ref[...]
        pr = range(npair)
        cols = [slice(128 * pp, 128 * (pp + 1)) for pp in pr]

        i = pl.program_id(1)

        @pl.when(i == 0)
        def _():
            def transpose_v(j, _):
                rows = pl.ds(pl.multiple_of(j * t, t), t)
                for pp in pr:
                    vt[pp, j] = v_ref[rows, cols[pp]].astype(F32).T.astype(BF16)
                return 0

            lax.fori_loop(0, nq, transpose_v, 0)

        for pp in pr:
            q2 = q_ref[:, cols[pp]] * ATT_SCALE
            qq[pp, 0:t, :] = jnp.where(head0, q2, 0)
            qq[pp, t:2 * t, :] = jnp.where(head0, 0, q2)
        acc[...] = jnp.zeros_like(acc)
        cf[...] = jnp.zeros_like(cf)
        csave[...] = jnp.full(csave.shape, 2.0 * SB_DEAD, F32)

        def tile(kb, diag):
            krows = pl.ds(pl.multiple_of(kb * t, t), t)
            zs = [_dot_nt(qq[pp], k_ref[krows, cols[pp]]) for pp in pr]
            lbk = [_log_sigmoid_pair(z) for z in zs]
            lks = [jnp.where(diag_mask, lk, 0.0) if diag else lk for _, lk in lbk]
            sufs = [_cumsum_mm(lk, uu) for lk in lks]
            carries = [cf[pp] for pp in pr]
            avs = []
            for pp in pr:
                a = jnp.exp(lbk[pp][0] + (sufs[pp] + jnp.concatenate([carries[pp]] * (t // 128), axis=1)))
                avs.append((jnp.where(diag_mask, a, 0.0) if diag else a).astype(BF16))
            pvs = [_dot_nt(vt[pp, kb], avs[pp]) for pp in pr]
            for pp in pr:
                acc[pp] += pvs[pp]
                csave[pp] = jnp.where(lane2 == kb, carries[pp], csave[pp])
                cf[pp] = carries[pp] + jnp.broadcast_to(jnp.sum(lks[pp], axis=-1, keepdims=True), (2 * t, 128))

        tile(i, True)

        def alive():
            return jnp.max(cf[...]) > SB_DEAD

        def k_block(state):
            kb, _ = state
            tile(kb, False)
            return kb - 1, alive()

        lax.while_loop(lambda state: jnp.logical_and(state[0] >= 0, state[1]), k_block, (i - 1, alive()))
        for pp in pr:
            o_ref[:, cols[pp]] = jnp.where(head0, acc[pp, :, 0:t].T, acc[pp, :, t:2 * t].T)
            c_ref[2 * pp] = csave[pp, 0:t, :]
            c_ref[2 * pp + 1] = csave[pp, t:2 * t, :]

    width = 128 * npair
    kv = pl.BlockSpec((s_len, width), lambda p, i: (0, p))
    qo = pl.BlockSpec((t, width), lambda p, i: (i, p))
    steps = 4 // npair

    def at(p, i):
        return lambda: jnp.logical_and(pl.program_id(0) == p, pl.program_id(1) == i)

    return _call_with_gather(
        body, shards, at(0, 0), at(steps - 1, (2 * nq) // 3), at(steps - 1, nq - 1),
        name="sb_fwd", grid=(steps, nq),
        in_specs=[qo, kv, kv, pl.BlockSpec((t, t), lambda p, i: (0, 0))],
        out_specs=(qo, pl.BlockSpec((2 * npair, t, 128), lambda p, i: (p, i, 0))),
        out_shape=(jax.ShapeDtypeStruct((s_len, D_GRP), F32),
                   jax.ShapeDtypeStruct((8, s_len, 128), F32)),
        scratch_shapes=[pltpu.VMEM((npair, 2 * t, 128), BF16), pltpu.VMEM((npair, nq, 128, t), BF16),
                        pltpu.VMEM((npair, 128, 2 * t), F32),
                        pltpu.VMEM((npair, 2 * t, 128), F32), pltpu.VMEM((npair, 2 * t, 128), F32)],
    )(qs, ks, vs, tri_suf, *shards)


def _sb_bwd(qs, ks, vs, dos, csaved, tri_suf, tri_pre, rides):
    s_len = qs.shape[0]
    t = SB_TILE
    nq = s_len // t

    npair = SB_BWD_PAIRS

    def body(q_ref, k_ref, v_ref, do_ref, c_ref, u_ref, p_ref, dq_ref, dk_ref, dv_ref,
             qq, dd, qqt, ddt, kt, dq_acc, dkt, dvt, cg):
        row = lax.broadcasted_iota(jnp.int32, (2 * t, t), 0) & (t - 1)
        col = lax.broadcasted_iota(jnp.int32, (2 * t, t), 1)
        diag_mask = col < row
        lane1 = lax.broadcasted_iota(jnp.int32, (t, 128), 1)
        head0 = lane1 < HEAD_DIM
        lane2 = lax.broadcasted_iota(jnp.int32, (2 * t, 128), 1)
        uu, pm = u_ref[...], p_ref[...]
        pr = range(npair)
        cols = [slice(128 * pp, 128 * (pp + 1)) for pp in pr]
        i = pl.program_id(1)

        @pl.when(i == 0)
        def _():
            dkt[...] = jnp.zeros_like(dkt)
            dvt[...] = jnp.zeros_like(dvt)

            def transpose_k(j, _):
                rows = pl.ds(pl.multiple_of(j * t, t), t)
                for pp in pr:
                    kt[pp, j] = k_ref[rows, cols[pp]].astype(F32).T.astype(BF16)
                return 0

            lax.fori_loop(0, nq, transpose_k, 0)

        for pp in pr:
            q2 = q_ref[:, cols[pp]].astype(F32) * ATT_SCALE
            do2 = do_ref[:, cols[pp]].astype(F32)
            for src, nat, tr in ((q2, qq, qqt), (do2, dd, ddt)):
                stacked = jnp.concatenate([jnp.where(head0, src, 0.0), jnp.where(head0, 0.0, src)], axis=0)
                nat[pp] = stacked.astype(BF16)
                tr[pp] = stacked.T.astype(BF16)
        dq_acc[...] = jnp.zeros_like(dq_acc)
        cg[...] = jnp.zeros_like(cg)

        def tile(kb, diag):
            krows = pl.ds(pl.multiple_of(kb * t, t), t)
            zs = [_dot_nt(qq[pp], k_ref[krows, cols[pp]]) for pp in pr]
            das = [_dot_nt(dd[pp], v_ref[krows, cols[pp]]) for pp in pr]
            lbk = [_log_sigmoid_pair(z) for z in zs]
            lks = [jnp.where(diag_mask, lk, 0.0) if diag else lk for _, lk in lbk]
            sufs = [_cumsum_mm(lk, uu) for lk in lks]
            avs, gs = [], []
            for pp in pr:
                cs = jnp.concatenate([c_ref[2 * pp], c_ref[2 * pp + 1]], axis=0)
                cf = jnp.sum(jnp.where(lane2 == kb, cs, 0.0), axis=-1, keepdims=True)
                a = jnp.exp(lbk[pp][0] + (sufs[pp] + cf))
                a = jnp.where(diag_mask, a, 0.0) if diag else a
                avs.append(a.astype(BF16))
                gs.append(a * das[pp])
            gpres = [_cumsum_mm(g, pm) for g in gs]
            dzs = []
            for pp in pr:
                carry = cg[pp]
                beta = jnp.exp(lbk[pp][0])
                dz = gs[pp] - beta * (gs[pp] + (gpres[pp] + jnp.concatenate([carry] * (t // 128), axis=1)))
                dzs.append((jnp.where(diag_mask, dz, 0.0) if diag else dz).astype(BF16))
                cg[pp] = carry + jnp.broadcast_to(jnp.sum(gs[pp], axis=-1, keepdims=True), (2 * t, 128))
            dqs = [_dot_nt(kt[pp, kb], dzs[pp]) for pp in pr]
            dks = [_dot(qqt[pp], dzs[pp]) for pp in pr]
            dvs = [_dot(ddt[pp], avs[pp]) for pp in pr]
            for pp in pr:
                dq_acc[pp] += dqs[pp]
                dkt[pp, kb] += dks[pp]
                dvt[pp, kb] += dvs[pp]

        def k_block(kb, _):
            tile(kb, False)
            return 0

        col_max = jnp.max(jnp.max(c_ref[...], axis=0), axis=0, keepdims=True)
        lane_row = lax.broadcasted_iota(jnp.int32, (1, 128), 1)
        n_live = jnp.sum(jnp.where(jnp.logical_and(col_max > SB_DEAD, lane_row < i), 1, 0))
        lax.fori_loop(i - n_live, i, k_block, 0)
        tile(i, True)
        for pp in pr:
            dq_ref[:, cols[pp]] = jnp.where(head0, dq_acc[pp, :, 0:t].T, dq_acc[pp, :, t:2 * t].T) * ATT_SCALE

        @pl.when(i == nq - 1)
        def _():
            def untranspose(j, _):
                rows = pl.ds(pl.multiple_of(j * t, t), t)
                for pp in pr:
                    dk_ref[rows, cols[pp]] = dkt[pp, j].T
                    dv_ref[rows, cols[pp]] = dvt[pp, j].T
                return 0

            lax.fori_loop(0, nq, untranspose, 0)

    width = 128 * npair
    kv = pl.BlockSpec((s_len, width), lambda p, i: (0, p))
    qo = pl.BlockSpec((t, width), lambda p, i: (i, p))
    tri = pl.BlockSpec((t, t), lambda p, i: (0, 0))
    out = jax.ShapeDtypeStruct((s_len, D_GRP), F32)
    steps = 4 // npair
    return _call_with_exchange(
        body, rides,
        lambda: jnp.logical_and(pl.program_id(0) == 0, pl.program_id(1) == 0),
        lambda: jnp.logical_and(pl.program_id(0) == steps - 1, pl.program_id(1) == nq - 1),
        name="sb_bwd", grid=(steps, nq),
        in_specs=[qo, kv, kv, qo, pl.BlockSpec((2 * npair, t, 128), lambda p, i: (p, i, 0)), tri, tri],
        out_specs=(qo, kv, kv), out_shape=(out, out, out),
        scratch_shapes=[pltpu.VMEM((npair, 2 * t, 128), BF16), pltpu.VMEM((npair, 2 * t, 128), BF16),
                        pltpu.VMEM((npair, 128, 2 * t), BF16), pltpu.VMEM((npair, 128, 2 * t), BF16),
                        pltpu.VMEM((npair, nq, 128, t), BF16),
                        pltpu.VMEM((npair, 128, 2 * t), F32),
                        pltpu.VMEM((npair, nq, 128, t), F32), pltpu.VMEM((npair, nq, 128, t), F32),
                        pltpu.VMEM((npair, 2 * t, 128), F32)],
    )(qs, ks, vs, dos, csaved, tri_suf, tri_pre, *rides)


def _attn_out(o_b, lse_b, o_sb, x2, wdil, wsb, out_g, shards):
    s_len = x2.shape[0]
    tm = ROW_TILE

    def body(o1_ref, l1_ref, o4_ref, l4_ref, o16_ref, l16_ref, osb_ref, x_ref, wdil_ref, wsb_ref, w_ref,
             odil_ref, lse_ref, lse4_ref, lse16_ref, mixed_ref, x1_ref, stage, nat4, nat16):
        _merge_views((o4_ref, l4_ref), (o16_ref, l16_ref), stage, nat4, nat16)
        os_ = (o1_ref[...], _slab_group(nat4, 0), _slab_group(nat16, 0))
        ls = (l1_ref[...], _slab_group(nat4, 1), _slab_group(nat16, 1))
        mx = jnp.maximum(jnp.maximum(ls[0], ls[1]), ls[2])
        es = [jnp.exp(l - mx) for l in ls]
        den = es[0] + es[1] + es[2]
        o_dil = (es[0] * os_[0] + es[1] * os_[1] + es[2] * os_[2]) / den
        odil_ref[...] = o_dil
        lse = mx + jnp.log(den)
        lse_ref[...] = lse
        for p in range(4):
            nat4[p] = lse[:, 128 * p:128 * (p + 1)]
        _split_views(nat4.at[0:4], stage.at[0:4], (lse4_ref,), (lse16_ref,))
        halves = []
        for t, w_r in ((o_dil, wdil_ref), (osb_ref[...], wsb_ref)):
            r = lax.rsqrt(jnp.mean(t * t, axis=-1, keepdims=True) + EPS)
            halves.append(((t * r) * w_r[...]).astype(BF16))
        mixed = jnp.concatenate(halves, axis=1)
        mixed_ref[...] = mixed
        w = w_ref[...].reshape(D_MODEL, D_MODEL)
        x1_ref[...] = x_ref[...] + _dot(mixed, w)

    row = lambda w: pl.BlockSpec((tm, w), lambda i: (i, 0))
    ni = s_len // tm
    return _call_with_gather(
        body, shards, lambda: pl.program_id(0) == 0, lambda: pl.program_id(0) == ni - 2,
        lambda: pl.program_id(0) == ni - 1,
        name="attn_out", grid=(ni,),
        in_specs=[row(D_GRP)] * 2 + [_view_spec(tm, 4)] * 2 + [_view_spec(tm, 16)] * 2
        + [row(D_GRP), row(D_MODEL), _full((1, D_GRP)), _full((1, D_GRP)), _full((N_DEV, OUT_SHARD, D_MODEL))],
        out_specs=(row(D_GRP), row(D_GRP), _view_spec(tm, 4), _view_spec(tm, 16), row(D_MODEL), row(D_MODEL)),
        out_shape=(jax.ShapeDtypeStruct((s_len, D_GRP), F32), jax.ShapeDtypeStruct((s_len, D_GRP), F32),
                   _view_shape(s_len, 4, F32), _view_shape(s_len, 16, F32),
                   jax.ShapeDtypeStruct((s_len, D_MODEL), BF16), jax.ShapeDtypeStruct((s_len, D_MODEL), F32)),
        scratch_shapes=[pltpu.VMEM((8, tm, 128), F32)] * 3,
    )(o_b[0], lse_b[0], o_b[1], lse_b[1], o_b[2], lse_b[2], o_sb, x2, wdil, wsb, out_g, *shards)


def _two_shards(w_ref):
    return w_ref[...].reshape(FF_BLOCK, D_MODEL)


def _ffn_fwd(x1, wn2, tgt, gate_g, up_g, down_g):
    s_len = x1.shape[0]
    tm = ROW_TILE
    ni = s_len // tm

    def body(x_ref, wn_ref, t_ref, wg_ref, wu_ref, wd_ref, g_ref, u_ref, h2_ref, dy_ref, loss_ref, acc):
        j = pl.program_id(1)

        @pl.when(j == 0)
        def _():
            xx = x_ref[...]
            r = lax.rsqrt(jnp.mean(xx * xx, axis=-1, keepdims=True) + EPS)
            h2_ref[...] = ((xx * r) * wn_ref[...]).astype(BF16)
            acc[...] = jnp.zeros_like(acc)

        h = h2_ref[...]
        g = _dot_nt(h, _two_shards(wg_ref))
        u = _dot_nt(h, _two_shards(wu_ref))
        g_ref[...] = g
        u_ref[...] = u
        act = (g * (1.0 / (1.0 + jnp.exp(-g)))) * u
        acc[...] += _dot(act.astype(BF16), _two_shards(wd_ref))

        @pl.when(j == FF_STEPS - 1)
        def _():
            err = (x_ref[...] + acc[...]) - t_ref[...]
            dy_ref[...] = err * (1.0 / D_MODEL)
            part = 0.5 * jnp.sum(jnp.mean(err * err, axis=-1, keepdims=True))
            loss_ref[...] = jnp.full((8, 128), part, F32)

    row = pl.BlockSpec((tm, D_MODEL), lambda i, j: (i, 0))
    hid = pl.BlockSpec((tm, FF_BLOCK), lambda i, j: (i, j))
    return pl.pallas_call(
        body, name="ffn_fwd", grid=(ni, FF_STEPS),
        in_specs=[row, pl.BlockSpec((1, D_MODEL), lambda i, j: (0, 0)), row,
                  pl.BlockSpec((2, FF_PAD, D_MODEL), lambda i, j: (j, 0, 0)),
                  pl.BlockSpec((2, FF_PAD, D_MODEL), lambda i, j: (j, 0, 0)),
                  pl.BlockSpec((2, FF_PAD, D_MODEL), lambda i, j: (j, 0, 0))],
        out_specs=(hid, hid, row, row, pl.BlockSpec((8, 128), lambda i, j: (i, 0))),
        out_shape=(jax.ShapeDtypeStruct((s_len, N_DEV * FF_PAD), F32),
                   jax.ShapeDtypeStruct((s_len, N_DEV * FF_PAD), F32),
                   jax.ShapeDtypeStruct((s_len, D_MODEL), BF16),
                   jax.ShapeDtypeStruct((s_len, D_MODEL), F32),
                   jax.ShapeDtypeStruct((ni * 8, 128), F32)),
        scratch_shapes=[pltpu.VMEM((tm, D_MODEL), F32)],
        compiler_params=_params(),
    )(x1, wn2, tgt, gate_g, up_g, down_g)


def _ffn_bwd_dx(dy, g, u, gate_g, up_g, down_g):
    s_len = dy.shape[0]
    tm = ROW_TILE

    def body(dy_ref, g_ref, u_ref, wg_ref, wu_ref, wd_ref, dg_ref, du_ref, act_ref, dh_ref, acc):
        j = pl.program_id(1)

        @pl.when(j == 0)
        def _():
            acc[...] = jnp.zeros_like(acc)

        halves = [slice(0, tm // 2), slice(tm // 2, tm)]
        wd, wg, wu = _two_shards(wd_ref), _two_shards(wg_ref), _two_shards(wu_ref)
        das = [_dot_nt(dy_ref[rows, :].astype(BF16), wd) for rows in halves]

        def elementwise(rows, da):
            gg, uu = g_ref[rows, :], u_ref[rows, :]
            sig = 1.0 / (1.0 + jnp.exp(-gg))
            silu = gg * sig
            act_ref[rows, :] = (silu * uu).astype(BF16)
            du = (da * silu).astype(BF16)
            dg = (da * uu * (sig * (1.0 + gg * (1.0 - sig)))).astype(BF16)
            du_ref[rows, :] = du
            dg_ref[rows, :] = dg
            return dg, du

        dg0, du0 = elementwise(halves[0], das[0])
        acc[halves[0], :] += _dot(dg0, wg) + _dot(du0, wu)
        dg1, du1 = elementwise(halves[1], das[1])
        acc[halves[1], :] += _dot(dg1, wg) + _dot(du1, wu)

        @pl.when(j == FF_STEPS - 1)
        def _():
            dh_ref[...] = acc[...]

    row = pl.BlockSpec((tm, D_MODEL), lambda i, j: (i, 0))
    hid = pl.BlockSpec((tm, FF_BLOCK), lambda i, j: (i, j))
    hid_bf = jax.ShapeDtypeStruct((s_len, N_DEV * FF_PAD), BF16)
    return pl.pallas_call(
        body, name="ffn_bwd_dx", grid=(s_len // tm, FF_STEPS),
        in_specs=[row, hid, hid,
                  pl.BlockSpec((2, FF_PAD, D_MODEL), lambda i, j: (j, 0, 0)),
                  pl.BlockSpec((2, FF_PAD, D_MODEL), lambda i, j: (j, 0, 0)),
                  pl.BlockSpec((2, FF_PAD, D_MODEL), lambda i, j: (j, 0, 0))],
        out_specs=(hid, hid, hid, row),
        out_shape=(hid_bf, hid_bf, hid_bf, jax.ShapeDtypeStruct((s_len, D_MODEL), F32)),
        scratch_shapes=[pltpu.VMEM((tm, D_MODEL), F32)],
        compiler_params=_params(),
    )(dy, g, u, gate_g, up_g, down_g)


def _ffn_bwd_dw(h2, dy, dg, du, act):
    s_len = h2.shape[0]
    tm = DW_ROW_TILE
    ni = s_len // tm

    def body(h_ref, dy_ref, dg_ref, du_ref, act_ref, dwg_ref, dwu_ref, dwd_ref, ag, au, ad):
        i = pl.program_id(1)

        @pl.when(i == 0)
        def _():
            ag[...] = jnp.zeros_like(ag)
            au[...] = jnp.zeros_like(au)
            ad[...] = jnp.zeros_like(ad)

        h = h_ref[...]
        ag[...] += _dot_tn(dg_ref[...], h)
        au[...] += _dot_tn(du_ref[...], h)
        ad[...] += _dot_tn(act_ref[...], dy_ref[...].astype(BF16))

        @pl.when(i == ni - 1)
        def _():
            for acc_ref, out_ref in ((ag, dwg_ref), (au, dwu_ref), (ad, dwd_ref)):
                out_ref[...] = acc_ref[...].astype(BF16).reshape(2, FF_PAD, D_MODEL)

    row = pl.BlockSpec((tm, D_MODEL), lambda j, i: (i, 0))
    hid = pl.BlockSpec((tm, FF_BLOCK), lambda j, i: (i, j))
    row_w = pl.BlockSpec((2, FF_PAD, D_MODEL), lambda j, i: (j, 0, 0))
    grad = jax.ShapeDtypeStruct((N_DEV, FF_PAD, D_MODEL), BF16)
    return pl.pallas_call(
        body, name="ffn_bwd_dw", grid=(FF_STEPS, ni),
        in_specs=[row, row, hid, hid, hid], out_specs=(row_w, row_w, row_w),
        out_shape=(grad, grad, grad),
        scratch_shapes=[pltpu.VMEM((FF_BLOCK, D_MODEL), F32)] * 3,
        compiler_params=_params(),
    )(h2, dy, dg, du, act)


def _rms_bwd(dy, t, w):
    r = lax.rsqrt(jnp.mean(t * t, axis=-1, keepdims=True) + EPS)
    gw = dy * w
    dt = r * (gw - t * ((r * r) * jnp.mean(gw * t, axis=-1, keepdims=True)))
    return dt, dy * t * r


def _attn_out_bwd(dy, dh2, x1, wn2, b_g, mixed, o_dil, o_sb, wdil, wsb, bd512):
    s_len = dy.shape[0]
    tm = ROW_TILE
    ni = s_len // tm

    def body(dy_ref, dh_ref, x1_ref, wn_ref, w_ref, mixed_ref, odil_ref, osb_ref, wdil_ref, wsb_ref, bd_ref,
             dx1_ref, dodil_ref, delta_ref, dosb_ref, dwout_ref, dwn_ref, dwdil_ref, dwsb_ref,
             do4_ref, dl4_ref, do16_ref, dl16_ref, wacc, both, stage):
        i = pl.program_id(0)

        @pl.when(i == 0)
        def _():
            wacc[...] = jnp.zeros_like(wacc)
            dwn_ref[...] = jnp.zeros_like(dwn_ref)
            dwdil_ref[...] = jnp.zeros_like(dwdil_ref)
            dwsb_ref[...] = jnp.zeros_like(dwsb_ref)

        dnorm, dw_rows = _rms_bwd(dh_ref[...], x1_ref[...], wn_ref[...])
        dx1 = dy_ref[...] + dnorm
        dx1_ref[...] = dx1
        dwn_ref[...] += jnp.sum(dw_rows, axis=0, keepdims=True)
        dx1b = dx1.astype(BF16)
        w = w_ref[...].reshape(D_MODEL, D_MODEL)
        dmixed = _dot_nt(dx1b, w)
        wacc[...] += _dot_tn(mixed_ref[...], dx1b)
        o_dil = odil_ref[...]
        d_odil, dw_rows = _rms_bwd(dmixed[:, :D_GRP], o_dil, wdil_ref[...])
        dwdil_ref[...] += jnp.sum(dw_rows, axis=0, keepdims=True)
        dodil_ref[...] = d_odil.astype(BF16)
        delta = _mm_split(d_odil * o_dil, bd_ref[...])
        delta_ref[...] = delta
        for p in range(4):
            both[p] = d_odil[:, 128 * p:128 * (p + 1)]
            both[4 + p] = delta[:, 128 * p:128 * (p + 1)]
        _split_views(both, stage, (do4_ref, dl4_ref), (do16_ref, dl16_ref))
        d_osb, dw_rows = _rms_bwd(dmixed[:, D_GRP:], osb_ref[...], wsb_ref[...])
        dwsb_ref[...] += jnp.sum(dw_rows, axis=0, keepdims=True)
        dosb_ref[...] = d_osb.astype(BF16)

        @pl.when(i == ni - 1)
        def _():
            dwout_ref[...] = wacc[...].astype(BF16).reshape(N_DEV, OUT_SHARD, D_MODEL)

    row = lambda w: pl.BlockSpec((tm, w), lambda i: (i, 0))
    return pl.pallas_call(
        body, name="attn_out_bwd", grid=(ni,),
        in_specs=[row(D_MODEL), row(D_MODEL), row(D_MODEL), _full((1, D_MODEL)),
                  _full((N_DEV, OUT_SHARD, D_MODEL)),
                  row(D_MODEL), row(D_GRP), row(D_GRP), _full((1, D_GRP)), _full((1, D_GRP)),
                  _full((D_GRP, D_GRP))],
        out_specs=(row(D_MODEL), row(D_GRP), row(D_GRP), row(D_GRP),
                   _full((N_DEV, OUT_SHARD, D_MODEL)), _full((1, D_MODEL)), _full((1, D_GRP)), _full((1, D_GRP)),
                   _view_spec(tm, 4), _view_spec(tm, 4), _view_spec(tm, 16), _view_spec(tm, 16)),
        out_shape=(jax.ShapeDtypeStruct((s_len, D_MODEL), F32), jax.ShapeDtypeStruct((s_len, D_GRP), BF16),
                   jax.ShapeDtypeStruct((s_len, D_GRP), F32), jax.ShapeDtypeStruct((s_len, D_GRP), BF16),
                   jax.ShapeDtypeStruct((N_DEV, OUT_SHARD, D_MODEL), BF16),
                   jax.ShapeDtypeStruct((1, D_MODEL), F32), jax.ShapeDtypeStruct((1, D_GRP), F32),
                   jax.ShapeDtypeStruct((1, D_GRP), F32),
                   _view_shape(s_len, 4, BF16), _view_shape(s_len, 4, F32),
                   _view_shape(s_len, 16, BF16), _view_shape(s_len, 16, F32)),
        scratch_shapes=[pltpu.VMEM((D_MODEL, D_MODEL), F32), pltpu.VMEM((8, tm, 128), F32),
                        pltpu.VMEM((8, tm, 128), F32)],
        compiler_params=_params(),
    )(dy, dh2, x1, wn2, b_g, mixed, o_dil, o_sb, wdil, wsb, bd512)


def _qkv_bwd(dq_b, dk_b, dv_b, dqs, dks, dvs, qraw, kraw, cos2, sin2, qnw, knw, bd):
    s_len = qraw.shape[0]
    tm = ROW_TILE
    ni = s_len // tm

    def body(dq1, dk1, dv1, dq4, dk4, dv4, dq16, dk16, dv16, dqs_ref, dks_ref, dvs_ref,
             qraw_ref, kraw_ref, cos_ref, sin_ref, qnw_ref, knw_ref, bd_ref,
             dproj_ref, dqn_ref, dkn_ref, stage, nat4, nat16):
        i = pl.program_id(0)

        @pl.when(i == 0)
        def _():
            dqn_ref[...] = jnp.zeros_like(dqn_ref)
            dkn_ref[...] = jnp.zeros_like(dkn_ref)

        _merge_views((dq4, dk4, dv4), (dq16, dk16, dv16), stage, nat4, nat16)
        cos_t, sin_t, bdm = cos_ref[...], sin_ref[...], bd_ref[...]
        for grp, (part1, raw_ref, nw_ref, dn_ref) in enumerate(((dq1, qraw_ref, qnw_ref, dqn_ref),
                                                                (dk1, kraw_ref, knw_ref, dkn_ref))):
            dn_acc = 0.0
            for p in range(4):
                cols = slice(128 * p, 128 * (p + 1))
                d_rope = part1[:, cols] + nat4[4 * grp + p] + nat16[4 * grp + p]
                d_norm = d_rope * cos_t + _swap_halves(d_rope * sin_t)
                t = raw_ref[:, cols]
                w = nw_ref[...]
                r = lax.rsqrt(_mm_split(t * t, bdm) * (1.0 / HEAD_DIM) + EPS)
                gw = d_norm * w
                corr = _mm_split(gw * t, bdm) * (1.0 / HEAD_DIM)
                dt = r * (gw - t * ((r * r) * corr))
                dn_acc = dn_acc + jnp.sum(d_norm * t * r, axis=0, keepdims=True)
                dproj_ref[:, D_GRP * grp + 128 * p:D_GRP * grp + 128 * (p + 1)] = dt.astype(BF16)
            dn_ref[...] += dn_acc
        dproj_ref[:, 2 * D_GRP:3 * D_GRP] = (dv1[...] + _slab_group(nat4, 2) + _slab_group(nat16, 2)).astype(BF16)
        dproj_ref[:, 3 * D_GRP:4 * D_GRP] = dqs_ref[...].astype(BF16)
        dproj_ref[:, 4 * D_GRP:5 * D_GRP] = dks_ref[...].astype(BF16)
        dproj_ref[:, 5 * D_GRP:6 * D_GRP] = dvs_ref[...].astype(BF16)

    row = lambda w: pl.BlockSpec((tm, w), lambda i: (i, 0))
    return pl.pallas_call(
        body, name="qkv_bwd", grid=(ni,),
        in_specs=[row(D_GRP)] * 3 + [_view_spec(tm, 4)] * 3 + [_view_spec(tm, 16)] * 3 + [row(D_GRP)] * 5
        + [row(128), row(128), _full((1, 128)), _full((1, 128)), _full((128, 128))],
        out_specs=(row(D_IN), _full((1, 128)), _full((1, 128))),
        out_shape=(jax.ShapeDtypeStruct((s_len, D_IN), BF16), jax.ShapeDtypeStruct((1, 128), F32),
                   jax.ShapeDtypeStruct((1, 128), F32)),
        scratch_shapes=[pltpu.VMEM((12, tm, 128), F32)] * 3,
        compiler_params=_params(),
    )(dq_b[0], dk_b[0], dv_b[0], dq_b[1], dk_b[1], dv_b[1], dq_b[2], dk_b[2], dv_b[2],
      dqs, dks, dvs, qraw, kraw, cos2, sin2, qnw, knw, bd)


def _in_bwd_dx(dproj, a_g, x2, dx1, wn1, rides):
    s_len = x2.shape[0]
    tm = ROW_TILE
    ni = s_len // tm

    def body(dp_ref, w_ref, x_ref, dx1_ref, wn_ref, gx_ref, dwn_ref, w_full):
        i = pl.program_id(0)

        @pl.when(i == 0)
        def _():
            dwn_ref[...] = jnp.zeros_like(dwn_ref)
            for d in range(N_DEV):
                w_full[:, IN_SHARD * d:IN_SHARD * (d + 1)] = w_ref[d]

        dh = _dot_nt(dp_ref[...], w_full[...])
        dnorm, dw_rows = _rms_bwd(dh, x_ref[...], wn_ref[...])
        gx_ref[...] = dx1_ref[...] + dnorm
        dwn_ref[...] += jnp.sum(dw_rows, axis=0, keepdims=True)

    row = lambda w: pl.BlockSpec((tm, w), lambda i: (i, 0))
    return _call_with_exchange(
        body, rides, lambda: pl.program_id(0) == 0, lambda: pl.program_id(0) == ni - 1,
        name="in_bwd_dx", grid=(ni,),
        in_specs=[row(D_IN), pl.BlockSpec((N_DEV, D_MODEL, IN_SHARD), lambda i: (0, 0, 0)),
                  row(D_MODEL), row(D_MODEL), _full((1, D_MODEL))],
        out_specs=(row(D_MODEL), _full((1, D_MODEL))),
        out_shape=(jax.ShapeDtypeStruct((s_len, D_MODEL), F32), jax.ShapeDtypeStruct((1, D_MODEL), F32)),
        scratch_shapes=[pltpu.VMEM((D_MODEL, D_IN), BF16)],
    )(dproj, a_g, x2, dx1, wn1, *_ride_arrays(rides))


def _in_bwd_dw(h1, dproj, part, rides):
    s_len = h1.shape[0]
    tm = DW_ROW_TILE
    ni = s_len // tm
    half_d = D_MODEL // 2

    def body(h_ref, dp_ref, dw_ref, acc):
        i = pl.program_id(1)

        @pl.when(i == 0)
        def _():
            acc[...] = jnp.zeros_like(acc)

        acc[...] += _dot_tn(h_ref[...], dp_ref[...])

        @pl.when(i == ni - 1)
        def _():
            for half in range(2):
                dw_ref[half] = acc[:, IN_SHARD * half:IN_SHARD * (half + 1)].astype(BF16)

    return _call_with_exchange(
        body, rides,
        lambda: jnp.logical_and(pl.program_id(0) == 0, pl.program_id(1) == 0),
        lambda: jnp.logical_and(pl.program_id(0) == N_DEV // 2 - 1, pl.program_id(1) == ni - 1),
        name=f"in_bwd_dw_{part}", grid=(N_DEV // 2, ni),
        in_specs=[pl.BlockSpec((tm, half_d), lambda d, i: (i, part)),
                  pl.BlockSpec((tm, 2 * IN_SHARD), lambda d, i: (i, d))],
        out_specs=pl.BlockSpec((2, half_d, IN_SHARD), lambda d, i: (d, 0, 0)),
        out_shape=jax.ShapeDtypeStruct((N_DEV, half_d, IN_SHARD), BF16),
        scratch_shapes=[pltpu.VMEM((half_d, 2 * IN_SHARD), F32)],
    )(h1, dproj, *_ride_arrays(rides))


def _adamw(recv, w, m, v, recv_hi=None):
    rows, cols = w.shape
    tr = next((t for t in (128, 32) if rows % t == 0), rows)
    recvs = [recv] if recv_hi is None else [recv, recv_hi]
    half_tiles = rows // tr // 2

    def body(*refs):
        p_refs = refs[:len(recvs)]
        w_ref, m_ref, v_ref, g_ref, d_ref, nm_ref, nv_ref = refs[len(recvs):]

        def slot(s):
            if len(p_refs) == 1:
                return p_refs[0][s].astype(F32)
            return jnp.where(pl.program_id(0) < half_tiles, p_refs[0][s], p_refs[1][s]).astype(F32)

        g = slot(0)
        for s in range(1, N_DEV):
            g = g + slot(s)
        m_new = ADAM_B1 * m_ref[...] + (1.0 - ADAM_B1) * g
        v_new = ADAM_B2 * v_ref[...] + (1.0 - ADAM_B2) * (g * g)
        m_hat = m_new / (1.0 - ADAM_B1 ** ADAM_STEP)
        v_hat = v_new / (1.0 - ADAM_B2 ** ADAM_STEP)
        g_ref[...] = g
        d_ref[...] = -ADAM_LR * (m_hat / (jnp.sqrt(v_hat) + ADAM_EPS) + ADAM_WD * w_ref[...])
        nm_ref[...] = m_new
        nv_ref[...] = v_new

    blk = pl.BlockSpec((tr, cols), lambda i: (i, 0))
    out = jax.ShapeDtypeStruct((rows, cols), F32)
    return pl.pallas_call(
        body, name=f"adamw_{rows}x{cols}", grid=(rows // tr,),
        in_specs=([pl.BlockSpec((N_DEV, tr, cols), lambda i: (0, i, 0))] if recv_hi is None else
                  [pl.BlockSpec((N_DEV, tr, cols), lambda i: (0, jnp.minimum(i, half_tiles - 1), 0)),
                   pl.BlockSpec((N_DEV, tr, cols), lambda i: (0, jnp.maximum(i - half_tiles, 0), 0))])
        + [blk, blk, blk],
        out_specs=(blk,) * 4, out_shape=(out,) * 4,
        compiler_params=_params(),
    )(*recvs, w, m, v)


def _rope_tables(s_len):
    pos = jnp.arange(s_len, dtype=F32)
    inv_freq = ROPE_THETA ** (-jnp.arange(0, HEAD_DIM, 2, dtype=F32) / HEAD_DIM)
    ang = pos[:, None] * inv_freq[None, :]
    cos, sin = jnp.cos(ang), jnp.sin(ang)
    cos2 = jnp.concatenate([cos, cos, cos, cos], axis=1)
    sin2 = jnp.concatenate([-sin, sin, -sin, sin], axis=1)
    return cos2, sin2


def _block_diag_ones(n):
    i = jnp.arange(n)
    return (i[:, None] // HEAD_DIM == i[None, :] // HEAD_DIM).astype(BF16)


def _pad_cols(t):
    return jnp.pad(t, ((0, 0), (0, FF_PAD - FF_SHARD)))


def _pad_rows(t):
    return jnp.pad(t, ((0, FF_PAD - FF_SHARD), (0, 0)))


LOSS_ROW = 26


def _pack_small(n1, n2, ndil, nsb, nq, nk, scalar=None):
    pad = lambda t: jnp.pad(t.reshape(1, -1), ((0, 0), (0, 128 - t.size)))
    last = jnp.zeros((1, 128), F32) if scalar is None else pad(scalar)
    rows = [n1.reshape(8, 128), n2.reshape(8, 128), ndil.reshape(4, 128), nsb.reshape(4, 128),
            pad(nq), pad(nk), last, jnp.zeros((5, 128), F32)]
    return jnp.concatenate(rows, axis=0)


def _unpack_small(t):
    return (t[0:8].reshape(1, D_MODEL), t[8:16].reshape(1, D_MODEL), t[16:20].reshape(1, D_GRP),
            t[20:24].reshape(1, D_GRP), t[24:25, :HEAD_DIM], t[25:26, :HEAD_DIM])


def kernel(x, attn_norm_w, w_in, q_norm_w, k_norm_w, dil_out_norm_w, sb_out_norm_w, w_out, ffn_norm_w, w_gate, w_up, w_down, loss_target, m_attn_norm_w, m_w_in, m_q_norm_w, m_k_norm_w, m_dil_out_norm_w, m_sb_out_norm_w, m_w_out, m_ffn_norm_w, m_w_gate, m_w_up, m_w_down, v_attn_norm_w, v_w_in, v_q_norm_w, v_k_norm_w, v_dil_out_norm_w, v_sb_out_norm_w, v_w_out, v_ffn_norm_w, v_w_gate, v_w_up, v_w_down):
    s_len = x.shape[1]
    x2, tgt = x[0], loss_target[0]

    (a_g,) = _gather_weights([w_in[0].astype(BF16)])
    gate_loc = _pad_cols(w_gate[0]).T.astype(BF16)
    up_loc = _pad_cols(w_up[0]).T.astype(BF16)
    down_loc = _pad_rows(w_down[0]).astype(BF16)
    out_loc = w_out[0].astype(BF16)

    cos2, sin2 = _rope_tables(s_len)
    bd128, bd512 = _block_diag_ones(128), _block_diag_ones(D_GRP)
    idx = jnp.arange(SB_TILE)
    tri_suf = (idx[:, None] > idx[None, :]).astype(BF16)
    tri_pre = (idx[:, None] < idx[None, :]).astype(BF16)
    qnw2 = jnp.concatenate([q_norm_w, q_norm_w], axis=1)
    knw2 = jnp.concatenate([k_norm_w, k_norm_w], axis=1)

    (h1, qraw, kraw, q, k, va, qs, ks, vs, q4, k4, v4, q16, k16, v16,
     out_g, gate_g) = _attn_in(x2, attn_norm_w, a_g, cos2, sin2, qnw2, knw2, bd128, shards=[out_loc, gate_loc])
    qkv_views = {1: (q, k, va), 4: (q4, k4, v4), 16: (q16, k16, v16)}
    o_b, lse_b = [], []
    for r in DILATIONS:
        o, lse = _dil_fwd(*qkv_views[r], r)
        o_b.append(o)
        lse_b.append(lse)
    o_sb, c_sb, up_g = _sb_fwd(qs, ks, vs, tri_suf, shards=[up_loc])
    o_dil, lse_tot, lse4, lse16, mixed, x1, down_g = _attn_out(
        o_b, lse_b, o_sb, x2, dil_out_norm_w, sb_out_norm_w, out_g, shards=[down_loc])
    g, u, h2, dy, loss_parts = _ffn_fwd(x1, ffn_norm_w, tgt, gate_g, up_g, down_g)
    loss_local = jnp.sum(loss_parts[::8, 0])

    dg, du, act, dh2 = _ffn_bwd_dx(dy, g, u, gate_g, up_g, down_g)
    (dx1, do_dil, delta, do_sb, dwout, dn2, dndil, dnsb, do4, dl4, do16, dl16) = _attn_out_bwd(
        dy, dh2, x1, ffn_norm_w, out_g, mixed, o_dil, o_sb, dil_out_norm_w, sb_out_norm_w, bd512)
    dwg, dwu, dwd = _ffn_bwd_dw(h2, dy, dg, du, act)
    dqs, dks, dvs, r_gate, r_down = _sb_bwd(qs, ks, vs, do_sb, c_sb, tri_suf, tri_pre, rides=[dwg, dwd])
    cot_views = {1: (do_dil, lse_tot, delta), 4: (do4, lse4, dl4), 16: (do16, lse16, dl16)}
    riders = {1: [], 4: [dwout], 16: [dwu]}
    dq_b, dk_b, dv_b, landed = [], [], [], {}
    for r in DILATIONS:
        dq, dk, dv, *landed[r] = _dil_bwd(*qkv_views[r], *cot_views[r], r, rides=riders[r])
        dq_b.append(dq)
        dk_b.append(dk)
        dv_b.append(dv)
    (r_out,), (r_up,) = landed[4], landed[16]
    dproj, dqn2, dkn2 = _qkv_bwd(dq_b, dk_b, dv_b, dqs, dks, dvs, qraw, kraw, cos2, sin2, qnw2, knw2, bd128)
    (dwin_lo,) = _in_bwd_dw(h1, dproj, 0, rides=[])
    dwin_hi, r_in_lo = _in_bwd_dw(h1, dproj, 1, rides=[dwin_lo])
    grad_x, dn1, r_in_hi = _in_bwd_dx(dproj, a_g, x2, dx1, attn_norm_w, rides=[dwin_hi])
    dqn = dqn2[:, :HEAD_DIM] + dqn2[:, HEAD_DIM:]
    dkn = dkn2[:, :HEAD_DIM] + dkn2[:, HEAD_DIM:]

    small = _pack_small(dn1, dn2, dndil, dnsb, dqn, dkn, loss_local)
    (r_small,) = _exchange_grads([], small)
    big = {
        "w_in": _adamw(r_in_lo, w_in[0], m_w_in[0], v_w_in[0], recv_hi=r_in_hi),
        "w_gate": tuple(t.T for t in _adamw(r_gate, w_gate[0].T, m_w_gate[0].T, v_w_gate[0].T)),
        "w_up": tuple(t.T for t in _adamw(r_up, w_up[0].T, m_w_up[0].T, v_w_up[0].T)),
        "w_down": _adamw(r_down, w_down[0], m_w_down[0], v_w_down[0]),
        "w_out": _adamw(r_out, w_out[0], m_w_out[0], v_w_out[0]),
    }
    packs = [_pack_small(*ts) for ts in (
        (attn_norm_w, ffn_norm_w, dil_out_norm_w, sb_out_norm_w, q_norm_w, k_norm_w),
        (m_attn_norm_w, m_ffn_norm_w, m_dil_out_norm_w, m_sb_out_norm_w, m_q_norm_w, m_k_norm_w),
        (v_attn_norm_w, v_ffn_norm_w, v_dil_out_norm_w, v_sb_out_norm_w, v_q_norm_w, v_k_norm_w))]
    small_raw = _adamw(r_small, *packs)
    loss = small_raw[0][LOSS_ROW, 0]
    small_out = [_unpack_small(t) for t in small_raw]
    names = ["attn_norm_w", "w_in", "q_norm_w", "k_norm_w", "dil_out_norm_w", "sb_out_norm_w", "w_out",
             "ffn_norm_w", "w_gate", "w_up", "w_down"]
    small_pos = {"attn_norm_w": 0, "ffn_norm_w": 1, "dil_out_norm_w": 2, "sb_out_norm_w": 3,
                 "q_norm_w": 4, "k_norm_w": 5}
    outs = [loss, grad_x[None]]
    for kind in range(4):
        for name in names:
            if name in small_pos:
                outs.append(small_out[kind][small_pos[name]])
            else:
                outs.append(big[name][kind][None])
    return tuple(outs)
```

```python
import functools

import jax
import jax.numpy as jnp
from jax import lax
from jax.experimental import pallas as pl
from jax.experimental.pallas import tpu as pltpu

F32 = jnp.float32
BF16 = jnp.bfloat16

N_DEV = 8
D_MODEL = 1024
HEAD_DIM = 64
D_GRP = 512
D_IN = 6 * D_GRP
IN_SHARD = D_IN // N_DEV
FF_SHARD = 352
FF_PAD = 384
FF_BLOCK = 2 * FF_PAD
FF_STEPS = N_DEV // 2
OUT_SHARD = D_MODEL // N_DEV
BLOCK = 128
DILATIONS = (1, 4, 16)
ROPE_THETA = 10000.0
EPS = 1e-6
ATT_SCALE = HEAD_DIM ** -0.5
NEG = -1e30

ADAM_LR = 0.001
ADAM_B1 = 0.9
ADAM_B2 = 0.999
ADAM_EPS = 1e-08
ADAM_WD = 0.01
ADAM_STEP = 10

SB_TILE = 256
SB_DEAD = -104.0
SB_PAIRS = 4
SB_BWD_PAIRS = 2
ROW_TILE = 512
DW_ROW_TILE = 1024
VMEM_LIMIT = 56 * 1024 * 1024
MESH = pl.DeviceIdType.MESH


def _dot(a, b):
    return jnp.dot(a, b, preferred_element_type=F32)


def _dot_nt(a, b):
    return lax.dot_general(a, b, (((1,), (1,)), ((), ())), preferred_element_type=F32)


def _dot_tn(a, b):
    return lax.dot_general(a, b, (((0,), (0,)), ((), ())), preferred_element_type=F32)


def _mm_split(t, m):
    hi = t.astype(BF16)
    lo = (t - hi.astype(F32)).astype(BF16)
    return _dot(hi, m) + _dot(lo, m)


def _params(**kw):
    return pltpu.CompilerParams(vmem_limit_bytes=VMEM_LIMIT, **kw)


def _full(shape):
    nd = len(shape)
    return pl.BlockSpec(shape, lambda *_: (0,) * nd)


def _view_shape(s_len, r, dtype):
    return jax.ShapeDtypeStruct((s_len // r, r * D_GRP), dtype)


def _view_spec(tm, r):
    return pl.BlockSpec((tm // r, r * D_GRP), lambda i: (i, 0))


def _swap_halves(t):
    lane = lax.broadcasted_iota(jnp.int32, t.shape, 1)
    first = (lane & 32) == 0
    return jnp.where(first, pltpu.roll(t, 96, 1), pltpu.roll(t, 32, 1))


def _log_sigmoid_pair(z):
    neg_abs = lax.bitcast_convert_type(lax.bitcast_convert_type(z, jnp.uint32) | jnp.uint32(0x80000000), F32)
    lb = jnp.minimum(z, 0.0) - jnp.log(1.0 + jnp.exp(neg_abs))
    return lb, lb - z


def _cumsum_mm(t, tri):
    return _dot(t.astype(BF16), tri)


def _split_views(src_ref, stage_ref, views4, views16):
    slabs, n, _ = src_ref.shape
    n4, n16 = n // 4, n // 16
    for j in range(slabs):
        g, lanes = j // 4, 128 * (j % 4)
        src, stage = src_ref.at[j], stage_ref.at[j]
        for c4 in range(4):
            blk = src[pl.ds(c4, n4, stride=4), :]
            stage[n4 * c4:n4 * (c4 + 1), :] = blk
            col = D_GRP * c4 + lanes
            views4[g][:, col:col + 128] = blk.astype(views4[g].dtype)
        for c4 in range(4):
            for c1 in range(4):
                blk = stage[pl.ds(n4 * c4 + c1, n16, stride=4), :]
                col = D_GRP * (4 * c1 + c4) + lanes
                views16[g][:, col:col + 128] = blk.astype(views16[g].dtype)


def _merge_views(views4, views16, stage_ref, dst4_ref, dst16_ref):
    slabs, n, _ = dst4_ref.shape
    n4, n16 = n // 4, n // 16
    for j in range(slabs):
        g, lanes = j // 4, 128 * (j % 4)
        dst4, dst16, stage = dst4_ref.at[j], dst16_ref.at[j], stage_ref.at[j]
        for c4 in range(4):
            col = D_GRP * c4 + lanes
            dst4[pl.ds(c4, n4, stride=4), :] = views4[g][:, col:col + 128].astype(F32)
            for c1 in range(4):
                col = D_GRP * (4 * c1 + c4) + lanes
                stage[pl.ds(n4 * c4 + c1, n16, stride=4), :] = views16[g][:, col:col + 128].astype(F32)
        for c4 in range(4):
            dst16[pl.ds(c4, n4, stride=4), :] = stage[n4 * c4:n4 * (c4 + 1), :]


def _slab_group(ref, g):
    return jnp.concatenate([ref[4 * g + p] for p in range(4)], axis=1)


def _mesh_pos():
    return lax.axis_index("x"), lax.axis_index("y"), lax.axis_index("c")


def _flat_index(p):
    return 4 * p[0] + 2 * p[1] + p[2]


def _gather_weights(shards):
    n_arr = len(shards)

    def body(*refs):
        srcs, outs = refs[:n_arr], refs[n_arr:2 * n_arr]
        send_sems, recv_sems, local_sems = refs[2 * n_arr:]
        x, y, c = _mesh_pos()
        me, sibling = (x, y, c), (x, y, 1 - c)
        chips = [(1 - x, y), (x, 1 - y), (1 - x, 1 - y)]

        def copy(arr, k, block, to, own=False):
            dst = outs[arr].at[_flat_index(block)]
            return pltpu.make_async_remote_copy(
                src_ref=srcs[arr] if own else dst, dst_ref=dst,
                send_sem=send_sems.at[arr, k], recv_sem=recv_sems.at[arr, k],
                device_id=to, device_id_type=MESH)

        for arr in range(n_arr):
            mine = pltpu.make_async_copy(srcs[arr], outs[arr].at[_flat_index(me)], local_sems.at[arr])
            mine.start()
            first = [copy(arr, 0, me, sibling, own=True)]
            first += [copy(arr, 1 + j, me, (*chip, c), own=True) for j, chip in enumerate(chips)]
            for cp in first:
                cp.start()
        for arr in range(n_arr):
            passed = [copy(arr, 4 + j, (*chip, c), sibling) for j, chip in enumerate(chips)]
            for j, chip in enumerate(chips):
                copy(arr, 1 + j, (*chip, c), me).wait_recv()
                passed[j].start()
        for arr in range(n_arr):
            copy(arr, 0, sibling, me).wait_recv()
            for j, chip in enumerate(chips):
                copy(arr, 4 + j, (*chip, 1 - c), me).wait_recv()
            for k in range(7):
                copy(arr, k, me, me).wait_send()
            pltpu.make_async_copy(srcs[arr], outs[arr].at[_flat_index(me)], local_sems.at[arr]).wait()

    any_spec = pl.BlockSpec(memory_space=pl.ANY)
    return pl.pallas_call(
        body, name="gather_weights",
        out_shape=tuple(jax.ShapeDtypeStruct((N_DEV,) + s.shape, s.dtype) for s in shards),
        in_specs=[any_spec] * n_arr, out_specs=(any_spec,) * n_arr,
        scratch_shapes=[pltpu.SemaphoreType.DMA((n_arr, 7)), pltpu.SemaphoreType.DMA((n_arr, 7)),
                        pltpu.SemaphoreType.DMA((n_arr,))],
        compiler_params=pltpu.CompilerParams(has_side_effects=True),
    )(*shards)


def _peer_list(x, y, c):
    return [(1 - x if m & 4 else x, 1 - y if m & 2 else y, 1 - c if m & 1 else c) for m in range(1, N_DEV)]


def _exchange_grads(parts, small):
    n_arr = len(parts)

    def body(*refs):
        ins, outs = refs[:n_arr + 1], refs[n_arr + 1:2 * (n_arr + 1)]
        send_sems, recv_sems, local_sems = refs[2 * (n_arr + 1):]
        x, y, c = _mesh_pos()
        me = (x, y, c)
        my_idx = _flat_index(me)
        peers = []
        for m in range(1, N_DEV):
            peers.append((1 - x if m & 4 else x, 1 - y if m & 2 else y, 1 - c if m & 1 else c))

        def src_block(arr, dev):
            return ins[arr] if arr == n_arr else ins[arr].at[_flat_index(dev)]

        def copy(arr, k):
            return pltpu.make_async_remote_copy(
                src_ref=src_block(arr, peers[k]), dst_ref=outs[arr].at[my_idx],
                send_sem=send_sems.at[arr, k], recv_sem=recv_sems.at[arr, k],
                device_id=peers[k], device_id_type=MESH)

        def local(arr):
            return pltpu.make_async_copy(src_block(arr, me), outs[arr].at[my_idx], local_sems.at[arr])

        for arr in range(n_arr + 1):
            local(arr).start()
            for k in range(N_DEV - 1):
                copy(arr, k).start()
        for arr in range(n_arr + 1):
            for k in range(N_DEV - 1):
                cp = copy(arr, k)
                cp.wait_send()
                cp.wait_recv()
            local(arr).wait()

    any_spec = pl.BlockSpec(memory_space=pl.ANY)
    out_shape = tuple(jax.ShapeDtypeStruct(p.shape, p.dtype) for p in parts)
    out_shape += (jax.ShapeDtypeStruct((N_DEV,) + small.shape, small.dtype),)
    return pl.pallas_call(
        body, name="exchange_grads",
        out_shape=out_shape,
        in_specs=[any_spec] * (n_arr + 1), out_specs=(any_spec,) * (n_arr + 1),
        scratch_shapes=[pltpu.SemaphoreType.DMA((n_arr + 1, N_DEV - 1)),
                        pltpu.SemaphoreType.DMA((n_arr + 1, N_DEV - 1)),
                        pltpu.SemaphoreType.DMA((n_arr + 1,))],
        compiler_params=pltpu.CompilerParams(has_side_effects=True),
    )(*parts, small)


def _call_with_gather(body, shards, first_step, mid_step, last_step, *, name, grid, in_specs, out_specs,
                      out_shape, scratch_shapes=()):
    out_specs = tuple(out_specs) if isinstance(out_specs, (tuple, list)) else (out_specs,)
    out_shape = tuple(out_shape) if isinstance(out_shape, (tuple, list)) else (out_shape,)
    n_in, n_out, n_scr, n = len(in_specs), len(out_specs), len(scratch_shapes), len(shards)

    def full_body(*refs):
        ins, srcs = refs[:n_in], refs[n_in:n_in + n]
        outs, lands = refs[n_in + n:n_in + n + n_out], refs[n_in + n + n_out:n_in + 2 * n + n_out]
        scratch = refs[n_in + 2 * n + n_out:n_in + 2 * n + n_out + n_scr]
        send_sems, recv_sems, local_sems = refs[-3:]
        x, y, c = _mesh_pos()
        me, sibling = (x, y, c), (x, y, 1 - c)
        chips = [(1 - x, y), (x, 1 - y), (1 - x, 1 - y)]

        def copy(a, k, block, to, own=False):
            dst = lands[a].at[_flat_index(block)]
            return pltpu.make_async_remote_copy(
                src_ref=srcs[a] if own else dst, dst_ref=dst,
                send_sem=send_sems.at[a, k], recv_sem=recv_sems.at[a, k],
                device_id=to, device_id_type=MESH)

        def local(a):
            return pltpu.make_async_copy(srcs[a], lands[a].at[_flat_index(me)], local_sems.at[a])

        @pl.when(first_step())
        def _():
            for a in range(n):
                local(a).start()
                copy(a, 0, me, sibling, own=True).start()
                for j, chip in enumerate(chips):
                    copy(a, 1 + j, me, (*chip, c), own=True).start()

        @pl.when(mid_step())
        def _():
            for a in range(n):
                for j, chip in enumerate(chips):
                    copy(a, 1 + j, (*chip, c), me).wait_recv()
                    copy(a, 4 + j, (*chip, c), sibling).start()

        body(*ins, *outs, *scratch)

        @pl.when(last_step())
        def _():
            for a in range(n):
                copy(a, 0, sibling, me).wait_recv()
                for j, chip in enumerate(chips):
                    copy(a, 4 + j, (*chip, 1 - c), me).wait_recv()
                for k in range(N_DEV - 1):
                    copy(a, k, me, me).wait_send()
                local(a).wait()

    any_spec = pl.BlockSpec(memory_space=pl.ANY)
    return pl.pallas_call(
        full_body, name=name, grid=grid,
        in_specs=list(in_specs) + [any_spec] * n,
        out_specs=out_specs + (any_spec,) * n,
        out_shape=out_shape + tuple(jax.ShapeDtypeStruct((N_DEV,) + t.shape, t.dtype) for t in shards),
        scratch_shapes=list(scratch_shapes) + [pltpu.SemaphoreType.DMA((n, N_DEV - 1)),
                                               pltpu.SemaphoreType.DMA((n, N_DEV - 1)),
                                               pltpu.SemaphoreType.DMA((n,))],
        compiler_params=_params(has_side_effects=True),
    )


def _ride_arrays(rides):
    return [r[0] if isinstance(r, tuple) else r for r in rides]


def _call_with_exchange(body, rides, first_step, last_step, *, name, grid, in_specs, out_specs, out_shape,
                        scratch_shapes=()):
    out_specs = tuple(out_specs) if isinstance(out_specs, (tuple, list)) else (out_specs,)
    out_shape = tuple(out_shape) if isinstance(out_shape, (tuple, list)) else (out_shape,)
    n_in, n_out, n_scr, n = len(in_specs), len(out_specs), len(scratch_shapes), len(rides)
    if n == 0:
        return pl.pallas_call(body, name=name, grid=grid, in_specs=list(in_specs), out_specs=out_specs,
                              out_shape=out_shape, scratch_shapes=list(scratch_shapes),
                              compiler_params=_params())

    def full_body(*refs):
        ins, srcs = refs[:n_in], refs[n_in:n_in + n]
        outs, lands = refs[n_in + n:n_in + n + n_out], refs[n_in + n + n_out:n_in + 2 * n + n_out]
        scratch = refs[n_in + 2 * n + n_out:n_in + 2 * n + n_out + n_scr]
        send_sems, recv_sems, local_sems = refs[-3:]
        x, y, c = _mesh_pos()
        my_idx = _flat_index((x, y, c))
        peers = _peer_list(x, y, c)

        def remote(a, k):
            return pltpu.make_async_remote_copy(
                src_ref=srcs[a].at[_flat_index(peers[k])], dst_ref=lands[a].at[my_idx],
                send_sem=send_sems.at[a, k], recv_sem=recv_sems.at[a, k],
                device_id=peers[k], device_id_type=MESH)

        def local(a):
            return pltpu.make_async_copy(srcs[a].at[my_idx], lands[a].at[my_idx], local_sems.at[a])

        @pl.when(first_step())
        def _():
            for a in range(n):
                local(a).start()
                for k in range(N_DEV - 1):
                    remote(a, k).start()

        body(*ins, *outs, *scratch)

        @pl.when(last_step())
        def _():
            for a in range(n):
                for k in range(N_DEV - 1):
                    cp = remote(a, k)
                    cp.wait_send()
                    cp.wait_recv()
                local(a).wait()

    any_spec = pl.BlockSpec(memory_space=pl.ANY)
    res = pl.pallas_call(
        full_body, name=name, grid=grid,
        in_specs=list(in_specs) + [any_spec] * n,
        out_specs=out_specs + (any_spec,) * n,
        out_shape=out_shape + tuple(jax.ShapeDtypeStruct(t.shape, t.dtype) for t in rides),
        scratch_shapes=list(scratch_shapes) + [pltpu.SemaphoreType.DMA((n, N_DEV - 1)),
                                               pltpu.SemaphoreType.DMA((n, N_DEV - 1)),
                                               pltpu.SemaphoreType.DMA((n,))],
        compiler_params=_params(has_side_effects=True),
    )
    return res


def _head_norm(t, w128, bd):
    ms = _mm_split(t * t, bd) * (1.0 / HEAD_DIM)
    r = lax.rsqrt(ms + EPS)
    return (t * r) * w128, r


def _attn_in(x2, wn1, a_g, cos2, sin2, qnw, knw, bd, shards):
    s_len = x2.shape[0]
    tm = ROW_TILE

    def body(x_ref, wn_ref, w_ref, cos_ref, sin_ref, qnw_ref, knw_ref, bd_ref,
             h1_ref, qraw_ref, kraw_ref, q_ref, k_ref, va_ref, qs_ref, ks_ref, vs_ref,
             q4_ref, k4_ref, v4_ref, q16_ref, k16_ref, v16_ref, proj, slabs, stage, w_full):
        @pl.when(pl.program_id(0) == 0)
        def _():
            for d in range(N_DEV):
                w_full[:, IN_SHARD * d:IN_SHARD * (d + 1)] = w_ref[d]

        xx = x_ref[...]
        r = lax.rsqrt(jnp.mean(xx * xx, axis=-1, keepdims=True) + EPS)
        h = ((xx * r) * wn_ref[...]).astype(BF16)
        h1_ref[...] = h
        proj[...] = _dot(h, w_full[...])
        cos_t, sin_t, bdm = cos_ref[...], sin_ref[...], bd_ref[...]
        for grp, (raw_ref, rope_ref, nw_ref) in enumerate(((qraw_ref, q_ref, qnw_ref),
                                                           (kraw_ref, k_ref, knw_ref))):
            for p in range(4):
                cols = slice(D_GRP * grp + 128 * p, D_GRP * grp + 128 * (p + 1))
                t = proj[:, cols]
                raw_ref[:, 128 * p:128 * (p + 1)] = t
                yn, _ = _head_norm(t, nw_ref[...], bdm)
                roped = yn * cos_t + _swap_halves(yn) * sin_t
                slabs[4 * grp + p] = roped
                rope_ref[:, 128 * p:128 * (p + 1)] = roped.astype(BF16)
        for p in range(4):
            slabs[8 + p] = proj[:, 2 * D_GRP + 128 * p:2 * D_GRP + 128 * (p + 1)]
        for grp, ref in ((2, va_ref), (3, qs_ref), (4, ks_ref), (5, vs_ref)):
            ref[...] = proj[:, D_GRP * grp:D_GRP * (grp + 1)].astype(BF16)
        _split_views(slabs, stage, (q4_ref, k4_ref, v4_ref), (q16_ref, k16_ref, v16_ref))

    row = lambda w: pl.BlockSpec((tm, w), lambda i: (i, 0))
    grp_bf = jax.ShapeDtypeStruct((s_len, D_GRP), BF16)
    grp_f32 = jax.ShapeDtypeStruct((s_len, D_GRP), F32)
    ni = s_len // tm
    return _call_with_gather(
        body, shards, lambda: pl.program_id(0) == 0, lambda: pl.program_id(0) == ni - 2,
        lambda: pl.program_id(0) == ni - 1,
        name="attn_in", grid=(ni,),
        in_specs=[row(D_MODEL), _full((1, D_MODEL)),
                  pl.BlockSpec((N_DEV, D_MODEL, IN_SHARD), lambda i: (0, 0, 0)),
                  row(128), row(128), _full((1, 128)), _full((1, 128)), _full((128, 128))],
        out_specs=(row(D_MODEL),) + (row(D_GRP),) * 8 + (_view_spec(tm, 4),) * 3 + (_view_spec(tm, 16),) * 3,
        out_shape=(jax.ShapeDtypeStruct((s_len, D_MODEL), BF16), grp_f32, grp_f32) + (grp_bf,) * 6
        + (_view_shape(s_len, 4, BF16),) * 3 + (_view_shape(s_len, 16, BF16),) * 3,
        scratch_shapes=[pltpu.VMEM((tm, D_IN), F32), pltpu.VMEM((12, tm, 128), F32), pltpu.VMEM((12, tm, 128), F32),
                        pltpu.VMEM((D_MODEL, D_IN), BF16)],
    )(x2, wn1, a_g, cos2, sin2, qnw, knw, bd, *shards)


def _band_mask(n):
    i = lax.broadcasted_iota(jnp.int32, (2 * BLOCK, 2 * BLOCK), 0) & (BLOCK - 1)
    j = lax.broadcasted_iota(jnp.int32, (2 * BLOCK, 2 * BLOCK), 1)
    dist = i + BLOCK - j
    return (dist >= 0) & (dist <= BLOCK) & ((n - 1) * BLOCK + j >= 0)


def _stack_heads(t2, head0):
    return jnp.concatenate([jnp.where(head0, t2, 0), jnp.where(head0, 0, t2)], axis=0)


def _unstack_heads(t, head0):
    return jnp.where(head0, t[0:BLOCK], t[BLOCK:2 * BLOCK])


def _dil_fwd(qv, kv, vv, r):
    sub_len = qv.shape[0]
    nb = sub_len // BLOCK

    qb = 2 if nb % 2 == 0 else 1

    def body(q_ref, kp_ref, kc_ref, vp_ref, vc_ref, o_ref, lse_ref):
        n = pl.program_id(1)
        lane = lax.broadcasted_iota(jnp.int32, (BLOCK, 128), 1)
        head0 = lane < HEAD_DIM
        units = [(b, slice(128 * p, 128 * (p + 1))) for b in range(qb) for p in range(4)]
        valid = [_band_mask(qb * n + b) for b in range(qb)]
        rows = [slice(BLOCK * b, BLOCK * (b + 1)) for b in range(qb)]

        def keys(prev_ref, cur_ref, b, c):
            before = prev_ref[:, c] if b == 0 else cur_ref[rows[b - 1], c]
            return jnp.concatenate([before, cur_ref[rows[b], c]], axis=0)

        qqs = [_stack_heads(q_ref[rows[b], c] * ATT_SCALE, head0) for b, c in units]
        kks = [keys(kp_ref, kc_ref, b, c) for b, c in units]
        vvs = [keys(vp_ref, vc_ref, b, c) for b, c in units]
        ss = [_dot_nt(qq, kk) for qq, kk in zip(qqs, kks)]
        prs, dens, lses = [], [], []
        for (b, _), s in zip(units, ss):
            s = jnp.where(valid[b], s, NEG)
            m = jnp.max(s, axis=-1, keepdims=True)
            pr = jnp.exp(s - m)
            den = jnp.sum(pr, axis=-1, keepdims=True)
            prs.append(pr.astype(BF16))
            dens.append(den)
            lses.append(m + jnp.log(den))
        pvs = [_dot(pr, vv2) for pr, vv2 in zip(prs, vvs)]
        for (b, c), pv, den, lse in zip(units, pvs, dens, lses):
            o_ref[rows[b], c] = _unstack_heads(pv / den, head0)
            lse_ref[rows[b], c] = _unstack_heads(jnp.broadcast_to(lse, (2 * BLOCK, 128)), head0)

    cur = pl.BlockSpec((qb * BLOCK, D_GRP), lambda c, n: (n, c))
    prev = pl.BlockSpec((BLOCK, D_GRP), lambda c, n: (jnp.maximum(qb * n - 1, 0), c))
    out = jax.ShapeDtypeStruct(qv.shape, F32)
    return pl.pallas_call(
        body, name=f"dil_fwd_r{r}", grid=(r, nb // qb),
        in_specs=[cur, prev, cur, prev, cur], out_specs=(cur, cur), out_shape=(out, out),
        compiler_params=_params(),
    )(qv, kv, kv, vv, vv)


def _dil_bwd(qv, kv, vv, dov, lsev, deltav, r, rides):
    sub_len = qv.shape[0]
    nb = sub_len // BLOCK

    def body(q_ref, kp_ref, kc_ref, vp_ref, vc_ref, do_ref, lse_ref, dl_ref,
             dq_ref, dk_ref, dv_ref, dk_carry, dv_carry):
        n = pl.program_id(1)

        @pl.when(n == 0)
        def _():
            dk_carry[...] = jnp.zeros_like(dk_carry)
            dv_carry[...] = jnp.zeros_like(dv_carry)

        @pl.when(n < nb)
        def _():
            valid = _band_mask(n)
            lane = lax.broadcasted_iota(jnp.int32, (BLOCK, 128), 1)
            head0 = lane < HEAD_DIM
            pairs = [slice(128 * p, 128 * (p + 1)) for p in range(4)]
            qqs = [_stack_heads(q_ref[:, c] * ATT_SCALE, head0) for c in pairs]
            dos = [_stack_heads(do_ref[:, c], head0) for c in pairs]
            kks = [jnp.concatenate([kp_ref[:, c], kc_ref[:, c]], axis=0) for c in pairs]
            vvs = [jnp.concatenate([vp_ref[:, c], vc_ref[:, c]], axis=0) for c in pairs]
            ss = [_dot_nt(qq, kk) for qq, kk in zip(qqs, kks)]
            dps = [_dot_nt(do, vv2) for do, vv2 in zip(dos, vvs)]
            def softmax_terms(p):
                stats = []
                for ref in (lse_ref, dl_ref):
                    t2 = ref[:, pairs[p]]
                    stats.append(jnp.concatenate(
                        [jnp.sum(jnp.where(lane == 0, t2, 0.0), axis=-1, keepdims=True),
                         jnp.sum(jnp.where(lane == HEAD_DIM, t2, 0.0), axis=-1, keepdims=True)], axis=0))
                pr = jnp.where(valid, jnp.exp(jnp.minimum(ss[p] - stats[0], 0.0)), 0.0)
                return pr.astype(BF16), (pr * (dps[p] - stats[1])).astype(BF16)

            terms = [softmax_terms(p) for p in range(4)]
            dqs = [_dot(terms[p][1], kks[p]) for p in range(4)]
            dkks = [_dot_tn(terms[p][1], qqs[p]) for p in range(4)]
            dvvs = [_dot_tn(terms[p][0], dos[p]) for p in range(4)]
            for c, dq, dkk, dvv in zip(pairs, dqs, dkks, dvvs):
                dq_ref[:, c] = _unstack_heads(dq, head0) * ATT_SCALE
                dk_ref[:, c] = dk_carry[:, c] + dkk[:BLOCK]
                dv_ref[:, c] = dv_carry[:, c] + dvv[:BLOCK]
                dk_carry[:, c] = dkk[BLOCK:]
                dv_carry[:, c] = dvv[BLOCK:]

        @pl.when(n == nb)
        def _():
            dk_ref[...] = dk_carry[...]
            dv_ref[...] = dv_carry[...]

    last = nb - 1
    cur = pl.BlockSpec((BLOCK, D_GRP), lambda c, n: (jnp.minimum(n, last), c))
    prev = pl.BlockSpec((BLOCK, D_GRP), lambda c, n: (jnp.clip(n - 1, 0, last), c))
    out = jax.ShapeDtypeStruct(qv.shape, F32)
    return _call_with_exchange(
        body, rides,
        lambda: jnp.logical_and(pl.program_id(0) == 0, pl.program_id(1) == 0),
        lambda: jnp.logical_and(pl.program_id(0) == r - 1, pl.program_id(1) == nb),
        name=f"dil_bwd_r{r}", grid=(r, nb + 1),
        in_specs=[cur, prev, cur, prev, cur, cur, cur, cur],
        out_specs=(cur, prev, prev), out_shape=(out, out, out),
        scratch_shapes=[pltpu.VMEM((BLOCK, D_GRP), F32), pltpu.VMEM((BLOCK, D_GRP), F32)],
    )(qv, kv, kv, vv, vv, dov, lsev, deltav, *rides)


def _sb_fwd(qs, ks, vs, tri_suf, shards):
    s_len = qs.shape[0]
    t = SB_TILE
    nq = s_len // t

    npair = SB_PAIRS

    def body(q_ref, k_ref, v_ref, u_ref, o_ref, c_ref, qq, vt, acc, cf, csave):
        row = lax.broadcasted_iota(jnp.int32, (2 * t, t), 0) & (t - 1)
        col = lax.broadcasted_iota(jnp.int32, (2 * t, t), 1)
        diag_mask = col < row
        lane1 = lax.broadcasted_iota(jnp.int32, (t, 128), 1)
        head0 = lane1 < HEAD_DIM
        lane2 = lax.broadcasted_iota(jnp.int32, (2 * t, 128), 1)
        uu = u_ref[...]
        pr = range(npair)
        cols = [slice(128 * pp, 128 * (pp + 1)) for pp in pr]

        i = pl.program_id(1)

        @pl.when(i == 0)
        def _():
            def transpose_v(j, _):
                rows = pl.ds(pl.multiple_of(j * t, t), t)
                for pp in pr:
                    vt[pp, j] = v_ref[rows, cols[pp]].astype(F32).T.astype(BF16)
                return 0

            lax.fori_loop(0, nq, transpose_v, 0)

        for pp in pr:
            q2 = q_ref[:, cols[pp]] * ATT_SCALE
            qq[pp, 0:t, :] = jnp.where(head0, q2, 0)
            qq[pp, t:2 * t, :] = jnp.where(head0, 0, q2)
        acc[...] = jnp.zeros_like(acc)
        cf[...] = jnp.zeros_like(cf)
        csave[...] = jnp.full(csave.shape, 2.0 * SB_DEAD, F32)

        def tile(kb, diag):
            krows = pl.ds(pl.multiple_of(kb * t, t), t)
            zs = [_dot_nt(qq[pp], k_ref[krows, cols[pp]]) for pp in pr]
            lbk = [_log_sigmoid_pair(z) for z in zs]
            lks = [jnp.where(diag_mask, lk, 0.0) if diag else lk for _, lk in lbk]
            sufs = [_cumsum_mm(lk, uu) for lk in lks]
            carries = [cf[pp] for pp in pr]
            avs = []
            for pp in pr:
                a = jnp.exp(lbk[pp][0] + (sufs[pp] + jnp.concatenate([carries[pp]] * (t // 128), axis=1)))
                avs.append((jnp.where(diag_mask, a, 0.0) if diag else a).astype(BF16))
            pvs = [_dot_nt(vt[pp, kb], avs[pp]) for pp in pr]
            for pp in pr:
                acc[pp] += pvs[pp]
                csave[pp] = jnp.where(lane2 == kb, carries[pp], csave[pp])
                cf[pp] = carries[pp] + jnp.broadcast_to(jnp.sum(lks[pp], axis=-1, keepdims=True), (2 * t, 128))

        tile(i, True)

        def alive():
            return jnp.max(cf[...]) > SB_DEAD

        def k_block(state):
            kb, _ = state
            tile(kb, False)
            return kb - 1, alive()

        lax.while_loop(lambda state: jnp.logical_and(state[0] >= 0, state[1]), k_block, (i - 1, alive()))
        for pp in pr:
            o_ref[:, cols[pp]] = jnp.where(head0, acc[pp, :, 0:t].T, acc[pp, :, t:2 * t].T)
            c_ref[2 * pp] = csave[pp, 0:t, :]
            c_ref[2 * pp + 1] = csave[pp, t:2 * t, :]

    width = 128 * npair
    kv = pl.BlockSpec((s_len, width), lambda p, i: (0, p))
    qo = pl.BlockSpec((t, width), lambda p, i: (i, p))
    steps = 4 // npair

    def at(p, i):
        return lambda: jnp.logical_and(pl.program_id(0) == p, pl.program_id(1) == i)

    return _call_with_gather(
        body, shards, at(0, 0), at(steps - 1, (2 * nq) // 3), at(steps - 1, nq - 1),
        name="sb_fwd", grid=(steps, nq),
        in_specs=[qo, kv, kv, pl.BlockSpec((t, t), lambda p, i: (0, 0))],
        out_specs=(qo, pl.BlockSpec((2 * npair, t, 128), lambda p, i: (p, i, 0))),
        out_shape=(jax.ShapeDtypeStruct((s_len, D_GRP), F32),
                   jax.ShapeDtypeStruct((8, s_len, 128), F32)),
        scratch_shapes=[pltpu.VMEM((npair, 2 * t, 128), BF16), pltpu.VMEM((npair, nq, 128, t), BF16),
                        pltpu.VMEM((npair, 128, 2 * t), F32),
                        pltpu.VMEM((npair, 2 * t, 128), F32), pltpu.VMEM((npair, 2 * t, 128), F32)],
    )(qs, ks, vs, tri_suf, *shards)


def _sb_bwd(qs, ks, vs, dos, csaved, tri_suf, tri_pre, rides):
    s_len = qs.shape[0]
    t = SB_TILE
    nq = s_len // t

    npair = SB_BWD_PAIRS

    def body(q_ref, k_ref, v_ref, do_ref, c_ref, u_ref, p_ref, dq_ref, dk_ref, dv_ref,
             qq, dd, qqt, ddt, kt, dq_acc, dkt, dvt, cg):
        row = lax.broadcasted_iota(jnp.int32, (2 * t, t), 0) & (t - 1)
        col = lax.broadcasted_iota(jnp.int32, (2 * t, t), 1)
        diag_mask = col < row
        lane1 = lax.broadcasted_iota(jnp.int32, (t, 128), 1)
        head0 = lane1 < HEAD_DIM
        lane2 = lax.broadcasted_iota(jnp.int32, (2 * t, 128), 1)
        uu, pm = u_ref[...], p_ref[...]
        pr = range(npair)
        cols = [slice(128 * pp, 128 * (pp + 1)) for pp in pr]
        i = pl.program_id(1)

        @pl.when(i == 0)
        def _():
            dkt[...] = jnp.zeros_like(dkt)
            dvt[...] = jnp.zeros_like(dvt)

            def transpose_k(j, _):
                rows = pl.ds(pl.multiple_of(j * t, t), t)
                for pp in pr:
                    kt[pp, j] = k_ref[rows, cols[pp]].astype(F32).T.astype(BF16)
                return 0

            lax.fori_loop(0, nq, transpose_k, 0)

        for pp in pr:
            q2 = q_ref[:, cols[pp]].astype(F32) * ATT_SCALE
            do2 = do_ref[:, cols[pp]].astype(F32)
            for src, nat, tr in ((q2, qq, qqt), (do2, dd, ddt)):
                stacked = jnp.concatenate([jnp.where(head0, src, 0.0), jnp.where(head0, 0.0, src)], axis=0)
                nat[pp] = stacked.astype(BF16)
                tr[pp] = stacked.T.astype(BF16)
        dq_acc[...] = jnp.zeros_like(dq_acc)
        cg[...] = jnp.zeros_like(cg)

        def tile(kb, diag):
            krows = pl.ds(pl.multiple_of(kb * t, t), t)
            zs = [_dot_nt(qq[pp], k_ref[krows, cols[pp]]) for pp in pr]
            das = [_dot_nt(dd[pp], v_ref[krows, cols[pp]]) for pp in pr]
            lbk = [_log_sigmoid_pair(z) for z in zs]
            lks = [jnp.where(diag_mask, lk, 0.0) if diag else lk for _, lk in lbk]
            sufs = [_cumsum_mm(lk, uu) for lk in lks]
            avs, gs = [], []
            for pp in pr:
                cs = jnp.concatenate([c_ref[2 * pp], c_ref[2 * pp + 1]], axis=0)
                cf = jnp.sum(jnp.where(lane2 == kb, cs, 0.0), axis=-1, keepdims=True)
                a = jnp.exp(lbk[pp][0] + (sufs[pp] + cf))
                a = jnp.where(diag_mask, a, 0.0) if diag else a
                avs.append(a.astype(BF16))
                gs.append(a * das[pp])
            gpres = [_cumsum_mm(g, pm) for g in gs]
            dzs = []
            for pp in pr:
                carry = cg[pp]
                beta = jnp.exp(lbk[pp][0])
                dz = gs[pp] - beta * (gs[pp] + (gpres[pp] + jnp.concatenate([carry] * (t // 128), axis=1)))
                dzs.append((jnp.where(diag_mask, dz, 0.0) if diag else dz).astype(BF16))
                cg[pp] = carry + jnp.broadcast_to(jnp.sum(gs[pp], axis=-1, keepdims=True), (2 * t, 128))
            dqs = [_dot_nt(kt[pp, kb], dzs[pp]) for pp in pr]
            dks = [_dot(qqt[pp], dzs[pp]) for pp in pr]
            dvs = [_dot(ddt[pp], avs[pp]) for pp in pr]
            for pp in pr:
                dq_acc[pp] += dqs[pp]
                dkt[pp, kb] += dks[pp]
                dvt[pp, kb] += dvs[pp]

        def k_block(kb, _):
            tile(kb, False)
            return 0

        col_max = jnp.max(jnp.max(c_ref[...], axis=0), axis=0, keepdims=True)
        lane_row = lax.broadcasted_iota(jnp.int32, (1, 128), 1)
        n_live = jnp.sum(jnp.where(jnp.logical_and(col_max > SB_DEAD, lane_row < i), 1, 0))
        lax.fori_loop(i - n_live, i, k_block, 0)
        tile(i, True)
        for pp in pr:
            dq_ref[:, cols[pp]] = jnp.where(head0, dq_acc[pp, :, 0:t].T, dq_acc[pp, :, t:2 * t].T) * ATT_SCALE

        @pl.when(i == nq - 1)
        def _():
            def untranspose(j, _):
                rows = pl.ds(pl.multiple_of(j * t, t), t)
                for pp in pr:
                    dk_ref[rows, cols[pp]] = dkt[pp, j].T
                    dv_ref[rows, cols[pp]] = dvt[pp, j].T
                return 0

            lax.fori_loop(0, nq, untranspose, 0)

    width = 128 * npair
    kv = pl.BlockSpec((s_len, width), lambda p, i: (0, p))
    qo = pl.BlockSpec((t, width), lambda p, i: (i, p))
    tri = pl.BlockSpec((t, t), lambda p, i: (0, 0))
    out = jax.ShapeDtypeStruct((s_len, D_GRP), F32)
    steps = 4 // npair
    return _call_with_exchange(
        body, rides,
        lambda: jnp.logical_and(pl.program_id(0) == 0, pl.program_id(1) == 0),
        lambda: jnp.logical_and(pl.program_id(0) == steps - 1, pl.program_id(1) == nq - 1),
        name="sb_bwd", grid=(steps, nq),
        in_specs=[qo, kv, kv, qo, pl.BlockSpec((2 * npair, t, 128), lambda p, i: (p, i, 0)), tri, tri],
        out_specs=(qo, kv, kv), out_shape=(out, out, out),
        scratch_shapes=[pltpu.VMEM((npair, 2 * t, 128), BF16), pltpu.VMEM((npair, 2 * t, 128), BF16),
                        pltpu.VMEM((npair, 128, 2 * t), BF16), pltpu.VMEM((npair, 128, 2 * t), BF16),
                        pltpu.VMEM((npair, nq, 128, t), BF16),
                        pltpu.VMEM((npair, 128, 2 * t), F32),
                        pltpu.VMEM((npair, nq, 128, t), F32), pltpu.VMEM((npair, nq, 128, t), F32),
                        pltpu.VMEM((npair, 2 * t, 128), F32)],
    )(qs, ks, vs, dos, csaved, tri_suf, tri_pre, *rides)


def _attn_out(o_b, lse_b, o_sb, x2, wdil, wsb, out_g, shards):
    s_len = x2.shape[0]
    tm = ROW_TILE

    def body(o1_ref, l1_ref, o4_ref, l4_ref, o16_ref, l16_ref, osb_ref, x_ref, wdil_ref, wsb_ref, w_ref,
             odil_ref, lse_ref, lse4_ref, lse16_ref, mixed_ref, x1_ref, stage, nat4, nat16):
        _merge_views((o4_ref, l4_ref), (o16_ref, l16_ref), stage, nat4, nat16)
        os_ = (o1_ref[...], _slab_group(nat4, 0), _slab_group(nat16, 0))
        ls = (l1_ref[...], _slab_group(nat4, 1), _slab_group(nat16, 1))
        mx = jnp.maximum(jnp.maximum(ls[0], ls[1]), ls[2])
        es = [jnp.exp(l - mx) for l in ls]
        den = es[0] + es[1] + es[2]
        o_dil = (es[0] * os_[0] + es[1] * os_[1] + es[2] * os_[2]) / den
        odil_ref[...] = o_dil
        lse = mx + jnp.log(den)
        lse_ref[...] = lse
        for p in range(4):
            nat4[p] = lse[:, 128 * p:128 * (p + 1)]
        _split_views(nat4.at[0:4], stage.at[0:4], (lse4_ref,), (lse16_ref,))
        halves = []
        for t, w_r in ((o_dil, wdil_ref), (osb_ref[...], wsb_ref)):
            r = lax.rsqrt(jnp.mean(t * t, axis=-1, keepdims=True) + EPS)
            halves.append(((t * r) * w_r[...]).astype(BF16))
        mixed = jnp.concatenate(halves, axis=1)
        mixed_ref[...] = mixed
        w = w_ref[...].reshape(D_MODEL, D_MODEL)
        x1_ref[...] = x_ref[...] + _dot(mixed, w)

    row = lambda w: pl.BlockSpec((tm, w), lambda i: (i, 0))
    ni = s_len // tm
    return _call_with_gather(
        body, shards, lambda: pl.program_id(0) == 0, lambda: pl.program_id(0) == ni - 2,
        lambda: pl.program_id(0) == ni - 1,
        name="attn_out", grid=(ni,),
        in_specs=[row(D_GRP)] * 2 + [_view_spec(tm, 4)] * 2 + [_view_spec(tm, 16)] * 2
        + [row(D_GRP), row(D_MODEL), _full((1, D_GRP)), _full((1, D_GRP)), _full((N_DEV, OUT_SHARD, D_MODEL))],
        out_specs=(row(D_GRP), row(D_GRP), _view_spec(tm, 4), _view_spec(tm, 16), row(D_MODEL), row(D_MODEL)),
        out_shape=(jax.ShapeDtypeStruct((s_len, D_GRP), F32), jax.ShapeDtypeStruct((s_len, D_GRP), F32),
                   _view_shape(s_len, 4, F32), _view_shape(s_len, 16, F32),
                   jax.ShapeDtypeStruct((s_len, D_MODEL), BF16), jax.ShapeDtypeStruct((s_len, D_MODEL), F32)),
        scratch_shapes=[pltpu.VMEM((8, tm, 128), F32)] * 3,
    )(o_b[0], lse_b[0], o_b[1], lse_b[1], o_b[2], lse_b[2], o_sb, x2, wdil, wsb, out_g, *shards)


def _two_shards(w_ref):
    return w_ref[...].reshape(FF_BLOCK, D_MODEL)


def _ffn_fwd(x1, wn2, tgt, gate_g, up_g, down_g):
    s_len = x1.shape[0]
    tm = ROW_TILE
    ni = s_len // tm

    def body(x_ref, wn_ref, t_ref, wg_ref, wu_ref, wd_ref, g_ref, u_ref, h2_ref, dy_ref, loss_ref, acc):
        j = pl.program_id(1)

        @pl.when(j == 0)
        def _():
            xx = x_ref[...]
            r = lax.rsqrt(jnp.mean(xx * xx, axis=-1, keepdims=True) + EPS)
            h2_ref[...] = ((xx * r) * wn_ref[...]).astype(BF16)
            acc[...] = jnp.zeros_like(acc)

        h = h2_ref[...]
        g = _dot_nt(h, _two_shards(wg_ref))
        u = _dot_nt(h, _two_shards(wu_ref))
        g_ref[...] = g
        u_ref[...] = u
        act = (g * (1.0 / (1.0 + jnp.exp(-g)))) * u
        acc[...] += _dot(act.astype(BF16), _two_shards(wd_ref))

        @pl.when(j == FF_STEPS - 1)
        def _():
            err = (x_ref[...] + acc[...]) - t_ref[...]
            dy_ref[...] = err * (1.0 / D_MODEL)
            part = 0.5 * jnp.sum(jnp.mean(err * err, axis=-1, keepdims=True))
            loss_ref[...] = jnp.full((8, 128), part, F32)

    row = pl.BlockSpec((tm, D_MODEL), lambda i, j: (i, 0))
    hid = pl.BlockSpec((tm, FF_BLOCK), lambda i, j: (i, j))
    return pl.pallas_call(
        body, name="ffn_fwd", grid=(ni, FF_STEPS),
        in_specs=[row, pl.BlockSpec((1, D_MODEL), lambda i, j: (0, 0)), row,
                  pl.BlockSpec((2, FF_PAD, D_MODEL), lambda i, j: (j, 0, 0)),
                  pl.BlockSpec((2, FF_PAD, D_MODEL), lambda i, j: (j, 0, 0)),
                  pl.BlockSpec((2, FF_PAD, D_MODEL), lambda i, j: (j, 0, 0))],
        out_specs=(hid, hid, row, row, pl.BlockSpec((8, 128), lambda i, j: (i, 0))),
        out_shape=(jax.ShapeDtypeStruct((s_len, N_DEV * FF_PAD), F32),
                   jax.ShapeDtypeStruct((s_len, N_DEV * FF_PAD), F32),
                   jax.ShapeDtypeStruct((s_len, D_MODEL), BF16),
                   jax.ShapeDtypeStruct((s_len, D_MODEL), F32),
                   jax.ShapeDtypeStruct((ni * 8, 128), F32)),
        scratch_shapes=[pltpu.VMEM((tm, D_MODEL), F32)],
        compiler_params=_params(),
    )(x1, wn2, tgt, gate_g, up_g, down_g)


def _ffn_bwd_dx(dy, g, u, gate_g, up_g, down_g):
    s_len = dy.shape[0]
    tm = ROW_TILE

    def body(dy_ref, g_ref, u_ref, wg_ref, wu_ref, wd_ref, dg_ref, du_ref, act_ref, dh_ref, acc):
        j = pl.program_id(1)

        @pl.when(j == 0)
        def _():
            acc[...] = jnp.zeros_like(acc)

        halves = [slice(0, tm // 2), slice(tm // 2, tm)]
        wd, wg, wu = _two_shards(wd_ref), _two_shards(wg_ref), _two_shards(wu_ref)
        das = [_dot_nt(dy_ref[rows, :].astype(BF16), wd) for rows in halves]

        def elementwise(rows, da):
            gg, uu = g_ref[rows, :], u_ref[rows, :]
            sig = 1.0 / (1.0 + jnp.exp(-gg))
            silu = gg * sig
            act_ref[rows, :] = (silu * uu).astype(BF16)
            du = (da * silu).astype(BF16)
            dg = (da * uu * (sig * (1.0 + gg * (1.0 - sig)))).astype(BF16)
            du_ref[rows, :] = du
            dg_ref[rows, :] = dg
            return dg, du

        dg0, du0 = elementwise(halves[0], das[0])
        acc[halves[0], :] += _dot(dg0, wg) + _dot(du0, wu)
        dg1, du1 = elementwise(halves[1], das[1])
        acc[halves[1], :] += _dot(dg1, wg) + _dot(du1, wu)

        @pl.when(j == FF_STEPS - 1)
        def _():
            dh_ref[...] = acc[...]

    row = pl.BlockSpec((tm, D_MODEL), lambda i, j: (i, 0))
    hid = pl.BlockSpec((tm, FF_BLOCK), lambda i, j: (i, j))
    hid_bf = jax.ShapeDtypeStruct((s_len, N_DEV * FF_PAD), BF16)
    return pl.pallas_call(
        body, name="ffn_bwd_dx", grid=(s_len // tm, FF_STEPS),
        in_specs=[row, hid, hid,
                  pl.BlockSpec((2, FF_PAD, D_MODEL), lambda i, j: (j, 0, 0)),
                  pl.BlockSpec((2, FF_PAD, D_MODEL), lambda i, j: (j, 0, 0)),
                  pl.BlockSpec((2, FF_PAD, D_MODEL), lambda i, j: (j, 0, 0))],
        out_specs=(hid, hid, hid, row),
        out_shape=(hid_bf, hid_bf, hid_bf, jax.ShapeDtypeStruct((s_len, D_MODEL), F32)),
        scratch_shapes=[pltpu.VMEM((tm, D_MODEL), F32)],
        compiler_params=_params(),
    )(dy, g, u, gate_g, up_g, down_g)


def _ffn_bwd_dw(h2, dy, dg, du, act):
    s_len = h2.shape[0]
    tm = DW_ROW_TILE
    ni = s_len // tm

    def body(h_ref, dy_ref, dg_ref, du_ref, act_ref, dwg_ref, dwu_ref, dwd_ref, ag, au, ad):
        i = pl.program_id(1)

        @pl.when(i == 0)
        def _():
            ag[...] = jnp.zeros_like(ag)
            au[...] = jnp.zeros_like(au)
            ad[...] = jnp.zeros_like(ad)

        h = h_ref[...]
        ag[...] += _dot_tn(dg_ref[...], h)
        au[...] += _dot_tn(du_ref[...], h)
        ad[...] += _dot_tn(act_ref[...], dy_ref[...].astype(BF16))

        @pl.when(i == ni - 1)
        def _():
            for acc_ref, out_ref in ((ag, dwg_ref), (au, dwu_ref), (ad, dwd_ref)):
                out_ref[...] = acc_ref[...].astype(BF16).reshape(2, FF_PAD, D_MODEL)

    row = pl.BlockSpec((tm, D_MODEL), lambda j, i: (i, 0))
    hid = pl.BlockSpec((tm, FF_BLOCK), lambda j, i: (i, j))
    row_w = pl.BlockSpec((2, FF_PAD, D_MODEL), lambda j, i: (j, 0, 0))
    grad = jax.ShapeDtypeStruct((N_DEV, FF_PAD, D_MODEL), BF16)
    return pl.pallas_call(
        body, name="ffn_bwd_dw", grid=(FF_STEPS, ni),
        in_specs=[row, row, hid, hid, hid], out_specs=(row_w, row_w, row_w),
        out_shape=(grad, grad, grad),
        scratch_shapes=[pltpu.VMEM((FF_BLOCK, D_MODEL), F32)] * 3,
        compiler_params=_params(),
    )(h2, dy, dg, du, act)


def _rms_bwd(dy, t, w):
    r = lax.rsqrt(jnp.mean(t * t, axis=-1, keepdims=True) + EPS)
    gw = dy * w
    dt = r * (gw - t * ((r * r) * jnp.mean(gw * t, axis=-1, keepdims=True)))
    return dt, dy * t * r


def _attn_out_bwd(dy, dh2, x1, wn2, b_g, mixed, o_dil, o_sb, wdil, wsb, bd512):
    s_len = dy.shape[0]
    tm = ROW_TILE
    ni = s_len // tm

    def body(dy_ref, dh_ref, x1_ref, wn_ref, w_ref, mixed_ref, odil_ref, osb_ref, wdil_ref, wsb_ref, bd_ref,
             dx1_ref, dodil_ref, delta_ref, dosb_ref, dwout_ref, dwn_ref, dwdil_ref, dwsb_ref,
             do4_ref, dl4_ref, do16_ref, dl16_ref, wacc, both, stage):
        i = pl.program_id(0)

        @pl.when(i == 0)
        def _():
            wacc[...] = jnp.zeros_like(wacc)
            dwn_ref[...] = jnp.zeros_like(dwn_ref)
            dwdil_ref[...] = jnp.zeros_like(dwdil_ref)
            dwsb_ref[...] = jnp.zeros_like(dwsb_ref)

        dnorm, dw_rows = _rms_bwd(dh_ref[...], x1_ref[...], wn_ref[...])
        dx1 = dy_ref[...] + dnorm
        dx1_ref[...] = dx1
        dwn_ref[...] += jnp.sum(dw_rows, axis=0, keepdims=True)
        dx1b = dx1.astype(BF16)
        w = w_ref[...].reshape(D_MODEL, D_MODEL)
        dmixed = _dot_nt(dx1b, w)
        wacc[...] += _dot_tn(mixed_ref[...], dx1b)
        o_dil = odil_ref[...]
        d_odil, dw_rows = _rms_bwd(dmixed[:, :D_GRP], o_dil, wdil_ref[...])
        dwdil_ref[...] += jnp.sum(dw_rows, axis=0, keepdims=True)
        dodil_ref[...] = d_odil.astype(BF16)
        delta = _mm_split(d_odil * o_dil, bd_ref[...])
        delta_ref[...] = delta
        for p in range(4):
            both[p] = d_odil[:, 128 * p:128 * (p + 1)]
            both[4 + p] = delta[:, 128 * p:128 * (p + 1)]
        _split_views(both, stage, (do4_ref, dl4_ref), (do16_ref, dl16_ref))
        d_osb, dw_rows = _rms_bwd(dmixed[:, D_GRP:], osb_ref[...], wsb_ref[...])
        dwsb_ref[...] += jnp.sum(dw_rows, axis=0, keepdims=True)
        dosb_ref[...] = d_osb.astype(BF16)

        @pl.when(i == ni - 1)
        def _():
            dwout_ref[...] = wacc[...].astype(BF16).reshape(N_DEV, OUT_SHARD, D_MODEL)

    row = lambda w: pl.BlockSpec((tm, w), lambda i: (i, 0))
    return pl.pallas_call(
        body, name="attn_out_bwd", grid=(ni,),
        in_specs=[row(D_MODEL), row(D_MODEL), row(D_MODEL), _full((1, D_MODEL)),
                  _full((N_DEV, OUT_SHARD, D_MODEL)),
                  row(D_MODEL), row(D_GRP), row(D_GRP), _full((1, D_GRP)), _full((1, D_GRP)),
                  _full((D_GRP, D_GRP))],
        out_specs=(row(D_MODEL), row(D_GRP), row(D_GRP), row(D_GRP),
                   _full((N_DEV, OUT_SHARD, D_MODEL)), _full((1, D_MODEL)), _full((1, D_GRP)), _full((1, D_GRP)),
                   _view_spec(tm, 4), _view_spec(tm, 4), _view_spec(tm, 16), _view_spec(tm, 16)),
        out_shape=(jax.ShapeDtypeStruct((s_len, D_MODEL), F32), jax.ShapeDtypeStruct((s_len, D_GRP), BF16),
                   jax.ShapeDtypeStruct((s_len, D_GRP), F32), jax.ShapeDtypeStruct((s_len, D_GRP), BF16),
                   jax.ShapeDtypeStruct((N_DEV, OUT_SHARD, D_MODEL), BF16),
                   jax.ShapeDtypeStruct((1, D_MODEL), F32), jax.ShapeDtypeStruct((1, D_GRP), F32),
                   jax.ShapeDtypeStruct((1, D_GRP), F32),
                   _view_shape(s_len, 4, BF16), _view_shape(s_len, 4, F32),
                   _view_shape(s_len, 16, BF16), _view_shape(s_len, 16, F32)),
        scratch_shapes=[pltpu.VMEM((D_MODEL, D_MODEL), F32), pltpu.VMEM((8, tm, 128), F32),
                        pltpu.VMEM((8, tm, 128), F32)],
        compiler_params=_params(),
    )(dy, dh2, x1, wn2, b_g, mixed, o_dil, o_sb, wdil, wsb, bd512)


def _qkv_bwd(dq_b, dk_b, dv_b, dqs, dks, dvs, qraw, kraw, cos2, sin2, qnw, knw, bd):
    s_len = qraw.shape[0]
    tm = ROW_TILE
    ni = s_len // tm

    def body(dq1, dk1, dv1, dq4, dk4, dv4, dq16, dk16, dv16, dqs_ref, dks_ref, dvs_ref,
             qraw_ref, kraw_ref, cos_ref, sin_ref, qnw_ref, knw_ref, bd_ref,
             dproj_ref, dqn_ref, dkn_ref, stage, nat4, nat16):
        i = pl.program_id(0)

        @pl.when(i == 0)
        def _():
            dqn_ref[...] = jnp.zeros_like(dqn_ref)
            dkn_ref[...] = jnp.zeros_like(dkn_ref)

        _merge_views((dq4, dk4, dv4), (dq16, dk16, dv16), stage, nat4, nat16)
        cos_t, sin_t, bdm = cos_ref[...], sin_ref[...], bd_ref[...]
        for grp, (part1, raw_ref, nw_ref, dn_ref) in enumerate(((dq1, qraw_ref, qnw_ref, dqn_ref),
                                                                (dk1, kraw_ref, knw_ref, dkn_ref))):
            dn_acc = 0.0
            for p in range(4):
                cols = slice(128 * p, 128 * (p + 1))
                d_rope = part1[:, cols] + nat4[4 * grp + p] + nat16[4 * grp + p]
                d_norm = d_rope * cos_t + _swap_halves(d_rope * sin_t)
                t = raw_ref[:, cols]
                w = nw_ref[...]
                r = lax.rsqrt(_mm_split(t * t, bdm) * (1.0 / HEAD_DIM) + EPS)
                gw = d_norm * w
                corr = _mm_split(gw * t, bdm) * (1.0 / HEAD_DIM)
                dt = r * (gw - t * ((r * r) * corr))
                dn_acc = dn_acc + jnp.sum(d_norm * t * r, axis=0, keepdims=True)
                dproj_ref[:, D_GRP * grp + 128 * p:D_GRP * grp + 128 * (p + 1)] = dt.astype(BF16)
            dn_ref[...] += dn_acc
        dproj_ref[:, 2 * D_GRP:3 * D_GRP] = (dv1[...] + _slab_group(nat4, 2) + _slab_group(nat16, 2)).astype(BF16)
        dproj_ref[:, 3 * D_GRP:4 * D_GRP] = dqs_ref[...].astype(BF16)
        dproj_ref[:, 4 * D_GRP:5 * D_GRP] = dks_ref[...].astype(BF16)
        dproj_ref[:, 5 * D_GRP:6 * D_GRP] = dvs_ref[...].astype(BF16)

    row = lambda w: pl.BlockSpec((tm, w), lambda i: (i, 0))
    return pl.pallas_call(
        body, name="qkv_bwd", grid=(ni,),
        in_specs=[row(D_GRP)] * 3 + [_view_spec(tm, 4)] * 3 + [_view_spec(tm, 16)] * 3 + [row(D_GRP)] * 5
        + [row(128), row(128), _full((1, 128)), _full((1, 128)), _full((128, 128))],
        out_specs=(row(D_IN), _full((1, 128)), _full((1, 128))),
        out_shape=(jax.ShapeDtypeStruct((s_len, D_IN), BF16), jax.ShapeDtypeStruct((1, 128), F32),
                   jax.ShapeDtypeStruct((1, 128), F32)),
        scratch_shapes=[pltpu.VMEM((12, tm, 128), F32)] * 3,
        compiler_params=_params(),
    )(dq_b[0], dk_b[0], dv_b[0], dq_b[1], dk_b[1], dv_b[1], dq_b[2], dk_b[2], dv_b[2],
      dqs, dks, dvs, qraw, kraw, cos2, sin2, qnw, knw, bd)


def _in_bwd_dx(dproj, a_g, x2, dx1, wn1, rides):
    s_len = x2.shape[0]
    tm = ROW_TILE
    ni = s_len // tm

    def body(dp_ref, w_ref, x_ref, dx1_ref, wn_ref, gx_ref, dwn_ref, w_full):
        i = pl.program_id(0)

        @pl.when(i == 0)
        def _():
            dwn_ref[...] = jnp.zeros_like(dwn_ref)
            for d in range(N_DEV):
                w_full[:, IN_SHARD * d:IN_SHARD * (d + 1)] = w_ref[d]

        dh = _dot_nt(dp_ref[...], w_full[...])
        dnorm, dw_rows = _rms_bwd(dh, x_ref[...], wn_ref[...])
        gx_ref[...] = dx1_ref[...] + dnorm
        dwn_ref[...] += jnp.sum(dw_rows, axis=0, keepdims=True)

    row = lambda w: pl.BlockSpec((tm, w), lambda i: (i, 0))
    return _call_with_exchange(
        body, rides, lambda: pl.program_id(0) == 0, lambda: pl.program_id(0) == ni - 1,
        name="in_bwd_dx", grid=(ni,),
        in_specs=[row(D_IN), pl.BlockSpec((N_DEV, D_MODEL, IN_SHARD), lambda i: (0, 0, 0)),
                  row(D_MODEL), row(D_MODEL), _full((1, D_MODEL))],
        out_specs=(row(D_MODEL), _full((1, D_MODEL))),
        out_shape=(jax.ShapeDtypeStruct((s_len, D_MODEL), F32), jax.ShapeDtypeStruct((1, D_MODEL), F32)),
        scratch_shapes=[pltpu.VMEM((D_MODEL, D_IN), BF16)],
    )(dproj, a_g, x2, dx1, wn1, *_ride_arrays(rides))


def _in_bwd_dw(h1, dproj, part, rides):
    s_len = h1.shape[0]
    tm = DW_ROW_TILE
    ni = s_len // tm
    half_d = D_MODEL // 2

    def body(h_ref, dp_ref, dw_ref, acc):
        i = pl.program_id(1)

        @pl.when(i == 0)
        def _():
            acc[...] = jnp.zeros_like(acc)

        acc[...] += _dot_tn(h_ref[...], dp_ref[...])

        @pl.when(i == ni - 1)
        def _():
            for half in range(2):
                dw_ref[half] = acc[:, IN_SHARD * half:IN_SHARD * (half + 1)].astype(BF16)

    return _call_with_exchange(
        body, rides,
        lambda: jnp.logical_and(pl.program_id(0) == 0, pl.program_id(1) == 0),
        lambda: jnp.logical_and(pl.program_id(0) == N_DEV // 2 - 1, pl.program_id(1) == ni - 1),
        name=f"in_bwd_dw_{part}", grid=(N_DEV // 2, ni),
        in_specs=[pl.BlockSpec((tm, half_d), lambda d, i: (i, part)),
                  pl.BlockSpec((tm, 2 * IN_SHARD), lambda d, i: (i, d))],
        out_specs=pl.BlockSpec((2, half_d, IN_SHARD), lambda d, i: (d, 0, 0)),
        out_shape=jax.ShapeDtypeStruct((N_DEV, half_d, IN_SHARD), BF16),
        scratch_shapes=[pltpu.VMEM((half_d, 2 * IN_SHARD), F32)],
    )(h1, dproj, *_ride_arrays(rides))


def _adamw(recv, w, m, v, recv_hi=None):
    rows, cols = w.shape
    tr = next((t for t in (128, 32) if rows % t == 0), rows)
    recvs = [recv] if recv_hi is None else [recv, recv_hi]
    half_tiles = rows // tr // 2

    def body(*refs):
        p_refs = refs[:len(recvs)]
        w_ref, m_ref, v_ref, g_ref, d_ref, nm_ref, nv_ref = refs[len(recvs):]

        def slot(s):
            if len(p_refs) == 1:
                return p_refs[0][s].astype(F32)
            return jnp.where(pl.program_id(0) < half_tiles, p_refs[0][s], p_refs[1][s]).astype(F32)

        g = slot(0)
        for s in range(1, N_DEV):
            g = g + slot(s)
        m_new = ADAM_B1 * m_ref[...] + (1.0 - ADAM_B1) * g
        v_new = ADAM_B2 * v_ref[...] + (1.0 - ADAM_B2) * (g * g)
        m_hat = m_new / (1.0 - ADAM_B1 ** ADAM_STEP)
        v_hat = v_new / (1.0 - ADAM_B2 ** ADAM_STEP)
        g_ref[...] = g
        d_ref[...] = -ADAM_LR * (m_hat / (jnp.sqrt(v_hat) + ADAM_EPS) + ADAM_WD * w_ref[...])
        nm_ref[...] = m_new
        nv_ref[...] = v_new

    blk = pl.BlockSpec((tr, cols), lambda i: (i, 0))
    out = jax.ShapeDtypeStruct((rows, cols), F32)
    return pl.pallas_call(
        body, name=f"adamw_{rows}x{cols}", grid=(rows // tr,),
        in_specs=([pl.BlockSpec((N_DEV, tr, cols), lambda i: (0, i, 0))] if recv_hi is None else
                  [pl.BlockSpec((N_DEV, tr, cols), lambda i: (0, jnp.minimum(i, half_tiles - 1), 0)),
                   pl.BlockSpec((N_DEV, tr, cols), lambda i: (0, jnp.maximum(i - half_tiles, 0), 0))])
        + [blk, blk, blk],
        out_specs=(blk,) * 4, out_shape=(out,) * 4,
        compiler_params=_params(),
    )(*recvs, w, m, v)


def _adamw_update(g, w_ref, m_ref, v_ref, g_ref, d_ref, nm_ref, nv_ref):
    m_new = ADAM_B1 * m_ref[...] + (1.0 - ADAM_B1) * g
    v_new = ADAM_B2 * v_ref[...] + (1.0 - ADAM_B2) * (g * g)
    m_hat = m_new / (1.0 - ADAM_B1 ** ADAM_STEP)
    v_hat = v_new / (1.0 - ADAM_B2 ** ADAM_STEP)
    g_ref[...] = g
    d_ref[...] = -ADAM_LR * (m_hat / (jnp.sqrt(v_hat) + ADAM_EPS) + ADAM_WD * w_ref[...])
    nm_ref[...] = m_new
    nv_ref[...] = v_new


def _adamw_many(items, rides):
    tr = 32
    tiles = [w.shape[0] // tr for _, w, _, _ in items]
    starts = [sum(tiles[:k]) for k in range(len(items))]
    total = sum(tiles)

    def body(*refs):
        i = pl.program_id(0)
        n_items = len(items)
        in_refs, out_refs = refs[:4 * n_items], refs[4 * n_items:]
        for k in range(n_items):
            p_ref, w_ref, m_ref, v_ref = in_refs[4 * k:4 * k + 4]

            @pl.when(jnp.logical_and(i >= starts[k], i < starts[k] + tiles[k]))
            def _(p_ref=p_ref, w_ref=w_ref, m_ref=m_ref, v_ref=v_ref, k=k):
                g = p_ref[0].astype(F32)
                for s in range(1, N_DEV):
                    g = g + p_ref[s].astype(F32)
                _adamw_update(g, w_ref, m_ref, v_ref, *out_refs[4 * k:4 * k + 4])

    in_specs, out_specs, out_shape, args = [], [], [], []
    for k, (recv, w, m, v) in enumerate(items):
        tile_of = functools.partial(lambda i, s0, nk: jnp.clip(i - s0, 0, nk - 1), s0=starts[k], nk=tiles[k])
        blk = pl.BlockSpec((tr, D_MODEL), functools.partial(lambda i, t: (t(i), 0), t=tile_of))
        in_specs += [pl.BlockSpec((N_DEV, tr, D_MODEL), functools.partial(lambda i, t: (0, t(i), 0), t=tile_of)),
                     blk, blk, blk]
        out_specs += [blk] * 4
        out_shape += [jax.ShapeDtypeStruct(w.shape, F32)] * 4
        args += [recv, w, m, v]
    res = _call_with_exchange(
        body, rides, lambda: pl.program_id(0) == 0, lambda: pl.program_id(0) == total - 1,
        name="adamw_many", grid=(total,), in_specs=in_specs, out_specs=out_specs, out_shape=out_shape,
    )(*args, *_ride_arrays(rides))
    return [tuple(res[4 * k:4 * k + 4]) for k in range(len(items))], list(res[4 * len(items):])


def _rope_tables(s_len):
    pos = jnp.arange(s_len, dtype=F32)
    inv_freq = ROPE_THETA ** (-jnp.arange(0, HEAD_DIM, 2, dtype=F32) / HEAD_DIM)
    ang = pos[:, None] * inv_freq[None, :]
    cos, sin = jnp.cos(ang), jnp.sin(ang)
    cos2 = jnp.concatenate([cos, cos, cos, cos], axis=1)
    sin2 = jnp.concatenate([-sin, sin, -sin, sin], axis=1)
    return cos2, sin2


def _block_diag_ones(n):
    i = jnp.arange(n)
    return (i[:, None] // HEAD_DIM == i[None, :] // HEAD_DIM).astype(BF16)


def _pad_cols(t):
    return jnp.pad(t, ((0, 0), (0, FF_PAD - FF_SHARD)))


def _pad_rows(t):
    return jnp.pad(t, ((0, FF_PAD - FF_SHARD), (0, 0)))


LOSS_ROW = 26


def _pack_small(n1, n2, ndil, nsb, nq, nk, scalar=None):
    pad = lambda t: jnp.pad(t.reshape(1, -1), ((0, 0), (0, 128 - t.size)))
    last = jnp.zeros((1, 128), F32) if scalar is None else pad(scalar)
    rows = [n1.reshape(8, 128), n2.reshape(8, 128), ndil.reshape(4, 128), nsb.reshape(4, 128),
            pad(nq), pad(nk), last, jnp.zeros((5, 128), F32)]
    return jnp.concatenate(rows, axis=0)


def _unpack_small(t):
    return (t[0:8].reshape(1, D_MODEL), t[8:16].reshape(1, D_MODEL), t[16:20].reshape(1, D_GRP),
            t[20:24].reshape(1, D_GRP), t[24:25, :HEAD_DIM], t[25:26, :HEAD_DIM])


def kernel(x, attn_norm_w, w_in, q_norm_w, k_norm_w, dil_out_norm_w, sb_out_norm_w, w_out, ffn_norm_w, w_gate, w_up, w_down, loss_target, m_attn_norm_w, m_w_in, m_q_norm_w, m_k_norm_w, m_dil_out_norm_w, m_sb_out_norm_w, m_w_out, m_ffn_norm_w, m_w_gate, m_w_up, m_w_down, v_attn_norm_w, v_w_in, v_q_norm_w, v_k_norm_w, v_dil_out_norm_w, v_sb_out_norm_w, v_w_out, v_ffn_norm_w, v_w_gate, v_w_up, v_w_down):
    s_len = x.shape[1]
    x2, tgt = x[0], loss_target[0]

    (a_g,) = _gather_weights([w_in[0].astype(BF16)])
    gate_loc = _pad_cols(w_gate[0]).T.astype(BF16)
    up_loc = _pad_cols(w_up[0]).T.astype(BF16)
    down_loc = _pad_rows(w_down[0]).astype(BF16)
    out_loc = w_out[0].astype(BF16)

    cos2, sin2 = _rope_tables(s_len)
    bd128, bd512 = _block_diag_ones(128), _block_diag_ones(D_GRP)
    idx = jnp.arange(SB_TILE)
    tri_suf = (idx[:, None] > idx[None, :]).astype(BF16)
    tri_pre = (idx[:, None] < idx[None, :]).astype(BF16)
    qnw2 = jnp.concatenate([q_norm_w, q_norm_w], axis=1)
    knw2 = jnp.concatenate([k_norm_w, k_norm_w], axis=1)

    (h1, qraw, kraw, q, k, va, qs, ks, vs, q4, k4, v4, q16, k16, v16,
     out_g, gate_g) = _attn_in(x2, attn_norm_w, a_g, cos2, sin2, qnw2, knw2, bd128, shards=[out_loc, gate_loc])
    qkv_views = {1: (q, k, va), 4: (q4, k4, v4), 16: (q16, k16, v16)}
    o_b, lse_b = [], []
    for r in DILATIONS:
        o, lse = _dil_fwd(*qkv_views[r], r)
        o_b.append(o)
        lse_b.append(lse)
    o_sb, c_sb, up_g = _sb_fwd(qs, ks, vs, tri_suf, shards=[up_loc])
    o_dil, lse_tot, lse4, lse16, mixed, x1, down_g = _attn_out(
        o_b, lse_b, o_sb, x2, dil_out_norm_w, sb_out_norm_w, out_g, shards=[down_loc])
    g, u, h2, dy, loss_parts = _ffn_fwd(x1, ffn_norm_w, tgt, gate_g, up_g, down_g)
    loss_local = jnp.sum(loss_parts[::8, 0])

    dg, du, act, dh2 = _ffn_bwd_dx(dy, g, u, gate_g, up_g, down_g)
    (dx1, do_dil, delta, do_sb, dwout, dn2, dndil, dnsb, do4, dl4, do16, dl16) = _attn_out_bwd(
        dy, dh2, x1, ffn_norm_w, out_g, mixed, o_dil, o_sb, dil_out_norm_w, sb_out_norm_w, bd512)
    dwg, dwu, dwd = _ffn_bwd_dw(h2, dy, dg, du, act)
    dqs, dks, dvs, r_gate, r_down = _sb_bwd(qs, ks, vs, do_sb, c_sb, tri_suf, tri_pre, rides=[dwg, dwd])
    cot_views = {1: (do_dil, lse_tot, delta), 4: (do4, lse4, dl4), 16: (do16, lse16, dl16)}
    riders = {1: [], 4: [dwout], 16: [dwu]}
    dq_b, dk_b, dv_b, landed = [], [], [], {}
    for r in DILATIONS:
        dq, dk, dv, *landed[r] = _dil_bwd(*qkv_views[r], *cot_views[r], r, rides=riders[r])
        dq_b.append(dq)
        dk_b.append(dk)
        dv_b.append(dv)
    (r_out,), (r_up,) = landed[4], landed[16]
    dproj, dqn2, dkn2 = _qkv_bwd(dq_b, dk_b, dv_b, dqs, dks, dvs, qraw, kraw, cos2, sin2, qnw2, knw2, bd128)
    (dwin_lo,) = _in_bwd_dw(h1, dproj, 0, rides=[])
    grad_x, dn1, r_in_lo = _in_bwd_dx(dproj, a_g, x2, dx1, attn_norm_w, rides=[dwin_lo])
    (dwin_hi,) = _in_bwd_dw(h1, dproj, 1, rides=[])
    dqn = dqn2[:, :HEAD_DIM] + dqn2[:, HEAD_DIM:]
    dkn = dkn2[:, :HEAD_DIM] + dkn2[:, HEAD_DIM:]

    small = _pack_small(dn1, dn2, dndil, dnsb, dqn, dkn, loss_local)
    (r_small,) = _exchange_grads([], small)
    (up_gate, up_up, up_down, up_out), (r_in_hi,) = _adamw_many(
        [(r_gate, w_gate[0].T, m_w_gate[0].T, v_w_gate[0].T), (r_up, w_up[0].T, m_w_up[0].T, v_w_up[0].T),
         (r_down, w_down[0], m_w_down[0], v_w_down[0]), (r_out, w_out[0], m_w_out[0], v_w_out[0])],
        rides=[dwin_hi])
    big = {
        "w_in": _adamw(r_in_lo, w_in[0], m_w_in[0], v_w_in[0], recv_hi=r_in_hi),
        "w_gate": tuple(t.T for t in up_gate), "w_up": tuple(t.T for t in up_up),
        "w_down": up_down, "w_out": up_out,
    }
    packs = [_pack_small(*ts) for ts in (
        (attn_norm_w, ffn_norm_w, dil_out_norm_w, sb_out_norm_w, q_norm_w, k_norm_w),
        (m_attn_norm_w, m_ffn_norm_w, m_dil_out_norm_w, m_sb_out_norm_w, m_q_norm_w, m_k_norm_w),
        (v_attn_norm_w, v_ffn_norm_w, v_dil_out_norm_w, v_sb_out_norm_w, v_q_norm_w, v_k_norm_w))]
    small_raw = _adamw(r_small, *packs)
    loss = small_raw[0][LOSS_ROW, 0]
    small_out = [_unpack_small(t) for t in small_raw]
    names = ["attn_norm_w", "w_in", "q_norm_w", "k_norm_w", "dil_out_norm_w", "sb_out_norm_w", "w_out",
             "ffn_norm_w", "w_gate", "w_up", "w_down"]
    small_pos = {"attn_norm_w": 0, "ffn_norm_w": 1, "dil_out_norm_w": 2, "sb_out_norm_w": 3,
                 "q_norm_w": 4, "k_norm_w": 5}
    outs = [loss, grad_x[None]]
    for kind in range(4):
        for name in names:
            if name in small_pos:
                outs.append(small_out[kind][small_pos[name]])
            else:
                outs.append(big[name][kind][None])
    return tuple(outs)
```

```python
import jax
import jax.numpy as jnp
from jax import lax
from jax.experimental import pallas as pl
from jax.experimental.pallas import tpu as pltpu

F32 = jnp.float32
BF16 = jnp.bfloat16

N_DEV = 8
D_MODEL = 1024
HEAD_DIM = 64
D_GRP = 512
D_IN = 6 * D_GRP
IN_SHARD = D_IN // N_DEV
FF_SHARD = 352
FF_PAD = 384
FF_BLOCK = 2 * FF_PAD
FF_STEPS = N_DEV // 2
OUT_SHARD = D_MODEL // N_DEV
BLOCK = 128
DILATIONS = (1, 4, 16)
ROPE_THETA = 10000.0
EPS = 1e-6
ATT_SCALE = HEAD_DIM ** -0.5
NEG = -1e30

ADAM_LR = 0.001
ADAM_B1 = 0.9
ADAM_B2 = 0.999
ADAM_EPS = 1e-08
ADAM_WD = 0.01
ADAM_STEP = 10

SB_TILE = 256
SB_DEAD = -104.0
SB_PAIRS = 4
SB_BWD_PAIRS = 2
ROW_TILE = 512
DW_ROW_TILE = 1024
VMEM_LIMIT = 56 * 1024 * 1024
MESH = pl.DeviceIdType.MESH


def _dot(a, b):
    return jnp.dot(a, b, preferred_element_type=F32)


def _dot_nt(a, b):
    return lax.dot_general(a, b, (((1,), (1,)), ((), ())), preferred_element_type=F32)


def _dot_tn(a, b):
    return lax.dot_general(a, b, (((0,), (0,)), ((), ())), preferred_element_type=F32)


def _mm_split(t, m):
    hi = t.astype(BF16)
    lo = (t - hi.astype(F32)).astype(BF16)
    return _dot(hi, m) + _dot(lo, m)


def _params(**kw):
    return pltpu.CompilerParams(vmem_limit_bytes=VMEM_LIMIT, **kw)


def _full(shape):
    nd = len(shape)
    return pl.BlockSpec(shape, lambda *_: (0,) * nd)


def _view_shape(s_len, r, dtype):
    return jax.ShapeDtypeStruct((s_len // r, r * D_GRP), dtype)


def _view_spec(tm, r):
    return pl.BlockSpec((tm // r, r * D_GRP), lambda i: (i, 0))


def _swap_halves(t):
    lane = lax.broadcasted_iota(jnp.int32, t.shape, 1)
    first = (lane & 32) == 0
    return jnp.where(first, pltpu.roll(t, 96, 1), pltpu.roll(t, 32, 1))


def _log_sigmoid_pair(z):
    neg_abs = lax.bitcast_convert_type(lax.bitcast_convert_type(z, jnp.uint32) | jnp.uint32(0x80000000), F32)
    lb = jnp.minimum(z, 0.0) - jnp.log(1.0 + jnp.exp(neg_abs))
    return lb, lb - z


def _cumsum_mm(t, tri):
    return _dot(t.astype(BF16), tri)


def _split_views(src_ref, stage_ref, views4, views16):
    slabs, n, _ = src_ref.shape
    n4, n16 = n // 4, n // 16
    for j in range(slabs):
        g, lanes = j // 4, 128 * (j % 4)
        src, stage = src_ref.at[j], stage_ref.at[j]
        for c4 in range(4):
            blk = src[pl.ds(c4, n4, stride=4), :]
            stage[n4 * c4:n4 * (c4 + 1), :] = blk
            col = D_GRP * c4 + lanes
            views4[g][:, col:col + 128] = blk.astype(views4[g].dtype)
        for c4 in range(4):
            for c1 in range(4):
                blk = stage[pl.ds(n4 * c4 + c1, n16, stride=4), :]
                col = D_GRP * (4 * c1 + c4) + lanes
                views16[g][:, col:col + 128] = blk.astype(views16[g].dtype)


def _merge_views(views4, views16, stage_ref, dst4_ref, dst16_ref):
    slabs, n, _ = dst4_ref.shape
    n4, n16 = n // 4, n // 16
    for j in range(slabs):
        g, lanes = j // 4, 128 * (j % 4)
        dst4, dst16, stage = dst4_ref.at[j], dst16_ref.at[j], stage_ref.at[j]
        for c4 in range(4):
            col = D_GRP * c4 + lanes
            dst4[pl.ds(c4, n4, stride=4), :] = views4[g][:, col:col + 128].astype(F32)
            for c1 in range(4):
                col = D_GRP * (4 * c1 + c4) + lanes
                stage[pl.ds(n4 * c4 + c1, n16, stride=4), :] = views16[g][:, col:col + 128].astype(F32)
        for c4 in range(4):
            dst16[pl.ds(c4, n4, stride=4), :] = stage[n4 * c4:n4 * (c4 + 1), :]


def _slab_group(ref, g):
    return jnp.concatenate([ref[4 * g + p] for p in range(4)], axis=1)


def _mesh_pos():
    return lax.axis_index("x"), lax.axis_index("y"), lax.axis_index("c")


def _flat_index(p):
    return 4 * p[0] + 2 * p[1] + p[2]


def _gather_weights(shards):
    n_arr = len(shards)

    def body(*refs):
        srcs, outs = refs[:n_arr], refs[n_arr:2 * n_arr]
        send_sems, recv_sems, local_sems = refs[2 * n_arr:]
        x, y, c = _mesh_pos()
        me, sibling = (x, y, c), (x, y, 1 - c)
        chips = [(1 - x, y), (x, 1 - y), (1 - x, 1 - y)]

        def copy(arr, k, block, to, own=False):
            dst = outs[arr].at[_flat_index(block)]
            return pltpu.make_async_remote_copy(
                src_ref=srcs[arr] if own else dst, dst_ref=dst,
                send_sem=send_sems.at[arr, k], recv_sem=recv_sems.at[arr, k],
                device_id=to, device_id_type=MESH)

        for arr in range(n_arr):
            mine = pltpu.make_async_copy(srcs[arr], outs[arr].at[_flat_index(me)], local_sems.at[arr])
            mine.start()
            first = [copy(arr, 0, me, sibling, own=True)]
            first += [copy(arr, 1 + j, me, (*chip, c), own=True) for j, chip in enumerate(chips)]
            for cp in first:
                cp.start()
        for arr in range(n_arr):
            passed = [copy(arr, 4 + j, (*chip, c), sibling) for j, chip in enumerate(chips)]
            for j, chip in enumerate(chips):
                copy(arr, 1 + j, (*chip, c), me).wait_recv()
                passed[j].start()
        for arr in range(n_arr):
            copy(arr, 0, sibling, me).wait_recv()
            for j, chip in enumerate(chips):
                copy(arr, 4 + j, (*chip, 1 - c), me).wait_recv()
            for k in range(7):
                copy(arr, k, me, me).wait_send()
            pltpu.make_async_copy(srcs[arr], outs[arr].at[_flat_index(me)], local_sems.at[arr]).wait()

    any_spec = pl.BlockSpec(memory_space=pl.ANY)
    return pl.pallas_call(
        body, name="gather_weights",
        out_shape=tuple(jax.ShapeDtypeStruct((N_DEV,) + s.shape, s.dtype) for s in shards),
        in_specs=[any_spec] * n_arr, out_specs=(any_spec,) * n_arr,
        scratch_shapes=[pltpu.SemaphoreType.DMA((n_arr, 7)), pltpu.SemaphoreType.DMA((n_arr, 7)),
                        pltpu.SemaphoreType.DMA((n_arr,))],
        compiler_params=pltpu.CompilerParams(has_side_effects=True),
    )(*shards)


def _peer_list(x, y, c):
    return [(1 - x if m & 4 else x, 1 - y if m & 2 else y, 1 - c if m & 1 else c) for m in range(1, N_DEV)]


def _exchange_grads(parts, small):
    n_arr = len(parts)

    def body(*refs):
        ins, outs = refs[:n_arr + 1], refs[n_arr + 1:2 * (n_arr + 1)]
        send_sems, recv_sems, local_sems = refs[2 * (n_arr + 1):]
        x, y, c = _mesh_pos()
        me = (x, y, c)
        my_idx = _flat_index(me)
        peers = []
        for m in range(1, N_DEV):
            peers.append((1 - x if m & 4 else x, 1 - y if m & 2 else y, 1 - c if m & 1 else c))

        def src_block(arr, dev):
            return ins[arr] if arr == n_arr else ins[arr].at[_flat_index(dev)]

        def copy(arr, k):
            return pltpu.make_async_remote_copy(
                src_ref=src_block(arr, peers[k]), dst_ref=outs[arr].at[my_idx],
                send_sem=send_sems.at[arr, k], recv_sem=recv_sems.at[arr, k],
                device_id=peers[k], device_id_type=MESH)

        def local(arr):
            return pltpu.make_async_copy(src_block(arr, me), outs[arr].at[my_idx], local_sems.at[arr])

        for arr in range(n_arr + 1):
            local(arr).start()
            for k in range(N_DEV - 1):
                copy(arr, k).start()
        for arr in range(n_arr + 1):
            for k in range(N_DEV - 1):
                cp = copy(arr, k)
                cp.wait_send()
                cp.wait_recv()
            local(arr).wait()

    any_spec = pl.BlockSpec(memory_space=pl.ANY)
    out_shape = tuple(jax.ShapeDtypeStruct(p.shape, p.dtype) for p in parts)
    out_shape += (jax.ShapeDtypeStruct((N_DEV,) + small.shape, small.dtype),)
    return pl.pallas_call(
        body, name="exchange_grads",
        out_shape=out_shape,
        in_specs=[any_spec] * (n_arr + 1), out_specs=(any_spec,) * (n_arr + 1),
        scratch_shapes=[pltpu.SemaphoreType.DMA((n_arr + 1, N_DEV - 1)),
                        pltpu.SemaphoreType.DMA((n_arr + 1, N_DEV - 1)),
                        pltpu.SemaphoreType.DMA((n_arr + 1,))],
        compiler_params=pltpu.CompilerParams(has_side_effects=True),
    )(*parts, small)


def _call_with_gather(body, shards, first_step, mid_step, last_step, *, name, grid, in_specs, out_specs,
                      out_shape, scratch_shapes=()):
    out_specs = tuple(out_specs) if isinstance(out_specs, (tuple, list)) else (out_specs,)
    out_shape = tuple(out_shape) if isinstance(out_shape, (tuple, list)) else (out_shape,)
    n_in, n_out, n_scr, n = len(in_specs), len(out_specs), len(scratch_shapes), len(shards)
    if n == 0:
        return pl.pallas_call(body, name=name, grid=grid, in_specs=list(in_specs), out_specs=out_specs,
                              out_shape=out_shape, scratch_shapes=list(scratch_shapes),
                              compiler_params=_params())

    def full_body(*refs):
        ins, srcs = refs[:n_in], refs[n_in:n_in + n]
        outs, lands = refs[n_in + n:n_in + n + n_out], refs[n_in + n + n_out:n_in + 2 * n + n_out]
        scratch = refs[n_in + 2 * n + n_out:n_in + 2 * n + n_out + n_scr]
        send_sems, recv_sems, local_sems = refs[-3:]
        x, y, c = _mesh_pos()
        me, sibling = (x, y, c), (x, y, 1 - c)
        chips = [(1 - x, y), (x, 1 - y), (1 - x, 1 - y)]

        def copy(a, k, block, to, own=False):
            dst = lands[a].at[_flat_index(block)]
            return pltpu.make_async_remote_copy(
                src_ref=srcs[a] if own else dst, dst_ref=dst,
                send_sem=send_sems.at[a, k], recv_sem=recv_sems.at[a, k],
                device_id=to, device_id_type=MESH)

        def local(a):
            return pltpu.make_async_copy(srcs[a], lands[a].at[_flat_index(me)], local_sems.at[a])

        @pl.when(first_step())
        def _():
            for a in range(n):
                local(a).start()
                copy(a, 0, me, sibling, own=True).start()
                for j, chip in enumerate(chips):
                    copy(a, 1 + j, me, (*chip, c), own=True).start()

        @pl.when(mid_step())
        def _():
            for a in range(n):
                for j, chip in enumerate(chips):
                    copy(a, 1 + j, (*chip, c), me).wait_recv()
                    copy(a, 4 + j, (*chip, c), sibling).start()

        body(*ins, *outs, *scratch)

        @pl.when(last_step())
        def _():
            for a in range(n):
                copy(a, 0, sibling, me).wait_recv()
                for j, chip in enumerate(chips):
                    copy(a, 4 + j, (*chip, 1 - c), me).wait_recv()
                for k in range(N_DEV - 1):
                    copy(a, k, me, me).wait_send()
                local(a).wait()

    any_spec = pl.BlockSpec(memory_space=pl.ANY)
    return pl.pallas_call(
        full_body, name=name, grid=grid,
        in_specs=list(in_specs) + [any_spec] * n,
        out_specs=out_specs + (any_spec,) * n,
        out_shape=out_shape + tuple(jax.ShapeDtypeStruct((N_DEV,) + t.shape, t.dtype) for t in shards),
        scratch_shapes=list(scratch_shapes) + [pltpu.SemaphoreType.DMA((n, N_DEV - 1)),
                                               pltpu.SemaphoreType.DMA((n, N_DEV - 1)),
                                               pltpu.SemaphoreType.DMA((n,))],
        compiler_params=_params(has_side_effects=True),
    )


def _call_with_exchange(body, rides, first_step, last_step, *, name, grid, in_specs, out_specs, out_shape,
                        scratch_shapes=()):
    out_specs = tuple(out_specs) if isinstance(out_specs, (tuple, list)) else (out_specs,)
    out_shape = tuple(out_shape) if isinstance(out_shape, (tuple, list)) else (out_shape,)
    n_in, n_out, n_scr, n = len(in_specs), len(out_specs), len(scratch_shapes), len(rides)
    if n == 0:
        return pl.pallas_call(body, name=name, grid=grid, in_specs=list(in_specs), out_specs=out_specs,
                              out_shape=out_shape, scratch_shapes=list(scratch_shapes),
                              compiler_params=_params())

    def full_body(*refs):
        ins, srcs = refs[:n_in], refs[n_in:n_in + n]
        outs, lands = refs[n_in + n:n_in + n + n_out], refs[n_in + n + n_out:n_in + 2 * n + n_out]
        scratch = refs[n_in + 2 * n + n_out:n_in + 2 * n + n_out + n_scr]
        send_sems, recv_sems, local_sems = refs[-3:]
        x, y, c = _mesh_pos()
        my_idx = _flat_index((x, y, c))
        peers = _peer_list(x, y, c)

        def remote(a, k):
            return pltpu.make_async_remote_copy(
                src_ref=srcs[a].at[_flat_index(peers[k])], dst_ref=lands[a].at[my_idx],
                send_sem=send_sems.at[a, k], recv_sem=recv_sems.at[a, k],
                device_id=peers[k], device_id_type=MESH)

        def local(a):
            return pltpu.make_async_copy(srcs[a].at[my_idx], lands[a].at[my_idx], local_sems.at[a])

        @pl.when(first_step())
        def _():
            for a in range(n):
                local(a).start()
                for k in range(N_DEV - 1):
                    remote(a, k).start()

        body(*ins, *outs, *scratch)

        @pl.when(last_step())
        def _():
            for a in range(n):
                for k in range(N_DEV - 1):
                    cp = remote(a, k)
                    cp.wait_send()
                    cp.wait_recv()
                local(a).wait()

    any_spec = pl.BlockSpec(memory_space=pl.ANY)
    res = pl.pallas_call(
        full_body, name=name, grid=grid,
        in_specs=list(in_specs) + [any_spec] * n,
        out_specs=out_specs + (any_spec,) * n,
        out_shape=out_shape + tuple(jax.ShapeDtypeStruct(t.shape, t.dtype) for t in rides),
        scratch_shapes=list(scratch_shapes) + [pltpu.SemaphoreType.DMA((n, N_DEV - 1)),
                                               pltpu.SemaphoreType.DMA((n, N_DEV - 1)),
                                               pltpu.SemaphoreType.DMA((n,))],
        compiler_params=_params(has_side_effects=True),
    )
    return res


def _head_norm(t, w128, bd):
    ms = _mm_split(t * t, bd) * (1.0 / HEAD_DIM)
    r = lax.rsqrt(ms + EPS)
    return (t * r) * w128, r


def _attn_in(x2, wn1, a_g, cos2, sin2, qnw, knw, bd, shards):
    s_len = x2.shape[0]
    tm = ROW_TILE

    def body(x_ref, wn_ref, w_ref, cos_ref, sin_ref, qnw_ref, knw_ref, bd_ref,
             h1_ref, qraw_ref, kraw_ref, q_ref, k_ref, va_ref, qs_ref, ks_ref, vs_ref,
             q4_ref, k4_ref, v4_ref, q16_ref, k16_ref, v16_ref, proj, slabs, stage, w_full):
        @pl.when(pl.program_id(0) == 0)
        def _():
            for d in range(N_DEV):
                w_full[:, IN_SHARD * d:IN_SHARD * (d + 1)] = w_ref[d]

        xx = x_ref[...]
        r = lax.rsqrt(jnp.mean(xx * xx, axis=-1, keepdims=True) + EPS)
        h = ((xx * r) * wn_ref[...]).astype(BF16)
        h1_ref[...] = h
        proj[...] = _dot(h, w_full[...])
        cos_t, sin_t, bdm = cos_ref[...], sin_ref[...], bd_ref[...]
        for grp, (raw_ref, rope_ref, nw_ref) in enumerate(((qraw_ref, q_ref, qnw_ref),
                                                           (kraw_ref, k_ref, knw_ref))):
            for p in range(4):
                cols = slice(D_GRP * grp + 128 * p, D_GRP * grp + 128 * (p + 1))
                t = proj[:, cols]
                raw_ref[:, 128 * p:128 * (p + 1)] = t
                yn, _ = _head_norm(t, nw_ref[...], bdm)
                roped = yn * cos_t + _swap_halves(yn) * sin_t
                slabs[4 * grp + p] = roped
                rope_ref[:, 128 * p:128 * (p + 1)] = roped.astype(BF16)
        for p in range(4):
            slabs[8 + p] = proj[:, 2 * D_GRP + 128 * p:2 * D_GRP + 128 * (p + 1)]
        for grp, ref in ((2, va_ref), (3, qs_ref), (4, ks_ref), (5, vs_ref)):
            ref[...] = proj[:, D_GRP * grp:D_GRP * (grp + 1)].astype(BF16)
        _split_views(slabs, stage, (q4_ref, k4_ref, v4_ref), (q16_ref, k16_ref, v16_ref))

    row = lambda w: pl.BlockSpec((tm, w), lambda i: (i, 0))
    grp_bf = jax.ShapeDtypeStruct((s_len, D_GRP), BF16)
    grp_f32 = jax.ShapeDtypeStruct((s_len, D_GRP), F32)
    ni = s_len // tm
    return _call_with_gather(
        body, shards, lambda: pl.program_id(0) == 0, lambda: pl.program_id(0) == ni - 2,
        lambda: pl.program_id(0) == ni - 1,
        name="attn_in", grid=(ni,),
        in_specs=[row(D_MODEL), _full((1, D_MODEL)),
                  pl.BlockSpec((N_DEV, D_MODEL, IN_SHARD), lambda i: (0, 0, 0)),
                  row(128), row(128), _full((1, 128)), _full((1, 128)), _full((128, 128))],
        out_specs=(row(D_MODEL),) + (row(D_GRP),) * 8 + (_view_spec(tm, 4),) * 3 + (_view_spec(tm, 16),) * 3,
        out_shape=(jax.ShapeDtypeStruct((s_len, D_MODEL), BF16), grp_f32, grp_f32) + (grp_bf,) * 6
        + (_view_shape(s_len, 4, BF16),) * 3 + (_view_shape(s_len, 16, BF16),) * 3,
        scratch_shapes=[pltpu.VMEM((tm, D_IN), F32), pltpu.VMEM((12, tm, 128), F32), pltpu.VMEM((12, tm, 128), F32),
                        pltpu.VMEM((D_MODEL, D_IN), BF16)],
    )(x2, wn1, a_g, cos2, sin2, qnw, knw, bd, *shards)


def _band_mask(n):
    i = lax.broadcasted_iota(jnp.int32, (2 * BLOCK, 2 * BLOCK), 0) & (BLOCK - 1)
    j = lax.broadcasted_iota(jnp.int32, (2 * BLOCK, 2 * BLOCK), 1)
    dist = i + BLOCK - j
    return (dist >= 0) & (dist <= BLOCK) & ((n - 1) * BLOCK + j >= 0)


def _stack_heads(t2, head0):
    return jnp.concatenate([jnp.where(head0, t2, 0), jnp.where(head0, 0, t2)], axis=0)


def _unstack_heads(t, head0):
    return jnp.where(head0, t[0:BLOCK], t[BLOCK:2 * BLOCK])


def _dil_fwd(qv, kv, vv, r, shards):
    sub_len = qv.shape[0]
    nb = sub_len // BLOCK

    qb = 2 if nb % 2 == 0 else 1

    def body(q_ref, kp_ref, kc_ref, vp_ref, vc_ref, o_ref, lse_ref):
        n = pl.program_id(1)
        lane = lax.broadcasted_iota(jnp.int32, (BLOCK, 128), 1)
        head0 = lane < HEAD_DIM
        units = [(b, slice(128 * p, 128 * (p + 1))) for b in range(qb) for p in range(4)]
        valid = [_band_mask(qb * n + b) for b in range(qb)]
        rows = [slice(BLOCK * b, BLOCK * (b + 1)) for b in range(qb)]

        def keys(prev_ref, cur_ref, b, c):
            before = prev_ref[:, c] if b == 0 else cur_ref[rows[b - 1], c]
            return jnp.concatenate([before, cur_ref[rows[b], c]], axis=0)

        qqs = [_stack_heads(q_ref[rows[b], c] * ATT_SCALE, head0) for b, c in units]
        kks = [keys(kp_ref, kc_ref, b, c) for b, c in units]
        vvs = [keys(vp_ref, vc_ref, b, c) for b, c in units]
        ss = [_dot_nt(qq, kk) for qq, kk in zip(qqs, kks)]
        prs, dens, lses = [], [], []
        for (b, _), s in zip(units, ss):
            s = jnp.where(valid[b], s, NEG)
            m = jnp.max(s, axis=-1, keepdims=True)
            pr = jnp.exp(s - m)
            den = jnp.sum(pr, axis=-1, keepdims=True)
            prs.append(pr.astype(BF16))
            dens.append(den)
            lses.append(m + jnp.log(den))
        pvs = [_dot(pr, vv2) for pr, vv2 in zip(prs, vvs)]
        for (b, c), pv, den, lse in zip(units, pvs, dens, lses):
            o_ref[rows[b], c] = _unstack_heads(pv / den, head0)
            lse_ref[rows[b], c] = _unstack_heads(jnp.broadcast_to(lse, (2 * BLOCK, 128)), head0)

    cur = pl.BlockSpec((qb * BLOCK, D_GRP), lambda c, n: (n, c))
    prev = pl.BlockSpec((BLOCK, D_GRP), lambda c, n: (jnp.maximum(qb * n - 1, 0), c))
    out = jax.ShapeDtypeStruct(qv.shape, F32)
    steps = nb // qb

    def at(t):
        return lambda: pl.program_id(0) * steps + pl.program_id(1) == t

    return _call_with_gather(
        body, shards, at(0), at((2 * r * steps) // 3), at(r * steps - 1),
        name=f"dil_fwd_r{r}", grid=(r, steps),
        in_specs=[cur, prev, cur, prev, cur], out_specs=(cur, cur), out_shape=(out, out),
    )(qv, kv, kv, vv, vv, *shards)


def _dil_bwd(qv, kv, vv, dov, lsev, deltav, r, rides):
    sub_len = qv.shape[0]
    nb = sub_len // BLOCK

    def body(q_ref, kp_ref, kc_ref, vp_ref, vc_ref, do_ref, lse_ref, dl_ref,
             dq_ref, dk_ref, dv_ref, dk_carry, dv_carry):
        n = pl.program_id(1)

        @pl.when(n == 0)
        def _():
            dk_carry[...] = jnp.zeros_like(dk_carry)
            dv_carry[...] = jnp.zeros_like(dv_carry)

        @pl.when(n < nb)
        def _():
            valid = _band_mask(n)
            lane = lax.broadcasted_iota(jnp.int32, (BLOCK, 128), 1)
            head0 = lane < HEAD_DIM
            pairs = [slice(128 * p, 128 * (p + 1)) for p in range(4)]
            qqs = [_stack_heads(q_ref[:, c] * ATT_SCALE, head0) for c in pairs]
            dos = [_stack_heads(do_ref[:, c], head0) for c in pairs]
            kks = [jnp.concatenate([kp_ref[:, c], kc_ref[:, c]], axis=0) for c in pairs]
            vvs = [jnp.concatenate([vp_ref[:, c], vc_ref[:, c]], axis=0) for c in pairs]
            ss = [_dot_nt(qq, kk) for qq, kk in zip(qqs, kks)]
            dps = [_dot_nt(do, vv2) for do, vv2 in zip(dos, vvs)]
            def softmax_terms(p):
                stats = []
                for ref in (lse_ref, dl_ref):
                    t2 = ref[:, pairs[p]]
                    stats.append(jnp.concatenate(
                        [jnp.sum(jnp.where(lane == 0, t2, 0.0), axis=-1, keepdims=True),
                         jnp.sum(jnp.where(lane == HEAD_DIM, t2, 0.0), axis=-1, keepdims=True)], axis=0))
                pr = jnp.where(valid, jnp.exp(jnp.minimum(ss[p] - stats[0], 0.0)), 0.0)
                return pr.astype(BF16), (pr * (dps[p] - stats[1])).astype(BF16)

            terms = [softmax_terms(p) for p in range(4)]
            dqs = [_dot(terms[p][1], kks[p]) for p in range(4)]
            dkks = [_dot_tn(terms[p][1], qqs[p]) for p in range(4)]
            dvvs = [_dot_tn(terms[p][0], dos[p]) for p in range(4)]
            for c, dq, dkk, dvv in zip(pairs, dqs, dkks, dvvs):
                dq_ref[:, c] = _unstack_heads(dq, head0) * ATT_SCALE
                dk_ref[:, c] = dk_carry[:, c] + dkk[:BLOCK]
                dv_ref[:, c] = dv_carry[:, c] + dvv[:BLOCK]
                dk_carry[:, c] = dkk[BLOCK:]
                dv_carry[:, c] = dvv[BLOCK:]

        @pl.when(n == nb)
        def _():
            dk_ref[...] = dk_carry[...]
            dv_ref[...] = dv_carry[...]

    last = nb - 1
    cur = pl.BlockSpec((BLOCK, D_GRP), lambda c, n: (jnp.minimum(n, last), c))
    prev = pl.BlockSpec((BLOCK, D_GRP), lambda c, n: (jnp.clip(n - 1, 0, last), c))
    out = jax.ShapeDtypeStruct(qv.shape, F32)
    return _call_with_exchange(
        body, rides,
        lambda: jnp.logical_and(pl.program_id(0) == 0, pl.program_id(1) == 0),
        lambda: jnp.logical_and(pl.program_id(0) == r - 1, pl.program_id(1) == nb),
        name=f"dil_bwd_r{r}", grid=(r, nb + 1),
        in_specs=[cur, prev, cur, prev, cur, cur, cur, cur],
        out_specs=(cur, prev, prev), out_shape=(out, out, out),
        scratch_shapes=[pltpu.VMEM((BLOCK, D_GRP), F32), pltpu.VMEM((BLOCK, D_GRP), F32)],
    )(qv, kv, kv, vv, vv, dov, lsev, deltav, *rides)


def _sb_fwd(qs, ks, vs, tri_suf, shards):
    s_len = qs.shape[0]
    t = SB_TILE
    nq = s_len // t

    npair = SB_PAIRS

    def body(q_ref, k_ref, v_ref, u_ref, o_ref, c_ref, qq, vt, acc, cf, csave):
        row = lax.broadcasted_iota(jnp.int32, (2 * t, t), 0) & (t - 1)
        col = lax.broadcasted_iota(jnp.int32, (2 * t, t), 1)
        diag_mask = col < row
        lane1 = lax.broadcasted_iota(jnp.int32, (t, 128), 1)
        head0 = lane1 < HEAD_DIM
        lane2 = lax.broadcasted_iota(jnp.int32, (2 * t, 128), 1)
        uu = u_ref[...]
        pr = range(npair)
        cols = [slice(128 * pp, 128 * (pp + 1)) for pp in pr]

        i = pl.program_id(1)

        @pl.when(i == 0)
        def _():
            def transpose_v(j, _):
                rows = pl.ds(pl.multiple_of(j * t, t), t)
                for pp in pr:
                    vt[pp, j] = v_ref[rows, cols[pp]].astype(F32).T.astype(BF16)
                return 0

            lax.fori_loop(0, nq, transpose_v, 0)

        for pp in pr:
            q2 = q_ref[:, cols[pp]] * ATT_SCALE
            qq[pp, 0:t, :] = jnp.where(head0, q2, 0)
            qq[pp, t:2 * t, :] = jnp.where(head0, 0, q2)
        acc[...] = jnp.zeros_like(acc)
        cf[...] = jnp.zeros_like(cf)
        csave[...] = jnp.full(csave.shape, 2.0 * SB_DEAD, F32)

        def tile(kb, diag):
            krows = pl.ds(pl.multiple_of(kb * t, t), t)
            zs = [_dot_nt(qq[pp], k_ref[krows, cols[pp]]) for pp in pr]
            lbk = [_log_sigmoid_pair(z) for z in zs]
            lks = [jnp.where(diag_mask, lk, 0.0) if diag else lk for _, lk in lbk]
            sufs = [_cumsum_mm(lk, uu) for lk in lks]
            carries = [cf[pp] for pp in pr]
            avs = []
            for pp in pr:
                a = jnp.exp(lbk[pp][0] + (sufs[pp] + jnp.concatenate([carries[pp]] * (t // 128), axis=1)))
                avs.append((jnp.where(diag_mask, a, 0.0) if diag else a).astype(BF16))
            pvs = [_dot_nt(vt[pp, kb], avs[pp]) for pp in pr]
            for pp in pr:
                acc[pp] += pvs[pp]
                csave[pp] = jnp.where(lane2 == kb, carries[pp], csave[pp])
                cf[pp] = carries[pp] + jnp.broadcast_to(jnp.sum(lks[pp], axis=-1, keepdims=True), (2 * t, 128))

        tile(i, True)

        def alive():
            return jnp.max(cf[...]) > SB_DEAD

        def k_block(state):
            kb, _ = state
            tile(kb, False)
            return kb - 1, alive()

        lax.while_loop(lambda state: jnp.logical_and(state[0] >= 0, state[1]), k_block, (i - 1, alive()))
        for pp in pr:
            o_ref[:, cols[pp]] = jnp.where(head0, acc[pp, :, 0:t].T, acc[pp, :, t:2 * t].T)
            c_ref[2 * pp] = csave[pp, 0:t, :]
            c_ref[2 * pp + 1] = csave[pp, t:2 * t, :]

    width = 128 * npair
    kv = pl.BlockSpec((s_len, width), lambda p, i: (0, p))
    qo = pl.BlockSpec((t, width), lambda p, i: (i, p))
    steps = 4 // npair

    def at(p, i):
        return lambda: jnp.logical_and(pl.program_id(0) == p, pl.program_id(1) == i)

    return _call_with_gather(
        body, shards, at(0, 0), at(steps - 1, (2 * nq) // 3), at(steps - 1, nq - 1),
        name="sb_fwd", grid=(steps, nq),
        in_specs=[qo, kv, kv, pl.BlockSpec((t, t), lambda p, i: (0, 0))],
        out_specs=(qo, pl.BlockSpec((2 * npair, t, 128), lambda p, i: (p, i, 0))),
        out_shape=(jax.ShapeDtypeStruct((s_len, D_GRP), F32),
                   jax.ShapeDtypeStruct((8, s_len, 128), F32)),
        scratch_shapes=[pltpu.VMEM((npair, 2 * t, 128), BF16), pltpu.VMEM((npair, nq, 128, t), BF16),
                        pltpu.VMEM((npair, 128, 2 * t), F32),
                        pltpu.VMEM((npair, 2 * t, 128), F32), pltpu.VMEM((npair, 2 * t, 128), F32)],
    )(qs, ks, vs, tri_suf, *shards)


def _sb_bwd(qs, ks, vs, dos, csaved, tri_suf, tri_pre, rides):
    s_len = qs.shape[0]
    t = SB_TILE
    nq = s_len // t

    npair = SB_BWD_PAIRS

    def body(q_ref, k_ref, v_ref, do_ref, c_ref, u_ref, p_ref, dq_ref, dk_ref, dv_ref,
             qq, dd, qqt, ddt, kt, dq_acc, dkt, dvt, cg):
        row = lax.broadcasted_iota(jnp.int32, (2 * t, t), 0) & (t - 1)
        col = lax.broadcasted_iota(jnp.int32, (2 * t, t), 1)
        diag_mask = col < row
        lane1 = lax.broadcasted_iota(jnp.int32, (t, 128), 1)
        head0 = lane1 < HEAD_DIM
        lane2 = lax.broadcasted_iota(jnp.int32, (2 * t, 128), 1)
        uu, pm = u_ref[...], p_ref[...]
        pr = range(npair)
        cols = [slice(128 * pp, 128 * (pp + 1)) for pp in pr]
        i = pl.program_id(1)

        @pl.when(i == 0)
        def _():
            dkt[...] = jnp.zeros_like(dkt)
            dvt[...] = jnp.zeros_like(dvt)

            def transpose_k(j, _):
                rows = pl.ds(pl.multiple_of(j * t, t), t)
                for pp in pr:
                    kt[pp, j] = k_ref[rows, cols[pp]].astype(F32).T.astype(BF16)
                return 0

            lax.fori_loop(0, nq, transpose_k, 0)

        for pp in pr:
            q2 = q_ref[:, cols[pp]].astype(F32) * ATT_SCALE
            do2 = do_ref[:, cols[pp]].astype(F32)
            for src, nat, tr in ((q2, qq, qqt), (do2, dd, ddt)):
                stacked = jnp.concatenate([jnp.where(head0, src, 0.0), jnp.where(head0, 0.0, src)], axis=0)
                nat[pp] = stacked.astype(BF16)
                tr[pp] = stacked.T.astype(BF16)
        dq_acc[...] = jnp.zeros_like(dq_acc)
        cg[...] = jnp.zeros_like(cg)

        def tile(kb, diag):
            krows = pl.ds(pl.multiple_of(kb * t, t), t)
            zs = [_dot_nt(qq[pp], k_ref[krows, cols[pp]]) for pp in pr]
            das = [_dot_nt(dd[pp], v_ref[krows, cols[pp]]) for pp in pr]
            lbk = [_log_sigmoid_pair(z) for z in zs]
            lks = [jnp.where(diag_mask, lk, 0.0) if diag else lk for _, lk in lbk]
            sufs = [_cumsum_mm(lk, uu) for lk in lks]
            avs, gs = [], []
            for pp in pr:
                cs = jnp.concatenate([c_ref[2 * pp], c_ref[2 * pp + 1]], axis=0)
                cf = jnp.sum(jnp.where(lane2 == kb, cs, 0.0), axis=-1, keepdims=True)
                a = jnp.exp(lbk[pp][0] + (sufs[pp] + cf))
                a = jnp.where(diag_mask, a, 0.0) if diag else a
                avs.append(a.astype(BF16))
                gs.append(a * das[pp])
            gpres = [_cumsum_mm(g, pm) for g in gs]
            dzs = []
            for pp in pr:
                carry = cg[pp]
                beta = jnp.exp(lbk[pp][0])
                dz = gs[pp] - beta * (gs[pp] + (gpres[pp] + jnp.concatenate([carry] * (t // 128), axis=1)))
                dzs.append((jnp.where(diag_mask, dz, 0.0) if diag else dz).astype(BF16))
                cg[pp] = carry + jnp.broadcast_to(jnp.sum(gs[pp], axis=-1, keepdims=True), (2 * t, 128))
            dqs = [_dot_nt(kt[pp, kb], dzs[pp]) for pp in pr]
            dks = [_dot(qqt[pp], dzs[pp]) for pp in pr]
            dvs = [_dot(ddt[pp], avs[pp]) for pp in pr]
            for pp in pr:
                dq_acc[pp] += dqs[pp]
                dkt[pp, kb] += dks[pp]
                dvt[pp, kb] += dvs[pp]

        def k_block(kb, _):
            tile(kb, False)
            return 0

        col_max = jnp.max(jnp.max(c_ref[...], axis=0), axis=0, keepdims=True)
        lane_row = lax.broadcasted_iota(jnp.int32, (1, 128), 1)
        n_live = jnp.sum(jnp.where(jnp.logical_and(col_max > SB_DEAD, lane_row < i), 1, 0))
        lax.fori_loop(i - n_live, i, k_block, 0)
        tile(i, True)
        for pp in pr:
            dq_ref[:, cols[pp]] = jnp.where(head0, dq_acc[pp, :, 0:t].T, dq_acc[pp, :, t:2 * t].T) * ATT_SCALE

        @pl.when(i == nq - 1)
        def _():
            def untranspose(j, _):
                rows = pl.ds(pl.multiple_of(j * t, t), t)
                for pp in pr:
                    dk_ref[rows, cols[pp]] = dkt[pp, j].T
                    dv_ref[rows, cols[pp]] = dvt[pp, j].T
                return 0

            lax.fori_loop(0, nq, untranspose, 0)

    width = 128 * npair
    kv = pl.BlockSpec((s_len, width), lambda p, i: (0, p))
    qo = pl.BlockSpec((t, width), lambda p, i: (i, p))
    tri = pl.BlockSpec((t, t), lambda p, i: (0, 0))
    out = jax.ShapeDtypeStruct((s_len, D_GRP), F32)
    steps = 4 // npair
    return _call_with_exchange(
        body, rides,
        lambda: jnp.logical_and(pl.program_id(0) == 0, pl.program_id(1) == 0),
        lambda: jnp.logical_and(pl.program_id(0) == steps - 1, pl.program_id(1) == nq - 1),
        name="sb_bwd", grid=(steps, nq),
        in_specs=[qo, kv, kv, qo, pl.BlockSpec((2 * npair, t, 128), lambda p, i: (p, i, 0)), tri, tri],
        out_specs=(qo, kv, kv), out_shape=(out, out, out),
        scratch_shapes=[pltpu.VMEM((npair, 2 * t, 128), BF16), pltpu.VMEM((npair, 2 * t, 128), BF16),
                        pltpu.VMEM((npair, 128, 2 * t), BF16), pltpu.VMEM((npair, 128, 2 * t), BF16),
                        pltpu.VMEM((npair, nq, 128, t), BF16),
                        pltpu.VMEM((npair, 128, 2 * t), F32),
                        pltpu.VMEM((npair, nq, 128, t), F32), pltpu.VMEM((npair, nq, 128, t), F32),
                        pltpu.VMEM((npair, 2 * t, 128), F32)],
    )(qs, ks, vs, dos, csaved, tri_suf, tri_pre, *rides)


def _attn_out(o_b, lse_b, o_sb, x2, wdil, wsb, out_g, shards):
    s_len = x2.shape[0]
    tm = ROW_TILE

    def body(o1_ref, l1_ref, o4_ref, l4_ref, o16_ref, l16_ref, osb_ref, x_ref, wdil_ref, wsb_ref, w_ref,
             odil_ref, lse_ref, lse4_ref, lse16_ref, mixed_ref, x1_ref, stage, nat4, nat16):
        _merge_views((o4_ref, l4_ref), (o16_ref, l16_ref), stage, nat4, nat16)
        os_ = (o1_ref[...], _slab_group(nat4, 0), _slab_group(nat16, 0))
        ls = (l1_ref[...], _slab_group(nat4, 1), _slab_group(nat16, 1))
        mx = jnp.maximum(jnp.maximum(ls[0], ls[1]), ls[2])
        es = [jnp.exp(l - mx) for l in ls]
        den = es[0] + es[1] + es[2]
        o_dil = (es[0] * os_[0] + es[1] * os_[1] + es[2] * os_[2]) / den
        odil_ref[...] = o_dil
        lse = mx + jnp.log(den)
        lse_ref[...] = lse
        for p in range(4):
            nat4[p] = lse[:, 128 * p:128 * (p + 1)]
        _split_views(nat4.at[0:4], stage.at[0:4], (lse4_ref,), (lse16_ref,))
        halves = []
        for t, w_r in ((o_dil, wdil_ref), (osb_ref[...], wsb_ref)):
            r = lax.rsqrt(jnp.mean(t * t, axis=-1, keepdims=True) + EPS)
            halves.append(((t * r) * w_r[...]).astype(BF16))
        mixed = jnp.concatenate(halves, axis=1)
        mixed_ref[...] = mixed
        w = w_ref[...].reshape(D_MODEL, D_MODEL)
        x1_ref[...] = x_ref[...] + _dot(mixed, w)

    row = lambda w: pl.BlockSpec((tm, w), lambda i: (i, 0))
    ni = s_len // tm
    return _call_with_gather(
        body, shards, lambda: pl.program_id(0) == 0, lambda: pl.program_id(0) == ni - 2,
        lambda: pl.program_id(0) == ni - 1,
        name="attn_out", grid=(ni,),
        in_specs=[row(D_GRP)] * 2 + [_view_spec(tm, 4)] * 2 + [_view_spec(tm, 16)] * 2
        + [row(D_GRP), row(D_MODEL), _full((1, D_GRP)), _full((1, D_GRP)), _full((N_DEV, OUT_SHARD, D_MODEL))],
        out_specs=(row(D_GRP), row(D_GRP), _view_spec(tm, 4), _view_spec(tm, 16), row(D_MODEL), row(D_MODEL)),
        out_shape=(jax.ShapeDtypeStruct((s_len, D_GRP), F32), jax.ShapeDtypeStruct((s_len, D_GRP), F32),
                   _view_shape(s_len, 4, F32), _view_shape(s_len, 16, F32),
                   jax.ShapeDtypeStruct((s_len, D_MODEL), BF16), jax.ShapeDtypeStruct((s_len, D_MODEL), F32)),
        scratch_shapes=[pltpu.VMEM((8, tm, 128), F32)] * 3,
    )(o_b[0], lse_b[0], o_b[1], lse_b[1], o_b[2], lse_b[2], o_sb, x2, wdil, wsb, out_g, *shards)


def _two_shards(w_ref):
    return w_ref[...].reshape(FF_BLOCK, D_MODEL)


def _ffn_fwd(x1, wn2, tgt, gate_g, up_g, down_g):
    s_len = x1.shape[0]
    tm = ROW_TILE
    ni = s_len // tm

    def body(x_ref, wn_ref, t_ref, wg_ref, wu_ref, wd_ref, g_ref, u_ref, h2_ref, dy_ref, loss_ref, acc):
        j = pl.program_id(1)

        @pl.when(j == 0)
        def _():
            xx = x_ref[...]
            r = lax.rsqrt(jnp.mean(xx * xx, axis=-1, keepdims=True) + EPS)
            h2_ref[...] = ((xx * r) * wn_ref[...]).astype(BF16)
            acc[...] = jnp.zeros_like(acc)

        h = h2_ref[...]
        g = _dot_nt(h, _two_shards(wg_ref))
        u = _dot_nt(h, _two_shards(wu_ref))
        g_ref[...] = g
        u_ref[...] = u
        act = (g * (1.0 / (1.0 + jnp.exp(-g)))) * u
        acc[...] += _dot(act.astype(BF16), _two_shards(wd_ref))

        @pl.when(j == FF_STEPS - 1)
        def _():
            err = (x_ref[...] + acc[...]) - t_ref[...]
            dy_ref[...] = err * (1.0 / D_MODEL)
            part = 0.5 * jnp.sum(jnp.mean(err * err, axis=-1, keepdims=True))
            loss_ref[...] = jnp.full((8, 128), part, F32)

    row = pl.BlockSpec((tm, D_MODEL), lambda i, j: (i, 0))
    hid = pl.BlockSpec((tm, FF_BLOCK), lambda i, j: (i, j))
    return pl.pallas_call(
        body, name="ffn_fwd", grid=(ni, FF_STEPS),
        in_specs=[row, pl.BlockSpec((1, D_MODEL), lambda i, j: (0, 0)), row,
                  pl.BlockSpec((2, FF_PAD, D_MODEL), lambda i, j: (j, 0, 0)),
                  pl.BlockSpec((2, FF_PAD, D_MODEL), lambda i, j: (j, 0, 0)),
                  pl.BlockSpec((2, FF_PAD, D_MODEL), lambda i, j: (j, 0, 0))],
        out_specs=(hid, hid, row, row, pl.BlockSpec((8, 128), lambda i, j: (i, 0))),
        out_shape=(jax.ShapeDtypeStruct((s_len, N_DEV * FF_PAD), F32),
                   jax.ShapeDtypeStruct((s_len, N_DEV * FF_PAD), F32),
                   jax.ShapeDtypeStruct((s_len, D_MODEL), BF16),
                   jax.ShapeDtypeStruct((s_len, D_MODEL), F32),
                   jax.ShapeDtypeStruct((ni * 8, 128), F32)),
        scratch_shapes=[pltpu.VMEM((tm, D_MODEL), F32)],
        compiler_params=_params(),
    )(x1, wn2, tgt, gate_g, up_g, down_g)


def _ffn_bwd_dx(dy, g, u, gate_g, up_g, down_g):
    s_len = dy.shape[0]
    tm = ROW_TILE

    def body(dy_ref, g_ref, u_ref, wg_ref, wu_ref, wd_ref, dg_ref, du_ref, act_ref, dh_ref, acc):
        j = pl.program_id(1)

        @pl.when(j == 0)
        def _():
            acc[...] = jnp.zeros_like(acc)

        halves = [slice(0, tm // 2), slice(tm // 2, tm)]
        wd, wg, wu = _two_shards(wd_ref), _two_shards(wg_ref), _two_shards(wu_ref)
        das = [_dot_nt(dy_ref[rows, :].astype(BF16), wd) for rows in halves]

        def elementwise(rows, da):
            gg, uu = g_ref[rows, :], u_ref[rows, :]
            sig = 1.0 / (1.0 + jnp.exp(-gg))
            silu = gg * sig
            act_ref[rows, :] = (silu * uu).astype(BF16)
            du = (da * silu).astype(BF16)
            dg = (da * uu * (sig * (1.0 + gg * (1.0 - sig)))).astype(BF16)
            du_ref[rows, :] = du
            dg_ref[rows, :] = dg
            return dg, du

        dg0, du0 = elementwise(halves[0], das[0])
        acc[halves[0], :] += _dot(dg0, wg) + _dot(du0, wu)
        dg1, du1 = elementwise(halves[1], das[1])
        acc[halves[1], :] += _dot(dg1, wg) + _dot(du1, wu)

        @pl.when(j == FF_STEPS - 1)
        def _():
            dh_ref[...] = acc[...]

    row = pl.BlockSpec((tm, D_MODEL), lambda i, j: (i, 0))
    hid = pl.BlockSpec((tm, FF_BLOCK), lambda i, j: (i, j))
    hid_bf = jax.ShapeDtypeStruct((s_len, N_DEV * FF_PAD), BF16)
    return pl.pallas_call(
        body, name="ffn_bwd_dx", grid=(s_len // tm, FF_STEPS),
        in_specs=[row, hid, hid,
                  pl.BlockSpec((2, FF_PAD, D_MODEL), lambda i, j: (j, 0, 0)),
                  pl.BlockSpec((2, FF_PAD, D_MODEL), lambda i, j: (j, 0, 0)),
                  pl.BlockSpec((2, FF_PAD, D_MODEL), lambda i, j: (j, 0, 0))],
        out_specs=(hid, hid, hid, row),
        out_shape=(hid_bf, hid_bf, hid_bf, jax.ShapeDtypeStruct((s_len, D_MODEL), F32)),
        scratch_shapes=[pltpu.VMEM((tm, D_MODEL), F32)],
        compiler_params=_params(),
    )(dy, g, u, gate_g, up_g, down_g)


def _ffn_bwd_dw(h2, dy, dg, du, act):
    s_len = h2.shape[0]
    tm = DW_ROW_TILE
    ni = s_len // tm

    def body(h_ref, dy_ref, dg_ref, du_ref, act_ref, dwg_ref, dwu_ref, dwd_ref, ag, au, ad):
        i = pl.program_id(1)

        @pl.when(i == 0)
        def _():
            ag[...] = jnp.zeros_like(ag)
            au[...] = jnp.zeros_like(au)
            ad[...] = jnp.zeros_like(ad)

        h = h_ref[...]
        ag[...] += _dot_tn(dg_ref[...], h)
        au[...] += _dot_tn(du_ref[...], h)
        ad[...] += _dot_tn(act_ref[...], dy_ref[...].astype(BF16))

        @pl.when(i == ni - 1)
        def _():
            for acc_ref, out_ref in ((ag, dwg_ref), (au, dwu_ref), (ad, dwd_ref)):
                out_ref[...] = acc_ref[...].astype(BF16).reshape(2, FF_PAD, D_MODEL)

    row = pl.BlockSpec((tm, D_MODEL), lambda j, i: (i, 0))
    hid = pl.BlockSpec((tm, FF_BLOCK), lambda j, i: (i, j))
    row_w = pl.BlockSpec((2, FF_PAD, D_MODEL), lambda j, i: (j, 0, 0))
    grad = jax.ShapeDtypeStruct((N_DEV, FF_PAD, D_MODEL), BF16)
    return pl.pallas_call(
        body, name="ffn_bwd_dw", grid=(FF_STEPS, ni),
        in_specs=[row, row, hid, hid, hid], out_specs=(row_w, row_w, row_w),
        out_shape=(grad, grad, grad),
        scratch_shapes=[pltpu.VMEM((FF_BLOCK, D_MODEL), F32)] * 3,
        compiler_params=_params(),
    )(h2, dy, dg, du, act)


def _rms_bwd(dy, t, w):
    r = lax.rsqrt(jnp.mean(t * t, axis=-1, keepdims=True) + EPS)
    gw = dy * w
    dt = r * (gw - t * ((r * r) * jnp.mean(gw * t, axis=-1, keepdims=True)))
    return dt, dy * t * r


def _attn_out_bwd(dy, dh2, x1, wn2, b_g, mixed, o_dil, o_sb, wdil, wsb, bd512):
    s_len = dy.shape[0]
    tm = ROW_TILE
    ni = s_len // tm

    def body(dy_ref, dh_ref, x1_ref, wn_ref, w_ref, mixed_ref, odil_ref, osb_ref, wdil_ref, wsb_ref, bd_ref,
             dx1_ref, dodil_ref, delta_ref, dosb_ref, dwout_ref, dwn_ref, dwdil_ref, dwsb_ref,
             do4_ref, dl4_ref, do16_ref, dl16_ref, wacc, both, stage):
        i = pl.program_id(0)

        @pl.when(i == 0)
        def _():
            wacc[...] = jnp.zeros_like(wacc)
            dwn_ref[...] = jnp.zeros_like(dwn_ref)
            dwdil_ref[...] = jnp.zeros_like(dwdil_ref)
            dwsb_ref[...] = jnp.zeros_like(dwsb_ref)

        dnorm, dw_rows = _rms_bwd(dh_ref[...], x1_ref[...], wn_ref[...])
        dx1 = dy_ref[...] + dnorm
        dx1_ref[...] = dx1
        dwn_ref[...] += jnp.sum(dw_rows, axis=0, keepdims=True)
        dx1b = dx1.astype(BF16)
        w = w_ref[...].reshape(D_MODEL, D_MODEL)
        dmixed = _dot_nt(dx1b, w)
        wacc[...] += _dot_tn(mixed_ref[...], dx1b)
        o_dil = odil_ref[...]
        d_odil, dw_rows = _rms_bwd(dmixed[:, :D_GRP], o_dil, wdil_ref[...])
        dwdil_ref[...] += jnp.sum(dw_rows, axis=0, keepdims=True)
        dodil_ref[...] = d_odil.astype(BF16)
        delta = _mm_split(d_odil * o_dil, bd_ref[...])
        delta_ref[...] = delta
        for p in range(4):
            both[p] = d_odil[:, 128 * p:128 * (p + 1)]
            both[4 + p] = delta[:, 128 * p:128 * (p + 1)]
        _split_views(both, stage, (do4_ref, dl4_ref), (do16_ref, dl16_ref))
        d_osb, dw_rows = _rms_bwd(dmixed[:, D_GRP:], osb_ref[...], wsb_ref[...])
        dwsb_ref[...] += jnp.sum(dw_rows, axis=0, keepdims=True)
        dosb_ref[...] = d_osb.astype(BF16)

        @pl.when(i == ni - 1)
        def _():
            dwout_ref[...] = wacc[...].astype(BF16).reshape(N_DEV, OUT_SHARD, D_MODEL)

    row = lambda w: pl.BlockSpec((tm, w), lambda i: (i, 0))
    return pl.pallas_call(
        body, name="attn_out_bwd", grid=(ni,),
        in_specs=[row(D_MODEL), row(D_MODEL), row(D_MODEL), _full((1, D_MODEL)),
                  _full((N_DEV, OUT_SHARD, D_MODEL)),
                  row(D_MODEL), row(D_GRP), row(D_GRP), _full((1, D_GRP)), _full((1, D_GRP)),
                  _full((D_GRP, D_GRP))],
        out_specs=(row(D_MODEL), row(D_GRP), row(D_GRP), row(D_GRP),
                   _full((N_DEV, OUT_SHARD, D_MODEL)), _full((1, D_MODEL)), _full((1, D_GRP)), _full((1, D_GRP)),
                   _view_spec(tm, 4), _view_spec(tm, 4), _view_spec(tm, 16), _view_spec(tm, 16)),
        out_shape=(jax.ShapeDtypeStruct((s_len, D_MODEL), F32), jax.ShapeDtypeStruct((s_len, D_GRP), BF16),
                   jax.ShapeDtypeStruct((s_len, D_GRP), F32), jax.ShapeDtypeStruct((s_len, D_GRP), BF16),
                   jax.ShapeDtypeStruct((N_DEV, OUT_SHARD, D_MODEL), BF16),
                   jax.ShapeDtypeStruct((1, D_MODEL), F32), jax.ShapeDtypeStruct((1, D_GRP), F32),
                   jax.ShapeDtypeStruct((1, D_GRP), F32),
                   _view_shape(s_len, 4, BF16), _view_shape(s_len, 4, F32),
                   _view_shape(s_len, 16, BF16), _view_shape(s_len, 16, F32)),
        scratch_shapes=[pltpu.VMEM((D_MODEL, D_MODEL), F32), pltpu.VMEM((8, tm, 128), F32),
                        pltpu.VMEM((8, tm, 128), F32)],
        compiler_params=_params(),
    )(dy, dh2, x1, wn2, b_g, mixed, o_dil, o_sb, wdil, wsb, bd512)


def _qkv_bwd(dq_b, dk_b, dv_b, dqs, dks, dvs, qraw, kraw, cos2, sin2, qnw, knw, bd):
    s_len = qraw.shape[0]
    tm = ROW_TILE
    ni = s_len // tm

    def body(dq1, dk1, dv1, dq4, dk4, dv4, dq16, dk16, dv16, dqs_ref, dks_ref, dvs_ref,
             qraw_ref, kraw_ref, cos_ref, sin_ref, qnw_ref, knw_ref, bd_ref,
             dproj_ref, dqn_ref, dkn_ref, stage, nat4, nat16):
        i = pl.program_id(0)

        @pl.when(i == 0)
        def _():
            dqn_ref[...] = jnp.zeros_like(dqn_ref)
            dkn_ref[...] = jnp.zeros_like(dkn_ref)

        _merge_views((dq4, dk4, dv4), (dq16, dk16, dv16), stage, nat4, nat16)
        cos_t, sin_t, bdm = cos_ref[...], sin_ref[...], bd_ref[...]
        for grp, (part1, raw_ref, nw_ref, dn_ref) in enumerate(((dq1, qraw_ref, qnw_ref, dqn_ref),
                                                                (dk1, kraw_ref, knw_ref, dkn_ref))):
            dn_acc = 0.0
            for p in range(4):
                cols = slice(128 * p, 128 * (p + 1))
                d_rope = part1[:, cols] + nat4[4 * grp + p] + nat16[4 * grp + p]
                d_norm = d_rope * cos_t + _swap_halves(d_rope * sin_t)
                t = raw_ref[:, cols]
                w = nw_ref[...]
                r = lax.rsqrt(_mm_split(t * t, bdm) * (1.0 / HEAD_DIM) + EPS)
                gw = d_norm * w
                corr = _mm_split(gw * t, bdm) * (1.0 / HEAD_DIM)
                dt = r * (gw - t * ((r * r) * corr))
                dn_acc = dn_acc + jnp.sum(d_norm * t * r, axis=0, keepdims=True)
                dproj_ref[:, D_GRP * grp + 128 * p:D_GRP * grp + 128 * (p + 1)] = dt.astype(BF16)
            dn_ref[...] += dn_acc
        dproj_ref[:, 2 * D_GRP:3 * D_GRP] = (dv1[...] + _slab_group(nat4, 2) + _slab_group(nat16, 2)).astype(BF16)
        dproj_ref[:, 3 * D_GRP:4 * D_GRP] = dqs_ref[...].astype(BF16)
        dproj_ref[:, 4 * D_GRP:5 * D_GRP] = dks_ref[...].astype(BF16)
        dproj_ref[:, 5 * D_GRP:6 * D_GRP] = dvs_ref[...].astype(BF16)

    row = lambda w: pl.BlockSpec((tm, w), lambda i: (i, 0))
    return pl.pallas_call(
        body, name="qkv_bwd", grid=(ni,),
        in_specs=[row(D_GRP)] * 3 + [_view_spec(tm, 4)] * 3 + [_view_spec(tm, 16)] * 3 + [row(D_GRP)] * 5
        + [row(128), row(128), _full((1, 128)), _full((1, 128)), _full((128, 128))],
        out_specs=(row(D_IN), _full((1, 128)), _full((1, 128))),
        out_shape=(jax.ShapeDtypeStruct((s_len, D_IN), BF16), jax.ShapeDtypeStruct((1, 128), F32),
                   jax.ShapeDtypeStruct((1, 128), F32)),
        scratch_shapes=[pltpu.VMEM((12, tm, 128), F32)] * 3,
        compiler_params=_params(),
    )(dq_b[0], dk_b[0], dv_b[0], dq_b[1], dk_b[1], dv_b[1], dq_b[2], dk_b[2], dv_b[2],
      dqs, dks, dvs, qraw, kraw, cos2, sin2, qnw, knw, bd)


def _in_bwd_dx(dproj, a_g, x2, dx1, wn1, rides):
    s_len = x2.shape[0]
    tm = ROW_TILE
    ni = s_len // tm

    def body(dp_ref, w_ref, x_ref, dx1_ref, wn_ref, gx_ref, dwn_ref, w_full):
        i = pl.program_id(0)

        @pl.when(i == 0)
        def _():
            dwn_ref[...] = jnp.zeros_like(dwn_ref)
            for d in range(N_DEV):
                w_full[:, IN_SHARD * d:IN_SHARD * (d + 1)] = w_ref[d]

        dh = _dot_nt(dp_ref[...], w_full[...])
        dnorm, dw_rows = _rms_bwd(dh, x_ref[...], wn_ref[...])
        gx_ref[...] = dx1_ref[...] + dnorm
        dwn_ref[...] += jnp.sum(dw_rows, axis=0, keepdims=True)

    row = lambda w: pl.BlockSpec((tm, w), lambda i: (i, 0))
    return _call_with_exchange(
        body, rides, lambda: pl.program_id(0) == 0, lambda: pl.program_id(0) == ni - 1,
        name="in_bwd_dx", grid=(ni,),
        in_specs=[row(D_IN), pl.BlockSpec((N_DEV, D_MODEL, IN_SHARD), lambda i: (0, 0, 0)),
                  row(D_MODEL), row(D_MODEL), _full((1, D_MODEL))],
        out_specs=(row(D_MODEL), _full((1, D_MODEL))),
        out_shape=(jax.ShapeDtypeStruct((s_len, D_MODEL), F32), jax.ShapeDtypeStruct((1, D_MODEL), F32)),
        scratch_shapes=[pltpu.VMEM((D_MODEL, D_IN), BF16)],
    )(dproj, a_g, x2, dx1, wn1, *rides)


def _in_bwd_dw(h1, dproj):
    s_len = h1.shape[0]
    tm = DW_ROW_TILE
    ni = s_len // tm

    def body(h_ref, dp_ref, dw_ref, acc):
        i = pl.program_id(1)

        @pl.when(i == 0)
        def _():
            acc[...] = jnp.zeros_like(acc)

        acc[...] += _dot_tn(h_ref[...], dp_ref[...])

        @pl.when(i == ni - 1)
        def _():
            for half in range(2):
                dw_ref[half] = acc[:, IN_SHARD * half:IN_SHARD * (half + 1)].astype(BF16)

    return pl.pallas_call(
        body, name="in_bwd_dw", grid=(N_DEV // 2, ni),
        in_specs=[pl.BlockSpec((tm, D_MODEL), lambda d, i: (i, 0)),
                  pl.BlockSpec((tm, 2 * IN_SHARD), lambda d, i: (i, d))],
        out_specs=pl.BlockSpec((2, D_MODEL, IN_SHARD), lambda d, i: (d, 0, 0)),
        out_shape=jax.ShapeDtypeStruct((N_DEV, D_MODEL, IN_SHARD), BF16),
        scratch_shapes=[pltpu.VMEM((D_MODEL, 2 * IN_SHARD), F32)],
        compiler_params=_params(),
    )(h1, dproj)


def _adamw(recv, w, m, v):
    rows, cols = w.shape
    tr = next((t for t in (128, 32) if rows % t == 0), rows)

    def body(p_ref, w_ref, m_ref, v_ref, g_ref, d_ref, nm_ref, nv_ref):
        g = p_ref[0].astype(F32)
        for s in range(1, N_DEV):
            g = g + p_ref[s].astype(F32)
        m_new = ADAM_B1 * m_ref[...] + (1.0 - ADAM_B1) * g
        v_new = ADAM_B2 * v_ref[...] + (1.0 - ADAM_B2) * (g * g)
        m_hat = m_new / (1.0 - ADAM_B1 ** ADAM_STEP)
        v_hat = v_new / (1.0 - ADAM_B2 ** ADAM_STEP)
        g_ref[...] = g
        d_ref[...] = -ADAM_LR * (m_hat / (jnp.sqrt(v_hat) + ADAM_EPS) + ADAM_WD * w_ref[...])
        nm_ref[...] = m_new
        nv_ref[...] = v_new

    blk = pl.BlockSpec((tr, cols), lambda i: (i, 0))
    out = jax.ShapeDtypeStruct((rows, cols), F32)
    return pl.pallas_call(
        body, name=f"adamw_{rows}x{cols}", grid=(rows // tr,),
        in_specs=[pl.BlockSpec((N_DEV, tr, cols), lambda i: (0, i, 0)), blk, blk, blk],
        out_specs=(blk,) * 4, out_shape=(out,) * 4,
        compiler_params=_params(),
    )(recv, w, m, v)


def _rope_tables(s_len):
    pos = jnp.arange(s_len, dtype=F32)
    inv_freq = ROPE_THETA ** (-jnp.arange(0, HEAD_DIM, 2, dtype=F32) / HEAD_DIM)
    ang = pos[:, None] * inv_freq[None, :]
    cos, sin = jnp.cos(ang), jnp.sin(ang)
    cos2 = jnp.concatenate([cos, cos, cos, cos], axis=1)
    sin2 = jnp.concatenate([-sin, sin, -sin, sin], axis=1)
    return cos2, sin2


def _block_diag_ones(n):
    i = jnp.arange(n)
    return (i[:, None] // HEAD_DIM == i[None, :] // HEAD_DIM).astype(BF16)


def _pad_cols(t):
    return jnp.pad(t, ((0, 0), (0, FF_PAD - FF_SHARD)))


def _pad_rows(t):
    return jnp.pad(t, ((0, FF_PAD - FF_SHARD), (0, 0)))


LOSS_ROW = 26


def _pack_small(n1, n2, ndil, nsb, nq, nk, scalar=None):
    pad = lambda t: jnp.pad(t.reshape(1, -1), ((0, 0), (0, 128 - t.size)))
    last = jnp.zeros((1, 128), F32) if scalar is None else pad(scalar)
    rows = [n1.reshape(8, 128), n2.reshape(8, 128), ndil.reshape(4, 128), nsb.reshape(4, 128),
            pad(nq), pad(nk), last, jnp.zeros((5, 128), F32)]
    return jnp.concatenate(rows, axis=0)


def _unpack_small(t):
    return (t[0:8].reshape(1, D_MODEL), t[8:16].reshape(1, D_MODEL), t[16:20].reshape(1, D_GRP),
            t[20:24].reshape(1, D_GRP), t[24:25, :HEAD_DIM], t[25:26, :HEAD_DIM])


def kernel(x, attn_norm_w, w_in, q_norm_w, k_norm_w, dil_out_norm_w, sb_out_norm_w, w_out, ffn_norm_w, w_gate, w_up, w_down, loss_target, m_attn_norm_w, m_w_in, m_q_norm_w, m_k_norm_w, m_dil_out_norm_w, m_sb_out_norm_w, m_w_out, m_ffn_norm_w, m_w_gate, m_w_up, m_w_down, v_attn_norm_w, v_w_in, v_q_norm_w, v_k_norm_w, v_dil_out_norm_w, v_sb_out_norm_w, v_w_out, v_ffn_norm_w, v_w_gate, v_w_up, v_w_down):
    s_len = x.shape[1]
    x2, tgt = x[0], loss_target[0]

    (a_g,) = _gather_weights([w_in[0].astype(BF16)])
    gate_loc = _pad_cols(w_gate[0]).T.astype(BF16)
    up_loc = _pad_cols(w_up[0]).T.astype(BF16)
    down_loc = _pad_rows(w_down[0]).astype(BF16)
    out_loc = w_out[0].astype(BF16)

    cos2, sin2 = _rope_tables(s_len)
    bd128, bd512 = _block_diag_ones(128), _block_diag_ones(D_GRP)
    idx = jnp.arange(SB_TILE)
    tri_suf = (idx[:, None] > idx[None, :]).astype(BF16)
    tri_pre = (idx[:, None] < idx[None, :]).astype(BF16)
    qnw2 = jnp.concatenate([q_norm_w, q_norm_w], axis=1)
    knw2 = jnp.concatenate([k_norm_w, k_norm_w], axis=1)

    (h1, qraw, kraw, q, k, va, qs, ks, vs, q4, k4, v4, q16, k16, v16,
     gate_g) = _attn_in(x2, attn_norm_w, a_g, cos2, sin2, qnw2, knw2, bd128, shards=[gate_loc])
    qkv_views = {1: (q, k, va), 4: (q4, k4, v4), 16: (q16, k16, v16)}
    fwd_riders = {1: [out_loc], 4: [], 16: []}
    o_b, lse_b, gathered = [], [], {}
    for r in DILATIONS:
        o, lse, *gathered[r] = _dil_fwd(*qkv_views[r], r, shards=fwd_riders[r])
        o_b.append(o)
        lse_b.append(lse)
    (out_g,) = gathered[1]
    o_sb, c_sb, up_g = _sb_fwd(qs, ks, vs, tri_suf, shards=[up_loc])
    o_dil, lse_tot, lse4, lse16, mixed, x1, down_g = _attn_out(
        o_b, lse_b, o_sb, x2, dil_out_norm_w, sb_out_norm_w, out_g, shards=[down_loc])
    g, u, h2, dy, loss_parts = _ffn_fwd(x1, ffn_norm_w, tgt, gate_g, up_g, down_g)
    loss_local = jnp.sum(loss_parts[::8, 0])

    dg, du, act, dh2 = _ffn_bwd_dx(dy, g, u, gate_g, up_g, down_g)
    (dx1, do_dil, delta, do_sb, dwout, dn2, dndil, dnsb, do4, dl4, do16, dl16) = _attn_out_bwd(
        dy, dh2, x1, ffn_norm_w, out_g, mixed, o_dil, o_sb, dil_out_norm_w, sb_out_norm_w, bd512)
    dwg, dwu, dwd = _ffn_bwd_dw(h2, dy, dg, du, act)
    dqs, dks, dvs, r_gate = _sb_bwd(qs, ks, vs, do_sb, c_sb, tri_suf, tri_pre, rides=[dwg])
    cot_views = {1: (do_dil, lse_tot, delta), 4: (do4, lse4, dl4), 16: (do16, lse16, dl16)}
    riders = {1: [dwout], 4: [dwd], 16: [dwu]}
    dq_b, dk_b, dv_b, landed = [], [], [], {}
    for r in DILATIONS:
        dq, dk, dv, *landed[r] = _dil_bwd(*qkv_views[r], *cot_views[r], r, rides=riders[r])
        dq_b.append(dq)
        dk_b.append(dk)
        dv_b.append(dv)
    (r_out,), (r_down,), (r_up,) = landed[1], landed[4], landed[16]
    dproj, dqn2, dkn2 = _qkv_bwd(dq_b, dk_b, dv_b, dqs, dks, dvs, qraw, kraw, cos2, sin2, qnw2, knw2, bd128)
    dwin = _in_bwd_dw(h1, dproj)
    grad_x, dn1, r_in = _in_bwd_dx(dproj, a_g, x2, dx1, attn_norm_w, rides=[dwin])
    dqn = dqn2[:, :HEAD_DIM] + dqn2[:, HEAD_DIM:]
    dkn = dkn2[:, :HEAD_DIM] + dkn2[:, HEAD_DIM:]

    small = _pack_small(dn1, dn2, dndil, dnsb, dqn, dkn, loss_local)
    (r_small,) = _exchange_grads([], small)
    big = {
        "w_in": _adamw(r_in, w_in[0], m_w_in[0], v_w_in[0]),
        "w_gate": tuple(t.T for t in _adamw(r_gate, w_gate[0].T, m_w_gate[0].T, v_w_gate[0].T)),
        "w_up": tuple(t.T for t in _adamw(r_up, w_up[0].T, m_w_up[0].T, v_w_up[0].T)),
        "w_down": _adamw(r_down, w_down[0], m_w_down[0], v_w_down[0]),
        "w_out": _adamw(r_out, w_out[0], m_w_out[0], v_w_out[0]),
    }
    packs = [_pack_small(*ts) for ts in (
        (attn_norm_w, ffn_norm_w, dil_out_norm_w, sb_out_norm_w, q_norm_w, k_norm_w),
        (m_attn_norm_w, m_ffn_norm_w, m_dil_out_norm_w, m_sb_out_norm_w, m_q_norm_w, m_k_norm_w),
        (v_attn_norm_w, v_ffn_norm_w, v_dil_out_norm_w, v_sb_out_norm_w, v_q_norm_w, v_k_norm_w))]
    small_raw = _adamw(r_small, *packs)
    loss = small_raw[0][LOSS_ROW, 0]
    small_out = [_unpack_small(t) for t in small_raw]
    names = ["attn_norm_w", "w_in", "q_norm_w", "k_norm_w", "dil_out_norm_w", "sb_out_norm_w", "w_out",
             "ffn_norm_w", "w_gate", "w_up", "w_down"]
    small_pos = {"attn_norm_w": 0, "ffn_norm_w": 1, "dil_out_norm_w": 2, "sb_out_norm_w": 3,
                 "q_norm_w": 4, "k_norm_w": 5}
    outs = [loss, grad_x[None]]
    for kind in range(4):
        for name in names:
            if name in small_pos:
                outs.append(small_out[kind][small_pos[name]])
            else:
                outs.append(big[name][kind][None])
    return tuple(outs)
```

```python
import jax
import jax.numpy as jnp
from jax import lax
from jax.experimental import pallas as pl
from jax.experimental.pallas import tpu as pltpu

F32 = jnp.float32
BF16 = jnp.bfloat16

N_DEV = 8
D_MODEL = 1024
HEAD_DIM = 64
D_GRP = 512
D_IN = 6 * D_GRP
IN_SHARD = D_IN // N_DEV
FF_SHARD = 352
FF_PAD = 384
FF_BLOCK = 2 * FF_PAD
FF_STEPS = N_DEV // 2
OUT_SHARD = D_MODEL // N_DEV
BLOCK = 128
DILATIONS = (1, 4, 16)
ROPE_THETA = 10000.0
EPS = 1e-6
ATT_SCALE = HEAD_DIM ** -0.5
NEG = -1e30

ADAM_LR = 0.001
ADAM_B1 = 0.9
ADAM_B2 = 0.999
ADAM_EPS = 1e-08
ADAM_WD = 0.01
ADAM_STEP = 10

SB_TILE = 256
SB_DEAD = -104.0
SB_PAIRS = 4
SB_BWD_PAIRS = 2
ROW_TILE = 512
DW_ROW_TILE = 1024
VMEM_LIMIT = 56 * 1024 * 1024
MESH = pl.DeviceIdType.MESH


def _dot(a, b):
    return jnp.dot(a, b, preferred_element_type=F32)


def _dot_nt(a, b):
    return lax.dot_general(a, b, (((1,), (1,)), ((), ())), preferred_element_type=F32)


def _dot_tn(a, b):
    return lax.dot_general(a, b, (((0,), (0,)), ((), ())), preferred_element_type=F32)


def _mm_split(t, m):
    hi = t.astype(BF16)
    lo = (t - hi.astype(F32)).astype(BF16)
    return _dot(hi, m) + _dot(lo, m)


def _params(**kw):
    return pltpu.CompilerParams(vmem_limit_bytes=VMEM_LIMIT, **kw)


def _full(shape):
    nd = len(shape)
    return pl.BlockSpec(shape, lambda *_: (0,) * nd)


def _view_shape(s_len, r, dtype):
    return jax.ShapeDtypeStruct((s_len // r, r * D_GRP), dtype)


def _view_spec(tm, r):
    return pl.BlockSpec((tm // r, r * D_GRP), lambda i: (i, 0))


def _swap_halves(t):
    lane = lax.broadcasted_iota(jnp.int32, t.shape, 1)
    first = (lane & 32) == 0
    return jnp.where(first, pltpu.roll(t, 96, 1), pltpu.roll(t, 32, 1))


def _log_sigmoid_pair(z):
    neg_abs = lax.bitcast_convert_type(lax.bitcast_convert_type(z, jnp.uint32) | jnp.uint32(0x80000000), F32)
    lb = jnp.minimum(z, 0.0) - jnp.log(1.0 + jnp.exp(neg_abs))
    return lb, lb - z


def _cumsum_mm(t, tri):
    return _dot(t.astype(BF16), tri)


def _split_views(src_ref, stage_ref, views4, views16):
    slabs, n, _ = src_ref.shape
    n4, n16 = n // 4, n // 16
    for j in range(slabs):
        g, lanes = j // 4, 128 * (j % 4)
        src, stage = src_ref.at[j], stage_ref.at[j]
        for c4 in range(4):
            blk = src[pl.ds(c4, n4, stride=4), :]
            stage[n4 * c4:n4 * (c4 + 1), :] = blk
            col = D_GRP * c4 + lanes
            views4[g][:, col:col + 128] = blk.astype(views4[g].dtype)
        for c4 in range(4):
            for c1 in range(4):
                blk = stage[pl.ds(n4 * c4 + c1, n16, stride=4), :]
                col = D_GRP * (4 * c1 + c4) + lanes
                views16[g][:, col:col + 128] = blk.astype(views16[g].dtype)


def _merge_views(views4, views16, stage_ref, dst4_ref, dst16_ref):
    slabs, n, _ = dst4_ref.shape
    n4, n16 = n // 4, n // 16
    for j in range(slabs):
        g, lanes = j // 4, 128 * (j % 4)
        dst4, dst16, stage = dst4_ref.at[j], dst16_ref.at[j], stage_ref.at[j]
        for c4 in range(4):
            col = D_GRP * c4 + lanes
            dst4[pl.ds(c4, n4, stride=4), :] = views4[g][:, col:col + 128].astype(F32)
            for c1 in range(4):
                col = D_GRP * (4 * c1 + c4) + lanes
                stage[pl.ds(n4 * c4 + c1, n16, stride=4), :] = views16[g][:, col:col + 128].astype(F32)
        for c4 in range(4):
            dst16[pl.ds(c4, n4, stride=4), :] = stage[n4 * c4:n4 * (c4 + 1), :]


def _slab_group(ref, g):
    return jnp.concatenate([ref[4 * g + p] for p in range(4)], axis=1)


def _mesh_pos():
    return lax.axis_index("x"), lax.axis_index("y"), lax.axis_index("c")


def _flat_index(p):
    return 4 * p[0] + 2 * p[1] + p[2]


def _gather_weights(shards):
    n_arr = len(shards)

    def body(*refs):
        srcs, outs = refs[:n_arr], refs[n_arr:2 * n_arr]
        send_sems, recv_sems, local_sems = refs[2 * n_arr:]
        x, y, c = _mesh_pos()
        me, sibling = (x, y, c), (x, y, 1 - c)
        chips = [(1 - x, y), (x, 1 - y), (1 - x, 1 - y)]

        def copy(arr, k, block, to, own=False):
            dst = outs[arr].at[_flat_index(block)]
            return pltpu.make_async_remote_copy(
                src_ref=srcs[arr] if own else dst, dst_ref=dst,
                send_sem=send_sems.at[arr, k], recv_sem=recv_sems.at[arr, k],
                device_id=to, device_id_type=MESH)

        for arr in range(n_arr):
            mine = pltpu.make_async_copy(srcs[arr], outs[arr].at[_flat_index(me)], local_sems.at[arr])
            mine.start()
            first = [copy(arr, 0, me, sibling, own=True)]
            first += [copy(arr, 1 + j, me, (*chip, c), own=True) for j, chip in enumerate(chips)]
            for cp in first:
                cp.start()
        for arr in range(n_arr):
            passed = [copy(arr, 4 + j, (*chip, c), sibling) for j, chip in enumerate(chips)]
            for j, chip in enumerate(chips):
                copy(arr, 1 + j, (*chip, c), me).wait_recv()
                passed[j].start()
        for arr in range(n_arr):
            copy(arr, 0, sibling, me).wait_recv()
            for j, chip in enumerate(chips):
                copy(arr, 4 + j, (*chip, 1 - c), me).wait_recv()
            for k in range(7):
                copy(arr, k, me, me).wait_send()
            pltpu.make_async_copy(srcs[arr], outs[arr].at[_flat_index(me)], local_sems.at[arr]).wait()

    any_spec = pl.BlockSpec(memory_space=pl.ANY)
    return pl.pallas_call(
        body, name="gather_weights",
        out_shape=tuple(jax.ShapeDtypeStruct((N_DEV,) + s.shape, s.dtype) for s in shards),
        in_specs=[any_spec] * n_arr, out_specs=(any_spec,) * n_arr,
        scratch_shapes=[pltpu.SemaphoreType.DMA((n_arr, 7)), pltpu.SemaphoreType.DMA((n_arr, 7)),
                        pltpu.SemaphoreType.DMA((n_arr,))],
        compiler_params=pltpu.CompilerParams(has_side_effects=True),
    )(*shards)


def _peer_list(x, y, c):
    return [(1 - x if m & 4 else x, 1 - y if m & 2 else y, 1 - c if m & 1 else c) for m in range(1, N_DEV)]


def _exchange_grads(parts, small):
    n_arr = len(parts)

    def body(*refs):
        ins, outs = refs[:n_arr + 1], refs[n_arr + 1:2 * (n_arr + 1)]
        send_sems, recv_sems, local_sems = refs[2 * (n_arr + 1):]
        x, y, c = _mesh_pos()
        me = (x, y, c)
        my_idx = _flat_index(me)
        peers = []
        for m in range(1, N_DEV):
            peers.append((1 - x if m & 4 else x, 1 - y if m & 2 else y, 1 - c if m & 1 else c))

        def src_block(arr, dev):
            return ins[arr] if arr == n_arr else ins[arr].at[_flat_index(dev)]

        def copy(arr, k):
            return pltpu.make_async_remote_copy(
                src_ref=src_block(arr, peers[k]), dst_ref=outs[arr].at[my_idx],
                send_sem=send_sems.at[arr, k], recv_sem=recv_sems.at[arr, k],
                device_id=peers[k], device_id_type=MESH)

        def local(arr):
            return pltpu.make_async_copy(src_block(arr, me), outs[arr].at[my_idx], local_sems.at[arr])

        for arr in range(n_arr + 1):
            local(arr).start()
            for k in range(N_DEV - 1):
                copy(arr, k).start()
        for arr in range(n_arr + 1):
            for k in range(N_DEV - 1):
                cp = copy(arr, k)
                cp.wait_send()
                cp.wait_recv()
            local(arr).wait()

    any_spec = pl.BlockSpec(memory_space=pl.ANY)
    out_shape = tuple(jax.ShapeDtypeStruct(p.shape, p.dtype) for p in parts)
    out_shape += (jax.ShapeDtypeStruct((N_DEV,) + small.shape, small.dtype),)
    return pl.pallas_call(
        body, name="exchange_grads",
        out_shape=out_shape,
        in_specs=[any_spec] * (n_arr + 1), out_specs=(any_spec,) * (n_arr + 1),
        scratch_shapes=[pltpu.SemaphoreType.DMA((n_arr + 1, N_DEV - 1)),
                        pltpu.SemaphoreType.DMA((n_arr + 1, N_DEV - 1)),
                        pltpu.SemaphoreType.DMA((n_arr + 1,))],
        compiler_params=pltpu.CompilerParams(has_side_effects=True),
    )(*parts, small)


def _call_with_gather(body, shards, first_step, mid_step, last_step, *, name, grid, in_specs, out_specs,
                      out_shape, scratch_shapes=()):
    out_specs = tuple(out_specs) if isinstance(out_specs, (tuple, list)) else (out_specs,)
    out_shape = tuple(out_shape) if isinstance(out_shape, (tuple, list)) else (out_shape,)
    n_in, n_out, n_scr, n = len(in_specs), len(out_specs), len(scratch_shapes), len(shards)
    if n == 0:
        return pl.pallas_call(body, name=name, grid=grid, in_specs=list(in_specs), out_specs=out_specs,
                              out_shape=out_shape, scratch_shapes=list(scratch_shapes),
                              compiler_params=_params())

    def full_body(*refs):
        ins, srcs = refs[:n_in], refs[n_in:n_in + n]
        outs, lands = refs[n_in + n:n_in + n + n_out], refs[n_in + n + n_out:n_in + 2 * n + n_out]
        scratch = refs[n_in + 2 * n + n_out:n_in + 2 * n + n_out + n_scr]
        send_sems, recv_sems, local_sems = refs[-3:]
        x, y, c = _mesh_pos()
        me, sibling = (x, y, c), (x, y, 1 - c)
        chips = [(1 - x, y), (x, 1 - y), (1 - x, 1 - y)]

        def copy(a, k, block, to, own=False):
            dst = lands[a].at[_flat_index(block)]
            return pltpu.make_async_remote_copy(
                src_ref=srcs[a] if own else dst, dst_ref=dst,
                send_sem=send_sems.at[a, k], recv_sem=recv_sems.at[a, k],
                device_id=to, device_id_type=MESH)

        def local(a):
            return pltpu.make_async_copy(srcs[a], lands[a].at[_flat_index(me)], local_sems.at[a])

        @pl.when(first_step())
        def _():
            for a in range(n):
                local(a).start()
                copy(a, 0, me, sibling, own=True).start()
                for j, chip in enumerate(chips):
                    copy(a, 1 + j, me, (*chip, c), own=True).start()

        @pl.when(mid_step())
        def _():
            for a in range(n):
                for j, chip in enumerate(chips):
                    copy(a, 1 + j, (*chip, c), me).wait_recv()
                    copy(a, 4 + j, (*chip, c), sibling).start()

        body(*ins, *outs, *scratch)

        @pl.when(last_step())
        def _():
            for a in range(n):
                copy(a, 0, sibling, me).wait_recv()
                for j, chip in enumerate(chips):
                    copy(a, 4 + j, (*chip, 1 - c), me).wait_recv()
                for k in range(N_DEV - 1):
                    copy(a, k, me, me).wait_send()
                local(a).wait()

    any_spec = pl.BlockSpec(memory_space=pl.ANY)
    return pl.pallas_call(
        full_body, name=name, grid=grid,
        in_specs=list(in_specs) + [any_spec] * n,
        out_specs=out_specs + (any_spec,) * n,
        out_shape=out_shape + tuple(jax.ShapeDtypeStruct((N_DEV,) + t.shape, t.dtype) for t in shards),
        scratch_shapes=list(scratch_shapes) + [pltpu.SemaphoreType.DMA((n, N_DEV - 1)),
                                               pltpu.SemaphoreType.DMA((n, N_DEV - 1)),
                                               pltpu.SemaphoreType.DMA((n,))],
        compiler_params=_params(has_side_effects=True),
    )


def _call_with_exchange(body, rides, first_step, last_step, *, name, grid, in_specs, out_specs, out_shape,
                        scratch_shapes=()):
    out_specs = tuple(out_specs) if isinstance(out_specs, (tuple, list)) else (out_specs,)
    out_shape = tuple(out_shape) if isinstance(out_shape, (tuple, list)) else (out_shape,)
    n_in, n_out, n_scr, n = len(in_specs), len(out_specs), len(scratch_shapes), len(rides)
    if n == 0:
        return pl.pallas_call(body, name=name, grid=grid, in_specs=list(in_specs), out_specs=out_specs,
                              out_shape=out_shape, scratch_shapes=list(scratch_shapes),
                              compiler_params=_params())

    def full_body(*refs):
        ins, srcs = refs[:n_in], refs[n_in:n_in + n]
        outs, lands = refs[n_in + n:n_in + n + n_out], refs[n_in + n + n_out:n_in + 2 * n + n_out]
        scratch = refs[n_in + 2 * n + n_out:n_in + 2 * n + n_out + n_scr]
        send_sems, recv_sems, local_sems = refs[-3:]
        x, y, c = _mesh_pos()
        my_idx = _flat_index((x, y, c))
        peers = _peer_list(x, y, c)

        def remote(a, k):
            return pltpu.make_async_remote_copy(
                src_ref=srcs[a].at[_flat_index(peers[k])], dst_ref=lands[a].at[my_idx],
                send_sem=send_sems.at[a, k], recv_sem=recv_sems.at[a, k],
                device_id=peers[k], device_id_type=MESH)

        def local(a):
            return pltpu.make_async_copy(srcs[a].at[my_idx], lands[a].at[my_idx], local_sems.at[a])

        @pl.when(first_step())
        def _():
            for a in range(n):
                local(a).start()
                for k in range(N_DEV - 1):
                    remote(a, k).start()

        body(*ins, *outs, *scratch)

        @pl.when(last_step())
        def _():
            for a in range(n):
                for k in range(N_DEV - 1):
                    cp = remote(a, k)
                    cp.wait_send()
                    cp.wait_recv()
                local(a).wait()

    any_spec = pl.BlockSpec(memory_space=pl.ANY)
    res = pl.pallas_call(
        full_body, name=name, grid=grid,
        in_specs=list(in_specs) + [any_spec] * n,
        out_specs=out_specs + (any_spec,) * n,
        out_shape=out_shape + tuple(jax.ShapeDtypeStruct(t.shape, t.dtype) for t in rides),
        scratch_shapes=list(scratch_shapes) + [pltpu.SemaphoreType.DMA((n, N_DEV - 1)),
                                               pltpu.SemaphoreType.DMA((n, N_DEV - 1)),
                                               pltpu.SemaphoreType.DMA((n,))],
        compiler_params=_params(has_side_effects=True),
    )
    return res


def _head_norm(t, w128, bd):
    ms = _mm_split(t * t, bd) * (1.0 / HEAD_DIM)
    r = lax.rsqrt(ms + EPS)
    return (t * r) * w128, r


def _attn_in(x2, wn1, a_g, cos2, sin2, qnw, knw, bd, shards):
    s_len = x2.shape[0]
    tm = ROW_TILE

    def body(x_ref, wn_ref, w_ref, cos_ref, sin_ref, qnw_ref, knw_ref, bd_ref,
             h1_ref, qraw_ref, kraw_ref, q_ref, k_ref, va_ref, qs_ref, ks_ref, vs_ref,
             q4_ref, k4_ref, v4_ref, q16_ref, k16_ref, v16_ref, proj, slabs, stage, w_full):
        @pl.when(pl.program_id(0) == 0)
        def _():
            for d in range(N_DEV):
                w_full[:, IN_SHARD * d:IN_SHARD * (d + 1)] = w_ref[d]

        xx = x_ref[...]
        r = lax.rsqrt(jnp.mean(xx * xx, axis=-1, keepdims=True) + EPS)
        h = ((xx * r) * wn_ref[...]).astype(BF16)
        h1_ref[...] = h
        proj[...] = _dot(h, w_full[...])
        cos_t, sin_t, bdm = cos_ref[...], sin_ref[...], bd_ref[...]
        for grp, (raw_ref, rope_ref, nw_ref) in enumerate(((qraw_ref, q_ref, qnw_ref),
                                                           (kraw_ref, k_ref, knw_ref))):
            for p in range(4):
                cols = slice(D_GRP * grp + 128 * p, D_GRP * grp + 128 * (p + 1))
                t = proj[:, cols]
                raw_ref[:, 128 * p:128 * (p + 1)] = t
                yn, _ = _head_norm(t, nw_ref[...], bdm)
                roped = yn * cos_t + _swap_halves(yn) * sin_t
                slabs[4 * grp + p] = roped
                rope_ref[:, 128 * p:128 * (p + 1)] = roped.astype(BF16)
        for p in range(4):
            slabs[8 + p] = proj[:, 2 * D_GRP + 128 * p:2 * D_GRP + 128 * (p + 1)]
        for grp, ref in ((2, va_ref), (3, qs_ref), (4, ks_ref), (5, vs_ref)):
            ref[...] = proj[:, D_GRP * grp:D_GRP * (grp + 1)].astype(BF16)
        _split_views(slabs, stage, (q4_ref, k4_ref, v4_ref), (q16_ref, k16_ref, v16_ref))

    row = lambda w: pl.BlockSpec((tm, w), lambda i: (i, 0))
    grp_bf = jax.ShapeDtypeStruct((s_len, D_GRP), BF16)
    grp_f32 = jax.ShapeDtypeStruct((s_len, D_GRP), F32)
    ni = s_len // tm
    return _call_with_gather(
        body, shards, lambda: pl.program_id(0) == 0, lambda: pl.program_id(0) == ni - 2,
        lambda: pl.program_id(0) == ni - 1,
        name="attn_in", grid=(ni,),
        in_specs=[row(D_MODEL), _full((1, D_MODEL)),
                  pl.BlockSpec((N_DEV, D_MODEL, IN_SHARD), lambda i: (0, 0, 0)),
                  row(128), row(128), _full((1, 128)), _full((1, 128)), _full((128, 128))],
        out_specs=(row(D_MODEL),) + (row(D_GRP),) * 8 + (_view_spec(tm, 4),) * 3 + (_view_spec(tm, 16),) * 3,
        out_shape=(jax.ShapeDtypeStruct((s_len, D_MODEL), BF16), grp_f32, grp_f32) + (grp_bf,) * 6
        + (_view_shape(s_len, 4, BF16),) * 3 + (_view_shape(s_len, 16, BF16),) * 3,
        scratch_shapes=[pltpu.VMEM((tm, D_IN), F32), pltpu.VMEM((12, tm, 128), F32), pltpu.VMEM((12, tm, 128), F32),
                        pltpu.VMEM((D_MODEL, D_IN), BF16)],
    )(x2, wn1, a_g, cos2, sin2, qnw, knw, bd, *shards)


def _band_mask(n):
    i = lax.broadcasted_iota(jnp.int32, (2 * BLOCK, 2 * BLOCK), 0) & (BLOCK - 1)
    j = lax.broadcasted_iota(jnp.int32, (2 * BLOCK, 2 * BLOCK), 1)
    dist = i + BLOCK - j
    return (dist >= 0) & (dist <= BLOCK) & ((n - 1) * BLOCK + j >= 0)


def _stack_heads(t2, head0):
    return jnp.concatenate([jnp.where(head0, t2, 0), jnp.where(head0, 0, t2)], axis=0)


def _unstack_heads(t, head0):
    return jnp.where(head0, t[0:BLOCK], t[BLOCK:2 * BLOCK])


def _dil_fwd(qv, kv, vv, r, shards):
    sub_len = qv.shape[0]
    nb = sub_len // BLOCK

    qb = 2 if nb % 2 == 0 else 1

    def body(q_ref, kp_ref, kc_ref, vp_ref, vc_ref, o_ref, lse_ref):
        n = pl.program_id(1)
        lane = lax.broadcasted_iota(jnp.int32, (BLOCK, 128), 1)
        head0 = lane < HEAD_DIM
        units = [(b, slice(128 * p, 128 * (p + 1))) for b in range(qb) for p in range(4)]
        valid = [_band_mask(qb * n + b) for b in range(qb)]
        rows = [slice(BLOCK * b, BLOCK * (b + 1)) for b in range(qb)]

        def keys(prev_ref, cur_ref, b, c):
            before = prev_ref[:, c] if b == 0 else cur_ref[rows[b - 1], c]
            return jnp.concatenate([before, cur_ref[rows[b], c]], axis=0)

        qqs = [_stack_heads(q_ref[rows[b], c] * ATT_SCALE, head0) for b, c in units]
        kks = [keys(kp_ref, kc_ref, b, c) for b, c in units]
        vvs = [keys(vp_ref, vc_ref, b, c) for b, c in units]
        ss = [_dot_nt(qq, kk) for qq, kk in zip(qqs, kks)]
        prs, dens, lses = [], [], []
        for (b, _), s in zip(units, ss):
            s = jnp.where(valid[b], s, NEG)
            m = jnp.max(s, axis=-1, keepdims=True)
            pr = jnp.exp(s - m)
            den = jnp.sum(pr, axis=-1, keepdims=True)
            prs.append(pr.astype(BF16))
            dens.append(den)
            lses.append(m + jnp.log(den))
        pvs = [_dot(pr, vv2) for pr, vv2 in zip(prs, vvs)]
        for (b, c), pv, den, lse in zip(units, pvs, dens, lses):
            o_ref[rows[b], c] = _unstack_heads(pv / den, head0)
            lse_ref[rows[b], c] = _unstack_heads(jnp.broadcast_to(lse, (2 * BLOCK, 128)), head0)

    cur = pl.BlockSpec((qb * BLOCK, D_GRP), lambda c, n: (n, c))
    prev = pl.BlockSpec((BLOCK, D_GRP), lambda c, n: (jnp.maximum(qb * n - 1, 0), c))
    out = jax.ShapeDtypeStruct(qv.shape, F32)
    steps = nb // qb

    def at(t):
        return lambda: pl.program_id(0) * steps + pl.program_id(1) == t

    return _call_with_gather(
        body, shards, at(0), at((2 * r * steps) // 3), at(r * steps - 1),
        name=f"dil_fwd_r{r}", grid=(r, steps),
        in_specs=[cur, prev, cur, prev, cur], out_specs=(cur, cur), out_shape=(out, out),
    )(qv, kv, kv, vv, vv, *shards)


def _dil_bwd(qv, kv, vv, dov, lsev, deltav, r, rides):
    sub_len = qv.shape[0]
    nb = sub_len // BLOCK

    def body(q_ref, kp_ref, kc_ref, vp_ref, vc_ref, do_ref, lse_ref, dl_ref,
             dq_ref, dk_ref, dv_ref, dk_carry, dv_carry):
        n = pl.program_id(1)

        @pl.when(n == 0)
        def _():
            dk_carry[...] = jnp.zeros_like(dk_carry)
            dv_carry[...] = jnp.zeros_like(dv_carry)

        @pl.when(n < nb)
        def _():
            valid = _band_mask(n)
            lane = lax.broadcasted_iota(jnp.int32, (BLOCK, 128), 1)
            head0 = lane < HEAD_DIM
            pairs = [slice(128 * p, 128 * (p + 1)) for p in range(4)]
            qqs = [_stack_heads(q_ref[:, c] * ATT_SCALE, head0) for c in pairs]
            dos = [_stack_heads(do_ref[:, c], head0) for c in pairs]
            kks = [jnp.concatenate([kp_ref[:, c], kc_ref[:, c]], axis=0) for c in pairs]
            vvs = [jnp.concatenate([vp_ref[:, c], vc_ref[:, c]], axis=0) for c in pairs]
            ss = [_dot_nt(qq, kk) for qq, kk in zip(qqs, kks)]
            dps = [_dot_nt(do, vv2) for do, vv2 in zip(dos, vvs)]
            def softmax_terms(p):
                stats = []
                for ref in (lse_ref, dl_ref):
                    t2 = ref[:, pairs[p]]
                    stats.append(jnp.concatenate(
                        [jnp.sum(jnp.where(lane == 0, t2, 0.0), axis=-1, keepdims=True),
                         jnp.sum(jnp.where(lane == HEAD_DIM, t2, 0.0), axis=-1, keepdims=True)], axis=0))
                pr = jnp.where(valid, jnp.exp(jnp.minimum(ss[p] - stats[0], 0.0)), 0.0)
                return pr.astype(BF16), (pr * (dps[p] - stats[1])).astype(BF16)

            terms = [softmax_terms(p) for p in range(4)]
            dqs = [_dot(terms[p][1], kks[p]) for p in range(4)]
            dkks = [_dot_tn(terms[p][1], qqs[p]) for p in range(4)]
            dvvs = [_dot_tn(terms[p][0], dos[p]) for p in range(4)]
            for c, dq, dkk, dvv in zip(pairs, dqs, dkks, dvvs):
                dq_ref[:, c] = _unstack_heads(dq, head0) * ATT_SCALE
                dk_ref[:, c] = dk_carry[:, c] + dkk[:BLOCK]
                dv_ref[:, c] = dv_carry[:, c] + dvv[:BLOCK]
                dk_carry[:, c] = dkk[BLOCK:]
                dv_carry[:, c] = dvv[BLOCK:]

        @pl.when(n == nb)
        def _():
            dk_ref[...] = dk_carry[...]
            dv_ref[...] = dv_carry[...]

    last = nb - 1
    cur = pl.BlockSpec((BLOCK, D_GRP), lambda c, n: (jnp.minimum(n, last), c))
    prev = pl.BlockSpec((BLOCK, D_GRP), lambda c, n: (jnp.clip(n - 1, 0, last), c))
    out = jax.ShapeDtypeStruct(qv.shape, F32)
    return _call_with_exchange(
        body, rides,
        lambda: jnp.logical_and(pl.program_id(0) == 0, pl.program_id(1) == 0),
        lambda: jnp.logical_and(pl.program_id(0) == r - 1, pl.program_id(1) == nb),
        name=f"dil_bwd_r{r}", grid=(r, nb + 1),
        in_specs=[cur, prev, cur, prev, cur, cur, cur, cur],
        out_specs=(cur, prev, prev), out_shape=(out, out, out),
        scratch_shapes=[pltpu.VMEM((BLOCK, D_GRP), F32), pltpu.VMEM((BLOCK, D_GRP), F32)],
    )(qv, kv, kv, vv, vv, dov, lsev, deltav, *rides)


def _sb_fwd(qs, ks, vs, tri_suf, shards):
    s_len = qs.shape[0]
    t = SB_TILE
    nq = s_len // t

    npair = SB_PAIRS

    def body(q_ref, k_ref, v_ref, u_ref, o_ref, c_ref, qq, vt, acc, cf, csave):
        row = lax.broadcasted_iota(jnp.int32, (2 * t, t), 0) & (t - 1)
        col = lax.broadcasted_iota(jnp.int32, (2 * t, t), 1)
        diag_mask = col < row
        lane1 = lax.broadcasted_iota(jnp.int32, (t, 128), 1)
        head0 = lane1 < HEAD_DIM
        lane2 = lax.broadcasted_iota(jnp.int32, (2 * t, 128), 1)
        uu = u_ref[...]
        pr = range(npair)
        cols = [slice(128 * pp, 128 * (pp + 1)) for pp in pr]

        i = pl.program_id(1)

        @pl.when(i == 0)
        def _():
            def transpose_v(j, _):
                rows = pl.ds(pl.multiple_of(j * t, t), t)
                for pp in pr:
                    vt[pp, j] = v_ref[rows, cols[pp]].astype(F32).T.astype(BF16)
                return 0

            lax.fori_loop(0, nq, transpose_v, 0)

        for pp in pr:
            q2 = q_ref[:, cols[pp]] * ATT_SCALE
            qq[pp, 0:t, :] = jnp.where(head0, q2, 0)
            qq[pp, t:2 * t, :] = jnp.where(head0, 0, q2)
        acc[...] = jnp.zeros_like(acc)
        cf[...] = jnp.zeros_like(cf)
        csave[...] = jnp.full(csave.shape, 2.0 * SB_DEAD, F32)

        def tile(kb, diag):
            krows = pl.ds(pl.multiple_of(kb * t, t), t)
            zs = [_dot_nt(qq[pp], k_ref[krows, cols[pp]]) for pp in pr]
            lbk = [_log_sigmoid_pair(z) for z in zs]
            lks = [jnp.where(diag_mask, lk, 0.0) if diag else lk for _, lk in lbk]
            sufs = [_cumsum_mm(lk, uu) for lk in lks]
            carries = [cf[pp] for pp in pr]
            avs = []
            for pp in pr:
                a = jnp.exp(lbk[pp][0] + (sufs[pp] + jnp.concatenate([carries[pp]] * (t // 128), axis=1)))
                avs.append((jnp.where(diag_mask, a, 0.0) if diag else a).astype(BF16))
            pvs = [_dot_nt(vt[pp, kb], avs[pp]) for pp in pr]
            for pp in pr:
                acc[pp] += pvs[pp]
                csave[pp] = jnp.where(lane2 == kb, carries[pp], csave[pp])
                cf[pp] = carries[pp] + jnp.broadcast_to(jnp.sum(lks[pp], axis=-1, keepdims=True), (2 * t, 128))

        tile(i, True)

        def alive():
            return jnp.max(cf[...]) > SB_DEAD

        def k_block(state):
            kb, _ = state
            tile(kb, False)
            return kb - 1, alive()

        lax.while_loop(lambda state: jnp.logical_and(state[0] >= 0, state[1]), k_block, (i - 1, alive()))
        for pp in pr:
            o_ref[:, cols[pp]] = jnp.where(head0, acc[pp, :, 0:t].T, acc[pp, :, t:2 * t].T)
            c_ref[2 * pp] = csave[pp, 0:t, :]
            c_ref[2 * pp + 1] = csave[pp, t:2 * t, :]

    width = 128 * npair
    kv = pl.BlockSpec((s_len, width), lambda p, i: (0, p))
    qo = pl.BlockSpec((t, width), lambda p, i: (i, p))
    steps = 4 // npair

    def at(p, i):
        return lambda: jnp.logical_and(pl.program_id(0) == p, pl.program_id(1) == i)

    return _call_with_gather(
        body, shards, at(0, 0), at(steps - 1, (2 * nq) // 3), at(steps - 1, nq - 1),
        name="sb_fwd", grid=(steps, nq),
        in_specs=[qo, kv, kv, pl.BlockSpec((t, t), lambda p, i: (0, 0))],
        out_specs=(qo, pl.BlockSpec((2 * npair, t, 128), lambda p, i: (p, i, 0))),
        out_shape=(jax.ShapeDtypeStruct((s_len, D_GRP), F32),
                   jax.ShapeDtypeStruct((8, s_len, 128), F32)),
        scratch_shapes=[pltpu.VMEM((npair, 2 * t, 128), BF16), pltpu.VMEM((npair, nq, 128, t), BF16),
                        pltpu.VMEM((npair, 128, 2 * t), F32),
                        pltpu.VMEM((npair, 2 * t, 128), F32), pltpu.VMEM((npair, 2 * t, 128), F32)],
    )(qs, ks, vs, tri_suf, *shards)


def _sb_bwd(qs, ks, vs, dos, csaved, tri_suf, tri_pre, rides):
    s_len = qs.shape[0]
    t = SB_TILE
    nq = s_len // t

    npair = SB_BWD_PAIRS

    def body(q_ref, k_ref, v_ref, do_ref, c_ref, u_ref, p_ref, dq_ref, dk_ref, dv_ref,
             qq, dd, qqt, ddt, kt, dq_acc, dkt, dvt, cg):
        row = lax.broadcasted_iota(jnp.int32, (2 * t, t), 0) & (t - 1)
        col = lax.broadcasted_iota(jnp.int32, (2 * t, t), 1)
        diag_mask = col < row
        lane1 = lax.broadcasted_iota(jnp.int32, (t, 128), 1)
        head0 = lane1 < HEAD_DIM
        lane2 = lax.broadcasted_iota(jnp.int32, (2 * t, 128), 1)
        uu, pm = u_ref[...], p_ref[...]
        pr = range(npair)
        cols = [slice(128 * pp, 128 * (pp + 1)) for pp in pr]
        i = pl.program_id(1)

        @pl.when(i == 0)
        def _():
            dkt[...] = jnp.zeros_like(dkt)
            dvt[...] = jnp.zeros_like(dvt)

            def transpose_k(j, _):
                rows = pl.ds(pl.multiple_of(j * t, t), t)
                for pp in pr:
                    kt[pp, j] = k_ref[rows, cols[pp]].astype(F32).T.astype(BF16)
                return 0

            lax.fori_loop(0, nq, transpose_k, 0)

        for pp in pr:
            q2 = q_ref[:, cols[pp]].astype(F32) * ATT_SCALE
            do2 = do_ref[:, cols[pp]].astype(F32)
            for src, nat, tr in ((q2, qq, qqt), (do2, dd, ddt)):
                stacked = jnp.concatenate([jnp.where(head0, src, 0.0), jnp.where(head0, 0.0, src)], axis=0)
                nat[pp] = stacked.astype(BF16)
                tr[pp] = stacked.T.astype(BF16)
        dq_acc[...] = jnp.zeros_like(dq_acc)
        cg[...] = jnp.zeros_like(cg)

        def tile(kb, diag):
            krows = pl.ds(pl.multiple_of(kb * t, t), t)
            zs = [_dot_nt(qq[pp], k_ref[krows, cols[pp]]) for pp in pr]
            das = [_dot_nt(dd[pp], v_ref[krows, cols[pp]]) for pp in pr]
            lbk = [_log_sigmoid_pair(z) for z in zs]
            lks = [jnp.where(diag_mask, lk, 0.0) if diag else lk for _, lk in lbk]
            sufs = [_cumsum_mm(lk, uu) for lk in lks]
            avs, gs = [], []
            for pp in pr:
                cs = jnp.concatenate([c_ref[2 * pp], c_ref[2 * pp + 1]], axis=0)
                cf = jnp.sum(jnp.where(lane2 == kb, cs, 0.0), axis=-1, keepdims=True)
                a = jnp.exp(lbk[pp][0] + (sufs[pp] + cf))
                a = jnp.where(diag_mask, a, 0.0) if diag else a
                avs.append(a.astype(BF16))
                gs.append(a * das[pp])
            gpres = [_cumsum_mm(g, pm) for g in gs]
            dzs = []
            for pp in pr:
                carry = cg[pp]
                beta = jnp.exp(lbk[pp][0])
                dz = gs[pp] - beta * (gs[pp] + (gpres[pp] + jnp.concatenate([carry] * (t // 128), axis=1)))
                dzs.append((jnp.where(diag_mask, dz, 0.0) if diag else dz).astype(BF16))
                cg[pp] = carry + jnp.broadcast_to(jnp.sum(gs[pp], axis=-1, keepdims=True), (2 * t, 128))
            dqs = [_dot_nt(kt[pp, kb], dzs[pp]) for pp in pr]
            dks = [_dot(qqt[pp], dzs[pp]) for pp in pr]
            dvs = [_dot(ddt[pp], avs[pp]) for pp in pr]
            for pp in pr:
                dq_acc[pp] += dqs[pp]
                dkt[pp, kb] += dks[pp]
                dvt[pp, kb] += dvs[pp]

        def k_block(kb, _):
            tile(kb, False)
            return 0

        col_max = jnp.max(jnp.max(c_ref[...], axis=0), axis=0, keepdims=True)
        lane_row = lax.broadcasted_iota(jnp.int32, (1, 128), 1)
        n_live = jnp.sum(jnp.where(jnp.logical_and(col_max > SB_DEAD, lane_row < i), 1, 0))
        lax.fori_loop(i - n_live, i, k_block, 0)
        tile(i, True)
        for pp in pr:
            dq_ref[:, cols[pp]] = jnp.where(head0, dq_acc[pp, :, 0:t].T, dq_acc[pp, :, t:2 * t].T) * ATT_SCALE

        @pl.when(i == nq - 1)
        def _():
            def untranspose(j, _):
                rows = pl.ds(pl.multiple_of(j * t, t), t)
                for pp in pr:
                    dk_ref[rows, cols[pp]] = dkt[pp, j].T
                    dv_ref[rows, cols[pp]] = dvt[pp, j].T
                return 0

            lax.fori_loop(0, nq, untranspose, 0)

    width = 128 * npair
    kv = pl.BlockSpec((s_len, width), lambda p, i: (0, p))
    qo = pl.BlockSpec((t, width), lambda p, i: (i, p))
    tri = pl.BlockSpec((t, t), lambda p, i: (0, 0))
    out = jax.ShapeDtypeStruct((s_len, D_GRP), F32)
    steps = 4 // npair
    return _call_with_exchange(
        body, rides,
        lambda: jnp.logical_and(pl.program_id(0) == 0, pl.program_id(1) == 0),
        lambda: jnp.logical_and(pl.program_id(0) == steps - 1, pl.program_id(1) == nq - 1),
        name="sb_bwd", grid=(steps, nq),
        in_specs=[qo, kv, kv, qo, pl.BlockSpec((2 * npair, t, 128), lambda p, i: (p, i, 0)), tri, tri],
        out_specs=(qo, kv, kv), out_shape=(out, out, out),
        scratch_shapes=[pltpu.VMEM((npair, 2 * t, 128), BF16), pltpu.VMEM((npair, 2 * t, 128), BF16),
                        pltpu.VMEM((npair, 128, 2 * t), BF16), pltpu.VMEM((npair, 128, 2 * t), BF16),
                        pltpu.VMEM((npair, nq, 128, t), BF16),
                        pltpu.VMEM((npair, 128, 2 * t), F32),
                        pltpu.VMEM((npair, nq, 128, t), F32), pltpu.VMEM((npair, nq, 128, t), F32),
                        pltpu.VMEM((npair, 2 * t, 128), F32)],
    )(qs, ks, vs, dos, csaved, tri_suf, tri_pre, *rides)


def _attn_out(o_b, lse_b, o_sb, x2, wdil, wsb, out_g, shards):
    s_len = x2.shape[0]
    tm = ROW_TILE

    def body(o1_ref, l1_ref, o4_ref, l4_ref, o16_ref, l16_ref, osb_ref, x_ref, wdil_ref, wsb_ref, w_ref,
             odil_ref, lse_ref, lse4_ref, lse16_ref, mixed_ref, x1_ref, stage, nat4, nat16):
        _merge_views((o4_ref, l4_ref), (o16_ref, l16_ref), stage, nat4, nat16)
        os_ = (o1_ref[...], _slab_group(nat4, 0), _slab_group(nat16, 0))
        ls = (l1_ref[...], _slab_group(nat4, 1), _slab_group(nat16, 1))
        mx = jnp.maximum(jnp.maximum(ls[0], ls[1]), ls[2])
        es = [jnp.exp(l - mx) for l in ls]
        den = es[0] + es[1] + es[2]
        o_dil = (es[0] * os_[0] + es[1] * os_[1] + es[2] * os_[2]) / den
        odil_ref[...] = o_dil
        lse = mx + jnp.log(den)
        lse_ref[...] = lse
        for p in range(4):
            nat4[p] = lse[:, 128 * p:128 * (p + 1)]
        _split_views(nat4.at[0:4], stage.at[0:4], (lse4_ref,), (lse16_ref,))
        halves = []
        for t, w_r in ((o_dil, wdil_ref), (osb_ref[...], wsb_ref)):
            r = lax.rsqrt(jnp.mean(t * t, axis=-1, keepdims=True) + EPS)
            halves.append(((t * r) * w_r[...]).astype(BF16))
        mixed = jnp.concatenate(halves, axis=1)
        mixed_ref[...] = mixed
        w = w_ref[...].reshape(D_MODEL, D_MODEL)
        x1_ref[...] = x_ref[...] + _dot(mixed, w)

    row = lambda w: pl.BlockSpec((tm, w), lambda i: (i, 0))
    ni = s_len // tm
    return _call_with_gather(
        body, shards, lambda: pl.program_id(0) == 0, lambda: pl.program_id(0) == ni - 2,
        lambda: pl.program_id(0) == ni - 1,
        name="attn_out", grid=(ni,),
        in_specs=[row(D_GRP)] * 2 + [_view_spec(tm, 4)] * 2 + [_view_spec(tm, 16)] * 2
        + [row(D_GRP), row(D_MODEL), _full((1, D_GRP)), _full((1, D_GRP)), _full((N_DEV, OUT_SHARD, D_MODEL))],
        out_specs=(row(D_GRP), row(D_GRP), _view_spec(tm, 4), _view_spec(tm, 16), row(D_MODEL), row(D_MODEL)),
        out_shape=(jax.ShapeDtypeStruct((s_len, D_GRP), F32), jax.ShapeDtypeStruct((s_len, D_GRP), F32),
                   _view_shape(s_len, 4, F32), _view_shape(s_len, 16, F32),
                   jax.ShapeDtypeStruct((s_len, D_MODEL), BF16), jax.ShapeDtypeStruct((s_len, D_MODEL), F32)),
        scratch_shapes=[pltpu.VMEM((8, tm, 128), F32)] * 3,
    )(o_b[0], lse_b[0], o_b[1], lse_b[1], o_b[2], lse_b[2], o_sb, x2, wdil, wsb, out_g, *shards)


def _two_shards(w_ref):
    return w_ref[...].reshape(FF_BLOCK, D_MODEL)


def _ffn_fwd(x1, wn2, tgt, gate_g, up_g, down_g):
    s_len = x1.shape[0]
    tm = ROW_TILE
    ni = s_len // tm

    def body(x_ref, wn_ref, t_ref, wg_ref, wu_ref, wd_ref, g_ref, u_ref, h2_ref, dy_ref, loss_ref, acc):
        j = pl.program_id(1)

        @pl.when(j == 0)
        def _():
            xx = x_ref[...]
            r = lax.rsqrt(jnp.mean(xx * xx, axis=-1, keepdims=True) + EPS)
            h2_ref[...] = ((xx * r) * wn_ref[...]).astype(BF16)
            acc[...] = jnp.zeros_like(acc)

        h = h2_ref[...]
        g = _dot_nt(h, _two_shards(wg_ref))
        u = _dot_nt(h, _two_shards(wu_ref))
        g_ref[...] = g
        u_ref[...] = u
        act = (g * (1.0 / (1.0 + jnp.exp(-g)))) * u
        acc[...] += _dot(act.astype(BF16), _two_shards(wd_ref))

        @pl.when(j == FF_STEPS - 1)
        def _():
            err = (x_ref[...] + acc[...]) - t_ref[...]
            dy_ref[...] = err * (1.0 / D_MODEL)
            part = 0.5 * jnp.sum(jnp.mean(err * err, axis=-1, keepdims=True))
            loss_ref[...] = jnp.full((8, 128), part, F32)

    row = pl.BlockSpec((tm, D_MODEL), lambda i, j: (i, 0))
    hid = pl.BlockSpec((tm, FF_BLOCK), lambda i, j: (i, j))
    return pl.pallas_call(
        body, name="ffn_fwd", grid=(ni, FF_STEPS),
        in_specs=[row, pl.BlockSpec((1, D_MODEL), lambda i, j: (0, 0)), row,
                  pl.BlockSpec((2, FF_PAD, D_MODEL), lambda i, j: (j, 0, 0)),
                  pl.BlockSpec((2, FF_PAD, D_MODEL), lambda i, j: (j, 0, 0)),
                  pl.BlockSpec((2, FF_PAD, D_MODEL), lambda i, j: (j, 0, 0))],
        out_specs=(hid, hid, row, row, pl.BlockSpec((8, 128), lambda i, j: (i, 0))),
        out_shape=(jax.ShapeDtypeStruct((s_len, N_DEV * FF_PAD), F32),
                   jax.ShapeDtypeStruct((s_len, N_DEV * FF_PAD), F32),
                   jax.ShapeDtypeStruct((s_len, D_MODEL), BF16),
                   jax.ShapeDtypeStruct((s_len, D_MODEL), F32),
                   jax.ShapeDtypeStruct((ni * 8, 128), F32)),
        scratch_shapes=[pltpu.VMEM((tm, D_MODEL), F32)],
        compiler_params=_params(),
    )(x1, wn2, tgt, gate_g, up_g, down_g)


def _ffn_bwd_dx(dy, g, u, gate_g, up_g, down_g):
    s_len = dy.shape[0]
    tm = ROW_TILE

    def body(dy_ref, g_ref, u_ref, wg_ref, wu_ref, wd_ref, dg_ref, du_ref, act_ref, dh_ref, acc):
        j = pl.program_id(1)

        @pl.when(j == 0)
        def _():
            acc[...] = jnp.zeros_like(acc)

        halves = [slice(0, tm // 2), slice(tm // 2, tm)]
        wd, wg, wu = _two_shards(wd_ref), _two_shards(wg_ref), _two_shards(wu_ref)
        das = [_dot_nt(dy_ref[rows, :].astype(BF16), wd) for rows in halves]

        def elementwise(rows, da):
            gg, uu = g_ref[rows, :], u_ref[rows, :]
            sig = 1.0 / (1.0 + jnp.exp(-gg))
            silu = gg * sig
            act_ref[rows, :] = (silu * uu).astype(BF16)
            du = (da * silu).astype(BF16)
            dg = (da * uu * (sig * (1.0 + gg * (1.0 - sig)))).astype(BF16)
            du_ref[rows, :] = du
            dg_ref[rows, :] = dg
            return dg, du

        dg0, du0 = elementwise(halves[0], das[0])
        acc[halves[0], :] += _dot(dg0, wg) + _dot(du0, wu)
        dg1, du1 = elementwise(halves[1], das[1])
        acc[halves[1], :] += _dot(dg1, wg) + _dot(du1, wu)

        @pl.when(j == FF_STEPS - 1)
        def _():
            dh_ref[...] = acc[...]

    row = pl.BlockSpec((tm, D_MODEL), lambda i, j: (i, 0))
    hid = pl.BlockSpec((tm, FF_BLOCK), lambda i, j: (i, j))
    hid_bf = jax.ShapeDtypeStruct((s_len, N_DEV * FF_PAD), BF16)
    return pl.pallas_call(
        body, name="ffn_bwd_dx", grid=(s_len // tm, FF_STEPS),
        in_specs=[row, hid, hid,
                  pl.BlockSpec((2, FF_PAD, D_MODEL), lambda i, j: (j, 0, 0)),
                  pl.BlockSpec((2, FF_PAD, D_MODEL), lambda i, j: (j, 0, 0)),
                  pl.BlockSpec((2, FF_PAD, D_MODEL), lambda i, j: (j, 0, 0))],
        out_specs=(hid, hid, hid, row),
        out_shape=(hid_bf, hid_bf, hid_bf, jax.ShapeDtypeStruct((s_len, D_MODEL), F32)),
        scratch_shapes=[pltpu.VMEM((tm, D_MODEL), F32)],
        compiler_params=_params(),
    )(dy, g, u, gate_g, up_g, down_g)


def _ffn_bwd_dw(h2, dy, dg, du, act):
    s_len = h2.shape[0]
    tm = DW_ROW_TILE
    ni = s_len // tm

    half = FF_PAD // 2

    def body(h_ref, dy_ref, dg_ref, du_ref, act_ref, dwg_ref, dwu_ref, dwd_lo_ref, dwd_hi_ref, ag, au, ad):
        i = pl.program_id(1)

        @pl.when(i == 0)
        def _():
            ag[...] = jnp.zeros_like(ag)
            au[...] = jnp.zeros_like(au)
            ad[...] = jnp.zeros_like(ad)

        h = h_ref[...]
        ag[...] += _dot_tn(dg_ref[...], h)
        au[...] += _dot_tn(du_ref[...], h)
        ad[...] += _dot_tn(act_ref[...], dy_ref[...].astype(BF16))

        @pl.when(i == ni - 1)
        def _():
            for acc_ref, out_ref in ((ag, dwg_ref), (au, dwu_ref)):
                out_ref[...] = acc_ref[...].astype(BF16).reshape(2, FF_PAD, D_MODEL)
            for dev in range(2):
                dwd_lo_ref[dev] = ad[FF_PAD * dev:FF_PAD * dev + half, :].astype(BF16)
                dwd_hi_ref[dev] = ad[FF_PAD * dev + half:FF_PAD * (dev + 1), :].astype(BF16)

    row = pl.BlockSpec((tm, D_MODEL), lambda j, i: (i, 0))
    hid = pl.BlockSpec((tm, FF_BLOCK), lambda j, i: (i, j))
    row_w = pl.BlockSpec((2, FF_PAD, D_MODEL), lambda j, i: (j, 0, 0))
    half_w = pl.BlockSpec((2, half, D_MODEL), lambda j, i: (j, 0, 0))
    grad = jax.ShapeDtypeStruct((N_DEV, FF_PAD, D_MODEL), BF16)
    half_grad = jax.ShapeDtypeStruct((N_DEV, half, D_MODEL), BF16)
    return pl.pallas_call(
        body, name="ffn_bwd_dw", grid=(FF_STEPS, ni),
        in_specs=[row, row, hid, hid, hid], out_specs=(row_w, row_w, half_w, half_w),
        out_shape=(grad, grad, half_grad, half_grad),
        scratch_shapes=[pltpu.VMEM((FF_BLOCK, D_MODEL), F32)] * 3,
        compiler_params=_params(),
    )(h2, dy, dg, du, act)


def _rms_bwd(dy, t, w):
    r = lax.rsqrt(jnp.mean(t * t, axis=-1, keepdims=True) + EPS)
    gw = dy * w
    dt = r * (gw - t * ((r * r) * jnp.mean(gw * t, axis=-1, keepdims=True)))
    return dt, dy * t * r


def _attn_out_bwd(dy, dh2, x1, wn2, b_g, mixed, o_dil, o_sb, wdil, wsb, bd512):
    s_len = dy.shape[0]
    tm = ROW_TILE
    ni = s_len // tm

    def body(dy_ref, dh_ref, x1_ref, wn_ref, w_ref, mixed_ref, odil_ref, osb_ref, wdil_ref, wsb_ref, bd_ref,
             dx1_ref, dodil_ref, delta_ref, dosb_ref, dwout_ref, dwn_ref, dwdil_ref, dwsb_ref,
             do4_ref, dl4_ref, do16_ref, dl16_ref, wacc, both, stage):
        i = pl.program_id(0)

        @pl.when(i == 0)
        def _():
            wacc[...] = jnp.zeros_like(wacc)
            dwn_ref[...] = jnp.zeros_like(dwn_ref)
            dwdil_ref[...] = jnp.zeros_like(dwdil_ref)
            dwsb_ref[...] = jnp.zeros_like(dwsb_ref)

        dnorm, dw_rows = _rms_bwd(dh_ref[...], x1_ref[...], wn_ref[...])
        dx1 = dy_ref[...] + dnorm
        dx1_ref[...] = dx1
        dwn_ref[...] += jnp.sum(dw_rows, axis=0, keepdims=True)
        dx1b = dx1.astype(BF16)
        w = w_ref[...].reshape(D_MODEL, D_MODEL)
        dmixed = _dot_nt(dx1b, w)
        wacc[...] += _dot_tn(mixed_ref[...], dx1b)
        o_dil = odil_ref[...]
        d_odil, dw_rows = _rms_bwd(dmixed[:, :D_GRP], o_dil, wdil_ref[...])
        dwdil_ref[...] += jnp.sum(dw_rows, axis=0, keepdims=True)
        dodil_ref[...] = d_odil.astype(BF16)
        delta = _mm_split(d_odil * o_dil, bd_ref[...])
        delta_ref[...] = delta
        for p in range(4):
            both[p] = d_odil[:, 128 * p:128 * (p + 1)]
            both[4 + p] = delta[:, 128 * p:128 * (p + 1)]
        _split_views(both, stage, (do4_ref, dl4_ref), (do16_ref, dl16_ref))
        d_osb, dw_rows = _rms_bwd(dmixed[:, D_GRP:], osb_ref[...], wsb_ref[...])
        dwsb_ref[...] += jnp.sum(dw_rows, axis=0, keepdims=True)
        dosb_ref[...] = d_osb.astype(BF16)

        @pl.when(i == ni - 1)
        def _():
            dwout_ref[...] = wacc[...].astype(BF16).reshape(N_DEV, OUT_SHARD, D_MODEL)

    row = lambda w: pl.BlockSpec((tm, w), lambda i: (i, 0))
    return pl.pallas_call(
        body, name="attn_out_bwd", grid=(ni,),
        in_specs=[row(D_MODEL), row(D_MODEL), row(D_MODEL), _full((1, D_MODEL)),
                  _full((N_DEV, OUT_SHARD, D_MODEL)),
                  row(D_MODEL), row(D_GRP), row(D_GRP), _full((1, D_GRP)), _full((1, D_GRP)),
                  _full((D_GRP, D_GRP))],
        out_specs=(row(D_MODEL), row(D_GRP), row(D_GRP), row(D_GRP),
                   _full((N_DEV, OUT_SHARD, D_MODEL)), _full((1, D_MODEL)), _full((1, D_GRP)), _full((1, D_GRP)),
                   _view_spec(tm, 4), _view_spec(tm, 4), _view_spec(tm, 16), _view_spec(tm, 16)),
        out_shape=(jax.ShapeDtypeStruct((s_len, D_MODEL), F32), jax.ShapeDtypeStruct((s_len, D_GRP), BF16),
                   jax.ShapeDtypeStruct((s_len, D_GRP), F32), jax.ShapeDtypeStruct((s_len, D_GRP), BF16),
                   jax.ShapeDtypeStruct((N_DEV, OUT_SHARD, D_MODEL), BF16),
                   jax.ShapeDtypeStruct((1, D_MODEL), F32), jax.ShapeDtypeStruct((1, D_GRP), F32),
                   jax.ShapeDtypeStruct((1, D_GRP), F32),
                   _view_shape(s_len, 4, BF16), _view_shape(s_len, 4, F32),
                   _view_shape(s_len, 16, BF16), _view_shape(s_len, 16, F32)),
        scratch_shapes=[pltpu.VMEM((D_MODEL, D_MODEL), F32), pltpu.VMEM((8, tm, 128), F32),
                        pltpu.VMEM((8, tm, 128), F32)],
        compiler_params=_params(),
    )(dy, dh2, x1, wn2, b_g, mixed, o_dil, o_sb, wdil, wsb, bd512)


def _qkv_bwd(dq_b, dk_b, dv_b, dqs, dks, dvs, qraw, kraw, cos2, sin2, qnw, knw, bd, rides):
    s_len = qraw.shape[0]
    tm = ROW_TILE
    ni = s_len // tm

    def body(dq1, dk1, dv1, dq4, dk4, dv4, dq16, dk16, dv16, dqs_ref, dks_ref, dvs_ref,
             qraw_ref, kraw_ref, cos_ref, sin_ref, qnw_ref, knw_ref, bd_ref,
             dproj_ref, dqn_ref, dkn_ref, stage, nat4, nat16):
        i = pl.program_id(0)

        @pl.when(i == 0)
        def _():
            dqn_ref[...] = jnp.zeros_like(dqn_ref)
            dkn_ref[...] = jnp.zeros_like(dkn_ref)

        _merge_views((dq4, dk4, dv4), (dq16, dk16, dv16), stage, nat4, nat16)
        cos_t, sin_t, bdm = cos_ref[...], sin_ref[...], bd_ref[...]
        for grp, (part1, raw_ref, nw_ref, dn_ref) in enumerate(((dq1, qraw_ref, qnw_ref, dqn_ref),
                                                                (dk1, kraw_ref, knw_ref, dkn_ref))):
            dn_acc = 0.0
            for p in range(4):
                cols = slice(128 * p, 128 * (p + 1))
                d_rope = part1[:, cols] + nat4[4 * grp + p] + nat16[4 * grp + p]
                d_norm = d_rope * cos_t + _swap_halves(d_rope * sin_t)
                t = raw_ref[:, cols]
                w = nw_ref[...]
                r = lax.rsqrt(_mm_split(t * t, bdm) * (1.0 / HEAD_DIM) + EPS)
                gw = d_norm * w
                corr = _mm_split(gw * t, bdm) * (1.0 / HEAD_DIM)
                dt = r * (gw - t * ((r * r) * corr))
                dn_acc = dn_acc + jnp.sum(d_norm * t * r, axis=0, keepdims=True)
                dproj_ref[:, D_GRP * grp + 128 * p:D_GRP * grp + 128 * (p + 1)] = dt.astype(BF16)
            dn_ref[...] += dn_acc
        dproj_ref[:, 2 * D_GRP:3 * D_GRP] = (dv1[...] + _slab_group(nat4, 2) + _slab_group(nat16, 2)).astype(BF16)
        dproj_ref[:, 3 * D_GRP:4 * D_GRP] = dqs_ref[...].astype(BF16)
        dproj_ref[:, 4 * D_GRP:5 * D_GRP] = dks_ref[...].astype(BF16)
        dproj_ref[:, 5 * D_GRP:6 * D_GRP] = dvs_ref[...].astype(BF16)

    row = lambda w: pl.BlockSpec((tm, w), lambda i: (i, 0))
    return _call_with_exchange(
        body, rides, lambda: pl.program_id(0) == 0, lambda: pl.program_id(0) == ni - 1,
        name="qkv_bwd", grid=(ni,),
        in_specs=[row(D_GRP)] * 3 + [_view_spec(tm, 4)] * 3 + [_view_spec(tm, 16)] * 3 + [row(D_GRP)] * 5
        + [row(128), row(128), _full((1, 128)), _full((1, 128)), _full((128, 128))],
        out_specs=(row(D_IN), _full((1, 128)), _full((1, 128))),
        out_shape=(jax.ShapeDtypeStruct((s_len, D_IN), BF16), jax.ShapeDtypeStruct((1, 128), F32),
                   jax.ShapeDtypeStruct((1, 128), F32)),
        scratch_shapes=[pltpu.VMEM((12, tm, 128), F32)] * 3,
    )(dq_b[0], dk_b[0], dv_b[0], dq_b[1], dk_b[1], dv_b[1], dq_b[2], dk_b[2], dv_b[2],
      dqs, dks, dvs, qraw, kraw, cos2, sin2, qnw, knw, bd, *rides)


def _in_bwd_dx(dproj, a_g, x2, dx1, wn1, rides):
    s_len = x2.shape[0]
    tm = ROW_TILE
    ni = s_len // tm

    def body(dp_ref, w_ref, x_ref, dx1_ref, wn_ref, gx_ref, dwn_ref, w_full):
        i = pl.program_id(0)

        @pl.when(i == 0)
        def _():
            dwn_ref[...] = jnp.zeros_like(dwn_ref)
            for d in range(N_DEV):
                w_full[:, IN_SHARD * d:IN_SHARD * (d + 1)] = w_ref[d]

        dh = _dot_nt(dp_ref[...], w_full[...])
        dnorm, dw_rows = _rms_bwd(dh, x_ref[...], wn_ref[...])
        gx_ref[...] = dx1_ref[...] + dnorm
        dwn_ref[...] += jnp.sum(dw_rows, axis=0, keepdims=True)

    row = lambda w: pl.BlockSpec((tm, w), lambda i: (i, 0))
    return _call_with_exchange(
        body, rides, lambda: pl.program_id(0) == 0, lambda: pl.program_id(0) == ni - 1,
        name="in_bwd_dx", grid=(ni,),
        in_specs=[row(D_IN), pl.BlockSpec((N_DEV, D_MODEL, IN_SHARD), lambda i: (0, 0, 0)),
                  row(D_MODEL), row(D_MODEL), _full((1, D_MODEL))],
        out_specs=(row(D_MODEL), _full((1, D_MODEL))),
        out_shape=(jax.ShapeDtypeStruct((s_len, D_MODEL), F32), jax.ShapeDtypeStruct((1, D_MODEL), F32)),
        scratch_shapes=[pltpu.VMEM((D_MODEL, D_IN), BF16)],
    )(dproj, a_g, x2, dx1, wn1, *rides)


def _in_bwd_dw(h1, dproj):
    s_len = h1.shape[0]
    tm = DW_ROW_TILE
    ni = s_len // tm

    def body(h_ref, dp_ref, dw_ref, acc):
        i = pl.program_id(1)

        @pl.when(i == 0)
        def _():
            acc[...] = jnp.zeros_like(acc)

        acc[...] += _dot_tn(h_ref[...], dp_ref[...])

        @pl.when(i == ni - 1)
        def _():
            for half in range(2):
                dw_ref[half] = acc[:, IN_SHARD * half:IN_SHARD * (half + 1)].astype(BF16)

    return pl.pallas_call(
        body, name="in_bwd_dw", grid=(N_DEV // 2, ni),
        in_specs=[pl.BlockSpec((tm, D_MODEL), lambda d, i: (i, 0)),
                  pl.BlockSpec((tm, 2 * IN_SHARD), lambda d, i: (i, d))],
        out_specs=pl.BlockSpec((2, D_MODEL, IN_SHARD), lambda d, i: (d, 0, 0)),
        out_shape=jax.ShapeDtypeStruct((N_DEV, D_MODEL, IN_SHARD), BF16),
        scratch_shapes=[pltpu.VMEM((D_MODEL, 2 * IN_SHARD), F32)],
        compiler_params=_params(),
    )(h1, dproj)


def _adamw(recv, w, m, v, recv_hi=None):
    rows, cols = w.shape
    tr = next((t for t in (128, 32) if rows % t == 0), rows)
    recvs = [recv] if recv_hi is None else [recv, recv_hi]
    lo_tiles = recv.shape[1] // tr

    def body(*refs):
        p_refs = refs[:len(recvs)]
        w_ref, m_ref, v_ref, g_ref, d_ref, nm_ref, nv_ref = refs[len(recvs):]

        def slot(s):
            if len(p_refs) == 1:
                return p_refs[0][s].astype(F32)
            return jnp.where(pl.program_id(0) < lo_tiles, p_refs[0][s], p_refs[1][s]).astype(F32)

        g = slot(0)
        for s in range(1, N_DEV):
            g = g + slot(s)
        m_new = ADAM_B1 * m_ref[...] + (1.0 - ADAM_B1) * g
        v_new = ADAM_B2 * v_ref[...] + (1.0 - ADAM_B2) * (g * g)
        m_hat = m_new / (1.0 - ADAM_B1 ** ADAM_STEP)
        v_hat = v_new / (1.0 - ADAM_B2 ** ADAM_STEP)
        g_ref[...] = g
        d_ref[...] = -ADAM_LR * (m_hat / (jnp.sqrt(v_hat) + ADAM_EPS) + ADAM_WD * w_ref[...])
        nm_ref[...] = m_new
        nv_ref[...] = v_new

    blk = pl.BlockSpec((tr, cols), lambda i: (i, 0))
    out = jax.ShapeDtypeStruct((rows, cols), F32)
    return pl.pallas_call(
        body, name=f"adamw_{rows}x{cols}", grid=(rows // tr,),
        in_specs=([pl.BlockSpec((N_DEV, tr, cols), lambda i: (0, i, 0))] if recv_hi is None else
                  [pl.BlockSpec((N_DEV, tr, cols), lambda i: (0, jnp.minimum(i, lo_tiles - 1), 0)),
                   pl.BlockSpec((N_DEV, tr, cols), lambda i: (0, jnp.maximum(i - lo_tiles, 0), 0))])
        + [blk, blk, blk],
        out_specs=(blk,) * 4, out_shape=(out,) * 4,
        compiler_params=_params(),
    )(*recvs, w, m, v)


def _rope_tables(s_len):
    pos = jnp.arange(s_len, dtype=F32)
    inv_freq = ROPE_THETA ** (-jnp.arange(0, HEAD_DIM, 2, dtype=F32) / HEAD_DIM)
    ang = pos[:, None] * inv_freq[None, :]
    cos, sin = jnp.cos(ang), jnp.sin(ang)
    cos2 = jnp.concatenate([cos, cos, cos, cos], axis=1)
    sin2 = jnp.concatenate([-sin, sin, -sin, sin], axis=1)
    return cos2, sin2


def _block_diag_ones(n):
    i = jnp.arange(n)
    return (i[:, None] // HEAD_DIM == i[None, :] // HEAD_DIM).astype(BF16)


def _pad_cols(t):
    return jnp.pad(t, ((0, 0), (0, FF_PAD - FF_SHARD)))


def _pad_rows(t):
    return jnp.pad(t, ((0, FF_PAD - FF_SHARD), (0, 0)))


LOSS_ROW = 26


def _pack_small(n1, n2, ndil, nsb, nq, nk, scalar=None):
    pad = lambda t: jnp.pad(t.reshape(1, -1), ((0, 0), (0, 128 - t.size)))
    last = jnp.zeros((1, 128), F32) if scalar is None else pad(scalar)
    rows = [n1.reshape(8, 128), n2.reshape(8, 128), ndil.reshape(4, 128), nsb.reshape(4, 128),
            pad(nq), pad(nk), last, jnp.zeros((5, 128), F32)]
    return jnp.concatenate(rows, axis=0)


def _unpack_small(t):
    return (t[0:8].reshape(1, D_MODEL), t[8:16].reshape(1, D_MODEL), t[16:20].reshape(1, D_GRP),
            t[20:24].reshape(1, D_GRP), t[24:25, :HEAD_DIM], t[25:26, :HEAD_DIM])


def kernel(x, attn_norm_w, w_in, q_norm_w, k_norm_w, dil_out_norm_w, sb_out_norm_w, w_out, ffn_norm_w, w_gate, w_up, w_down, loss_target, m_attn_norm_w, m_w_in, m_q_norm_w, m_k_norm_w, m_dil_out_norm_w, m_sb_out_norm_w, m_w_out, m_ffn_norm_w, m_w_gate, m_w_up, m_w_down, v_attn_norm_w, v_w_in, v_q_norm_w, v_k_norm_w, v_dil_out_norm_w, v_sb_out_norm_w, v_w_out, v_ffn_norm_w, v_w_gate, v_w_up, v_w_down):
    s_len = x.shape[1]
    x2, tgt = x[0], loss_target[0]

    (a_g,) = _gather_weights([w_in[0].astype(BF16)])
    gate_loc = _pad_cols(w_gate[0]).T.astype(BF16)
    up_loc = _pad_cols(w_up[0]).T.astype(BF16)
    down_loc = _pad_rows(w_down[0]).astype(BF16)
    out_loc = w_out[0].astype(BF16)

    cos2, sin2 = _rope_tables(s_len)
    bd128, bd512 = _block_diag_ones(128), _block_diag_ones(D_GRP)
    idx = jnp.arange(SB_TILE)
    tri_suf = (idx[:, None] > idx[None, :]).astype(BF16)
    tri_pre = (idx[:, None] < idx[None, :]).astype(BF16)
    qnw2 = jnp.concatenate([q_norm_w, q_norm_w], axis=1)
    knw2 = jnp.concatenate([k_norm_w, k_norm_w], axis=1)

    (h1, qraw, kraw, q, k, va, qs, ks, vs, q4, k4, v4, q16, k16, v16,
     gate_g) = _attn_in(x2, attn_norm_w, a_g, cos2, sin2, qnw2, knw2, bd128, shards=[gate_loc])
    qkv_views = {1: (q, k, va), 4: (q4, k4, v4), 16: (q16, k16, v16)}
    fwd_riders = {1: [out_loc], 4: [], 16: []}
    o_b, lse_b, gathered = [], [], {}
    for r in DILATIONS:
        o, lse, *gathered[r] = _dil_fwd(*qkv_views[r], r, shards=fwd_riders[r])
        o_b.append(o)
        lse_b.append(lse)
    (out_g,) = gathered[1]
    o_sb, c_sb, up_g = _sb_fwd(qs, ks, vs, tri_suf, shards=[up_loc])
    o_dil, lse_tot, lse4, lse16, mixed, x1, down_g = _attn_out(
        o_b, lse_b, o_sb, x2, dil_out_norm_w, sb_out_norm_w, out_g, shards=[down_loc])
    g, u, h2, dy, loss_parts = _ffn_fwd(x1, ffn_norm_w, tgt, gate_g, up_g, down_g)
    loss_local = jnp.sum(loss_parts[::8, 0])

    dg, du, act, dh2 = _ffn_bwd_dx(dy, g, u, gate_g, up_g, down_g)
    (dx1, do_dil, delta, do_sb, dwout, dn2, dndil, dnsb, do4, dl4, do16, dl16) = _attn_out_bwd(
        dy, dh2, x1, ffn_norm_w, out_g, mixed, o_dil, o_sb, dil_out_norm_w, sb_out_norm_w, bd512)
    dwg, dwu, dwd_lo, dwd_hi = _ffn_bwd_dw(h2, dy, dg, du, act)
    dqs, dks, dvs, r_gate = _sb_bwd(qs, ks, vs, do_sb, c_sb, tri_suf, tri_pre, rides=[dwg])
    cot_views = {1: (do_dil, lse_tot, delta), 4: (do4, lse4, dl4), 16: (do16, lse16, dl16)}
    riders = {1: [dwd_lo], 4: [dwd_hi], 16: [dwu]}
    dq_b, dk_b, dv_b, landed = [], [], [], {}
    for r in DILATIONS:
        dq, dk, dv, *landed[r] = _dil_bwd(*qkv_views[r], *cot_views[r], r, rides=riders[r])
        dq_b.append(dq)
        dk_b.append(dk)
        dv_b.append(dv)
    (r_down_lo,), (r_down_hi,), (r_up,) = landed[1], landed[4], landed[16]
    dproj, dqn2, dkn2, r_out = _qkv_bwd(dq_b, dk_b, dv_b, dqs, dks, dvs, qraw, kraw, cos2, sin2, qnw2, knw2,
                                        bd128, rides=[dwout])
    dwin = _in_bwd_dw(h1, dproj)
    grad_x, dn1, r_in = _in_bwd_dx(dproj, a_g, x2, dx1, attn_norm_w, rides=[dwin])
    dqn = dqn2[:, :HEAD_DIM] + dqn2[:, HEAD_DIM:]
    dkn = dkn2[:, :HEAD_DIM] + dkn2[:, HEAD_DIM:]

    small = _pack_small(dn1, dn2, dndil, dnsb, dqn, dkn, loss_local)
    (r_small,) = _exchange_grads([], small)
    big = {
        "w_in": _adamw(r_in, w_in[0], m_w_in[0], v_w_in[0]),
        "w_gate": tuple(t.T for t in _adamw(r_gate, w_gate[0].T, m_w_gate[0].T, v_w_gate[0].T)),
        "w_up": tuple(t.T for t in _adamw(r_up, w_up[0].T, m_w_up[0].T, v_w_up[0].T)),
        "w_down": _adamw(r_down_lo, w_down[0], m_w_down[0], v_w_down[0], recv_hi=r_down_hi),
        "w_out": _adamw(r_out, w_out[0], m_w_out[0], v_w_out[0]),
    }
    packs = [_pack_small(*ts) for ts in (
        (attn_norm_w, ffn_norm_w, dil_out_norm_w, sb_out_norm_w, q_norm_w, k_norm_w),
        (m_attn_norm_w, m_ffn_norm_w, m_dil_out_norm_w, m_sb_out_norm_w, m_q_norm_w, m_k_norm_w),
        (v_attn_norm_w, v_ffn_norm_w, v_dil_out_norm_w, v_sb_out_norm_w, v_q_norm_w, v_k_norm_w))]
    small_raw = _adamw(r_small, *packs)
    loss = small_raw[0][LOSS_ROW, 0]
    small_out = [_unpack_small(t) for t in small_raw]
    names = ["attn_norm_w", "w_in", "q_norm_w", "k_norm_w", "dil_out_norm_w", "sb_out_norm_w", "w_out",
             "ffn_norm_w", "w_gate", "w_up", "w_down"]
    small_pos = {"attn_norm_w": 0, "ffn_norm_w": 1, "dil_out_norm_w": 2, "sb_out_norm_w": 3,
                 "q_norm_w": 4, "k_norm_w": 5}
    outs = [loss, grad_x[None]]
    for kind in range(4):
        for name in names:
            if name in small_pos:
                outs.append(small_out[kind][small_pos[name]])
            else:
                outs.append(big[name][kind][None])
    return tuple(outs)
```

```python
import functools

import jax
import jax.numpy as jnp
from jax import lax
from jax.experimental import pallas as pl
from jax.experimental.pallas import tpu as pltpu

F32 = jnp.float32
BF16 = jnp.bfloat16

N_DEV = 8
D_MODEL = 1024
HEAD_DIM = 64
D_GRP = 512
D_IN = 6 * D_GRP
IN_SHARD = D_IN // N_DEV
FF_SHARD = 352
FF_PAD = 384
FF_BLOCK = 2 * FF_PAD
FF_STEPS = N_DEV // 2
OUT_SHARD = D_MODEL // N_DEV
BLOCK = 128
DILATIONS = (1, 4, 16)
ROPE_THETA = 10000.0
EPS = 1e-6
ATT_SCALE = HEAD_DIM ** -0.5
NEG = -1e30

ADAM_LR = 0.001
ADAM_B1 = 0.9
ADAM_B2 = 0.999
ADAM_EPS = 1e-08
ADAM_WD = 0.01
ADAM_STEP = 10

SB_TILE = 256
SB_DEAD = -104.0
SB_PAIRS = 4
SB_BWD_PAIRS = 2
ROW_TILE = 512
DW_ROW_TILE = 1024
VMEM_LIMIT = 56 * 1024 * 1024
MESH = pl.DeviceIdType.MESH


def _dot(a, b):
    return jnp.dot(a, b, preferred_element_type=F32)


def _dot_nt(a, b):
    return lax.dot_general(a, b, (((1,), (1,)), ((), ())), preferred_element_type=F32)


def _dot_tn(a, b):
    return lax.dot_general(a, b, (((0,), (0,)), ((), ())), preferred_element_type=F32)


def _mm_split(t, m):
    hi = t.astype(BF16)
    lo = (t - hi.astype(F32)).astype(BF16)
    return _dot(hi, m) + _dot(lo, m)


def _params(**kw):
    return pltpu.CompilerParams(vmem_limit_bytes=VMEM_LIMIT, **kw)


def _full(shape):
    nd = len(shape)
    return pl.BlockSpec(shape, lambda *_: (0,) * nd)


def _view_shape(s_len, r, dtype):
    return jax.ShapeDtypeStruct((s_len // r, r * D_GRP), dtype)


def _view_spec(tm, r):
    return pl.BlockSpec((tm // r, r * D_GRP), lambda i: (i, 0))


def _swap_halves(t):
    lane = lax.broadcasted_iota(jnp.int32, t.shape, 1)
    first = (lane & 32) == 0
    return jnp.where(first, pltpu.roll(t, 96, 1), pltpu.roll(t, 32, 1))


def _log_sigmoid_pair(z):
    neg_abs = lax.bitcast_convert_type(lax.bitcast_convert_type(z, jnp.uint32) | jnp.uint32(0x80000000), F32)
    lb = jnp.minimum(z, 0.0) - jnp.log(1.0 + jnp.exp(neg_abs))
    return lb, lb - z


def _cumsum_mm(t, tri):
    return _dot(t.astype(BF16), tri)


def _split_views(src_ref, stage_ref, views4, views16):
    slabs, n, _ = src_ref.shape
    n4, n16 = n // 4, n // 16
    for j in range(slabs):
        g, lanes = j // 4, 128 * (j % 4)
        src, stage = src_ref.at[j], stage_ref.at[j]
        for c4 in range(4):
            blk = src[pl.ds(c4, n4, stride=4), :]
            stage[n4 * c4:n4 * (c4 + 1), :] = blk
            col = D_GRP * c4 + lanes
            views4[g][:, col:col + 128] = blk.astype(views4[g].dtype)
        for c4 in range(4):
            for c1 in range(4):
                blk = stage[pl.ds(n4 * c4 + c1, n16, stride=4), :]
                col = D_GRP * (4 * c1 + c4) + lanes
                views16[g][:, col:col + 128] = blk.astype(views16[g].dtype)


def _merge_views(views4, views16, stage_ref, dst4_ref, dst16_ref):
    slabs, n, _ = dst4_ref.shape
    n4, n16 = n // 4, n // 16
    for j in range(slabs):
        g, lanes = j // 4, 128 * (j % 4)
        dst4, dst16, stage = dst4_ref.at[j], dst16_ref.at[j], stage_ref.at[j]
        for c4 in range(4):
            col = D_GRP * c4 + lanes
            dst4[pl.ds(c4, n4, stride=4), :] = views4[g][:, col:col + 128].astype(F32)
            for c1 in range(4):
                col = D_GRP * (4 * c1 + c4) + lanes
                stage[pl.ds(n4 * c4 + c1, n16, stride=4), :] = views16[g][:, col:col + 128].astype(F32)
        for c4 in range(4):
            dst16[pl.ds(c4, n4, stride=4), :] = stage[n4 * c4:n4 * (c4 + 1), :]


def _slab_group(ref, g):
    return jnp.concatenate([ref[4 * g + p] for p in range(4)], axis=1)


def _mesh_pos():
    return lax.axis_index("x"), lax.axis_index("y"), lax.axis_index("c")


def _flat_index(p):
    return 4 * p[0] + 2 * p[1] + p[2]


def _gather_weights(shards):
    n_arr = len(shards)

    def body(*refs):
        srcs, outs = refs[:n_arr], refs[n_arr:2 * n_arr]
        send_sems, recv_sems, local_sems = refs[2 * n_arr:]
        x, y, c = _mesh_pos()
        me, sibling = (x, y, c), (x, y, 1 - c)
        chips = [(1 - x, y), (x, 1 - y), (1 - x, 1 - y)]

        def copy(arr, k, block, to, own=False):
            dst = outs[arr].at[_flat_index(block)]
            return pltpu.make_async_remote_copy(
                src_ref=srcs[arr] if own else dst, dst_ref=dst,
                send_sem=send_sems.at[arr, k], recv_sem=recv_sems.at[arr, k],
                device_id=to, device_id_type=MESH)

        for arr in range(n_arr):
            mine = pltpu.make_async_copy(srcs[arr], outs[arr].at[_flat_index(me)], local_sems.at[arr])
            mine.start()
            first = [copy(arr, 0, me, sibling, own=True)]
            first += [copy(arr, 1 + j, me, (*chip, c), own=True) for j, chip in enumerate(chips)]
            for cp in first:
                cp.start()
        for arr in range(n_arr):
            passed = [copy(arr, 4 + j, (*chip, c), sibling) for j, chip in enumerate(chips)]
            for j, chip in enumerate(chips):
                copy(arr, 1 + j, (*chip, c), me).wait_recv()
                passed[j].start()
        for arr in range(n_arr):
            copy(arr, 0, sibling, me).wait_recv()
            for j, chip in enumerate(chips):
                copy(arr, 4 + j, (*chip, 1 - c), me).wait_recv()
            for k in range(7):
                copy(arr, k, me, me).wait_send()
            pltpu.make_async_copy(srcs[arr], outs[arr].at[_flat_index(me)], local_sems.at[arr]).wait()

    any_spec = pl.BlockSpec(memory_space=pl.ANY)
    return pl.pallas_call(
        body, name="gather_weights",
        out_shape=tuple(jax.ShapeDtypeStruct((N_DEV,) + s.shape, s.dtype) for s in shards),
        in_specs=[any_spec] * n_arr, out_specs=(any_spec,) * n_arr,
        scratch_shapes=[pltpu.SemaphoreType.DMA((n_arr, 7)), pltpu.SemaphoreType.DMA((n_arr, 7)),
                        pltpu.SemaphoreType.DMA((n_arr,))],
        compiler_params=pltpu.CompilerParams(has_side_effects=True),
    )(*shards)


def _peer_list(x, y, c):
    return [(1 - x if m & 4 else x, 1 - y if m & 2 else y, 1 - c if m & 1 else c) for m in range(1, N_DEV)]


def _exchange_grads(parts, small):
    n_arr = len(parts)

    def body(*refs):
        ins, outs = refs[:n_arr + 1], refs[n_arr + 1:2 * (n_arr + 1)]
        send_sems, recv_sems, local_sems = refs[2 * (n_arr + 1):]
        x, y, c = _mesh_pos()
        me = (x, y, c)
        my_idx = _flat_index(me)
        peers = []
        for m in range(1, N_DEV):
            peers.append((1 - x if m & 4 else x, 1 - y if m & 2 else y, 1 - c if m & 1 else c))

        def src_block(arr, dev):
            return ins[arr] if arr == n_arr else ins[arr].at[_flat_index(dev)]

        def copy(arr, k):
            return pltpu.make_async_remote_copy(
                src_ref=src_block(arr, peers[k]), dst_ref=outs[arr].at[my_idx],
                send_sem=send_sems.at[arr, k], recv_sem=recv_sems.at[arr, k],
                device_id=peers[k], device_id_type=MESH)

        def local(arr):
            return pltpu.make_async_copy(src_block(arr, me), outs[arr].at[my_idx], local_sems.at[arr])

        for arr in range(n_arr + 1):
            local(arr).start()
            for k in range(N_DEV - 1):
                copy(arr, k).start()
        for arr in range(n_arr + 1):
            for k in range(N_DEV - 1):
                cp = copy(arr, k)
                cp.wait_send()
                cp.wait_recv()
            local(arr).wait()

    any_spec = pl.BlockSpec(memory_space=pl.ANY)
    out_shape = tuple(jax.ShapeDtypeStruct(p.shape, p.dtype) for p in parts)
    out_shape += (jax.ShapeDtypeStruct((N_DEV,) + small.shape, small.dtype),)
    return pl.pallas_call(
        body, name="exchange_grads",
        out_shape=out_shape,
        in_specs=[any_spec] * (n_arr + 1), out_specs=(any_spec,) * (n_arr + 1),
        scratch_shapes=[pltpu.SemaphoreType.DMA((n_arr + 1, N_DEV - 1)),
                        pltpu.SemaphoreType.DMA((n_arr + 1, N_DEV - 1)),
                        pltpu.SemaphoreType.DMA((n_arr + 1,))],
        compiler_params=pltpu.CompilerParams(has_side_effects=True),
    )(*parts, small)


def _call_with_gather(body, shards, first_step, mid_step, last_step, *, name, grid, in_specs, out_specs,
                      out_shape, scratch_shapes=()):
    out_specs = tuple(out_specs) if isinstance(out_specs, (tuple, list)) else (out_specs,)
    out_shape = tuple(out_shape) if isinstance(out_shape, (tuple, list)) else (out_shape,)
    n_in, n_out, n_scr, n = len(in_specs), len(out_specs), len(scratch_shapes), len(shards)
    if n == 0:
        return pl.pallas_call(body, name=name, grid=grid, in_specs=list(in_specs), out_specs=out_specs,
                              out_shape=out_shape, scratch_shapes=list(scratch_shapes),
                              compiler_params=_params())

    def full_body(*refs):
        ins, srcs = refs[:n_in], refs[n_in:n_in + n]
        outs, lands = refs[n_in + n:n_in + n + n_out], refs[n_in + n + n_out:n_in + 2 * n + n_out]
        scratch = refs[n_in + 2 * n + n_out:n_in + 2 * n + n_out + n_scr]
        send_sems, recv_sems, local_sems = refs[-3:]
        x, y, c = _mesh_pos()
        me, sibling = (x, y, c), (x, y, 1 - c)
        chips = [(1 - x, y), (x, 1 - y), (1 - x, 1 - y)]

        def copy(a, k, block, to, own=False):
            dst = lands[a].at[_flat_index(block)]
            return pltpu.make_async_remote_copy(
                src_ref=srcs[a] if own else dst, dst_ref=dst,
                send_sem=send_sems.at[a, k], recv_sem=recv_sems.at[a, k],
                device_id=to, device_id_type=MESH)

        def local(a):
            return pltpu.make_async_copy(srcs[a], lands[a].at[_flat_index(me)], local_sems.at[a])

        @pl.when(first_step())
        def _():
            for a in range(n):
                local(a).start()
                copy(a, 0, me, sibling, own=True).start()
                for j, chip in enumerate(chips):
                    copy(a, 1 + j, me, (*chip, c), own=True).start()

        @pl.when(mid_step())
        def _():
            for a in range(n):
                for j, chip in enumerate(chips):
                    copy(a, 1 + j, (*chip, c), me).wait_recv()
                    copy(a, 4 + j, (*chip, c), sibling).start()

        body(*ins, *outs, *scratch)

        @pl.when(last_step())
        def _():
            for a in range(n):
                copy(a, 0, sibling, me).wait_recv()
                for j, chip in enumerate(chips):
                    copy(a, 4 + j, (*chip, 1 - c), me).wait_recv()
                for k in range(N_DEV - 1):
                    copy(a, k, me, me).wait_send()
                local(a).wait()

    any_spec = pl.BlockSpec(memory_space=pl.ANY)
    return pl.pallas_call(
        full_body, name=name, grid=grid,
        in_specs=list(in_specs) + [any_spec] * n,
        out_specs=out_specs + (any_spec,) * n,
        out_shape=out_shape + tuple(jax.ShapeDtypeStruct((N_DEV,) + t.shape, t.dtype) for t in shards),
        scratch_shapes=list(scratch_shapes) + [pltpu.SemaphoreType.DMA((n, N_DEV - 1)),
                                               pltpu.SemaphoreType.DMA((n, N_DEV - 1)),
                                               pltpu.SemaphoreType.DMA((n,))],
        compiler_params=_params(has_side_effects=True),
    )


def _call_with_exchange(body, rides, first_step, last_step, *, name, grid, in_specs, out_specs, out_shape,
                        scratch_shapes=()):
    out_specs = tuple(out_specs) if isinstance(out_specs, (tuple, list)) else (out_specs,)
    out_shape = tuple(out_shape) if isinstance(out_shape, (tuple, list)) else (out_shape,)
    n_in, n_out, n_scr, n = len(in_specs), len(out_specs), len(scratch_shapes), len(rides)
    if n == 0:
        return pl.pallas_call(body, name=name, grid=grid, in_specs=list(in_specs), out_specs=out_specs,
                              out_shape=out_shape, scratch_shapes=list(scratch_shapes),
                              compiler_params=_params())

    def full_body(*refs):
        ins, srcs = refs[:n_in], refs[n_in:n_in + n]
        outs, lands = refs[n_in + n:n_in + n + n_out], refs[n_in + n + n_out:n_in + 2 * n + n_out]
        scratch = refs[n_in + 2 * n + n_out:n_in + 2 * n + n_out + n_scr]
        send_sems, recv_sems, local_sems = refs[-3:]
        x, y, c = _mesh_pos()
        my_idx = _flat_index((x, y, c))
        peers = _peer_list(x, y, c)

        def remote(a, k):
            return pltpu.make_async_remote_copy(
                src_ref=srcs[a].at[_flat_index(peers[k])], dst_ref=lands[a].at[my_idx],
                send_sem=send_sems.at[a, k], recv_sem=recv_sems.at[a, k],
                device_id=peers[k], device_id_type=MESH)

        def local(a):
            return pltpu.make_async_copy(srcs[a].at[my_idx], lands[a].at[my_idx], local_sems.at[a])

        @pl.when(first_step())
        def _():
            for a in range(n):
                local(a).start()
                for k in range(N_DEV - 1):
                    remote(a, k).start()

        body(*ins, *outs, *scratch)

        @pl.when(last_step())
        def _():
            for a in range(n):
                for k in range(N_DEV - 1):
                    cp = remote(a, k)
                    cp.wait_send()
                    cp.wait_recv()
                local(a).wait()

    any_spec = pl.BlockSpec(memory_space=pl.ANY)
    res = pl.pallas_call(
        full_body, name=name, grid=grid,
        in_specs=list(in_specs) + [any_spec] * n,
        out_specs=out_specs + (any_spec,) * n,
        out_shape=out_shape + tuple(jax.ShapeDtypeStruct(t.shape, t.dtype) for t in rides),
        scratch_shapes=list(scratch_shapes) + [pltpu.SemaphoreType.DMA((n, N_DEV - 1)),
                                               pltpu.SemaphoreType.DMA((n, N_DEV - 1)),
                                               pltpu.SemaphoreType.DMA((n,))],
        compiler_params=_params(has_side_effects=True),
    )
    return res


def _head_norm(t, w128, bd):
    ms = _mm_split(t * t, bd) * (1.0 / HEAD_DIM)
    r = lax.rsqrt(ms + EPS)
    return (t * r) * w128, r


def _attn_in(x2, wn1, a_g, cos2, sin2, qnw, knw, bd, shards):
    s_len = x2.shape[0]
    tm = ROW_TILE

    def body(x_ref, wn_ref, w_ref, cos_ref, sin_ref, qnw_ref, knw_ref, bd_ref,
             h1_ref, qraw_ref, kraw_ref, q_ref, k_ref, va_ref, qs_ref, ks_ref, vs_ref,
             q4_ref, k4_ref, v4_ref, q16_ref, k16_ref, v16_ref, proj, slabs, stage, w_full):
        @pl.when(pl.program_id(0) == 0)
        def _():
            for d in range(N_DEV):
                w_full[:, IN_SHARD * d:IN_SHARD * (d + 1)] = w_ref[d]

        xx = x_ref[...]
        r = lax.rsqrt(jnp.mean(xx * xx, axis=-1, keepdims=True) + EPS)
        h = ((xx * r) * wn_ref[...]).astype(BF16)
        h1_ref[...] = h
        proj[...] = _dot(h, w_full[...])
        cos_t, sin_t, bdm = cos_ref[...], sin_ref[...], bd_ref[...]
        for grp, (raw_ref, rope_ref, nw_ref) in enumerate(((qraw_ref, q_ref, qnw_ref),
                                                           (kraw_ref, k_ref, knw_ref))):
            for p in range(4):
                cols = slice(D_GRP * grp + 128 * p, D_GRP * grp + 128 * (p + 1))
                t = proj[:, cols]
                raw_ref[:, 128 * p:128 * (p + 1)] = t
                yn, _ = _head_norm(t, nw_ref[...], bdm)
                roped = yn * cos_t + _swap_halves(yn) * sin_t
                slabs[4 * grp + p] = roped
                rope_ref[:, 128 * p:128 * (p + 1)] = roped.astype(BF16)
        for p in range(4):
            slabs[8 + p] = proj[:, 2 * D_GRP + 128 * p:2 * D_GRP + 128 * (p + 1)]
        for grp, ref in ((2, va_ref), (3, qs_ref), (4, ks_ref), (5, vs_ref)):
            ref[...] = proj[:, D_GRP * grp:D_GRP * (grp + 1)].astype(BF16)
        _split_views(slabs, stage, (q4_ref, k4_ref, v4_ref), (q16_ref, k16_ref, v16_ref))

    row = lambda w: pl.BlockSpec((tm, w), lambda i: (i, 0))
    grp_bf = jax.ShapeDtypeStruct((s_len, D_GRP), BF16)
    grp_f32 = jax.ShapeDtypeStruct((s_len, D_GRP), F32)
    ni = s_len // tm
    return _call_with_gather(
        body, shards, lambda: pl.program_id(0) == 0, lambda: pl.program_id(0) == ni - 2,
        lambda: pl.program_id(0) == ni - 1,
        name="attn_in", grid=(ni,),
        in_specs=[row(D_MODEL), _full((1, D_MODEL)),
                  pl.BlockSpec((N_DEV, D_MODEL, IN_SHARD), lambda i: (0, 0, 0)),
                  row(128), row(128), _full((1, 128)), _full((1, 128)), _full((128, 128))],
        out_specs=(row(D_MODEL),) + (row(D_GRP),) * 8 + (_view_spec(tm, 4),) * 3 + (_view_spec(tm, 16),) * 3,
        out_shape=(jax.ShapeDtypeStruct((s_len, D_MODEL), BF16), grp_f32, grp_f32) + (grp_bf,) * 6
        + (_view_shape(s_len, 4, BF16),) * 3 + (_view_shape(s_len, 16, BF16),) * 3,
        scratch_shapes=[pltpu.VMEM((tm, D_IN), F32), pltpu.VMEM((12, tm, 128), F32), pltpu.VMEM((12, tm, 128), F32),
                        pltpu.VMEM((D_MODEL, D_IN), BF16)],
    )(x2, wn1, a_g, cos2, sin2, qnw, knw, bd, *shards)


def _band_mask(n):
    i = lax.broadcasted_iota(jnp.int32, (2 * BLOCK, 2 * BLOCK), 0) & (BLOCK - 1)
    j = lax.broadcasted_iota(jnp.int32, (2 * BLOCK, 2 * BLOCK), 1)
    dist = i + BLOCK - j
    return (dist >= 0) & (dist <= BLOCK) & ((n - 1) * BLOCK + j >= 0)


def _stack_heads(t2, head0):
    return jnp.concatenate([jnp.where(head0, t2, 0), jnp.where(head0, 0, t2)], axis=0)


def _unstack_heads(t, head0):
    return jnp.where(head0, t[0:BLOCK], t[BLOCK:2 * BLOCK])


def _dil_fwd(qv, kv, vv, r, shards):
    sub_len = qv.shape[0]
    nb = sub_len // BLOCK

    qb = 2 if nb % 2 == 0 else 1

    def body(q_ref, kp_ref, kc_ref, vp_ref, vc_ref, o_ref, lse_ref):
        n = pl.program_id(1)
        lane = lax.broadcasted_iota(jnp.int32, (BLOCK, 128), 1)
        head0 = lane < HEAD_DIM
        units = [(b, slice(128 * p, 128 * (p + 1))) for b in range(qb) for p in range(4)]
        valid = [_band_mask(qb * n + b) for b in range(qb)]
        rows = [slice(BLOCK * b, BLOCK * (b + 1)) for b in range(qb)]

        def keys(prev_ref, cur_ref, b, c):
            before = prev_ref[:, c] if b == 0 else cur_ref[rows[b - 1], c]
            return jnp.concatenate([before, cur_ref[rows[b], c]], axis=0)

        qqs = [_stack_heads(q_ref[rows[b], c] * ATT_SCALE, head0) for b, c in units]
        kks = [keys(kp_ref, kc_ref, b, c) for b, c in units]
        vvs = [keys(vp_ref, vc_ref, b, c) for b, c in units]
        ss = [_dot_nt(qq, kk) for qq, kk in zip(qqs, kks)]
        prs, dens, lses = [], [], []
        for (b, _), s in zip(units, ss):
            s = jnp.where(valid[b], s, NEG)
            m = jnp.max(s, axis=-1, keepdims=True)
            pr = jnp.exp(s - m)
            den = jnp.sum(pr, axis=-1, keepdims=True)
            prs.append(pr.astype(BF16))
            dens.append(den)
            lses.append(m + jnp.log(den))
        pvs = [_dot(pr, vv2) for pr, vv2 in zip(prs, vvs)]
        for (b, c), pv, den, lse in zip(units, pvs, dens, lses):
            o_ref[rows[b], c] = _unstack_heads(pv / den, head0)
            lse_ref[rows[b], c] = _unstack_heads(jnp.broadcast_to(lse, (2 * BLOCK, 128)), head0)

    cur = pl.BlockSpec((qb * BLOCK, D_GRP), lambda c, n: (n, c))
    prev = pl.BlockSpec((BLOCK, D_GRP), lambda c, n: (jnp.maximum(qb * n - 1, 0), c))
    out = jax.ShapeDtypeStruct(qv.shape, F32)
    steps = nb // qb

    def at(t):
        return lambda: pl.program_id(0) * steps + pl.program_id(1) == t

    return _call_with_gather(
        body, shards, at(0), at((2 * r * steps) // 3), at(r * steps - 1),
        name=f"dil_fwd_r{r}", grid=(r, steps),
        in_specs=[cur, prev, cur, prev, cur], out_specs=(cur, cur), out_shape=(out, out),
    )(qv, kv, kv, vv, vv, *shards)


def _dil_bwd(qv, kv, vv, dov, lsev, deltav, r, rides):
    sub_len = qv.shape[0]
    nb = sub_len // BLOCK

    def body(q_ref, kp_ref, kc_ref, vp_ref, vc_ref, do_ref, lse_ref, dl_ref,
             dq_ref, dk_ref, dv_ref, dk_carry, dv_carry):
        n = pl.program_id(1)

        @pl.when(n == 0)
        def _():
            dk_carry[...] = jnp.zeros_like(dk_carry)
            dv_carry[...] = jnp.zeros_like(dv_carry)

        @pl.when(n < nb)
        def _():
            valid = _band_mask(n)
            lane = lax.broadcasted_iota(jnp.int32, (BLOCK, 128), 1)
            head0 = lane < HEAD_DIM
            pairs = [slice(128 * p, 128 * (p + 1)) for p in range(4)]
            qqs = [_stack_heads(q_ref[:, c] * ATT_SCALE, head0) for c in pairs]
            dos = [_stack_heads(do_ref[:, c], head0) for c in pairs]
            kks = [jnp.concatenate([kp_ref[:, c], kc_ref[:, c]], axis=0) for c in pairs]
            vvs = [jnp.concatenate([vp_ref[:, c], vc_ref[:, c]], axis=0) for c in pairs]
            ss = [_dot_nt(qq, kk) for qq, kk in zip(qqs, kks)]
            dps = [_dot_nt(do, vv2) for do, vv2 in zip(dos, vvs)]
            def softmax_terms(p):
                stats = []
                for ref in (lse_ref, dl_ref):
                    t2 = ref[:, pairs[p]]
                    stats.append(jnp.concatenate(
                        [jnp.sum(jnp.where(lane == 0, t2, 0.0), axis=-1, keepdims=True),
                         jnp.sum(jnp.where(lane == HEAD_DIM, t2, 0.0), axis=-1, keepdims=True)], axis=0))
                pr = jnp.where(valid, jnp.exp(jnp.minimum(ss[p] - stats[0], 0.0)), 0.0)
                return pr.astype(BF16), (pr * (dps[p] - stats[1])).astype(BF16)

            terms = [softmax_terms(p) for p in range(4)]
            dqs = [_dot(terms[p][1], kks[p]) for p in range(4)]
            dkks = [_dot_tn(terms[p][1], qqs[p]) for p in range(4)]
            dvvs = [_dot_tn(terms[p][0], dos[p]) for p in range(4)]
            for c, dq, dkk, dvv in zip(pairs, dqs, dkks, dvvs):
                dq_ref[:, c] = _unstack_heads(dq, head0) * ATT_SCALE
                dk_ref[:, c] = dk_carry[:, c] + dkk[:BLOCK]
                dv_ref[:, c] = dv_carry[:, c] + dvv[:BLOCK]
                dk_carry[:, c] = dkk[BLOCK:]
                dv_carry[:, c] = dvv[BLOCK:]

        @pl.when(n == nb)
        def _():
            dk_ref[...] = dk_carry[...]
            dv_ref[...] = dv_carry[...]

    last = nb - 1
    cur = pl.BlockSpec((BLOCK, D_GRP), lambda c, n: (jnp.minimum(n, last), c))
    prev = pl.BlockSpec((BLOCK, D_GRP), lambda c, n: (jnp.clip(n - 1, 0, last), c))
    out = jax.ShapeDtypeStruct(qv.shape, F32)
    return _call_with_exchange(
        body, rides,
        lambda: jnp.logical_and(pl.program_id(0) == 0, pl.program_id(1) == 0),
        lambda: jnp.logical_and(pl.program_id(0) == r - 1, pl.program_id(1) == nb),
        name=f"dil_bwd_r{r}", grid=(r, nb + 1),
        in_specs=[cur, prev, cur, prev, cur, cur, cur, cur],
        out_specs=(cur, prev, prev), out_shape=(out, out, out),
        scratch_shapes=[pltpu.VMEM((BLOCK, D_GRP), F32), pltpu.VMEM((BLOCK, D_GRP), F32)],
    )(qv, kv, kv, vv, vv, dov, lsev, deltav, *rides)


def _sb_fwd(qs, ks, vs, tri_suf, shards):
    s_len = qs.shape[0]
    t = SB_TILE
    nq = s_len // t

    npair = SB_PAIRS

    def body(q_ref, k_ref, v_ref, u_ref, o_ref, c_ref, qq, vt, acc, cf, csave):
        row = lax.broadcasted_iota(jnp.int32, (2 * t, t), 0) & (t - 1)
        col = lax.broadcasted_iota(jnp.int32, (2 * t, t), 1)
        diag_mask = col < row
        lane1 = lax.broadcasted_iota(jnp.int32, (t, 128), 1)
        head0 = lane1 < HEAD_DIM
        lane2 = lax.broadcasted_iota(jnp.int32, (2 * t, 128), 1)
        uu = u_ref[...]
        pr = range(npair)
        cols = [slice(128 * pp, 128 * (pp + 1)) for pp in pr]

        i = pl.program_id(1)

        @pl.when(i == 0)
        def _():
            def transpose_v(j, _):
                rows = pl.ds(pl.multiple_of(j * t, t), t)
                for pp in pr:
                    vt[pp, j] = v_ref[rows, cols[pp]].astype(F32).T.astype(BF16)
                return 0

            lax.fori_loop(0, nq, transpose_v, 0)

        for pp in pr:
            q2 = q_ref[:, cols[pp]] * ATT_SCALE
            qq[pp, 0:t, :] = jnp.where(head0, q2, 0)
            qq[pp, t:2 * t, :] = jnp.where(head0, 0, q2)
        acc[...] = jnp.zeros_like(acc)
        cf[...] = jnp.zeros_like(cf)
        csave[...] = jnp.full(csave.shape, 2.0 * SB_DEAD, F32)

        def tile(kb, diag):
            krows = pl.ds(pl.multiple_of(kb * t, t), t)
            zs = [_dot_nt(qq[pp], k_ref[krows, cols[pp]]) for pp in pr]
            lbk = [_log_sigmoid_pair(z) for z in zs]
            lks = [jnp.where(diag_mask, lk, 0.0) if diag else lk for _, lk in lbk]
            sufs = [_cumsum_mm(lk, uu) for lk in lks]
            carries = [cf[pp] for pp in pr]
            avs = []
            for pp in pr:
                a = jnp.exp(lbk[pp][0] + (sufs[pp] + jnp.concatenate([carries[pp]] * (t // 128), axis=1)))
                avs.append((jnp.where(diag_mask, a, 0.0) if diag else a).astype(BF16))
            pvs = [_dot_nt(vt[pp, kb], avs[pp]) for pp in pr]
            for pp in pr:
                acc[pp] += pvs[pp]
                csave[pp] = jnp.where(lane2 == kb, carries[pp], csave[pp])
                cf[pp] = carries[pp] + jnp.broadcast_to(jnp.sum(lks[pp], axis=-1, keepdims=True), (2 * t, 128))

        tile(i, True)

        def alive():
            return jnp.max(cf[...]) > SB_DEAD

        def k_block(state):
            kb, _ = state
            tile(kb, False)
            return kb - 1, alive()

        lax.while_loop(lambda state: jnp.logical_and(state[0] >= 0, state[1]), k_block, (i - 1, alive()))
        for pp in pr:
            o_ref[:, cols[pp]] = jnp.where(head0, acc[pp, :, 0:t].T, acc[pp, :, t:2 * t].T)
            c_ref[2 * pp] = csave[pp, 0:t, :]
            c_ref[2 * pp + 1] = csave[pp, t:2 * t, :]

    width = 128 * npair
    kv = pl.BlockSpec((s_len, width), lambda p, i: (0, p))
    qo = pl.BlockSpec((t, width), lambda p, i: (i, p))
    steps = 4 // npair

    def at(p, i):
        return lambda: jnp.logical_and(pl.program_id(0) == p, pl.program_id(1) == i)

    return _call_with_gather(
        body, shards, at(0, 0), at(steps - 1, (2 * nq) // 3), at(steps - 1, nq - 1),
        name="sb_fwd", grid=(steps, nq),
        in_specs=[qo, kv, kv, pl.BlockSpec((t, t), lambda p, i: (0, 0))],
        out_specs=(qo, pl.BlockSpec((2 * npair, t, 128), lambda p, i: (p, i, 0))),
        out_shape=(jax.ShapeDtypeStruct((s_len, D_GRP), F32),
                   jax.ShapeDtypeStruct((8, s_len, 128), F32)),
        scratch_shapes=[pltpu.VMEM((npair, 2 * t, 128), BF16), pltpu.VMEM((npair, nq, 128, t), BF16),
                        pltpu.VMEM((npair, 128, 2 * t), F32),
                        pltpu.VMEM((npair, 2 * t, 128), F32), pltpu.VMEM((npair, 2 * t, 128), F32)],
    )(qs, ks, vs, tri_suf, *shards)


def _sb_bwd(qs, ks, vs, dos, csaved, tri_suf, tri_pre, rides):
    s_len = qs.shape[0]
    t = SB_TILE
    nq = s_len // t

    npair = SB_BWD_PAIRS

    def body(q_ref, k_ref, v_ref, do_ref, c_ref, u_ref, p_ref, dq_ref, dk_ref, dv_ref,
             qq, dd, qqt, ddt, kt, dq_acc, dkt, dvt, cg):
        row = lax.broadcasted_iota(jnp.int32, (2 * t, t), 0) & (t - 1)
        col = lax.broadcasted_iota(jnp.int32, (2 * t, t), 1)
        diag_mask = col < row
        lane1 = lax.broadcasted_iota(jnp.int32, (t, 128), 1)
        head0 = lane1 < HEAD_DIM
        lane2 = lax.broadcasted_iota(jnp.int32, (2 * t, 128), 1)
        uu, pm = u_ref[...], p_ref[...]
        pr = range(npair)
        cols = [slice(128 * pp, 128 * (pp + 1)) for pp in pr]
        i = pl.program_id(1)

        @pl.when(i == 0)
        def _():
            dkt[...] = jnp.zeros_like(dkt)
            dvt[...] = jnp.zeros_like(dvt)

            def transpose_k(j, _):
                rows = pl.ds(pl.multiple_of(j * t, t), t)
                for pp in pr:
                    kt[pp, j] = k_ref[rows, cols[pp]].astype(F32).T.astype(BF16)
                return 0

            lax.fori_loop(0, nq, transpose_k, 0)

        for pp in pr:
            q2 = q_ref[:, cols[pp]].astype(F32) * ATT_SCALE
            do2 = do_ref[:, cols[pp]].astype(F32)
            for src, nat, tr in ((q2, qq, qqt), (do2, dd, ddt)):
                stacked = jnp.concatenate([jnp.where(head0, src, 0.0), jnp.where(head0, 0.0, src)], axis=0)
                nat[pp] = stacked.astype(BF16)
                tr[pp] = stacked.T.astype(BF16)
        dq_acc[...] = jnp.zeros_like(dq_acc)
        cg[...] = jnp.zeros_like(cg)

        def tile(kb, diag):
            krows = pl.ds(pl.multiple_of(kb * t, t), t)
            zs = [_dot_nt(qq[pp], k_ref[krows, cols[pp]]) for pp in pr]
            das = [_dot_nt(dd[pp], v_ref[krows, cols[pp]]) for pp in pr]
            lbk = [_log_sigmoid_pair(z) for z in zs]
            lks = [jnp.where(diag_mask, lk, 0.0) if diag else lk for _, lk in lbk]
            sufs = [_cumsum_mm(lk, uu) for lk in lks]
            avs, gs = [], []
            for pp in pr:
                cs = jnp.concatenate([c_ref[2 * pp], c_ref[2 * pp + 1]], axis=0)
                cf = jnp.sum(jnp.where(lane2 == kb, cs, 0.0), axis=-1, keepdims=True)
                a = jnp.exp(lbk[pp][0] + (sufs[pp] + cf))
                a = jnp.where(diag_mask, a, 0.0) if diag else a
                avs.append(a.astype(BF16))
                gs.append(a * das[pp])
            gpres = [_cumsum_mm(g, pm) for g in gs]
            dzs = []
            for pp in pr:
                carry = cg[pp]
                beta = jnp.exp(lbk[pp][0])
                dz = gs[pp] - beta * (gs[pp] + (gpres[pp] + jnp.concatenate([carry] * (t // 128), axis=1)))
                dzs.append((jnp.where(diag_mask, dz, 0.0) if diag else dz).astype(BF16))
                cg[pp] = carry + jnp.broadcast_to(jnp.sum(gs[pp], axis=-1, keepdims=True), (2 * t, 128))
            dqs = [_dot_nt(kt[pp, kb], dzs[pp]) for pp in pr]
            dks = [_dot(qqt[pp], dzs[pp]) for pp in pr]
            dvs = [_dot(ddt[pp], avs[pp]) for pp in pr]
            for pp in pr:
                dq_acc[pp] += dqs[pp]
                dkt[pp, kb] += dks[pp]
                dvt[pp, kb] += dvs[pp]

        def k_block(kb, _):
            tile(kb, False)
            return 0

        col_max = jnp.max(jnp.max(c_ref[...], axis=0), axis=0, keepdims=True)
        lane_row = lax.broadcasted_iota(jnp.int32, (1, 128), 1)
        n_live = jnp.sum(jnp.where(jnp.logical_and(col_max > SB_DEAD, lane_row < i), 1, 0))
        lax.fori_loop(i - n_live, i, k_block, 0)
        tile(i, True)
        for pp in pr:
            dq_ref[:, cols[pp]] = jnp.where(head0, dq_acc[pp, :, 0:t].T, dq_acc[pp, :, t:2 * t].T) * ATT_SCALE

        @pl.when(i == nq - 1)
        def _():
            def untranspose(j, _):
                rows = pl.ds(pl.multiple_of(j * t, t), t)
                for pp in pr:
                    dk_ref[rows, cols[pp]] = dkt[pp, j].T
                    dv_ref[rows, cols[pp]] = dvt[pp, j].T
                return 0

            lax.fori_loop(0, nq, untranspose, 0)

    width = 128 * npair
    kv = pl.BlockSpec((s_len, width), lambda p, i: (0, p))
    qo = pl.BlockSpec((t, width), lambda p, i: (i, p))
    tri = pl.BlockSpec((t, t), lambda p, i: (0, 0))
    out = jax.ShapeDtypeStruct((s_len, D_GRP), F32)
    steps = 4 // npair
    return _call_with_exchange(
        body, rides,
        lambda: jnp.logical_and(pl.program_id(0) == 0, pl.program_id(1) == 0),
        lambda: jnp.logical_and(pl.program_id(0) == steps - 1, pl.program_id(1) == nq - 1),
        name="sb_bwd", grid=(steps, nq),
        in_specs=[qo, kv, kv, qo, pl.BlockSpec((2 * npair, t, 128), lambda p, i: (p, i, 0)), tri, tri],
        out_specs=(qo, kv, kv), out_shape=(out, out, out),
        scratch_shapes=[pltpu.VMEM((npair, 2 * t, 128), BF16), pltpu.VMEM((npair, 2 * t, 128), BF16),
                        pltpu.VMEM((npair, 128, 2 * t), BF16), pltpu.VMEM((npair, 128, 2 * t), BF16),
                        pltpu.VMEM((npair, nq, 128, t), BF16),
                        pltpu.VMEM((npair, 128, 2 * t), F32),
                        pltpu.VMEM((npair, nq, 128, t), F32), pltpu.VMEM((npair, nq, 128, t), F32),
                        pltpu.VMEM((npair, 2 * t, 128), F32)],
    )(qs, ks, vs, dos, csaved, tri_suf, tri_pre, *rides)


def _attn_out(o_b, lse_b, o_sb, x2, wdil, wsb, out_g, shards):
    s_len = x2.shape[0]
    tm = ROW_TILE

    def body(o1_ref, l1_ref, o4_ref, l4_ref, o16_ref, l16_ref, osb_ref, x_ref, wdil_ref, wsb_ref, w_ref,
             odil_ref, lse_ref, lse4_ref, lse16_ref, mixed_ref, x1_ref, stage, nat4, nat16):
        _merge_views((o4_ref, l4_ref), (o16_ref, l16_ref), stage, nat4, nat16)
        os_ = (o1_ref[...], _slab_group(nat4, 0), _slab_group(nat16, 0))
        ls = (l1_ref[...], _slab_group(nat4, 1), _slab_group(nat16, 1))
        mx = jnp.maximum(jnp.maximum(ls[0], ls[1]), ls[2])
        es = [jnp.exp(l - mx) for l in ls]
        den = es[0] + es[1] + es[2]
        o_dil = (es[0] * os_[0] + es[1] * os_[1] + es[2] * os_[2]) / den
        odil_ref[...] = o_dil
        lse = mx + jnp.log(den)
        lse_ref[...] = lse
        for p in range(4):
            nat4[p] = lse[:, 128 * p:128 * (p + 1)]
        _split_views(nat4.at[0:4], stage.at[0:4], (lse4_ref,), (lse16_ref,))
        halves = []
        for t, w_r in ((o_dil, wdil_ref), (osb_ref[...], wsb_ref)):
            r = lax.rsqrt(jnp.mean(t * t, axis=-1, keepdims=True) + EPS)
            halves.append(((t * r) * w_r[...]).astype(BF16))
        mixed = jnp.concatenate(halves, axis=1)
        mixed_ref[...] = mixed
        w = w_ref[...].reshape(D_MODEL, D_MODEL)
        x1_ref[...] = x_ref[...] + _dot(mixed, w)

    row = lambda w: pl.BlockSpec((tm, w), lambda i: (i, 0))
    ni = s_len // tm
    return _call_with_gather(
        body, shards, lambda: pl.program_id(0) == 0, lambda: pl.program_id(0) == ni - 2,
        lambda: pl.program_id(0) == ni - 1,
        name="attn_out", grid=(ni,),
        in_specs=[row(D_GRP)] * 2 + [_view_spec(tm, 4)] * 2 + [_view_spec(tm, 16)] * 2
        + [row(D_GRP), row(D_MODEL), _full((1, D_GRP)), _full((1, D_GRP)), _full((N_DEV, OUT_SHARD, D_MODEL))],
        out_specs=(row(D_GRP), row(D_GRP), _view_spec(tm, 4), _view_spec(tm, 16), row(D_MODEL), row(D_MODEL)),
        out_shape=(jax.ShapeDtypeStruct((s_len, D_GRP), F32), jax.ShapeDtypeStruct((s_len, D_GRP), F32),
                   _view_shape(s_len, 4, F32), _view_shape(s_len, 16, F32),
                   jax.ShapeDtypeStruct((s_len, D_MODEL), BF16), jax.ShapeDtypeStruct((s_len, D_MODEL), F32)),
        scratch_shapes=[pltpu.VMEM((8, tm, 128), F32)] * 3,
    )(o_b[0], lse_b[0], o_b[1], lse_b[1], o_b[2], lse_b[2], o_sb, x2, wdil, wsb, out_g, *shards)


def _two_shards(w_ref):
    return w_ref[...].reshape(FF_BLOCK, D_MODEL)


def _ffn_fwd(x1, wn2, tgt, gate_g, up_g, down_g):
    s_len = x1.shape[0]
    tm = ROW_TILE
    ni = s_len // tm

    def body(x_ref, wn_ref, t_ref, wg_ref, wu_ref, wd_ref, g_ref, u_ref, h2_ref, dy_ref, loss_ref, acc):
        j = pl.program_id(1)

        @pl.when(j == 0)
        def _():
            xx = x_ref[...]
            r = lax.rsqrt(jnp.mean(xx * xx, axis=-1, keepdims=True) + EPS)
            h2_ref[...] = ((xx * r) * wn_ref[...]).astype(BF16)
            acc[...] = jnp.zeros_like(acc)

        h = h2_ref[...]
        g = _dot_nt(h, _two_shards(wg_ref))
        u = _dot_nt(h, _two_shards(wu_ref))
        g_ref[...] = g
        u_ref[...] = u
        act = (g * (1.0 / (1.0 + jnp.exp(-g)))) * u
        acc[...] += _dot(act.astype(BF16), _two_shards(wd_ref))

        @pl.when(j == FF_STEPS - 1)
        def _():
            err = (x_ref[...] + acc[...]) - t_ref[...]
            dy_ref[...] = err * (1.0 / D_MODEL)
            part = 0.5 * jnp.sum(jnp.mean(err * err, axis=-1, keepdims=True))
            loss_ref[...] = jnp.full((8, 128), part, F32)

    row = pl.BlockSpec((tm, D_MODEL), lambda i, j: (i, 0))
    hid = pl.BlockSpec((tm, FF_BLOCK), lambda i, j: (i, j))
    return pl.pallas_call(
        body, name="ffn_fwd", grid=(ni, FF_STEPS),
        in_specs=[row, pl.BlockSpec((1, D_MODEL), lambda i, j: (0, 0)), row,
                  pl.BlockSpec((2, FF_PAD, D_MODEL), lambda i, j: (j, 0, 0)),
                  pl.BlockSpec((2, FF_PAD, D_MODEL), lambda i, j: (j, 0, 0)),
                  pl.BlockSpec((2, FF_PAD, D_MODEL), lambda i, j: (j, 0, 0))],
        out_specs=(hid, hid, row, row, pl.BlockSpec((8, 128), lambda i, j: (i, 0))),
        out_shape=(jax.ShapeDtypeStruct((s_len, N_DEV * FF_PAD), F32),
                   jax.ShapeDtypeStruct((s_len, N_DEV * FF_PAD), F32),
                   jax.ShapeDtypeStruct((s_len, D_MODEL), BF16),
                   jax.ShapeDtypeStruct((s_len, D_MODEL), F32),
                   jax.ShapeDtypeStruct((ni * 8, 128), F32)),
        scratch_shapes=[pltpu.VMEM((tm, D_MODEL), F32)],
        compiler_params=_params(),
    )(x1, wn2, tgt, gate_g, up_g, down_g)


def _ffn_bwd_dx(dy, g, u, gate_g, up_g, down_g):
    s_len = dy.shape[0]
    tm = ROW_TILE

    def body(dy_ref, g_ref, u_ref, wg_ref, wu_ref, wd_ref, dg_ref, du_ref, act_ref, dh_ref, acc):
        j = pl.program_id(1)

        @pl.when(j == 0)
        def _():
            acc[...] = jnp.zeros_like(acc)

        halves = [slice(0, tm // 2), slice(tm // 2, tm)]
        wd, wg, wu = _two_shards(wd_ref), _two_shards(wg_ref), _two_shards(wu_ref)
        das = [_dot_nt(dy_ref[rows, :].astype(BF16), wd) for rows in halves]

        def elementwise(rows, da):
            gg, uu = g_ref[rows, :], u_ref[rows, :]
            sig = 1.0 / (1.0 + jnp.exp(-gg))
            silu = gg * sig
            act_ref[rows, :] = (silu * uu).astype(BF16)
            du = (da * silu).astype(BF16)
            dg = (da * uu * (sig * (1.0 + gg * (1.0 - sig)))).astype(BF16)
            du_ref[rows, :] = du
            dg_ref[rows, :] = dg
            return dg, du

        dg0, du0 = elementwise(halves[0], das[0])
        acc[halves[0], :] += _dot(dg0, wg) + _dot(du0, wu)
        dg1, du1 = elementwise(halves[1], das[1])
        acc[halves[1], :] += _dot(dg1, wg) + _dot(du1, wu)

        @pl.when(j == FF_STEPS - 1)
        def _():
            dh_ref[...] = acc[...]

    row = pl.BlockSpec((tm, D_MODEL), lambda i, j: (i, 0))
    hid = pl.BlockSpec((tm, FF_BLOCK), lambda i, j: (i, j))
    hid_bf = jax.ShapeDtypeStruct((s_len, N_DEV * FF_PAD), BF16)
    return pl.pallas_call(
        body, name="ffn_bwd_dx", grid=(s_len // tm, FF_STEPS),
        in_specs=[row, hid, hid,
                  pl.BlockSpec((2, FF_PAD, D_MODEL), lambda i, j: (j, 0, 0)),
                  pl.BlockSpec((2, FF_PAD, D_MODEL), lambda i, j: (j, 0, 0)),
                  pl.BlockSpec((2, FF_PAD, D_MODEL), lambda i, j: (j, 0, 0))],
        out_specs=(hid, hid, hid, row),
        out_shape=(hid_bf, hid_bf, hid_bf, jax.ShapeDtypeStruct((s_len, D_MODEL), F32)),
        scratch_shapes=[pltpu.VMEM((tm, D_MODEL), F32)],
        compiler_params=_params(),
    )(dy, g, u, gate_g, up_g, down_g)


def _ffn_bwd_dw(h2, dy, dg, du, act):
    s_len = h2.shape[0]
    tm = DW_ROW_TILE
    ni = s_len // tm

    half = FF_PAD // 2

    def body(h_ref, dy_ref, dg_ref, du_ref, act_ref, dwg_ref, dwu_ref, dwd_lo_ref, dwd_hi_ref, ag, au, ad):
        i = pl.program_id(1)

        @pl.when(i == 0)
        def _():
            ag[...] = jnp.zeros_like(ag)
            au[...] = jnp.zeros_like(au)
            ad[...] = jnp.zeros_like(ad)

        h = h_ref[...]
        ag[...] += _dot_tn(dg_ref[...], h)
        au[...] += _dot_tn(du_ref[...], h)
        ad[...] += _dot_tn(act_ref[...], dy_ref[...].astype(BF16))

        @pl.when(i == ni - 1)
        def _():
            for acc_ref, out_ref in ((ag, dwg_ref), (au, dwu_ref)):
                out_ref[...] = acc_ref[...].astype(BF16).reshape(2, FF_PAD, D_MODEL)
            for dev in range(2):
                dwd_lo_ref[dev] = ad[FF_PAD * dev:FF_PAD * dev + half, :].astype(BF16)
                dwd_hi_ref[dev] = ad[FF_PAD * dev + half:FF_PAD * (dev + 1), :].astype(BF16)

    row = pl.BlockSpec((tm, D_MODEL), lambda j, i: (i, 0))
    hid = pl.BlockSpec((tm, FF_BLOCK), lambda j, i: (i, j))
    row_w = pl.BlockSpec((2, FF_PAD, D_MODEL), lambda j, i: (j, 0, 0))
    half_w = pl.BlockSpec((2, half, D_MODEL), lambda j, i: (j, 0, 0))
    grad = jax.ShapeDtypeStruct((N_DEV, FF_PAD, D_MODEL), BF16)
    half_grad = jax.ShapeDtypeStruct((N_DEV, half, D_MODEL), BF16)
    return pl.pallas_call(
        body, name="ffn_bwd_dw", grid=(FF_STEPS, ni),
        in_specs=[row, row, hid, hid, hid], out_specs=(row_w, row_w, half_w, half_w),
        out_shape=(grad, grad, half_grad, half_grad),
        scratch_shapes=[pltpu.VMEM((FF_BLOCK, D_MODEL), F32)] * 3,
        compiler_params=_params(),
    )(h2, dy, dg, du, act)


def _rms_bwd(dy, t, w):
    r = lax.rsqrt(jnp.mean(t * t, axis=-1, keepdims=True) + EPS)
    gw = dy * w
    dt = r * (gw - t * ((r * r) * jnp.mean(gw * t, axis=-1, keepdims=True)))
    return dt, dy * t * r


def _attn_out_bwd(dy, dh2, x1, wn2, b_g, mixed, o_dil, o_sb, wdil, wsb, bd512):
    s_len = dy.shape[0]
    tm = ROW_TILE
    ni = s_len // tm

    def body(dy_ref, dh_ref, x1_ref, wn_ref, w_ref, mixed_ref, odil_ref, osb_ref, wdil_ref, wsb_ref, bd_ref,
             dx1_ref, dodil_ref, delta_ref, dosb_ref, dwout_ref, dwn_ref, dwdil_ref, dwsb_ref,
             do4_ref, dl4_ref, do16_ref, dl16_ref, wacc, both, stage):
        i = pl.program_id(0)

        @pl.when(i == 0)
        def _():
            wacc[...] = jnp.zeros_like(wacc)
            dwn_ref[...] = jnp.zeros_like(dwn_ref)
            dwdil_ref[...] = jnp.zeros_like(dwdil_ref)
            dwsb_ref[...] = jnp.zeros_like(dwsb_ref)

        dnorm, dw_rows = _rms_bwd(dh_ref[...], x1_ref[...], wn_ref[...])
        dx1 = dy_ref[...] + dnorm
        dx1_ref[...] = dx1
        dwn_ref[...] += jnp.sum(dw_rows, axis=0, keepdims=True)
        dx1b = dx1.astype(BF16)
        w = w_ref[...].reshape(D_MODEL, D_MODEL)
        dmixed = _dot_nt(dx1b, w)
        wacc[...] += _dot_tn(mixed_ref[...], dx1b)
        o_dil = odil_ref[...]
        d_odil, dw_rows = _rms_bwd(dmixed[:, :D_GRP], o_dil, wdil_ref[...])
        dwdil_ref[...] += jnp.sum(dw_rows, axis=0, keepdims=True)
        dodil_ref[...] = d_odil.astype(BF16)
        delta = _mm_split(d_odil * o_dil, bd_ref[...])
        delta_ref[...] = delta
        for p in range(4):
            both[p] = d_odil[:, 128 * p:128 * (p + 1)]
            both[4 + p] = delta[:, 128 * p:128 * (p + 1)]
        _split_views(both, stage, (do4_ref, dl4_ref), (do16_ref, dl16_ref))
        d_osb, dw_rows = _rms_bwd(dmixed[:, D_GRP:], osb_ref[...], wsb_ref[...])
        dwsb_ref[...] += jnp.sum(dw_rows, axis=0, keepdims=True)
        dosb_ref[...] = d_osb.astype(BF16)

        @pl.when(i == ni - 1)
        def _():
            dwout_ref[...] = wacc[...].astype(BF16).reshape(N_DEV, OUT_SHARD, D_MODEL)

    row = lambda w: pl.BlockSpec((tm, w), lambda i: (i, 0))
    return pl.pallas_call(
        body, name="attn_out_bwd", grid=(ni,),
        in_specs=[row(D_MODEL), row(D_MODEL), row(D_MODEL), _full((1, D_MODEL)),
                  _full((N_DEV, OUT_SHARD, D_MODEL)),
                  row(D_MODEL), row(D_GRP), row(D_GRP), _full((1, D_GRP)), _full((1, D_GRP)),
                  _full((D_GRP, D_GRP))],
        out_specs=(row(D_MODEL), row(D_GRP), row(D_GRP), row(D_GRP),
                   _full((N_DEV, OUT_SHARD, D_MODEL)), _full((1, D_MODEL)), _full((1, D_GRP)), _full((1, D_GRP)),
                   _view_spec(tm, 4), _view_spec(tm, 4), _view_spec(tm, 16), _view_spec(tm, 16)),
        out_shape=(jax.ShapeDtypeStruct((s_len, D_MODEL), F32), jax.ShapeDtypeStruct((s_len, D_GRP), BF16),
                   jax.ShapeDtypeStruct((s_len, D_GRP), F32), jax.ShapeDtypeStruct((s_len, D_GRP), BF16),
                   jax.ShapeDtypeStruct((N_DEV, OUT_SHARD, D_MODEL), BF16),
                   jax.ShapeDtypeStruct((1, D_MODEL), F32), jax.ShapeDtypeStruct((1, D_GRP), F32),
                   jax.ShapeDtypeStruct((1, D_GRP), F32),
                   _view_shape(s_len, 4, BF16), _view_shape(s_len, 4, F32),
                   _view_shape(s_len, 16, BF16), _view_shape(s_len, 16, F32)),
        scratch_shapes=[pltpu.VMEM((D_MODEL, D_MODEL), F32), pltpu.VMEM((8, tm, 128), F32),
                        pltpu.VMEM((8, tm, 128), F32)],
        compiler_params=_params(),
    )(dy, dh2, x1, wn2, b_g, mixed, o_dil, o_sb, wdil, wsb, bd512)


def _qkv_bwd(dq_b, dk_b, dv_b, dqs, dks, dvs, qraw, kraw, cos2, sin2, qnw, knw, bd, rides):
    s_len = qraw.shape[0]
    tm = ROW_TILE
    ni = s_len // tm

    def body(dq1, dk1, dv1, dq4, dk4, dv4, dq16, dk16, dv16, dqs_ref, dks_ref, dvs_ref,
             qraw_ref, kraw_ref, cos_ref, sin_ref, qnw_ref, knw_ref, bd_ref,
             dproj_ref, dqn_ref, dkn_ref, stage, nat4, nat16):
        i = pl.program_id(0)

        @pl.when(i == 0)
        def _():
            dqn_ref[...] = jnp.zeros_like(dqn_ref)
            dkn_ref[...] = jnp.zeros_like(dkn_ref)

        _merge_views((dq4, dk4, dv4), (dq16, dk16, dv16), stage, nat4, nat16)
        cos_t, sin_t, bdm = cos_ref[...], sin_ref[...], bd_ref[...]
        for grp, (part1, raw_ref, nw_ref, dn_ref) in enumerate(((dq1, qraw_ref, qnw_ref, dqn_ref),
                                                                (dk1, kraw_ref, knw_ref, dkn_ref))):
            dn_acc = 0.0
            for p in range(4):
                cols = slice(128 * p, 128 * (p + 1))
                d_rope = part1[:, cols] + nat4[4 * grp + p] + nat16[4 * grp + p]
                d_norm = d_rope * cos_t + _swap_halves(d_rope * sin_t)
                t = raw_ref[:, cols]
                w = nw_ref[...]
                r = lax.rsqrt(_mm_split(t * t, bdm) * (1.0 / HEAD_DIM) + EPS)
                gw = d_norm * w
                corr = _mm_split(gw * t, bdm) * (1.0 / HEAD_DIM)
                dt = r * (gw - t * ((r * r) * corr))
                dn_acc = dn_acc + jnp.sum(d_norm * t * r, axis=0, keepdims=True)
                dproj_ref[:, D_GRP * grp + 128 * p:D_GRP * grp + 128 * (p + 1)] = dt.astype(BF16)
            dn_ref[...] += dn_acc
        dproj_ref[:, 2 * D_GRP:3 * D_GRP] = (dv1[...] + _slab_group(nat4, 2) + _slab_group(nat16, 2)).astype(BF16)
        dproj_ref[:, 3 * D_GRP:4 * D_GRP] = dqs_ref[...].astype(BF16)
        dproj_ref[:, 4 * D_GRP:5 * D_GRP] = dks_ref[...].astype(BF16)
        dproj_ref[:, 5 * D_GRP:6 * D_GRP] = dvs_ref[...].astype(BF16)

    row = lambda w: pl.BlockSpec((tm, w), lambda i: (i, 0))
    return _call_with_exchange(
        body, rides, lambda: pl.program_id(0) == 0, lambda: pl.program_id(0) == ni - 1,
        name="qkv_bwd", grid=(ni,),
        in_specs=[row(D_GRP)] * 3 + [_view_spec(tm, 4)] * 3 + [_view_spec(tm, 16)] * 3 + [row(D_GRP)] * 5
        + [row(128), row(128), _full((1, 128)), _full((1, 128)), _full((128, 128))],
        out_specs=(row(D_IN), _full((1, 128)), _full((1, 128))),
        out_shape=(jax.ShapeDtypeStruct((s_len, D_IN), BF16), jax.ShapeDtypeStruct((1, 128), F32),
                   jax.ShapeDtypeStruct((1, 128), F32)),
        scratch_shapes=[pltpu.VMEM((12, tm, 128), F32)] * 3,
    )(dq_b[0], dk_b[0], dv_b[0], dq_b[1], dk_b[1], dv_b[1], dq_b[2], dk_b[2], dv_b[2],
      dqs, dks, dvs, qraw, kraw, cos2, sin2, qnw, knw, bd, *rides)


def _in_bwd_dx(dproj, a_g, x2, dx1, wn1, rides):
    s_len = x2.shape[0]
    tm = ROW_TILE
    ni = s_len // tm

    def body(dp_ref, w_ref, x_ref, dx1_ref, wn_ref, gx_ref, dwn_ref, w_full):
        i = pl.program_id(0)

        @pl.when(i == 0)
        def _():
            dwn_ref[...] = jnp.zeros_like(dwn_ref)
            for d in range(N_DEV):
                w_full[:, IN_SHARD * d:IN_SHARD * (d + 1)] = w_ref[d]

        dh = _dot_nt(dp_ref[...], w_full[...])
        dnorm, dw_rows = _rms_bwd(dh, x_ref[...], wn_ref[...])
        gx_ref[...] = dx1_ref[...] + dnorm
        dwn_ref[...] += jnp.sum(dw_rows, axis=0, keepdims=True)

    row = lambda w: pl.BlockSpec((tm, w), lambda i: (i, 0))
    return _call_with_exchange(
        body, rides, lambda: pl.program_id(0) == 0, lambda: pl.program_id(0) == ni - 1,
        name="in_bwd_dx", grid=(ni,),
        in_specs=[row(D_IN), pl.BlockSpec((N_DEV, D_MODEL, IN_SHARD), lambda i: (0, 0, 0)),
                  row(D_MODEL), row(D_MODEL), _full((1, D_MODEL))],
        out_specs=(row(D_MODEL), _full((1, D_MODEL))),
        out_shape=(jax.ShapeDtypeStruct((s_len, D_MODEL), F32), jax.ShapeDtypeStruct((1, D_MODEL), F32)),
        scratch_shapes=[pltpu.VMEM((D_MODEL, D_IN), BF16)],
    )(dproj, a_g, x2, dx1, wn1, *rides)


def _in_bwd_dw(h1, dproj):
    s_len = h1.shape[0]
    tm = DW_ROW_TILE
    ni = s_len // tm

    half_d = D_MODEL // 2

    def body(h_ref, dp_ref, lo_ref, hi_ref, acc):
        i = pl.program_id(1)

        @pl.when(i == 0)
        def _():
            acc[...] = jnp.zeros_like(acc)

        acc[...] += _dot_tn(h_ref[...], dp_ref[...])

        @pl.when(i == ni - 1)
        def _():
            for dev in range(2):
                cols = slice(IN_SHARD * dev, IN_SHARD * (dev + 1))
                lo_ref[dev] = acc[0:half_d, cols].astype(BF16)
                hi_ref[dev] = acc[half_d:D_MODEL, cols].astype(BF16)

    half_w = pl.BlockSpec((2, half_d, IN_SHARD), lambda d, i: (d, 0, 0))
    half_grad = jax.ShapeDtypeStruct((N_DEV, half_d, IN_SHARD), BF16)
    return pl.pallas_call(
        body, name="in_bwd_dw", grid=(N_DEV // 2, ni),
        in_specs=[pl.BlockSpec((tm, D_MODEL), lambda d, i: (i, 0)),
                  pl.BlockSpec((tm, 2 * IN_SHARD), lambda d, i: (i, d))],
        out_specs=(half_w, half_w),
        out_shape=(half_grad, half_grad),
        scratch_shapes=[pltpu.VMEM((D_MODEL, 2 * IN_SHARD), F32)],
        compiler_params=_params(),
    )(h1, dproj)


def _adamw(recv, w, m, v, recv_hi=None):
    rows, cols = w.shape
    tr = next((t for t in (128, 32) if rows % t == 0), rows)
    recvs = [recv] if recv_hi is None else [recv, recv_hi]
    lo_tiles = recv.shape[1] // tr

    def body(*refs):
        p_refs = refs[:len(recvs)]
        w_ref, m_ref, v_ref, g_ref, d_ref, nm_ref, nv_ref = refs[len(recvs):]

        def slot(s):
            if len(p_refs) == 1:
                return p_refs[0][s].astype(F32)
            return jnp.where(pl.program_id(0) < lo_tiles, p_refs[0][s], p_refs[1][s]).astype(F32)

        g = slot(0)
        for s in range(1, N_DEV):
            g = g + slot(s)
        m_new = ADAM_B1 * m_ref[...] + (1.0 - ADAM_B1) * g
        v_new = ADAM_B2 * v_ref[...] + (1.0 - ADAM_B2) * (g * g)
        m_hat = m_new / (1.0 - ADAM_B1 ** ADAM_STEP)
        v_hat = v_new / (1.0 - ADAM_B2 ** ADAM_STEP)
        g_ref[...] = g
        d_ref[...] = -ADAM_LR * (m_hat / (jnp.sqrt(v_hat) + ADAM_EPS) + ADAM_WD * w_ref[...])
        nm_ref[...] = m_new
        nv_ref[...] = v_new

    blk = pl.BlockSpec((tr, cols), lambda i: (i, 0))
    out = jax.ShapeDtypeStruct((rows, cols), F32)
    return pl.pallas_call(
        body, name=f"adamw_{rows}x{cols}", grid=(rows // tr,),
        in_specs=([pl.BlockSpec((N_DEV, tr, cols), lambda i: (0, i, 0))] if recv_hi is None else
                  [pl.BlockSpec((N_DEV, tr, cols), lambda i: (0, jnp.minimum(i, lo_tiles - 1), 0)),
                   pl.BlockSpec((N_DEV, tr, cols), lambda i: (0, jnp.maximum(i - lo_tiles, 0), 0))])
        + [blk, blk, blk],
        out_specs=(blk,) * 4, out_shape=(out,) * 4,
        compiler_params=_params(),
    )(*recvs, w, m, v)


def _adamw_many(items, rides):
    tr = 32
    tiles = [w.shape[0] // tr for _, w, _, _ in items]
    starts = [sum(tiles[:k]) for k in range(len(items))]
    total = sum(tiles)
    n_items = len(items)

    def body(*refs):
        i = pl.program_id(0)
        in_refs, out_refs = refs[:4 * n_items], refs[4 * n_items:]
        for k in range(n_items):
            def update(k=k):
                p_ref, w_ref, m_ref, v_ref = in_refs[4 * k:4 * k + 4]
                g_ref, d_ref, nm_ref, nv_ref = out_refs[4 * k:4 * k + 4]
                g = p_ref[0].astype(F32)
                for s in range(1, N_DEV):
                    g = g + p_ref[s].astype(F32)
                m_new = ADAM_B1 * m_ref[...] + (1.0 - ADAM_B1) * g
                v_new = ADAM_B2 * v_ref[...] + (1.0 - ADAM_B2) * (g * g)
                m_hat = m_new / (1.0 - ADAM_B1 ** ADAM_STEP)
                v_hat = v_new / (1.0 - ADAM_B2 ** ADAM_STEP)
                g_ref[...] = g
                d_ref[...] = -ADAM_LR * (m_hat / (jnp.sqrt(v_hat) + ADAM_EPS) + ADAM_WD * w_ref[...])
                nm_ref[...] = m_new
                nv_ref[...] = v_new

            pl.when(jnp.logical_and(i >= starts[k], i < starts[k] + tiles[k]))(update)

    in_specs, out_specs, out_shape, args = [], [], [], []
    for k, (recv, w, m, v) in enumerate(items):
        tile_of = functools.partial(lambda i, s0, nk: jnp.clip(i - s0, 0, nk - 1), s0=starts[k], nk=tiles[k])
        blk = pl.BlockSpec((tr, D_MODEL), functools.partial(lambda i, t: (t(i), 0), t=tile_of))
        in_specs += [pl.BlockSpec((N_DEV, tr, D_MODEL), functools.partial(lambda i, t: (0, t(i), 0), t=tile_of)),
                     blk, blk, blk]
        out_specs += [blk] * 4
        out_shape += [jax.ShapeDtypeStruct(w.shape, F32)] * 4
        args += [recv, w, m, v]
    res = _call_with_exchange(
        body, rides, lambda: pl.program_id(0) == 0, lambda: pl.program_id(0) == total - 1,
        name="adamw_many", grid=(total,), in_specs=in_specs, out_specs=out_specs, out_shape=out_shape,
    )(*args, *rides)
    return [tuple(res[4 * k:4 * k + 4]) for k in range(n_items)], list(res[4 * n_items:])


def _rope_tables(s_len):
    pos = jnp.arange(s_len, dtype=F32)
    inv_freq = ROPE_THETA ** (-jnp.arange(0, HEAD_DIM, 2, dtype=F32) / HEAD_DIM)
    ang = pos[:, None] * inv_freq[None, :]
    cos, sin = jnp.cos(ang), jnp.sin(ang)
    cos2 = jnp.concatenate([cos, cos, cos, cos], axis=1)
    sin2 = jnp.concatenate([-sin, sin, -sin, sin], axis=1)
    return cos2, sin2


def _block_diag_ones(n):
    i = jnp.arange(n)
    return (i[:, None] // HEAD_DIM == i[None, :] // HEAD_DIM).astype(BF16)


def _pad_cols(t):
    return jnp.pad(t, ((0, 0), (0, FF_PAD - FF_SHARD)))


def _pad_rows(t):
    return jnp.pad(t, ((0, FF_PAD - FF_SHARD), (0, 0)))


LOSS_ROW = 26


def _pack_small(n1, n2, ndil, nsb, nq, nk, scalar=None):
    pad = lambda t: jnp.pad(t.reshape(1, -1), ((0, 0), (0, 128 - t.size)))
    last = jnp.zeros((1, 128), F32) if scalar is None else pad(scalar)
    rows = [n1.reshape(8, 128), n2.reshape(8, 128), ndil.reshape(4, 128), nsb.reshape(4, 128),
            pad(nq), pad(nk), last, jnp.zeros((5, 128), F32)]
    return jnp.concatenate(rows, axis=0)


def _unpack_small(t):
    return (t[0:8].reshape(1, D_MODEL), t[8:16].reshape(1, D_MODEL), t[16:20].reshape(1, D_GRP),
            t[20:24].reshape(1, D_GRP), t[24:25, :HEAD_DIM], t[25:26, :HEAD_DIM])


def kernel(x, attn_norm_w, w_in, q_norm_w, k_norm_w, dil_out_norm_w, sb_out_norm_w, w_out, ffn_norm_w, w_gate, w_up, w_down, loss_target, m_attn_norm_w, m_w_in, m_q_norm_w, m_k_norm_w, m_dil_out_norm_w, m_sb_out_norm_w, m_w_out, m_ffn_norm_w, m_w_gate, m_w_up, m_w_down, v_attn_norm_w, v_w_in, v_q_norm_w, v_k_norm_w, v_dil_out_norm_w, v_sb_out_norm_w, v_w_out, v_ffn_norm_w, v_w_gate, v_w_up, v_w_down):
    s_len = x.shape[1]
    x2, tgt = x[0], loss_target[0]

    (a_g,) = _gather_weights([w_in[0].astype(BF16)])
    gate_loc = _pad_cols(w_gate[0]).T.astype(BF16)
    up_loc = _pad_cols(w_up[0]).T.astype(BF16)
    down_loc = _pad_rows(w_down[0]).astype(BF16)
    out_loc = w_out[0].astype(BF16)

    cos2, sin2 = _rope_tables(s_len)
    bd128, bd512 = _block_diag_ones(128), _block_diag_ones(D_GRP)
    idx = jnp.arange(SB_TILE)
    tri_suf = (idx[:, None] > idx[None, :]).astype(BF16)
    tri_pre = (idx[:, None] < idx[None, :]).astype(BF16)
    qnw2 = jnp.concatenate([q_norm_w, q_norm_w], axis=1)
    knw2 = jnp.concatenate([k_norm_w, k_norm_w], axis=1)

    (h1, qraw, kraw, q, k, va, qs, ks, vs, q4, k4, v4, q16, k16, v16,
     gate_g) = _attn_in(x2, attn_norm_w, a_g, cos2, sin2, qnw2, knw2, bd128, shards=[gate_loc])
    qkv_views = {1: (q, k, va), 4: (q4, k4, v4), 16: (q16, k16, v16)}
    fwd_riders = {1: [out_loc], 4: [], 16: []}
    o_b, lse_b, gathered = [], [], {}
    for r in DILATIONS:
        o, lse, *gathered[r] = _dil_fwd(*qkv_views[r], r, shards=fwd_riders[r])
        o_b.append(o)
        lse_b.append(lse)
    (out_g,) = gathered[1]
    o_sb, c_sb, up_g = _sb_fwd(qs, ks, vs, tri_suf, shards=[up_loc])
    o_dil, lse_tot, lse4, lse16, mixed, x1, down_g = _attn_out(
        o_b, lse_b, o_sb, x2, dil_out_norm_w, sb_out_norm_w, out_g, shards=[down_loc])
    g, u, h2, dy, loss_parts = _ffn_fwd(x1, ffn_norm_w, tgt, gate_g, up_g, down_g)
    loss_local = jnp.sum(loss_parts[::8, 0])

    dg, du, act, dh2 = _ffn_bwd_dx(dy, g, u, gate_g, up_g, down_g)
    (dx1, do_dil, delta, do_sb, dwout, dn2, dndil, dnsb, do4, dl4, do16, dl16) = _attn_out_bwd(
        dy, dh2, x1, ffn_norm_w, out_g, mixed, o_dil, o_sb, dil_out_norm_w, sb_out_norm_w, bd512)
    dwg, dwu, dwd_lo, dwd_hi = _ffn_bwd_dw(h2, dy, dg, du, act)
    dqs, dks, dvs, r_gate = _sb_bwd(qs, ks, vs, do_sb, c_sb, tri_suf, tri_pre, rides=[dwg])
    cot_views = {1: (do_dil, lse_tot, delta), 4: (do4, lse4, dl4), 16: (do16, lse16, dl16)}
    riders = {1: [dwd_lo], 4: [dwd_hi], 16: [dwu]}
    dq_b, dk_b, dv_b, landed = [], [], [], {}
    for r in DILATIONS:
        dq, dk, dv, *landed[r] = _dil_bwd(*qkv_views[r], *cot_views[r], r, rides=riders[r])
        dq_b.append(dq)
        dk_b.append(dk)
        dv_b.append(dv)
    (r_down_lo,), (r_down_hi,), (r_up,) = landed[1], landed[4], landed[16]
    dproj, dqn2, dkn2, r_out = _qkv_bwd(dq_b, dk_b, dv_b, dqs, dks, dvs, qraw, kraw, cos2, sin2, qnw2, knw2,
                                        bd128, rides=[dwout])
    dwin_lo, dwin_hi = _in_bwd_dw(h1, dproj)
    grad_x, dn1, r_in_lo = _in_bwd_dx(dproj, a_g, x2, dx1, attn_norm_w, rides=[dwin_lo])
    dqn = dqn2[:, :HEAD_DIM] + dqn2[:, HEAD_DIM:]
    dkn = dkn2[:, :HEAD_DIM] + dkn2[:, HEAD_DIM:]

    small = _pack_small(dn1, dn2, dndil, dnsb, dqn, dkn, loss_local)
    (r_small,) = _exchange_grads([], small)
    (new_gate, new_up, new_out), (r_in_hi,) = _adamw_many(
        [(r_gate, w_gate[0].T, m_w_gate[0].T, v_w_gate[0].T), (r_up, w_up[0].T, m_w_up[0].T, v_w_up[0].T),
         (r_out, w_out[0], m_w_out[0], v_w_out[0])], rides=[dwin_hi])
    big = {
        "w_in": _adamw(r_in_lo, w_in[0], m_w_in[0], v_w_in[0], recv_hi=r_in_hi),
        "w_gate": tuple(t.T for t in new_gate),
        "w_up": tuple(t.T for t in new_up),
        "w_down": _adamw(r_down_lo, w_down[0], m_w_down[0], v_w_down[0], recv_hi=r_down_hi),
        "w_out": new_out,
    }
    packs = [_pack_small(*ts) for ts in (
        (attn_norm_w, ffn_norm_w, dil_out_norm_w, sb_out_norm_w, q_norm_w, k_norm_w),
        (m_attn_norm_w, m_ffn_norm_w, m_dil_out_norm_w, m_sb_out_norm_w, m_q_norm_w, m_k_norm_w),
        (v_attn_norm_w, v_ffn_norm_w, v_dil_out_norm_w, v_sb_out_norm_w, v_q_norm_w, v_k_norm_w))]
    small_raw = _adamw(r_small, *packs)
    loss = small_raw[0][LOSS_ROW, 0]
    small_out = [_unpack_small(t) for t in small_raw]
    names = ["attn_norm_w", "w_in", "q_norm_w", "k_norm_w", "dil_out_norm_w", "sb_out_norm_w", "w_out",
             "ffn_norm_w", "w_gate", "w_up", "w_down"]
    small_pos = {"attn_norm_w": 0, "ffn_norm_w": 1, "dil_out_norm_w": 2, "sb_out_norm_w": 3,
                 "q_norm_w": 4, "k_norm_w": 5}
    outs = [loss, grad_x[None]]
    for kind in range(4):
        for name in names:
            if name in small_pos:
                outs.append(small_out[kind][small_pos[name]])
            else:
                outs.append(big[name][kind][None])
    return tuple(outs)
```

```python
import functools

import jax
import jax.numpy as jnp
from jax import lax
from jax.experimental import pallas as pl
from jax.experimental.pallas import tpu as pltpu

F32 = jnp.float32
BF16 = jnp.bfloat16

N_DEV = 8
D_MODEL = 1024
HEAD_DIM = 64
D_GRP = 512
D_IN = 6 * D_GRP
IN_SHARD = D_IN // N_DEV
FF_SHARD = 352
FF_PAD = 384
FF_BLOCK = 2 * FF_PAD
FF_STEPS = N_DEV // 2
OUT_SHARD = D_MODEL // N_DEV
BLOCK = 128
DILATIONS = (1, 4, 16)
ROPE_THETA = 10000.0
EPS = 1e-6
ATT_SCALE = HEAD_DIM ** -0.5
NEG = -1e30

ADAM_LR = 0.001
ADAM_B1 = 0.9
ADAM_B2 = 0.999
ADAM_EPS = 1e-08
ADAM_WD = 0.01
ADAM_STEP = 10

SB_TILE = 256
SB_DEAD = -104.0
SB_PAIRS = 4
SB_BWD_PAIRS = 2
ROW_TILE = 512
DW_ROW_TILE = 1024
VMEM_LIMIT = 56 * 1024 * 1024
MESH = pl.DeviceIdType.MESH


def _dot(a, b):
    return jnp.dot(a, b, preferred_element_type=F32)


def _dot_nt(a, b):
    return lax.dot_general(a, b, (((1,), (1,)), ((), ())), preferred_element_type=F32)


def _dot_tn(a, b):
    return lax.dot_general(a, b, (((0,), (0,)), ((), ())), preferred_element_type=F32)


def _mm_split(t, m):
    hi = t.astype(BF16)
    lo = (t - hi.astype(F32)).astype(BF16)
    return _dot(hi, m) + _dot(lo, m)


def _params(**kw):
    return pltpu.CompilerParams(vmem_limit_bytes=VMEM_LIMIT, **kw)


def _full(shape):
    nd = len(shape)
    return pl.BlockSpec(shape, lambda *_: (0,) * nd)


def _view_shape(s_len, r, dtype):
    return jax.ShapeDtypeStruct((s_len // r, r * D_GRP), dtype)


def _view_spec(tm, r):
    return pl.BlockSpec((tm // r, r * D_GRP), lambda i: (i, 0))


def _swap_halves(t):
    lane = lax.broadcasted_iota(jnp.int32, t.shape, 1)
    first = (lane & 32) == 0
    return jnp.where(first, pltpu.roll(t, 96, 1), pltpu.roll(t, 32, 1))


def _log_sigmoid_pair(z):
    neg_abs = lax.bitcast_convert_type(lax.bitcast_convert_type(z, jnp.uint32) | jnp.uint32(0x80000000), F32)
    lb = jnp.minimum(z, 0.0) - jnp.log(1.0 + jnp.exp(neg_abs))
    return lb, lb - z


def _cumsum_mm(t, tri):
    return _dot(t.astype(BF16), tri)


def _split_views(src_ref, stage_ref, views4, views16):
    slabs, n, _ = src_ref.shape
    n4, n16 = n // 4, n // 16
    for j in range(slabs):
        g, lanes = j // 4, 128 * (j % 4)
        src, stage = src_ref.at[j], stage_ref.at[j]
        for c4 in range(4):
            blk = src[pl.ds(c4, n4, stride=4), :]
            stage[n4 * c4:n4 * (c4 + 1), :] = blk
            col = D_GRP * c4 + lanes
            views4[g][:, col:col + 128] = blk.astype(views4[g].dtype)
        for c4 in range(4):
            for c1 in range(4):
                blk = stage[pl.ds(n4 * c4 + c1, n16, stride=4), :]
                col = D_GRP * (4 * c1 + c4) + lanes
                views16[g][:, col:col + 128] = blk.astype(views16[g].dtype)


def _merge_views(views4, views16, stage_ref, dst4_ref, dst16_ref):
    slabs, n, _ = dst4_ref.shape
    n4, n16 = n // 4, n // 16
    for j in range(slabs):
        g, lanes = j // 4, 128 * (j % 4)
        dst4, dst16, stage = dst4_ref.at[j], dst16_ref.at[j], stage_ref.at[j]
        for c4 in range(4):
            col = D_GRP * c4 + lanes
            dst4[pl.ds(c4, n4, stride=4), :] = views4[g][:, col:col + 128].astype(F32)
            for c1 in range(4):
                col = D_GRP * (4 * c1 + c4) + lanes
                stage[pl.ds(n4 * c4 + c1, n16, stride=4), :] = views16[g][:, col:col + 128].astype(F32)
        for c4 in range(4):
            dst16[pl.ds(c4, n4, stride=4), :] = stage[n4 * c4:n4 * (c4 + 1), :]


def _slab_group(ref, g):
    return jnp.concatenate([ref[4 * g + p] for p in range(4)], axis=1)


def _mesh_pos():
    return lax.axis_index("x"), lax.axis_index("y"), lax.axis_index("c")


def _flat_index(p):
    return 4 * p[0] + 2 * p[1] + p[2]


def _gather_weights(shards):
    n_arr = len(shards)

    def body(*refs):
        srcs, outs = refs[:n_arr], refs[n_arr:2 * n_arr]
        send_sems, recv_sems, local_sems = refs[2 * n_arr:]
        x, y, c = _mesh_pos()
        me, sibling = (x, y, c), (x, y, 1 - c)
        chips = [(1 - x, y), (x, 1 - y), (1 - x, 1 - y)]

        def copy(arr, k, block, to, own=False):
            dst = outs[arr].at[_flat_index(block)]
            return pltpu.make_async_remote_copy(
                src_ref=srcs[arr] if own else dst, dst_ref=dst,
                send_sem=send_sems.at[arr, k], recv_sem=recv_sems.at[arr, k],
                device_id=to, device_id_type=MESH)

        for arr in range(n_arr):
            mine = pltpu.make_async_copy(srcs[arr], outs[arr].at[_flat_index(me)], local_sems.at[arr])
            mine.start()
            first = [copy(arr, 0, me, sibling, own=True)]
            first += [copy(arr, 1 + j, me, (*chip, c), own=True) for j, chip in enumerate(chips)]
            for cp in first:
                cp.start()
        for arr in range(n_arr):
            passed = [copy(arr, 4 + j, (*chip, c), sibling) for j, chip in enumerate(chips)]
            for j, chip in enumerate(chips):
                copy(arr, 1 + j, (*chip, c), me).wait_recv()
                passed[j].start()
        for arr in range(n_arr):
            copy(arr, 0, sibling, me).wait_recv()
            for j, chip in enumerate(chips):
                copy(arr, 4 + j, (*chip, 1 - c), me).wait_recv()
            for k in range(7):
                copy(arr, k, me, me).wait_send()
            pltpu.make_async_copy(srcs[arr], outs[arr].at[_flat_index(me)], local_sems.at[arr]).wait()

    any_spec = pl.BlockSpec(memory_space=pl.ANY)
    return pl.pallas_call(
        body, name="gather_weights",
        out_shape=tuple(jax.ShapeDtypeStruct((N_DEV,) + s.shape, s.dtype) for s in shards),
        in_specs=[any_spec] * n_arr, out_specs=(any_spec,) * n_arr,
        scratch_shapes=[pltpu.SemaphoreType.DMA((n_arr, 7)), pltpu.SemaphoreType.DMA((n_arr, 7)),
                        pltpu.SemaphoreType.DMA((n_arr,))],
        compiler_params=pltpu.CompilerParams(has_side_effects=True),
    )(*shards)


def _peer_list(x, y, c):
    return [(1 - x if m & 4 else x, 1 - y if m & 2 else y, 1 - c if m & 1 else c) for m in range(1, N_DEV)]


def _exchange_grads(parts, small):
    n_arr = len(parts)

    def body(*refs):
        ins, outs = refs[:n_arr + 1], refs[n_arr + 1:2 * (n_arr + 1)]
        send_sems, recv_sems, local_sems = refs[2 * (n_arr + 1):]
        x, y, c = _mesh_pos()
        me = (x, y, c)
        my_idx = _flat_index(me)
        peers = []
        for m in range(1, N_DEV):
            peers.append((1 - x if m & 4 else x, 1 - y if m & 2 else y, 1 - c if m & 1 else c))

        def src_block(arr, dev):
            return ins[arr] if arr == n_arr else ins[arr].at[_flat_index(dev)]

        def copy(arr, k):
            return pltpu.make_async_remote_copy(
                src_ref=src_block(arr, peers[k]), dst_ref=outs[arr].at[my_idx],
                send_sem=send_sems.at[arr, k], recv_sem=recv_sems.at[arr, k],
                device_id=peers[k], device_id_type=MESH)

        def local(arr):
            return pltpu.make_async_copy(src_block(arr, me), outs[arr].at[my_idx], local_sems.at[arr])

        for arr in range(n_arr + 1):
            local(arr).start()
            for k in range(N_DEV - 1):
                copy(arr, k).start()
        for arr in range(n_arr + 1):
            for k in range(N_DEV - 1):
                cp = copy(arr, k)
                cp.wait_send()
                cp.wait_recv()
            local(arr).wait()

    any_spec = pl.BlockSpec(memory_space=pl.ANY)
    out_shape = tuple(jax.ShapeDtypeStruct(p.shape, p.dtype) for p in parts)
    out_shape += (jax.ShapeDtypeStruct((N_DEV,) + small.shape, small.dtype),)
    return pl.pallas_call(
        body, name="exchange_grads",
        out_shape=out_shape,
        in_specs=[any_spec] * (n_arr + 1), out_specs=(any_spec,) * (n_arr + 1),
        scratch_shapes=[pltpu.SemaphoreType.DMA((n_arr + 1, N_DEV - 1)),
                        pltpu.SemaphoreType.DMA((n_arr + 1, N_DEV - 1)),
                        pltpu.SemaphoreType.DMA((n_arr + 1,))],
        compiler_params=pltpu.CompilerParams(has_side_effects=True),
    )(*parts, small)


def _call_with_gather(body, shards, first_step, mid_step, last_step, *, name, grid, in_specs, out_specs,
                      out_shape, scratch_shapes=()):
    out_specs = tuple(out_specs) if isinstance(out_specs, (tuple, list)) else (out_specs,)
    out_shape = tuple(out_shape) if isinstance(out_shape, (tuple, list)) else (out_shape,)
    n_in, n_out, n_scr, n = len(in_specs), len(out_specs), len(scratch_shapes), len(shards)
    if n == 0:
        return pl.pallas_call(body, name=name, grid=grid, in_specs=list(in_specs), out_specs=out_specs,
                              out_shape=out_shape, scratch_shapes=list(scratch_shapes),
                              compiler_params=_params())

    def full_body(*refs):
        ins, srcs = refs[:n_in], refs[n_in:n_in + n]
        outs, lands = refs[n_in + n:n_in + n + n_out], refs[n_in + n + n_out:n_in + 2 * n + n_out]
        scratch = refs[n_in + 2 * n + n_out:n_in + 2 * n + n_out + n_scr]
        send_sems, recv_sems, local_sems = refs[-3:]
        x, y, c = _mesh_pos()
        me, sibling = (x, y, c), (x, y, 1 - c)
        chips = [(1 - x, y), (x, 1 - y), (1 - x, 1 - y)]

        def copy(a, k, block, to, own=False):
            dst = lands[a].at[_flat_index(block)]
            return pltpu.make_async_remote_copy(
                src_ref=srcs[a] if own else dst, dst_ref=dst,
                send_sem=send_sems.at[a, k], recv_sem=recv_sems.at[a, k],
                device_id=to, device_id_type=MESH)

        def local(a):
            return pltpu.make_async_copy(srcs[a], lands[a].at[_flat_index(me)], local_sems.at[a])

        @pl.when(first_step())
        def _():
            for a in range(n):
                local(a).start()
                copy(a, 0, me, sibling, own=True).start()
                for j, chip in enumerate(chips):
                    copy(a, 1 + j, me, (*chip, c), own=True).start()

        @pl.when(mid_step())
        def _():
            for a in range(n):
                for j, chip in enumerate(chips):
                    copy(a, 1 + j, (*chip, c), me).wait_recv()
                    copy(a, 4 + j, (*chip, c), sibling).start()

        body(*ins, *outs, *scratch)

        @pl.when(last_step())
        def _():
            for a in range(n):
                copy(a, 0, sibling, me).wait_recv()
                for j, chip in enumerate(chips):
                    copy(a, 4 + j, (*chip, 1 - c), me).wait_recv()
                for k in range(N_DEV - 1):
                    copy(a, k, me, me).wait_send()
                local(a).wait()

    any_spec = pl.BlockSpec(memory_space=pl.ANY)
    return pl.pallas_call(
        full_body, name=name, grid=grid,
        in_specs=list(in_specs) + [any_spec] * n,
        out_specs=out_specs + (any_spec,) * n,
        out_shape=out_shape + tuple(jax.ShapeDtypeStruct((N_DEV,) + t.shape, t.dtype) for t in shards),
        scratch_shapes=list(scratch_shapes) + [pltpu.SemaphoreType.DMA((n, N_DEV - 1)),
                                               pltpu.SemaphoreType.DMA((n, N_DEV - 1)),
                                               pltpu.SemaphoreType.DMA((n,))],
        compiler_params=_params(has_side_effects=True),
    )


def _call_with_exchange(body, rides, first_step, last_step, *, name, grid, in_specs, out_specs, out_shape,
                        scratch_shapes=()):
    out_specs = tuple(out_specs) if isinstance(out_specs, (tuple, list)) else (out_specs,)
    out_shape = tuple(out_shape) if isinstance(out_shape, (tuple, list)) else (out_shape,)
    n_in, n_out, n_scr, n = len(in_specs), len(out_specs), len(scratch_shapes), len(rides)
    if n == 0:
        return pl.pallas_call(body, name=name, grid=grid, in_specs=list(in_specs), out_specs=out_specs,
                              out_shape=out_shape, scratch_shapes=list(scratch_shapes),
                              compiler_params=_params())

    def full_body(*refs):
        ins, srcs = refs[:n_in], refs[n_in:n_in + n]
        outs, lands = refs[n_in + n:n_in + n + n_out], refs[n_in + n + n_out:n_in + 2 * n + n_out]
        scratch = refs[n_in + 2 * n + n_out:n_in + 2 * n + n_out + n_scr]
        send_sems, recv_sems, local_sems = refs[-3:]
        x, y, c = _mesh_pos()
        my_idx = _flat_index((x, y, c))
        peers = _peer_list(x, y, c)

        def remote(a, k):
            return pltpu.make_async_remote_copy(
                src_ref=srcs[a].at[_flat_index(peers[k])], dst_ref=lands[a].at[my_idx],
                send_sem=send_sems.at[a, k], recv_sem=recv_sems.at[a, k],
                device_id=peers[k], device_id_type=MESH)

        def local(a):
            return pltpu.make_async_copy(srcs[a].at[my_idx], lands[a].at[my_idx], local_sems.at[a])

        @pl.when(first_step())
        def _():
            for a in range(n):
                local(a).start()
                for k in range(N_DEV - 1):
                    remote(a, k).start()

        body(*ins, *outs, *scratch)

        @pl.when(last_step())
        def _():
            for a in range(n):
                for k in range(N_DEV - 1):
                    cp = remote(a, k)
                    cp.wait_send()
                    cp.wait_recv()
                local(a).wait()

    any_spec = pl.BlockSpec(memory_space=pl.ANY)
    res = pl.pallas_call(
        full_body, name=name, grid=grid,
        in_specs=list(in_specs) + [any_spec] * n,
        out_specs=out_specs + (any_spec,) * n,
        out_shape=out_shape + tuple(jax.ShapeDtypeStruct(t.shape, t.dtype) for t in rides),
        scratch_shapes=list(scratch_shapes) + [pltpu.SemaphoreType.DMA((n, N_DEV - 1)),
                                               pltpu.SemaphoreType.DMA((n, N_DEV - 1)),
                                               pltpu.SemaphoreType.DMA((n,))],
        compiler_params=_params(has_side_effects=True),
    )
    return res


def _head_norm(t, w128, bd):
    ms = _mm_split(t * t, bd) * (1.0 / HEAD_DIM)
    r = lax.rsqrt(ms + EPS)
    return (t * r) * w128, r


def _attn_in(x2, wn1, a_g, cos2, sin2, qnw, knw, bd, shards):
    s_len = x2.shape[0]
    tm = ROW_TILE

    def body(x_ref, wn_ref, w_ref, cos_ref, sin_ref, qnw_ref, knw_ref, bd_ref,
             h1_ref, qraw_ref, kraw_ref, q_ref, k_ref, va_ref, qs_ref, ks_ref, vs_ref,
             q4_ref, k4_ref, v4_ref, q16_ref, k16_ref, v16_ref, proj, slabs, stage, w_full):
        @pl.when(pl.program_id(0) == 0)
        def _():
            for d in range(N_DEV):
                w_full[:, IN_SHARD * d:IN_SHARD * (d + 1)] = w_ref[d]

        xx = x_ref[...]
        r = lax.rsqrt(jnp.mean(xx * xx, axis=-1, keepdims=True) + EPS)
        h = ((xx * r) * wn_ref[...]).astype(BF16)
        h1_ref[...] = h
        proj[...] = _dot(h, w_full[...])
        cos_t, sin_t, bdm = cos_ref[...], sin_ref[...], bd_ref[...]
        for grp, (raw_ref, rope_ref, nw_ref) in enumerate(((qraw_ref, q_ref, qnw_ref),
                                                           (kraw_ref, k_ref, knw_ref))):
            for p in range(4):
                cols = slice(D_GRP * grp + 128 * p, D_GRP * grp + 128 * (p + 1))
                t = proj[:, cols]
                raw_ref[:, 128 * p:128 * (p + 1)] = t
                yn, _ = _head_norm(t, nw_ref[...], bdm)
                roped = yn * cos_t + _swap_halves(yn) * sin_t
                slabs[4 * grp + p] = roped
                rope_ref[:, 128 * p:128 * (p + 1)] = roped.astype(BF16)
        for p in range(4):
            slabs[8 + p] = proj[:, 2 * D_GRP + 128 * p:2 * D_GRP + 128 * (p + 1)]
        for grp, ref in ((2, va_ref), (3, qs_ref), (4, ks_ref), (5, vs_ref)):
            ref[...] = proj[:, D_GRP * grp:D_GRP * (grp + 1)].astype(BF16)
        _split_views(slabs, stage, (q4_ref, k4_ref, v4_ref), (q16_ref, k16_ref, v16_ref))

    row = lambda w: pl.BlockSpec((tm, w), lambda i: (i, 0))
    grp_bf = jax.ShapeDtypeStruct((s_len, D_GRP), BF16)
    grp_f32 = jax.ShapeDtypeStruct((s_len, D_GRP), F32)
    ni = s_len // tm
    return _call_with_gather(
        body, shards, lambda: pl.program_id(0) == 0, lambda: pl.program_id(0) == ni - 2,
        lambda: pl.program_id(0) == ni - 1,
        name="attn_in", grid=(ni,),
        in_specs=[row(D_MODEL), _full((1, D_MODEL)),
                  pl.BlockSpec((N_DEV, D_MODEL, IN_SHARD), lambda i: (0, 0, 0)),
                  row(128), row(128), _full((1, 128)), _full((1, 128)), _full((128, 128))],
        out_specs=(row(D_MODEL),) + (row(D_GRP),) * 8 + (_view_spec(tm, 4),) * 3 + (_view_spec(tm, 16),) * 3,
        out_shape=(jax.ShapeDtypeStruct((s_len, D_MODEL), BF16), grp_f32, grp_f32) + (grp_bf,) * 6
        + (_view_shape(s_len, 4, BF16),) * 3 + (_view_shape(s_len, 16, BF16),) * 3,
        scratch_shapes=[pltpu.VMEM((tm, D_IN), F32), pltpu.VMEM((12, tm, 128), F32), pltpu.VMEM((12, tm, 128), F32),
                        pltpu.VMEM((D_MODEL, D_IN), BF16)],
    )(x2, wn1, a_g, cos2, sin2, qnw, knw, bd, *shards)


def _band_mask(n):
    i = lax.broadcasted_iota(jnp.int32, (2 * BLOCK, 2 * BLOCK), 0) & (BLOCK - 1)
    j = lax.broadcasted_iota(jnp.int32, (2 * BLOCK, 2 * BLOCK), 1)
    dist = i + BLOCK - j
    return (dist >= 0) & (dist <= BLOCK) & ((n - 1) * BLOCK + j >= 0)


def _stack_heads(t2, head0):
    return jnp.concatenate([jnp.where(head0, t2, 0), jnp.where(head0, 0, t2)], axis=0)


def _unstack_heads(t, head0):
    return jnp.where(head0, t[0:BLOCK], t[BLOCK:2 * BLOCK])


def _dil_fwd(qv, kv, vv, r, shards):
    sub_len = qv.shape[0]
    nb = sub_len // BLOCK

    qb = 2 if nb % 2 == 0 else 1

    def body(q_ref, kp_ref, kc_ref, vp_ref, vc_ref, o_ref, lse_ref):
        n = pl.program_id(1)
        lane = lax.broadcasted_iota(jnp.int32, (BLOCK, 128), 1)
        head0 = lane < HEAD_DIM
        units = [(b, slice(128 * p, 128 * (p + 1))) for b in range(qb) for p in range(4)]
        valid = [_band_mask(qb * n + b) for b in range(qb)]
        rows = [slice(BLOCK * b, BLOCK * (b + 1)) for b in range(qb)]

        def keys(prev_ref, cur_ref, b, c):
            before = prev_ref[:, c] if b == 0 else cur_ref[rows[b - 1], c]
            return jnp.concatenate([before, cur_ref[rows[b], c]], axis=0)

        qqs = [_stack_heads(q_ref[rows[b], c] * ATT_SCALE, head0) for b, c in units]
        kks = [keys(kp_ref, kc_ref, b, c) for b, c in units]
        vvs = [keys(vp_ref, vc_ref, b, c) for b, c in units]
        ss = [_dot_nt(qq, kk) for qq, kk in zip(qqs, kks)]
        prs, dens, lses = [], [], []
        for (b, _), s in zip(units, ss):
            s = jnp.where(valid[b], s, NEG)
            m = jnp.max(s, axis=-1, keepdims=True)
            pr = jnp.exp(s - m)
            den = jnp.sum(pr, axis=-1, keepdims=True)
            prs.append(pr.astype(BF16))
            dens.append(den)
            lses.append(m + jnp.log(den))
        pvs = [_dot(pr, vv2) for pr, vv2 in zip(prs, vvs)]
        for (b, c), pv, den, lse in zip(units, pvs, dens, lses):
            o_ref[rows[b], c] = _unstack_heads(pv / den, head0)
            lse_ref[rows[b], c] = _unstack_heads(jnp.broadcast_to(lse, (2 * BLOCK, 128)), head0)

    cur = pl.BlockSpec((qb * BLOCK, D_GRP), lambda c, n: (n, c))
    prev = pl.BlockSpec((BLOCK, D_GRP), lambda c, n: (jnp.maximum(qb * n - 1, 0), c))
    out = jax.ShapeDtypeStruct(qv.shape, F32)
    steps = nb // qb

    def at(t):
        return lambda: pl.program_id(0) * steps + pl.program_id(1) == t

    return _call_with_gather(
        body, shards, at(0), at((2 * r * steps) // 3), at(r * steps - 1),
        name=f"dil_fwd_r{r}", grid=(r, steps),
        in_specs=[cur, prev, cur, prev, cur], out_specs=(cur, cur), out_shape=(out, out),
    )(qv, kv, kv, vv, vv, *shards)


def _dil_bwd(qv, kv, vv, dov, lsev, deltav, r, rides):
    sub_len = qv.shape[0]
    nb = sub_len // BLOCK

    def body(q_ref, kp_ref, kc_ref, vp_ref, vc_ref, do_ref, lse_ref, dl_ref,
             dq_ref, dk_ref, dv_ref, dk_carry, dv_carry):
        n = pl.program_id(1)

        @pl.when(n == 0)
        def _():
            dk_carry[...] = jnp.zeros_like(dk_carry)
            dv_carry[...] = jnp.zeros_like(dv_carry)

        @pl.when(n < nb)
        def _():
            valid = _band_mask(n)
            lane = lax.broadcasted_iota(jnp.int32, (BLOCK, 128), 1)
            head0 = lane < HEAD_DIM
            pairs = [slice(128 * p, 128 * (p + 1)) for p in range(4)]
            qqs = [_stack_heads(q_ref[:, c] * ATT_SCALE, head0) for c in pairs]
            dos = [_stack_heads(do_ref[:, c], head0) for c in pairs]
            kks = [jnp.concatenate([kp_ref[:, c], kc_ref[:, c]], axis=0) for c in pairs]
            vvs = [jnp.concatenate([vp_ref[:, c], vc_ref[:, c]], axis=0) for c in pairs]
            ss = [_dot_nt(qq, kk) for qq, kk in zip(qqs, kks)]
            dps = [_dot_nt(do, vv2) for do, vv2 in zip(dos, vvs)]
            def softmax_terms(p):
                stats = []
                for ref in (lse_ref, dl_ref):
                    t2 = ref[:, pairs[p]]
                    stats.append(jnp.concatenate(
                        [jnp.sum(jnp.where(lane == 0, t2, 0.0), axis=-1, keepdims=True),
                         jnp.sum(jnp.where(lane == HEAD_DIM, t2, 0.0), axis=-1, keepdims=True)], axis=0))
                pr = jnp.where(valid, jnp.exp(jnp.minimum(ss[p] - stats[0], 0.0)), 0.0)
                return pr.astype(BF16), (pr * (dps[p] - stats[1])).astype(BF16)

            terms = [softmax_terms(p) for p in range(4)]
            dqs = [_dot(terms[p][1], kks[p]) for p in range(4)]
            dkks = [_dot_tn(terms[p][1], qqs[p]) for p in range(4)]
            dvvs = [_dot_tn(terms[p][0], dos[p]) for p in range(4)]
            for c, dq, dkk, dvv in zip(pairs, dqs, dkks, dvvs):
                dq_ref[:, c] = _unstack_heads(dq, head0) * ATT_SCALE
                dk_ref[:, c] = dk_carry[:, c] + dkk[:BLOCK]
                dv_ref[:, c] = dv_carry[:, c] + dvv[:BLOCK]
                dk_carry[:, c] = dkk[BLOCK:]
                dv_carry[:, c] = dvv[BLOCK:]

        @pl.when(n == nb)
        def _():
            dk_ref[...] = dk_carry[...]
            dv_ref[...] = dv_carry[...]

    last = nb - 1
    cur = pl.BlockSpec((BLOCK, D_GRP), lambda c, n: (jnp.minimum(n, last), c))
    prev = pl.BlockSpec((BLOCK, D_GRP), lambda c, n: (jnp.clip(n - 1, 0, last), c))
    out = jax.ShapeDtypeStruct(qv.shape, F32)
    return _call_with_exchange(
        body, rides,
        lambda: jnp.logical_and(pl.program_id(0) == 0, pl.program_id(1) == 0),
        lambda: jnp.logical_and(pl.program_id(0) == r - 1, pl.program_id(1) == nb),
        name=f"dil_bwd_r{r}", grid=(r, nb + 1),
        in_specs=[cur, prev, cur, prev, cur, cur, cur, cur],
        out_specs=(cur, prev, prev), out_shape=(out, out, out),
        scratch_shapes=[pltpu.VMEM((BLOCK, D_GRP), F32), pltpu.VMEM((BLOCK, D_GRP), F32)],
    )(qv, kv, kv, vv, vv, dov, lsev, deltav, *rides)


def _sb_fwd(qs, ks, vs, tri_suf, shards):
    s_len = qs.shape[0]
    t = SB_TILE
    nq = s_len // t

    npair = SB_PAIRS

    def body(q_ref, k_ref, v_ref, u_ref, o_ref, c_ref, qq, vt, acc, cf, csave):
        row = lax.broadcasted_iota(jnp.int32, (2 * t, t), 0) & (t - 1)
        col = lax.broadcasted_iota(jnp.int32, (2 * t, t), 1)
        diag_mask = col < row
        lane1 = lax.broadcasted_iota(jnp.int32, (t, 128), 1)
        head0 = lane1 < HEAD_DIM
        lane2 = lax.broadcasted_iota(jnp.int32, (2 * t, 128), 1)
        uu = u_ref[...]
        pr = range(npair)
        cols = [slice(128 * pp, 128 * (pp + 1)) for pp in pr]

        i = pl.program_id(1)

        @pl.when(i == 0)
        def _():
            def transpose_v(j, _):
                rows = pl.ds(pl.multiple_of(j * t, t), t)
                for pp in pr:
                    vt[pp, j] = v_ref[rows, cols[pp]].astype(F32).T.astype(BF16)
                return 0

            lax.fori_loop(0, nq, transpose_v, 0)

        for pp in pr:
            q2 = q_ref[:, cols[pp]] * ATT_SCALE
            qq[pp, 0:t, :] = jnp.where(head0, q2, 0)
            qq[pp, t:2 * t, :] = jnp.where(head0, 0, q2)
        acc[...] = jnp.zeros_like(acc)
        cf[...] = jnp.zeros_like(cf)
        csave[...] = jnp.full(csave.shape, 2.0 * SB_DEAD, F32)

        def tile(kb, diag):
            krows = pl.ds(pl.multiple_of(kb * t, t), t)
            zs = [_dot_nt(qq[pp], k_ref[krows, cols[pp]]) for pp in pr]
            lbk = [_log_sigmoid_pair(z) for z in zs]
            lks = [jnp.where(diag_mask, lk, 0.0) if diag else lk for _, lk in lbk]
            sufs = [_cumsum_mm(lk, uu) for lk in lks]
            carries = [cf[pp] for pp in pr]
            avs = []
            for pp in pr:
                a = jnp.exp(lbk[pp][0] + (sufs[pp] + jnp.concatenate([carries[pp]] * (t // 128), axis=1)))
                avs.append((jnp.where(diag_mask, a, 0.0) if diag else a).astype(BF16))
            pvs = [_dot_nt(vt[pp, kb], avs[pp]) for pp in pr]
            for pp in pr:
                acc[pp] += pvs[pp]
                csave[pp] = jnp.where(lane2 == kb, carries[pp], csave[pp])
                cf[pp] = carries[pp] + jnp.broadcast_to(jnp.sum(lks[pp], axis=-1, keepdims=True), (2 * t, 128))

        tile(i, True)

        def alive():
            return jnp.max(cf[...]) > SB_DEAD

        def k_block(state):
            kb, _ = state
            tile(kb, False)
            return kb - 1, alive()

        lax.while_loop(lambda state: jnp.logical_and(state[0] >= 0, state[1]), k_block, (i - 1, alive()))
        for pp in pr:
            o_ref[:, cols[pp]] = jnp.where(head0, acc[pp, :, 0:t].T, acc[pp, :, t:2 * t].T)
            c_ref[2 * pp] = csave[pp, 0:t, :]
            c_ref[2 * pp + 1] = csave[pp, t:2 * t, :]

    width = 128 * npair
    kv = pl.BlockSpec((s_len, width), lambda p, i: (0, p))
    qo = pl.BlockSpec((t, width), lambda p, i: (i, p))
    steps = 4 // npair

    def at(p, i):
        return lambda: jnp.logical_and(pl.program_id(0) == p, pl.program_id(1) == i)

    return _call_with_gather(
        body, shards, at(0, 0), at(steps - 1, (2 * nq) // 3), at(steps - 1, nq - 1),
        name="sb_fwd", grid=(steps, nq),
        in_specs=[qo, kv, kv, pl.BlockSpec((t, t), lambda p, i: (0, 0))],
        out_specs=(qo, pl.BlockSpec((2 * npair, t, 128), lambda p, i: (p, i, 0))),
        out_shape=(jax.ShapeDtypeStruct((s_len, D_GRP), F32),
                   jax.ShapeDtypeStruct((8, s_len, 128), F32)),
        scratch_shapes=[pltpu.VMEM((npair, 2 * t, 128), BF16), pltpu.VMEM((npair, nq, 128, t), BF16),
                        pltpu.VMEM((npair, 128, 2 * t), F32),
                        pltpu.VMEM((npair, 2 * t, 128), F32), pltpu.VMEM((npair, 2 * t, 128), F32)],
    )(qs, ks, vs, tri_suf, *shards)


def _sb_bwd(qs, ks, vs, dos, csaved, tri_suf, tri_pre, rides):
    s_len = qs.shape[0]
    t = SB_TILE
    nq = s_len // t

    npair = SB_BWD_PAIRS

    def body(q_ref, k_ref, v_ref, do_ref, c_ref, u_ref, p_ref, dq_ref, dk_ref, dv_ref,
             qq, dd, qqt, ddt, kt, dq_acc, dkt, dvt, cg):
        row = lax.broadcasted_iota(jnp.int32, (2 * t, t), 0) & (t - 1)
        col = lax.broadcasted_iota(jnp.int32, (2 * t, t), 1)
        diag_mask = col < row
        lane1 = lax.broadcasted_iota(jnp.int32, (t, 128), 1)
        head0 = lane1 < HEAD_DIM
        lane2 = lax.broadcasted_iota(jnp.int32, (2 * t, 128), 1)
        uu, pm = u_ref[...], p_ref[...]
        pr = range(npair)
        cols = [slice(128 * pp, 128 * (pp + 1)) for pp in pr]
        i = pl.program_id(1)

        @pl.when(i == 0)
        def _():
            dkt[...] = jnp.zeros_like(dkt)
            dvt[...] = jnp.zeros_like(dvt)

            def transpose_k(j, _):
                rows = pl.ds(pl.multiple_of(j * t, t), t)
                for pp in pr:
                    kt[pp, j] = k_ref[rows, cols[pp]].astype(F32).T.astype(BF16)
                return 0

            lax.fori_loop(0, nq, transpose_k, 0)

        for pp in pr:
            q2 = q_ref[:, cols[pp]].astype(F32) * ATT_SCALE
            do2 = do_ref[:, cols[pp]].astype(F32)
            for src, nat, tr in ((q2, qq, qqt), (do2, dd, ddt)):
                stacked = jnp.concatenate([jnp.where(head0, src, 0.0), jnp.where(head0, 0.0, src)], axis=0)
                nat[pp] = stacked.astype(BF16)
                tr[pp] = stacked.T.astype(BF16)
        dq_acc[...] = jnp.zeros_like(dq_acc)
        cg[...] = jnp.zeros_like(cg)

        def tile(kb, diag):
            krows = pl.ds(pl.multiple_of(kb * t, t), t)
            zs = [_dot_nt(qq[pp], k_ref[krows, cols[pp]]) for pp in pr]
            das = [_dot_nt(dd[pp], v_ref[krows, cols[pp]]) for pp in pr]
            lbk = [_log_sigmoid_pair(z) for z in zs]
            lks = [jnp.where(diag_mask, lk, 0.0) if diag else lk for _, lk in lbk]
            sufs = [_cumsum_mm(lk, uu) for lk in lks]
            avs, gs = [], []
            for pp in pr:
                cs = jnp.concatenate([c_ref[2 * pp], c_ref[2 * pp + 1]], axis=0)
                cf = jnp.sum(jnp.where(lane2 == kb, cs, 0.0), axis=-1, keepdims=True)
                a = jnp.exp(lbk[pp][0] + (sufs[pp] + cf))
                a = jnp.where(diag_mask, a, 0.0) if diag else a
                avs.append(a.astype(BF16))
                gs.append(a * das[pp])
            gpres = [_cumsum_mm(g, pm) for g in gs]
            dzs = []
            for pp in pr:
                carry = cg[pp]
                beta = jnp.exp(lbk[pp][0])
                dz = gs[pp] - beta * (gs[pp] + (gpres[pp] + jnp.concatenate([carry] * (t // 128), axis=1)))
                dzs.append((jnp.where(diag_mask, dz, 0.0) if diag else dz).astype(BF16))
                cg[pp] = carry + jnp.broadcast_to(jnp.sum(gs[pp], axis=-1, keepdims=True), (2 * t, 128))
            dqs = [_dot_nt(kt[pp, kb], dzs[pp]) for pp in pr]
            dks = [_dot(qqt[pp], dzs[pp]) for pp in pr]
            dvs = [_dot(ddt[pp], avs[pp]) for pp in pr]
            for pp in pr:
                dq_acc[pp] += dqs[pp]
                dkt[pp, kb] += dks[pp]
                dvt[pp, kb] += dvs[pp]

        def k_block(kb, _):
            tile(kb, False)
            return 0

        col_max = jnp.max(jnp.max(c_ref[...], axis=0), axis=0, keepdims=True)
        lane_row = lax.broadcasted_iota(jnp.int32, (1, 128), 1)
        n_live = jnp.sum(jnp.where(jnp.logical_and(col_max > SB_DEAD, lane_row < i), 1, 0))
        lax.fori_loop(i - n_live, i, k_block, 0)
        tile(i, True)
        for pp in pr:
            dq_ref[:, cols[pp]] = jnp.where(head0, dq_acc[pp, :, 0:t].T, dq_acc[pp, :, t:2 * t].T) * ATT_SCALE

        @pl.when(i == nq - 1)
        def _():
            def untranspose(j, _):
                rows = pl.ds(pl.multiple_of(j * t, t), t)
                for pp in pr:
                    dk_ref[rows, cols[pp]] = dkt[pp, j].T
                    dv_ref[rows, cols[pp]] = dvt[pp, j].T
                return 0

            lax.fori_loop(0, nq, untranspose, 0)

    width = 128 * npair
    kv = pl.BlockSpec((s_len, width), lambda p, i: (0, p))
    qo = pl.BlockSpec((t, width), lambda p, i: (i, p))
    tri = pl.BlockSpec((t, t), lambda p, i: (0, 0))
    out = jax.ShapeDtypeStruct((s_len, D_GRP), F32)
    steps = 4 // npair
    return _call_with_exchange(
        body, rides,
        lambda: jnp.logical_and(pl.program_id(0) == 0, pl.program_id(1) == 0),
        lambda: jnp.logical_and(pl.program_id(0) == steps - 1, pl.program_id(1) == nq - 1),
        name="sb_bwd", grid=(steps, nq),
        in_specs=[qo, kv, kv, qo, pl.BlockSpec((2 * npair, t, 128), lambda p, i: (p, i, 0)), tri, tri],
        out_specs=(qo, kv, kv), out_shape=(out, out, out),
        scratch_shapes=[pltpu.VMEM((npair, 2 * t, 128), BF16), pltpu.VMEM((npair, 2 * t, 128), BF16),
                        pltpu.VMEM((npair, 128, 2 * t), BF16), pltpu.VMEM((npair, 128, 2 * t), BF16),
                        pltpu.VMEM((npair, nq, 128, t), BF16),
                        pltpu.VMEM((npair, 128, 2 * t), F32),
                        pltpu.VMEM((npair, nq, 128, t), F32), pltpu.VMEM((npair, nq, 128, t), F32),
                        pltpu.VMEM((npair, 2 * t, 128), F32)],
    )(qs, ks, vs, dos, csaved, tri_suf, tri_pre, *rides)


def _attn_out(o_b, lse_b, o_sb, x2, wdil, wsb, out_g, shards):
    s_len = x2.shape[0]
    tm = ROW_TILE

    def body(o1_ref, l1_ref, o4_ref, l4_ref, o16_ref, l16_ref, osb_ref, x_ref, wdil_ref, wsb_ref, w_ref,
             odil_ref, lse_ref, lse4_ref, lse16_ref, mixed_ref, x1_ref, stage, nat4, nat16):
        _merge_views((o4_ref, l4_ref), (o16_ref, l16_ref), stage, nat4, nat16)
        os_ = (o1_ref[...], _slab_group(nat4, 0), _slab_group(nat16, 0))
        ls = (l1_ref[...], _slab_group(nat4, 1), _slab_group(nat16, 1))
        mx = jnp.maximum(jnp.maximum(ls[0], ls[1]), ls[2])
        es = [jnp.exp(l - mx) for l in ls]
        den = es[0] + es[1] + es[2]
        o_dil = (es[0] * os_[0] + es[1] * os_[1] + es[2] * os_[2]) / den
        odil_ref[...] = o_dil
        lse = mx + jnp.log(den)
        lse_ref[...] = lse
        for p in range(4):
            nat4[p] = lse[:, 128 * p:128 * (p + 1)]
        _split_views(nat4.at[0:4], stage.at[0:4], (lse4_ref,), (lse16_ref,))
        halves = []
        for t, w_r in ((o_dil, wdil_ref), (osb_ref[...], wsb_ref)):
            r = lax.rsqrt(jnp.mean(t * t, axis=-1, keepdims=True) + EPS)
            halves.append(((t * r) * w_r[...]).astype(BF16))
        mixed = jnp.concatenate(halves, axis=1)
        mixed_ref[...] = mixed
        w = w_ref[...].reshape(D_MODEL, D_MODEL)
        x1_ref[...] = x_ref[...] + _dot(mixed, w)

    row = lambda w: pl.BlockSpec((tm, w), lambda i: (i, 0))
    ni = s_len // tm
    return _call_with_gather(
        body, shards, lambda: pl.program_id(0) == 0, lambda: pl.program_id(0) == ni - 2,
        lambda: pl.program_id(0) == ni - 1,
        name="attn_out", grid=(ni,),
        in_specs=[row(D_GRP)] * 2 + [_view_spec(tm, 4)] * 2 + [_view_spec(tm, 16)] * 2
        + [row(D_GRP), row(D_MODEL), _full((1, D_GRP)), _full((1, D_GRP)), _full((N_DEV, OUT_SHARD, D_MODEL))],
        out_specs=(row(D_GRP), row(D_GRP), _view_spec(tm, 4), _view_spec(tm, 16), row(D_MODEL), row(D_MODEL)),
        out_shape=(jax.ShapeDtypeStruct((s_len, D_GRP), F32), jax.ShapeDtypeStruct((s_len, D_GRP), F32),
                   _view_shape(s_len, 4, F32), _view_shape(s_len, 16, F32),
                   jax.ShapeDtypeStruct((s_len, D_MODEL), BF16), jax.ShapeDtypeStruct((s_len, D_MODEL), F32)),
        scratch_shapes=[pltpu.VMEM((8, tm, 128), F32)] * 3,
    )(o_b[0], lse_b[0], o_b[1], lse_b[1], o_b[2], lse_b[2], o_sb, x2, wdil, wsb, out_g, *shards)


def _two_shards(w_ref):
    return w_ref[...].reshape(FF_BLOCK, D_MODEL)


def _two_split_shards(lo_ref, hi_ref):
    return jnp.concatenate([lo_ref[0], hi_ref[0], lo_ref[1], hi_ref[1]], axis=0)


def _ffn_fwd(x1, wn2, tgt, gate_g, up_g, down_lo_g, down_hi_g):
    s_len = x1.shape[0]
    tm = ROW_TILE
    ni = s_len // tm

    def body(x_ref, wn_ref, t_ref, wg_ref, wu_ref, wd_lo_ref, wd_hi_ref, g_ref, u_ref, h2_ref, dy_ref, loss_ref,
             acc):
        j = pl.program_id(1)

        @pl.when(j == 0)
        def _():
            xx = x_ref[...]
            r = lax.rsqrt(jnp.mean(xx * xx, axis=-1, keepdims=True) + EPS)
            h2_ref[...] = ((xx * r) * wn_ref[...]).astype(BF16)
            acc[...] = jnp.zeros_like(acc)

        h = h2_ref[...]
        g = _dot_nt(h, _two_shards(wg_ref))
        u = _dot_nt(h, _two_shards(wu_ref))
        g_ref[...] = g
        u_ref[...] = u
        act = (g * (1.0 / (1.0 + jnp.exp(-g)))) * u
        acc[...] += _dot(act.astype(BF16), _two_split_shards(wd_lo_ref, wd_hi_ref))

        @pl.when(j == FF_STEPS - 1)
        def _():
            err = (x_ref[...] + acc[...]) - t_ref[...]
            dy_ref[...] = err * (1.0 / D_MODEL)
            part = 0.5 * jnp.sum(jnp.mean(err * err, axis=-1, keepdims=True))
            loss_ref[...] = jnp.full((8, 128), part, F32)

    row = pl.BlockSpec((tm, D_MODEL), lambda i, j: (i, 0))
    hid = pl.BlockSpec((tm, FF_BLOCK), lambda i, j: (i, j))
    return pl.pallas_call(
        body, name="ffn_fwd", grid=(ni, FF_STEPS),
        in_specs=[row, pl.BlockSpec((1, D_MODEL), lambda i, j: (0, 0)), row,
                  pl.BlockSpec((2, FF_PAD, D_MODEL), lambda i, j: (j, 0, 0)),
                  pl.BlockSpec((2, FF_PAD, D_MODEL), lambda i, j: (j, 0, 0)),
                  pl.BlockSpec((2, FF_PAD // 2, D_MODEL), lambda i, j: (j, 0, 0)),
                  pl.BlockSpec((2, FF_PAD // 2, D_MODEL), lambda i, j: (j, 0, 0))],
        out_specs=(hid, hid, row, row, pl.BlockSpec((8, 128), lambda i, j: (i, 0))),
        out_shape=(jax.ShapeDtypeStruct((s_len, N_DEV * FF_PAD), F32),
                   jax.ShapeDtypeStruct((s_len, N_DEV * FF_PAD), F32),
                   jax.ShapeDtypeStruct((s_len, D_MODEL), BF16),
                   jax.ShapeDtypeStruct((s_len, D_MODEL), F32),
                   jax.ShapeDtypeStruct((ni * 8, 128), F32)),
        scratch_shapes=[pltpu.VMEM((tm, D_MODEL), F32)],
        compiler_params=_params(),
    )(x1, wn2, tgt, gate_g, up_g, down_lo_g, down_hi_g)


def _ffn_bwd_dx(dy, g, u, gate_g, up_g, down_lo_g, down_hi_g):
    s_len = dy.shape[0]
    tm = ROW_TILE

    def body(dy_ref, g_ref, u_ref, wg_ref, wu_ref, wd_lo_ref, wd_hi_ref, dg_ref, du_ref, act_ref, dh_ref, acc):
        j = pl.program_id(1)

        @pl.when(j == 0)
        def _():
            acc[...] = jnp.zeros_like(acc)

        halves = [slice(0, tm // 2), slice(tm // 2, tm)]
        wd, wg, wu = _two_split_shards(wd_lo_ref, wd_hi_ref), _two_shards(wg_ref), _two_shards(wu_ref)
        das = [_dot_nt(dy_ref[rows, :].astype(BF16), wd) for rows in halves]

        def elementwise(rows, da):
            gg, uu = g_ref[rows, :], u_ref[rows, :]
            sig = 1.0 / (1.0 + jnp.exp(-gg))
            silu = gg * sig
            act_ref[rows, :] = (silu * uu).astype(BF16)
            du = (da * silu).astype(BF16)
            dg = (da * uu * (sig * (1.0 + gg * (1.0 - sig)))).astype(BF16)
            du_ref[rows, :] = du
            dg_ref[rows, :] = dg
            return dg, du

        dg0, du0 = elementwise(halves[0], das[0])
        acc[halves[0], :] += _dot(dg0, wg) + _dot(du0, wu)
        dg1, du1 = elementwise(halves[1], das[1])
        acc[halves[1], :] += _dot(dg1, wg) + _dot(du1, wu)

        @pl.when(j == FF_STEPS - 1)
        def _():
            dh_ref[...] = acc[...]

    row = pl.BlockSpec((tm, D_MODEL), lambda i, j: (i, 0))
    hid = pl.BlockSpec((tm, FF_BLOCK), lambda i, j: (i, j))
    hid_bf = jax.ShapeDtypeStruct((s_len, N_DEV * FF_PAD), BF16)
    return pl.pallas_call(
        body, name="ffn_bwd_dx", grid=(s_len // tm, FF_STEPS),
        in_specs=[row, hid, hid,
                  pl.BlockSpec((2, FF_PAD, D_MODEL), lambda i, j: (j, 0, 0)),
                  pl.BlockSpec((2, FF_PAD, D_MODEL), lambda i, j: (j, 0, 0)),
                  pl.BlockSpec((2, FF_PAD // 2, D_MODEL), lambda i, j: (j, 0, 0)),
                  pl.BlockSpec((2, FF_PAD // 2, D_MODEL), lambda i, j: (j, 0, 0))],
        out_specs=(hid, hid, hid, row),
        out_shape=(hid_bf, hid_bf, hid_bf, jax.ShapeDtypeStruct((s_len, D_MODEL), F32)),
        scratch_shapes=[pltpu.VMEM((tm, D_MODEL), F32)],
        compiler_params=_params(),
    )(dy, g, u, gate_g, up_g, down_lo_g, down_hi_g)


def _ffn_bwd_dw(h2, dy, dg, du, act):
    s_len = h2.shape[0]
    tm = DW_ROW_TILE
    ni = s_len // tm

    half = FF_PAD // 2

    def body(h_ref, dy_ref, dg_ref, du_ref, act_ref, dwg_ref, dwu_ref, dwd_lo_ref, dwd_hi_ref, ag, au, ad):
        i = pl.program_id(1)

        @pl.when(i == 0)
        def _():
            ag[...] = jnp.zeros_like(ag)
            au[...] = jnp.zeros_like(au)
            ad[...] = jnp.zeros_like(ad)

        h = h_ref[...]
        ag[...] += _dot_tn(dg_ref[...], h)
        au[...] += _dot_tn(du_ref[...], h)
        ad[...] += _dot_tn(act_ref[...], dy_ref[...].astype(BF16))

        @pl.when(i == ni - 1)
        def _():
            for acc_ref, out_ref in ((ag, dwg_ref), (au, dwu_ref)):
                out_ref[...] = acc_ref[...].astype(BF16).reshape(2, FF_PAD, D_MODEL)
            for dev in range(2):
                dwd_lo_ref[dev] = ad[FF_PAD * dev:FF_PAD * dev + half, :].astype(BF16)
                dwd_hi_ref[dev] = ad[FF_PAD * dev + half:FF_PAD * (dev + 1), :].astype(BF16)

    row = pl.BlockSpec((tm, D_MODEL), lambda j, i: (i, 0))
    hid = pl.BlockSpec((tm, FF_BLOCK), lambda j, i: (i, j))
    row_w = pl.BlockSpec((2, FF_PAD, D_MODEL), lambda j, i: (j, 0, 0))
    half_w = pl.BlockSpec((2, half, D_MODEL), lambda j, i: (j, 0, 0))
    grad = jax.ShapeDtypeStruct((N_DEV, FF_PAD, D_MODEL), BF16)
    half_grad = jax.ShapeDtypeStruct((N_DEV, half, D_MODEL), BF16)
    return pl.pallas_call(
        body, name="ffn_bwd_dw", grid=(FF_STEPS, ni),
        in_specs=[row, row, hid, hid, hid], out_specs=(row_w, row_w, half_w, half_w),
        out_shape=(grad, grad, half_grad, half_grad),
        scratch_shapes=[pltpu.VMEM((FF_BLOCK, D_MODEL), F32)] * 3,
        compiler_params=_params(),
    )(h2, dy, dg, du, act)


def _rms_bwd(dy, t, w):
    r = lax.rsqrt(jnp.mean(t * t, axis=-1, keepdims=True) + EPS)
    gw = dy * w
    dt = r * (gw - t * ((r * r) * jnp.mean(gw * t, axis=-1, keepdims=True)))
    return dt, dy * t * r


def _attn_out_bwd(dy, dh2, x1, wn2, b_g, mixed, o_dil, o_sb, wdil, wsb, bd512):
    s_len = dy.shape[0]
    tm = ROW_TILE
    ni = s_len // tm

    def body(dy_ref, dh_ref, x1_ref, wn_ref, w_ref, mixed_ref, odil_ref, osb_ref, wdil_ref, wsb_ref, bd_ref,
             dx1_ref, dodil_ref, delta_ref, dosb_ref, dwout_ref, dwn_ref, dwdil_ref, dwsb_ref,
             do4_ref, dl4_ref, do16_ref, dl16_ref, wacc, both, stage):
        i = pl.program_id(0)

        @pl.when(i == 0)
        def _():
            wacc[...] = jnp.zeros_like(wacc)
            dwn_ref[...] = jnp.zeros_like(dwn_ref)
            dwdil_ref[...] = jnp.zeros_like(dwdil_ref)
            dwsb_ref[...] = jnp.zeros_like(dwsb_ref)

        dnorm, dw_rows = _rms_bwd(dh_ref[...], x1_ref[...], wn_ref[...])
        dx1 = dy_ref[...] + dnorm
        dx1_ref[...] = dx1
        dwn_ref[...] += jnp.sum(dw_rows, axis=0, keepdims=True)
        dx1b = dx1.astype(BF16)
        w = w_ref[...].reshape(D_MODEL, D_MODEL)
        dmixed = _dot_nt(dx1b, w)
        wacc[...] += _dot_tn(mixed_ref[...], dx1b)
        o_dil = odil_ref[...]
        d_odil, dw_rows = _rms_bwd(dmixed[:, :D_GRP], o_dil, wdil_ref[...])
        dwdil_ref[...] += jnp.sum(dw_rows, axis=0, keepdims=True)
        dodil_ref[...] = d_odil.astype(BF16)
        delta = _mm_split(d_odil * o_dil, bd_ref[...])
        delta_ref[...] = delta
        for p in range(4):
            both[p] = d_odil[:, 128 * p:128 * (p + 1)]
            both[4 + p] = delta[:, 128 * p:128 * (p + 1)]
        _split_views(both, stage, (do4_ref, dl4_ref), (do16_ref, dl16_ref))
        d_osb, dw_rows = _rms_bwd(dmixed[:, D_GRP:], osb_ref[...], wsb_ref[...])
        dwsb_ref[...] += jnp.sum(dw_rows, axis=0, keepdims=True)
        dosb_ref[...] = d_osb.astype(BF16)

        @pl.when(i == ni - 1)
        def _():
            dwout_ref[...] = wacc[...].astype(BF16).reshape(N_DEV, OUT_SHARD, D_MODEL)

    row = lambda w: pl.BlockSpec((tm, w), lambda i: (i, 0))
    return pl.pallas_call(
        body, name="attn_out_bwd", grid=(ni,),
        in_specs=[row(D_MODEL), row(D_MODEL), row(D_MODEL), _full((1, D_MODEL)),
                  _full((N_DEV, OUT_SHARD, D_MODEL)),
                  row(D_MODEL), row(D_GRP), row(D_GRP), _full((1, D_GRP)), _full((1, D_GRP)),
                  _full((D_GRP, D_GRP))],
        out_specs=(row(D_MODEL), row(D_GRP), row(D_GRP), row(D_GRP),
                   _full((N_DEV, OUT_SHARD, D_MODEL)), _full((1, D_MODEL)), _full((1, D_GRP)), _full((1, D_GRP)),
                   _view_spec(tm, 4), _view_spec(tm, 4), _view_spec(tm, 16), _view_spec(tm, 16)),
        out_shape=(jax.ShapeDtypeStruct((s_len, D_MODEL), F32), jax.ShapeDtypeStruct((s_len, D_GRP), BF16),
                   jax.ShapeDtypeStruct((s_len, D_GRP), F32), jax.ShapeDtypeStruct((s_len, D_GRP), BF16),
                   jax.ShapeDtypeStruct((N_DEV, OUT_SHARD, D_MODEL), BF16),
                   jax.ShapeDtypeStruct((1, D_MODEL), F32), jax.ShapeDtypeStruct((1, D_GRP), F32),
                   jax.ShapeDtypeStruct((1, D_GRP), F32),
                   _view_shape(s_len, 4, BF16), _view_shape(s_len, 4, F32),
                   _view_shape(s_len, 16, BF16), _view_shape(s_len, 16, F32)),
        scratch_shapes=[pltpu.VMEM((D_MODEL, D_MODEL), F32), pltpu.VMEM((8, tm, 128), F32),
                        pltpu.VMEM((8, tm, 128), F32)],
        compiler_params=_params(),
    )(dy, dh2, x1, wn2, b_g, mixed, o_dil, o_sb, wdil, wsb, bd512)


def _qkv_bwd(dq_b, dk_b, dv_b, dqs, dks, dvs, qraw, kraw, cos2, sin2, qnw, knw, bd, rides):
    s_len = qraw.shape[0]
    tm = ROW_TILE
    ni = s_len // tm

    def body(dq1, dk1, dv1, dq4, dk4, dv4, dq16, dk16, dv16, dqs_ref, dks_ref, dvs_ref,
             qraw_ref, kraw_ref, cos_ref, sin_ref, qnw_ref, knw_ref, bd_ref,
             dproj_ref, dqn_ref, dkn_ref, stage, nat4, nat16):
        i = pl.program_id(0)

        @pl.when(i == 0)
        def _():
            dqn_ref[...] = jnp.zeros_like(dqn_ref)
            dkn_ref[...] = jnp.zeros_like(dkn_ref)

        _merge_views((dq4, dk4, dv4), (dq16, dk16, dv16), stage, nat4, nat16)
        cos_t, sin_t, bdm = cos_ref[...], sin_ref[...], bd_ref[...]
        for grp, (part1, raw_ref, nw_ref, dn_ref) in enumerate(((dq1, qraw_ref, qnw_ref, dqn_ref),
                                                                (dk1, kraw_ref, knw_ref, dkn_ref))):
            dn_acc = 0.0
            for p in range(4):
                cols = slice(128 * p, 128 * (p + 1))
                d_rope = part1[:, cols] + nat4[4 * grp + p] + nat16[4 * grp + p]
                d_norm = d_rope * cos_t + _swap_halves(d_rope * sin_t)
                t = raw_ref[:, cols]
                w = nw_ref[...]
                r = lax.rsqrt(_mm_split(t * t, bdm) * (1.0 / HEAD_DIM) + EPS)
                gw = d_norm * w
                corr = _mm_split(gw * t, bdm) * (1.0 / HEAD_DIM)
                dt = r * (gw - t * ((r * r) * corr))
                dn_acc = dn_acc + jnp.sum(d_norm * t * r, axis=0, keepdims=True)
                dproj_ref[:, D_GRP * grp + 128 * p:D_GRP * grp + 128 * (p + 1)] = dt.astype(BF16)
            dn_ref[...] += dn_acc
        dproj_ref[:, 2 * D_GRP:3 * D_GRP] = (dv1[...] + _slab_group(nat4, 2) + _slab_group(nat16, 2)).astype(BF16)
        dproj_ref[:, 3 * D_GRP:4 * D_GRP] = dqs_ref[...].astype(BF16)
        dproj_ref[:, 4 * D_GRP:5 * D_GRP] = dks_ref[...].astype(BF16)
        dproj_ref[:, 5 * D_GRP:6 * D_GRP] = dvs_ref[...].astype(BF16)

    row = lambda w: pl.BlockSpec((tm, w), lambda i: (i, 0))
    return _call_with_exchange(
        body, rides, lambda: pl.program_id(0) == 0, lambda: pl.program_id(0) == ni - 1,
        name="qkv_bwd", grid=(ni,),
        in_specs=[row(D_GRP)] * 3 + [_view_spec(tm, 4)] * 3 + [_view_spec(tm, 16)] * 3 + [row(D_GRP)] * 5
        + [row(128), row(128), _full((1, 128)), _full((1, 128)), _full((128, 128))],
        out_specs=(row(D_IN), _full((1, 128)), _full((1, 128))),
        out_shape=(jax.ShapeDtypeStruct((s_len, D_IN), BF16), jax.ShapeDtypeStruct((1, 128), F32),
                   jax.ShapeDtypeStruct((1, 128), F32)),
        scratch_shapes=[pltpu.VMEM((12, tm, 128), F32)] * 3,
    )(dq_b[0], dk_b[0], dv_b[0], dq_b[1], dk_b[1], dv_b[1], dq_b[2], dk_b[2], dv_b[2],
      dqs, dks, dvs, qraw, kraw, cos2, sin2, qnw, knw, bd, *rides)


def _in_bwd_dx(dproj, a_g, x2, dx1, wn1, rides):
    s_len = x2.shape[0]
    tm = ROW_TILE
    ni = s_len // tm

    def body(dp_ref, w_ref, x_ref, dx1_ref, wn_ref, gx_ref, dwn_ref, w_full):
        i = pl.program_id(0)

        @pl.when(i == 0)
        def _():
            dwn_ref[...] = jnp.zeros_like(dwn_ref)
            for d in range(N_DEV):
                w_full[:, IN_SHARD * d:IN_SHARD * (d + 1)] = w_ref[d]

        dh = _dot_nt(dp_ref[...], w_full[...])
        dnorm, dw_rows = _rms_bwd(dh, x_ref[...], wn_ref[...])
        gx_ref[...] = dx1_ref[...] + dnorm
        dwn_ref[...] += jnp.sum(dw_rows, axis=0, keepdims=True)

    row = lambda w: pl.BlockSpec((tm, w), lambda i: (i, 0))
    return _call_with_exchange(
        body, rides, lambda: pl.program_id(0) == 0, lambda: pl.program_id(0) == ni - 1,
        name="in_bwd_dx", grid=(ni,),
        in_specs=[row(D_IN), pl.BlockSpec((N_DEV, D_MODEL, IN_SHARD), lambda i: (0, 0, 0)),
                  row(D_MODEL), row(D_MODEL), _full((1, D_MODEL))],
        out_specs=(row(D_MODEL), _full((1, D_MODEL))),
        out_shape=(jax.ShapeDtypeStruct((s_len, D_MODEL), F32), jax.ShapeDtypeStruct((1, D_MODEL), F32)),
        scratch_shapes=[pltpu.VMEM((D_MODEL, D_IN), BF16)],
    )(dproj, a_g, x2, dx1, wn1, *rides)


def _in_bwd_dw(h1, dproj):
    s_len = h1.shape[0]
    tm = DW_ROW_TILE
    ni = s_len // tm

    half_d = D_MODEL // 2

    def body(h_ref, dp_ref, lo_ref, hi_ref, acc):
        i = pl.program_id(1)

        @pl.when(i == 0)
        def _():
            acc[...] = jnp.zeros_like(acc)

        acc[...] += _dot_tn(h_ref[...], dp_ref[...])

        @pl.when(i == ni - 1)
        def _():
            for dev in range(2):
                cols = slice(IN_SHARD * dev, IN_SHARD * (dev + 1))
                lo_ref[dev] = acc[0:half_d, cols].astype(BF16)
                hi_ref[dev] = acc[half_d:D_MODEL, cols].astype(BF16)

    half_w = pl.BlockSpec((2, half_d, IN_SHARD), lambda d, i: (d, 0, 0))
    half_grad = jax.ShapeDtypeStruct((N_DEV, half_d, IN_SHARD), BF16)
    return pl.pallas_call(
        body, name="in_bwd_dw", grid=(N_DEV // 2, ni),
        in_specs=[pl.BlockSpec((tm, D_MODEL), lambda d, i: (i, 0)),
                  pl.BlockSpec((tm, 2 * IN_SHARD), lambda d, i: (i, d))],
        out_specs=(half_w, half_w),
        out_shape=(half_grad, half_grad),
        scratch_shapes=[pltpu.VMEM((D_MODEL, 2 * IN_SHARD), F32)],
        compiler_params=_params(),
    )(h1, dproj)


def _adamw(recv, w, m, v, recv_hi=None):
    rows, cols = w.shape
    tr = next((t for t in (128, 32) if rows % t == 0), rows)
    recvs = [recv] if recv_hi is None else [recv, recv_hi]
    lo_tiles = recv.shape[1] // tr

    def body(*refs):
        p_refs = refs[:len(recvs)]
        w_ref, m_ref, v_ref, g_ref, d_ref, nm_ref, nv_ref = refs[len(recvs):]

        def slot(s):
            if len(p_refs) == 1:
                return p_refs[0][s].astype(F32)
            return jnp.where(pl.program_id(0) < lo_tiles, p_refs[0][s], p_refs[1][s]).astype(F32)

        g = slot(0)
        for s in range(1, N_DEV):
            g = g + slot(s)
        m_new = ADAM_B1 * m_ref[...] + (1.0 - ADAM_B1) * g
        v_new = ADAM_B2 * v_ref[...] + (1.0 - ADAM_B2) * (g * g)
        m_hat = m_new / (1.0 - ADAM_B1 ** ADAM_STEP)
        v_hat = v_new / (1.0 - ADAM_B2 ** ADAM_STEP)
        g_ref[...] = g
        d_ref[...] = -ADAM_LR * (m_hat / (jnp.sqrt(v_hat) + ADAM_EPS) + ADAM_WD * w_ref[...])
        nm_ref[...] = m_new
        nv_ref[...] = v_new

    blk = pl.BlockSpec((tr, cols), lambda i: (i, 0))
    out = jax.ShapeDtypeStruct((rows, cols), F32)
    return pl.pallas_call(
        body, name=f"adamw_{rows}x{cols}", grid=(rows // tr,),
        in_specs=([pl.BlockSpec((N_DEV, tr, cols), lambda i: (0, i, 0))] if recv_hi is None else
                  [pl.BlockSpec((N_DEV, tr, cols), lambda i: (0, jnp.minimum(i, lo_tiles - 1), 0)),
                   pl.BlockSpec((N_DEV, tr, cols), lambda i: (0, jnp.maximum(i - lo_tiles, 0), 0))])
        + [blk, blk, blk],
        out_specs=(blk,) * 4, out_shape=(out,) * 4,
        compiler_params=_params(),
    )(*recvs, w, m, v)


def _adamw_many(items, rides):
    tr = 32
    tiles = [w.shape[0] // tr for _, w, _, _ in items]
    starts = [sum(tiles[:k]) for k in range(len(items))]
    total = sum(tiles)
    n_items = len(items)

    def body(*refs):
        i = pl.program_id(0)
        in_refs, out_refs = refs[:4 * n_items], refs[4 * n_items:]
        for k in range(n_items):
            def update(k=k):
                p_ref, w_ref, m_ref, v_ref = in_refs[4 * k:4 * k + 4]
                g_ref, d_ref, nm_ref, nv_ref = out_refs[4 * k:4 * k + 4]
                g = p_ref[0].astype(F32)
                for s in range(1, N_DEV):
                    g = g + p_ref[s].astype(F32)
                m_new = ADAM_B1 * m_ref[...] + (1.0 - ADAM_B1) * g
                v_new = ADAM_B2 * v_ref[...] + (1.0 - ADAM_B2) * (g * g)
                m_hat = m_new / (1.0 - ADAM_B1 ** ADAM_STEP)
                v_hat = v_new / (1.0 - ADAM_B2 ** ADAM_STEP)
                g_ref[...] = g
                d_ref[...] = -ADAM_LR * (m_hat / (jnp.sqrt(v_hat) + ADAM_EPS) + ADAM_WD * w_ref[...])
                nm_ref[...] = m_new
                nv_ref[...] = v_new

            pl.when(jnp.logical_and(i >= starts[k], i < starts[k] + tiles[k]))(update)

    in_specs, out_specs, out_shape, args = [], [], [], []
    for k, (recv, w, m, v) in enumerate(items):
        tile_of = functools.partial(lambda i, s0, nk: jnp.clip(i - s0, 0, nk - 1), s0=starts[k], nk=tiles[k])
        blk = pl.BlockSpec((tr, D_MODEL), functools.partial(lambda i, t: (t(i), 0), t=tile_of))
        in_specs += [pl.BlockSpec((N_DEV, tr, D_MODEL), functools.partial(lambda i, t: (0, t(i), 0), t=tile_of)),
                     blk, blk, blk]
        out_specs += [blk] * 4
        out_shape += [jax.ShapeDtypeStruct(w.shape, F32)] * 4
        args += [recv, w, m, v]
    res = _call_with_exchange(
        body, rides, lambda: pl.program_id(0) == 0, lambda: pl.program_id(0) == total - 1,
        name="adamw_many", grid=(total,), in_specs=in_specs, out_specs=out_specs, out_shape=out_shape,
    )(*args, *rides)
    return [tuple(res[4 * k:4 * k + 4]) for k in range(n_items)], list(res[4 * n_items:])


def _rope_tables(s_len):
    pos = jnp.arange(s_len, dtype=F32)
    inv_freq = ROPE_THETA ** (-jnp.arange(0, HEAD_DIM, 2, dtype=F32) / HEAD_DIM)
    ang = pos[:, None] * inv_freq[None, :]
    cos, sin = jnp.cos(ang), jnp.sin(ang)
    cos2 = jnp.concatenate([cos, cos, cos, cos], axis=1)
    sin2 = jnp.concatenate([-sin, sin, -sin, sin], axis=1)
    return cos2, sin2


def _block_diag_ones(n):
    i = jnp.arange(n)
    return (i[:, None] // HEAD_DIM == i[None, :] // HEAD_DIM).astype(BF16)


def _pad_cols(t):
    return jnp.pad(t, ((0, 0), (0, FF_PAD - FF_SHARD)))


def _pad_rows(t):
    return jnp.pad(t, ((0, FF_PAD - FF_SHARD), (0, 0)))


LOSS_ROW = 26


def _pack_small(n1, n2, ndil, nsb, nq, nk, scalar=None):
    pad = lambda t: jnp.pad(t.reshape(1, -1), ((0, 0), (0, 128 - t.size)))
    last = jnp.zeros((1, 128), F32) if scalar is None else pad(scalar)
    rows = [n1.reshape(8, 128), n2.reshape(8, 128), ndil.reshape(4, 128), nsb.reshape(4, 128),
            pad(nq), pad(nk), last, jnp.zeros((5, 128), F32)]
    return jnp.concatenate(rows, axis=0)


def _unpack_small(t):
    return (t[0:8].reshape(1, D_MODEL), t[8:16].reshape(1, D_MODEL), t[16:20].reshape(1, D_GRP),
            t[20:24].reshape(1, D_GRP), t[24:25, :HEAD_DIM], t[25:26, :HEAD_DIM])


def kernel(x, attn_norm_w, w_in, q_norm_w, k_norm_w, dil_out_norm_w, sb_out_norm_w, w_out, ffn_norm_w, w_gate, w_up, w_down, loss_target, m_attn_norm_w, m_w_in, m_q_norm_w, m_k_norm_w, m_dil_out_norm_w, m_sb_out_norm_w, m_w_out, m_ffn_norm_w, m_w_gate, m_w_up, m_w_down, v_attn_norm_w, v_w_in, v_q_norm_w, v_k_norm_w, v_dil_out_norm_w, v_sb_out_norm_w, v_w_out, v_ffn_norm_w, v_w_gate, v_w_up, v_w_down):
    s_len = x.shape[1]
    x2, tgt = x[0], loss_target[0]

    (a_g,) = _gather_weights([w_in[0].astype(BF16)])
    gate_loc = _pad_cols(w_gate[0]).T.astype(BF16)
    up_loc = _pad_cols(w_up[0]).T.astype(BF16)
    down_loc = _pad_rows(w_down[0]).astype(BF16)
    out_loc = w_out[0].astype(BF16)

    cos2, sin2 = _rope_tables(s_len)
    bd128, bd512 = _block_diag_ones(128), _block_diag_ones(D_GRP)
    idx = jnp.arange(SB_TILE)
    tri_suf = (idx[:, None] > idx[None, :]).astype(BF16)
    tri_pre = (idx[:, None] < idx[None, :]).astype(BF16)
    qnw2 = jnp.concatenate([q_norm_w, q_norm_w], axis=1)
    knw2 = jnp.concatenate([k_norm_w, k_norm_w], axis=1)

    (h1, qraw, kraw, q, k, va, qs, ks, vs, q4, k4, v4, q16, k16, v16,
     gate_g) = _attn_in(x2, attn_norm_w, a_g, cos2, sin2, qnw2, knw2, bd128, shards=[gate_loc])
    qkv_views = {1: (q, k, va), 4: (q4, k4, v4), 16: (q16, k16, v16)}
    fwd_riders = {1: [out_loc], 4: [down_loc[:FF_PAD // 2]], 16: [down_loc[FF_PAD // 2:]]}
    o_b, lse_b, gathered = [], [], {}
    for r in DILATIONS:
        o, lse, *gathered[r] = _dil_fwd(*qkv_views[r], r, shards=fwd_riders[r])
        o_b.append(o)
        lse_b.append(lse)
    (out_g,), (down_lo_g,), (down_hi_g,) = gathered[1], gathered[4], gathered[16]
    o_sb, c_sb, up_g = _sb_fwd(qs, ks, vs, tri_suf, shards=[up_loc])
    o_dil, lse_tot, lse4, lse16, mixed, x1 = _attn_out(
        o_b, lse_b, o_sb, x2, dil_out_norm_w, sb_out_norm_w, out_g, shards=[])
    g, u, h2, dy, loss_parts = _ffn_fwd(x1, ffn_norm_w, tgt, gate_g, up_g, down_lo_g, down_hi_g)
    loss_local = jnp.sum(loss_parts[::8, 0])

    dg, du, act, dh2 = _ffn_bwd_dx(dy, g, u, gate_g, up_g, down_lo_g, down_hi_g)
    (dx1, do_dil, delta, do_sb, dwout, dn2, dndil, dnsb, do4, dl4, do16, dl16) = _attn_out_bwd(
        dy, dh2, x1, ffn_norm_w, out_g, mixed, o_dil, o_sb, dil_out_norm_w, sb_out_norm_w, bd512)
    dwg, dwu, dwd_lo, dwd_hi = _ffn_bwd_dw(h2, dy, dg, du, act)
    dqs, dks, dvs, r_gate = _sb_bwd(qs, ks, vs, do_sb, c_sb, tri_suf, tri_pre, rides=[dwg])
    cot_views = {1: (do_dil, lse_tot, delta), 4: (do4, lse4, dl4), 16: (do16, lse16, dl16)}
    riders = {1: [dwd_lo], 4: [dwd_hi], 16: [dwu]}
    dq_b, dk_b, dv_b, landed = [], [], [], {}
    for r in DILATIONS:
        dq, dk, dv, *landed[r] = _dil_bwd(*qkv_views[r], *cot_views[r], r, rides=riders[r])
        dq_b.append(dq)
        dk_b.append(dk)
        dv_b.append(dv)
    (r_down_lo,), (r_down_hi,), (r_up,) = landed[1], landed[4], landed[16]
    dproj, dqn2, dkn2, r_out = _qkv_bwd(dq_b, dk_b, dv_b, dqs, dks, dvs, qraw, kraw, cos2, sin2, qnw2, knw2,
                                        bd128, rides=[dwout])
    dwin_lo, dwin_hi = _in_bwd_dw(h1, dproj)
    grad_x, dn1, r_in_lo = _in_bwd_dx(dproj, a_g, x2, dx1, attn_norm_w, rides=[dwin_lo])
    dqn = dqn2[:, :HEAD_DIM] + dqn2[:, HEAD_DIM:]
    dkn = dkn2[:, :HEAD_DIM] + dkn2[:, HEAD_DIM:]

    small = _pack_small(dn1, dn2, dndil, dnsb, dqn, dkn, loss_local)
    (r_small,) = _exchange_grads([], small)
    (new_gate, new_up, new_out), (r_in_hi,) = _adamw_many(
        [(r_gate, w_gate[0].T, m_w_gate[0].T, v_w_gate[0].T), (r_up, w_up[0].T, m_w_up[0].T, v_w_up[0].T),
         (r_out, w_out[0], m_w_out[0], v_w_out[0])], rides=[dwin_hi])
    big = {
        "w_in": _adamw(r_in_lo, w_in[0], m_w_in[0], v_w_in[0], recv_hi=r_in_hi),
        "w_gate": tuple(t.T for t in new_gate),
        "w_up": tuple(t.T for t in new_up),
        "w_down": _adamw(r_down_lo, w_down[0], m_w_down[0], v_w_down[0], recv_hi=r_down_hi),
        "w_out": new_out,
    }
    packs = [_pack_small(*ts) for ts in (
        (attn_norm_w, ffn_norm_w, dil_out_norm_w, sb_out_norm_w, q_norm_w, k_norm_w),
        (m_attn_norm_w, m_ffn_norm_w, m_dil_out_norm_w, m_sb_out_norm_w, m_q_norm_w, m_k_norm_w),
        (v_attn_norm_w, v_ffn_norm_w, v_dil_out_norm_w, v_sb_out_norm_w, v_q_norm_w, v_k_norm_w))]
    small_raw = _adamw(r_small, *packs)
    loss = small_raw[0][LOSS_ROW, 0]
    small_out = [_unpack_small(t) for t in small_raw]
    names = ["attn_norm_w", "w_in", "q_norm_w", "k_norm_w", "dil_out_norm_w", "sb_out_norm_w", "w_out",
             "ffn_norm_w", "w_gate", "w_up", "w_down"]
    small_pos = {"attn_norm_w": 0, "ffn_norm_w": 1, "dil_out_norm_w": 2, "sb_out_norm_w": 3,
                 "q_norm_w": 4, "k_norm_w": 5}
    outs = [loss, grad_x[None]]
    for kind in range(4):
        for name in names:
            if name in small_pos:
                outs.append(small_out[kind][small_pos[name]])
            else:
                outs.append(big[name][kind][None])
    return tuple(outs)
```

```python
import functools

import jax
import jax.numpy as jnp
from jax import lax
from jax.experimental import pallas as pl
from jax.experimental.pallas import tpu as pltpu

F32 = jnp.float32
BF16 = jnp.bfloat16

N_DEV = 8
D_MODEL = 1024
HEAD_DIM = 64
D_GRP = 512
D_IN = 6 * D_GRP
IN_SHARD = D_IN // N_DEV
FF_SHARD = 352
FF_PAD = 384
FF_BLOCK = 2 * FF_PAD
FF_STEPS = N_DEV // 2
OUT_SHARD = D_MODEL // N_DEV
BLOCK = 128
DILATIONS = (1, 4, 16)
ROPE_THETA = 10000.0
EPS = 1e-6
ATT_SCALE = HEAD_DIM ** -0.5
NEG = -1e30

ADAM_LR = 0.001
ADAM_B1 = 0.9
ADAM_B2 = 0.999
ADAM_EPS = 1e-08
ADAM_WD = 0.01
ADAM_STEP = 10

SB_TILE = 256
SB_DEAD = -104.0
SB_PAIRS = 4
SB_BWD_PAIRS = 2
ROW_TILE = 512
DW_ROW_TILE = 1024
VMEM_LIMIT = 56 * 1024 * 1024
MESH = pl.DeviceIdType.MESH


def _dot(a, b):
    return jnp.dot(a, b, preferred_element_type=F32)


def _dot_nt(a, b):
    return lax.dot_general(a, b, (((1,), (1,)), ((), ())), preferred_element_type=F32)


def _dot_tn(a, b):
    return lax.dot_general(a, b, (((0,), (0,)), ((), ())), preferred_element_type=F32)


def _mm_split(t, m):
    hi = t.astype(BF16)
    lo = (t - hi.astype(F32)).astype(BF16)
    return _dot(hi, m) + _dot(lo, m)


def _params(**kw):
    return pltpu.CompilerParams(vmem_limit_bytes=VMEM_LIMIT, **kw)


def _full(shape):
    nd = len(shape)
    return pl.BlockSpec(shape, lambda *_: (0,) * nd)


def _view_shape(s_len, r, dtype):
    return jax.ShapeDtypeStruct((s_len // r, r * D_GRP), dtype)


def _view_spec(tm, r):
    return pl.BlockSpec((tm // r, r * D_GRP), lambda i: (i, 0))


def _swap_halves(t):
    lane = lax.broadcasted_iota(jnp.int32, t.shape, 1)
    first = (lane & 32) == 0
    return jnp.where(first, pltpu.roll(t, 96, 1), pltpu.roll(t, 32, 1))


def _log_sigmoid_pair(z):
    neg_abs = lax.bitcast_convert_type(lax.bitcast_convert_type(z, jnp.uint32) | jnp.uint32(0x80000000), F32)
    lb = jnp.minimum(z, 0.0) - jnp.log(1.0 + jnp.exp(neg_abs))
    return lb, lb - z


def _cumsum_mm(t, tri):
    return _dot(t.astype(BF16), tri)


def _split_views(src_ref, stage_ref, views4, views16):
    slabs, n, _ = src_ref.shape
    n4, n16 = n // 4, n // 16
    for j in range(slabs):
        g, lanes = j // 4, 128 * (j % 4)
        src, stage = src_ref.at[j], stage_ref.at[j]
        for c4 in range(4):
            blk = src[pl.ds(c4, n4, stride=4), :]
            stage[n4 * c4:n4 * (c4 + 1), :] = blk
            col = D_GRP * c4 + lanes
            views4[g][:, col:col + 128] = blk.astype(views4[g].dtype)
        for c4 in range(4):
            for c1 in range(4):
                blk = stage[pl.ds(n4 * c4 + c1, n16, stride=4), :]
                col = D_GRP * (4 * c1 + c4) + lanes
                views16[g][:, col:col + 128] = blk.astype(views16[g].dtype)


def _merge_views(views4, views16, stage_ref, dst4_ref, dst16_ref):
    slabs, n, _ = dst4_ref.shape
    n4, n16 = n // 4, n // 16
    for j in range(slabs):
        g, lanes = j // 4, 128 * (j % 4)
        dst4, dst16, stage = dst4_ref.at[j], dst16_ref.at[j], stage_ref.at[j]
        for c4 in range(4):
            col = D_GRP * c4 + lanes
            dst4[pl.ds(c4, n4, stride=4), :] = views4[g][:, col:col + 128].astype(F32)
            for c1 in range(4):
                col = D_GRP * (4 * c1 + c4) + lanes
                stage[pl.ds(n4 * c4 + c1, n16, stride=4), :] = views16[g][:, col:col + 128].astype(F32)
        for c4 in range(4):
            dst16[pl.ds(c4, n4, stride=4), :] = stage[n4 * c4:n4 * (c4 + 1), :]


def _slab_group(ref, g):
    return jnp.concatenate([ref[4 * g + p] for p in range(4)], axis=1)


def _mesh_pos():
    return lax.axis_index("x"), lax.axis_index("y"), lax.axis_index("c")


def _flat_index(p):
    return 4 * p[0] + 2 * p[1] + p[2]


def _gather_weights(shards):
    n_arr = len(shards)

    def body(*refs):
        srcs, outs = refs[:n_arr], refs[n_arr:2 * n_arr]
        send_sems, recv_sems, local_sems = refs[2 * n_arr:]
        x, y, c = _mesh_pos()
        me, sibling = (x, y, c), (x, y, 1 - c)
        chips = [(1 - x, y), (x, 1 - y), (1 - x, 1 - y)]

        def copy(arr, k, block, to, own=False):
            dst = outs[arr].at[_flat_index(block)]
            return pltpu.make_async_remote_copy(
                src_ref=srcs[arr] if own else dst, dst_ref=dst,
                send_sem=send_sems.at[arr, k], recv_sem=recv_sems.at[arr, k],
                device_id=to, device_id_type=MESH)

        for arr in range(n_arr):
            mine = pltpu.make_async_copy(srcs[arr], outs[arr].at[_flat_index(me)], local_sems.at[arr])
            mine.start()
            first = [copy(arr, 0, me, sibling, own=True)]
            first += [copy(arr, 1 + j, me, (*chip, c), own=True) for j, chip in enumerate(chips)]
            for cp in first:
                cp.start()
        for arr in range(n_arr):
            passed = [copy(arr, 4 + j, (*chip, c), sibling) for j, chip in enumerate(chips)]
            for j, chip in enumerate(chips):
                copy(arr, 1 + j, (*chip, c), me).wait_recv()
                passed[j].start()
        for arr in range(n_arr):
            copy(arr, 0, sibling, me).wait_recv()
            for j, chip in enumerate(chips):
                copy(arr, 4 + j, (*chip, 1 - c), me).wait_recv()
            for k in range(7):
                copy(arr, k, me, me).wait_send()
            pltpu.make_async_copy(srcs[arr], outs[arr].at[_flat_index(me)], local_sems.at[arr]).wait()

    any_spec = pl.BlockSpec(memory_space=pl.ANY)
    return pl.pallas_call(
        body, name="gather_weights",
        out_shape=tuple(jax.ShapeDtypeStruct((N_DEV,) + s.shape, s.dtype) for s in shards),
        in_specs=[any_spec] * n_arr, out_specs=(any_spec,) * n_arr,
        scratch_shapes=[pltpu.SemaphoreType.DMA((n_arr, 7)), pltpu.SemaphoreType.DMA((n_arr, 7)),
                        pltpu.SemaphoreType.DMA((n_arr,))],
        compiler_params=pltpu.CompilerParams(has_side_effects=True),
    )(*shards)


def _peer_list(x, y, c):
    return [(1 - x if m & 4 else x, 1 - y if m & 2 else y, 1 - c if m & 1 else c) for m in range(1, N_DEV)]


def _exchange_grads(parts, small):
    n_arr = len(parts)

    def body(*refs):
        ins, outs = refs[:n_arr + 1], refs[n_arr + 1:2 * (n_arr + 1)]
        send_sems, recv_sems, local_sems = refs[2 * (n_arr + 1):]
        x, y, c = _mesh_pos()
        me = (x, y, c)
        my_idx = _flat_index(me)
        peers = []
        for m in range(1, N_DEV):
            peers.append((1 - x if m & 4 else x, 1 - y if m & 2 else y, 1 - c if m & 1 else c))

        def src_block(arr, dev):
            return ins[arr] if arr == n_arr else ins[arr].at[_flat_index(dev)]

        def copy(arr, k):
            return pltpu.make_async_remote_copy(
                src_ref=src_block(arr, peers[k]), dst_ref=outs[arr].at[my_idx],
                send_sem=send_sems.at[arr, k], recv_sem=recv_sems.at[arr, k],
                device_id=peers[k], device_id_type=MESH)

        def local(arr):
            return pltpu.make_async_copy(src_block(arr, me), outs[arr].at[my_idx], local_sems.at[arr])

        for arr in range(n_arr + 1):
            local(arr).start()
            for k in range(N_DEV - 1):
                copy(arr, k).start()
        for arr in range(n_arr + 1):
            for k in range(N_DEV - 1):
                cp = copy(arr, k)
                cp.wait_send()
                cp.wait_recv()
            local(arr).wait()

    any_spec = pl.BlockSpec(memory_space=pl.ANY)
    out_shape = tuple(jax.ShapeDtypeStruct(p.shape, p.dtype) for p in parts)
    out_shape += (jax.ShapeDtypeStruct((N_DEV,) + small.shape, small.dtype),)
    return pl.pallas_call(
        body, name="exchange_grads",
        out_shape=out_shape,
        in_specs=[any_spec] * (n_arr + 1), out_specs=(any_spec,) * (n_arr + 1),
        scratch_shapes=[pltpu.SemaphoreType.DMA((n_arr + 1, N_DEV - 1)),
                        pltpu.SemaphoreType.DMA((n_arr + 1, N_DEV - 1)),
                        pltpu.SemaphoreType.DMA((n_arr + 1,))],
        compiler_params=pltpu.CompilerParams(has_side_effects=True),
    )(*parts, small)


def _call_with_gather(body, shards, first_step, mid_step, last_step, *, name, grid, in_specs, out_specs,
                      out_shape, scratch_shapes=()):
    out_specs = tuple(out_specs) if isinstance(out_specs, (tuple, list)) else (out_specs,)
    out_shape = tuple(out_shape) if isinstance(out_shape, (tuple, list)) else (out_shape,)
    n_in, n_out, n_scr, n = len(in_specs), len(out_specs), len(scratch_shapes), len(shards)
    if n == 0:
        return pl.pallas_call(body, name=name, grid=grid, in_specs=list(in_specs), out_specs=out_specs,
                              out_shape=out_shape, scratch_shapes=list(scratch_shapes),
                              compiler_params=_params())

    def full_body(*refs):
        ins, srcs = refs[:n_in], refs[n_in:n_in + n]
        outs, lands = refs[n_in + n:n_in + n + n_out], refs[n_in + n + n_out:n_in + 2 * n + n_out]
        scratch = refs[n_in + 2 * n + n_out:n_in + 2 * n + n_out + n_scr]
        send_sems, recv_sems, local_sems = refs[-3:]
        x, y, c = _mesh_pos()
        me, sibling = (x, y, c), (x, y, 1 - c)
        chips = [(1 - x, y), (x, 1 - y), (1 - x, 1 - y)]

        def copy(a, k, block, to, own=False):
            dst = lands[a].at[_flat_index(block)]
            return pltpu.make_async_remote_copy(
                src_ref=srcs[a] if own else dst, dst_ref=dst,
                send_sem=send_sems.at[a, k], recv_sem=recv_sems.at[a, k],
                device_id=to, device_id_type=MESH)

        def local(a):
            return pltpu.make_async_copy(srcs[a], lands[a].at[_flat_index(me)], local_sems.at[a])

        @pl.when(first_step())
        def _():
            for a in range(n):
                local(a).start()
                copy(a, 0, me, sibling, own=True).start()
                for j, chip in enumerate(chips):
                    copy(a, 1 + j, me, (*chip, c), own=True).start()

        @pl.when(mid_step())
        def _():
            for a in range(n):
                for j, chip in enumerate(chips):
                    copy(a, 1 + j, (*chip, c), me).wait_recv()
                    copy(a, 4 + j, (*chip, c), sibling).start()

        body(*ins, *outs, *scratch)

        @pl.when(last_step())
        def _():
            for a in range(n):
                copy(a, 0, sibling, me).wait_recv()
                for j, chip in enumerate(chips):
                    copy(a, 4 + j, (*chip, 1 - c), me).wait_recv()
                for k in range(N_DEV - 1):
                    copy(a, k, me, me).wait_send()
                local(a).wait()

    any_spec = pl.BlockSpec(memory_space=pl.ANY)
    return pl.pallas_call(
        full_body, name=name, grid=grid,
        in_specs=list(in_specs) + [any_spec] * n,
        out_specs=out_specs + (any_spec,) * n,
        out_shape=out_shape + tuple(jax.ShapeDtypeStruct((N_DEV,) + t.shape, t.dtype) for t in shards),
        scratch_shapes=list(scratch_shapes) + [pltpu.SemaphoreType.DMA((n, N_DEV - 1)),
                                               pltpu.SemaphoreType.DMA((n, N_DEV - 1)),
                                               pltpu.SemaphoreType.DMA((n,))],
        compiler_params=_params(has_side_effects=True),
    )


def _call_with_exchange(body, rides, first_step, last_step, *, name, grid, in_specs, out_specs, out_shape,
                        scratch_shapes=()):
    out_specs = tuple(out_specs) if isinstance(out_specs, (tuple, list)) else (out_specs,)
    out_shape = tuple(out_shape) if isinstance(out_shape, (tuple, list)) else (out_shape,)
    n_in, n_out, n_scr, n = len(in_specs), len(out_specs), len(scratch_shapes), len(rides)
    if n == 0:
        return pl.pallas_call(body, name=name, grid=grid, in_specs=list(in_specs), out_specs=out_specs,
                              out_shape=out_shape, scratch_shapes=list(scratch_shapes),
                              compiler_params=_params())

    def full_body(*refs):
        ins, srcs = refs[:n_in], refs[n_in:n_in + n]
        outs, lands = refs[n_in + n:n_in + n + n_out], refs[n_in + n + n_out:n_in + 2 * n + n_out]
        scratch = refs[n_in + 2 * n + n_out:n_in + 2 * n + n_out + n_scr]
        send_sems, recv_sems, local_sems = refs[-3:]
        x, y, c = _mesh_pos()
        my_idx = _flat_index((x, y, c))
        peers = _peer_list(x, y, c)

        def remote(a, k):
            return pltpu.make_async_remote_copy(
                src_ref=srcs[a].at[_flat_index(peers[k])], dst_ref=lands[a].at[my_idx],
                send_sem=send_sems.at[a, k], recv_sem=recv_sems.at[a, k],
                device_id=peers[k], device_id_type=MESH)

        def local(a):
            return pltpu.make_async_copy(srcs[a].at[my_idx], lands[a].at[my_idx], local_sems.at[a])

        @pl.when(first_step())
        def _():
            for a in range(n):
                local(a).start()
                for k in range(N_DEV - 1):
                    remote(a, k).start()

        body(*ins, *outs, *scratch)

        @pl.when(last_step())
        def _():
            for a in range(n):
                for k in range(N_DEV - 1):
                    cp = remote(a, k)
                    cp.wait_send()
                    cp.wait_recv()
                local(a).wait()

    any_spec = pl.BlockSpec(memory_space=pl.ANY)
    res = pl.pallas_call(
        full_body, name=name, grid=grid,
        in_specs=list(in_specs) + [any_spec] * n,
        out_specs=out_specs + (any_spec,) * n,
        out_shape=out_shape + tuple(jax.ShapeDtypeStruct(t.shape, t.dtype) for t in rides),
        scratch_shapes=list(scratch_shapes) + [pltpu.SemaphoreType.DMA((n, N_DEV - 1)),
                                               pltpu.SemaphoreType.DMA((n, N_DEV - 1)),
                                               pltpu.SemaphoreType.DMA((n,))],
        compiler_params=_params(has_side_effects=True),
    )
    return res


def _head_norm(t, w128, bd):
    ms = _mm_split(t * t, bd) * (1.0 / HEAD_DIM)
    r = lax.rsqrt(ms + EPS)
    return (t * r) * w128, r


def _attn_in(x2, wn1, a_g, cos2, sin2, qnw, knw, bd, shards):
    s_len = x2.shape[0]
    tm = ROW_TILE

    def body(x_ref, wn_ref, w_ref, cos_ref, sin_ref, qnw_ref, knw_ref, bd_ref,
             h1_ref, qraw_ref, kraw_ref, q_ref, k_ref, va_ref, qs_ref, ks_ref, vs_ref,
             q4_ref, k4_ref, v4_ref, q16_ref, k16_ref, v16_ref, proj, slabs, stage, w_full):
        @pl.when(pl.program_id(0) == 0)
        def _():
            for d in range(N_DEV):
                w_full[:, IN_SHARD * d:IN_SHARD * (d + 1)] = w_ref[d]

        xx = x_ref[...]
        r = lax.rsqrt(jnp.mean(xx * xx, axis=-1, keepdims=True) + EPS)
        h = ((xx * r) * wn_ref[...]).astype(BF16)
        h1_ref[...] = h
        proj[...] = _dot(h, w_full[...])
        cos_t, sin_t, bdm = cos_ref[...], sin_ref[...], bd_ref[...]
        for grp, (raw_ref, rope_ref, nw_ref) in enumerate(((qraw_ref, q_ref, qnw_ref),
                                                           (kraw_ref, k_ref, knw_ref))):
            for p in range(4):
                cols = slice(D_GRP * grp + 128 * p, D_GRP * grp + 128 * (p + 1))
                t = proj[:, cols]
                raw_ref[:, 128 * p:128 * (p + 1)] = t
                yn, _ = _head_norm(t, nw_ref[...], bdm)
                roped = yn * cos_t + _swap_halves(yn) * sin_t
                slabs[4 * grp + p] = roped
                rope_ref[:, 128 * p:128 * (p + 1)] = roped.astype(BF16)
        for p in range(4):
            slabs[8 + p] = proj[:, 2 * D_GRP + 128 * p:2 * D_GRP + 128 * (p + 1)]
        for grp, ref in ((2, va_ref), (3, qs_ref), (4, ks_ref), (5, vs_ref)):
            ref[...] = proj[:, D_GRP * grp:D_GRP * (grp + 1)].astype(BF16)
        _split_views(slabs, stage, (q4_ref, k4_ref, v4_ref), (q16_ref, k16_ref, v16_ref))

    row = lambda w: pl.BlockSpec((tm, w), lambda i: (i, 0))
    grp_bf = jax.ShapeDtypeStruct((s_len, D_GRP), BF16)
    grp_f32 = jax.ShapeDtypeStruct((s_len, D_GRP), F32)
    ni = s_len // tm
    return _call_with_gather(
        body, shards, lambda: pl.program_id(0) == 0, lambda: pl.program_id(0) == ni - 2,
        lambda: pl.program_id(0) == ni - 1,
        name="attn_in", grid=(ni,),
        in_specs=[row(D_MODEL), _full((1, D_MODEL)),
                  pl.BlockSpec((N_DEV, D_MODEL, IN_SHARD), lambda i: (0, 0, 0)),
                  row(128), row(128), _full((1, 128)), _full((1, 128)), _full((128, 128))],
        out_specs=(row(D_MODEL),) + (row(D_GRP),) * 8 + (_view_spec(tm, 4),) * 3 + (_view_spec(tm, 16),) * 3,
        out_shape=(jax.ShapeDtypeStruct((s_len, D_MODEL), BF16), grp_f32, grp_f32) + (grp_bf,) * 6
        + (_view_shape(s_len, 4, BF16),) * 3 + (_view_shape(s_len, 16, BF16),) * 3,
        scratch_shapes=[pltpu.VMEM((tm, D_IN), F32), pltpu.VMEM((12, tm, 128), F32), pltpu.VMEM((12, tm, 128), F32),
                        pltpu.VMEM((D_MODEL, D_IN), BF16)],
    )(x2, wn1, a_g, cos2, sin2, qnw, knw, bd, *shards)


def _band_mask(n):
    i = lax.broadcasted_iota(jnp.int32, (2 * BLOCK, 2 * BLOCK), 0) & (BLOCK - 1)
    j = lax.broadcasted_iota(jnp.int32, (2 * BLOCK, 2 * BLOCK), 1)
    dist = i + BLOCK - j
    return (dist >= 0) & (dist <= BLOCK) & ((n - 1) * BLOCK + j >= 0)


def _stack_heads(t2, head0):
    return jnp.concatenate([jnp.where(head0, t2, 0), jnp.where(head0, 0, t2)], axis=0)


def _unstack_heads(t, head0):
    return jnp.where(head0, t[0:BLOCK], t[BLOCK:2 * BLOCK])


def _dil_fwd(qv, kv, vv, r, shards):
    sub_len = qv.shape[0]
    nb = sub_len // BLOCK

    qb = 2 if nb % 2 == 0 else 1

    def body(q_ref, kp_ref, kc_ref, vp_ref, vc_ref, o_ref, lse_ref):
        n = pl.program_id(1)
        lane = lax.broadcasted_iota(jnp.int32, (BLOCK, 128), 1)
        head0 = lane < HEAD_DIM
        units = [(b, slice(128 * p, 128 * (p + 1))) for b in range(qb) for p in range(4)]
        valid = [_band_mask(qb * n + b) for b in range(qb)]
        rows = [slice(BLOCK * b, BLOCK * (b + 1)) for b in range(qb)]

        def keys(prev_ref, cur_ref, b, c):
            before = prev_ref[:, c] if b == 0 else cur_ref[rows[b - 1], c]
            return jnp.concatenate([before, cur_ref[rows[b], c]], axis=0)

        qqs = [_stack_heads(q_ref[rows[b], c] * ATT_SCALE, head0) for b, c in units]
        kks = [keys(kp_ref, kc_ref, b, c) for b, c in units]
        vvs = [keys(vp_ref, vc_ref, b, c) for b, c in units]
        ss = [_dot_nt(qq, kk) for qq, kk in zip(qqs, kks)]
        prs, dens, lses = [], [], []
        for (b, _), s in zip(units, ss):
            s = jnp.where(valid[b], s, NEG)
            m = jnp.max(s, axis=-1, keepdims=True)
            pr = jnp.exp(s - m)
            den = jnp.sum(pr, axis=-1, keepdims=True)
            prs.append(pr.astype(BF16))
            dens.append(den)
            lses.append(m + jnp.log(den))
        pvs = [_dot(pr, vv2) for pr, vv2 in zip(prs, vvs)]
        for (b, c), pv, den, lse in zip(units, pvs, dens, lses):
            o_ref[rows[b], c] = _unstack_heads(pv / den, head0)
            lse_ref[rows[b], c] = _unstack_heads(jnp.broadcast_to(lse, (2 * BLOCK, 128)), head0)

    cur = pl.BlockSpec((qb * BLOCK, D_GRP), lambda c, n: (n, c))
    prev = pl.BlockSpec((BLOCK, D_GRP), lambda c, n: (jnp.maximum(qb * n - 1, 0), c))
    out = jax.ShapeDtypeStruct(qv.shape, F32)
    steps = nb // qb

    def at(t):
        return lambda: pl.program_id(0) * steps + pl.program_id(1) == t

    return _call_with_gather(
        body, shards, at(0), at((2 * r * steps) // 3), at(r * steps - 1),
        name=f"dil_fwd_r{r}", grid=(r, steps),
        in_specs=[cur, prev, cur, prev, cur], out_specs=(cur, cur), out_shape=(out, out),
    )(qv, kv, kv, vv, vv, *shards)


def _dil_bwd(qv, kv, vv, dov, lsev, deltav, r, rides):
    sub_len = qv.shape[0]
    nb = sub_len // BLOCK

    def body(q_ref, kp_ref, kc_ref, vp_ref, vc_ref, do_ref, lse_ref, dl_ref,
             dq_ref, dk_ref, dv_ref, dk_carry, dv_carry):
        n = pl.program_id(1)

        @pl.when(n == 0)
        def _():
            dk_carry[...] = jnp.zeros_like(dk_carry)
            dv_carry[...] = jnp.zeros_like(dv_carry)

        @pl.when(n < nb)
        def _():
            valid = _band_mask(n)
            lane = lax.broadcasted_iota(jnp.int32, (BLOCK, 128), 1)
            head0 = lane < HEAD_DIM
            pairs = [slice(128 * p, 128 * (p + 1)) for p in range(4)]
            qqs = [_stack_heads(q_ref[:, c] * ATT_SCALE, head0) for c in pairs]
            dos = [_stack_heads(do_ref[:, c], head0) for c in pairs]
            kks = [jnp.concatenate([kp_ref[:, c], kc_ref[:, c]], axis=0) for c in pairs]
            vvs = [jnp.concatenate([vp_ref[:, c], vc_ref[:, c]], axis=0) for c in pairs]
            ss = [_dot_nt(qq, kk) for qq, kk in zip(qqs, kks)]
            dps = [_dot_nt(do, vv2) for do, vv2 in zip(dos, vvs)]
            def softmax_terms(p):
                stats = []
                for ref in (lse_ref, dl_ref):
                    t2 = ref[:, pairs[p]]
                    stats.append(jnp.concatenate(
                        [jnp.sum(jnp.where(lane == 0, t2, 0.0), axis=-1, keepdims=True),
                         jnp.sum(jnp.where(lane == HEAD_DIM, t2, 0.0), axis=-1, keepdims=True)], axis=0))
                pr = jnp.where(valid, jnp.exp(jnp.minimum(ss[p] - stats[0], 0.0)), 0.0)
                return pr.astype(BF16), (pr * (dps[p] - stats[1])).astype(BF16)

            terms = [softmax_terms(p) for p in range(4)]
            dqs = [_dot(terms[p][1], kks[p]) for p in range(4)]
            dkks = [_dot_tn(terms[p][1], qqs[p]) for p in range(4)]
            dvvs = [_dot_tn(terms[p][0], dos[p]) for p in range(4)]
            for c, dq, dkk, dvv in zip(pairs, dqs, dkks, dvvs):
                dq_ref[:, c] = _unstack_heads(dq, head0) * ATT_SCALE
                dk_ref[:, c] = dk_carry[:, c] + dkk[:BLOCK]
                dv_ref[:, c] = dv_carry[:, c] + dvv[:BLOCK]
                dk_carry[:, c] = dkk[BLOCK:]
                dv_carry[:, c] = dvv[BLOCK:]

        @pl.when(n == nb)
        def _():
            dk_ref[...] = dk_carry[...]
            dv_ref[...] = dv_carry[...]

    last = nb - 1
    cur = pl.BlockSpec((BLOCK, D_GRP), lambda c, n: (jnp.minimum(n, last), c))
    prev = pl.BlockSpec((BLOCK, D_GRP), lambda c, n: (jnp.clip(n - 1, 0, last), c))
    out = jax.ShapeDtypeStruct(qv.shape, F32)
    return _call_with_exchange(
        body, rides,
        lambda: jnp.logical_and(pl.program_id(0) == 0, pl.program_id(1) == 0),
        lambda: jnp.logical_and(pl.program_id(0) == r - 1, pl.program_id(1) == nb),
        name=f"dil_bwd_r{r}", grid=(r, nb + 1),
        in_specs=[cur, prev, cur, prev, cur, cur, cur, cur],
        out_specs=(cur, prev, prev), out_shape=(out, out, out),
        scratch_shapes=[pltpu.VMEM((BLOCK, D_GRP), F32), pltpu.VMEM((BLOCK, D_GRP), F32)],
    )(qv, kv, kv, vv, vv, dov, lsev, deltav, *rides)


def _sb_fwd(qs, ks, vs, tri_suf, shards):
    s_len = qs.shape[0]
    t = SB_TILE
    nq = s_len // t

    npair = SB_PAIRS

    def body(q_ref, k_ref, v_ref, u_ref, o_ref, c_ref, qq, vt, acc, cf, csave):
        row = lax.broadcasted_iota(jnp.int32, (2 * t, t), 0) & (t - 1)
        col = lax.broadcasted_iota(jnp.int32, (2 * t, t), 1)
        diag_mask = col < row
        lane1 = lax.broadcasted_iota(jnp.int32, (t, 128), 1)
        head0 = lane1 < HEAD_DIM
        lane2 = lax.broadcasted_iota(jnp.int32, (2 * t, 128), 1)
        uu = u_ref[...]
        pr = range(npair)
        cols = [slice(128 * pp, 128 * (pp + 1)) for pp in pr]

        i = pl.program_id(1)

        @pl.when(i == 0)
        def _():
            def transpose_v(j, _):
                rows = pl.ds(pl.multiple_of(j * t, t), t)
                for pp in pr:
                    vt[pp, j] = v_ref[rows, cols[pp]].astype(F32).T.astype(BF16)
                return 0

            lax.fori_loop(0, nq, transpose_v, 0)

        for pp in pr:
            q2 = q_ref[:, cols[pp]] * ATT_SCALE
            qq[pp, 0:t, :] = jnp.where(head0, q2, 0)
            qq[pp, t:2 * t, :] = jnp.where(head0, 0, q2)
        acc[...] = jnp.zeros_like(acc)
        cf[...] = jnp.zeros_like(cf)
        csave[...] = jnp.full(csave.shape, 2.0 * SB_DEAD, F32)

        def tile(kb, diag):
            krows = pl.ds(pl.multiple_of(kb * t, t), t)
            zs = [_dot_nt(qq[pp], k_ref[krows, cols[pp]]) for pp in pr]
            lbk = [_log_sigmoid_pair(z) for z in zs]
            lks = [jnp.where(diag_mask, lk, 0.0) if diag else lk for _, lk in lbk]
            sufs = [_cumsum_mm(lk, uu) for lk in lks]
            carries = [cf[pp] for pp in pr]
            avs = []
            for pp in pr:
                a = jnp.exp(lbk[pp][0] + (sufs[pp] + jnp.concatenate([carries[pp]] * (t // 128), axis=1)))
                avs.append((jnp.where(diag_mask, a, 0.0) if diag else a).astype(BF16))
            pvs = [_dot_nt(vt[pp, kb], avs[pp]) for pp in pr]
            for pp in pr:
                acc[pp] += pvs[pp]
                csave[pp] = jnp.where(lane2 == kb, carries[pp], csave[pp])
                cf[pp] = carries[pp] + jnp.broadcast_to(jnp.sum(lks[pp], axis=-1, keepdims=True), (2 * t, 128))

        tile(i, True)

        def alive():
            return jnp.max(cf[...]) > SB_DEAD

        def k_block(state):
            kb, _ = state
            tile(kb, False)
            return kb - 1, alive()

        lax.while_loop(lambda state: jnp.logical_and(state[0] >= 0, state[1]), k_block, (i - 1, alive()))
        for pp in pr:
            o_ref[:, cols[pp]] = jnp.where(head0, acc[pp, :, 0:t].T, acc[pp, :, t:2 * t].T)
            c_ref[2 * pp] = csave[pp, 0:t, :]
            c_ref[2 * pp + 1] = csave[pp, t:2 * t, :]

    width = 128 * npair
    kv = pl.BlockSpec((s_len, width), lambda p, i: (0, p))
    qo = pl.BlockSpec((t, width), lambda p, i: (i, p))
    steps = 4 // npair

    def at(p, i):
        return lambda: jnp.logical_and(pl.program_id(0) == p, pl.program_id(1) == i)

    return _call_with_gather(
        body, shards, at(0, 0), at(steps - 1, (2 * nq) // 3), at(steps - 1, nq - 1),
        name="sb_fwd", grid=(steps, nq),
        in_specs=[qo, kv, kv, pl.BlockSpec((t, t), lambda p, i: (0, 0))],
        out_specs=(qo, pl.BlockSpec((2 * npair, t, 128), lambda p, i: (p, i, 0))),
        out_shape=(jax.ShapeDtypeStruct((s_len, D_GRP), F32),
                   jax.ShapeDtypeStruct((8, s_len, 128), F32)),
        scratch_shapes=[pltpu.VMEM((npair, 2 * t, 128), BF16), pltpu.VMEM((npair, nq, 128, t), BF16),
                        pltpu.VMEM((npair, 128, 2 * t), F32),
                        pltpu.VMEM((npair, 2 * t, 128), F32), pltpu.VMEM((npair, 2 * t, 128), F32)],
    )(qs, ks, vs, tri_suf, *shards)


def _sb_bwd(qs, ks, vs, dos, csaved, tri_suf, tri_pre, rides):
    s_len = qs.shape[0]
    t = SB_TILE
    nq = s_len // t

    npair = SB_BWD_PAIRS

    def body(q_ref, k_ref, v_ref, do_ref, c_ref, u_ref, p_ref, dq_ref, dk_ref, dv_ref,
             qq, dd, qqt, ddt, kt, dq_acc, dkt, dvt, cg):
        row = lax.broadcasted_iota(jnp.int32, (2 * t, t), 0) & (t - 1)
        col = lax.broadcasted_iota(jnp.int32, (2 * t, t), 1)
        diag_mask = col < row
        lane1 = lax.broadcasted_iota(jnp.int32, (t, 128), 1)
        head0 = lane1 < HEAD_DIM
        lane2 = lax.broadcasted_iota(jnp.int32, (2 * t, 128), 1)
        uu, pm = u_ref[...], p_ref[...]
        pr = range(npair)
        cols = [slice(128 * pp, 128 * (pp + 1)) for pp in pr]
        i = pl.program_id(1)

        @pl.when(i == 0)
        def _():
            dkt[...] = jnp.zeros_like(dkt)
            dvt[...] = jnp.zeros_like(dvt)

            def transpose_k(j, _):
                rows = pl.ds(pl.multiple_of(j * t, t), t)
                for pp in pr:
                    kt[pp, j] = k_ref[rows, cols[pp]].astype(F32).T.astype(BF16)
                return 0

            lax.fori_loop(0, nq, transpose_k, 0)

        for pp in pr:
            q2 = q_ref[:, cols[pp]].astype(F32) * ATT_SCALE
            do2 = do_ref[:, cols[pp]].astype(F32)
            for src, nat, tr in ((q2, qq, qqt), (do2, dd, ddt)):
                stacked = jnp.concatenate([jnp.where(head0, src, 0.0), jnp.where(head0, 0.0, src)], axis=0)
                nat[pp] = stacked.astype(BF16)
                tr[pp] = stacked.T.astype(BF16)
        dq_acc[...] = jnp.zeros_like(dq_acc)
        cg[...] = jnp.zeros_like(cg)

        def tile(kb, diag):
            krows = pl.ds(pl.multiple_of(kb * t, t), t)
            zs = [_dot_nt(qq[pp], k_ref[krows, cols[pp]]) for pp in pr]
            das = [_dot_nt(dd[pp], v_ref[krows, cols[pp]]) for pp in pr]
            lbk = [_log_sigmoid_pair(z) for z in zs]
            lks = [jnp.where(diag_mask, lk, 0.0) if diag else lk for _, lk in lbk]
            sufs = [_cumsum_mm(lk, uu) for lk in lks]
            avs, gs = [], []
            for pp in pr:
                cs = jnp.concatenate([c_ref[2 * pp], c_ref[2 * pp + 1]], axis=0)
                cf = jnp.sum(jnp.where(lane2 == kb, cs, 0.0), axis=-1, keepdims=True)
                a = jnp.exp(lbk[pp][0] + (sufs[pp] + cf))
                a = jnp.where(diag_mask, a, 0.0) if diag else a
                avs.append(a.astype(BF16))
                gs.append(a * das[pp])
            gpres = [_cumsum_mm(g, pm) for g in gs]
            dzs = []
            for pp in pr:
                carry = cg[pp]
                beta = jnp.exp(lbk[pp][0])
                dz = gs[pp] - beta * (gs[pp] + (gpres[pp] + jnp.concatenate([carry] * (t // 128), axis=1)))
                dzs.append((jnp.where(diag_mask, dz, 0.0) if diag else dz).astype(BF16))
                cg[pp] = carry + jnp.broadcast_to(jnp.sum(gs[pp], axis=-1, keepdims=True), (2 * t, 128))
            dqs = [_dot_nt(kt[pp, kb], dzs[pp]) for pp in pr]
            dks = [_dot(qqt[pp], dzs[pp]) for pp in pr]
            dvs = [_dot(ddt[pp], avs[pp]) for pp in pr]
            for pp in pr:
                dq_acc[pp] += dqs[pp]
                dkt[pp, kb] += dks[pp]
                dvt[pp, kb] += dvs[pp]

        def k_block(kb, _):
            tile(kb, False)
            return 0

        col_max = jnp.max(jnp.max(c_ref[...], axis=0), axis=0, keepdims=True)
        lane_row = lax.broadcasted_iota(jnp.int32, (1, 128), 1)
        n_live = jnp.sum(jnp.where(jnp.logical_and(col_max > SB_DEAD, lane_row < i), 1, 0))
        lax.fori_loop(i - n_live, i, k_block, 0)
        tile(i, True)
        for pp in pr:
            dq_ref[:, cols[pp]] = jnp.where(head0, dq_acc[pp, :, 0:t].T, dq_acc[pp, :, t:2 * t].T) * ATT_SCALE

        @pl.when(i == nq - 1)
        def _():
            def untranspose(j, _):
                rows = pl.ds(pl.multiple_of(j * t, t), t)
                for pp in pr:
                    dk_ref[rows, cols[pp]] = dkt[pp, j].T
                    dv_ref[rows, cols[pp]] = dvt[pp, j].T
                return 0

            lax.fori_loop(0, nq, untranspose, 0)

    width = 128 * npair
    kv = pl.BlockSpec((s_len, width), lambda p, i: (0, p))
    qo = pl.BlockSpec((t, width), lambda p, i: (i, p))
    tri = pl.BlockSpec((t, t), lambda p, i: (0, 0))
    out = jax.ShapeDtypeStruct((s_len, D_GRP), F32)
    steps = 4 // npair
    return _call_with_exchange(
        body, rides,
        lambda: jnp.logical_and(pl.program_id(0) == 0, pl.program_id(1) == 0),
        lambda: jnp.logical_and(pl.program_id(0) == steps - 1, pl.program_id(1) == nq - 1),
        name="sb_bwd", grid=(steps, nq),
        in_specs=[qo, kv, kv, qo, pl.BlockSpec((2 * npair, t, 128), lambda p, i: (p, i, 0)), tri, tri],
        out_specs=(qo, kv, kv), out_shape=(out, out, out),
        scratch_shapes=[pltpu.VMEM((npair, 2 * t, 128), BF16), pltpu.VMEM((npair, 2 * t, 128), BF16),
                        pltpu.VMEM((npair, 128, 2 * t), BF16), pltpu.VMEM((npair, 128, 2 * t), BF16),
                        pltpu.VMEM((npair, nq, 128, t), BF16),
                        pltpu.VMEM((npair, 128, 2 * t), F32),
                        pltpu.VMEM((npair, nq, 128, t), F32), pltpu.VMEM((npair, nq, 128, t), F32),
                        pltpu.VMEM((npair, 2 * t, 128), F32)],
    )(qs, ks, vs, dos, csaved, tri_suf, tri_pre, *rides)


def _attn_out(o_b, lse_b, o_sb, x2, wdil, wsb, out_g, shards):
    s_len = x2.shape[0]
    tm = ROW_TILE

    def body(o1_ref, l1_ref, o4_ref, l4_ref, o16_ref, l16_ref, osb_ref, x_ref, wdil_ref, wsb_ref, w_ref,
             odil_ref, lse_ref, lse4_ref, lse16_ref, mixed_ref, x1_ref, stage, nat4, nat16):
        _merge_views((o4_ref, l4_ref), (o16_ref, l16_ref), stage, nat4, nat16)
        os_ = (o1_ref[...], _slab_group(nat4, 0), _slab_group(nat16, 0))
        ls = (l1_ref[...], _slab_group(nat4, 1), _slab_group(nat16, 1))
        mx = jnp.maximum(jnp.maximum(ls[0], ls[1]), ls[2])
        es = [jnp.exp(l - mx) for l in ls]
        den = es[0] + es[1] + es[2]
        o_dil = (es[0] * os_[0] + es[1] * os_[1] + es[2] * os_[2]) / den
        odil_ref[...] = o_dil
        lse = mx + jnp.log(den)
        lse_ref[...] = lse
        for p in range(4):
            nat4[p] = lse[:, 128 * p:128 * (p + 1)]
        _split_views(nat4.at[0:4], stage.at[0:4], (lse4_ref,), (lse16_ref,))
        halves = []
        for t, w_r in ((o_dil, wdil_ref), (osb_ref[...], wsb_ref)):
            r = lax.rsqrt(jnp.mean(t * t, axis=-1, keepdims=True) + EPS)
            halves.append(((t * r) * w_r[...]).astype(BF16))
        mixed = jnp.concatenate(halves, axis=1)
        mixed_ref[...] = mixed
        w = w_ref[...].reshape(D_MODEL, D_MODEL)
        x1_ref[...] = x_ref[...] + _dot(mixed, w)

    row = lambda w: pl.BlockSpec((tm, w), lambda i: (i, 0))
    ni = s_len // tm
    return _call_with_gather(
        body, shards, lambda: pl.program_id(0) == 0, lambda: pl.program_id(0) == ni - 2,
        lambda: pl.program_id(0) == ni - 1,
        name="attn_out", grid=(ni,),
        in_specs=[row(D_GRP)] * 2 + [_view_spec(tm, 4)] * 2 + [_view_spec(tm, 16)] * 2
        + [row(D_GRP), row(D_MODEL), _full((1, D_GRP)), _full((1, D_GRP)), _full((N_DEV, OUT_SHARD, D_MODEL))],
        out_specs=(row(D_GRP), row(D_GRP), _view_spec(tm, 4), _view_spec(tm, 16), row(D_MODEL), row(D_MODEL)),
        out_shape=(jax.ShapeDtypeStruct((s_len, D_GRP), F32), jax.ShapeDtypeStruct((s_len, D_GRP), F32),
                   _view_shape(s_len, 4, F32), _view_shape(s_len, 16, F32),
                   jax.ShapeDtypeStruct((s_len, D_MODEL), BF16), jax.ShapeDtypeStruct((s_len, D_MODEL), F32)),
        scratch_shapes=[pltpu.VMEM((8, tm, 128), F32)] * 3,
    )(o_b[0], lse_b[0], o_b[1], lse_b[1], o_b[2], lse_b[2], o_sb, x2, wdil, wsb, out_g, *shards)


def _two_shards(w_ref):
    return w_ref[...].reshape(FF_BLOCK, D_MODEL)


def _ffn_fwd(x1, wn2, tgt, gate_g, up_g, down_g):
    s_len = x1.shape[0]
    tm = ROW_TILE
    ni = s_len // tm

    def body(x_ref, wn_ref, t_ref, wg_ref, wu_ref, wd_ref, g_ref, u_ref, h2_ref, dy_ref, loss_ref, acc):
        j = pl.program_id(1)

        @pl.when(j == 0)
        def _():
            xx = x_ref[...]
            r = lax.rsqrt(jnp.mean(xx * xx, axis=-1, keepdims=True) + EPS)
            h2_ref[...] = ((xx * r) * wn_ref[...]).astype(BF16)
            acc[...] = jnp.zeros_like(acc)

        h = h2_ref[...]
        g = _dot_nt(h, _two_shards(wg_ref))
        u = _dot_nt(h, _two_shards(wu_ref))
        g_ref[...] = g
        u_ref[...] = u
        act = (g * (1.0 / (1.0 + jnp.exp(-g)))) * u
        acc[...] += _dot(act.astype(BF16), _two_shards(wd_ref))

        @pl.when(j == FF_STEPS - 1)
        def _():
            err = (x_ref[...] + acc[...]) - t_ref[...]
            dy_ref[...] = err * (1.0 / D_MODEL)
            part = 0.5 * jnp.sum(jnp.mean(err * err, axis=-1, keepdims=True))
            loss_ref[...] = jnp.full((8, 128), part, F32)

    row = pl.BlockSpec((tm, D_MODEL), lambda i, j: (i, 0))
    hid = pl.BlockSpec((tm, FF_BLOCK), lambda i, j: (i, j))
    return pl.pallas_call(
        body, name="ffn_fwd", grid=(ni, FF_STEPS),
        in_specs=[row, pl.BlockSpec((1, D_MODEL), lambda i, j: (0, 0)), row,
                  pl.BlockSpec((2, FF_PAD, D_MODEL), lambda i, j: (j, 0, 0)),
                  pl.BlockSpec((2, FF_PAD, D_MODEL), lambda i, j: (j, 0, 0)),
                  pl.BlockSpec((2, FF_PAD, D_MODEL), lambda i, j: (j, 0, 0))],
        out_specs=(hid, hid, row, row, pl.BlockSpec((8, 128), lambda i, j: (i, 0))),
        out_shape=(jax.ShapeDtypeStruct((s_len, N_DEV * FF_PAD), F32),
                   jax.ShapeDtypeStruct((s_len, N_DEV * FF_PAD), F32),
                   jax.ShapeDtypeStruct((s_len, D_MODEL), BF16),
                   jax.ShapeDtypeStruct((s_len, D_MODEL), F32),
                   jax.ShapeDtypeStruct((ni * 8, 128), F32)),
        scratch_shapes=[pltpu.VMEM((tm, D_MODEL), F32)],
        compiler_params=_params(),
    )(x1, wn2, tgt, gate_g, up_g, down_g)


def _ffn_bwd_dx(dy, g, u, gate_g, up_g, down_g):
    s_len = dy.shape[0]
    tm = ROW_TILE

    def body(dy_ref, g_ref, u_ref, wg_ref, wu_ref, wd_ref, dg_ref, du_ref, act_ref, dh_ref, acc):
        j = pl.program_id(1)

        @pl.when(j == 0)
        def _():
            acc[...] = jnp.zeros_like(acc)

        halves = [slice(0, tm // 2), slice(tm // 2, tm)]
        wd, wg, wu = _two_shards(wd_ref), _two_shards(wg_ref), _two_shards(wu_ref)
        das = [_dot_nt(dy_ref[rows, :].astype(BF16), wd) for rows in halves]

        def elementwise(rows, da):
            gg, uu = g_ref[rows, :], u_ref[rows, :]
            sig = 1.0 / (1.0 + jnp.exp(-gg))
            silu = gg * sig
            act_ref[rows, :] = (silu * uu).astype(BF16)
            du = (da * silu).astype(BF16)
            dg = (da * uu * (sig * (1.0 + gg * (1.0 - sig)))).astype(BF16)
            du_ref[rows, :] = du
            dg_ref[rows, :] = dg
            return dg, du

        dg0, du0 = elementwise(halves[0], das[0])
        acc[halves[0], :] += _dot(dg0, wg) + _dot(du0, wu)
        dg1, du1 = elementwise(halves[1], das[1])
        acc[halves[1], :] += _dot(dg1, wg) + _dot(du1, wu)

        @pl.when(j == FF_STEPS - 1)
        def _():
            dh_ref[...] = acc[...]

    row = pl.BlockSpec((tm, D_MODEL), lambda i, j: (i, 0))
    hid = pl.BlockSpec((tm, FF_BLOCK), lambda i, j: (i, j))
    hid_bf = jax.ShapeDtypeStruct((s_len, N_DEV * FF_PAD), BF16)
    return pl.pallas_call(
        body, name="ffn_bwd_dx", grid=(s_len // tm, FF_STEPS),
        in_specs=[row, hid, hid,
                  pl.BlockSpec((2, FF_PAD, D_MODEL), lambda i, j: (j, 0, 0)),
                  pl.BlockSpec((2, FF_PAD, D_MODEL), lambda i, j: (j, 0, 0)),
                  pl.BlockSpec((2, FF_PAD, D_MODEL), lambda i, j: (j, 0, 0))],
        out_specs=(hid, hid, hid, row),
        out_shape=(hid_bf, hid_bf, hid_bf, jax.ShapeDtypeStruct((s_len, D_MODEL), F32)),
        scratch_shapes=[pltpu.VMEM((tm, D_MODEL), F32)],
        compiler_params=_params(),
    )(dy, g, u, gate_g, up_g, down_g)


def _ffn_bwd_dw(h2, dy, dg, du, act):
    s_len = h2.shape[0]
    tm = DW_ROW_TILE
    ni = s_len // tm

    half = FF_PAD // 2

    def body(h_ref, dy_ref, dg_ref, du_ref, act_ref, dwg_ref, dwu_ref, dwd_lo_ref, dwd_hi_ref, ag, au, ad):
        i = pl.program_id(1)

        @pl.when(i == 0)
        def _():
            ag[...] = jnp.zeros_like(ag)
            au[...] = jnp.zeros_like(au)
            ad[...] = jnp.zeros_like(ad)

        h = h_ref[...]
        ag[...] += _dot_tn(dg_ref[...], h)
        au[...] += _dot_tn(du_ref[...], h)
        ad[...] += _dot_tn(act_ref[...], dy_ref[...].astype(BF16))

        @pl.when(i == ni - 1)
        def _():
            for acc_ref, out_ref in ((ag, dwg_ref), (au, dwu_ref)):
                out_ref[...] = acc_ref[...].astype(BF16).reshape(2, FF_PAD, D_MODEL)
            for dev in range(2):
                dwd_lo_ref[dev] = ad[FF_PAD * dev:FF_PAD * dev + half, :].astype(BF16)
                dwd_hi_ref[dev] = ad[FF_PAD * dev + half:FF_PAD * (dev + 1), :].astype(BF16)

    row = pl.BlockSpec((tm, D_MODEL), lambda j, i: (i, 0))
    hid = pl.BlockSpec((tm, FF_BLOCK), lambda j, i: (i, j))
    row_w = pl.BlockSpec((2, FF_PAD, D_MODEL), lambda j, i: (j, 0, 0))
    half_w = pl.BlockSpec((2, half, D_MODEL), lambda j, i: (j, 0, 0))
    grad = jax.ShapeDtypeStruct((N_DEV, FF_PAD, D_MODEL), BF16)
    half_grad = jax.ShapeDtypeStruct((N_DEV, half, D_MODEL), BF16)
    return pl.pallas_call(
        body, name="ffn_bwd_dw", grid=(FF_STEPS, ni),
        in_specs=[row, row, hid, hid, hid], out_specs=(row_w, row_w, half_w, half_w),
        out_shape=(grad, grad, half_grad, half_grad),
        scratch_shapes=[pltpu.VMEM((FF_BLOCK, D_MODEL), F32)] * 3,
        compiler_params=_params(),
    )(h2, dy, dg, du, act)


def _rms_bwd(dy, t, w):
    r = lax.rsqrt(jnp.mean(t * t, axis=-1, keepdims=True) + EPS)
    gw = dy * w
    dt = r * (gw - t * ((r * r) * jnp.mean(gw * t, axis=-1, keepdims=True)))
    return dt, dy * t * r


def _attn_out_bwd(dy, dh2, x1, wn2, b_g, mixed, o_dil, o_sb, wdil, wsb, bd512):
    s_len = dy.shape[0]
    tm = ROW_TILE
    ni = s_len // tm

    def body(dy_ref, dh_ref, x1_ref, wn_ref, w_ref, mixed_ref, odil_ref, osb_ref, wdil_ref, wsb_ref, bd_ref,
             dx1_ref, dodil_ref, delta_ref, dosb_ref, dwout_ref, dwn_ref, dwdil_ref, dwsb_ref,
             do4_ref, dl4_ref, do16_ref, dl16_ref, wacc, both, stage):
        i = pl.program_id(0)

        @pl.when(i == 0)
        def _():
            wacc[...] = jnp.zeros_like(wacc)
            dwn_ref[...] = jnp.zeros_like(dwn_ref)
            dwdil_ref[...] = jnp.zeros_like(dwdil_ref)
            dwsb_ref[...] = jnp.zeros_like(dwsb_ref)

        dnorm, dw_rows = _rms_bwd(dh_ref[...], x1_ref[...], wn_ref[...])
        dx1 = dy_ref[...] + dnorm
        dx1_ref[...] = dx1
        dwn_ref[...] += jnp.sum(dw_rows, axis=0, keepdims=True)
        dx1b = dx1.astype(BF16)
        w = w_ref[...].reshape(D_MODEL, D_MODEL)
        dmixed = _dot_nt(dx1b, w)
        wacc[...] += _dot_tn(mixed_ref[...], dx1b)
        o_dil = odil_ref[...]
        d_odil, dw_rows = _rms_bwd(dmixed[:, :D_GRP], o_dil, wdil_ref[...])
        dwdil_ref[...] += jnp.sum(dw_rows, axis=0, keepdims=True)
        dodil_ref[...] = d_odil.astype(BF16)
        delta = _mm_split(d_odil * o_dil, bd_ref[...])
        delta_ref[...] = delta
        for p in range(4):
            both[p] = d_odil[:, 128 * p:128 * (p + 1)]
            both[4 + p] = delta[:, 128 * p:128 * (p + 1)]
        _split_views(both, stage, (do4_ref, dl4_ref), (do16_ref, dl16_ref))
        d_osb, dw_rows = _rms_bwd(dmixed[:, D_GRP:], osb_ref[...], wsb_ref[...])
        dwsb_ref[...] += jnp.sum(dw_rows, axis=0, keepdims=True)
        dosb_ref[...] = d_osb.astype(BF16)

        @pl.when(i == ni - 1)
        def _():
            dwout_ref[...] = wacc[...].astype(BF16).reshape(N_DEV, OUT_SHARD, D_MODEL)

    row = lambda w: pl.BlockSpec((tm, w), lambda i: (i, 0))
    return pl.pallas_call(
        body, name="attn_out_bwd", grid=(ni,),
        in_specs=[row(D_MODEL), row(D_MODEL), row(D_MODEL), _full((1, D_MODEL)),
                  _full((N_DEV, OUT_SHARD, D_MODEL)),
                  row(D_MODEL), row(D_GRP), row(D_GRP), _full((1, D_GRP)), _full((1, D_GRP)),
                  _full((D_GRP, D_GRP))],
        out_specs=(row(D_MODEL), row(D_GRP), row(D_GRP), row(D_GRP),
                   _full((N_DEV, OUT_SHARD, D_MODEL)), _full((1, D_MODEL)), _full((1, D_GRP)), _full((1, D_GRP)),
                   _view_spec(tm, 4), _view_spec(tm, 4), _view_spec(tm, 16), _view_spec(tm, 16)),
        out_shape=(jax.ShapeDtypeStruct((s_len, D_MODEL), F32), jax.ShapeDtypeStruct((s_len, D_GRP), BF16),
                   jax.ShapeDtypeStruct((s_len, D_GRP), F32), jax.ShapeDtypeStruct((s_len, D_GRP), BF16),
                   jax.ShapeDtypeStruct((N_DEV, OUT_SHARD, D_MODEL), BF16),
                   jax.ShapeDtypeStruct((1, D_MODEL), F32), jax.ShapeDtypeStruct((1, D_GRP), F32),
                   jax.ShapeDtypeStruct((1, D_GRP), F32),
                   _view_shape(s_len, 4, BF16), _view_shape(s_len, 4, F32),
                   _view_shape(s_len, 16, BF16), _view_shape(s_len, 16, F32)),
        scratch_shapes=[pltpu.VMEM((D_MODEL, D_MODEL), F32), pltpu.VMEM((8, tm, 128), F32),
                        pltpu.VMEM((8, tm, 128), F32)],
        compiler_params=_params(),
    )(dy, dh2, x1, wn2, b_g, mixed, o_dil, o_sb, wdil, wsb, bd512)


def _qkv_bwd(dq_b, dk_b, dv_b, dqs, dks, dvs, qraw, kraw, cos2, sin2, qnw, knw, bd, rides):
    s_len = qraw.shape[0]
    tm = ROW_TILE
    ni = s_len // tm

    def body(dq1, dk1, dv1, dq4, dk4, dv4, dq16, dk16, dv16, dqs_ref, dks_ref, dvs_ref,
             qraw_ref, kraw_ref, cos_ref, sin_ref, qnw_ref, knw_ref, bd_ref,
             dproj_ref, dqn_ref, dkn_ref, stage, nat4, nat16):
        i = pl.program_id(0)

        @pl.when(i == 0)
        def _():
            dqn_ref[...] = jnp.zeros_like(dqn_ref)
            dkn_ref[...] = jnp.zeros_like(dkn_ref)

        _merge_views((dq4, dk4, dv4), (dq16, dk16, dv16), stage, nat4, nat16)
        cos_t, sin_t, bdm = cos_ref[...], sin_ref[...], bd_ref[...]
        for grp, (part1, raw_ref, nw_ref, dn_ref) in enumerate(((dq1, qraw_ref, qnw_ref, dqn_ref),
                                                                (dk1, kraw_ref, knw_ref, dkn_ref))):
            dn_acc = 0.0
            for p in range(4):
                cols = slice(128 * p, 128 * (p + 1))
                d_rope = part1[:, cols] + nat4[4 * grp + p] + nat16[4 * grp + p]
                d_norm = d_rope * cos_t + _swap_halves(d_rope * sin_t)
                t = raw_ref[:, cols]
                w = nw_ref[...]
                r = lax.rsqrt(_mm_split(t * t, bdm) * (1.0 / HEAD_DIM) + EPS)
                gw = d_norm * w
                corr = _mm_split(gw * t, bdm) * (1.0 / HEAD_DIM)
                dt = r * (gw - t * ((r * r) * corr))
                dn_acc = dn_acc + jnp.sum(d_norm * t * r, axis=0, keepdims=True)
                dproj_ref[:, D_GRP * grp + 128 * p:D_GRP * grp + 128 * (p + 1)] = dt.astype(BF16)
            dn_ref[...] += dn_acc
        dproj_ref[:, 2 * D_GRP:3 * D_GRP] = (dv1[...] + _slab_group(nat4, 2) + _slab_group(nat16, 2)).astype(BF16)
        dproj_ref[:, 3 * D_GRP:4 * D_GRP] = dqs_ref[...].astype(BF16)
        dproj_ref[:, 4 * D_GRP:5 * D_GRP] = dks_ref[...].astype(BF16)
        dproj_ref[:, 5 * D_GRP:6 * D_GRP] = dvs_ref[...].astype(BF16)

    row = lambda w: pl.BlockSpec((tm, w), lambda i: (i, 0))
    return _call_with_exchange(
        body, rides, lambda: pl.program_id(0) == 0, lambda: pl.program_id(0) == ni - 1,
        name="qkv_bwd", grid=(ni,),
        in_specs=[row(D_GRP)] * 3 + [_view_spec(tm, 4)] * 3 + [_view_spec(tm, 16)] * 3 + [row(D_GRP)] * 5
        + [row(128), row(128), _full((1, 128)), _full((1, 128)), _full((128, 128))],
        out_specs=(row(D_IN), _full((1, 128)), _full((1, 128))),
        out_shape=(jax.ShapeDtypeStruct((s_len, D_IN), BF16), jax.ShapeDtypeStruct((1, 128), F32),
                   jax.ShapeDtypeStruct((1, 128), F32)),
        scratch_shapes=[pltpu.VMEM((12, tm, 128), F32)] * 3,
    )(dq_b[0], dk_b[0], dv_b[0], dq_b[1], dk_b[1], dv_b[1], dq_b[2], dk_b[2], dv_b[2],
      dqs, dks, dvs, qraw, kraw, cos2, sin2, qnw, knw, bd, *rides)


def _in_bwd_dx(dproj, a_g, x2, dx1, wn1, rides):
    s_len = x2.shape[0]
    tm = ROW_TILE
    ni = s_len // tm

    def body(dp_ref, w_ref, x_ref, dx1_ref, wn_ref, gx_ref, dwn_ref, w_full):
        i = pl.program_id(0)

        @pl.when(i == 0)
        def _():
            dwn_ref[...] = jnp.zeros_like(dwn_ref)
            for d in range(N_DEV):
                w_full[:, IN_SHARD * d:IN_SHARD * (d + 1)] = w_ref[d]

        dh = _dot_nt(dp_ref[...], w_full[...])
        dnorm, dw_rows = _rms_bwd(dh, x_ref[...], wn_ref[...])
        gx_ref[0] = dx1_ref[...] + dnorm
        dwn_ref[...] += jnp.sum(dw_rows, axis=0, keepdims=True)

    row = lambda w: pl.BlockSpec((tm, w), lambda i: (i, 0))
    gx_spec = pl.BlockSpec((1, tm, D_MODEL), lambda i: (0, i, 0))
    return _call_with_exchange(
        body, rides, lambda: pl.program_id(0) == 0, lambda: pl.program_id(0) == ni - 1,
        name="in_bwd_dx", grid=(ni,),
        in_specs=[row(D_IN), pl.BlockSpec((N_DEV, D_MODEL, IN_SHARD), lambda i: (0, 0, 0)),
                  row(D_MODEL), row(D_MODEL), _full((1, D_MODEL))],
        out_specs=(gx_spec, _full((1, D_MODEL))),
        out_shape=(jax.ShapeDtypeStruct((1, s_len, D_MODEL), F32), jax.ShapeDtypeStruct((1, D_MODEL), F32)),
        scratch_shapes=[pltpu.VMEM((D_MODEL, D_IN), BF16)],
    )(dproj, a_g, x2, dx1, wn1, *rides)


def _in_bwd_dw(h1, dproj):
    s_len = h1.shape[0]
    tm = DW_ROW_TILE
    ni = s_len // tm

    half_d = D_MODEL // 2

    def body(h_ref, dp_ref, lo_ref, hi_ref, acc):
        i = pl.program_id(1)

        @pl.when(i == 0)
        def _():
            acc[...] = jnp.zeros_like(acc)

        acc[...] += _dot_tn(h_ref[...], dp_ref[...])

        @pl.when(i == ni - 1)
        def _():
            for dev in range(2):
                cols = slice(IN_SHARD * dev, IN_SHARD * (dev + 1))
                lo_ref[dev] = acc[0:half_d, cols].astype(BF16)
                hi_ref[dev] = acc[half_d:D_MODEL, cols].astype(BF16)

    half_w = pl.BlockSpec((2, half_d, IN_SHARD), lambda d, i: (d, 0, 0))
    half_grad = jax.ShapeDtypeStruct((N_DEV, half_d, IN_SHARD), BF16)
    return pl.pallas_call(
        body, name="in_bwd_dw", grid=(N_DEV // 2, ni),
        in_specs=[pl.BlockSpec((tm, D_MODEL), lambda d, i: (i, 0)),
                  pl.BlockSpec((tm, 2 * IN_SHARD), lambda d, i: (i, d))],
        out_specs=(half_w, half_w),
        out_shape=(half_grad, half_grad),
        scratch_shapes=[pltpu.VMEM((D_MODEL, 2 * IN_SHARD), F32)],
        compiler_params=_params(),
    )(h1, dproj)


def _adamw(recv, w, m, v, recv_hi=None):
    rows, cols = w.shape
    tr = next((t for t in (128, 32) if rows % t == 0), rows)
    recvs = [recv] if recv_hi is None else [recv, recv_hi]
    lo_tiles = recv.shape[1] // tr

    def body(*refs):
        p_refs = refs[:len(recvs)]
        w_ref, m_ref, v_ref, g_ref, d_ref, nm_ref, nv_ref = refs[len(recvs):]

        def slot(s):
            if len(p_refs) == 1:
                return p_refs[0][s].astype(F32)
            return jnp.where(pl.program_id(0) < lo_tiles, p_refs[0][s], p_refs[1][s]).astype(F32)

        g = slot(0)
        for s in range(1, N_DEV):
            g = g + slot(s)
        m_new = ADAM_B1 * m_ref[...] + (1.0 - ADAM_B1) * g
        v_new = ADAM_B2 * v_ref[...] + (1.0 - ADAM_B2) * (g * g)
        m_hat = m_new / (1.0 - ADAM_B1 ** ADAM_STEP)
        v_hat = v_new / (1.0 - ADAM_B2 ** ADAM_STEP)
        g_ref[...] = g
        d_ref[...] = -ADAM_LR * (m_hat / (jnp.sqrt(v_hat) + ADAM_EPS) + ADAM_WD * w_ref[...])
        nm_ref[...] = m_new
        nv_ref[...] = v_new

    blk = pl.BlockSpec((tr, cols), lambda i: (i, 0))
    out = jax.ShapeDtypeStruct((rows, cols), F32)
    return pl.pallas_call(
        body, name=f"adamw_{rows}x{cols}", grid=(rows // tr,),
        in_specs=([pl.BlockSpec((N_DEV, tr, cols), lambda i: (0, i, 0))] if recv_hi is None else
                  [pl.BlockSpec((N_DEV, tr, cols), lambda i: (0, jnp.minimum(i, lo_tiles - 1), 0)),
                   pl.BlockSpec((N_DEV, tr, cols), lambda i: (0, jnp.maximum(i - lo_tiles, 0), 0))])
        + [blk, blk, blk],
        out_specs=(blk,) * 4, out_shape=(out,) * 4,
        compiler_params=_params(),
    )(*recvs, w, m, v)


def _adamw_many(items, rides):
    tr = 32
    tiles = [w.shape[0] // tr for _, w, _, _ in items]
    starts = [sum(tiles[:k]) for k in range(len(items))]
    total = sum(tiles)
    n_items = len(items)

    def body(*refs):
        i = pl.program_id(0)
        in_refs, out_refs = refs[:4 * n_items], refs[4 * n_items:]
        for k in range(n_items):
            def update(k=k):
                p_ref, w_ref, m_ref, v_ref = in_refs[4 * k:4 * k + 4]
                g_ref, d_ref, nm_ref, nv_ref = out_refs[4 * k:4 * k + 4]
                g = p_ref[0].astype(F32)
                for s in range(1, N_DEV):
                    g = g + p_ref[s].astype(F32)
                m_new = ADAM_B1 * m_ref[...] + (1.0 - ADAM_B1) * g
                v_new = ADAM_B2 * v_ref[...] + (1.0 - ADAM_B2) * (g * g)
                m_hat = m_new / (1.0 - ADAM_B1 ** ADAM_STEP)
                v_hat = v_new / (1.0 - ADAM_B2 ** ADAM_STEP)
                g_ref[...] = g
                d_ref[...] = -ADAM_LR * (m_hat / (jnp.sqrt(v_hat) + ADAM_EPS) + ADAM_WD * w_ref[...])
                nm_ref[...] = m_new
                nv_ref[...] = v_new

            pl.when(jnp.logical_and(i >= starts[k], i < starts[k] + tiles[k]))(update)

    in_specs, out_specs, out_shape, args = [], [], [], []
    for k, (recv, w, m, v) in enumerate(items):
        tile_of = functools.partial(lambda i, s0, nk: jnp.clip(i - s0, 0, nk - 1), s0=starts[k], nk=tiles[k])
        blk = pl.BlockSpec((tr, D_MODEL), functools.partial(lambda i, t: (t(i), 0), t=tile_of))
        in_specs += [pl.BlockSpec((N_DEV, tr, D_MODEL), functools.partial(lambda i, t: (0, t(i), 0), t=tile_of)),
                     blk, blk, blk]
        out_specs += [blk] * 4
        out_shape += [jax.ShapeDtypeStruct(w.shape, F32)] * 4
        args += [recv, w, m, v]
    res = _call_with_exchange(
        body, rides, lambda: pl.program_id(0) == 0, lambda: pl.program_id(0) == total - 1,
        name="adamw_many", grid=(total,), in_specs=in_specs, out_specs=out_specs, out_shape=out_shape,
    )(*args, *rides)
    return [tuple(res[4 * k:4 * k + 4]) for k in range(n_items)], list(res[4 * n_items:])


def _rope_tables(s_len):
    pos = jnp.arange(s_len, dtype=F32)
    inv_freq = ROPE_THETA ** (-jnp.arange(0, HEAD_DIM, 2, dtype=F32) / HEAD_DIM)
    ang = pos[:, None] * inv_freq[None, :]
    cos, sin = jnp.cos(ang), jnp.sin(ang)
    cos2 = jnp.concatenate([cos, cos, cos, cos], axis=1)
    sin2 = jnp.concatenate([-sin, sin, -sin, sin], axis=1)
    return cos2, sin2


def _block_diag_ones(n):
    i = jnp.arange(n)
    return (i[:, None] // HEAD_DIM == i[None, :] // HEAD_DIM).astype(BF16)


def _pad_cols(t):
    return jnp.pad(t, ((0, 0), (0, FF_PAD - FF_SHARD)))


def _pad_rows(t):
    return jnp.pad(t, ((0, FF_PAD - FF_SHARD), (0, 0)))


LOSS_ROW = 26


def _pack_small(n1, n2, ndil, nsb, nq, nk, scalar=None):
    pad = lambda t: jnp.pad(t.reshape(1, -1), ((0, 0), (0, 128 - t.size)))
    last = jnp.zeros((1, 128), F32) if scalar is None else pad(scalar)
    rows = [n1.reshape(8, 128), n2.reshape(8, 128), ndil.reshape(4, 128), nsb.reshape(4, 128),
            pad(nq), pad(nk), last, jnp.zeros((5, 128), F32)]
    return jnp.concatenate(rows, axis=0)


def _unpack_small(t):
    return (t[0:8].reshape(1, D_MODEL), t[8:16].reshape(1, D_MODEL), t[16:20].reshape(1, D_GRP),
            t[20:24].reshape(1, D_GRP), t[24:25, :HEAD_DIM], t[25:26, :HEAD_DIM])


def kernel(x, attn_norm_w, w_in, q_norm_w, k_norm_w, dil_out_norm_w, sb_out_norm_w, w_out, ffn_norm_w, w_gate, w_up, w_down, loss_target, m_attn_norm_w, m_w_in, m_q_norm_w, m_k_norm_w, m_dil_out_norm_w, m_sb_out_norm_w, m_w_out, m_ffn_norm_w, m_w_gate, m_w_up, m_w_down, v_attn_norm_w, v_w_in, v_q_norm_w, v_k_norm_w, v_dil_out_norm_w, v_sb_out_norm_w, v_w_out, v_ffn_norm_w, v_w_gate, v_w_up, v_w_down):
    s_len = x.shape[1]
    x2, tgt = x[0], loss_target[0]

    (a_g,) = _gather_weights([w_in[0].astype(BF16)])
    gate_loc = _pad_cols(w_gate[0]).T.astype(BF16)
    up_loc = _pad_cols(w_up[0]).T.astype(BF16)
    down_loc = _pad_rows(w_down[0]).astype(BF16)
    out_loc = w_out[0].astype(BF16)

    cos2, sin2 = _rope_tables(s_len)
    bd128, bd512 = _block_diag_ones(128), _block_diag_ones(D_GRP)
    idx = jnp.arange(SB_TILE)
    tri_suf = (idx[:, None] > idx[None, :]).astype(BF16)
    tri_pre = (idx[:, None] < idx[None, :]).astype(BF16)
    qnw2 = jnp.concatenate([q_norm_w, q_norm_w], axis=1)
    knw2 = jnp.concatenate([k_norm_w, k_norm_w], axis=1)

    (h1, qraw, kraw, q, k, va, qs, ks, vs, q4, k4, v4, q16, k16, v16,
     gate_g) = _attn_in(x2, attn_norm_w, a_g, cos2, sin2, qnw2, knw2, bd128, shards=[gate_loc])
    qkv_views = {1: (q, k, va), 4: (q4, k4, v4), 16: (q16, k16, v16)}
    fwd_riders = {1: [out_loc], 4: [], 16: []}
    o_b, lse_b, gathered = [], [], {}
    for r in DILATIONS:
        o, lse, *gathered[r] = _dil_fwd(*qkv_views[r], r, shards=fwd_riders[r])
        o_b.append(o)
        lse_b.append(lse)
    (out_g,) = gathered[1]
    o_sb, c_sb, up_g = _sb_fwd(qs, ks, vs, tri_suf, shards=[up_loc])
    o_dil, lse_tot, lse4, lse16, mixed, x1, down_g = _attn_out(
        o_b, lse_b, o_sb, x2, dil_out_norm_w, sb_out_norm_w, out_g, shards=[down_loc])
    g, u, h2, dy, loss_parts = _ffn_fwd(x1, ffn_norm_w, tgt, gate_g, up_g, down_g)
    loss_local = jnp.sum(loss_parts[::8, 0])

    dg, du, act, dh2 = _ffn_bwd_dx(dy, g, u, gate_g, up_g, down_g)
    (dx1, do_dil, delta, do_sb, dwout, dn2, dndil, dnsb, do4, dl4, do16, dl16) = _attn_out_bwd(
        dy, dh2, x1, ffn_norm_w, out_g, mixed, o_dil, o_sb, dil_out_norm_w, sb_out_norm_w, bd512)
    dwg, dwu, dwd_lo, dwd_hi = _ffn_bwd_dw(h2, dy, dg, du, act)
    dqs, dks, dvs, r_gate = _sb_bwd(qs, ks, vs, do_sb, c_sb, tri_suf, tri_pre, rides=[dwg])
    cot_views = {1: (do_dil, lse_tot, delta), 4: (do4, lse4, dl4), 16: (do16, lse16, dl16)}
    riders = {1: [dwd_lo], 4: [dwd_hi], 16: [dwu]}
    dq_b, dk_b, dv_b, landed = [], [], [], {}
    for r in DILATIONS:
        dq, dk, dv, *landed[r] = _dil_bwd(*qkv_views[r], *cot_views[r], r, rides=riders[r])
        dq_b.append(dq)
        dk_b.append(dk)
        dv_b.append(dv)
    (r_down_lo,), (r_down_hi,), (r_up,) = landed[1], landed[4], landed[16]
    dproj, dqn2, dkn2, r_out = _qkv_bwd(dq_b, dk_b, dv_b, dqs, dks, dvs, qraw, kraw, cos2, sin2, qnw2, knw2,
                                        bd128, rides=[dwout])
    dwin_lo, dwin_hi = _in_bwd_dw(h1, dproj)
    grad_x, dn1, r_in_lo = _in_bwd_dx(dproj, a_g, x2, dx1, attn_norm_w, rides=[dwin_lo])
    dqn = dqn2[:, :HEAD_DIM] + dqn2[:, HEAD_DIM:]
    dkn = dkn2[:, :HEAD_DIM] + dkn2[:, HEAD_DIM:]

    small = _pack_small(dn1, dn2, dndil, dnsb, dqn, dkn, loss_local)
    (r_small,) = _exchange_grads([], small)
    (new_gate, new_up, new_out), (r_in_hi,) = _adamw_many(
        [(r_gate, w_gate[0].T, m_w_gate[0].T, v_w_gate[0].T), (r_up, w_up[0].T, m_w_up[0].T, v_w_up[0].T),
         (r_out, w_out[0], m_w_out[0], v_w_out[0])], rides=[dwin_hi])
    big = {
        "w_in": _adamw(r_in_lo, w_in[0], m_w_in[0], v_w_in[0], recv_hi=r_in_hi),
        "w_gate": tuple(t.T for t in new_gate),
        "w_up": tuple(t.T for t in new_up),
        "w_down": _adamw(r_down_lo, w_down[0], m_w_down[0], v_w_down[0], recv_hi=r_down_hi),
        "w_out": new_out,
    }
    packs = [_pack_small(*ts) for ts in (
        (attn_norm_w, ffn_norm_w, dil_out_norm_w, sb_out_norm_w, q_norm_w, k_norm_w),
        (m_attn_norm_w, m_ffn_norm_w, m_dil_out_norm_w, m_sb_out_norm_w, m_q_norm_w, m_k_norm_w),
        (v_attn_norm_w, v_ffn_norm_w, v_dil_out_norm_w, v_sb_out_norm_w, v_q_norm_w, v_k_norm_w))]
    small_raw = _adamw(r_small, *packs)
    loss = small_raw[0][LOSS_ROW, 0]
    small_out = [_unpack_small(t) for t in small_raw]
    names = ["attn_norm_w", "w_in", "q_norm_w", "k_norm_w", "dil_out_norm_w", "sb_out_norm_w", "w_out",
             "ffn_norm_w", "w_gate", "w_up", "w_down"]
    small_pos = {"attn_norm_w": 0, "ffn_norm_w": 1, "dil_out_norm_w": 2, "sb_out_norm_w": 3,
                 "q_norm_w": 4, "k_norm_w": 5}
    outs = [loss, grad_x]
    for kind in range(4):
        for name in names:
            if name in small_pos:
                outs.append(small_out[kind][small_pos[name]])
            else:
                outs.append(big[name][kind][None])
    return tuple(outs)
```

```python
import functools

import jax
import jax.numpy as jnp
import numpy as np
from jax import lax
from jax.experimental import pallas as pl
from jax.experimental.pallas import tpu as pltpu

F32 = jnp.float32
BF16 = jnp.bfloat16

N_DEV = 8
D_MODEL = 1024
HEAD_DIM = 64
D_GRP = 512
D_IN = 6 * D_GRP
IN_SHARD = D_IN // N_DEV
FF_SHARD = 352
FF_PAD = 384
FF_BLOCK = 2 * FF_PAD
FF_STEPS = N_DEV // 2
OUT_SHARD = D_MODEL // N_DEV
BLOCK = 128
DILATIONS = (1, 4, 16)
ROPE_THETA = 10000.0
EPS = 1e-6
ATT_SCALE = HEAD_DIM ** -0.5
NEG = -1e30

ADAM_LR = 0.001
ADAM_B1 = 0.9
ADAM_B2 = 0.999
ADAM_EPS = 1e-08
ADAM_WD = 0.01
ADAM_STEP = 10

SB_TILE = 256
SB_DEAD = -104.0
SB_PAIRS = 4
SB_BWD_PAIRS = 2
ROW_TILE = 512
DW_ROW_TILE = 1024
VMEM_LIMIT = 56 * 1024 * 1024
MESH = pl.DeviceIdType.MESH


def _dot(a, b):
    return jnp.dot(a, b, preferred_element_type=F32)


def _dot_nt(a, b):
    return lax.dot_general(a, b, (((1,), (1,)), ((), ())), preferred_element_type=F32)


def _dot_tn(a, b):
    return lax.dot_general(a, b, (((0,), (0,)), ((), ())), preferred_element_type=F32)


def _mm_split(t, m):
    hi = t.astype(BF16)
    lo = (t - hi.astype(F32)).astype(BF16)
    return _dot(hi, m) + _dot(lo, m)


def _params(**kw):
    return pltpu.CompilerParams(vmem_limit_bytes=VMEM_LIMIT, **kw)


def _full(shape):
    nd = len(shape)
    return pl.BlockSpec(shape, lambda *_: (0,) * nd)


def _view_shape(s_len, r, dtype):
    return jax.ShapeDtypeStruct((s_len // r, r * D_GRP), dtype)


def _view_spec(tm, r):
    return pl.BlockSpec((tm // r, r * D_GRP), lambda i: (i, 0))


def _swap_halves(t):
    lane = lax.broadcasted_iota(jnp.int32, t.shape, 1)
    first = (lane & 32) == 0
    return jnp.where(first, pltpu.roll(t, 96, 1), pltpu.roll(t, 32, 1))


def _log_sigmoid_pair(z):
    neg_abs = lax.bitcast_convert_type(lax.bitcast_convert_type(z, jnp.uint32) | jnp.uint32(0x80000000), F32)
    lb = jnp.minimum(z, 0.0) - jnp.log(1.0 + jnp.exp(neg_abs))
    return lb, lb - z


def _cumsum_mm(t, tri):
    return _dot(t.astype(BF16), tri)


def _split_views(src_ref, stage_ref, views4, views16):
    slabs, n, _ = src_ref.shape
    n4, n16 = n // 4, n // 16
    for j in range(slabs):
        g, lanes = j // 4, 128 * (j % 4)
        src, stage = src_ref.at[j], stage_ref.at[j]
        for c4 in range(4):
            blk = src[pl.ds(c4, n4, stride=4), :]
            stage[n4 * c4:n4 * (c4 + 1), :] = blk
            col = D_GRP * c4 + lanes
            views4[g][:, col:col + 128] = blk.astype(views4[g].dtype)
        for c4 in range(4):
            for c1 in range(4):
                blk = stage[pl.ds(n4 * c4 + c1, n16, stride=4), :]
                col = D_GRP * (4 * c1 + c4) + lanes
                views16[g][:, col:col + 128] = blk.astype(views16[g].dtype)


def _merge_views(views4, views16, stage_ref, dst4_ref, dst16_ref):
    slabs, n, _ = dst4_ref.shape
    n4, n16 = n // 4, n // 16
    for j in range(slabs):
        g, lanes = j // 4, 128 * (j % 4)
        dst4, dst16, stage = dst4_ref.at[j], dst16_ref.at[j], stage_ref.at[j]
        for c4 in range(4):
            col = D_GRP * c4 + lanes
            dst4[pl.ds(c4, n4, stride=4), :] = views4[g][:, col:col + 128].astype(F32)
            for c1 in range(4):
                col = D_GRP * (4 * c1 + c4) + lanes
                stage[pl.ds(n4 * c4 + c1, n16, stride=4), :] = views16[g][:, col:col + 128].astype(F32)
        for c4 in range(4):
            dst16[pl.ds(c4, n4, stride=4), :] = stage[n4 * c4:n4 * (c4 + 1), :]


def _slab_group(ref, g):
    return jnp.concatenate([ref[4 * g + p] for p in range(4)], axis=1)


def _mesh_pos():
    return lax.axis_index("x"), lax.axis_index("y"), lax.axis_index("c")


def _flat_index(p):
    return 4 * p[0] + 2 * p[1] + p[2]


def _gather_weights(shards):
    n_arr = len(shards)

    def body(*refs):
        srcs, outs = refs[:n_arr], refs[n_arr:2 * n_arr]
        send_sems, recv_sems, local_sems = refs[2 * n_arr:]
        x, y, c = _mesh_pos()
        me, sibling = (x, y, c), (x, y, 1 - c)
        chips = [(1 - x, y), (x, 1 - y), (1 - x, 1 - y)]

        def copy(arr, k, block, to, own=False):
            dst = outs[arr].at[_flat_index(block)]
            return pltpu.make_async_remote_copy(
                src_ref=srcs[arr] if own else dst, dst_ref=dst,
                send_sem=send_sems.at[arr, k], recv_sem=recv_sems.at[arr, k],
                device_id=to, device_id_type=MESH)

        for arr in range(n_arr):
            mine = pltpu.make_async_copy(srcs[arr], outs[arr].at[_flat_index(me)], local_sems.at[arr])
            mine.start()
            first = [copy(arr, 0, me, sibling, own=True)]
            first += [copy(arr, 1 + j, me, (*chip, c), own=True) for j, chip in enumerate(chips)]
            for cp in first:
                cp.start()
        for arr in range(n_arr):
            passed = [copy(arr, 4 + j, (*chip, c), sibling) for j, chip in enumerate(chips)]
            for j, chip in enumerate(chips):
                copy(arr, 1 + j, (*chip, c), me).wait_recv()
                passed[j].start()
        for arr in range(n_arr):
            copy(arr, 0, sibling, me).wait_recv()
            for j, chip in enumerate(chips):
                copy(arr, 4 + j, (*chip, 1 - c), me).wait_recv()
            for k in range(7):
                copy(arr, k, me, me).wait_send()
            pltpu.make_async_copy(srcs[arr], outs[arr].at[_flat_index(me)], local_sems.at[arr]).wait()

    any_spec = pl.BlockSpec(memory_space=pl.ANY)
    return pl.pallas_call(
        body, name="gather_weights",
        out_shape=tuple(jax.ShapeDtypeStruct((N_DEV,) + s.shape, s.dtype) for s in shards),
        in_specs=[any_spec] * n_arr, out_specs=(any_spec,) * n_arr,
        scratch_shapes=[pltpu.SemaphoreType.DMA((n_arr, 7)), pltpu.SemaphoreType.DMA((n_arr, 7)),
                        pltpu.SemaphoreType.DMA((n_arr,))],
        compiler_params=pltpu.CompilerParams(has_side_effects=True),
    )(*shards)


def _peer_list(x, y, c):
    return [(1 - x if m & 4 else x, 1 - y if m & 2 else y, 1 - c if m & 1 else c) for m in range(1, N_DEV)]


def _exchange_grads(parts, small):
    n_arr = len(parts)

    def body(*refs):
        ins, outs = refs[:n_arr + 1], refs[n_arr + 1:2 * (n_arr + 1)]
        send_sems, recv_sems, local_sems = refs[2 * (n_arr + 1):]
        x, y, c = _mesh_pos()
        me = (x, y, c)
        my_idx = _flat_index(me)
        peers = []
        for m in range(1, N_DEV):
            peers.append((1 - x if m & 4 else x, 1 - y if m & 2 else y, 1 - c if m & 1 else c))

        def src_block(arr, dev):
            return ins[arr] if arr == n_arr else ins[arr].at[_flat_index(dev)]

        def copy(arr, k):
            return pltpu.make_async_remote_copy(
                src_ref=src_block(arr, peers[k]), dst_ref=outs[arr].at[my_idx],
                send_sem=send_sems.at[arr, k], recv_sem=recv_sems.at[arr, k],
                device_id=peers[k], device_id_type=MESH)

        def local(arr):
            return pltpu.make_async_copy(src_block(arr, me), outs[arr].at[my_idx], local_sems.at[arr])

        for arr in range(n_arr + 1):
            local(arr).start()
            for k in range(N_DEV - 1):
                copy(arr, k).start()
        for arr in range(n_arr + 1):
            for k in range(N_DEV - 1):
                cp = copy(arr, k)
                cp.wait_send()
                cp.wait_recv()
            local(arr).wait()

    any_spec = pl.BlockSpec(memory_space=pl.ANY)
    out_shape = tuple(jax.ShapeDtypeStruct(p.shape, p.dtype) for p in parts)
    out_shape += (jax.ShapeDtypeStruct((N_DEV,) + small.shape, small.dtype),)
    return pl.pallas_call(
        body, name="exchange_grads",
        out_shape=out_shape,
        in_specs=[any_spec] * (n_arr + 1), out_specs=(any_spec,) * (n_arr + 1),
        scratch_shapes=[pltpu.SemaphoreType.DMA((n_arr + 1, N_DEV - 1)),
                        pltpu.SemaphoreType.DMA((n_arr + 1, N_DEV - 1)),
                        pltpu.SemaphoreType.DMA((n_arr + 1,))],
        compiler_params=pltpu.CompilerParams(has_side_effects=True),
    )(*parts, small)


def _call_with_gather(body, shards, first_step, mid_step, last_step, *, name, grid, in_specs, out_specs,
                      out_shape, scratch_shapes=()):
    out_specs = tuple(out_specs) if isinstance(out_specs, (tuple, list)) else (out_specs,)
    out_shape = tuple(out_shape) if isinstance(out_shape, (tuple, list)) else (out_shape,)
    n_in, n_out, n_scr, n = len(in_specs), len(out_specs), len(scratch_shapes), len(shards)
    if n == 0:
        return pl.pallas_call(body, name=name, grid=grid, in_specs=list(in_specs), out_specs=out_specs,
                              out_shape=out_shape, scratch_shapes=list(scratch_shapes),
                              compiler_params=_params())

    def full_body(*refs):
        ins, srcs = refs[:n_in], refs[n_in:n_in + n]
        outs, lands = refs[n_in + n:n_in + n + n_out], refs[n_in + n + n_out:n_in + 2 * n + n_out]
        scratch = refs[n_in + 2 * n + n_out:n_in + 2 * n + n_out + n_scr]
        send_sems, recv_sems, local_sems = refs[-3:]
        x, y, c = _mesh_pos()
        me, sibling = (x, y, c), (x, y, 1 - c)
        chips = [(1 - x, y), (x, 1 - y), (1 - x, 1 - y)]

        def copy(a, k, block, to, own=False):
            dst = lands[a].at[_flat_index(block)]
            return pltpu.make_async_remote_copy(
                src_ref=srcs[a] if own else dst, dst_ref=dst,
                send_sem=send_sems.at[a, k], recv_sem=recv_sems.at[a, k],
                device_id=to, device_id_type=MESH)

        def local(a):
            return pltpu.make_async_copy(srcs[a], lands[a].at[_flat_index(me)], local_sems.at[a])

        @pl.when(first_step())
        def _():
            for a in range(n):
                local(a).start()
                copy(a, 0, me, sibling, own=True).start()
                for j, chip in enumerate(chips):
                    copy(a, 1 + j, me, (*chip, c), own=True).start()

        @pl.when(mid_step())
        def _():
            for a in range(n):
                for j, chip in enumerate(chips):
                    copy(a, 1 + j, (*chip, c), me).wait_recv()
                    copy(a, 4 + j, (*chip, c), sibling).start()

        body(*ins, *outs, *scratch)

        @pl.when(last_step())
        def _():
            for a in range(n):
                copy(a, 0, sibling, me).wait_recv()
                for j, chip in enumerate(chips):
                    copy(a, 4 + j, (*chip, 1 - c), me).wait_recv()
                for k in range(N_DEV - 1):
                    copy(a, k, me, me).wait_send()
                local(a).wait()

    any_spec = pl.BlockSpec(memory_space=pl.ANY)
    return pl.pallas_call(
        full_body, name=name, grid=grid,
        in_specs=list(in_specs) + [any_spec] * n,
        out_specs=out_specs + (any_spec,) * n,
        out_shape=out_shape + tuple(jax.ShapeDtypeStruct((N_DEV,) + t.shape, t.dtype) for t in shards),
        scratch_shapes=list(scratch_shapes) + [pltpu.SemaphoreType.DMA((n, N_DEV - 1)),
                                               pltpu.SemaphoreType.DMA((n, N_DEV - 1)),
                                               pltpu.SemaphoreType.DMA((n,))],
        compiler_params=_params(has_side_effects=True),
    )


def _call_with_exchange(body, rides, first_step, last_step, *, name, grid, in_specs, out_specs, out_shape,
                        scratch_shapes=()):
    out_specs = tuple(out_specs) if isinstance(out_specs, (tuple, list)) else (out_specs,)
    out_shape = tuple(out_shape) if isinstance(out_shape, (tuple, list)) else (out_shape,)
    n_in, n_out, n_scr, n = len(in_specs), len(out_specs), len(scratch_shapes), len(rides)
    if n == 0:
        return pl.pallas_call(body, name=name, grid=grid, in_specs=list(in_specs), out_specs=out_specs,
                              out_shape=out_shape, scratch_shapes=list(scratch_shapes),
                              compiler_params=_params())

    def full_body(*refs):
        ins, srcs = refs[:n_in], refs[n_in:n_in + n]
        outs, lands = refs[n_in + n:n_in + n + n_out], refs[n_in + n + n_out:n_in + 2 * n + n_out]
        scratch = refs[n_in + 2 * n + n_out:n_in + 2 * n + n_out + n_scr]
        send_sems, recv_sems, local_sems = refs[-3:]
        x, y, c = _mesh_pos()
        my_idx = _flat_index((x, y, c))
        peers = _peer_list(x, y, c)

        def remote(a, k):
            return pltpu.make_async_remote_copy(
                src_ref=srcs[a].at[_flat_index(peers[k])], dst_ref=lands[a].at[my_idx],
                send_sem=send_sems.at[a, k], recv_sem=recv_sems.at[a, k],
                device_id=peers[k], device_id_type=MESH)

        def local(a):
            return pltpu.make_async_copy(srcs[a].at[my_idx], lands[a].at[my_idx], local_sems.at[a])

        @pl.when(first_step())
        def _():
            for a in range(n):
                local(a).start()
                for k in range(N_DEV - 1):
                    remote(a, k).start()

        body(*ins, *outs, *scratch)

        @pl.when(last_step())
        def _():
            for a in range(n):
                for k in range(N_DEV - 1):
                    cp = remote(a, k)
                    cp.wait_send()
                    cp.wait_recv()
                local(a).wait()

    any_spec = pl.BlockSpec(memory_space=pl.ANY)
    res = pl.pallas_call(
        full_body, name=name, grid=grid,
        in_specs=list(in_specs) + [any_spec] * n,
        out_specs=out_specs + (any_spec,) * n,
        out_shape=out_shape + tuple(jax.ShapeDtypeStruct(t.shape, t.dtype) for t in rides),
        scratch_shapes=list(scratch_shapes) + [pltpu.SemaphoreType.DMA((n, N_DEV - 1)),
                                               pltpu.SemaphoreType.DMA((n, N_DEV - 1)),
                                               pltpu.SemaphoreType.DMA((n,))],
        compiler_params=_params(has_side_effects=True),
    )
    return res


def _head_norm(t, w128, bd):
    ms = _mm_split(t * t, bd) * (1.0 / HEAD_DIM)
    r = lax.rsqrt(ms + EPS)
    return (t * r) * w128, r


def _attn_in(x2, wn1, a_g, cos2, sin2, qnw, knw, bd, shards):
    s_len = x2.shape[0]
    tm = ROW_TILE

    def body(x_ref, wn_ref, w_ref, cos_ref, sin_ref, qnw_ref, knw_ref, bd_ref,
             h1_ref, qraw_ref, kraw_ref, q_ref, k_ref, va_ref, qs_ref, ks_ref, vs_ref,
             q4_ref, k4_ref, v4_ref, q16_ref, k16_ref, v16_ref, proj, slabs, stage, w_full):
        @pl.when(pl.program_id(0) == 0)
        def _():
            for d in range(N_DEV):
                w_full[:, IN_SHARD * d:IN_SHARD * (d + 1)] = w_ref[d]

        xx = x_ref[...]
        r = lax.rsqrt(jnp.mean(xx * xx, axis=-1, keepdims=True) + EPS)
        h = ((xx * r) * wn_ref[...]).astype(BF16)
        h1_ref[...] = h
        proj[...] = _dot(h, w_full[...])
        cos_t, sin_t, bdm = cos_ref[...], sin_ref[...], bd_ref[...]
        for grp, (raw_ref, rope_ref, nw_ref) in enumerate(((qraw_ref, q_ref, qnw_ref),
                                                           (kraw_ref, k_ref, knw_ref))):
            for p in range(4):
                cols = slice(D_GRP * grp + 128 * p, D_GRP * grp + 128 * (p + 1))
                t = proj[:, cols]
                raw_ref[:, 128 * p:128 * (p + 1)] = t
                yn, _ = _head_norm(t, nw_ref[...], bdm)
                roped = yn * cos_t + _swap_halves(yn) * sin_t
                slabs[4 * grp + p] = roped
                rope_ref[:, 128 * p:128 * (p + 1)] = roped.astype(BF16)
        for p in range(4):
            slabs[8 + p] = proj[:, 2 * D_GRP + 128 * p:2 * D_GRP + 128 * (p + 1)]
        for grp, ref in ((2, va_ref), (3, qs_ref), (4, ks_ref), (5, vs_ref)):
            ref[...] = proj[:, D_GRP * grp:D_GRP * (grp + 1)].astype(BF16)
        _split_views(slabs, stage, (q4_ref, k4_ref, v4_ref), (q16_ref, k16_ref, v16_ref))

    row = lambda w: pl.BlockSpec((tm, w), lambda i: (i, 0))
    grp_bf = jax.ShapeDtypeStruct((s_len, D_GRP), BF16)
    grp_f32 = jax.ShapeDtypeStruct((s_len, D_GRP), F32)
    ni = s_len // tm
    return _call_with_gather(
        body, shards, lambda: pl.program_id(0) == 0, lambda: pl.program_id(0) == ni - 2,
        lambda: pl.program_id(0) == ni - 1,
        name="attn_in", grid=(ni,),
        in_specs=[row(D_MODEL), _full((1, D_MODEL)),
                  pl.BlockSpec((N_DEV, D_MODEL, IN_SHARD), lambda i: (0, 0, 0)),
                  row(128), row(128), _full((1, 128)), _full((1, 128)), _full((128, 128))],
        out_specs=(row(D_MODEL),) + (row(D_GRP),) * 8 + (_view_spec(tm, 4),) * 3 + (_view_spec(tm, 16),) * 3,
        out_shape=(jax.ShapeDtypeStruct((s_len, D_MODEL), BF16), grp_f32, grp_f32) + (grp_bf,) * 6
        + (_view_shape(s_len, 4, BF16),) * 3 + (_view_shape(s_len, 16, BF16),) * 3,
        scratch_shapes=[pltpu.VMEM((tm, D_IN), F32), pltpu.VMEM((12, tm, 128), F32), pltpu.VMEM((12, tm, 128), F32),
                        pltpu.VMEM((D_MODEL, D_IN), BF16)],
    )(x2, wn1, a_g, cos2, sin2, qnw, knw, bd, *shards)


def _band_mask(n):
    i = lax.broadcasted_iota(jnp.int32, (2 * BLOCK, 2 * BLOCK), 0) & (BLOCK - 1)
    j = lax.broadcasted_iota(jnp.int32, (2 * BLOCK, 2 * BLOCK), 1)
    dist = i + BLOCK - j
    return (dist >= 0) & (dist <= BLOCK) & ((n - 1) * BLOCK + j >= 0)


def _stack_heads(t2, head0):
    return jnp.concatenate([jnp.where(head0, t2, 0), jnp.where(head0, 0, t2)], axis=0)


def _unstack_heads(t, head0):
    return jnp.where(head0, t[0:BLOCK], t[BLOCK:2 * BLOCK])


def _dil_fwd(qv, kv, vv, r, shards):
    sub_len = qv.shape[0]
    nb = sub_len // BLOCK

    qb = 2 if nb % 2 == 0 else 1

    def body(q_ref, kp_ref, kc_ref, vp_ref, vc_ref, o_ref, lse_ref):
        n = pl.program_id(1)
        lane = lax.broadcasted_iota(jnp.int32, (BLOCK, 128), 1)
        head0 = lane < HEAD_DIM
        units = [(b, slice(128 * p, 128 * (p + 1))) for b in range(qb) for p in range(4)]
        valid = [_band_mask(qb * n + b) for b in range(qb)]
        rows = [slice(BLOCK * b, BLOCK * (b + 1)) for b in range(qb)]

        def keys(prev_ref, cur_ref, b, c):
            before = prev_ref[:, c] if b == 0 else cur_ref[rows[b - 1], c]
            return jnp.concatenate([before, cur_ref[rows[b], c]], axis=0)

        qqs = [_stack_heads(q_ref[rows[b], c] * ATT_SCALE, head0) for b, c in units]
        kks = [keys(kp_ref, kc_ref, b, c) for b, c in units]
        vvs = [keys(vp_ref, vc_ref, b, c) for b, c in units]
        ss = [_dot_nt(qq, kk) for qq, kk in zip(qqs, kks)]
        prs, dens, lses = [], [], []
        for (b, _), s in zip(units, ss):
            s = jnp.where(valid[b], s, NEG)
            m = jnp.max(s, axis=-1, keepdims=True)
            pr = jnp.exp(s - m)
            den = jnp.sum(pr, axis=-1, keepdims=True)
            prs.append(pr.astype(BF16))
            dens.append(den)
            lses.append(m + jnp.log(den))
        pvs = [_dot(pr, vv2) for pr, vv2 in zip(prs, vvs)]
        for (b, c), pv, den, lse in zip(units, pvs, dens, lses):
            o_ref[rows[b], c] = _unstack_heads(pv / den, head0)
            lse_ref[rows[b], c] = _unstack_heads(jnp.broadcast_to(lse, (2 * BLOCK, 128)), head0)

    cur = pl.BlockSpec((qb * BLOCK, D_GRP), lambda c, n: (n, c))
    prev = pl.BlockSpec((BLOCK, D_GRP), lambda c, n: (jnp.maximum(qb * n - 1, 0), c))
    out = jax.ShapeDtypeStruct(qv.shape, F32)
    steps = nb // qb

    def at(t):
        return lambda: pl.program_id(0) * steps + pl.program_id(1) == t

    return _call_with_gather(
        body, shards, at(0), at((2 * r * steps) // 3), at(r * steps - 1),
        name=f"dil_fwd_r{r}", grid=(r, steps),
        in_specs=[cur, prev, cur, prev, cur], out_specs=(cur, cur), out_shape=(out, out),
    )(qv, kv, kv, vv, vv, *shards)


def _dil_bwd(qv, kv, vv, dov, lsev, deltav, r, rides):
    sub_len = qv.shape[0]
    nb = sub_len // BLOCK

    def body(q_ref, kp_ref, kc_ref, vp_ref, vc_ref, do_ref, lse_ref, dl_ref,
             dq_ref, dk_ref, dv_ref, dk_carry, dv_carry):
        n = pl.program_id(1)

        @pl.when(n == 0)
        def _():
            dk_carry[...] = jnp.zeros_like(dk_carry)
            dv_carry[...] = jnp.zeros_like(dv_carry)

        @pl.when(n < nb)
        def _():
            valid = _band_mask(n)
            lane = lax.broadcasted_iota(jnp.int32, (BLOCK, 128), 1)
            head0 = lane < HEAD_DIM
            pairs = [slice(128 * p, 128 * (p + 1)) for p in range(4)]
            qqs = [_stack_heads(q_ref[:, c] * ATT_SCALE, head0) for c in pairs]
            dos = [_stack_heads(do_ref[:, c], head0) for c in pairs]
            kks = [jnp.concatenate([kp_ref[:, c], kc_ref[:, c]], axis=0) for c in pairs]
            vvs = [jnp.concatenate([vp_ref[:, c], vc_ref[:, c]], axis=0) for c in pairs]
            ss = [_dot_nt(qq, kk) for qq, kk in zip(qqs, kks)]
            dps = [_dot_nt(do, vv2) for do, vv2 in zip(dos, vvs)]
            def softmax_terms(p):
                stats = []
                for ref in (lse_ref, dl_ref):
                    t2 = ref[:, pairs[p]]
                    stats.append(jnp.concatenate(
                        [jnp.sum(jnp.where(lane == 0, t2, 0.0), axis=-1, keepdims=True),
                         jnp.sum(jnp.where(lane == HEAD_DIM, t2, 0.0), axis=-1, keepdims=True)], axis=0))
                pr = jnp.where(valid, jnp.exp(jnp.minimum(ss[p] - stats[0], 0.0)), 0.0)
                return pr.astype(BF16), (pr * (dps[p] - stats[1])).astype(BF16)

            terms = [softmax_terms(p) for p in range(4)]
            dqs = [_dot(terms[p][1], kks[p]) for p in range(4)]
            dkks = [_dot_tn(terms[p][1], qqs[p]) for p in range(4)]
            dvvs = [_dot_tn(terms[p][0], dos[p]) for p in range(4)]
            for c, dq, dkk, dvv in zip(pairs, dqs, dkks, dvvs):
                dq_ref[:, c] = _unstack_heads(dq, head0) * ATT_SCALE
                dk_ref[:, c] = dk_carry[:, c] + dkk[:BLOCK]
                dv_ref[:, c] = dv_carry[:, c] + dvv[:BLOCK]
                dk_carry[:, c] = dkk[BLOCK:]
                dv_carry[:, c] = dvv[BLOCK:]

        @pl.when(n == nb)
        def _():
            dk_ref[...] = dk_carry[...]
            dv_ref[...] = dv_carry[...]

    last = nb - 1
    cur = pl.BlockSpec((BLOCK, D_GRP), lambda c, n: (jnp.minimum(n, last), c))
    prev = pl.BlockSpec((BLOCK, D_GRP), lambda c, n: (jnp.clip(n - 1, 0, last), c))
    out = jax.ShapeDtypeStruct(qv.shape, F32)
    return _call_with_exchange(
        body, rides,
        lambda: jnp.logical_and(pl.program_id(0) == 0, pl.program_id(1) == 0),
        lambda: jnp.logical_and(pl.program_id(0) == r - 1, pl.program_id(1) == nb),
        name=f"dil_bwd_r{r}", grid=(r, nb + 1),
        in_specs=[cur, prev, cur, prev, cur, cur, cur, cur],
        out_specs=(cur, prev, prev), out_shape=(out, out, out),
        scratch_shapes=[pltpu.VMEM((BLOCK, D_GRP), F32), pltpu.VMEM((BLOCK, D_GRP), F32)],
    )(qv, kv, kv, vv, vv, dov, lsev, deltav, *rides)


def _sb_fwd(qs, ks, vs, tri_suf, shards):
    s_len = qs.shape[0]
    t = SB_TILE
    nq = s_len // t

    npair = SB_PAIRS

    def body(q_ref, k_ref, v_ref, u_ref, o_ref, c_ref, qq, vt, acc, cf, csave):
        row = lax.broadcasted_iota(jnp.int32, (2 * t, t), 0) & (t - 1)
        col = lax.broadcasted_iota(jnp.int32, (2 * t, t), 1)
        diag_mask = col < row
        lane1 = lax.broadcasted_iota(jnp.int32, (t, 128), 1)
        head0 = lane1 < HEAD_DIM
        lane2 = lax.broadcasted_iota(jnp.int32, (2 * t, 128), 1)
        uu = u_ref[...]
        pr = range(npair)
        cols = [slice(128 * pp, 128 * (pp + 1)) for pp in pr]

        i = pl.program_id(1)

        @pl.when(i == 0)
        def _():
            def transpose_v(j, _):
                rows = pl.ds(pl.multiple_of(j * t, t), t)
                for pp in pr:
                    vt[pp, j] = v_ref[rows, cols[pp]].astype(F32).T.astype(BF16)
                return 0

            lax.fori_loop(0, nq, transpose_v, 0)

        for pp in pr:
            q2 = q_ref[:, cols[pp]] * ATT_SCALE
            qq[pp, 0:t, :] = jnp.where(head0, q2, 0)
            qq[pp, t:2 * t, :] = jnp.where(head0, 0, q2)
        acc[...] = jnp.zeros_like(acc)
        cf[...] = jnp.zeros_like(cf)
        csave[...] = jnp.full(csave.shape, 2.0 * SB_DEAD, F32)

        def tile(kb, diag):
            krows = pl.ds(pl.multiple_of(kb * t, t), t)
            zs = [_dot_nt(qq[pp], k_ref[krows, cols[pp]]) for pp in pr]
            lbk = [_log_sigmoid_pair(z) for z in zs]
            lks = [jnp.where(diag_mask, lk, 0.0) if diag else lk for _, lk in lbk]
            sufs = [_cumsum_mm(lk, uu) for lk in lks]
            carries = [cf[pp] for pp in pr]
            avs = []
            for pp in pr:
                a = jnp.exp(lbk[pp][0] + (sufs[pp] + jnp.concatenate([carries[pp]] * (t // 128), axis=1)))
                avs.append((jnp.where(diag_mask, a, 0.0) if diag else a).astype(BF16))
            pvs = [_dot_nt(vt[pp, kb], avs[pp]) for pp in pr]
            for pp in pr:
                acc[pp] += pvs[pp]
                csave[pp] = jnp.where(lane2 == kb, carries[pp], csave[pp])
                cf[pp] = carries[pp] + jnp.broadcast_to(jnp.sum(lks[pp], axis=-1, keepdims=True), (2 * t, 128))

        tile(i, True)

        def alive():
            return jnp.max(cf[...]) > SB_DEAD

        def k_block(state):
            kb, _ = state
            tile(kb, False)
            return kb - 1, alive()

        lax.while_loop(lambda state: jnp.logical_and(state[0] >= 0, state[1]), k_block, (i - 1, alive()))
        for pp in pr:
            o_ref[:, cols[pp]] = jnp.where(head0, acc[pp, :, 0:t].T, acc[pp, :, t:2 * t].T)
            c_ref[2 * pp] = csave[pp, 0:t, :]
            c_ref[2 * pp + 1] = csave[pp, t:2 * t, :]

    width = 128 * npair
    kv = pl.BlockSpec((s_len, width), lambda p, i: (0, p))
    qo = pl.BlockSpec((t, width), lambda p, i: (i, p))
    steps = 4 // npair

    def at(p, i):
        return lambda: jnp.logical_and(pl.program_id(0) == p, pl.program_id(1) == i)

    return _call_with_gather(
        body, shards, at(0, 0), at(steps - 1, (2 * nq) // 3), at(steps - 1, nq - 1),
        name="sb_fwd", grid=(steps, nq),
        in_specs=[qo, kv, kv, pl.BlockSpec((t, t), lambda p, i: (0, 0))],
        out_specs=(qo, pl.BlockSpec((2 * npair, t, 128), lambda p, i: (p, i, 0))),
        out_shape=(jax.ShapeDtypeStruct((s_len, D_GRP), F32),
                   jax.ShapeDtypeStruct((8, s_len, 128), F32)),
        scratch_shapes=[pltpu.VMEM((npair, 2 * t, 128), BF16), pltpu.VMEM((npair, nq, 128, t), BF16),
                        pltpu.VMEM((npair, 128, 2 * t), F32),
                        pltpu.VMEM((npair, 2 * t, 128), F32), pltpu.VMEM((npair, 2 * t, 128), F32)],
    )(qs, ks, vs, tri_suf, *shards)


def _sb_bwd(qs, ks, vs, dos, csaved, tri_suf, tri_pre, rides):
    s_len = qs.shape[0]
    t = SB_TILE
    nq = s_len // t

    npair = SB_BWD_PAIRS

    def body(q_ref, k_ref, v_ref, do_ref, c_ref, u_ref, p_ref, dq_ref, dk_ref, dv_ref,
             qq, dd, qqt, ddt, kt, dq_acc, dkt, dvt, cg):
        row = lax.broadcasted_iota(jnp.int32, (2 * t, t), 0) & (t - 1)
        col = lax.broadcasted_iota(jnp.int32, (2 * t, t), 1)
        diag_mask = col < row
        lane1 = lax.broadcasted_iota(jnp.int32, (t, 128), 1)
        head0 = lane1 < HEAD_DIM
        lane2 = lax.broadcasted_iota(jnp.int32, (2 * t, 128), 1)
        uu, pm = u_ref[...], p_ref[...]
        pr = range(npair)
        cols = [slice(128 * pp, 128 * (pp + 1)) for pp in pr]
        i = pl.program_id(1)

        @pl.when(i == 0)
        def _():
            dkt[...] = jnp.zeros_like(dkt)
            dvt[...] = jnp.zeros_like(dvt)

            def transpose_k(j, _):
                rows = pl.ds(pl.multiple_of(j * t, t), t)
                for pp in pr:
                    kt[pp, j] = k_ref[rows, cols[pp]].astype(F32).T.astype(BF16)
                return 0

            lax.fori_loop(0, nq, transpose_k, 0)

        for pp in pr:
            q2 = q_ref[:, cols[pp]].astype(F32) * ATT_SCALE
            do2 = do_ref[:, cols[pp]].astype(F32)
            for src, nat, tr in ((q2, qq, qqt), (do2, dd, ddt)):
                stacked = jnp.concatenate([jnp.where(head0, src, 0.0), jnp.where(head0, 0.0, src)], axis=0)
                nat[pp] = stacked.astype(BF16)
                tr[pp] = stacked.T.astype(BF16)
        dq_acc[...] = jnp.zeros_like(dq_acc)
        cg[...] = jnp.zeros_like(cg)

        def tile(kb, diag):
            krows = pl.ds(pl.multiple_of(kb * t, t), t)
            zs = [_dot_nt(qq[pp], k_ref[krows, cols[pp]]) for pp in pr]
            das = [_dot_nt(dd[pp], v_ref[krows, cols[pp]]) for pp in pr]
            lbk = [_log_sigmoid_pair(z) for z in zs]
            lks = [jnp.where(diag_mask, lk, 0.0) if diag else lk for _, lk in lbk]
            sufs = [_cumsum_mm(lk, uu) for lk in lks]
            avs, gs = [], []
            for pp in pr:
                cs = jnp.concatenate([c_ref[2 * pp], c_ref[2 * pp + 1]], axis=0)
                cf = jnp.sum(jnp.where(lane2 == kb, cs, 0.0), axis=-1, keepdims=True)
                a = jnp.exp(lbk[pp][0] + (sufs[pp] + cf))
                a = jnp.where(diag_mask, a, 0.0) if diag else a
                avs.append(a.astype(BF16))
                gs.append(a * das[pp])
            gpres = [_cumsum_mm(g, pm) for g in gs]
            dzs = []
            for pp in pr:
                carry = cg[pp]
                beta = jnp.exp(lbk[pp][0])
                dz = gs[pp] - beta * (gs[pp] + (gpres[pp] + jnp.concatenate([carry] * (t // 128), axis=1)))
                dzs.append((jnp.where(diag_mask, dz, 0.0) if diag else dz).astype(BF16))
                cg[pp] = carry + jnp.broadcast_to(jnp.sum(gs[pp], axis=-1, keepdims=True), (2 * t, 128))
            dqs = [_dot_nt(kt[pp, kb], dzs[pp]) for pp in pr]
            dks = [_dot(qqt[pp], dzs[pp]) for pp in pr]
            dvs = [_dot(ddt[pp], avs[pp]) for pp in pr]
            for pp in pr:
                dq_acc[pp] += dqs[pp]
                dkt[pp, kb] += dks[pp]
                dvt[pp, kb] += dvs[pp]

        def k_block(kb, _):
            tile(kb, False)
            return 0

        col_max = jnp.max(jnp.max(c_ref[...], axis=0), axis=0, keepdims=True)
        lane_row = lax.broadcasted_iota(jnp.int32, (1, 128), 1)
        n_live = jnp.sum(jnp.where(jnp.logical_and(col_max > SB_DEAD, lane_row < i), 1, 0))
        lax.fori_loop(i - n_live, i, k_block, 0)
        tile(i, True)
        for pp in pr:
            dq_ref[:, cols[pp]] = jnp.where(head0, dq_acc[pp, :, 0:t].T, dq_acc[pp, :, t:2 * t].T) * ATT_SCALE

        @pl.when(i == nq - 1)
        def _():
            def untranspose(j, _):
                rows = pl.ds(pl.multiple_of(j * t, t), t)
                for pp in pr:
                    dk_ref[rows, cols[pp]] = dkt[pp, j].T
                    dv_ref[rows, cols[pp]] = dvt[pp, j].T
                return 0

            lax.fori_loop(0, nq, untranspose, 0)

    width = 128 * npair
    kv = pl.BlockSpec((s_len, width), lambda p, i: (0, p))
    qo = pl.BlockSpec((t, width), lambda p, i: (i, p))
    tri = pl.BlockSpec((t, t), lambda p, i: (0, 0))
    out = jax.ShapeDtypeStruct((s_len, D_GRP), F32)
    steps = 4 // npair
    return _call_with_exchange(
        body, rides,
        lambda: jnp.logical_and(pl.program_id(0) == 0, pl.program_id(1) == 0),
        lambda: jnp.logical_and(pl.program_id(0) == steps - 1, pl.program_id(1) == nq - 1),
        name="sb_bwd", grid=(steps, nq),
        in_specs=[qo, kv, kv, qo, pl.BlockSpec((2 * npair, t, 128), lambda p, i: (p, i, 0)), tri, tri],
        out_specs=(qo, kv, kv), out_shape=(out, out, out),
        scratch_shapes=[pltpu.VMEM((npair, 2 * t, 128), BF16), pltpu.VMEM((npair, 2 * t, 128), BF16),
                        pltpu.VMEM((npair, 128, 2 * t), BF16), pltpu.VMEM((npair, 128, 2 * t), BF16),
                        pltpu.VMEM((npair, nq, 128, t), BF16),
                        pltpu.VMEM((npair, 128, 2 * t), F32),
                        pltpu.VMEM((npair, nq, 128, t), F32), pltpu.VMEM((npair, nq, 128, t), F32),
                        pltpu.VMEM((npair, 2 * t, 128), F32)],
    )(qs, ks, vs, dos, csaved, tri_suf, tri_pre, *rides)


def _attn_out(o_b, lse_b, o_sb, x2, wdil, wsb, out_g, shards):
    s_len = x2.shape[0]
    tm = ROW_TILE

    def body(o1_ref, l1_ref, o4_ref, l4_ref, o16_ref, l16_ref, osb_ref, x_ref, wdil_ref, wsb_ref, w_ref,
             odil_ref, lse_ref, lse4_ref, lse16_ref, mixed_ref, x1_ref, stage, nat4, nat16):
        _merge_views((o4_ref, l4_ref), (o16_ref, l16_ref), stage, nat4, nat16)
        os_ = (o1_ref[...], _slab_group(nat4, 0), _slab_group(nat16, 0))
        ls = (l1_ref[...], _slab_group(nat4, 1), _slab_group(nat16, 1))
        mx = jnp.maximum(jnp.maximum(ls[0], ls[1]), ls[2])
        es = [jnp.exp(l - mx) for l in ls]
        den = es[0] + es[1] + es[2]
        o_dil = (es[0] * os_[0] + es[1] * os_[1] + es[2] * os_[2]) / den
        odil_ref[...] = o_dil
        lse = mx + jnp.log(den)
        lse_ref[...] = lse
        for p in range(4):
            nat4[p] = lse[:, 128 * p:128 * (p + 1)]
        _split_views(nat4.at[0:4], stage.at[0:4], (lse4_ref,), (lse16_ref,))
        halves = []
        for t, w_r in ((o_dil, wdil_ref), (osb_ref[...], wsb_ref)):
            r = lax.rsqrt(jnp.mean(t * t, axis=-1, keepdims=True) + EPS)
            halves.append(((t * r) * w_r[...]).astype(BF16))
        mixed = jnp.concatenate(halves, axis=1)
        mixed_ref[...] = mixed
        w = w_ref[...].reshape(D_MODEL, D_MODEL)
        x1_ref[...] = x_ref[...] + _dot(mixed, w)

    row = lambda w: pl.BlockSpec((tm, w), lambda i: (i, 0))
    ni = s_len // tm
    return _call_with_gather(
        body, shards, lambda: pl.program_id(0) == 0, lambda: pl.program_id(0) == ni - 2,
        lambda: pl.program_id(0) == ni - 1,
        name="attn_out", grid=(ni,),
        in_specs=[row(D_GRP)] * 2 + [_view_spec(tm, 4)] * 2 + [_view_spec(tm, 16)] * 2
        + [row(D_GRP), row(D_MODEL), _full((1, D_GRP)), _full((1, D_GRP)), _full((N_DEV, OUT_SHARD, D_MODEL))],
        out_specs=(row(D_GRP), row(D_GRP), _view_spec(tm, 4), _view_spec(tm, 16), row(D_MODEL), row(D_MODEL)),
        out_shape=(jax.ShapeDtypeStruct((s_len, D_GRP), F32), jax.ShapeDtypeStruct((s_len, D_GRP), F32),
                   _view_shape(s_len, 4, F32), _view_shape(s_len, 16, F32),
                   jax.ShapeDtypeStruct((s_len, D_MODEL), BF16), jax.ShapeDtypeStruct((s_len, D_MODEL), F32)),
        scratch_shapes=[pltpu.VMEM((8, tm, 128), F32)] * 3,
    )(o_b[0], lse_b[0], o_b[1], lse_b[1], o_b[2], lse_b[2], o_sb, x2, wdil, wsb, out_g, *shards)


def _two_shards(w_ref):
    return w_ref[...].reshape(FF_BLOCK, D_MODEL)


def _ffn_fwd(x1, wn2, tgt, gate_g, up_g, down_g):
    s_len = x1.shape[0]
    tm = ROW_TILE
    ni = s_len // tm

    def body(x_ref, wn_ref, t_ref, wg_ref, wu_ref, wd_ref, g_ref, u_ref, h2_ref, dy_ref, loss_ref, acc):
        j = pl.program_id(1)

        @pl.when(j == 0)
        def _():
            xx = x_ref[...]
            r = lax.rsqrt(jnp.mean(xx * xx, axis=-1, keepdims=True) + EPS)
            h2_ref[...] = ((xx * r) * wn_ref[...]).astype(BF16)
            acc[...] = jnp.zeros_like(acc)

        h = h2_ref[...]
        g = _dot_nt(h, _two_shards(wg_ref))
        u = _dot_nt(h, _two_shards(wu_ref))
        g_ref[...] = g
        u_ref[...] = u
        act = (g * (1.0 / (1.0 + jnp.exp(-g)))) * u
        acc[...] += _dot(act.astype(BF16), _two_shards(wd_ref))

        @pl.when(j == FF_STEPS - 1)
        def _():
            err = (x_ref[...] + acc[...]) - t_ref[...]
            dy_ref[...] = err * (1.0 / D_MODEL)
            part = 0.5 * jnp.sum(jnp.mean(err * err, axis=-1, keepdims=True))
            loss_ref[...] = jnp.full((8, 128), part, F32)

    row = pl.BlockSpec((tm, D_MODEL), lambda i, j: (i, 0))
    hid = pl.BlockSpec((tm, FF_BLOCK), lambda i, j: (i, j))
    return pl.pallas_call(
        body, name="ffn_fwd", grid=(ni, FF_STEPS),
        in_specs=[row, pl.BlockSpec((1, D_MODEL), lambda i, j: (0, 0)), row,
                  pl.BlockSpec((2, FF_PAD, D_MODEL), lambda i, j: (j, 0, 0)),
                  pl.BlockSpec((2, FF_PAD, D_MODEL), lambda i, j: (j, 0, 0)),
                  pl.BlockSpec((2, FF_PAD, D_MODEL), lambda i, j: (j, 0, 0))],
        out_specs=(hid, hid, row, row, pl.BlockSpec((8, 128), lambda i, j: (i, 0))),
        out_shape=(jax.ShapeDtypeStruct((s_len, N_DEV * FF_PAD), F32),
                   jax.ShapeDtypeStruct((s_len, N_DEV * FF_PAD), F32),
                   jax.ShapeDtypeStruct((s_len, D_MODEL), BF16),
                   jax.ShapeDtypeStruct((s_len, D_MODEL), F32),
                   jax.ShapeDtypeStruct((ni * 8, 128), F32)),
        scratch_shapes=[pltpu.VMEM((tm, D_MODEL), F32)],
        compiler_params=_params(),
    )(x1, wn2, tgt, gate_g, up_g, down_g)


def _ffn_bwd_dx(dy, g, u, gate_g, up_g, down_g):
    s_len = dy.shape[0]
    tm = ROW_TILE

    def body(dy_ref, g_ref, u_ref, wg_ref, wu_ref, wd_ref, dg_ref, du_ref, act_ref, dh_ref, acc):
        j = pl.program_id(1)

        @pl.when(j == 0)
        def _():
            acc[...] = jnp.zeros_like(acc)

        halves = [slice(0, tm // 2), slice(tm // 2, tm)]
        wd, wg, wu = _two_shards(wd_ref), _two_shards(wg_ref), _two_shards(wu_ref)
        das = [_dot_nt(dy_ref[rows, :].astype(BF16), wd) for rows in halves]

        def elementwise(rows, da):
            gg, uu = g_ref[rows, :], u_ref[rows, :]
            sig = 1.0 / (1.0 + jnp.exp(-gg))
            silu = gg * sig
            act_ref[rows, :] = (silu * uu).astype(BF16)
            du = (da * silu).astype(BF16)
            dg = (da * uu * (sig * (1.0 + gg * (1.0 - sig)))).astype(BF16)
            du_ref[rows, :] = du
            dg_ref[rows, :] = dg
            return dg, du

        dg0, du0 = elementwise(halves[0], das[0])
        acc[halves[0], :] += _dot(dg0, wg) + _dot(du0, wu)
        dg1, du1 = elementwise(halves[1], das[1])
        acc[halves[1], :] += _dot(dg1, wg) + _dot(du1, wu)

        @pl.when(j == FF_STEPS - 1)
        def _():
            dh_ref[...] = acc[...]

    row = pl.BlockSpec((tm, D_MODEL), lambda i, j: (i, 0))
    hid = pl.BlockSpec((tm, FF_BLOCK), lambda i, j: (i, j))
    hid_bf = jax.ShapeDtypeStruct((s_len, N_DEV * FF_PAD), BF16)
    return pl.pallas_call(
        body, name="ffn_bwd_dx", grid=(s_len // tm, FF_STEPS),
        in_specs=[row, hid, hid,
                  pl.BlockSpec((2, FF_PAD, D_MODEL), lambda i, j: (j, 0, 0)),
                  pl.BlockSpec((2, FF_PAD, D_MODEL), lambda i, j: (j, 0, 0)),
                  pl.BlockSpec((2, FF_PAD, D_MODEL), lambda i, j: (j, 0, 0))],
        out_specs=(hid, hid, hid, row),
        out_shape=(hid_bf, hid_bf, hid_bf, jax.ShapeDtypeStruct((s_len, D_MODEL), F32)),
        scratch_shapes=[pltpu.VMEM((tm, D_MODEL), F32)],
        compiler_params=_params(),
    )(dy, g, u, gate_g, up_g, down_g)


def _ffn_bwd_dw(h2, dy, dg, du, act):
    s_len = h2.shape[0]
    tm = DW_ROW_TILE
    ni = s_len // tm

    half = FF_PAD // 2

    def body(h_ref, dy_ref, dg_ref, du_ref, act_ref, dwg_ref, dwu_ref, dwd_lo_ref, dwd_hi_ref, ag, au, ad):
        i = pl.program_id(1)

        @pl.when(i == 0)
        def _():
            ag[...] = jnp.zeros_like(ag)
            au[...] = jnp.zeros_like(au)
            ad[...] = jnp.zeros_like(ad)

        h = h_ref[...]
        ag[...] += _dot_tn(dg_ref[...], h)
        au[...] += _dot_tn(du_ref[...], h)
        ad[...] += _dot_tn(act_ref[...], dy_ref[...].astype(BF16))

        @pl.when(i == ni - 1)
        def _():
            for acc_ref, out_ref in ((ag, dwg_ref), (au, dwu_ref)):
                out_ref[...] = acc_ref[...].astype(BF16).reshape(2, FF_PAD, D_MODEL)
            for dev in range(2):
                dwd_lo_ref[dev] = ad[FF_PAD * dev:FF_PAD * dev + half, :].astype(BF16)
                dwd_hi_ref[dev] = ad[FF_PAD * dev + half:FF_PAD * (dev + 1), :].astype(BF16)

    row = pl.BlockSpec((tm, D_MODEL), lambda j, i: (i, 0))
    hid = pl.BlockSpec((tm, FF_BLOCK), lambda j, i: (i, j))
    row_w = pl.BlockSpec((2, FF_PAD, D_MODEL), lambda j, i: (j, 0, 0))
    half_w = pl.BlockSpec((2, half, D_MODEL), lambda j, i: (j, 0, 0))
    grad = jax.ShapeDtypeStruct((N_DEV, FF_PAD, D_MODEL), BF16)
    half_grad = jax.ShapeDtypeStruct((N_DEV, half, D_MODEL), BF16)
    return pl.pallas_call(
        body, name="ffn_bwd_dw", grid=(FF_STEPS, ni),
        in_specs=[row, row, hid, hid, hid], out_specs=(row_w, row_w, half_w, half_w),
        out_shape=(grad, grad, half_grad, half_grad),
        scratch_shapes=[pltpu.VMEM((FF_BLOCK, D_MODEL), F32)] * 3,
        compiler_params=_params(),
    )(h2, dy, dg, du, act)


def _rms_bwd(dy, t, w):
    r = lax.rsqrt(jnp.mean(t * t, axis=-1, keepdims=True) + EPS)
    gw = dy * w
    dt = r * (gw - t * ((r * r) * jnp.mean(gw * t, axis=-1, keepdims=True)))
    return dt, dy * t * r


def _attn_out_bwd(dy, dh2, x1, wn2, b_g, mixed, o_dil, o_sb, wdil, wsb, bd512):
    s_len = dy.shape[0]
    tm = ROW_TILE
    ni = s_len // tm

    def body(dy_ref, dh_ref, x1_ref, wn_ref, w_ref, mixed_ref, odil_ref, osb_ref, wdil_ref, wsb_ref, bd_ref,
             dx1_ref, dodil_ref, delta_ref, dosb_ref, dwout_ref, dwn_ref, dwdil_ref, dwsb_ref,
             do4_ref, dl4_ref, do16_ref, dl16_ref, wacc, both, stage):
        i = pl.program_id(0)

        @pl.when(i == 0)
        def _():
            wacc[...] = jnp.zeros_like(wacc)
            dwn_ref[...] = jnp.zeros_like(dwn_ref)
            dwdil_ref[...] = jnp.zeros_like(dwdil_ref)
            dwsb_ref[...] = jnp.zeros_like(dwsb_ref)

        dnorm, dw_rows = _rms_bwd(dh_ref[...], x1_ref[...], wn_ref[...])
        dx1 = dy_ref[...] + dnorm
        dx1_ref[...] = dx1
        dwn_ref[...] += jnp.sum(dw_rows, axis=0, keepdims=True)
        dx1b = dx1.astype(BF16)
        w = w_ref[...].reshape(D_MODEL, D_MODEL)
        dmixed = _dot_nt(dx1b, w)
        wacc[...] += _dot_tn(mixed_ref[...], dx1b)
        o_dil = odil_ref[...]
        d_odil, dw_rows = _rms_bwd(dmixed[:, :D_GRP], o_dil, wdil_ref[...])
        dwdil_ref[...] += jnp.sum(dw_rows, axis=0, keepdims=True)
        dodil_ref[...] = d_odil.astype(BF16)
        delta = _mm_split(d_odil * o_dil, bd_ref[...])
        delta_ref[...] = delta
        for p in range(4):
            both[p] = d_odil[:, 128 * p:128 * (p + 1)]
            both[4 + p] = delta[:, 128 * p:128 * (p + 1)]
        _split_views(both, stage, (do4_ref, dl4_ref), (do16_ref, dl16_ref))
        d_osb, dw_rows = _rms_bwd(dmixed[:, D_GRP:], osb_ref[...], wsb_ref[...])
        dwsb_ref[...] += jnp.sum(dw_rows, axis=0, keepdims=True)
        dosb_ref[...] = d_osb.astype(BF16)

        @pl.when(i == ni - 1)
        def _():
            dwout_ref[...] = wacc[...].astype(BF16).reshape(N_DEV, OUT_SHARD, D_MODEL)

    row = lambda w: pl.BlockSpec((tm, w), lambda i: (i, 0))
    return pl.pallas_call(
        body, name="attn_out_bwd", grid=(ni,),
        in_specs=[row(D_MODEL), row(D_MODEL), row(D_MODEL), _full((1, D_MODEL)),
                  _full((N_DEV, OUT_SHARD, D_MODEL)),
                  row(D_MODEL), row(D_GRP), row(D_GRP), _full((1, D_GRP)), _full((1, D_GRP)),
                  _full((D_GRP, D_GRP))],
        out_specs=(row(D_MODEL), row(D_GRP), row(D_GRP), row(D_GRP),
                   _full((N_DEV, OUT_SHARD, D_MODEL)), _full((1, D_MODEL)), _full((1, D_GRP)), _full((1, D_GRP)),
                   _view_spec(tm, 4), _view_spec(tm, 4), _view_spec(tm, 16), _view_spec(tm, 16)),
        out_shape=(jax.ShapeDtypeStruct((s_len, D_MODEL), F32), jax.ShapeDtypeStruct((s_len, D_GRP), BF16),
                   jax.ShapeDtypeStruct((s_len, D_GRP), F32), jax.ShapeDtypeStruct((s_len, D_GRP), BF16),
                   jax.ShapeDtypeStruct((N_DEV, OUT_SHARD, D_MODEL), BF16),
                   jax.ShapeDtypeStruct((1, D_MODEL), F32), jax.ShapeDtypeStruct((1, D_GRP), F32),
                   jax.ShapeDtypeStruct((1, D_GRP), F32),
                   _view_shape(s_len, 4, BF16), _view_shape(s_len, 4, F32),
                   _view_shape(s_len, 16, BF16), _view_shape(s_len, 16, F32)),
        scratch_shapes=[pltpu.VMEM((D_MODEL, D_MODEL), F32), pltpu.VMEM((8, tm, 128), F32),
                        pltpu.VMEM((8, tm, 128), F32)],
        compiler_params=_params(),
    )(dy, dh2, x1, wn2, b_g, mixed, o_dil, o_sb, wdil, wsb, bd512)


def _qkv_bwd(dq_b, dk_b, dv_b, dqs, dks, dvs, qraw, kraw, cos2, sin2, qnw, knw, bd, rides):
    s_len = qraw.shape[0]
    tm = ROW_TILE
    ni = s_len // tm

    def body(dq1, dk1, dv1, dq4, dk4, dv4, dq16, dk16, dv16, dqs_ref, dks_ref, dvs_ref,
             qraw_ref, kraw_ref, cos_ref, sin_ref, qnw_ref, knw_ref, bd_ref,
             dproj_ref, dqn_ref, dkn_ref, stage, nat4, nat16):
        i = pl.program_id(0)

        @pl.when(i == 0)
        def _():
            dqn_ref[...] = jnp.zeros_like(dqn_ref)
            dkn_ref[...] = jnp.zeros_like(dkn_ref)

        _merge_views((dq4, dk4, dv4), (dq16, dk16, dv16), stage, nat4, nat16)
        cos_t, sin_t, bdm = cos_ref[...], sin_ref[...], bd_ref[...]
        for grp, (part1, raw_ref, nw_ref, dn_ref) in enumerate(((dq1, qraw_ref, qnw_ref, dqn_ref),
                                                                (dk1, kraw_ref, knw_ref, dkn_ref))):
            dn_acc = 0.0
            for p in range(4):
                cols = slice(128 * p, 128 * (p + 1))
                d_rope = part1[:, cols] + nat4[4 * grp + p] + nat16[4 * grp + p]
                d_norm = d_rope * cos_t + _swap_halves(d_rope * sin_t)
                t = raw_ref[:, cols]
                w = nw_ref[...]
                r = lax.rsqrt(_mm_split(t * t, bdm) * (1.0 / HEAD_DIM) + EPS)
                gw = d_norm * w
                corr = _mm_split(gw * t, bdm) * (1.0 / HEAD_DIM)
                dt = r * (gw - t * ((r * r) * corr))
                dn_acc = dn_acc + jnp.sum(d_norm * t * r, axis=0, keepdims=True)
                dproj_ref[:, D_GRP * grp + 128 * p:D_GRP * grp + 128 * (p + 1)] = dt.astype(BF16)
            dn_ref[...] += dn_acc
        dproj_ref[:, 2 * D_GRP:3 * D_GRP] = (dv1[...] + _slab_group(nat4, 2) + _slab_group(nat16, 2)).astype(BF16)
        dproj_ref[:, 3 * D_GRP:4 * D_GRP] = dqs_ref[...].astype(BF16)
        dproj_ref[:, 4 * D_GRP:5 * D_GRP] = dks_ref[...].astype(BF16)
        dproj_ref[:, 5 * D_GRP:6 * D_GRP] = dvs_ref[...].astype(BF16)

    row = lambda w: pl.BlockSpec((tm, w), lambda i: (i, 0))
    return _call_with_exchange(
        body, rides, lambda: pl.program_id(0) == 0, lambda: pl.program_id(0) == ni - 1,
        name="qkv_bwd", grid=(ni,),
        in_specs=[row(D_GRP)] * 3 + [_view_spec(tm, 4)] * 3 + [_view_spec(tm, 16)] * 3 + [row(D_GRP)] * 5
        + [row(128), row(128), _full((1, 128)), _full((1, 128)), _full((128, 128))],
        out_specs=(row(D_IN), _full((1, 128)), _full((1, 128))),
        out_shape=(jax.ShapeDtypeStruct((s_len, D_IN), BF16), jax.ShapeDtypeStruct((1, 128), F32),
                   jax.ShapeDtypeStruct((1, 128), F32)),
        scratch_shapes=[pltpu.VMEM((12, tm, 128), F32)] * 3,
    )(dq_b[0], dk_b[0], dv_b[0], dq_b[1], dk_b[1], dv_b[1], dq_b[2], dk_b[2], dv_b[2],
      dqs, dks, dvs, qraw, kraw, cos2, sin2, qnw, knw, bd, *rides)


def _in_bwd_dx(dproj, a_g, x2, dx1, wn1, rides):
    s_len = x2.shape[0]
    tm = ROW_TILE
    ni = s_len // tm

    def body(dp_ref, w_ref, x_ref, dx1_ref, wn_ref, gx_ref, dwn_ref, w_full):
        i = pl.program_id(0)

        @pl.when(i == 0)
        def _():
            dwn_ref[...] = jnp.zeros_like(dwn_ref)
            for d in range(N_DEV):
                w_full[:, IN_SHARD * d:IN_SHARD * (d + 1)] = w_ref[d]

        dh = _dot_nt(dp_ref[...], w_full[...])
        dnorm, dw_rows = _rms_bwd(dh, x_ref[...], wn_ref[...])
        gx_ref[...] = dx1_ref[...] + dnorm
        dwn_ref[...] += jnp.sum(dw_rows, axis=0, keepdims=True)

    row = lambda w: pl.BlockSpec((tm, w), lambda i: (i, 0))
    return _call_with_exchange(
        body, rides, lambda: pl.program_id(0) == 0, lambda: pl.program_id(0) == ni - 1,
        name="in_bwd_dx", grid=(ni,),
        in_specs=[row(D_IN), pl.BlockSpec((N_DEV, D_MODEL, IN_SHARD), lambda i: (0, 0, 0)),
                  row(D_MODEL), row(D_MODEL), _full((1, D_MODEL))],
        out_specs=(row(D_MODEL), _full((1, D_MODEL))),
        out_shape=(jax.ShapeDtypeStruct((s_len, D_MODEL), F32), jax.ShapeDtypeStruct((1, D_MODEL), F32)),
        scratch_shapes=[pltpu.VMEM((D_MODEL, D_IN), BF16)],
    )(dproj, a_g, x2, dx1, wn1, *rides)


def _in_bwd_dw(h1, dproj):
    s_len = h1.shape[0]
    tm = DW_ROW_TILE
    ni = s_len // tm

    half_d = D_MODEL // 2

    def body(h_ref, dp_ref, lo_ref, hi_ref, acc):
        i = pl.program_id(1)

        @pl.when(i == 0)
        def _():
            acc[...] = jnp.zeros_like(acc)

        acc[...] += _dot_tn(h_ref[...], dp_ref[...])

        @pl.when(i == ni - 1)
        def _():
            for dev in range(2):
                cols = slice(IN_SHARD * dev, IN_SHARD * (dev + 1))
                lo_ref[dev] = acc[0:half_d, cols].astype(BF16)
                hi_ref[dev] = acc[half_d:D_MODEL, cols].astype(BF16)

    half_w = pl.BlockSpec((2, half_d, IN_SHARD), lambda d, i: (d, 0, 0))
    half_grad = jax.ShapeDtypeStruct((N_DEV, half_d, IN_SHARD), BF16)
    return pl.pallas_call(
        body, name="in_bwd_dw", grid=(N_DEV // 2, ni),
        in_specs=[pl.BlockSpec((tm, D_MODEL), lambda d, i: (i, 0)),
                  pl.BlockSpec((tm, 2 * IN_SHARD), lambda d, i: (i, d))],
        out_specs=(half_w, half_w),
        out_shape=(half_grad, half_grad),
        scratch_shapes=[pltpu.VMEM((D_MODEL, 2 * IN_SHARD), F32)],
        compiler_params=_params(),
    )(h1, dproj)


def _adamw(recv, w, m, v, recv_hi=None):
    rows, cols = w.shape
    tr = next((t for t in (128, 32) if rows % t == 0), rows)
    recvs = [recv] if recv_hi is None else [recv, recv_hi]
    lo_tiles = recv.shape[1] // tr

    def body(*refs):
        p_refs = refs[:len(recvs)]
        w_ref, m_ref, v_ref, g_ref, d_ref, nm_ref, nv_ref = refs[len(recvs):]

        def slot(s):
            if len(p_refs) == 1:
                return p_refs[0][s].astype(F32)
            return jnp.where(pl.program_id(0) < lo_tiles, p_refs[0][s], p_refs[1][s]).astype(F32)

        g = slot(0)
        for s in range(1, N_DEV):
            g = g + slot(s)
        m_new = ADAM_B1 * m_ref[...] + (1.0 - ADAM_B1) * g
        v_new = ADAM_B2 * v_ref[...] + (1.0 - ADAM_B2) * (g * g)
        m_hat = m_new / (1.0 - ADAM_B1 ** ADAM_STEP)
        v_hat = v_new / (1.0 - ADAM_B2 ** ADAM_STEP)
        g_ref[...] = g
        d_ref[...] = -ADAM_LR * (m_hat / (jnp.sqrt(v_hat) + ADAM_EPS) + ADAM_WD * w_ref[...])
        nm_ref[...] = m_new
        nv_ref[...] = v_new

    blk = pl.BlockSpec((tr, cols), lambda i: (i, 0))
    out = jax.ShapeDtypeStruct((rows, cols), F32)
    return pl.pallas_call(
        body, name=f"adamw_{rows}x{cols}", grid=(rows // tr,),
        in_specs=([pl.BlockSpec((N_DEV, tr, cols), lambda i: (0, i, 0))] if recv_hi is None else
                  [pl.BlockSpec((N_DEV, tr, cols), lambda i: (0, jnp.minimum(i, lo_tiles - 1), 0)),
                   pl.BlockSpec((N_DEV, tr, cols), lambda i: (0, jnp.maximum(i - lo_tiles, 0), 0))])
        + [blk, blk, blk],
        out_specs=(blk,) * 4, out_shape=(out,) * 4,
        compiler_params=_params(),
    )(*recvs, w, m, v)


def _adamw_many(items, rides):
    tr = 32
    tiles = [w.shape[0] // tr for _, w, _, _ in items]
    starts = [sum(tiles[:k]) for k in range(len(items))]
    total = sum(tiles)
    n_items = len(items)

    def body(*refs):
        i = pl.program_id(0)
        in_refs, out_refs = refs[:4 * n_items], refs[4 * n_items:]
        for k in range(n_items):
            def update(k=k):
                p_ref, w_ref, m_ref, v_ref = in_refs[4 * k:4 * k + 4]
                g_ref, d_ref, nm_ref, nv_ref = out_refs[4 * k:4 * k + 4]
                g = p_ref[0].astype(F32)
                for s in range(1, N_DEV):
                    g = g + p_ref[s].astype(F32)
                m_new = ADAM_B1 * m_ref[...] + (1.0 - ADAM_B1) * g
                v_new = ADAM_B2 * v_ref[...] + (1.0 - ADAM_B2) * (g * g)
                m_hat = m_new / (1.0 - ADAM_B1 ** ADAM_STEP)
                v_hat = v_new / (1.0 - ADAM_B2 ** ADAM_STEP)
                g_ref[...] = g
                d_ref[...] = -ADAM_LR * (m_hat / (jnp.sqrt(v_hat) + ADAM_EPS) + ADAM_WD * w_ref[...])
                nm_ref[...] = m_new
                nv_ref[...] = v_new

            pl.when(jnp.logical_and(i >= starts[k], i < starts[k] + tiles[k]))(update)

    in_specs, out_specs, out_shape, args = [], [], [], []
    for k, (recv, w, m, v) in enumerate(items):
        tile_of = functools.partial(lambda i, s0, nk: jnp.clip(i - s0, 0, nk - 1), s0=starts[k], nk=tiles[k])
        blk = pl.BlockSpec((tr, D_MODEL), functools.partial(lambda i, t: (t(i), 0), t=tile_of))
        in_specs += [pl.BlockSpec((N_DEV, tr, D_MODEL), functools.partial(lambda i, t: (0, t(i), 0), t=tile_of)),
                     blk, blk, blk]
        out_specs += [blk] * 4
        out_shape += [jax.ShapeDtypeStruct(w.shape, F32)] * 4
        args += [recv, w, m, v]
    res = _call_with_exchange(
        body, rides, lambda: pl.program_id(0) == 0, lambda: pl.program_id(0) == total - 1,
        name="adamw_many", grid=(total,), in_specs=in_specs, out_specs=out_specs, out_shape=out_shape,
    )(*args, *rides)
    return [tuple(res[4 * k:4 * k + 4]) for k in range(n_items)], list(res[4 * n_items:])


def _rope_tables(s_len):
    pos = np.arange(s_len, dtype=np.float32)
    inv_freq = np.float32(ROPE_THETA) ** (-np.arange(0, HEAD_DIM, 2, dtype=np.float32) / np.float32(HEAD_DIM))
    ang = (pos[:, None] * inv_freq[None, :]).astype(np.float32)
    cos, sin = np.cos(ang), np.sin(ang)
    cos2 = np.concatenate([cos, cos, cos, cos], axis=1)
    sin2 = np.concatenate([-sin, sin, -sin, sin], axis=1)
    return jnp.asarray(cos2, F32), jnp.asarray(sin2, F32)


def _block_diag_ones(n):
    i = jnp.arange(n)
    return (i[:, None] // HEAD_DIM == i[None, :] // HEAD_DIM).astype(BF16)


def _pad_cols(t):
    return jnp.pad(t, ((0, 0), (0, FF_PAD - FF_SHARD)))


def _pad_rows(t):
    return jnp.pad(t, ((0, FF_PAD - FF_SHARD), (0, 0)))


LOSS_ROW = 26


def _pack_small(n1, n2, ndil, nsb, nq, nk, scalar=None):
    pad = lambda t: jnp.pad(t.reshape(1, -1), ((0, 0), (0, 128 - t.size)))
    last = jnp.zeros((1, 128), F32) if scalar is None else pad(scalar)
    rows = [n1.reshape(8, 128), n2.reshape(8, 128), ndil.reshape(4, 128), nsb.reshape(4, 128),
            pad(nq), pad(nk), last, jnp.zeros((5, 128), F32)]
    return jnp.concatenate(rows, axis=0)


def _unpack_small(t):
    return (t[0:8].reshape(1, D_MODEL), t[8:16].reshape(1, D_MODEL), t[16:20].reshape(1, D_GRP),
            t[20:24].reshape(1, D_GRP), t[24:25, :HEAD_DIM], t[25:26, :HEAD_DIM])


def kernel(x, attn_norm_w, w_in, q_norm_w, k_norm_w, dil_out_norm_w, sb_out_norm_w, w_out, ffn_norm_w, w_gate, w_up, w_down, loss_target, m_attn_norm_w, m_w_in, m_q_norm_w, m_k_norm_w, m_dil_out_norm_w, m_sb_out_norm_w, m_w_out, m_ffn_norm_w, m_w_gate, m_w_up, m_w_down, v_attn_norm_w, v_w_in, v_q_norm_w, v_k_norm_w, v_dil_out_norm_w, v_sb_out_norm_w, v_w_out, v_ffn_norm_w, v_w_gate, v_w_up, v_w_down):
    s_len = x.shape[1]
    x2, tgt = x[0], loss_target[0]

    (a_g,) = _gather_weights([w_in[0].astype(BF16)])
    gate_loc = _pad_cols(w_gate[0]).T.astype(BF16)
    up_loc = _pad_cols(w_up[0]).T.astype(BF16)
    down_loc = _pad_rows(w_down[0]).astype(BF16)
    out_loc = w_out[0].astype(BF16)

    cos2, sin2 = _rope_tables(s_len)
    bd128, bd512 = _block_diag_ones(128), _block_diag_ones(D_GRP)
    idx = jnp.arange(SB_TILE)
    tri_suf = (idx[:, None] > idx[None, :]).astype(BF16)
    tri_pre = (idx[:, None] < idx[None, :]).astype(BF16)
    qnw2 = jnp.concatenate([q_norm_w, q_norm_w], axis=1)
    knw2 = jnp.concatenate([k_norm_w, k_norm_w], axis=1)

    (h1, qraw, kraw, q, k, va, qs, ks, vs, q4, k4, v4, q16, k16, v16,
     gate_g) = _attn_in(x2, attn_norm_w, a_g, cos2, sin2, qnw2, knw2, bd128, shards=[gate_loc])
    qkv_views = {1: (q, k, va), 4: (q4, k4, v4), 16: (q16, k16, v16)}
    fwd_riders = {1: [out_loc], 4: [], 16: []}
    o_b, lse_b, gathered = [], [], {}
    for r in DILATIONS:
        o, lse, *gathered[r] = _dil_fwd(*qkv_views[r], r, shards=fwd_riders[r])
        o_b.append(o)
        lse_b.append(lse)
    (out_g,) = gathered[1]
    o_sb, c_sb, up_g = _sb_fwd(qs, ks, vs, tri_suf, shards=[up_loc])
    o_dil, lse_tot, lse4, lse16, mixed, x1, down_g = _attn_out(
        o_b, lse_b, o_sb, x2, dil_out_norm_w, sb_out_norm_w, out_g, shards=[down_loc])
    g, u, h2, dy, loss_parts = _ffn_fwd(x1, ffn_norm_w, tgt, gate_g, up_g, down_g)
    loss_local = jnp.sum(loss_parts[::8, 0])

    dg, du, act, dh2 = _ffn_bwd_dx(dy, g, u, gate_g, up_g, down_g)
    (dx1, do_dil, delta, do_sb, dwout, dn2, dndil, dnsb, do4, dl4, do16, dl16) = _attn_out_bwd(
        dy, dh2, x1, ffn_norm_w, out_g, mixed, o_dil, o_sb, dil_out_norm_w, sb_out_norm_w, bd512)
    dwg, dwu, dwd_lo, dwd_hi = _ffn_bwd_dw(h2, dy, dg, du, act)
    dqs, dks, dvs, r_gate = _sb_bwd(qs, ks, vs, do_sb, c_sb, tri_suf, tri_pre, rides=[dwg])
    cot_views = {1: (do_dil, lse_tot, delta), 4: (do4, lse4, dl4), 16: (do16, lse16, dl16)}
    riders = {1: [dwd_lo], 4: [dwd_hi], 16: [dwu]}
    dq_b, dk_b, dv_b, landed = [], [], [], {}
    for r in DILATIONS:
        dq, dk, dv, *landed[r] = _dil_bwd(*qkv_views[r], *cot_views[r], r, rides=riders[r])
        dq_b.append(dq)
        dk_b.append(dk)
        dv_b.append(dv)
    (r_down_lo,), (r_down_hi,), (r_up,) = landed[1], landed[4], landed[16]
    dproj, dqn2, dkn2, r_out = _qkv_bwd(dq_b, dk_b, dv_b, dqs, dks, dvs, qraw, kraw, cos2, sin2, qnw2, knw2,
                                        bd128, rides=[dwout])
    dwin_lo, dwin_hi = _in_bwd_dw(h1, dproj)
    grad_x, dn1, r_in_lo = _in_bwd_dx(dproj, a_g, x2, dx1, attn_norm_w, rides=[dwin_lo])
    dqn = dqn2[:, :HEAD_DIM] + dqn2[:, HEAD_DIM:]
    dkn = dkn2[:, :HEAD_DIM] + dkn2[:, HEAD_DIM:]

    small = _pack_small(dn1, dn2, dndil, dnsb, dqn, dkn, loss_local)
    (r_small,) = _exchange_grads([], small)
    (new_gate, new_up, new_out), (r_in_hi,) = _adamw_many(
        [(r_gate, w_gate[0].T, m_w_gate[0].T, v_w_gate[0].T), (r_up, w_up[0].T, m_w_up[0].T, v_w_up[0].T),
         (r_out, w_out[0], m_w_out[0], v_w_out[0])], rides=[dwin_hi])
    big = {
        "w_in": _adamw(r_in_lo, w_in[0], m_w_in[0], v_w_in[0], recv_hi=r_in_hi),
        "w_gate": tuple(t.T for t in new_gate),
        "w_up": tuple(t.T for t in new_up),
        "w_down": _adamw(r_down_lo, w_down[0], m_w_down[0], v_w_down[0], recv_hi=r_down_hi),
        "w_out": new_out,
    }
    packs = [_pack_small(*ts) for ts in (
        (attn_norm_w, ffn_norm_w, dil_out_norm_w, sb_out_norm_w, q_norm_w, k_norm_w),
        (m_attn_norm_w, m_ffn_norm_w, m_dil_out_norm_w, m_sb_out_norm_w, m_q_norm_w, m_k_norm_w),
        (v_attn_norm_w, v_ffn_norm_w, v_dil_out_norm_w, v_sb_out_norm_w, v_q_norm_w, v_k_norm_w))]
    small_raw = _adamw(r_small, *packs)
    loss = small_raw[0][LOSS_ROW, 0]
    small_out = [_unpack_small(t) for t in small_raw]
    names = ["attn_norm_w", "w_in", "q_norm_w", "k_norm_w", "dil_out_norm_w", "sb_out_norm_w", "w_out",
             "ffn_norm_w", "w_gate", "w_up", "w_down"]
    small_pos = {"attn_norm_w": 0, "ffn_norm_w": 1, "dil_out_norm_w": 2, "sb_out_norm_w": 3,
                 "q_norm_w": 4, "k_norm_w": 5}
    outs = [loss, grad_x[None]]
    for kind in range(4):
        for name in names:
            if name in small_pos:
                outs.append(small_out[kind][small_pos[name]])
            else:
                outs.append(big[name][kind][None])
    return tuple(outs)
```

```python
import functools

import jax
import jax.numpy as jnp
import numpy as np
from jax import lax
from jax.experimental import pallas as pl
from jax.experimental.pallas import tpu as pltpu

F32 = jnp.float32
BF16 = jnp.bfloat16

N_DEV = 8
D_MODEL = 1024
HEAD_DIM = 64
D_GRP = 512
D_IN = 6 * D_GRP
IN_SHARD = D_IN // N_DEV
FF_SHARD = 352
FF_PAD = 384
FF_BLOCK = 2 * FF_PAD
FF_STEPS = N_DEV // 2
OUT_SHARD = D_MODEL // N_DEV
BLOCK = 128
DILATIONS = (1, 4, 16)
ROPE_THETA = 10000.0
EPS = 1e-6
ATT_SCALE = HEAD_DIM ** -0.5
NEG = -1e30

ADAM_LR = 0.001
ADAM_B1 = 0.9
ADAM_B2 = 0.999
ADAM_EPS = 1e-08
ADAM_WD = 0.01
ADAM_STEP = 10

SB_TILE = 256
SB_DEAD = -104.0
SB_PAIRS = 4
SB_BWD_PAIRS = 2
ROW_TILE = 512
DW_ROW_TILE = 1024
VMEM_LIMIT = 56 * 1024 * 1024
MESH = pl.DeviceIdType.MESH


def _dot(a, b):
    return jnp.dot(a, b, preferred_element_type=F32)


def _dot_nt(a, b):
    return lax.dot_general(a, b, (((1,), (1,)), ((), ())), preferred_element_type=F32)


def _dot_tn(a, b):
    return lax.dot_general(a, b, (((0,), (0,)), ((), ())), preferred_element_type=F32)


def _mm_split(t, m):
    hi = t.astype(BF16)
    lo = (t - hi.astype(F32)).astype(BF16)
    return _dot(hi, m) + _dot(lo, m)


def _params(**kw):
    return pltpu.CompilerParams(vmem_limit_bytes=VMEM_LIMIT, **kw)


def _full(shape):
    nd = len(shape)
    return pl.BlockSpec(shape, lambda *_: (0,) * nd)


def _view_shape(s_len, r, dtype):
    return jax.ShapeDtypeStruct((s_len // r, r * D_GRP), dtype)


def _view_spec(tm, r):
    return pl.BlockSpec((tm // r, r * D_GRP), lambda i: (i, 0))


def _swap_halves(t):
    lane = lax.broadcasted_iota(jnp.int32, t.shape, 1)
    first = (lane & 32) == 0
    return jnp.where(first, pltpu.roll(t, 96, 1), pltpu.roll(t, 32, 1))


def _log_sigmoid_pair(z):
    neg_abs = lax.bitcast_convert_type(lax.bitcast_convert_type(z, jnp.uint32) | jnp.uint32(0x80000000), F32)
    lb = jnp.minimum(z, 0.0) - jnp.log(1.0 + jnp.exp(neg_abs))
    return lb, lb - z


def _cumsum_mm(t, tri):
    return _dot(t.astype(BF16), tri)


def _split_views(src_ref, stage_ref, views4, views16):
    slabs, n, _ = src_ref.shape
    n4, n16 = n // 4, n // 16
    for j in range(slabs):
        g, lanes = j // 4, 128 * (j % 4)
        src, stage = src_ref.at[j], stage_ref.at[j]
        for c4 in range(4):
            blk = src[pl.ds(c4, n4, stride=4), :]
            stage[n4 * c4:n4 * (c4 + 1), :] = blk
            col = D_GRP * c4 + lanes
            views4[g][:, col:col + 128] = blk.astype(views4[g].dtype)
        for c4 in range(4):
            for c1 in range(4):
                blk = stage[pl.ds(n4 * c4 + c1, n16, stride=4), :]
                col = D_GRP * (4 * c1 + c4) + lanes
                views16[g][:, col:col + 128] = blk.astype(views16[g].dtype)


def _merge_views(views4, views16, stage_ref, dst4_ref, dst16_ref):
    slabs, n, _ = dst4_ref.shape
    n4, n16 = n // 4, n // 16
    for j in range(slabs):
        g, lanes = j // 4, 128 * (j % 4)
        dst4, dst16, stage = dst4_ref.at[j], dst16_ref.at[j], stage_ref.at[j]
        for c4 in range(4):
            col = D_GRP * c4 + lanes
            dst4[pl.ds(c4, n4, stride=4), :] = views4[g][:, col:col + 128].astype(F32)
            for c1 in range(4):
                col = D_GRP * (4 * c1 + c4) + lanes
                stage[pl.ds(n4 * c4 + c1, n16, stride=4), :] = views16[g][:, col:col + 128].astype(F32)
        for c4 in range(4):
            dst16[pl.ds(c4, n4, stride=4), :] = stage[n4 * c4:n4 * (c4 + 1), :]


def _slab_group(ref, g):
    return jnp.concatenate([ref[4 * g + p] for p in range(4)], axis=1)


def _mesh_pos():
    return lax.axis_index("x"), lax.axis_index("y"), lax.axis_index("c")


def _flat_index(p):
    return 4 * p[0] + 2 * p[1] + p[2]


def _gather_weights(shards):
    n_arr = len(shards)

    def body(*refs):
        srcs, outs = refs[:n_arr], refs[n_arr:2 * n_arr]
        send_sems, recv_sems, local_sems = refs[2 * n_arr:]
        x, y, c = _mesh_pos()
        me, sibling = (x, y, c), (x, y, 1 - c)
        chips = [(1 - x, y), (x, 1 - y), (1 - x, 1 - y)]

        def copy(arr, k, block, to, own=False):
            dst = outs[arr].at[_flat_index(block)]
            return pltpu.make_async_remote_copy(
                src_ref=srcs[arr] if own else dst, dst_ref=dst,
                send_sem=send_sems.at[arr, k], recv_sem=recv_sems.at[arr, k],
                device_id=to, device_id_type=MESH)

        for arr in range(n_arr):
            mine = pltpu.make_async_copy(srcs[arr], outs[arr].at[_flat_index(me)], local_sems.at[arr])
            mine.start()
            first = [copy(arr, 0, me, sibling, own=True)]
            first += [copy(arr, 1 + j, me, (*chip, c), own=True) for j, chip in enumerate(chips)]
            for cp in first:
                cp.start()
        for arr in range(n_arr):
            passed = [copy(arr, 4 + j, (*chip, c), sibling) for j, chip in enumerate(chips)]
            for j, chip in enumerate(chips):
                copy(arr, 1 + j, (*chip, c), me).wait_recv()
                passed[j].start()
        for arr in range(n_arr):
            copy(arr, 0, sibling, me).wait_recv()
            for j, chip in enumerate(chips):
                copy(arr, 4 + j, (*chip, 1 - c), me).wait_recv()
            for k in range(7):
                copy(arr, k, me, me).wait_send()
            pltpu.make_async_copy(srcs[arr], outs[arr].at[_flat_index(me)], local_sems.at[arr]).wait()

    any_spec = pl.BlockSpec(memory_space=pl.ANY)
    return pl.pallas_call(
        body, name="gather_weights",
        out_shape=tuple(jax.ShapeDtypeStruct((N_DEV,) + s.shape, s.dtype) for s in shards),
        in_specs=[any_spec] * n_arr, out_specs=(any_spec,) * n_arr,
        scratch_shapes=[pltpu.SemaphoreType.DMA((n_arr, 7)), pltpu.SemaphoreType.DMA((n_arr, 7)),
                        pltpu.SemaphoreType.DMA((n_arr,))],
        compiler_params=pltpu.CompilerParams(has_side_effects=True),
    )(*shards)


def _peer_list(x, y, c):
    return [(1 - x if m & 4 else x, 1 - y if m & 2 else y, 1 - c if m & 1 else c) for m in range(1, N_DEV)]


def _exchange_grads(parts, small):
    n_arr = len(parts)

    def body(*refs):
        ins, outs = refs[:n_arr + 1], refs[n_arr + 1:2 * (n_arr + 1)]
        send_sems, recv_sems, local_sems = refs[2 * (n_arr + 1):]
        x, y, c = _mesh_pos()
        me = (x, y, c)
        my_idx = _flat_index(me)
        peers = []
        for m in range(1, N_DEV):
            peers.append((1 - x if m & 4 else x, 1 - y if m & 2 else y, 1 - c if m & 1 else c))

        def src_block(arr, dev):
            return ins[arr] if arr == n_arr else ins[arr].at[_flat_index(dev)]

        def copy(arr, k):
            return pltpu.make_async_remote_copy(
                src_ref=src_block(arr, peers[k]), dst_ref=outs[arr].at[my_idx],
                send_sem=send_sems.at[arr, k], recv_sem=recv_sems.at[arr, k],
                device_id=peers[k], device_id_type=MESH)

        def local(arr):
            return pltpu.make_async_copy(src_block(arr, me), outs[arr].at[my_idx], local_sems.at[arr])

        for arr in range(n_arr + 1):
            local(arr).start()
            for k in range(N_DEV - 1):
                copy(arr, k).start()
        for arr in range(n_arr + 1):
            for k in range(N_DEV - 1):
                cp = copy(arr, k)
                cp.wait_send()
                cp.wait_recv()
            local(arr).wait()

    any_spec = pl.BlockSpec(memory_space=pl.ANY)
    out_shape = tuple(jax.ShapeDtypeStruct(p.shape, p.dtype) for p in parts)
    out_shape += (jax.ShapeDtypeStruct((N_DEV,) + small.shape, small.dtype),)
    return pl.pallas_call(
        body, name="exchange_grads",
        out_shape=out_shape,
        in_specs=[any_spec] * (n_arr + 1), out_specs=(any_spec,) * (n_arr + 1),
        scratch_shapes=[pltpu.SemaphoreType.DMA((n_arr + 1, N_DEV - 1)),
                        pltpu.SemaphoreType.DMA((n_arr + 1, N_DEV - 1)),
                        pltpu.SemaphoreType.DMA((n_arr + 1,))],
        compiler_params=pltpu.CompilerParams(has_side_effects=True),
    )(*parts, small)


def _call_with_gather(body, shards, first_step, mid_step, last_step, *, name, grid, in_specs, out_specs,
                      out_shape, scratch_shapes=()):
    out_specs = tuple(out_specs) if isinstance(out_specs, (tuple, list)) else (out_specs,)
    out_shape = tuple(out_shape) if isinstance(out_shape, (tuple, list)) else (out_shape,)
    n_in, n_out, n_scr, n = len(in_specs), len(out_specs), len(scratch_shapes), len(shards)
    if n == 0:
        return pl.pallas_call(body, name=name, grid=grid, in_specs=list(in_specs), out_specs=out_specs,
                              out_shape=out_shape, scratch_shapes=list(scratch_shapes),
                              compiler_params=_params())

    def full_body(*refs):
        ins, srcs = refs[:n_in], refs[n_in:n_in + n]
        outs, lands = refs[n_in + n:n_in + n + n_out], refs[n_in + n + n_out:n_in + 2 * n + n_out]
        scratch = refs[n_in + 2 * n + n_out:n_in + 2 * n + n_out + n_scr]
        send_sems, recv_sems, local_sems = refs[-3:]
        x, y, c = _mesh_pos()
        me, sibling = (x, y, c), (x, y, 1 - c)
        chips = [(1 - x, y), (x, 1 - y), (1 - x, 1 - y)]

        def copy(a, k, block, to, own=False):
            dst = lands[a].at[_flat_index(block)]
            return pltpu.make_async_remote_copy(
                src_ref=srcs[a] if own else dst, dst_ref=dst,
                send_sem=send_sems.at[a, k], recv_sem=recv_sems.at[a, k],
                device_id=to, device_id_type=MESH)

        def local(a):
            return pltpu.make_async_copy(srcs[a], lands[a].at[_flat_index(me)], local_sems.at[a])

        @pl.when(first_step())
        def _():
            for a in range(n):
                local(a).start()
                copy(a, 0, me, sibling, own=True).start()
                for j, chip in enumerate(chips):
                    copy(a, 1 + j, me, (*chip, c), own=True).start()

        @pl.when(mid_step())
        def _():
            for a in range(n):
                for j, chip in enumerate(chips):
                    copy(a, 1 + j, (*chip, c), me).wait_recv()
                    copy(a, 4 + j, (*chip, c), sibling).start()

        body(*ins, *outs, *scratch)

        @pl.when(last_step())
        def _():
            for a in range(n):
                copy(a, 0, sibling, me).wait_recv()
                for j, chip in enumerate(chips):
                    copy(a, 4 + j, (*chip, 1 - c), me).wait_recv()
                for k in range(N_DEV - 1):
                    copy(a, k, me, me).wait_send()
                local(a).wait()

    any_spec = pl.BlockSpec(memory_space=pl.ANY)
    return pl.pallas_call(
        full_body, name=name, grid=grid,
        in_specs=list(in_specs) + [any_spec] * n,
        out_specs=out_specs + (any_spec,) * n,
        out_shape=out_shape + tuple(jax.ShapeDtypeStruct((N_DEV,) + t.shape, t.dtype) for t in shards),
        scratch_shapes=list(scratch_shapes) + [pltpu.SemaphoreType.DMA((n, N_DEV - 1)),
                                               pltpu.SemaphoreType.DMA((n, N_DEV - 1)),
                                               pltpu.SemaphoreType.DMA((n,))],
        compiler_params=_params(has_side_effects=True),
    )


def _call_with_exchange(body, rides, first_step, last_step, *, name, grid, in_specs, out_specs, out_shape,
                        scratch_shapes=()):
    out_specs = tuple(out_specs) if isinstance(out_specs, (tuple, list)) else (out_specs,)
    out_shape = tuple(out_shape) if isinstance(out_shape, (tuple, list)) else (out_shape,)
    n_in, n_out, n_scr, n = len(in_specs), len(out_specs), len(scratch_shapes), len(rides)
    if n == 0:
        return pl.pallas_call(body, name=name, grid=grid, in_specs=list(in_specs), out_specs=out_specs,
                              out_shape=out_shape, scratch_shapes=list(scratch_shapes),
                              compiler_params=_params())

    def full_body(*refs):
        ins, srcs = refs[:n_in], refs[n_in:n_in + n]
        outs, lands = refs[n_in + n:n_in + n + n_out], refs[n_in + n + n_out:n_in + 2 * n + n_out]
        scratch = refs[n_in + 2 * n + n_out:n_in + 2 * n + n_out + n_scr]
        send_sems, recv_sems, local_sems = refs[-3:]
        x, y, c = _mesh_pos()
        my_idx = _flat_index((x, y, c))
        peers = _peer_list(x, y, c)

        def remote(a, k):
            return pltpu.make_async_remote_copy(
                src_ref=srcs[a].at[_flat_index(peers[k])], dst_ref=lands[a].at[my_idx],
                send_sem=send_sems.at[a, k], recv_sem=recv_sems.at[a, k],
                device_id=peers[k], device_id_type=MESH)

        def local(a):
            return pltpu.make_async_copy(srcs[a].at[my_idx], lands[a].at[my_idx], local_sems.at[a])

        @pl.when(first_step())
        def _():
            for a in range(n):
                local(a).start()
                for k in range(N_DEV - 1):
                    remote(a, k).start()

        body(*ins, *outs, *scratch)

        @pl.when(last_step())
        def _():
            for a in range(n):
                for k in range(N_DEV - 1):
                    cp = remote(a, k)
                    cp.wait_send()
                    cp.wait_recv()
                local(a).wait()

    any_spec = pl.BlockSpec(memory_space=pl.ANY)
    res = pl.pallas_call(
        full_body, name=name, grid=grid,
        in_specs=list(in_specs) + [any_spec] * n,
        out_specs=out_specs + (any_spec,) * n,
        out_shape=out_shape + tuple(jax.ShapeDtypeStruct(t.shape, t.dtype) for t in rides),
        scratch_shapes=list(scratch_shapes) + [pltpu.SemaphoreType.DMA((n, N_DEV - 1)),
                                               pltpu.SemaphoreType.DMA((n, N_DEV - 1)),
                                               pltpu.SemaphoreType.DMA((n,))],
        compiler_params=_params(has_side_effects=True),
    )
    return res


def _head_norm(t, w128, bd):
    ms = _mm_split(t * t, bd) * (1.0 / HEAD_DIM)
    r = lax.rsqrt(ms + EPS)
    return (t * r) * w128, r


def _attn_in(x2, wn1, a_g, cos2, sin2, qnw, knw, bd, shards):
    s_len = x2.shape[0]
    tm = ROW_TILE

    def body(x_ref, wn_ref, w_ref, cos_ref, sin_ref, qnw_ref, knw_ref, bd_ref,
             h1_ref, qraw_ref, kraw_ref, q_ref, k_ref, va_ref, qs_ref, ks_ref, vs_ref,
             q4_ref, k4_ref, v4_ref, q16_ref, k16_ref, v16_ref, proj, slabs, stage, w_full):
        @pl.when(pl.program_id(0) == 0)
        def _():
            for d in range(N_DEV):
                w_full[:, IN_SHARD * d:IN_SHARD * (d + 1)] = w_ref[d]

        xx = x_ref[...]
        r = lax.rsqrt(jnp.mean(xx * xx, axis=-1, keepdims=True) + EPS)
        h = ((xx * r) * wn_ref[...]).astype(BF16)
        h1_ref[...] = h
        proj[...] = _dot(h, w_full[...])
        cos_t, sin_t, bdm = cos_ref[...], sin_ref[...], bd_ref[...]
        for grp, (raw_ref, rope_ref, nw_ref) in enumerate(((qraw_ref, q_ref, qnw_ref),
                                                           (kraw_ref, k_ref, knw_ref))):
            for p in range(4):
                cols = slice(D_GRP * grp + 128 * p, D_GRP * grp + 128 * (p + 1))
                t = proj[:, cols]
                raw_ref[:, 128 * p:128 * (p + 1)] = t
                yn, _ = _head_norm(t, nw_ref[...], bdm)
                roped = yn * cos_t + _swap_halves(yn) * sin_t
                slabs[4 * grp + p] = roped
                rope_ref[:, 128 * p:128 * (p + 1)] = roped.astype(BF16)
        for p in range(4):
            slabs[8 + p] = proj[:, 2 * D_GRP + 128 * p:2 * D_GRP + 128 * (p + 1)]
        for grp, ref in ((2, va_ref), (3, qs_ref), (4, ks_ref), (5, vs_ref)):
            ref[...] = proj[:, D_GRP * grp:D_GRP * (grp + 1)].astype(BF16)
        _split_views(slabs, stage, (q4_ref, k4_ref, v4_ref), (q16_ref, k16_ref, v16_ref))

    row = lambda w: pl.BlockSpec((tm, w), lambda i: (i, 0))
    grp_bf = jax.ShapeDtypeStruct((s_len, D_GRP), BF16)
    grp_f32 = jax.ShapeDtypeStruct((s_len, D_GRP), F32)
    ni = s_len // tm
    return _call_with_gather(
        body, shards, lambda: pl.program_id(0) == 0, lambda: pl.program_id(0) == ni - 2,
        lambda: pl.program_id(0) == ni - 1,
        name="attn_in", grid=(ni,),
        in_specs=[row(D_MODEL), _full((1, D_MODEL)),
                  pl.BlockSpec((N_DEV, D_MODEL, IN_SHARD), lambda i: (0, 0, 0)),
                  row(128), row(128), _full((1, 128)), _full((1, 128)), _full((128, 128))],
        out_specs=(row(D_MODEL),) + (row(D_GRP),) * 8 + (_view_spec(tm, 4),) * 3 + (_view_spec(tm, 16),) * 3,
        out_shape=(jax.ShapeDtypeStruct((s_len, D_MODEL), BF16), grp_f32, grp_f32) + (grp_bf,) * 6
        + (_view_shape(s_len, 4, BF16),) * 3 + (_view_shape(s_len, 16, BF16),) * 3,
        scratch_shapes=[pltpu.VMEM((tm, D_IN), F32), pltpu.VMEM((12, tm, 128), F32), pltpu.VMEM((12, tm, 128), F32),
                        pltpu.VMEM((D_MODEL, D_IN), BF16)],
    )(x2, wn1, a_g, cos2, sin2, qnw, knw, bd, *shards)


def _band_mask(n):
    i = lax.broadcasted_iota(jnp.int32, (2 * BLOCK, 2 * BLOCK), 0) & (BLOCK - 1)
    j = lax.broadcasted_iota(jnp.int32, (2 * BLOCK, 2 * BLOCK), 1)
    dist = i + BLOCK - j
    return (dist >= 0) & (dist <= BLOCK) & ((n - 1) * BLOCK + j >= 0)


def _stack_heads(t2, head0):
    return jnp.concatenate([jnp.where(head0, t2, 0), jnp.where(head0, 0, t2)], axis=0)


def _unstack_heads(t, head0):
    return jnp.where(head0, t[0:BLOCK], t[BLOCK:2 * BLOCK])


def _dil_fwd(qv, kv, vv, r, shards):
    sub_len = qv.shape[0]
    nb = sub_len // BLOCK

    qb = 2 if nb % 2 == 0 else 1

    def body(q_ref, kp_ref, kc_ref, vp_ref, vc_ref, o_ref, lse_ref):
        n = pl.program_id(1)
        lane = lax.broadcasted_iota(jnp.int32, (BLOCK, 128), 1)
        head0 = lane < HEAD_DIM
        units = [(b, slice(128 * p, 128 * (p + 1))) for b in range(qb) for p in range(4)]
        valid = [_band_mask(qb * n + b) for b in range(qb)]
        rows = [slice(BLOCK * b, BLOCK * (b + 1)) for b in range(qb)]

        def keys(prev_ref, cur_ref, b, c):
            before = prev_ref[:, c] if b == 0 else cur_ref[rows[b - 1], c]
            return jnp.concatenate([before, cur_ref[rows[b], c]], axis=0)

        qqs = [_stack_heads(q_ref[rows[b], c] * ATT_SCALE, head0) for b, c in units]
        kks = [keys(kp_ref, kc_ref, b, c) for b, c in units]
        vvs = [keys(vp_ref, vc_ref, b, c) for b, c in units]
        ss = [_dot_nt(qq, kk) for qq, kk in zip(qqs, kks)]
        prs, dens, lses = [], [], []
        for (b, _), s in zip(units, ss):
            s = jnp.where(valid[b], s, NEG)
            m = jnp.max(s, axis=-1, keepdims=True)
            pr = jnp.exp(s - m)
            den = jnp.sum(pr, axis=-1, keepdims=True)
            prs.append(pr.astype(BF16))
            dens.append(den)
            lses.append(m + jnp.log(den))
        pvs = [_dot(pr, vv2) for pr, vv2 in zip(prs, vvs)]
        for (b, c), pv, den, lse in zip(units, pvs, dens, lses):
            o_ref[rows[b], c] = _unstack_heads(pv / den, head0)
            lse_ref[rows[b], c] = _unstack_heads(jnp.broadcast_to(lse, (2 * BLOCK, 128)), head0)

    cur = pl.BlockSpec((qb * BLOCK, D_GRP), lambda c, n: (n, c))
    prev = pl.BlockSpec((BLOCK, D_GRP), lambda c, n: (jnp.maximum(qb * n - 1, 0), c))
    out = jax.ShapeDtypeStruct(qv.shape, F32)
    steps = nb // qb

    def at(t):
        return lambda: pl.program_id(0) * steps + pl.program_id(1) == t

    return _call_with_gather(
        body, shards, at(0), at((2 * r * steps) // 3), at(r * steps - 1),
        name=f"dil_fwd_r{r}", grid=(r, steps),
        in_specs=[cur, prev, cur, prev, cur], out_specs=(cur, cur), out_shape=(out, out),
    )(qv, kv, kv, vv, vv, *shards)


def _dil_bwd(qv, kv, vv, dov, lsev, deltav, r, rides):
    sub_len = qv.shape[0]
    nb = sub_len // BLOCK

    def body(q_ref, kp_ref, kc_ref, vp_ref, vc_ref, do_ref, lse_ref, dl_ref,
             dq_ref, dk_ref, dv_ref, dk_carry, dv_carry):
        n = pl.program_id(1)

        @pl.when(n == 0)
        def _():
            dk_carry[...] = jnp.zeros_like(dk_carry)
            dv_carry[...] = jnp.zeros_like(dv_carry)

        @pl.when(n < nb)
        def _():
            valid = _band_mask(n)
            lane = lax.broadcasted_iota(jnp.int32, (BLOCK, 128), 1)
            head0 = lane < HEAD_DIM
            pairs = [slice(128 * p, 128 * (p + 1)) for p in range(4)]
            qqs = [_stack_heads(q_ref[:, c] * ATT_SCALE, head0) for c in pairs]
            dos = [_stack_heads(do_ref[:, c], head0) for c in pairs]
            kks = [jnp.concatenate([kp_ref[:, c], kc_ref[:, c]], axis=0) for c in pairs]
            vvs = [jnp.concatenate([vp_ref[:, c], vc_ref[:, c]], axis=0) for c in pairs]
            ss = [_dot_nt(qq, kk) for qq, kk in zip(qqs, kks)]
            dps = [_dot_nt(do, vv2) for do, vv2 in zip(dos, vvs)]
            def softmax_terms(p):
                stats = []
                for ref in (lse_ref, dl_ref):
                    t2 = ref[:, pairs[p]]
                    stats.append(jnp.concatenate(
                        [jnp.sum(jnp.where(lane == 0, t2, 0.0), axis=-1, keepdims=True),
                         jnp.sum(jnp.where(lane == HEAD_DIM, t2, 0.0), axis=-1, keepdims=True)], axis=0))
                pr = jnp.where(valid, jnp.exp(jnp.minimum(ss[p] - stats[0], 0.0)), 0.0)
                return pr.astype(BF16), (pr * (dps[p] - stats[1])).astype(BF16)

            terms = [softmax_terms(p) for p in range(4)]
            dqs = [_dot(terms[p][1], kks[p]) for p in range(4)]
            dkks = [_dot_tn(terms[p][1], qqs[p]) for p in range(4)]
            dvvs = [_dot_tn(terms[p][0], dos[p]) for p in range(4)]
            for c, dq, dkk, dvv in zip(pairs, dqs, dkks, dvvs):
                dq_ref[:, c] = _unstack_heads(dq, head0) * ATT_SCALE
                dk_ref[:, c] = dk_carry[:, c] + dkk[:BLOCK]
                dv_ref[:, c] = dv_carry[:, c] + dvv[:BLOCK]
                dk_carry[:, c] = dkk[BLOCK:]
                dv_carry[:, c] = dvv[BLOCK:]

        @pl.when(n == nb)
        def _():
            dk_ref[...] = dk_carry[...]
            dv_ref[...] = dv_carry[...]

    last = nb - 1
    cur = pl.BlockSpec((BLOCK, D_GRP), lambda c, n: (jnp.minimum(n, last), c))
    prev = pl.BlockSpec((BLOCK, D_GRP), lambda c, n: (jnp.clip(n - 1, 0, last), c))
    out = jax.ShapeDtypeStruct(qv.shape, F32)
    return _call_with_exchange(
        body, rides,
        lambda: jnp.logical_and(pl.program_id(0) == 0, pl.program_id(1) == 0),
        lambda: jnp.logical_and(pl.program_id(0) == r - 1, pl.program_id(1) == nb),
        name=f"dil_bwd_r{r}", grid=(r, nb + 1),
        in_specs=[cur, prev, cur, prev, cur, cur, cur, cur],
        out_specs=(cur, prev, prev), out_shape=(out, out, out),
        scratch_shapes=[pltpu.VMEM((BLOCK, D_GRP), F32), pltpu.VMEM((BLOCK, D_GRP), F32)],
    )(qv, kv, kv, vv, vv, dov, lsev, deltav, *rides)


def _sb_fwd(qs, ks, vs, tri_suf, shards):
    s_len = qs.shape[0]
    t = SB_TILE
    nq = s_len // t

    npair = SB_PAIRS

    def body(q_ref, k_ref, v_ref, u_ref, o_ref, c_ref, qq, vt, acc, cf, csave):
        row = lax.broadcasted_iota(jnp.int32, (2 * t, t), 0) & (t - 1)
        col = lax.broadcasted_iota(jnp.int32, (2 * t, t), 1)
        diag_mask = col < row
        lane1 = lax.broadcasted_iota(jnp.int32, (t, 128), 1)
        head0 = lane1 < HEAD_DIM
        lane2 = lax.broadcasted_iota(jnp.int32, (2 * t, 128), 1)
        uu = u_ref[...]
        pr = range(npair)
        cols = [slice(128 * pp, 128 * (pp + 1)) for pp in pr]

        i = pl.program_id(1)

        @pl.when(i == 0)
        def _():
            def transpose_v(j, _):
                rows = pl.ds(pl.multiple_of(j * t, t), t)
                for pp in pr:
                    vt[pp, j] = v_ref[rows, cols[pp]].astype(F32).T.astype(BF16)
                return 0

            lax.fori_loop(0, nq, transpose_v, 0)

        for pp in pr:
            q2 = q_ref[:, cols[pp]] * ATT_SCALE
            qq[pp, 0:t, :] = jnp.where(head0, q2, 0)
            qq[pp, t:2 * t, :] = jnp.where(head0, 0, q2)
        acc[...] = jnp.zeros_like(acc)
        cf[...] = jnp.zeros_like(cf)
        csave[...] = jnp.full(csave.shape, 2.0 * SB_DEAD, F32)

        def tile(kb, diag):
            krows = pl.ds(pl.multiple_of(kb * t, t), t)
            zs = [_dot_nt(qq[pp], k_ref[krows, cols[pp]]) for pp in pr]
            lbk = [_log_sigmoid_pair(z) for z in zs]
            lks = [jnp.where(diag_mask, lk, 0.0) if diag else lk for _, lk in lbk]
            sufs = [_cumsum_mm(lk, uu) for lk in lks]
            carries = [cf[pp] for pp in pr]
            avs = []
            for pp in pr:
                a = jnp.exp(lbk[pp][0] + (sufs[pp] + jnp.concatenate([carries[pp]] * (t // 128), axis=1)))
                avs.append((jnp.where(diag_mask, a, 0.0) if diag else a).astype(BF16))
            pvs = [_dot_nt(vt[pp, kb], avs[pp]) for pp in pr]
            for pp in pr:
                acc[pp] += pvs[pp]
                csave[pp] = jnp.where(lane2 == kb, carries[pp], csave[pp])
                cf[pp] = carries[pp] + jnp.broadcast_to(jnp.sum(lks[pp], axis=-1, keepdims=True), (2 * t, 128))

        tile(i, True)

        def alive():
            return jnp.max(cf[...]) > SB_DEAD

        def k_block(state):
            kb, _ = state
            tile(kb, False)
            return kb - 1, alive()

        lax.while_loop(lambda state: jnp.logical_and(state[0] >= 0, state[1]), k_block, (i - 1, alive()))
        for pp in pr:
            o_ref[:, cols[pp]] = jnp.where(head0, acc[pp, :, 0:t].T, acc[pp, :, t:2 * t].T)
            c_ref[2 * pp] = csave[pp, 0:t, :]
            c_ref[2 * pp + 1] = csave[pp, t:2 * t, :]

    width = 128 * npair
    kv = pl.BlockSpec((s_len, width), lambda p, i: (0, p))
    qo = pl.BlockSpec((t, width), lambda p, i: (i, p))
    steps = 4 // npair

    def at(p, i):
        return lambda: jnp.logical_and(pl.program_id(0) == p, pl.program_id(1) == i)

    return _call_with_gather(
        body, shards, at(0, 0), at(steps - 1, (2 * nq) // 3), at(steps - 1, nq - 1),
        name="sb_fwd", grid=(steps, nq),
        in_specs=[qo, kv, kv, pl.BlockSpec((t, t), lambda p, i: (0, 0))],
        out_specs=(qo, pl.BlockSpec((2 * npair, t, 128), lambda p, i: (p, i, 0))),
        out_shape=(jax.ShapeDtypeStruct((s_len, D_GRP), F32),
                   jax.ShapeDtypeStruct((8, s_len, 128), F32)),
        scratch_shapes=[pltpu.VMEM((npair, 2 * t, 128), BF16), pltpu.VMEM((npair, nq, 128, t), BF16),
                        pltpu.VMEM((npair, 128, 2 * t), F32),
                        pltpu.VMEM((npair, 2 * t, 128), F32), pltpu.VMEM((npair, 2 * t, 128), F32)],
    )(qs, ks, vs, tri_suf, *shards)


def _sb_bwd(qs, ks, vs, dos, csaved, tri_suf, tri_pre, rides):
    s_len = qs.shape[0]
    t = SB_TILE
    nq = s_len // t

    npair = SB_BWD_PAIRS

    def body(q_ref, k_ref, v_ref, do_ref, c_ref, u_ref, p_ref, dq_ref, dk_ref, dv_ref,
             qq, dd, qqt, ddt, kt, dq_acc, dkt, dvt, cg):
        row = lax.broadcasted_iota(jnp.int32, (2 * t, t), 0) & (t - 1)
        col = lax.broadcasted_iota(jnp.int32, (2 * t, t), 1)
        diag_mask = col < row
        lane1 = lax.broadcasted_iota(jnp.int32, (t, 128), 1)
        head0 = lane1 < HEAD_DIM
        lane2 = lax.broadcasted_iota(jnp.int32, (2 * t, 128), 1)
        uu, pm = u_ref[...], p_ref[...]
        pr = range(npair)
        cols = [slice(128 * pp, 128 * (pp + 1)) for pp in pr]
        i = pl.program_id(1)

        @pl.when(i == 0)
        def _():
            dkt[...] = jnp.zeros_like(dkt)
            dvt[...] = jnp.zeros_like(dvt)

            def transpose_k(j, _):
                rows = pl.ds(pl.multiple_of(j * t, t), t)
                for pp in pr:
                    kt[pp, j] = k_ref[rows, cols[pp]].astype(F32).T.astype(BF16)
                return 0

            lax.fori_loop(0, nq, transpose_k, 0)

        for pp in pr:
            q2 = q_ref[:, cols[pp]].astype(F32) * ATT_SCALE
            do2 = do_ref[:, cols[pp]].astype(F32)
            for src, nat, tr in ((q2, qq, qqt), (do2, dd, ddt)):
                stacked = jnp.concatenate([jnp.where(head0, src, 0.0), jnp.where(head0, 0.0, src)], axis=0)
                nat[pp] = stacked.astype(BF16)
                tr[pp] = stacked.T.astype(BF16)
        dq_acc[...] = jnp.zeros_like(dq_acc)
        cg[...] = jnp.zeros_like(cg)

        def tile(kb, diag):
            krows = pl.ds(pl.multiple_of(kb * t, t), t)
            zs = [_dot_nt(qq[pp], k_ref[krows, cols[pp]]) for pp in pr]
            das = [_dot_nt(dd[pp], v_ref[krows, cols[pp]]) for pp in pr]
            lbk = [_log_sigmoid_pair(z) for z in zs]
            lks = [jnp.where(diag_mask, lk, 0.0) if diag else lk for _, lk in lbk]
            sufs = [_cumsum_mm(lk, uu) for lk in lks]
            avs, gs = [], []
            for pp in pr:
                cs = jnp.concatenate([c_ref[2 * pp], c_ref[2 * pp + 1]], axis=0)
                cf = jnp.sum(jnp.where(lane2 == kb, cs, 0.0), axis=-1, keepdims=True)
                a = jnp.exp(lbk[pp][0] + (sufs[pp] + cf))
                a = jnp.where(diag_mask, a, 0.0) if diag else a
                avs.append(a.astype(BF16))
                gs.append(a * das[pp])
            gpres = [_cumsum_mm(g, pm) for g in gs]
            dzs = []
            for pp in pr:
                carry = cg[pp]
                beta = jnp.exp(lbk[pp][0])
                dz = gs[pp] - beta * (gs[pp] + (gpres[pp] + jnp.concatenate([carry] * (t // 128), axis=1)))
                dzs.append((jnp.where(diag_mask, dz, 0.0) if diag else dz).astype(BF16))
                cg[pp] = carry + jnp.broadcast_to(jnp.sum(gs[pp], axis=-1, keepdims=True), (2 * t, 128))
            dqs = [_dot_nt(kt[pp, kb], dzs[pp]) for pp in pr]
            dks = [_dot(qqt[pp], dzs[pp]) for pp in pr]
            dvs = [_dot(ddt[pp], avs[pp]) for pp in pr]
            for pp in pr:
                dq_acc[pp] += dqs[pp]
                dkt[pp, kb] += dks[pp]
                dvt[pp, kb] += dvs[pp]

        def k_block(kb, _):
            tile(kb, False)
            return 0

        col_max = jnp.max(jnp.max(c_ref[...], axis=0), axis=0, keepdims=True)
        lane_row = lax.broadcasted_iota(jnp.int32, (1, 128), 1)
        n_live = jnp.sum(jnp.where(jnp.logical_and(col_max > SB_DEAD, lane_row < i), 1, 0))
        lax.fori_loop(i - n_live, i, k_block, 0)
        tile(i, True)
        for pp in pr:
            dq_ref[:, cols[pp]] = jnp.where(head0, dq_acc[pp, :, 0:t].T, dq_acc[pp, :, t:2 * t].T) * ATT_SCALE

        @pl.when(i == nq - 1)
        def _():
            def untranspose(j, _):
                rows = pl.ds(pl.multiple_of(j * t, t), t)
                for pp in pr:
                    dk_ref[rows, cols[pp]] = dkt[pp, j].T
                    dv_ref[rows, cols[pp]] = dvt[pp, j].T
                return 0

            lax.fori_loop(0, nq, untranspose, 0)

    width = 128 * npair
    kv = pl.BlockSpec((s_len, width), lambda p, i: (0, p))
    qo = pl.BlockSpec((t, width), lambda p, i: (i, p))
    tri = pl.BlockSpec((t, t), lambda p, i: (0, 0))
    out = jax.ShapeDtypeStruct((s_len, D_GRP), F32)
    steps = 4 // npair
    return _call_with_exchange(
        body, rides,
        lambda: jnp.logical_and(pl.program_id(0) == 0, pl.program_id(1) == 0),
        lambda: jnp.logical_and(pl.program_id(0) == steps - 1, pl.program_id(1) == nq - 1),
        name="sb_bwd", grid=(steps, nq),
        in_specs=[qo, kv, kv, qo, pl.BlockSpec((2 * npair, t, 128), lambda p, i: (p, i, 0)), tri, tri],
        out_specs=(qo, kv, kv), out_shape=(out, out, out),
        scratch_shapes=[pltpu.VMEM((npair, 2 * t, 128), BF16), pltpu.VMEM((npair, 2 * t, 128), BF16),
                        pltpu.VMEM((npair, 128, 2 * t), BF16), pltpu.VMEM((npair, 128, 2 * t), BF16),
                        pltpu.VMEM((npair, nq, 128, t), BF16),
                        pltpu.VMEM((npair, 128, 2 * t), F32),
                        pltpu.VMEM((npair, nq, 128, t), F32), pltpu.VMEM((npair, nq, 128, t), F32),
                        pltpu.VMEM((npair, 2 * t, 128), F32)],
    )(qs, ks, vs, dos, csaved, tri_suf, tri_pre, *rides)


def _attn_out(o_b, lse_b, o_sb, x2, wdil, wsb, out_g, shards):
    s_len = x2.shape[0]
    tm = ROW_TILE

    def body(o1_ref, l1_ref, o4_ref, l4_ref, o16_ref, l16_ref, osb_ref, x_ref, wdil_ref, wsb_ref, w_ref,
             odil_ref, lse_ref, lse4_ref, lse16_ref, mixed_ref, x1_ref, stage, nat4, nat16):
        _merge_views((o4_ref, l4_ref), (o16_ref, l16_ref), stage, nat4, nat16)
        os_ = (o1_ref[...], _slab_group(nat4, 0), _slab_group(nat16, 0))
        ls = (l1_ref[...], _slab_group(nat4, 1), _slab_group(nat16, 1))
        mx = jnp.maximum(jnp.maximum(ls[0], ls[1]), ls[2])
        es = [jnp.exp(l - mx) for l in ls]
        den = es[0] + es[1] + es[2]
        o_dil = (es[0] * os_[0] + es[1] * os_[1] + es[2] * os_[2]) / den
        odil_ref[...] = o_dil
        lse = mx + jnp.log(den)
        lse_ref[...] = lse
        for p in range(4):
            nat4[p] = lse[:, 128 * p:128 * (p + 1)]
        _split_views(nat4.at[0:4], stage.at[0:4], (lse4_ref,), (lse16_ref,))
        halves = []
        for t, w_r in ((o_dil, wdil_ref), (osb_ref[...], wsb_ref)):
            r = lax.rsqrt(jnp.mean(t * t, axis=-1, keepdims=True) + EPS)
            halves.append(((t * r) * w_r[...]).astype(BF16))
        mixed = jnp.concatenate(halves, axis=1)
        mixed_ref[...] = mixed
        w = w_ref[...].reshape(D_MODEL, D_MODEL)
        x1_ref[...] = x_ref[...] + _dot(mixed, w)

    row = lambda w: pl.BlockSpec((tm, w), lambda i: (i, 0))
    ni = s_len // tm
    return _call_with_gather(
        body, shards, lambda: pl.program_id(0) == 0, lambda: pl.program_id(0) == ni - 2,
        lambda: pl.program_id(0) == ni - 1,
        name="attn_out", grid=(ni,),
        in_specs=[row(D_GRP)] * 2 + [_view_spec(tm, 4)] * 2 + [_view_spec(tm, 16)] * 2
        + [row(D_GRP), row(D_MODEL), _full((1, D_GRP)), _full((1, D_GRP)), _full((N_DEV, OUT_SHARD, D_MODEL))],
        out_specs=(row(D_GRP), row(D_GRP), _view_spec(tm, 4), _view_spec(tm, 16), row(D_MODEL), row(D_MODEL)),
        out_shape=(jax.ShapeDtypeStruct((s_len, D_GRP), F32), jax.ShapeDtypeStruct((s_len, D_GRP), F32),
                   _view_shape(s_len, 4, F32), _view_shape(s_len, 16, F32),
                   jax.ShapeDtypeStruct((s_len, D_MODEL), BF16), jax.ShapeDtypeStruct((s_len, D_MODEL), F32)),
        scratch_shapes=[pltpu.VMEM((8, tm, 128), F32)] * 3,
    )(o_b[0], lse_b[0], o_b[1], lse_b[1], o_b[2], lse_b[2], o_sb, x2, wdil, wsb, out_g, *shards)


def _two_shards(w_ref):
    return w_ref[...].reshape(FF_BLOCK, D_MODEL)


def _ffn_fwd(x1, wn2, tgt, gate_g, up_g, down_g):
    s_len = x1.shape[0]
    tm = ROW_TILE
    ni = s_len // tm

    def body(x_ref, wn_ref, t_ref, wg_ref, wu_ref, wd_ref, g_ref, u_ref, h2_ref, dy_ref, loss_ref, acc):
        j = pl.program_id(1)

        @pl.when(j == 0)
        def _():
            xx = x_ref[...]
            r = lax.rsqrt(jnp.mean(xx * xx, axis=-1, keepdims=True) + EPS)
            h2_ref[...] = ((xx * r) * wn_ref[...]).astype(BF16)
            acc[...] = jnp.zeros_like(acc)

        h = h2_ref[...]
        g = _dot_nt(h, _two_shards(wg_ref))
        u = _dot_nt(h, _two_shards(wu_ref))
        g_ref[...] = g
        u_ref[...] = u
        act = (g * (1.0 / (1.0 + jnp.exp(-g)))) * u
        acc[...] += _dot(act.astype(BF16), _two_shards(wd_ref))

        @pl.when(j == FF_STEPS - 1)
        def _():
            err = (x_ref[...] + acc[...]) - t_ref[...]
            dy_ref[...] = err * (1.0 / D_MODEL)
            part = 0.5 * jnp.sum(jnp.mean(err * err, axis=-1, keepdims=True))
            loss_ref[...] = jnp.full((8, 128), part, F32)

    row = pl.BlockSpec((tm, D_MODEL), lambda i, j: (i, 0))
    hid = pl.BlockSpec((tm, FF_BLOCK), lambda i, j: (i, j))
    return pl.pallas_call(
        body, name="ffn_fwd", grid=(ni, FF_STEPS),
        in_specs=[row, pl.BlockSpec((1, D_MODEL), lambda i, j: (0, 0)), row,
                  pl.BlockSpec((2, FF_PAD, D_MODEL), lambda i, j: (j, 0, 0)),
                  pl.BlockSpec((2, FF_PAD, D_MODEL), lambda i, j: (j, 0, 0)),
                  pl.BlockSpec((2, FF_PAD, D_MODEL), lambda i, j: (j, 0, 0))],
        out_specs=(hid, hid, row, row, pl.BlockSpec((8, 128), lambda i, j: (i, 0))),
        out_shape=(jax.ShapeDtypeStruct((s_len, N_DEV * FF_PAD), F32),
                   jax.ShapeDtypeStruct((s_len, N_DEV * FF_PAD), F32),
                   jax.ShapeDtypeStruct((s_len, D_MODEL), BF16),
                   jax.ShapeDtypeStruct((s_len, D_MODEL), F32),
                   jax.ShapeDtypeStruct((ni * 8, 128), F32)),
        scratch_shapes=[pltpu.VMEM((tm, D_MODEL), F32)],
        compiler_params=_params(),
    )(x1, wn2, tgt, gate_g, up_g, down_g)


def _ffn_bwd_dx(dy, g, u, gate_g, up_g, down_g):
    s_len = dy.shape[0]
    tm = ROW_TILE

    def body(dy_ref, g_ref, u_ref, wg_ref, wu_ref, wd_ref, dg_ref, du_ref, act_ref, dh_ref, acc):
        j = pl.program_id(1)

        @pl.when(j == 0)
        def _():
            acc[...] = jnp.zeros_like(acc)

        halves = [slice(0, tm // 2), slice(tm // 2, tm)]
        wd, wg, wu = _two_shards(wd_ref), _two_shards(wg_ref), _two_shards(wu_ref)
        das = [_dot_nt(dy_ref[rows, :].astype(BF16), wd) for rows in halves]

        def elementwise(rows, da):
            gg, uu = g_ref[rows, :], u_ref[rows, :]
            sig = 1.0 / (1.0 + jnp.exp(-gg))
            silu = gg * sig
            act_ref[rows, :] = (silu * uu).astype(BF16)
            du = (da * silu).astype(BF16)
            dg = (da * uu * (sig * (1.0 + gg * (1.0 - sig)))).astype(BF16)
            du_ref[rows, :] = du
            dg_ref[rows, :] = dg
            return dg, du

        dg0, du0 = elementwise(halves[0], das[0])
        acc[halves[0], :] += _dot(dg0, wg) + _dot(du0, wu)
        dg1, du1 = elementwise(halves[1], das[1])
        acc[halves[1], :] += _dot(dg1, wg) + _dot(du1, wu)

        @pl.when(j == FF_STEPS - 1)
        def _():
            dh_ref[...] = acc[...]

    row = pl.BlockSpec((tm, D_MODEL), lambda i, j: (i, 0))
    hid = pl.BlockSpec((tm, FF_BLOCK), lambda i, j: (i, j))
    hid_bf = jax.ShapeDtypeStruct((s_len, N_DEV * FF_PAD), BF16)
    return pl.pallas_call(
        body, name="ffn_bwd_dx", grid=(s_len // tm, FF_STEPS),
        in_specs=[row, hid, hid,
                  pl.BlockSpec((2, FF_PAD, D_MODEL), lambda i, j: (j, 0, 0)),
                  pl.BlockSpec((2, FF_PAD, D_MODEL), lambda i, j: (j, 0, 0)),
                  pl.BlockSpec((2, FF_PAD, D_MODEL), lambda i, j: (j, 0, 0))],
        out_specs=(hid, hid, hid, row),
        out_shape=(hid_bf, hid_bf, hid_bf, jax.ShapeDtypeStruct((s_len, D_MODEL), F32)),
        scratch_shapes=[pltpu.VMEM((tm, D_MODEL), F32)],
        compiler_params=_params(),
    )(dy, g, u, gate_g, up_g, down_g)


def _ffn_bwd_dw(h2, dy, dg, du, act):
    s_len = h2.shape[0]
    tm = DW_ROW_TILE
    ni = s_len // tm

    half = FF_PAD // 2

    def body(h_ref, dy_ref, dg_ref, du_ref, act_ref, dwg_ref, dwu_ref, dwd_lo_ref, dwd_hi_ref, ag, au, ad):
        i = pl.program_id(1)

        @pl.when(i == 0)
        def _():
            ag[...] = jnp.zeros_like(ag)
            au[...] = jnp.zeros_like(au)
            ad[...] = jnp.zeros_like(ad)

        h = h_ref[...]
        ag[...] += _dot_tn(dg_ref[...], h)
        au[...] += _dot_tn(du_ref[...], h)
        ad[...] += _dot_tn(act_ref[...], dy_ref[...].astype(BF16))

        @pl.when(i == ni - 1)
        def _():
            for acc_ref, out_ref in ((ag, dwg_ref), (au, dwu_ref)):
                out_ref[...] = acc_ref[...].astype(BF16).reshape(2, FF_PAD, D_MODEL)
            for dev in range(2):
                dwd_lo_ref[dev] = ad[FF_PAD * dev:FF_PAD * dev + half, :].astype(BF16)
                dwd_hi_ref[dev] = ad[FF_PAD * dev + half:FF_PAD * (dev + 1), :].astype(BF16)

    row = pl.BlockSpec((tm, D_MODEL), lambda j, i: (i, 0))
    hid = pl.BlockSpec((tm, FF_BLOCK), lambda j, i: (i, j))
    row_w = pl.BlockSpec((2, FF_PAD, D_MODEL), lambda j, i: (j, 0, 0))
    half_w = pl.BlockSpec((2, half, D_MODEL), lambda j, i: (j, 0, 0))
    grad = jax.ShapeDtypeStruct((N_DEV, FF_PAD, D_MODEL), BF16)
    half_grad = jax.ShapeDtypeStruct((N_DEV, half, D_MODEL), BF16)
    return pl.pallas_call(
        body, name="ffn_bwd_dw", grid=(FF_STEPS, ni),
        in_specs=[row, row, hid, hid, hid], out_specs=(row_w, row_w, half_w, half_w),
        out_shape=(grad, grad, half_grad, half_grad),
        scratch_shapes=[pltpu.VMEM((FF_BLOCK, D_MODEL), F32)] * 3,
        compiler_params=_params(),
    )(h2, dy, dg, du, act)


def _rms_bwd(dy, t, w):
    r = lax.rsqrt(jnp.mean(t * t, axis=-1, keepdims=True) + EPS)
    gw = dy * w
    dt = r * (gw - t * ((r * r) * jnp.mean(gw * t, axis=-1, keepdims=True)))
    return dt, dy * t * r


def _attn_out_bwd(dy, dh2, x1, wn2, b_g, mixed, o_dil, o_sb, wdil, wsb, bd512):
    s_len = dy.shape[0]
    tm = ROW_TILE
    ni = s_len // tm

    def body(dy_ref, dh_ref, x1_ref, wn_ref, w_ref, mixed_ref, odil_ref, osb_ref, wdil_ref, wsb_ref, bd_ref,
             dx1_ref, dodil_ref, delta_ref, dosb_ref, dwout_ref, dwn_ref, dwdil_ref, dwsb_ref,
             do4_ref, dl4_ref, do16_ref, dl16_ref, wacc, both, stage):
        i = pl.program_id(0)

        @pl.when(i == 0)
        def _():
            wacc[...] = jnp.zeros_like(wacc)
            dwn_ref[...] = jnp.zeros_like(dwn_ref)
            dwdil_ref[...] = jnp.zeros_like(dwdil_ref)
            dwsb_ref[...] = jnp.zeros_like(dwsb_ref)

        dnorm, dw_rows = _rms_bwd(dh_ref[...], x1_ref[...], wn_ref[...])
        dx1 = dy_ref[...] + dnorm
        dx1_ref[...] = dx1
        dwn_ref[...] += jnp.sum(dw_rows, axis=0, keepdims=True)
        dx1b = dx1.astype(BF16)
        w = w_ref[...].reshape(D_MODEL, D_MODEL)
        dmixed = _dot_nt(dx1b, w)
        wacc[...] += _dot_tn(mixed_ref[...], dx1b)
        o_dil = odil_ref[...]
        d_odil, dw_rows = _rms_bwd(dmixed[:, :D_GRP], o_dil, wdil_ref[...])
        dwdil_ref[...] += jnp.sum(dw_rows, axis=0, keepdims=True)
        dodil_ref[...] = d_odil.astype(BF16)
        delta = _mm_split(d_odil * o_dil, bd_ref[...])
        delta_ref[...] = delta
        for p in range(4):
            both[p] = d_odil[:, 128 * p:128 * (p + 1)]
            both[4 + p] = delta[:, 128 * p:128 * (p + 1)]
        _split_views(both, stage, (do4_ref, dl4_ref), (do16_ref, dl16_ref))
        d_osb, dw_rows = _rms_bwd(dmixed[:, D_GRP:], osb_ref[...], wsb_ref[...])
        dwsb_ref[...] += jnp.sum(dw_rows, axis=0, keepdims=True)
        dosb_ref[...] = d_osb.astype(BF16)

        @pl.when(i == ni - 1)
        def _():
            dwout_ref[...] = wacc[...].astype(BF16).reshape(N_DEV, OUT_SHARD, D_MODEL)

    row = lambda w: pl.BlockSpec((tm, w), lambda i: (i, 0))
    return pl.pallas_call(
        body, name="attn_out_bwd", grid=(ni,),
        in_specs=[row(D_MODEL), row(D_MODEL), row(D_MODEL), _full((1, D_MODEL)),
                  _full((N_DEV, OUT_SHARD, D_MODEL)),
                  row(D_MODEL), row(D_GRP), row(D_GRP), _full((1, D_GRP)), _full((1, D_GRP)),
                  _full((D_GRP, D_GRP))],
        out_specs=(row(D_MODEL), row(D_GRP), row(D_GRP), row(D_GRP),
                   _full((N_DEV, OUT_SHARD, D_MODEL)), _full((1, D_MODEL)), _full((1, D_GRP)), _full((1, D_GRP)),
                   _view_spec(tm, 4), _view_spec(tm, 4), _view_spec(tm, 16), _view_spec(tm, 16)),
        out_shape=(jax.ShapeDtypeStruct((s_len, D_MODEL), F32), jax.ShapeDtypeStruct((s_len, D_GRP), BF16),
                   jax.ShapeDtypeStruct((s_len, D_GRP), F32), jax.ShapeDtypeStruct((s_len, D_GRP), BF16),
                   jax.ShapeDtypeStruct((N_DEV, OUT_SHARD, D_MODEL), BF16),
                   jax.ShapeDtypeStruct((1, D_MODEL), F32), jax.ShapeDtypeStruct((1, D_GRP), F32),
                   jax.ShapeDtypeStruct((1, D_GRP), F32),
                   _view_shape(s_len, 4, BF16), _view_shape(s_len, 4, F32),
                   _view_shape(s_len, 16, BF16), _view_shape(s_len, 16, F32)),
        scratch_shapes=[pltpu.VMEM((D_MODEL, D_MODEL), F32), pltpu.VMEM((8, tm, 128), F32),
                        pltpu.VMEM((8, tm, 128), F32)],
        compiler_params=_params(),
    )(dy, dh2, x1, wn2, b_g, mixed, o_dil, o_sb, wdil, wsb, bd512)


def _qkv_bwd(dq_b, dk_b, dv_b, dqs, dks, dvs, qraw, kraw, cos2, sin2, qnw, knw, bd, rides):
    s_len = qraw.shape[0]
    tm = ROW_TILE
    ni = s_len // tm

    def body(dq1, dk1, dv1, dq4, dk4, dv4, dq16, dk16, dv16, dqs_ref, dks_ref, dvs_ref,
             qraw_ref, kraw_ref, cos_ref, sin_ref, qnw_ref, knw_ref, bd_ref,
             dproj_ref, dqn_ref, dkn_ref, stage, nat4, nat16):
        i = pl.program_id(0)

        @pl.when(i == 0)
        def _():
            dqn_ref[...] = jnp.zeros_like(dqn_ref)
            dkn_ref[...] = jnp.zeros_like(dkn_ref)

        _merge_views((dq4, dk4, dv4), (dq16, dk16, dv16), stage, nat4, nat16)
        cos_t, sin_t, bdm = cos_ref[...], sin_ref[...], bd_ref[...]
        for grp, (part1, raw_ref, nw_ref, dn_ref) in enumerate(((dq1, qraw_ref, qnw_ref, dqn_ref),
                                                                (dk1, kraw_ref, knw_ref, dkn_ref))):
            dn_acc = 0.0
            for p in range(4):
                cols = slice(128 * p, 128 * (p + 1))
                d_rope = part1[:, cols] + nat4[4 * grp + p] + nat16[4 * grp + p]
                d_norm = d_rope * cos_t + _swap_halves(d_rope * sin_t)
                t = raw_ref[:, cols]
                w = nw_ref[...]
                r = lax.rsqrt(_mm_split(t * t, bdm) * (1.0 / HEAD_DIM) + EPS)
                gw = d_norm * w
                corr = _mm_split(gw * t, bdm) * (1.0 / HEAD_DIM)
                dt = r * (gw - t * ((r * r) * corr))
                dn_acc = dn_acc + jnp.sum(d_norm * t * r, axis=0, keepdims=True)
                dproj_ref[:, D_GRP * grp + 128 * p:D_GRP * grp + 128 * (p + 1)] = dt.astype(BF16)
            dn_ref[...] += dn_acc
        dproj_ref[:, 2 * D_GRP:3 * D_GRP] = (dv1[...] + _slab_group(nat4, 2) + _slab_group(nat16, 2)).astype(BF16)
        dproj_ref[:, 3 * D_GRP:4 * D_GRP] = dqs_ref[...].astype(BF16)
        dproj_ref[:, 4 * D_GRP:5 * D_GRP] = dks_ref[...].astype(BF16)
        dproj_ref[:, 5 * D_GRP:6 * D_GRP] = dvs_ref[...].astype(BF16)

    row = lambda w: pl.BlockSpec((tm, w), lambda i: (i, 0))
    return _call_with_exchange(
        body, rides, lambda: pl.program_id(0) == 0, lambda: pl.program_id(0) == ni - 1,
        name="qkv_bwd", grid=(ni,),
        in_specs=[row(D_GRP)] * 3 + [_view_spec(tm, 4)] * 3 + [_view_spec(tm, 16)] * 3 + [row(D_GRP)] * 5
        + [row(128), row(128), _full((1, 128)), _full((1, 128)), _full((128, 128))],
        out_specs=(row(D_IN), _full((1, 128)), _full((1, 128))),
        out_shape=(jax.ShapeDtypeStruct((s_len, D_IN), BF16), jax.ShapeDtypeStruct((1, 128), F32),
                   jax.ShapeDtypeStruct((1, 128), F32)),
        scratch_shapes=[pltpu.VMEM((12, tm, 128), F32)] * 3,
    )(dq_b[0], dk_b[0], dv_b[0], dq_b[1], dk_b[1], dv_b[1], dq_b[2], dk_b[2], dv_b[2],
      dqs, dks, dvs, qraw, kraw, cos2, sin2, qnw, knw, bd, *rides)


def _in_bwd_dx(dproj, a_g, x2, dx1, wn1, rides):
    s_len = x2.shape[0]
    tm = ROW_TILE
    ni = s_len // tm

    def body(dp_ref, w_ref, x_ref, dx1_ref, wn_ref, gx_ref, dwn_ref, w_full):
        i = pl.program_id(0)

        @pl.when(i == 0)
        def _():
            dwn_ref[...] = jnp.zeros_like(dwn_ref)
            for d in range(N_DEV):
                w_full[:, IN_SHARD * d:IN_SHARD * (d + 1)] = w_ref[d]

        dh = _dot_nt(dp_ref[...], w_full[...])
        dnorm, dw_rows = _rms_bwd(dh, x_ref[...], wn_ref[...])
        gx_ref[...] = dx1_ref[...] + dnorm
        dwn_ref[...] += jnp.sum(dw_rows, axis=0, keepdims=True)

    row = lambda w: pl.BlockSpec((tm, w), lambda i: (i, 0))
    return _call_with_exchange(
        body, rides, lambda: pl.program_id(0) == 0, lambda: pl.program_id(0) == ni - 1,
        name="in_bwd_dx", grid=(ni,),
        in_specs=[row(D_IN), pl.BlockSpec((N_DEV, D_MODEL, IN_SHARD), lambda i: (0, 0, 0)),
                  row(D_MODEL), row(D_MODEL), _full((1, D_MODEL))],
        out_specs=(row(D_MODEL), _full((1, D_MODEL))),
        out_shape=(jax.ShapeDtypeStruct((s_len, D_MODEL), F32), jax.ShapeDtypeStruct((1, D_MODEL), F32)),
        scratch_shapes=[pltpu.VMEM((D_MODEL, D_IN), BF16)],
    )(dproj, a_g, x2, dx1, wn1, *rides)


def _in_bwd_dw(h1, dproj):
    s_len = h1.shape[0]
    tm = DW_ROW_TILE
    ni = s_len // tm

    half_d = D_MODEL // 2

    def body(h_ref, dp_ref, lo_ref, hi_ref, acc):
        i = pl.program_id(1)

        @pl.when(i == 0)
        def _():
            acc[...] = jnp.zeros_like(acc)

        acc[...] += _dot_tn(h_ref[...], dp_ref[...])

        @pl.when(i == ni - 1)
        def _():
            for dev in range(2):
                cols = slice(IN_SHARD * dev, IN_SHARD * (dev + 1))
                lo_ref[dev] = acc[0:half_d, cols].astype(BF16)
                hi_ref[dev] = acc[half_d:D_MODEL, cols].astype(BF16)

    half_w = pl.BlockSpec((2, half_d, IN_SHARD), lambda d, i: (d, 0, 0))
    half_grad = jax.ShapeDtypeStruct((N_DEV, half_d, IN_SHARD), BF16)
    return pl.pallas_call(
        body, name="in_bwd_dw", grid=(N_DEV // 2, ni),
        in_specs=[pl.BlockSpec((tm, D_MODEL), lambda d, i: (i, 0)),
                  pl.BlockSpec((tm, 2 * IN_SHARD), lambda d, i: (i, d))],
        out_specs=(half_w, half_w),
        out_shape=(half_grad, half_grad),
        scratch_shapes=[pltpu.VMEM((D_MODEL, 2 * IN_SHARD), F32)],
        compiler_params=_params(),
    )(h1, dproj)


def _adamw(recv, w, m, v, recv_hi=None):
    rows, cols = w.shape
    tr = next((t for t in (128, 32) if rows % t == 0), rows)
    recvs = [recv] if recv_hi is None else [recv, recv_hi]
    lo_tiles = recv.shape[1] // tr

    def body(*refs):
        p_refs = refs[:len(recvs)]
        w_ref, m_ref, v_ref, g_ref, d_ref, nm_ref, nv_ref = refs[len(recvs):]

        def slot(s):
            if len(p_refs) == 1:
                return p_refs[0][s].astype(F32)
            return jnp.where(pl.program_id(0) < lo_tiles, p_refs[0][s], p_refs[1][s]).astype(F32)

        g = slot(0)
        for s in range(1, N_DEV):
            g = g + slot(s)
        m_new = ADAM_B1 * m_ref[...] + (1.0 - ADAM_B1) * g
        v_new = ADAM_B2 * v_ref[...] + (1.0 - ADAM_B2) * (g * g)
        m_hat = m_new / (1.0 - ADAM_B1 ** ADAM_STEP)
        v_hat = v_new / (1.0 - ADAM_B2 ** ADAM_STEP)
        g_ref[...] = g
        d_ref[...] = -ADAM_LR * (m_hat / (jnp.sqrt(v_hat) + ADAM_EPS) + ADAM_WD * w_ref[...])
        nm_ref[...] = m_new
        nv_ref[...] = v_new

    blk = pl.BlockSpec((tr, cols), lambda i: (i, 0))
    out = jax.ShapeDtypeStruct((rows, cols), F32)
    return pl.pallas_call(
        body, name=f"adamw_{rows}x{cols}", grid=(rows // tr,),
        in_specs=([pl.BlockSpec((N_DEV, tr, cols), lambda i: (0, i, 0))] if recv_hi is None else
                  [pl.BlockSpec((N_DEV, tr, cols), lambda i: (0, jnp.minimum(i, lo_tiles - 1), 0)),
                   pl.BlockSpec((N_DEV, tr, cols), lambda i: (0, jnp.maximum(i - lo_tiles, 0), 0))])
        + [blk, blk, blk],
        out_specs=(blk,) * 4, out_shape=(out,) * 4,
        compiler_params=_params(),
    )(*recvs, w, m, v)


def _adamw_many(items, rides):
    tr = 32
    tiles = [w.shape[0] // tr for _, w, _, _ in items]
    starts = [sum(tiles[:k]) for k in range(len(items))]
    total = sum(tiles)
    n_items = len(items)

    def body(*refs):
        i = pl.program_id(0)
        in_refs, out_refs = refs[:4 * n_items], refs[4 * n_items:]
        for k in range(n_items):
            def update(k=k):
                p_ref, w_ref, m_ref, v_ref = in_refs[4 * k:4 * k + 4]
                g_ref, d_ref, nm_ref, nv_ref = out_refs[4 * k:4 * k + 4]
                g = p_ref[0].astype(F32)
                for s in range(1, N_DEV):
                    g = g + p_ref[s].astype(F32)
                m_new = ADAM_B1 * m_ref[...] + (1.0 - ADAM_B1) * g
                v_new = ADAM_B2 * v_ref[...] + (1.0 - ADAM_B2) * (g * g)
                m_hat = m_new / (1.0 - ADAM_B1 ** ADAM_STEP)
                v_hat = v_new / (1.0 - ADAM_B2 ** ADAM_STEP)
                g_ref[...] = g
                d_ref[...] = -ADAM_LR * (m_hat / (jnp.sqrt(v_hat) + ADAM_EPS) + ADAM_WD * w_ref[...])
                nm_ref[...] = m_new
                nv_ref[...] = v_new

            pl.when(jnp.logical_and(i >= starts[k], i < starts[k] + tiles[k]))(update)

    in_specs, out_specs, out_shape, args = [], [], [], []
    for k, (recv, w, m, v) in enumerate(items):
        tile_of = functools.partial(lambda i, s0, nk: jnp.clip(i - s0, 0, nk - 1), s0=starts[k], nk=tiles[k])
        blk = pl.BlockSpec((tr, D_MODEL), functools.partial(lambda i, t: (t(i), 0), t=tile_of))
        in_specs += [pl.BlockSpec((N_DEV, tr, D_MODEL), functools.partial(lambda i, t: (0, t(i), 0), t=tile_of)),
                     blk, blk, blk]
        out_specs += [blk] * 4
        out_shape += [jax.ShapeDtypeStruct(w.shape, F32)] * 4
        args += [recv, w, m, v]
    res = _call_with_exchange(
        body, rides, lambda: pl.program_id(0) == 0, lambda: pl.program_id(0) == total - 1,
        name="adamw_many", grid=(total,), in_specs=in_specs, out_specs=out_specs, out_shape=out_shape,
    )(*args, *rides)
    return [tuple(res[4 * k:4 * k + 4]) for k in range(n_items)], list(res[4 * n_items:])


def _rope_tables(s_len):
    pos = np.arange(s_len, dtype=np.float32)
    inv_freq = np.float32(ROPE_THETA) ** (-np.arange(0, HEAD_DIM, 2, dtype=np.float32) / np.float32(HEAD_DIM))
    ang = (pos[:, None] * inv_freq[None, :]).astype(np.float32)
    cos, sin = np.cos(ang), np.sin(ang)
    cos2 = np.concatenate([cos, cos, cos, cos], axis=1)
    sin2 = np.concatenate([-sin, sin, -sin, sin], axis=1)
    return jnp.asarray(cos2, F32), jnp.asarray(sin2, F32)


def _block_diag_ones(n):
    i = jnp.arange(n)
    return (i[:, None] // HEAD_DIM == i[None, :] // HEAD_DIM).astype(BF16)


def _pad_cols(t):
    return jnp.pad(t, ((0, 0), (0, FF_PAD - FF_SHARD)))


def _pad_rows(t):
    return jnp.pad(t, ((0, FF_PAD - FF_SHARD), (0, 0)))


LOSS_ROW = 26


def _pack_small(n1, n2, ndil, nsb, nq, nk, scalar=None):
    pad = lambda t: jnp.pad(t.reshape(1, -1), ((0, 0), (0, 128 - t.size)))
    last = jnp.zeros((1, 128), F32) if scalar is None else pad(scalar)
    rows = [n1.reshape(8, 128), n2.reshape(8, 128), ndil.reshape(4, 128), nsb.reshape(4, 128),
            pad(nq), pad(nk), last, jnp.zeros((5, 128), F32)]
    return jnp.concatenate(rows, axis=0)


SMALL_ROWS = ((0, 8), (8, 8), (16, 4), (20, 4), (24, 1), (25, 1))


def _adamw_small(recv, ws, ms, vs):
    n_vec = len(ws)

    def body(*refs):
        p_ref = refs[0]
        w_refs, m_refs, v_refs = (refs[1 + n_vec * k:1 + n_vec * (k + 1)] for k in range(3))
        sum_ref = refs[1 + 3 * n_vec]
        out_refs = refs[2 + 3 * n_vec:]
        total = p_ref[0]
        for s in range(1, N_DEV):
            total = total + p_ref[s]
        sum_ref[...] = total
        for i, (row0, n_rows) in enumerate(SMALL_ROWS):
            g_ref, d_ref, nm_ref, nv_ref = out_refs[4 * i:4 * i + 4]
            width = w_refs[i].shape[1]
            for k in range(n_rows):
                n = min(128, width - 128 * k)
                lanes = slice(128 * k, 128 * k + n)
                g = sum_ref[row0 + k:row0 + k + 1, :n]
                m_new = ADAM_B1 * m_refs[i][:, lanes] + (1.0 - ADAM_B1) * g
                v_new = ADAM_B2 * v_refs[i][:, lanes] + (1.0 - ADAM_B2) * (g * g)
                m_hat = m_new / (1.0 - ADAM_B1 ** ADAM_STEP)
                v_hat = v_new / (1.0 - ADAM_B2 ** ADAM_STEP)
                g_ref[:, lanes] = g
                d_ref[:, lanes] = -ADAM_LR * (m_hat / (jnp.sqrt(v_hat) + ADAM_EPS) + ADAM_WD * w_refs[i][:, lanes])
                nm_ref[:, lanes] = m_new
                nv_ref[:, lanes] = v_new

    vec_specs = [_full(w.shape) for w in ws]
    vec_shapes = [jax.ShapeDtypeStruct(w.shape, F32) for w in ws]
    res = pl.pallas_call(
        body, name="adamw_small", grid=(1,),
        in_specs=[_full(recv.shape)] + vec_specs * 3,
        out_specs=tuple([_full((32, 128))] + [s for s in vec_specs for _ in range(4)]),
        out_shape=tuple([jax.ShapeDtypeStruct((32, 128), F32)] + [s for s in vec_shapes for _ in range(4)]),
        compiler_params=_params(),
    )(recv, *ws, *ms, *vs)
    return res[0], [tuple(res[1 + 4 * i:5 + 4 * i]) for i in range(n_vec)]


def kernel(x, attn_norm_w, w_in, q_norm_w, k_norm_w, dil_out_norm_w, sb_out_norm_w, w_out, ffn_norm_w, w_gate, w_up, w_down, loss_target, m_attn_norm_w, m_w_in, m_q_norm_w, m_k_norm_w, m_dil_out_norm_w, m_sb_out_norm_w, m_w_out, m_ffn_norm_w, m_w_gate, m_w_up, m_w_down, v_attn_norm_w, v_w_in, v_q_norm_w, v_k_norm_w, v_dil_out_norm_w, v_sb_out_norm_w, v_w_out, v_ffn_norm_w, v_w_gate, v_w_up, v_w_down):
    s_len = x.shape[1]
    x2, tgt = x[0], loss_target[0]

    (a_g,) = _gather_weights([w_in[0].astype(BF16)])
    gate_loc = _pad_cols(w_gate[0]).T.astype(BF16)
    up_loc = _pad_cols(w_up[0]).T.astype(BF16)
    down_loc = _pad_rows(w_down[0]).astype(BF16)
    out_loc = w_out[0].astype(BF16)

    cos2, sin2 = _rope_tables(s_len)
    bd128, bd512 = _block_diag_ones(128), _block_diag_ones(D_GRP)
    idx = jnp.arange(SB_TILE)
    tri_suf = (idx[:, None] > idx[None, :]).astype(BF16)
    tri_pre = (idx[:, None] < idx[None, :]).astype(BF16)
    qnw2 = jnp.concatenate([q_norm_w, q_norm_w], axis=1)
    knw2 = jnp.concatenate([k_norm_w, k_norm_w], axis=1)

    (h1, qraw, kraw, q, k, va, qs, ks, vs, q4, k4, v4, q16, k16, v16,
     gate_g) = _attn_in(x2, attn_norm_w, a_g, cos2, sin2, qnw2, knw2, bd128, shards=[gate_loc])
    qkv_views = {1: (q, k, va), 4: (q4, k4, v4), 16: (q16, k16, v16)}
    fwd_riders = {1: [out_loc], 4: [], 16: []}
    o_b, lse_b, gathered = [], [], {}
    for r in DILATIONS:
        o, lse, *gathered[r] = _dil_fwd(*qkv_views[r], r, shards=fwd_riders[r])
        o_b.append(o)
        lse_b.append(lse)
    (out_g,) = gathered[1]
    o_sb, c_sb, up_g = _sb_fwd(qs, ks, vs, tri_suf, shards=[up_loc])
    o_dil, lse_tot, lse4, lse16, mixed, x1, down_g = _attn_out(
        o_b, lse_b, o_sb, x2, dil_out_norm_w, sb_out_norm_w, out_g, shards=[down_loc])
    g, u, h2, dy, loss_parts = _ffn_fwd(x1, ffn_norm_w, tgt, gate_g, up_g, down_g)
    loss_local = jnp.sum(loss_parts[::8, 0])

    dg, du, act, dh2 = _ffn_bwd_dx(dy, g, u, gate_g, up_g, down_g)
    (dx1, do_dil, delta, do_sb, dwout, dn2, dndil, dnsb, do4, dl4, do16, dl16) = _attn_out_bwd(
        dy, dh2, x1, ffn_norm_w, out_g, mixed, o_dil, o_sb, dil_out_norm_w, sb_out_norm_w, bd512)
    dwg, dwu, dwd_lo, dwd_hi = _ffn_bwd_dw(h2, dy, dg, du, act)
    dqs, dks, dvs, r_gate = _sb_bwd(qs, ks, vs, do_sb, c_sb, tri_suf, tri_pre, rides=[dwg])
    cot_views = {1: (do_dil, lse_tot, delta), 4: (do4, lse4, dl4), 16: (do16, lse16, dl16)}
    riders = {1: [dwd_lo], 4: [dwd_hi], 16: [dwu]}
    dq_b, dk_b, dv_b, landed = [], [], [], {}
    for r in DILATIONS:
        dq, dk, dv, *landed[r] = _dil_bwd(*qkv_views[r], *cot_views[r], r, rides=riders[r])
        dq_b.append(dq)
        dk_b.append(dk)
        dv_b.append(dv)
    (r_down_lo,), (r_down_hi,), (r_up,) = landed[1], landed[4], landed[16]
    dproj, dqn2, dkn2, r_out = _qkv_bwd(dq_b, dk_b, dv_b, dqs, dks, dvs, qraw, kraw, cos2, sin2, qnw2, knw2,
                                        bd128, rides=[dwout])
    dwin_lo, dwin_hi = _in_bwd_dw(h1, dproj)
    grad_x, dn1, r_in_lo = _in_bwd_dx(dproj, a_g, x2, dx1, attn_norm_w, rides=[dwin_lo])
    dqn = dqn2[:, :HEAD_DIM] + dqn2[:, HEAD_DIM:]
    dkn = dkn2[:, :HEAD_DIM] + dkn2[:, HEAD_DIM:]

    small = _pack_small(dn1, dn2, dndil, dnsb, dqn, dkn, loss_local)
    (r_small,) = _exchange_grads([], small)
    (new_gate, new_up, new_out), (r_in_hi,) = _adamw_many(
        [(r_gate, w_gate[0].T, m_w_gate[0].T, v_w_gate[0].T), (r_up, w_up[0].T, m_w_up[0].T, v_w_up[0].T),
         (r_out, w_out[0], m_w_out[0], v_w_out[0])], rides=[dwin_hi])
    big = {
        "w_in": _adamw(r_in_lo, w_in[0], m_w_in[0], v_w_in[0], recv_hi=r_in_hi),
        "w_gate": tuple(t.T for t in new_gate),
        "w_up": tuple(t.T for t in new_up),
        "w_down": _adamw(r_down_lo, w_down[0], m_w_down[0], v_w_down[0], recv_hi=r_down_hi),
        "w_out": new_out,
    }
    small_sum, small_res = _adamw_small(
        r_small,
        (attn_norm_w, ffn_norm_w, dil_out_norm_w, sb_out_norm_w, q_norm_w, k_norm_w),
        (m_attn_norm_w, m_ffn_norm_w, m_dil_out_norm_w, m_sb_out_norm_w, m_q_norm_w, m_k_norm_w),
        (v_attn_norm_w, v_ffn_norm_w, v_dil_out_norm_w, v_sb_out_norm_w, v_q_norm_w, v_k_norm_w))
    loss = small_sum[LOSS_ROW, 0]
    small_out = [[res[kind] for res in small_res] for kind in range(4)]
    names = ["attn_norm_w", "w_in", "q_norm_w", "k_norm_w", "dil_out_norm_w", "sb_out_norm_w", "w_out",
             "ffn_norm_w", "w_gate", "w_up", "w_down"]
    small_pos = {"attn_norm_w": 0, "ffn_norm_w": 1, "dil_out_norm_w": 2, "sb_out_norm_w": 3,
                 "q_norm_w": 4, "k_norm_w": 5}
    outs = [loss, grad_x[None]]
    for kind in range(4):
        for name in names:
            if name in small_pos:
                outs.append(small_out[kind][small_pos[name]])
            else:
                outs.append(big[name][kind][None])
    return tuple(outs)
```

```python
import functools

import jax
import jax.numpy as jnp
import numpy as np
from jax import lax
from jax.experimental import pallas as pl
from jax.experimental.pallas import tpu as pltpu

F32 = jnp.float32
BF16 = jnp.bfloat16

N_DEV = 8
D_MODEL = 1024
HEAD_DIM = 64
D_GRP = 512
D_IN = 6 * D_GRP
IN_SHARD = D_IN // N_DEV
FF_SHARD = 352
FF_PAD = 384
FF_BLOCK = 2 * FF_PAD
FF_STEPS = N_DEV // 2
OUT_SHARD = D_MODEL // N_DEV
BLOCK = 128
DILATIONS = (1, 4, 16)
ROPE_THETA = 10000.0
EPS = 1e-6
ATT_SCALE = HEAD_DIM ** -0.5
NEG = -1e30

ADAM_LR = 0.001
ADAM_B1 = 0.9
ADAM_B2 = 0.999
ADAM_EPS = 1e-08
ADAM_WD = 0.01
ADAM_STEP = 10

SB_TILE = 256
SB_DEAD = -104.0
SB_PAIRS = 4
SB_BWD_PAIRS = 2
ROW_TILE = 512
DW_ROW_TILE = 1024
VMEM_LIMIT = 56 * 1024 * 1024
MESH = pl.DeviceIdType.MESH


def _dot(a, b):
    return jnp.dot(a, b, preferred_element_type=F32)


def _dot_nt(a, b):
    return lax.dot_general(a, b, (((1,), (1,)), ((), ())), preferred_element_type=F32)


def _dot_tn(a, b):
    return lax.dot_general(a, b, (((0,), (0,)), ((), ())), preferred_element_type=F32)


def _mm_split(t, m):
    hi = t.astype(BF16)
    lo = (t - hi.astype(F32)).astype(BF16)
    return _dot(hi, m) + _dot(lo, m)


def _params(**kw):
    return pltpu.CompilerParams(vmem_limit_bytes=VMEM_LIMIT, **kw)


def _full(shape):
    nd = len(shape)
    return pl.BlockSpec(shape, lambda *_: (0,) * nd)


def _view_shape(s_len, r, dtype):
    return jax.ShapeDtypeStruct((s_len // r, r * D_GRP), dtype)


def _view_spec(tm, r):
    return pl.BlockSpec((tm // r, r * D_GRP), lambda i: (i, 0))


def _swap_halves(t):
    lane = lax.broadcasted_iota(jnp.int32, t.shape, 1)
    first = (lane & 32) == 0
    return jnp.where(first, pltpu.roll(t, 96, 1), pltpu.roll(t, 32, 1))


def _log_sigmoid_pair(z):
    neg_abs = lax.bitcast_convert_type(lax.bitcast_convert_type(z, jnp.uint32) | jnp.uint32(0x80000000), F32)
    lb = jnp.minimum(z, 0.0) - jnp.log(1.0 + jnp.exp(neg_abs))
    return lb, lb - z


def _cumsum_mm(t, tri):
    return _dot(t.astype(BF16), tri)


def _split_views(src_ref, stage_ref, views4, views16):
    slabs, n, _ = src_ref.shape
    n4, n16 = n // 4, n // 16
    for j in range(slabs):
        g, lanes = j // 4, 128 * (j % 4)
        src, stage = src_ref.at[j], stage_ref.at[j]
        for c4 in range(4):
            blk = src[pl.ds(c4, n4, stride=4), :]
            stage[n4 * c4:n4 * (c4 + 1), :] = blk
            col = D_GRP * c4 + lanes
            views4[g][:, col:col + 128] = blk.astype(views4[g].dtype)
        for c4 in range(4):
            for c1 in range(4):
                blk = stage[pl.ds(n4 * c4 + c1, n16, stride=4), :]
                col = D_GRP * (4 * c1 + c4) + lanes
                views16[g][:, col:col + 128] = blk.astype(views16[g].dtype)


def _merge_views(views4, views16, stage_ref, dst4_ref, dst16_ref):
    slabs, n, _ = dst4_ref.shape
    n4, n16 = n // 4, n // 16
    for j in range(slabs):
        g, lanes = j // 4, 128 * (j % 4)
        dst4, dst16, stage = dst4_ref.at[j], dst16_ref.at[j], stage_ref.at[j]
        for c4 in range(4):
            col = D_GRP * c4 + lanes
            dst4[pl.ds(c4, n4, stride=4), :] = views4[g][:, col:col + 128].astype(F32)
            for c1 in range(4):
                col = D_GRP * (4 * c1 + c4) + lanes
                stage[pl.ds(n4 * c4 + c1, n16, stride=4), :] = views16[g][:, col:col + 128].astype(F32)
        for c4 in range(4):
            dst16[pl.ds(c4, n4, stride=4), :] = stage[n4 * c4:n4 * (c4 + 1), :]


def _slab_group(ref, g):
    return jnp.concatenate([ref[4 * g + p] for p in range(4)], axis=1)


def _mesh_pos():
    return lax.axis_index("x"), lax.axis_index("y"), lax.axis_index("c")


def _flat_index(p):
    return 4 * p[0] + 2 * p[1] + p[2]


def _gather_weights(shards):
    n_arr = len(shards)

    def body(*refs):
        srcs, outs = refs[:n_arr], refs[n_arr:2 * n_arr]
        send_sems, recv_sems, local_sems = refs[2 * n_arr:]
        x, y, c = _mesh_pos()
        me, sibling = (x, y, c), (x, y, 1 - c)
        chips = [(1 - x, y), (x, 1 - y), (1 - x, 1 - y)]

        def copy(arr, k, block, to, own=False):
            dst = outs[arr].at[_flat_index(block)]
            return pltpu.make_async_remote_copy(
                src_ref=srcs[arr] if own else dst, dst_ref=dst,
                send_sem=send_sems.at[arr, k], recv_sem=recv_sems.at[arr, k],
                device_id=to, device_id_type=MESH)

        for arr in range(n_arr):
            mine = pltpu.make_async_copy(srcs[arr], outs[arr].at[_flat_index(me)], local_sems.at[arr])
            mine.start()
            first = [copy(arr, 0, me, sibling, own=True)]
            first += [copy(arr, 1 + j, me, (*chip, c), own=True) for j, chip in enumerate(chips)]
            for cp in first:
                cp.start()
        for arr in range(n_arr):
            passed = [copy(arr, 4 + j, (*chip, c), sibling) for j, chip in enumerate(chips)]
            for j, chip in enumerate(chips):
                copy(arr, 1 + j, (*chip, c), me).wait_recv()
                passed[j].start()
        for arr in range(n_arr):
            copy(arr, 0, sibling, me).wait_recv()
            for j, chip in enumerate(chips):
                copy(arr, 4 + j, (*chip, 1 - c), me).wait_recv()
            for k in range(7):
                copy(arr, k, me, me).wait_send()
            pltpu.make_async_copy(srcs[arr], outs[arr].at[_flat_index(me)], local_sems.at[arr]).wait()

    any_spec = pl.BlockSpec(memory_space=pl.ANY)
    return pl.pallas_call(
        body, name="gather_weights",
        out_shape=tuple(jax.ShapeDtypeStruct((N_DEV,) + s.shape, s.dtype) for s in shards),
        in_specs=[any_spec] * n_arr, out_specs=(any_spec,) * n_arr,
        scratch_shapes=[pltpu.SemaphoreType.DMA((n_arr, 7)), pltpu.SemaphoreType.DMA((n_arr, 7)),
                        pltpu.SemaphoreType.DMA((n_arr,))],
        compiler_params=pltpu.CompilerParams(has_side_effects=True),
    )(*shards)


def _peer_list(x, y, c):
    return [(1 - x if m & 4 else x, 1 - y if m & 2 else y, 1 - c if m & 1 else c) for m in range(1, N_DEV)]


def _exchange_grads(parts, small):
    n_arr = len(parts)

    def body(*refs):
        ins, outs = refs[:n_arr + 1], refs[n_arr + 1:2 * (n_arr + 1)]
        send_sems, recv_sems, local_sems = refs[2 * (n_arr + 1):]
        x, y, c = _mesh_pos()
        me = (x, y, c)
        my_idx = _flat_index(me)
        peers = []
        for m in range(1, N_DEV):
            peers.append((1 - x if m & 4 else x, 1 - y if m & 2 else y, 1 - c if m & 1 else c))

        def src_block(arr, dev):
            return ins[arr] if arr == n_arr else ins[arr].at[_flat_index(dev)]

        def copy(arr, k):
            return pltpu.make_async_remote_copy(
                src_ref=src_block(arr, peers[k]), dst_ref=outs[arr].at[my_idx],
                send_sem=send_sems.at[arr, k], recv_sem=recv_sems.at[arr, k],
                device_id=peers[k], device_id_type=MESH)

        def local(arr):
            return pltpu.make_async_copy(src_block(arr, me), outs[arr].at[my_idx], local_sems.at[arr])

        for arr in range(n_arr + 1):
            local(arr).start()
            for k in range(N_DEV - 1):
                copy(arr, k).start()
        for arr in range(n_arr + 1):
            for k in range(N_DEV - 1):
                cp = copy(arr, k)
                cp.wait_send()
                cp.wait_recv()
            local(arr).wait()

    any_spec = pl.BlockSpec(memory_space=pl.ANY)
    out_shape = tuple(jax.ShapeDtypeStruct(p.shape, p.dtype) for p in parts)
    out_shape += (jax.ShapeDtypeStruct((N_DEV,) + small.shape, small.dtype),)
    return pl.pallas_call(
        body, name="exchange_grads",
        out_shape=out_shape,
        in_specs=[any_spec] * (n_arr + 1), out_specs=(any_spec,) * (n_arr + 1),
        scratch_shapes=[pltpu.SemaphoreType.DMA((n_arr + 1, N_DEV - 1)),
                        pltpu.SemaphoreType.DMA((n_arr + 1, N_DEV - 1)),
                        pltpu.SemaphoreType.DMA((n_arr + 1,))],
        compiler_params=pltpu.CompilerParams(has_side_effects=True),
    )(*parts, small)


def _call_with_gather(body, shards, first_step, mid_step, last_step, *, name, grid, in_specs, out_specs,
                      out_shape, scratch_shapes=()):
    out_specs = tuple(out_specs) if isinstance(out_specs, (tuple, list)) else (out_specs,)
    out_shape = tuple(out_shape) if isinstance(out_shape, (tuple, list)) else (out_shape,)
    n_in, n_out, n_scr, n = len(in_specs), len(out_specs), len(scratch_shapes), len(shards)
    if n == 0:
        return pl.pallas_call(body, name=name, grid=grid, in_specs=list(in_specs), out_specs=out_specs,
                              out_shape=out_shape, scratch_shapes=list(scratch_shapes),
                              compiler_params=_params())

    def full_body(*refs):
        ins, srcs = refs[:n_in], refs[n_in:n_in + n]
        outs, lands = refs[n_in + n:n_in + n + n_out], refs[n_in + n + n_out:n_in + 2 * n + n_out]
        scratch = refs[n_in + 2 * n + n_out:n_in + 2 * n + n_out + n_scr]
        send_sems, recv_sems, local_sems = refs[-3:]
        x, y, c = _mesh_pos()
        me, sibling = (x, y, c), (x, y, 1 - c)
        chips = [(1 - x, y), (x, 1 - y), (1 - x, 1 - y)]

        def copy(a, k, block, to, own=False):
            dst = lands[a].at[_flat_index(block)]
            return pltpu.make_async_remote_copy(
                src_ref=srcs[a] if own else dst, dst_ref=dst,
                send_sem=send_sems.at[a, k], recv_sem=recv_sems.at[a, k],
                device_id=to, device_id_type=MESH)

        def local(a):
            return pltpu.make_async_copy(srcs[a], lands[a].at[_flat_index(me)], local_sems.at[a])

        @pl.when(first_step())
        def _():
            for a in range(n):
                local(a).start()
                copy(a, 0, me, sibling, own=True).start()
                for j, chip in enumerate(chips):
                    copy(a, 1 + j, me, (*chip, c), own=True).start()

        @pl.when(mid_step())
        def _():
            for a in range(n):
                for j, chip in enumerate(chips):
                    copy(a, 1 + j, (*chip, c), me).wait_recv()
                    copy(a, 4 + j, (*chip, c), sibling).start()

        body(*ins, *outs, *scratch)

        @pl.when(last_step())
        def _():
            for a in range(n):
                copy(a, 0, sibling, me).wait_recv()
                for j, chip in enumerate(chips):
                    copy(a, 4 + j, (*chip, 1 - c), me).wait_recv()
                for k in range(N_DEV - 1):
                    copy(a, k, me, me).wait_send()
                local(a).wait()

    any_spec = pl.BlockSpec(memory_space=pl.ANY)
    return pl.pallas_call(
        full_body, name=name, grid=grid,
        in_specs=list(in_specs) + [any_spec] * n,
        out_specs=out_specs + (any_spec,) * n,
        out_shape=out_shape + tuple(jax.ShapeDtypeStruct((N_DEV,) + t.shape, t.dtype) for t in shards),
        scratch_shapes=list(scratch_shapes) + [pltpu.SemaphoreType.DMA((n, N_DEV - 1)),
                                               pltpu.SemaphoreType.DMA((n, N_DEV - 1)),
                                               pltpu.SemaphoreType.DMA((n,))],
        compiler_params=_params(has_side_effects=True),
    )


def _call_with_exchange(body, rides, first_step, last_step, *, name, grid, in_specs, out_specs, out_shape,
                        scratch_shapes=()):
    out_specs = tuple(out_specs) if isinstance(out_specs, (tuple, list)) else (out_specs,)
    out_shape = tuple(out_shape) if isinstance(out_shape, (tuple, list)) else (out_shape,)
    n_in, n_out, n_scr, n = len(in_specs), len(out_specs), len(scratch_shapes), len(rides)
    if n == 0:
        return pl.pallas_call(body, name=name, grid=grid, in_specs=list(in_specs), out_specs=out_specs,
                              out_shape=out_shape, scratch_shapes=list(scratch_shapes),
                              compiler_params=_params())

    def full_body(*refs):
        ins, srcs = refs[:n_in], refs[n_in:n_in + n]
        outs, lands = refs[n_in + n:n_in + n + n_out], refs[n_in + n + n_out:n_in + 2 * n + n_out]
        scratch = refs[n_in + 2 * n + n_out:n_in + 2 * n + n_out + n_scr]
        send_sems, recv_sems, local_sems = refs[-3:]
        x, y, c = _mesh_pos()
        my_idx = _flat_index((x, y, c))
        peers = _peer_list(x, y, c)

        def remote(a, k):
            return pltpu.make_async_remote_copy(
                src_ref=srcs[a].at[_flat_index(peers[k])], dst_ref=lands[a].at[my_idx],
                send_sem=send_sems.at[a, k], recv_sem=recv_sems.at[a, k],
                device_id=peers[k], device_id_type=MESH)

        def local(a):
            return pltpu.make_async_copy(srcs[a].at[my_idx], lands[a].at[my_idx], local_sems.at[a])

        @pl.when(first_step())
        def _():
            for a in range(n):
                local(a).start()
                for k in range(N_DEV - 1):
                    remote(a, k).start()

        body(*ins, *outs, *scratch)

        @pl.when(last_step())
        def _():
            for a in range(n):
                for k in range(N_DEV - 1):
                    cp = remote(a, k)
                    cp.wait_send()
                    cp.wait_recv()
                local(a).wait()

    any_spec = pl.BlockSpec(memory_space=pl.ANY)
    res = pl.pallas_call(
        full_body, name=name, grid=grid,
        in_specs=list(in_specs) + [any_spec] * n,
        out_specs=out_specs + (any_spec,) * n,
        out_shape=out_shape + tuple(jax.ShapeDtypeStruct(t.shape, t.dtype) for t in rides),
        scratch_shapes=list(scratch_shapes) + [pltpu.SemaphoreType.DMA((n, N_DEV - 1)),
                                               pltpu.SemaphoreType.DMA((n, N_DEV - 1)),
                                               pltpu.SemaphoreType.DMA((n,))],
        compiler_params=_params(has_side_effects=True),
    )
    return res


def _head_norm(t, w128, bd):
    ms = _mm_split(t * t, bd) * (1.0 / HEAD_DIM)
    r = lax.rsqrt(ms + EPS)
    return (t * r) * w128, r


def _attn_in(x2, wn1, a_g, cos2, sin2, qnw, knw, bd, shards):
    s_len = x2.shape[0]
    tm = ROW_TILE

    def body(x_ref, wn_ref, w_ref, cos_ref, sin_ref, qnw_ref, knw_ref, bd_ref,
             h1_ref, qraw_ref, kraw_ref, q_ref, k_ref, va_ref, qs_ref, ks_ref, vs_ref,
             q4_ref, k4_ref, v4_ref, q16_ref, k16_ref, v16_ref, proj, slabs, stage, w_full):
        @pl.when(pl.program_id(0) == 0)
        def _():
            for d in range(N_DEV):
                w_full[:, IN_SHARD * d:IN_SHARD * (d + 1)] = w_ref[d]

        xx = x_ref[...]
        r = lax.rsqrt(jnp.mean(xx * xx, axis=-1, keepdims=True) + EPS)
        h = ((xx * r) * wn_ref[...]).astype(BF16)
        h1_ref[...] = h
        proj[...] = _dot(h, w_full[...])
        cos_t, sin_t, bdm = cos_ref[...], sin_ref[...], bd_ref[...]
        for grp, (raw_ref, rope_ref, nw_ref) in enumerate(((qraw_ref, q_ref, qnw_ref),
                                                           (kraw_ref, k_ref, knw_ref))):
            for p in range(4):
                cols = slice(D_GRP * grp + 128 * p, D_GRP * grp + 128 * (p + 1))
                t = proj[:, cols]
                raw_ref[:, 128 * p:128 * (p + 1)] = t
                yn, _ = _head_norm(t, nw_ref[...], bdm)
                roped = yn * cos_t + _swap_halves(yn) * sin_t
                slabs[4 * grp + p] = roped
                rope_ref[:, 128 * p:128 * (p + 1)] = roped.astype(BF16)
        for p in range(4):
            slabs[8 + p] = proj[:, 2 * D_GRP + 128 * p:2 * D_GRP + 128 * (p + 1)]
        for grp, ref in ((2, va_ref), (3, qs_ref), (4, ks_ref), (5, vs_ref)):
            ref[...] = proj[:, D_GRP * grp:D_GRP * (grp + 1)].astype(BF16)
        _split_views(slabs, stage, (q4_ref, k4_ref, v4_ref), (q16_ref, k16_ref, v16_ref))

    row = lambda w: pl.BlockSpec((tm, w), lambda i: (i, 0))
    grp_bf = jax.ShapeDtypeStruct((s_len, D_GRP), BF16)
    grp_f32 = jax.ShapeDtypeStruct((s_len, D_GRP), F32)
    ni = s_len // tm
    return _call_with_gather(
        body, shards, lambda: pl.program_id(0) == 0, lambda: pl.program_id(0) == ni - 2,
        lambda: pl.program_id(0) == ni - 1,
        name="attn_in", grid=(ni,),
        in_specs=[row(D_MODEL), _full((1, D_MODEL)),
                  pl.BlockSpec((N_DEV, D_MODEL, IN_SHARD), lambda i: (0, 0, 0)),
                  row(128), row(128), _full((1, 128)), _full((1, 128)), _full((128, 128))],
        out_specs=(row(D_MODEL),) + (row(D_GRP),) * 8 + (_view_spec(tm, 4),) * 3 + (_view_spec(tm, 16),) * 3,
        out_shape=(jax.ShapeDtypeStruct((s_len, D_MODEL), BF16), grp_f32, grp_f32) + (grp_bf,) * 6
        + (_view_shape(s_len, 4, BF16),) * 3 + (_view_shape(s_len, 16, BF16),) * 3,
        scratch_shapes=[pltpu.VMEM((tm, D_IN), F32), pltpu.VMEM((12, tm, 128), F32), pltpu.VMEM((12, tm, 128), F32),
                        pltpu.VMEM((D_MODEL, D_IN), BF16)],
    )(x2, wn1, a_g, cos2, sin2, qnw, knw, bd, *shards)


def _band_mask(n):
    i = lax.broadcasted_iota(jnp.int32, (2 * BLOCK, 2 * BLOCK), 0) & (BLOCK - 1)
    j = lax.broadcasted_iota(jnp.int32, (2 * BLOCK, 2 * BLOCK), 1)
    dist = i + BLOCK - j
    return (dist >= 0) & (dist <= BLOCK) & ((n - 1) * BLOCK + j >= 0)


def _stack_heads(t2, head0):
    return jnp.concatenate([jnp.where(head0, t2, 0), jnp.where(head0, 0, t2)], axis=0)


def _unstack_heads(t, head0):
    return jnp.where(head0, t[0:BLOCK], t[BLOCK:2 * BLOCK])


def _dil_fwd(qv, kv, vv, r, shards):
    sub_len = qv.shape[0]
    nb = sub_len // BLOCK

    qb = 2 if nb % 2 == 0 else 1

    def body(q_ref, kp_ref, kc_ref, vp_ref, vc_ref, o_ref, lse_ref):
        n = pl.program_id(1)
        lane = lax.broadcasted_iota(jnp.int32, (BLOCK, 128), 1)
        head0 = lane < HEAD_DIM
        units = [(b, slice(128 * p, 128 * (p + 1))) for b in range(qb) for p in range(4)]
        valid = [_band_mask(qb * n + b) for b in range(qb)]
        rows = [slice(BLOCK * b, BLOCK * (b + 1)) for b in range(qb)]

        def keys(prev_ref, cur_ref, b, c):
            before = prev_ref[:, c] if b == 0 else cur_ref[rows[b - 1], c]
            return jnp.concatenate([before, cur_ref[rows[b], c]], axis=0)

        qqs = [_stack_heads(q_ref[rows[b], c] * ATT_SCALE, head0) for b, c in units]
        kks = [keys(kp_ref, kc_ref, b, c) for b, c in units]
        vvs = [keys(vp_ref, vc_ref, b, c) for b, c in units]
        ss = [_dot_nt(qq, kk) for qq, kk in zip(qqs, kks)]
        prs, dens, lses = [], [], []
        for (b, _), s in zip(units, ss):
            s = jnp.where(valid[b], s, NEG)
            m = jnp.max(s, axis=-1, keepdims=True)
            pr = jnp.exp(s - m)
            den = jnp.sum(pr, axis=-1, keepdims=True)
            prs.append(pr.astype(BF16))
            dens.append(den)
            lses.append(m + jnp.log(den))
        pvs = [_dot(pr, vv2) for pr, vv2 in zip(prs, vvs)]
        for (b, c), pv, den, lse in zip(units, pvs, dens, lses):
            o_ref[rows[b], c] = _unstack_heads(pv / den, head0)
            lse_ref[rows[b], c] = _unstack_heads(jnp.broadcast_to(lse, (2 * BLOCK, 128)), head0)

    cur = pl.BlockSpec((qb * BLOCK, D_GRP), lambda c, n: (n, c))
    prev = pl.BlockSpec((BLOCK, D_GRP), lambda c, n: (jnp.maximum(qb * n - 1, 0), c))
    out = jax.ShapeDtypeStruct(qv.shape, F32)
    steps = nb // qb

    def at(t):
        return lambda: pl.program_id(0) * steps + pl.program_id(1) == t

    return _call_with_gather(
        body, shards, at(0), at((2 * r * steps) // 3), at(r * steps - 1),
        name=f"dil_fwd_r{r}", grid=(r, steps),
        in_specs=[cur, prev, cur, prev, cur], out_specs=(cur, cur), out_shape=(out, out),
    )(qv, kv, kv, vv, vv, *shards)


def _dil_bwd(qv, kv, vv, dov, lsev, deltav, r, rides):
    sub_len = qv.shape[0]
    nb = sub_len // BLOCK

    def body(q_ref, kp_ref, kc_ref, vp_ref, vc_ref, do_ref, lse_ref, dl_ref,
             dq_ref, dk_ref, dv_ref, dk_carry, dv_carry):
        n = pl.program_id(1)

        @pl.when(n == 0)
        def _():
            dk_carry[...] = jnp.zeros_like(dk_carry)
            dv_carry[...] = jnp.zeros_like(dv_carry)

        @pl.when(n < nb)
        def _():
            valid = _band_mask(n)
            lane = lax.broadcasted_iota(jnp.int32, (BLOCK, 128), 1)
            head0 = lane < HEAD_DIM
            pairs = [slice(128 * p, 128 * (p + 1)) for p in range(4)]
            qqs = [_stack_heads(q_ref[:, c] * ATT_SCALE, head0) for c in pairs]
            dos = [_stack_heads(do_ref[:, c], head0) for c in pairs]
            kks = [jnp.concatenate([kp_ref[:, c], kc_ref[:, c]], axis=0) for c in pairs]
            vvs = [jnp.concatenate([vp_ref[:, c], vc_ref[:, c]], axis=0) for c in pairs]
            ss = [_dot_nt(qq, kk) for qq, kk in zip(qqs, kks)]
            dps = [_dot_nt(do, vv2) for do, vv2 in zip(dos, vvs)]
            def softmax_terms(p):
                stats = []
                for ref in (lse_ref, dl_ref):
                    t2 = ref[:, pairs[p]]
                    stats.append(jnp.concatenate(
                        [jnp.sum(jnp.where(lane == 0, t2, 0.0), axis=-1, keepdims=True),
                         jnp.sum(jnp.where(lane == HEAD_DIM, t2, 0.0), axis=-1, keepdims=True)], axis=0))
                pr = jnp.where(valid, jnp.exp(jnp.minimum(ss[p] - stats[0], 0.0)), 0.0)
                return pr.astype(BF16), (pr * (dps[p] - stats[1])).astype(BF16)

            terms = [softmax_terms(p) for p in range(4)]
            dqs = [_dot(terms[p][1], kks[p]) for p in range(4)]
            dkks = [_dot_tn(terms[p][1], qqs[p]) for p in range(4)]
            dvvs = [_dot_tn(terms[p][0], dos[p]) for p in range(4)]
            for c, dq, dkk, dvv in zip(pairs, dqs, dkks, dvvs):
                dq_ref[:, c] = _unstack_heads(dq, head0) * ATT_SCALE
                dk_ref[:, c] = dk_carry[:, c] + dkk[:BLOCK]
                dv_ref[:, c] = dv_carry[:, c] + dvv[:BLOCK]
                dk_carry[:, c] = dkk[BLOCK:]
                dv_carry[:, c] = dvv[BLOCK:]

        @pl.when(n == nb)
        def _():
            dk_ref[...] = dk_carry[...]
            dv_ref[...] = dv_carry[...]

    last = nb - 1
    cur = pl.BlockSpec((BLOCK, D_GRP), lambda c, n: (jnp.minimum(n, last), c))
    prev = pl.BlockSpec((BLOCK, D_GRP), lambda c, n: (jnp.clip(n - 1, 0, last), c))
    out = jax.ShapeDtypeStruct(qv.shape, F32)
    return _call_with_exchange(
        body, rides,
        lambda: jnp.logical_and(pl.program_id(0) == 0, pl.program_id(1) == 0),
        lambda: jnp.logical_and(pl.program_id(0) == r - 1, pl.program_id(1) == nb),
        name=f"dil_bwd_r{r}", grid=(r, nb + 1),
        in_specs=[cur, prev, cur, prev, cur, cur, cur, cur],
        out_specs=(cur, prev, prev), out_shape=(out, out, out),
        scratch_shapes=[pltpu.VMEM((BLOCK, D_GRP), F32), pltpu.VMEM((BLOCK, D_GRP), F32)],
    )(qv, kv, kv, vv, vv, dov, lsev, deltav, *rides)


def _sb_fwd(qs, ks, vs, tri_suf, shards):
    s_len = qs.shape[0]
    t = SB_TILE
    nq = s_len // t

    npair = SB_PAIRS

    def body(q_ref, k_ref, v_ref, u_ref, o_ref, c_ref, qq, vt, acc, cf, csave):
        row = lax.broadcasted_iota(jnp.int32, (2 * t, t), 0) & (t - 1)
        col = lax.broadcasted_iota(jnp.int32, (2 * t, t), 1)
        diag_mask = col < row
        lane1 = lax.broadcasted_iota(jnp.int32, (t, 128), 1)
        head0 = lane1 < HEAD_DIM
        lane2 = lax.broadcasted_iota(jnp.int32, (2 * t, 128), 1)
        uu = u_ref[...]
        pr = range(npair)
        cols = [slice(128 * pp, 128 * (pp + 1)) for pp in pr]

        i = pl.program_id(1)

        @pl.when(i == 0)
        def _():
            def transpose_v(j, _):
                rows = pl.ds(pl.multiple_of(j * t, t), t)
                for pp in pr:
                    vt[pp, j] = v_ref[rows, cols[pp]].astype(F32).T.astype(BF16)
                return 0

            lax.fori_loop(0, nq, transpose_v, 0)

        for pp in pr:
            q2 = q_ref[:, cols[pp]] * ATT_SCALE
            qq[pp, 0:t, :] = jnp.where(head0, q2, 0)
            qq[pp, t:2 * t, :] = jnp.where(head0, 0, q2)
        acc[...] = jnp.zeros_like(acc)
        cf[...] = jnp.zeros_like(cf)
        csave[...] = jnp.full(csave.shape, 2.0 * SB_DEAD, F32)

        def tile(kb, diag):
            krows = pl.ds(pl.multiple_of(kb * t, t), t)
            zs = [_dot_nt(qq[pp], k_ref[krows, cols[pp]]) for pp in pr]
            lbk = [_log_sigmoid_pair(z) for z in zs]
            lks = [jnp.where(diag_mask, lk, 0.0) if diag else lk for _, lk in lbk]
            sufs = [_cumsum_mm(lk, uu) for lk in lks]
            carries = [cf[pp] for pp in pr]
            avs = []
            for pp in pr:
                a = jnp.exp(lbk[pp][0] + (sufs[pp] + jnp.concatenate([carries[pp]] * (t // 128), axis=1)))
                avs.append((jnp.where(diag_mask, a, 0.0) if diag else a).astype(BF16))
            pvs = [_dot_nt(vt[pp, kb], avs[pp]) for pp in pr]
            for pp in pr:
                acc[pp] += pvs[pp]
                csave[pp] = jnp.where(lane2 == kb, carries[pp], csave[pp])
                cf[pp] = carries[pp] + jnp.broadcast_to(jnp.sum(lks[pp], axis=-1, keepdims=True), (2 * t, 128))

        tile(i, True)

        def alive():
            return jnp.max(cf[...]) > SB_DEAD

        def k_block(state):
            kb, _ = state
            tile(kb, False)
            return kb - 1, alive()

        lax.while_loop(lambda state: jnp.logical_and(state[0] >= 0, state[1]), k_block, (i - 1, alive()))
        for pp in pr:
            o_ref[:, cols[pp]] = jnp.where(head0, acc[pp, :, 0:t].T, acc[pp, :, t:2 * t].T)
            c_ref[2 * pp] = csave[pp, 0:t, :]
            c_ref[2 * pp + 1] = csave[pp, t:2 * t, :]

    width = 128 * npair
    kv = pl.BlockSpec((s_len, width), lambda p, i: (0, p))
    qo = pl.BlockSpec((t, width), lambda p, i: (i, p))
    steps = 4 // npair

    def at(p, i):
        return lambda: jnp.logical_and(pl.program_id(0) == p, pl.program_id(1) == i)

    return _call_with_gather(
        body, shards, at(0, 0), at(steps - 1, (2 * nq) // 3), at(steps - 1, nq - 1),
        name="sb_fwd", grid=(steps, nq),
        in_specs=[qo, kv, kv, pl.BlockSpec((t, t), lambda p, i: (0, 0))],
        out_specs=(qo, pl.BlockSpec((2 * npair, t, 128), lambda p, i: (p, i, 0))),
        out_shape=(jax.ShapeDtypeStruct((s_len, D_GRP), F32),
                   jax.ShapeDtypeStruct((8, s_len, 128), F32)),
        scratch_shapes=[pltpu.VMEM((npair, 2 * t, 128), BF16), pltpu.VMEM((npair, nq, 128, t), BF16),
                        pltpu.VMEM((npair, 128, 2 * t), F32),
                        pltpu.VMEM((npair, 2 * t, 128), F32), pltpu.VMEM((npair, 2 * t, 128), F32)],
    )(qs, ks, vs, tri_suf, *shards)


def _sb_bwd(qs, ks, vs, dos, csaved, tri_suf, tri_pre, rides):
    s_len = qs.shape[0]
    t = SB_TILE
    nq = s_len // t

    npair = SB_BWD_PAIRS

    def body(q_ref, k_ref, v_ref, do_ref, c_ref, u_ref, p_ref, dq_ref, dk_ref, dv_ref,
             qq, dd, qqt, ddt, kt, dq_acc, dkt, dvt, cg):
        row = lax.broadcasted_iota(jnp.int32, (2 * t, t), 0) & (t - 1)
        col = lax.broadcasted_iota(jnp.int32, (2 * t, t), 1)
        diag_mask = col < row
        lane1 = lax.broadcasted_iota(jnp.int32, (t, 128), 1)
        head0 = lane1 < HEAD_DIM
        lane2 = lax.broadcasted_iota(jnp.int32, (2 * t, 128), 1)
        uu, pm = u_ref[...], p_ref[...]
        pr = range(npair)
        cols = [slice(128 * pp, 128 * (pp + 1)) for pp in pr]
        i = pl.program_id(1)

        @pl.when(i == 0)
        def _():
            dkt[...] = jnp.zeros_like(dkt)
            dvt[...] = jnp.zeros_like(dvt)

            def transpose_k(j, _):
                rows = pl.ds(pl.multiple_of(j * t, t), t)
                for pp in pr:
                    kt[pp, j] = k_ref[rows, cols[pp]].astype(F32).T.astype(BF16)
                return 0

            lax.fori_loop(0, nq, transpose_k, 0)

        for pp in pr:
            q2 = q_ref[:, cols[pp]].astype(F32) * ATT_SCALE
            do2 = do_ref[:, cols[pp]].astype(F32)
            for src, nat, tr in ((q2, qq, qqt), (do2, dd, ddt)):
                stacked = jnp.concatenate([jnp.where(head0, src, 0.0), jnp.where(head0, 0.0, src)], axis=0)
                nat[pp] = stacked.astype(BF16)
                tr[pp] = stacked.T.astype(BF16)
        dq_acc[...] = jnp.zeros_like(dq_acc)
        cg[...] = jnp.zeros_like(cg)

        def tile(kb, diag):
            krows = pl.ds(pl.multiple_of(kb * t, t), t)
            zs = [_dot_nt(qq[pp], k_ref[krows, cols[pp]]) for pp in pr]
            das = [_dot_nt(dd[pp], v_ref[krows, cols[pp]]) for pp in pr]
            lbk = [_log_sigmoid_pair(z) for z in zs]
            lks = [jnp.where(diag_mask, lk, 0.0) if diag else lk for _, lk in lbk]
            sufs = [_cumsum_mm(lk, uu) for lk in lks]
            avs, gs = [], []
            for pp in pr:
                cs = jnp.concatenate([c_ref[2 * pp], c_ref[2 * pp + 1]], axis=0)
                cf = jnp.sum(jnp.where(lane2 == kb, cs, 0.0), axis=-1, keepdims=True)
                a = jnp.exp(lbk[pp][0] + (sufs[pp] + cf))
                a = jnp.where(diag_mask, a, 0.0) if diag else a
                avs.append(a.astype(BF16))
                gs.append(a * das[pp])
            gpres = [_cumsum_mm(g, pm) for g in gs]
            dzs = []
            for pp in pr:
                carry = cg[pp]
                beta = jnp.exp(lbk[pp][0])
                dz = gs[pp] - beta * (gs[pp] + (gpres[pp] + jnp.concatenate([carry] * (t // 128), axis=1)))
                dzs.append((jnp.where(diag_mask, dz, 0.0) if diag else dz).astype(BF16))
                cg[pp] = carry + jnp.broadcast_to(jnp.sum(gs[pp], axis=-1, keepdims=True), (2 * t, 128))
            dqs = [_dot_nt(kt[pp, kb], dzs[pp]) for pp in pr]
            dks = [_dot(qqt[pp], dzs[pp]) for pp in pr]
            dvs = [_dot(ddt[pp], avs[pp]) for pp in pr]
            for pp in pr:
                dq_acc[pp] += dqs[pp]
                dkt[pp, kb] += dks[pp]
                dvt[pp, kb] += dvs[pp]

        def k_block(kb, _):
            tile(kb, False)
            return 0

        col_max = jnp.max(jnp.max(c_ref[...], axis=0), axis=0, keepdims=True)
        lane_row = lax.broadcasted_iota(jnp.int32, (1, 128), 1)
        n_live = jnp.sum(jnp.where(jnp.logical_and(col_max > SB_DEAD, lane_row < i), 1, 0))
        lax.fori_loop(i - n_live, i, k_block, 0)
        tile(i, True)
        for pp in pr:
            dq_ref[:, cols[pp]] = jnp.where(head0, dq_acc[pp, :, 0:t].T, dq_acc[pp, :, t:2 * t].T) * ATT_SCALE

        @pl.when(i == nq - 1)
        def _():
            def untranspose(j, _):
                rows = pl.ds(pl.multiple_of(j * t, t), t)
                for pp in pr:
                    dk_ref[rows, cols[pp]] = dkt[pp, j].T
                    dv_ref[rows, cols[pp]] = dvt[pp, j].T
                return 0

            lax.fori_loop(0, nq, untranspose, 0)

    width = 128 * npair
    kv = pl.BlockSpec((s_len, width), lambda p, i: (0, p))
    qo = pl.BlockSpec((t, width), lambda p, i: (i, p))
    tri = pl.BlockSpec((t, t), lambda p, i: (0, 0))
    out = jax.ShapeDtypeStruct((s_len, D_GRP), F32)
    steps = 4 // npair
    return _call_with_exchange(
        body, rides,
        lambda: jnp.logical_and(pl.program_id(0) == 0, pl.program_id(1) == 0),
        lambda: jnp.logical_and(pl.program_id(0) == steps - 1, pl.program_id(1) == nq - 1),
        name="sb_bwd", grid=(steps, nq),
        in_specs=[qo, kv, kv, qo, pl.BlockSpec((2 * npair, t, 128), lambda p, i: (p, i, 0)), tri, tri],
        out_specs=(qo, kv, kv), out_shape=(out, out, out),
        scratch_shapes=[pltpu.VMEM((npair, 2 * t, 128), BF16), pltpu.VMEM((npair, 2 * t, 128), BF16),
                        pltpu.VMEM((npair, 128, 2 * t), BF16), pltpu.VMEM((npair, 128, 2 * t), BF16),
                        pltpu.VMEM((npair, nq, 128, t), BF16),
                        pltpu.VMEM((npair, 128, 2 * t), F32),
                        pltpu.VMEM((npair, nq, 128, t), F32), pltpu.VMEM((npair, nq, 128, t), F32),
                        pltpu.VMEM((npair, 2 * t, 128), F32)],
    )(qs, ks, vs, dos, csaved, tri_suf, tri_pre, *rides)


def _attn_out(o_b, lse_b, o_sb, x2, wdil, wsb, out_g, shards):
    s_len = x2.shape[0]
    tm = ROW_TILE

    def body(o1_ref, l1_ref, o4_ref, l4_ref, o16_ref, l16_ref, osb_ref, x_ref, wdil_ref, wsb_ref, w_ref,
             odil_ref, lse_ref, lse4_ref, lse16_ref, mixed_ref, x1_ref, stage, nat4, nat16):
        _merge_views((o4_ref, l4_ref), (o16_ref, l16_ref), stage, nat4, nat16)
        os_ = (o1_ref[...], _slab_group(nat4, 0), _slab_group(nat16, 0))
        ls = (l1_ref[...], _slab_group(nat4, 1), _slab_group(nat16, 1))
        mx = jnp.maximum(jnp.maximum(ls[0], ls[1]), ls[2])
        es = [jnp.exp(l - mx) for l in ls]
        den = es[0] + es[1] + es[2]
        o_dil = (es[0] * os_[0] + es[1] * os_[1] + es[2] * os_[2]) / den
        odil_ref[...] = o_dil
        lse = mx + jnp.log(den)
        lse_ref[...] = lse
        for p in range(4):
            nat4[p] = lse[:, 128 * p:128 * (p + 1)]
        _split_views(nat4.at[0:4], stage.at[0:4], (lse4_ref,), (lse16_ref,))
        halves = []
        for t, w_r in ((o_dil, wdil_ref), (osb_ref[...], wsb_ref)):
            r = lax.rsqrt(jnp.mean(t * t, axis=-1, keepdims=True) + EPS)
            halves.append(((t * r) * w_r[...]).astype(BF16))
        mixed = jnp.concatenate(halves, axis=1)
        mixed_ref[...] = mixed
        w = w_ref[...].reshape(D_MODEL, D_MODEL)
        x1_ref[...] = x_ref[...] + _dot(mixed, w)

    row = lambda w: pl.BlockSpec((tm, w), lambda i: (i, 0))
    ni = s_len // tm
    return _call_with_gather(
        body, shards, lambda: pl.program_id(0) == 0, lambda: pl.program_id(0) == ni - 2,
        lambda: pl.program_id(0) == ni - 1,
        name="attn_out", grid=(ni,),
        in_specs=[row(D_GRP)] * 2 + [_view_spec(tm, 4)] * 2 + [_view_spec(tm, 16)] * 2
        + [row(D_GRP), row(D_MODEL), _full((1, D_GRP)), _full((1, D_GRP)), _full((N_DEV, OUT_SHARD, D_MODEL))],
        out_specs=(row(D_GRP), row(D_GRP), _view_spec(tm, 4), _view_spec(tm, 16), row(D_MODEL), row(D_MODEL)),
        out_shape=(jax.ShapeDtypeStruct((s_len, D_GRP), F32), jax.ShapeDtypeStruct((s_len, D_GRP), F32),
                   _view_shape(s_len, 4, F32), _view_shape(s_len, 16, F32),
                   jax.ShapeDtypeStruct((s_len, D_MODEL), BF16), jax.ShapeDtypeStruct((s_len, D_MODEL), F32)),
        scratch_shapes=[pltpu.VMEM((8, tm, 128), F32)] * 3,
    )(o_b[0], lse_b[0], o_b[1], lse_b[1], o_b[2], lse_b[2], o_sb, x2, wdil, wsb, out_g, *shards)


def _two_shards(w_ref):
    return w_ref[...].reshape(FF_BLOCK, D_MODEL)


def _ffn_fwd(x1, wn2, tgt, gate_g, up_g, down_g):
    s_len = x1.shape[0]
    tm = ROW_TILE
    ni = s_len // tm

    def body(x_ref, wn_ref, t_ref, wg_ref, wu_ref, wd_ref, g_ref, u_ref, h2_ref, dy_ref, loss_ref, acc):
        j = pl.program_id(1)

        @pl.when(j == 0)
        def _():
            xx = x_ref[...]
            r = lax.rsqrt(jnp.mean(xx * xx, axis=-1, keepdims=True) + EPS)
            h2_ref[...] = ((xx * r) * wn_ref[...]).astype(BF16)
            acc[...] = jnp.zeros_like(acc)

        h = h2_ref[...]
        g = _dot_nt(h, _two_shards(wg_ref))
        u = _dot_nt(h, _two_shards(wu_ref))
        g_ref[...] = g
        u_ref[...] = u
        act = (g * (1.0 / (1.0 + jnp.exp(-g)))) * u
        acc[...] += _dot(act.astype(BF16), _two_shards(wd_ref))

        @pl.when(j == FF_STEPS - 1)
        def _():
            err = (x_ref[...] + acc[...]) - t_ref[...]
            dy_ref[...] = err * (1.0 / D_MODEL)
            part = 0.5 * jnp.sum(jnp.mean(err * err, axis=-1, keepdims=True))
            loss_ref[...] = jnp.full((8, 128), part, F32)

    row = pl.BlockSpec((tm, D_MODEL), lambda i, j: (i, 0))
    hid = pl.BlockSpec((tm, FF_BLOCK), lambda i, j: (i, j))
    return pl.pallas_call(
        body, name="ffn_fwd", grid=(ni, FF_STEPS),
        in_specs=[row, pl.BlockSpec((1, D_MODEL), lambda i, j: (0, 0)), row,
                  pl.BlockSpec((2, FF_PAD, D_MODEL), lambda i, j: (j, 0, 0)),
                  pl.BlockSpec((2, FF_PAD, D_MODEL), lambda i, j: (j, 0, 0)),
                  pl.BlockSpec((2, FF_PAD, D_MODEL), lambda i, j: (j, 0, 0))],
        out_specs=(hid, hid, row, row, pl.BlockSpec((8, 128), lambda i, j: (i, 0))),
        out_shape=(jax.ShapeDtypeStruct((s_len, N_DEV * FF_PAD), F32),
                   jax.ShapeDtypeStruct((s_len, N_DEV * FF_PAD), F32),
                   jax.ShapeDtypeStruct((s_len, D_MODEL), BF16),
                   jax.ShapeDtypeStruct((s_len, D_MODEL), F32),
                   jax.ShapeDtypeStruct((ni * 8, 128), F32)),
        scratch_shapes=[pltpu.VMEM((tm, D_MODEL), F32)],
        compiler_params=_params(),
    )(x1, wn2, tgt, gate_g, up_g, down_g)


def _ffn_bwd_dx(dy, g, u, gate_g, up_g, down_g):
    s_len = dy.shape[0]
    tm = ROW_TILE

    def body(dy_ref, g_ref, u_ref, wg_ref, wu_ref, wd_ref, dg_ref, du_ref, act_ref, dh_ref, acc):
        j = pl.program_id(1)

        @pl.when(j == 0)
        def _():
            acc[...] = jnp.zeros_like(acc)

        halves = [slice(0, tm // 2), slice(tm // 2, tm)]
        wd, wg, wu = _two_shards(wd_ref), _two_shards(wg_ref), _two_shards(wu_ref)
        das = [_dot_nt(dy_ref[rows, :].astype(BF16), wd) for rows in halves]

        def elementwise(rows, da):
            gg, uu = g_ref[rows, :], u_ref[rows, :]
            sig = 1.0 / (1.0 + jnp.exp(-gg))
            silu = gg * sig
            act_ref[rows, :] = (silu * uu).astype(BF16)
            du = (da * silu).astype(BF16)
            dg = (da * uu * (sig * (1.0 + gg * (1.0 - sig)))).astype(BF16)
            du_ref[rows, :] = du
            dg_ref[rows, :] = dg
            return dg, du

        dg0, du0 = elementwise(halves[0], das[0])
        acc[halves[0], :] += _dot(dg0, wg) + _dot(du0, wu)
        dg1, du1 = elementwise(halves[1], das[1])
        acc[halves[1], :] += _dot(dg1, wg) + _dot(du1, wu)

        @pl.when(j == FF_STEPS - 1)
        def _():
            dh_ref[...] = acc[...]

    row = pl.BlockSpec((tm, D_MODEL), lambda i, j: (i, 0))
    hid = pl.BlockSpec((tm, FF_BLOCK), lambda i, j: (i, j))
    hid_bf = jax.ShapeDtypeStruct((s_len, N_DEV * FF_PAD), BF16)
    return pl.pallas_call(
        body, name="ffn_bwd_dx", grid=(s_len // tm, FF_STEPS),
        in_specs=[row, hid, hid,
                  pl.BlockSpec((2, FF_PAD, D_MODEL), lambda i, j: (j, 0, 0)),
                  pl.BlockSpec((2, FF_PAD, D_MODEL), lambda i, j: (j, 0, 0)),
                  pl.BlockSpec((2, FF_PAD, D_MODEL), lambda i, j: (j, 0, 0))],
        out_specs=(hid, hid, hid, row),
        out_shape=(hid_bf, hid_bf, hid_bf, jax.ShapeDtypeStruct((s_len, D_MODEL), F32)),
        scratch_shapes=[pltpu.VMEM((tm, D_MODEL), F32)],
        compiler_params=_params(),
    )(dy, g, u, gate_g, up_g, down_g)


def _ffn_bwd_dw(h2, dy, dg, du, act):
    s_len = h2.shape[0]
    tm = DW_ROW_TILE
    ni = s_len // tm

    half = FF_PAD // 2

    def body(h_ref, dy_ref, dg_ref, du_ref, act_ref, dwg_ref, dwu_ref, dwd_lo_ref, dwd_hi_ref, ag, au, ad):
        i = pl.program_id(1)

        @pl.when(i == 0)
        def _():
            ag[...] = jnp.zeros_like(ag)
            au[...] = jnp.zeros_like(au)
            ad[...] = jnp.zeros_like(ad)

        h = h_ref[...]
        ag[...] += _dot_tn(dg_ref[...], h)
        au[...] += _dot_tn(du_ref[...], h)
        ad[...] += _dot_tn(act_ref[...], dy_ref[...].astype(BF16))

        @pl.when(i == ni - 1)
        def _():
            for acc_ref, out_ref in ((ag, dwg_ref), (au, dwu_ref)):
                out_ref[...] = acc_ref[...].astype(BF16).reshape(2, FF_PAD, D_MODEL)
            for dev in range(2):
                dwd_lo_ref[dev] = ad[FF_PAD * dev:FF_PAD * dev + half, :].astype(BF16)
                dwd_hi_ref[dev] = ad[FF_PAD * dev + half:FF_PAD * (dev + 1), :].astype(BF16)

    row = pl.BlockSpec((tm, D_MODEL), lambda j, i: (i, 0))
    hid = pl.BlockSpec((tm, FF_BLOCK), lambda j, i: (i, j))
    row_w = pl.BlockSpec((2, FF_PAD, D_MODEL), lambda j, i: (j, 0, 0))
    half_w = pl.BlockSpec((2, half, D_MODEL), lambda j, i: (j, 0, 0))
    grad = jax.ShapeDtypeStruct((N_DEV, FF_PAD, D_MODEL), BF16)
    half_grad = jax.ShapeDtypeStruct((N_DEV, half, D_MODEL), BF16)
    return pl.pallas_call(
        body, name="ffn_bwd_dw", grid=(FF_STEPS, ni),
        in_specs=[row, row, hid, hid, hid], out_specs=(row_w, row_w, half_w, half_w),
        out_shape=(grad, grad, half_grad, half_grad),
        scratch_shapes=[pltpu.VMEM((FF_BLOCK, D_MODEL), F32)] * 3,
        compiler_params=_params(),
    )(h2, dy, dg, du, act)


def _rms_bwd(dy, t, w):
    r = lax.rsqrt(jnp.mean(t * t, axis=-1, keepdims=True) + EPS)
    gw = dy * w
    dt = r * (gw - t * ((r * r) * jnp.mean(gw * t, axis=-1, keepdims=True)))
    return dt, dy * t * r


def _attn_out_bwd(dy, dh2, x1, wn2, b_g, mixed, o_dil, o_sb, wdil, wsb, bd512):
    s_len = dy.shape[0]
    tm = ROW_TILE
    ni = s_len // tm

    def body(dy_ref, dh_ref, x1_ref, wn_ref, w_ref, mixed_ref, odil_ref, osb_ref, wdil_ref, wsb_ref, bd_ref,
             dx1_ref, dodil_ref, delta_ref, dosb_ref, dwout_ref, dwn_ref, dwdil_ref, dwsb_ref,
             do4_ref, dl4_ref, do16_ref, dl16_ref, wacc, both, stage):
        i = pl.program_id(0)

        @pl.when(i == 0)
        def _():
            wacc[...] = jnp.zeros_like(wacc)
            dwn_ref[...] = jnp.zeros_like(dwn_ref)
            dwdil_ref[...] = jnp.zeros_like(dwdil_ref)
            dwsb_ref[...] = jnp.zeros_like(dwsb_ref)

        dnorm, dw_rows = _rms_bwd(dh_ref[...], x1_ref[...], wn_ref[...])
        dx1 = dy_ref[...] + dnorm
        dx1_ref[...] = dx1
        dwn_ref[...] += jnp.sum(dw_rows, axis=0, keepdims=True)
        dx1b = dx1.astype(BF16)
        w = w_ref[...].reshape(D_MODEL, D_MODEL)
        dmixed = _dot_nt(dx1b, w)
        wacc[...] += _dot_tn(mixed_ref[...], dx1b)
        o_dil = odil_ref[...]
        d_odil, dw_rows = _rms_bwd(dmixed[:, :D_GRP], o_dil, wdil_ref[...])
        dwdil_ref[...] += jnp.sum(dw_rows, axis=0, keepdims=True)
        dodil_ref[...] = d_odil.astype(BF16)
        delta = _mm_split(d_odil * o_dil, bd_ref[...])
        delta_ref[...] = delta
        for p in range(4):
            both[p] = d_odil[:, 128 * p:128 * (p + 1)]
            both[4 + p] = delta[:, 128 * p:128 * (p + 1)]
        _split_views(both, stage, (do4_ref, dl4_ref), (do16_ref, dl16_ref))
        d_osb, dw_rows = _rms_bwd(dmixed[:, D_GRP:], osb_ref[...], wsb_ref[...])
        dwsb_ref[...] += jnp.sum(dw_rows, axis=0, keepdims=True)
        dosb_ref[...] = d_osb.astype(BF16)

        @pl.when(i == ni - 1)
        def _():
            dwout_ref[...] = wacc[...].astype(BF16).reshape(N_DEV, OUT_SHARD, D_MODEL)

    row = lambda w: pl.BlockSpec((tm, w), lambda i: (i, 0))
    return pl.pallas_call(
        body, name="attn_out_bwd", grid=(ni,),
        in_specs=[row(D_MODEL), row(D_MODEL), row(D_MODEL), _full((1, D_MODEL)),
                  _full((N_DEV, OUT_SHARD, D_MODEL)),
                  row(D_MODEL), row(D_GRP), row(D_GRP), _full((1, D_GRP)), _full((1, D_GRP)),
                  _full((D_GRP, D_GRP))],
        out_specs=(row(D_MODEL), row(D_GRP), row(D_GRP), row(D_GRP),
                   _full((N_DEV, OUT_SHARD, D_MODEL)), _full((1, D_MODEL)), _full((1, D_GRP)), _full((1, D_GRP)),
                   _view_spec(tm, 4), _view_spec(tm, 4), _view_spec(tm, 16), _view_spec(tm, 16)),
        out_shape=(jax.ShapeDtypeStruct((s_len, D_MODEL), F32), jax.ShapeDtypeStruct((s_len, D_GRP), BF16),
                   jax.ShapeDtypeStruct((s_len, D_GRP), F32), jax.ShapeDtypeStruct((s_len, D_GRP), BF16),
                   jax.ShapeDtypeStruct((N_DEV, OUT_SHARD, D_MODEL), BF16),
                   jax.ShapeDtypeStruct((1, D_MODEL), F32), jax.ShapeDtypeStruct((1, D_GRP), F32),
                   jax.ShapeDtypeStruct((1, D_GRP), F32),
                   _view_shape(s_len, 4, BF16), _view_shape(s_len, 4, F32),
                   _view_shape(s_len, 16, BF16), _view_shape(s_len, 16, F32)),
        scratch_shapes=[pltpu.VMEM((D_MODEL, D_MODEL), F32), pltpu.VMEM((8, tm, 128), F32),
                        pltpu.VMEM((8, tm, 128), F32)],
        compiler_params=_params(),
    )(dy, dh2, x1, wn2, b_g, mixed, o_dil, o_sb, wdil, wsb, bd512)


def _qkv_bwd(dq_b, dk_b, dv_b, dqs, dks, dvs, qraw, kraw, cos2, sin2, qnw, knw, bd, rides):
    s_len = qraw.shape[0]
    tm = ROW_TILE
    ni = s_len // tm

    def body(dq1, dk1, dv1, dq4, dk4, dv4, dq16, dk16, dv16, dqs_ref, dks_ref, dvs_ref,
             qraw_ref, kraw_ref, cos_ref, sin_ref, qnw_ref, knw_ref, bd_ref,
             dproj_ref, dqn_ref, dkn_ref, stage, nat4, nat16):
        i = pl.program_id(0)

        @pl.when(i == 0)
        def _():
            dqn_ref[...] = jnp.zeros_like(dqn_ref)
            dkn_ref[...] = jnp.zeros_like(dkn_ref)

        _merge_views((dq4, dk4, dv4), (dq16, dk16, dv16), stage, nat4, nat16)
        cos_t, sin_t, bdm = cos_ref[...], sin_ref[...], bd_ref[...]
        for grp, (part1, raw_ref, nw_ref, dn_ref) in enumerate(((dq1, qraw_ref, qnw_ref, dqn_ref),
                                                                (dk1, kraw_ref, knw_ref, dkn_ref))):
            dn_acc = 0.0
            for p in range(4):
                cols = slice(128 * p, 128 * (p + 1))
                d_rope = part1[:, cols] + nat4[4 * grp + p] + nat16[4 * grp + p]
                d_norm = d_rope * cos_t + _swap_halves(d_rope * sin_t)
                t = raw_ref[:, cols]
                w = nw_ref[...]
                r = lax.rsqrt(_mm_split(t * t, bdm) * (1.0 / HEAD_DIM) + EPS)
                gw = d_norm * w
                corr = _mm_split(gw * t, bdm) * (1.0 / HEAD_DIM)
                dt = r * (gw - t * ((r * r) * corr))
                dn_acc = dn_acc + jnp.sum(d_norm * t * r, axis=0, keepdims=True)
                dproj_ref[:, D_GRP * grp + 128 * p:D_GRP * grp + 128 * (p + 1)] = dt.astype(BF16)
            dn_ref[...] += dn_acc
        dproj_ref[:, 2 * D_GRP:3 * D_GRP] = (dv1[...] + _slab_group(nat4, 2) + _slab_group(nat16, 2)).astype(BF16)
        dproj_ref[:, 3 * D_GRP:4 * D_GRP] = dqs_ref[...].astype(BF16)
        dproj_ref[:, 4 * D_GRP:5 * D_GRP] = dks_ref[...].astype(BF16)
        dproj_ref[:, 5 * D_GRP:6 * D_GRP] = dvs_ref[...].astype(BF16)

    row = lambda w: pl.BlockSpec((tm, w), lambda i: (i, 0))
    return _call_with_exchange(
        body, rides, lambda: pl.program_id(0) == 0, lambda: pl.program_id(0) == ni - 1,
        name="qkv_bwd", grid=(ni,),
        in_specs=[row(D_GRP)] * 3 + [_view_spec(tm, 4)] * 3 + [_view_spec(tm, 16)] * 3 + [row(D_GRP)] * 5
        + [row(128), row(128), _full((1, 128)), _full((1, 128)), _full((128, 128))],
        out_specs=(row(D_IN), _full((1, 128)), _full((1, 128))),
        out_shape=(jax.ShapeDtypeStruct((s_len, D_IN), BF16), jax.ShapeDtypeStruct((1, 128), F32),
                   jax.ShapeDtypeStruct((1, 128), F32)),
        scratch_shapes=[pltpu.VMEM((12, tm, 128), F32)] * 3,
    )(dq_b[0], dk_b[0], dv_b[0], dq_b[1], dk_b[1], dv_b[1], dq_b[2], dk_b[2], dv_b[2],
      dqs, dks, dvs, qraw, kraw, cos2, sin2, qnw, knw, bd, *rides)


def _in_bwd_dx(dproj, a_g, x2, dx1, wn1, rides):
    s_len = x2.shape[0]
    tm = ROW_TILE
    ni = s_len // tm

    def body(dp_ref, w_ref, x_ref, dx1_ref, wn_ref, gx_ref, dwn_ref, w_full):
        i = pl.program_id(0)

        @pl.when(i == 0)
        def _():
            dwn_ref[...] = jnp.zeros_like(dwn_ref)
            for d in range(N_DEV):
                w_full[:, IN_SHARD * d:IN_SHARD * (d + 1)] = w_ref[d]

        dh = _dot_nt(dp_ref[...], w_full[...])
        dnorm, dw_rows = _rms_bwd(dh, x_ref[...], wn_ref[...])
        gx_ref[...] = dx1_ref[...] + dnorm
        dwn_ref[...] += jnp.sum(dw_rows, axis=0, keepdims=True)

    row = lambda w: pl.BlockSpec((tm, w), lambda i: (i, 0))
    return _call_with_exchange(
        body, rides, lambda: pl.program_id(0) == 0, lambda: pl.program_id(0) == ni - 1,
        name="in_bwd_dx", grid=(ni,),
        in_specs=[row(D_IN), pl.BlockSpec((N_DEV, D_MODEL, IN_SHARD), lambda i: (0, 0, 0)),
                  row(D_MODEL), row(D_MODEL), _full((1, D_MODEL))],
        out_specs=(row(D_MODEL), _full((1, D_MODEL))),
        out_shape=(jax.ShapeDtypeStruct((s_len, D_MODEL), F32), jax.ShapeDtypeStruct((1, D_MODEL), F32)),
        scratch_shapes=[pltpu.VMEM((D_MODEL, D_IN), BF16)],
    )(dproj, a_g, x2, dx1, wn1, *rides)


def _in_bwd_dw(h1, dproj):
    s_len = h1.shape[0]
    tm = DW_ROW_TILE
    ni = s_len // tm

    half_d = D_MODEL // 2

    def body(h_ref, dp_ref, lo_ref, hi_ref, acc):
        i = pl.program_id(1)

        @pl.when(i == 0)
        def _():
            acc[...] = jnp.zeros_like(acc)

        acc[...] += _dot_tn(h_ref[...], dp_ref[...])

        @pl.when(i == ni - 1)
        def _():
            for dev in range(2):
                cols = slice(IN_SHARD * dev, IN_SHARD * (dev + 1))
                lo_ref[dev] = acc[0:half_d, cols].astype(BF16)
                hi_ref[dev] = acc[half_d:D_MODEL, cols].astype(BF16)

    half_w = pl.BlockSpec((2, half_d, IN_SHARD), lambda d, i: (d, 0, 0))
    half_grad = jax.ShapeDtypeStruct((N_DEV, half_d, IN_SHARD), BF16)
    return pl.pallas_call(
        body, name="in_bwd_dw", grid=(N_DEV // 2, ni),
        in_specs=[pl.BlockSpec((tm, D_MODEL), lambda d, i: (i, 0)),
                  pl.BlockSpec((tm, 2 * IN_SHARD), lambda d, i: (i, d))],
        out_specs=(half_w, half_w),
        out_shape=(half_grad, half_grad),
        scratch_shapes=[pltpu.VMEM((D_MODEL, 2 * IN_SHARD), F32)],
        compiler_params=_params(),
    )(h1, dproj)


def _adamw(recv, w, m, v, recv_hi=None):
    rows, cols = w.shape
    tr = next((t for t in (128, 32) if rows % t == 0), rows)
    recvs = [recv] if recv_hi is None else [recv, recv_hi]
    lo_tiles = recv.shape[1] // tr

    def body(*refs):
        p_refs = refs[:len(recvs)]
        w_ref, m_ref, v_ref, g_ref, d_ref, nm_ref, nv_ref = refs[len(recvs):]

        def slot(s):
            if len(p_refs) == 1:
                return p_refs[0][s].astype(F32)
            return jnp.where(pl.program_id(0) < lo_tiles, p_refs[0][s], p_refs[1][s]).astype(F32)

        g = slot(0)
        for s in range(1, N_DEV):
            g = g + slot(s)
        m_new = ADAM_B1 * m_ref[...] + (1.0 - ADAM_B1) * g
        v_new = ADAM_B2 * v_ref[...] + (1.0 - ADAM_B2) * (g * g)
        m_hat = m_new / (1.0 - ADAM_B1 ** ADAM_STEP)
        v_hat = v_new / (1.0 - ADAM_B2 ** ADAM_STEP)
        g_ref[...] = g
        d_ref[...] = -ADAM_LR * (m_hat / (jnp.sqrt(v_hat) + ADAM_EPS) + ADAM_WD * w_ref[...])
        nm_ref[...] = m_new
        nv_ref[...] = v_new

    blk = pl.BlockSpec((tr, cols), lambda i: (i, 0))
    out = jax.ShapeDtypeStruct((rows, cols), F32)
    return pl.pallas_call(
        body, name=f"adamw_{rows}x{cols}", grid=(rows // tr,),
        in_specs=([pl.BlockSpec((N_DEV, tr, cols), lambda i: (0, i, 0))] if recv_hi is None else
                  [pl.BlockSpec((N_DEV, tr, cols), lambda i: (0, jnp.minimum(i, lo_tiles - 1), 0)),
                   pl.BlockSpec((N_DEV, tr, cols), lambda i: (0, jnp.maximum(i - lo_tiles, 0), 0))])
        + [blk, blk, blk],
        out_specs=(blk,) * 4, out_shape=(out,) * 4,
        compiler_params=_params(),
    )(*recvs, w, m, v)


def _adamw_many(items, rides):
    tr = 32
    tiles = [w.shape[0] // tr for _, w, _, _ in items]
    starts = [sum(tiles[:k]) for k in range(len(items))]
    total = sum(tiles)
    n_items = len(items)

    def body(*refs):
        i = pl.program_id(0)
        in_refs, out_refs = refs[:4 * n_items], refs[4 * n_items:]
        for k in range(n_items):
            def update(k=k):
                p_ref, w_ref, m_ref, v_ref = in_refs[4 * k:4 * k + 4]
                g_ref, d_ref, nm_ref, nv_ref = out_refs[4 * k:4 * k + 4]
                g = p_ref[0].astype(F32)
                for s in range(1, N_DEV):
                    g = g + p_ref[s].astype(F32)
                m_new = ADAM_B1 * m_ref[...] + (1.0 - ADAM_B1) * g
                v_new = ADAM_B2 * v_ref[...] + (1.0 - ADAM_B2) * (g * g)
                m_hat = m_new / (1.0 - ADAM_B1 ** ADAM_STEP)
                v_hat = v_new / (1.0 - ADAM_B2 ** ADAM_STEP)
                g_ref[...] = g
                d_ref[...] = -ADAM_LR * (m_hat / (jnp.sqrt(v_hat) + ADAM_EPS) + ADAM_WD * w_ref[...])
                nm_ref[...] = m_new
                nv_ref[...] = v_new

            pl.when(jnp.logical_and(i >= starts[k], i < starts[k] + tiles[k]))(update)

    in_specs, out_specs, out_shape, args = [], [], [], []
    for k, (recv, w, m, v) in enumerate(items):
        tile_of = functools.partial(lambda i, s0, nk: jnp.clip(i - s0, 0, nk - 1), s0=starts[k], nk=tiles[k])
        blk = pl.BlockSpec((tr, D_MODEL), functools.partial(lambda i, t: (t(i), 0), t=tile_of))
        in_specs += [pl.BlockSpec((N_DEV, tr, D_MODEL), functools.partial(lambda i, t: (0, t(i), 0), t=tile_of)),
                     blk, blk, blk]
        out_specs += [blk] * 4
        out_shape += [jax.ShapeDtypeStruct(w.shape, F32)] * 4
        args += [recv, w, m, v]
    res = _call_with_exchange(
        body, rides, lambda: pl.program_id(0) == 0, lambda: pl.program_id(0) == total - 1,
        name="adamw_many", grid=(total,), in_specs=in_specs, out_specs=out_specs, out_shape=out_shape,
    )(*args, *rides)
    return [tuple(res[4 * k:4 * k + 4]) for k in range(n_items)], list(res[4 * n_items:])


def _rope_tables(s_len):
    pos = np.arange(s_len, dtype=np.float32)
    inv_freq = np.float32(ROPE_THETA) ** (-np.arange(0, HEAD_DIM, 2, dtype=np.float32) / np.float32(HEAD_DIM))
    ang = (pos[:, None] * inv_freq[None, :]).astype(np.float32)
    cos, sin = np.cos(ang), np.sin(ang)
    cos2 = np.concatenate([cos, cos, cos, cos], axis=1)
    sin2 = np.concatenate([-sin, sin, -sin, sin], axis=1)
    return jnp.asarray(cos2, F32), jnp.asarray(sin2, F32)


def _block_diag_ones(n):
    i = np.arange(n)
    return jnp.asarray(i[:, None] // HEAD_DIM == i[None, :] // HEAD_DIM, BF16)


def _pad_cols(t):
    return jnp.pad(t, ((0, 0), (0, FF_PAD - FF_SHARD)))


def _pad_rows(t):
    return jnp.pad(t, ((0, FF_PAD - FF_SHARD), (0, 0)))


LOSS_ROW = 26


def _pack_small(n1, n2, ndil, nsb, nq, nk, scalar=None):
    pad = lambda t: jnp.pad(t.reshape(1, -1), ((0, 0), (0, 128 - t.size)))
    last = jnp.zeros((1, 128), F32) if scalar is None else pad(scalar)
    rows = [n1.reshape(8, 128), n2.reshape(8, 128), ndil.reshape(4, 128), nsb.reshape(4, 128),
            pad(nq), pad(nk), last, jnp.zeros((5, 128), F32)]
    return jnp.concatenate(rows, axis=0)


SMALL_ROWS = ((0, 8), (8, 8), (16, 4), (20, 4), (24, 1), (25, 1))


def _adamw_small(recv, ws, ms, vs):
    n_vec = len(ws)

    def body(*refs):
        p_ref = refs[0]
        w_refs, m_refs, v_refs = (refs[1 + n_vec * k:1 + n_vec * (k + 1)] for k in range(3))
        sum_ref = refs[1 + 3 * n_vec]
        out_refs = refs[2 + 3 * n_vec:]
        total = p_ref[0]
        for s in range(1, N_DEV):
            total = total + p_ref[s]
        sum_ref[...] = total
        for i, (row0, n_rows) in enumerate(SMALL_ROWS):
            g_ref, d_ref, nm_ref, nv_ref = out_refs[4 * i:4 * i + 4]
            width = w_refs[i].shape[1]
            for k in range(n_rows):
                n = min(128, width - 128 * k)
                lanes = slice(128 * k, 128 * k + n)
                g = sum_ref[row0 + k:row0 + k + 1, :n]
                m_new = ADAM_B1 * m_refs[i][:, lanes] + (1.0 - ADAM_B1) * g
                v_new = ADAM_B2 * v_refs[i][:, lanes] + (1.0 - ADAM_B2) * (g * g)
                m_hat = m_new / (1.0 - ADAM_B1 ** ADAM_STEP)
                v_hat = v_new / (1.0 - ADAM_B2 ** ADAM_STEP)
                g_ref[:, lanes] = g
                d_ref[:, lanes] = -ADAM_LR * (m_hat / (jnp.sqrt(v_hat) + ADAM_EPS) + ADAM_WD * w_refs[i][:, lanes])
                nm_ref[:, lanes] = m_new
                nv_ref[:, lanes] = v_new

    vec_specs = [_full(w.shape) for w in ws]
    vec_shapes = [jax.ShapeDtypeStruct(w.shape, F32) for w in ws]
    res = pl.pallas_call(
        body, name="adamw_small", grid=(1,),
        in_specs=[_full(recv.shape)] + vec_specs * 3,
        out_specs=tuple([_full((32, 128))] + [s for s in vec_specs for _ in range(4)]),
        out_shape=tuple([jax.ShapeDtypeStruct((32, 128), F32)] + [s for s in vec_shapes for _ in range(4)]),
        compiler_params=_params(),
    )(recv, *ws, *ms, *vs)
    return res[0], [tuple(res[1 + 4 * i:5 + 4 * i]) for i in range(n_vec)]


def kernel(x, attn_norm_w, w_in, q_norm_w, k_norm_w, dil_out_norm_w, sb_out_norm_w, w_out, ffn_norm_w, w_gate, w_up, w_down, loss_target, m_attn_norm_w, m_w_in, m_q_norm_w, m_k_norm_w, m_dil_out_norm_w, m_sb_out_norm_w, m_w_out, m_ffn_norm_w, m_w_gate, m_w_up, m_w_down, v_attn_norm_w, v_w_in, v_q_norm_w, v_k_norm_w, v_dil_out_norm_w, v_sb_out_norm_w, v_w_out, v_ffn_norm_w, v_w_gate, v_w_up, v_w_down):
    s_len = x.shape[1]
    x2, tgt = x[0], loss_target[0]

    (a_g,) = _gather_weights([w_in[0].astype(BF16)])
    gate_loc = _pad_cols(w_gate[0]).T.astype(BF16)
    up_loc = _pad_cols(w_up[0]).T.astype(BF16)
    down_loc = _pad_rows(w_down[0]).astype(BF16)
    out_loc = w_out[0].astype(BF16)

    cos2, sin2 = _rope_tables(s_len)
    bd128, bd512 = _block_diag_ones(128), _block_diag_ones(D_GRP)
    idx = np.arange(SB_TILE)
    tri_suf = jnp.asarray(idx[:, None] > idx[None, :], BF16)
    tri_pre = jnp.asarray(idx[:, None] < idx[None, :], BF16)
    qnw2 = jnp.concatenate([q_norm_w, q_norm_w], axis=1)
    knw2 = jnp.concatenate([k_norm_w, k_norm_w], axis=1)

    (h1, qraw, kraw, q, k, va, qs, ks, vs, q4, k4, v4, q16, k16, v16,
     gate_g) = _attn_in(x2, attn_norm_w, a_g, cos2, sin2, qnw2, knw2, bd128, shards=[gate_loc])
    qkv_views = {1: (q, k, va), 4: (q4, k4, v4), 16: (q16, k16, v16)}
    fwd_riders = {1: [out_loc], 4: [], 16: []}
    o_b, lse_b, gathered = [], [], {}
    for r in DILATIONS:
        o, lse, *gathered[r] = _dil_fwd(*qkv_views[r], r, shards=fwd_riders[r])
        o_b.append(o)
        lse_b.append(lse)
    (out_g,) = gathered[1]
    o_sb, c_sb, up_g = _sb_fwd(qs, ks, vs, tri_suf, shards=[up_loc])
    o_dil, lse_tot, lse4, lse16, mixed, x1, down_g = _attn_out(
        o_b, lse_b, o_sb, x2, dil_out_norm_w, sb_out_norm_w, out_g, shards=[down_loc])
    g, u, h2, dy, loss_parts = _ffn_fwd(x1, ffn_norm_w, tgt, gate_g, up_g, down_g)
    loss_local = jnp.sum(loss_parts[::8, 0])

    dg, du, act, dh2 = _ffn_bwd_dx(dy, g, u, gate_g, up_g, down_g)
    (dx1, do_dil, delta, do_sb, dwout, dn2, dndil, dnsb, do4, dl4, do16, dl16) = _attn_out_bwd(
        dy, dh2, x1, ffn_norm_w, out_g, mixed, o_dil, o_sb, dil_out_norm_w, sb_out_norm_w, bd512)
    dwg, dwu, dwd_lo, dwd_hi = _ffn_bwd_dw(h2, dy, dg, du, act)
    dqs, dks, dvs, r_gate = _sb_bwd(qs, ks, vs, do_sb, c_sb, tri_suf, tri_pre, rides=[dwg])
    cot_views = {1: (do_dil, lse_tot, delta), 4: (do4, lse4, dl4), 16: (do16, lse16, dl16)}
    riders = {1: [dwd_lo], 4: [dwd_hi], 16: [dwu]}
    dq_b, dk_b, dv_b, landed = [], [], [], {}
    for r in DILATIONS:
        dq, dk, dv, *landed[r] = _dil_bwd(*qkv_views[r], *cot_views[r], r, rides=riders[r])
        dq_b.append(dq)
        dk_b.append(dk)
        dv_b.append(dv)
    (r_down_lo,), (r_down_hi,), (r_up,) = landed[1], landed[4], landed[16]
    dproj, dqn2, dkn2, r_out = _qkv_bwd(dq_b, dk_b, dv_b, dqs, dks, dvs, qraw, kraw, cos2, sin2, qnw2, knw2,
                                        bd128, rides=[dwout])
    dwin_lo, dwin_hi = _in_bwd_dw(h1, dproj)
    grad_x, dn1, r_in_lo = _in_bwd_dx(dproj, a_g, x2, dx1, attn_norm_w, rides=[dwin_lo])
    dqn = dqn2[:, :HEAD_DIM] + dqn2[:, HEAD_DIM:]
    dkn = dkn2[:, :HEAD_DIM] + dkn2[:, HEAD_DIM:]

    small = _pack_small(dn1, dn2, dndil, dnsb, dqn, dkn, loss_local)
    (r_small,) = _exchange_grads([], small)
    (new_gate, new_up, new_out), (r_in_hi,) = _adamw_many(
        [(r_gate, w_gate[0].T, m_w_gate[0].T, v_w_gate[0].T), (r_up, w_up[0].T, m_w_up[0].T, v_w_up[0].T),
         (r_out, w_out[0], m_w_out[0], v_w_out[0])], rides=[dwin_hi])
    big = {
        "w_in": _adamw(r_in_lo, w_in[0], m_w_in[0], v_w_in[0], recv_hi=r_in_hi),
        "w_gate": tuple(t.T for t in new_gate),
        "w_up": tuple(t.T for t in new_up),
        "w_down": _adamw(r_down_lo, w_down[0], m_w_down[0], v_w_down[0], recv_hi=r_down_hi),
        "w_out": new_out,
    }
    small_sum, small_res = _adamw_small(
        r_small,
        (attn_norm_w, ffn_norm_w, dil_out_norm_w, sb_out_norm_w, q_norm_w, k_norm_w),
        (m_attn_norm_w, m_ffn_norm_w, m_dil_out_norm_w, m_sb_out_norm_w, m_q_norm_w, m_k_norm_w),
        (v_attn_norm_w, v_ffn_norm_w, v_dil_out_norm_w, v_sb_out_norm_w, v_q_norm_w, v_k_norm_w))
    loss = small_sum[LOSS_ROW, 0]
    small_out = [[res[kind] for res in small_res] for kind in range(4)]
    names = ["attn_norm_w", "w_in", "q_norm_w", "k_norm_w", "dil_out_norm_w", "sb_out_norm_w", "w_out",
             "ffn_norm_w", "w_gate", "w_up", "w_down"]
    small_pos = {"attn_norm_w": 0, "ffn_norm_w": 1, "dil_out_norm_w": 2, "sb_out_norm_w": 3,
                 "q_norm_w": 4, "k_norm_w": 5}
    outs = [loss, grad_x[None]]
    for kind in range(4):
        for name in names:
            if name in small_pos:
                outs.append(small_out[kind][small_pos[name]])
            else:
                outs.append(big[name][kind][None])
    return tuple(outs)
```
